```python
import math
import jax, jax.numpy as jnp
from jax import lax
import numpy as np

D_MODEL = 1024
BATCH = 8
SEQ = 4096
DEPTH = 1

HEAD_DIM = 64
N_HEADS_FOX = 8
N_HEADS_DIL = 8
FOX_WIDTH = N_HEADS_FOX * HEAD_DIM
DIL_WIDTH = N_HEADS_DIL * HEAD_DIM
DIL_PATTERNS = ((128, 1), (512, 4), (2048, 16))
ROPE_DIM = HEAD_DIM // 4
ROPE_THETA = 500000.0
Q_BLOCK = 128
D_FF = 2816
CONV_WIDTH = 3
RMS_EPS = 1e-6
NEG_INF = -1e30
IN_SPLITS = (FOX_WIDTH, FOX_WIDTH, FOX_WIDTH, N_HEADS_FOX,
             DIL_WIDTH, DIL_WIDTH, DIL_WIDTH, D_MODEL, D_MODEL)
IN_WIDTH = sum(IN_SPLITS)

kernel_name = "hybrid_fox_dilated_gated_convffn"


def rmsnorm(x, g):
    xf = x.astype(jnp.float32)
    inv = lax.rsqrt(jnp.mean(xf * xf, axis=-1, keepdims=True) + RMS_EPS)
    return (xf * inv * g.astype(jnp.float32)).astype(x.dtype)


def partial_rope(t):
    S = t.shape[1]
    half = ROPE_DIM // 2
    inv_freq = ROPE_THETA ** (-jnp.arange(half, dtype=jnp.float32) * 2.0 / ROPE_DIM)
    ang = jnp.arange(S, dtype=jnp.float32)[:, None] * inv_freq[None, :]
    cos = jnp.cos(ang)[:, None, :]
    sin = jnp.sin(ang)[:, None, :]
    tf = t.astype(jnp.float32)
    t1, t2, rest = tf[..., :half], tf[..., half:ROPE_DIM], tf[..., ROPE_DIM:]
    out = jnp.concatenate([t1 * cos - t2 * sin, t2 * cos + t1 * sin, rest], axis=-1)
    return out.astype(t.dtype)


def split_heads(t, n_heads):
    B, S, _ = t.shape
    return t.reshape(B, S, n_heads, HEAD_DIM)


def fox_attention(q, k, v, log_f):
    B, S, H, dh = q.shape
    nb = S // Q_BLOCK
    scale = 1.0 / math.sqrt(dh)
    F = jnp.cumsum(log_f, axis=1).transpose(0, 2, 1)
    kt = k.transpose(0, 2, 1, 3)
    vt = v.transpose(0, 2, 1, 3)
    q_blocks = q.transpose(0, 2, 1, 3).reshape(B, H, nb, Q_BLOCK, dh).transpose(2, 0, 1, 3, 4)
    f_blocks = F.reshape(B, H, nb, Q_BLOCK).transpose(2, 0, 1, 3)
    starts = jnp.arange(nb, dtype=jnp.int32) * Q_BLOCK
    kpos = jnp.arange(S, dtype=jnp.int32)

    def one_block(args):
        qb, fqb, start = args
        s = jnp.einsum('bhqd,bhkd->bhqk', qb, kt).astype(jnp.float32) * scale
        s = s + fqb[..., None] - F[:, :, None, :]
        qpos = start + jnp.arange(Q_BLOCK, dtype=jnp.int32)
        causal = kpos[None, :] <= qpos[:, None]
        s = jnp.where(causal[None, None], s, NEG_INF)
        p = jax.nn.softmax(s, axis=-1)
        return jnp.einsum('bhqk,bhkd->bhqd', p.astype(vt.dtype), vt)

    out = lax.map(one_block, (q_blocks, f_blocks, starts))
    return out.transpose(1, 0, 3, 2, 4).reshape(B, S, H, dh)


def dilated_branch(q, k, v, window, dilation):
    B, S, H, dh = q.shape
    L = S // dilation
    w_sub = window // dilation
    blk = w_sub
    nb = -(-L // blk)
    Lp = nb * blk
    scale = 1.0 / math.sqrt(dh)

    def prep(t):
        t = t.reshape(B, L, dilation, H, dh)
        t = jnp.pad(t, ((0, 0), (0, Lp - L), (0, 0), (0, 0), (0, 0)))
        return t.reshape(B, nb, blk, dilation, H, dh)

    def with_prev(t):
        prev = jnp.pad(t[:, :-1], ((0, 0), (1, 0), (0, 0), (0, 0), (0, 0), (0, 0)))
        return jnp.concatenate([prev, t], axis=2)

    qs = prep(q)
    kk = with_prev(prep(k))
    vv = with_prev(prep(v))
    s = jnp.einsum('bnqrhd,bnkrhd->bnrhqk', qs, kk).astype(jnp.float32) * scale
    qi = jnp.arange(blk)[:, None]
    ki = jnp.arange(2 * blk)[None, :]
    dist = qi + blk - ki
    band = (dist >= 0) & (dist <= w_sub)
    exists = (jnp.arange(nb)[:, None, None] > 0) | (ki[None] >= blk)
    valid = band[None] & exists
    s = jnp.where(valid[None, :, None, None], s, NEG_INF)
    lse = jax.nn.logsumexp(s, axis=-1)
    p = jnp.exp(s - lse[..., None])
    o = jnp.einsum('bnrhqk,bnkrhd->bnqrhd', p.astype(vv.dtype), vv)
    o = o.reshape(B, Lp, dilation, H, dh)[:, :L].reshape(B, S, H, dh)
    lse = lse.transpose(0, 1, 4, 2, 3).reshape(B, Lp, dilation, H)[:, :L].reshape(B, S, H)
    return o, lse


def dilated_attention(q, k, v):
    outs, lses = [], []
    for window, dilation in DIL_PATTERNS:
        o, l = dilated_branch(q, k, v, window, dilation)
        outs.append(o)
        lses.append(l)
    lse = jnp.stack(lses, axis=0)
    alpha = jax.nn.softmax(lse, axis=0)
    out = jnp.stack(outs, axis=0).astype(jnp.float32)
    return jnp.sum(alpha[..., None] * out, axis=0).astype(q.dtype)


def causal_dwconv(u, w, b):
    S = u.shape[1]
    up = jnp.pad(u, ((0, 0), (CONV_WIDTH - 1, 0), (0, 0)))
    y = sum(up[:, i:i + S] * w[i] for i in range(CONV_WIDTH))
    return y + b


def _fwd_setup_inputs(seed: int = 0) -> dict:
    key = jax.random.key(seed)
    ks = jax.random.split(key, 20)
    f32 = jnp.float32

    def nrm(k, shape, fan_in):
        return jax.random.normal(k, shape, f32) * (fan_in ** -0.5)

    def gain(k):
        return 1.0 + 0.05 * jax.random.normal(k, (DEPTH, D_MODEL), f32)

    return {
        "x": jax.random.normal(ks[0], (BATCH, SEQ, D_MODEL), f32),
        "g_pre_mix": gain(ks[1]),
        "w_in": nrm(ks[2], (DEPTH, D_MODEL, IN_WIDTH), D_MODEL),
        "b_forget": 2.0 + 0.5 * jax.random.normal(ks[3], (DEPTH, N_HEADS_FOX), f32),
        "w_o_fox": nrm(ks[4], (DEPTH, FOX_WIDTH, D_MODEL), FOX_WIDTH),
        "w_o_dil": nrm(ks[5], (DEPTH, DIL_WIDTH, D_MODEL), DIL_WIDTH),
        "w_out": nrm(ks[6], (DEPTH, D_MODEL, D_MODEL), D_MODEL),
        "g_post_mix": gain(ks[7]),
        "g_pre_ffn": gain(ks[8]),
        "w_up": nrm(ks[9], (DEPTH, D_MODEL, 2 * D_FF), D_MODEL),
        "conv_w": nrm(ks[10], (DEPTH, CONV_WIDTH, 2 * D_FF), CONV_WIDTH),
        "conv_b": 0.02 * jax.random.normal(ks[11], (DEPTH, 2 * D_FF), f32),
        "w_down": nrm(ks[12], (DEPTH, D_FF, D_MODEL), D_FF),
        "g_post_ffn": gain(ks[13]),
    }


def _fwd_reference(x, g_pre_mix, w_in, b_forget, w_o_fox, w_o_dil, w_out, g_post_mix,
              g_pre_ffn, w_up, conv_w, conv_b, w_down, g_post_ffn):
    B, S, _ = x.shape
    offsets = np.cumsum((0,) + IN_SPLITS)
    for l in range(DEPTH):
        h = rmsnorm(x, g_pre_mix[l])
        z = h @ w_in[l]
        qa, ka, va, fa, qb, kb, vb, ga, gb = [z[..., offsets[i]:offsets[i + 1]]
                                              for i in range(len(IN_SPLITS))]
        log_f = jax.nn.log_sigmoid((fa + b_forget[l]).astype(jnp.float32))
        ya = fox_attention(split_heads(qa, N_HEADS_FOX), split_heads(ka, N_HEADS_FOX),
                           split_heads(va, N_HEADS_FOX), log_f)
        ya = ya.reshape(B, S, FOX_WIDTH) @ w_o_fox[l]
        qd = partial_rope(split_heads(qb, N_HEADS_DIL))
        kd = partial_rope(split_heads(kb, N_HEADS_DIL))
        yb = dilated_attention(qd, kd, split_heads(vb, N_HEADS_DIL))
        yb = yb.reshape(B, S, DIL_WIDTH) @ w_o_dil[l]
        mixed = jax.nn.sigmoid(ga) * ya + jax.nn.sigmoid(gb) * yb
        x = x + rmsnorm(mixed @ w_out[l], g_post_mix[l])
        h = rmsnorm(x, g_pre_ffn[l])
        u = causal_dwconv(h @ w_up[l], conv_w[l], conv_b[l])
        a, b = u[..., :D_FF], u[..., D_FF:]
        m = jax.nn.gelu(a, approximate=True) * b
        x = x + rmsnorm(m @ w_down[l], g_post_ffn[l])
    return x


import jax as _jax
import jax.numpy as _jnp

TWIN_FORMAT = 'train_step'
FWD_PARAMS = ['x', 'g_pre_mix', 'w_in', 'b_forget', 'w_o_fox', 'w_o_dil', 'w_out', 'g_post_mix', 'g_pre_ffn', 'w_up', 'conv_w', 'conv_b', 'w_down', 'g_post_ffn']
TWIN_WEIGHTS = ['g_pre_mix', 'w_in', 'b_forget', 'w_o_fox', 'w_o_dil', 'w_out', 'g_post_mix', 'g_pre_ffn', 'w_up', 'conv_w', 'conv_b', 'w_down', 'g_post_ffn']
TWIN_DIFF_INPUT = 'x'
TWIN_INPUTS = ['x', 'g_pre_mix', 'w_in', 'b_forget', 'w_o_fox', 'w_o_dil', 'w_out', 'g_post_mix', 'g_pre_ffn', 'w_up', 'conv_w', 'conv_b', 'w_down', 'g_post_ffn', 'loss_target', 'm_g_pre_mix', 'm_w_in', 'm_b_forget', 'm_w_o_fox', 'm_w_o_dil', 'm_w_out', 'm_g_post_mix', 'm_g_pre_ffn', 'm_w_up', 'm_conv_w', 'm_conv_b', 'm_w_down', 'm_g_post_ffn', 'v_g_pre_mix', 'v_w_in', 'v_b_forget', 'v_w_o_fox', 'v_w_o_dil', 'v_w_out', 'v_g_post_mix', 'v_g_pre_ffn', 'v_w_up', 'v_conv_w', 'v_conv_b', 'v_w_down', 'v_g_post_ffn']
TWIN_OUTPUTS = ['loss', 'grad_x', 'grad_g_pre_mix', 'grad_w_in', 'grad_b_forget', 'grad_w_o_fox', 'grad_w_o_dil', 'grad_w_out', 'grad_g_post_mix', 'grad_g_pre_ffn', 'grad_w_up', 'grad_conv_w', 'grad_conv_b', 'grad_w_down', 'grad_g_post_ffn', 'delta_g_pre_mix', 'delta_w_in', 'delta_b_forget', 'delta_w_o_fox', 'delta_w_o_dil', 'delta_w_out', 'delta_g_post_mix', 'delta_g_pre_ffn', 'delta_w_up', 'delta_conv_w', 'delta_conv_b', 'delta_w_down', 'delta_g_post_ffn', 'new_m_g_pre_mix', 'new_m_w_in', 'new_m_b_forget', 'new_m_w_o_fox', 'new_m_w_o_dil', 'new_m_w_out', 'new_m_g_post_mix', 'new_m_g_pre_ffn', 'new_m_w_up', 'new_m_conv_w', 'new_m_conv_b', 'new_m_w_down', 'new_m_g_post_ffn', 'new_v_g_pre_mix', 'new_v_w_in', 'new_v_b_forget', 'new_v_w_o_fox', 'new_v_w_o_dil', 'new_v_w_out', 'new_v_g_post_mix', 'new_v_g_pre_ffn', 'new_v_w_up', 'new_v_conv_w', 'new_v_conv_b', 'new_v_w_down', 'new_v_g_post_ffn']
TWIN_LEAF_KINDS = {'loss': 'loss', 'grad_x': 'grad_x', 'grad_g_pre_mix': 'grad_w', 'grad_w_in': 'grad_w', 'grad_b_forget': 'grad_w', 'grad_w_o_fox': 'grad_w', 'grad_w_o_dil': 'grad_w', 'grad_w_out': 'grad_w', 'grad_g_post_mix': 'grad_w', 'grad_g_pre_ffn': 'grad_w', 'grad_w_up': 'grad_w', 'grad_conv_w': 'grad_w', 'grad_conv_b': 'grad_w', 'grad_w_down': 'grad_w', 'grad_g_post_ffn': 'grad_w', 'delta_g_pre_mix': 'delta_w', 'delta_w_in': 'delta_w', 'delta_b_forget': 'delta_w', 'delta_w_o_fox': 'delta_w', 'delta_w_o_dil': 'delta_w', 'delta_w_out': 'delta_w', 'delta_g_post_mix': 'delta_w', 'delta_g_pre_ffn': 'delta_w', 'delta_w_up': 'delta_w', 'delta_conv_w': 'delta_w', 'delta_conv_b': 'delta_w', 'delta_w_down': 'delta_w', 'delta_g_post_ffn': 'delta_w', 'new_m_g_pre_mix': 'new_m', 'new_m_w_in': 'new_m', 'new_m_b_forget': 'new_m', 'new_m_w_o_fox': 'new_m', 'new_m_w_o_dil': 'new_m', 'new_m_w_out': 'new_m', 'new_m_g_post_mix': 'new_m', 'new_m_g_pre_ffn': 'new_m', 'new_m_w_up': 'new_m', 'new_m_conv_w': 'new_m', 'new_m_conv_b': 'new_m', 'new_m_w_down': 'new_m', 'new_m_g_post_ffn': 'new_m', 'new_v_g_pre_mix': 'new_v', 'new_v_w_in': 'new_v', 'new_v_b_forget': 'new_v', 'new_v_w_o_fox': 'new_v', 'new_v_w_o_dil': 'new_v', 'new_v_w_out': 'new_v', 'new_v_g_post_mix': 'new_v', 'new_v_g_pre_ffn': 'new_v', 'new_v_w_up': 'new_v', 'new_v_conv_w': 'new_v', 'new_v_conv_b': 'new_v', 'new_v_w_down': 'new_v', 'new_v_g_post_ffn': 'new_v'}


def _forward(args):
    return _fwd_reference(*[args[k] for k in FWD_PARAMS])


def _output_shape():
    def fwd():
        inp = _fwd_setup_inputs(0)
        return _fwd_reference(*[inp[k] for k in FWD_PARAMS])
    out = _jax.eval_shape(fwd)
    return out.shape, out.dtype

N_MICROBATCH = 1
ADAM_LR = 0.001
ADAM_B1 = 0.9
ADAM_B2 = 0.999
ADAM_EPS = 1e-08
ADAM_WD = 0.01
ADAM_STEP = 10
PER_EXAMPLE_BATCH_AXIS = {'x': 0, 'loss_target': 0}
SHARED_INPUTS = []
_WEIGHT_DTYPES = {'g_pre_mix': _jnp.float32, 'w_in': _jnp.float32, 'b_forget': _jnp.float32, 'w_o_fox': _jnp.float32, 'w_o_dil': _jnp.float32, 'w_out': _jnp.float32, 'g_post_mix': _jnp.float32, 'g_pre_ffn': _jnp.float32, 'w_up': _jnp.float32, 'conv_w': _jnp.float32, 'conv_b': _jnp.float32, 'w_down': _jnp.float32, 'g_post_ffn': _jnp.float32}
MOMENT_SCALE = {'g_pre_mix': 8.503191e-01, 'w_in': 3.364379e-01, 'b_forget': 6.035263e+00, 'w_o_fox': 6.138175e-01, 'w_o_dil': 2.064231e-01, 'w_out': 6.481100e-01, 'g_post_mix': 3.191607e+01, 'g_pre_ffn': 6.198958e-01, 'w_up': 2.505209e-01, 'conv_w': 2.745870e-01, 'conv_b': 4.402688e-01, 'w_down': 5.070431e-01, 'g_post_ffn': 3.205995e+01}


def _to_microbatches(a, axis):
    t = _jnp.moveaxis(a, axis, 0)
    t = t.reshape((N_MICROBATCH, t.shape[0] // N_MICROBATCH) + t.shape[1:])
    return _jnp.moveaxis(t, 1, axis + 1)


def setup_inputs(seed: int = 0) -> dict:
    inp = _fwd_setup_inputs(seed)
    key = _jax.random.fold_in(_jax.random.key(seed), 7919)
    shape, _ = _output_shape()
    out = dict(inp)
    out["loss_target"] = _jax.random.normal(_jax.random.fold_in(key, 0), shape, _jnp.float32)
    for i, name in enumerate(TWIN_WEIGHTS):
        w = inp[name].astype(_jnp.float32)
        if MOMENT_SCALE is None:
            s = _jnp.sqrt(_jnp.mean(_jnp.square(w)) + 1e-30)
        else:
            s = MOMENT_SCALE[name]
        km, kv = _jax.random.split(_jax.random.fold_in(key, i + 1))
        out[name] = w
        out["m_" + name] = s * _jax.random.normal(km, w.shape, _jnp.float32)
        out["v_" + name] = (s * s) * _jax.random.uniform(kv, w.shape, _jnp.float32, 0.5, 1.5)
    if N_MICROBATCH > 1:
        for name, axis in PER_EXAMPLE_BATCH_AXIS.items():
            out[name] = _to_microbatches(out[name], axis)
    return {'x': out['x'], 'g_pre_mix': out['g_pre_mix'], 'w_in': out['w_in'], 'b_forget': out['b_forget'], 'w_o_fox': out['w_o_fox'], 'w_o_dil': out['w_o_dil'], 'w_out': out['w_out'], 'g_post_mix': out['g_post_mix'], 'g_pre_ffn': out['g_pre_ffn'], 'w_up': out['w_up'], 'conv_w': out['conv_w'], 'conv_b': out['conv_b'], 'w_down': out['w_down'], 'g_post_ffn': out['g_post_ffn'], 'loss_target': out['loss_target'], 'm_g_pre_mix': out['m_g_pre_mix'], 'm_w_in': out['m_w_in'], 'm_b_forget': out['m_b_forget'], 'm_w_o_fox': out['m_w_o_fox'], 'm_w_o_dil': out['m_w_o_dil'], 'm_w_out': out['m_w_out'], 'm_g_post_mix': out['m_g_post_mix'], 'm_g_pre_ffn': out['m_g_pre_ffn'], 'm_w_up': out['m_w_up'], 'm_conv_w': out['m_conv_w'], 'm_conv_b': out['m_conv_b'], 'm_w_down': out['m_w_down'], 'm_g_post_ffn': out['m_g_post_ffn'], 'v_g_pre_mix': out['v_g_pre_mix'], 'v_w_in': out['v_w_in'], 'v_b_forget': out['v_b_forget'], 'v_w_o_fox': out['v_w_o_fox'], 'v_w_o_dil': out['v_w_o_dil'], 'v_w_out': out['v_w_out'], 'v_g_post_mix': out['v_g_post_mix'], 'v_g_pre_ffn': out['v_g_pre_ffn'], 'v_w_up': out['v_w_up'], 'v_conv_w': out['v_conv_w'], 'v_conv_b': out['v_conv_b'], 'v_w_down': out['v_w_down'], 'v_g_post_ffn': out['v_g_post_ffn']}


def _loss(weights, diff, rest, loss_target):
    with _jax.named_scope("forward"):
        args = {**rest, TWIN_DIFF_INPUT: diff, **{k: w.astype(_WEIGHT_DTYPES[k]) for k, w in weights.items()}}
        y = _forward(args)
    with _jax.named_scope("loss_head"):
        err = _jnp.square(y.astype(_jnp.float32) - loss_target)
        return 0.5 * _jnp.sum(_jnp.mean(err, axis=-1)) if err.ndim else 0.5 * err


def _adamw(w, g, m, v):
    m = ADAM_B1 * m + (1.0 - ADAM_B1) * g
    v = ADAM_B2 * v + (1.0 - ADAM_B2) * _jnp.square(g)
    m_hat = m / (1.0 - ADAM_B1 ** ADAM_STEP)
    v_hat = v / (1.0 - ADAM_B2 ** ADAM_STEP)
    delta = -ADAM_LR * (m_hat / (_jnp.sqrt(v_hat) + ADAM_EPS) + ADAM_WD * w)
    return delta, m, v


def reference(x, g_pre_mix, w_in, b_forget, w_o_fox, w_o_dil, w_out, g_post_mix, g_pre_ffn, w_up, conv_w, conv_b, w_down, g_post_ffn, loss_target, m_g_pre_mix, m_w_in, m_b_forget, m_w_o_fox, m_w_o_dil, m_w_out, m_g_post_mix, m_g_pre_ffn, m_w_up, m_conv_w, m_conv_b, m_w_down, m_g_post_ffn, v_g_pre_mix, v_w_in, v_b_forget, v_w_o_fox, v_w_o_dil, v_w_out, v_g_post_mix, v_g_pre_ffn, v_w_up, v_conv_w, v_conv_b, v_w_down, v_g_post_ffn):
    given = dict(x=x, g_pre_mix=g_pre_mix, w_in=w_in, b_forget=b_forget, w_o_fox=w_o_fox, w_o_dil=w_o_dil, w_out=w_out, g_post_mix=g_post_mix, g_pre_ffn=g_pre_ffn, w_up=w_up, conv_w=conv_w, conv_b=conv_b, w_down=w_down, g_post_ffn=g_post_ffn, loss_target=loss_target, m_g_pre_mix=m_g_pre_mix, m_w_in=m_w_in, m_b_forget=m_b_forget, m_w_o_fox=m_w_o_fox, m_w_o_dil=m_w_o_dil, m_w_out=m_w_out, m_g_post_mix=m_g_post_mix, m_g_pre_ffn=m_g_pre_ffn, m_w_up=m_w_up, m_conv_w=m_conv_w, m_conv_b=m_conv_b, m_w_down=m_w_down, m_g_post_ffn=m_g_post_ffn, v_g_pre_mix=v_g_pre_mix, v_w_in=v_w_in, v_b_forget=v_b_forget, v_w_o_fox=v_w_o_fox, v_w_o_dil=v_w_o_dil, v_w_out=v_w_out, v_g_post_mix=v_g_post_mix, v_g_pre_ffn=v_g_pre_ffn, v_w_up=v_w_up, v_conv_w=v_conv_w, v_conv_b=v_conv_b, v_w_down=v_w_down, v_g_post_ffn=v_g_post_ffn)
    weights = {n: given[n] for n in TWIN_WEIGHTS}
    shared = {n: given[n] for n in SHARED_INPUTS}
    per_example = {n: given[n] for n in ['x']}
    grad_fn = _jax.value_and_grad(_loss, argnums=(0, 1))

    def one_microbatch(ex, loss_target):
        ex = dict(ex)
        diff = ex.pop(TWIN_DIFF_INPUT)
        return grad_fn(weights, diff, {**shared, **ex}, loss_target)

    if N_MICROBATCH == 1:
        loss, (grad_w, grad_x) = one_microbatch(per_example, given["loss_target"])
    else:
        def body(carry, xs):
            loss_sum, grad_sum = carry
            l_k, (gw_k, gx_k) = one_microbatch(xs[0], xs[1])
            with _jax.named_scope("update"):
                return (loss_sum + l_k, _jax.tree.map(_jnp.add, grad_sum, gw_k)), gx_k

        init = (_jnp.zeros((), _jnp.float32), _jax.tree.map(_jnp.zeros_like, weights))
        (loss, grad_w), grad_x = _jax.lax.scan(body, init, (per_example, given["loss_target"]))
    with _jax.named_scope("update"):
        delta_w, new_m, new_v = {}, {}, {}
        for n in TWIN_WEIGHTS:
            delta_w[n], new_m[n], new_v[n] = _adamw(weights[n], grad_w[n], given["m_" + n], given["v_" + n])
    return (loss, grad_x, *[grad_w[n] for n in TWIN_WEIGHTS], *[delta_w[n] for n in TWIN_WEIGHTS],
            *[new_m[n] for n in TWIN_WEIGHTS], *[new_v[n] for n in TWIN_WEIGHTS])
```

```python
import functools
import math

import jax
import jax.numpy as jnp
from jax import lax
from jax.experimental import pallas as pl
from jax.experimental.pallas import tpu as pltpu

F32 = jnp.float32
BF16 = jnp.bfloat16

SEQ = 4096
D_MODEL = 1024
N_HEADS = 8
HEAD_DIM = 64
ATT_W = N_HEADS * HEAD_DIM
D_FF = 2816
Z_MAIN = 5120
F_PAD = 128
ROPE_DIM = 16
ROPE_THETA = 500000.0
RMS_EPS = 1e-6
NEG_INF = -1e30
SCALE = 1.0 / math.sqrt(HEAD_DIM)
DIL_PATTERNS = ((128, 1), (512, 4), (2048, 16))
DIL_BLK = 128
N_DEV = 8

ADAM_LR = 0.001
ADAM_B1 = 0.9
ADAM_B2 = 0.999
ADAM_EPS = 1e-08
ADAM_WD = 0.01
ADAM_STEP = 10

LANE = 128
SUBLANE = 8
VMEM_LIMIT = 56 * 1024 * 1024
MESH_ID = pl.DeviceIdType.MESH
ANY = pl.BlockSpec(memory_space=pl.ANY)


def _params(*sem):
    return pltpu.CompilerParams(dimension_semantics=sem, vmem_limit_bytes=VMEM_LIMIT)


def _sds(shape, dtype):
    return jax.ShapeDtypeStruct(shape, dtype)


def _matmul(a, b, *, ta=False, tb=False, out_dtype, tm, tn, tk, name):
    if ta:
        kk, m = a.shape
    else:
        m, kk = a.shape
    n = b.shape[0] if tb else b.shape[1]
    assert (b.shape[1] if tb else b.shape[0]) == kk
    tm, tn, tk = min(tm, m), min(tn, n), min(tk, kk)
    assert m % tm == 0 and n % tn == 0 and kk % tk == 0, (name, m, n, kk, tm, tn, tk)
    nk = kk // tk
    dims = (((0 if ta else 1,), (1 if tb else 0,)), ((), ()))

    def body(a_ref, b_ref, o_ref, *scratch):
        p = lax.dot_general(a_ref[...].astype(BF16), b_ref[...].astype(BF16), dims,
                            preferred_element_type=F32)
        if nk == 1:
            o_ref[...] = p.astype(o_ref.dtype)
        else:
            acc = scratch[0]
            k = pl.program_id(2)

            @pl.when(k == 0)
            def _():
                acc[...] = p

            @pl.when(k > 0)
            def _():
                acc[...] += p

            @pl.when(k == nk - 1)
            def _():
                o_ref[...] = acc[...].astype(o_ref.dtype)

    a_spec = (pl.BlockSpec((tk, tm), lambda i, j, k: (k, i)) if ta
              else pl.BlockSpec((tm, tk), lambda i, j, k: (i, k)))
    b_spec = (pl.BlockSpec((tn, tk), lambda i, j, k: (j, k)) if tb
              else pl.BlockSpec((tk, tn), lambda i, j, k: (k, j)))
    return pl.pallas_call(
        body, name=name, grid=(m // tm, n // tn, nk),
        in_specs=[a_spec, b_spec],
        out_specs=pl.BlockSpec((tm, tn), lambda i, j, k: (i, j)),
        out_shape=_sds((m, n), out_dtype),
        scratch_shapes=[pltpu.VMEM((tm, tn), F32)] if nk > 1 else [],
        compiler_params=_params("parallel", "parallel", "arbitrary"),
    )(a, b)


def _rms_fwd(x, g, *, name, tm=512):
    def body(x_ref, g_ref, h_ref):
        xv = x_ref[...]
        r = lax.rsqrt(jnp.mean(xv * xv, axis=-1, keepdims=True) + RMS_EPS)
        h_ref[...] = (xv * r * g_ref[...]).astype(h_ref.dtype)

    return pl.pallas_call(
        body, name=name, grid=(SEQ // tm,),
        in_specs=[pl.BlockSpec((tm, D_MODEL), lambda i: (i, 0)), pl.BlockSpec((1, D_MODEL), lambda i: (0, 0))],
        out_specs=pl.BlockSpec((tm, D_MODEL), lambda i: (i, 0)),
        out_shape=_sds((SEQ, D_MODEL), BF16),
        compiler_params=_params("parallel"),
    )(x, g)


def _rms_bwd(dh_parts, xin, g, dres, *, out_dtype, name, tm=512):
    n_parts = len(dh_parts)
    has_res = dres is not None

    def body(*refs):
        parts = refs[:n_parts]
        x_ref, g_ref = refs[n_parts], refs[n_parts + 1]
        res_ref = refs[n_parts + 2] if has_res else None
        o_ref, gg_ref = refs[-2], refs[-1]
        dh = parts[0][...].astype(F32)
        for p in parts[1:]:
            dh = dh + p[...].astype(F32)
        xv = x_ref[...]
        r = lax.rsqrt(jnp.mean(xv * xv, axis=-1, keepdims=True) + RMS_EPS)
        xn = xv * r

        @pl.when(pl.program_id(0) == 0)
        def _():
            gg_ref[...] = jnp.zeros_like(gg_ref)

        gg_ref[...] += jnp.sum(dh * xn, axis=0, keepdims=True)
        dxn = dh * g_ref[...]
        dx = r * (dxn - xn * jnp.mean(dxn * xn, axis=-1, keepdims=True))
        if has_res:
            dx = dx + res_ref[...]
        o_ref[...] = dx.astype(o_ref.dtype)

    row = pl.BlockSpec((tm, D_MODEL), lambda i: (i, 0))
    vec = pl.BlockSpec((1, D_MODEL), lambda i: (0, 0))
    args = list(dh_parts) + [xin, g] + ([dres] if has_res else [])
    return pl.pallas_call(
        body, name=name, grid=(SEQ // tm,),
        in_specs=[row] * n_parts + [row, vec] + ([row] if has_res else []),
        out_specs=[row, vec],
        out_shape=[_sds((SEQ, D_MODEL), out_dtype), _sds((1, D_MODEL), F32)],
        compiler_params=_params("arbitrary"),
    )(*args)


SCAN_BLK = 512


def _split_dot(v, tri):
    hi = v.astype(BF16)
    r1 = v - hi.astype(F32)
    mid = r1.astype(BF16)
    lo = (r1 - mid.astype(F32)).astype(BF16)
    dot = functools.partial(jnp.dot, preferred_element_type=F32)
    return dot(hi, tri) + dot(mid, tri) + dot(lo, tri)


def _fox_prep(fa_t, b_col):
    nblk = SEQ // SCAN_BLK

    def body(fa_ref, b_ref, f_ref, sg_ref):
        row = lax.broadcasted_iota(jnp.int32, (SCAN_BLK, SCAN_BLK), 0)
        col = lax.broadcasted_iota(jnp.int32, (SCAN_BLK, SCAN_BLK), 1)
        upper = (row <= col).astype(BF16)
        carry = jnp.zeros((N_HEADS, 1), F32)
        for blk in range(nblk):
            sl = pl.ds(blk * SCAN_BLK, SCAN_BLK)
            xx = fa_ref[:, sl] + b_ref[...]
            e = jnp.exp(-jnp.abs(xx))
            logf = jnp.minimum(xx, 0.0) - jnp.log(1.0 + e)
            sg_ref[:, sl] = jnp.where(xx >= 0.0, e, 1.0) / (1.0 + e)
            c = _split_dot(logf, upper) + carry
            f_ref[:, sl] = c
            carry = c[:, SCAN_BLK - 1:SCAN_BLK]

    return pl.pallas_call(
        body, name="fox_prep",
        out_shape=[_sds((N_HEADS, SEQ), F32), _sds((N_HEADS, SEQ), F32)],
        compiler_params=pltpu.CompilerParams(vmem_limit_bytes=VMEM_LIMIT),
    )(fa_t, b_col)


def _fox_post_bwd(df_rows_t, df_cols_t, sg_t):
    nblk = SEQ // SCAN_BLK

    def body(dfr_ref, dfc_ref, sg_ref, dfa_ref, gb_ref):
        row = lax.broadcasted_iota(jnp.int32, (SCAN_BLK, SCAN_BLK), 0)
        col = lax.broadcasted_iota(jnp.int32, (SCAN_BLK, SCAN_BLK), 1)
        lower = (row >= col).astype(BF16)
        carry = jnp.zeros((N_HEADS, 1), F32)
        gb = jnp.zeros((N_HEADS, 1), F32)
        for blk in reversed(range(nblk)):
            sl = pl.ds(blk * SCAN_BLK, SCAN_BLK)
            c = _split_dot(dfr_ref[:, sl] + dfc_ref[:, sl], lower) + carry
            carry = c[:, 0:1]
            dfa = c * sg_ref[:, sl]
            dfa_ref[:, sl] = dfa
            gb = gb + jnp.sum(dfa, axis=1, keepdims=True)
        gb_ref[...] = gb

    return pl.pallas_call(
        body, name="fox_post_bwd",
        out_shape=[_sds((N_HEADS, SEQ), F32), _sds((N_HEADS, 1), F32)],
        compiler_params=pltpu.CompilerParams(vmem_limit_bytes=VMEM_LIMIT),
    )(df_rows_t, df_cols_t, sg_t)


FOX_T = 512
NT_DIMS = (((1,), (1,)), ((), ()))
TN_DIMS = (((0,), (0,)), ((), ()))


def _head(ref_or_val, h):
    return ref_or_val[:, h * HEAD_DIM:(h + 1) * HEAD_DIM]


def _fox_scores(q, k, fc, fr, i, j):
    s = lax.dot_general(q, k, NT_DIMS, preferred_element_type=F32) * SCALE + fc - fr
    qpos = i * FOX_T + lax.broadcasted_iota(jnp.int32, (FOX_T, FOX_T), 0)
    kpos = j * FOX_T + lax.broadcasted_iota(jnp.int32, (FOX_T, FOX_T), 1)
    return s, kpos <= qpos


def _fox_fwd(zm, fc, fr):
    nb = SEQ // FOX_T

    def body(q_ref, k_ref, v_ref, fc_ref, fr_ref, o_ref, lse_ref, m_s, l_s, acc_s):
        i, j = pl.program_id(1), pl.program_id(2)

        @pl.when(j == 0)
        def _():
            m_s[...] = jnp.full_like(m_s, NEG_INF)
            l_s[...] = jnp.zeros_like(l_s)
            acc_s[...] = jnp.zeros_like(acc_s)

        @pl.when(j <= i)
        def _():
            for h in range(2):
                s, ok = _fox_scores(_head(q_ref, h), _head(k_ref, h), fc_ref[h], fr_ref[h], i, j)
                s = jnp.where(ok, s, NEG_INF)
                m_prev = m_s[h]
                m_new = jnp.maximum(m_prev, jnp.max(s, axis=-1, keepdims=True))
                alpha = jnp.exp(m_prev - m_new)
                p = jnp.exp(s - m_new)
                l_s[h] = alpha * l_s[h] + jnp.sum(p, axis=-1, keepdims=True)
                acc_s[h] = alpha * acc_s[h] + jnp.dot(p.astype(BF16), _head(v_ref, h),
                                                      preferred_element_type=F32)
                m_s[h] = m_new

        @pl.when(j == i)
        def _():
            o_ref[...] = jnp.concatenate([acc_s[h] / l_s[h] for h in range(2)], axis=1).astype(o_ref.dtype)
            for h in range(2):
                lse_ref[h] = m_s[h] + jnp.log(l_s[h])

    kv = lambda off: pl.BlockSpec((FOX_T, LANE), lambda p, i, j: (jnp.minimum(j, i), off + p))
    return pl.pallas_call(
        body, name="fox_fwd", grid=(N_HEADS // 2, nb, nb),
        in_specs=[pl.BlockSpec((FOX_T, LANE), lambda p, i, j: (i, p)), kv(4), kv(8),
                  pl.BlockSpec((2, FOX_T, 1), lambda p, i, j: (p, i, 0)),
                  pl.BlockSpec((2, 1, FOX_T), lambda p, i, j: (p, 0, jnp.minimum(j, i)))],
        out_specs=[pl.BlockSpec((FOX_T, LANE), lambda p, i, j: (i, p)),
                   pl.BlockSpec((2, FOX_T, 1), lambda p, i, j: (p, i, 0))],
        out_shape=[_sds((SEQ, ATT_W), BF16), _sds((N_HEADS, SEQ, 1), F32)],
        scratch_shapes=[pltpu.VMEM((2, FOX_T, 1), F32), pltpu.VMEM((2, FOX_T, 1), F32),
                        pltpu.VMEM((2, FOX_T, HEAD_DIM), F32)],
        compiler_params=_params("parallel", "parallel", "arbitrary"),
    )(zm, zm, zm, fc, fr)


def _fox_bwd(zm, fc, fr, lse, delta, do):
    nb = SEQ // FOX_T

    def body(q_ref, k_ref, v_ref, fc_ref, fr_ref, lse_ref, dl_ref, do_ref,
             dq_ref, dk_ref, dv_ref, df_ref, dk_s, dv_s, df_s):
        j, i = pl.program_id(1), pl.program_id(2)

        @pl.when((j == 0) & (i == 0))
        def _():
            dq_ref[...] = jnp.zeros_like(dq_ref)

        @pl.when(i == 0)
        def _():
            dk_s[...] = jnp.zeros_like(dk_s)
            dv_s[...] = jnp.zeros_like(dv_s)
            df_s[...] = jnp.zeros_like(df_s)

        @pl.when(i >= j)
        def _():
            rows = pl.ds(pl.multiple_of(i * FOX_T, FOX_T), FOX_T)
            ones_k = jnp.ones((FOX_T, HEAD_DIM), BF16)
            ones_q = jnp.ones((SUBLANE, FOX_T), BF16)
            lane = lax.broadcasted_iota(jnp.int32, (1, LANE), 1)
            lane_scale = jnp.where(lane < HEAD_DIM, SCALE, 1.0)
            for h in range(2):
                q, k, v, dout = _head(q_ref, h), _head(k_ref, h), _head(v_ref, h), _head(do_ref, h)
                s, ok = _fox_scores(q, k, fc_ref[h], fr_ref[h], i, j)
                p = jnp.where(ok, jnp.exp(s - lse_ref[h]), 0.0)
                dp = lax.dot_general(dout, v, NT_DIMS, preferred_element_type=F32)
                dsb = (p * (dp - dl_ref[h])).astype(BF16)
                dv_s[h] += lax.dot_general(p.astype(BF16), dout, TN_DIMS, preferred_element_type=F32)
                dk_s[h] += lax.dot_general(dsb, q, TN_DIMS, preferred_element_type=F32) * SCALE
                df_s[h] -= jnp.dot(ones_q, dsb, preferred_element_type=F32)[0:1]
                k_aug = jnp.concatenate([k, ones_k], axis=1)
                dq_ref[rows, h * LANE:(h + 1) * LANE] += jnp.dot(dsb, k_aug, preferred_element_type=F32) * lane_scale

        @pl.when(i == nb - 1)
        def _():
            dk_ref[...] = jnp.concatenate([dk_s[0], dk_s[1]], axis=1)
            dv_ref[...] = jnp.concatenate([dv_s[0], dv_s[1]], axis=1)
            df_ref[...] = df_s[...]

    qrow = lambda off: pl.BlockSpec((FOX_T, LANE), lambda p, j, i: (jnp.maximum(i, j), off + p))
    krow = lambda off: pl.BlockSpec((FOX_T, LANE), lambda p, j, i: (j, off + p))
    qcol = pl.BlockSpec((2, FOX_T, 1), lambda p, j, i: (p, jnp.maximum(i, j), 0))
    return pl.pallas_call(
        body, name="fox_bwd", grid=(N_HEADS // 2, nb, nb),
        in_specs=[qrow(0), krow(4), krow(8), qcol,
                  pl.BlockSpec((2, 1, FOX_T), lambda p, j, i: (p, 0, j)), qcol, qcol,
                  pl.BlockSpec((FOX_T, LANE), lambda p, j, i: (jnp.maximum(i, j), p))],
        out_specs=[pl.BlockSpec((SEQ, 2 * LANE), lambda p, j, i: (0, p)),
                   pl.BlockSpec((FOX_T, LANE), lambda p, j, i: (j, p)),
                   pl.BlockSpec((FOX_T, LANE), lambda p, j, i: (j, p)),
                   pl.BlockSpec((2, 1, FOX_T), lambda p, j, i: (p, 0, j))],
        out_shape=[_sds((SEQ, 2 * ATT_W), F32), _sds((SEQ, ATT_W), F32), _sds((SEQ, ATT_W), F32),
                   _sds((N_HEADS, 1, SEQ), F32)],
        scratch_shapes=[pltpu.VMEM((2, FOX_T, HEAD_DIM), F32), pltpu.VMEM((2, FOX_T, HEAD_DIM), F32),
                        pltpu.VMEM((2, 1, FOX_T), F32)],
        compiler_params=_params("arbitrary", "arbitrary", "arbitrary"),
    )(zm, zm, zm, fc, fr, lse, delta, do)


def _attn_delta(o, do, *, head_major, name, tm=512):
    def body(o_ref, do_ref, d_ref):
        prod = o_ref[...].astype(F32) * do_ref[...].astype(F32)
        if head_major:
            for h in range(N_HEADS):
                d_ref[h] = jnp.sum(_head(prod, h), axis=1, keepdims=True)
        else:
            lane = lax.broadcasted_iota(jnp.int32, (tm, LANE), 1)
            out = jnp.zeros((tm, LANE), F32)
            for h in range(N_HEADS):
                out = jnp.where(lane == h, jnp.sum(_head(prod, h), axis=1, keepdims=True), out)
            d_ref[...] = out

    row = pl.BlockSpec((tm, ATT_W), lambda i: (i, 0))
    if head_major:
        out_spec, out_shape = pl.BlockSpec((N_HEADS, tm, 1), lambda i: (0, i, 0)), _sds((N_HEADS, SEQ, 1), F32)
    else:
        out_spec, out_shape = pl.BlockSpec((tm, LANE), lambda i: (i, 0)), _sds((SEQ, LANE), F32)
    return pl.pallas_call(
        body, name=name, grid=(SEQ // tm,), in_specs=[row, row], out_specs=out_spec, out_shape=out_shape,
        compiler_params=_params("parallel"),
    )(o, do)


def _rope_tables():
    half = ROPE_DIM // 2
    inv_freq = ROPE_THETA ** (-jnp.arange(half, dtype=F32) * 2.0 / ROPE_DIM)
    ang = jnp.arange(SEQ, dtype=F32)[:, None] * inv_freq[None, :]
    cos, sin = jnp.cos(ang), jnp.sin(ang)
    ones = jnp.ones((SEQ, HEAD_DIM - ROPE_DIM), F32)
    zeros = jnp.zeros((SEQ, HEAD_DIM - ROPE_DIM), F32)
    zh = jnp.zeros((SEQ, half), F32)
    c_tab = jnp.concatenate([cos, cos, ones], axis=1)
    a_tab = jnp.concatenate([-sin, zh, zeros], axis=1)
    b_tab = jnp.concatenate([zh, sin, zeros], axis=1)
    two = lambda t: jnp.concatenate([t, t], axis=1)
    return two(c_tab), two(a_tab), two(b_tab)


def _rotate(x, c_tab, a_tab, b_tab):
    return x * c_tab + pltpu.roll(x, LANE - ROPE_DIM // 2, 1) * a_tab + pltpu.roll(x, ROPE_DIM // 2, 1) * b_tab


def _rope_fwd(zm, tabs, *, tm=512):
    def body(q_ref, k_ref, c_ref, a_ref, b_ref, o_ref):
        for part, x_ref in enumerate((q_ref, k_ref)):
            for cc in range(ATT_W // LANE):
                sl = slice(cc * LANE, (cc + 1) * LANE)
                o_ref[:, part * ATT_W + cc * LANE:part * ATT_W + (cc + 1) * LANE] = _rotate(
                    x_ref[:, sl].astype(F32), c_ref[...], a_ref[...], b_ref[...]).astype(o_ref.dtype)

    tab = pl.BlockSpec((tm, LANE), lambda i: (i, 0))
    return pl.pallas_call(
        body, name="rope_fwd", grid=(SEQ // tm,),
        in_specs=[pl.BlockSpec((tm, ATT_W), lambda i: (i, 3)), pl.BlockSpec((tm, ATT_W), lambda i: (i, 4)),
                  tab, tab, tab],
        out_specs=pl.BlockSpec((tm, 2 * ATT_W), lambda i: (i, 0)),
        out_shape=_sds((SEQ, 2 * ATT_W), BF16),
        compiler_params=_params("parallel"),
    )(zm, zm, *tabs)


def _dil_grad_combine(dqs, dks, dvs, tabs, *, tm=256):
    def body(*refs):
        q_refs, k_refs, v_refs = refs[0:3], refs[3:6], refs[6:9]
        c_ref, a_ref, b_ref, o_ref = refs[9:]
        total = lambda rs, sl: rs[0][:, sl] + rs[1][:, sl] + rs[2][:, sl]
        for cc in range(ATT_W // LANE):
            sl = slice(cc * LANE, (cc + 1) * LANE)
            for part, rs in enumerate((q_refs, k_refs)):
                o_ref[:, part * ATT_W + cc * LANE:part * ATT_W + (cc + 1) * LANE] = _rotate(
                    total(rs, sl), c_ref[...], -a_ref[...], -b_ref[...]).astype(o_ref.dtype)
            o_ref[:, 2 * ATT_W + cc * LANE:2 * ATT_W + (cc + 1) * LANE] = total(v_refs, sl).astype(o_ref.dtype)

    row = pl.BlockSpec((tm, ATT_W), lambda i: (i, 0))
    tab = pl.BlockSpec((tm, LANE), lambda i: (i, 0))
    return pl.pallas_call(
        body, name="dil_grad_combine", grid=(SEQ // tm,),
        in_specs=[row] * 9 + [tab] * 3,
        out_specs=pl.BlockSpec((tm, 3 * ATT_W), lambda i: (i, 0)),
        out_shape=_sds((SEQ, 3 * ATT_W), BF16),
        compiler_params=_params("parallel"),
    )(*dqs, *dks, *dvs, *tabs)


def _dil_valid(n):
    qi = lax.broadcasted_iota(jnp.int32, (DIL_BLK, 2 * DIL_BLK), 0)
    ki = lax.broadcasted_iota(jnp.int32, (DIL_BLK, 2 * DIL_BLK), 1)
    dist = qi + DIL_BLK - ki
    return (dist >= 0) & (dist <= DIL_BLK) & ((n > 0) | (ki >= DIL_BLK))


def _dil_views(qk, zm, d):
    length = SEQ // d
    return qk.reshape(length, d * 2 * ATT_W), zm.reshape(length, d * Z_MAIN), length // DIL_BLK


def _dil_fwd(qk, zm, d):
    qk_v, zm_v, nb = _dil_views(qk, zm, d)
    length = SEQ // d
    v_blk = Z_MAIN // ATT_W

    def body(q_ref, kp_ref, kc_ref, vp_ref, vc_ref, o_ref, lse_ref):
        n = pl.program_id(1)
        ok = _dil_valid(n)
        lane = lax.broadcasted_iota(jnp.int32, (DIL_BLK, LANE), 1)
        lse_all = jnp.zeros((DIL_BLK, LANE), F32)
        outs = []
        for h in range(N_HEADS):
            kk = jnp.concatenate([_head(kp_ref, h), _head(kc_ref, h)], axis=0)
            vv = jnp.concatenate([_head(vp_ref, h), _head(vc_ref, h)], axis=0)
            s = lax.dot_general(_head(q_ref, h), kk, NT_DIMS, preferred_element_type=F32) * SCALE
            s = jnp.where(ok, s, NEG_INF)
            m = jnp.max(s, axis=-1, keepdims=True)
            p = jnp.exp(s - m)
            l = jnp.sum(p, axis=-1, keepdims=True)
            outs.append(jnp.dot(p.astype(BF16), vv, preferred_element_type=F32) / l)
            lse_all = jnp.where(lane == h, m + jnp.log(l), lse_all)
        o_ref[...] = jnp.concatenate(outs, axis=1)
        lse_ref[...] = lse_all

    blk = lambda f: pl.BlockSpec((DIL_BLK, ATT_W), f)
    prev = lambda n: jnp.maximum(n - 1, 0)
    o, lse = pl.pallas_call(
        body, name=f"dil_fwd_d{d}", grid=(d, nb),
        in_specs=[blk(lambda r, n: (n, 2 * r)),
                  blk(lambda r, n: (prev(n), 2 * r + 1)), blk(lambda r, n: (n, 2 * r + 1)),
                  blk(lambda r, n: (prev(n), v_blk * r + 5)), blk(lambda r, n: (n, v_blk * r + 5))],
        out_specs=[blk(lambda r, n: (n, r)), pl.BlockSpec((DIL_BLK, LANE), lambda r, n: (n, r))],
        out_shape=[_sds((length, d * ATT_W), F32), _sds((length, d * LANE), F32)],
        compiler_params=_params("parallel", "arbitrary"),
    )(qk_v, qk_v, qk_v, zm_v, zm_v)
    return o.reshape(SEQ, ATT_W), lse.reshape(SEQ, LANE)


def _dil_merge(os_, lses, *, tm=512):
    def body(o0, o1, o2, l0, l1, l2, y_ref, lse_ref):
        ls = [l0[...], l1[...], l2[...]]
        m = jnp.maximum(jnp.maximum(ls[0], ls[1]), ls[2])
        es = [jnp.exp(l - m) for l in ls]
        tot = es[0] + es[1] + es[2]
        lse_ref[...] = m + jnp.log(tot)
        alphas = [e / tot for e in es]
        outs = []
        for h in range(N_HEADS):
            acc = None
            for g, o_ref in enumerate((o0, o1, o2)):
                term = alphas[g][:, h:h + 1] * _head(o_ref, h)
                acc = term if acc is None else acc + term
            outs.append(acc)
        y_ref[...] = jnp.concatenate(outs, axis=1).astype(y_ref.dtype)

    row = pl.BlockSpec((tm, ATT_W), lambda i: (i, 0))
    vec = pl.BlockSpec((tm, LANE), lambda i: (i, 0))
    return pl.pallas_call(
        body, name="dil_merge", grid=(SEQ // tm,),
        in_specs=[row] * 3 + [vec] * 3, out_specs=[row, vec],
        out_shape=[_sds((SEQ, ATT_W), BF16), _sds((SEQ, LANE), F32)],
        compiler_params=_params("parallel"),
    )(*os_, *lses)


def _dil_bwd(qk, zm, lse, delta, do, d):
    qk_v, zm_v, nb = _dil_views(qk, zm, d)
    length = SEQ // d
    v_blk = Z_MAIN // ATT_W
    lse_v, dl_v, do_v = lse.reshape(length, d * LANE), delta.reshape(length, d * LANE), do.reshape(length, d * ATT_W)

    def body(q_ref, kp_ref, kc_ref, vp_ref, vc_ref, lse_ref, dl_ref, do_ref,
             dq_ref, dk_ref, dv_ref, ck_s, cv_s):
        n = pl.program_id(1)

        @pl.when(n == 0)
        def _():
            ck_s[...] = jnp.zeros_like(ck_s)
            cv_s[...] = jnp.zeros_like(cv_s)

        @pl.when(n < nb)
        def _():
            ok = _dil_valid(n)
            for h in range(N_HEADS):
                cols = slice(h * HEAD_DIM, (h + 1) * HEAD_DIM)
                q, dout = _head(q_ref, h), _head(do_ref, h)
                kk = jnp.concatenate([_head(kp_ref, h), _head(kc_ref, h)], axis=0)
                vv = jnp.concatenate([_head(vp_ref, h), _head(vc_ref, h)], axis=0)
                s = lax.dot_general(q, kk, NT_DIMS, preferred_element_type=F32) * SCALE
                p = jnp.where(ok, jnp.exp(s - lse_ref[:, h:h + 1]), 0.0)
                dp = lax.dot_general(dout, vv, NT_DIMS, preferred_element_type=F32)
                ds = (p * (dp - dl_ref[:, h:h + 1])).astype(BF16)
                dq_ref[:, cols] = jnp.dot(ds, kk, preferred_element_type=F32) * SCALE
                dkk = lax.dot_general(ds, q, TN_DIMS, preferred_element_type=F32) * SCALE
                dvv = lax.dot_general(p.astype(BF16), dout, TN_DIMS, preferred_element_type=F32)
                dk_ref[:, cols] = ck_s[:, cols] + dkk[:DIL_BLK]
                dv_ref[:, cols] = cv_s[:, cols] + dvv[:DIL_BLK]
                ck_s[:, cols] = dkk[DIL_BLK:]
                cv_s[:, cols] = dvv[DIL_BLK:]

        @pl.when(n == nb)
        def _():
            dk_ref[...] = ck_s[...]
            dv_ref[...] = cv_s[...]

    blk = lambda f: pl.BlockSpec((DIL_BLK, ATT_W), f)
    vec = lambda f: pl.BlockSpec((DIL_BLK, LANE), f)
    cur = lambda n: jnp.minimum(n, nb - 1)
    prev = lambda n: jnp.maximum(cur(n) - 1, 0)
    back = lambda n: jnp.maximum(n - 1, 0)
    outs = pl.pallas_call(
        body, name=f"dil_bwd_d{d}", grid=(d, nb + 1),
        in_specs=[blk(lambda r, n: (cur(n), 2 * r)),
                  blk(lambda r, n: (prev(n), 2 * r + 1)), blk(lambda r, n: (cur(n), 2 * r + 1)),
                  blk(lambda r, n: (prev(n), v_blk * r + 5)), blk(lambda r, n: (cur(n), v_blk * r + 5)),
                  vec(lambda r, n: (cur(n), r)), vec(lambda r, n: (cur(n), r)),
                  blk(lambda r, n: (cur(n), r))],
        out_specs=[blk(lambda r, n: (cur(n), r)), blk(lambda r, n: (back(n), r)), blk(lambda r, n: (back(n), r))],
        out_shape=[_sds((length, d * ATT_W), F32)] * 3,
        scratch_shapes=[pltpu.VMEM((DIL_BLK, ATT_W), F32), pltpu.VMEM((DIL_BLK, ATT_W), F32)],
        compiler_params=_params("arbitrary", "arbitrary"),
    )(qk_v, qk_v, qk_v, zm_v, zm_v, lse_v, dl_v, do_v)
    return [t.reshape(SEQ, ATT_W) for t in outs]


def _sigmoid(x):
    return 1.0 / (1.0 + jnp.exp(-x))


def _mix_fwd(ya, yb, w_oa, w_ob, zm, *, tm=512):
    def body(ya_ref, yb_ref, wa_ref, wb_ref, ga_ref, gb_ref, pa_ref, pb_ref, mix_ref):
        pa = jnp.dot(ya_ref[...], wa_ref[...], preferred_element_type=F32)
        pb = jnp.dot(yb_ref[...], wb_ref[...], preferred_element_type=F32)
        pa_ref[...] = pa.astype(pa_ref.dtype)
        pb_ref[...] = pb.astype(pb_ref.dtype)
        mix_ref[...] = (_sigmoid(ga_ref[...].astype(F32)) * pa + _sigmoid(gb_ref[...].astype(F32)) * pb
                        ).astype(mix_ref.dtype)

    row = pl.BlockSpec((tm, ATT_W), lambda i: (i, 0))
    wsp = pl.BlockSpec((ATT_W, D_MODEL), lambda i: (0, 0))
    wide = pl.BlockSpec((tm, D_MODEL), lambda i: (i, 0))
    return pl.pallas_call(
        body, name="mix_fwd", grid=(SEQ // tm,),
        in_specs=[row, row, wsp, wsp, pl.BlockSpec((tm, D_MODEL), lambda i: (i, 3)),
                  pl.BlockSpec((tm, D_MODEL), lambda i: (i, 4))],
        out_specs=[wide] * 3, out_shape=[_sds((SEQ, D_MODEL), BF16)] * 3,
        compiler_params=_params("parallel"),
    )(ya, yb, w_oa, w_ob, zm, zm)


def _gate_bwd(dmix, zm, pa, pb, *, tm=512):
    def body(dm_ref, ga_ref, gb_ref, pa_ref, pb_ref, dpa_ref, dpb_ref, dg_ref):
        dm = dm_ref[...].astype(F32)
        sa, sb = _sigmoid(ga_ref[...].astype(F32)), _sigmoid(gb_ref[...].astype(F32))
        dpa_ref[...] = (dm * sa).astype(dpa_ref.dtype)
        dpb_ref[...] = (dm * sb).astype(dpb_ref.dtype)
        dg_ref[:, :D_MODEL] = (dm * pa_ref[...].astype(F32) * sa * (1.0 - sa)).astype(dg_ref.dtype)
        dg_ref[:, D_MODEL:] = (dm * pb_ref[...].astype(F32) * sb * (1.0 - sb)).astype(dg_ref.dtype)

    wide = pl.BlockSpec((tm, D_MODEL), lambda i: (i, 0))
    return pl.pallas_call(
        body, name="gate_bwd", grid=(SEQ // tm,),
        in_specs=[wide, pl.BlockSpec((tm, D_MODEL), lambda i: (i, 3)), pl.BlockSpec((tm, D_MODEL), lambda i: (i, 4)),
                  wide, wide],
        out_specs=[wide, wide, pl.BlockSpec((tm, 2 * D_MODEL), lambda i: (i, 0))],
        out_shape=[_sds((SEQ, D_MODEL), BF16), _sds((SEQ, D_MODEL), BF16), _sds((SEQ, 2 * D_MODEL), BF16)],
        compiler_params=_params("parallel"),
    )(dmix, zm, zm, pa, pb)


def _out_fwd(mixed, w_out, x, g_post, g_pre, *, tm=512):
    def body(m_ref, w_ref, x_ref, gp_ref, gn_ref, y_ref, x2_ref, h_ref):
        y = jnp.dot(m_ref[...], w_ref[...], preferred_element_type=F32)
        y_ref[...] = y
        r = lax.rsqrt(jnp.mean(y * y, axis=-1, keepdims=True) + RMS_EPS)
        x2 = x_ref[...] + y * r * gp_ref[...]
        x2_ref[...] = x2
        r2 = lax.rsqrt(jnp.mean(x2 * x2, axis=-1, keepdims=True) + RMS_EPS)
        h_ref[...] = (x2 * r2 * gn_ref[...]).astype(h_ref.dtype)

    row = pl.BlockSpec((tm, D_MODEL), lambda i: (i, 0))
    vec = pl.BlockSpec((1, D_MODEL), lambda i: (0, 0))
    return pl.pallas_call(
        body, name="out_fwd", grid=(SEQ // tm,),
        in_specs=[row, pl.BlockSpec((D_MODEL, D_MODEL), lambda i: (0, 0)), row, vec, vec],
        out_specs=[row] * 3,
        out_shape=[_sds((SEQ, D_MODEL), F32), _sds((SEQ, D_MODEL), F32), _sds((SEQ, D_MODEL), BF16)],
        compiler_params=_params("parallel"),
    )(mixed, w_out, x, g_post, g_pre)


FFN_TM = 256
FFN_TN = 256
FFN_NJ = D_FF // FFN_TN


def _gelu_parts(a):
    c = math.sqrt(2.0 / math.pi)
    t = jnp.tanh(c * (a + 0.044715 * a * a * a))
    gelu = 0.5 * a * (1.0 + t)
    dgelu = 0.5 * (1.0 + t) + 0.5 * a * (1.0 - t * t) * c * (1.0 + 3.0 * 0.044715 * a * a)
    return gelu, dgelu


def _shift_down(u, halo, k):
    s = pltpu.roll(u, k, 0)
    hs = pltpu.roll(halo, k, 0)
    row = lax.broadcasted_iota(jnp.int32, hs.shape, 0)
    top = jnp.where(row < k, hs, s[:SUBLANE])
    return jnp.concatenate([top, s[SUBLANE:]], axis=0)


def _shift_up(u, halo, k):
    rows = u.shape[0]
    s = pltpu.roll(u, rows - k, 0)
    hs = pltpu.roll(halo, SUBLANE - k, 0)
    row = lax.broadcasted_iota(jnp.int32, hs.shape, 0)
    bottom = jnp.where(row >= SUBLANE - k, hs, s[rows - SUBLANE:])
    return jnp.concatenate([s[:rows - SUBLANE], bottom], axis=0)


def _conv_taps(u, halo, w_ref, b_ref):
    s1, s2 = _shift_down(u, halo, 1), _shift_down(u, halo, 2)
    return w_ref[0:1, :] * s2 + w_ref[1:2, :] * s1 + w_ref[2:3, :] * u + b_ref[...], s1, s2


def _ffn_specs(rev):
    nrow = SEQ // FFN_TM
    per = FFN_TM // SUBLANE
    ri = (lambda i: nrow - 1 - i) if rev else (lambda i: i)
    main = lambda off: pl.BlockSpec((FFN_TM, FFN_TN), lambda j, i: (ri(i), j + off))
    halo = lambda off: pl.BlockSpec((SUBLANE, FFN_TN), lambda j, i: (jnp.maximum(ri(i) * per - 1, 0), j + off))
    wsp = lambda off: pl.BlockSpec((3, FFN_TN), lambda j, i: (0, j + off))
    bsp = lambda off: pl.BlockSpec((1, FFN_TN), lambda j, i: (0, j + off))
    return ri, main, halo, wsp, bsp


def _ffn_mid_fwd(u, conv_w, conv_b):
    ri, main, halo, wsp, bsp = _ffn_specs(False)

    def body(ua_ref, ub_ref, ha_ref, hb_ref, wa_ref, wb_ref, ba_ref, bb_ref, m_ref):
        live = (pl.program_id(1) > 0).astype(F32)
        a, _, _ = _conv_taps(ua_ref[...].astype(F32), ha_ref[...].astype(F32) * live, wa_ref, ba_ref)
        b, _, _ = _conv_taps(ub_ref[...].astype(F32), hb_ref[...].astype(F32) * live, wb_ref, bb_ref)
        m_ref[...] = (_gelu_parts(a)[0] * b).astype(m_ref.dtype)

    return pl.pallas_call(
        body, name="ffn_mid_fwd", grid=(FFN_NJ, SEQ // FFN_TM),
        in_specs=[main(0), main(FFN_NJ), halo(0), halo(FFN_NJ), wsp(0), wsp(FFN_NJ), bsp(0), bsp(FFN_NJ)],
        out_specs=pl.BlockSpec((FFN_TM, FFN_TN), lambda j, i: (i, j)),
        out_shape=_sds((SEQ, D_FF), BF16),
        compiler_params=_params("parallel", "arbitrary"),
    )(u, u, u, u, conv_w, conv_w, conv_b, conv_b)


def _ffn_mid_bwd(dm, u, conv_w, conv_b):
    ri, main, halo, wsp, bsp = _ffn_specs(True)
    nrow = SEQ // FFN_TM

    def body(dm_ref, ua_ref, ub_ref, ha_ref, hb_ref, wa_ref, wb_ref, ba_ref, bb_ref,
             dua_ref, dub_ref, gwa_ref, gwb_ref, gba_ref, gbb_ref, ca_s, cb_s):
        i = pl.program_id(1)
        live = (i < nrow - 1).astype(F32)

        @pl.when(i == 0)
        def _():
            ca_s[...] = jnp.zeros_like(ca_s)
            cb_s[...] = jnp.zeros_like(cb_s)
            for r in (gwa_ref, gwb_ref, gba_ref, gbb_ref):
                r[...] = jnp.zeros_like(r)

        ua, ub = ua_ref[...].astype(F32), ub_ref[...].astype(F32)
        a, a1, a2 = _conv_taps(ua, ha_ref[...].astype(F32) * live, wa_ref, ba_ref)
        b, b1, b2 = _conv_taps(ub, hb_ref[...].astype(F32) * live, wb_ref, bb_ref)
        gelu, dgelu = _gelu_parts(a)
        dmv = dm_ref[...].astype(F32)
        for du, s0, s1, s2, w_ref, c_s, du_ref, gw_ref, gb_ref in (
                (dmv * b * dgelu, ua, a1, a2, wa_ref, ca_s, dua_ref, gwa_ref, gba_ref),
                (dmv * gelu, ub, b1, b2, wb_ref, cb_s, dub_ref, gwb_ref, gbb_ref)):
            gw_ref[0:1, :] += jnp.sum(du * s2, axis=0, keepdims=True)
            gw_ref[1:2, :] += jnp.sum(du * s1, axis=0, keepdims=True)
            gw_ref[2:3, :] += jnp.sum(du * s0, axis=0, keepdims=True)
            gb_ref[...] += jnp.sum(du, axis=0, keepdims=True)
            nxt = c_s[...]
            du_ref[...] = (w_ref[2:3, :] * du + w_ref[1:2, :] * _shift_up(du, nxt, 1)
                           + w_ref[0:1, :] * _shift_up(du, nxt, 2)).astype(du_ref.dtype)
            c_s[...] = du[:SUBLANE]

    acc3 = pl.BlockSpec((3, FFN_TN), lambda j, i: (0, j))
    acc1 = pl.BlockSpec((1, FFN_TN), lambda j, i: (0, j))
    out_blk = pl.BlockSpec((FFN_TM, FFN_TN), lambda j, i: (ri(i), j))
    return pl.pallas_call(
        body, name="ffn_mid_bwd", grid=(FFN_NJ, nrow),
        in_specs=[pl.BlockSpec((FFN_TM, FFN_TN), lambda j, i: (ri(i), j)),
                  main(0), main(FFN_NJ), halo(0), halo(FFN_NJ), wsp(0), wsp(FFN_NJ), bsp(0), bsp(FFN_NJ)],
        out_specs=[out_blk, out_blk, acc3, acc3, acc1, acc1],
        out_shape=[_sds((SEQ, D_FF), BF16), _sds((SEQ, D_FF), BF16), _sds((3, D_FF), F32), _sds((3, D_FF), F32),
                   _sds((1, D_FF), F32), _sds((1, D_FF), F32)],
        scratch_shapes=[pltpu.VMEM((SUBLANE, FFN_TN), F32), pltpu.VMEM((SUBLANE, FFN_TN), F32)],
        compiler_params=_params("parallel", "arbitrary"),
    )(dm, u, u, u, u, conv_w, conv_w, conv_b, conv_b)


def _down_fwd(m, w_down, x2, g_post, target, *, tm=512):
    def body(m_ref, w_ref, x2_ref, g_ref, t_ref, dout_ref, dy_ref, gg_ref, loss_ref):
        @pl.when(pl.program_id(0) == 0)
        def _():
            gg_ref[...] = jnp.zeros_like(gg_ref)
            loss_ref[...] = jnp.zeros_like(loss_ref)

        y = jnp.dot(m_ref[...], w_ref[...], preferred_element_type=F32)
        r = lax.rsqrt(jnp.mean(y * y, axis=-1, keepdims=True) + RMS_EPS)
        yn = y * r
        diff = (x2_ref[...] + yn * g_ref[...]) - t_ref[...]
        loss_ref[...] += jnp.sum(diff * diff)
        dout = diff * (1.0 / D_MODEL)
        dout_ref[...] = dout
        gg_ref[...] += jnp.sum(dout * yn, axis=0, keepdims=True)
        dn = dout * g_ref[...]
        dy_ref[...] = (r * (dn - yn * jnp.mean(dn * yn, axis=-1, keepdims=True))).astype(dy_ref.dtype)

    row = pl.BlockSpec((tm, D_MODEL), lambda i: (i, 0))
    vec = pl.BlockSpec((1, D_MODEL), lambda i: (0, 0))
    return pl.pallas_call(
        body, name="down_fwd", grid=(SEQ // tm,),
        in_specs=[pl.BlockSpec((tm, D_FF), lambda i: (i, 0)), pl.BlockSpec((D_FF, D_MODEL), lambda i: (0, 0)),
                  row, vec, row],
        out_specs=[row, row, vec, pl.BlockSpec((1, LANE), lambda i: (0, 0))],
        out_shape=[_sds((SEQ, D_MODEL), F32), _sds((SEQ, D_MODEL), BF16), _sds((1, D_MODEL), F32),
                   _sds((1, LANE), F32)],
        compiler_params=_params("arbitrary"),
    )(m, w_down, x2, g_post, target)


def _local_step(x, target, w_main, w_f, b_forget, w_oa, w_ob, w_out, w_up, conv_w, conv_b, w_down,
                g_pre_mix, g_post_mix, g_pre_ffn, g_post_ffn):
    mm = _matmul
    tabs = _rope_tables()

    h1 = _rms_fwd(x, g_pre_mix, name="rms_pre_mix")
    zm = mm(h1, w_main, out_dtype=BF16, tm=1024, tn=512, tk=1024, name="in_proj")
    zf = mm(h1, w_f, out_dtype=F32, tm=1024, tn=F_PAD, tk=1024, name="in_proj_forget")
    f_row, sg_row = _fox_prep(zf[:, :N_HEADS].T, b_forget.reshape(N_HEADS, 1))
    fc, fr = f_row.reshape(N_HEADS, SEQ, 1), f_row.reshape(N_HEADS, 1, SEQ)
    ya, lse_a = _fox_fwd(zm, fc, fr)
    qk = _rope_fwd(zm, tabs)
    dil = [_dil_fwd(qk, zm, d) for _, d in DIL_PATTERNS]
    yb, lse_b = _dil_merge([o for o, _ in dil], [l for _, l in dil])
    pa, pb, mixed = _mix_fwd(ya, yb, w_oa, w_ob, zm)
    y1, x2, h2 = _out_fwd(mixed, w_out, x, g_post_mix, g_pre_ffn)
    u = mm(h2, w_up, out_dtype=BF16, tm=1024, tn=512, tk=1024, name="up_proj")
    m = _ffn_mid_fwd(u, conv_w, conv_b)
    dout, dy2, gg_post_ffn, sq_err = _down_fwd(m, w_down, x2, g_post_ffn, target)

    g_w_down = mm(m, dy2, ta=True, out_dtype=F32, tm=256, tn=1024, tk=1024, name="grad_w_down")
    dm = mm(dy2, w_down, tb=True, out_dtype=BF16, tm=1024, tn=256, tk=1024, name="d_ffn_mid")
    du_a, du_b, gcw_a, gcw_b, gcb_a, gcb_b = _ffn_mid_bwd(dm, u, conv_w, conv_b)
    g_w_up = [mm(h2, t, ta=True, out_dtype=F32, tm=512, tn=256, tk=1024, name=f"grad_w_up_{s}")
              for s, t in (("a", du_a), ("b", du_b))]
    dh2 = [mm(t, w_up[:, o:o + D_FF], tb=True, out_dtype=F32, tm=1024, tn=512, tk=D_FF, name=f"d_h2_{s}")
           for s, t, o in (("a", du_a, 0), ("b", du_b, D_FF))]
    dx2, gg_pre_ffn = _rms_bwd(dh2, x2, g_pre_ffn, dout, out_dtype=F32, name="rms_pre_ffn_bwd")

    dy1, gg_post_mix = _rms_bwd([dx2], y1, g_post_mix, None, out_dtype=BF16, name="rms_post_mix_bwd")
    g_w_out = mm(mixed, dy1, ta=True, out_dtype=F32, tm=512, tn=1024, tk=1024, name="grad_w_out")
    dmix = mm(dy1, w_out, tb=True, out_dtype=BF16, tm=1024, tn=512, tk=1024, name="d_mixed")
    dpa, dpb, dgates = _gate_bwd(dmix, zm, pa, pb)
    g_w_oa = mm(ya, dpa, ta=True, out_dtype=F32, tm=512, tn=1024, tk=1024, name="grad_w_o_fox")
    g_w_ob = mm(yb, dpb, ta=True, out_dtype=F32, tm=512, tn=1024, tk=1024, name="grad_w_o_dil")
    dya = mm(dpa, w_oa, tb=True, out_dtype=BF16, tm=1024, tn=512, tk=1024, name="d_y_fox")
    dyb = mm(dpb, w_ob, tb=True, out_dtype=BF16, tm=1024, tn=512, tk=1024, name="d_y_dil")

    delta_a = _attn_delta(ya, dya, head_major=True, name="delta_fox")
    dq_aug, dk_a, dv_a, df_cols = _fox_bwd(zm, fc, fr, lse_a, delta_a, dya)
    dq_aug = dq_aug.reshape(SEQ, N_HEADS, 2, HEAD_DIM)
    dq_a, df_rows = dq_aug[:, :, 0, :].reshape(SEQ, ATT_W), dq_aug[:, :, 1, 0]
    dfa_t, g_b_forget = _fox_post_bwd(df_rows.T, df_cols.reshape(N_HEADS, SEQ), sg_row)

    delta_b = _attn_delta(yb, dyb, head_major=False, name="delta_dil")
    dil_g = [_dil_bwd(qk, zm, lse_b, delta_b, dyb, d) for _, d in DIL_PATTERNS]
    d_dil = _dil_grad_combine([g[0] for g in dil_g], [g[1] for g in dil_g], [g[2] for g in dil_g], tabs)

    dz = jnp.concatenate([dq_a.astype(BF16), dk_a.astype(BF16), dv_a.astype(BF16), d_dil, dgates], axis=1)
    dzf = jnp.pad(dfa_t.T, ((0, 0), (0, F_PAD - N_HEADS)))
    g_w_main = mm(h1, dz, ta=True, out_dtype=F32, tm=512, tn=512, tk=1024, name="grad_w_in")
    g_w_f = mm(h1, dzf, ta=True, out_dtype=F32, tm=512, tn=F_PAD, tk=1024, name="grad_w_in_forget")
    dh1 = [mm(dz, w_main, tb=True, out_dtype=F32, tm=1024, tn=512, tk=1024, name="d_h1"),
           mm(dzf, w_f, tb=True, out_dtype=F32, tm=1024, tn=512, tk=F_PAD, name="d_h1_forget")]
    grad_x, gg_pre_mix = _rms_bwd(dh1, x, g_pre_mix, dx2, out_dtype=F32, name="rms_pre_mix_bwd")

    grads = dict(
        w_main=g_w_main, w_f=g_w_f, b_forget=g_b_forget.reshape(1, N_HEADS), w_o_fox=g_w_oa, w_o_dil=g_w_ob,
        w_out=g_w_out, w_up=jnp.concatenate(g_w_up, axis=1), conv_w=jnp.concatenate([gcw_a, gcw_b], axis=1),
        conv_b=jnp.concatenate([gcb_a, gcb_b], axis=1), w_down=g_w_down,
        g_pre_mix=gg_pre_mix, g_post_mix=gg_post_mix, g_pre_ffn=gg_pre_ffn, g_post_ffn=gg_post_ffn)
    return sq_err[0, 0], grad_x, grads


def _exchange(arrays, scatter, *, name):
    n = len(arrays)

    def body(*refs):
        ins, outs = refs[:n], refs[n:2 * n]
        send_sems, recv_sems, local_sems = refs[2 * n:]
        x, y, c = lax.axis_index("x"), lax.axis_index("y"), lax.axis_index("c")
        me = 4 * x + 2 * y + c
        peers = []
        for k in range(1, N_DEV):
            px = 1 - x if k & 4 else x
            py = 1 - y if k & 2 else y
            pc = 1 - c if k & 1 else c
            peers.append(((px, py, pc), 4 * px + 2 * py + pc))

        def remote(a, k):
            dev, slot = peers[k]
            return pltpu.make_async_remote_copy(
                src_ref=ins[a].at[slot] if scatter else ins[a], dst_ref=outs[a].at[me],
                send_sem=send_sems.at[a, k], recv_sem=recv_sems.at[a, k],
                device_id=dev, device_id_type=MESH_ID)

        def landed(a, k):
            dev, slot = peers[k]
            return pltpu.make_async_remote_copy(
                src_ref=outs[a].at[slot], dst_ref=outs[a].at[slot],
                send_sem=send_sems.at[a, k], recv_sem=recv_sems.at[a, k],
                device_id=dev, device_id_type=MESH_ID)

        own = [pltpu.make_async_copy(ins[a].at[me] if scatter else ins[a], outs[a].at[me], local_sems.at[a])
               for a in range(n)]
        copies = [remote(a, k) for k in range(N_DEV - 1) for a in range(n)]
        for cp in own + copies:
            cp.start()
        for k in range(N_DEV - 1):
            for a in range(n):
                landed(a, k).wait_recv()
        for cp in copies:
            cp.wait_send()
        for cp in own:
            cp.wait()

    out_shape = [_sds(((N_DEV,) + a.shape[-2:]), a.dtype) for a in arrays]
    return pl.pallas_call(
        body, name=name, in_specs=[ANY] * n, out_specs=[ANY] * n, out_shape=out_shape,
        scratch_shapes=[pltpu.SemaphoreType.DMA((n, N_DEV - 1)), pltpu.SemaphoreType.DMA((n, N_DEV - 1)),
                        pltpu.SemaphoreType.DMA((n,))],
    )(*arrays)


def _adamw(parts, w, m, v, *, name, tm):
    r, c = w.shape
    assert r % tm == 0

    def body(p_ref, w_ref, m_ref, v_ref, g_ref, d_ref, nm_ref, nv_ref):
        g = p_ref[0].astype(F32)
        for s in range(1, N_DEV):
            g = g + p_ref[s].astype(F32)
        g_ref[...] = g
        m_new = ADAM_B1 * m_ref[...] + (1.0 - ADAM_B1) * g
        v_new = ADAM_B2 * v_ref[...] + (1.0 - ADAM_B2) * (g * g)
        nm_ref[...] = m_new
        nv_ref[...] = v_new
        m_hat = m_new / (1.0 - ADAM_B1 ** ADAM_STEP)
        v_hat = v_new / (1.0 - ADAM_B2 ** ADAM_STEP)
        d_ref[...] = -ADAM_LR * (m_hat / (jnp.sqrt(v_hat) + ADAM_EPS) + ADAM_WD * w_ref[...])

    blk = pl.BlockSpec((tm, c), lambda i: (i, 0))
    return pl.pallas_call(
        body, name=name, grid=(r // tm,),
        in_specs=[pl.BlockSpec((N_DEV, tm, c), lambda i: (0, i, 0)), blk, blk, blk],
        out_specs=[blk] * 4, out_shape=[_sds((r, c), F32)] * 4,
        compiler_params=_params("parallel"),
    )(parts, w, m, v)


SMALL = (("g_pre_mix", D_MODEL), ("b_forget", LANE), ("g_post_mix", D_MODEL), ("g_pre_ffn", D_MODEL),
         ("conv_b", 2 * D_FF), ("g_post_ffn", D_MODEL))
SMALL_ROWS = 80


def _pack_small(vals):
    flat = [jnp.pad(vals[n].reshape(-1), (0, size - vals[n].size)) for n, size in SMALL]
    flat = jnp.concatenate(flat)
    return jnp.pad(flat, (0, SMALL_ROWS * LANE - flat.size)).reshape(SMALL_ROWS, LANE)


def _unpack_small(packed, shapes):
    flat, out, off = packed.reshape(-1), {}, 0
    for n, size in SMALL:
        cnt = math.prod(shapes[n])
        out[n] = flat[off:off + cnt].reshape(shapes[n])
        off += size
    return out


def kernel(x, g_pre_mix, w_in, b_forget, w_o_fox, w_o_dil, w_out, g_post_mix, g_pre_ffn, w_up, conv_w, conv_b, w_down, g_post_ffn, loss_target, m_g_pre_mix, m_w_in, m_b_forget, m_w_o_fox, m_w_o_dil, m_w_out, m_g_post_mix, m_g_pre_ffn, m_w_up, m_conv_w, m_conv_b, m_w_down, m_g_post_ffn, v_g_pre_mix, v_w_in, v_b_forget, v_w_o_fox, v_w_o_dil, v_w_out, v_g_post_mix, v_g_pre_ffn, v_w_up, v_conv_w, v_conv_b, v_w_down, v_g_post_ffn):
    names = ("g_pre_mix", "w_in", "b_forget", "w_o_fox", "w_o_dil", "w_out", "g_post_mix", "g_pre_ffn",
             "w_up", "conv_w", "conv_b", "w_down", "g_post_ffn")
    w = dict(g_pre_mix=g_pre_mix, w_in=w_in, b_forget=b_forget, w_o_fox=w_o_fox, w_o_dil=w_o_dil, w_out=w_out,
             g_post_mix=g_post_mix, g_pre_ffn=g_pre_ffn, w_up=w_up, conv_w=conv_w, conv_b=conv_b, w_down=w_down,
             g_post_ffn=g_post_ffn)
    m = dict(g_pre_mix=m_g_pre_mix, w_in=m_w_in, b_forget=m_b_forget, w_o_fox=m_w_o_fox, w_o_dil=m_w_o_dil,
             w_out=m_w_out, g_post_mix=m_g_post_mix, g_pre_ffn=m_g_pre_ffn, w_up=m_w_up, conv_w=m_conv_w,
             conv_b=m_conv_b, w_down=m_w_down, g_post_ffn=m_g_post_ffn)
    v = dict(g_pre_mix=v_g_pre_mix, w_in=v_w_in, b_forget=v_b_forget, w_o_fox=v_w_o_fox, w_o_dil=v_w_o_dil,
             w_out=v_w_out, g_post_mix=v_g_post_mix, g_pre_ffn=v_g_pre_ffn, w_up=v_w_up, conv_w=v_conv_w,
             conv_b=v_conv_b, w_down=v_w_down, g_post_ffn=v_g_post_ffn)
    sharded = ("w_in", "w_o_fox", "w_o_dil", "w_out", "w_up", "w_down", "conv_w")
    wire = lambda n: F32 if n == "conv_w" else BF16

    gathered = _exchange([w[n][0].astype(wire(n)) for n in sharded], False, name="gather_weights")
    gathered = dict(zip(sharded, gathered))
    by_cols = lambda t: jnp.transpose(t, (1, 0, 2)).reshape(t.shape[1], N_DEV * t.shape[2])
    by_rows = lambda t: t.reshape(N_DEV * t.shape[1], t.shape[2])
    w_in_full = by_cols(gathered["w_in"])
    f_lo, f_hi = 3 * ATT_W, 3 * ATT_W + N_HEADS
    w_main = jnp.concatenate([w_in_full[:, :f_lo], w_in_full[:, f_hi:]], axis=1)
    w_f = jnp.pad(w_in_full[:, f_lo:f_hi], ((0, 0), (0, F_PAD - N_HEADS)))

    sq_err, grad_x, g = _local_step(
        x[0], loss_target[0], w_main, w_f, b_forget, by_cols(gathered["w_o_fox"]), by_cols(gathered["w_o_dil"]),
        by_rows(gathered["w_out"]), by_cols(gathered["w_up"]), by_cols(gathered["conv_w"]), conv_b,
        by_rows(gathered["w_down"]), g_pre_mix, g_post_mix, g_pre_ffn, g_post_ffn)
    loss = lax.psum(0.5 * sq_err / D_MODEL, ("x", "y", "c"))

    g["w_in"] = jnp.concatenate([g["w_main"][:, :f_lo], g["w_f"][:, :N_HEADS], g["w_main"][:, f_lo:]], axis=1)
    col_slots = lambda t: jnp.transpose(t.reshape(t.shape[0], N_DEV, t.shape[1] // N_DEV), (1, 0, 2))
    row_slots = lambda t: t.reshape(N_DEV, t.shape[0] // N_DEV, t.shape[1])
    slots = {n: (row_slots if n in ("w_out", "w_down") else col_slots)(g[n]).astype(wire(n)) for n in sharded}
    parts = dict(zip(sharded, _exchange([slots[n] for n in sharded], True, name="scatter_grads")))
    small_parts = _exchange([_pack_small(g)], False, name="gather_small_grads")[0]

    tiles = dict(w_in=256, w_o_fox=512, w_o_dil=512, w_out=128, w_up=256, w_down=176, conv_w=3)
    res = {n: _adamw(parts[n], w[n][0], m[n][0], v[n][0], name=f"adamw_{n}", tm=tiles[n]) for n in sharded}
    small = _adamw(small_parts, _pack_small(w), _pack_small(m), _pack_small(v), name="adamw_small", tm=SMALL_ROWS)
    shapes = {n: w[n].shape for n, _ in SMALL}
    small = [_unpack_small(t, shapes) for t in small]
    out = [[(res[n][k][None] if n in sharded else small[k][n]) for n in names] for k in range(4)]
    return (loss, grad_x[None], *out[0], *out[1], *out[2], *out[3])
```

```python
import functools
import math

import jax
import jax.numpy as jnp
from jax import lax
from jax.experimental import pallas as pl
from jax.experimental.pallas import tpu as pltpu

F32 = jnp.float32
BF16 = jnp.bfloat16

SEQ = 4096
D_MODEL = 1024
N_HEADS = 8
HEAD_DIM = 64
ATT_W = N_HEADS * HEAD_DIM
D_FF = 2816
Z_MAIN = 5120
F_PAD = 128
ROPE_DIM = 16
ROPE_THETA = 500000.0
RMS_EPS = 1e-6
NEG_INF = -1e30
SCALE = 1.0 / math.sqrt(HEAD_DIM)
DIL_PATTERNS = ((128, 1), (512, 4), (2048, 16))
DIL_BLK = 128
N_DEV = 8

ADAM_LR = 0.001
ADAM_B1 = 0.9
ADAM_B2 = 0.999
ADAM_EPS = 1e-08
ADAM_WD = 0.01
ADAM_STEP = 10

LANE = 128
SUBLANE = 8
VMEM_LIMIT = 56 * 1024 * 1024
MESH_ID = pl.DeviceIdType.MESH
ANY = pl.BlockSpec(memory_space=pl.ANY)


def _params(*sem):
    return pltpu.CompilerParams(dimension_semantics=sem, vmem_limit_bytes=VMEM_LIMIT)


def _sds(shape, dtype):
    return jax.ShapeDtypeStruct(shape, dtype)


def _matmul(a, b, *, ta=False, tb=False, out_dtype, tm, tn, tk, name):
    if ta:
        kk, m = a.shape
    else:
        m, kk = a.shape
    n = b.shape[0] if tb else b.shape[1]
    assert (b.shape[1] if tb else b.shape[0]) == kk
    tm, tn, tk = min(tm, m), min(tn, n), min(tk, kk)
    assert m % tm == 0 and n % tn == 0 and kk % tk == 0, (name, m, n, kk, tm, tn, tk)
    nk = kk // tk
    dims = (((0 if ta else 1,), (1 if tb else 0,)), ((), ()))

    def body(a_ref, b_ref, o_ref, *scratch):
        p = lax.dot_general(a_ref[...].astype(BF16), b_ref[...].astype(BF16), dims,
                            preferred_element_type=F32)
        if nk == 1:
            o_ref[...] = p.astype(o_ref.dtype)
        else:
            acc = scratch[0]
            k = pl.program_id(2)

            @pl.when(k == 0)
            def _():
                acc[...] = p

            @pl.when(k > 0)
            def _():
                acc[...] += p

            @pl.when(k == nk - 1)
            def _():
                o_ref[...] = acc[...].astype(o_ref.dtype)

    a_spec = (pl.BlockSpec((tk, tm), lambda i, j, k: (k, i)) if ta
              else pl.BlockSpec((tm, tk), lambda i, j, k: (i, k)))
    b_spec = (pl.BlockSpec((tn, tk), lambda i, j, k: (j, k)) if tb
              else pl.BlockSpec((tk, tn), lambda i, j, k: (k, j)))
    return pl.pallas_call(
        body, name=name, grid=(m // tm, n // tn, nk),
        in_specs=[a_spec, b_spec],
        out_specs=pl.BlockSpec((tm, tn), lambda i, j, k: (i, j)),
        out_shape=_sds((m, n), out_dtype),
        scratch_shapes=[pltpu.VMEM((tm, tn), F32)] if nk > 1 else [],
        compiler_params=_params("parallel", "parallel", "arbitrary"),
    )(a, b)


def _rms_fwd(x, g, *, name, tm=512):
    def body(x_ref, g_ref, h_ref):
        xv = x_ref[...]
        r = lax.rsqrt(jnp.mean(xv * xv, axis=-1, keepdims=True) + RMS_EPS)
        h_ref[...] = (xv * r * g_ref[...]).astype(h_ref.dtype)

    return pl.pallas_call(
        body, name=name, grid=(SEQ // tm,),
        in_specs=[pl.BlockSpec((tm, D_MODEL), lambda i: (i, 0)), pl.BlockSpec((1, D_MODEL), lambda i: (0, 0))],
        out_specs=pl.BlockSpec((tm, D_MODEL), lambda i: (i, 0)),
        out_shape=_sds((SEQ, D_MODEL), BF16),
        compiler_params=_params("parallel"),
    )(x, g)


def _rms_bwd(dh_parts, xin, g, dres, *, out_dtype, name, tm=512):
    n_parts = len(dh_parts)
    has_res = dres is not None

    def body(*refs):
        parts = refs[:n_parts]
        x_ref, g_ref = refs[n_parts], refs[n_parts + 1]
        res_ref = refs[n_parts + 2] if has_res else None
        o_ref, gg_ref = refs[-2], refs[-1]
        dh = parts[0][...].astype(F32)
        for p in parts[1:]:
            dh = dh + p[...].astype(F32)
        xv = x_ref[...]
        r = lax.rsqrt(jnp.mean(xv * xv, axis=-1, keepdims=True) + RMS_EPS)
        xn = xv * r

        @pl.when(pl.program_id(0) == 0)
        def _():
            gg_ref[...] = jnp.zeros_like(gg_ref)

        gg_ref[...] += jnp.sum(dh * xn, axis=0, keepdims=True)
        dxn = dh * g_ref[...]
        dx = r * (dxn - xn * jnp.mean(dxn * xn, axis=-1, keepdims=True))
        if has_res:
            dx = dx + res_ref[...]
        o_ref[...] = dx.astype(o_ref.dtype)

    row = pl.BlockSpec((tm, D_MODEL), lambda i: (i, 0))
    vec = pl.BlockSpec((1, D_MODEL), lambda i: (0, 0))
    args = list(dh_parts) + [xin, g] + ([dres] if has_res else [])
    return pl.pallas_call(
        body, name=name, grid=(SEQ // tm,),
        in_specs=[row] * n_parts + [row, vec] + ([row] if has_res else []),
        out_specs=[row, vec],
        out_shape=[_sds((SEQ, D_MODEL), out_dtype), _sds((1, D_MODEL), F32)],
        compiler_params=_params("arbitrary"),
    )(*args)


SCAN_BLK = 512


def _split_dot(v, tri):
    hi = v.astype(BF16)
    r1 = v - hi.astype(F32)
    mid = r1.astype(BF16)
    lo = (r1 - mid.astype(F32)).astype(BF16)
    dot = functools.partial(jnp.dot, preferred_element_type=F32)
    return dot(hi, tri) + dot(mid, tri) + dot(lo, tri)


def _fox_prep(fa_t, b_col):
    nblk = SEQ // SCAN_BLK

    def body(fa_ref, b_ref, f_ref, sg_ref):
        row = lax.broadcasted_iota(jnp.int32, (SCAN_BLK, SCAN_BLK), 0)
        col = lax.broadcasted_iota(jnp.int32, (SCAN_BLK, SCAN_BLK), 1)
        upper = (row <= col).astype(BF16)
        carry = jnp.zeros((N_HEADS, 1), F32)
        for blk in range(nblk):
            sl = pl.ds(blk * SCAN_BLK, SCAN_BLK)
            xx = fa_ref[:, sl] + b_ref[...]
            e = jnp.exp(-jnp.abs(xx))
            logf = jnp.minimum(xx, 0.0) - jnp.log(1.0 + e)
            sg_ref[:, sl] = jnp.where(xx >= 0.0, e, 1.0) / (1.0 + e)
            c = _split_dot(logf, upper) + carry
            f_ref[:, sl] = c
            carry = c[:, SCAN_BLK - 1:SCAN_BLK]

    return pl.pallas_call(
        body, name="fox_prep",
        out_shape=[_sds((N_HEADS, SEQ), F32), _sds((N_HEADS, SEQ), F32)],
        compiler_params=pltpu.CompilerParams(vmem_limit_bytes=VMEM_LIMIT),
    )(fa_t, b_col)


def _fox_post_bwd(df_rows_t, df_cols_t, sg_t):
    nblk = SEQ // SCAN_BLK

    def body(dfr_ref, dfc_ref, sg_ref, dfa_ref, gb_ref):
        row = lax.broadcasted_iota(jnp.int32, (SCAN_BLK, SCAN_BLK), 0)
        col = lax.broadcasted_iota(jnp.int32, (SCAN_BLK, SCAN_BLK), 1)
        lower = (row >= col).astype(BF16)
        carry = jnp.zeros((N_HEADS, 1), F32)
        gb = jnp.zeros((N_HEADS, 1), F32)
        for blk in reversed(range(nblk)):
            sl = pl.ds(blk * SCAN_BLK, SCAN_BLK)
            c = _split_dot(dfr_ref[:, sl] + dfc_ref[:, sl], lower) + carry
            carry = c[:, 0:1]
            dfa = c * sg_ref[:, sl]
            dfa_ref[:, sl] = dfa
            gb = gb + jnp.sum(dfa, axis=1, keepdims=True)
        gb_ref[...] = gb

    return pl.pallas_call(
        body, name="fox_post_bwd",
        out_shape=[_sds((N_HEADS, SEQ), F32), _sds((N_HEADS, 1), F32)],
        compiler_params=pltpu.CompilerParams(vmem_limit_bytes=VMEM_LIMIT),
    )(df_rows_t, df_cols_t, sg_t)


FOX_T = 512
NT_DIMS = (((1,), (1,)), ((), ()))
TN_DIMS = (((0,), (0,)), ((), ()))


def _head(ref_or_val, h):
    return ref_or_val[:, h * HEAD_DIM:(h + 1) * HEAD_DIM]


def _split3(v):
    hi = v.astype(BF16).astype(F32)
    r1 = v - hi
    mid = r1.astype(BF16).astype(F32)
    return hi, mid, (r1 - mid).astype(BF16).astype(F32)


def _aux_lanes(rows, terms):
    lane = lax.broadcasted_iota(jnp.int32, (rows, HEAD_DIM), 1)
    out = jnp.zeros((rows, HEAD_DIM), F32)
    for i, t in enumerate(terms):
        out = jnp.where(lane == i, t, out)
    return out


SLOT = 2 * HEAD_DIM
N_SPLIT = 3


def _slot(ref, h):
    return ref[:, h * SLOT:(h + 1) * SLOT]


def _fox_pack_fwd(zm, f_cols, *, tm=512):
    def body(q_ref, k_ref, v_ref, f_ref, qs_ref, ks_ref, vs_ref):
        ones = jnp.ones((tm, HEAD_DIM), BF16)
        for h in range(N_HEADS):
            fh = _split3(f_ref[:, h:h + 1])
            q_aux = _aux_lanes(tm, list(fh) + [1.0] * N_SPLIT)
            k_aux = _aux_lanes(tm, [1.0] * N_SPLIT + [-t for t in fh])
            qs_ref[:, h * SLOT:(h + 1) * SLOT] = jnp.concatenate(
                [(_head(q_ref, h).astype(F32) * SCALE).astype(BF16), q_aux.astype(BF16)], axis=1)
            ks_ref[:, h * SLOT:(h + 1) * SLOT] = jnp.concatenate([_head(k_ref, h), k_aux.astype(BF16)], axis=1)
            vs_ref[:, h * SLOT:(h + 1) * SLOT] = jnp.concatenate([_head(v_ref, h), ones], axis=1)

    col = lambda b: pl.BlockSpec((tm, ATT_W), lambda i: (i, b))
    wide = pl.BlockSpec((tm, N_HEADS * SLOT), lambda i: (i, 0))
    return pl.pallas_call(
        body, name="fox_pack_fwd", grid=(SEQ // tm,),
        in_specs=[col(0), col(1), col(2), pl.BlockSpec((tm, LANE), lambda i: (i, 0))],
        out_specs=[wide] * 3, out_shape=[_sds((SEQ, N_HEADS * SLOT), BF16)] * 3,
        compiler_params=_params("parallel"),
    )(zm, zm, zm, f_cols)


def _fox_pack_bwd(zm, f_cols, lse, o, do, *, tm=512):
    def body(q_ref, f_ref, lse_ref, o_ref, do_ref, qs_ref, ds_ref):
        for h in range(N_HEADS):
            gh = _split3(f_ref[:, h:h + 1] - lse_ref[:, h * HEAD_DIM:h * HEAD_DIM + 1])
            dout = _head(do_ref, h)
            delta = jnp.sum(_head(o_ref, h).astype(F32) * dout.astype(F32), axis=1, keepdims=True)
            q_aux = _aux_lanes(tm, list(gh) + [1.0] * N_SPLIT)
            d_aux = _aux_lanes(tm, [-t for t in _split3(delta)])
            qs_ref[:, h * SLOT:(h + 1) * SLOT] = jnp.concatenate(
                [(_head(q_ref, h).astype(F32) * SCALE).astype(BF16), q_aux.astype(BF16)], axis=1)
            ds_ref[:, h * SLOT:(h + 1) * SLOT] = jnp.concatenate([dout, d_aux.astype(BF16)], axis=1)

    row = pl.BlockSpec((tm, ATT_W), lambda i: (i, 0))
    wide = pl.BlockSpec((tm, N_HEADS * SLOT), lambda i: (i, 0))
    return pl.pallas_call(
        body, name="fox_pack_bwd", grid=(SEQ // tm,),
        in_specs=[row, pl.BlockSpec((tm, LANE), lambda i: (i, 0)), row, row, row],
        out_specs=[wide] * 2, out_shape=[_sds((SEQ, N_HEADS * SLOT), BF16)] * 2,
        compiler_params=_params("parallel"),
    )(zm, f_cols, lse, o, do)


def _causal_pairs(key_major):
    nb = SEQ // FOX_T
    if key_major:
        pairs = [(i, j) for j in range(nb) for i in range(j, nb)]
    else:
        pairs = [(i, j) for i in range(nb) for j in range(i + 1)]
    return (jnp.array([p[0] for p in pairs], jnp.int32), jnp.array([p[1] for p in pairs], jnp.int32), len(pairs))


def _diag_mask():
    row = lax.broadcasted_iota(jnp.int32, (FOX_T, FOX_T), 0)
    col = lax.broadcasted_iota(jnp.int32, (FOX_T, FOX_T), 1)
    return col <= row


def _fox_fwd(q_slots, k_slots, v_slots):
    i_tab, j_tab, n_pairs = _causal_pairs(False)

    def body(i_tab, j_tab, q_ref, k_ref, v_ref, o_ref, lse_ref, m_s, acc_s):
        t = pl.program_id(1)
        i, j = i_tab[t], j_tab[t]

        @pl.when(j == 0)
        def _():
            m_s[...] = jnp.full_like(m_s, NEG_INF)
            acc_s[...] = jnp.zeros_like(acc_s)

        def step(masked):
            for h in range(2):
                s = lax.dot_general(_slot(q_ref, h), _slot(k_ref, h), NT_DIMS, preferred_element_type=F32)
                if masked:
                    s = jnp.where(_diag_mask(), s, NEG_INF)
                m_prev = m_s[h]
                m_new = jnp.maximum(m_prev, jnp.max(s, axis=-1, keepdims=True))
                p = jnp.exp(s - jnp.tile(m_new, (1, FOX_T // LANE)))
                acc_s[h] = jnp.exp(m_prev - m_new) * acc_s[h] + jnp.dot(
                    p.astype(BF16), _slot(v_ref, h), preferred_element_type=F32)
                m_s[h] = m_new

        @pl.when(j < i)
        def _():
            step(False)

        @pl.when(j == i)
        def _():
            step(True)
            outs, lses = [], []
            for h in range(2):
                acc = acc_s[h]
                l = acc[:, HEAD_DIM:]
                outs.append(acc[:, :HEAD_DIM] / l)
                lses.append(m_s[h][:, :HEAD_DIM] + jnp.log(l))
            o_ref[...] = jnp.concatenate(outs, axis=1).astype(o_ref.dtype)
            lse_ref[...] = jnp.concatenate(lses, axis=1)

    qspec = pl.BlockSpec((FOX_T, 2 * SLOT), lambda p, t, it, jt: (it[t], p))
    kspec = pl.BlockSpec((FOX_T, 2 * SLOT), lambda p, t, it, jt: (jt[t], p))
    ospec = pl.BlockSpec((FOX_T, LANE), lambda p, t, it, jt: (it[t], p))
    return pl.pallas_call(
        body, name="fox_fwd",
        grid_spec=pltpu.PrefetchScalarGridSpec(
            num_scalar_prefetch=2, grid=(N_HEADS // 2, n_pairs),
            in_specs=[qspec, kspec, kspec], out_specs=[ospec, ospec],
            scratch_shapes=[pltpu.VMEM((2, FOX_T, LANE), F32), pltpu.VMEM((2, FOX_T, SLOT), F32)]),
        out_shape=[_sds((SEQ, ATT_W), BF16), _sds((SEQ, ATT_W), F32)],
        compiler_params=_params("parallel", "arbitrary"),
    )(i_tab, j_tab, q_slots, k_slots, v_slots)


def _fox_bwd(q_slots, k_slots, v_slots, do_slots):
    i_tab, j_tab, n_pairs = _causal_pairs(True)

    def body(i_tab, j_tab, q_ref, k_ref, v_ref, do_ref, dq_ref, dk_ref, dv_ref):
        t = pl.program_id(1)
        i, j = i_tab[t], j_tab[t]

        @pl.when(t == 0)
        def _():
            dq_ref[...] = jnp.zeros_like(dq_ref)

        @pl.when(i == j)
        def _():
            dk_ref[...] = jnp.zeros_like(dk_ref)
            dv_ref[...] = jnp.zeros_like(dv_ref)

        def step(masked):
            rows = pl.ds(pl.multiple_of(i * FOX_T, FOX_T), FOX_T)
            for h in range(2):
                cols = slice(h * SLOT, (h + 1) * SLOT)
                q, k, v, dout = _slot(q_ref, h), _slot(k_ref, h), _slot(v_ref, h), _slot(do_ref, h)
                p = jnp.exp(lax.dot_general(q, k, NT_DIMS, preferred_element_type=F32))
                if masked:
                    p = jnp.where(_diag_mask(), p, 0.0)
                dp = lax.dot_general(dout, v, NT_DIMS, preferred_element_type=F32)
                ds = (p * dp).astype(BF16)
                dv_ref[:, cols] += lax.dot_general(p.astype(BF16), dout, TN_DIMS, preferred_element_type=F32)
                dk_ref[:, cols] += lax.dot_general(ds, q, TN_DIMS, preferred_element_type=F32)
                dq_ref[rows, cols] += jnp.dot(ds, k, preferred_element_type=F32)

        @pl.when(i > j)
        def _():
            step(False)

        @pl.when(i == j)
        def _():
            step(True)

    qspec = pl.BlockSpec((FOX_T, 2 * SLOT), lambda p, t, it, jt: (it[t], p))
    kspec = pl.BlockSpec((FOX_T, 2 * SLOT), lambda p, t, it, jt: (jt[t], p))
    return pl.pallas_call(
        body, name="fox_bwd",
        grid_spec=pltpu.PrefetchScalarGridSpec(
            num_scalar_prefetch=2, grid=(N_HEADS // 2, n_pairs),
            in_specs=[qspec, kspec, kspec, qspec],
            out_specs=[pl.BlockSpec((SEQ, 2 * SLOT), lambda p, t, it, jt: (0, p)), kspec, kspec]),
        out_shape=[_sds((SEQ, N_HEADS * SLOT), F32)] * 3,
        compiler_params=_params("arbitrary", "arbitrary"),
    )(i_tab, j_tab, q_slots, k_slots, v_slots, do_slots)


def _attn_delta(o, do, *, name, tm=512):
    def body(o_ref, do_ref, d_ref):
        prod = o_ref[...].astype(F32) * do_ref[...].astype(F32)
        lane = lax.broadcasted_iota(jnp.int32, (tm, LANE), 1)
        out = jnp.zeros((tm, LANE), F32)
        for h in range(N_HEADS):
            out = jnp.where(lane == h, jnp.sum(_head(prod, h), axis=1, keepdims=True), out)
        d_ref[...] = out

    row = pl.BlockSpec((tm, ATT_W), lambda i: (i, 0))
    return pl.pallas_call(
        body, name=name, grid=(SEQ // tm,), in_specs=[row, row],
        out_specs=pl.BlockSpec((tm, LANE), lambda i: (i, 0)), out_shape=_sds((SEQ, LANE), F32),
        compiler_params=_params("parallel"),
    )(o, do)


def _rope_tables():
    half = ROPE_DIM // 2
    inv_freq = ROPE_THETA ** (-jnp.arange(half, dtype=F32) * 2.0 / ROPE_DIM)
    ang = jnp.arange(SEQ, dtype=F32)[:, None] * inv_freq[None, :]
    cos, sin = jnp.cos(ang), jnp.sin(ang)
    ones = jnp.ones((SEQ, HEAD_DIM - ROPE_DIM), F32)
    zeros = jnp.zeros((SEQ, HEAD_DIM - ROPE_DIM), F32)
    zh = jnp.zeros((SEQ, half), F32)
    c_tab = jnp.concatenate([cos, cos, ones], axis=1)
    a_tab = jnp.concatenate([-sin, zh, zeros], axis=1)
    b_tab = jnp.concatenate([zh, sin, zeros], axis=1)
    two = lambda t: jnp.concatenate([t, t], axis=1)
    return two(c_tab), two(a_tab), two(b_tab)


def _rotate(x, c_tab, a_tab, b_tab):
    return x * c_tab + pltpu.roll(x, LANE - ROPE_DIM // 2, 1) * a_tab + pltpu.roll(x, ROPE_DIM // 2, 1) * b_tab


def _rope_fwd(zm, tabs, *, tm=512):
    def body(q_ref, k_ref, c_ref, a_ref, b_ref, o_ref):
        for part, x_ref in enumerate((q_ref, k_ref)):
            for cc in range(ATT_W // LANE):
                sl = slice(cc * LANE, (cc + 1) * LANE)
                o_ref[:, part * ATT_W + cc * LANE:part * ATT_W + (cc + 1) * LANE] = _rotate(
                    x_ref[:, sl].astype(F32), c_ref[...], a_ref[...], b_ref[...]).astype(o_ref.dtype)

    tab = pl.BlockSpec((tm, LANE), lambda i: (i, 0))
    return pl.pallas_call(
        body, name="rope_fwd", grid=(SEQ // tm,),
        in_specs=[pl.BlockSpec((tm, ATT_W), lambda i: (i, 3)), pl.BlockSpec((tm, ATT_W), lambda i: (i, 4)),
                  tab, tab, tab],
        out_specs=pl.BlockSpec((tm, 2 * ATT_W), lambda i: (i, 0)),
        out_shape=_sds((SEQ, 2 * ATT_W), BF16),
        compiler_params=_params("parallel"),
    )(zm, zm, *tabs)


def _dil_grad_combine(dqs, dks, dvs, tabs, *, tm=256):
    def body(*refs):
        q_refs, k_refs, v_refs = refs[0:3], refs[3:6], refs[6:9]
        c_ref, a_ref, b_ref, o_ref = refs[9:]
        total = lambda rs, sl: rs[0][:, sl] + rs[1][:, sl] + rs[2][:, sl]
        for cc in range(ATT_W // LANE):
            sl = slice(cc * LANE, (cc + 1) * LANE)
            for part, rs in enumerate((q_refs, k_refs)):
                o_ref[:, part * ATT_W + cc * LANE:part * ATT_W + (cc + 1) * LANE] = _rotate(
                    total(rs, sl), c_ref[...], -a_ref[...], -b_ref[...]).astype(o_ref.dtype)
            o_ref[:, 2 * ATT_W + cc * LANE:2 * ATT_W + (cc + 1) * LANE] = total(v_refs, sl).astype(o_ref.dtype)

    row = pl.BlockSpec((tm, ATT_W), lambda i: (i, 0))
    tab = pl.BlockSpec((tm, LANE), lambda i: (i, 0))
    return pl.pallas_call(
        body, name="dil_grad_combine", grid=(SEQ // tm,),
        in_specs=[row] * 9 + [tab] * 3,
        out_specs=pl.BlockSpec((tm, 3 * ATT_W), lambda i: (i, 0)),
        out_shape=_sds((SEQ, 3 * ATT_W), BF16),
        compiler_params=_params("parallel"),
    )(*dqs, *dks, *dvs, *tabs)


def _dil_valid(n):
    qi = lax.broadcasted_iota(jnp.int32, (DIL_BLK, 2 * DIL_BLK), 0)
    ki = lax.broadcasted_iota(jnp.int32, (DIL_BLK, 2 * DIL_BLK), 1)
    dist = qi + DIL_BLK - ki
    return (dist >= 0) & (dist <= DIL_BLK) & ((n > 0) | (ki >= DIL_BLK))


def _dil_views(qk, zm, d):
    length = SEQ // d
    return qk.reshape(length, d * 2 * ATT_W), zm.reshape(length, d * Z_MAIN), length // DIL_BLK


def _dil_fwd(qk, zm, d):
    qk_v, zm_v, nb = _dil_views(qk, zm, d)
    length = SEQ // d
    v_blk = Z_MAIN // ATT_W

    def body(q_ref, kp_ref, kc_ref, vp_ref, vc_ref, o_ref, lse_ref):
        n = pl.program_id(1)
        ok = _dil_valid(n)
        lane = lax.broadcasted_iota(jnp.int32, (DIL_BLK, LANE), 1)
        lse_all = jnp.zeros((DIL_BLK, LANE), F32)
        outs = []
        for h in range(N_HEADS):
            kk = jnp.concatenate([_head(kp_ref, h), _head(kc_ref, h)], axis=0)
            vv = jnp.concatenate([_head(vp_ref, h), _head(vc_ref, h)], axis=0)
            s = lax.dot_general(_head(q_ref, h), kk, NT_DIMS, preferred_element_type=F32) * SCALE
            s = jnp.where(ok, s, NEG_INF)
            m = jnp.max(s, axis=-1, keepdims=True)
            p = jnp.exp(s - m)
            l = jnp.sum(p, axis=-1, keepdims=True)
            outs.append(jnp.dot(p.astype(BF16), vv, preferred_element_type=F32) / l)
            lse_all = jnp.where(lane == h, m + jnp.log(l), lse_all)
        o_ref[...] = jnp.concatenate(outs, axis=1)
        lse_ref[...] = lse_all

    blk = lambda f: pl.BlockSpec((DIL_BLK, ATT_W), f)
    prev = lambda n: jnp.maximum(n - 1, 0)
    o, lse = pl.pallas_call(
        body, name=f"dil_fwd_d{d}", grid=(d, nb),
        in_specs=[blk(lambda r, n: (n, 2 * r)),
                  blk(lambda r, n: (prev(n), 2 * r + 1)), blk(lambda r, n: (n, 2 * r + 1)),
                  blk(lambda r, n: (prev(n), v_blk * r + 5)), blk(lambda r, n: (n, v_blk * r + 5))],
        out_specs=[blk(lambda r, n: (n, r)), pl.BlockSpec((DIL_BLK, LANE), lambda r, n: (n, r))],
        out_shape=[_sds((length, d * ATT_W), F32), _sds((length, d * LANE), F32)],
        compiler_params=_params("parallel", "arbitrary"),
    )(qk_v, qk_v, qk_v, zm_v, zm_v)
    return o.reshape(SEQ, ATT_W), lse.reshape(SEQ, LANE)


def _dil_merge(os_, lses, *, tm=512):
    def body(o0, o1, o2, l0, l1, l2, y_ref, lse_ref):
        ls = [l0[...], l1[...], l2[...]]
        m = jnp.maximum(jnp.maximum(ls[0], ls[1]), ls[2])
        es = [jnp.exp(l - m) for l in ls]
        tot = es[0] + es[1] + es[2]
        lse_ref[...] = m + jnp.log(tot)
        alphas = [e / tot for e in es]
        outs = []
        for h in range(N_HEADS):
            acc = None
            for g, o_ref in enumerate((o0, o1, o2)):
                term = alphas[g][:, h:h + 1] * _head(o_ref, h)
                acc = term if acc is None else acc + term
            outs.append(acc)
        y_ref[...] = jnp.concatenate(outs, axis=1).astype(y_ref.dtype)

    row = pl.BlockSpec((tm, ATT_W), lambda i: (i, 0))
    vec = pl.BlockSpec((tm, LANE), lambda i: (i, 0))
    return pl.pallas_call(
        body, name="dil_merge", grid=(SEQ // tm,),
        in_specs=[row] * 3 + [vec] * 3, out_specs=[row, vec],
        out_shape=[_sds((SEQ, ATT_W), BF16), _sds((SEQ, LANE), F32)],
        compiler_params=_params("parallel"),
    )(*os_, *lses)


def _dil_bwd(qk, zm, lse, delta, do, d):
    qk_v, zm_v, nb = _dil_views(qk, zm, d)
    length = SEQ // d
    v_blk = Z_MAIN // ATT_W
    lse_v, dl_v, do_v = lse.reshape(length, d * LANE), delta.reshape(length, d * LANE), do.reshape(length, d * ATT_W)

    def body(q_ref, kp_ref, kc_ref, vp_ref, vc_ref, lse_ref, dl_ref, do_ref,
             dq_ref, dk_ref, dv_ref, ck_s, cv_s):
        n = pl.program_id(1)

        @pl.when(n == 0)
        def _():
            ck_s[...] = jnp.zeros_like(ck_s)
            cv_s[...] = jnp.zeros_like(cv_s)

        @pl.when(n < nb)
        def _():
            ok = _dil_valid(n)
            for h in range(N_HEADS):
                cols = slice(h * HEAD_DIM, (h + 1) * HEAD_DIM)
                q, dout = _head(q_ref, h), _head(do_ref, h)
                kk = jnp.concatenate([_head(kp_ref, h), _head(kc_ref, h)], axis=0)
                vv = jnp.concatenate([_head(vp_ref, h), _head(vc_ref, h)], axis=0)
                s = lax.dot_general(q, kk, NT_DIMS, preferred_element_type=F32) * SCALE
                p = jnp.where(ok, jnp.exp(s - lse_ref[:, h:h + 1]), 0.0)
                dp = lax.dot_general(dout, vv, NT_DIMS, preferred_element_type=F32)
                ds = (p * (dp - dl_ref[:, h:h + 1])).astype(BF16)
                dq_ref[:, cols] = jnp.dot(ds, kk, preferred_element_type=F32) * SCALE
                dkk = lax.dot_general(ds, q, TN_DIMS, preferred_element_type=F32) * SCALE
                dvv = lax.dot_general(p.astype(BF16), dout, TN_DIMS, preferred_element_type=F32)
                dk_ref[:, cols] = ck_s[:, cols] + dkk[:DIL_BLK]
                dv_ref[:, cols] = cv_s[:, cols] + dvv[:DIL_BLK]
                ck_s[:, cols] = dkk[DIL_BLK:]
                cv_s[:, cols] = dvv[DIL_BLK:]

        @pl.when(n == nb)
        def _():
            dk_ref[...] = ck_s[...]
            dv_ref[...] = cv_s[...]

    blk = lambda f: pl.BlockSpec((DIL_BLK, ATT_W), f)
    vec = lambda f: pl.BlockSpec((DIL_BLK, LANE), f)
    cur = lambda n: jnp.minimum(n, nb - 1)
    prev = lambda n: jnp.maximum(cur(n) - 1, 0)
    back = lambda n: jnp.maximum(n - 1, 0)
    outs = pl.pallas_call(
        body, name=f"dil_bwd_d{d}", grid=(d, nb + 1),
        in_specs=[blk(lambda r, n: (cur(n), 2 * r)),
                  blk(lambda r, n: (prev(n), 2 * r + 1)), blk(lambda r, n: (cur(n), 2 * r + 1)),
                  blk(lambda r, n: (prev(n), v_blk * r + 5)), blk(lambda r, n: (cur(n), v_blk * r + 5)),
                  vec(lambda r, n: (cur(n), r)), vec(lambda r, n: (cur(n), r)),
                  blk(lambda r, n: (cur(n), r))],
        out_specs=[blk(lambda r, n: (cur(n), r)), blk(lambda r, n: (back(n), r)), blk(lambda r, n: (back(n), r))],
        out_shape=[_sds((length, d * ATT_W), F32)] * 3,
        scratch_shapes=[pltpu.VMEM((DIL_BLK, ATT_W), F32), pltpu.VMEM((DIL_BLK, ATT_W), F32)],
        compiler_params=_params("arbitrary", "arbitrary"),
    )(qk_v, qk_v, qk_v, zm_v, zm_v, lse_v, dl_v, do_v)
    return [t.reshape(SEQ, ATT_W) for t in outs]


def _sigmoid(x):
    return 1.0 / (1.0 + jnp.exp(-x))


def _mix_fwd(ya, yb, w_oa, w_ob, zm, *, tm=512):
    def body(ya_ref, yb_ref, wa_ref, wb_ref, ga_ref, gb_ref, pa_ref, pb_ref, mix_ref):
        pa = jnp.dot(ya_ref[...], wa_ref[...], preferred_element_type=F32)
        pb = jnp.dot(yb_ref[...], wb_ref[...], preferred_element_type=F32)
        pa_ref[...] = pa.astype(pa_ref.dtype)
        pb_ref[...] = pb.astype(pb_ref.dtype)
        mix_ref[...] = (_sigmoid(ga_ref[...].astype(F32)) * pa + _sigmoid(gb_ref[...].astype(F32)) * pb
                        ).astype(mix_ref.dtype)

    row = pl.BlockSpec((tm, ATT_W), lambda i: (i, 0))
    wsp = pl.BlockSpec((ATT_W, D_MODEL), lambda i: (0, 0))
    wide = pl.BlockSpec((tm, D_MODEL), lambda i: (i, 0))
    return pl.pallas_call(
        body, name="mix_fwd", grid=(SEQ // tm,),
        in_specs=[row, row, wsp, wsp, pl.BlockSpec((tm, D_MODEL), lambda i: (i, 3)),
                  pl.BlockSpec((tm, D_MODEL), lambda i: (i, 4))],
        out_specs=[wide] * 3, out_shape=[_sds((SEQ, D_MODEL), BF16)] * 3,
        compiler_params=_params("parallel"),
    )(ya, yb, w_oa, w_ob, zm, zm)


def _gate_bwd(dmix, zm, pa, pb, *, tm=512):
    def body(dm_ref, ga_ref, gb_ref, pa_ref, pb_ref, dpa_ref, dpb_ref, dg_ref):
        dm = dm_ref[...].astype(F32)
        sa, sb = _sigmoid(ga_ref[...].astype(F32)), _sigmoid(gb_ref[...].astype(F32))
        dpa_ref[...] = (dm * sa).astype(dpa_ref.dtype)
        dpb_ref[...] = (dm * sb).astype(dpb_ref.dtype)
        dg_ref[:, :D_MODEL] = (dm * pa_ref[...].astype(F32) * sa * (1.0 - sa)).astype(dg_ref.dtype)
        dg_ref[:, D_MODEL:] = (dm * pb_ref[...].astype(F32) * sb * (1.0 - sb)).astype(dg_ref.dtype)

    wide = pl.BlockSpec((tm, D_MODEL), lambda i: (i, 0))
    return pl.pallas_call(
        body, name="gate_bwd", grid=(SEQ // tm,),
        in_specs=[wide, pl.BlockSpec((tm, D_MODEL), lambda i: (i, 3)), pl.BlockSpec((tm, D_MODEL), lambda i: (i, 4)),
                  wide, wide],
        out_specs=[wide, wide, pl.BlockSpec((tm, 2 * D_MODEL), lambda i: (i, 0))],
        out_shape=[_sds((SEQ, D_MODEL), BF16), _sds((SEQ, D_MODEL), BF16), _sds((SEQ, 2 * D_MODEL), BF16)],
        compiler_params=_params("parallel"),
    )(dmix, zm, zm, pa, pb)


def _out_fwd(mixed, w_out, x, g_post, g_pre, *, tm=512):
    def body(m_ref, w_ref, x_ref, gp_ref, gn_ref, y_ref, x2_ref, h_ref):
        y = jnp.dot(m_ref[...], w_ref[...], preferred_element_type=F32)
        y_ref[...] = y
        r = lax.rsqrt(jnp.mean(y * y, axis=-1, keepdims=True) + RMS_EPS)
        x2 = x_ref[...] + y * r * gp_ref[...]
        x2_ref[...] = x2
        r2 = lax.rsqrt(jnp.mean(x2 * x2, axis=-1, keepdims=True) + RMS_EPS)
        h_ref[...] = (x2 * r2 * gn_ref[...]).astype(h_ref.dtype)

    row = pl.BlockSpec((tm, D_MODEL), lambda i: (i, 0))
    vec = pl.BlockSpec((1, D_MODEL), lambda i: (0, 0))
    return pl.pallas_call(
        body, name="out_fwd", grid=(SEQ // tm,),
        in_specs=[row, pl.BlockSpec((D_MODEL, D_MODEL), lambda i: (0, 0)), row, vec, vec],
        out_specs=[row] * 3,
        out_shape=[_sds((SEQ, D_MODEL), F32), _sds((SEQ, D_MODEL), F32), _sds((SEQ, D_MODEL), BF16)],
        compiler_params=_params("parallel"),
    )(mixed, w_out, x, g_post, g_pre)


FFN_TM = 256
FFN_TN = 256
FFN_NJ = D_FF // FFN_TN


def _gelu_parts(a):
    c = math.sqrt(2.0 / math.pi)
    t = jnp.tanh(c * (a + 0.044715 * a * a * a))
    gelu = 0.5 * a * (1.0 + t)
    dgelu = 0.5 * (1.0 + t) + 0.5 * a * (1.0 - t * t) * c * (1.0 + 3.0 * 0.044715 * a * a)
    return gelu, dgelu


def _shift_down(u, halo, k):
    s = pltpu.roll(u, k, 0)
    hs = pltpu.roll(halo, k, 0)
    row = lax.broadcasted_iota(jnp.int32, hs.shape, 0)
    top = jnp.where(row < k, hs, s[:SUBLANE])
    return jnp.concatenate([top, s[SUBLANE:]], axis=0)


def _shift_up(u, halo, k):
    rows = u.shape[0]
    s = pltpu.roll(u, rows - k, 0)
    hs = pltpu.roll(halo, SUBLANE - k, 0)
    row = lax.broadcasted_iota(jnp.int32, hs.shape, 0)
    bottom = jnp.where(row >= SUBLANE - k, hs, s[rows - SUBLANE:])
    return jnp.concatenate([s[:rows - SUBLANE], bottom], axis=0)


def _conv_taps(u, halo, w_ref, b_ref):
    s1, s2 = _shift_down(u, halo, 1), _shift_down(u, halo, 2)
    return w_ref[0:1, :] * s2 + w_ref[1:2, :] * s1 + w_ref[2:3, :] * u + b_ref[...], s1, s2


def _ffn_specs(rev):
    nrow = SEQ // FFN_TM
    per = FFN_TM // SUBLANE
    ri = (lambda i: nrow - 1 - i) if rev else (lambda i: i)
    main = lambda off: pl.BlockSpec((FFN_TM, FFN_TN), lambda j, i: (ri(i), j + off))
    halo = lambda off: pl.BlockSpec((SUBLANE, FFN_TN), lambda j, i: (jnp.maximum(ri(i) * per - 1, 0), j + off))
    wsp = lambda off: pl.BlockSpec((3, FFN_TN), lambda j, i: (0, j + off))
    bsp = lambda off: pl.BlockSpec((1, FFN_TN), lambda j, i: (0, j + off))
    return ri, main, halo, wsp, bsp


def _ffn_mid_fwd(u, conv_w, conv_b):
    ri, main, halo, wsp, bsp = _ffn_specs(False)

    def body(ua_ref, ub_ref, ha_ref, hb_ref, wa_ref, wb_ref, ba_ref, bb_ref, m_ref):
        live = (pl.program_id(1) > 0).astype(F32)
        a, _, _ = _conv_taps(ua_ref[...].astype(F32), ha_ref[...].astype(F32) * live, wa_ref, ba_ref)
        b, _, _ = _conv_taps(ub_ref[...].astype(F32), hb_ref[...].astype(F32) * live, wb_ref, bb_ref)
        m_ref[...] = (_gelu_parts(a)[0] * b).astype(m_ref.dtype)

    return pl.pallas_call(
        body, name="ffn_mid_fwd", grid=(FFN_NJ, SEQ // FFN_TM),
        in_specs=[main(0), main(FFN_NJ), halo(0), halo(FFN_NJ), wsp(0), wsp(FFN_NJ), bsp(0), bsp(FFN_NJ)],
        out_specs=pl.BlockSpec((FFN_TM, FFN_TN), lambda j, i: (i, j)),
        out_shape=_sds((SEQ, D_FF), BF16),
        compiler_params=_params("parallel", "arbitrary"),
    )(u, u, u, u, conv_w, conv_w, conv_b, conv_b)


def _ffn_mid_bwd(dm, u, conv_w, conv_b):
    ri, main, halo, wsp, bsp = _ffn_specs(True)
    nrow = SEQ // FFN_TM

    def body(dm_ref, ua_ref, ub_ref, ha_ref, hb_ref, wa_ref, wb_ref, ba_ref, bb_ref,
             dua_ref, dub_ref, gwa_ref, gwb_ref, gba_ref, gbb_ref, ca_s, cb_s):
        i = pl.program_id(1)
        live = (i < nrow - 1).astype(F32)

        @pl.when(i == 0)
        def _():
            ca_s[...] = jnp.zeros_like(ca_s)
            cb_s[...] = jnp.zeros_like(cb_s)
            for r in (gwa_ref, gwb_ref, gba_ref, gbb_ref):
                r[...] = jnp.zeros_like(r)

        ua, ub = ua_ref[...].astype(F32), ub_ref[...].astype(F32)
        a, a1, a2 = _conv_taps(ua, ha_ref[...].astype(F32) * live, wa_ref, ba_ref)
        b, b1, b2 = _conv_taps(ub, hb_ref[...].astype(F32) * live, wb_ref, bb_ref)
        gelu, dgelu = _gelu_parts(a)
        dmv = dm_ref[...].astype(F32)
        for du, s0, s1, s2, w_ref, c_s, du_ref, gw_ref, gb_ref in (
                (dmv * b * dgelu, ua, a1, a2, wa_ref, ca_s, dua_ref, gwa_ref, gba_ref),
                (dmv * gelu, ub, b1, b2, wb_ref, cb_s, dub_ref, gwb_ref, gbb_ref)):
            gw_ref[0:1, :] += jnp.sum(du * s2, axis=0, keepdims=True)
            gw_ref[1:2, :] += jnp.sum(du * s1, axis=0, keepdims=True)
            gw_ref[2:3, :] += jnp.sum(du * s0, axis=0, keepdims=True)
            gb_ref[...] += jnp.sum(du, axis=0, keepdims=True)
            nxt = c_s[...]
            du_ref[...] = (w_ref[2:3, :] * du + w_ref[1:2, :] * _shift_up(du, nxt, 1)
                           + w_ref[0:1, :] * _shift_up(du, nxt, 2)).astype(du_ref.dtype)
            c_s[...] = du[:SUBLANE]

    acc3 = pl.BlockSpec((3, FFN_TN), lambda j, i: (0, j))
    acc1 = pl.BlockSpec((1, FFN_TN), lambda j, i: (0, j))
    out_blk = pl.BlockSpec((FFN_TM, FFN_TN), lambda j, i: (ri(i), j))
    return pl.pallas_call(
        body, name="ffn_mid_bwd", grid=(FFN_NJ, nrow),
        in_specs=[pl.BlockSpec((FFN_TM, FFN_TN), lambda j, i: (ri(i), j)),
                  main(0), main(FFN_NJ), halo(0), halo(FFN_NJ), wsp(0), wsp(FFN_NJ), bsp(0), bsp(FFN_NJ)],
        out_specs=[out_blk, out_blk, acc3, acc3, acc1, acc1],
        out_shape=[_sds((SEQ, D_FF), BF16), _sds((SEQ, D_FF), BF16), _sds((3, D_FF), F32), _sds((3, D_FF), F32),
                   _sds((1, D_FF), F32), _sds((1, D_FF), F32)],
        scratch_shapes=[pltpu.VMEM((SUBLANE, FFN_TN), F32), pltpu.VMEM((SUBLANE, FFN_TN), F32)],
        compiler_params=_params("parallel", "arbitrary"),
    )(dm, u, u, u, u, conv_w, conv_w, conv_b, conv_b)


def _down_fwd(m, w_down, x2, g_post, target, *, tm=512):
    def body(m_ref, w_ref, x2_ref, g_ref, t_ref, dout_ref, dy_ref, gg_ref, loss_ref):
        @pl.when(pl.program_id(0) == 0)
        def _():
            gg_ref[...] = jnp.zeros_like(gg_ref)
            loss_ref[...] = jnp.zeros_like(loss_ref)

        y = jnp.dot(m_ref[...], w_ref[...], preferred_element_type=F32)
        r = lax.rsqrt(jnp.mean(y * y, axis=-1, keepdims=True) + RMS_EPS)
        yn = y * r
        diff = (x2_ref[...] + yn * g_ref[...]) - t_ref[...]
        loss_ref[...] += jnp.sum(diff * diff)
        dout = diff * (1.0 / D_MODEL)
        dout_ref[...] = dout
        gg_ref[...] += jnp.sum(dout * yn, axis=0, keepdims=True)
        dn = dout * g_ref[...]
        dy_ref[...] = (r * (dn - yn * jnp.mean(dn * yn, axis=-1, keepdims=True))).astype(dy_ref.dtype)

    row = pl.BlockSpec((tm, D_MODEL), lambda i: (i, 0))
    vec = pl.BlockSpec((1, D_MODEL), lambda i: (0, 0))
    return pl.pallas_call(
        body, name="down_fwd", grid=(SEQ // tm,),
        in_specs=[pl.BlockSpec((tm, D_FF), lambda i: (i, 0)), pl.BlockSpec((D_FF, D_MODEL), lambda i: (0, 0)),
                  row, vec, row],
        out_specs=[row, row, vec, pl.BlockSpec((1, LANE), lambda i: (0, 0))],
        out_shape=[_sds((SEQ, D_MODEL), F32), _sds((SEQ, D_MODEL), BF16), _sds((1, D_MODEL), F32),
                   _sds((1, LANE), F32)],
        compiler_params=_params("arbitrary"),
    )(m, w_down, x2, g_post, target)


def _local_step(x, target, w_main, w_f, b_forget, conv_b, g_pre_mix, g_post_mix, g_pre_ffn, g_post_ffn,
                late_weights, ffn_grads_ready, mixer_grads_ready):
    mm = _matmul
    tabs = _rope_tables()

    h1 = _rms_fwd(x, g_pre_mix, name="rms_pre_mix")
    zm = mm(h1, w_main, out_dtype=BF16, tm=1024, tn=512, tk=1024, name="in_proj")
    zf = mm(h1, w_f, out_dtype=F32, tm=1024, tn=F_PAD, tk=1024, name="in_proj_forget")
    f_row, sg_row = _fox_prep(zf[:, :N_HEADS].T, b_forget.reshape(N_HEADS, 1))
    f_cols = jnp.pad(f_row.T, ((0, 0), (0, LANE - N_HEADS)))
    q_slots, k_slots, v_slots = _fox_pack_fwd(zm, f_cols)
    ya, lse_a = _fox_fwd(q_slots, k_slots, v_slots)
    qk = _rope_fwd(zm, tabs)
    dil = [_dil_fwd(qk, zm, d) for _, d in DIL_PATTERNS]
    yb, lse_b = _dil_merge([o for o, _ in dil], [l for _, l in dil])
    w_oa, w_ob, w_out, w_up, conv_w, w_down = late_weights(yb)
    pa, pb, mixed = _mix_fwd(ya, yb, w_oa, w_ob, zm)
    y1, x2, h2 = _out_fwd(mixed, w_out, x, g_post_mix, g_pre_ffn)
    u = mm(h2, w_up, out_dtype=BF16, tm=1024, tn=512, tk=1024, name="up_proj")
    m = _ffn_mid_fwd(u, conv_w, conv_b)
    dout, dy2, gg_post_ffn, sq_err = _down_fwd(m, w_down, x2, g_post_ffn, target)

    g_w_down = mm(m, dy2, ta=True, out_dtype=F32, tm=256, tn=1024, tk=1024, name="grad_w_down")
    dm = mm(dy2, w_down, tb=True, out_dtype=BF16, tm=1024, tn=256, tk=1024, name="d_ffn_mid")
    du_a, du_b, gcw_a, gcw_b, gcb_a, gcb_b = _ffn_mid_bwd(dm, u, conv_w, conv_b)
    g_w_up = [mm(h2, t, ta=True, out_dtype=F32, tm=512, tn=256, tk=1024, name=f"grad_w_up_{s}")
              for s, t in (("a", du_a), ("b", du_b))]
    dh2 = [mm(t, w_up[:, o:o + D_FF], tb=True, out_dtype=F32, tm=1024, tn=512, tk=D_FF, name=f"d_h2_{s}")
           for s, t, o in (("a", du_a, 0), ("b", du_b, D_FF))]
    dx2, gg_pre_ffn = _rms_bwd(dh2, x2, g_pre_ffn, dout, out_dtype=F32, name="rms_pre_ffn_bwd")
    tok = ffn_grads_ready(dict(w_down=g_w_down, w_up=jnp.concatenate(g_w_up, axis=1),
                               conv_w=jnp.concatenate([gcw_a, gcw_b], axis=1)))

    dy1, gg_post_mix = _rms_bwd([dx2], y1, g_post_mix + tok, None, out_dtype=BF16, name="rms_post_mix_bwd")
    g_w_out = mm(mixed, dy1, ta=True, out_dtype=F32, tm=512, tn=1024, tk=1024, name="grad_w_out")
    dmix = mm(dy1, w_out, tb=True, out_dtype=BF16, tm=1024, tn=512, tk=1024, name="d_mixed")
    dpa, dpb, dgates = _gate_bwd(dmix, zm, pa, pb)
    g_w_oa = mm(ya, dpa, ta=True, out_dtype=F32, tm=512, tn=1024, tk=1024, name="grad_w_o_fox")
    g_w_ob = mm(yb, dpb, ta=True, out_dtype=F32, tm=512, tn=1024, tk=1024, name="grad_w_o_dil")
    dya = mm(dpa, w_oa, tb=True, out_dtype=BF16, tm=1024, tn=512, tk=1024, name="d_y_fox")
    dyb = mm(dpb, w_ob, tb=True, out_dtype=BF16, tm=1024, tn=512, tk=1024, name="d_y_dil")

    qb_slots, do_slots = _fox_pack_bwd(zm, f_cols, lse_a, ya, dya)
    dq_s, dk_s, dv_s = [t.reshape(SEQ, N_HEADS, SLOT) for t in _fox_bwd(qb_slots, k_slots, v_slots, do_slots)]
    data = lambda t: t[:, :, :HEAD_DIM].reshape(SEQ, ATT_W)
    dq_a, dk_a, dv_a = data(dq_s) * SCALE, data(dk_s), data(dv_s)
    dfa_t, g_b_forget = _fox_post_bwd(dq_s[:, :, HEAD_DIM].T, -dk_s[:, :, HEAD_DIM + N_SPLIT].T, sg_row)

    delta_b = _attn_delta(yb, dyb, name="delta_dil")
    dil_g = [_dil_bwd(qk, zm, lse_b, delta_b, dyb, d) for _, d in DIL_PATTERNS]
    d_dil = _dil_grad_combine([g[0] for g in dil_g], [g[1] for g in dil_g], [g[2] for g in dil_g], tabs)

    dz = jnp.concatenate([dq_a.astype(BF16), dk_a.astype(BF16), dv_a.astype(BF16), d_dil, dgates], axis=1)
    dzf = jnp.pad(dfa_t.T, ((0, 0), (0, F_PAD - N_HEADS)))
    g_w_main = mm(h1, dz, ta=True, out_dtype=F32, tm=512, tn=512, tk=1024, name="grad_w_in")
    g_w_f = mm(h1, dzf, ta=True, out_dtype=F32, tm=512, tn=F_PAD, tk=1024, name="grad_w_in_forget")
    tok = mixer_grads_ready(dict(w_main=g_w_main, w_f=g_w_f, w_o_fox=g_w_oa, w_o_dil=g_w_ob, w_out=g_w_out))
    dh1 = [mm(dz, w_main, tb=True, out_dtype=F32, tm=1024, tn=512, tk=1024, name="d_h1"),
           mm(dzf + tok, w_f, tb=True, out_dtype=F32, tm=1024, tn=512, tk=F_PAD, name="d_h1_forget")]
    grad_x, gg_pre_mix = _rms_bwd(dh1, x, g_pre_mix, dx2, out_dtype=F32, name="rms_pre_mix_bwd")

    grads = dict(
        b_forget=g_b_forget.reshape(1, N_HEADS), conv_b=jnp.concatenate([gcb_a, gcb_b], axis=1),
        g_pre_mix=gg_pre_mix, g_post_mix=gg_post_mix, g_pre_ffn=gg_pre_ffn, g_post_ffn=gg_post_ffn)
    return sq_err[0, 0], grad_x, grads


def _exchange(arrays, scatter, *, name):
    n = len(arrays)

    def body(*refs):
        ins, outs = refs[:n], refs[n:2 * n]
        send_sems, recv_sems, local_sems = refs[2 * n:]
        x, y, c = lax.axis_index("x"), lax.axis_index("y"), lax.axis_index("c")
        me = 4 * x + 2 * y + c
        peers = []
        for k in range(1, N_DEV):
            px = 1 - x if k & 4 else x
            py = 1 - y if k & 2 else y
            pc = 1 - c if k & 1 else c
            peers.append(((px, py, pc), 4 * px + 2 * py + pc))

        def remote(a, k):
            dev, slot = peers[k]
            return pltpu.make_async_remote_copy(
                src_ref=ins[a].at[slot] if scatter else ins[a], dst_ref=outs[a].at[me],
                send_sem=send_sems.at[a, k], recv_sem=recv_sems.at[a, k],
                device_id=dev, device_id_type=MESH_ID)

        def landed(a, k):
            dev, slot = peers[k]
            return pltpu.make_async_remote_copy(
                src_ref=outs[a].at[slot], dst_ref=outs[a].at[slot],
                send_sem=send_sems.at[a, k], recv_sem=recv_sems.at[a, k],
                device_id=dev, device_id_type=MESH_ID)

        own = [pltpu.make_async_copy(ins[a].at[me] if scatter else ins[a], outs[a].at[me], local_sems.at[a])
               for a in range(n)]
        copies = [remote(a, k) for k in range(N_DEV - 1) for a in range(n)]
        for cp in own + copies:
            cp.start()
        for k in range(N_DEV - 1):
            for a in range(n):
                landed(a, k).wait_recv()
        for cp in copies:
            cp.wait_send()
        for cp in own:
            cp.wait()

    out_shape = [_sds(((N_DEV,) + a.shape[-2:]), a.dtype) for a in arrays]
    return pl.pallas_call(
        body, name=name, in_specs=[ANY] * n, out_specs=[ANY] * n, out_shape=out_shape,
        scratch_shapes=[pltpu.SemaphoreType.DMA((n, N_DEV - 1)), pltpu.SemaphoreType.DMA((n, N_DEV - 1)),
                        pltpu.SemaphoreType.DMA((n,))],
    )(*arrays)


def _peers():
    x, y, c = lax.axis_index("x"), lax.axis_index("y"), lax.axis_index("c")
    out = []
    for k in range(1, N_DEV):
        px = 1 - x if k & 4 else x
        py = 1 - y if k & 2 else y
        pc = 1 - c if k & 1 else c
        out.append(((px, py, pc), 4 * px + 2 * py + pc))
    return 4 * x + 2 * y + c, out


HBM = pl.BlockSpec(memory_space=pltpu.HBM)
SEM = pl.BlockSpec(memory_space=pltpu.SEMAPHORE)
DATAFLOW = pltpu.SideEffectType.DATAFLOW_SIDE_EFFECTING


def _split_copy(srcs, lands, send_sems, recv_sems, scatter, a, k, me, peers, incoming=False):
    dev, slot = peers[k]
    if incoming:
        src = dst = lands[a].at[slot]
    else:
        src, dst = (srcs[a].at[slot] if scatter else srcs[a]), lands[a].at[me]
    return pltpu.make_async_remote_copy(
        src_ref=src, dst_ref=dst, send_sem=send_sems.at[a * (N_DEV - 1) + k], recv_sem=recv_sems.at[a * (N_DEV - 1) + k],
        device_id=dev, device_id_type=MESH_ID)


def _exchange_start(arrays, scatter, *, name):
    n = len(arrays)

    def body(*refs):
        srcs, lands = refs[:n], refs[n:2 * n]
        send_sems, recv_sems = refs[2 * n], refs[2 * n + 1]
        token = refs[-1]
        me, peers = _peers()
        for k in range(N_DEV - 1):
            for a in range(n):
                _split_copy(srcs, lands, send_sems, recv_sems, scatter, a, k, me, peers).start()
        token[...] = jnp.zeros_like(token)

    land_shapes = [((N_DEV,) + a.shape[-2:], a.dtype) for a in arrays]
    sems = pltpu.SemaphoreType.DMA((n * (N_DEV - 1),))
    outs = pl.pallas_call(
        body, name=name,
        out_shape=(sems, sems, *[pltpu.HBM(a.shape, a.dtype) for a in arrays],
                   *[pltpu.HBM(s, d) for s, d in land_shapes], _sds((SUBLANE, LANE), F32)),
        in_specs=[HBM] * (2 * n),
        out_specs=(SEM, SEM, *[HBM] * (2 * n), pl.BlockSpec(memory_space=pltpu.VMEM)),
        input_output_aliases={i: 2 + i for i in range(2 * n)},
        compiler_params=pltpu.CompilerParams(has_side_effects=DATAFLOW),
    )(*[pltpu.with_memory_space_constraint(a, pltpu.HBM) for a in arrays],
      *[pltpu.with_memory_space_constraint(lax.empty(s, d), pltpu.HBM) for s, d in land_shapes])
    return (outs[0], outs[1], outs[2:2 + n], outs[2 + n:2 + 2 * n], scatter), outs[-1]


def _exchange_wait(handles, after, *, name):
    send_sems, recv_sems, srcs, lands, scatter = handles
    n = len(srcs)

    def body(*refs):
        src_refs, land_refs = refs[:n], refs[n:2 * n]
        send_ref, recv_ref = refs[2 * n], refs[2 * n + 1]
        me, peers = _peers()
        for k in range(N_DEV - 1):
            for a in range(n):
                _split_copy(src_refs, land_refs, send_ref, recv_ref, scatter, a, k, me, peers).wait_send()
                _split_copy(src_refs, land_refs, send_ref, recv_ref, scatter, a, k, me, peers, True).wait_recv()

    outs = pl.pallas_call(
        body, name=name,
        out_shape=tuple(pltpu.HBM(t.shape, t.dtype) for t in (*srcs, *lands)),
        in_specs=[HBM] * (2 * n) + [SEM, SEM, pl.BlockSpec(memory_space=pl.ANY)],
        out_specs=tuple([HBM] * (2 * n)),
        input_output_aliases={i: i for i in range(2 * n)},
        compiler_params=pltpu.CompilerParams(has_side_effects=DATAFLOW),
    )(*srcs, *lands, send_sems, recv_sems, after)
    return _with_own_slot(outs[n:], outs[:n], scatter)


def _with_own_slot(landed, own, scatter):
    me = 4 * lax.axis_index("x") + 2 * lax.axis_index("y") + lax.axis_index("c")
    out = []
    for buf, src in zip(landed, own):
        mine = lax.dynamic_index_in_dim(src, me, 0, keepdims=False) if scatter else src
        out.append(lax.dynamic_update_index_in_dim(buf, mine, me, 0))
    return out


def _adamw(parts, w, m, v, *, name, tm):
    r, c = w.shape
    assert r % tm == 0

    def body(p_ref, w_ref, m_ref, v_ref, g_ref, d_ref, nm_ref, nv_ref):
        g = p_ref[0].astype(F32)
        for s in range(1, N_DEV):
            g = g + p_ref[s].astype(F32)
        g_ref[...] = g
        m_new = ADAM_B1 * m_ref[...] + (1.0 - ADAM_B1) * g
        v_new = ADAM_B2 * v_ref[...] + (1.0 - ADAM_B2) * (g * g)
        nm_ref[...] = m_new
        nv_ref[...] = v_new
        m_hat = m_new / (1.0 - ADAM_B1 ** ADAM_STEP)
        v_hat = v_new / (1.0 - ADAM_B2 ** ADAM_STEP)
        d_ref[...] = -ADAM_LR * (m_hat / (jnp.sqrt(v_hat) + ADAM_EPS) + ADAM_WD * w_ref[...])

    blk = pl.BlockSpec((tm, c), lambda i: (i, 0))
    return pl.pallas_call(
        body, name=name, grid=(r // tm,),
        in_specs=[pl.BlockSpec((N_DEV, tm, c), lambda i: (0, i, 0)), blk, blk, blk],
        out_specs=[blk] * 4, out_shape=[_sds((r, c), F32)] * 4,
        compiler_params=_params("parallel"),
    )(parts, w, m, v)


SMALL = (("g_pre_mix", D_MODEL), ("b_forget", LANE), ("g_post_mix", D_MODEL), ("g_pre_ffn", D_MODEL),
         ("conv_b", 2 * D_FF), ("g_post_ffn", D_MODEL))
SMALL_ROWS = 80


def _pack_small(vals):
    flat = [jnp.pad(vals[n].reshape(-1), (0, size - vals[n].size)) for n, size in SMALL]
    flat = jnp.concatenate(flat)
    return jnp.pad(flat, (0, SMALL_ROWS * LANE - flat.size)).reshape(SMALL_ROWS, LANE)


def _unpack_small(packed, shapes):
    flat, out, off = packed.reshape(-1), {}, 0
    for n, size in SMALL:
        cnt = math.prod(shapes[n])
        out[n] = flat[off:off + cnt].reshape(shapes[n])
        off += size
    return out


def kernel(x, g_pre_mix, w_in, b_forget, w_o_fox, w_o_dil, w_out, g_post_mix, g_pre_ffn, w_up, conv_w, conv_b, w_down, g_post_ffn, loss_target, m_g_pre_mix, m_w_in, m_b_forget, m_w_o_fox, m_w_o_dil, m_w_out, m_g_post_mix, m_g_pre_ffn, m_w_up, m_conv_w, m_conv_b, m_w_down, m_g_post_ffn, v_g_pre_mix, v_w_in, v_b_forget, v_w_o_fox, v_w_o_dil, v_w_out, v_g_post_mix, v_g_pre_ffn, v_w_up, v_conv_w, v_conv_b, v_w_down, v_g_post_ffn):
    names = ("g_pre_mix", "w_in", "b_forget", "w_o_fox", "w_o_dil", "w_out", "g_post_mix", "g_pre_ffn",
             "w_up", "conv_w", "conv_b", "w_down", "g_post_ffn")
    w = dict(g_pre_mix=g_pre_mix, w_in=w_in, b_forget=b_forget, w_o_fox=w_o_fox, w_o_dil=w_o_dil, w_out=w_out,
             g_post_mix=g_post_mix, g_pre_ffn=g_pre_ffn, w_up=w_up, conv_w=conv_w, conv_b=conv_b, w_down=w_down,
             g_post_ffn=g_post_ffn)
    m = dict(g_pre_mix=m_g_pre_mix, w_in=m_w_in, b_forget=m_b_forget, w_o_fox=m_w_o_fox, w_o_dil=m_w_o_dil,
             w_out=m_w_out, g_post_mix=m_g_post_mix, g_pre_ffn=m_g_pre_ffn, w_up=m_w_up, conv_w=m_conv_w,
             conv_b=m_conv_b, w_down=m_w_down, g_post_ffn=m_g_post_ffn)
    v = dict(g_pre_mix=v_g_pre_mix, w_in=v_w_in, b_forget=v_b_forget, w_o_fox=v_w_o_fox, w_o_dil=v_w_o_dil,
             w_out=v_w_out, g_post_mix=v_g_post_mix, g_pre_ffn=v_g_pre_ffn, w_up=v_w_up, conv_w=v_conv_w,
             conv_b=v_conv_b, w_down=v_w_down, g_post_ffn=v_g_post_ffn)
    sharded = ("w_in", "w_o_fox", "w_o_dil", "w_out", "w_up", "w_down", "conv_w")
    wire = lambda n: F32 if n == "conv_w" else BF16

    by_cols = lambda t: jnp.transpose(t, (1, 0, 2)).reshape(t.shape[1], N_DEV * t.shape[2])
    by_rows = lambda t: t.reshape(N_DEV * t.shape[1], t.shape[2])
    col_slots = lambda t: jnp.transpose(t.reshape(t.shape[0], N_DEV, t.shape[1] // N_DEV), (1, 0, 2))
    row_slots = lambda t: t.reshape(N_DEV, t.shape[0] // N_DEV, t.shape[1])
    to_slots = lambda n, t: (row_slots if n in ("w_out", "w_down") else col_slots)(t).astype(wire(n))
    shard = lambda n: w[n][0].astype(wire(n))
    f_lo, f_hi = 3 * ATT_W, 3 * ATT_W + N_HEADS

    w_in_full = by_cols(_exchange([shard("w_in")], False, name="gather_w_in")[0])
    w_main = jnp.concatenate([w_in_full[:, :f_lo], w_in_full[:, f_hi:]], axis=1)
    w_f = jnp.pad(w_in_full[:, f_lo:f_hi], ((0, 0), (0, F_PAD - N_HEADS)))
    late = ("w_o_fox", "w_o_dil", "w_out", "w_up", "conv_w", "w_down")
    order = jnp.minimum(jnp.abs(w_in_full[0, 0].astype(F32)), 0.0)
    late_handles, late_tok = _exchange_start(
        [shard(n) + order.astype(wire(n)) if n == "conv_w" else shard(n) for n in late], False,
        name="gather_late_start")

    def late_weights(after):
        got = dict(zip(late, _exchange_wait(late_handles, after, name="gather_late_wait")))
        return (by_cols(got["w_o_fox"]), by_cols(got["w_o_dil"]), by_rows(got["w_out"]), by_cols(got["w_up"]),
                by_cols(got["conv_w"]), by_rows(got["w_down"]))

    pending = {}

    def ffn_grads_ready(g):
        pending["ffn"] = _exchange_start([to_slots(n, g[n]) for n in ("w_down", "w_up", "conv_w")], True,
                                         name="scatter_ffn_start")
        return pending["ffn"][1][0, 0]

    def mixer_grads_ready(g):
        pending["ffn_parts"] = _exchange_wait(pending["ffn"][0], g["w_main"], name="scatter_ffn_wait")
        g_w_in = jnp.concatenate([g["w_main"][:, :f_lo], g["w_f"][:, :N_HEADS], g["w_main"][:, f_lo:]], axis=1)
        slots = [to_slots("w_in", g_w_in)] + [to_slots(n, g[n]) for n in ("w_o_fox", "w_o_dil", "w_out")]
        pending["mixer"] = _exchange_start(slots, True, name="scatter_mixer_start")
        return pending["mixer"][1][0, 0]

    sq_err, grad_x, g = _local_step(
        x[0], loss_target[0], w_main, w_f, b_forget, conv_b, g_pre_mix + late_tok[0, 0], g_post_mix, g_pre_ffn,
        g_post_ffn, late_weights, ffn_grads_ready, mixer_grads_ready)
    loss = lax.psum(0.5 * sq_err / D_MODEL, ("x", "y", "c"))
    parts = dict(zip(("w_down", "w_up", "conv_w"), pending["ffn_parts"]))
    parts.update(zip(("w_in", "w_o_fox", "w_o_dil", "w_out"),
                     _exchange_wait(pending["mixer"][0], grad_x, name="scatter_mixer_wait")))
    small_parts = _exchange([_pack_small(g)], False, name="gather_small_grads")[0]

    tiles = dict(w_in=256, w_o_fox=512, w_o_dil=512, w_out=128, w_up=256, w_down=176, conv_w=3)
    res = {n: _adamw(parts[n], w[n][0], m[n][0], v[n][0], name=f"adamw_{n}", tm=tiles[n]) for n in sharded}
    small = _adamw(small_parts, _pack_small(w), _pack_small(m), _pack_small(v), name="adamw_small", tm=SMALL_ROWS)
    shapes = {n: w[n].shape for n, _ in SMALL}
    small = [_unpack_small(t, shapes) for t in small]
    out = [[(res[n][k][None] if n in sharded else small[k][n]) for n in names] for k in range(4)]
    return (loss, grad_x[None], *out[0], *out[1], *out[2], *out[3])
```

```python
import functools
import math

import jax
import jax.numpy as jnp
from jax import lax
from jax.experimental import pallas as pl
from jax.experimental.pallas import tpu as pltpu

F32 = jnp.float32
BF16 = jnp.bfloat16

SEQ = 4096
D_MODEL = 1024
N_HEADS = 8
HEAD_DIM = 64
ATT_W = N_HEADS * HEAD_DIM
D_FF = 2816
Z_MAIN = 5120
F_PAD = 128
ROPE_DIM = 16
ROPE_THETA = 500000.0
RMS_EPS = 1e-6
NEG_INF = -1e30
SCALE = 1.0 / math.sqrt(HEAD_DIM)
DIL_PATTERNS = ((128, 1), (512, 4), (2048, 16))
DIL_BLK = 128
N_DEV = 8

ADAM_LR = 0.001
ADAM_B1 = 0.9
ADAM_B2 = 0.999
ADAM_EPS = 1e-08
ADAM_WD = 0.01
ADAM_STEP = 10

LANE = 128
SUBLANE = 8
VMEM_LIMIT = 56 * 1024 * 1024
MESH_ID = pl.DeviceIdType.MESH
ANY = pl.BlockSpec(memory_space=pl.ANY)


def _params(*sem):
    return pltpu.CompilerParams(dimension_semantics=sem, vmem_limit_bytes=VMEM_LIMIT)


def _sds(shape, dtype):
    return jax.ShapeDtypeStruct(shape, dtype)


def _matmul(a, b, *, ta=False, tb=False, out_dtype, tm, tn, tk, name):
    if ta:
        kk, m = a.shape
    else:
        m, kk = a.shape
    n = b.shape[0] if tb else b.shape[1]
    assert (b.shape[1] if tb else b.shape[0]) == kk
    tm, tn, tk = min(tm, m), min(tn, n), min(tk, kk)
    assert m % tm == 0 and n % tn == 0 and kk % tk == 0, (name, m, n, kk, tm, tn, tk)
    nk = kk // tk
    dims = (((0 if ta else 1,), (1 if tb else 0,)), ((), ()))

    def body(a_ref, b_ref, o_ref, *scratch):
        p = lax.dot_general(a_ref[...].astype(BF16), b_ref[...].astype(BF16), dims,
                            preferred_element_type=F32)
        if nk == 1:
            o_ref[...] = p.astype(o_ref.dtype)
        else:
            acc = scratch[0]
            k = pl.program_id(2)

            @pl.when(k == 0)
            def _():
                acc[...] = p

            @pl.when(k > 0)
            def _():
                acc[...] += p

            @pl.when(k == nk - 1)
            def _():
                o_ref[...] = acc[...].astype(o_ref.dtype)

    a_spec = (pl.BlockSpec((tk, tm), lambda i, j, k: (k, i)) if ta
              else pl.BlockSpec((tm, tk), lambda i, j, k: (i, k)))
    b_spec = (pl.BlockSpec((tn, tk), lambda i, j, k: (j, k)) if tb
              else pl.BlockSpec((tk, tn), lambda i, j, k: (k, j)))
    return pl.pallas_call(
        body, name=name, grid=(m // tm, n // tn, nk),
        in_specs=[a_spec, b_spec],
        out_specs=pl.BlockSpec((tm, tn), lambda i, j, k: (i, j)),
        out_shape=_sds((m, n), out_dtype),
        scratch_shapes=[pltpu.VMEM((tm, tn), F32)] if nk > 1 else [],
        compiler_params=_params("parallel", "parallel", "arbitrary"),
    )(a, b)


def _rms_fwd(x, g, *, name, tm=512):
    def body(x_ref, g_ref, h_ref):
        xv = x_ref[...]
        r = lax.rsqrt(jnp.mean(xv * xv, axis=-1, keepdims=True) + RMS_EPS)
        h_ref[...] = (xv * r * g_ref[...]).astype(h_ref.dtype)

    return pl.pallas_call(
        body, name=name, grid=(SEQ // tm,),
        in_specs=[pl.BlockSpec((tm, D_MODEL), lambda i: (i, 0)), pl.BlockSpec((1, D_MODEL), lambda i: (0, 0))],
        out_specs=pl.BlockSpec((tm, D_MODEL), lambda i: (i, 0)),
        out_shape=_sds((SEQ, D_MODEL), BF16),
        compiler_params=_params("parallel"),
    )(x, g)


def _rms_bwd(dh_parts, xin, g, dres, *, out_dtype, name, tm=512):
    n_parts = len(dh_parts)
    has_res = dres is not None

    def body(*refs):
        parts = refs[:n_parts]
        x_ref, g_ref = refs[n_parts], refs[n_parts + 1]
        res_ref = refs[n_parts + 2] if has_res else None
        o_ref, gg_ref = refs[-2], refs[-1]
        dh = parts[0][...].astype(F32)
        for p in parts[1:]:
            dh = dh + p[...].astype(F32)
        xv = x_ref[...]
        r = lax.rsqrt(jnp.mean(xv * xv, axis=-1, keepdims=True) + RMS_EPS)
        xn = xv * r

        @pl.when(pl.program_id(0) == 0)
        def _():
            gg_ref[...] = jnp.zeros_like(gg_ref)

        gg_ref[...] += jnp.sum(dh * xn, axis=0, keepdims=True)
        dxn = dh * g_ref[...]
        dx = r * (dxn - xn * jnp.mean(dxn * xn, axis=-1, keepdims=True))
        if has_res:
            dx = dx + res_ref[...]
        o_ref[...] = dx.astype(o_ref.dtype)

    row = pl.BlockSpec((tm, D_MODEL), lambda i: (i, 0))
    vec = pl.BlockSpec((1, D_MODEL), lambda i: (0, 0))
    args = list(dh_parts) + [xin, g] + ([dres] if has_res else [])
    return pl.pallas_call(
        body, name=name, grid=(SEQ // tm,),
        in_specs=[row] * n_parts + [row, vec] + ([row] if has_res else []),
        out_specs=[row, vec],
        out_shape=[_sds((SEQ, D_MODEL), out_dtype), _sds((1, D_MODEL), F32)],
        compiler_params=_params("arbitrary"),
    )(*args)


SCAN_BLK = 512


def _split_dot(v, tri):
    hi = v.astype(BF16)
    r1 = v - hi.astype(F32)
    mid = r1.astype(BF16)
    lo = (r1 - mid.astype(F32)).astype(BF16)
    dot = functools.partial(jnp.dot, preferred_element_type=F32)
    return dot(hi, tri) + dot(mid, tri) + dot(lo, tri)


def _fox_prep(fa_t, b_col):
    nblk = SEQ // SCAN_BLK

    def body(fa_ref, b_ref, f_ref, sg_ref):
        row = lax.broadcasted_iota(jnp.int32, (SCAN_BLK, SCAN_BLK), 0)
        col = lax.broadcasted_iota(jnp.int32, (SCAN_BLK, SCAN_BLK), 1)
        upper = (row <= col).astype(BF16)
        carry = jnp.zeros((N_HEADS, 1), F32)
        for blk in range(nblk):
            sl = pl.ds(blk * SCAN_BLK, SCAN_BLK)
            xx = fa_ref[:, sl] + b_ref[...]
            e = jnp.exp(-jnp.abs(xx))
            logf = jnp.minimum(xx, 0.0) - jnp.log(1.0 + e)
            sg_ref[:, sl] = jnp.where(xx >= 0.0, e, 1.0) / (1.0 + e)
            c = _split_dot(logf, upper) + carry
            f_ref[:, sl] = c
            carry = c[:, SCAN_BLK - 1:SCAN_BLK]

    return pl.pallas_call(
        body, name="fox_prep",
        out_shape=[_sds((N_HEADS, SEQ), F32), _sds((N_HEADS, SEQ), F32)],
        compiler_params=pltpu.CompilerParams(vmem_limit_bytes=VMEM_LIMIT),
    )(fa_t, b_col)


def _fox_post_bwd(df_rows_t, df_cols_t, sg_t):
    nblk = SEQ // SCAN_BLK

    def body(dfr_ref, dfc_ref, sg_ref, dfa_ref, gb_ref):
        row = lax.broadcasted_iota(jnp.int32, (SCAN_BLK, SCAN_BLK), 0)
        col = lax.broadcasted_iota(jnp.int32, (SCAN_BLK, SCAN_BLK), 1)
        lower = (row >= col).astype(BF16)
        carry = jnp.zeros((N_HEADS, 1), F32)
        gb = jnp.zeros((N_HEADS, 1), F32)
        for blk in reversed(range(nblk)):
            sl = pl.ds(blk * SCAN_BLK, SCAN_BLK)
            c = _split_dot(dfr_ref[:, sl] + dfc_ref[:, sl], lower) + carry
            carry = c[:, 0:1]
            dfa = c * sg_ref[:, sl]
            dfa_ref[:, sl] = dfa
            gb = gb + jnp.sum(dfa, axis=1, keepdims=True)
        gb_ref[...] = gb

    return pl.pallas_call(
        body, name="fox_post_bwd",
        out_shape=[_sds((N_HEADS, SEQ), F32), _sds((N_HEADS, 1), F32)],
        compiler_params=pltpu.CompilerParams(vmem_limit_bytes=VMEM_LIMIT),
    )(df_rows_t, df_cols_t, sg_t)


FOX_T = 512
NT_DIMS = (((1,), (1,)), ((), ()))
TN_DIMS = (((0,), (0,)), ((), ()))


def _head(ref_or_val, h):
    return ref_or_val[:, h * HEAD_DIM:(h + 1) * HEAD_DIM]


def _split3(v):
    hi = v.astype(BF16).astype(F32)
    r1 = v - hi
    mid = r1.astype(BF16).astype(F32)
    return hi, mid, (r1 - mid).astype(BF16).astype(F32)


def _aux_lanes(rows, terms):
    lane = lax.broadcasted_iota(jnp.int32, (rows, HEAD_DIM), 1)
    out = jnp.zeros((rows, HEAD_DIM), F32)
    for i, t in enumerate(terms):
        out = jnp.where(lane == i, t, out)
    return out


SLOT = 2 * HEAD_DIM
N_SPLIT = 3


def _slot(ref, h):
    return ref[:, h * SLOT:(h + 1) * SLOT]


def _fox_pack_fwd(zm, f_cols, *, tm=512):
    def body(q_ref, k_ref, v_ref, f_ref, qs_ref, ks_ref, vs_ref):
        ones = jnp.ones((tm, HEAD_DIM), BF16)
        for h in range(N_HEADS):
            fh = _split3(f_ref[:, h:h + 1])
            q_aux = _aux_lanes(tm, list(fh) + [1.0] * N_SPLIT)
            k_aux = _aux_lanes(tm, [1.0] * N_SPLIT + [-t for t in fh])
            qs_ref[:, h * SLOT:(h + 1) * SLOT] = jnp.concatenate(
                [(_head(q_ref, h).astype(F32) * SCALE).astype(BF16), q_aux.astype(BF16)], axis=1)
            ks_ref[:, h * SLOT:(h + 1) * SLOT] = jnp.concatenate([_head(k_ref, h), k_aux.astype(BF16)], axis=1)
            vs_ref[:, h * SLOT:(h + 1) * SLOT] = jnp.concatenate([_head(v_ref, h), ones], axis=1)

    col = lambda b: pl.BlockSpec((tm, ATT_W), lambda i: (i, b))
    wide = pl.BlockSpec((tm, N_HEADS * SLOT), lambda i: (i, 0))
    return pl.pallas_call(
        body, name="fox_pack_fwd", grid=(SEQ // tm,),
        in_specs=[col(0), col(1), col(2), pl.BlockSpec((tm, LANE), lambda i: (i, 0))],
        out_specs=[wide] * 3, out_shape=[_sds((SEQ, N_HEADS * SLOT), BF16)] * 3,
        compiler_params=_params("parallel"),
    )(zm, zm, zm, f_cols)


def _fox_pack_bwd(zm, f_cols, lse, o, do, *, tm=512):
    def body(q_ref, f_ref, lse_ref, o_ref, do_ref, qs_ref, ds_ref):
        for h in range(N_HEADS):
            gh = _split3(f_ref[:, h:h + 1] - lse_ref[:, h * HEAD_DIM:h * HEAD_DIM + 1])
            dout = _head(do_ref, h)
            delta = jnp.sum(_head(o_ref, h).astype(F32) * dout.astype(F32), axis=1, keepdims=True)
            q_aux = _aux_lanes(tm, list(gh) + [1.0] * N_SPLIT)
            d_aux = _aux_lanes(tm, [-t for t in _split3(delta)])
            qs_ref[:, h * SLOT:(h + 1) * SLOT] = jnp.concatenate(
                [(_head(q_ref, h).astype(F32) * SCALE).astype(BF16), q_aux.astype(BF16)], axis=1)
            ds_ref[:, h * SLOT:(h + 1) * SLOT] = jnp.concatenate([dout, d_aux.astype(BF16)], axis=1)

    row = pl.BlockSpec((tm, ATT_W), lambda i: (i, 0))
    wide = pl.BlockSpec((tm, N_HEADS * SLOT), lambda i: (i, 0))
    return pl.pallas_call(
        body, name="fox_pack_bwd", grid=(SEQ // tm,),
        in_specs=[row, pl.BlockSpec((tm, LANE), lambda i: (i, 0)), row, row, row],
        out_specs=[wide] * 2, out_shape=[_sds((SEQ, N_HEADS * SLOT), BF16)] * 2,
        compiler_params=_params("parallel"),
    )(zm, f_cols, lse, o, do)


def _causal_pairs(key_major):
    nb = SEQ // FOX_T
    if key_major:
        pairs = [(i, j) for j in range(nb) for i in range(j, nb)]
    else:
        pairs = [(i, j) for i in range(nb) for j in range(i + 1)]
    return (jnp.array([p[0] for p in pairs], jnp.int32), jnp.array([p[1] for p in pairs], jnp.int32), len(pairs))


def _diag_mask():
    row = lax.broadcasted_iota(jnp.int32, (FOX_T, FOX_T), 0)
    col = lax.broadcasted_iota(jnp.int32, (FOX_T, FOX_T), 1)
    return col <= row


def _fox_fwd(q_slots, k_slots, v_slots):
    i_tab, j_tab, n_pairs = _causal_pairs(False)

    def body(i_tab, j_tab, q_ref, k_ref, v_ref, o_ref, lse_ref, m_s, acc_s):
        t = pl.program_id(1)
        i, j = i_tab[t], j_tab[t]

        @pl.when(j == 0)
        def _():
            m_s[...] = jnp.full_like(m_s, NEG_INF)
            acc_s[...] = jnp.zeros_like(acc_s)

        def step(masked):
            for h in range(2):
                s = lax.dot_general(_slot(q_ref, h), _slot(k_ref, h), NT_DIMS, preferred_element_type=F32)
                if masked:
                    s = jnp.where(_diag_mask(), s, NEG_INF)
                m_prev = m_s[h]
                m_new = jnp.maximum(m_prev, jnp.max(s, axis=-1, keepdims=True))
                p = jnp.exp(s - jnp.tile(m_new, (1, FOX_T // LANE)))
                acc_s[h] = jnp.exp(m_prev - m_new) * acc_s[h] + jnp.dot(
                    p.astype(BF16), _slot(v_ref, h), preferred_element_type=F32)
                m_s[h] = m_new

        @pl.when(j < i)
        def _():
            step(False)

        @pl.when(j == i)
        def _():
            step(True)
            outs, lses = [], []
            for h in range(2):
                acc = acc_s[h]
                l = acc[:, HEAD_DIM:]
                outs.append(acc[:, :HEAD_DIM] / l)
                lses.append(m_s[h][:, :HEAD_DIM] + jnp.log(l))
            o_ref[...] = jnp.concatenate(outs, axis=1).astype(o_ref.dtype)
            lse_ref[...] = jnp.concatenate(lses, axis=1)

    qspec = pl.BlockSpec((FOX_T, 2 * SLOT), lambda p, t, it, jt: (it[t], p))
    kspec = pl.BlockSpec((FOX_T, 2 * SLOT), lambda p, t, it, jt: (jt[t], p))
    ospec = pl.BlockSpec((FOX_T, LANE), lambda p, t, it, jt: (it[t], p))
    return pl.pallas_call(
        body, name="fox_fwd",
        grid_spec=pltpu.PrefetchScalarGridSpec(
            num_scalar_prefetch=2, grid=(N_HEADS // 2, n_pairs),
            in_specs=[qspec, kspec, kspec], out_specs=[ospec, ospec],
            scratch_shapes=[pltpu.VMEM((2, FOX_T, LANE), F32), pltpu.VMEM((2, FOX_T, SLOT), F32)]),
        out_shape=[_sds((SEQ, ATT_W), BF16), _sds((SEQ, ATT_W), F32)],
        compiler_params=_params("parallel", "arbitrary"),
    )(i_tab, j_tab, q_slots, k_slots, v_slots)


def _fox_bwd(q_slots, k_slots, v_slots, do_slots):
    i_tab, j_tab, n_pairs = _causal_pairs(True)

    def body(i_tab, j_tab, q_ref, k_ref, v_ref, do_ref, dq_ref, dk_ref, dv_ref):
        t = pl.program_id(1)
        i, j = i_tab[t], j_tab[t]

        @pl.when(t == 0)
        def _():
            dq_ref[...] = jnp.zeros_like(dq_ref)

        @pl.when(i == j)
        def _():
            dk_ref[...] = jnp.zeros_like(dk_ref)
            dv_ref[...] = jnp.zeros_like(dv_ref)

        def step(masked):
            rows = pl.ds(pl.multiple_of(i * FOX_T, FOX_T), FOX_T)
            for h in range(2):
                cols = slice(h * SLOT, (h + 1) * SLOT)
                q, k, v, dout = _slot(q_ref, h), _slot(k_ref, h), _slot(v_ref, h), _slot(do_ref, h)
                p = jnp.exp(lax.dot_general(q, k, NT_DIMS, preferred_element_type=F32))
                if masked:
                    p = jnp.where(_diag_mask(), p, 0.0)
                dp = lax.dot_general(dout, v, NT_DIMS, preferred_element_type=F32)
                ds = (p * dp).astype(BF16)
                dv_ref[:, cols] += lax.dot_general(p.astype(BF16), dout, TN_DIMS, preferred_element_type=F32)
                dk_ref[:, cols] += lax.dot_general(ds, q, TN_DIMS, preferred_element_type=F32)
                dq_ref[rows, cols] += jnp.dot(ds, k, preferred_element_type=F32)

        @pl.when(i > j)
        def _():
            step(False)

        @pl.when(i == j)
        def _():
            step(True)

    qspec = pl.BlockSpec((FOX_T, 2 * SLOT), lambda p, t, it, jt: (it[t], p))
    kspec = pl.BlockSpec((FOX_T, 2 * SLOT), lambda p, t, it, jt: (jt[t], p))
    return pl.pallas_call(
        body, name="fox_bwd",
        grid_spec=pltpu.PrefetchScalarGridSpec(
            num_scalar_prefetch=2, grid=(N_HEADS // 2, n_pairs),
            in_specs=[qspec, kspec, kspec, qspec],
            out_specs=[pl.BlockSpec((SEQ, 2 * SLOT), lambda p, t, it, jt: (0, p)), kspec, kspec]),
        out_shape=[_sds((SEQ, N_HEADS * SLOT), F32)] * 3,
        compiler_params=_params("arbitrary", "arbitrary"),
    )(i_tab, j_tab, q_slots, k_slots, v_slots, do_slots)


def _attn_delta(o, do, *, name, tm=512):
    def body(o_ref, do_ref, d_ref):
        prod = o_ref[...].astype(F32) * do_ref[...].astype(F32)
        lane = lax.broadcasted_iota(jnp.int32, (tm, LANE), 1)
        out = jnp.zeros((tm, LANE), F32)
        for h in range(N_HEADS):
            out = jnp.where(lane == h, jnp.sum(_head(prod, h), axis=1, keepdims=True), out)
        d_ref[...] = out

    row = pl.BlockSpec((tm, ATT_W), lambda i: (i, 0))
    return pl.pallas_call(
        body, name=name, grid=(SEQ // tm,), in_specs=[row, row],
        out_specs=pl.BlockSpec((tm, LANE), lambda i: (i, 0)), out_shape=_sds((SEQ, LANE), F32),
        compiler_params=_params("parallel"),
    )(o, do)


def _rope_tables():
    half = ROPE_DIM // 2
    inv_freq = ROPE_THETA ** (-jnp.arange(half, dtype=F32) * 2.0 / ROPE_DIM)
    ang = jnp.arange(SEQ, dtype=F32)[:, None] * inv_freq[None, :]
    cos, sin = jnp.cos(ang), jnp.sin(ang)
    ones = jnp.ones((SEQ, HEAD_DIM - ROPE_DIM), F32)
    zeros = jnp.zeros((SEQ, HEAD_DIM - ROPE_DIM), F32)
    zh = jnp.zeros((SEQ, half), F32)
    c_tab = jnp.concatenate([cos, cos, ones], axis=1)
    a_tab = jnp.concatenate([-sin, zh, zeros], axis=1)
    b_tab = jnp.concatenate([zh, sin, zeros], axis=1)
    two = lambda t: jnp.concatenate([t, t], axis=1)
    return two(c_tab), two(a_tab), two(b_tab)


def _rotate(x, c_tab, a_tab, b_tab):
    return x * c_tab + pltpu.roll(x, LANE - ROPE_DIM // 2, 1) * a_tab + pltpu.roll(x, ROPE_DIM // 2, 1) * b_tab


def _rope_fwd(zm, tabs, *, tm=512):
    def body(q_ref, k_ref, c_ref, a_ref, b_ref, o_ref):
        for part, x_ref in enumerate((q_ref, k_ref)):
            for cc in range(ATT_W // LANE):
                sl = slice(cc * LANE, (cc + 1) * LANE)
                o_ref[:, part * ATT_W + cc * LANE:part * ATT_W + (cc + 1) * LANE] = _rotate(
                    x_ref[:, sl].astype(F32), c_ref[...], a_ref[...], b_ref[...]).astype(o_ref.dtype)

    tab = pl.BlockSpec((tm, LANE), lambda i: (i, 0))
    return pl.pallas_call(
        body, name="rope_fwd", grid=(SEQ // tm,),
        in_specs=[pl.BlockSpec((tm, ATT_W), lambda i: (i, 3)), pl.BlockSpec((tm, ATT_W), lambda i: (i, 4)),
                  tab, tab, tab],
        out_specs=pl.BlockSpec((tm, 2 * ATT_W), lambda i: (i, 0)),
        out_shape=_sds((SEQ, 2 * ATT_W), BF16),
        compiler_params=_params("parallel"),
    )(zm, zm, *tabs)


def _dil_grad_combine(dqs, dks, dvs, tabs, *, tm=256):
    def body(*refs):
        q_refs, k_refs, v_refs = refs[0:3], refs[3:6], refs[6:9]
        c_ref, a_ref, b_ref, o_ref = refs[9:]
        total = lambda rs, sl: rs[0][:, sl] + rs[1][:, sl] + rs[2][:, sl]
        for cc in range(ATT_W // LANE):
            sl = slice(cc * LANE, (cc + 1) * LANE)
            for part, rs in enumerate((q_refs, k_refs)):
                o_ref[:, part * ATT_W + cc * LANE:part * ATT_W + (cc + 1) * LANE] = _rotate(
                    total(rs, sl), c_ref[...], -a_ref[...], -b_ref[...]).astype(o_ref.dtype)
            o_ref[:, 2 * ATT_W + cc * LANE:2 * ATT_W + (cc + 1) * LANE] = total(v_refs, sl).astype(o_ref.dtype)

    row = pl.BlockSpec((tm, ATT_W), lambda i: (i, 0))
    tab = pl.BlockSpec((tm, LANE), lambda i: (i, 0))
    return pl.pallas_call(
        body, name="dil_grad_combine", grid=(SEQ // tm,),
        in_specs=[row] * 9 + [tab] * 3,
        out_specs=pl.BlockSpec((tm, 3 * ATT_W), lambda i: (i, 0)),
        out_shape=_sds((SEQ, 3 * ATT_W), BF16),
        compiler_params=_params("parallel"),
    )(*dqs, *dks, *dvs, *tabs)


def _dil_valid(n):
    qi = lax.broadcasted_iota(jnp.int32, (DIL_BLK, 2 * DIL_BLK), 0)
    ki = lax.broadcasted_iota(jnp.int32, (DIL_BLK, 2 * DIL_BLK), 1)
    dist = qi + DIL_BLK - ki
    return (dist >= 0) & (dist <= DIL_BLK) & ((n > 0) | (ki >= DIL_BLK))


def _dil_views(qk, zm, d):
    length = SEQ // d
    return qk.reshape(length, d * 2 * ATT_W), zm.reshape(length, d * Z_MAIN), length // DIL_BLK


def _dil_fwd(qk, zm, d):
    qk_v, zm_v, nb = _dil_views(qk, zm, d)
    length = SEQ // d
    v_blk = Z_MAIN // ATT_W

    def body(q_ref, kp_ref, kc_ref, vp_ref, vc_ref, o_ref, lse_ref):
        n = pl.program_id(1)
        ok = _dil_valid(n)
        lane = lax.broadcasted_iota(jnp.int32, (DIL_BLK, LANE), 1)
        lse_all = jnp.zeros((DIL_BLK, LANE), F32)
        outs = []
        for h in range(N_HEADS):
            kk = jnp.concatenate([_head(kp_ref, h), _head(kc_ref, h)], axis=0)
            vv = jnp.concatenate([_head(vp_ref, h), _head(vc_ref, h)], axis=0)
            s = lax.dot_general(_head(q_ref, h), kk, NT_DIMS, preferred_element_type=F32) * SCALE
            s = jnp.where(ok, s, NEG_INF)
            m = jnp.max(s, axis=-1, keepdims=True)
            p = jnp.exp(s - m)
            l = jnp.sum(p, axis=-1, keepdims=True)
            outs.append(jnp.dot(p.astype(BF16), vv, preferred_element_type=F32) / l)
            lse_all = jnp.where(lane == h, m + jnp.log(l), lse_all)
        o_ref[...] = jnp.concatenate(outs, axis=1)
        lse_ref[...] = lse_all

    blk = lambda f: pl.BlockSpec((DIL_BLK, ATT_W), f)
    prev = lambda n: jnp.maximum(n - 1, 0)
    o, lse = pl.pallas_call(
        body, name=f"dil_fwd_d{d}", grid=(d, nb),
        in_specs=[blk(lambda r, n: (n, 2 * r)),
                  blk(lambda r, n: (prev(n), 2 * r + 1)), blk(lambda r, n: (n, 2 * r + 1)),
                  blk(lambda r, n: (prev(n), v_blk * r + 5)), blk(lambda r, n: (n, v_blk * r + 5))],
        out_specs=[blk(lambda r, n: (n, r)), pl.BlockSpec((DIL_BLK, LANE), lambda r, n: (n, r))],
        out_shape=[_sds((length, d * ATT_W), F32), _sds((length, d * LANE), F32)],
        compiler_params=_params("parallel", "arbitrary"),
    )(qk_v, qk_v, qk_v, zm_v, zm_v)
    return o.reshape(SEQ, ATT_W), lse.reshape(SEQ, LANE)


def _dil_merge(os_, lses, *, tm=512):
    def body(o0, o1, o2, l0, l1, l2, y_ref, lse_ref):
        ls = [l0[...], l1[...], l2[...]]
        m = jnp.maximum(jnp.maximum(ls[0], ls[1]), ls[2])
        es = [jnp.exp(l - m) for l in ls]
        tot = es[0] + es[1] + es[2]
        lse_ref[...] = m + jnp.log(tot)
        alphas = [e / tot for e in es]
        outs = []
        for h in range(N_HEADS):
            acc = None
            for g, o_ref in enumerate((o0, o1, o2)):
                term = alphas[g][:, h:h + 1] * _head(o_ref, h)
                acc = term if acc is None else acc + term
            outs.append(acc)
        y_ref[...] = jnp.concatenate(outs, axis=1).astype(y_ref.dtype)

    row = pl.BlockSpec((tm, ATT_W), lambda i: (i, 0))
    vec = pl.BlockSpec((tm, LANE), lambda i: (i, 0))
    return pl.pallas_call(
        body, name="dil_merge", grid=(SEQ // tm,),
        in_specs=[row] * 3 + [vec] * 3, out_specs=[row, vec],
        out_shape=[_sds((SEQ, ATT_W), BF16), _sds((SEQ, LANE), F32)],
        compiler_params=_params("parallel"),
    )(*os_, *lses)


def _dil_bwd(qk, zm, lse, delta, do, d):
    qk_v, zm_v, nb = _dil_views(qk, zm, d)
    length = SEQ // d
    v_blk = Z_MAIN // ATT_W
    lse_v, dl_v, do_v = lse.reshape(length, d * LANE), delta.reshape(length, d * LANE), do.reshape(length, d * ATT_W)

    def body(q_ref, kp_ref, kc_ref, vp_ref, vc_ref, lse_ref, dl_ref, do_ref,
             dq_ref, dk_ref, dv_ref, ck_s, cv_s):
        n = pl.program_id(1)

        @pl.when(n == 0)
        def _():
            ck_s[...] = jnp.zeros_like(ck_s)
            cv_s[...] = jnp.zeros_like(cv_s)

        @pl.when(n < nb)
        def _():
            ok = _dil_valid(n)
            for h in range(N_HEADS):
                cols = slice(h * HEAD_DIM, (h + 1) * HEAD_DIM)
                q, dout = _head(q_ref, h), _head(do_ref, h)
                kk = jnp.concatenate([_head(kp_ref, h), _head(kc_ref, h)], axis=0)
                vv = jnp.concatenate([_head(vp_ref, h), _head(vc_ref, h)], axis=0)
                s = lax.dot_general(q, kk, NT_DIMS, preferred_element_type=F32) * SCALE
                p = jnp.where(ok, jnp.exp(s - lse_ref[:, h:h + 1]), 0.0)
                dp = lax.dot_general(dout, vv, NT_DIMS, preferred_element_type=F32)
                ds = (p * (dp - dl_ref[:, h:h + 1])).astype(BF16)
                dq_ref[:, cols] = jnp.dot(ds, kk, preferred_element_type=F32) * SCALE
                dkk = lax.dot_general(ds, q, TN_DIMS, preferred_element_type=F32) * SCALE
                dvv = lax.dot_general(p.astype(BF16), dout, TN_DIMS, preferred_element_type=F32)
                dk_ref[:, cols] = ck_s[:, cols] + dkk[:DIL_BLK]
                dv_ref[:, cols] = cv_s[:, cols] + dvv[:DIL_BLK]
                ck_s[:, cols] = dkk[DIL_BLK:]
                cv_s[:, cols] = dvv[DIL_BLK:]

        @pl.when(n == nb)
        def _():
            dk_ref[...] = ck_s[...]
            dv_ref[...] = cv_s[...]

    blk = lambda f: pl.BlockSpec((DIL_BLK, ATT_W), f)
    vec = lambda f: pl.BlockSpec((DIL_BLK, LANE), f)
    cur = lambda n: jnp.minimum(n, nb - 1)
    prev = lambda n: jnp.maximum(cur(n) - 1, 0)
    back = lambda n: jnp.maximum(n - 1, 0)
    outs = pl.pallas_call(
        body, name=f"dil_bwd_d{d}", grid=(d, nb + 1),
        in_specs=[blk(lambda r, n: (cur(n), 2 * r)),
                  blk(lambda r, n: (prev(n), 2 * r + 1)), blk(lambda r, n: (cur(n), 2 * r + 1)),
                  blk(lambda r, n: (prev(n), v_blk * r + 5)), blk(lambda r, n: (cur(n), v_blk * r + 5)),
                  vec(lambda r, n: (cur(n), r)), vec(lambda r, n: (cur(n), r)),
                  blk(lambda r, n: (cur(n), r))],
        out_specs=[blk(lambda r, n: (cur(n), r)), blk(lambda r, n: (back(n), r)), blk(lambda r, n: (back(n), r))],
        out_shape=[_sds((length, d * ATT_W), F32)] * 3,
        scratch_shapes=[pltpu.VMEM((DIL_BLK, ATT_W), F32), pltpu.VMEM((DIL_BLK, ATT_W), F32)],
        compiler_params=_params("arbitrary", "arbitrary"),
    )(qk_v, qk_v, qk_v, zm_v, zm_v, lse_v, dl_v, do_v)
    return [t.reshape(SEQ, ATT_W) for t in outs]


def _sigmoid(x):
    return 1.0 / (1.0 + jnp.exp(-x))


def _mix_fwd(ya, yb, w_oa, w_ob, zm, *, tm=512):
    def body(ya_ref, yb_ref, wa_ref, wb_ref, ga_ref, gb_ref, pa_ref, pb_ref, mix_ref):
        pa = jnp.dot(ya_ref[...], wa_ref[...], preferred_element_type=F32)
        pb = jnp.dot(yb_ref[...], wb_ref[...], preferred_element_type=F32)
        pa_ref[...] = pa.astype(pa_ref.dtype)
        pb_ref[...] = pb.astype(pb_ref.dtype)
        mix_ref[...] = (_sigmoid(ga_ref[...].astype(F32)) * pa + _sigmoid(gb_ref[...].astype(F32)) * pb
                        ).astype(mix_ref.dtype)

    row = pl.BlockSpec((tm, ATT_W), lambda i: (i, 0))
    wsp = pl.BlockSpec((ATT_W, D_MODEL), lambda i: (0, 0))
    wide = pl.BlockSpec((tm, D_MODEL), lambda i: (i, 0))
    return pl.pallas_call(
        body, name="mix_fwd", grid=(SEQ // tm,),
        in_specs=[row, row, wsp, wsp, pl.BlockSpec((tm, D_MODEL), lambda i: (i, 3)),
                  pl.BlockSpec((tm, D_MODEL), lambda i: (i, 4))],
        out_specs=[wide] * 3, out_shape=[_sds((SEQ, D_MODEL), BF16)] * 3,
        compiler_params=_params("parallel"),
    )(ya, yb, w_oa, w_ob, zm, zm)


def _gate_bwd(dmix, zm, pa, pb, *, tm=512):
    def body(dm_ref, ga_ref, gb_ref, pa_ref, pb_ref, dpa_ref, dpb_ref, dg_ref):
        dm = dm_ref[...].astype(F32)
        sa, sb = _sigmoid(ga_ref[...].astype(F32)), _sigmoid(gb_ref[...].astype(F32))
        dpa_ref[...] = (dm * sa).astype(dpa_ref.dtype)
        dpb_ref[...] = (dm * sb).astype(dpb_ref.dtype)
        dg_ref[:, :D_MODEL] = (dm * pa_ref[...].astype(F32) * sa * (1.0 - sa)).astype(dg_ref.dtype)
        dg_ref[:, D_MODEL:] = (dm * pb_ref[...].astype(F32) * sb * (1.0 - sb)).astype(dg_ref.dtype)

    wide = pl.BlockSpec((tm, D_MODEL), lambda i: (i, 0))
    return pl.pallas_call(
        body, name="gate_bwd", grid=(SEQ // tm,),
        in_specs=[wide, pl.BlockSpec((tm, D_MODEL), lambda i: (i, 3)), pl.BlockSpec((tm, D_MODEL), lambda i: (i, 4)),
                  wide, wide],
        out_specs=[wide, wide, pl.BlockSpec((tm, 2 * D_MODEL), lambda i: (i, 0))],
        out_shape=[_sds((SEQ, D_MODEL), BF16), _sds((SEQ, D_MODEL), BF16), _sds((SEQ, 2 * D_MODEL), BF16)],
        compiler_params=_params("parallel"),
    )(dmix, zm, zm, pa, pb)


def _out_fwd(mixed, w_out, x, g_post, g_pre, *, tm=512):
    def body(m_ref, w_ref, x_ref, gp_ref, gn_ref, y_ref, x2_ref, h_ref):
        y = jnp.dot(m_ref[...], w_ref[...], preferred_element_type=F32)
        y_ref[...] = y
        r = lax.rsqrt(jnp.mean(y * y, axis=-1, keepdims=True) + RMS_EPS)
        x2 = x_ref[...] + y * r * gp_ref[...]
        x2_ref[...] = x2
        r2 = lax.rsqrt(jnp.mean(x2 * x2, axis=-1, keepdims=True) + RMS_EPS)
        h_ref[...] = (x2 * r2 * gn_ref[...]).astype(h_ref.dtype)

    row = pl.BlockSpec((tm, D_MODEL), lambda i: (i, 0))
    vec = pl.BlockSpec((1, D_MODEL), lambda i: (0, 0))
    return pl.pallas_call(
        body, name="out_fwd", grid=(SEQ // tm,),
        in_specs=[row, pl.BlockSpec((D_MODEL, D_MODEL), lambda i: (0, 0)), row, vec, vec],
        out_specs=[row] * 3,
        out_shape=[_sds((SEQ, D_MODEL), F32), _sds((SEQ, D_MODEL), F32), _sds((SEQ, D_MODEL), BF16)],
        compiler_params=_params("parallel"),
    )(mixed, w_out, x, g_post, g_pre)


FFN_TM = 256
FFN_TN = 256
FFN_NJ = D_FF // FFN_TN


def _gelu_parts(a):
    c = math.sqrt(2.0 / math.pi)
    t = jnp.tanh(c * (a + 0.044715 * a * a * a))
    gelu = 0.5 * a * (1.0 + t)
    dgelu = 0.5 * (1.0 + t) + 0.5 * a * (1.0 - t * t) * c * (1.0 + 3.0 * 0.044715 * a * a)
    return gelu, dgelu


def _shift_down(u, halo, k):
    s = pltpu.roll(u, k, 0)
    hs = pltpu.roll(halo, k, 0)
    row = lax.broadcasted_iota(jnp.int32, hs.shape, 0)
    top = jnp.where(row < k, hs, s[:SUBLANE])
    return jnp.concatenate([top, s[SUBLANE:]], axis=0)


def _shift_up(u, halo, k):
    rows = u.shape[0]
    s = pltpu.roll(u, rows - k, 0)
    hs = pltpu.roll(halo, SUBLANE - k, 0)
    row = lax.broadcasted_iota(jnp.int32, hs.shape, 0)
    bottom = jnp.where(row >= SUBLANE - k, hs, s[rows - SUBLANE:])
    return jnp.concatenate([s[:rows - SUBLANE], bottom], axis=0)


def _conv_taps(u, halo, w_ref, b_ref):
    s1, s2 = _shift_down(u, halo, 1), _shift_down(u, halo, 2)
    return w_ref[0:1, :] * s2 + w_ref[1:2, :] * s1 + w_ref[2:3, :] * u + b_ref[...], s1, s2


def _ffn_specs(rev):
    nrow = SEQ // FFN_TM
    per = FFN_TM // SUBLANE
    ri = (lambda i: nrow - 1 - i) if rev else (lambda i: i)
    main = lambda off: pl.BlockSpec((FFN_TM, FFN_TN), lambda j, i: (ri(i), j + off))
    halo = lambda off: pl.BlockSpec((SUBLANE, FFN_TN), lambda j, i: (jnp.maximum(ri(i) * per - 1, 0), j + off))
    wsp = lambda off: pl.BlockSpec((3, FFN_TN), lambda j, i: (0, j + off))
    bsp = lambda off: pl.BlockSpec((1, FFN_TN), lambda j, i: (0, j + off))
    return ri, main, halo, wsp, bsp


def _ffn_mid_fwd(u, conv_w, conv_b):
    ri, main, halo, wsp, bsp = _ffn_specs(False)

    def body(ua_ref, ub_ref, ha_ref, hb_ref, wa_ref, wb_ref, ba_ref, bb_ref, m_ref):
        live = (pl.program_id(1) > 0).astype(F32)
        a, _, _ = _conv_taps(ua_ref[...].astype(F32), ha_ref[...].astype(F32) * live, wa_ref, ba_ref)
        b, _, _ = _conv_taps(ub_ref[...].astype(F32), hb_ref[...].astype(F32) * live, wb_ref, bb_ref)
        m_ref[...] = (_gelu_parts(a)[0] * b).astype(m_ref.dtype)

    return pl.pallas_call(
        body, name="ffn_mid_fwd", grid=(FFN_NJ, SEQ // FFN_TM),
        in_specs=[main(0), main(FFN_NJ), halo(0), halo(FFN_NJ), wsp(0), wsp(FFN_NJ), bsp(0), bsp(FFN_NJ)],
        out_specs=pl.BlockSpec((FFN_TM, FFN_TN), lambda j, i: (i, j)),
        out_shape=_sds((SEQ, D_FF), BF16),
        compiler_params=_params("parallel", "arbitrary"),
    )(u, u, u, u, conv_w, conv_w, conv_b, conv_b)


def _ffn_mid_bwd(dm, u, conv_w, conv_b):
    ri, main, halo, wsp, bsp = _ffn_specs(True)
    nrow = SEQ // FFN_TM

    def body(dm_ref, ua_ref, ub_ref, ha_ref, hb_ref, wa_ref, wb_ref, ba_ref, bb_ref,
             dua_ref, dub_ref, gwa_ref, gwb_ref, gba_ref, gbb_ref, ca_s, cb_s):
        i = pl.program_id(1)
        live = (i < nrow - 1).astype(F32)

        @pl.when(i == 0)
        def _():
            ca_s[...] = jnp.zeros_like(ca_s)
            cb_s[...] = jnp.zeros_like(cb_s)
            for r in (gwa_ref, gwb_ref, gba_ref, gbb_ref):
                r[...] = jnp.zeros_like(r)

        ua, ub = ua_ref[...].astype(F32), ub_ref[...].astype(F32)
        a, a1, a2 = _conv_taps(ua, ha_ref[...].astype(F32) * live, wa_ref, ba_ref)
        b, b1, b2 = _conv_taps(ub, hb_ref[...].astype(F32) * live, wb_ref, bb_ref)
        gelu, dgelu = _gelu_parts(a)
        dmv = dm_ref[...].astype(F32)
        for du, s0, s1, s2, w_ref, c_s, du_ref, gw_ref, gb_ref in (
                (dmv * b * dgelu, ua, a1, a2, wa_ref, ca_s, dua_ref, gwa_ref, gba_ref),
                (dmv * gelu, ub, b1, b2, wb_ref, cb_s, dub_ref, gwb_ref, gbb_ref)):
            gw_ref[0:1, :] += jnp.sum(du * s2, axis=0, keepdims=True)
            gw_ref[1:2, :] += jnp.sum(du * s1, axis=0, keepdims=True)
            gw_ref[2:3, :] += jnp.sum(du * s0, axis=0, keepdims=True)
            gb_ref[...] += jnp.sum(du, axis=0, keepdims=True)
            nxt = c_s[...]
            du_ref[...] = (w_ref[2:3, :] * du + w_ref[1:2, :] * _shift_up(du, nxt, 1)
                           + w_ref[0:1, :] * _shift_up(du, nxt, 2)).astype(du_ref.dtype)
            c_s[...] = du[:SUBLANE]

    acc3 = pl.BlockSpec((3, FFN_TN), lambda j, i: (0, j))
    acc1 = pl.BlockSpec((1, FFN_TN), lambda j, i: (0, j))
    out_blk = pl.BlockSpec((FFN_TM, FFN_TN), lambda j, i: (ri(i), j))
    return pl.pallas_call(
        body, name="ffn_mid_bwd", grid=(FFN_NJ, nrow),
        in_specs=[pl.BlockSpec((FFN_TM, FFN_TN), lambda j, i: (ri(i), j)),
                  main(0), main(FFN_NJ), halo(0), halo(FFN_NJ), wsp(0), wsp(FFN_NJ), bsp(0), bsp(FFN_NJ)],
        out_specs=[out_blk, out_blk, acc3, acc3, acc1, acc1],
        out_shape=[_sds((SEQ, D_FF), BF16), _sds((SEQ, D_FF), BF16), _sds((3, D_FF), F32), _sds((3, D_FF), F32),
                   _sds((1, D_FF), F32), _sds((1, D_FF), F32)],
        scratch_shapes=[pltpu.VMEM((SUBLANE, FFN_TN), F32), pltpu.VMEM((SUBLANE, FFN_TN), F32)],
        compiler_params=_params("parallel", "arbitrary"),
    )(dm, u, u, u, u, conv_w, conv_w, conv_b, conv_b)


def _down_fwd(m, w_down, x2, g_post, target, *, tm=512):
    def body(m_ref, w_ref, x2_ref, g_ref, t_ref, dout_ref, dy_ref, gg_ref, loss_ref):
        @pl.when(pl.program_id(0) == 0)
        def _():
            gg_ref[...] = jnp.zeros_like(gg_ref)
            loss_ref[...] = jnp.zeros_like(loss_ref)

        y = jnp.dot(m_ref[...], w_ref[...], preferred_element_type=F32)
        r = lax.rsqrt(jnp.mean(y * y, axis=-1, keepdims=True) + RMS_EPS)
        yn = y * r
        diff = (x2_ref[...] + yn * g_ref[...]) - t_ref[...]
        loss_ref[...] += jnp.sum(diff * diff)
        dout = diff * (1.0 / D_MODEL)
        dout_ref[...] = dout
        gg_ref[...] += jnp.sum(dout * yn, axis=0, keepdims=True)
        dn = dout * g_ref[...]
        dy_ref[...] = (r * (dn - yn * jnp.mean(dn * yn, axis=-1, keepdims=True))).astype(dy_ref.dtype)

    row = pl.BlockSpec((tm, D_MODEL), lambda i: (i, 0))
    vec = pl.BlockSpec((1, D_MODEL), lambda i: (0, 0))
    return pl.pallas_call(
        body, name="down_fwd", grid=(SEQ // tm,),
        in_specs=[pl.BlockSpec((tm, D_FF), lambda i: (i, 0)), pl.BlockSpec((D_FF, D_MODEL), lambda i: (0, 0)),
                  row, vec, row],
        out_specs=[row, row, vec, pl.BlockSpec((1, LANE), lambda i: (0, 0))],
        out_shape=[_sds((SEQ, D_MODEL), F32), _sds((SEQ, D_MODEL), BF16), _sds((1, D_MODEL), F32),
                   _sds((1, LANE), F32)],
        compiler_params=_params("arbitrary"),
    )(m, w_down, x2, g_post, target)


def _local_step(x, target, w_main, w_f, b_forget, conv_b, g_pre_mix, g_post_mix, g_pre_ffn, g_post_ffn,
                late_weights, ffn_grads_ready, proj_grads_ready, mixer_grads_ready):
    mm = _matmul
    tabs = _rope_tables()

    h1 = _rms_fwd(x, g_pre_mix, name="rms_pre_mix")
    zm = mm(h1, w_main, out_dtype=BF16, tm=2048, tn=512, tk=1024, name="in_proj")
    zf = mm(h1, w_f, out_dtype=F32, tm=2048, tn=F_PAD, tk=1024, name="in_proj_forget")
    f_row, sg_row = _fox_prep(zf[:, :N_HEADS].T, b_forget.reshape(N_HEADS, 1))
    f_cols = jnp.pad(f_row.T, ((0, 0), (0, LANE - N_HEADS)))
    q_slots, k_slots, v_slots = _fox_pack_fwd(zm, f_cols)
    ya, lse_a = _fox_fwd(q_slots, k_slots, v_slots)
    qk = _rope_fwd(zm, tabs)
    dil = [_dil_fwd(qk, zm, d) for _, d in DIL_PATTERNS]
    yb, lse_b = _dil_merge([o for o, _ in dil], [l for _, l in dil])
    w_oa, w_ob, w_out, w_up, conv_w, w_down = late_weights(yb)
    pa, pb, mixed = _mix_fwd(ya, yb, w_oa, w_ob, zm)
    y1, x2, h2 = _out_fwd(mixed, w_out, x, g_post_mix, g_pre_ffn)
    u = mm(h2, w_up, out_dtype=BF16, tm=2048, tn=512, tk=1024, name="up_proj")
    m = _ffn_mid_fwd(u, conv_w, conv_b)
    dout, dy2, gg_post_ffn, sq_err = _down_fwd(m, w_down, x2, g_post_ffn, target)

    g_w_down = mm(m, dy2, ta=True, out_dtype=F32, tm=D_FF // 2, tn=1024, tk=512, name="grad_w_down")
    dm = mm(dy2, w_down, tb=True, out_dtype=BF16, tm=2048, tn=D_FF // 2, tk=1024, name="d_ffn_mid")
    du_a, du_b, gcw_a, gcw_b, gcb_a, gcb_b = _ffn_mid_bwd(dm, u, conv_w, conv_b)
    g_w_up = [mm(h2, t, ta=True, out_dtype=F32, tm=1024, tn=D_FF // 2, tk=512, name=f"grad_w_up_{s}")
              for s, t in (("a", du_a), ("b", du_b))]
    dh2 = [mm(t, w_up[:, o:o + D_FF], tb=True, out_dtype=F32, tm=1024, tn=1024, tk=D_FF, name=f"d_h2_{s}")
           for s, t, o in (("a", du_a, 0), ("b", du_b, D_FF))]
    dx2, gg_pre_ffn = _rms_bwd(dh2, x2, g_pre_ffn, dout, out_dtype=F32, name="rms_pre_ffn_bwd")
    tok = ffn_grads_ready(dict(w_down=g_w_down, w_up=jnp.concatenate(g_w_up, axis=1),
                               conv_w=jnp.concatenate([gcw_a, gcw_b], axis=1)))

    dy1, gg_post_mix = _rms_bwd([dx2], y1, g_post_mix + tok, None, out_dtype=BF16, name="rms_post_mix_bwd")
    g_w_out = mm(mixed, dy1, ta=True, out_dtype=F32, tm=1024, tn=1024, tk=512, name="grad_w_out")
    dmix = mm(dy1, w_out, tb=True, out_dtype=BF16, tm=2048, tn=1024, tk=1024, name="d_mixed")
    dpa, dpb, dgates = _gate_bwd(dmix, zm, pa, pb)
    g_w_oa = mm(ya, dpa, ta=True, out_dtype=F32, tm=512, tn=1024, tk=1024, name="grad_w_o_fox")
    g_w_ob = mm(yb, dpb, ta=True, out_dtype=F32, tm=512, tn=1024, tk=1024, name="grad_w_o_dil")
    tok = proj_grads_ready(dict(w_o_fox=g_w_oa, w_o_dil=g_w_ob, w_out=g_w_out))
    dya = mm(dpa, w_oa, tb=True, out_dtype=BF16, tm=2048, tn=512, tk=1024, name="d_y_fox")
    dyb = mm(dpb, w_ob, tb=True, out_dtype=BF16, tm=2048, tn=512, tk=1024, name="d_y_dil")

    qb_slots, do_slots = _fox_pack_bwd(zm, f_cols + tok, lse_a, ya, dya)
    dq_s, dk_s, dv_s = [t.reshape(SEQ, N_HEADS, SLOT) for t in _fox_bwd(qb_slots, k_slots, v_slots, do_slots)]
    data = lambda t: t[:, :, :HEAD_DIM].reshape(SEQ, ATT_W)
    dq_a, dk_a, dv_a = data(dq_s) * SCALE, data(dk_s), data(dv_s)
    dfa_t, g_b_forget = _fox_post_bwd(dq_s[:, :, HEAD_DIM].T, -dk_s[:, :, HEAD_DIM + N_SPLIT].T, sg_row)

    delta_b = _attn_delta(yb, dyb, name="delta_dil")
    dil_g = [_dil_bwd(qk, zm, lse_b, delta_b, dyb, d) for _, d in DIL_PATTERNS]
    d_dil = _dil_grad_combine([g[0] for g in dil_g], [g[1] for g in dil_g], [g[2] for g in dil_g], tabs)

    dz = jnp.concatenate([dq_a.astype(BF16), dk_a.astype(BF16), dv_a.astype(BF16), d_dil, dgates], axis=1)
    dzf = jnp.pad(dfa_t.T, ((0, 0), (0, F_PAD - N_HEADS)))
    g_w_main = mm(h1, dz, ta=True, out_dtype=F32, tm=1024, tn=Z_MAIN // 4, tk=512, name="grad_w_in")
    g_w_f = mm(h1, dzf, ta=True, out_dtype=F32, tm=1024, tn=F_PAD, tk=1024, name="grad_w_in_forget")
    tok = mixer_grads_ready(dict(w_main=g_w_main, w_f=g_w_f))
    dh1 = [mm(dz, w_main, tb=True, out_dtype=F32, tm=2048, tn=1024, tk=1024, name="d_h1"),
           mm(dzf + tok, w_f, tb=True, out_dtype=F32, tm=2048, tn=1024, tk=F_PAD, name="d_h1_forget")]
    grad_x, gg_pre_mix = _rms_bwd(dh1, x, g_pre_mix, dx2, out_dtype=F32, name="rms_pre_mix_bwd")

    grads = dict(
        b_forget=g_b_forget.reshape(1, N_HEADS), conv_b=jnp.concatenate([gcb_a, gcb_b], axis=1),
        g_pre_mix=gg_pre_mix, g_post_mix=gg_post_mix, g_pre_ffn=gg_pre_ffn, g_post_ffn=gg_post_ffn)
    return sq_err[0, 0], grad_x, grads


def _exchange(arrays, scatter, *, name):
    n = len(arrays)

    def body(*refs):
        ins, outs = refs[:n], refs[n:2 * n]
        send_sems, recv_sems, local_sems = refs[2 * n:]
        x, y, c = lax.axis_index("x"), lax.axis_index("y"), lax.axis_index("c")
        me = 4 * x + 2 * y + c
        peers = []
        for k in range(1, N_DEV):
            px = 1 - x if k & 4 else x
            py = 1 - y if k & 2 else y
            pc = 1 - c if k & 1 else c
            peers.append(((px, py, pc), 4 * px + 2 * py + pc))

        def remote(a, k):
            dev, slot = peers[k]
            return pltpu.make_async_remote_copy(
                src_ref=ins[a].at[slot] if scatter else ins[a], dst_ref=outs[a].at[me],
                send_sem=send_sems.at[a, k], recv_sem=recv_sems.at[a, k],
                device_id=dev, device_id_type=MESH_ID)

        def landed(a, k):
            dev, slot = peers[k]
            return pltpu.make_async_remote_copy(
                src_ref=outs[a].at[slot], dst_ref=outs[a].at[slot],
                send_sem=send_sems.at[a, k], recv_sem=recv_sems.at[a, k],
                device_id=dev, device_id_type=MESH_ID)

        own = [pltpu.make_async_copy(ins[a].at[me] if scatter else ins[a], outs[a].at[me], local_sems.at[a])
               for a in range(n)]
        copies = [remote(a, k) for k in range(N_DEV - 1) for a in range(n)]
        for cp in own + copies:
            cp.start()
        for k in range(N_DEV - 1):
            for a in range(n):
                landed(a, k).wait_recv()
        for cp in copies:
            cp.wait_send()
        for cp in own:
            cp.wait()

    out_shape = [_sds(((N_DEV,) + a.shape[-2:]), a.dtype) for a in arrays]
    return pl.pallas_call(
        body, name=name, in_specs=[ANY] * n, out_specs=[ANY] * n, out_shape=out_shape,
        scratch_shapes=[pltpu.SemaphoreType.DMA((n, N_DEV - 1)), pltpu.SemaphoreType.DMA((n, N_DEV - 1)),
                        pltpu.SemaphoreType.DMA((n,))],
    )(*arrays)


def _peers():
    x, y, c = lax.axis_index("x"), lax.axis_index("y"), lax.axis_index("c")
    out = []
    for k in range(1, N_DEV):
        px = 1 - x if k & 4 else x
        py = 1 - y if k & 2 else y
        pc = 1 - c if k & 1 else c
        out.append(((px, py, pc), 4 * px + 2 * py + pc))
    return 4 * x + 2 * y + c, out


HBM = pl.BlockSpec(memory_space=pltpu.HBM)
SEM = pl.BlockSpec(memory_space=pltpu.SEMAPHORE)
DATAFLOW = pltpu.SideEffectType.DATAFLOW_SIDE_EFFECTING


def _split_copy(srcs, lands, send_sems, recv_sems, scatter, a, k, me, peers, incoming=False):
    dev, slot = peers[k]
    if incoming:
        src = dst = lands[a].at[slot]
    else:
        src, dst = (srcs[a].at[slot] if scatter else srcs[a]), lands[a].at[me]
    return pltpu.make_async_remote_copy(
        src_ref=src, dst_ref=dst, send_sem=send_sems.at[a * (N_DEV - 1) + k], recv_sem=recv_sems.at[a * (N_DEV - 1) + k],
        device_id=dev, device_id_type=MESH_ID)


def _exchange_start(arrays, scatter, *, name):
    n = len(arrays)

    def body(*refs):
        srcs, lands = refs[:n], refs[n:2 * n]
        send_sems, recv_sems = refs[2 * n], refs[2 * n + 1]
        token = refs[-1]
        me, peers = _peers()
        for k in range(N_DEV - 1):
            for a in range(n):
                _split_copy(srcs, lands, send_sems, recv_sems, scatter, a, k, me, peers).start()
        token[...] = jnp.zeros_like(token)

    land_shapes = [((N_DEV,) + a.shape[-2:], a.dtype) for a in arrays]
    sems = pltpu.SemaphoreType.DMA((n * (N_DEV - 1),))
    outs = pl.pallas_call(
        body, name=name,
        out_shape=(sems, sems, *[pltpu.HBM(a.shape, a.dtype) for a in arrays],
                   *[pltpu.HBM(s, d) for s, d in land_shapes], _sds((SUBLANE, LANE), F32)),
        in_specs=[HBM] * (2 * n),
        out_specs=(SEM, SEM, *[HBM] * (2 * n), pl.BlockSpec(memory_space=pltpu.VMEM)),
        input_output_aliases={i: 2 + i for i in range(2 * n)},
        compiler_params=pltpu.CompilerParams(has_side_effects=DATAFLOW),
    )(*[pltpu.with_memory_space_constraint(a, pltpu.HBM) for a in arrays],
      *[pltpu.with_memory_space_constraint(lax.empty(s, d), pltpu.HBM) for s, d in land_shapes])
    return (outs[0], outs[1], outs[2:2 + n], outs[2 + n:2 + 2 * n], scatter), outs[-1]


def _exchange_wait(handles, after, *, name):
    send_sems, recv_sems, srcs, lands, scatter = handles
    n = len(srcs)

    def body(*refs):
        src_refs, land_refs = refs[:n], refs[n:2 * n]
        send_ref, recv_ref = refs[2 * n], refs[2 * n + 1]
        me, peers = _peers()
        for k in range(N_DEV - 1):
            for a in range(n):
                _split_copy(src_refs, land_refs, send_ref, recv_ref, scatter, a, k, me, peers).wait_send()
                _split_copy(src_refs, land_refs, send_ref, recv_ref, scatter, a, k, me, peers, True).wait_recv()

    outs = pl.pallas_call(
        body, name=name,
        out_shape=tuple(pltpu.HBM(t.shape, t.dtype) for t in (*srcs, *lands)),
        in_specs=[HBM] * (2 * n) + [SEM, SEM, pl.BlockSpec(memory_space=pl.ANY)],
        out_specs=tuple([HBM] * (2 * n)),
        input_output_aliases={i: i for i in range(2 * n)},
        compiler_params=pltpu.CompilerParams(has_side_effects=DATAFLOW),
    )(*srcs, *lands, send_sems, recv_sems, after)
    return _with_own_slot(outs[n:], outs[:n], scatter)


def _with_own_slot(landed, own, scatter):
    me = 4 * lax.axis_index("x") + 2 * lax.axis_index("y") + lax.axis_index("c")
    out = []
    for buf, src in zip(landed, own):
        mine = lax.dynamic_index_in_dim(src, me, 0, keepdims=False) if scatter else src
        out.append(lax.dynamic_update_index_in_dim(buf, mine, me, 0))
    return out


def _adamw(parts, w, m, v, *, name, tm):
    r, c = w.shape
    assert r % tm == 0

    def body(p_ref, w_ref, m_ref, v_ref, g_ref, d_ref, nm_ref, nv_ref):
        g = p_ref[0].astype(F32)
        for s in range(1, N_DEV):
            g = g + p_ref[s].astype(F32)
        g_ref[...] = g
        m_new = ADAM_B1 * m_ref[...] + (1.0 - ADAM_B1) * g
        v_new = ADAM_B2 * v_ref[...] + (1.0 - ADAM_B2) * (g * g)
        nm_ref[...] = m_new
        nv_ref[...] = v_new
        m_hat = m_new / (1.0 - ADAM_B1 ** ADAM_STEP)
        v_hat = v_new / (1.0 - ADAM_B2 ** ADAM_STEP)
        d_ref[...] = -ADAM_LR * (m_hat / (jnp.sqrt(v_hat) + ADAM_EPS) + ADAM_WD * w_ref[...])

    blk = pl.BlockSpec((tm, c), lambda i: (i, 0))
    return pl.pallas_call(
        body, name=name, grid=(r // tm,),
        in_specs=[pl.BlockSpec((N_DEV, tm, c), lambda i: (0, i, 0)), blk, blk, blk],
        out_specs=[blk] * 4, out_shape=[_sds((r, c), F32)] * 4,
        compiler_params=_params("parallel"),
    )(parts, w, m, v)


SMALL = (("g_pre_mix", D_MODEL), ("b_forget", LANE), ("g_post_mix", D_MODEL), ("g_pre_ffn", D_MODEL),
         ("conv_b", 2 * D_FF), ("g_post_ffn", D_MODEL))
SMALL_ROWS = 80


def _pack_small(vals):
    flat = [jnp.pad(vals[n].reshape(-1), (0, size - vals[n].size)) for n, size in SMALL]
    flat = jnp.concatenate(flat)
    return jnp.pad(flat, (0, SMALL_ROWS * LANE - flat.size)).reshape(SMALL_ROWS, LANE)


def _unpack_small(packed, shapes):
    flat, out, off = packed.reshape(-1), {}, 0
    for n, size in SMALL:
        cnt = math.prod(shapes[n])
        out[n] = flat[off:off + cnt].reshape(shapes[n])
        off += size
    return out


def kernel(x, g_pre_mix, w_in, b_forget, w_o_fox, w_o_dil, w_out, g_post_mix, g_pre_ffn, w_up, conv_w, conv_b, w_down, g_post_ffn, loss_target, m_g_pre_mix, m_w_in, m_b_forget, m_w_o_fox, m_w_o_dil, m_w_out, m_g_post_mix, m_g_pre_ffn, m_w_up, m_conv_w, m_conv_b, m_w_down, m_g_post_ffn, v_g_pre_mix, v_w_in, v_b_forget, v_w_o_fox, v_w_o_dil, v_w_out, v_g_post_mix, v_g_pre_ffn, v_w_up, v_conv_w, v_conv_b, v_w_down, v_g_post_ffn):
    names = ("g_pre_mix", "w_in", "b_forget", "w_o_fox", "w_o_dil", "w_out", "g_post_mix", "g_pre_ffn",
             "w_up", "conv_w", "conv_b", "w_down", "g_post_ffn")
    w = dict(g_pre_mix=g_pre_mix, w_in=w_in, b_forget=b_forget, w_o_fox=w_o_fox, w_o_dil=w_o_dil, w_out=w_out,
             g_post_mix=g_post_mix, g_pre_ffn=g_pre_ffn, w_up=w_up, conv_w=conv_w, conv_b=conv_b, w_down=w_down,
             g_post_ffn=g_post_ffn)
    m = dict(g_pre_mix=m_g_pre_mix, w_in=m_w_in, b_forget=m_b_forget, w_o_fox=m_w_o_fox, w_o_dil=m_w_o_dil,
             w_out=m_w_out, g_post_mix=m_g_post_mix, g_pre_ffn=m_g_pre_ffn, w_up=m_w_up, conv_w=m_conv_w,
             conv_b=m_conv_b, w_down=m_w_down, g_post_ffn=m_g_post_ffn)
    v = dict(g_pre_mix=v_g_pre_mix, w_in=v_w_in, b_forget=v_b_forget, w_o_fox=v_w_o_fox, w_o_dil=v_w_o_dil,
             w_out=v_w_out, g_post_mix=v_g_post_mix, g_pre_ffn=v_g_pre_ffn, w_up=v_w_up, conv_w=v_conv_w,
             conv_b=v_conv_b, w_down=v_w_down, g_post_ffn=v_g_post_ffn)
    sharded = ("w_in", "w_o_fox", "w_o_dil", "w_out", "w_up", "w_down", "conv_w")
    wire = lambda n: F32 if n == "conv_w" else BF16

    by_cols = lambda t: jnp.transpose(t, (1, 0, 2)).reshape(t.shape[1], N_DEV * t.shape[2])
    by_rows = lambda t: t.reshape(N_DEV * t.shape[1], t.shape[2])
    col_slots = lambda t: jnp.transpose(t.reshape(t.shape[0], N_DEV, t.shape[1] // N_DEV), (1, 0, 2))
    row_slots = lambda t: t.reshape(N_DEV, t.shape[0] // N_DEV, t.shape[1])
    to_slots = lambda n, t: (row_slots if n in ("w_out", "w_down") else col_slots)(t).astype(wire(n))
    shard = lambda n: w[n][0].astype(wire(n))
    f_lo, f_hi = 3 * ATT_W, 3 * ATT_W + N_HEADS

    w_in_full = by_cols(_exchange([shard("w_in")], False, name="gather_w_in")[0])
    w_main = jnp.concatenate([w_in_full[:, :f_lo], w_in_full[:, f_hi:]], axis=1)
    w_f = jnp.pad(w_in_full[:, f_lo:f_hi], ((0, 0), (0, F_PAD - N_HEADS)))
    late = ("w_o_fox", "w_o_dil", "w_out", "w_up", "conv_w", "w_down")
    order = jnp.minimum(jnp.abs(w_in_full[0, 0].astype(F32)), 0.0)
    late_handles, late_tok = _exchange_start(
        [shard(n) + order.astype(wire(n)) if n == "conv_w" else shard(n) for n in late], False,
        name="gather_late_start")

    def late_weights(after):
        got = dict(zip(late, _exchange_wait(late_handles, after, name="gather_late_wait")))
        return (by_cols(got["w_o_fox"]), by_cols(got["w_o_dil"]), by_rows(got["w_out"]), by_cols(got["w_up"]),
                by_cols(got["conv_w"]), by_rows(got["w_down"]))

    pending = {}

    def ffn_grads_ready(g):
        pending["ffn"] = _exchange_start([to_slots(n, g[n]) for n in ("w_down", "w_up", "conv_w")], True,
                                         name="scatter_ffn_start")
        return pending["ffn"][1][0, 0]

    def proj_grads_ready(g):
        pending["proj"] = _exchange_start([to_slots(n, g[n]) for n in ("w_o_fox", "w_o_dil", "w_out")], True,
                                          name="scatter_proj_start")
        return pending["proj"][1][0, 0]

    def mixer_grads_ready(g):
        g_w_in = jnp.concatenate([g["w_main"][:, :f_lo], g["w_f"][:, :N_HEADS], g["w_main"][:, f_lo:]], axis=1)
        pending["w_in"] = _exchange_start([to_slots("w_in", g_w_in)], True, name="scatter_w_in_start")
        return pending["w_in"][1][0, 0]

    sq_err, grad_x, g = _local_step(
        x[0], loss_target[0], w_main, w_f, b_forget, conv_b, g_pre_mix + late_tok[0, 0], g_post_mix, g_pre_ffn,
        g_post_ffn, late_weights, ffn_grads_ready, proj_grads_ready, mixer_grads_ready)
    loss = lax.psum(0.5 * sq_err / D_MODEL, ("x", "y", "c"))

    tiles = dict(w_in=256, w_o_fox=512, w_o_dil=512, w_out=128, w_up=256, w_down=176, conv_w=3)
    adam = lambda n, p: _adamw(p, w[n][0], m[n][0], v[n][0], name=f"adamw_{n}", tm=tiles[n])
    res = {}
    for key, group in (("ffn", ("w_down", "w_up", "conv_w")), ("proj", ("w_o_fox", "w_o_dil", "w_out"))):
        landed = _exchange_wait(pending[key][0], grad_x, name=f"scatter_{key}_wait")
        res.update({n: adam(n, p) for n, p in zip(group, landed)})
    small_parts = _exchange([_pack_small(g)], False, name="gather_small_grads")[0]
    done = res["w_up"][3]
    res["w_in"] = adam("w_in", _exchange_wait(pending["w_in"][0], done, name="scatter_w_in_wait")[0])
    small = _adamw(small_parts, _pack_small(w), _pack_small(m), _pack_small(v), name="adamw_small", tm=SMALL_ROWS)
    shapes = {n: w[n].shape for n, _ in SMALL}
    small = [_unpack_small(t, shapes) for t in small]
    out = [[(res[n][k][None] if n in sharded else small[k][n]) for n in names] for k in range(4)]
    return (loss, grad_x[None], *out[0], *out[1], *out[2], *out[3])
```

```python
import functools
import math

import jax
import jax.numpy as jnp
from jax import lax
from jax.experimental import pallas as pl
from jax.experimental.pallas import tpu as pltpu

F32 = jnp.float32
BF16 = jnp.bfloat16

SEQ = 4096
D_MODEL = 1024
N_HEADS = 8
HEAD_DIM = 64
ATT_W = N_HEADS * HEAD_DIM
D_FF = 2816
Z_MAIN = 5120
F_PAD = 128
ROPE_DIM = 16
ROPE_THETA = 500000.0
RMS_EPS = 1e-6
NEG_INF = -1e30
SCALE = 1.0 / math.sqrt(HEAD_DIM)
DIL_PATTERNS = ((128, 1), (512, 4), (2048, 16))
DIL_BLK = 128
N_DEV = 8

ADAM_LR = 0.001
ADAM_B1 = 0.9
ADAM_B2 = 0.999
ADAM_EPS = 1e-08
ADAM_WD = 0.01
ADAM_STEP = 10

LANE = 128
SUBLANE = 8
VMEM_LIMIT = 56 * 1024 * 1024
MESH_ID = pl.DeviceIdType.MESH
ANY = pl.BlockSpec(memory_space=pl.ANY)


def _params(*sem):
    return pltpu.CompilerParams(dimension_semantics=sem, vmem_limit_bytes=VMEM_LIMIT)


def _sds(shape, dtype):
    return jax.ShapeDtypeStruct(shape, dtype)


def _matmul(a, b, *, ta=False, tb=False, out_dtype, tm, tn, tk, name):
    if ta:
        kk, m = a.shape
    else:
        m, kk = a.shape
    n = b.shape[0] if tb else b.shape[1]
    assert (b.shape[1] if tb else b.shape[0]) == kk
    tm, tn, tk = min(tm, m), min(tn, n), min(tk, kk)
    assert m % tm == 0 and n % tn == 0 and kk % tk == 0, (name, m, n, kk, tm, tn, tk)
    nk = kk // tk
    dims = (((0 if ta else 1,), (1 if tb else 0,)), ((), ()))

    def body(a_ref, b_ref, o_ref, *scratch):
        p = lax.dot_general(a_ref[...].astype(BF16), b_ref[...].astype(BF16), dims,
                            preferred_element_type=F32)
        if nk == 1:
            o_ref[...] = p.astype(o_ref.dtype)
        else:
            acc = scratch[0]
            k = pl.program_id(2)

            @pl.when(k == 0)
            def _():
                acc[...] = p

            @pl.when(k > 0)
            def _():
                acc[...] += p

            @pl.when(k == nk - 1)
            def _():
                o_ref[...] = acc[...].astype(o_ref.dtype)

    a_spec = (pl.BlockSpec((tk, tm), lambda i, j, k: (k, i)) if ta
              else pl.BlockSpec((tm, tk), lambda i, j, k: (i, k)))
    b_spec = (pl.BlockSpec((tn, tk), lambda i, j, k: (j, k)) if tb
              else pl.BlockSpec((tk, tn), lambda i, j, k: (k, j)))
    return pl.pallas_call(
        body, name=name, grid=(m // tm, n // tn, nk),
        in_specs=[a_spec, b_spec],
        out_specs=pl.BlockSpec((tm, tn), lambda i, j, k: (i, j)),
        out_shape=_sds((m, n), out_dtype),
        scratch_shapes=[pltpu.VMEM((tm, tn), F32)] if nk > 1 else [],
        compiler_params=_params("parallel", "parallel", "arbitrary"),
    )(a, b)


def _rms_fwd(x, g, *, name, tm=512):
    def body(x_ref, g_ref, h_ref):
        xv = x_ref[...]
        r = lax.rsqrt(jnp.mean(xv * xv, axis=-1, keepdims=True) + RMS_EPS)
        h_ref[...] = (xv * r * g_ref[...]).astype(h_ref.dtype)

    return pl.pallas_call(
        body, name=name, grid=(SEQ // tm,),
        in_specs=[pl.BlockSpec((tm, D_MODEL), lambda i: (i, 0)), pl.BlockSpec((1, D_MODEL), lambda i: (0, 0))],
        out_specs=pl.BlockSpec((tm, D_MODEL), lambda i: (i, 0)),
        out_shape=_sds((SEQ, D_MODEL), BF16),
        compiler_params=_params("parallel"),
    )(x, g)


def _rms_bwd(dh_parts, xin, g, dres, *, out_dtype, name, tm=512):
    n_parts = len(dh_parts)
    has_res = dres is not None

    def body(*refs):
        parts = refs[:n_parts]
        x_ref, g_ref = refs[n_parts], refs[n_parts + 1]
        res_ref = refs[n_parts + 2] if has_res else None
        o_ref, gg_ref = refs[-2], refs[-1]
        dh = parts[0][...].astype(F32)
        for p in parts[1:]:
            dh = dh + p[...].astype(F32)
        xv = x_ref[...]
        r = lax.rsqrt(jnp.mean(xv * xv, axis=-1, keepdims=True) + RMS_EPS)
        xn = xv * r

        @pl.when(pl.program_id(0) == 0)
        def _():
            gg_ref[...] = jnp.zeros_like(gg_ref)

        gg_ref[...] += jnp.sum(dh * xn, axis=0, keepdims=True)
        dxn = dh * g_ref[...]
        dx = r * (dxn - xn * jnp.mean(dxn * xn, axis=-1, keepdims=True))
        if has_res:
            dx = dx + res_ref[...]
        o_ref[...] = dx.astype(o_ref.dtype)

    row = pl.BlockSpec((tm, D_MODEL), lambda i: (i, 0))
    vec = pl.BlockSpec((1, D_MODEL), lambda i: (0, 0))
    args = list(dh_parts) + [xin, g] + ([dres] if has_res else [])
    return pl.pallas_call(
        body, name=name, grid=(SEQ // tm,),
        in_specs=[row] * n_parts + [row, vec] + ([row] if has_res else []),
        out_specs=[row, vec],
        out_shape=[_sds((SEQ, D_MODEL), out_dtype), _sds((1, D_MODEL), F32)],
        compiler_params=_params("arbitrary"),
    )(*args)


SCAN_BLK = 512


def _split_dot(v, tri):
    hi = v.astype(BF16)
    r1 = v - hi.astype(F32)
    mid = r1.astype(BF16)
    lo = (r1 - mid.astype(F32)).astype(BF16)
    dot = functools.partial(jnp.dot, preferred_element_type=F32)
    return dot(hi, tri) + dot(mid, tri) + dot(lo, tri)


def _fox_prep(fa_t, b_col):
    nblk = SEQ // SCAN_BLK

    def body(fa_ref, b_ref, f_ref, sg_ref):
        row = lax.broadcasted_iota(jnp.int32, (SCAN_BLK, SCAN_BLK), 0)
        col = lax.broadcasted_iota(jnp.int32, (SCAN_BLK, SCAN_BLK), 1)
        upper = (row <= col).astype(BF16)
        carry = jnp.zeros((N_HEADS, 1), F32)
        for blk in range(nblk):
            sl = pl.ds(blk * SCAN_BLK, SCAN_BLK)
            xx = fa_ref[:, sl] + b_ref[...]
            e = jnp.exp(-jnp.abs(xx))
            logf = jnp.minimum(xx, 0.0) - jnp.log(1.0 + e)
            sg_ref[:, sl] = jnp.where(xx >= 0.0, e, 1.0) / (1.0 + e)
            c = _split_dot(logf, upper) + carry
            f_ref[:, sl] = c
            carry = c[:, SCAN_BLK - 1:SCAN_BLK]

    return pl.pallas_call(
        body, name="fox_prep",
        out_shape=[_sds((N_HEADS, SEQ), F32), _sds((N_HEADS, SEQ), F32)],
        compiler_params=pltpu.CompilerParams(vmem_limit_bytes=VMEM_LIMIT),
    )(fa_t, b_col)


def _fox_post_bwd(df_rows_t, df_cols_t, sg_t):
    nblk = SEQ // SCAN_BLK

    def body(dfr_ref, dfc_ref, sg_ref, dfa_ref, gb_ref):
        row = lax.broadcasted_iota(jnp.int32, (SCAN_BLK, SCAN_BLK), 0)
        col = lax.broadcasted_iota(jnp.int32, (SCAN_BLK, SCAN_BLK), 1)
        lower = (row >= col).astype(BF16)
        carry = jnp.zeros((N_HEADS, 1), F32)
        gb = jnp.zeros((N_HEADS, 1), F32)
        for blk in reversed(range(nblk)):
            sl = pl.ds(blk * SCAN_BLK, SCAN_BLK)
            c = _split_dot(dfr_ref[:, sl] + dfc_ref[:, sl], lower) + carry
            carry = c[:, 0:1]
            dfa = c * sg_ref[:, sl]
            dfa_ref[:, sl] = dfa
            gb = gb + jnp.sum(dfa, axis=1, keepdims=True)
        gb_ref[...] = gb

    return pl.pallas_call(
        body, name="fox_post_bwd",
        out_shape=[_sds((N_HEADS, SEQ), F32), _sds((N_HEADS, 1), F32)],
        compiler_params=pltpu.CompilerParams(vmem_limit_bytes=VMEM_LIMIT),
    )(df_rows_t, df_cols_t, sg_t)


FOX_T = 512
NT_DIMS = (((1,), (1,)), ((), ()))
TN_DIMS = (((0,), (0,)), ((), ()))


def _head(ref_or_val, h):
    return ref_or_val[:, h * HEAD_DIM:(h + 1) * HEAD_DIM]


def _split3(v):
    hi = v.astype(BF16).astype(F32)
    r1 = v - hi
    mid = r1.astype(BF16).astype(F32)
    return hi, mid, (r1 - mid).astype(BF16).astype(F32)


def _aux_lanes(rows, terms):
    lane = lax.broadcasted_iota(jnp.int32, (rows, HEAD_DIM), 1)
    out = jnp.zeros((rows, HEAD_DIM), F32)
    for i, t in enumerate(terms):
        out = jnp.where(lane == i, t, out)
    return out


SLOT = 2 * HEAD_DIM
N_SPLIT = 3


def _slot(ref, h):
    return ref[:, h * SLOT:(h + 1) * SLOT]


def _fox_pack_fwd(zm, f_cols, *, tm=512):
    def body(q_ref, k_ref, v_ref, f_ref, qs_ref, ks_ref, vs_ref):
        ones = jnp.ones((tm, HEAD_DIM), BF16)
        for h in range(N_HEADS):
            fh = _split3(f_ref[:, h:h + 1])
            q_aux = _aux_lanes(tm, list(fh) + [1.0] * N_SPLIT)
            k_aux = _aux_lanes(tm, [1.0] * N_SPLIT + [-t for t in fh])
            qs_ref[:, h * SLOT:(h + 1) * SLOT] = jnp.concatenate(
                [(_head(q_ref, h).astype(F32) * SCALE).astype(BF16), q_aux.astype(BF16)], axis=1)
            ks_ref[:, h * SLOT:(h + 1) * SLOT] = jnp.concatenate([_head(k_ref, h), k_aux.astype(BF16)], axis=1)
            vs_ref[:, h * SLOT:(h + 1) * SLOT] = jnp.concatenate([_head(v_ref, h), ones], axis=1)

    col = lambda b: pl.BlockSpec((tm, ATT_W), lambda i: (i, b))
    wide = pl.BlockSpec((tm, N_HEADS * SLOT), lambda i: (i, 0))
    return pl.pallas_call(
        body, name="fox_pack_fwd", grid=(SEQ // tm,),
        in_specs=[col(0), col(1), col(2), pl.BlockSpec((tm, LANE), lambda i: (i, 0))],
        out_specs=[wide] * 3, out_shape=[_sds((SEQ, N_HEADS * SLOT), BF16)] * 3,
        compiler_params=_params("parallel"),
    )(zm, zm, zm, f_cols)


def _fox_pack_bwd(zm, f_cols, lse, o, do, *, tm=512):
    def body(q_ref, f_ref, lse_ref, o_ref, do_ref, qs_ref, ds_ref):
        for h in range(N_HEADS):
            gh = _split3(f_ref[:, h:h + 1] - lse_ref[:, h * HEAD_DIM:h * HEAD_DIM + 1])
            dout = _head(do_ref, h)
            delta = jnp.sum(_head(o_ref, h).astype(F32) * dout.astype(F32), axis=1, keepdims=True)
            q_aux = _aux_lanes(tm, list(gh) + [1.0] * N_SPLIT)
            d_aux = _aux_lanes(tm, [-t for t in _split3(delta)])
            qs_ref[:, h * SLOT:(h + 1) * SLOT] = jnp.concatenate(
                [(_head(q_ref, h).astype(F32) * SCALE).astype(BF16), q_aux.astype(BF16)], axis=1)
            ds_ref[:, h * SLOT:(h + 1) * SLOT] = jnp.concatenate([dout, d_aux.astype(BF16)], axis=1)

    row = pl.BlockSpec((tm, ATT_W), lambda i: (i, 0))
    wide = pl.BlockSpec((tm, N_HEADS * SLOT), lambda i: (i, 0))
    return pl.pallas_call(
        body, name="fox_pack_bwd", grid=(SEQ // tm,),
        in_specs=[row, pl.BlockSpec((tm, LANE), lambda i: (i, 0)), row, row, row],
        out_specs=[wide] * 2, out_shape=[_sds((SEQ, N_HEADS * SLOT), BF16)] * 2,
        compiler_params=_params("parallel"),
    )(zm, f_cols, lse, o, do)


def _causal_pairs(key_major):
    nb = SEQ // FOX_T
    if key_major:
        pairs = [(i, j) for j in range(nb) for i in range(j, nb)]
    else:
        pairs = [(i, j) for i in range(nb) for j in range(i + 1)]
    return (jnp.array([p[0] for p in pairs], jnp.int32), jnp.array([p[1] for p in pairs], jnp.int32), len(pairs))


def _diag_mask():
    row = lax.broadcasted_iota(jnp.int32, (FOX_T, FOX_T), 0)
    col = lax.broadcasted_iota(jnp.int32, (FOX_T, FOX_T), 1)
    return col <= row


def _fox_fwd(q_slots, k_slots, v_slots):
    i_tab, j_tab, n_pairs = _causal_pairs(False)

    def body(i_tab, j_tab, q_ref, k_ref, v_ref, o_ref, lse_ref, m_s, acc_s):
        t = pl.program_id(1)
        i, j = i_tab[t], j_tab[t]

        @pl.when(j == 0)
        def _():
            m_s[...] = jnp.full_like(m_s, NEG_INF)
            acc_s[...] = jnp.zeros_like(acc_s)

        def step(masked):
            for h in range(2):
                s = lax.dot_general(_slot(q_ref, h), _slot(k_ref, h), NT_DIMS, preferred_element_type=F32)
                if masked:
                    s = jnp.where(_diag_mask(), s, NEG_INF)
                m_prev = m_s[h]
                m_new = jnp.maximum(m_prev, jnp.max(s, axis=-1, keepdims=True))
                p = jnp.exp(s - jnp.tile(m_new, (1, FOX_T // LANE)))
                acc_s[h] = jnp.exp(m_prev - m_new) * acc_s[h] + jnp.dot(
                    p.astype(BF16), _slot(v_ref, h), preferred_element_type=F32)
                m_s[h] = m_new

        @pl.when(j < i)
        def _():
            step(False)

        @pl.when(j == i)
        def _():
            step(True)
            outs, lses = [], []
            for h in range(2):
                acc = acc_s[h]
                l = acc[:, HEAD_DIM:]
                outs.append(acc[:, :HEAD_DIM] / l)
                lses.append(m_s[h][:, :HEAD_DIM] + jnp.log(l))
            o_ref[...] = jnp.concatenate(outs, axis=1).astype(o_ref.dtype)
            lse_ref[...] = jnp.concatenate(lses, axis=1)

    qspec = pl.BlockSpec((FOX_T, 2 * SLOT), lambda p, t, it, jt: (it[t], p))
    kspec = pl.BlockSpec((FOX_T, 2 * SLOT), lambda p, t, it, jt: (jt[t], p))
    ospec = pl.BlockSpec((FOX_T, LANE), lambda p, t, it, jt: (it[t], p))
    return pl.pallas_call(
        body, name="fox_fwd",
        grid_spec=pltpu.PrefetchScalarGridSpec(
            num_scalar_prefetch=2, grid=(N_HEADS // 2, n_pairs),
            in_specs=[qspec, kspec, kspec], out_specs=[ospec, ospec],
            scratch_shapes=[pltpu.VMEM((2, FOX_T, LANE), F32), pltpu.VMEM((2, FOX_T, SLOT), F32)]),
        out_shape=[_sds((SEQ, ATT_W), BF16), _sds((SEQ, ATT_W), F32)],
        compiler_params=_params("parallel", "arbitrary"),
    )(i_tab, j_tab, q_slots, k_slots, v_slots)


def _fox_bwd(q_slots, k_slots, v_slots, do_slots):
    i_tab, j_tab, n_pairs = _causal_pairs(True)

    def body(i_tab, j_tab, q_ref, k_ref, v_ref, do_ref, dq_ref, dk_ref, dv_ref):
        t = pl.program_id(1)
        i, j = i_tab[t], j_tab[t]

        @pl.when(t == 0)
        def _():
            dq_ref[...] = jnp.zeros_like(dq_ref)

        @pl.when(i == j)
        def _():
            dk_ref[...] = jnp.zeros_like(dk_ref)
            dv_ref[...] = jnp.zeros_like(dv_ref)

        def step(masked):
            rows = pl.ds(pl.multiple_of(i * FOX_T, FOX_T), FOX_T)
            for h in range(2):
                cols = slice(h * SLOT, (h + 1) * SLOT)
                q, k, v, dout = _slot(q_ref, h), _slot(k_ref, h), _slot(v_ref, h), _slot(do_ref, h)
                p = jnp.exp(lax.dot_general(q, k, NT_DIMS, preferred_element_type=F32))
                if masked:
                    p = jnp.where(_diag_mask(), p, 0.0)
                dp = lax.dot_general(dout, v, NT_DIMS, preferred_element_type=F32)
                ds = (p * dp).astype(BF16)
                dv_ref[:, cols] += lax.dot_general(p.astype(BF16), dout, TN_DIMS, preferred_element_type=F32)
                dk_ref[:, cols] += lax.dot_general(ds, q, TN_DIMS, preferred_element_type=F32)
                dq_ref[rows, cols] += jnp.dot(ds, k, preferred_element_type=F32)

        @pl.when(i > j)
        def _():
            step(False)

        @pl.when(i == j)
        def _():
            step(True)

    qspec = pl.BlockSpec((FOX_T, 2 * SLOT), lambda p, t, it, jt: (it[t], p))
    kspec = pl.BlockSpec((FOX_T, 2 * SLOT), lambda p, t, it, jt: (jt[t], p))
    return pl.pallas_call(
        body, name="fox_bwd",
        grid_spec=pltpu.PrefetchScalarGridSpec(
            num_scalar_prefetch=2, grid=(N_HEADS // 2, n_pairs),
            in_specs=[qspec, kspec, kspec, qspec],
            out_specs=[pl.BlockSpec((SEQ, 2 * SLOT), lambda p, t, it, jt: (0, p)), kspec, kspec]),
        out_shape=[_sds((SEQ, N_HEADS * SLOT), F32)] * 3,
        compiler_params=_params("arbitrary", "arbitrary"),
    )(i_tab, j_tab, q_slots, k_slots, v_slots, do_slots)


def _attn_delta(o, do, *, name, tm=512):
    def body(o_ref, do_ref, d_ref):
        prod = o_ref[...].astype(F32) * do_ref[...].astype(F32)
        lane = lax.broadcasted_iota(jnp.int32, (tm, LANE), 1)
        out = jnp.zeros((tm, LANE), F32)
        for h in range(N_HEADS):
            out = jnp.where(lane == h, jnp.sum(_head(prod, h), axis=1, keepdims=True), out)
        d_ref[...] = out

    row = pl.BlockSpec((tm, ATT_W), lambda i: (i, 0))
    return pl.pallas_call(
        body, name=name, grid=(SEQ // tm,), in_specs=[row, row],
        out_specs=pl.BlockSpec((tm, LANE), lambda i: (i, 0)), out_shape=_sds((SEQ, LANE), F32),
        compiler_params=_params("parallel"),
    )(o, do)


def _rope_tables():
    half = ROPE_DIM // 2
    inv_freq = ROPE_THETA ** (-jnp.arange(half, dtype=F32) * 2.0 / ROPE_DIM)
    ang = jnp.arange(SEQ, dtype=F32)[:, None] * inv_freq[None, :]
    cos, sin = jnp.cos(ang), jnp.sin(ang)
    ones = jnp.ones((SEQ, HEAD_DIM - ROPE_DIM), F32)
    zeros = jnp.zeros((SEQ, HEAD_DIM - ROPE_DIM), F32)
    zh = jnp.zeros((SEQ, half), F32)
    c_tab = jnp.concatenate([cos, cos, ones], axis=1)
    a_tab = jnp.concatenate([-sin, zh, zeros], axis=1)
    b_tab = jnp.concatenate([zh, sin, zeros], axis=1)
    two = lambda t: jnp.concatenate([t, t], axis=1)
    return two(c_tab), two(a_tab), two(b_tab)


def _rotate(x, c_tab, a_tab, b_tab):
    return x * c_tab + pltpu.roll(x, LANE - ROPE_DIM // 2, 1) * a_tab + pltpu.roll(x, ROPE_DIM // 2, 1) * b_tab


def _rope_fwd(zm, tabs, *, tm=512):
    def body(q_ref, k_ref, v_ref, c_ref, a_ref, b_ref, o_ref):
        for part, (x_ref, mult) in enumerate(((q_ref, SCALE), (k_ref, 1.0))):
            for cc in range(ATT_W // LANE):
                sl = slice(cc * LANE, (cc + 1) * LANE)
                rot = _rotate(x_ref[:, sl].astype(F32), c_ref[...], a_ref[...], b_ref[...])
                o_ref[:, part * ATT_W + cc * LANE:part * ATT_W + (cc + 1) * LANE] = (rot * mult).astype(o_ref.dtype)
        o_ref[:, 2 * ATT_W:] = v_ref[...]

    tab = pl.BlockSpec((tm, LANE), lambda i: (i, 0))
    col = lambda b: pl.BlockSpec((tm, ATT_W), lambda i: (i, b))
    return pl.pallas_call(
        body, name="rope_fwd", grid=(SEQ // tm,),
        in_specs=[col(3), col(4), col(5), tab, tab, tab],
        out_specs=pl.BlockSpec((tm, 3 * ATT_W), lambda i: (i, 0)),
        out_shape=_sds((SEQ, 3 * ATT_W), BF16),
        compiler_params=_params("parallel"),
    )(zm, zm, zm, *tabs)


def _dil_grad_combine(dqs, dks, dvs, tabs, *, tm=256):
    def body(*refs):
        q_refs, k_refs, v_refs = refs[0:3], refs[3:6], refs[6:9]
        c_ref, a_ref, b_ref, o_ref = refs[9:]
        total = lambda rs, sl: rs[0][:, sl].astype(F32) + rs[1][:, sl].astype(F32) + rs[2][:, sl].astype(F32)
        for cc in range(ATT_W // LANE):
            sl = slice(cc * LANE, (cc + 1) * LANE)
            for part, rs in enumerate((q_refs, k_refs)):
                o_ref[:, part * ATT_W + cc * LANE:part * ATT_W + (cc + 1) * LANE] = _rotate(
                    total(rs, sl), c_ref[...], -a_ref[...], -b_ref[...]).astype(o_ref.dtype)
            o_ref[:, 2 * ATT_W + cc * LANE:2 * ATT_W + (cc + 1) * LANE] = total(v_refs, sl).astype(o_ref.dtype)

    row = pl.BlockSpec((tm, ATT_W), lambda i: (i, 0))
    tab = pl.BlockSpec((tm, LANE), lambda i: (i, 0))
    return pl.pallas_call(
        body, name="dil_grad_combine", grid=(SEQ // tm,),
        in_specs=[row] * 9 + [tab] * 3,
        out_specs=pl.BlockSpec((tm, 3 * ATT_W), lambda i: (i, 0)),
        out_shape=_sds((SEQ, 3 * ATT_W), BF16),
        compiler_params=_params("parallel"),
    )(*dqs, *dks, *dvs, *tabs)


def _dil_valid(n):
    qi = lax.broadcasted_iota(jnp.int32, (DIL_BLK, 2 * DIL_BLK), 0)
    ki = lax.broadcasted_iota(jnp.int32, (DIL_BLK, 2 * DIL_BLK), 1)
    dist = qi + DIL_BLK - ki
    return (dist >= 0) & (dist <= DIL_BLK) & ((n > 0) | (ki >= DIL_BLK))


def _dil_fwd(qkv, d):
    length = SEQ // d
    nb = length // DIL_BLK
    qkv_v = qkv.reshape(length, d * 3 * ATT_W)

    def body(q_ref, kp_ref, kc_ref, vp_ref, vc_ref, o_ref, lse_ref):
        n = pl.program_id(1)
        ok = _dil_valid(n)
        lane = lax.broadcasted_iota(jnp.int32, (DIL_BLK, LANE), 1)
        lse_all = jnp.zeros((DIL_BLK, LANE), F32)
        heads = range(N_HEADS)
        scores = [lax.dot_general(_head(q_ref, h), jnp.concatenate([_head(kp_ref, h), _head(kc_ref, h)], axis=0),
                                  NT_DIMS, preferred_element_type=F32) for h in heads]
        probs, inv_l = [], []
        for h in heads:
            s = jnp.where(ok, scores[h], NEG_INF)
            m = jnp.max(s, axis=-1, keepdims=True)
            p = jnp.exp(s - m)
            l = jnp.sum(p, axis=-1, keepdims=True)
            probs.append(p.astype(BF16))
            inv_l.append(1.0 / l)
            lse_all = jnp.where(lane == h, m + jnp.log(l), lse_all)
        outs = [jnp.dot(probs[h], jnp.concatenate([_head(vp_ref, h), _head(vc_ref, h)], axis=0),
                        preferred_element_type=F32) * inv_l[h] for h in heads]
        o_ref[...] = jnp.concatenate(outs, axis=1).astype(o_ref.dtype)
        lse_ref[...] = lse_all

    blk = lambda f: pl.BlockSpec((DIL_BLK, ATT_W), f)
    prev = lambda n: jnp.maximum(n - 1, 0)
    o, lse = pl.pallas_call(
        body, name=f"dil_fwd_d{d}", grid=(d, nb),
        in_specs=[blk(lambda r, n: (n, 3 * r)),
                  blk(lambda r, n: (prev(n), 3 * r + 1)), blk(lambda r, n: (n, 3 * r + 1)),
                  blk(lambda r, n: (prev(n), 3 * r + 2)), blk(lambda r, n: (n, 3 * r + 2))],
        out_specs=[blk(lambda r, n: (n, r)), pl.BlockSpec((DIL_BLK, LANE), lambda r, n: (n, r))],
        out_shape=[_sds((length, d * ATT_W), BF16), _sds((length, d * LANE), F32)],
        compiler_params=_params("parallel", "arbitrary"),
    )(qkv_v, qkv_v, qkv_v, qkv_v, qkv_v)
    return o.reshape(SEQ, ATT_W), lse.reshape(SEQ, LANE)


def _dil_merge(os_, lses, *, tm=512):
    def body(o0, o1, o2, l0, l1, l2, y_ref, lse_ref):
        ls = [l0[...], l1[...], l2[...]]
        m = jnp.maximum(jnp.maximum(ls[0], ls[1]), ls[2])
        es = [jnp.exp(l - m) for l in ls]
        tot = es[0] + es[1] + es[2]
        lse_ref[...] = m + jnp.log(tot)
        alphas = [e / tot for e in es]
        outs = []
        for h in range(N_HEADS):
            acc = None
            for g, o_ref in enumerate((o0, o1, o2)):
                term = alphas[g][:, h:h + 1] * _head(o_ref, h).astype(F32)
                acc = term if acc is None else acc + term
            outs.append(acc)
        y_ref[...] = jnp.concatenate(outs, axis=1).astype(y_ref.dtype)

    row = pl.BlockSpec((tm, ATT_W), lambda i: (i, 0))
    vec = pl.BlockSpec((tm, LANE), lambda i: (i, 0))
    return pl.pallas_call(
        body, name="dil_merge", grid=(SEQ // tm,),
        in_specs=[row] * 3 + [vec] * 3, out_specs=[row, vec],
        out_shape=[_sds((SEQ, ATT_W), BF16), _sds((SEQ, LANE), F32)],
        compiler_params=_params("parallel"),
    )(*os_, *lses)


def _dil_bwd(qkv, lse, delta, do, d):
    length = SEQ // d
    nb = length // DIL_BLK
    qkv_v = qkv.reshape(length, d * 3 * ATT_W)
    lse_v, dl_v, do_v = lse.reshape(length, d * LANE), delta.reshape(length, d * LANE), do.reshape(length, d * ATT_W)

    def body(q_ref, kp_ref, kc_ref, vp_ref, vc_ref, lse_ref, dl_ref, do_ref,
             dq_ref, dk_ref, dv_ref, ck_s, cv_s):
        n = pl.program_id(1)

        @pl.when(n == 0)
        def _():
            ck_s[...] = jnp.zeros_like(ck_s)
            cv_s[...] = jnp.zeros_like(cv_s)

        @pl.when(n < nb)
        def _():
            ok = _dil_valid(n)
            heads = range(N_HEADS)
            kks = [jnp.concatenate([_head(kp_ref, h), _head(kc_ref, h)], axis=0) for h in heads]
            scores = [lax.dot_general(_head(q_ref, h), kks[h], NT_DIMS, preferred_element_type=F32) for h in heads]
            dps = [lax.dot_general(_head(do_ref, h), jnp.concatenate([_head(vp_ref, h), _head(vc_ref, h)], axis=0),
                                   NT_DIMS, preferred_element_type=F32) for h in heads]
            ps, dss = [], []
            for h in heads:
                p = jnp.where(ok, jnp.exp(scores[h] - lse_ref[:, h:h + 1]), 0.0)
                ps.append(p.astype(BF16))
                dss.append((p * (dps[h] - dl_ref[:, h:h + 1])).astype(BF16))
            dqs = [jnp.dot(dss[h], kks[h], preferred_element_type=F32) * SCALE for h in heads]
            dkks = [lax.dot_general(dss[h], _head(q_ref, h), TN_DIMS, preferred_element_type=F32) for h in heads]
            dvvs = [lax.dot_general(ps[h], _head(do_ref, h), TN_DIMS, preferred_element_type=F32) for h in heads]
            dq_ref[...] = jnp.concatenate(dqs, axis=1).astype(dq_ref.dtype)
            dk_ref[...] = (ck_s[...] + jnp.concatenate([t[:DIL_BLK] for t in dkks], axis=1)).astype(dk_ref.dtype)
            dv_ref[...] = (cv_s[...] + jnp.concatenate([t[:DIL_BLK] for t in dvvs], axis=1)).astype(dv_ref.dtype)
            ck_s[...] = jnp.concatenate([t[DIL_BLK:] for t in dkks], axis=1)
            cv_s[...] = jnp.concatenate([t[DIL_BLK:] for t in dvvs], axis=1)

        @pl.when(n == nb)
        def _():
            dk_ref[...] = ck_s[...].astype(dk_ref.dtype)
            dv_ref[...] = cv_s[...].astype(dv_ref.dtype)

    blk = lambda f: pl.BlockSpec((DIL_BLK, ATT_W), f)
    vec = lambda f: pl.BlockSpec((DIL_BLK, LANE), f)
    cur = lambda n: jnp.minimum(n, nb - 1)
    prev = lambda n: jnp.maximum(cur(n) - 1, 0)
    back = lambda n: jnp.maximum(n - 1, 0)
    outs = pl.pallas_call(
        body, name=f"dil_bwd_d{d}", grid=(d, nb + 1),
        in_specs=[blk(lambda r, n: (cur(n), 3 * r)),
                  blk(lambda r, n: (prev(n), 3 * r + 1)), blk(lambda r, n: (cur(n), 3 * r + 1)),
                  blk(lambda r, n: (prev(n), 3 * r + 2)), blk(lambda r, n: (cur(n), 3 * r + 2)),
                  vec(lambda r, n: (cur(n), r)), vec(lambda r, n: (cur(n), r)),
                  blk(lambda r, n: (cur(n), r))],
        out_specs=[blk(lambda r, n: (cur(n), r)), blk(lambda r, n: (back(n), r)), blk(lambda r, n: (back(n), r))],
        out_shape=[_sds((length, d * ATT_W), BF16)] * 3,
        scratch_shapes=[pltpu.VMEM((DIL_BLK, ATT_W), F32), pltpu.VMEM((DIL_BLK, ATT_W), F32)],
        compiler_params=_params("arbitrary", "arbitrary"),
    )(qkv_v, qkv_v, qkv_v, qkv_v, qkv_v, lse_v, dl_v, do_v)
    return [t.reshape(SEQ, ATT_W) for t in outs]


def _sigmoid(x):
    return 1.0 / (1.0 + jnp.exp(-x))


def _mix_fwd(ya, yb, w_oa, w_ob, zm, *, tm=512):
    def body(ya_ref, yb_ref, wa_ref, wb_ref, ga_ref, gb_ref, pa_ref, pb_ref, mix_ref):
        pa = jnp.dot(ya_ref[...], wa_ref[...], preferred_element_type=F32)
        pb = jnp.dot(yb_ref[...], wb_ref[...], preferred_element_type=F32)
        pa_ref[...] = pa.astype(pa_ref.dtype)
        pb_ref[...] = pb.astype(pb_ref.dtype)
        mix_ref[...] = (_sigmoid(ga_ref[...].astype(F32)) * pa + _sigmoid(gb_ref[...].astype(F32)) * pb
                        ).astype(mix_ref.dtype)

    row = pl.BlockSpec((tm, ATT_W), lambda i: (i, 0))
    wsp = pl.BlockSpec((ATT_W, D_MODEL), lambda i: (0, 0))
    wide = pl.BlockSpec((tm, D_MODEL), lambda i: (i, 0))
    return pl.pallas_call(
        body, name="mix_fwd", grid=(SEQ // tm,),
        in_specs=[row, row, wsp, wsp, pl.BlockSpec((tm, D_MODEL), lambda i: (i, 3)),
                  pl.BlockSpec((tm, D_MODEL), lambda i: (i, 4))],
        out_specs=[wide] * 3, out_shape=[_sds((SEQ, D_MODEL), BF16)] * 3,
        compiler_params=_params("parallel"),
    )(ya, yb, w_oa, w_ob, zm, zm)


def _gate_bwd(dmix, zm, pa, pb, *, tm=512):
    def body(dm_ref, ga_ref, gb_ref, pa_ref, pb_ref, dpa_ref, dpb_ref, dg_ref):
        dm = dm_ref[...].astype(F32)
        sa, sb = _sigmoid(ga_ref[...].astype(F32)), _sigmoid(gb_ref[...].astype(F32))
        dpa_ref[...] = (dm * sa).astype(dpa_ref.dtype)
        dpb_ref[...] = (dm * sb).astype(dpb_ref.dtype)
        dg_ref[:, :D_MODEL] = (dm * pa_ref[...].astype(F32) * sa * (1.0 - sa)).astype(dg_ref.dtype)
        dg_ref[:, D_MODEL:] = (dm * pb_ref[...].astype(F32) * sb * (1.0 - sb)).astype(dg_ref.dtype)

    wide = pl.BlockSpec((tm, D_MODEL), lambda i: (i, 0))
    return pl.pallas_call(
        body, name="gate_bwd", grid=(SEQ // tm,),
        in_specs=[wide, pl.BlockSpec((tm, D_MODEL), lambda i: (i, 3)), pl.BlockSpec((tm, D_MODEL), lambda i: (i, 4)),
                  wide, wide],
        out_specs=[wide, wide, pl.BlockSpec((tm, 2 * D_MODEL), lambda i: (i, 0))],
        out_shape=[_sds((SEQ, D_MODEL), BF16), _sds((SEQ, D_MODEL), BF16), _sds((SEQ, 2 * D_MODEL), BF16)],
        compiler_params=_params("parallel"),
    )(dmix, zm, zm, pa, pb)


def _out_fwd(mixed, w_out, x, g_post, g_pre, *, tm=512):
    def body(m_ref, w_ref, x_ref, gp_ref, gn_ref, y_ref, x2_ref, h_ref):
        y = jnp.dot(m_ref[...], w_ref[...], preferred_element_type=F32)
        y_ref[...] = y
        r = lax.rsqrt(jnp.mean(y * y, axis=-1, keepdims=True) + RMS_EPS)
        x2 = x_ref[...] + y * r * gp_ref[...]
        x2_ref[...] = x2
        r2 = lax.rsqrt(jnp.mean(x2 * x2, axis=-1, keepdims=True) + RMS_EPS)
        h_ref[...] = (x2 * r2 * gn_ref[...]).astype(h_ref.dtype)

    row = pl.BlockSpec((tm, D_MODEL), lambda i: (i, 0))
    vec = pl.BlockSpec((1, D_MODEL), lambda i: (0, 0))
    return pl.pallas_call(
        body, name="out_fwd", grid=(SEQ // tm,),
        in_specs=[row, pl.BlockSpec((D_MODEL, D_MODEL), lambda i: (0, 0)), row, vec, vec],
        out_specs=[row] * 3,
        out_shape=[_sds((SEQ, D_MODEL), F32), _sds((SEQ, D_MODEL), F32), _sds((SEQ, D_MODEL), BF16)],
        compiler_params=_params("parallel"),
    )(mixed, w_out, x, g_post, g_pre)


FFN_TM = 512
FFN_TN = 256
FFN_NJ = D_FF // FFN_TN
FFN_GROUP = 2 * SUBLANE


def _gelu_parts(a):
    c = math.sqrt(2.0 / math.pi)
    t = jnp.tanh(c * (a + 0.044715 * a * a * a))
    gelu = 0.5 * a * (1.0 + t)
    dgelu = 0.5 * (1.0 + t) + 0.5 * a * (1.0 - t * t) * c * (1.0 + 3.0 * 0.044715 * a * a)
    return gelu, dgelu


def _shift_down(cur, above, k):
    row = lax.broadcasted_iota(jnp.int32, cur.shape, 0)
    return jnp.where(row < k, pltpu.roll(above, k, 0), pltpu.roll(cur, k, 0))


def _shift_up(cur, below, k):
    row = lax.broadcasted_iota(jnp.int32, cur.shape, 0)
    return jnp.where(row >= SUBLANE - k, pltpu.roll(below, SUBLANE - k, 0), pltpu.roll(cur, SUBLANE - k, 0))


def _conv_consts(w_ref, b_ref):
    shape = (SUBLANE, FFN_TN)
    return [jnp.broadcast_to(w_ref[k:k + 1, :], shape) for k in range(3)] + [jnp.broadcast_to(b_ref[...], shape)]


def _conv_taps(cur, above, consts):
    w0, w1, w2, bias = consts
    s1, s2 = _shift_down(cur, above, 1), _shift_down(cur, above, 2)
    return w0 * s2 + w1 * s1 + w2 * cur + bias, s1, s2


def _ffn_specs(rev):
    nrow = SEQ // FFN_TM
    per = FFN_TM // SUBLANE
    ri = (lambda i: nrow - 1 - i) if rev else (lambda i: i)
    main = lambda off: pl.BlockSpec((FFN_TM, FFN_TN), lambda j, i: (ri(i), j + off))
    halo = lambda off: pl.BlockSpec((SUBLANE, FFN_TN), lambda j, i: (jnp.maximum(ri(i) * per - 1, 0), j + off))
    wsp = lambda off: pl.BlockSpec((3, FFN_TN), lambda j, i: (0, j + off))
    bsp = lambda off: pl.BlockSpec((1, FFN_TN), lambda j, i: (0, j + off))
    return ri, main, halo, wsp, bsp


def _ffn_mid_fwd(u, conv_w, conv_b):
    ri, main, halo, wsp, bsp = _ffn_specs(False)

    def body(ua_ref, ub_ref, ha_ref, hb_ref, wa_ref, wb_ref, ba_ref, bb_ref, m_ref):
        live = (pl.program_id(1) > 0).astype(F32)
        ca, cb = _conv_consts(wa_ref, ba_ref), _conv_consts(wb_ref, bb_ref)

        def group(g, carry):
            above_a, above_b = carry
            rows = pl.ds(pl.multiple_of(g * FFN_GROUP, FFN_GROUP), FFN_GROUP)
            xa, xb = ua_ref[rows, :].astype(F32), ub_ref[rows, :].astype(F32)
            outs = []
            for c in range(2):
                cur_a, cur_b = xa[c * SUBLANE:(c + 1) * SUBLANE], xb[c * SUBLANE:(c + 1) * SUBLANE]
                a = _conv_taps(cur_a, above_a, ca)[0]
                b = _conv_taps(cur_b, above_b, cb)[0]
                outs.append(_gelu_parts(a)[0] * b)
                above_a, above_b = cur_a, cur_b
            m_ref[rows, :] = jnp.concatenate(outs, axis=0).astype(m_ref.dtype)
            return above_a, above_b

        lax.fori_loop(0, FFN_TM // FFN_GROUP, group,
                      (ha_ref[...].astype(F32) * live, hb_ref[...].astype(F32) * live))

    return pl.pallas_call(
        body, name="ffn_mid_fwd", grid=(FFN_NJ, SEQ // FFN_TM),
        in_specs=[main(0), main(FFN_NJ), halo(0), halo(FFN_NJ), wsp(0), wsp(FFN_NJ), bsp(0), bsp(FFN_NJ)],
        out_specs=pl.BlockSpec((FFN_TM, FFN_TN), lambda j, i: (i, j)),
        out_shape=_sds((SEQ, D_FF), BF16),
        compiler_params=_params("parallel", "arbitrary"),
    )(u, u, u, u, conv_w, conv_w, conv_b, conv_b)


def _ffn_mid_bwd(dm, u, conv_w, conv_b):
    ri, main, halo, wsp, bsp = _ffn_specs(True)
    nrow = SEQ // FFN_TM

    def body(dm_ref, ua_ref, ub_ref, ha_ref, hb_ref, wa_ref, wb_ref, ba_ref, bb_ref,
             dua_ref, dub_ref, gwa_ref, gwb_ref, gba_ref, gbb_ref, ca_s, cb_s):
        i = pl.program_id(1)
        live = (i < nrow - 1).astype(F32)

        @pl.when(i == 0)
        def _():
            ca_s[...] = jnp.zeros_like(ca_s)
            cb_s[...] = jnp.zeros_like(cb_s)
            for r in (gwa_ref, gwb_ref, gba_ref, gbb_ref):
                r[...] = jnp.zeros_like(r)

        consts_a, consts_b = _conv_consts(wa_ref, ba_ref), _conv_consts(wb_ref, bb_ref)
        halo_a, halo_b = ha_ref[...].astype(F32) * live, hb_ref[...].astype(F32) * live
        n_groups = FFN_TM // FFN_GROUP

        def load(g):
            rows = pl.ds(pl.multiple_of(g * FFN_GROUP, FFN_GROUP), FFN_GROUP)
            return ua_ref[rows, :].astype(F32), ub_ref[rows, :].astype(F32)

        def group(t, carry):
            xa, xb, below_a, below_b, acc = carry
            g = n_groups - 1 - t
            rows = pl.ds(pl.multiple_of(g * FFN_GROUP, FFN_GROUP), FFN_GROUP)
            ya, yb = load(jnp.maximum(g - 1, 0))
            top_a = jnp.where(g > 0, ya[SUBLANE:], halo_a)
            top_b = jnp.where(g > 0, yb[SUBLANE:], halo_b)
            dmv = dm_ref[rows, :].astype(F32)
            acc = list(acc)
            pre_a, pre_b = [None, None], [None, None]
            for c in (1, 0):
                sl = slice(c * SUBLANE, (c + 1) * SUBLANE)
                cur_a, cur_b = xa[sl], xb[sl]
                above_a, above_b = (xa[:SUBLANE], xb[:SUBLANE]) if c == 1 else (top_a, top_b)
                a, a1, a2 = _conv_taps(cur_a, above_a, consts_a)
                b, b1, b2 = _conv_taps(cur_b, above_b, consts_b)
                gelu, dgelu = _gelu_parts(a)
                du_a, du_b = dmv[sl] * b * dgelu, dmv[sl] * gelu
                for base, du, taps in ((0, du_a, (a2, a1, cur_a)), (4, du_b, (b2, b1, cur_b))):
                    for k in range(3):
                        acc[base + k] = acc[base + k] + du * taps[k]
                    acc[base + 3] = acc[base + 3] + du
                pre_a[c] = (consts_a[2] * du_a + consts_a[1] * _shift_up(du_a, below_a, 1)
                            + consts_a[0] * _shift_up(du_a, below_a, 2))
                pre_b[c] = (consts_b[2] * du_b + consts_b[1] * _shift_up(du_b, below_b, 1)
                            + consts_b[0] * _shift_up(du_b, below_b, 2))
                below_a, below_b = du_a, du_b
            dua_ref[rows, :] = jnp.concatenate(pre_a, axis=0).astype(dua_ref.dtype)
            dub_ref[rows, :] = jnp.concatenate(pre_b, axis=0).astype(dub_ref.dtype)
            return ya, yb, below_a, below_b, tuple(acc)

        zeros = jnp.zeros((SUBLANE, FFN_TN), F32)
        xa0, xb0 = load(n_groups - 1)
        _, _, below_a, below_b, acc = lax.fori_loop(
            0, n_groups, group, (xa0, xb0, ca_s[...], cb_s[...], (zeros,) * 8))
        ca_s[...] = below_a
        cb_s[...] = below_b
        for base, gw_ref, gb_ref in ((0, gwa_ref, gba_ref), (4, gwb_ref, gbb_ref)):
            for k in range(3):
                gw_ref[k:k + 1, :] += jnp.sum(acc[base + k], axis=0, keepdims=True)
            gb_ref[...] += jnp.sum(acc[base + 3], axis=0, keepdims=True)

    acc3 = pl.BlockSpec((3, FFN_TN), lambda j, i: (0, j))
    acc1 = pl.BlockSpec((1, FFN_TN), lambda j, i: (0, j))
    out_blk = pl.BlockSpec((FFN_TM, FFN_TN), lambda j, i: (ri(i), j))
    return pl.pallas_call(
        body, name="ffn_mid_bwd", grid=(FFN_NJ, nrow),
        in_specs=[pl.BlockSpec((FFN_TM, FFN_TN), lambda j, i: (ri(i), j)),
                  main(0), main(FFN_NJ), halo(0), halo(FFN_NJ), wsp(0), wsp(FFN_NJ), bsp(0), bsp(FFN_NJ)],
        out_specs=[out_blk, out_blk, acc3, acc3, acc1, acc1],
        out_shape=[_sds((SEQ, D_FF), BF16), _sds((SEQ, D_FF), BF16), _sds((3, D_FF), F32), _sds((3, D_FF), F32),
                   _sds((1, D_FF), F32), _sds((1, D_FF), F32)],
        scratch_shapes=[pltpu.VMEM((SUBLANE, FFN_TN), F32), pltpu.VMEM((SUBLANE, FFN_TN), F32)],
        compiler_params=_params("parallel", "arbitrary"),
    )(dm, u, u, u, u, conv_w, conv_w, conv_b, conv_b)


def _down_fwd(m, w_down, x2, g_post, target, *, tm=512):
    def body(m_ref, w_ref, x2_ref, g_ref, t_ref, dout_ref, dy_ref, gg_ref, loss_ref):
        @pl.when(pl.program_id(0) == 0)
        def _():
            gg_ref[...] = jnp.zeros_like(gg_ref)
            loss_ref[...] = jnp.zeros_like(loss_ref)

        y = jnp.dot(m_ref[...], w_ref[...], preferred_element_type=F32)
        r = lax.rsqrt(jnp.mean(y * y, axis=-1, keepdims=True) + RMS_EPS)
        yn = y * r
        diff = (x2_ref[...] + yn * g_ref[...]) - t_ref[...]
        loss_ref[...] += jnp.sum(diff * diff)
        dout = diff * (1.0 / D_MODEL)
        dout_ref[...] = dout
        gg_ref[...] += jnp.sum(dout * yn, axis=0, keepdims=True)
        dn = dout * g_ref[...]
        dy_ref[...] = (r * (dn - yn * jnp.mean(dn * yn, axis=-1, keepdims=True))).astype(dy_ref.dtype)

    row = pl.BlockSpec((tm, D_MODEL), lambda i: (i, 0))
    vec = pl.BlockSpec((1, D_MODEL), lambda i: (0, 0))
    return pl.pallas_call(
        body, name="down_fwd", grid=(SEQ // tm,),
        in_specs=[pl.BlockSpec((tm, D_FF), lambda i: (i, 0)), pl.BlockSpec((D_FF, D_MODEL), lambda i: (0, 0)),
                  row, vec, row],
        out_specs=[row, row, vec, pl.BlockSpec((1, LANE), lambda i: (0, 0))],
        out_shape=[_sds((SEQ, D_MODEL), F32), _sds((SEQ, D_MODEL), BF16), _sds((1, D_MODEL), F32),
                   _sds((1, LANE), F32)],
        compiler_params=_params("arbitrary"),
    )(m, w_down, x2, g_post, target)


def _local_step(x, target, w_main, w_f, b_forget, conv_b, g_pre_mix, g_post_mix, g_pre_ffn, g_post_ffn,
                late_weights, ffn_grads_ready, proj_grads_ready, mixer_grads_ready):
    mm = _matmul
    tabs = _rope_tables()

    h1 = _rms_fwd(x, g_pre_mix, name="rms_pre_mix")
    zm = mm(h1, w_main, out_dtype=BF16, tm=2048, tn=512, tk=1024, name="in_proj")
    zf = mm(h1, w_f, out_dtype=F32, tm=2048, tn=F_PAD, tk=1024, name="in_proj_forget")
    f_row, sg_row = _fox_prep(zf[:, :N_HEADS].T, b_forget.reshape(N_HEADS, 1))
    f_cols = jnp.pad(f_row.T, ((0, 0), (0, LANE - N_HEADS)))
    q_slots, k_slots, v_slots = _fox_pack_fwd(zm, f_cols)
    ya, lse_a = _fox_fwd(q_slots, k_slots, v_slots)
    qkv_d = _rope_fwd(zm, tabs)
    dil = [_dil_fwd(qkv_d, d) for _, d in DIL_PATTERNS]
    yb, lse_b = _dil_merge([o for o, _ in dil], [l for _, l in dil])
    w_oa, w_ob, w_out, w_up, conv_w, w_down = late_weights(yb)
    pa, pb, mixed = _mix_fwd(ya, yb, w_oa, w_ob, zm)
    y1, x2, h2 = _out_fwd(mixed, w_out, x, g_post_mix, g_pre_ffn)
    u = mm(h2, w_up, out_dtype=BF16, tm=2048, tn=512, tk=1024, name="up_proj")
    m = _ffn_mid_fwd(u, conv_w, conv_b)
    dout, dy2, gg_post_ffn, sq_err = _down_fwd(m, w_down, x2, g_post_ffn, target)

    g_w_down = mm(m, dy2, ta=True, out_dtype=F32, tm=D_FF // 2, tn=1024, tk=512, name="grad_w_down")
    dm = mm(dy2, w_down, tb=True, out_dtype=BF16, tm=2048, tn=D_FF // 2, tk=1024, name="d_ffn_mid")
    du_a, du_b, gcw_a, gcw_b, gcb_a, gcb_b = _ffn_mid_bwd(dm, u, conv_w, conv_b)
    g_w_up = [mm(h2, t, ta=True, out_dtype=F32, tm=1024, tn=D_FF // 2, tk=512, name=f"grad_w_up_{s}")
              for s, t in (("a", du_a), ("b", du_b))]
    dh2 = [mm(t, w_up[:, o:o + D_FF], tb=True, out_dtype=F32, tm=1024, tn=1024, tk=D_FF, name=f"d_h2_{s}")
           for s, t, o in (("a", du_a, 0), ("b", du_b, D_FF))]
    dx2, gg_pre_ffn = _rms_bwd(dh2, x2, g_pre_ffn, dout, out_dtype=F32, name="rms_pre_ffn_bwd")
    tok = ffn_grads_ready(dict(w_down=g_w_down, w_up=jnp.concatenate(g_w_up, axis=1),
                               conv_w=jnp.concatenate([gcw_a, gcw_b], axis=1)))

    dy1, gg_post_mix = _rms_bwd([dx2], y1, g_post_mix + tok, None, out_dtype=BF16, name="rms_post_mix_bwd")
    g_w_out = mm(mixed, dy1, ta=True, out_dtype=F32, tm=1024, tn=1024, tk=512, name="grad_w_out")
    dmix = mm(dy1, w_out, tb=True, out_dtype=BF16, tm=2048, tn=1024, tk=1024, name="d_mixed")
    dpa, dpb, dgates = _gate_bwd(dmix, zm, pa, pb)
    g_w_oa = mm(ya, dpa, ta=True, out_dtype=F32, tm=512, tn=1024, tk=1024, name="grad_w_o_fox")
    g_w_ob = mm(yb, dpb, ta=True, out_dtype=F32, tm=512, tn=1024, tk=1024, name="grad_w_o_dil")
    tok = proj_grads_ready(dict(w_o_fox=g_w_oa, w_o_dil=g_w_ob, w_out=g_w_out))
    dya = mm(dpa, w_oa, tb=True, out_dtype=BF16, tm=2048, tn=512, tk=1024, name="d_y_fox")
    dyb = mm(dpb, w_ob, tb=True, out_dtype=BF16, tm=2048, tn=512, tk=1024, name="d_y_dil")

    qb_slots, do_slots = _fox_pack_bwd(zm, f_cols + tok, lse_a, ya, dya)
    dq_s, dk_s, dv_s = [t.reshape(SEQ, N_HEADS, SLOT) for t in _fox_bwd(qb_slots, k_slots, v_slots, do_slots)]
    data = lambda t: t[:, :, :HEAD_DIM].reshape(SEQ, ATT_W)
    dq_a, dk_a, dv_a = data(dq_s) * SCALE, data(dk_s), data(dv_s)
    dfa_t, g_b_forget = _fox_post_bwd(dq_s[:, :, HEAD_DIM].T, -dk_s[:, :, HEAD_DIM + N_SPLIT].T, sg_row)

    delta_b = _attn_delta(yb, dyb, name="delta_dil")
    dil_g = [_dil_bwd(qkv_d, lse_b, delta_b, dyb, d) for _, d in DIL_PATTERNS]
    d_dil = _dil_grad_combine([g[0] for g in dil_g], [g[1] for g in dil_g], [g[2] for g in dil_g], tabs)

    dz = jnp.concatenate([dq_a.astype(BF16), dk_a.astype(BF16), dv_a.astype(BF16), d_dil, dgates], axis=1)
    dzf = jnp.pad(dfa_t.T, ((0, 0), (0, F_PAD - N_HEADS)))
    g_w_main = mm(h1, dz, ta=True, out_dtype=F32, tm=1024, tn=Z_MAIN // 4, tk=512, name="grad_w_in")
    g_w_f = mm(h1, dzf, ta=True, out_dtype=F32, tm=1024, tn=F_PAD, tk=1024, name="grad_w_in_forget")
    tok = mixer_grads_ready(dict(w_main=g_w_main, w_f=g_w_f))
    dh1 = [mm(dz, w_main, tb=True, out_dtype=F32, tm=2048, tn=1024, tk=1024, name="d_h1"),
           mm(dzf + tok, w_f, tb=True, out_dtype=F32, tm=2048, tn=1024, tk=F_PAD, name="d_h1_forget")]
    grad_x, gg_pre_mix = _rms_bwd(dh1, x, g_pre_mix, dx2, out_dtype=F32, name="rms_pre_mix_bwd")

    grads = dict(
        b_forget=g_b_forget.reshape(1, N_HEADS), conv_b=jnp.concatenate([gcb_a, gcb_b], axis=1),
        g_pre_mix=gg_pre_mix, g_post_mix=gg_post_mix, g_pre_ffn=gg_pre_ffn, g_post_ffn=gg_post_ffn)
    return sq_err[0, 0], grad_x, grads


def _exchange(arrays, scatter, *, name):
    n = len(arrays)

    def body(*refs):
        ins, outs = refs[:n], refs[n:2 * n]
        send_sems, recv_sems, local_sems = refs[2 * n:]
        x, y, c = lax.axis_index("x"), lax.axis_index("y"), lax.axis_index("c")
        me = 4 * x + 2 * y + c
        peers = []
        for k in range(1, N_DEV):
            px = 1 - x if k & 4 else x
            py = 1 - y if k & 2 else y
            pc = 1 - c if k & 1 else c
            peers.append(((px, py, pc), 4 * px + 2 * py + pc))

        def remote(a, k):
            dev, slot = peers[k]
            return pltpu.make_async_remote_copy(
                src_ref=ins[a].at[slot] if scatter else ins[a], dst_ref=outs[a].at[me],
                send_sem=send_sems.at[a, k], recv_sem=recv_sems.at[a, k],
                device_id=dev, device_id_type=MESH_ID)

        def landed(a, k):
            dev, slot = peers[k]
            return pltpu.make_async_remote_copy(
                src_ref=outs[a].at[slot], dst_ref=outs[a].at[slot],
                send_sem=send_sems.at[a, k], recv_sem=recv_sems.at[a, k],
                device_id=dev, device_id_type=MESH_ID)

        own = [pltpu.make_async_copy(ins[a].at[me] if scatter else ins[a], outs[a].at[me], local_sems.at[a])
               for a in range(n)]
        copies = [remote(a, k) for k in range(N_DEV - 1) for a in range(n)]
        for cp in own + copies:
            cp.start()
        for k in range(N_DEV - 1):
            for a in range(n):
                landed(a, k).wait_recv()
        for cp in copies:
            cp.wait_send()
        for cp in own:
            cp.wait()

    out_shape = [_sds(((N_DEV,) + a.shape[-2:]), a.dtype) for a in arrays]
    return pl.pallas_call(
        body, name=name, in_specs=[ANY] * n, out_specs=[ANY] * n, out_shape=out_shape,
        scratch_shapes=[pltpu.SemaphoreType.DMA((n, N_DEV - 1)), pltpu.SemaphoreType.DMA((n, N_DEV - 1)),
                        pltpu.SemaphoreType.DMA((n,))],
    )(*arrays)


def _peers():
    x, y, c = lax.axis_index("x"), lax.axis_index("y"), lax.axis_index("c")
    out = []
    for k in range(1, N_DEV):
        px = 1 - x if k & 4 else x
        py = 1 - y if k & 2 else y
        pc = 1 - c if k & 1 else c
        out.append(((px, py, pc), 4 * px + 2 * py + pc))
    return 4 * x + 2 * y + c, out


HBM = pl.BlockSpec(memory_space=pltpu.HBM)
SEM = pl.BlockSpec(memory_space=pltpu.SEMAPHORE)
DATAFLOW = pltpu.SideEffectType.DATAFLOW_SIDE_EFFECTING


def _split_copy(srcs, lands, send_sems, recv_sems, scatter, a, k, me, peers, incoming=False):
    dev, slot = peers[k]
    if incoming:
        src = dst = lands[a].at[slot]
    else:
        src, dst = (srcs[a].at[slot] if scatter else srcs[a]), lands[a].at[me]
    return pltpu.make_async_remote_copy(
        src_ref=src, dst_ref=dst, send_sem=send_sems.at[a * (N_DEV - 1) + k], recv_sem=recv_sems.at[a * (N_DEV - 1) + k],
        device_id=dev, device_id_type=MESH_ID)


def _exchange_start(arrays, scatter, *, name):
    n = len(arrays)

    def body(*refs):
        srcs, lands = refs[:n], refs[n:2 * n]
        send_sems, recv_sems = refs[2 * n], refs[2 * n + 1]
        token = refs[-1]
        me, peers = _peers()
        for k in range(N_DEV - 1):
            for a in range(n):
                _split_copy(srcs, lands, send_sems, recv_sems, scatter, a, k, me, peers).start()
        token[...] = jnp.zeros_like(token)

    land_shapes = [((N_DEV,) + a.shape[-2:], a.dtype) for a in arrays]
    sems = pltpu.SemaphoreType.DMA((n * (N_DEV - 1),))
    outs = pl.pallas_call(
        body, name=name,
        out_shape=(sems, sems, *[pltpu.HBM(a.shape, a.dtype) for a in arrays],
                   *[pltpu.HBM(s, d) for s, d in land_shapes], _sds((SUBLANE, LANE), F32)),
        in_specs=[HBM] * (2 * n),
        out_specs=(SEM, SEM, *[HBM] * (2 * n), pl.BlockSpec(memory_space=pltpu.VMEM)),
        input_output_aliases={i: 2 + i for i in range(2 * n)},
        compiler_params=pltpu.CompilerParams(has_side_effects=DATAFLOW),
    )(*[pltpu.with_memory_space_constraint(a, pltpu.HBM) for a in arrays],
      *[pltpu.with_memory_space_constraint(lax.empty(s, d), pltpu.HBM) for s, d in land_shapes])
    return (outs[0], outs[1], outs[2:2 + n], outs[2 + n:2 + 2 * n], scatter), outs[-1]


def _exchange_wait(handles, after, *, name):
    send_sems, recv_sems, srcs, lands, scatter = handles
    n = len(srcs)

    def body(*refs):
        src_refs, land_refs = refs[:n], refs[n:2 * n]
        send_ref, recv_ref = refs[2 * n], refs[2 * n + 1]
        me, peers = _peers()
        for k in range(N_DEV - 1):
            for a in range(n):
                _split_copy(src_refs, land_refs, send_ref, recv_ref, scatter, a, k, me, peers).wait_send()
                _split_copy(src_refs, land_refs, send_ref, recv_ref, scatter, a, k, me, peers, True).wait_recv()

    outs = pl.pallas_call(
        body, name=name,
        out_shape=tuple(pltpu.HBM(t.shape, t.dtype) for t in (*srcs, *lands)),
        in_specs=[HBM] * (2 * n) + [SEM, SEM, pl.BlockSpec(memory_space=pl.ANY)],
        out_specs=tuple([HBM] * (2 * n)),
        input_output_aliases={i: i for i in range(2 * n)},
        compiler_params=pltpu.CompilerParams(has_side_effects=DATAFLOW),
    )(*srcs, *lands, send_sems, recv_sems, after)
    return _with_own_slot(outs[n:], outs[:n], scatter)


def _with_own_slot(landed, own, scatter):
    me = 4 * lax.axis_index("x") + 2 * lax.axis_index("y") + lax.axis_index("c")
    out = []
    for buf, src in zip(landed, own):
        mine = lax.dynamic_index_in_dim(src, me, 0, keepdims=False) if scatter else src
        out.append(lax.dynamic_update_index_in_dim(buf, mine, me, 0))
    return out


def _adamw(parts, w, m, v, *, name, tm):
    r, c = w.shape
    assert r % tm == 0

    def body(p_ref, w_ref, m_ref, v_ref, g_ref, d_ref, nm_ref, nv_ref):
        g = p_ref[0].astype(F32)
        for s in range(1, N_DEV):
            g = g + p_ref[s].astype(F32)
        g_ref[...] = g
        m_new = ADAM_B1 * m_ref[...] + (1.0 - ADAM_B1) * g
        v_new = ADAM_B2 * v_ref[...] + (1.0 - ADAM_B2) * (g * g)
        nm_ref[...] = m_new
        nv_ref[...] = v_new
        m_hat = m_new / (1.0 - ADAM_B1 ** ADAM_STEP)
        v_hat = v_new / (1.0 - ADAM_B2 ** ADAM_STEP)
        d_ref[...] = -ADAM_LR * (m_hat / (jnp.sqrt(v_hat) + ADAM_EPS) + ADAM_WD * w_ref[...])

    blk = pl.BlockSpec((tm, c), lambda i: (i, 0))
    return pl.pallas_call(
        body, name=name, grid=(r // tm,),
        in_specs=[pl.BlockSpec((N_DEV, tm, c), lambda i: (0, i, 0)), blk, blk, blk],
        out_specs=[blk] * 4, out_shape=[_sds((r, c), F32)] * 4,
        compiler_params=_params("parallel"),
    )(parts, w, m, v)


SMALL = (("g_pre_mix", D_MODEL), ("b_forget", LANE), ("g_post_mix", D_MODEL), ("g_pre_ffn", D_MODEL),
         ("conv_b", 2 * D_FF), ("g_post_ffn", D_MODEL))
SMALL_ROWS = 80


def _pack_small(vals):
    flat = [jnp.pad(vals[n].reshape(-1), (0, size - vals[n].size)) for n, size in SMALL]
    flat = jnp.concatenate(flat)
    return jnp.pad(flat, (0, SMALL_ROWS * LANE - flat.size)).reshape(SMALL_ROWS, LANE)


def _unpack_small(packed, shapes):
    flat, out, off = packed.reshape(-1), {}, 0
    for n, size in SMALL:
        cnt = math.prod(shapes[n])
        out[n] = flat[off:off + cnt].reshape(shapes[n])
        off += size
    return out


def kernel(x, g_pre_mix, w_in, b_forget, w_o_fox, w_o_dil, w_out, g_post_mix, g_pre_ffn, w_up, conv_w, conv_b, w_down, g_post_ffn, loss_target, m_g_pre_mix, m_w_in, m_b_forget, m_w_o_fox, m_w_o_dil, m_w_out, m_g_post_mix, m_g_pre_ffn, m_w_up, m_conv_w, m_conv_b, m_w_down, m_g_post_ffn, v_g_pre_mix, v_w_in, v_b_forget, v_w_o_fox, v_w_o_dil, v_w_out, v_g_post_mix, v_g_pre_ffn, v_w_up, v_conv_w, v_conv_b, v_w_down, v_g_post_ffn):
    names = ("g_pre_mix", "w_in", "b_forget", "w_o_fox", "w_o_dil", "w_out", "g_post_mix", "g_pre_ffn",
             "w_up", "conv_w", "conv_b", "w_down", "g_post_ffn")
    w = dict(g_pre_mix=g_pre_mix, w_in=w_in, b_forget=b_forget, w_o_fox=w_o_fox, w_o_dil=w_o_dil, w_out=w_out,
             g_post_mix=g_post_mix, g_pre_ffn=g_pre_ffn, w_up=w_up, conv_w=conv_w, conv_b=conv_b, w_down=w_down,
             g_post_ffn=g_post_ffn)
    m = dict(g_pre_mix=m_g_pre_mix, w_in=m_w_in, b_forget=m_b_forget, w_o_fox=m_w_o_fox, w_o_dil=m_w_o_dil,
             w_out=m_w_out, g_post_mix=m_g_post_mix, g_pre_ffn=m_g_pre_ffn, w_up=m_w_up, conv_w=m_conv_w,
             conv_b=m_conv_b, w_down=m_w_down, g_post_ffn=m_g_post_ffn)
    v = dict(g_pre_mix=v_g_pre_mix, w_in=v_w_in, b_forget=v_b_forget, w_o_fox=v_w_o_fox, w_o_dil=v_w_o_dil,
             w_out=v_w_out, g_post_mix=v_g_post_mix, g_pre_ffn=v_g_pre_ffn, w_up=v_w_up, conv_w=v_conv_w,
             conv_b=v_conv_b, w_down=v_w_down, g_post_ffn=v_g_post_ffn)
    sharded = ("w_in", "w_o_fox", "w_o_dil", "w_out", "w_up", "w_down", "conv_w")
    wire = lambda n: F32 if n == "conv_w" else BF16

    by_cols = lambda t: jnp.transpose(t, (1, 0, 2)).reshape(t.shape[1], N_DEV * t.shape[2])
    by_rows = lambda t: t.reshape(N_DEV * t.shape[1], t.shape[2])
    col_slots = lambda t: jnp.transpose(t.reshape(t.shape[0], N_DEV, t.shape[1] // N_DEV), (1, 0, 2))
    row_slots = lambda t: t.reshape(N_DEV, t.shape[0] // N_DEV, t.shape[1])
    to_slots = lambda n, t: (row_slots if n in ("w_out", "w_down") else col_slots)(t).astype(wire(n))
    shard = lambda n: w[n][0].astype(wire(n))
    f_lo, f_hi = 3 * ATT_W, 3 * ATT_W + N_HEADS

    w_in_full = by_cols(_exchange([shard("w_in")], False, name="gather_w_in")[0])
    w_main = jnp.concatenate([w_in_full[:, :f_lo], w_in_full[:, f_hi:]], axis=1)
    w_f = jnp.pad(w_in_full[:, f_lo:f_hi], ((0, 0), (0, F_PAD - N_HEADS)))
    late = ("w_o_fox", "w_o_dil", "w_out", "w_up", "conv_w", "w_down")
    order = jnp.minimum(jnp.abs(w_in_full[0, 0].astype(F32)), 0.0)
    late_handles, late_tok = _exchange_start(
        [shard(n) + order.astype(wire(n)) if n == "conv_w" else shard(n) for n in late], False,
        name="gather_late_start")

    def late_weights(after):
        got = dict(zip(late, _exchange_wait(late_handles, after, name="gather_late_wait")))
        return (by_cols(got["w_o_fox"]), by_cols(got["w_o_dil"]), by_rows(got["w_out"]), by_cols(got["w_up"]),
                by_cols(got["conv_w"]), by_rows(got["w_down"]))

    pending = {}

    def ffn_grads_ready(g):
        pending["ffn"] = _exchange_start([to_slots(n, g[n]) for n in ("w_down", "w_up", "conv_w")], True,
                                         name="scatter_ffn_start")
        return pending["ffn"][1][0, 0]

    def proj_grads_ready(g):
        pending["proj"] = _exchange_start([to_slots(n, g[n]) for n in ("w_o_fox", "w_o_dil", "w_out")], True,
                                          name="scatter_proj_start")
        return pending["proj"][1][0, 0]

    def mixer_grads_ready(g):
        g_w_in = jnp.concatenate([g["w_main"][:, :f_lo], g["w_f"][:, :N_HEADS], g["w_main"][:, f_lo:]], axis=1)
        pending["w_in"] = _exchange_start([to_slots("w_in", g_w_in)], True, name="scatter_w_in_start")
        return pending["w_in"][1][0, 0]

    sq_err, grad_x, g = _local_step(
        x[0], loss_target[0], w_main, w_f, b_forget, conv_b, g_pre_mix + late_tok[0, 0], g_post_mix, g_pre_ffn,
        g_post_ffn, late_weights, ffn_grads_ready, proj_grads_ready, mixer_grads_ready)
    loss = lax.psum(0.5 * sq_err / D_MODEL, ("x", "y", "c"))

    tiles = dict(w_in=256, w_o_fox=512, w_o_dil=512, w_out=128, w_up=256, w_down=176, conv_w=3)
    adam = lambda n, p: _adamw(p, w[n][0], m[n][0], v[n][0], name=f"adamw_{n}", tm=tiles[n])
    res = {}
    for key, group in (("ffn", ("w_down", "w_up", "conv_w")), ("proj", ("w_o_fox", "w_o_dil", "w_out"))):
        landed = _exchange_wait(pending[key][0], grad_x, name=f"scatter_{key}_wait")
        res.update({n: adam(n, p) for n, p in zip(group, landed)})
    small_parts = _exchange([_pack_small(g)], False, name="gather_small_grads")[0]
    done = res["w_up"][3]
    res["w_in"] = adam("w_in", _exchange_wait(pending["w_in"][0], done, name="scatter_w_in_wait")[0])
    small = _adamw(small_parts, _pack_small(w), _pack_small(m), _pack_small(v), name="adamw_small", tm=SMALL_ROWS)
    shapes = {n: w[n].shape for n, _ in SMALL}
    small = [_unpack_small(t, shapes) for t in small]
    out = [[(res[n][k][None] if n in sharded else small[k][n]) for n in names] for k in range(4)]
    return (loss, grad_x[None], *out[0], *out[1], *out[2], *out[3])
```

```python
import functools
import math

import jax
import jax.numpy as jnp
import numpy as np
from jax import lax
from jax.experimental import pallas as pl
from jax.experimental.pallas import tpu as pltpu

F32 = jnp.float32
BF16 = jnp.bfloat16

SEQ = 4096
D_MODEL = 1024
N_HEADS = 8
HEAD_DIM = 64
ATT_W = N_HEADS * HEAD_DIM
D_FF = 2816
Z_MAIN = 5120
F_PAD = 128
ROPE_DIM = 16
ROPE_THETA = 500000.0
RMS_EPS = 1e-6
NEG_INF = -1e30
SCALE = 1.0 / math.sqrt(HEAD_DIM)
DIL_PATTERNS = ((128, 1), (512, 4), (2048, 16))
DIL_BLK = 128
N_DEV = 8

ADAM_LR = 0.001
ADAM_B1 = 0.9
ADAM_B2 = 0.999
ADAM_EPS = 1e-08
ADAM_WD = 0.01
ADAM_STEP = 10

LANE = 128
SUBLANE = 8
VMEM_LIMIT = 56 * 1024 * 1024
MESH_ID = pl.DeviceIdType.MESH
ANY = pl.BlockSpec(memory_space=pl.ANY)


def _params(*sem):
    return pltpu.CompilerParams(dimension_semantics=sem, vmem_limit_bytes=VMEM_LIMIT)


def _sds(shape, dtype):
    return jax.ShapeDtypeStruct(shape, dtype)


def _matmul(a, b, *, ta=False, tb=False, out_dtype, tm, tn, tk, name, b_k_off=0):
    if ta:
        kk, m = a.shape
    else:
        m, kk = a.shape
    n = b.shape[0] if tb else b.shape[1]
    tm, tn, tk = min(tm, m), min(tn, n), min(tk, kk)
    assert (b.shape[1] if tb else b.shape[0]) >= b_k_off * tk + kk
    assert m % tm == 0 and n % tn == 0 and kk % tk == 0, (name, m, n, kk, tm, tn, tk)
    nk = kk // tk
    dims = (((0 if ta else 1,), (1 if tb else 0,)), ((), ()))

    def body(a_ref, b_ref, o_ref, *scratch):
        p = lax.dot_general(a_ref[...].astype(BF16), b_ref[...].astype(BF16), dims,
                            preferred_element_type=F32)
        if nk == 1:
            o_ref[...] = p.astype(o_ref.dtype)
        else:
            acc = scratch[0]
            k = pl.program_id(2)

            @pl.when(k == 0)
            def _():
                acc[...] = p

            @pl.when(k > 0)
            def _():
                acc[...] += p

            @pl.when(k == nk - 1)
            def _():
                o_ref[...] = acc[...].astype(o_ref.dtype)

    a_spec = (pl.BlockSpec((tk, tm), lambda i, j, k: (k, i)) if ta
              else pl.BlockSpec((tm, tk), lambda i, j, k: (i, k)))
    b_spec = (pl.BlockSpec((tn, tk), lambda i, j, k: (j, k + b_k_off)) if tb
              else pl.BlockSpec((tk, tn), lambda i, j, k: (k + b_k_off, j)))
    return pl.pallas_call(
        body, name=name, grid=(m // tm, n // tn, nk),
        in_specs=[a_spec, b_spec],
        out_specs=pl.BlockSpec((tm, tn), lambda i, j, k: (i, j)),
        out_shape=_sds((m, n), out_dtype),
        scratch_shapes=[pltpu.VMEM((tm, tn), F32)] if nk > 1 else [],
        compiler_params=_params("parallel", "parallel", "arbitrary"),
    )(a, b)


def _rms_fwd(x, g, *, name, tm=512):
    def body(x_ref, g_ref, h_ref):
        xv = x_ref[...]
        r = lax.rsqrt(jnp.mean(xv * xv, axis=-1, keepdims=True) + RMS_EPS)
        h_ref[...] = (xv * r * g_ref[...]).astype(h_ref.dtype)

    return pl.pallas_call(
        body, name=name, grid=(SEQ // tm,),
        in_specs=[pl.BlockSpec((tm, D_MODEL), lambda i: (i, 0)), pl.BlockSpec((1, D_MODEL), lambda i: (0, 0))],
        out_specs=pl.BlockSpec((tm, D_MODEL), lambda i: (i, 0)),
        out_shape=_sds((SEQ, D_MODEL), BF16),
        compiler_params=_params("parallel"),
    )(x, g)


def _rms_bwd(dh_parts, xin, g, dres, *, out_dtype, name, tm=512):
    n_parts = len(dh_parts)
    has_res = dres is not None

    def body(*refs):
        parts = refs[:n_parts]
        x_ref, g_ref = refs[n_parts], refs[n_parts + 1]
        res_ref = refs[n_parts + 2] if has_res else None
        o_ref, gg_ref = refs[-2], refs[-1]
        dh = parts[0][...].astype(F32)
        for p in parts[1:]:
            dh = dh + p[...].astype(F32)
        xv = x_ref[...]
        r = lax.rsqrt(jnp.mean(xv * xv, axis=-1, keepdims=True) + RMS_EPS)
        xn = xv * r

        @pl.when(pl.program_id(0) == 0)
        def _():
            gg_ref[...] = jnp.zeros_like(gg_ref)

        gg_ref[...] += jnp.sum(dh * xn, axis=0, keepdims=True)
        dxn = dh * g_ref[...]
        dx = r * (dxn - xn * jnp.mean(dxn * xn, axis=-1, keepdims=True))
        if has_res:
            dx = dx + res_ref[...]
        o_ref[...] = dx.astype(o_ref.dtype)

    row = pl.BlockSpec((tm, D_MODEL), lambda i: (i, 0))
    vec = pl.BlockSpec((1, D_MODEL), lambda i: (0, 0))
    args = list(dh_parts) + [xin, g] + ([dres] if has_res else [])
    return pl.pallas_call(
        body, name=name, grid=(SEQ // tm,),
        in_specs=[row] * n_parts + [row, vec] + ([row] if has_res else []),
        out_specs=[row, vec],
        out_shape=[_sds((SEQ, D_MODEL), out_dtype), _sds((1, D_MODEL), F32)],
        compiler_params=_params("arbitrary"),
    )(*args)


SCAN_BLK = 512


def _split_dot(v, tri):
    hi = v.astype(BF16)
    r1 = v - hi.astype(F32)
    mid = r1.astype(BF16)
    lo = (r1 - mid.astype(F32)).astype(BF16)
    dot = functools.partial(jnp.dot, preferred_element_type=F32)
    return dot(hi, tri) + dot(mid, tri) + dot(lo, tri)


def _fox_prep(fa_t, b_col):
    nblk = SEQ // SCAN_BLK

    def body(fa_ref, b_ref, f_ref, sg_ref):
        row = lax.broadcasted_iota(jnp.int32, (SCAN_BLK, SCAN_BLK), 0)
        col = lax.broadcasted_iota(jnp.int32, (SCAN_BLK, SCAN_BLK), 1)
        upper = (row <= col).astype(BF16)
        carry = jnp.zeros((N_HEADS, 1), F32)
        for blk in range(nblk):
            sl = pl.ds(blk * SCAN_BLK, SCAN_BLK)
            xx = fa_ref[:, sl] + b_ref[...]
            e = jnp.exp(-jnp.abs(xx))
            logf = jnp.minimum(xx, 0.0) - jnp.log(1.0 + e)
            sg_ref[:, sl] = jnp.where(xx >= 0.0, e, 1.0) / (1.0 + e)
            c = _split_dot(logf, upper) + carry
            f_ref[:, sl] = c
            carry = c[:, SCAN_BLK - 1:SCAN_BLK]

    return pl.pallas_call(
        body, name="fox_prep",
        out_shape=[_sds((N_HEADS, SEQ), F32), _sds((N_HEADS, SEQ), F32)],
        compiler_params=pltpu.CompilerParams(vmem_limit_bytes=VMEM_LIMIT),
    )(fa_t, b_col)


def _fox_post_bwd(df_t, sg_t):
    nblk = SEQ // SCAN_BLK

    def body(df_ref, sg_ref, dfa_ref, gb_ref):
        row = lax.broadcasted_iota(jnp.int32, (SCAN_BLK, SCAN_BLK), 0)
        col = lax.broadcasted_iota(jnp.int32, (SCAN_BLK, SCAN_BLK), 1)
        lower = (row >= col).astype(BF16)
        carry = jnp.zeros((N_HEADS, 1), F32)
        gb = jnp.zeros((N_HEADS, 1), F32)
        for blk in reversed(range(nblk)):
            sl = pl.ds(blk * SCAN_BLK, SCAN_BLK)
            c = _split_dot(df_ref[:, sl], lower) + carry
            carry = c[:, 0:1]
            dfa = c * sg_ref[:, sl]
            dfa_ref[:, sl] = dfa
            gb = gb + jnp.sum(dfa, axis=1, keepdims=True)
        gb_ref[...] = gb

    return pl.pallas_call(
        body, name="fox_post_bwd",
        out_shape=[_sds((N_HEADS, SEQ), F32), _sds((N_HEADS, 1), F32)],
        compiler_params=pltpu.CompilerParams(vmem_limit_bytes=VMEM_LIMIT),
    )(df_t, sg_t)


FOX_T = 512
NT_DIMS = (((1,), (1,)), ((), ()))
TN_DIMS = (((0,), (0,)), ((), ()))


def _head(ref_or_val, h):
    return ref_or_val[:, h * HEAD_DIM:(h + 1) * HEAD_DIM]


def _split3(v):
    hi = v.astype(BF16).astype(F32)
    r1 = v - hi
    mid = r1.astype(BF16).astype(F32)
    return hi, mid, (r1 - mid).astype(BF16).astype(F32)


def _aux_lanes(rows, terms):
    lane = lax.broadcasted_iota(jnp.int32, (rows, HEAD_DIM), 1)
    out = jnp.zeros((rows, HEAD_DIM), F32)
    for i, t in enumerate(terms):
        out = jnp.where(lane == i, t, out)
    return out


SLOT = 2 * HEAD_DIM
N_SPLIT = 3


def _slot(ref, h):
    return ref[:, h * SLOT:(h + 1) * SLOT]


def _fox_pack_fwd(zm, f_cols, *, tm=512):
    def body(q_ref, k_ref, v_ref, f_ref, qs_ref, ks_ref, vs_ref):
        ones = jnp.ones((tm, HEAD_DIM), BF16)
        for h in range(N_HEADS):
            fh = _split3(f_ref[:, h:h + 1])
            q_aux = _aux_lanes(tm, list(fh) + [1.0] * N_SPLIT)
            k_aux = _aux_lanes(tm, [1.0] * N_SPLIT + [-t for t in fh])
            qs_ref[:, h * SLOT:(h + 1) * SLOT] = jnp.concatenate(
                [(_head(q_ref, h).astype(F32) * SCALE).astype(BF16), q_aux.astype(BF16)], axis=1)
            ks_ref[:, h * SLOT:(h + 1) * SLOT] = jnp.concatenate([_head(k_ref, h), k_aux.astype(BF16)], axis=1)
            vs_ref[:, h * SLOT:(h + 1) * SLOT] = jnp.concatenate([_head(v_ref, h), ones], axis=1)

    col = lambda b: pl.BlockSpec((tm, ATT_W), lambda i: (i, b))
    wide = pl.BlockSpec((tm, N_HEADS * SLOT), lambda i: (i, 0))
    return pl.pallas_call(
        body, name="fox_pack_fwd", grid=(SEQ // tm,),
        in_specs=[col(0), col(1), col(2), pl.BlockSpec((tm, LANE), lambda i: (i, 0))],
        out_specs=[wide] * 3, out_shape=[_sds((SEQ, N_HEADS * SLOT), BF16)] * 3,
        compiler_params=_params("parallel"),
    )(zm, zm, zm, f_cols)


def _fox_pack_bwd(zm, f_cols, lse, o, do, *, tm=512):
    def body(q_ref, f_ref, lse_ref, o_ref, do_ref, qs_ref, ds_ref):
        for h in range(N_HEADS):
            gh = _split3(f_ref[:, h:h + 1] - lse_ref[:, h * HEAD_DIM:h * HEAD_DIM + 1])
            dout = _head(do_ref, h)
            delta = jnp.sum(_head(o_ref, h).astype(F32) * dout.astype(F32), axis=1, keepdims=True)
            q_aux = _aux_lanes(tm, list(gh) + [1.0] * N_SPLIT)
            d_aux = _aux_lanes(tm, [-t for t in _split3(delta)])
            qs_ref[:, h * SLOT:(h + 1) * SLOT] = jnp.concatenate(
                [(_head(q_ref, h).astype(F32) * SCALE).astype(BF16), q_aux.astype(BF16)], axis=1)
            ds_ref[:, h * SLOT:(h + 1) * SLOT] = jnp.concatenate([dout, d_aux.astype(BF16)], axis=1)

    row = pl.BlockSpec((tm, ATT_W), lambda i: (i, 0))
    wide = pl.BlockSpec((tm, N_HEADS * SLOT), lambda i: (i, 0))
    return pl.pallas_call(
        body, name="fox_pack_bwd", grid=(SEQ // tm,),
        in_specs=[row, pl.BlockSpec((tm, LANE), lambda i: (i, 0)), row, row, row],
        out_specs=[wide] * 2, out_shape=[_sds((SEQ, N_HEADS * SLOT), BF16)] * 2,
        compiler_params=_params("parallel"),
    )(zm, f_cols, lse, o, do)


def _causal_pairs(key_major):
    nb = SEQ // FOX_T
    if key_major:
        pairs = [(i, j) for j in range(nb) for i in range(j, nb)]
    else:
        pairs = [(i, j) for i in range(nb) for j in range(i + 1)]
    return (jnp.array([p[0] for p in pairs], jnp.int32), jnp.array([p[1] for p in pairs], jnp.int32), len(pairs))


def _diag_mask():
    row = lax.broadcasted_iota(jnp.int32, (FOX_T, FOX_T), 0)
    col = lax.broadcasted_iota(jnp.int32, (FOX_T, FOX_T), 1)
    return col <= row


def _fox_fwd(q_slots, k_slots, v_slots):
    i_tab, j_tab, n_pairs = _causal_pairs(False)

    def body(i_tab, j_tab, q_ref, k_ref, v_ref, o_ref, lse_ref, m_s, acc_s):
        t = pl.program_id(1)
        i, j = i_tab[t], j_tab[t]

        @pl.when(j == 0)
        def _():
            m_s[...] = jnp.full_like(m_s, NEG_INF)
            acc_s[...] = jnp.zeros_like(acc_s)

        def step(masked):
            scores = [lax.dot_general(_slot(q_ref, h), _slot(k_ref, h), NT_DIMS, preferred_element_type=F32)
                      for h in range(2)]
            probs, alphas = [], []
            for h in range(2):
                s = jnp.where(_diag_mask(), scores[h], NEG_INF) if masked else scores[h]
                m_prev = m_s[h]
                m_new = jnp.maximum(m_prev, jnp.max(s, axis=-1, keepdims=True))
                probs.append(jnp.exp(s - jnp.tile(m_new, (1, FOX_T // LANE))).astype(BF16))
                alphas.append(jnp.exp(m_prev - m_new))
                m_s[h] = m_new
            for h in range(2):
                acc_s[h] = alphas[h] * acc_s[h] + jnp.dot(probs[h], _slot(v_ref, h), preferred_element_type=F32)

        @pl.when(j < i)
        def _():
            step(False)

        @pl.when(j == i)
        def _():
            step(True)
            outs, lses = [], []
            for h in range(2):
                acc = acc_s[h]
                l = acc[:, HEAD_DIM:]
                outs.append(acc[:, :HEAD_DIM] / l)
                lses.append(m_s[h][:, :HEAD_DIM] + jnp.log(l))
            o_ref[...] = jnp.concatenate(outs, axis=1).astype(o_ref.dtype)
            lse_ref[...] = jnp.concatenate(lses, axis=1)

    qspec = pl.BlockSpec((FOX_T, 2 * SLOT), lambda p, t, it, jt: (it[t], p))
    kspec = pl.BlockSpec((FOX_T, 2 * SLOT), lambda p, t, it, jt: (jt[t], p))
    ospec = pl.BlockSpec((FOX_T, LANE), lambda p, t, it, jt: (it[t], p))
    return pl.pallas_call(
        body, name="fox_fwd",
        grid_spec=pltpu.PrefetchScalarGridSpec(
            num_scalar_prefetch=2, grid=(N_HEADS // 2, n_pairs),
            in_specs=[qspec, kspec, kspec], out_specs=[ospec, ospec],
            scratch_shapes=[pltpu.VMEM((2, FOX_T, LANE), F32), pltpu.VMEM((2, FOX_T, SLOT), F32)]),
        out_shape=[_sds((SEQ, ATT_W), BF16), _sds((SEQ, ATT_W), F32)],
        compiler_params=_params("parallel", "arbitrary"),
    )(i_tab, j_tab, q_slots, k_slots, v_slots)


def _fox_bwd(q_slots, k_slots, v_slots, do_slots):
    i_tab, j_tab, n_pairs = _causal_pairs(True)

    def body(i_tab, j_tab, q_ref, k_ref, v_ref, do_ref, dq_ref, dk_ref, dv_ref):
        t = pl.program_id(1)
        i, j = i_tab[t], j_tab[t]

        @pl.when(t == 0)
        def _():
            dq_ref[...] = jnp.zeros_like(dq_ref)

        @pl.when(i == j)
        def _():
            dk_ref[...] = jnp.zeros_like(dk_ref)
            dv_ref[...] = jnp.zeros_like(dv_ref)

        def step(masked):
            rows = pl.ds(pl.multiple_of(i * FOX_T, FOX_T), FOX_T)
            heads = range(2)
            scores = [lax.dot_general(_slot(q_ref, h), _slot(k_ref, h), NT_DIMS, preferred_element_type=F32)
                      for h in heads]
            dps = [lax.dot_general(_slot(do_ref, h), _slot(v_ref, h), NT_DIMS, preferred_element_type=F32)
                   for h in heads]
            ps, dss = [], []
            for h in heads:
                p = jnp.exp(scores[h])
                if masked:
                    p = jnp.where(_diag_mask(), p, 0.0)
                ps.append(p.astype(BF16))
                dss.append((p * dps[h]).astype(BF16))
            for h in heads:
                cols = slice(h * SLOT, (h + 1) * SLOT)
                dv_ref[:, cols] += lax.dot_general(ps[h], _slot(do_ref, h), TN_DIMS, preferred_element_type=F32)
                dk_ref[:, cols] += lax.dot_general(dss[h], _slot(q_ref, h), TN_DIMS, preferred_element_type=F32)
                dq_ref[rows, cols] += jnp.dot(dss[h], _slot(k_ref, h), preferred_element_type=F32)

        @pl.when(i > j)
        def _():
            step(False)

        @pl.when(i == j)
        def _():
            step(True)

    qspec = pl.BlockSpec((FOX_T, 2 * SLOT), lambda p, t, it, jt: (it[t], p))
    kspec = pl.BlockSpec((FOX_T, 2 * SLOT), lambda p, t, it, jt: (jt[t], p))
    return pl.pallas_call(
        body, name="fox_bwd",
        grid_spec=pltpu.PrefetchScalarGridSpec(
            num_scalar_prefetch=2, grid=(N_HEADS // 2, n_pairs),
            in_specs=[qspec, kspec, kspec, qspec],
            out_specs=[pl.BlockSpec((SEQ, 2 * SLOT), lambda p, t, it, jt: (0, p)), kspec, kspec]),
        out_shape=[_sds((SEQ, N_HEADS * SLOT), F32)] * 3,
        compiler_params=_params("arbitrary", "arbitrary"),
    )(i_tab, j_tab, q_slots, k_slots, v_slots, do_slots)


def _fox_unpack(dq_slots, dk_slots, dv_slots, *, tm=512):
    def body(dq_ref, dk_ref, dv_ref, o_ref, df_ref):
        lane = lax.broadcasted_iota(jnp.int32, (tm, LANE), 1)
        df = jnp.zeros((tm, LANE), F32)
        for h in range(N_HEADS):
            lo = h * SLOT
            for part, (ref, mult) in enumerate(((dq_ref, SCALE), (dk_ref, 1.0), (dv_ref, 1.0))):
                o_ref[:, part * ATT_W + h * HEAD_DIM:part * ATT_W + (h + 1) * HEAD_DIM] = (
                    ref[:, lo:lo + HEAD_DIM] * mult).astype(o_ref.dtype)
            rows = dq_ref[:, lo + HEAD_DIM:lo + HEAD_DIM + 1]
            cols = dk_ref[:, lo + HEAD_DIM + N_SPLIT:lo + HEAD_DIM + N_SPLIT + 1]
            df = jnp.where(lane == h, rows - cols, df)
        df_ref[...] = df

    wide = pl.BlockSpec((tm, N_HEADS * SLOT), lambda i: (i, 0))
    return pl.pallas_call(
        body, name="fox_unpack", grid=(SEQ // tm,), in_specs=[wide] * 3,
        out_specs=[pl.BlockSpec((tm, 3 * ATT_W), lambda i: (i, 0)), pl.BlockSpec((tm, LANE), lambda i: (i, 0))],
        out_shape=[_sds((SEQ, 3 * ATT_W), BF16), _sds((SEQ, LANE), F32)],
        compiler_params=_params("parallel"),
    )(dq_slots, dk_slots, dv_slots)


def _attn_delta(o, do, *, name, tm=512):
    def body(o_ref, do_ref, d_ref):
        prod = o_ref[...].astype(F32) * do_ref[...].astype(F32)
        lane = lax.broadcasted_iota(jnp.int32, (tm, LANE), 1)
        out = jnp.zeros((tm, LANE), F32)
        for h in range(N_HEADS):
            out = jnp.where(lane == h, jnp.sum(_head(prod, h), axis=1, keepdims=True), out)
        d_ref[...] = out

    row = pl.BlockSpec((tm, ATT_W), lambda i: (i, 0))
    return pl.pallas_call(
        body, name=name, grid=(SEQ // tm,), in_specs=[row, row],
        out_specs=pl.BlockSpec((tm, LANE), lambda i: (i, 0)), out_shape=_sds((SEQ, LANE), F32),
        compiler_params=_params("parallel"),
    )(o, do)


def _rope_tables():
    half = ROPE_DIM // 2
    inv_freq = np.float32(ROPE_THETA) ** (-np.arange(half, dtype=np.float32) * np.float32(2.0) / np.float32(ROPE_DIM))
    ang = np.arange(SEQ, dtype=np.float32)[:, None] * inv_freq.astype(np.float32)[None, :]
    cos, sin = jnp.asarray(np.cos(ang).astype(np.float32)), jnp.asarray(np.sin(ang).astype(np.float32))
    ones = jnp.ones((SEQ, HEAD_DIM - ROPE_DIM), F32)
    zeros = jnp.zeros((SEQ, HEAD_DIM - ROPE_DIM), F32)
    zh = jnp.zeros((SEQ, half), F32)
    c_tab = jnp.concatenate([cos, cos, ones], axis=1)
    a_tab = jnp.concatenate([-sin, zh, zeros], axis=1)
    b_tab = jnp.concatenate([zh, sin, zeros], axis=1)
    two = lambda t: jnp.concatenate([t, t], axis=1)
    return two(c_tab), two(a_tab), two(b_tab)


def _rotate(x, c_tab, a_tab, b_tab):
    return x * c_tab + pltpu.roll(x, LANE - ROPE_DIM // 2, 1) * a_tab + pltpu.roll(x, ROPE_DIM // 2, 1) * b_tab


def _rope_fwd(zm, tabs, *, tm=512):
    def body(q_ref, k_ref, v_ref, c_ref, a_ref, b_ref, o_ref):
        for part, (x_ref, mult) in enumerate(((q_ref, SCALE), (k_ref, 1.0))):
            for cc in range(ATT_W // LANE):
                sl = slice(cc * LANE, (cc + 1) * LANE)
                rot = _rotate(x_ref[:, sl].astype(F32), c_ref[...], a_ref[...], b_ref[...])
                o_ref[:, part * ATT_W + cc * LANE:part * ATT_W + (cc + 1) * LANE] = (rot * mult).astype(o_ref.dtype)
        o_ref[:, 2 * ATT_W:] = v_ref[...]

    tab = pl.BlockSpec((tm, LANE), lambda i: (i, 0))
    col = lambda b: pl.BlockSpec((tm, ATT_W), lambda i: (i, b))
    return pl.pallas_call(
        body, name="rope_fwd", grid=(SEQ // tm,),
        in_specs=[col(3), col(4), col(5), tab, tab, tab],
        out_specs=pl.BlockSpec((tm, 3 * ATT_W), lambda i: (i, 0)),
        out_shape=_sds((SEQ, 3 * ATT_W), BF16),
        compiler_params=_params("parallel"),
    )(zm, zm, zm, *tabs)


def _dil_grad_combine(dqs, dks, dvs, tabs, *, tm=256):
    def body(*refs):
        q_refs, k_refs, v_refs = refs[0:3], refs[3:6], refs[6:9]
        c_ref, a_ref, b_ref, o_ref = refs[9:]
        total = lambda rs, sl: rs[0][:, sl].astype(F32) + rs[1][:, sl].astype(F32) + rs[2][:, sl].astype(F32)
        for cc in range(ATT_W // LANE):
            sl = slice(cc * LANE, (cc + 1) * LANE)
            for part, rs in enumerate((q_refs, k_refs)):
                o_ref[:, part * ATT_W + cc * LANE:part * ATT_W + (cc + 1) * LANE] = _rotate(
                    total(rs, sl), c_ref[...], -a_ref[...], -b_ref[...]).astype(o_ref.dtype)
            o_ref[:, 2 * ATT_W + cc * LANE:2 * ATT_W + (cc + 1) * LANE] = total(v_refs, sl).astype(o_ref.dtype)

    row = pl.BlockSpec((tm, ATT_W), lambda i: (i, 0))
    tab = pl.BlockSpec((tm, LANE), lambda i: (i, 0))
    return pl.pallas_call(
        body, name="dil_grad_combine", grid=(SEQ // tm,),
        in_specs=[row] * 9 + [tab] * 3,
        out_specs=pl.BlockSpec((tm, 3 * ATT_W), lambda i: (i, 0)),
        out_shape=_sds((SEQ, 3 * ATT_W), BF16),
        compiler_params=_params("parallel"),
    )(*dqs, *dks, *dvs, *tabs)


def _dil_valid(n):
    qi = lax.broadcasted_iota(jnp.int32, (DIL_BLK, 2 * DIL_BLK), 0)
    ki = lax.broadcasted_iota(jnp.int32, (DIL_BLK, 2 * DIL_BLK), 1)
    dist = qi + DIL_BLK - ki
    return (dist >= 0) & (dist <= DIL_BLK) & ((n > 0) | (ki >= DIL_BLK))


def _dil_fwd(qkv, d):
    length = SEQ // d
    nb = length // DIL_BLK
    qkv_v = qkv.reshape(length, d * 3 * ATT_W)

    def body(q_ref, kp_ref, kc_ref, vp_ref, vc_ref, o_ref, lse_ref):
        n = pl.program_id(1)
        ok = _dil_valid(n)
        lane = lax.broadcasted_iota(jnp.int32, (DIL_BLK, LANE), 1)
        lse_all = jnp.zeros((DIL_BLK, LANE), F32)
        heads = range(N_HEADS)
        scores = [lax.dot_general(_head(q_ref, h), jnp.concatenate([_head(kp_ref, h), _head(kc_ref, h)], axis=0),
                                  NT_DIMS, preferred_element_type=F32) for h in heads]
        probs, inv_l = [], []
        for h in heads:
            s = jnp.where(ok, scores[h], NEG_INF)
            m = jnp.max(s, axis=-1, keepdims=True)
            p = jnp.exp(s - m)
            l = jnp.sum(p, axis=-1, keepdims=True)
            probs.append(p.astype(BF16))
            inv_l.append(1.0 / l)
            lse_all = jnp.where(lane == h, m + jnp.log(l), lse_all)
        outs = [jnp.dot(probs[h], jnp.concatenate([_head(vp_ref, h), _head(vc_ref, h)], axis=0),
                        preferred_element_type=F32) * inv_l[h] for h in heads]
        o_ref[...] = jnp.concatenate(outs, axis=1).astype(o_ref.dtype)
        lse_ref[...] = lse_all

    blk = lambda f: pl.BlockSpec((DIL_BLK, ATT_W), f)
    prev = lambda n: jnp.maximum(n - 1, 0)
    o, lse = pl.pallas_call(
        body, name=f"dil_fwd_d{d}", grid=(d, nb),
        in_specs=[blk(lambda r, n: (n, 3 * r)),
                  blk(lambda r, n: (prev(n), 3 * r + 1)), blk(lambda r, n: (n, 3 * r + 1)),
                  blk(lambda r, n: (prev(n), 3 * r + 2)), blk(lambda r, n: (n, 3 * r + 2))],
        out_specs=[blk(lambda r, n: (n, r)), pl.BlockSpec((DIL_BLK, LANE), lambda r, n: (n, r))],
        out_shape=[_sds((length, d * ATT_W), BF16), _sds((length, d * LANE), F32)],
        compiler_params=_params("parallel", "arbitrary"),
    )(qkv_v, qkv_v, qkv_v, qkv_v, qkv_v)
    return o.reshape(SEQ, ATT_W), lse.reshape(SEQ, LANE)


def _dil_merge(os_, lses, *, tm=512):
    def body(o0, o1, o2, l0, l1, l2, y_ref, lse_ref):
        ls = [l0[...], l1[...], l2[...]]
        m = jnp.maximum(jnp.maximum(ls[0], ls[1]), ls[2])
        es = [jnp.exp(l - m) for l in ls]
        tot = es[0] + es[1] + es[2]
        lse_ref[...] = m + jnp.log(tot)
        alphas = [e / tot for e in es]
        outs = []
        for h in range(N_HEADS):
            acc = None
            for g, o_ref in enumerate((o0, o1, o2)):
                term = alphas[g][:, h:h + 1] * _head(o_ref, h).astype(F32)
                acc = term if acc is None else acc + term
            outs.append(acc)
        y_ref[...] = jnp.concatenate(outs, axis=1).astype(y_ref.dtype)

    row = pl.BlockSpec((tm, ATT_W), lambda i: (i, 0))
    vec = pl.BlockSpec((tm, LANE), lambda i: (i, 0))
    return pl.pallas_call(
        body, name="dil_merge", grid=(SEQ // tm,),
        in_specs=[row] * 3 + [vec] * 3, out_specs=[row, vec],
        out_shape=[_sds((SEQ, ATT_W), BF16), _sds((SEQ, LANE), F32)],
        compiler_params=_params("parallel"),
    )(*os_, *lses)


def _dil_bwd(qkv, lse, delta, do, d):
    length = SEQ // d
    nb = length // DIL_BLK
    qkv_v = qkv.reshape(length, d * 3 * ATT_W)
    lse_v, dl_v, do_v = lse.reshape(length, d * LANE), delta.reshape(length, d * LANE), do.reshape(length, d * ATT_W)

    def body(q_ref, kp_ref, kc_ref, vp_ref, vc_ref, lse_ref, dl_ref, do_ref,
             dq_ref, dk_ref, dv_ref, ck_s, cv_s):
        n = pl.program_id(1)

        @pl.when(n == 0)
        def _():
            ck_s[...] = jnp.zeros_like(ck_s)
            cv_s[...] = jnp.zeros_like(cv_s)

        @pl.when(n < nb)
        def _():
            ok = _dil_valid(n)
            heads = range(N_HEADS)
            kks = [jnp.concatenate([_head(kp_ref, h), _head(kc_ref, h)], axis=0) for h in heads]
            scores = [lax.dot_general(_head(q_ref, h), kks[h], NT_DIMS, preferred_element_type=F32) for h in heads]
            dps = [lax.dot_general(_head(do_ref, h), jnp.concatenate([_head(vp_ref, h), _head(vc_ref, h)], axis=0),
                                   NT_DIMS, preferred_element_type=F32) for h in heads]
            ps, dss = [], []
            for h in heads:
                p = jnp.where(ok, jnp.exp(scores[h] - lse_ref[:, h:h + 1]), 0.0)
                ps.append(p.astype(BF16))
                dss.append((p * (dps[h] - dl_ref[:, h:h + 1])).astype(BF16))
            dqs = [jnp.dot(dss[h], kks[h], preferred_element_type=F32) * SCALE for h in heads]
            dkks = [lax.dot_general(dss[h], _head(q_ref, h), TN_DIMS, preferred_element_type=F32) for h in heads]
            dvvs = [lax.dot_general(ps[h], _head(do_ref, h), TN_DIMS, preferred_element_type=F32) for h in heads]
            dq_ref[...] = jnp.concatenate(dqs, axis=1).astype(dq_ref.dtype)
            dk_ref[...] = (ck_s[...] + jnp.concatenate([t[:DIL_BLK] for t in dkks], axis=1)).astype(dk_ref.dtype)
            dv_ref[...] = (cv_s[...] + jnp.concatenate([t[:DIL_BLK] for t in dvvs], axis=1)).astype(dv_ref.dtype)
            ck_s[...] = jnp.concatenate([t[DIL_BLK:] for t in dkks], axis=1)
            cv_s[...] = jnp.concatenate([t[DIL_BLK:] for t in dvvs], axis=1)

        @pl.when(n == nb)
        def _():
            dk_ref[...] = ck_s[...].astype(dk_ref.dtype)
            dv_ref[...] = cv_s[...].astype(dv_ref.dtype)

    blk = lambda f: pl.BlockSpec((DIL_BLK, ATT_W), f)
    vec = lambda f: pl.BlockSpec((DIL_BLK, LANE), f)
    cur = lambda n: jnp.minimum(n, nb - 1)
    prev = lambda n: jnp.maximum(cur(n) - 1, 0)
    back = lambda n: jnp.maximum(n - 1, 0)
    outs = pl.pallas_call(
        body, name=f"dil_bwd_d{d}", grid=(d, nb + 1),
        in_specs=[blk(lambda r, n: (cur(n), 3 * r)),
                  blk(lambda r, n: (prev(n), 3 * r + 1)), blk(lambda r, n: (cur(n), 3 * r + 1)),
                  blk(lambda r, n: (prev(n), 3 * r + 2)), blk(lambda r, n: (cur(n), 3 * r + 2)),
                  vec(lambda r, n: (cur(n), r)), vec(lambda r, n: (cur(n), r)),
                  blk(lambda r, n: (cur(n), r))],
        out_specs=[blk(lambda r, n: (cur(n), r)), blk(lambda r, n: (back(n), r)), blk(lambda r, n: (back(n), r))],
        out_shape=[_sds((length, d * ATT_W), BF16)] * 3,
        scratch_shapes=[pltpu.VMEM((DIL_BLK, ATT_W), F32), pltpu.VMEM((DIL_BLK, ATT_W), F32)],
        compiler_params=_params("arbitrary", "arbitrary"),
    )(qkv_v, qkv_v, qkv_v, qkv_v, qkv_v, lse_v, dl_v, do_v)
    return [t.reshape(SEQ, ATT_W) for t in outs]


def _sigmoid(x):
    return 1.0 / (1.0 + jnp.exp(-x))


def _mix_fwd(ya, yb, w_oa, w_ob, zm, *, tm=512):
    def body(ya_ref, yb_ref, wa_ref, wb_ref, ga_ref, gb_ref, pa_ref, pb_ref, mix_ref):
        pa = jnp.dot(ya_ref[...], wa_ref[...], preferred_element_type=F32)
        pb = jnp.dot(yb_ref[...], wb_ref[...], preferred_element_type=F32)
        pa_ref[...] = pa.astype(pa_ref.dtype)
        pb_ref[...] = pb.astype(pb_ref.dtype)
        mix_ref[...] = (_sigmoid(ga_ref[...].astype(F32)) * pa + _sigmoid(gb_ref[...].astype(F32)) * pb
                        ).astype(mix_ref.dtype)

    row = pl.BlockSpec((tm, ATT_W), lambda i: (i, 0))
    wsp = pl.BlockSpec((ATT_W, D_MODEL), lambda i: (0, 0))
    wide = pl.BlockSpec((tm, D_MODEL), lambda i: (i, 0))
    return pl.pallas_call(
        body, name="mix_fwd", grid=(SEQ // tm,),
        in_specs=[row, row, wsp, wsp, pl.BlockSpec((tm, D_MODEL), lambda i: (i, 3)),
                  pl.BlockSpec((tm, D_MODEL), lambda i: (i, 4))],
        out_specs=[wide] * 3, out_shape=[_sds((SEQ, D_MODEL), BF16)] * 3,
        compiler_params=_params("parallel"),
    )(ya, yb, w_oa, w_ob, zm, zm)


def _gate_bwd(dmix, zm, pa, pb, *, tm=512):
    def body(dm_ref, ga_ref, gb_ref, pa_ref, pb_ref, dpa_ref, dpb_ref, dg_ref):
        dm = dm_ref[...].astype(F32)
        sa, sb = _sigmoid(ga_ref[...].astype(F32)), _sigmoid(gb_ref[...].astype(F32))
        dpa_ref[...] = (dm * sa).astype(dpa_ref.dtype)
        dpb_ref[...] = (dm * sb).astype(dpb_ref.dtype)
        dg_ref[:, :D_MODEL] = (dm * pa_ref[...].astype(F32) * sa * (1.0 - sa)).astype(dg_ref.dtype)
        dg_ref[:, D_MODEL:] = (dm * pb_ref[...].astype(F32) * sb * (1.0 - sb)).astype(dg_ref.dtype)

    wide = pl.BlockSpec((tm, D_MODEL), lambda i: (i, 0))
    return pl.pallas_call(
        body, name="gate_bwd", grid=(SEQ // tm,),
        in_specs=[wide, pl.BlockSpec((tm, D_MODEL), lambda i: (i, 3)), pl.BlockSpec((tm, D_MODEL), lambda i: (i, 4)),
                  wide, wide],
        out_specs=[wide, wide, pl.BlockSpec((tm, 2 * D_MODEL), lambda i: (i, 0))],
        out_shape=[_sds((SEQ, D_MODEL), BF16), _sds((SEQ, D_MODEL), BF16), _sds((SEQ, 2 * D_MODEL), BF16)],
        compiler_params=_params("parallel"),
    )(dmix, zm, zm, pa, pb)


def _out_fwd(mixed, w_out, x, g_post, g_pre, *, tm=512):
    def body(m_ref, w_ref, x_ref, gp_ref, gn_ref, y_ref, x2_ref, h_ref):
        y = jnp.dot(m_ref[...], w_ref[...], preferred_element_type=F32)
        y_ref[...] = y
        r = lax.rsqrt(jnp.mean(y * y, axis=-1, keepdims=True) + RMS_EPS)
        x2 = x_ref[...] + y * r * gp_ref[...]
        x2_ref[...] = x2
        r2 = lax.rsqrt(jnp.mean(x2 * x2, axis=-1, keepdims=True) + RMS_EPS)
        h_ref[...] = (x2 * r2 * gn_ref[...]).astype(h_ref.dtype)

    row = pl.BlockSpec((tm, D_MODEL), lambda i: (i, 0))
    vec = pl.BlockSpec((1, D_MODEL), lambda i: (0, 0))
    return pl.pallas_call(
        body, name="out_fwd", grid=(SEQ // tm,),
        in_specs=[row, pl.BlockSpec((D_MODEL, D_MODEL), lambda i: (0, 0)), row, vec, vec],
        out_specs=[row] * 3,
        out_shape=[_sds((SEQ, D_MODEL), F32), _sds((SEQ, D_MODEL), F32), _sds((SEQ, D_MODEL), BF16)],
        compiler_params=_params("parallel"),
    )(mixed, w_out, x, g_post, g_pre)


FFN_TM = 512
FFN_TN = 256
FFN_NJ = D_FF // FFN_TN
FFN_GROUP = 2 * SUBLANE


def _gelu_parts(a):
    c = math.sqrt(2.0 / math.pi)
    t = jnp.tanh(c * (a + 0.044715 * a * a * a))
    gelu = 0.5 * a * (1.0 + t)
    dgelu = 0.5 * (1.0 + t) + 0.5 * a * (1.0 - t * t) * c * (1.0 + 3.0 * 0.044715 * a * a)
    return gelu, dgelu


def _shift_down(cur, above, k):
    row = lax.broadcasted_iota(jnp.int32, cur.shape, 0)
    return jnp.where(row < k, pltpu.roll(above, k, 0), pltpu.roll(cur, k, 0))


def _shift_up(cur, below, k):
    row = lax.broadcasted_iota(jnp.int32, cur.shape, 0)
    return jnp.where(row >= SUBLANE - k, pltpu.roll(below, SUBLANE - k, 0), pltpu.roll(cur, SUBLANE - k, 0))


def _conv_consts(w_ref, b_ref):
    shape = (SUBLANE, FFN_TN)
    return [jnp.broadcast_to(w_ref[k:k + 1, :], shape) for k in range(3)] + [jnp.broadcast_to(b_ref[...], shape)]


def _conv_taps(cur, above, consts):
    w0, w1, w2, bias = consts
    s1, s2 = _shift_down(cur, above, 1), _shift_down(cur, above, 2)
    return w0 * s2 + w1 * s1 + w2 * cur + bias, s1, s2


def _ffn_specs(rev):
    nrow = SEQ // FFN_TM
    per = FFN_TM // SUBLANE
    ri = (lambda i: nrow - 1 - i) if rev else (lambda i: i)
    main = lambda off: pl.BlockSpec((FFN_TM, FFN_TN), lambda j, i: (ri(i), j + off))
    halo = lambda off: pl.BlockSpec((SUBLANE, FFN_TN), lambda j, i: (jnp.maximum(ri(i) * per - 1, 0), j + off))
    wsp = lambda off: pl.BlockSpec((3, FFN_TN), lambda j, i: (0, j + off))
    bsp = lambda off: pl.BlockSpec((1, FFN_TN), lambda j, i: (0, j + off))
    return ri, main, halo, wsp, bsp


def _ffn_mid_fwd(u, conv_w, conv_b):
    ri, main, halo, wsp, bsp = _ffn_specs(False)

    def body(ua_ref, ub_ref, ha_ref, hb_ref, wa_ref, wb_ref, ba_ref, bb_ref, m_ref):
        live = (pl.program_id(1) > 0).astype(F32)
        ca, cb = _conv_consts(wa_ref, ba_ref), _conv_consts(wb_ref, bb_ref)

        def group(g, carry):
            above_a, above_b = carry
            rows = pl.ds(pl.multiple_of(g * FFN_GROUP, FFN_GROUP), FFN_GROUP)
            xa, xb = ua_ref[rows, :].astype(F32), ub_ref[rows, :].astype(F32)
            outs = []
            for c in range(2):
                cur_a, cur_b = xa[c * SUBLANE:(c + 1) * SUBLANE], xb[c * SUBLANE:(c + 1) * SUBLANE]
                a = _conv_taps(cur_a, above_a, ca)[0]
                b = _conv_taps(cur_b, above_b, cb)[0]
                outs.append(_gelu_parts(a)[0] * b)
                above_a, above_b = cur_a, cur_b
            m_ref[rows, :] = jnp.concatenate(outs, axis=0).astype(m_ref.dtype)
            return above_a, above_b

        lax.fori_loop(0, FFN_TM // FFN_GROUP, group,
                      (ha_ref[...].astype(F32) * live, hb_ref[...].astype(F32) * live))

    return pl.pallas_call(
        body, name="ffn_mid_fwd", grid=(FFN_NJ, SEQ // FFN_TM),
        in_specs=[main(0), main(FFN_NJ), halo(0), halo(FFN_NJ), wsp(0), wsp(FFN_NJ), bsp(0), bsp(FFN_NJ)],
        out_specs=pl.BlockSpec((FFN_TM, FFN_TN), lambda j, i: (i, j)),
        out_shape=_sds((SEQ, D_FF), BF16),
        compiler_params=_params("parallel", "arbitrary"),
    )(u, u, u, u, conv_w, conv_w, conv_b, conv_b)


def _ffn_mid_bwd(dm, u, conv_w, conv_b):
    ri, main, halo, wsp, bsp = _ffn_specs(True)
    nrow = SEQ // FFN_TM

    def body(dm_ref, ua_ref, ub_ref, ha_ref, hb_ref, wa_ref, wb_ref, ba_ref, bb_ref,
             dua_ref, dub_ref, gwa_ref, gwb_ref, gba_ref, gbb_ref, ca_s, cb_s):
        i = pl.program_id(1)
        live = (i < nrow - 1).astype(F32)

        @pl.when(i == 0)
        def _():
            ca_s[...] = jnp.zeros_like(ca_s)
            cb_s[...] = jnp.zeros_like(cb_s)
            for r in (gwa_ref, gwb_ref, gba_ref, gbb_ref):
                r[...] = jnp.zeros_like(r)

        consts_a, consts_b = _conv_consts(wa_ref, ba_ref), _conv_consts(wb_ref, bb_ref)
        halo_a, halo_b = ha_ref[...].astype(F32) * live, hb_ref[...].astype(F32) * live
        n_groups = FFN_TM // FFN_GROUP

        def load(g):
            rows = pl.ds(pl.multiple_of(g * FFN_GROUP, FFN_GROUP), FFN_GROUP)
            return ua_ref[rows, :].astype(F32), ub_ref[rows, :].astype(F32)

        def group(t, carry):
            xa, xb, below_a, below_b, acc = carry
            g = n_groups - 1 - t
            rows = pl.ds(pl.multiple_of(g * FFN_GROUP, FFN_GROUP), FFN_GROUP)
            ya, yb = load(jnp.maximum(g - 1, 0))
            top_a = jnp.where(g > 0, ya[SUBLANE:], halo_a)
            top_b = jnp.where(g > 0, yb[SUBLANE:], halo_b)
            dmv = dm_ref[rows, :].astype(F32)
            acc = list(acc)
            pre_a, pre_b = [None, None], [None, None]
            for c in (1, 0):
                sl = slice(c * SUBLANE, (c + 1) * SUBLANE)
                cur_a, cur_b = xa[sl], xb[sl]
                above_a, above_b = (xa[:SUBLANE], xb[:SUBLANE]) if c == 1 else (top_a, top_b)
                a, a1, a2 = _conv_taps(cur_a, above_a, consts_a)
                b, b1, b2 = _conv_taps(cur_b, above_b, consts_b)
                gelu, dgelu = _gelu_parts(a)
                du_a, du_b = dmv[sl] * b * dgelu, dmv[sl] * gelu
                for base, du, taps in ((0, du_a, (a2, a1, cur_a)), (4, du_b, (b2, b1, cur_b))):
                    for k in range(3):
                        acc[base + k] = acc[base + k] + du * taps[k]
                    acc[base + 3] = acc[base + 3] + du
                pre_a[c] = (consts_a[2] * du_a + consts_a[1] * _shift_up(du_a, below_a, 1)
                            + consts_a[0] * _shift_up(du_a, below_a, 2))
                pre_b[c] = (consts_b[2] * du_b + consts_b[1] * _shift_up(du_b, below_b, 1)
                            + consts_b[0] * _shift_up(du_b, below_b, 2))
                below_a, below_b = du_a, du_b
            dua_ref[rows, :] = jnp.concatenate(pre_a, axis=0).astype(dua_ref.dtype)
            dub_ref[rows, :] = jnp.concatenate(pre_b, axis=0).astype(dub_ref.dtype)
            return ya, yb, below_a, below_b, tuple(acc)

        zeros = jnp.zeros((SUBLANE, FFN_TN), F32)
        xa0, xb0 = load(n_groups - 1)
        _, _, below_a, below_b, acc = lax.fori_loop(
            0, n_groups, group, (xa0, xb0, ca_s[...], cb_s[...], (zeros,) * 8))
        ca_s[...] = below_a
        cb_s[...] = below_b
        for base, gw_ref, gb_ref in ((0, gwa_ref, gba_ref), (4, gwb_ref, gbb_ref)):
            for k in range(3):
                gw_ref[k:k + 1, :] += jnp.sum(acc[base + k], axis=0, keepdims=True)
            gb_ref[...] += jnp.sum(acc[base + 3], axis=0, keepdims=True)

    acc3 = pl.BlockSpec((3, FFN_TN), lambda j, i: (0, j))
    acc1 = pl.BlockSpec((1, FFN_TN), lambda j, i: (0, j))
    out_blk = pl.BlockSpec((FFN_TM, FFN_TN), lambda j, i: (ri(i), j))
    return pl.pallas_call(
        body, name="ffn_mid_bwd", grid=(FFN_NJ, nrow),
        in_specs=[pl.BlockSpec((FFN_TM, FFN_TN), lambda j, i: (ri(i), j)),
                  main(0), main(FFN_NJ), halo(0), halo(FFN_NJ), wsp(0), wsp(FFN_NJ), bsp(0), bsp(FFN_NJ)],
        out_specs=[out_blk, out_blk, acc3, acc3, acc1, acc1],
        out_shape=[_sds((SEQ, D_FF), BF16), _sds((SEQ, D_FF), BF16), _sds((3, D_FF), F32), _sds((3, D_FF), F32),
                   _sds((1, D_FF), F32), _sds((1, D_FF), F32)],
        scratch_shapes=[pltpu.VMEM((SUBLANE, FFN_TN), F32), pltpu.VMEM((SUBLANE, FFN_TN), F32)],
        compiler_params=_params("parallel", "arbitrary"),
    )(dm, u, u, u, u, conv_w, conv_w, conv_b, conv_b)


def _down_fwd(m, w_down, x2, g_post, target, *, tm=512):
    def body(m_ref, w_ref, x2_ref, g_ref, t_ref, dout_ref, dy_ref, gg_ref, loss_ref):
        @pl.when(pl.program_id(0) == 0)
        def _():
            gg_ref[...] = jnp.zeros_like(gg_ref)
            loss_ref[...] = jnp.zeros_like(loss_ref)

        y = jnp.dot(m_ref[...], w_ref[...], preferred_element_type=F32)
        r = lax.rsqrt(jnp.mean(y * y, axis=-1, keepdims=True) + RMS_EPS)
        yn = y * r
        diff = (x2_ref[...] + yn * g_ref[...]) - t_ref[...]
        loss_ref[...] += jnp.sum(diff * diff)
        dout = diff * (1.0 / D_MODEL)
        dout_ref[...] = dout
        gg_ref[...] += jnp.sum(dout * yn, axis=0, keepdims=True)
        dn = dout * g_ref[...]
        dy_ref[...] = (r * (dn - yn * jnp.mean(dn * yn, axis=-1, keepdims=True))).astype(dy_ref.dtype)

    row = pl.BlockSpec((tm, D_MODEL), lambda i: (i, 0))
    vec = pl.BlockSpec((1, D_MODEL), lambda i: (0, 0))
    return pl.pallas_call(
        body, name="down_fwd", grid=(SEQ // tm,),
        in_specs=[pl.BlockSpec((tm, D_FF), lambda i: (i, 0)), pl.BlockSpec((D_FF, D_MODEL), lambda i: (0, 0)),
                  row, vec, row],
        out_specs=[row, row, vec, pl.BlockSpec((1, LANE), lambda i: (0, 0))],
        out_shape=[_sds((SEQ, D_MODEL), F32), _sds((SEQ, D_MODEL), BF16), _sds((1, D_MODEL), F32),
                   _sds((1, LANE), F32)],
        compiler_params=_params("arbitrary"),
    )(m, w_down, x2, g_post, target)


def _local_step(x, target, w_main, w_f, b_forget, conv_b, g_pre_mix, g_post_mix, g_pre_ffn, g_post_ffn,
                late_weights, ffn_grads_ready, proj_grads_ready, mixer_grads_ready):
    mm = _matmul
    tabs = _rope_tables()

    h1 = _rms_fwd(x, g_pre_mix, name="rms_pre_mix")
    zm = mm(h1, w_main, out_dtype=BF16, tm=2048, tn=512, tk=1024, name="in_proj")
    zf = mm(h1, w_f, out_dtype=F32, tm=2048, tn=F_PAD, tk=1024, name="in_proj_forget")
    f_row, sg_row = _fox_prep(zf[:, :N_HEADS].T, b_forget.reshape(N_HEADS, 1))
    f_cols = jnp.pad(f_row.T, ((0, 0), (0, LANE - N_HEADS)))
    q_slots, k_slots, v_slots = _fox_pack_fwd(zm, f_cols)
    ya, lse_a = _fox_fwd(q_slots, k_slots, v_slots)
    qkv_d = _rope_fwd(zm, tabs)
    dil = [_dil_fwd(qkv_d, d) for _, d in DIL_PATTERNS]
    yb, lse_b = _dil_merge([o for o, _ in dil], [l for _, l in dil])
    w_oa, w_ob, w_out, w_up, conv_w, w_down = late_weights(yb)
    pa, pb, mixed = _mix_fwd(ya, yb, w_oa, w_ob, zm)
    y1, x2, h2 = _out_fwd(mixed, w_out, x, g_post_mix, g_pre_ffn)
    u = mm(h2, w_up, out_dtype=BF16, tm=2048, tn=512, tk=1024, name="up_proj")
    m = _ffn_mid_fwd(u, conv_w, conv_b)
    dout, dy2, gg_post_ffn, sq_err = _down_fwd(m, w_down, x2, g_post_ffn, target)

    g_w_down = mm(m, dy2, ta=True, out_dtype=BF16, tm=D_FF // 2, tn=1024, tk=512, name="grad_w_down")
    dm = mm(dy2, w_down, tb=True, out_dtype=BF16, tm=2048, tn=D_FF // 2, tk=1024, name="d_ffn_mid")
    du_a, du_b, gcw_a, gcw_b, gcb_a, gcb_b = _ffn_mid_bwd(dm, u, conv_w, conv_b)
    g_w_up = [mm(h2, t, ta=True, out_dtype=BF16, tm=1024, tn=D_FF // 2, tk=512, name=f"grad_w_up_{s}")
              for s, t in (("a", du_a), ("b", du_b))]
    dh2 = [mm(t, w_up, tb=True, out_dtype=F32, tm=1024, tn=1024, tk=D_FF, b_k_off=o, name=f"d_h2_{s}")
           for s, t, o in (("a", du_a, 0), ("b", du_b, 1))]
    dx2, gg_pre_ffn = _rms_bwd(dh2, x2, g_pre_ffn, dout, out_dtype=F32, name="rms_pre_ffn_bwd")
    tok = ffn_grads_ready(dict(w_down=g_w_down, w_up=jnp.concatenate(g_w_up, axis=1),
                               conv_w=jnp.concatenate([gcw_a, gcw_b], axis=1)))

    dy1, gg_post_mix = _rms_bwd([dx2], y1, g_post_mix + tok, None, out_dtype=BF16, name="rms_post_mix_bwd")
    g_w_out = mm(mixed, dy1, ta=True, out_dtype=BF16, tm=1024, tn=1024, tk=512, name="grad_w_out")
    dmix = mm(dy1, w_out, tb=True, out_dtype=BF16, tm=2048, tn=1024, tk=1024, name="d_mixed")
    dpa, dpb, dgates = _gate_bwd(dmix, zm, pa, pb)
    g_w_oa = mm(ya, dpa, ta=True, out_dtype=BF16, tm=512, tn=1024, tk=1024, name="grad_w_o_fox")
    g_w_ob = mm(yb, dpb, ta=True, out_dtype=BF16, tm=512, tn=1024, tk=1024, name="grad_w_o_dil")
    tok = proj_grads_ready(dict(w_o_fox=g_w_oa, w_o_dil=g_w_ob, w_out=g_w_out))
    dya = mm(dpa, w_oa, tb=True, out_dtype=BF16, tm=2048, tn=512, tk=1024, name="d_y_fox")
    dyb = mm(dpb, w_ob, tb=True, out_dtype=BF16, tm=2048, tn=512, tk=1024, name="d_y_dil")

    qb_slots, do_slots = _fox_pack_bwd(zm, f_cols + tok, lse_a, ya, dya)
    d_fox, df_cols = _fox_unpack(*_fox_bwd(qb_slots, k_slots, v_slots, do_slots))
    dfa_t, g_b_forget = _fox_post_bwd(df_cols[:, :N_HEADS].T, sg_row)

    delta_b = _attn_delta(yb, dyb, name="delta_dil")
    dil_g = [_dil_bwd(qkv_d, lse_b, delta_b, dyb, d) for _, d in DIL_PATTERNS]
    d_dil = _dil_grad_combine([g[0] for g in dil_g], [g[1] for g in dil_g], [g[2] for g in dil_g], tabs)

    dz = jnp.concatenate([d_fox, d_dil, dgates], axis=1)
    dzf = jnp.pad(dfa_t.T, ((0, 0), (0, F_PAD - N_HEADS)))
    g_w_main = mm(h1, dz, ta=True, out_dtype=BF16, tm=1024, tn=Z_MAIN // 4, tk=512, name="grad_w_in")
    g_w_f = mm(h1, dzf, ta=True, out_dtype=BF16, tm=1024, tn=F_PAD, tk=1024, name="grad_w_in_forget")
    tok = mixer_grads_ready(dict(w_main=g_w_main, w_f=g_w_f))
    dh1 = [mm(dz, w_main, tb=True, out_dtype=F32, tm=2048, tn=1024, tk=1024, name="d_h1"),
           mm(dzf + tok, w_f, tb=True, out_dtype=F32, tm=2048, tn=1024, tk=F_PAD, name="d_h1_forget")]
    grad_x, gg_pre_mix = _rms_bwd(dh1, x, g_pre_mix, dx2, out_dtype=F32, name="rms_pre_mix_bwd")

    grads = dict(
        b_forget=g_b_forget.reshape(1, N_HEADS), conv_b=jnp.concatenate([gcb_a, gcb_b], axis=1),
        g_pre_mix=gg_pre_mix, g_post_mix=gg_post_mix, g_pre_ffn=gg_pre_ffn, g_post_ffn=gg_post_ffn)
    return sq_err[0, 0], grad_x, grads


def _exchange(arrays, scatter, *, name):
    n = len(arrays)

    def body(*refs):
        ins, outs = refs[:n], refs[n:2 * n]
        send_sems, recv_sems, local_sems = refs[2 * n:]
        x, y, c = lax.axis_index("x"), lax.axis_index("y"), lax.axis_index("c")
        me = 4 * x + 2 * y + c
        peers = []
        for k in range(1, N_DEV):
            px = 1 - x if k & 4 else x
            py = 1 - y if k & 2 else y
            pc = 1 - c if k & 1 else c
            peers.append(((px, py, pc), 4 * px + 2 * py + pc))

        def remote(a, k):
            dev, slot = peers[k]
            return pltpu.make_async_remote_copy(
                src_ref=ins[a].at[slot] if scatter else ins[a], dst_ref=outs[a].at[me],
                send_sem=send_sems.at[a, k], recv_sem=recv_sems.at[a, k],
                device_id=dev, device_id_type=MESH_ID)

        def landed(a, k):
            dev, slot = peers[k]
            return pltpu.make_async_remote_copy(
                src_ref=outs[a].at[slot], dst_ref=outs[a].at[slot],
                send_sem=send_sems.at[a, k], recv_sem=recv_sems.at[a, k],
                device_id=dev, device_id_type=MESH_ID)

        own = [pltpu.make_async_copy(ins[a].at[me] if scatter else ins[a], outs[a].at[me], local_sems.at[a])
               for a in range(n)]
        copies = [remote(a, k) for k in range(N_DEV - 1) for a in range(n)]
        for cp in own + copies:
            cp.start()
        for k in range(N_DEV - 1):
            for a in range(n):
                landed(a, k).wait_recv()
        for cp in copies:
            cp.wait_send()
        for cp in own:
            cp.wait()

    out_shape = [_sds(((N_DEV,) + a.shape[-2:]), a.dtype) for a in arrays]
    return pl.pallas_call(
        body, name=name, in_specs=[ANY] * n, out_specs=[ANY] * n, out_shape=out_shape,
        scratch_shapes=[pltpu.SemaphoreType.DMA((n, N_DEV - 1)), pltpu.SemaphoreType.DMA((n, N_DEV - 1)),
                        pltpu.SemaphoreType.DMA((n,))],
    )(*arrays)


def _gather_two_level(shard, *, name):
    def body(x_ref, out_ref, send_sems, recv_sems, local_sem):
        x, y, c = lax.axis_index("x"), lax.axis_index("y"), lax.axis_index("c")
        me, sibling = (x, y, c), (x, y, 1 - c)
        chips = [(1 - x, y), (x, 1 - y), (1 - x, 1 - y)]

        def slot(px, py, pc):
            return out_ref.at[4 * px + 2 * py + pc]

        def copy(k, block, to, src=None):
            return pltpu.make_async_remote_copy(
                src_ref=slot(*block) if src is None else src, dst_ref=slot(*block),
                send_sem=send_sems.at[k], recv_sem=recv_sems.at[k], device_id=to, device_id_type=MESH_ID)

        mine = pltpu.make_async_copy(x_ref, slot(*me), local_sem)
        mine.start()
        first = [copy(0, me, sibling, src=x_ref)]
        first += [copy(1 + j, me, (*chip, c), src=x_ref) for j, chip in enumerate(chips)]
        for cp in first:
            cp.start()
        passed = [copy(4 + j, (*chip, c), sibling) for j, chip in enumerate(chips)]
        for j, chip in enumerate(chips):
            copy(1 + j, (*chip, c), me).wait_recv()
            passed[j].start()
        copy(0, sibling, me).wait_recv()
        for j, chip in enumerate(chips):
            copy(4 + j, (*chip, 1 - c), me).wait_recv()
        for cp in first + passed:
            cp.wait_send()
        mine.wait()

    return pl.pallas_call(
        body, name=name, in_specs=[ANY], out_specs=ANY, out_shape=_sds((N_DEV,) + shard.shape, shard.dtype),
        scratch_shapes=[pltpu.SemaphoreType.DMA((N_DEV - 1,)), pltpu.SemaphoreType.DMA((N_DEV - 1,)),
                        pltpu.SemaphoreType.DMA],
    )(shard)


def _peers():
    x, y, c = lax.axis_index("x"), lax.axis_index("y"), lax.axis_index("c")
    out = []
    for k in range(1, N_DEV):
        px = 1 - x if k & 4 else x
        py = 1 - y if k & 2 else y
        pc = 1 - c if k & 1 else c
        out.append(((px, py, pc), 4 * px + 2 * py + pc))
    return 4 * x + 2 * y + c, out


HBM = pl.BlockSpec(memory_space=pltpu.HBM)
SEM = pl.BlockSpec(memory_space=pltpu.SEMAPHORE)
DATAFLOW = pltpu.SideEffectType.DATAFLOW_SIDE_EFFECTING


def _split_copy(srcs, lands, send_sems, recv_sems, scatter, a, k, me, peers, incoming=False):
    dev, slot = peers[k]
    if incoming:
        src = dst = lands[a].at[slot]
    else:
        src, dst = (srcs[a].at[slot] if scatter else srcs[a]), lands[a].at[me]
    return pltpu.make_async_remote_copy(
        src_ref=src, dst_ref=dst, send_sem=send_sems.at[a * (N_DEV - 1) + k], recv_sem=recv_sems.at[a * (N_DEV - 1) + k],
        device_id=dev, device_id_type=MESH_ID)


def _exchange_start(arrays, scatter, *, name):
    n = len(arrays)

    def body(*refs):
        srcs, lands = refs[:n], refs[n:2 * n]
        send_sems, recv_sems = refs[2 * n], refs[2 * n + 1]
        token = refs[-1]
        me, peers = _peers()
        for k in range(N_DEV - 1):
            for a in range(n):
                _split_copy(srcs, lands, send_sems, recv_sems, scatter, a, k, me, peers).start()
        token[...] = jnp.zeros_like(token)

    land_shapes = [((N_DEV,) + a.shape[-2:], a.dtype) for a in arrays]
    sems = pltpu.SemaphoreType.DMA((n * (N_DEV - 1),))
    outs = pl.pallas_call(
        body, name=name,
        out_shape=(sems, sems, *[pltpu.HBM(a.shape, a.dtype) for a in arrays],
                   *[pltpu.HBM(s, d) for s, d in land_shapes], _sds((SUBLANE, LANE), F32)),
        in_specs=[HBM] * (2 * n),
        out_specs=(SEM, SEM, *[HBM] * (2 * n), pl.BlockSpec(memory_space=pltpu.VMEM)),
        input_output_aliases={i: 2 + i for i in range(2 * n)},
        compiler_params=pltpu.CompilerParams(has_side_effects=DATAFLOW),
    )(*[pltpu.with_memory_space_constraint(a, pltpu.HBM) for a in arrays],
      *[pltpu.with_memory_space_constraint(lax.empty(s, d), pltpu.HBM) for s, d in land_shapes])
    return (outs[0], outs[1], outs[2:2 + n], outs[2 + n:2 + 2 * n], scatter), outs[-1]


def _exchange_wait(handles, after, *, name):
    send_sems, recv_sems, srcs, lands, scatter = handles
    n = len(srcs)

    def body(*refs):
        src_refs, land_refs = refs[:n], refs[n:2 * n]
        send_ref, recv_ref = refs[2 * n], refs[2 * n + 1]
        me, peers = _peers()
        for k in range(N_DEV - 1):
            for a in range(n):
                _split_copy(src_refs, land_refs, send_ref, recv_ref, scatter, a, k, me, peers).wait_send()
                _split_copy(src_refs, land_refs, send_ref, recv_ref, scatter, a, k, me, peers, True).wait_recv()

    outs = pl.pallas_call(
        body, name=name,
        out_shape=tuple(pltpu.HBM(t.shape, t.dtype) for t in (*srcs, *lands)),
        in_specs=[HBM] * (2 * n) + [SEM, SEM, pl.BlockSpec(memory_space=pl.ANY)],
        out_specs=tuple([HBM] * (2 * n)),
        input_output_aliases={i: i for i in range(2 * n)},
        compiler_params=pltpu.CompilerParams(has_side_effects=DATAFLOW),
    )(*srcs, *lands, send_sems, recv_sems, after)
    return _with_own_slot(outs[n:], outs[:n], scatter)


def _with_own_slot(landed, own, scatter):
    me = 4 * lax.axis_index("x") + 2 * lax.axis_index("y") + lax.axis_index("c")
    out = []
    for buf, src in zip(landed, own):
        mine = lax.dynamic_index_in_dim(src, me, 0, keepdims=False) if scatter else src
        out.append(lax.dynamic_update_index_in_dim(buf, mine, me, 0))
    return out


def _adamw(parts, w, m, v, *, name, tm):
    r, c = w.shape
    assert r % tm == 0

    def body(p_ref, w_ref, m_ref, v_ref, g_ref, d_ref, nm_ref, nv_ref):
        _adamw_update(p_ref, w_ref, m_ref, v_ref, g_ref, d_ref, nm_ref, nv_ref)

    blk = pl.BlockSpec((tm, c), lambda i: (i, 0))
    return pl.pallas_call(
        body, name=name, grid=(r // tm,),
        in_specs=[pl.BlockSpec((N_DEV, tm, c), lambda i: (0, i, 0)), blk, blk, blk],
        out_specs=[blk] * 4, out_shape=[_sds((r, c), F32)] * 4,
        compiler_params=_params("parallel"),
    )(parts, w, m, v)


def _adamw_update(p_ref, w_ref, m_ref, v_ref, g_ref, d_ref, nm_ref, nv_ref):
    g = p_ref[0].astype(F32)
    for s in range(1, N_DEV):
        g = g + p_ref[s].astype(F32)
    g_ref[...] = g
    m_new = ADAM_B1 * m_ref[...] + (1.0 - ADAM_B1) * g
    v_new = ADAM_B2 * v_ref[...] + (1.0 - ADAM_B2) * (g * g)
    nm_ref[...] = m_new
    nv_ref[...] = v_new
    m_hat = m_new / (1.0 - ADAM_B1 ** ADAM_STEP)
    v_hat = v_new / (1.0 - ADAM_B2 ** ADAM_STEP)
    d_ref[...] = -ADAM_LR * (m_hat / (jnp.sqrt(v_hat) + ADAM_EPS) + ADAM_WD * w_ref[...])


SMALL = ("g_pre_mix", "b_forget", "g_post_mix", "g_pre_ffn", "conv_b", "g_post_ffn")


def _adamw_small(parts, ws, ms, vs):
    n = len(ws)

    def body(*refs):
        ins, outs = refs[:4 * n], refs[4 * n:]
        for i in range(n):
            _adamw_update(ins[i], ins[n + i], ins[2 * n + i], ins[3 * n + i], *outs[4 * i:4 * i + 4])

    res = pl.pallas_call(
        body, name="adamw_small", out_shape=[_sds(w.shape, F32) for w in ws for _ in range(4)],
        compiler_params=pltpu.CompilerParams(vmem_limit_bytes=VMEM_LIMIT),
    )(*parts, *ws, *ms, *vs)
    return [res[4 * i:4 * i + 4] for i in range(n)]


def kernel(x, g_pre_mix, w_in, b_forget, w_o_fox, w_o_dil, w_out, g_post_mix, g_pre_ffn, w_up, conv_w, conv_b, w_down, g_post_ffn, loss_target, m_g_pre_mix, m_w_in, m_b_forget, m_w_o_fox, m_w_o_dil, m_w_out, m_g_post_mix, m_g_pre_ffn, m_w_up, m_conv_w, m_conv_b, m_w_down, m_g_post_ffn, v_g_pre_mix, v_w_in, v_b_forget, v_w_o_fox, v_w_o_dil, v_w_out, v_g_post_mix, v_g_pre_ffn, v_w_up, v_conv_w, v_conv_b, v_w_down, v_g_post_ffn):
    names = ("g_pre_mix", "w_in", "b_forget", "w_o_fox", "w_o_dil", "w_out", "g_post_mix", "g_pre_ffn",
             "w_up", "conv_w", "conv_b", "w_down", "g_post_ffn")
    w = dict(g_pre_mix=g_pre_mix, w_in=w_in, b_forget=b_forget, w_o_fox=w_o_fox, w_o_dil=w_o_dil, w_out=w_out,
             g_post_mix=g_post_mix, g_pre_ffn=g_pre_ffn, w_up=w_up, conv_w=conv_w, conv_b=conv_b, w_down=w_down,
             g_post_ffn=g_post_ffn)
    m = dict(g_pre_mix=m_g_pre_mix, w_in=m_w_in, b_forget=m_b_forget, w_o_fox=m_w_o_fox, w_o_dil=m_w_o_dil,
             w_out=m_w_out, g_post_mix=m_g_post_mix, g_pre_ffn=m_g_pre_ffn, w_up=m_w_up, conv_w=m_conv_w,
             conv_b=m_conv_b, w_down=m_w_down, g_post_ffn=m_g_post_ffn)
    v = dict(g_pre_mix=v_g_pre_mix, w_in=v_w_in, b_forget=v_b_forget, w_o_fox=v_w_o_fox, w_o_dil=v_w_o_dil,
             w_out=v_w_out, g_post_mix=v_g_post_mix, g_pre_ffn=v_g_pre_ffn, w_up=v_w_up, conv_w=v_conv_w,
             conv_b=v_conv_b, w_down=v_w_down, g_post_ffn=v_g_post_ffn)
    sharded = ("w_in", "w_o_fox", "w_o_dil", "w_out", "w_up", "w_down", "conv_w")
    wire = lambda n: F32 if n == "conv_w" else BF16

    by_cols = lambda t: jnp.transpose(t, (1, 0, 2)).reshape(t.shape[1], N_DEV * t.shape[2])
    by_rows = lambda t: t.reshape(N_DEV * t.shape[1], t.shape[2])
    col_slots = lambda t: jnp.transpose(t.reshape(t.shape[0], N_DEV, t.shape[1] // N_DEV), (1, 0, 2))
    row_slots = lambda t: t.reshape(N_DEV, t.shape[0] // N_DEV, t.shape[1])
    to_slots = lambda n, t: (row_slots if n in ("w_out", "w_down") else col_slots)(t).astype(wire(n))
    shard = lambda n: w[n][0].astype(wire(n))
    f_lo, f_hi = 3 * ATT_W, 3 * ATT_W + N_HEADS

    w_in_full = by_cols(_gather_two_level(shard("w_in"), name="gather_w_in"))
    w_main = jnp.concatenate([w_in_full[:, :f_lo], w_in_full[:, f_hi:]], axis=1)
    w_f = jnp.pad(w_in_full[:, f_lo:f_hi], ((0, 0), (0, F_PAD - N_HEADS)))
    late = ("w_o_fox", "w_o_dil", "w_out", "w_up", "conv_w", "w_down")
    order = jnp.minimum(jnp.abs(w_in_full[0, 0].astype(F32)), 0.0)
    late_handles, late_tok = _exchange_start(
        [shard(n) + order.astype(wire(n)) if n == "conv_w" else shard(n) for n in late], False,
        name="gather_late_start")

    def late_weights(after):
        got = dict(zip(late, _exchange_wait(late_handles, after, name="gather_late_wait")))
        return (by_cols(got["w_o_fox"]), by_cols(got["w_o_dil"]), by_rows(got["w_out"]), by_cols(got["w_up"]),
                by_cols(got["conv_w"]), by_rows(got["w_down"]))

    pending = {}

    def ffn_grads_ready(g):
        pending["ffn"] = _exchange_start([to_slots(n, g[n]) for n in ("w_down", "w_up", "conv_w")], True,
                                         name="scatter_ffn_start")
        return pending["ffn"][1][0, 0]

    def proj_grads_ready(g):
        pending["proj"] = _exchange_start([to_slots(n, g[n]) for n in ("w_o_fox", "w_o_dil", "w_out")], True,
                                          name="scatter_proj_start")
        return pending["proj"][1][0, 0]

    def mixer_grads_ready(g):
        g_w_in = jnp.concatenate([g["w_main"][:, :f_lo], g["w_f"][:, :N_HEADS], g["w_main"][:, f_lo:]], axis=1)
        pending["w_in"] = _exchange_start([to_slots("w_in", g_w_in)], True, name="scatter_w_in_start")
        return pending["w_in"][1][0, 0]

    sq_err, grad_x, g = _local_step(
        x[0], loss_target[0], w_main, w_f, b_forget, conv_b, g_pre_mix + late_tok[0, 0], g_post_mix, g_pre_ffn,
        g_post_ffn, late_weights, ffn_grads_ready, proj_grads_ready, mixer_grads_ready)
    loss = lax.psum(0.5 * sq_err / D_MODEL, ("x", "y", "c"))

    tiles = dict(w_in=256, w_o_fox=512, w_o_dil=512, w_out=128, w_up=256, w_down=176, conv_w=3)
    adam = lambda n, p: _adamw(p, w[n][0], m[n][0], v[n][0], name=f"adamw_{n}", tm=tiles[n])
    res = {}
    for key, group in (("ffn", ("w_down", "w_up", "conv_w")), ("proj", ("w_o_fox", "w_o_dil", "w_out"))):
        landed = _exchange_wait(pending[key][0], grad_x, name=f"scatter_{key}_wait")
        res.update({n: adam(n, p) for n, p in zip(group, landed)})
    small_parts = _exchange([g[n] for n in SMALL], False, name="gather_small_grads")
    done = res["w_up"][3]
    res["w_in"] = adam("w_in", _exchange_wait(pending["w_in"][0], done, name="scatter_w_in_wait")[0])
    small = dict(zip(SMALL, _adamw_small(small_parts, *[[t[n] for n in SMALL] for t in (w, m, v)])))
    out = [[(res[n][k][None] if n in sharded else small[n][k]) for n in names] for k in range(4)]
    return (loss, grad_x[None], *out[0], *out[1], *out[2], *out[3])
```

```python
import functools
import math

import jax
import jax.numpy as jnp
import numpy as np
from jax import lax
from jax.experimental import pallas as pl
from jax.experimental.pallas import tpu as pltpu

F32 = jnp.float32
BF16 = jnp.bfloat16

SEQ = 4096
D_MODEL = 1024
N_HEADS = 8
HEAD_DIM = 64
ATT_W = N_HEADS * HEAD_DIM
D_FF = 2816
Z_MAIN = 5120
F_PAD = 128
ROPE_DIM = 16
ROPE_THETA = 500000.0
RMS_EPS = 1e-6
NEG_INF = -1e30
SCALE = 1.0 / math.sqrt(HEAD_DIM)
DIL_PATTERNS = ((128, 1), (512, 4), (2048, 16))
DIL_BLK = 128
N_DEV = 8

ADAM_LR = 0.001
ADAM_B1 = 0.9
ADAM_B2 = 0.999
ADAM_EPS = 1e-08
ADAM_WD = 0.01
ADAM_STEP = 10

LANE = 128
SUBLANE = 8
VMEM_LIMIT = 56 * 1024 * 1024
MESH_ID = pl.DeviceIdType.MESH
ANY = pl.BlockSpec(memory_space=pl.ANY)


def _params(*sem):
    return pltpu.CompilerParams(dimension_semantics=sem, vmem_limit_bytes=VMEM_LIMIT)


def _sds(shape, dtype):
    return jax.ShapeDtypeStruct(shape, dtype)


def _matmul(a, b, *, ta=False, tb=False, out_dtype, tm, tn, tk, name, b_k_off=0):
    if ta:
        kk, m = a.shape
    else:
        m, kk = a.shape
    n = b.shape[0] if tb else b.shape[1]
    tm, tn, tk = min(tm, m), min(tn, n), min(tk, kk)
    assert (b.shape[1] if tb else b.shape[0]) >= b_k_off * tk + kk
    assert m % tm == 0 and n % tn == 0 and kk % tk == 0, (name, m, n, kk, tm, tn, tk)
    nk = kk // tk
    dims = (((0 if ta else 1,), (1 if tb else 0,)), ((), ()))

    def body(a_ref, b_ref, o_ref, *scratch):
        p = lax.dot_general(a_ref[...].astype(BF16), b_ref[...].astype(BF16), dims,
                            preferred_element_type=F32)
        if nk == 1:
            o_ref[...] = p.astype(o_ref.dtype)
        else:
            acc = scratch[0]
            k = pl.program_id(2)

            @pl.when(k == 0)
            def _():
                acc[...] = p

            @pl.when(k > 0)
            def _():
                acc[...] += p

            @pl.when(k == nk - 1)
            def _():
                o_ref[...] = acc[...].astype(o_ref.dtype)

    a_spec = (pl.BlockSpec((tk, tm), lambda i, j, k: (k, i)) if ta
              else pl.BlockSpec((tm, tk), lambda i, j, k: (i, k)))
    b_spec = (pl.BlockSpec((tn, tk), lambda i, j, k: (j, k + b_k_off)) if tb
              else pl.BlockSpec((tk, tn), lambda i, j, k: (k + b_k_off, j)))
    return pl.pallas_call(
        body, name=name, grid=(m // tm, n // tn, nk),
        in_specs=[a_spec, b_spec],
        out_specs=pl.BlockSpec((tm, tn), lambda i, j, k: (i, j)),
        out_shape=_sds((m, n), out_dtype),
        scratch_shapes=[pltpu.VMEM((tm, tn), F32)] if nk > 1 else [],
        compiler_params=_params("parallel", "parallel", "arbitrary"),
    )(a, b)


def _rms_fwd(x, g, *, name, tm=512):
    def body(x_ref, g_ref, h_ref):
        xv = x_ref[...]
        r = lax.rsqrt(jnp.mean(xv * xv, axis=-1, keepdims=True) + RMS_EPS)
        h_ref[...] = (xv * r * g_ref[...]).astype(h_ref.dtype)

    return pl.pallas_call(
        body, name=name, grid=(SEQ // tm,),
        in_specs=[pl.BlockSpec((tm, D_MODEL), lambda i: (i, 0)), pl.BlockSpec((1, D_MODEL), lambda i: (0, 0))],
        out_specs=pl.BlockSpec((tm, D_MODEL), lambda i: (i, 0)),
        out_shape=_sds((SEQ, D_MODEL), BF16),
        compiler_params=_params("parallel"),
    )(x, g)


def _rms_bwd(dh_parts, xin, g, dres, *, out_dtype, name, tm=512):
    n_parts = len(dh_parts)
    has_res = dres is not None

    def body(*refs):
        parts = refs[:n_parts]
        x_ref, g_ref = refs[n_parts], refs[n_parts + 1]
        res_ref = refs[n_parts + 2] if has_res else None
        o_ref, gg_ref = refs[-2], refs[-1]
        dh = parts[0][...].astype(F32)
        for p in parts[1:]:
            dh = dh + p[...].astype(F32)
        xv = x_ref[...]
        r = lax.rsqrt(jnp.mean(xv * xv, axis=-1, keepdims=True) + RMS_EPS)
        xn = xv * r

        @pl.when(pl.program_id(0) == 0)
        def _():
            gg_ref[...] = jnp.zeros_like(gg_ref)

        gg_ref[...] += jnp.sum(dh * xn, axis=0, keepdims=True)
        dxn = dh * g_ref[...]
        dx = r * (dxn - xn * jnp.mean(dxn * xn, axis=-1, keepdims=True))
        if has_res:
            dx = dx + res_ref[...]
        o_ref[...] = dx.astype(o_ref.dtype)

    row = pl.BlockSpec((tm, D_MODEL), lambda i: (i, 0))
    vec = pl.BlockSpec((1, D_MODEL), lambda i: (0, 0))
    args = list(dh_parts) + [xin, g] + ([dres] if has_res else [])
    return pl.pallas_call(
        body, name=name, grid=(SEQ // tm,),
        in_specs=[row] * n_parts + [row, vec] + ([row] if has_res else []),
        out_specs=[row, vec],
        out_shape=[_sds((SEQ, D_MODEL), out_dtype), _sds((1, D_MODEL), F32)],
        compiler_params=_params("arbitrary"),
    )(*args)


def _rms_pair_bwd(dh_parts, x2, g_pre, dres, y1, g_post, *, tm=512):
    n_parts = len(dh_parts)

    def norm_bwd(dh, xin, g_ref, gg_ref):
        r = lax.rsqrt(jnp.mean(xin * xin, axis=-1, keepdims=True) + RMS_EPS)
        xn = xin * r
        gg_ref[...] += jnp.sum(dh * xn, axis=0, keepdims=True)
        dxn = dh * g_ref[...]
        return r * (dxn - xn * jnp.mean(dxn * xn, axis=-1, keepdims=True))

    def body(*refs):
        parts = refs[:n_parts]
        x2_ref, gpre_ref, res_ref, y1_ref, gpost_ref, dx2_ref, dy1_ref, ggpre_ref, ggpost_ref = refs[n_parts:]

        @pl.when(pl.program_id(0) == 0)
        def _():
            ggpre_ref[...] = jnp.zeros_like(ggpre_ref)
            ggpost_ref[...] = jnp.zeros_like(ggpost_ref)

        dh = parts[0][...].astype(F32)
        for p in parts[1:]:
            dh = dh + p[...].astype(F32)
        dx2 = res_ref[...] + norm_bwd(dh, x2_ref[...], gpre_ref, ggpre_ref)
        dx2_ref[...] = dx2
        dy1_ref[...] = norm_bwd(dx2, y1_ref[...], gpost_ref, ggpost_ref).astype(dy1_ref.dtype)

    row = pl.BlockSpec((tm, D_MODEL), lambda i: (i, 0))
    vec = pl.BlockSpec((1, D_MODEL), lambda i: (0, 0))
    return pl.pallas_call(
        body, name="rms_pair_bwd", grid=(SEQ // tm,),
        in_specs=[row] * n_parts + [row, vec, row, row, vec],
        out_specs=[row, row, vec, vec],
        out_shape=[_sds((SEQ, D_MODEL), F32), _sds((SEQ, D_MODEL), BF16), _sds((1, D_MODEL), F32),
                   _sds((1, D_MODEL), F32)],
        compiler_params=_params("arbitrary"),
    )(*dh_parts, x2, g_pre, dres, y1, g_post)


SCAN_BLK = 512


def _split_dot(v, tri):
    hi = v.astype(BF16)
    r1 = v - hi.astype(F32)
    mid = r1.astype(BF16)
    lo = (r1 - mid.astype(F32)).astype(BF16)
    dot = functools.partial(jnp.dot, preferred_element_type=F32)
    return dot(hi, tri) + dot(mid, tri) + dot(lo, tri)


def _fox_prep(fa_t, b_col):
    nblk = SEQ // SCAN_BLK

    def body(fa_ref, b_ref, f_ref, sg_ref):
        row = lax.broadcasted_iota(jnp.int32, (SCAN_BLK, SCAN_BLK), 0)
        col = lax.broadcasted_iota(jnp.int32, (SCAN_BLK, SCAN_BLK), 1)
        upper = (row <= col).astype(BF16)
        carry = jnp.zeros((N_HEADS, 1), F32)
        for blk in range(nblk):
            sl = pl.ds(blk * SCAN_BLK, SCAN_BLK)
            xx = fa_ref[:, sl] + b_ref[...]
            e = jnp.exp(-jnp.abs(xx))
            logf = jnp.minimum(xx, 0.0) - jnp.log(1.0 + e)
            sg_ref[:, sl] = jnp.where(xx >= 0.0, e, 1.0) / (1.0 + e)
            c = _split_dot(logf, upper) + carry
            f_ref[:, sl] = c
            carry = c[:, SCAN_BLK - 1:SCAN_BLK]

    return pl.pallas_call(
        body, name="fox_prep",
        out_shape=[_sds((N_HEADS, SEQ), F32), _sds((N_HEADS, SEQ), F32)],
        compiler_params=pltpu.CompilerParams(vmem_limit_bytes=VMEM_LIMIT),
    )(fa_t, b_col)


def _fox_post_bwd(df_t, sg_t):
    nblk = SEQ // SCAN_BLK

    def body(df_ref, sg_ref, dfa_ref, gb_ref):
        row = lax.broadcasted_iota(jnp.int32, (SCAN_BLK, SCAN_BLK), 0)
        col = lax.broadcasted_iota(jnp.int32, (SCAN_BLK, SCAN_BLK), 1)
        lower = (row >= col).astype(BF16)
        carry = jnp.zeros((N_HEADS, 1), F32)
        gb = jnp.zeros((N_HEADS, 1), F32)
        for blk in reversed(range(nblk)):
            sl = pl.ds(blk * SCAN_BLK, SCAN_BLK)
            c = _split_dot(df_ref[:, sl], lower) + carry
            carry = c[:, 0:1]
            dfa = c * sg_ref[:, sl]
            dfa_ref[:, sl] = dfa
            gb = gb + jnp.sum(dfa, axis=1, keepdims=True)
        gb_ref[...] = gb

    return pl.pallas_call(
        body, name="fox_post_bwd",
        out_shape=[_sds((N_HEADS, SEQ), F32), _sds((N_HEADS, 1), F32)],
        compiler_params=pltpu.CompilerParams(vmem_limit_bytes=VMEM_LIMIT),
    )(df_t, sg_t)


FOX_T = 512
NT_DIMS = (((1,), (1,)), ((), ()))
TN_DIMS = (((0,), (0,)), ((), ()))


def _head(ref_or_val, h):
    return ref_or_val[:, h * HEAD_DIM:(h + 1) * HEAD_DIM]


def _split3(v):
    hi = v.astype(BF16).astype(F32)
    r1 = v - hi
    mid = r1.astype(BF16).astype(F32)
    return hi, mid, (r1 - mid).astype(BF16).astype(F32)


def _aux_lanes(rows, terms):
    lane = lax.broadcasted_iota(jnp.int32, (rows, HEAD_DIM), 1)
    out = jnp.zeros((rows, HEAD_DIM), F32)
    for i, t in enumerate(terms):
        out = jnp.where(lane == i, t, out)
    return out


SLOT = 2 * HEAD_DIM
N_SPLIT = 3


def _slot(ref, h):
    return ref[:, h * SLOT:(h + 1) * SLOT]


def _fox_pack_fwd(zm, f_cols, *, tm=512):
    def body(q_ref, k_ref, v_ref, f_ref, qs_ref, ks_ref, vs_ref):
        ones = jnp.ones((tm, HEAD_DIM), BF16)
        for h in range(N_HEADS):
            fh = _split3(f_ref[:, h:h + 1])
            q_aux = _aux_lanes(tm, list(fh) + [1.0] * N_SPLIT)
            k_aux = _aux_lanes(tm, [1.0] * N_SPLIT + [-t for t in fh])
            qs_ref[:, h * SLOT:(h + 1) * SLOT] = jnp.concatenate(
                [(_head(q_ref, h).astype(F32) * SCALE).astype(BF16), q_aux.astype(BF16)], axis=1)
            ks_ref[:, h * SLOT:(h + 1) * SLOT] = jnp.concatenate([_head(k_ref, h), k_aux.astype(BF16)], axis=1)
            vs_ref[:, h * SLOT:(h + 1) * SLOT] = jnp.concatenate([_head(v_ref, h), ones], axis=1)

    col = lambda b: pl.BlockSpec((tm, ATT_W), lambda i: (i, b))
    wide = pl.BlockSpec((tm, N_HEADS * SLOT), lambda i: (i, 0))
    return pl.pallas_call(
        body, name="fox_pack_fwd", grid=(SEQ // tm,),
        in_specs=[col(0), col(1), col(2), pl.BlockSpec((tm, LANE), lambda i: (i, 0))],
        out_specs=[wide] * 3, out_shape=[_sds((SEQ, N_HEADS * SLOT), BF16)] * 3,
        compiler_params=_params("parallel"),
    )(zm, zm, zm, f_cols)


def _fox_pack_bwd(zm, f_cols, lse, o, do, *, tm=512):
    def body(q_ref, f_ref, lse_ref, o_ref, do_ref, qs_ref, ds_ref):
        for h in range(N_HEADS):
            gh = _split3(f_ref[:, h:h + 1] - lse_ref[:, h * HEAD_DIM:h * HEAD_DIM + 1])
            dout = _head(do_ref, h)
            delta = jnp.sum(_head(o_ref, h).astype(F32) * dout.astype(F32), axis=1, keepdims=True)
            q_aux = _aux_lanes(tm, list(gh) + [1.0] * N_SPLIT)
            d_aux = _aux_lanes(tm, [-t for t in _split3(delta)])
            qs_ref[:, h * SLOT:(h + 1) * SLOT] = jnp.concatenate(
                [(_head(q_ref, h).astype(F32) * SCALE).astype(BF16), q_aux.astype(BF16)], axis=1)
            ds_ref[:, h * SLOT:(h + 1) * SLOT] = jnp.concatenate([dout, d_aux.astype(BF16)], axis=1)

    row = pl.BlockSpec((tm, ATT_W), lambda i: (i, 0))
    wide = pl.BlockSpec((tm, N_HEADS * SLOT), lambda i: (i, 0))
    return pl.pallas_call(
        body, name="fox_pack_bwd", grid=(SEQ // tm,),
        in_specs=[row, pl.BlockSpec((tm, LANE), lambda i: (i, 0)), row, row, row],
        out_specs=[wide] * 2, out_shape=[_sds((SEQ, N_HEADS * SLOT), BF16)] * 2,
        compiler_params=_params("parallel"),
    )(zm, f_cols, lse, o, do)


def _causal_pairs(key_major):
    nb = SEQ // FOX_T
    if key_major:
        pairs = [(i, j) for j in range(nb) for i in range(j, nb)]
    else:
        pairs = [(i, j) for i in range(nb) for j in range(i + 1)]
    return (jnp.array([p[0] for p in pairs], jnp.int32), jnp.array([p[1] for p in pairs], jnp.int32), len(pairs))


def _diag_mask():
    row = lax.broadcasted_iota(jnp.int32, (FOX_T, FOX_T), 0)
    col = lax.broadcasted_iota(jnp.int32, (FOX_T, FOX_T), 1)
    return col <= row


def _fox_fwd(q_slots, k_slots, v_slots):
    i_tab, j_tab, n_pairs = _causal_pairs(False)

    def body(i_tab, j_tab, q_ref, k_ref, v_ref, o_ref, lse_ref, m_s, acc_s):
        t = pl.program_id(1)
        i, j = i_tab[t], j_tab[t]

        @pl.when(j == 0)
        def _():
            m_s[...] = jnp.full_like(m_s, NEG_INF)
            acc_s[...] = jnp.zeros_like(acc_s)

        def step(masked):
            scores = [lax.dot_general(_slot(q_ref, h), _slot(k_ref, h), NT_DIMS, preferred_element_type=F32)
                      for h in range(2)]
            probs, alphas = [], []
            for h in range(2):
                s = jnp.where(_diag_mask(), scores[h], NEG_INF) if masked else scores[h]
                m_prev = m_s[h]
                m_new = jnp.maximum(m_prev, jnp.max(s, axis=-1, keepdims=True))
                probs.append(jnp.exp(s - jnp.tile(m_new, (1, FOX_T // LANE))).astype(BF16))
                alphas.append(jnp.exp(m_prev - m_new))
                m_s[h] = m_new
            for h in range(2):
                acc_s[h] = alphas[h] * acc_s[h] + jnp.dot(probs[h], _slot(v_ref, h), preferred_element_type=F32)

        @pl.when(j < i)
        def _():
            step(False)

        @pl.when(j == i)
        def _():
            step(True)
            outs, lses = [], []
            for h in range(2):
                acc = acc_s[h]
                l = acc[:, HEAD_DIM:]
                outs.append(acc[:, :HEAD_DIM] / l)
                lses.append(m_s[h][:, :HEAD_DIM] + jnp.log(l))
            o_ref[...] = jnp.concatenate(outs, axis=1).astype(o_ref.dtype)
            lse_ref[...] = jnp.concatenate(lses, axis=1)

    qspec = pl.BlockSpec((FOX_T, 2 * SLOT), lambda p, t, it, jt: (it[t], p))
    kspec = pl.BlockSpec((FOX_T, 2 * SLOT), lambda p, t, it, jt: (jt[t], p))
    ospec = pl.BlockSpec((FOX_T, LANE), lambda p, t, it, jt: (it[t], p))
    return pl.pallas_call(
        body, name="fox_fwd",
        grid_spec=pltpu.PrefetchScalarGridSpec(
            num_scalar_prefetch=2, grid=(N_HEADS // 2, n_pairs),
            in_specs=[qspec, kspec, kspec], out_specs=[ospec, ospec],
            scratch_shapes=[pltpu.VMEM((2, FOX_T, LANE), F32), pltpu.VMEM((2, FOX_T, SLOT), F32)]),
        out_shape=[_sds((SEQ, ATT_W), BF16), _sds((SEQ, ATT_W), F32)],
        compiler_params=_params("parallel", "arbitrary"),
    )(i_tab, j_tab, q_slots, k_slots, v_slots)


def _fox_bwd(q_slots, k_slots, v_slots, do_slots):
    i_tab, j_tab, n_pairs = _causal_pairs(True)

    def body(i_tab, j_tab, q_ref, k_ref, v_ref, do_ref, dq_ref, dk_ref, dv_ref):
        t = pl.program_id(1)
        i, j = i_tab[t], j_tab[t]

        @pl.when(t == 0)
        def _():
            dq_ref[...] = jnp.zeros_like(dq_ref)

        @pl.when(i == j)
        def _():
            dk_ref[...] = jnp.zeros_like(dk_ref)
            dv_ref[...] = jnp.zeros_like(dv_ref)

        def step(masked):
            rows = pl.ds(pl.multiple_of(i * FOX_T, FOX_T), FOX_T)
            heads = range(2)
            scores = [lax.dot_general(_slot(q_ref, h), _slot(k_ref, h), NT_DIMS, preferred_element_type=F32)
                      for h in heads]
            dps = [lax.dot_general(_slot(do_ref, h), _slot(v_ref, h), NT_DIMS, preferred_element_type=F32)
                   for h in heads]
            ps, dss = [], []
            for h in heads:
                p = jnp.exp(scores[h])
                if masked:
                    p = jnp.where(_diag_mask(), p, 0.0)
                ps.append(p.astype(BF16))
                dss.append((p * dps[h]).astype(BF16))
            for h in heads:
                cols = slice(h * SLOT, (h + 1) * SLOT)
                dv_ref[:, cols] += lax.dot_general(ps[h], _slot(do_ref, h), TN_DIMS, preferred_element_type=F32)
                dk_ref[:, cols] += lax.dot_general(dss[h], _slot(q_ref, h), TN_DIMS, preferred_element_type=F32)
                dq_ref[rows, cols] += jnp.dot(dss[h], _slot(k_ref, h), preferred_element_type=F32)

        @pl.when(i > j)
        def _():
            step(False)

        @pl.when(i == j)
        def _():
            step(True)

    qspec = pl.BlockSpec((FOX_T, 2 * SLOT), lambda p, t, it, jt: (it[t], p))
    kspec = pl.BlockSpec((FOX_T, 2 * SLOT), lambda p, t, it, jt: (jt[t], p))
    return pl.pallas_call(
        body, name="fox_bwd",
        grid_spec=pltpu.PrefetchScalarGridSpec(
            num_scalar_prefetch=2, grid=(N_HEADS // 2, n_pairs),
            in_specs=[qspec, kspec, kspec, qspec],
            out_specs=[pl.BlockSpec((SEQ, 2 * SLOT), lambda p, t, it, jt: (0, p)), kspec, kspec]),
        out_shape=[_sds((SEQ, N_HEADS * SLOT), F32)] * 3,
        compiler_params=_params("arbitrary", "arbitrary"),
    )(i_tab, j_tab, q_slots, k_slots, v_slots, do_slots)


def _fox_unpack(dq_slots, dk_slots, dv_slots, *, tm=512):
    def body(dq_ref, dk_ref, dv_ref, o_ref, df_ref):
        lane = lax.broadcasted_iota(jnp.int32, (tm, LANE), 1)
        df = jnp.zeros((tm, LANE), F32)
        for h in range(N_HEADS):
            lo = h * SLOT
            for part, (ref, mult) in enumerate(((dq_ref, SCALE), (dk_ref, 1.0), (dv_ref, 1.0))):
                o_ref[:, part * ATT_W + h * HEAD_DIM:part * ATT_W + (h + 1) * HEAD_DIM] = (
                    ref[:, lo:lo + HEAD_DIM] * mult).astype(o_ref.dtype)
            rows = dq_ref[:, lo + HEAD_DIM:lo + HEAD_DIM + 1]
            cols = dk_ref[:, lo + HEAD_DIM + N_SPLIT:lo + HEAD_DIM + N_SPLIT + 1]
            df = jnp.where(lane == h, rows - cols, df)
        df_ref[...] = df

    wide = pl.BlockSpec((tm, N_HEADS * SLOT), lambda i: (i, 0))
    return pl.pallas_call(
        body, name="fox_unpack", grid=(SEQ // tm,), in_specs=[wide] * 3,
        out_specs=[pl.BlockSpec((tm, 3 * ATT_W), lambda i: (i, 0)), pl.BlockSpec((tm, LANE), lambda i: (i, 0))],
        out_shape=[_sds((SEQ, 3 * ATT_W), BF16), _sds((SEQ, LANE), F32)],
        compiler_params=_params("parallel"),
    )(dq_slots, dk_slots, dv_slots)


def _attn_delta(o, do, *, name, tm=512):
    def body(o_ref, do_ref, d_ref):
        prod = o_ref[...].astype(F32) * do_ref[...].astype(F32)
        lane = lax.broadcasted_iota(jnp.int32, (tm, LANE), 1)
        out = jnp.zeros((tm, LANE), F32)
        for h in range(N_HEADS):
            out = jnp.where(lane == h, jnp.sum(_head(prod, h), axis=1, keepdims=True), out)
        d_ref[...] = out

    row = pl.BlockSpec((tm, ATT_W), lambda i: (i, 0))
    return pl.pallas_call(
        body, name=name, grid=(SEQ // tm,), in_specs=[row, row],
        out_specs=pl.BlockSpec((tm, LANE), lambda i: (i, 0)), out_shape=_sds((SEQ, LANE), F32),
        compiler_params=_params("parallel"),
    )(o, do)


def _rope_tables():
    half = ROPE_DIM // 2
    inv_freq = np.float32(ROPE_THETA) ** (-np.arange(half, dtype=np.float32) * np.float32(2.0) / np.float32(ROPE_DIM))
    ang = np.arange(SEQ, dtype=np.float32)[:, None] * inv_freq.astype(np.float32)[None, :]
    cos, sin = jnp.asarray(np.cos(ang).astype(np.float32)), jnp.asarray(np.sin(ang).astype(np.float32))
    ones = jnp.ones((SEQ, HEAD_DIM - ROPE_DIM), F32)
    zeros = jnp.zeros((SEQ, HEAD_DIM - ROPE_DIM), F32)
    zh = jnp.zeros((SEQ, half), F32)
    c_tab = jnp.concatenate([cos, cos, ones], axis=1)
    a_tab = jnp.concatenate([-sin, zh, zeros], axis=1)
    b_tab = jnp.concatenate([zh, sin, zeros], axis=1)
    two = lambda t: jnp.concatenate([t, t], axis=1)
    return two(c_tab), two(a_tab), two(b_tab)


def _rotate(x, c_tab, a_tab, b_tab):
    return x * c_tab + pltpu.roll(x, LANE - ROPE_DIM // 2, 1) * a_tab + pltpu.roll(x, ROPE_DIM // 2, 1) * b_tab


def _rope_fwd(zm, tabs, *, tm=512):
    def body(q_ref, k_ref, v_ref, c_ref, a_ref, b_ref, o_ref):
        for part, (x_ref, mult) in enumerate(((q_ref, SCALE), (k_ref, 1.0))):
            for cc in range(ATT_W // LANE):
                sl = slice(cc * LANE, (cc + 1) * LANE)
                rot = _rotate(x_ref[:, sl].astype(F32), c_ref[...], a_ref[...], b_ref[...])
                o_ref[:, part * ATT_W + cc * LANE:part * ATT_W + (cc + 1) * LANE] = (rot * mult).astype(o_ref.dtype)
        o_ref[:, 2 * ATT_W:] = v_ref[...]

    tab = pl.BlockSpec((tm, LANE), lambda i: (i, 0))
    col = lambda b: pl.BlockSpec((tm, ATT_W), lambda i: (i, b))
    return pl.pallas_call(
        body, name="rope_fwd", grid=(SEQ // tm,),
        in_specs=[col(3), col(4), col(5), tab, tab, tab],
        out_specs=pl.BlockSpec((tm, 3 * ATT_W), lambda i: (i, 0)),
        out_shape=_sds((SEQ, 3 * ATT_W), BF16),
        compiler_params=_params("parallel"),
    )(zm, zm, zm, *tabs)


def _dil_grad_combine(dqs, dks, dvs, tabs, *, tm=256):
    def body(*refs):
        q_refs, k_refs, v_refs = refs[0:3], refs[3:6], refs[6:9]
        c_ref, a_ref, b_ref, o_ref = refs[9:]
        total = lambda rs, sl: rs[0][:, sl].astype(F32) + rs[1][:, sl].astype(F32) + rs[2][:, sl].astype(F32)
        for cc in range(ATT_W // LANE):
            sl = slice(cc * LANE, (cc + 1) * LANE)
            for part, rs in enumerate((q_refs, k_refs)):
                o_ref[:, part * ATT_W + cc * LANE:part * ATT_W + (cc + 1) * LANE] = _rotate(
                    total(rs, sl), c_ref[...], -a_ref[...], -b_ref[...]).astype(o_ref.dtype)
            o_ref[:, 2 * ATT_W + cc * LANE:2 * ATT_W + (cc + 1) * LANE] = total(v_refs, sl).astype(o_ref.dtype)

    row = pl.BlockSpec((tm, ATT_W), lambda i: (i, 0))
    tab = pl.BlockSpec((tm, LANE), lambda i: (i, 0))
    return pl.pallas_call(
        body, name="dil_grad_combine", grid=(SEQ // tm,),
        in_specs=[row] * 9 + [tab] * 3,
        out_specs=pl.BlockSpec((tm, 3 * ATT_W), lambda i: (i, 0)),
        out_shape=_sds((SEQ, 3 * ATT_W), BF16),
        compiler_params=_params("parallel"),
    )(*dqs, *dks, *dvs, *tabs)


def _dil_valid(n):
    qi = lax.broadcasted_iota(jnp.int32, (DIL_BLK, 2 * DIL_BLK), 0)
    ki = lax.broadcasted_iota(jnp.int32, (DIL_BLK, 2 * DIL_BLK), 1)
    dist = qi + DIL_BLK - ki
    return (dist >= 0) & (dist <= DIL_BLK) & ((n > 0) | (ki >= DIL_BLK))


def _dil_fwd(qkv, d):
    length = SEQ // d
    nb = length // DIL_BLK
    qkv_v = qkv.reshape(length, d * 3 * ATT_W)

    def body(q_ref, kp_ref, kc_ref, vp_ref, vc_ref, o_ref, lse_ref):
        n = pl.program_id(1)
        ok = _dil_valid(n)
        lane = lax.broadcasted_iota(jnp.int32, (DIL_BLK, LANE), 1)
        lse_all = jnp.zeros((DIL_BLK, LANE), F32)
        heads = range(N_HEADS)
        scores = [lax.dot_general(_head(q_ref, h), jnp.concatenate([_head(kp_ref, h), _head(kc_ref, h)], axis=0),
                                  NT_DIMS, preferred_element_type=F32) for h in heads]
        probs, inv_l = [], []
        for h in heads:
            s = jnp.where(ok, scores[h], NEG_INF)
            m = jnp.max(s, axis=-1, keepdims=True)
            p = jnp.exp(s - m)
            l = jnp.sum(p, axis=-1, keepdims=True)
            probs.append(p.astype(BF16))
            inv_l.append(1.0 / l)
            lse_all = jnp.where(lane == h, m + jnp.log(l), lse_all)
        outs = [jnp.dot(probs[h], jnp.concatenate([_head(vp_ref, h), _head(vc_ref, h)], axis=0),
                        preferred_element_type=F32) * inv_l[h] for h in heads]
        o_ref[...] = jnp.concatenate(outs, axis=1).astype(o_ref.dtype)
        lse_ref[...] = lse_all

    blk = lambda f: pl.BlockSpec((DIL_BLK, ATT_W), f)
    prev = lambda n: jnp.maximum(n - 1, 0)
    o, lse = pl.pallas_call(
        body, name=f"dil_fwd_d{d}", grid=(d, nb),
        in_specs=[blk(lambda r, n: (n, 3 * r)),
                  blk(lambda r, n: (prev(n), 3 * r + 1)), blk(lambda r, n: (n, 3 * r + 1)),
                  blk(lambda r, n: (prev(n), 3 * r + 2)), blk(lambda r, n: (n, 3 * r + 2))],
        out_specs=[blk(lambda r, n: (n, r)), pl.BlockSpec((DIL_BLK, LANE), lambda r, n: (n, r))],
        out_shape=[_sds((length, d * ATT_W), BF16), _sds((length, d * LANE), F32)],
        compiler_params=_params("parallel", "arbitrary"),
    )(qkv_v, qkv_v, qkv_v, qkv_v, qkv_v)
    return o.reshape(SEQ, ATT_W), lse.reshape(SEQ, LANE)


def _dil_merge(os_, lses, *, tm=512):
    def body(o0, o1, o2, l0, l1, l2, y_ref, lse_ref):
        ls = [l0[...], l1[...], l2[...]]
        m = jnp.maximum(jnp.maximum(ls[0], ls[1]), ls[2])
        es = [jnp.exp(l - m) for l in ls]
        tot = es[0] + es[1] + es[2]
        lse_ref[...] = m + jnp.log(tot)
        alphas = [e / tot for e in es]
        outs = []
        for h in range(N_HEADS):
            acc = None
            for g, o_ref in enumerate((o0, o1, o2)):
                term = alphas[g][:, h:h + 1] * _head(o_ref, h).astype(F32)
                acc = term if acc is None else acc + term
            outs.append(acc)
        y_ref[...] = jnp.concatenate(outs, axis=1).astype(y_ref.dtype)

    row = pl.BlockSpec((tm, ATT_W), lambda i: (i, 0))
    vec = pl.BlockSpec((tm, LANE), lambda i: (i, 0))
    return pl.pallas_call(
        body, name="dil_merge", grid=(SEQ // tm,),
        in_specs=[row] * 3 + [vec] * 3, out_specs=[row, vec],
        out_shape=[_sds((SEQ, ATT_W), BF16), _sds((SEQ, LANE), F32)],
        compiler_params=_params("parallel"),
    )(*os_, *lses)


def _dil_bwd(qkv, lse, delta, do, d):
    length = SEQ // d
    nb = length // DIL_BLK
    qkv_v = qkv.reshape(length, d * 3 * ATT_W)
    lse_v, dl_v, do_v = lse.reshape(length, d * LANE), delta.reshape(length, d * LANE), do.reshape(length, d * ATT_W)

    def body(q_ref, kp_ref, kc_ref, vp_ref, vc_ref, lse_ref, dl_ref, do_ref,
             dq_ref, dk_ref, dv_ref, ck_s, cv_s):
        n = pl.program_id(1)

        @pl.when(n == 0)
        def _():
            ck_s[...] = jnp.zeros_like(ck_s)
            cv_s[...] = jnp.zeros_like(cv_s)

        @pl.when(n < nb)
        def _():
            ok = _dil_valid(n)
            heads = range(N_HEADS)
            kks = [jnp.concatenate([_head(kp_ref, h), _head(kc_ref, h)], axis=0) for h in heads]
            scores = [lax.dot_general(_head(q_ref, h), kks[h], NT_DIMS, preferred_element_type=F32) for h in heads]
            dps = [lax.dot_general(_head(do_ref, h), jnp.concatenate([_head(vp_ref, h), _head(vc_ref, h)], axis=0),
                                   NT_DIMS, preferred_element_type=F32) for h in heads]
            ps, dss = [], []
            for h in heads:
                p = jnp.where(ok, jnp.exp(scores[h] - lse_ref[:, h:h + 1]), 0.0)
                ps.append(p.astype(BF16))
                dss.append((p * (dps[h] - dl_ref[:, h:h + 1])).astype(BF16))
            dqs = [jnp.dot(dss[h], kks[h], preferred_element_type=F32) * SCALE for h in heads]
            dkks = [lax.dot_general(dss[h], _head(q_ref, h), TN_DIMS, preferred_element_type=F32) for h in heads]
            dvvs = [lax.dot_general(ps[h], _head(do_ref, h), TN_DIMS, preferred_element_type=F32) for h in heads]
            dq_ref[...] = jnp.concatenate(dqs, axis=1).astype(dq_ref.dtype)
            dk_ref[...] = (ck_s[...] + jnp.concatenate([t[:DIL_BLK] for t in dkks], axis=1)).astype(dk_ref.dtype)
            dv_ref[...] = (cv_s[...] + jnp.concatenate([t[:DIL_BLK] for t in dvvs], axis=1)).astype(dv_ref.dtype)
            ck_s[...] = jnp.concatenate([t[DIL_BLK:] for t in dkks], axis=1)
            cv_s[...] = jnp.concatenate([t[DIL_BLK:] for t in dvvs], axis=1)

        @pl.when(n == nb)
        def _():
            dk_ref[...] = ck_s[...].astype(dk_ref.dtype)
            dv_ref[...] = cv_s[...].astype(dv_ref.dtype)

    blk = lambda f: pl.BlockSpec((DIL_BLK, ATT_W), f)
    vec = lambda f: pl.BlockSpec((DIL_BLK, LANE), f)
    cur = lambda n: jnp.minimum(n, nb - 1)
    prev = lambda n: jnp.maximum(cur(n) - 1, 0)
    back = lambda n: jnp.maximum(n - 1, 0)
    outs = pl.pallas_call(
        body, name=f"dil_bwd_d{d}", grid=(d, nb + 1),
        in_specs=[blk(lambda r, n: (cur(n), 3 * r)),
                  blk(lambda r, n: (prev(n), 3 * r + 1)), blk(lambda r, n: (cur(n), 3 * r + 1)),
                  blk(lambda r, n: (prev(n), 3 * r + 2)), blk(lambda r, n: (cur(n), 3 * r + 2)),
                  vec(lambda r, n: (cur(n), r)), vec(lambda r, n: (cur(n), r)),
                  blk(lambda r, n: (cur(n), r))],
        out_specs=[blk(lambda r, n: (cur(n), r)), blk(lambda r, n: (back(n), r)), blk(lambda r, n: (back(n), r))],
        out_shape=[_sds((length, d * ATT_W), BF16)] * 3,
        scratch_shapes=[pltpu.VMEM((DIL_BLK, ATT_W), F32), pltpu.VMEM((DIL_BLK, ATT_W), F32)],
        compiler_params=_params("arbitrary", "arbitrary"),
    )(qkv_v, qkv_v, qkv_v, qkv_v, qkv_v, lse_v, dl_v, do_v)
    return [t.reshape(SEQ, ATT_W) for t in outs]


def _sigmoid(x):
    return 1.0 / (1.0 + jnp.exp(-x))


def _mix_fwd(ya, yb, w_oa, w_ob, zm, *, tm=512):
    def body(ya_ref, yb_ref, wa_ref, wb_ref, ga_ref, gb_ref, pa_ref, pb_ref, mix_ref):
        pa = jnp.dot(ya_ref[...], wa_ref[...], preferred_element_type=F32)
        pb = jnp.dot(yb_ref[...], wb_ref[...], preferred_element_type=F32)
        pa_ref[...] = pa.astype(pa_ref.dtype)
        pb_ref[...] = pb.astype(pb_ref.dtype)
        mix_ref[...] = (_sigmoid(ga_ref[...].astype(F32)) * pa + _sigmoid(gb_ref[...].astype(F32)) * pb
                        ).astype(mix_ref.dtype)

    row = pl.BlockSpec((tm, ATT_W), lambda i: (i, 0))
    wsp = pl.BlockSpec((ATT_W, D_MODEL), lambda i: (0, 0))
    wide = pl.BlockSpec((tm, D_MODEL), lambda i: (i, 0))
    return pl.pallas_call(
        body, name="mix_fwd", grid=(SEQ // tm,),
        in_specs=[row, row, wsp, wsp, pl.BlockSpec((tm, D_MODEL), lambda i: (i, 3)),
                  pl.BlockSpec((tm, D_MODEL), lambda i: (i, 4))],
        out_specs=[wide] * 3, out_shape=[_sds((SEQ, D_MODEL), BF16)] * 3,
        compiler_params=_params("parallel"),
    )(ya, yb, w_oa, w_ob, zm, zm)


def _gate_bwd(dmix, zm, pa, pb, *, tm=512):
    def body(dm_ref, ga_ref, gb_ref, pa_ref, pb_ref, dpa_ref, dpb_ref, dg_ref):
        dm = dm_ref[...].astype(F32)
        sa, sb = _sigmoid(ga_ref[...].astype(F32)), _sigmoid(gb_ref[...].astype(F32))
        dpa_ref[...] = (dm * sa).astype(dpa_ref.dtype)
        dpb_ref[...] = (dm * sb).astype(dpb_ref.dtype)
        dg_ref[:, :D_MODEL] = (dm * pa_ref[...].astype(F32) * sa * (1.0 - sa)).astype(dg_ref.dtype)
        dg_ref[:, D_MODEL:] = (dm * pb_ref[...].astype(F32) * sb * (1.0 - sb)).astype(dg_ref.dtype)

    wide = pl.BlockSpec((tm, D_MODEL), lambda i: (i, 0))
    return pl.pallas_call(
        body, name="gate_bwd", grid=(SEQ // tm,),
        in_specs=[wide, pl.BlockSpec((tm, D_MODEL), lambda i: (i, 3)), pl.BlockSpec((tm, D_MODEL), lambda i: (i, 4)),
                  wide, wide],
        out_specs=[wide, wide, pl.BlockSpec((tm, 2 * D_MODEL), lambda i: (i, 0))],
        out_shape=[_sds((SEQ, D_MODEL), BF16), _sds((SEQ, D_MODEL), BF16), _sds((SEQ, 2 * D_MODEL), BF16)],
        compiler_params=_params("parallel"),
    )(dmix, zm, zm, pa, pb)


def _out_fwd(mixed, w_out, x, g_post, g_pre, *, tm=512):
    def body(m_ref, w_ref, x_ref, gp_ref, gn_ref, y_ref, x2_ref, h_ref):
        y = jnp.dot(m_ref[...], w_ref[...], preferred_element_type=F32)
        y_ref[...] = y
        r = lax.rsqrt(jnp.mean(y * y, axis=-1, keepdims=True) + RMS_EPS)
        x2 = x_ref[...] + y * r * gp_ref[...]
        x2_ref[...] = x2
        r2 = lax.rsqrt(jnp.mean(x2 * x2, axis=-1, keepdims=True) + RMS_EPS)
        h_ref[...] = (x2 * r2 * gn_ref[...]).astype(h_ref.dtype)

    row = pl.BlockSpec((tm, D_MODEL), lambda i: (i, 0))
    vec = pl.BlockSpec((1, D_MODEL), lambda i: (0, 0))
    return pl.pallas_call(
        body, name="out_fwd", grid=(SEQ // tm,),
        in_specs=[row, pl.BlockSpec((D_MODEL, D_MODEL), lambda i: (0, 0)), row, vec, vec],
        out_specs=[row] * 3,
        out_shape=[_sds((SEQ, D_MODEL), F32), _sds((SEQ, D_MODEL), F32), _sds((SEQ, D_MODEL), BF16)],
        compiler_params=_params("parallel"),
    )(mixed, w_out, x, g_post, g_pre)


FFN_TM = 512
FFN_TN = 256
FFN_NJ = D_FF // FFN_TN
FFN_GROUP = 2 * SUBLANE


def _gelu_parts(a):
    c = math.sqrt(2.0 / math.pi)
    t = jnp.tanh(c * (a + 0.044715 * a * a * a))
    gelu = 0.5 * a * (1.0 + t)
    dgelu = 0.5 * (1.0 + t) + 0.5 * a * (1.0 - t * t) * c * (1.0 + 3.0 * 0.044715 * a * a)
    return gelu, dgelu


def _shift_down(cur, above, k):
    row = lax.broadcasted_iota(jnp.int32, cur.shape, 0)
    return jnp.where(row < k, pltpu.roll(above, k, 0), pltpu.roll(cur, k, 0))


def _shift_up(cur, below, k):
    row = lax.broadcasted_iota(jnp.int32, cur.shape, 0)
    return jnp.where(row >= SUBLANE - k, pltpu.roll(below, SUBLANE - k, 0), pltpu.roll(cur, SUBLANE - k, 0))


def _conv_consts(w_ref, b_ref):
    shape = (SUBLANE, FFN_TN)
    return [jnp.broadcast_to(w_ref[k:k + 1, :], shape) for k in range(3)] + [jnp.broadcast_to(b_ref[...], shape)]


def _conv_taps(cur, above, consts):
    w0, w1, w2, bias = consts
    s1, s2 = _shift_down(cur, above, 1), _shift_down(cur, above, 2)
    return w0 * s2 + w1 * s1 + w2 * cur + bias, s1, s2


def _ffn_specs(rev):
    nrow = SEQ // FFN_TM
    per = FFN_TM // SUBLANE
    ri = (lambda i: nrow - 1 - i) if rev else (lambda i: i)
    main = lambda off: pl.BlockSpec((FFN_TM, FFN_TN), lambda j, i: (ri(i), j + off))
    halo = lambda off: pl.BlockSpec((SUBLANE, FFN_TN), lambda j, i: (jnp.maximum(ri(i) * per - 1, 0), j + off))
    wsp = lambda off: pl.BlockSpec((3, FFN_TN), lambda j, i: (0, j + off))
    bsp = lambda off: pl.BlockSpec((1, FFN_TN), lambda j, i: (0, j + off))
    return ri, main, halo, wsp, bsp


def _ffn_mid_fwd(u, conv_w, conv_b):
    ri, main, halo, wsp, bsp = _ffn_specs(False)

    def body(ua_ref, ub_ref, ha_ref, hb_ref, wa_ref, wb_ref, ba_ref, bb_ref, m_ref):
        live = (pl.program_id(1) > 0).astype(F32)
        ca, cb = _conv_consts(wa_ref, ba_ref), _conv_consts(wb_ref, bb_ref)

        def group(g, carry):
            above_a, above_b = carry
            rows = pl.ds(pl.multiple_of(g * FFN_GROUP, FFN_GROUP), FFN_GROUP)
            xa, xb = ua_ref[rows, :].astype(F32), ub_ref[rows, :].astype(F32)
            outs = []
            for c in range(2):
                cur_a, cur_b = xa[c * SUBLANE:(c + 1) * SUBLANE], xb[c * SUBLANE:(c + 1) * SUBLANE]
                a = _conv_taps(cur_a, above_a, ca)[0]
                b = _conv_taps(cur_b, above_b, cb)[0]
                outs.append(_gelu_parts(a)[0] * b)
                above_a, above_b = cur_a, cur_b
            m_ref[rows, :] = jnp.concatenate(outs, axis=0).astype(m_ref.dtype)
            return above_a, above_b

        lax.fori_loop(0, FFN_TM // FFN_GROUP, group,
                      (ha_ref[...].astype(F32) * live, hb_ref[...].astype(F32) * live))

    return pl.pallas_call(
        body, name="ffn_mid_fwd", grid=(FFN_NJ, SEQ // FFN_TM),
        in_specs=[main(0), main(FFN_NJ), halo(0), halo(FFN_NJ), wsp(0), wsp(FFN_NJ), bsp(0), bsp(FFN_NJ)],
        out_specs=pl.BlockSpec((FFN_TM, FFN_TN), lambda j, i: (i, j)),
        out_shape=_sds((SEQ, D_FF), BF16),
        compiler_params=_params("parallel", "arbitrary"),
    )(u, u, u, u, conv_w, conv_w, conv_b, conv_b)


def _ffn_mid_bwd(dm, u, conv_w, conv_b):
    ri, main, halo, wsp, bsp = _ffn_specs(True)
    nrow = SEQ // FFN_TM

    def body(dm_ref, ua_ref, ub_ref, ha_ref, hb_ref, wa_ref, wb_ref, ba_ref, bb_ref,
             dua_ref, dub_ref, gwa_ref, gwb_ref, gba_ref, gbb_ref, ca_s, cb_s):
        i = pl.program_id(1)
        live = (i < nrow - 1).astype(F32)

        @pl.when(i == 0)
        def _():
            ca_s[...] = jnp.zeros_like(ca_s)
            cb_s[...] = jnp.zeros_like(cb_s)
            for r in (gwa_ref, gwb_ref, gba_ref, gbb_ref):
                r[...] = jnp.zeros_like(r)

        consts_a, consts_b = _conv_consts(wa_ref, ba_ref), _conv_consts(wb_ref, bb_ref)
        halo_a, halo_b = ha_ref[...].astype(F32) * live, hb_ref[...].astype(F32) * live
        n_groups = FFN_TM // FFN_GROUP

        def load(g):
            rows = pl.ds(pl.multiple_of(g * FFN_GROUP, FFN_GROUP), FFN_GROUP)
            return ua_ref[rows, :].astype(F32), ub_ref[rows, :].astype(F32)

        def group(t, carry):
            xa, xb, below_a, below_b, acc = carry
            g = n_groups - 1 - t
            rows = pl.ds(pl.multiple_of(g * FFN_GROUP, FFN_GROUP), FFN_GROUP)
            ya, yb = load(jnp.maximum(g - 1, 0))
            top_a = jnp.where(g > 0, ya[SUBLANE:], halo_a)
            top_b = jnp.where(g > 0, yb[SUBLANE:], halo_b)
            dmv = dm_ref[rows, :].astype(F32)
            acc = list(acc)
            pre_a, pre_b = [None, None], [None, None]
            for c in (1, 0):
                sl = slice(c * SUBLANE, (c + 1) * SUBLANE)
                cur_a, cur_b = xa[sl], xb[sl]
                above_a, above_b = (xa[:SUBLANE], xb[:SUBLANE]) if c == 1 else (top_a, top_b)
                a, a1, a2 = _conv_taps(cur_a, above_a, consts_a)
                b, b1, b2 = _conv_taps(cur_b, above_b, consts_b)
                gelu, dgelu = _gelu_parts(a)
                du_a, du_b = dmv[sl] * b * dgelu, dmv[sl] * gelu
                for base, du, taps in ((0, du_a, (a2, a1, cur_a)), (4, du_b, (b2, b1, cur_b))):
                    for k in range(3):
                        acc[base + k] = acc[base + k] + du * taps[k]
                    acc[base + 3] = acc[base + 3] + du
                pre_a[c] = (consts_a[2] * du_a + consts_a[1] * _shift_up(du_a, below_a, 1)
                            + consts_a[0] * _shift_up(du_a, below_a, 2))
                pre_b[c] = (consts_b[2] * du_b + consts_b[1] * _shift_up(du_b, below_b, 1)
                            + consts_b[0] * _shift_up(du_b, below_b, 2))
                below_a, below_b = du_a, du_b
            dua_ref[rows, :] = jnp.concatenate(pre_a, axis=0).astype(dua_ref.dtype)
            dub_ref[rows, :] = jnp.concatenate(pre_b, axis=0).astype(dub_ref.dtype)
            return ya, yb, below_a, below_b, tuple(acc)

        zeros = jnp.zeros((SUBLANE, FFN_TN), F32)
        xa0, xb0 = load(n_groups - 1)
        _, _, below_a, below_b, acc = lax.fori_loop(
            0, n_groups, group, (xa0, xb0, ca_s[...], cb_s[...], (zeros,) * 8))
        ca_s[...] = below_a
        cb_s[...] = below_b
        for base, gw_ref, gb_ref in ((0, gwa_ref, gba_ref), (4, gwb_ref, gbb_ref)):
            for k in range(3):
                gw_ref[k:k + 1, :] += jnp.sum(acc[base + k], axis=0, keepdims=True)
            gb_ref[...] += jnp.sum(acc[base + 3], axis=0, keepdims=True)

    acc3 = pl.BlockSpec((3, FFN_TN), lambda j, i: (0, j))
    acc1 = pl.BlockSpec((1, FFN_TN), lambda j, i: (0, j))
    out_blk = pl.BlockSpec((FFN_TM, FFN_TN), lambda j, i: (ri(i), j))
    return pl.pallas_call(
        body, name="ffn_mid_bwd", grid=(FFN_NJ, nrow),
        in_specs=[pl.BlockSpec((FFN_TM, FFN_TN), lambda j, i: (ri(i), j)),
                  main(0), main(FFN_NJ), halo(0), halo(FFN_NJ), wsp(0), wsp(FFN_NJ), bsp(0), bsp(FFN_NJ)],
        out_specs=[out_blk, out_blk, acc3, acc3, acc1, acc1],
        out_shape=[_sds((SEQ, D_FF), BF16), _sds((SEQ, D_FF), BF16), _sds((3, D_FF), F32), _sds((3, D_FF), F32),
                   _sds((1, D_FF), F32), _sds((1, D_FF), F32)],
        scratch_shapes=[pltpu.VMEM((SUBLANE, FFN_TN), F32), pltpu.VMEM((SUBLANE, FFN_TN), F32)],
        compiler_params=_params("parallel", "arbitrary"),
    )(dm, u, u, u, u, conv_w, conv_w, conv_b, conv_b)


def _down_fwd(m, w_down, x2, g_post, target, *, tm=512):
    def body(m_ref, w_ref, x2_ref, g_ref, t_ref, dout_ref, dy_ref, gg_ref, loss_ref):
        @pl.when(pl.program_id(0) == 0)
        def _():
            gg_ref[...] = jnp.zeros_like(gg_ref)
            loss_ref[...] = jnp.zeros_like(loss_ref)

        y = jnp.dot(m_ref[...], w_ref[...], preferred_element_type=F32)
        r = lax.rsqrt(jnp.mean(y * y, axis=-1, keepdims=True) + RMS_EPS)
        yn = y * r
        diff = (x2_ref[...] + yn * g_ref[...]) - t_ref[...]
        loss_ref[...] += jnp.sum(diff * diff)
        dout = diff * (1.0 / D_MODEL)
        dout_ref[...] = dout
        gg_ref[...] += jnp.sum(dout * yn, axis=0, keepdims=True)
        dn = dout * g_ref[...]
        dy_ref[...] = (r * (dn - yn * jnp.mean(dn * yn, axis=-1, keepdims=True))).astype(dy_ref.dtype)

    row = pl.BlockSpec((tm, D_MODEL), lambda i: (i, 0))
    vec = pl.BlockSpec((1, D_MODEL), lambda i: (0, 0))
    return pl.pallas_call(
        body, name="down_fwd", grid=(SEQ // tm,),
        in_specs=[pl.BlockSpec((tm, D_FF), lambda i: (i, 0)), pl.BlockSpec((D_FF, D_MODEL), lambda i: (0, 0)),
                  row, vec, row],
        out_specs=[row, row, vec, pl.BlockSpec((1, LANE), lambda i: (0, 0))],
        out_shape=[_sds((SEQ, D_MODEL), F32), _sds((SEQ, D_MODEL), BF16), _sds((1, D_MODEL), F32),
                   _sds((1, LANE), F32)],
        compiler_params=_params("arbitrary"),
    )(m, w_down, x2, g_post, target)


def _local_step(x, target, w_main, w_f, b_forget, conv_b, g_pre_mix, g_post_mix, g_pre_ffn, g_post_ffn,
                late_weights, ffn_grads_ready, proj_grads_ready, mixer_grads_ready):
    mm = _matmul
    tabs = _rope_tables()

    h1 = _rms_fwd(x, g_pre_mix, name="rms_pre_mix")
    zm = mm(h1, w_main, out_dtype=BF16, tm=2048, tn=512, tk=1024, name="in_proj")
    zf = mm(h1, w_f, out_dtype=F32, tm=2048, tn=F_PAD, tk=1024, name="in_proj_forget")
    f_row, sg_row = _fox_prep(zf[:, :N_HEADS].T, b_forget.reshape(N_HEADS, 1))
    f_cols = jnp.pad(f_row.T, ((0, 0), (0, LANE - N_HEADS)))
    q_slots, k_slots, v_slots = _fox_pack_fwd(zm, f_cols)
    ya, lse_a = _fox_fwd(q_slots, k_slots, v_slots)
    qkv_d = _rope_fwd(zm, tabs)
    dil = [_dil_fwd(qkv_d, d) for _, d in DIL_PATTERNS]
    yb, lse_b = _dil_merge([o for o, _ in dil], [l for _, l in dil])
    w_oa, w_ob, w_out, w_up, conv_w, w_down = late_weights(yb)
    pa, pb, mixed = _mix_fwd(ya, yb, w_oa, w_ob, zm)
    y1, x2, h2 = _out_fwd(mixed, w_out, x, g_post_mix, g_pre_ffn)
    u = mm(h2, w_up, out_dtype=BF16, tm=2048, tn=512, tk=1024, name="up_proj")
    m = _ffn_mid_fwd(u, conv_w, conv_b)
    dout, dy2, gg_post_ffn, sq_err = _down_fwd(m, w_down, x2, g_post_ffn, target)

    g_w_down = mm(m, dy2, ta=True, out_dtype=BF16, tm=D_FF // 2, tn=1024, tk=2048, name="grad_w_down")
    dm = mm(dy2, w_down, tb=True, out_dtype=BF16, tm=2048, tn=D_FF // 2, tk=1024, name="d_ffn_mid")
    du_a, du_b, gcw_a, gcw_b, gcb_a, gcb_b = _ffn_mid_bwd(dm, u, conv_w, conv_b)
    g_w_up = [mm(h2, t, ta=True, out_dtype=BF16, tm=1024, tn=D_FF // 2, tk=2048, name=f"grad_w_up_{s}")
              for s, t in (("a", du_a), ("b", du_b))]
    tok = ffn_grads_ready(dict(w_down=g_w_down, w_up=jnp.concatenate(g_w_up, axis=1),
                               conv_w=jnp.concatenate([gcw_a, gcw_b], axis=1)))
    dh2 = [mm(t, w_up, tb=True, out_dtype=BF16, tm=1024, tn=1024, tk=D_FF, b_k_off=o, name=f"d_h2_{s}")
           for s, t, o in (("a", du_a, 0), ("b", du_b, 1))]

    dx2, dy1, gg_pre_ffn, gg_post_mix = _rms_pair_bwd(dh2, x2, g_pre_ffn, dout, y1, g_post_mix + tok)
    g_w_out = mm(mixed, dy1, ta=True, out_dtype=BF16, tm=1024, tn=1024, tk=2048, name="grad_w_out")
    dmix = mm(dy1, w_out, tb=True, out_dtype=BF16, tm=2048, tn=1024, tk=1024, name="d_mixed")
    dpa, dpb, dgates = _gate_bwd(dmix, zm, pa, pb)
    g_w_oa = mm(ya, dpa, ta=True, out_dtype=BF16, tm=512, tn=1024, tk=SEQ, name="grad_w_o_fox")
    g_w_ob = mm(yb, dpb, ta=True, out_dtype=BF16, tm=512, tn=1024, tk=SEQ, name="grad_w_o_dil")
    tok = proj_grads_ready(dict(w_o_fox=g_w_oa, w_o_dil=g_w_ob, w_out=g_w_out))
    dya = mm(dpa, w_oa, tb=True, out_dtype=BF16, tm=2048, tn=512, tk=1024, name="d_y_fox")
    dyb = mm(dpb, w_ob, tb=True, out_dtype=BF16, tm=2048, tn=512, tk=1024, name="d_y_dil")

    qb_slots, do_slots = _fox_pack_bwd(zm, f_cols + tok, lse_a, ya, dya)
    d_fox, df_cols = _fox_unpack(*_fox_bwd(qb_slots, k_slots, v_slots, do_slots))
    dfa_t, g_b_forget = _fox_post_bwd(df_cols[:, :N_HEADS].T, sg_row)

    delta_b = _attn_delta(yb, dyb, name="delta_dil")
    dil_g = [_dil_bwd(qkv_d, lse_b, delta_b, dyb, d) for _, d in DIL_PATTERNS]
    d_dil = _dil_grad_combine([g[0] for g in dil_g], [g[1] for g in dil_g], [g[2] for g in dil_g], tabs)

    dz = jnp.concatenate([d_fox, d_dil, dgates], axis=1)
    dzf = jnp.pad(dfa_t.T, ((0, 0), (0, F_PAD - N_HEADS)))
    g_w_main = mm(h1, dz, ta=True, out_dtype=BF16, tm=1024, tn=Z_MAIN // 4, tk=2048, name="grad_w_in")
    g_w_f = mm(h1, dzf, ta=True, out_dtype=BF16, tm=1024, tn=F_PAD, tk=1024, name="grad_w_in_forget")
    tok = mixer_grads_ready(dict(w_main=g_w_main, w_f=g_w_f))
    dh1 = [mm(dz, w_main, tb=True, out_dtype=BF16, tm=512, tn=1024, tk=Z_MAIN, name="d_h1"),
           mm(dzf + tok, w_f, tb=True, out_dtype=F32, tm=2048, tn=1024, tk=F_PAD, name="d_h1_forget")]
    grad_x, gg_pre_mix = _rms_bwd(dh1, x, g_pre_mix, dx2, out_dtype=F32, name="rms_pre_mix_bwd")

    grads = dict(
        b_forget=g_b_forget.reshape(1, N_HEADS), conv_b=jnp.concatenate([gcb_a, gcb_b], axis=1),
        g_pre_mix=gg_pre_mix, g_post_mix=gg_post_mix, g_pre_ffn=gg_pre_ffn, g_post_ffn=gg_post_ffn)
    return sq_err[0, 0], grad_x, grads


def _exchange(arrays, scatter, *, name):
    n = len(arrays)

    def body(*refs):
        ins, outs = refs[:n], refs[n:2 * n]
        send_sems, recv_sems, local_sems = refs[2 * n:]
        x, y, c = lax.axis_index("x"), lax.axis_index("y"), lax.axis_index("c")
        me = 4 * x + 2 * y + c
        peers = []
        for k in range(1, N_DEV):
            px = 1 - x if k & 4 else x
            py = 1 - y if k & 2 else y
            pc = 1 - c if k & 1 else c
            peers.append(((px, py, pc), 4 * px + 2 * py + pc))

        def remote(a, k):
            dev, slot = peers[k]
            return pltpu.make_async_remote_copy(
                src_ref=ins[a].at[slot] if scatter else ins[a], dst_ref=outs[a].at[me],
                send_sem=send_sems.at[a, k], recv_sem=recv_sems.at[a, k],
                device_id=dev, device_id_type=MESH_ID)

        def landed(a, k):
            dev, slot = peers[k]
            return pltpu.make_async_remote_copy(
                src_ref=outs[a].at[slot], dst_ref=outs[a].at[slot],
                send_sem=send_sems.at[a, k], recv_sem=recv_sems.at[a, k],
                device_id=dev, device_id_type=MESH_ID)

        own = [pltpu.make_async_copy(ins[a].at[me] if scatter else ins[a], outs[a].at[me], local_sems.at[a])
               for a in range(n)]
        copies = [remote(a, k) for k in range(N_DEV - 1) for a in range(n)]
        for cp in own + copies:
            cp.start()
        for k in range(N_DEV - 1):
            for a in range(n):
                landed(a, k).wait_recv()
        for cp in copies:
            cp.wait_send()
        for cp in own:
            cp.wait()

    out_shape = [_sds(((N_DEV,) + a.shape[-2:]), a.dtype) for a in arrays]
    return pl.pallas_call(
        body, name=name, in_specs=[ANY] * n, out_specs=[ANY] * n, out_shape=out_shape,
        scratch_shapes=[pltpu.SemaphoreType.DMA((n, N_DEV - 1)), pltpu.SemaphoreType.DMA((n, N_DEV - 1)),
                        pltpu.SemaphoreType.DMA((n,))],
    )(*arrays)


def _gather_two_level(shard, *, name):
    def body(x_ref, out_ref, send_sems, recv_sems, local_sem):
        x, y, c = lax.axis_index("x"), lax.axis_index("y"), lax.axis_index("c")
        me, sibling = (x, y, c), (x, y, 1 - c)
        chips = [(1 - x, y), (x, 1 - y), (1 - x, 1 - y)]

        def slot(px, py, pc):
            return out_ref.at[4 * px + 2 * py + pc]

        def copy(k, block, to, src=None):
            return pltpu.make_async_remote_copy(
                src_ref=slot(*block) if src is None else src, dst_ref=slot(*block),
                send_sem=send_sems.at[k], recv_sem=recv_sems.at[k], device_id=to, device_id_type=MESH_ID)

        mine = pltpu.make_async_copy(x_ref, slot(*me), local_sem)
        mine.start()
        first = [copy(0, me, sibling, src=x_ref)]
        first += [copy(1 + j, me, (*chip, c), src=x_ref) for j, chip in enumerate(chips)]
        for cp in first:
            cp.start()
        passed = [copy(4 + j, (*chip, c), sibling) for j, chip in enumerate(chips)]
        for j, chip in enumerate(chips):
            copy(1 + j, (*chip, c), me).wait_recv()
            passed[j].start()
        copy(0, sibling, me).wait_recv()
        for j, chip in enumerate(chips):
            copy(4 + j, (*chip, 1 - c), me).wait_recv()
        for cp in first + passed:
            cp.wait_send()
        mine.wait()

    return pl.pallas_call(
        body, name=name, in_specs=[ANY], out_specs=ANY, out_shape=_sds((N_DEV,) + shard.shape, shard.dtype),
        scratch_shapes=[pltpu.SemaphoreType.DMA((N_DEV - 1,)), pltpu.SemaphoreType.DMA((N_DEV - 1,)),
                        pltpu.SemaphoreType.DMA],
    )(shard)


N_CHIPS = N_DEV // 2


def _peers(chips_only=False):
    x, y, c = lax.axis_index("x"), lax.axis_index("y"), lax.axis_index("c")
    out = []
    if chips_only:
        for k in range(1, N_CHIPS):
            px = 1 - x if k & 2 else x
            py = 1 - y if k & 1 else y
            out.append(((px, py, c), 2 * px + py))
        return 2 * x + y, out
    for k in range(1, N_DEV):
        px = 1 - x if k & 4 else x
        py = 1 - y if k & 2 else y
        pc = 1 - c if k & 1 else c
        out.append(((px, py, pc), 4 * px + 2 * py + pc))
    return 4 * x + 2 * y + c, out


def _sibling_swap(slots, *, name):
    def body(in_ref, out_ref, send_sems, recv_sems):
        x, y, c = lax.axis_index("x"), lax.axis_index("y"), lax.axis_index("c")
        copies = [pltpu.make_async_remote_copy(
            src_ref=in_ref.at[2 * q + (1 - c)], dst_ref=out_ref.at[q], send_sem=send_sems.at[q],
            recv_sem=recv_sems.at[q], device_id=(x, y, 1 - c), device_id_type=MESH_ID) for q in range(N_CHIPS)]
        for cp in copies:
            cp.start()
        for cp in copies:
            cp.wait_recv()
        for cp in copies:
            cp.wait_send()

    return pl.pallas_call(
        body, name=name, in_specs=[ANY], out_specs=ANY,
        out_shape=_sds((N_CHIPS,) + slots.shape[1:], slots.dtype),
        scratch_shapes=[pltpu.SemaphoreType.DMA((N_CHIPS,)), pltpu.SemaphoreType.DMA((N_CHIPS,))],
    )(slots)


def _pair_sum(slots, from_sibling, *, name, tm=256):
    _, r, c = slots.shape
    core = lax.axis_index("c").astype(jnp.int32).reshape(1)

    def body(core_ref, a_ref, b_ref, o_ref):
        o_ref[...] = (a_ref[...].astype(F32) + b_ref[...].astype(F32)).astype(o_ref.dtype)

    blk = lambda f: pl.BlockSpec((1, tm, c), f)
    return pl.pallas_call(
        body, name=name,
        grid_spec=pltpu.PrefetchScalarGridSpec(
            num_scalar_prefetch=1, grid=(N_CHIPS, r // tm),
            in_specs=[blk(lambda q, i, core: (2 * q + core[0], i, 0)), blk(lambda q, i, core: (q, i, 0))],
            out_specs=blk(lambda q, i, core: (q, i, 0))),
        out_shape=_sds((N_CHIPS, r, c), slots.dtype),
        compiler_params=_params("parallel", "parallel"),
    )(core, slots, from_sibling)


HBM = pl.BlockSpec(memory_space=pltpu.HBM)
SEM = pl.BlockSpec(memory_space=pltpu.SEMAPHORE)
DATAFLOW = pltpu.SideEffectType.DATAFLOW_SIDE_EFFECTING


def _split_copy(srcs, lands, send_sems, recv_sems, scatter, a, k, me, peers, incoming=False):
    dev, slot = peers[k]
    if incoming:
        src = dst = lands[a].at[slot]
    else:
        src, dst = (srcs[a].at[slot] if scatter else srcs[a]), lands[a].at[me]
    sem = a * len(peers) + k
    return pltpu.make_async_remote_copy(
        src_ref=src, dst_ref=dst, send_sem=send_sems.at[sem], recv_sem=recv_sems.at[sem],
        device_id=dev, device_id_type=MESH_ID)


def _exchange_start(arrays, scatter, *, name, chips_only=False):
    n = len(arrays)
    n_slots = N_CHIPS if chips_only else N_DEV

    def body(*refs):
        srcs, lands = refs[:n], refs[n:2 * n]
        send_sems, recv_sems = refs[2 * n], refs[2 * n + 1]
        token = refs[-1]
        me, peers = _peers(chips_only)
        for k in range(len(peers)):
            for a in range(n):
                _split_copy(srcs, lands, send_sems, recv_sems, scatter, a, k, me, peers).start()
        token[...] = jnp.zeros_like(token)

    land_shapes = [((n_slots,) + a.shape[-2:], a.dtype) for a in arrays]
    sems = pltpu.SemaphoreType.DMA((n * (n_slots - 1),))
    outs = pl.pallas_call(
        body, name=name,
        out_shape=(sems, sems, *[pltpu.HBM(a.shape, a.dtype) for a in arrays],
                   *[pltpu.HBM(s, d) for s, d in land_shapes], _sds((SUBLANE, LANE), F32)),
        in_specs=[HBM] * (2 * n),
        out_specs=(SEM, SEM, *[HBM] * (2 * n), pl.BlockSpec(memory_space=pltpu.VMEM)),
        input_output_aliases={i: 2 + i for i in range(2 * n)},
        compiler_params=pltpu.CompilerParams(has_side_effects=DATAFLOW),
    )(*[pltpu.with_memory_space_constraint(a, pltpu.HBM) for a in arrays],
      *[pltpu.with_memory_space_constraint(lax.empty(s, d), pltpu.HBM) for s, d in land_shapes])
    return (outs[0], outs[1], outs[2:2 + n], outs[2 + n:2 + 2 * n], scatter, chips_only), outs[-1]


def _exchange_wait(handles, after, *, name):
    send_sems, recv_sems, srcs, lands, scatter, chips_only = handles
    n = len(srcs)

    def body(*refs):
        src_refs, land_refs = refs[:n], refs[n:2 * n]
        send_ref, recv_ref = refs[2 * n], refs[2 * n + 1]
        me, peers = _peers(chips_only)
        for k in range(len(peers)):
            for a in range(n):
                _split_copy(src_refs, land_refs, send_ref, recv_ref, scatter, a, k, me, peers).wait_send()
                _split_copy(src_refs, land_refs, send_ref, recv_ref, scatter, a, k, me, peers, True).wait_recv()

    outs = pl.pallas_call(
        body, name=name,
        out_shape=tuple(pltpu.HBM(t.shape, t.dtype) for t in (*srcs, *lands)),
        in_specs=[HBM] * (2 * n) + [SEM, SEM, pl.BlockSpec(memory_space=pl.ANY)],
        out_specs=tuple([HBM] * (2 * n)),
        input_output_aliases={i: i for i in range(2 * n)},
        compiler_params=pltpu.CompilerParams(has_side_effects=DATAFLOW),
    )(*srcs, *lands, send_sems, recv_sems, after)
    return _with_own_slot(outs[n:], outs[:n], scatter, chips_only)


def _with_own_slot(landed, own, scatter, chips_only):
    me = 2 * lax.axis_index("x") + lax.axis_index("y")
    if not chips_only:
        me = 2 * me + lax.axis_index("c")
    out = []
    for buf, src in zip(landed, own):
        mine = lax.dynamic_index_in_dim(src, me, 0, keepdims=False) if scatter else src
        out.append(lax.dynamic_update_index_in_dim(buf, mine, me, 0))
    return out


def _adamw(parts, w, m, v, *, name, tm):
    r, c = w.shape
    assert r % tm == 0

    def body(p_ref, w_ref, m_ref, v_ref, g_ref, d_ref, nm_ref, nv_ref):
        _adamw_update(p_ref, w_ref, m_ref, v_ref, g_ref, d_ref, nm_ref, nv_ref)

    blk = pl.BlockSpec((tm, c), lambda i: (i, 0))
    return pl.pallas_call(
        body, name=name, grid=(r // tm,),
        in_specs=[pl.BlockSpec((parts.shape[0], tm, c), lambda i: (0, i, 0)), blk, blk, blk],
        out_specs=[blk] * 4, out_shape=[_sds((r, c), F32)] * 4,
        compiler_params=_params("parallel"),
    )(parts, w, m, v)


def _adamw_update(p_ref, w_ref, m_ref, v_ref, g_ref, d_ref, nm_ref, nv_ref):
    g = p_ref[0].astype(F32)
    for s in range(1, p_ref.shape[0]):
        g = g + p_ref[s].astype(F32)
    g_ref[...] = g
    m_new = ADAM_B1 * m_ref[...] + (1.0 - ADAM_B1) * g
    v_new = ADAM_B2 * v_ref[...] + (1.0 - ADAM_B2) * (g * g)
    nm_ref[...] = m_new
    nv_ref[...] = v_new
    m_hat = m_new / (1.0 - ADAM_B1 ** ADAM_STEP)
    v_hat = v_new / (1.0 - ADAM_B2 ** ADAM_STEP)
    d_ref[...] = -ADAM_LR * (m_hat / (jnp.sqrt(v_hat) + ADAM_EPS) + ADAM_WD * w_ref[...])


SMALL = ("g_pre_mix", "b_forget", "g_post_mix", "g_pre_ffn", "conv_b", "g_post_ffn")


def _adamw_small(parts, ws, ms, vs):
    n = len(ws)

    def body(*refs):
        ins, outs = refs[:4 * n], refs[4 * n:]
        for i in range(n):
            _adamw_update(ins[i], ins[n + i], ins[2 * n + i], ins[3 * n + i], *outs[4 * i:4 * i + 4])

    res = pl.pallas_call(
        body, name="adamw_small", out_shape=[_sds(w.shape, F32) for w in ws for _ in range(4)],
        compiler_params=pltpu.CompilerParams(vmem_limit_bytes=VMEM_LIMIT),
    )(*parts, *ws, *ms, *vs)
    return [res[4 * i:4 * i + 4] for i in range(n)]


def kernel(x, g_pre_mix, w_in, b_forget, w_o_fox, w_o_dil, w_out, g_post_mix, g_pre_ffn, w_up, conv_w, conv_b, w_down, g_post_ffn, loss_target, m_g_pre_mix, m_w_in, m_b_forget, m_w_o_fox, m_w_o_dil, m_w_out, m_g_post_mix, m_g_pre_ffn, m_w_up, m_conv_w, m_conv_b, m_w_down, m_g_post_ffn, v_g_pre_mix, v_w_in, v_b_forget, v_w_o_fox, v_w_o_dil, v_w_out, v_g_post_mix, v_g_pre_ffn, v_w_up, v_conv_w, v_conv_b, v_w_down, v_g_post_ffn):
    names = ("g_pre_mix", "w_in", "b_forget", "w_o_fox", "w_o_dil", "w_out", "g_post_mix", "g_pre_ffn",
             "w_up", "conv_w", "conv_b", "w_down", "g_post_ffn")
    w = dict(g_pre_mix=g_pre_mix, w_in=w_in, b_forget=b_forget, w_o_fox=w_o_fox, w_o_dil=w_o_dil, w_out=w_out,
             g_post_mix=g_post_mix, g_pre_ffn=g_pre_ffn, w_up=w_up, conv_w=conv_w, conv_b=conv_b, w_down=w_down,
             g_post_ffn=g_post_ffn)
    m = dict(g_pre_mix=m_g_pre_mix, w_in=m_w_in, b_forget=m_b_forget, w_o_fox=m_w_o_fox, w_o_dil=m_w_o_dil,
             w_out=m_w_out, g_post_mix=m_g_post_mix, g_pre_ffn=m_g_pre_ffn, w_up=m_w_up, conv_w=m_conv_w,
             conv_b=m_conv_b, w_down=m_w_down, g_post_ffn=m_g_post_ffn)
    v = dict(g_pre_mix=v_g_pre_mix, w_in=v_w_in, b_forget=v_b_forget, w_o_fox=v_w_o_fox, w_o_dil=v_w_o_dil,
             w_out=v_w_out, g_post_mix=v_g_post_mix, g_pre_ffn=v_g_pre_ffn, w_up=v_w_up, conv_w=v_conv_w,
             conv_b=v_conv_b, w_down=v_w_down, g_post_ffn=v_g_post_ffn)
    sharded = ("w_in", "w_o_fox", "w_o_dil", "w_out", "w_up", "w_down", "conv_w")
    wire = lambda n: F32 if n == "conv_w" else BF16

    by_cols = lambda t: jnp.transpose(t, (1, 0, 2)).reshape(t.shape[1], N_DEV * t.shape[2])
    by_rows = lambda t: t.reshape(N_DEV * t.shape[1], t.shape[2])
    col_slots = lambda t: jnp.transpose(t.reshape(t.shape[0], N_DEV, t.shape[1] // N_DEV), (1, 0, 2))
    row_slots = lambda t: t.reshape(N_DEV, t.shape[0] // N_DEV, t.shape[1])
    to_slots = lambda n, t: (row_slots if n in ("w_out", "w_down") else col_slots)(t).astype(wire(n))
    shard = lambda n: w[n][0].astype(wire(n))
    f_lo, f_hi = 3 * ATT_W, 3 * ATT_W + N_HEADS

    w_in_full = by_cols(_gather_two_level(shard("w_in"), name="gather_w_in"))
    w_main = jnp.concatenate([w_in_full[:, :f_lo], w_in_full[:, f_hi:]], axis=1)
    w_f = jnp.pad(w_in_full[:, f_lo:f_hi], ((0, 0), (0, F_PAD - N_HEADS)))
    late = ("w_o_fox", "w_o_dil", "w_out", "w_up", "conv_w", "w_down")
    order = jnp.minimum(jnp.abs(w_in_full[0, 0].astype(F32)), 0.0)
    late_handles, late_tok = _exchange_start(
        [shard(n) + order.astype(wire(n)) if n == "conv_w" else shard(n) for n in late], False,
        name="gather_late_start")

    def late_weights(after):
        got = dict(zip(late, _exchange_wait(late_handles, after, name="gather_late_wait")))
        return (by_cols(got["w_o_fox"]), by_cols(got["w_o_dil"]), by_rows(got["w_out"]), by_cols(got["w_up"]),
                by_cols(got["conv_w"]), by_rows(got["w_down"]))

    pending = {}

    def ffn_grads_ready(g):
        pending["ffn"] = _exchange_start([to_slots(n, g[n]) for n in ("w_down", "w_up", "conv_w")], True,
                                         name="scatter_ffn_start")
        return pending["ffn"][1][0, 0]

    def proj_grads_ready(g):
        pending["proj"] = _exchange_start([to_slots(n, g[n]) for n in ("w_o_fox", "w_o_dil", "w_out")], True,
                                          name="scatter_proj_start")
        return pending["proj"][1][0, 0]

    def mixer_grads_ready(g):
        g_w_in = jnp.concatenate([g["w_main"][:, :f_lo], g["w_f"][:, :N_HEADS], g["w_main"][:, f_lo:]], axis=1)
        slots = to_slots("w_in", g_w_in)
        chip_sums = _pair_sum(slots, _sibling_swap(slots, name="scatter_w_in_swap"), name="scatter_w_in_pair_sum")
        pending["w_in"] = _exchange_start([chip_sums], True, name="scatter_w_in_start", chips_only=True)
        return pending["w_in"][1][0, 0]

    sq_err, grad_x, g = _local_step(
        x[0], loss_target[0], w_main, w_f, b_forget, conv_b, g_pre_mix + late_tok[0, 0], g_post_mix, g_pre_ffn,
        g_post_ffn, late_weights, ffn_grads_ready, proj_grads_ready, mixer_grads_ready)
    loss = lax.psum(0.5 * sq_err / D_MODEL, ("x", "y", "c"))

    tiles = dict(w_in=256, w_o_fox=512, w_o_dil=512, w_out=128, w_up=256, w_down=176, conv_w=3)
    adam = lambda n, p: _adamw(p, w[n][0], m[n][0], v[n][0], name=f"adamw_{n}", tm=tiles[n])
    res = {}
    for key, group in (("ffn", ("w_down", "w_up", "conv_w")), ("proj", ("w_o_fox", "w_o_dil", "w_out"))):
        landed = _exchange_wait(pending[key][0], grad_x, name=f"scatter_{key}_wait")
        res.update({n: adam(n, p) for n, p in zip(group, landed)})
    small_parts = _exchange([g[n] for n in SMALL], False, name="gather_small_grads")
    done = res["w_up"][3]
    res["w_in"] = adam("w_in", _exchange_wait(pending["w_in"][0], done, name="scatter_w_in_wait")[0])
    small = dict(zip(SMALL, _adamw_small(small_parts, *[[t[n] for n in SMALL] for t in (w, m, v)])))
    out = [[(res[n][k][None] if n in sharded else small[n][k]) for n in names] for k in range(4)]
    return (loss, grad_x[None], *out[0], *out[1], *out[2], *out[3])
```

```python
import functools
import math

import jax
import jax.numpy as jnp
import numpy as np
from jax import lax
from jax.experimental import pallas as pl
from jax.experimental.pallas import tpu as pltpu

F32 = jnp.float32
BF16 = jnp.bfloat16

SEQ = 4096
D_MODEL = 1024
N_HEADS = 8
HEAD_DIM = 64
ATT_W = N_HEADS * HEAD_DIM
D_FF = 2816
Z_MAIN = 5120
F_PAD = 128
ROPE_DIM = 16
ROPE_THETA = 500000.0
RMS_EPS = 1e-6
NEG_INF = -1e30
SCALE = 1.0 / math.sqrt(HEAD_DIM)
DIL_PATTERNS = ((128, 1), (512, 4), (2048, 16))
DIL_BLK = 128
N_DEV = 8

ADAM_LR = 0.001
ADAM_B1 = 0.9
ADAM_B2 = 0.999
ADAM_EPS = 1e-08
ADAM_WD = 0.01
ADAM_STEP = 10

LANE = 128
SUBLANE = 8
VMEM_LIMIT = 56 * 1024 * 1024
MESH_ID = pl.DeviceIdType.MESH
ANY = pl.BlockSpec(memory_space=pl.ANY)


def _params(*sem):
    return pltpu.CompilerParams(dimension_semantics=sem, vmem_limit_bytes=VMEM_LIMIT)


def _sds(shape, dtype):
    return jax.ShapeDtypeStruct(shape, dtype)


def _matmul(a, b, *, ta=False, tb=False, out_dtype, tm, tn, tk, name, b_k_off=0):
    if ta:
        kk, m = a.shape
    else:
        m, kk = a.shape
    n = b.shape[0] if tb else b.shape[1]
    tm, tn, tk = min(tm, m), min(tn, n), min(tk, kk)
    assert (b.shape[1] if tb else b.shape[0]) >= b_k_off * tk + kk
    assert m % tm == 0 and n % tn == 0 and kk % tk == 0, (name, m, n, kk, tm, tn, tk)
    nk = kk // tk
    dims = (((0 if ta else 1,), (1 if tb else 0,)), ((), ()))

    def body(a_ref, b_ref, o_ref, *scratch):
        p = lax.dot_general(a_ref[...].astype(BF16), b_ref[...].astype(BF16), dims,
                            preferred_element_type=F32)
        if nk == 1:
            o_ref[...] = p.astype(o_ref.dtype)
        else:
            acc = scratch[0]
            k = pl.program_id(2)

            @pl.when(k == 0)
            def _():
                acc[...] = p

            @pl.when(k > 0)
            def _():
                acc[...] += p

            @pl.when(k == nk - 1)
            def _():
                o_ref[...] = acc[...].astype(o_ref.dtype)

    a_spec = (pl.BlockSpec((tk, tm), lambda i, j, k: (k, i)) if ta
              else pl.BlockSpec((tm, tk), lambda i, j, k: (i, k)))
    b_spec = (pl.BlockSpec((tn, tk), lambda i, j, k: (j, k + b_k_off)) if tb
              else pl.BlockSpec((tk, tn), lambda i, j, k: (k + b_k_off, j)))
    return pl.pallas_call(
        body, name=name, grid=(m // tm, n // tn, nk),
        in_specs=[a_spec, b_spec],
        out_specs=pl.BlockSpec((tm, tn), lambda i, j, k: (i, j)),
        out_shape=_sds((m, n), out_dtype),
        scratch_shapes=[pltpu.VMEM((tm, tn), F32)] if nk > 1 else [],
        compiler_params=_params("parallel", "parallel", "arbitrary"),
    )(a, b)


def _rms_fwd(x, g, *, name, tm=512):
    def body(x_ref, g_ref, h_ref):
        xv = x_ref[...]
        r = lax.rsqrt(jnp.mean(xv * xv, axis=-1, keepdims=True) + RMS_EPS)
        h_ref[...] = (xv * r * g_ref[...]).astype(h_ref.dtype)

    return pl.pallas_call(
        body, name=name, grid=(SEQ // tm,),
        in_specs=[pl.BlockSpec((tm, D_MODEL), lambda i: (i, 0)), pl.BlockSpec((1, D_MODEL), lambda i: (0, 0))],
        out_specs=pl.BlockSpec((tm, D_MODEL), lambda i: (i, 0)),
        out_shape=_sds((SEQ, D_MODEL), BF16),
        compiler_params=_params("parallel"),
    )(x, g)


def _rms_bwd(dh_parts, xin, g, dres, *, out_dtype, name, tm=512):
    n_parts = len(dh_parts)
    has_res = dres is not None

    def body(*refs):
        parts = refs[:n_parts]
        x_ref, g_ref = refs[n_parts], refs[n_parts + 1]
        res_ref = refs[n_parts + 2] if has_res else None
        o_ref, gg_ref = refs[-2], refs[-1]
        dh = parts[0][...].astype(F32)
        for p in parts[1:]:
            dh = dh + p[...].astype(F32)
        xv = x_ref[...]
        r = lax.rsqrt(jnp.mean(xv * xv, axis=-1, keepdims=True) + RMS_EPS)
        xn = xv * r

        @pl.when(pl.program_id(0) == 0)
        def _():
            gg_ref[...] = jnp.zeros_like(gg_ref)

        gg_ref[...] += jnp.sum(dh * xn, axis=0, keepdims=True)
        dxn = dh * g_ref[...]
        dx = r * (dxn - xn * jnp.mean(dxn * xn, axis=-1, keepdims=True))
        if has_res:
            dx = dx + res_ref[...]
        o_ref[...] = dx.astype(o_ref.dtype)

    row = pl.BlockSpec((tm, D_MODEL), lambda i: (i, 0))
    vec = pl.BlockSpec((1, D_MODEL), lambda i: (0, 0))
    args = list(dh_parts) + [xin, g] + ([dres] if has_res else [])
    return pl.pallas_call(
        body, name=name, grid=(SEQ // tm,),
        in_specs=[row] * n_parts + [row, vec] + ([row] if has_res else []),
        out_specs=[row, vec],
        out_shape=[_sds((SEQ, D_MODEL), out_dtype), _sds((1, D_MODEL), F32)],
        compiler_params=_params("arbitrary"),
    )(*args)


def _rms_pair_bwd(dh_parts, x2, g_pre, dres, y1, g_post, *, tm=512):
    n_parts = len(dh_parts)

    def norm_bwd(dh, xin, g_ref, gg_ref):
        r = lax.rsqrt(jnp.mean(xin * xin, axis=-1, keepdims=True) + RMS_EPS)
        xn = xin * r
        gg_ref[...] += jnp.sum(dh * xn, axis=0, keepdims=True)
        dxn = dh * g_ref[...]
        return r * (dxn - xn * jnp.mean(dxn * xn, axis=-1, keepdims=True))

    def body(*refs):
        parts = refs[:n_parts]
        x2_ref, gpre_ref, res_ref, y1_ref, gpost_ref, dx2_ref, dy1_ref, ggpre_ref, ggpost_ref = refs[n_parts:]

        @pl.when(pl.program_id(0) == 0)
        def _():
            ggpre_ref[...] = jnp.zeros_like(ggpre_ref)
            ggpost_ref[...] = jnp.zeros_like(ggpost_ref)

        dh = parts[0][...].astype(F32)
        for p in parts[1:]:
            dh = dh + p[...].astype(F32)
        dx2 = res_ref[...] + norm_bwd(dh, x2_ref[...], gpre_ref, ggpre_ref)
        dx2_ref[...] = dx2
        dy1_ref[...] = norm_bwd(dx2, y1_ref[...], gpost_ref, ggpost_ref).astype(dy1_ref.dtype)

    row = pl.BlockSpec((tm, D_MODEL), lambda i: (i, 0))
    vec = pl.BlockSpec((1, D_MODEL), lambda i: (0, 0))
    return pl.pallas_call(
        body, name="rms_pair_bwd", grid=(SEQ // tm,),
        in_specs=[row] * n_parts + [row, vec, row, row, vec],
        out_specs=[row, row, vec, vec],
        out_shape=[_sds((SEQ, D_MODEL), F32), _sds((SEQ, D_MODEL), BF16), _sds((1, D_MODEL), F32),
                   _sds((1, D_MODEL), F32)],
        compiler_params=_params("arbitrary"),
    )(*dh_parts, x2, g_pre, dres, y1, g_post)


SCAN_BLK = 512


def _split_dot(v, tri):
    hi = v.astype(BF16)
    r1 = v - hi.astype(F32)
    mid = r1.astype(BF16)
    lo = (r1 - mid.astype(F32)).astype(BF16)
    dot = functools.partial(jnp.dot, preferred_element_type=F32)
    return dot(hi, tri) + dot(mid, tri) + dot(lo, tri)


def _fox_prep(fa_t, b_col):
    nblk = SEQ // SCAN_BLK

    def body(fa_ref, b_ref, f_ref, sg_ref):
        row = lax.broadcasted_iota(jnp.int32, (SCAN_BLK, SCAN_BLK), 0)
        col = lax.broadcasted_iota(jnp.int32, (SCAN_BLK, SCAN_BLK), 1)
        upper = (row <= col).astype(BF16)
        carry = jnp.zeros((N_HEADS, 1), F32)
        for blk in range(nblk):
            sl = pl.ds(blk * SCAN_BLK, SCAN_BLK)
            xx = fa_ref[:, sl] + b_ref[...]
            e = jnp.exp(-jnp.abs(xx))
            logf = jnp.minimum(xx, 0.0) - jnp.log(1.0 + e)
            sg_ref[:, sl] = jnp.where(xx >= 0.0, e, 1.0) / (1.0 + e)
            c = _split_dot(logf, upper) + carry
            f_ref[:, sl] = c
            carry = c[:, SCAN_BLK - 1:SCAN_BLK]

    return pl.pallas_call(
        body, name="fox_prep",
        out_shape=[_sds((N_HEADS, SEQ), F32), _sds((N_HEADS, SEQ), F32)],
        compiler_params=pltpu.CompilerParams(vmem_limit_bytes=VMEM_LIMIT),
    )(fa_t, b_col)


def _fox_post_bwd(df_t, sg_t):
    nblk = SEQ // SCAN_BLK

    def body(df_ref, sg_ref, dfa_ref, gb_ref):
        row = lax.broadcasted_iota(jnp.int32, (SCAN_BLK, SCAN_BLK), 0)
        col = lax.broadcasted_iota(jnp.int32, (SCAN_BLK, SCAN_BLK), 1)
        lower = (row >= col).astype(BF16)
        carry = jnp.zeros((N_HEADS, 1), F32)
        gb = jnp.zeros((N_HEADS, 1), F32)
        for blk in reversed(range(nblk)):
            sl = pl.ds(blk * SCAN_BLK, SCAN_BLK)
            c = _split_dot(df_ref[:, sl], lower) + carry
            carry = c[:, 0:1]
            dfa = c * sg_ref[:, sl]
            dfa_ref[:, sl] = dfa
            gb = gb + jnp.sum(dfa, axis=1, keepdims=True)
        gb_ref[...] = gb

    return pl.pallas_call(
        body, name="fox_post_bwd",
        out_shape=[_sds((N_HEADS, SEQ), F32), _sds((N_HEADS, 1), F32)],
        compiler_params=pltpu.CompilerParams(vmem_limit_bytes=VMEM_LIMIT),
    )(df_t, sg_t)


FOX_T = 512
NT_DIMS = (((1,), (1,)), ((), ()))
TN_DIMS = (((0,), (0,)), ((), ()))


def _head(ref_or_val, h):
    return ref_or_val[:, h * HEAD_DIM:(h + 1) * HEAD_DIM]


def _split3(v):
    hi = v.astype(BF16).astype(F32)
    r1 = v - hi
    mid = r1.astype(BF16).astype(F32)
    return hi, mid, (r1 - mid).astype(BF16).astype(F32)


def _aux_lanes(rows, terms):
    lane = lax.broadcasted_iota(jnp.int32, (rows, HEAD_DIM), 1)
    out = jnp.zeros((rows, HEAD_DIM), F32)
    for i, t in enumerate(terms):
        out = jnp.where(lane == i, t, out)
    return out


SLOT = 2 * HEAD_DIM
N_SPLIT = 3


def _slot(ref, h):
    return ref[:, h * SLOT:(h + 1) * SLOT]


def _fox_pack_fwd(zm, f_cols, *, tm=512):
    def body(q_ref, k_ref, v_ref, f_ref, qs_ref, ks_ref, vs_ref):
        ones = jnp.ones((tm, HEAD_DIM), BF16)
        for h in range(N_HEADS):
            fh = _split3(f_ref[:, h:h + 1])
            q_aux = _aux_lanes(tm, list(fh) + [1.0] * N_SPLIT)
            k_aux = _aux_lanes(tm, [1.0] * N_SPLIT + [-t for t in fh])
            qs_ref[:, h * SLOT:(h + 1) * SLOT] = jnp.concatenate(
                [(_head(q_ref, h).astype(F32) * SCALE).astype(BF16), q_aux.astype(BF16)], axis=1)
            ks_ref[:, h * SLOT:(h + 1) * SLOT] = jnp.concatenate([_head(k_ref, h), k_aux.astype(BF16)], axis=1)
            vs_ref[:, h * SLOT:(h + 1) * SLOT] = jnp.concatenate([_head(v_ref, h), ones], axis=1)

    col = lambda b: pl.BlockSpec((tm, ATT_W), lambda i: (i, b))
    wide = pl.BlockSpec((tm, N_HEADS * SLOT), lambda i: (i, 0))
    return pl.pallas_call(
        body, name="fox_pack_fwd", grid=(SEQ // tm,),
        in_specs=[col(0), col(1), col(2), pl.BlockSpec((tm, LANE), lambda i: (i, 0))],
        out_specs=[wide] * 3, out_shape=[_sds((SEQ, N_HEADS * SLOT), BF16)] * 3,
        compiler_params=_params("parallel"),
    )(zm, zm, zm, f_cols)


def _fox_pack_bwd(zm, f_cols, lse, o, do, *, tm=512):
    def body(q_ref, f_ref, lse_ref, o_ref, do_ref, qs_ref, ds_ref):
        for h in range(N_HEADS):
            gh = _split3(f_ref[:, h:h + 1] - lse_ref[:, h * HEAD_DIM:h * HEAD_DIM + 1])
            dout = _head(do_ref, h)
            delta = jnp.sum(_head(o_ref, h).astype(F32) * dout.astype(F32), axis=1, keepdims=True)
            q_aux = _aux_lanes(tm, list(gh) + [1.0] * N_SPLIT)
            d_aux = _aux_lanes(tm, [-t for t in _split3(delta)])
            qs_ref[:, h * SLOT:(h + 1) * SLOT] = jnp.concatenate(
                [(_head(q_ref, h).astype(F32) * SCALE).astype(BF16), q_aux.astype(BF16)], axis=1)
            ds_ref[:, h * SLOT:(h + 1) * SLOT] = jnp.concatenate([dout, d_aux.astype(BF16)], axis=1)

    row = pl.BlockSpec((tm, ATT_W), lambda i: (i, 0))
    wide = pl.BlockSpec((tm, N_HEADS * SLOT), lambda i: (i, 0))
    return pl.pallas_call(
        body, name="fox_pack_bwd", grid=(SEQ // tm,),
        in_specs=[row, pl.BlockSpec((tm, LANE), lambda i: (i, 0)), row, row, row],
        out_specs=[wide] * 2, out_shape=[_sds((SEQ, N_HEADS * SLOT), BF16)] * 2,
        compiler_params=_params("parallel"),
    )(zm, f_cols, lse, o, do)


def _causal_pairs(key_major):
    nb = SEQ // FOX_T
    if key_major:
        pairs = [(i, j) for j in range(nb) for i in range(j, nb)]
    else:
        pairs = [(i, j) for i in range(nb) for j in range(i + 1)]
    return (jnp.array([p[0] for p in pairs], jnp.int32), jnp.array([p[1] for p in pairs], jnp.int32), len(pairs))


def _diag_mask():
    row = lax.broadcasted_iota(jnp.int32, (FOX_T, FOX_T), 0)
    col = lax.broadcasted_iota(jnp.int32, (FOX_T, FOX_T), 1)
    return col <= row


def _fox_fwd(q_slots, k_slots, v_slots):
    i_tab, j_tab, n_pairs = _causal_pairs(False)

    def body(i_tab, j_tab, q_ref, k_ref, v_ref, o_ref, lse_ref, m_s, acc_s):
        t = pl.program_id(1)
        i, j = i_tab[t], j_tab[t]

        @pl.when(j == 0)
        def _():
            m_s[...] = jnp.full_like(m_s, NEG_INF)
            acc_s[...] = jnp.zeros_like(acc_s)

        def step(masked):
            scores = [lax.dot_general(_slot(q_ref, h), _slot(k_ref, h), NT_DIMS, preferred_element_type=F32)
                      for h in range(2)]
            probs, alphas = [], []
            for h in range(2):
                s = jnp.where(_diag_mask(), scores[h], NEG_INF) if masked else scores[h]
                m_prev = m_s[h]
                m_new = jnp.maximum(m_prev, jnp.max(s, axis=-1, keepdims=True))
                probs.append(jnp.exp(s - jnp.tile(m_new, (1, FOX_T // LANE))).astype(BF16))
                alphas.append(jnp.exp(m_prev - m_new))
                m_s[h] = m_new
            for h in range(2):
                acc_s[h] = alphas[h] * acc_s[h] + jnp.dot(probs[h], _slot(v_ref, h), preferred_element_type=F32)

        @pl.when(j < i)
        def _():
            step(False)

        @pl.when(j == i)
        def _():
            step(True)
            outs, lses = [], []
            for h in range(2):
                acc = acc_s[h]
                l = acc[:, HEAD_DIM:]
                outs.append(acc[:, :HEAD_DIM] / l)
                lses.append(m_s[h][:, :HEAD_DIM] + jnp.log(l))
            o_ref[...] = jnp.concatenate(outs, axis=1).astype(o_ref.dtype)
            lse_ref[...] = jnp.concatenate(lses, axis=1)

    qspec = pl.BlockSpec((FOX_T, 2 * SLOT), lambda p, t, it, jt: (it[t], p))
    kspec = pl.BlockSpec((FOX_T, 2 * SLOT), lambda p, t, it, jt: (jt[t], p))
    ospec = pl.BlockSpec((FOX_T, LANE), lambda p, t, it, jt: (it[t], p))
    return pl.pallas_call(
        body, name="fox_fwd",
        grid_spec=pltpu.PrefetchScalarGridSpec(
            num_scalar_prefetch=2, grid=(N_HEADS // 2, n_pairs),
            in_specs=[qspec, kspec, kspec], out_specs=[ospec, ospec],
            scratch_shapes=[pltpu.VMEM((2, FOX_T, LANE), F32), pltpu.VMEM((2, FOX_T, SLOT), F32)]),
        out_shape=[_sds((SEQ, ATT_W), BF16), _sds((SEQ, ATT_W), F32)],
        compiler_params=_params("parallel", "arbitrary"),
    )(i_tab, j_tab, q_slots, k_slots, v_slots)


def _fox_bwd(q_slots, k_slots, v_slots, do_slots):
    i_tab, j_tab, n_pairs = _causal_pairs(True)

    def body(i_tab, j_tab, q_ref, k_ref, v_ref, do_ref, dq_ref, dk_ref, dv_ref):
        t = pl.program_id(1)
        i, j = i_tab[t], j_tab[t]

        @pl.when(t == 0)
        def _():
            dq_ref[...] = jnp.zeros_like(dq_ref)

        @pl.when(i == j)
        def _():
            dk_ref[...] = jnp.zeros_like(dk_ref)
            dv_ref[...] = jnp.zeros_like(dv_ref)

        def step(masked):
            rows = pl.ds(pl.multiple_of(i * FOX_T, FOX_T), FOX_T)
            heads = range(2)
            scores = [lax.dot_general(_slot(q_ref, h), _slot(k_ref, h), NT_DIMS, preferred_element_type=F32)
                      for h in heads]
            dps = [lax.dot_general(_slot(do_ref, h), _slot(v_ref, h), NT_DIMS, preferred_element_type=F32)
                   for h in heads]
            ps, dss = [], []
            for h in heads:
                p = jnp.exp(scores[h])
                if masked:
                    p = jnp.where(_diag_mask(), p, 0.0)
                ps.append(p.astype(BF16))
                dss.append((p * dps[h]).astype(BF16))
            for h in heads:
                cols = slice(h * SLOT, (h + 1) * SLOT)
                dv_ref[:, cols] += lax.dot_general(ps[h], _slot(do_ref, h), TN_DIMS, preferred_element_type=F32)
                dk_ref[:, cols] += lax.dot_general(dss[h], _slot(q_ref, h), TN_DIMS, preferred_element_type=F32)
                dq_ref[rows, cols] += jnp.dot(dss[h], _slot(k_ref, h), preferred_element_type=F32)

        @pl.when(i > j)
        def _():
            step(False)

        @pl.when(i == j)
        def _():
            step(True)

    qspec = pl.BlockSpec((FOX_T, 2 * SLOT), lambda p, t, it, jt: (it[t], p))
    kspec = pl.BlockSpec((FOX_T, 2 * SLOT), lambda p, t, it, jt: (jt[t], p))
    return pl.pallas_call(
        body, name="fox_bwd",
        grid_spec=pltpu.PrefetchScalarGridSpec(
            num_scalar_prefetch=2, grid=(N_HEADS // 2, n_pairs),
            in_specs=[qspec, kspec, kspec, qspec],
            out_specs=[pl.BlockSpec((SEQ, 2 * SLOT), lambda p, t, it, jt: (0, p)), kspec, kspec]),
        out_shape=[_sds((SEQ, N_HEADS * SLOT), F32)] * 3,
        compiler_params=_params("arbitrary", "arbitrary"),
    )(i_tab, j_tab, q_slots, k_slots, v_slots, do_slots)


def _fox_unpack(dq_slots, dk_slots, dv_slots, *, tm=512):
    def body(dq_ref, dk_ref, dv_ref, o_ref, df_ref):
        lane = lax.broadcasted_iota(jnp.int32, (tm, LANE), 1)
        df = jnp.zeros((tm, LANE), F32)
        for h in range(N_HEADS):
            lo = h * SLOT
            for part, (ref, mult) in enumerate(((dq_ref, SCALE), (dk_ref, 1.0), (dv_ref, 1.0))):
                o_ref[:, part * ATT_W + h * HEAD_DIM:part * ATT_W + (h + 1) * HEAD_DIM] = (
                    ref[:, lo:lo + HEAD_DIM] * mult).astype(o_ref.dtype)
            rows = dq_ref[:, lo + HEAD_DIM:lo + HEAD_DIM + 1]
            cols = dk_ref[:, lo + HEAD_DIM + N_SPLIT:lo + HEAD_DIM + N_SPLIT + 1]
            df = jnp.where(lane == h, rows - cols, df)
        df_ref[...] = df

    wide = pl.BlockSpec((tm, N_HEADS * SLOT), lambda i: (i, 0))
    return pl.pallas_call(
        body, name="fox_unpack", grid=(SEQ // tm,), in_specs=[wide] * 3,
        out_specs=[pl.BlockSpec((tm, 3 * ATT_W), lambda i: (i, 0)), pl.BlockSpec((tm, LANE), lambda i: (i, 0))],
        out_shape=[_sds((SEQ, 3 * ATT_W), BF16), _sds((SEQ, LANE), F32)],
        compiler_params=_params("parallel"),
    )(dq_slots, dk_slots, dv_slots)


def _attn_delta(o, do, *, name, tm=512):
    def body(o_ref, do_ref, d_ref):
        prod = o_ref[...].astype(F32) * do_ref[...].astype(F32)
        lane = lax.broadcasted_iota(jnp.int32, (tm, LANE), 1)
        out = jnp.zeros((tm, LANE), F32)
        for h in range(N_HEADS):
            out = jnp.where(lane == h, jnp.sum(_head(prod, h), axis=1, keepdims=True), out)
        d_ref[...] = out

    row = pl.BlockSpec((tm, ATT_W), lambda i: (i, 0))
    return pl.pallas_call(
        body, name=name, grid=(SEQ // tm,), in_specs=[row, row],
        out_specs=pl.BlockSpec((tm, LANE), lambda i: (i, 0)), out_shape=_sds((SEQ, LANE), F32),
        compiler_params=_params("parallel"),
    )(o, do)


def _rope_tables():
    half = ROPE_DIM // 2
    inv_freq = np.float32(ROPE_THETA) ** (-np.arange(half, dtype=np.float32) * np.float32(2.0) / np.float32(ROPE_DIM))
    ang = np.arange(SEQ, dtype=np.float32)[:, None] * inv_freq.astype(np.float32)[None, :]
    cos, sin = jnp.asarray(np.cos(ang).astype(np.float32)), jnp.asarray(np.sin(ang).astype(np.float32))
    ones = jnp.ones((SEQ, HEAD_DIM - ROPE_DIM), F32)
    zeros = jnp.zeros((SEQ, HEAD_DIM - ROPE_DIM), F32)
    zh = jnp.zeros((SEQ, half), F32)
    c_tab = jnp.concatenate([cos, cos, ones], axis=1)
    a_tab = jnp.concatenate([-sin, zh, zeros], axis=1)
    b_tab = jnp.concatenate([zh, sin, zeros], axis=1)
    two = lambda t: jnp.concatenate([t, t], axis=1)
    return two(c_tab), two(a_tab), two(b_tab)


def _rotate(x, c_tab, a_tab, b_tab):
    return x * c_tab + pltpu.roll(x, LANE - ROPE_DIM // 2, 1) * a_tab + pltpu.roll(x, ROPE_DIM // 2, 1) * b_tab


def _rope_fwd(zm, tabs, *, tm=512):
    def body(q_ref, k_ref, v_ref, c_ref, a_ref, b_ref, o_ref):
        for part, (x_ref, mult) in enumerate(((q_ref, SCALE), (k_ref, 1.0))):
            for cc in range(ATT_W // LANE):
                sl = slice(cc * LANE, (cc + 1) * LANE)
                rot = _rotate(x_ref[:, sl].astype(F32), c_ref[...], a_ref[...], b_ref[...])
                o_ref[:, part * ATT_W + cc * LANE:part * ATT_W + (cc + 1) * LANE] = (rot * mult).astype(o_ref.dtype)
        o_ref[:, 2 * ATT_W:] = v_ref[...]

    tab = pl.BlockSpec((tm, LANE), lambda i: (i, 0))
    col = lambda b: pl.BlockSpec((tm, ATT_W), lambda i: (i, b))
    return pl.pallas_call(
        body, name="rope_fwd", grid=(SEQ // tm,),
        in_specs=[col(3), col(4), col(5), tab, tab, tab],
        out_specs=pl.BlockSpec((tm, 3 * ATT_W), lambda i: (i, 0)),
        out_shape=_sds((SEQ, 3 * ATT_W), BF16),
        compiler_params=_params("parallel"),
    )(zm, zm, zm, *tabs)


def _dil_grad_combine(dqs, dks, dvs, tabs, *, tm=256):
    def body(*refs):
        q_refs, k_refs, v_refs = refs[0:3], refs[3:6], refs[6:9]
        c_ref, a_ref, b_ref, o_ref = refs[9:]
        total = lambda rs, sl: rs[0][:, sl].astype(F32) + rs[1][:, sl].astype(F32) + rs[2][:, sl].astype(F32)
        for cc in range(ATT_W // LANE):
            sl = slice(cc * LANE, (cc + 1) * LANE)
            for part, rs in enumerate((q_refs, k_refs)):
                o_ref[:, part * ATT_W + cc * LANE:part * ATT_W + (cc + 1) * LANE] = _rotate(
                    total(rs, sl), c_ref[...], -a_ref[...], -b_ref[...]).astype(o_ref.dtype)
            o_ref[:, 2 * ATT_W + cc * LANE:2 * ATT_W + (cc + 1) * LANE] = total(v_refs, sl).astype(o_ref.dtype)

    row = pl.BlockSpec((tm, ATT_W), lambda i: (i, 0))
    tab = pl.BlockSpec((tm, LANE), lambda i: (i, 0))
    return pl.pallas_call(
        body, name="dil_grad_combine", grid=(SEQ // tm,),
        in_specs=[row] * 9 + [tab] * 3,
        out_specs=pl.BlockSpec((tm, 3 * ATT_W), lambda i: (i, 0)),
        out_shape=_sds((SEQ, 3 * ATT_W), BF16),
        compiler_params=_params("parallel"),
    )(*dqs, *dks, *dvs, *tabs)


def _dil_valid(n):
    qi = lax.broadcasted_iota(jnp.int32, (DIL_BLK, 2 * DIL_BLK), 0)
    ki = lax.broadcasted_iota(jnp.int32, (DIL_BLK, 2 * DIL_BLK), 1)
    dist = qi + DIL_BLK - ki
    return (dist >= 0) & (dist <= DIL_BLK) & ((n > 0) | (ki >= DIL_BLK))


def _dil_fwd(qkv, d):
    length = SEQ // d
    nb = length // DIL_BLK
    qkv_v = qkv.reshape(length, d * 3 * ATT_W)

    def body(q_ref, kp_ref, kc_ref, vp_ref, vc_ref, o_ref, lse_ref):
        n = pl.program_id(1)
        ok = _dil_valid(n)
        lane = lax.broadcasted_iota(jnp.int32, (DIL_BLK, LANE), 1)
        lse_all = jnp.zeros((DIL_BLK, LANE), F32)
        heads = range(N_HEADS)
        scores = [lax.dot_general(_head(q_ref, h), jnp.concatenate([_head(kp_ref, h), _head(kc_ref, h)], axis=0),
                                  NT_DIMS, preferred_element_type=F32) for h in heads]
        probs, inv_l = [], []
        for h in heads:
            s = jnp.where(ok, scores[h], NEG_INF)
            m = jnp.max(s, axis=-1, keepdims=True)
            p = jnp.exp(s - m)
            l = jnp.sum(p, axis=-1, keepdims=True)
            probs.append(p.astype(BF16))
            inv_l.append(1.0 / l)
            lse_all = jnp.where(lane == h, m + jnp.log(l), lse_all)
        outs = [jnp.dot(probs[h], jnp.concatenate([_head(vp_ref, h), _head(vc_ref, h)], axis=0),
                        preferred_element_type=F32) * inv_l[h] for h in heads]
        o_ref[...] = jnp.concatenate(outs, axis=1).astype(o_ref.dtype)
        lse_ref[...] = lse_all

    blk = lambda f: pl.BlockSpec((DIL_BLK, ATT_W), f)
    prev = lambda n: jnp.maximum(n - 1, 0)
    o, lse = pl.pallas_call(
        body, name=f"dil_fwd_d{d}", grid=(d, nb),
        in_specs=[blk(lambda r, n: (n, 3 * r)),
                  blk(lambda r, n: (prev(n), 3 * r + 1)), blk(lambda r, n: (n, 3 * r + 1)),
                  blk(lambda r, n: (prev(n), 3 * r + 2)), blk(lambda r, n: (n, 3 * r + 2))],
        out_specs=[blk(lambda r, n: (n, r)), pl.BlockSpec((DIL_BLK, LANE), lambda r, n: (n, r))],
        out_shape=[_sds((length, d * ATT_W), BF16), _sds((length, d * LANE), F32)],
        compiler_params=_params("parallel", "arbitrary"),
    )(qkv_v, qkv_v, qkv_v, qkv_v, qkv_v)
    return o.reshape(SEQ, ATT_W), lse.reshape(SEQ, LANE)


def _dil_merge(os_, lses, *, tm=512):
    def body(o0, o1, o2, l0, l1, l2, y_ref, lse_ref):
        ls = [l0[...], l1[...], l2[...]]
        m = jnp.maximum(jnp.maximum(ls[0], ls[1]), ls[2])
        es = [jnp.exp(l - m) for l in ls]
        tot = es[0] + es[1] + es[2]
        lse_ref[...] = m + jnp.log(tot)
        alphas = [e / tot for e in es]
        outs = []
        for h in range(N_HEADS):
            acc = None
            for g, o_ref in enumerate((o0, o1, o2)):
                term = alphas[g][:, h:h + 1] * _head(o_ref, h).astype(F32)
                acc = term if acc is None else acc + term
            outs.append(acc)
        y_ref[...] = jnp.concatenate(outs, axis=1).astype(y_ref.dtype)

    row = pl.BlockSpec((tm, ATT_W), lambda i: (i, 0))
    vec = pl.BlockSpec((tm, LANE), lambda i: (i, 0))
    return pl.pallas_call(
        body, name="dil_merge", grid=(SEQ // tm,),
        in_specs=[row] * 3 + [vec] * 3, out_specs=[row, vec],
        out_shape=[_sds((SEQ, ATT_W), BF16), _sds((SEQ, LANE), F32)],
        compiler_params=_params("parallel"),
    )(*os_, *lses)


def _dil_bwd(qkv, lse, delta, do, d):
    length = SEQ // d
    nb = length // DIL_BLK
    qkv_v = qkv.reshape(length, d * 3 * ATT_W)
    lse_v, dl_v, do_v = lse.reshape(length, d * LANE), delta.reshape(length, d * LANE), do.reshape(length, d * ATT_W)

    def body(q_ref, kp_ref, kc_ref, vp_ref, vc_ref, lse_ref, dl_ref, do_ref,
             dq_ref, dk_ref, dv_ref, ck_s, cv_s):
        n = pl.program_id(1)

        @pl.when(n == 0)
        def _():
            ck_s[...] = jnp.zeros_like(ck_s)
            cv_s[...] = jnp.zeros_like(cv_s)

        @pl.when(n < nb)
        def _():
            ok = _dil_valid(n)
            heads = range(N_HEADS)
            kks = [jnp.concatenate([_head(kp_ref, h), _head(kc_ref, h)], axis=0) for h in heads]
            scores = [lax.dot_general(_head(q_ref, h), kks[h], NT_DIMS, preferred_element_type=F32) for h in heads]
            dps = [lax.dot_general(_head(do_ref, h), jnp.concatenate([_head(vp_ref, h), _head(vc_ref, h)], axis=0),
                                   NT_DIMS, preferred_element_type=F32) for h in heads]
            ps, dss = [], []
            for h in heads:
                p = jnp.where(ok, jnp.exp(scores[h] - lse_ref[:, h:h + 1]), 0.0)
                ps.append(p.astype(BF16))
                dss.append((p * (dps[h] - dl_ref[:, h:h + 1])).astype(BF16))
            dqs = [jnp.dot(dss[h], kks[h], preferred_element_type=F32) * SCALE for h in heads]
            dkks = [lax.dot_general(dss[h], _head(q_ref, h), TN_DIMS, preferred_element_type=F32) for h in heads]
            dvvs = [lax.dot_general(ps[h], _head(do_ref, h), TN_DIMS, preferred_element_type=F32) for h in heads]
            dq_ref[...] = jnp.concatenate(dqs, axis=1).astype(dq_ref.dtype)
            dk_ref[...] = (ck_s[...] + jnp.concatenate([t[:DIL_BLK] for t in dkks], axis=1)).astype(dk_ref.dtype)
            dv_ref[...] = (cv_s[...] + jnp.concatenate([t[:DIL_BLK] for t in dvvs], axis=1)).astype(dv_ref.dtype)
            ck_s[...] = jnp.concatenate([t[DIL_BLK:] for t in dkks], axis=1)
            cv_s[...] = jnp.concatenate([t[DIL_BLK:] for t in dvvs], axis=1)

        @pl.when(n == nb)
        def _():
            dk_ref[...] = ck_s[...].astype(dk_ref.dtype)
            dv_ref[...] = cv_s[...].astype(dv_ref.dtype)

    blk = lambda f: pl.BlockSpec((DIL_BLK, ATT_W), f)
    vec = lambda f: pl.BlockSpec((DIL_BLK, LANE), f)
    cur = lambda n: jnp.minimum(n, nb - 1)
    prev = lambda n: jnp.maximum(cur(n) - 1, 0)
    back = lambda n: jnp.maximum(n - 1, 0)
    outs = pl.pallas_call(
        body, name=f"dil_bwd_d{d}", grid=(d, nb + 1),
        in_specs=[blk(lambda r, n: (cur(n), 3 * r)),
                  blk(lambda r, n: (prev(n), 3 * r + 1)), blk(lambda r, n: (cur(n), 3 * r + 1)),
                  blk(lambda r, n: (prev(n), 3 * r + 2)), blk(lambda r, n: (cur(n), 3 * r + 2)),
                  vec(lambda r, n: (cur(n), r)), vec(lambda r, n: (cur(n), r)),
                  blk(lambda r, n: (cur(n), r))],
        out_specs=[blk(lambda r, n: (cur(n), r)), blk(lambda r, n: (back(n), r)), blk(lambda r, n: (back(n), r))],
        out_shape=[_sds((length, d * ATT_W), BF16)] * 3,
        scratch_shapes=[pltpu.VMEM((DIL_BLK, ATT_W), F32), pltpu.VMEM((DIL_BLK, ATT_W), F32)],
        compiler_params=_params("arbitrary", "arbitrary"),
    )(qkv_v, qkv_v, qkv_v, qkv_v, qkv_v, lse_v, dl_v, do_v)
    return [t.reshape(SEQ, ATT_W) for t in outs]


def _sigmoid(x):
    return 1.0 / (1.0 + jnp.exp(-x))


def _mix_fwd(ya, yb, w_oa, w_ob, zm, *, tm=512):
    def body(ya_ref, yb_ref, wa_ref, wb_ref, ga_ref, gb_ref, pa_ref, pb_ref, mix_ref):
        pa = jnp.dot(ya_ref[...], wa_ref[...], preferred_element_type=F32)
        pb = jnp.dot(yb_ref[...], wb_ref[...], preferred_element_type=F32)
        pa_ref[...] = pa.astype(pa_ref.dtype)
        pb_ref[...] = pb.astype(pb_ref.dtype)
        mix_ref[...] = (_sigmoid(ga_ref[...].astype(F32)) * pa + _sigmoid(gb_ref[...].astype(F32)) * pb
                        ).astype(mix_ref.dtype)

    row = pl.BlockSpec((tm, ATT_W), lambda i: (i, 0))
    wsp = pl.BlockSpec((ATT_W, D_MODEL), lambda i: (0, 0))
    wide = pl.BlockSpec((tm, D_MODEL), lambda i: (i, 0))
    return pl.pallas_call(
        body, name="mix_fwd", grid=(SEQ // tm,),
        in_specs=[row, row, wsp, wsp, pl.BlockSpec((tm, D_MODEL), lambda i: (i, 3)),
                  pl.BlockSpec((tm, D_MODEL), lambda i: (i, 4))],
        out_specs=[wide] * 3, out_shape=[_sds((SEQ, D_MODEL), BF16)] * 3,
        compiler_params=_params("parallel"),
    )(ya, yb, w_oa, w_ob, zm, zm)


def _gate_bwd(dmix, zm, pa, pb, *, tm=512):
    def body(dm_ref, ga_ref, gb_ref, pa_ref, pb_ref, dpa_ref, dpb_ref, dg_ref):
        dm = dm_ref[...].astype(F32)
        sa, sb = _sigmoid(ga_ref[...].astype(F32)), _sigmoid(gb_ref[...].astype(F32))
        dpa_ref[...] = (dm * sa).astype(dpa_ref.dtype)
        dpb_ref[...] = (dm * sb).astype(dpb_ref.dtype)
        dg_ref[:, :D_MODEL] = (dm * pa_ref[...].astype(F32) * sa * (1.0 - sa)).astype(dg_ref.dtype)
        dg_ref[:, D_MODEL:] = (dm * pb_ref[...].astype(F32) * sb * (1.0 - sb)).astype(dg_ref.dtype)

    wide = pl.BlockSpec((tm, D_MODEL), lambda i: (i, 0))
    return pl.pallas_call(
        body, name="gate_bwd", grid=(SEQ // tm,),
        in_specs=[wide, pl.BlockSpec((tm, D_MODEL), lambda i: (i, 3)), pl.BlockSpec((tm, D_MODEL), lambda i: (i, 4)),
                  wide, wide],
        out_specs=[wide, wide, pl.BlockSpec((tm, 2 * D_MODEL), lambda i: (i, 0))],
        out_shape=[_sds((SEQ, D_MODEL), BF16), _sds((SEQ, D_MODEL), BF16), _sds((SEQ, 2 * D_MODEL), BF16)],
        compiler_params=_params("parallel"),
    )(dmix, zm, zm, pa, pb)


def _out_fwd(mixed, w_out, x, g_post, g_pre, *, tm=512):
    def body(m_ref, w_ref, x_ref, gp_ref, gn_ref, y_ref, x2_ref, h_ref):
        y = jnp.dot(m_ref[...], w_ref[...], preferred_element_type=F32)
        y_ref[...] = y
        r = lax.rsqrt(jnp.mean(y * y, axis=-1, keepdims=True) + RMS_EPS)
        x2 = x_ref[...] + y * r * gp_ref[...]
        x2_ref[...] = x2
        r2 = lax.rsqrt(jnp.mean(x2 * x2, axis=-1, keepdims=True) + RMS_EPS)
        h_ref[...] = (x2 * r2 * gn_ref[...]).astype(h_ref.dtype)

    row = pl.BlockSpec((tm, D_MODEL), lambda i: (i, 0))
    vec = pl.BlockSpec((1, D_MODEL), lambda i: (0, 0))
    return pl.pallas_call(
        body, name="out_fwd", grid=(SEQ // tm,),
        in_specs=[row, pl.BlockSpec((D_MODEL, D_MODEL), lambda i: (0, 0)), row, vec, vec],
        out_specs=[row] * 3,
        out_shape=[_sds((SEQ, D_MODEL), F32), _sds((SEQ, D_MODEL), F32), _sds((SEQ, D_MODEL), BF16)],
        compiler_params=_params("parallel"),
    )(mixed, w_out, x, g_post, g_pre)


FFN_TM = 512
FFN_HALF = 256
FFN_TN = 2 * FFN_HALF
FFN_NJ = D_FF // FFN_HALF
FFN_GROUP = 2 * SUBLANE


def _ffn_interleave(t):
    lead = t.shape[:-1]
    return jnp.swapaxes(t.reshape(*lead, 2, FFN_NJ, FFN_HALF), -3, -2).reshape(*lead, 2 * D_FF)


def _ffn_deinterleave(t):
    lead = t.shape[:-1]
    return jnp.swapaxes(t.reshape(*lead, FFN_NJ, 2, FFN_HALF), -3, -2).reshape(*lead, 2 * D_FF)


def _gelu_parts(a):
    c = math.sqrt(2.0 / math.pi)
    a2 = a * a
    t = jnp.tanh((c * a) * (1.0 + 0.044715 * a2))
    half_a, one_t = 0.5 * a, 1.0 + t
    gelu = half_a * one_t
    dgelu = 0.5 * one_t + half_a * (1.0 - t * t) * (c + (3.0 * 0.044715 * c) * a2)
    return gelu, dgelu


def _shift_down(cur, above, k):
    row = lax.broadcasted_iota(jnp.int32, cur.shape, 0)
    return jnp.where(row < k, pltpu.roll(above, k, 0), pltpu.roll(cur, k, 0))


def _shift_up(cur, below, k):
    row = lax.broadcasted_iota(jnp.int32, cur.shape, 0)
    return jnp.where(row >= SUBLANE - k, pltpu.roll(below, SUBLANE - k, 0), pltpu.roll(cur, SUBLANE - k, 0))


def _conv_consts(w_ref, b_ref):
    shape = (SUBLANE, FFN_TN)
    return [jnp.broadcast_to(w_ref[k:k + 1, :], shape) for k in range(3)] + [jnp.broadcast_to(b_ref[...], shape)]


def _conv_taps(cur, above, consts):
    w0, w1, w2, bias = consts
    s1, s2 = _shift_down(cur, above, 1), _shift_down(cur, above, 2)
    return w0 * s2 + w1 * s1 + w2 * cur + bias, s1, s2


def _ffn_mid_fwd(u, conv_w, conv_b):
    per = FFN_TM // SUBLANE

    def body(u_ref, h_ref, w_ref, b_ref, m_ref, ab_ref):
        live = (pl.program_id(1) > 0).astype(F32)
        consts = _conv_consts(w_ref, b_ref)

        def group(g, above):
            rows = pl.ds(pl.multiple_of(g * FFN_GROUP, FFN_GROUP), FFN_GROUP)
            x = u_ref[rows, :].astype(F32)
            convs = []
            for c in range(2):
                cur = x[c * SUBLANE:(c + 1) * SUBLANE]
                convs.append(_conv_taps(cur, above, consts)[0])
                above = cur
            y = jnp.concatenate(convs, axis=0)
            ab_ref[rows, :] = y.astype(ab_ref.dtype)
            m_ref[rows, :] = (_gelu_parts(y[:, :FFN_HALF])[0] * y[:, FFN_HALF:]).astype(m_ref.dtype)
            return above

        lax.fori_loop(0, FFN_TM // FFN_GROUP, group, h_ref[...].astype(F32) * live)

    blk = pl.BlockSpec((FFN_TM, FFN_TN), lambda j, i: (i, j))
    return pl.pallas_call(
        body, name="ffn_mid_fwd", grid=(FFN_NJ, SEQ // FFN_TM),
        in_specs=[blk, pl.BlockSpec((SUBLANE, FFN_TN), lambda j, i: (jnp.maximum(i * per - 1, 0), j)),
                  pl.BlockSpec((3, FFN_TN), lambda j, i: (0, j)), pl.BlockSpec((1, FFN_TN), lambda j, i: (0, j))],
        out_specs=[pl.BlockSpec((FFN_TM, FFN_HALF), lambda j, i: (i, j)), blk],
        out_shape=[_sds((SEQ, D_FF), BF16), _sds((SEQ, 2 * D_FF), BF16)],
        compiler_params=_params("parallel", "arbitrary"),
    )(u, u, conv_w, conv_b)


def _ffn_mid_bwd(dm, u, ab, conv_w):
    nrow = SEQ // FFN_TM
    n_groups = FFN_TM // FFN_GROUP

    def body(dm_ref, u_ref, ab_ref, w_ref, du_ref, gw_ref, gb_ref, c_s):
        @pl.when(pl.program_id(1) == 0)
        def _():
            c_s[...] = jnp.zeros_like(c_s)
            gw_ref[...] = jnp.zeros_like(gw_ref)
            gb_ref[...] = jnp.zeros_like(gb_ref)

        taps = [jnp.broadcast_to(w_ref[k:k + 1, :], (SUBLANE, FFN_TN)) for k in range(3)]

        def group(t, carry):
            below, acc = carry
            rows = pl.ds(pl.multiple_of((n_groups - 1 - t) * FFN_GROUP, FFN_GROUP), FFN_GROUP)
            x, y, dmv = u_ref[rows, :].astype(F32), ab_ref[rows, :].astype(F32), dm_ref[rows, :].astype(F32)
            gelu, dgelu = _gelu_parts(y[:, :FFN_HALF])
            d = jnp.concatenate([dmv * y[:, FFN_HALF:] * dgelu, dmv * gelu], axis=1)
            acc, pre = list(acc), [None, None]
            for c in (1, 0):
                sl = slice(c * SUBLANE, (c + 1) * SUBLANE)
                cur, xs = d[sl], x[sl]
                up1, up2 = _shift_up(cur, below, 1), _shift_up(cur, below, 2)
                acc = [acc[0] + up2 * xs, acc[1] + up1 * xs, acc[2] + cur * xs, acc[3] + cur]
                pre[c] = taps[2] * cur + taps[1] * up1 + taps[0] * up2
                below = cur
            du_ref[rows, :] = jnp.concatenate(pre, axis=0).astype(du_ref.dtype)
            return below, tuple(acc)

        zeros = jnp.zeros((SUBLANE, FFN_TN), F32)
        below, acc = lax.fori_loop(0, n_groups, group, (c_s[...], (zeros,) * 4))
        c_s[...] = below
        for k in range(3):
            gw_ref[k:k + 1, :] += jnp.sum(acc[k], axis=0, keepdims=True)
        gb_ref[...] += jnp.sum(acc[3], axis=0, keepdims=True)

    blk = pl.BlockSpec((FFN_TM, FFN_TN), lambda j, i: (nrow - 1 - i, j))
    return pl.pallas_call(
        body, name="ffn_mid_bwd", grid=(FFN_NJ, nrow),
        in_specs=[pl.BlockSpec((FFN_TM, FFN_HALF), lambda j, i: (nrow - 1 - i, j)), blk, blk,
                  pl.BlockSpec((3, FFN_TN), lambda j, i: (0, j))],
        out_specs=[blk, pl.BlockSpec((3, FFN_TN), lambda j, i: (0, j)), pl.BlockSpec((1, FFN_TN), lambda j, i: (0, j))],
        out_shape=[_sds((SEQ, 2 * D_FF), BF16), _sds((3, 2 * D_FF), F32), _sds((1, 2 * D_FF), F32)],
        scratch_shapes=[pltpu.VMEM((SUBLANE, FFN_TN), F32)],
        compiler_params=_params("parallel", "arbitrary"),
    )(dm, u, ab, conv_w)


def _down_fwd(m, w_down, x2, g_post, target, *, tm=512):
    def body(m_ref, w_ref, x2_ref, g_ref, t_ref, dout_ref, dy_ref, gg_ref, loss_ref):
        @pl.when(pl.program_id(0) == 0)
        def _():
            gg_ref[...] = jnp.zeros_like(gg_ref)
            loss_ref[...] = jnp.zeros_like(loss_ref)

        y = jnp.dot(m_ref[...], w_ref[...], preferred_element_type=F32)
        r = lax.rsqrt(jnp.mean(y * y, axis=-1, keepdims=True) + RMS_EPS)
        yn = y * r
        diff = (x2_ref[...] + yn * g_ref[...]) - t_ref[...]
        loss_ref[...] += jnp.sum(diff * diff)
        dout = diff * (1.0 / D_MODEL)
        dout_ref[...] = dout
        gg_ref[...] += jnp.sum(dout * yn, axis=0, keepdims=True)
        dn = dout * g_ref[...]
        dy_ref[...] = (r * (dn - yn * jnp.mean(dn * yn, axis=-1, keepdims=True))).astype(dy_ref.dtype)

    row = pl.BlockSpec((tm, D_MODEL), lambda i: (i, 0))
    vec = pl.BlockSpec((1, D_MODEL), lambda i: (0, 0))
    return pl.pallas_call(
        body, name="down_fwd", grid=(SEQ // tm,),
        in_specs=[pl.BlockSpec((tm, D_FF), lambda i: (i, 0)), pl.BlockSpec((D_FF, D_MODEL), lambda i: (0, 0)),
                  row, vec, row],
        out_specs=[row, row, vec, pl.BlockSpec((1, LANE), lambda i: (0, 0))],
        out_shape=[_sds((SEQ, D_MODEL), F32), _sds((SEQ, D_MODEL), BF16), _sds((1, D_MODEL), F32),
                   _sds((1, LANE), F32)],
        compiler_params=_params("arbitrary"),
    )(m, w_down, x2, g_post, target)


def _local_step(x, target, w_main, w_f, b_forget, conv_b, g_pre_mix, g_post_mix, g_pre_ffn, g_post_ffn,
                late_weights, ffn_grads_ready, proj_grads_ready, mixer_grads_ready):
    mm = _matmul
    tabs = _rope_tables()

    h1 = _rms_fwd(x, g_pre_mix, name="rms_pre_mix")
    zm = mm(h1, w_main, out_dtype=BF16, tm=2048, tn=512, tk=1024, name="in_proj")
    zf = mm(h1, w_f, out_dtype=F32, tm=2048, tn=F_PAD, tk=1024, name="in_proj_forget")
    f_row, sg_row = _fox_prep(zf[:, :N_HEADS].T, b_forget.reshape(N_HEADS, 1))
    f_cols = jnp.pad(f_row.T, ((0, 0), (0, LANE - N_HEADS)))
    q_slots, k_slots, v_slots = _fox_pack_fwd(zm, f_cols)
    ya, lse_a = _fox_fwd(q_slots, k_slots, v_slots)
    qkv_d = _rope_fwd(zm, tabs)
    dil = [_dil_fwd(qkv_d, d) for _, d in DIL_PATTERNS]
    yb, lse_b = _dil_merge([o for o, _ in dil], [l for _, l in dil])
    w_oa, w_ob, w_out, w_up, conv_w, w_down = late_weights(yb)
    pa, pb, mixed = _mix_fwd(ya, yb, w_oa, w_ob, zm)
    y1, x2, h2 = _out_fwd(mixed, w_out, x, g_post_mix, g_pre_ffn)
    u = mm(h2, w_up, out_dtype=BF16, tm=2048, tn=512, tk=1024, name="up_proj")
    m, ab = _ffn_mid_fwd(u, conv_w, _ffn_interleave(conv_b))
    dout, dy2, gg_post_ffn, sq_err = _down_fwd(m, w_down, x2, g_post_ffn, target)

    g_w_down = mm(m, dy2, ta=True, out_dtype=BF16, tm=D_FF // 2, tn=1024, tk=2048, name="grad_w_down")
    dm = mm(dy2, w_down, tb=True, out_dtype=BF16, tm=2048, tn=D_FF // 2, tk=1024, name="d_ffn_mid")
    du, g_conv_w, g_conv_b = _ffn_mid_bwd(dm, u, ab, conv_w)
    g_w_up = mm(h2, du, ta=True, out_dtype=BF16, tm=1024, tn=D_FF // 2, tk=2048, name="grad_w_up")
    tok = ffn_grads_ready(dict(w_down=g_w_down, w_up=_ffn_deinterleave(g_w_up), conv_w=_ffn_deinterleave(g_conv_w)))
    dh2 = mm(du, w_up, tb=True, out_dtype=BF16, tm=512, tn=1024, tk=2 * D_FF, name="d_h2")

    dx2, dy1, gg_pre_ffn, gg_post_mix = _rms_pair_bwd([dh2], x2, g_pre_ffn, dout, y1, g_post_mix + tok)
    g_w_out = mm(mixed, dy1, ta=True, out_dtype=BF16, tm=1024, tn=1024, tk=2048, name="grad_w_out")
    dmix = mm(dy1, w_out, tb=True, out_dtype=BF16, tm=2048, tn=1024, tk=1024, name="d_mixed")
    dpa, dpb, dgates = _gate_bwd(dmix, zm, pa, pb)
    g_w_oa = mm(ya, dpa, ta=True, out_dtype=BF16, tm=512, tn=1024, tk=SEQ, name="grad_w_o_fox")
    g_w_ob = mm(yb, dpb, ta=True, out_dtype=BF16, tm=512, tn=1024, tk=SEQ, name="grad_w_o_dil")
    tok = proj_grads_ready(dict(w_o_fox=g_w_oa, w_o_dil=g_w_ob, w_out=g_w_out))
    dya = mm(dpa, w_oa, tb=True, out_dtype=BF16, tm=2048, tn=512, tk=1024, name="d_y_fox")
    dyb = mm(dpb, w_ob, tb=True, out_dtype=BF16, tm=2048, tn=512, tk=1024, name="d_y_dil")

    qb_slots, do_slots = _fox_pack_bwd(zm, f_cols + tok, lse_a, ya, dya)
    d_fox, df_cols = _fox_unpack(*_fox_bwd(qb_slots, k_slots, v_slots, do_slots))
    dfa_t, g_b_forget = _fox_post_bwd(df_cols[:, :N_HEADS].T, sg_row)

    delta_b = _attn_delta(yb, dyb, name="delta_dil")
    dil_g = [_dil_bwd(qkv_d, lse_b, delta_b, dyb, d) for _, d in DIL_PATTERNS]
    d_dil = _dil_grad_combine([g[0] for g in dil_g], [g[1] for g in dil_g], [g[2] for g in dil_g], tabs)

    dz = jnp.concatenate([d_fox, d_dil, dgates], axis=1)
    dzf = jnp.pad(dfa_t.T, ((0, 0), (0, F_PAD - N_HEADS)))
    g_w_main = mm(h1, dz, ta=True, out_dtype=BF16, tm=1024, tn=Z_MAIN // 4, tk=2048, name="grad_w_in")
    g_w_f = mm(h1, dzf, ta=True, out_dtype=BF16, tm=1024, tn=F_PAD, tk=1024, name="grad_w_in_forget")
    tok = mixer_grads_ready(dict(w_main=g_w_main, w_f=g_w_f))
    dh1 = [mm(dz, w_main, tb=True, out_dtype=BF16, tm=512, tn=1024, tk=Z_MAIN, name="d_h1"),
           mm(dzf + tok, w_f, tb=True, out_dtype=F32, tm=2048, tn=1024, tk=F_PAD, name="d_h1_forget")]
    grad_x, gg_pre_mix = _rms_bwd(dh1, x, g_pre_mix, dx2, out_dtype=F32, name="rms_pre_mix_bwd")

    grads = dict(
        b_forget=g_b_forget.reshape(1, N_HEADS), conv_b=_ffn_deinterleave(g_conv_b),
        g_pre_mix=gg_pre_mix, g_post_mix=gg_post_mix, g_pre_ffn=gg_pre_ffn, g_post_ffn=gg_post_ffn)
    return sq_err[0, 0], grad_x, grads


def _exchange(arrays, scatter, *, name):
    n = len(arrays)

    def body(*refs):
        ins, outs = refs[:n], refs[n:2 * n]
        send_sems, recv_sems, local_sems = refs[2 * n:]
        x, y, c = lax.axis_index("x"), lax.axis_index("y"), lax.axis_index("c")
        me = 4 * x + 2 * y + c
        peers = []
        for k in range(1, N_DEV):
            px = 1 - x if k & 4 else x
            py = 1 - y if k & 2 else y
            pc = 1 - c if k & 1 else c
            peers.append(((px, py, pc), 4 * px + 2 * py + pc))

        def remote(a, k):
            dev, slot = peers[k]
            return pltpu.make_async_remote_copy(
                src_ref=ins[a].at[slot] if scatter else ins[a], dst_ref=outs[a].at[me],
                send_sem=send_sems.at[a, k], recv_sem=recv_sems.at[a, k],
                device_id=dev, device_id_type=MESH_ID)

        def landed(a, k):
            dev, slot = peers[k]
            return pltpu.make_async_remote_copy(
                src_ref=outs[a].at[slot], dst_ref=outs[a].at[slot],
                send_sem=send_sems.at[a, k], recv_sem=recv_sems.at[a, k],
                device_id=dev, device_id_type=MESH_ID)

        own = [pltpu.make_async_copy(ins[a].at[me] if scatter else ins[a], outs[a].at[me], local_sems.at[a])
               for a in range(n)]
        copies = [remote(a, k) for k in range(N_DEV - 1) for a in range(n)]
        for cp in own + copies:
            cp.start()
        for k in range(N_DEV - 1):
            for a in range(n):
                landed(a, k).wait_recv()
        for cp in copies:
            cp.wait_send()
        for cp in own:
            cp.wait()

    out_shape = [_sds(((N_DEV,) + a.shape[-2:]), a.dtype) for a in arrays]
    return pl.pallas_call(
        body, name=name, in_specs=[ANY] * n, out_specs=[ANY] * n, out_shape=out_shape,
        scratch_shapes=[pltpu.SemaphoreType.DMA((n, N_DEV - 1)), pltpu.SemaphoreType.DMA((n, N_DEV - 1)),
                        pltpu.SemaphoreType.DMA((n,))],
    )(*arrays)


def _gather_two_level(shard, *, name):
    def body(x_ref, out_ref, send_sems, recv_sems, local_sem):
        x, y, c = lax.axis_index("x"), lax.axis_index("y"), lax.axis_index("c")
        me, sibling = (x, y, c), (x, y, 1 - c)
        chips = [(1 - x, y), (x, 1 - y), (1 - x, 1 - y)]

        def slot(px, py, pc):
            return out_ref.at[4 * px + 2 * py + pc]

        def copy(k, block, to, src=None):
            return pltpu.make_async_remote_copy(
                src_ref=slot(*block) if src is None else src, dst_ref=slot(*block),
                send_sem=send_sems.at[k], recv_sem=recv_sems.at[k], device_id=to, device_id_type=MESH_ID)

        mine = pltpu.make_async_copy(x_ref, slot(*me), local_sem)
        mine.start()
        first = [copy(0, me, sibling, src=x_ref)]
        first += [copy(1 + j, me, (*chip, c), src=x_ref) for j, chip in enumerate(chips)]
        for cp in first:
            cp.start()
        passed = [copy(4 + j, (*chip, c), sibling) for j, chip in enumerate(chips)]
        for j, chip in enumerate(chips):
            copy(1 + j, (*chip, c), me).wait_recv()
            passed[j].start()
        copy(0, sibling, me).wait_recv()
        for j, chip in enumerate(chips):
            copy(4 + j, (*chip, 1 - c), me).wait_recv()
        for cp in first + passed:
            cp.wait_send()
        mine.wait()

    return pl.pallas_call(
        body, name=name, in_specs=[ANY], out_specs=ANY, out_shape=_sds((N_DEV,) + shard.shape, shard.dtype),
        scratch_shapes=[pltpu.SemaphoreType.DMA((N_DEV - 1,)), pltpu.SemaphoreType.DMA((N_DEV - 1,)),
                        pltpu.SemaphoreType.DMA],
    )(shard)


N_CHIPS = N_DEV // 2


def _peers(chips_only=False):
    x, y, c = lax.axis_index("x"), lax.axis_index("y"), lax.axis_index("c")
    out = []
    if chips_only:
        for k in range(1, N_CHIPS):
            px = 1 - x if k & 2 else x
            py = 1 - y if k & 1 else y
            out.append(((px, py, c), 2 * px + py))
        return 2 * x + y, out
    for k in range(1, N_DEV):
        px = 1 - x if k & 4 else x
        py = 1 - y if k & 2 else y
        pc = 1 - c if k & 1 else c
        out.append(((px, py, pc), 4 * px + 2 * py + pc))
    return 4 * x + 2 * y + c, out


def _sibling_swap(slots, *, name):
    def body(in_ref, out_ref, send_sems, recv_sems):
        x, y, c = lax.axis_index("x"), lax.axis_index("y"), lax.axis_index("c")
        copies = [pltpu.make_async_remote_copy(
            src_ref=in_ref.at[2 * q + (1 - c)], dst_ref=out_ref.at[q], send_sem=send_sems.at[q],
            recv_sem=recv_sems.at[q], device_id=(x, y, 1 - c), device_id_type=MESH_ID) for q in range(N_CHIPS)]
        for cp in copies:
            cp.start()
        for cp in copies:
            cp.wait_recv()
        for cp in copies:
            cp.wait_send()

    return pl.pallas_call(
        body, name=name, in_specs=[ANY], out_specs=ANY,
        out_shape=_sds((N_CHIPS,) + slots.shape[1:], slots.dtype),
        scratch_shapes=[pltpu.SemaphoreType.DMA((N_CHIPS,)), pltpu.SemaphoreType.DMA((N_CHIPS,))],
    )(slots)


def _pair_sum(slots, from_sibling, *, name, tm=256):
    _, r, c = slots.shape
    core = lax.axis_index("c").astype(jnp.int32).reshape(1)

    def body(core_ref, a_ref, b_ref, o_ref):
        o_ref[...] = (a_ref[...].astype(F32) + b_ref[...].astype(F32)).astype(o_ref.dtype)

    blk = lambda f: pl.BlockSpec((1, tm, c), f)
    return pl.pallas_call(
        body, name=name,
        grid_spec=pltpu.PrefetchScalarGridSpec(
            num_scalar_prefetch=1, grid=(N_CHIPS, r // tm),
            in_specs=[blk(lambda q, i, core: (2 * q + core[0], i, 0)), blk(lambda q, i, core: (q, i, 0))],
            out_specs=blk(lambda q, i, core: (q, i, 0))),
        out_shape=_sds((N_CHIPS, r, c), slots.dtype),
        compiler_params=_params("parallel", "parallel"),
    )(core, slots, from_sibling)


HBM = pl.BlockSpec(memory_space=pltpu.HBM)
SEM = pl.BlockSpec(memory_space=pltpu.SEMAPHORE)
DATAFLOW = pltpu.SideEffectType.DATAFLOW_SIDE_EFFECTING


def _split_copy(srcs, lands, send_sems, recv_sems, scatter, a, k, me, peers, incoming=False):
    dev, slot = peers[k]
    if incoming:
        src = dst = lands[a].at[slot]
    else:
        src, dst = (srcs[a].at[slot] if scatter else srcs[a]), lands[a].at[me]
    sem = a * len(peers) + k
    return pltpu.make_async_remote_copy(
        src_ref=src, dst_ref=dst, send_sem=send_sems.at[sem], recv_sem=recv_sems.at[sem],
        device_id=dev, device_id_type=MESH_ID)


def _exchange_start(arrays, scatter, *, name, chips_only=False):
    n = len(arrays)
    n_slots = N_CHIPS if chips_only else N_DEV

    def body(*refs):
        srcs, lands = refs[:n], refs[n:2 * n]
        send_sems, recv_sems = refs[2 * n], refs[2 * n + 1]
        token = refs[-1]
        me, peers = _peers(chips_only)
        for k in range(len(peers)):
            for a in range(n):
                _split_copy(srcs, lands, send_sems, recv_sems, scatter, a, k, me, peers).start()
        token[...] = jnp.zeros_like(token)

    land_shapes = [((n_slots,) + a.shape[-2:], a.dtype) for a in arrays]
    sems = pltpu.SemaphoreType.DMA((n * (n_slots - 1),))
    outs = pl.pallas_call(
        body, name=name,
        out_shape=(sems, sems, *[pltpu.HBM(a.shape, a.dtype) for a in arrays],
                   *[pltpu.HBM(s, d) for s, d in land_shapes], _sds((SUBLANE, LANE), F32)),
        in_specs=[HBM] * (2 * n),
        out_specs=(SEM, SEM, *[HBM] * (2 * n), pl.BlockSpec(memory_space=pltpu.VMEM)),
        input_output_aliases={i: 2 + i for i in range(2 * n)},
        compiler_params=pltpu.CompilerParams(has_side_effects=DATAFLOW),
    )(*[pltpu.with_memory_space_constraint(a, pltpu.HBM) for a in arrays],
      *[pltpu.with_memory_space_constraint(lax.empty(s, d), pltpu.HBM) for s, d in land_shapes])
    return (outs[0], outs[1], outs[2:2 + n], outs[2 + n:2 + 2 * n], scatter, chips_only), outs[-1]


def _exchange_wait(handles, after, *, name):
    send_sems, recv_sems, srcs, lands, scatter, chips_only = handles
    n = len(srcs)

    def body(*refs):
        src_refs, land_refs = refs[:n], refs[n:2 * n]
        send_ref, recv_ref = refs[2 * n], refs[2 * n + 1]
        me, peers = _peers(chips_only)
        for k in range(len(peers)):
            for a in range(n):
                _split_copy(src_refs, land_refs, send_ref, recv_ref, scatter, a, k, me, peers).wait_send()
                _split_copy(src_refs, land_refs, send_ref, recv_ref, scatter, a, k, me, peers, True).wait_recv()

    outs = pl.pallas_call(
        body, name=name,
        out_shape=tuple(pltpu.HBM(t.shape, t.dtype) for t in (*srcs, *lands)),
        in_specs=[HBM] * (2 * n) + [SEM, SEM, pl.BlockSpec(memory_space=pl.ANY)],
        out_specs=tuple([HBM] * (2 * n)),
        input_output_aliases={i: i for i in range(2 * n)},
        compiler_params=pltpu.CompilerParams(has_side_effects=DATAFLOW),
    )(*srcs, *lands, send_sems, recv_sems, after)
    return _with_own_slot(outs[n:], outs[:n], scatter, chips_only)


def _with_own_slot(landed, own, scatter, chips_only):
    me = 2 * lax.axis_index("x") + lax.axis_index("y")
    if not chips_only:
        me = 2 * me + lax.axis_index("c")
    out = []
    for buf, src in zip(landed, own):
        mine = lax.dynamic_index_in_dim(src, me, 0, keepdims=False) if scatter else src
        out.append(lax.dynamic_update_index_in_dim(buf, mine, me, 0))
    return out


def _adamw(parts, w, m, v, *, name, tm):
    r, c = w.shape
    assert r % tm == 0

    def body(p_ref, w_ref, m_ref, v_ref, g_ref, d_ref, nm_ref, nv_ref):
        _adamw_update(p_ref, w_ref, m_ref, v_ref, g_ref, d_ref, nm_ref, nv_ref)

    blk = pl.BlockSpec((tm, c), lambda i: (i, 0))
    return pl.pallas_call(
        body, name=name, grid=(r // tm,),
        in_specs=[pl.BlockSpec((parts.shape[0], tm, c), lambda i: (0, i, 0)), blk, blk, blk],
        out_specs=[blk] * 4, out_shape=[_sds((r, c), F32)] * 4,
        compiler_params=_params("parallel"),
    )(parts, w, m, v)


def _adamw_update(p_ref, w_ref, m_ref, v_ref, g_ref, d_ref, nm_ref, nv_ref):
    g = p_ref[0].astype(F32)
    for s in range(1, p_ref.shape[0]):
        g = g + p_ref[s].astype(F32)
    g_ref[...] = g
    m_new = ADAM_B1 * m_ref[...] + (1.0 - ADAM_B1) * g
    v_new = ADAM_B2 * v_ref[...] + (1.0 - ADAM_B2) * (g * g)
    nm_ref[...] = m_new
    nv_ref[...] = v_new
    m_hat = m_new / (1.0 - ADAM_B1 ** ADAM_STEP)
    v_hat = v_new / (1.0 - ADAM_B2 ** ADAM_STEP)
    d_ref[...] = -ADAM_LR * (m_hat / (jnp.sqrt(v_hat) + ADAM_EPS) + ADAM_WD * w_ref[...])


SMALL = ("g_pre_mix", "b_forget", "g_post_mix", "g_pre_ffn", "conv_b", "g_post_ffn")


def _adamw_small(parts, ws, ms, vs):
    n = len(ws)

    def body(*refs):
        ins, outs = refs[:4 * n], refs[4 * n:]
        for i in range(n):
            _adamw_update(ins[i], ins[n + i], ins[2 * n + i], ins[3 * n + i], *outs[4 * i:4 * i + 4])

    res = pl.pallas_call(
        body, name="adamw_small", out_shape=[_sds(w.shape, F32) for w in ws for _ in range(4)],
        compiler_params=pltpu.CompilerParams(vmem_limit_bytes=VMEM_LIMIT),
    )(*parts, *ws, *ms, *vs)
    return [res[4 * i:4 * i + 4] for i in range(n)]


def kernel(x, g_pre_mix, w_in, b_forget, w_o_fox, w_o_dil, w_out, g_post_mix, g_pre_ffn, w_up, conv_w, conv_b, w_down, g_post_ffn, loss_target, m_g_pre_mix, m_w_in, m_b_forget, m_w_o_fox, m_w_o_dil, m_w_out, m_g_post_mix, m_g_pre_ffn, m_w_up, m_conv_w, m_conv_b, m_w_down, m_g_post_ffn, v_g_pre_mix, v_w_in, v_b_forget, v_w_o_fox, v_w_o_dil, v_w_out, v_g_post_mix, v_g_pre_ffn, v_w_up, v_conv_w, v_conv_b, v_w_down, v_g_post_ffn):
    names = ("g_pre_mix", "w_in", "b_forget", "w_o_fox", "w_o_dil", "w_out", "g_post_mix", "g_pre_ffn",
             "w_up", "conv_w", "conv_b", "w_down", "g_post_ffn")
    w = dict(g_pre_mix=g_pre_mix, w_in=w_in, b_forget=b_forget, w_o_fox=w_o_fox, w_o_dil=w_o_dil, w_out=w_out,
             g_post_mix=g_post_mix, g_pre_ffn=g_pre_ffn, w_up=w_up, conv_w=conv_w, conv_b=conv_b, w_down=w_down,
             g_post_ffn=g_post_ffn)
    m = dict(g_pre_mix=m_g_pre_mix, w_in=m_w_in, b_forget=m_b_forget, w_o_fox=m_w_o_fox, w_o_dil=m_w_o_dil,
             w_out=m_w_out, g_post_mix=m_g_post_mix, g_pre_ffn=m_g_pre_ffn, w_up=m_w_up, conv_w=m_conv_w,
             conv_b=m_conv_b, w_down=m_w_down, g_post_ffn=m_g_post_ffn)
    v = dict(g_pre_mix=v_g_pre_mix, w_in=v_w_in, b_forget=v_b_forget, w_o_fox=v_w_o_fox, w_o_dil=v_w_o_dil,
             w_out=v_w_out, g_post_mix=v_g_post_mix, g_pre_ffn=v_g_pre_ffn, w_up=v_w_up, conv_w=v_conv_w,
             conv_b=v_conv_b, w_down=v_w_down, g_post_ffn=v_g_post_ffn)
    sharded = ("w_in", "w_o_fox", "w_o_dil", "w_out", "w_up", "w_down", "conv_w")
    wire = lambda n: F32 if n == "conv_w" else BF16

    by_cols = lambda t: jnp.transpose(t, (1, 0, 2)).reshape(t.shape[1], N_DEV * t.shape[2])
    by_rows = lambda t: t.reshape(N_DEV * t.shape[1], t.shape[2])
    col_slots = lambda t: jnp.transpose(t.reshape(t.shape[0], N_DEV, t.shape[1] // N_DEV), (1, 0, 2))
    row_slots = lambda t: t.reshape(N_DEV, t.shape[0] // N_DEV, t.shape[1])
    to_slots = lambda n, t: (row_slots if n in ("w_out", "w_down") else col_slots)(t).astype(wire(n))
    shard = lambda n: w[n][0].astype(wire(n))
    f_lo, f_hi = 3 * ATT_W, 3 * ATT_W + N_HEADS

    w_in_full = by_cols(_gather_two_level(shard("w_in"), name="gather_w_in"))
    w_main = jnp.concatenate([w_in_full[:, :f_lo], w_in_full[:, f_hi:]], axis=1)
    w_f = jnp.pad(w_in_full[:, f_lo:f_hi], ((0, 0), (0, F_PAD - N_HEADS)))
    late = ("w_o_fox", "w_o_dil", "w_out", "w_up", "conv_w", "w_down")
    order = jnp.minimum(jnp.abs(w_in_full[0, 0].astype(F32)), 0.0)
    late_handles, late_tok = _exchange_start(
        [shard(n) + order.astype(wire(n)) if n == "conv_w" else shard(n) for n in late], False,
        name="gather_late_start")

    def late_weights(after):
        got = dict(zip(late, _exchange_wait(late_handles, after, name="gather_late_wait")))
        return (by_cols(got["w_o_fox"]), by_cols(got["w_o_dil"]), by_rows(got["w_out"]),
                _ffn_interleave(by_cols(got["w_up"])), _ffn_interleave(by_cols(got["conv_w"])),
                by_rows(got["w_down"]))

    pending = {}

    def ffn_grads_ready(g):
        pending["ffn"] = _exchange_start([to_slots(n, g[n]) for n in ("w_down", "w_up", "conv_w")], True,
                                         name="scatter_ffn_start")
        return pending["ffn"][1][0, 0]

    def proj_grads_ready(g):
        pending["proj"] = _exchange_start([to_slots(n, g[n]) for n in ("w_o_fox", "w_o_dil", "w_out")], True,
                                          name="scatter_proj_start")
        return pending["proj"][1][0, 0]

    def mixer_grads_ready(g):
        g_w_in = jnp.concatenate([g["w_main"][:, :f_lo], g["w_f"][:, :N_HEADS], g["w_main"][:, f_lo:]], axis=1)
        slots = to_slots("w_in", g_w_in)
        chip_sums = _pair_sum(slots, _sibling_swap(slots, name="scatter_w_in_swap"), name="scatter_w_in_pair_sum")
        pending["w_in"] = _exchange_start([chip_sums], True, name="scatter_w_in_start", chips_only=True)
        return pending["w_in"][1][0, 0]

    sq_err, grad_x, g = _local_step(
        x[0], loss_target[0], w_main, w_f, b_forget, conv_b, g_pre_mix + late_tok[0, 0], g_post_mix, g_pre_ffn,
        g_post_ffn, late_weights, ffn_grads_ready, proj_grads_ready, mixer_grads_ready)
    loss = lax.psum(0.5 * sq_err / D_MODEL, ("x", "y", "c"))

    tiles = dict(w_in=256, w_o_fox=512, w_o_dil=512, w_out=128, w_up=256, w_down=176, conv_w=3)
    adam = lambda n, p: _adamw(p, w[n][0], m[n][0], v[n][0], name=f"adamw_{n}", tm=tiles[n])
    res = {}
    for key, group in (("ffn", ("w_down", "w_up", "conv_w")), ("proj", ("w_o_fox", "w_o_dil", "w_out"))):
        landed = _exchange_wait(pending[key][0], grad_x, name=f"scatter_{key}_wait")
        res.update({n: adam(n, p) for n, p in zip(group, landed)})
    small_parts = _exchange([g[n] for n in SMALL], False, name="gather_small_grads")
    done = res["w_up"][3]
    res["w_in"] = adam("w_in", _exchange_wait(pending["w_in"][0], done, name="scatter_w_in_wait")[0])
    small = dict(zip(SMALL, _adamw_small(small_parts, *[[t[n] for n in SMALL] for t in (w, m, v)])))
    out = [[(res[n][k][None] if n in sharded else small[n][k]) for n in names] for k in range(4)]
    return (loss, grad_x[None], *out[0], *out[1], *out[2], *out[3])
```

```python
import functools
import math

import jax
import jax.numpy as jnp
import numpy as np
from jax import lax
from jax.experimental import pallas as pl
from jax.experimental.pallas import tpu as pltpu

F32 = jnp.float32
BF16 = jnp.bfloat16

SEQ = 4096
D_MODEL = 1024
N_HEADS = 8
HEAD_DIM = 64
ATT_W = N_HEADS * HEAD_DIM
D_FF = 2816
Z_MAIN = 5120
F_PAD = 128
ROPE_DIM = 16
ROPE_THETA = 500000.0
RMS_EPS = 1e-6
NEG_INF = -1e30
SCALE = 1.0 / math.sqrt(HEAD_DIM)
DIL_PATTERNS = ((128, 1), (512, 4), (2048, 16))
DIL_BLK = 128
N_DEV = 8

ADAM_LR = 0.001
ADAM_B1 = 0.9
ADAM_B2 = 0.999
ADAM_EPS = 1e-08
ADAM_WD = 0.01
ADAM_STEP = 10

LANE = 128
SUBLANE = 8
VMEM_LIMIT = 56 * 1024 * 1024
MESH_ID = pl.DeviceIdType.MESH
ANY = pl.BlockSpec(memory_space=pl.ANY)


def _params(*sem):
    return pltpu.CompilerParams(dimension_semantics=sem, vmem_limit_bytes=VMEM_LIMIT)


def _sds(shape, dtype):
    return jax.ShapeDtypeStruct(shape, dtype)


def _matmul(a, b, *, ta=False, tb=False, out_dtype, tm, tn, tk, name, b_k_off=0):
    if ta:
        kk, m = a.shape
    else:
        m, kk = a.shape
    n = b.shape[0] if tb else b.shape[1]
    tm, tn, tk = min(tm, m), min(tn, n), min(tk, kk)
    assert (b.shape[1] if tb else b.shape[0]) >= b_k_off * tk + kk
    assert m % tm == 0 and n % tn == 0 and kk % tk == 0, (name, m, n, kk, tm, tn, tk)
    nk = kk // tk
    dims = (((0 if ta else 1,), (1 if tb else 0,)), ((), ()))

    def body(a_ref, b_ref, o_ref, *scratch):
        p = lax.dot_general(a_ref[...].astype(BF16), b_ref[...].astype(BF16), dims,
                            preferred_element_type=F32)
        if nk == 1:
            o_ref[...] = p.astype(o_ref.dtype)
        else:
            acc = scratch[0]
            k = pl.program_id(2)

            @pl.when(k == 0)
            def _():
                acc[...] = p

            @pl.when(k > 0)
            def _():
                acc[...] += p

            @pl.when(k == nk - 1)
            def _():
                o_ref[...] = acc[...].astype(o_ref.dtype)

    a_spec = (pl.BlockSpec((tk, tm), lambda i, j, k: (k, i)) if ta
              else pl.BlockSpec((tm, tk), lambda i, j, k: (i, k)))
    b_spec = (pl.BlockSpec((tn, tk), lambda i, j, k: (j, k + b_k_off)) if tb
              else pl.BlockSpec((tk, tn), lambda i, j, k: (k + b_k_off, j)))
    return pl.pallas_call(
        body, name=name, grid=(m // tm, n // tn, nk),
        in_specs=[a_spec, b_spec],
        out_specs=pl.BlockSpec((tm, tn), lambda i, j, k: (i, j)),
        out_shape=_sds((m, n), out_dtype),
        scratch_shapes=[pltpu.VMEM((tm, tn), F32)] if nk > 1 else [],
        compiler_params=_params("parallel", "parallel", "arbitrary"),
    )(a, b)


def _rms_fwd(x, g, *, name, tm=512):
    def body(x_ref, g_ref, h_ref):
        xv = x_ref[...]
        r = lax.rsqrt(jnp.mean(xv * xv, axis=-1, keepdims=True) + RMS_EPS)
        h_ref[...] = (xv * r * g_ref[...]).astype(h_ref.dtype)

    return pl.pallas_call(
        body, name=name, grid=(SEQ // tm,),
        in_specs=[pl.BlockSpec((tm, D_MODEL), lambda i: (i, 0)), pl.BlockSpec((1, D_MODEL), lambda i: (0, 0))],
        out_specs=pl.BlockSpec((tm, D_MODEL), lambda i: (i, 0)),
        out_shape=_sds((SEQ, D_MODEL), BF16),
        compiler_params=_params("parallel"),
    )(x, g)


def _rms_bwd(dh_parts, xin, g, dres, *, out_dtype, name, tm=512):
    n_parts = len(dh_parts)
    has_res = dres is not None

    def body(*refs):
        parts = refs[:n_parts]
        x_ref, g_ref = refs[n_parts], refs[n_parts + 1]
        res_ref = refs[n_parts + 2] if has_res else None
        o_ref, gg_ref = refs[-2], refs[-1]
        dh = parts[0][...].astype(F32)
        for p in parts[1:]:
            dh = dh + p[...].astype(F32)
        xv = x_ref[...]
        r = lax.rsqrt(jnp.mean(xv * xv, axis=-1, keepdims=True) + RMS_EPS)
        xn = xv * r

        @pl.when(pl.program_id(0) == 0)
        def _():
            gg_ref[...] = jnp.zeros_like(gg_ref)

        gg_ref[...] += jnp.sum(dh * xn, axis=0, keepdims=True)
        dxn = dh * g_ref[...]
        dx = r * (dxn - xn * jnp.mean(dxn * xn, axis=-1, keepdims=True))
        if has_res:
            dx = dx + res_ref[...]
        o_ref[...] = dx.astype(o_ref.dtype)

    row = pl.BlockSpec((tm, D_MODEL), lambda i: (i, 0))
    vec = pl.BlockSpec((1, D_MODEL), lambda i: (0, 0))
    args = list(dh_parts) + [xin, g] + ([dres] if has_res else [])
    return pl.pallas_call(
        body, name=name, grid=(SEQ // tm,),
        in_specs=[row] * n_parts + [row, vec] + ([row] if has_res else []),
        out_specs=[row, vec],
        out_shape=[_sds((SEQ, D_MODEL), out_dtype), _sds((1, D_MODEL), F32)],
        compiler_params=_params("arbitrary"),
    )(*args)


def _rms_pair_bwd(dh_parts, x2, g_pre, dres, y1, g_post, *, tm=512):
    n_parts = len(dh_parts)

    def norm_bwd(dh, xin, g_ref, gg_ref):
        r = lax.rsqrt(jnp.mean(xin * xin, axis=-1, keepdims=True) + RMS_EPS)
        xn = xin * r
        gg_ref[...] += jnp.sum(dh * xn, axis=0, keepdims=True)
        dxn = dh * g_ref[...]
        return r * (dxn - xn * jnp.mean(dxn * xn, axis=-1, keepdims=True))

    def body(*refs):
        parts = refs[:n_parts]
        x2_ref, gpre_ref, res_ref, y1_ref, gpost_ref, dx2_ref, dy1_ref, ggpre_ref, ggpost_ref = refs[n_parts:]

        @pl.when(pl.program_id(0) == 0)
        def _():
            ggpre_ref[...] = jnp.zeros_like(ggpre_ref)
            ggpost_ref[...] = jnp.zeros_like(ggpost_ref)

        dh = parts[0][...].astype(F32)
        for p in parts[1:]:
            dh = dh + p[...].astype(F32)
        dx2 = res_ref[...] + norm_bwd(dh, x2_ref[...], gpre_ref, ggpre_ref)
        dx2_ref[...] = dx2
        dy1_ref[...] = norm_bwd(dx2, y1_ref[...], gpost_ref, ggpost_ref).astype(dy1_ref.dtype)

    row = pl.BlockSpec((tm, D_MODEL), lambda i: (i, 0))
    vec = pl.BlockSpec((1, D_MODEL), lambda i: (0, 0))
    return pl.pallas_call(
        body, name="rms_pair_bwd", grid=(SEQ // tm,),
        in_specs=[row] * n_parts + [row, vec, row, row, vec],
        out_specs=[row, row, vec, vec],
        out_shape=[_sds((SEQ, D_MODEL), F32), _sds((SEQ, D_MODEL), BF16), _sds((1, D_MODEL), F32),
                   _sds((1, D_MODEL), F32)],
        compiler_params=_params("arbitrary"),
    )(*dh_parts, x2, g_pre, dres, y1, g_post)


SCAN_BLK = 512


def _split_dot(v, tri):
    hi = v.astype(BF16)
    r1 = v - hi.astype(F32)
    mid = r1.astype(BF16)
    lo = (r1 - mid.astype(F32)).astype(BF16)
    dot = functools.partial(jnp.dot, preferred_element_type=F32)
    return dot(hi, tri) + dot(mid, tri) + dot(lo, tri)


def _fox_prep(fa_t, b_col):
    nblk = SEQ // SCAN_BLK

    def body(fa_ref, b_ref, f_ref, sg_ref):
        row = lax.broadcasted_iota(jnp.int32, (SCAN_BLK, SCAN_BLK), 0)
        col = lax.broadcasted_iota(jnp.int32, (SCAN_BLK, SCAN_BLK), 1)
        upper = (row <= col).astype(BF16)
        carry = jnp.zeros((N_HEADS, 1), F32)
        for blk in range(nblk):
            sl = pl.ds(blk * SCAN_BLK, SCAN_BLK)
            xx = fa_ref[:, sl] + b_ref[...]
            e = jnp.exp(-jnp.abs(xx))
            logf = jnp.minimum(xx, 0.0) - jnp.log(1.0 + e)
            sg_ref[:, sl] = jnp.where(xx >= 0.0, e, 1.0) / (1.0 + e)
            c = _split_dot(logf, upper) + carry
            f_ref[:, sl] = c
            carry = c[:, SCAN_BLK - 1:SCAN_BLK]

    return pl.pallas_call(
        body, name="fox_prep",
        out_shape=[_sds((N_HEADS, SEQ), F32), _sds((N_HEADS, SEQ), F32)],
        compiler_params=pltpu.CompilerParams(vmem_limit_bytes=VMEM_LIMIT),
    )(fa_t, b_col)


def _fox_post_bwd(df_t, sg_t):
    nblk = SEQ // SCAN_BLK

    def body(df_ref, sg_ref, dfa_ref, gb_ref):
        row = lax.broadcasted_iota(jnp.int32, (SCAN_BLK, SCAN_BLK), 0)
        col = lax.broadcasted_iota(jnp.int32, (SCAN_BLK, SCAN_BLK), 1)
        lower = (row >= col).astype(BF16)
        carry = jnp.zeros((N_HEADS, 1), F32)
        gb = jnp.zeros((N_HEADS, 1), F32)
        for blk in reversed(range(nblk)):
            sl = pl.ds(blk * SCAN_BLK, SCAN_BLK)
            c = _split_dot(df_ref[:, sl], lower) + carry
            carry = c[:, 0:1]
            dfa = c * sg_ref[:, sl]
            dfa_ref[:, sl] = dfa
            gb = gb + jnp.sum(dfa, axis=1, keepdims=True)
        gb_ref[...] = gb

    return pl.pallas_call(
        body, name="fox_post_bwd",
        out_shape=[_sds((N_HEADS, SEQ), F32), _sds((N_HEADS, 1), F32)],
        compiler_params=pltpu.CompilerParams(vmem_limit_bytes=VMEM_LIMIT),
    )(df_t, sg_t)


FOX_T = 512
NT_DIMS = (((1,), (1,)), ((), ()))
TN_DIMS = (((0,), (0,)), ((), ()))


def _head(ref_or_val, h):
    return ref_or_val[:, h * HEAD_DIM:(h + 1) * HEAD_DIM]


def _split3(v):
    hi = v.astype(BF16).astype(F32)
    r1 = v - hi
    mid = r1.astype(BF16).astype(F32)
    return hi, mid, (r1 - mid).astype(BF16).astype(F32)


ONE_LANE = 3 * N_HEADS


def _pack_terms(v, with_one):
    hi, mid, lo = _split3(v)
    t = hi + pltpu.roll(mid, N_HEADS, 1) + pltpu.roll(lo, 2 * N_HEADS, 1)
    if with_one:
        t = t + (lax.broadcasted_iota(jnp.int32, v.shape, 1) == ONE_LANE).astype(F32)
    return t.astype(BF16)


def _aux_matrices():
    to_q = np.zeros((LANE, N_HEADS * 2 * HEAD_DIM), np.float32)
    to_k = np.zeros_like(to_q)
    for h in range(N_HEADS):
        base = h * 2 * HEAD_DIM + HEAD_DIM
        for s in range(3):
            to_q[s * N_HEADS + h, base + s] = 1.0
            to_q[ONE_LANE, base + 3 + s] = 1.0
            to_k[ONE_LANE, base + s] = 1.0
            to_k[s * N_HEADS + h, base + 3 + s] = -1.0
    return jnp.asarray(to_q, BF16), jnp.asarray(to_k, BF16)


def _head_sums():
    total = np.zeros((N_HEADS * HEAD_DIM, LANE), np.float32)
    first = np.zeros_like(total)
    for h in range(N_HEADS):
        total[h * HEAD_DIM:(h + 1) * HEAD_DIM, h] = 1.0
        first[h * HEAD_DIM, h] = 1.0
    return jnp.asarray(total, BF16), jnp.asarray(first, BF16)


SLOT = 2 * HEAD_DIM
N_SPLIT = 3


def _slot(ref, h):
    return ref[:, h * SLOT:(h + 1) * SLOT]


def _fox_pack_fwd(zm, f_cols, *, tm=512):
    def body(q_ref, k_ref, v_ref, f_ref, tq_ref, tk_ref, qs_ref, ks_ref, vs_ref):
        ones = jnp.ones((tm, HEAD_DIM), BF16)
        terms = _pack_terms(f_ref[...], True)
        q_aux = jnp.dot(terms, tq_ref[...], preferred_element_type=F32).astype(BF16)
        k_aux = jnp.dot(terms, tk_ref[...], preferred_element_type=F32).astype(BF16)
        for h in range(N_HEADS):
            aux = slice(h * SLOT + HEAD_DIM, (h + 1) * SLOT)
            qs_ref[:, h * SLOT:(h + 1) * SLOT] = jnp.concatenate(
                [(_head(q_ref, h).astype(F32) * SCALE).astype(BF16), q_aux[:, aux]], axis=1)
            ks_ref[:, h * SLOT:(h + 1) * SLOT] = jnp.concatenate([_head(k_ref, h), k_aux[:, aux]], axis=1)
            vs_ref[:, h * SLOT:(h + 1) * SLOT] = jnp.concatenate([_head(v_ref, h), ones], axis=1)

    col = lambda b: pl.BlockSpec((tm, ATT_W), lambda i: (i, b))
    wide = pl.BlockSpec((tm, N_HEADS * SLOT), lambda i: (i, 0))
    const = pl.BlockSpec((LANE, N_HEADS * SLOT), lambda i: (0, 0))
    return pl.pallas_call(
        body, name="fox_pack_fwd", grid=(SEQ // tm,),
        in_specs=[col(0), col(1), col(2), pl.BlockSpec((tm, LANE), lambda i: (i, 0)), const, const],
        out_specs=[wide] * 3, out_shape=[_sds((SEQ, N_HEADS * SLOT), BF16)] * 3,
        compiler_params=_params("parallel"),
    )(zm, zm, zm, f_cols, *_aux_matrices())


def _fox_pack_bwd(zm, f_cols, lse, o, do, *, tm=512):
    def body(q_ref, f_ref, lse_ref, o_ref, do_ref, tq_ref, total_ref, first_ref, qs_ref, ds_ref):
        delta = _split_dot(o_ref[...].astype(F32) * do_ref[...].astype(F32), total_ref[...])
        lse_h = _split_dot(lse_ref[...], first_ref[...])
        q_aux = jnp.dot(_pack_terms(f_ref[...] - lse_h, True), tq_ref[...], preferred_element_type=F32).astype(BF16)
        d_aux = jnp.dot(_pack_terms(-delta, False), tq_ref[...], preferred_element_type=F32).astype(BF16)
        for h in range(N_HEADS):
            aux = slice(h * SLOT + HEAD_DIM, (h + 1) * SLOT)
            qs_ref[:, h * SLOT:(h + 1) * SLOT] = jnp.concatenate(
                [(_head(q_ref, h).astype(F32) * SCALE).astype(BF16), q_aux[:, aux]], axis=1)
            ds_ref[:, h * SLOT:(h + 1) * SLOT] = jnp.concatenate([_head(do_ref, h), d_aux[:, aux]], axis=1)

    row = pl.BlockSpec((tm, ATT_W), lambda i: (i, 0))
    wide = pl.BlockSpec((tm, N_HEADS * SLOT), lambda i: (i, 0))
    const = lambda r, c: pl.BlockSpec((r, c), lambda i: (0, 0))
    return pl.pallas_call(
        body, name="fox_pack_bwd", grid=(SEQ // tm,),
        in_specs=[row, pl.BlockSpec((tm, LANE), lambda i: (i, 0)), row, row, row,
                  const(LANE, N_HEADS * SLOT), const(ATT_W, LANE), const(ATT_W, LANE)],
        out_specs=[wide] * 2, out_shape=[_sds((SEQ, N_HEADS * SLOT), BF16)] * 2,
        compiler_params=_params("parallel"),
    )(zm, f_cols, lse, o, do, _aux_matrices()[0], *_head_sums())


def _causal_pairs(key_major):
    nb = SEQ // FOX_T
    if key_major:
        pairs = [(i, j) for j in range(nb) for i in range(j, nb)]
    else:
        pairs = [(i, j) for i in range(nb) for j in range(i + 1)]
    return (jnp.array([p[0] for p in pairs], jnp.int32), jnp.array([p[1] for p in pairs], jnp.int32), len(pairs))


def _diag_mask():
    row = lax.broadcasted_iota(jnp.int32, (FOX_T, FOX_T), 0)
    col = lax.broadcasted_iota(jnp.int32, (FOX_T, FOX_T), 1)
    return col <= row


def _fox_fwd(q_slots, k_slots, v_slots):
    i_tab, j_tab, n_pairs = _causal_pairs(False)

    def body(i_tab, j_tab, q_ref, k_ref, v_ref, o_ref, lse_ref, m_s, acc_s):
        t = pl.program_id(1)
        i, j = i_tab[t], j_tab[t]

        @pl.when(j == 0)
        def _():
            m_s[...] = jnp.full_like(m_s, NEG_INF)
            acc_s[...] = jnp.zeros_like(acc_s)

        def step(masked):
            scores = [lax.dot_general(_slot(q_ref, h), _slot(k_ref, h), NT_DIMS, preferred_element_type=F32)
                      for h in range(2)]
            probs, alphas = [], []
            for h in range(2):
                s = jnp.where(_diag_mask(), scores[h], NEG_INF) if masked else scores[h]
                m_prev = m_s[h]
                m_new = jnp.maximum(m_prev, jnp.max(s, axis=-1, keepdims=True))
                probs.append(jnp.exp(s - jnp.tile(m_new, (1, FOX_T // LANE))).astype(BF16))
                alphas.append(jnp.exp(m_prev - m_new))
                m_s[h] = m_new
            for h in range(2):
                acc_s[h] = alphas[h] * acc_s[h] + jnp.dot(probs[h], _slot(v_ref, h), preferred_element_type=F32)

        @pl.when(j < i)
        def _():
            step(False)

        @pl.when(j == i)
        def _():
            step(True)
            outs, lses = [], []
            for h in range(2):
                acc = acc_s[h]
                l = acc[:, HEAD_DIM:]
                outs.append(acc[:, :HEAD_DIM] / l)
                lses.append(m_s[h][:, :HEAD_DIM] + jnp.log(l))
            o_ref[...] = jnp.concatenate(outs, axis=1).astype(o_ref.dtype)
            lse_ref[...] = jnp.concatenate(lses, axis=1)

    qspec = pl.BlockSpec((FOX_T, 2 * SLOT), lambda p, t, it, jt: (it[t], p))
    kspec = pl.BlockSpec((FOX_T, 2 * SLOT), lambda p, t, it, jt: (jt[t], p))
    ospec = pl.BlockSpec((FOX_T, LANE), lambda p, t, it, jt: (it[t], p))
    return pl.pallas_call(
        body, name="fox_fwd",
        grid_spec=pltpu.PrefetchScalarGridSpec(
            num_scalar_prefetch=2, grid=(N_HEADS // 2, n_pairs),
            in_specs=[qspec, kspec, kspec], out_specs=[ospec, ospec],
            scratch_shapes=[pltpu.VMEM((2, FOX_T, LANE), F32), pltpu.VMEM((2, FOX_T, SLOT), F32)]),
        out_shape=[_sds((SEQ, ATT_W), BF16), _sds((SEQ, ATT_W), F32)],
        compiler_params=_params("parallel", "arbitrary"),
    )(i_tab, j_tab, q_slots, k_slots, v_slots)


def _fox_bwd(q_slots, k_slots, v_slots, do_slots):
    i_tab, j_tab, n_pairs = _causal_pairs(True)

    def body(i_tab, j_tab, q_ref, k_ref, v_ref, do_ref, dq_ref, dk_ref, dv_ref):
        t = pl.program_id(1)
        i, j = i_tab[t], j_tab[t]

        @pl.when(t == 0)
        def _():
            dq_ref[...] = jnp.zeros_like(dq_ref)

        @pl.when(i == j)
        def _():
            dk_ref[...] = jnp.zeros_like(dk_ref)
            dv_ref[...] = jnp.zeros_like(dv_ref)

        def step(masked):
            rows = pl.ds(pl.multiple_of(i * FOX_T, FOX_T), FOX_T)
            heads = range(2)
            scores = [lax.dot_general(_slot(q_ref, h), _slot(k_ref, h), NT_DIMS, preferred_element_type=F32)
                      for h in heads]
            dps = [lax.dot_general(_slot(do_ref, h), _slot(v_ref, h), NT_DIMS, preferred_element_type=F32)
                   for h in heads]
            ps, dss = [], []
            for h in heads:
                p = jnp.exp(scores[h])
                if masked:
                    p = jnp.where(_diag_mask(), p, 0.0)
                ps.append(p.astype(BF16))
                dss.append((p * dps[h]).astype(BF16))
            for h in heads:
                cols = slice(h * SLOT, (h + 1) * SLOT)
                dv_ref[:, cols] += lax.dot_general(ps[h], _slot(do_ref, h), TN_DIMS, preferred_element_type=F32)
                dk_ref[:, cols] += lax.dot_general(dss[h], _slot(q_ref, h), TN_DIMS, preferred_element_type=F32)
                dq_ref[rows, cols] += jnp.dot(dss[h], _slot(k_ref, h), preferred_element_type=F32)

        @pl.when(i > j)
        def _():
            step(False)

        @pl.when(i == j)
        def _():
            step(True)

    qspec = pl.BlockSpec((FOX_T, 2 * SLOT), lambda p, t, it, jt: (it[t], p))
    kspec = pl.BlockSpec((FOX_T, 2 * SLOT), lambda p, t, it, jt: (jt[t], p))
    return pl.pallas_call(
        body, name="fox_bwd",
        grid_spec=pltpu.PrefetchScalarGridSpec(
            num_scalar_prefetch=2, grid=(N_HEADS // 2, n_pairs),
            in_specs=[qspec, kspec, kspec, qspec],
            out_specs=[pl.BlockSpec((SEQ, 2 * SLOT), lambda p, t, it, jt: (0, p)), kspec, kspec]),
        out_shape=[_sds((SEQ, N_HEADS * SLOT), F32)] * 3,
        compiler_params=_params("arbitrary", "arbitrary"),
    )(i_tab, j_tab, q_slots, k_slots, v_slots, do_slots)


def _fox_unpack(dq_slots, dk_slots, dv_slots, *, tm=512):
    def body(dq_ref, dk_ref, dv_ref, o_ref, df_ref):
        lane = lax.broadcasted_iota(jnp.int32, (tm, LANE), 1)
        df = jnp.zeros((tm, LANE), F32)
        for h in range(N_HEADS):
            lo = h * SLOT
            for part, (ref, mult) in enumerate(((dq_ref, SCALE), (dk_ref, 1.0), (dv_ref, 1.0))):
                o_ref[:, part * ATT_W + h * HEAD_DIM:part * ATT_W + (h + 1) * HEAD_DIM] = (
                    ref[:, lo:lo + HEAD_DIM] * mult).astype(o_ref.dtype)
            rows = dq_ref[:, lo + HEAD_DIM:lo + HEAD_DIM + 1]
            cols = dk_ref[:, lo + HEAD_DIM + N_SPLIT:lo + HEAD_DIM + N_SPLIT + 1]
            df = jnp.where(lane == h, rows - cols, df)
        df_ref[...] = df

    wide = pl.BlockSpec((tm, N_HEADS * SLOT), lambda i: (i, 0))
    return pl.pallas_call(
        body, name="fox_unpack", grid=(SEQ // tm,), in_specs=[wide] * 3,
        out_specs=[pl.BlockSpec((tm, 3 * ATT_W), lambda i: (i, 0)), pl.BlockSpec((tm, LANE), lambda i: (i, 0))],
        out_shape=[_sds((SEQ, 3 * ATT_W), BF16), _sds((SEQ, LANE), F32)],
        compiler_params=_params("parallel"),
    )(dq_slots, dk_slots, dv_slots)


def _attn_delta(o, do, *, name, tm=512):
    def body(o_ref, do_ref, d_ref):
        prod = o_ref[...].astype(F32) * do_ref[...].astype(F32)
        lane = lax.broadcasted_iota(jnp.int32, (tm, LANE), 1)
        out = jnp.zeros((tm, LANE), F32)
        for h in range(N_HEADS):
            out = jnp.where(lane == h, jnp.sum(_head(prod, h), axis=1, keepdims=True), out)
        d_ref[...] = out

    row = pl.BlockSpec((tm, ATT_W), lambda i: (i, 0))
    return pl.pallas_call(
        body, name=name, grid=(SEQ // tm,), in_specs=[row, row],
        out_specs=pl.BlockSpec((tm, LANE), lambda i: (i, 0)), out_shape=_sds((SEQ, LANE), F32),
        compiler_params=_params("parallel"),
    )(o, do)


def _rope_tables():
    half = ROPE_DIM // 2
    inv_freq = np.float32(ROPE_THETA) ** (-np.arange(half, dtype=np.float32) * np.float32(2.0) / np.float32(ROPE_DIM))
    ang = np.arange(SEQ, dtype=np.float32)[:, None] * inv_freq.astype(np.float32)[None, :]
    cos, sin = jnp.asarray(np.cos(ang).astype(np.float32)), jnp.asarray(np.sin(ang).astype(np.float32))
    ones = jnp.ones((SEQ, HEAD_DIM - ROPE_DIM), F32)
    zeros = jnp.zeros((SEQ, HEAD_DIM - ROPE_DIM), F32)
    zh = jnp.zeros((SEQ, half), F32)
    c_tab = jnp.concatenate([cos, cos, ones], axis=1)
    a_tab = jnp.concatenate([-sin, zh, zeros], axis=1)
    b_tab = jnp.concatenate([zh, sin, zeros], axis=1)
    two = lambda t: jnp.concatenate([t, t], axis=1)
    return two(c_tab), two(a_tab), two(b_tab)


def _rotate(x, c_tab, a_tab, b_tab):
    return x * c_tab + pltpu.roll(x, LANE - ROPE_DIM // 2, 1) * a_tab + pltpu.roll(x, ROPE_DIM // 2, 1) * b_tab


def _rope_fwd(zm, tabs, *, tm=512):
    def body(q_ref, k_ref, v_ref, c_ref, a_ref, b_ref, o_ref):
        for part, (x_ref, mult) in enumerate(((q_ref, SCALE), (k_ref, 1.0))):
            for cc in range(ATT_W // LANE):
                sl = slice(cc * LANE, (cc + 1) * LANE)
                rot = _rotate(x_ref[:, sl].astype(F32), c_ref[...], a_ref[...], b_ref[...])
                o_ref[:, part * ATT_W + cc * LANE:part * ATT_W + (cc + 1) * LANE] = (rot * mult).astype(o_ref.dtype)
        o_ref[:, 2 * ATT_W:] = v_ref[...]

    tab = pl.BlockSpec((tm, LANE), lambda i: (i, 0))
    col = lambda b: pl.BlockSpec((tm, ATT_W), lambda i: (i, b))
    return pl.pallas_call(
        body, name="rope_fwd", grid=(SEQ // tm,),
        in_specs=[col(3), col(4), col(5), tab, tab, tab],
        out_specs=pl.BlockSpec((tm, 3 * ATT_W), lambda i: (i, 0)),
        out_shape=_sds((SEQ, 3 * ATT_W), BF16),
        compiler_params=_params("parallel"),
    )(zm, zm, zm, *tabs)


def _dil_grad_combine(dqs, dks, dvs, tabs, *, tm=256):
    def body(*refs):
        q_refs, k_refs, v_refs = refs[0:3], refs[3:6], refs[6:9]
        c_ref, a_ref, b_ref, o_ref = refs[9:]
        total = lambda rs, sl: rs[0][:, sl].astype(F32) + rs[1][:, sl].astype(F32) + rs[2][:, sl].astype(F32)
        for cc in range(ATT_W // LANE):
            sl = slice(cc * LANE, (cc + 1) * LANE)
            for part, rs in enumerate((q_refs, k_refs)):
                o_ref[:, part * ATT_W + cc * LANE:part * ATT_W + (cc + 1) * LANE] = _rotate(
                    total(rs, sl), c_ref[...], -a_ref[...], -b_ref[...]).astype(o_ref.dtype)
            o_ref[:, 2 * ATT_W + cc * LANE:2 * ATT_W + (cc + 1) * LANE] = total(v_refs, sl).astype(o_ref.dtype)

    row = pl.BlockSpec((tm, ATT_W), lambda i: (i, 0))
    tab = pl.BlockSpec((tm, LANE), lambda i: (i, 0))
    return pl.pallas_call(
        body, name="dil_grad_combine", grid=(SEQ // tm,),
        in_specs=[row] * 9 + [tab] * 3,
        out_specs=pl.BlockSpec((tm, 3 * ATT_W), lambda i: (i, 0)),
        out_shape=_sds((SEQ, 3 * ATT_W), BF16),
        compiler_params=_params("parallel"),
    )(*dqs, *dks, *dvs, *tabs)


def _dil_valid(n):
    qi = lax.broadcasted_iota(jnp.int32, (DIL_BLK, 2 * DIL_BLK), 0)
    ki = lax.broadcasted_iota(jnp.int32, (DIL_BLK, 2 * DIL_BLK), 1)
    dist = qi + DIL_BLK - ki
    return (dist >= 0) & (dist <= DIL_BLK) & ((n > 0) | (ki >= DIL_BLK))


def _dil_fwd(qkv, d):
    length = SEQ // d
    nb = length // DIL_BLK
    qkv_v = qkv.reshape(length, d * 3 * ATT_W)

    def body(q_ref, kp_ref, kc_ref, vp_ref, vc_ref, o_ref, lse_ref):
        n = pl.program_id(1)
        ok = _dil_valid(n)
        lane = lax.broadcasted_iota(jnp.int32, (DIL_BLK, LANE), 1)
        lse_all = jnp.zeros((DIL_BLK, LANE), F32)
        heads = range(N_HEADS)
        scores = [lax.dot_general(_head(q_ref, h), jnp.concatenate([_head(kp_ref, h), _head(kc_ref, h)], axis=0),
                                  NT_DIMS, preferred_element_type=F32) for h in heads]
        probs, inv_l = [], []
        for h in heads:
            s = jnp.where(ok, scores[h], NEG_INF)
            m = jnp.max(s, axis=-1, keepdims=True)
            p = jnp.exp(s - m)
            l = jnp.sum(p, axis=-1, keepdims=True)
            probs.append(p.astype(BF16))
            inv_l.append(1.0 / l)
            lse_all = jnp.where(lane == h, m + jnp.log(l), lse_all)
        outs = [jnp.dot(probs[h], jnp.concatenate([_head(vp_ref, h), _head(vc_ref, h)], axis=0),
                        preferred_element_type=F32) * inv_l[h] for h in heads]
        o_ref[...] = jnp.concatenate(outs, axis=1).astype(o_ref.dtype)
        lse_ref[...] = lse_all

    blk = lambda f: pl.BlockSpec((DIL_BLK, ATT_W), f)
    prev = lambda n: jnp.maximum(n - 1, 0)
    o, lse = pl.pallas_call(
        body, name=f"dil_fwd_d{d}", grid=(d, nb),
        in_specs=[blk(lambda r, n: (n, 3 * r)),
                  blk(lambda r, n: (prev(n), 3 * r + 1)), blk(lambda r, n: (n, 3 * r + 1)),
                  blk(lambda r, n: (prev(n), 3 * r + 2)), blk(lambda r, n: (n, 3 * r + 2))],
        out_specs=[blk(lambda r, n: (n, r)), pl.BlockSpec((DIL_BLK, LANE), lambda r, n: (n, r))],
        out_shape=[_sds((length, d * ATT_W), BF16), _sds((length, d * LANE), F32)],
        compiler_params=_params("parallel", "arbitrary"),
    )(qkv_v, qkv_v, qkv_v, qkv_v, qkv_v)
    return o.reshape(SEQ, ATT_W), lse.reshape(SEQ, LANE)


def _dil_merge(os_, lses, *, tm=512):
    def body(o0, o1, o2, l0, l1, l2, y_ref, lse_ref):
        ls = [l0[...], l1[...], l2[...]]
        m = jnp.maximum(jnp.maximum(ls[0], ls[1]), ls[2])
        es = [jnp.exp(l - m) for l in ls]
        tot = es[0] + es[1] + es[2]
        lse_ref[...] = m + jnp.log(tot)
        alphas = [e / tot for e in es]
        outs = []
        for h in range(N_HEADS):
            acc = None
            for g, o_ref in enumerate((o0, o1, o2)):
                term = alphas[g][:, h:h + 1] * _head(o_ref, h).astype(F32)
                acc = term if acc is None else acc + term
            outs.append(acc)
        y_ref[...] = jnp.concatenate(outs, axis=1).astype(y_ref.dtype)

    row = pl.BlockSpec((tm, ATT_W), lambda i: (i, 0))
    vec = pl.BlockSpec((tm, LANE), lambda i: (i, 0))
    return pl.pallas_call(
        body, name="dil_merge", grid=(SEQ // tm,),
        in_specs=[row] * 3 + [vec] * 3, out_specs=[row, vec],
        out_shape=[_sds((SEQ, ATT_W), BF16), _sds((SEQ, LANE), F32)],
        compiler_params=_params("parallel"),
    )(*os_, *lses)


def _dil_bwd(qkv, lse, delta, do, d):
    length = SEQ // d
    nb = length // DIL_BLK
    qkv_v = qkv.reshape(length, d * 3 * ATT_W)
    lse_v, dl_v, do_v = lse.reshape(length, d * LANE), delta.reshape(length, d * LANE), do.reshape(length, d * ATT_W)

    def body(q_ref, kp_ref, kc_ref, vp_ref, vc_ref, lse_ref, dl_ref, do_ref,
             dq_ref, dk_ref, dv_ref, ck_s, cv_s):
        n = pl.program_id(1)

        @pl.when(n == 0)
        def _():
            ck_s[...] = jnp.zeros_like(ck_s)
            cv_s[...] = jnp.zeros_like(cv_s)

        @pl.when(n < nb)
        def _():
            ok = _dil_valid(n)
            heads = range(N_HEADS)
            kks = [jnp.concatenate([_head(kp_ref, h), _head(kc_ref, h)], axis=0) for h in heads]
            scores = [lax.dot_general(_head(q_ref, h), kks[h], NT_DIMS, preferred_element_type=F32) for h in heads]
            dps = [lax.dot_general(_head(do_ref, h), jnp.concatenate([_head(vp_ref, h), _head(vc_ref, h)], axis=0),
                                   NT_DIMS, preferred_element_type=F32) for h in heads]
            ps, dss = [], []
            for h in heads:
                p = jnp.where(ok, jnp.exp(scores[h] - lse_ref[:, h:h + 1]), 0.0)
                ps.append(p.astype(BF16))
                dss.append((p * (dps[h] - dl_ref[:, h:h + 1])).astype(BF16))
            dqs = [jnp.dot(dss[h], kks[h], preferred_element_type=F32) * SCALE for h in heads]
            dkks = [lax.dot_general(dss[h], _head(q_ref, h), TN_DIMS, preferred_element_type=F32) for h in heads]
            dvvs = [lax.dot_general(ps[h], _head(do_ref, h), TN_DIMS, preferred_element_type=F32) for h in heads]
            dq_ref[...] = jnp.concatenate(dqs, axis=1).astype(dq_ref.dtype)
            dk_ref[...] = (ck_s[...] + jnp.concatenate([t[:DIL_BLK] for t in dkks], axis=1)).astype(dk_ref.dtype)
            dv_ref[...] = (cv_s[...] + jnp.concatenate([t[:DIL_BLK] for t in dvvs], axis=1)).astype(dv_ref.dtype)
            ck_s[...] = jnp.concatenate([t[DIL_BLK:] for t in dkks], axis=1)
            cv_s[...] = jnp.concatenate([t[DIL_BLK:] for t in dvvs], axis=1)

        @pl.when(n == nb)
        def _():
            dk_ref[...] = ck_s[...].astype(dk_ref.dtype)
            dv_ref[...] = cv_s[...].astype(dv_ref.dtype)

    blk = lambda f: pl.BlockSpec((DIL_BLK, ATT_W), f)
    vec = lambda f: pl.BlockSpec((DIL_BLK, LANE), f)
    cur = lambda n: jnp.minimum(n, nb - 1)
    prev = lambda n: jnp.maximum(cur(n) - 1, 0)
    back = lambda n: jnp.maximum(n - 1, 0)
    outs = pl.pallas_call(
        body, name=f"dil_bwd_d{d}", grid=(d, nb + 1),
        in_specs=[blk(lambda r, n: (cur(n), 3 * r)),
                  blk(lambda r, n: (prev(n), 3 * r + 1)), blk(lambda r, n: (cur(n), 3 * r + 1)),
                  blk(lambda r, n: (prev(n), 3 * r + 2)), blk(lambda r, n: (cur(n), 3 * r + 2)),
                  vec(lambda r, n: (cur(n), r)), vec(lambda r, n: (cur(n), r)),
                  blk(lambda r, n: (cur(n), r))],
        out_specs=[blk(lambda r, n: (cur(n), r)), blk(lambda r, n: (back(n), r)), blk(lambda r, n: (back(n), r))],
        out_shape=[_sds((length, d * ATT_W), BF16)] * 3,
        scratch_shapes=[pltpu.VMEM((DIL_BLK, ATT_W), F32), pltpu.VMEM((DIL_BLK, ATT_W), F32)],
        compiler_params=_params("arbitrary", "arbitrary"),
    )(qkv_v, qkv_v, qkv_v, qkv_v, qkv_v, lse_v, dl_v, do_v)
    return [t.reshape(SEQ, ATT_W) for t in outs]


def _sigmoid(x):
    return 1.0 / (1.0 + jnp.exp(-x))


def _mix_fwd(ya, yb, w_oa, w_ob, zm, *, tm=512):
    def body(ya_ref, yb_ref, wa_ref, wb_ref, ga_ref, gb_ref, pa_ref, pb_ref, mix_ref):
        pa = jnp.dot(ya_ref[...], wa_ref[...], preferred_element_type=F32)
        pb = jnp.dot(yb_ref[...], wb_ref[...], preferred_element_type=F32)
        pa_ref[...] = pa.astype(pa_ref.dtype)
        pb_ref[...] = pb.astype(pb_ref.dtype)
        mix_ref[...] = (_sigmoid(ga_ref[...].astype(F32)) * pa + _sigmoid(gb_ref[...].astype(F32)) * pb
                        ).astype(mix_ref.dtype)

    row = pl.BlockSpec((tm, ATT_W), lambda i: (i, 0))
    wsp = pl.BlockSpec((ATT_W, D_MODEL), lambda i: (0, 0))
    wide = pl.BlockSpec((tm, D_MODEL), lambda i: (i, 0))
    return pl.pallas_call(
        body, name="mix_fwd", grid=(SEQ // tm,),
        in_specs=[row, row, wsp, wsp, pl.BlockSpec((tm, D_MODEL), lambda i: (i, 3)),
                  pl.BlockSpec((tm, D_MODEL), lambda i: (i, 4))],
        out_specs=[wide] * 3, out_shape=[_sds((SEQ, D_MODEL), BF16)] * 3,
        compiler_params=_params("parallel"),
    )(ya, yb, w_oa, w_ob, zm, zm)


def _gate_bwd(dmix, zm, pa, pb, *, tm=512):
    def body(dm_ref, ga_ref, gb_ref, pa_ref, pb_ref, dpa_ref, dpb_ref, dg_ref):
        dm = dm_ref[...].astype(F32)
        sa, sb = _sigmoid(ga_ref[...].astype(F32)), _sigmoid(gb_ref[...].astype(F32))
        dpa_ref[...] = (dm * sa).astype(dpa_ref.dtype)
        dpb_ref[...] = (dm * sb).astype(dpb_ref.dtype)
        dg_ref[:, :D_MODEL] = (dm * pa_ref[...].astype(F32) * sa * (1.0 - sa)).astype(dg_ref.dtype)
        dg_ref[:, D_MODEL:] = (dm * pb_ref[...].astype(F32) * sb * (1.0 - sb)).astype(dg_ref.dtype)

    wide = pl.BlockSpec((tm, D_MODEL), lambda i: (i, 0))
    return pl.pallas_call(
        body, name="gate_bwd", grid=(SEQ // tm,),
        in_specs=[wide, pl.BlockSpec((tm, D_MODEL), lambda i: (i, 3)), pl.BlockSpec((tm, D_MODEL), lambda i: (i, 4)),
                  wide, wide],
        out_specs=[wide, wide, pl.BlockSpec((tm, 2 * D_MODEL), lambda i: (i, 0))],
        out_shape=[_sds((SEQ, D_MODEL), BF16), _sds((SEQ, D_MODEL), BF16), _sds((SEQ, 2 * D_MODEL), BF16)],
        compiler_params=_params("parallel"),
    )(dmix, zm, zm, pa, pb)


def _out_fwd(mixed, w_out, x, g_post, g_pre, *, tm=512):
    def body(m_ref, w_ref, x_ref, gp_ref, gn_ref, y_ref, x2_ref, h_ref):
        y = jnp.dot(m_ref[...], w_ref[...], preferred_element_type=F32)
        y_ref[...] = y
        r = lax.rsqrt(jnp.mean(y * y, axis=-1, keepdims=True) + RMS_EPS)
        x2 = x_ref[...] + y * r * gp_ref[...]
        x2_ref[...] = x2
        r2 = lax.rsqrt(jnp.mean(x2 * x2, axis=-1, keepdims=True) + RMS_EPS)
        h_ref[...] = (x2 * r2 * gn_ref[...]).astype(h_ref.dtype)

    row = pl.BlockSpec((tm, D_MODEL), lambda i: (i, 0))
    vec = pl.BlockSpec((1, D_MODEL), lambda i: (0, 0))
    return pl.pallas_call(
        body, name="out_fwd", grid=(SEQ // tm,),
        in_specs=[row, pl.BlockSpec((D_MODEL, D_MODEL), lambda i: (0, 0)), row, vec, vec],
        out_specs=[row] * 3,
        out_shape=[_sds((SEQ, D_MODEL), F32), _sds((SEQ, D_MODEL), F32), _sds((SEQ, D_MODEL), BF16)],
        compiler_params=_params("parallel"),
    )(mixed, w_out, x, g_post, g_pre)


FFN_TM = 512
FFN_HALF = 256
FFN_TN = 2 * FFN_HALF
FFN_NJ = D_FF // FFN_HALF
FFN_GROUP = 2 * SUBLANE


def _ffn_interleave(t):
    lead = t.shape[:-1]
    return jnp.swapaxes(t.reshape(*lead, 2, FFN_NJ, FFN_HALF), -3, -2).reshape(*lead, 2 * D_FF)


def _ffn_deinterleave(t):
    lead = t.shape[:-1]
    return jnp.swapaxes(t.reshape(*lead, FFN_NJ, 2, FFN_HALF), -3, -2).reshape(*lead, 2 * D_FF)


def _ffn_move_blocks(t, *, interleave, name):
    rows = t.shape[0]
    if interleave:
        src = lambda jb: (0, (jb % 2) * FFN_NJ + jb // 2)
    else:
        src = lambda jb: (0, 2 * (jb % FFN_NJ) + jb // FFN_NJ)

    def body(x_ref, o_ref):
        o_ref[...] = x_ref[...]

    return pl.pallas_call(
        body, name=name, grid=(2 * FFN_NJ,),
        in_specs=[pl.BlockSpec((rows, FFN_HALF), src)],
        out_specs=pl.BlockSpec((rows, FFN_HALF), lambda jb: (0, jb)),
        out_shape=_sds(t.shape, t.dtype),
        compiler_params=_params("parallel"),
    )(t)


def _gelu_parts(a):
    c = math.sqrt(2.0 / math.pi)
    a2 = a * a
    t = jnp.tanh((c * a) * (1.0 + 0.044715 * a2))
    half_a, one_t = 0.5 * a, 1.0 + t
    gelu = half_a * one_t
    dgelu = 0.5 * one_t + half_a * (1.0 - t * t) * (c + (3.0 * 0.044715 * c) * a2)
    return gelu, dgelu


def _shift_down(cur, above, k):
    row = lax.broadcasted_iota(jnp.int32, cur.shape, 0)
    return jnp.where(row < k, pltpu.roll(above, k, 0), pltpu.roll(cur, k, 0))


def _shift_up(cur, below, k):
    row = lax.broadcasted_iota(jnp.int32, cur.shape, 0)
    return jnp.where(row >= SUBLANE - k, pltpu.roll(below, SUBLANE - k, 0), pltpu.roll(cur, SUBLANE - k, 0))


def _conv_consts(w_ref, b_ref):
    shape = (SUBLANE, FFN_TN)
    return [jnp.broadcast_to(w_ref[k:k + 1, :], shape) for k in range(3)] + [jnp.broadcast_to(b_ref[...], shape)]


def _conv_taps(cur, above, consts):
    w0, w1, w2, bias = consts
    s1, s2 = _shift_down(cur, above, 1), _shift_down(cur, above, 2)
    return w0 * s2 + w1 * s1 + w2 * cur + bias, s1, s2


def _ffn_mid_fwd(u, conv_w, conv_b):
    per = FFN_TM // SUBLANE

    def body(u_ref, h_ref, w_ref, b_ref, m_ref, ab_ref):
        live = (pl.program_id(1) > 0).astype(F32)
        consts = _conv_consts(w_ref, b_ref)

        def group(g, above):
            rows = pl.ds(pl.multiple_of(g * FFN_GROUP, FFN_GROUP), FFN_GROUP)
            x = u_ref[rows, :].astype(F32)
            convs = []
            for c in range(2):
                cur = x[c * SUBLANE:(c + 1) * SUBLANE]
                convs.append(_conv_taps(cur, above, consts)[0])
                above = cur
            y = jnp.concatenate(convs, axis=0)
            ab_ref[rows, :] = y.astype(ab_ref.dtype)
            m_ref[rows, :] = (_gelu_parts(y[:, :FFN_HALF])[0] * y[:, FFN_HALF:]).astype(m_ref.dtype)
            return above

        lax.fori_loop(0, FFN_TM // FFN_GROUP, group, h_ref[...].astype(F32) * live)

    blk = pl.BlockSpec((FFN_TM, FFN_TN), lambda j, i: (i, j))
    return pl.pallas_call(
        body, name="ffn_mid_fwd", grid=(FFN_NJ, SEQ // FFN_TM),
        in_specs=[blk, pl.BlockSpec((SUBLANE, FFN_TN), lambda j, i: (jnp.maximum(i * per - 1, 0), j)),
                  pl.BlockSpec((3, FFN_TN), lambda j, i: (0, j)), pl.BlockSpec((1, FFN_TN), lambda j, i: (0, j))],
        out_specs=[pl.BlockSpec((FFN_TM, FFN_HALF), lambda j, i: (i, j)), blk],
        out_shape=[_sds((SEQ, D_FF), BF16), _sds((SEQ, 2 * D_FF), BF16)],
        compiler_params=_params("parallel", "arbitrary"),
    )(u, u, conv_w, conv_b)


def _ffn_mid_bwd(dm, u, ab, conv_w):
    nrow = SEQ // FFN_TM
    n_groups = FFN_TM // FFN_GROUP

    def body(dm_ref, u_ref, ab_ref, w_ref, du_ref, gw_ref, gb_ref, c_s):
        @pl.when(pl.program_id(1) == 0)
        def _():
            c_s[...] = jnp.zeros_like(c_s)
            gw_ref[...] = jnp.zeros_like(gw_ref)
            gb_ref[...] = jnp.zeros_like(gb_ref)

        taps = [jnp.broadcast_to(w_ref[k:k + 1, :], (SUBLANE, FFN_TN)) for k in range(3)]

        def group(t, carry):
            below, acc = carry
            rows = pl.ds(pl.multiple_of((n_groups - 1 - t) * FFN_GROUP, FFN_GROUP), FFN_GROUP)
            x, y, dmv = u_ref[rows, :].astype(F32), ab_ref[rows, :].astype(F32), dm_ref[rows, :].astype(F32)
            gelu, dgelu = _gelu_parts(y[:, :FFN_HALF])
            d = jnp.concatenate([dmv * y[:, FFN_HALF:] * dgelu, dmv * gelu], axis=1)
            acc, pre = list(acc), [None, None]
            for c in (1, 0):
                sl = slice(c * SUBLANE, (c + 1) * SUBLANE)
                cur, xs = d[sl], x[sl]
                up1, up2 = _shift_up(cur, below, 1), _shift_up(cur, below, 2)
                acc = [acc[0] + up2 * xs, acc[1] + up1 * xs, acc[2] + cur * xs, acc[3] + cur]
                pre[c] = taps[2] * cur + taps[1] * up1 + taps[0] * up2
                below = cur
            du_ref[rows, :] = jnp.concatenate(pre, axis=0).astype(du_ref.dtype)
            return below, tuple(acc)

        zeros = jnp.zeros((SUBLANE, FFN_TN), F32)
        below, acc = lax.fori_loop(0, n_groups, group, (c_s[...], (zeros,) * 4))
        c_s[...] = below
        for k in range(3):
            gw_ref[k:k + 1, :] += jnp.sum(acc[k], axis=0, keepdims=True)
        gb_ref[...] += jnp.sum(acc[3], axis=0, keepdims=True)

    blk = pl.BlockSpec((FFN_TM, FFN_TN), lambda j, i: (nrow - 1 - i, j))
    return pl.pallas_call(
        body, name="ffn_mid_bwd", grid=(FFN_NJ, nrow),
        in_specs=[pl.BlockSpec((FFN_TM, FFN_HALF), lambda j, i: (nrow - 1 - i, j)), blk, blk,
                  pl.BlockSpec((3, FFN_TN), lambda j, i: (0, j))],
        out_specs=[blk, pl.BlockSpec((3, FFN_TN), lambda j, i: (0, j)), pl.BlockSpec((1, FFN_TN), lambda j, i: (0, j))],
        out_shape=[_sds((SEQ, 2 * D_FF), BF16), _sds((3, 2 * D_FF), F32), _sds((1, 2 * D_FF), F32)],
        scratch_shapes=[pltpu.VMEM((SUBLANE, FFN_TN), F32)],
        compiler_params=_params("parallel", "arbitrary"),
    )(dm, u, ab, conv_w)


def _down_fwd(m, w_down, x2, g_post, target, *, tm=512):
    def body(m_ref, w_ref, x2_ref, g_ref, t_ref, dout_ref, dy_ref, gg_ref, loss_ref):
        @pl.when(pl.program_id(0) == 0)
        def _():
            gg_ref[...] = jnp.zeros_like(gg_ref)
            loss_ref[...] = jnp.zeros_like(loss_ref)

        y = jnp.dot(m_ref[...], w_ref[...], preferred_element_type=F32)
        r = lax.rsqrt(jnp.mean(y * y, axis=-1, keepdims=True) + RMS_EPS)
        yn = y * r
        diff = (x2_ref[...] + yn * g_ref[...]) - t_ref[...]
        loss_ref[...] += jnp.sum(diff * diff)
        dout = diff * (1.0 / D_MODEL)
        dout_ref[...] = dout
        gg_ref[...] += jnp.sum(dout * yn, axis=0, keepdims=True)
        dn = dout * g_ref[...]
        dy_ref[...] = (r * (dn - yn * jnp.mean(dn * yn, axis=-1, keepdims=True))).astype(dy_ref.dtype)

    row = pl.BlockSpec((tm, D_MODEL), lambda i: (i, 0))
    vec = pl.BlockSpec((1, D_MODEL), lambda i: (0, 0))
    return pl.pallas_call(
        body, name="down_fwd", grid=(SEQ // tm,),
        in_specs=[pl.BlockSpec((tm, D_FF), lambda i: (i, 0)), pl.BlockSpec((D_FF, D_MODEL), lambda i: (0, 0)),
                  row, vec, row],
        out_specs=[row, row, vec, pl.BlockSpec((1, LANE), lambda i: (0, 0))],
        out_shape=[_sds((SEQ, D_MODEL), F32), _sds((SEQ, D_MODEL), BF16), _sds((1, D_MODEL), F32),
                   _sds((1, LANE), F32)],
        compiler_params=_params("arbitrary"),
    )(m, w_down, x2, g_post, target)


def _local_step(x, target, w_main, w_f, b_forget, conv_b, g_pre_mix, g_post_mix, g_pre_ffn, g_post_ffn,
                late_weights, ffn_grads_ready, proj_grads_ready, mixer_grads_ready):
    mm = _matmul
    tabs = _rope_tables()

    h1 = _rms_fwd(x, g_pre_mix, name="rms_pre_mix")
    zm = mm(h1, w_main, out_dtype=BF16, tm=2048, tn=512, tk=1024, name="in_proj")
    zf = mm(h1, w_f, out_dtype=F32, tm=2048, tn=F_PAD, tk=1024, name="in_proj_forget")
    f_row, sg_row = _fox_prep(zf[:, :N_HEADS].T, b_forget.reshape(N_HEADS, 1))
    f_cols = jnp.pad(f_row.T, ((0, 0), (0, LANE - N_HEADS)))
    q_slots, k_slots, v_slots = _fox_pack_fwd(zm, f_cols)
    ya, lse_a = _fox_fwd(q_slots, k_slots, v_slots)
    qkv_d = _rope_fwd(zm, tabs)
    dil = [_dil_fwd(qkv_d, d) for _, d in DIL_PATTERNS]
    yb, lse_b = _dil_merge([o for o, _ in dil], [l for _, l in dil])
    w_oa, w_ob, w_out, w_up, conv_w, w_down = late_weights(yb)
    pa, pb, mixed = _mix_fwd(ya, yb, w_oa, w_ob, zm)
    y1, x2, h2 = _out_fwd(mixed, w_out, x, g_post_mix, g_pre_ffn)
    u = mm(h2, w_up, out_dtype=BF16, tm=2048, tn=512, tk=1024, name="up_proj")
    m, ab = _ffn_mid_fwd(u, conv_w, _ffn_interleave(conv_b))
    dout, dy2, gg_post_ffn, sq_err = _down_fwd(m, w_down, x2, g_post_ffn, target)

    g_w_down = mm(m, dy2, ta=True, out_dtype=BF16, tm=D_FF // 2, tn=1024, tk=2048, name="grad_w_down")
    dm = mm(dy2, w_down, tb=True, out_dtype=BF16, tm=2048, tn=D_FF // 2, tk=1024, name="d_ffn_mid")
    du, g_conv_w, g_conv_b = _ffn_mid_bwd(dm, u, ab, conv_w)
    g_w_up = mm(h2, du, ta=True, out_dtype=BF16, tm=1024, tn=D_FF // 2, tk=2048, name="grad_w_up")
    tok = ffn_grads_ready(dict(w_down=g_w_down, w_up=_ffn_move_blocks(g_w_up, interleave=False, name="grad_w_up_cols"),
                               conv_w=_ffn_deinterleave(g_conv_w)))
    dh2 = mm(du, w_up, tb=True, out_dtype=BF16, tm=512, tn=1024, tk=2 * D_FF, name="d_h2")

    dx2, dy1, gg_pre_ffn, gg_post_mix = _rms_pair_bwd([dh2], x2, g_pre_ffn, dout, y1, g_post_mix + tok)
    g_w_out = mm(mixed, dy1, ta=True, out_dtype=BF16, tm=1024, tn=1024, tk=2048, name="grad_w_out")
    dmix = mm(dy1, w_out, tb=True, out_dtype=BF16, tm=2048, tn=1024, tk=1024, name="d_mixed")
    dpa, dpb, dgates = _gate_bwd(dmix, zm, pa, pb)
    g_w_oa = mm(ya, dpa, ta=True, out_dtype=BF16, tm=512, tn=1024, tk=SEQ, name="grad_w_o_fox")
    g_w_ob = mm(yb, dpb, ta=True, out_dtype=BF16, tm=512, tn=1024, tk=SEQ, name="grad_w_o_dil")
    tok = proj_grads_ready(dict(w_o_fox=g_w_oa, w_o_dil=g_w_ob, w_out=g_w_out))
    dya = mm(dpa, w_oa, tb=True, out_dtype=BF16, tm=2048, tn=512, tk=1024, name="d_y_fox")
    dyb = mm(dpb, w_ob, tb=True, out_dtype=BF16, tm=2048, tn=512, tk=1024, name="d_y_dil")

    qb_slots, do_slots = _fox_pack_bwd(zm, f_cols + tok, lse_a, ya, dya)
    d_fox, df_cols = _fox_unpack(*_fox_bwd(qb_slots, k_slots, v_slots, do_slots))
    dfa_t, g_b_forget = _fox_post_bwd(df_cols[:, :N_HEADS].T, sg_row)

    delta_b = _attn_delta(yb, dyb, name="delta_dil")
    dil_g = [_dil_bwd(qkv_d, lse_b, delta_b, dyb, d) for _, d in DIL_PATTERNS]
    d_dil = _dil_grad_combine([g[0] for g in dil_g], [g[1] for g in dil_g], [g[2] for g in dil_g], tabs)

    dz = jnp.concatenate([d_fox, d_dil, dgates], axis=1)
    dzf = jnp.pad(dfa_t.T, ((0, 0), (0, F_PAD - N_HEADS)))
    g_w_main = mm(h1, dz, ta=True, out_dtype=BF16, tm=1024, tn=Z_MAIN // 4, tk=2048, name="grad_w_in")
    g_w_f = mm(h1, dzf, ta=True, out_dtype=BF16, tm=1024, tn=F_PAD, tk=1024, name="grad_w_in_forget")
    tok = mixer_grads_ready(dict(w_main=g_w_main, w_f=g_w_f))
    dh1 = [mm(dz, w_main, tb=True, out_dtype=BF16, tm=512, tn=1024, tk=Z_MAIN, name="d_h1"),
           mm(dzf + tok, w_f, tb=True, out_dtype=F32, tm=2048, tn=1024, tk=F_PAD, name="d_h1_forget")]
    grad_x, gg_pre_mix = _rms_bwd(dh1, x, g_pre_mix, dx2, out_dtype=F32, name="rms_pre_mix_bwd")

    grads = dict(
        b_forget=g_b_forget.reshape(1, N_HEADS), conv_b=_ffn_deinterleave(g_conv_b),
        g_pre_mix=gg_pre_mix, g_post_mix=gg_post_mix, g_pre_ffn=gg_pre_ffn, g_post_ffn=gg_post_ffn)
    return sq_err[0, 0], grad_x, grads


def _exchange(arrays, scatter, *, name):
    n = len(arrays)

    def body(*refs):
        ins, outs = refs[:n], refs[n:2 * n]
        send_sems, recv_sems, local_sems = refs[2 * n:]
        x, y, c = lax.axis_index("x"), lax.axis_index("y"), lax.axis_index("c")
        me = 4 * x + 2 * y + c
        peers = []
        for k in range(1, N_DEV):
            px = 1 - x if k & 4 else x
            py = 1 - y if k & 2 else y
            pc = 1 - c if k & 1 else c
            peers.append(((px, py, pc), 4 * px + 2 * py + pc))

        def remote(a, k):
            dev, slot = peers[k]
            return pltpu.make_async_remote_copy(
                src_ref=ins[a].at[slot] if scatter else ins[a], dst_ref=outs[a].at[me],
                send_sem=send_sems.at[a, k], recv_sem=recv_sems.at[a, k],
                device_id=dev, device_id_type=MESH_ID)

        def landed(a, k):
            dev, slot = peers[k]
            return pltpu.make_async_remote_copy(
                src_ref=outs[a].at[slot], dst_ref=outs[a].at[slot],
                send_sem=send_sems.at[a, k], recv_sem=recv_sems.at[a, k],
                device_id=dev, device_id_type=MESH_ID)

        own = [pltpu.make_async_copy(ins[a].at[me] if scatter else ins[a], outs[a].at[me], local_sems.at[a])
               for a in range(n)]
        copies = [remote(a, k) for k in range(N_DEV - 1) for a in range(n)]
        for cp in own + copies:
            cp.start()
        for k in range(N_DEV - 1):
            for a in range(n):
                landed(a, k).wait_recv()
        for cp in copies:
            cp.wait_send()
        for cp in own:
            cp.wait()

    out_shape = [_sds(((N_DEV,) + a.shape[-2:]), a.dtype) for a in arrays]
    return pl.pallas_call(
        body, name=name, in_specs=[ANY] * n, out_specs=[ANY] * n, out_shape=out_shape,
        scratch_shapes=[pltpu.SemaphoreType.DMA((n, N_DEV - 1)), pltpu.SemaphoreType.DMA((n, N_DEV - 1)),
                        pltpu.SemaphoreType.DMA((n,))],
    )(*arrays)


def _gather_two_level(shard, *, name):
    def body(x_ref, out_ref, send_sems, recv_sems, local_sem):
        x, y, c = lax.axis_index("x"), lax.axis_index("y"), lax.axis_index("c")
        me, sibling = (x, y, c), (x, y, 1 - c)
        chips = [(1 - x, y), (x, 1 - y), (1 - x, 1 - y)]

        def slot(px, py, pc):
            return out_ref.at[4 * px + 2 * py + pc]

        def copy(k, block, to, src=None):
            return pltpu.make_async_remote_copy(
                src_ref=slot(*block) if src is None else src, dst_ref=slot(*block),
                send_sem=send_sems.at[k], recv_sem=recv_sems.at[k], device_id=to, device_id_type=MESH_ID)

        mine = pltpu.make_async_copy(x_ref, slot(*me), local_sem)
        mine.start()
        first = [copy(0, me, sibling, src=x_ref)]
        first += [copy(1 + j, me, (*chip, c), src=x_ref) for j, chip in enumerate(chips)]
        for cp in first:
            cp.start()
        passed = [copy(4 + j, (*chip, c), sibling) for j, chip in enumerate(chips)]
        for j, chip in enumerate(chips):
            copy(1 + j, (*chip, c), me).wait_recv()
            passed[j].start()
        copy(0, sibling, me).wait_recv()
        for j, chip in enumerate(chips):
            copy(4 + j, (*chip, 1 - c), me).wait_recv()
        for cp in first + passed:
            cp.wait_send()
        mine.wait()

    return pl.pallas_call(
        body, name=name, in_specs=[ANY], out_specs=ANY, out_shape=_sds((N_DEV,) + shard.shape, shard.dtype),
        scratch_shapes=[pltpu.SemaphoreType.DMA((N_DEV - 1,)), pltpu.SemaphoreType.DMA((N_DEV - 1,)),
                        pltpu.SemaphoreType.DMA],
    )(shard)


N_CHIPS = N_DEV // 2


def _peers(chips_only=False):
    x, y, c = lax.axis_index("x"), lax.axis_index("y"), lax.axis_index("c")
    out = []
    if chips_only:
        for k in range(1, N_CHIPS):
            px = 1 - x if k & 2 else x
            py = 1 - y if k & 1 else y
            out.append(((px, py, c), 2 * px + py))
        return 2 * x + y, out
    for k in range(1, N_DEV):
        px = 1 - x if k & 4 else x
        py = 1 - y if k & 2 else y
        pc = 1 - c if k & 1 else c
        out.append(((px, py, pc), 4 * px + 2 * py + pc))
    return 4 * x + 2 * y + c, out


def _sibling_swap(slots, *, name):
    def body(in_ref, out_ref, send_sems, recv_sems):
        x, y, c = lax.axis_index("x"), lax.axis_index("y"), lax.axis_index("c")
        copies = [pltpu.make_async_remote_copy(
            src_ref=in_ref.at[2 * q + (1 - c)], dst_ref=out_ref.at[q], send_sem=send_sems.at[q],
            recv_sem=recv_sems.at[q], device_id=(x, y, 1 - c), device_id_type=MESH_ID) for q in range(N_CHIPS)]
        for cp in copies:
            cp.start()
        for cp in copies:
            cp.wait_recv()
        for cp in copies:
            cp.wait_send()

    return pl.pallas_call(
        body, name=name, in_specs=[ANY], out_specs=ANY,
        out_shape=_sds((N_CHIPS,) + slots.shape[1:], slots.dtype),
        scratch_shapes=[pltpu.SemaphoreType.DMA((N_CHIPS,)), pltpu.SemaphoreType.DMA((N_CHIPS,))],
    )(slots)


def _pair_sum(slots, from_sibling, *, name, tm=256):
    _, r, c = slots.shape
    core = lax.axis_index("c").astype(jnp.int32).reshape(1)

    def body(core_ref, a_ref, b_ref, o_ref):
        o_ref[...] = (a_ref[...].astype(F32) + b_ref[...].astype(F32)).astype(o_ref.dtype)

    blk = lambda f: pl.BlockSpec((1, tm, c), f)
    return pl.pallas_call(
        body, name=name,
        grid_spec=pltpu.PrefetchScalarGridSpec(
            num_scalar_prefetch=1, grid=(N_CHIPS, r // tm),
            in_specs=[blk(lambda q, i, core: (2 * q + core[0], i, 0)), blk(lambda q, i, core: (q, i, 0))],
            out_specs=blk(lambda q, i, core: (q, i, 0))),
        out_shape=_sds((N_CHIPS, r, c), slots.dtype),
        compiler_params=_params("parallel", "parallel"),
    )(core, slots, from_sibling)


HBM = pl.BlockSpec(memory_space=pltpu.HBM)
SEM = pl.BlockSpec(memory_space=pltpu.SEMAPHORE)
DATAFLOW = pltpu.SideEffectType.DATAFLOW_SIDE_EFFECTING


def _split_copy(srcs, lands, send_sems, recv_sems, scatter, a, k, me, peers, incoming=False):
    dev, slot = peers[k]
    if incoming:
        src = dst = lands[a].at[slot]
    else:
        src, dst = (srcs[a].at[slot] if scatter else srcs[a]), lands[a].at[me]
    sem = a * len(peers) + k
    return pltpu.make_async_remote_copy(
        src_ref=src, dst_ref=dst, send_sem=send_sems.at[sem], recv_sem=recv_sems.at[sem],
        device_id=dev, device_id_type=MESH_ID)


def _exchange_start(arrays, scatter, *, name, chips_only=False):
    n = len(arrays)
    n_slots = N_CHIPS if chips_only else N_DEV

    def body(*refs):
        srcs, lands = refs[:n], refs[n:2 * n]
        send_sems, recv_sems = refs[2 * n], refs[2 * n + 1]
        token = refs[-1]
        me, peers = _peers(chips_only)
        for k in range(len(peers)):
            for a in range(n):
                _split_copy(srcs, lands, send_sems, recv_sems, scatter, a, k, me, peers).start()
        token[...] = jnp.zeros_like(token)

    land_shapes = [((n_slots,) + a.shape[-2:], a.dtype) for a in arrays]
    sems = pltpu.SemaphoreType.DMA((n * (n_slots - 1),))
    outs = pl.pallas_call(
        body, name=name,
        out_shape=(sems, sems, *[pltpu.HBM(a.shape, a.dtype) for a in arrays],
                   *[pltpu.HBM(s, d) for s, d in land_shapes], _sds((SUBLANE, LANE), F32)),
        in_specs=[HBM] * (2 * n),
        out_specs=(SEM, SEM, *[HBM] * (2 * n), pl.BlockSpec(memory_space=pltpu.VMEM)),
        input_output_aliases={i: 2 + i for i in range(2 * n)},
        compiler_params=pltpu.CompilerParams(has_side_effects=DATAFLOW),
    )(*[pltpu.with_memory_space_constraint(a, pltpu.HBM) for a in arrays],
      *[pltpu.with_memory_space_constraint(lax.empty(s, d), pltpu.HBM) for s, d in land_shapes])
    return (outs[0], outs[1], outs[2:2 + n], outs[2 + n:2 + 2 * n], scatter, chips_only), outs[-1]


def _exchange_wait(handles, after, *, name):
    send_sems, recv_sems, srcs, lands, scatter, chips_only = handles
    n = len(srcs)

    def body(*refs):
        src_refs, land_refs = refs[:n], refs[n:2 * n]
        send_ref, recv_ref = refs[2 * n], refs[2 * n + 1]
        me, peers = _peers(chips_only)
        for k in range(len(peers)):
            for a in range(n):
                _split_copy(src_refs, land_refs, send_ref, recv_ref, scatter, a, k, me, peers).wait_send()
                _split_copy(src_refs, land_refs, send_ref, recv_ref, scatter, a, k, me, peers, True).wait_recv()

    outs = pl.pallas_call(
        body, name=name,
        out_shape=tuple(pltpu.HBM(t.shape, t.dtype) for t in (*srcs, *lands)),
        in_specs=[HBM] * (2 * n) + [SEM, SEM, pl.BlockSpec(memory_space=pl.ANY)],
        out_specs=tuple([HBM] * (2 * n)),
        input_output_aliases={i: i for i in range(2 * n)},
        compiler_params=pltpu.CompilerParams(has_side_effects=DATAFLOW),
    )(*srcs, *lands, send_sems, recv_sems, after)
    return _with_own_slot(outs[n:], outs[:n], scatter, chips_only)


def _with_own_slot(landed, own, scatter, chips_only):
    me = 2 * lax.axis_index("x") + lax.axis_index("y")
    if not chips_only:
        me = 2 * me + lax.axis_index("c")
    out = []
    for buf, src in zip(landed, own):
        mine = lax.dynamic_index_in_dim(src, me, 0, keepdims=False) if scatter else src
        out.append(lax.dynamic_update_index_in_dim(buf, mine, me, 0))
    return out


def _adamw(parts, w, m, v, *, name, tm):
    r, c = w.shape
    assert r % tm == 0

    def body(p_ref, w_ref, m_ref, v_ref, g_ref, d_ref, nm_ref, nv_ref):
        _adamw_update(p_ref, w_ref, m_ref, v_ref, g_ref, d_ref, nm_ref, nv_ref)

    blk = pl.BlockSpec((tm, c), lambda i: (i, 0))
    return pl.pallas_call(
        body, name=name, grid=(r // tm,),
        in_specs=[pl.BlockSpec((parts.shape[0], tm, c), lambda i: (0, i, 0)), blk, blk, blk],
        out_specs=[blk] * 4, out_shape=[_sds((r, c), F32)] * 4,
        compiler_params=_params("parallel"),
    )(parts, w, m, v)


def _adamw_update(p_ref, w_ref, m_ref, v_ref, g_ref, d_ref, nm_ref, nv_ref):
    g = p_ref[0].astype(F32)
    for s in range(1, p_ref.shape[0]):
        g = g + p_ref[s].astype(F32)
    g_ref[...] = g
    m_new = ADAM_B1 * m_ref[...] + (1.0 - ADAM_B1) * g
    v_new = ADAM_B2 * v_ref[...] + (1.0 - ADAM_B2) * (g * g)
    nm_ref[...] = m_new
    nv_ref[...] = v_new
    m_hat = m_new / (1.0 - ADAM_B1 ** ADAM_STEP)
    v_hat = v_new / (1.0 - ADAM_B2 ** ADAM_STEP)
    d_ref[...] = -ADAM_LR * (m_hat / (jnp.sqrt(v_hat) + ADAM_EPS) + ADAM_WD * w_ref[...])


SMALL = ("g_pre_mix", "b_forget", "g_post_mix", "g_pre_ffn", "conv_b", "g_post_ffn")


def _adamw_small(parts, ws, ms, vs):
    n = len(ws)

    def body(*refs):
        ins, outs = refs[:4 * n], refs[4 * n:]
        for i in range(n):
            _adamw_update(ins[i], ins[n + i], ins[2 * n + i], ins[3 * n + i], *outs[4 * i:4 * i + 4])

    res = pl.pallas_call(
        body, name="adamw_small", out_shape=[_sds(w.shape, F32) for w in ws for _ in range(4)],
        compiler_params=pltpu.CompilerParams(vmem_limit_bytes=VMEM_LIMIT),
    )(*parts, *ws, *ms, *vs)
    return [res[4 * i:4 * i + 4] for i in range(n)]


def kernel(x, g_pre_mix, w_in, b_forget, w_o_fox, w_o_dil, w_out, g_post_mix, g_pre_ffn, w_up, conv_w, conv_b, w_down, g_post_ffn, loss_target, m_g_pre_mix, m_w_in, m_b_forget, m_w_o_fox, m_w_o_dil, m_w_out, m_g_post_mix, m_g_pre_ffn, m_w_up, m_conv_w, m_conv_b, m_w_down, m_g_post_ffn, v_g_pre_mix, v_w_in, v_b_forget, v_w_o_fox, v_w_o_dil, v_w_out, v_g_post_mix, v_g_pre_ffn, v_w_up, v_conv_w, v_conv_b, v_w_down, v_g_post_ffn):
    names = ("g_pre_mix", "w_in", "b_forget", "w_o_fox", "w_o_dil", "w_out", "g_post_mix", "g_pre_ffn",
             "w_up", "conv_w", "conv_b", "w_down", "g_post_ffn")
    w = dict(g_pre_mix=g_pre_mix, w_in=w_in, b_forget=b_forget, w_o_fox=w_o_fox, w_o_dil=w_o_dil, w_out=w_out,
             g_post_mix=g_post_mix, g_pre_ffn=g_pre_ffn, w_up=w_up, conv_w=conv_w, conv_b=conv_b, w_down=w_down,
             g_post_ffn=g_post_ffn)
    m = dict(g_pre_mix=m_g_pre_mix, w_in=m_w_in, b_forget=m_b_forget, w_o_fox=m_w_o_fox, w_o_dil=m_w_o_dil,
             w_out=m_w_out, g_post_mix=m_g_post_mix, g_pre_ffn=m_g_pre_ffn, w_up=m_w_up, conv_w=m_conv_w,
             conv_b=m_conv_b, w_down=m_w_down, g_post_ffn=m_g_post_ffn)
    v = dict(g_pre_mix=v_g_pre_mix, w_in=v_w_in, b_forget=v_b_forget, w_o_fox=v_w_o_fox, w_o_dil=v_w_o_dil,
             w_out=v_w_out, g_post_mix=v_g_post_mix, g_pre_ffn=v_g_pre_ffn, w_up=v_w_up, conv_w=v_conv_w,
             conv_b=v_conv_b, w_down=v_w_down, g_post_ffn=v_g_post_ffn)
    sharded = ("w_in", "w_o_fox", "w_o_dil", "w_out", "w_up", "w_down", "conv_w")
    wire = lambda n: F32 if n == "conv_w" else BF16

    by_cols = lambda t: jnp.transpose(t, (1, 0, 2)).reshape(t.shape[1], N_DEV * t.shape[2])
    by_rows = lambda t: t.reshape(N_DEV * t.shape[1], t.shape[2])
    col_slots = lambda t: jnp.transpose(t.reshape(t.shape[0], N_DEV, t.shape[1] // N_DEV), (1, 0, 2))
    row_slots = lambda t: t.reshape(N_DEV, t.shape[0] // N_DEV, t.shape[1])
    to_slots = lambda n, t: (row_slots if n in ("w_out", "w_down") else col_slots)(t).astype(wire(n))
    shard = lambda n: w[n][0].astype(wire(n))
    f_lo, f_hi = 3 * ATT_W, 3 * ATT_W + N_HEADS

    w_in_full = by_cols(_gather_two_level(shard("w_in"), name="gather_w_in"))
    w_main = jnp.concatenate([w_in_full[:, :f_lo], w_in_full[:, f_hi:]], axis=1)
    w_f = jnp.pad(w_in_full[:, f_lo:f_hi], ((0, 0), (0, F_PAD - N_HEADS)))
    late = ("w_o_fox", "w_o_dil", "w_out", "w_up", "conv_w", "w_down")
    order = jnp.minimum(jnp.abs(w_in_full[0, 0].astype(F32)), 0.0)
    late_handles, late_tok = _exchange_start(
        [shard(n) + order.astype(wire(n)) if n == "conv_w" else shard(n) for n in late], False,
        name="gather_late_start")

    def late_weights(after):
        got = dict(zip(late, _exchange_wait(late_handles, after, name="gather_late_wait")))
        return (by_cols(got["w_o_fox"]), by_cols(got["w_o_dil"]), by_rows(got["w_out"]),
                _ffn_move_blocks(by_cols(got["w_up"]), interleave=True, name="w_up_cols"),
                _ffn_interleave(by_cols(got["conv_w"])),
                by_rows(got["w_down"]))

    pending = {}

    def ffn_grads_ready(g):
        pending["ffn"] = _exchange_start([to_slots(n, g[n]) for n in ("w_down", "w_up", "conv_w")], True,
                                         name="scatter_ffn_start")
        return pending["ffn"][1][0, 0]

    def proj_grads_ready(g):
        pending["proj"] = _exchange_start([to_slots(n, g[n]) for n in ("w_o_fox", "w_o_dil", "w_out")], True,
                                          name="scatter_proj_start")
        return pending["proj"][1][0, 0]

    def mixer_grads_ready(g):
        g_w_in = jnp.concatenate([g["w_main"][:, :f_lo], g["w_f"][:, :N_HEADS], g["w_main"][:, f_lo:]], axis=1)
        slots = to_slots("w_in", g_w_in)
        chip_sums = _pair_sum(slots, _sibling_swap(slots, name="scatter_w_in_swap"), name="scatter_w_in_pair_sum")
        pending["w_in"] = _exchange_start([chip_sums], True, name="scatter_w_in_start", chips_only=True)
        return pending["w_in"][1][0, 0]

    sq_err, grad_x, g = _local_step(
        x[0], loss_target[0], w_main, w_f, b_forget, conv_b, g_pre_mix + late_tok[0, 0], g_post_mix, g_pre_ffn,
        g_post_ffn, late_weights, ffn_grads_ready, proj_grads_ready, mixer_grads_ready)
    loss = lax.psum(0.5 * sq_err / D_MODEL, ("x", "y", "c"))

    tiles = dict(w_in=256, w_o_fox=512, w_o_dil=512, w_out=128, w_up=256, w_down=176, conv_w=3)
    adam = lambda n, p: _adamw(p, w[n][0], m[n][0], v[n][0], name=f"adamw_{n}", tm=tiles[n])
    res = {}
    for key, group in (("ffn", ("w_down", "w_up", "conv_w")), ("proj", ("w_o_fox", "w_o_dil", "w_out"))):
        landed = _exchange_wait(pending[key][0], grad_x, name=f"scatter_{key}_wait")
        res.update({n: adam(n, p) for n, p in zip(group, landed)})
    small_parts = _exchange([g[n] for n in SMALL], False, name="gather_small_grads")
    done = res["w_up"][3]
    res["w_in"] = adam("w_in", _exchange_wait(pending["w_in"][0], done, name="scatter_w_in_wait")[0])
    small = dict(zip(SMALL, _adamw_small(small_parts, *[[t[n] for n in SMALL] for t in (w, m, v)])))
    out = [[(res[n][k][None] if n in sharded else small[n][k]) for n in names] for k in range(4)]
    return (loss, grad_x[None], *out[0], *out[1], *out[2], *out[3])
```

```python
import functools
import math

import jax
import jax.numpy as jnp
import numpy as np
from jax import lax
from jax.experimental import pallas as pl
from jax.experimental.pallas import tpu as pltpu

F32 = jnp.float32
BF16 = jnp.bfloat16

SEQ = 4096
D_MODEL = 1024
N_HEADS = 8
HEAD_DIM = 64
ATT_W = N_HEADS * HEAD_DIM
D_FF = 2816
Z_MAIN = 5120
F_PAD = 128
ROPE_DIM = 16
ROPE_THETA = 500000.0
RMS_EPS = 1e-6
NEG_INF = -1e30
SCALE = 1.0 / math.sqrt(HEAD_DIM)
DIL_PATTERNS = ((128, 1), (512, 4), (2048, 16))
DIL_BLK = 128
N_DEV = 8

ADAM_LR = 0.001
ADAM_B1 = 0.9
ADAM_B2 = 0.999
ADAM_EPS = 1e-08
ADAM_WD = 0.01
ADAM_STEP = 10

LANE = 128
SUBLANE = 8
VMEM_LIMIT = 56 * 1024 * 1024
MESH_ID = pl.DeviceIdType.MESH
ANY = pl.BlockSpec(memory_space=pl.ANY)


def _params(*sem):
    return pltpu.CompilerParams(dimension_semantics=sem, vmem_limit_bytes=VMEM_LIMIT)


def _sds(shape, dtype):
    return jax.ShapeDtypeStruct(shape, dtype)


def _matmul(a, b, *, ta=False, tb=False, out_dtype, tm, tn, tk, name, b_k_off=0):
    if ta:
        kk, m = a.shape
    else:
        m, kk = a.shape
    n = b.shape[0] if tb else b.shape[1]
    tm, tn, tk = min(tm, m), min(tn, n), min(tk, kk)
    assert (b.shape[1] if tb else b.shape[0]) >= b_k_off * tk + kk
    assert m % tm == 0 and n % tn == 0 and kk % tk == 0, (name, m, n, kk, tm, tn, tk)
    nk = kk // tk
    dims = (((0 if ta else 1,), (1 if tb else 0,)), ((), ()))

    def body(a_ref, b_ref, o_ref, *scratch):
        p = lax.dot_general(a_ref[...].astype(BF16), b_ref[...].astype(BF16), dims,
                            preferred_element_type=F32)
        if nk == 1:
            o_ref[...] = p.astype(o_ref.dtype)
        else:
            acc = scratch[0]
            k = pl.program_id(2)

            @pl.when(k == 0)
            def _():
                acc[...] = p

            @pl.when(k > 0)
            def _():
                acc[...] += p

            @pl.when(k == nk - 1)
            def _():
                o_ref[...] = acc[...].astype(o_ref.dtype)

    a_spec = (pl.BlockSpec((tk, tm), lambda i, j, k: (k, i)) if ta
              else pl.BlockSpec((tm, tk), lambda i, j, k: (i, k)))
    b_spec = (pl.BlockSpec((tn, tk), lambda i, j, k: (j, k + b_k_off)) if tb
              else pl.BlockSpec((tk, tn), lambda i, j, k: (k + b_k_off, j)))
    return pl.pallas_call(
        body, name=name, grid=(m // tm, n // tn, nk),
        in_specs=[a_spec, b_spec],
        out_specs=pl.BlockSpec((tm, tn), lambda i, j, k: (i, j)),
        out_shape=_sds((m, n), out_dtype),
        scratch_shapes=[pltpu.VMEM((tm, tn), F32)] if nk > 1 else [],
        compiler_params=_params("parallel", "parallel", "arbitrary"),
    )(a, b)


def _rms_fwd(x, g, *, name, tm=512):
    def body(x_ref, g_ref, h_ref):
        xv = x_ref[...]
        r = lax.rsqrt(jnp.mean(xv * xv, axis=-1, keepdims=True) + RMS_EPS)
        h_ref[...] = (xv * r * g_ref[...]).astype(h_ref.dtype)

    return pl.pallas_call(
        body, name=name, grid=(SEQ // tm,),
        in_specs=[pl.BlockSpec((tm, D_MODEL), lambda i: (i, 0)), pl.BlockSpec((1, D_MODEL), lambda i: (0, 0))],
        out_specs=pl.BlockSpec((tm, D_MODEL), lambda i: (i, 0)),
        out_shape=_sds((SEQ, D_MODEL), BF16),
        compiler_params=_params("parallel"),
    )(x, g)


def _rms_bwd(dh_parts, xin, g, dres, *, out_dtype, name, tm=512):
    n_parts = len(dh_parts)
    has_res = dres is not None

    def body(*refs):
        parts = refs[:n_parts]
        x_ref, g_ref = refs[n_parts], refs[n_parts + 1]
        res_ref = refs[n_parts + 2] if has_res else None
        o_ref, gg_ref = refs[-2], refs[-1]
        dh = parts[0][...].astype(F32)
        for p in parts[1:]:
            dh = dh + p[...].astype(F32)
        xv = x_ref[...]
        r = lax.rsqrt(jnp.mean(xv * xv, axis=-1, keepdims=True) + RMS_EPS)
        xn = xv * r

        @pl.when(pl.program_id(0) == 0)
        def _():
            gg_ref[...] = jnp.zeros_like(gg_ref)

        gg_ref[...] += jnp.sum(dh * xn, axis=0, keepdims=True)
        dxn = dh * g_ref[...]
        dx = r * (dxn - xn * jnp.mean(dxn * xn, axis=-1, keepdims=True))
        if has_res:
            dx = dx + res_ref[...]
        o_ref[...] = dx.astype(o_ref.dtype)

    row = pl.BlockSpec((tm, D_MODEL), lambda i: (i, 0))
    vec = pl.BlockSpec((1, D_MODEL), lambda i: (0, 0))
    args = list(dh_parts) + [xin, g] + ([dres] if has_res else [])
    return pl.pallas_call(
        body, name=name, grid=(SEQ // tm,),
        in_specs=[row] * n_parts + [row, vec] + ([row] if has_res else []),
        out_specs=[row, vec],
        out_shape=[_sds((SEQ, D_MODEL), out_dtype), _sds((1, D_MODEL), F32)],
        compiler_params=_params("arbitrary"),
    )(*args)


def _rms_pair_bwd(dh_parts, x2, g_pre, dres, y1, g_post, *, tm=512):
    n_parts = len(dh_parts)

    def norm_bwd(dh, xin, g_ref, gg_ref):
        r = lax.rsqrt(jnp.mean(xin * xin, axis=-1, keepdims=True) + RMS_EPS)
        xn = xin * r
        gg_ref[...] += jnp.sum(dh * xn, axis=0, keepdims=True)
        dxn = dh * g_ref[...]
        return r * (dxn - xn * jnp.mean(dxn * xn, axis=-1, keepdims=True))

    def body(*refs):
        parts = refs[:n_parts]
        x2_ref, gpre_ref, res_ref, y1_ref, gpost_ref, dx2_ref, dy1_ref, ggpre_ref, ggpost_ref = refs[n_parts:]

        @pl.when(pl.program_id(0) == 0)
        def _():
            ggpre_ref[...] = jnp.zeros_like(ggpre_ref)
            ggpost_ref[...] = jnp.zeros_like(ggpost_ref)

        dh = parts[0][...].astype(F32)
        for p in parts[1:]:
            dh = dh + p[...].astype(F32)
        dx2 = res_ref[...] + norm_bwd(dh, x2_ref[...], gpre_ref, ggpre_ref)
        dx2_ref[...] = dx2
        dy1_ref[...] = norm_bwd(dx2, y1_ref[...], gpost_ref, ggpost_ref).astype(dy1_ref.dtype)

    row = pl.BlockSpec((tm, D_MODEL), lambda i: (i, 0))
    vec = pl.BlockSpec((1, D_MODEL), lambda i: (0, 0))
    return pl.pallas_call(
        body, name="rms_pair_bwd", grid=(SEQ // tm,),
        in_specs=[row] * n_parts + [row, vec, row, row, vec],
        out_specs=[row, row, vec, vec],
        out_shape=[_sds((SEQ, D_MODEL), F32), _sds((SEQ, D_MODEL), BF16), _sds((1, D_MODEL), F32),
                   _sds((1, D_MODEL), F32)],
        compiler_params=_params("arbitrary"),
    )(*dh_parts, x2, g_pre, dres, y1, g_post)


SCAN_BLK = 512


def _split_dot(v, tri):
    hi = v.astype(BF16)
    r1 = v - hi.astype(F32)
    mid = r1.astype(BF16)
    lo = (r1 - mid.astype(F32)).astype(BF16)
    dot = functools.partial(jnp.dot, preferred_element_type=F32)
    return dot(hi, tri) + dot(mid, tri) + dot(lo, tri)


def _fox_prep(fa_t, b_col):
    nblk = SEQ // SCAN_BLK

    def body(fa_ref, b_ref, f_ref, sg_ref):
        row = lax.broadcasted_iota(jnp.int32, (SCAN_BLK, SCAN_BLK), 0)
        col = lax.broadcasted_iota(jnp.int32, (SCAN_BLK, SCAN_BLK), 1)
        upper = (row <= col).astype(BF16)
        carry = jnp.zeros((N_HEADS, 1), F32)
        for blk in range(nblk):
            sl = pl.ds(blk * SCAN_BLK, SCAN_BLK)
            xx = fa_ref[:, sl] + b_ref[...]
            e = jnp.exp(-jnp.abs(xx))
            logf = jnp.minimum(xx, 0.0) - jnp.log(1.0 + e)
            sg_ref[:, sl] = jnp.where(xx >= 0.0, e, 1.0) / (1.0 + e)
            c = _split_dot(logf, upper) + carry
            f_ref[:, sl] = c
            carry = c[:, SCAN_BLK - 1:SCAN_BLK]

    return pl.pallas_call(
        body, name="fox_prep",
        out_shape=[_sds((N_HEADS, SEQ), F32), _sds((N_HEADS, SEQ), F32)],
        compiler_params=pltpu.CompilerParams(vmem_limit_bytes=VMEM_LIMIT),
    )(fa_t, b_col)


def _fox_post_bwd(df_t, sg_t):
    nblk = SEQ // SCAN_BLK

    def body(df_ref, sg_ref, dfa_ref, gb_ref):
        row = lax.broadcasted_iota(jnp.int32, (SCAN_BLK, SCAN_BLK), 0)
        col = lax.broadcasted_iota(jnp.int32, (SCAN_BLK, SCAN_BLK), 1)
        lower = (row >= col).astype(BF16)
        carry = jnp.zeros((N_HEADS, 1), F32)
        gb = jnp.zeros((N_HEADS, 1), F32)
        for blk in reversed(range(nblk)):
            sl = pl.ds(blk * SCAN_BLK, SCAN_BLK)
            c = _split_dot(df_ref[:, sl], lower) + carry
            carry = c[:, 0:1]
            dfa = c * sg_ref[:, sl]
            dfa_ref[:, sl] = dfa
            gb = gb + jnp.sum(dfa, axis=1, keepdims=True)
        gb_ref[...] = gb

    return pl.pallas_call(
        body, name="fox_post_bwd",
        out_shape=[_sds((N_HEADS, SEQ), F32), _sds((N_HEADS, 1), F32)],
        compiler_params=pltpu.CompilerParams(vmem_limit_bytes=VMEM_LIMIT),
    )(df_t, sg_t)


FOX_T = 512
NT_DIMS = (((1,), (1,)), ((), ()))
TN_DIMS = (((0,), (0,)), ((), ()))


def _head(ref_or_val, h):
    return ref_or_val[:, h * HEAD_DIM:(h + 1) * HEAD_DIM]


def _split3(v):
    hi = v.astype(BF16).astype(F32)
    r1 = v - hi
    mid = r1.astype(BF16).astype(F32)
    return hi, mid, (r1 - mid).astype(BF16).astype(F32)


ONE_LANE = 3 * N_HEADS


def _pack_terms(v, with_one):
    hi, mid, lo = _split3(v)
    t = hi + pltpu.roll(mid, N_HEADS, 1) + pltpu.roll(lo, 2 * N_HEADS, 1)
    if with_one:
        t = t + (lax.broadcasted_iota(jnp.int32, v.shape, 1) == ONE_LANE).astype(F32)
    return t.astype(BF16)


def _aux_matrices():
    to_q = np.zeros((LANE, N_HEADS * 2 * HEAD_DIM), np.float32)
    to_k = np.zeros_like(to_q)
    for h in range(N_HEADS):
        base = h * 2 * HEAD_DIM + HEAD_DIM
        for s in range(3):
            to_q[s * N_HEADS + h, base + s] = 1.0
            to_q[ONE_LANE, base + 3 + s] = 1.0
            to_k[ONE_LANE, base + s] = 1.0
            to_k[s * N_HEADS + h, base + 3 + s] = -1.0
    return jnp.asarray(to_q, BF16), jnp.asarray(to_k, BF16)


def _head_sums():
    total = np.zeros((N_HEADS * HEAD_DIM, LANE), np.float32)
    first = np.zeros_like(total)
    for h in range(N_HEADS):
        total[h * HEAD_DIM:(h + 1) * HEAD_DIM, h] = 1.0
        first[h * HEAD_DIM, h] = 1.0
    return jnp.asarray(total, BF16), jnp.asarray(first, BF16)


SLOT = 2 * HEAD_DIM
N_SPLIT = 3
FOX_FWD_HEADS = 4
FOX_BWD_HEADS = 4


def _slot(ref, h):
    return ref[:, h * SLOT:(h + 1) * SLOT]


def _fox_pack_fwd(zm, f_cols, *, tm=512):
    def body(q_ref, k_ref, v_ref, f_ref, tq_ref, tk_ref, qs_ref, ks_ref, vs_ref):
        ones = jnp.ones((tm, HEAD_DIM), BF16)
        terms = _pack_terms(f_ref[...], True)
        q_aux = jnp.dot(terms, tq_ref[...], preferred_element_type=F32).astype(BF16)
        k_aux = jnp.dot(terms, tk_ref[...], preferred_element_type=F32).astype(BF16)
        for h in range(N_HEADS):
            aux = slice(h * SLOT + HEAD_DIM, (h + 1) * SLOT)
            qs_ref[:, h * SLOT:(h + 1) * SLOT] = jnp.concatenate(
                [(_head(q_ref, h).astype(F32) * SCALE).astype(BF16), q_aux[:, aux]], axis=1)
            ks_ref[:, h * SLOT:(h + 1) * SLOT] = jnp.concatenate([_head(k_ref, h), k_aux[:, aux]], axis=1)
            vs_ref[:, h * SLOT:(h + 1) * SLOT] = jnp.concatenate([_head(v_ref, h), ones], axis=1)

    col = lambda b: pl.BlockSpec((tm, ATT_W), lambda i: (i, b))
    wide = pl.BlockSpec((tm, N_HEADS * SLOT), lambda i: (i, 0))
    const = pl.BlockSpec((LANE, N_HEADS * SLOT), lambda i: (0, 0))
    return pl.pallas_call(
        body, name="fox_pack_fwd", grid=(SEQ // tm,),
        in_specs=[col(0), col(1), col(2), pl.BlockSpec((tm, LANE), lambda i: (i, 0)), const, const],
        out_specs=[wide] * 3, out_shape=[_sds((SEQ, N_HEADS * SLOT), BF16)] * 3,
        compiler_params=_params("parallel"),
    )(zm, zm, zm, f_cols, *_aux_matrices())


def _fox_pack_bwd(zm, f_cols, lse, o, do, *, tm=512):
    def body(q_ref, f_ref, lse_ref, o_ref, do_ref, tq_ref, total_ref, first_ref, qs_ref, ds_ref):
        delta = _split_dot(o_ref[...].astype(F32) * do_ref[...].astype(F32), total_ref[...])
        lse_h = _split_dot(lse_ref[...], first_ref[...])
        q_aux = jnp.dot(_pack_terms(f_ref[...] - lse_h, True), tq_ref[...], preferred_element_type=F32).astype(BF16)
        d_aux = jnp.dot(_pack_terms(-delta, False), tq_ref[...], preferred_element_type=F32).astype(BF16)
        for h in range(N_HEADS):
            aux = slice(h * SLOT + HEAD_DIM, (h + 1) * SLOT)
            qs_ref[:, h * SLOT:(h + 1) * SLOT] = jnp.concatenate(
                [(_head(q_ref, h).astype(F32) * SCALE).astype(BF16), q_aux[:, aux]], axis=1)
            ds_ref[:, h * SLOT:(h + 1) * SLOT] = jnp.concatenate([_head(do_ref, h), d_aux[:, aux]], axis=1)

    row = pl.BlockSpec((tm, ATT_W), lambda i: (i, 0))
    wide = pl.BlockSpec((tm, N_HEADS * SLOT), lambda i: (i, 0))
    const = lambda r, c: pl.BlockSpec((r, c), lambda i: (0, 0))
    return pl.pallas_call(
        body, name="fox_pack_bwd", grid=(SEQ // tm,),
        in_specs=[row, pl.BlockSpec((tm, LANE), lambda i: (i, 0)), row, row, row,
                  const(LANE, N_HEADS * SLOT), const(ATT_W, LANE), const(ATT_W, LANE)],
        out_specs=[wide] * 2, out_shape=[_sds((SEQ, N_HEADS * SLOT), BF16)] * 2,
        compiler_params=_params("parallel"),
    )(zm, f_cols, lse, o, do, _aux_matrices()[0], *_head_sums())


def _causal_pairs(key_major):
    nb = SEQ // FOX_T
    if key_major:
        pairs = [(i, j) for j in range(nb) for i in range(j, nb)]
    else:
        pairs = [(i, j) for i in range(nb) for j in range(i + 1)]
    return (jnp.array([p[0] for p in pairs], jnp.int32), jnp.array([p[1] for p in pairs], jnp.int32), len(pairs))


def _diag_mask():
    row = lax.broadcasted_iota(jnp.int32, (FOX_T, FOX_T), 0)
    col = lax.broadcasted_iota(jnp.int32, (FOX_T, FOX_T), 1)
    return col <= row


def _fox_fwd(q_slots, k_slots, v_slots):
    i_tab, j_tab, n_pairs = _causal_pairs(False)

    def body(i_tab, j_tab, q_ref, k_ref, v_ref, o_ref, lse_ref, m_s, acc_s):
        t = pl.program_id(1)
        i, j = i_tab[t], j_tab[t]

        @pl.when(j == 0)
        def _():
            m_s[...] = jnp.full_like(m_s, NEG_INF)
            acc_s[...] = jnp.zeros_like(acc_s)

        def step(masked):
            scores = [lax.dot_general(_slot(q_ref, h), _slot(k_ref, h), NT_DIMS, preferred_element_type=F32)
                      for h in range(FOX_FWD_HEADS)]
            probs, alphas = [], []
            for h in range(FOX_FWD_HEADS):
                s = jnp.where(_diag_mask(), scores[h], NEG_INF) if masked else scores[h]
                m_prev = m_s[h]
                m_new = jnp.maximum(m_prev, jnp.max(s, axis=-1, keepdims=True))
                probs.append(jnp.exp(s - jnp.tile(m_new, (1, FOX_T // LANE))).astype(BF16))
                alphas.append(jnp.exp(m_prev - m_new))
                m_s[h] = m_new
            for h in range(FOX_FWD_HEADS):
                acc_s[h] = alphas[h] * acc_s[h] + jnp.dot(probs[h], _slot(v_ref, h), preferred_element_type=F32)

        @pl.when(j < i)
        def _():
            step(False)

        @pl.when(j == i)
        def _():
            step(True)
            outs, lses = [], []
            for h in range(FOX_FWD_HEADS):
                acc = acc_s[h]
                l = acc[:, HEAD_DIM:]
                outs.append(acc[:, :HEAD_DIM] / l)
                lses.append(m_s[h][:, :HEAD_DIM] + jnp.log(l))
            o_ref[...] = jnp.concatenate(outs, axis=1).astype(o_ref.dtype)
            lse_ref[...] = jnp.concatenate(lses, axis=1)

    qspec = pl.BlockSpec((FOX_T, FOX_FWD_HEADS * SLOT), lambda p, t, it, jt: (it[t], p))
    kspec = pl.BlockSpec((FOX_T, FOX_FWD_HEADS * SLOT), lambda p, t, it, jt: (jt[t], p))
    ospec = pl.BlockSpec((FOX_T, FOX_FWD_HEADS * HEAD_DIM), lambda p, t, it, jt: (it[t], p))
    return pl.pallas_call(
        body, name="fox_fwd",
        grid_spec=pltpu.PrefetchScalarGridSpec(
            num_scalar_prefetch=2, grid=(N_HEADS // FOX_FWD_HEADS, n_pairs),
            in_specs=[qspec, kspec, kspec], out_specs=[ospec, ospec],
            scratch_shapes=[pltpu.VMEM((FOX_FWD_HEADS, FOX_T, LANE), F32),
                            pltpu.VMEM((FOX_FWD_HEADS, FOX_T, SLOT), F32)]),
        out_shape=[_sds((SEQ, ATT_W), BF16), _sds((SEQ, ATT_W), F32)],
        compiler_params=_params("parallel", "arbitrary"),
    )(i_tab, j_tab, q_slots, k_slots, v_slots)


def _fox_bwd(q_slots, k_slots, v_slots, do_slots):
    i_tab, j_tab, n_pairs = _causal_pairs(True)

    def body(i_tab, j_tab, q_ref, k_ref, v_ref, do_ref, dq_ref, dk_ref, dv_ref):
        t = pl.program_id(1)
        i, j = i_tab[t], j_tab[t]

        @pl.when(t == 0)
        def _():
            dq_ref[...] = jnp.zeros_like(dq_ref)

        @pl.when(i == j)
        def _():
            dk_ref[...] = jnp.zeros_like(dk_ref)
            dv_ref[...] = jnp.zeros_like(dv_ref)

        def step(masked):
            rows = pl.ds(pl.multiple_of(i * FOX_T, FOX_T), FOX_T)
            heads = range(FOX_BWD_HEADS)
            scores = [lax.dot_general(_slot(q_ref, h), _slot(k_ref, h), NT_DIMS, preferred_element_type=F32)
                      for h in heads]
            dps = [lax.dot_general(_slot(do_ref, h), _slot(v_ref, h), NT_DIMS, preferred_element_type=F32)
                   for h in heads]
            ps, dss = [], []
            for h in heads:
                p = jnp.exp(scores[h])
                if masked:
                    p = jnp.where(_diag_mask(), p, 0.0)
                ps.append(p.astype(BF16))
                dss.append((p * dps[h]).astype(BF16))
            for h in heads:
                cols = slice(h * SLOT, (h + 1) * SLOT)
                dv_ref[:, cols] += lax.dot_general(ps[h], _slot(do_ref, h), TN_DIMS, preferred_element_type=F32)
                dk_ref[:, cols] += lax.dot_general(dss[h], _slot(q_ref, h), TN_DIMS, preferred_element_type=F32)
                dq_ref[rows, cols] += jnp.dot(dss[h], _slot(k_ref, h), preferred_element_type=F32)

        @pl.when(i > j)
        def _():
            step(False)

        @pl.when(i == j)
        def _():
            step(True)

    qspec = pl.BlockSpec((FOX_T, FOX_BWD_HEADS * SLOT), lambda p, t, it, jt: (it[t], p))
    kspec = pl.BlockSpec((FOX_T, FOX_BWD_HEADS * SLOT), lambda p, t, it, jt: (jt[t], p))
    return pl.pallas_call(
        body, name="fox_bwd",
        grid_spec=pltpu.PrefetchScalarGridSpec(
            num_scalar_prefetch=2, grid=(N_HEADS // FOX_BWD_HEADS, n_pairs),
            in_specs=[qspec, kspec, kspec, qspec],
            out_specs=[pl.BlockSpec((SEQ, FOX_BWD_HEADS * SLOT), lambda p, t, it, jt: (0, p)), kspec, kspec]),
        out_shape=[_sds((SEQ, N_HEADS * SLOT), F32)] * 3,
        compiler_params=_params("arbitrary", "arbitrary"),
    )(i_tab, j_tab, q_slots, k_slots, v_slots, do_slots)


def _fox_unpack(dq_slots, dk_slots, dv_slots, *, tm=512):
    def body(dq_ref, dk_ref, dv_ref, o_ref, df_ref):
        lane = lax.broadcasted_iota(jnp.int32, (tm, LANE), 1)
        df = jnp.zeros((tm, LANE), F32)
        for h in range(N_HEADS):
            lo = h * SLOT
            for part, (ref, mult) in enumerate(((dq_ref, SCALE), (dk_ref, 1.0), (dv_ref, 1.0))):
                o_ref[:, part * ATT_W + h * HEAD_DIM:part * ATT_W + (h + 1) * HEAD_DIM] = (
                    ref[:, lo:lo + HEAD_DIM] * mult).astype(o_ref.dtype)
            rows = dq_ref[:, lo + HEAD_DIM:lo + HEAD_DIM + 1]
            cols = dk_ref[:, lo + HEAD_DIM + N_SPLIT:lo + HEAD_DIM + N_SPLIT + 1]
            df = jnp.where(lane == h, rows - cols, df)
        df_ref[...] = df

    wide = pl.BlockSpec((tm, N_HEADS * SLOT), lambda i: (i, 0))
    return pl.pallas_call(
        body, name="fox_unpack", grid=(SEQ // tm,), in_specs=[wide] * 3,
        out_specs=[pl.BlockSpec((tm, 3 * ATT_W), lambda i: (i, 0)), pl.BlockSpec((tm, LANE), lambda i: (i, 0))],
        out_shape=[_sds((SEQ, 3 * ATT_W), BF16), _sds((SEQ, LANE), F32)],
        compiler_params=_params("parallel"),
    )(dq_slots, dk_slots, dv_slots)


def _attn_delta(o, do, *, name, tm=512):
    def body(o_ref, do_ref, d_ref):
        prod = o_ref[...].astype(F32) * do_ref[...].astype(F32)
        lane = lax.broadcasted_iota(jnp.int32, (tm, LANE), 1)
        out = jnp.zeros((tm, LANE), F32)
        for h in range(N_HEADS):
            out = jnp.where(lane == h, jnp.sum(_head(prod, h), axis=1, keepdims=True), out)
        d_ref[...] = out

    row = pl.BlockSpec((tm, ATT_W), lambda i: (i, 0))
    return pl.pallas_call(
        body, name=name, grid=(SEQ // tm,), in_specs=[row, row],
        out_specs=pl.BlockSpec((tm, LANE), lambda i: (i, 0)), out_shape=_sds((SEQ, LANE), F32),
        compiler_params=_params("parallel"),
    )(o, do)


def _rope_tables():
    half = ROPE_DIM // 2
    inv_freq = np.float32(ROPE_THETA) ** (-np.arange(half, dtype=np.float32) * np.float32(2.0) / np.float32(ROPE_DIM))
    ang = np.arange(SEQ, dtype=np.float32)[:, None] * inv_freq.astype(np.float32)[None, :]
    cos, sin = jnp.asarray(np.cos(ang).astype(np.float32)), jnp.asarray(np.sin(ang).astype(np.float32))
    ones = jnp.ones((SEQ, HEAD_DIM - ROPE_DIM), F32)
    zeros = jnp.zeros((SEQ, HEAD_DIM - ROPE_DIM), F32)
    zh = jnp.zeros((SEQ, half), F32)
    c_tab = jnp.concatenate([cos, cos, ones], axis=1)
    a_tab = jnp.concatenate([-sin, zh, zeros], axis=1)
    b_tab = jnp.concatenate([zh, sin, zeros], axis=1)
    two = lambda t: jnp.concatenate([t, t], axis=1)
    return two(c_tab), two(a_tab), two(b_tab)


def _rotate(x, c_tab, a_tab, b_tab):
    return x * c_tab + pltpu.roll(x, LANE - ROPE_DIM // 2, 1) * a_tab + pltpu.roll(x, ROPE_DIM // 2, 1) * b_tab


def _rope_fwd(zm, tabs, *, tm=512):
    def body(q_ref, k_ref, v_ref, c_ref, a_ref, b_ref, o_ref):
        for part, (x_ref, mult) in enumerate(((q_ref, SCALE), (k_ref, 1.0))):
            for cc in range(ATT_W // LANE):
                sl = slice(cc * LANE, (cc + 1) * LANE)
                rot = _rotate(x_ref[:, sl].astype(F32), c_ref[...], a_ref[...], b_ref[...])
                o_ref[:, part * ATT_W + cc * LANE:part * ATT_W + (cc + 1) * LANE] = (rot * mult).astype(o_ref.dtype)
        o_ref[:, 2 * ATT_W:] = v_ref[...]

    tab = pl.BlockSpec((tm, LANE), lambda i: (i, 0))
    col = lambda b: pl.BlockSpec((tm, ATT_W), lambda i: (i, b))
    return pl.pallas_call(
        body, name="rope_fwd", grid=(SEQ // tm,),
        in_specs=[col(3), col(4), col(5), tab, tab, tab],
        out_specs=pl.BlockSpec((tm, 3 * ATT_W), lambda i: (i, 0)),
        out_shape=_sds((SEQ, 3 * ATT_W), BF16),
        compiler_params=_params("parallel"),
    )(zm, zm, zm, *tabs)


def _dil_grad_combine(dqs, dks, dvs, tabs, *, tm=256):
    def body(*refs):
        q_refs, k_refs, v_refs = refs[0:3], refs[3:6], refs[6:9]
        c_ref, a_ref, b_ref, o_ref = refs[9:]
        total = lambda rs, sl: rs[0][:, sl].astype(F32) + rs[1][:, sl].astype(F32) + rs[2][:, sl].astype(F32)
        for cc in range(ATT_W // LANE):
            sl = slice(cc * LANE, (cc + 1) * LANE)
            for part, rs in enumerate((q_refs, k_refs)):
                o_ref[:, part * ATT_W + cc * LANE:part * ATT_W + (cc + 1) * LANE] = _rotate(
                    total(rs, sl), c_ref[...], -a_ref[...], -b_ref[...]).astype(o_ref.dtype)
            o_ref[:, 2 * ATT_W + cc * LANE:2 * ATT_W + (cc + 1) * LANE] = total(v_refs, sl).astype(o_ref.dtype)

    row = pl.BlockSpec((tm, ATT_W), lambda i: (i, 0))
    tab = pl.BlockSpec((tm, LANE), lambda i: (i, 0))
    return pl.pallas_call(
        body, name="dil_grad_combine", grid=(SEQ // tm,),
        in_specs=[row] * 9 + [tab] * 3,
        out_specs=pl.BlockSpec((tm, 3 * ATT_W), lambda i: (i, 0)),
        out_shape=_sds((SEQ, 3 * ATT_W), BF16),
        compiler_params=_params("parallel"),
    )(*dqs, *dks, *dvs, *tabs)


def _dil_valid(n):
    qi = lax.broadcasted_iota(jnp.int32, (DIL_BLK, 2 * DIL_BLK), 0)
    ki = lax.broadcasted_iota(jnp.int32, (DIL_BLK, 2 * DIL_BLK), 1)
    dist = qi + DIL_BLK - ki
    return (dist >= 0) & (dist <= DIL_BLK) & ((n > 0) | (ki >= DIL_BLK))


def _dil_fwd(qkv, d):
    length = SEQ // d
    nb = length // DIL_BLK
    qkv_v = qkv.reshape(length, d * 3 * ATT_W)

    def body(q_ref, kp_ref, kc_ref, vp_ref, vc_ref, o_ref, lse_ref):
        n = pl.program_id(1)
        ok = _dil_valid(n)
        lane = lax.broadcasted_iota(jnp.int32, (DIL_BLK, LANE), 1)
        lse_all = jnp.zeros((DIL_BLK, LANE), F32)
        heads = range(N_HEADS)
        scores = [lax.dot_general(_head(q_ref, h), jnp.concatenate([_head(kp_ref, h), _head(kc_ref, h)], axis=0),
                                  NT_DIMS, preferred_element_type=F32) for h in heads]
        probs, inv_l = [], []
        for h in heads:
            s = jnp.where(ok, scores[h], NEG_INF)
            m = jnp.max(s, axis=-1, keepdims=True)
            p = jnp.exp(s - m)
            l = jnp.sum(p, axis=-1, keepdims=True)
            probs.append(p.astype(BF16))
            inv_l.append(1.0 / l)
            lse_all = jnp.where(lane == h, m + jnp.log(l), lse_all)
        outs = [jnp.dot(probs[h], jnp.concatenate([_head(vp_ref, h), _head(vc_ref, h)], axis=0),
                        preferred_element_type=F32) * inv_l[h] for h in heads]
        o_ref[...] = jnp.concatenate(outs, axis=1).astype(o_ref.dtype)
        lse_ref[...] = lse_all

    blk = lambda f: pl.BlockSpec((DIL_BLK, ATT_W), f)
    prev = lambda n: jnp.maximum(n - 1, 0)
    o, lse = pl.pallas_call(
        body, name=f"dil_fwd_d{d}", grid=(d, nb),
        in_specs=[blk(lambda r, n: (n, 3 * r)),
                  blk(lambda r, n: (prev(n), 3 * r + 1)), blk(lambda r, n: (n, 3 * r + 1)),
                  blk(lambda r, n: (prev(n), 3 * r + 2)), blk(lambda r, n: (n, 3 * r + 2))],
        out_specs=[blk(lambda r, n: (n, r)), pl.BlockSpec((DIL_BLK, LANE), lambda r, n: (n, r))],
        out_shape=[_sds((length, d * ATT_W), BF16), _sds((length, d * LANE), F32)],
        compiler_params=_params("parallel", "arbitrary"),
    )(qkv_v, qkv_v, qkv_v, qkv_v, qkv_v)
    return o.reshape(SEQ, ATT_W), lse.reshape(SEQ, LANE)


def _dil_merge(os_, lses, *, tm=512):
    def body(o0, o1, o2, l0, l1, l2, y_ref, lse_ref):
        ls = [l0[...], l1[...], l2[...]]
        m = jnp.maximum(jnp.maximum(ls[0], ls[1]), ls[2])
        es = [jnp.exp(l - m) for l in ls]
        tot = es[0] + es[1] + es[2]
        lse_ref[...] = m + jnp.log(tot)
        alphas = [e / tot for e in es]
        outs = []
        for h in range(N_HEADS):
            acc = None
            for g, o_ref in enumerate((o0, o1, o2)):
                term = alphas[g][:, h:h + 1] * _head(o_ref, h).astype(F32)
                acc = term if acc is None else acc + term
            outs.append(acc)
        y_ref[...] = jnp.concatenate(outs, axis=1).astype(y_ref.dtype)

    row = pl.BlockSpec((tm, ATT_W), lambda i: (i, 0))
    vec = pl.BlockSpec((tm, LANE), lambda i: (i, 0))
    return pl.pallas_call(
        body, name="dil_merge", grid=(SEQ // tm,),
        in_specs=[row] * 3 + [vec] * 3, out_specs=[row, vec],
        out_shape=[_sds((SEQ, ATT_W), BF16), _sds((SEQ, LANE), F32)],
        compiler_params=_params("parallel"),
    )(*os_, *lses)


def _dil_bwd(qkv, lse, delta, do, d):
    length = SEQ // d
    nb = length // DIL_BLK
    qkv_v = qkv.reshape(length, d * 3 * ATT_W)
    lse_v, dl_v, do_v = lse.reshape(length, d * LANE), delta.reshape(length, d * LANE), do.reshape(length, d * ATT_W)

    def body(q_ref, kp_ref, kc_ref, vp_ref, vc_ref, lse_ref, dl_ref, do_ref,
             dq_ref, dk_ref, dv_ref, ck_s, cv_s):
        n = pl.program_id(1)

        @pl.when(n == 0)
        def _():
            ck_s[...] = jnp.zeros_like(ck_s)
            cv_s[...] = jnp.zeros_like(cv_s)

        @pl.when(n < nb)
        def _():
            ok = _dil_valid(n)
            heads = range(N_HEADS)
            kks = [jnp.concatenate([_head(kp_ref, h), _head(kc_ref, h)], axis=0) for h in heads]
            scores = [lax.dot_general(_head(q_ref, h), kks[h], NT_DIMS, preferred_element_type=F32) for h in heads]
            dps = [lax.dot_general(_head(do_ref, h), jnp.concatenate([_head(vp_ref, h), _head(vc_ref, h)], axis=0),
                                   NT_DIMS, preferred_element_type=F32) for h in heads]
            ps, dss = [], []
            for h in heads:
                p = jnp.where(ok, jnp.exp(scores[h] - lse_ref[:, h:h + 1]), 0.0)
                ps.append(p.astype(BF16))
                dss.append((p * (dps[h] - dl_ref[:, h:h + 1])).astype(BF16))
            dqs = [jnp.dot(dss[h], kks[h], preferred_element_type=F32) * SCALE for h in heads]
            dkks = [lax.dot_general(dss[h], _head(q_ref, h), TN_DIMS, preferred_element_type=F32) for h in heads]
            dvvs = [lax.dot_general(ps[h], _head(do_ref, h), TN_DIMS, preferred_element_type=F32) for h in heads]
            dq_ref[...] = jnp.concatenate(dqs, axis=1).astype(dq_ref.dtype)
            dk_ref[...] = (ck_s[...] + jnp.concatenate([t[:DIL_BLK] for t in dkks], axis=1)).astype(dk_ref.dtype)
            dv_ref[...] = (cv_s[...] + jnp.concatenate([t[:DIL_BLK] for t in dvvs], axis=1)).astype(dv_ref.dtype)
            ck_s[...] = jnp.concatenate([t[DIL_BLK:] for t in dkks], axis=1)
            cv_s[...] = jnp.concatenate([t[DIL_BLK:] for t in dvvs], axis=1)

        @pl.when(n == nb)
        def _():
            dk_ref[...] = ck_s[...].astype(dk_ref.dtype)
            dv_ref[...] = cv_s[...].astype(dv_ref.dtype)

    blk = lambda f: pl.BlockSpec((DIL_BLK, ATT_W), f)
    vec = lambda f: pl.BlockSpec((DIL_BLK, LANE), f)
    cur = lambda n: jnp.minimum(n, nb - 1)
    prev = lambda n: jnp.maximum(cur(n) - 1, 0)
    back = lambda n: jnp.maximum(n - 1, 0)
    outs = pl.pallas_call(
        body, name=f"dil_bwd_d{d}", grid=(d, nb + 1),
        in_specs=[blk(lambda r, n: (cur(n), 3 * r)),
                  blk(lambda r, n: (prev(n), 3 * r + 1)), blk(lambda r, n: (cur(n), 3 * r + 1)),
                  blk(lambda r, n: (prev(n), 3 * r + 2)), blk(lambda r, n: (cur(n), 3 * r + 2)),
                  vec(lambda r, n: (cur(n), r)), vec(lambda r, n: (cur(n), r)),
                  blk(lambda r, n: (cur(n), r))],
        out_specs=[blk(lambda r, n: (cur(n), r)), blk(lambda r, n: (back(n), r)), blk(lambda r, n: (back(n), r))],
        out_shape=[_sds((length, d * ATT_W), BF16)] * 3,
        scratch_shapes=[pltpu.VMEM((DIL_BLK, ATT_W), F32), pltpu.VMEM((DIL_BLK, ATT_W), F32)],
        compiler_params=_params("arbitrary", "arbitrary"),
    )(qkv_v, qkv_v, qkv_v, qkv_v, qkv_v, lse_v, dl_v, do_v)
    return [t.reshape(SEQ, ATT_W) for t in outs]


def _sigmoid(x):
    return 1.0 / (1.0 + jnp.exp(-x))


def _mix_fwd(ya, yb, w_oa, w_ob, zm, *, tm=512):
    def body(ya_ref, yb_ref, wa_ref, wb_ref, ga_ref, gb_ref, pa_ref, pb_ref, mix_ref):
        pa = jnp.dot(ya_ref[...], wa_ref[...], preferred_element_type=F32)
        pb = jnp.dot(yb_ref[...], wb_ref[...], preferred_element_type=F32)
        pa_ref[...] = pa.astype(pa_ref.dtype)
        pb_ref[...] = pb.astype(pb_ref.dtype)
        mix_ref[...] = (_sigmoid(ga_ref[...].astype(F32)) * pa + _sigmoid(gb_ref[...].astype(F32)) * pb
                        ).astype(mix_ref.dtype)

    row = pl.BlockSpec((tm, ATT_W), lambda i: (i, 0))
    wsp = pl.BlockSpec((ATT_W, D_MODEL), lambda i: (0, 0))
    wide = pl.BlockSpec((tm, D_MODEL), lambda i: (i, 0))
    return pl.pallas_call(
        body, name="mix_fwd", grid=(SEQ // tm,),
        in_specs=[row, row, wsp, wsp, pl.BlockSpec((tm, D_MODEL), lambda i: (i, 3)),
                  pl.BlockSpec((tm, D_MODEL), lambda i: (i, 4))],
        out_specs=[wide] * 3, out_shape=[_sds((SEQ, D_MODEL), BF16)] * 3,
        compiler_params=_params("parallel"),
    )(ya, yb, w_oa, w_ob, zm, zm)


def _gate_bwd(dmix, zm, pa, pb, *, tm=512):
    def body(dm_ref, ga_ref, gb_ref, pa_ref, pb_ref, dpa_ref, dpb_ref, dg_ref):
        dm = dm_ref[...].astype(F32)
        sa, sb = _sigmoid(ga_ref[...].astype(F32)), _sigmoid(gb_ref[...].astype(F32))
        dpa_ref[...] = (dm * sa).astype(dpa_ref.dtype)
        dpb_ref[...] = (dm * sb).astype(dpb_ref.dtype)
        dg_ref[:, :D_MODEL] = (dm * pa_ref[...].astype(F32) * sa * (1.0 - sa)).astype(dg_ref.dtype)
        dg_ref[:, D_MODEL:] = (dm * pb_ref[...].astype(F32) * sb * (1.0 - sb)).astype(dg_ref.dtype)

    wide = pl.BlockSpec((tm, D_MODEL), lambda i: (i, 0))
    return pl.pallas_call(
        body, name="gate_bwd", grid=(SEQ // tm,),
        in_specs=[wide, pl.BlockSpec((tm, D_MODEL), lambda i: (i, 3)), pl.BlockSpec((tm, D_MODEL), lambda i: (i, 4)),
                  wide, wide],
        out_specs=[wide, wide, pl.BlockSpec((tm, 2 * D_MODEL), lambda i: (i, 0))],
        out_shape=[_sds((SEQ, D_MODEL), BF16), _sds((SEQ, D_MODEL), BF16), _sds((SEQ, 2 * D_MODEL), BF16)],
        compiler_params=_params("parallel"),
    )(dmix, zm, zm, pa, pb)


def _out_fwd(mixed, w_out, x, g_post, g_pre, *, tm=512):
    def body(m_ref, w_ref, x_ref, gp_ref, gn_ref, y_ref, x2_ref, h_ref):
        y = jnp.dot(m_ref[...], w_ref[...], preferred_element_type=F32)
        y_ref[...] = y
        r = lax.rsqrt(jnp.mean(y * y, axis=-1, keepdims=True) + RMS_EPS)
        x2 = x_ref[...] + y * r * gp_ref[...]
        x2_ref[...] = x2
        r2 = lax.rsqrt(jnp.mean(x2 * x2, axis=-1, keepdims=True) + RMS_EPS)
        h_ref[...] = (x2 * r2 * gn_ref[...]).astype(h_ref.dtype)

    row = pl.BlockSpec((tm, D_MODEL), lambda i: (i, 0))
    vec = pl.BlockSpec((1, D_MODEL), lambda i: (0, 0))
    return pl.pallas_call(
        body, name="out_fwd", grid=(SEQ // tm,),
        in_specs=[row, pl.BlockSpec((D_MODEL, D_MODEL), lambda i: (0, 0)), row, vec, vec],
        out_specs=[row] * 3,
        out_shape=[_sds((SEQ, D_MODEL), F32), _sds((SEQ, D_MODEL), F32), _sds((SEQ, D_MODEL), BF16)],
        compiler_params=_params("parallel"),
    )(mixed, w_out, x, g_post, g_pre)


FFN_TM = 512
FFN_HALF = 256
FFN_TN = 2 * FFN_HALF
FFN_NJ = D_FF // FFN_HALF
FFN_GROUP = 2 * SUBLANE


def _ffn_interleave(t):
    lead = t.shape[:-1]
    return jnp.swapaxes(t.reshape(*lead, 2, FFN_NJ, FFN_HALF), -3, -2).reshape(*lead, 2 * D_FF)


def _ffn_deinterleave(t):
    lead = t.shape[:-1]
    return jnp.swapaxes(t.reshape(*lead, FFN_NJ, 2, FFN_HALF), -3, -2).reshape(*lead, 2 * D_FF)


def _ffn_move_blocks(t, *, interleave, name):
    rows = t.shape[0]
    if interleave:
        src = lambda jb: (0, (jb % 2) * FFN_NJ + jb // 2)
    else:
        src = lambda jb: (0, 2 * (jb % FFN_NJ) + jb // FFN_NJ)

    def body(x_ref, o_ref):
        o_ref[...] = x_ref[...]

    return pl.pallas_call(
        body, name=name, grid=(2 * FFN_NJ,),
        in_specs=[pl.BlockSpec((rows, FFN_HALF), src)],
        out_specs=pl.BlockSpec((rows, FFN_HALF), lambda jb: (0, jb)),
        out_shape=_sds(t.shape, t.dtype),
        compiler_params=_params("parallel"),
    )(t)


def _gelu_parts(a):
    c = math.sqrt(2.0 / math.pi)
    a2 = a * a
    t = jnp.tanh((c * a) * (1.0 + 0.044715 * a2))
    half_a, one_t = 0.5 * a, 1.0 + t
    gelu = half_a * one_t
    dgelu = 0.5 * one_t + half_a * (1.0 - t * t) * (c + (3.0 * 0.044715 * c) * a2)
    return gelu, dgelu


def _shift_down(cur, above, k):
    row = lax.broadcasted_iota(jnp.int32, cur.shape, 0)
    return jnp.where(row < k, pltpu.roll(above, k, 0), pltpu.roll(cur, k, 0))


def _shift_up(cur, below, k):
    row = lax.broadcasted_iota(jnp.int32, cur.shape, 0)
    return jnp.where(row >= SUBLANE - k, pltpu.roll(below, SUBLANE - k, 0), pltpu.roll(cur, SUBLANE - k, 0))


def _conv_consts(w_ref, b_ref):
    shape = (SUBLANE, FFN_TN)
    return [jnp.broadcast_to(w_ref[k:k + 1, :], shape) for k in range(3)] + [jnp.broadcast_to(b_ref[...], shape)]


def _conv_taps(cur, above, consts):
    w0, w1, w2, bias = consts
    s1, s2 = _shift_down(cur, above, 1), _shift_down(cur, above, 2)
    return w0 * s2 + w1 * s1 + w2 * cur + bias, s1, s2


def _ffn_mid_fwd(u, conv_w, conv_b):
    per = FFN_TM // SUBLANE

    def body(u_ref, h_ref, w_ref, b_ref, m_ref, ab_ref):
        live = (pl.program_id(1) > 0).astype(F32)
        consts = _conv_consts(w_ref, b_ref)

        def group(g, above):
            rows = pl.ds(pl.multiple_of(g * FFN_GROUP, FFN_GROUP), FFN_GROUP)
            x = u_ref[rows, :].astype(F32)
            convs = []
            for c in range(2):
                cur = x[c * SUBLANE:(c + 1) * SUBLANE]
                convs.append(_conv_taps(cur, above, consts)[0])
                above = cur
            y = jnp.concatenate(convs, axis=0)
            ab_ref[rows, :] = y.astype(ab_ref.dtype)
            m_ref[rows, :] = (_gelu_parts(y[:, :FFN_HALF])[0] * y[:, FFN_HALF:]).astype(m_ref.dtype)
            return above

        lax.fori_loop(0, FFN_TM // (2 * FFN_GROUP), lambda g2, carry: group(2 * g2 + 1, group(2 * g2, carry)),
                      h_ref[...].astype(F32) * live)

    blk = pl.BlockSpec((FFN_TM, FFN_TN), lambda j, i: (i, j))
    return pl.pallas_call(
        body, name="ffn_mid_fwd", grid=(FFN_NJ, SEQ // FFN_TM),
        in_specs=[blk, pl.BlockSpec((SUBLANE, FFN_TN), lambda j, i: (jnp.maximum(i * per - 1, 0), j)),
                  pl.BlockSpec((3, FFN_TN), lambda j, i: (0, j)), pl.BlockSpec((1, FFN_TN), lambda j, i: (0, j))],
        out_specs=[pl.BlockSpec((FFN_TM, FFN_HALF), lambda j, i: (i, j)), blk],
        out_shape=[_sds((SEQ, D_FF), BF16), _sds((SEQ, 2 * D_FF), BF16)],
        compiler_params=_params("parallel", "arbitrary"),
    )(u, u, conv_w, conv_b)


def _ffn_mid_bwd(dm, u, ab, conv_w):
    nrow = SEQ // FFN_TM
    n_groups = FFN_TM // FFN_GROUP

    def body(dm_ref, u_ref, ab_ref, w_ref, du_ref, gw_ref, gb_ref, c_s):
        @pl.when(pl.program_id(1) == 0)
        def _():
            c_s[...] = jnp.zeros_like(c_s)
            gw_ref[...] = jnp.zeros_like(gw_ref)
            gb_ref[...] = jnp.zeros_like(gb_ref)

        taps = [jnp.broadcast_to(w_ref[k:k + 1, :], (SUBLANE, FFN_TN)) for k in range(3)]

        def group(t, carry):
            below, acc = carry
            rows = pl.ds(pl.multiple_of((n_groups - 1 - t) * FFN_GROUP, FFN_GROUP), FFN_GROUP)
            x, y, dmv = u_ref[rows, :].astype(F32), ab_ref[rows, :].astype(F32), dm_ref[rows, :].astype(F32)
            gelu, dgelu = _gelu_parts(y[:, :FFN_HALF])
            d = jnp.concatenate([dmv * y[:, FFN_HALF:] * dgelu, dmv * gelu], axis=1)
            acc, pre = list(acc), [None, None]
            for c in (1, 0):
                sl = slice(c * SUBLANE, (c + 1) * SUBLANE)
                cur, xs = d[sl], x[sl]
                up1, up2 = _shift_up(cur, below, 1), _shift_up(cur, below, 2)
                acc = [acc[0] + up2 * xs, acc[1] + up1 * xs, acc[2] + cur * xs, acc[3] + cur]
                pre[c] = taps[2] * cur + taps[1] * up1 + taps[0] * up2
                below = cur
            du_ref[rows, :] = jnp.concatenate(pre, axis=0).astype(du_ref.dtype)
            return below, tuple(acc)

        zeros = jnp.zeros((SUBLANE, FFN_TN), F32)
        below, acc = lax.fori_loop(0, n_groups // 2, lambda t2, carry: group(2 * t2 + 1, group(2 * t2, carry)),
                                   (c_s[...], (zeros,) * 4))
        c_s[...] = below
        for k in range(3):
            gw_ref[k:k + 1, :] += jnp.sum(acc[k], axis=0, keepdims=True)
        gb_ref[...] += jnp.sum(acc[3], axis=0, keepdims=True)

    blk = pl.BlockSpec((FFN_TM, FFN_TN), lambda j, i: (nrow - 1 - i, j))
    return pl.pallas_call(
        body, name="ffn_mid_bwd", grid=(FFN_NJ, nrow),
        in_specs=[pl.BlockSpec((FFN_TM, FFN_HALF), lambda j, i: (nrow - 1 - i, j)), blk, blk,
                  pl.BlockSpec((3, FFN_TN), lambda j, i: (0, j))],
        out_specs=[blk, pl.BlockSpec((3, FFN_TN), lambda j, i: (0, j)), pl.BlockSpec((1, FFN_TN), lambda j, i: (0, j))],
        out_shape=[_sds((SEQ, 2 * D_FF), BF16), _sds((3, 2 * D_FF), F32), _sds((1, 2 * D_FF), F32)],
        scratch_shapes=[pltpu.VMEM((SUBLANE, FFN_TN), F32)],
        compiler_params=_params("parallel", "arbitrary"),
    )(dm, u, ab, conv_w)


def _down_fwd(m, w_down, x2, g_post, target, *, tm=512):
    def body(m_ref, w_ref, x2_ref, g_ref, t_ref, dout_ref, dy_ref, gg_ref, loss_ref):
        @pl.when(pl.program_id(0) == 0)
        def _():
            gg_ref[...] = jnp.zeros_like(gg_ref)
            loss_ref[...] = jnp.zeros_like(loss_ref)

        y = jnp.dot(m_ref[...], w_ref[...], preferred_element_type=F32)
        r = lax.rsqrt(jnp.mean(y * y, axis=-1, keepdims=True) + RMS_EPS)
        yn = y * r
        diff = (x2_ref[...] + yn * g_ref[...]) - t_ref[...]
        loss_ref[...] += jnp.sum(diff * diff)
        dout = diff * (1.0 / D_MODEL)
        dout_ref[...] = dout
        gg_ref[...] += jnp.sum(dout * yn, axis=0, keepdims=True)
        dn = dout * g_ref[...]
        dy_ref[...] = (r * (dn - yn * jnp.mean(dn * yn, axis=-1, keepdims=True))).astype(dy_ref.dtype)

    row = pl.BlockSpec((tm, D_MODEL), lambda i: (i, 0))
    vec = pl.BlockSpec((1, D_MODEL), lambda i: (0, 0))
    return pl.pallas_call(
        body, name="down_fwd", grid=(SEQ // tm,),
        in_specs=[pl.BlockSpec((tm, D_FF), lambda i: (i, 0)), pl.BlockSpec((D_FF, D_MODEL), lambda i: (0, 0)),
                  row, vec, row],
        out_specs=[row, row, vec, pl.BlockSpec((1, LANE), lambda i: (0, 0))],
        out_shape=[_sds((SEQ, D_MODEL), F32), _sds((SEQ, D_MODEL), BF16), _sds((1, D_MODEL), F32),
                   _sds((1, LANE), F32)],
        compiler_params=_params("arbitrary"),
    )(m, w_down, x2, g_post, target)


def _local_step(x, target, w_main, w_f, b_forget, conv_b, g_pre_mix, g_post_mix, g_pre_ffn, g_post_ffn,
                late_weights, ffn_grads_ready, proj_grads_ready, mixer_grads_ready):
    mm = _matmul
    tabs = _rope_tables()

    h1 = _rms_fwd(x, g_pre_mix, name="rms_pre_mix")
    zm = mm(h1, w_main, out_dtype=BF16, tm=2048, tn=512, tk=1024, name="in_proj")
    zf = mm(h1, w_f, out_dtype=F32, tm=2048, tn=F_PAD, tk=1024, name="in_proj_forget")
    f_row, sg_row = _fox_prep(zf[:, :N_HEADS].T, b_forget.reshape(N_HEADS, 1))
    f_cols = jnp.pad(f_row.T, ((0, 0), (0, LANE - N_HEADS)))
    q_slots, k_slots, v_slots = _fox_pack_fwd(zm, f_cols)
    ya, lse_a = _fox_fwd(q_slots, k_slots, v_slots)
    qkv_d = _rope_fwd(zm, tabs)
    dil = [_dil_fwd(qkv_d, d) for _, d in DIL_PATTERNS]
    yb, lse_b = _dil_merge([o for o, _ in dil], [l for _, l in dil])
    w_oa, w_ob, w_out, w_up, conv_w, w_down = late_weights(yb)
    pa, pb, mixed = _mix_fwd(ya, yb, w_oa, w_ob, zm)
    y1, x2, h2 = _out_fwd(mixed, w_out, x, g_post_mix, g_pre_ffn)
    u = mm(h2, w_up, out_dtype=BF16, tm=2048, tn=512, tk=1024, name="up_proj")
    m, ab = _ffn_mid_fwd(u, conv_w, _ffn_interleave(conv_b))
    dout, dy2, gg_post_ffn, sq_err = _down_fwd(m, w_down, x2, g_post_ffn, target)

    g_w_down = mm(m, dy2, ta=True, out_dtype=BF16, tm=D_FF // 2, tn=1024, tk=2048, name="grad_w_down")
    dm = mm(dy2, w_down, tb=True, out_dtype=BF16, tm=2048, tn=D_FF // 2, tk=1024, name="d_ffn_mid")
    du, g_conv_w, g_conv_b = _ffn_mid_bwd(dm, u, ab, conv_w)
    g_w_up = mm(h2, du, ta=True, out_dtype=BF16, tm=1024, tn=D_FF // 2, tk=2048, name="grad_w_up")
    tok = ffn_grads_ready(dict(w_down=g_w_down, w_up=_ffn_move_blocks(g_w_up, interleave=False, name="grad_w_up_cols"),
                               conv_w=_ffn_deinterleave(g_conv_w)))
    dh2 = mm(du, w_up, tb=True, out_dtype=BF16, tm=512, tn=1024, tk=2 * D_FF, name="d_h2")

    dx2, dy1, gg_pre_ffn, gg_post_mix = _rms_pair_bwd([dh2], x2, g_pre_ffn, dout, y1, g_post_mix + tok)
    g_w_out = mm(mixed, dy1, ta=True, out_dtype=BF16, tm=1024, tn=1024, tk=2048, name="grad_w_out")
    dmix = mm(dy1, w_out, tb=True, out_dtype=BF16, tm=2048, tn=1024, tk=1024, name="d_mixed")
    dpa, dpb, dgates = _gate_bwd(dmix, zm, pa, pb)
    g_w_oa = mm(ya, dpa, ta=True, out_dtype=BF16, tm=512, tn=1024, tk=SEQ, name="grad_w_o_fox")
    g_w_ob = mm(yb, dpb, ta=True, out_dtype=BF16, tm=512, tn=1024, tk=SEQ, name="grad_w_o_dil")
    tok = proj_grads_ready(dict(w_o_fox=g_w_oa, w_o_dil=g_w_ob, w_out=g_w_out))
    dya = mm(dpa, w_oa, tb=True, out_dtype=BF16, tm=2048, tn=512, tk=1024, name="d_y_fox")
    dyb = mm(dpb, w_ob, tb=True, out_dtype=BF16, tm=2048, tn=512, tk=1024, name="d_y_dil")

    qb_slots, do_slots = _fox_pack_bwd(zm, f_cols + tok, lse_a, ya, dya)
    d_fox, df_cols = _fox_unpack(*_fox_bwd(qb_slots, k_slots, v_slots, do_slots))
    dfa_t, g_b_forget = _fox_post_bwd(df_cols[:, :N_HEADS].T, sg_row)

    delta_b = _attn_delta(yb, dyb, name="delta_dil")
    dil_g = [_dil_bwd(qkv_d, lse_b, delta_b, dyb, d) for _, d in DIL_PATTERNS]
    d_dil = _dil_grad_combine([g[0] for g in dil_g], [g[1] for g in dil_g], [g[2] for g in dil_g], tabs)

    dz = jnp.concatenate([d_fox, d_dil, dgates], axis=1)
    dzf = jnp.pad(dfa_t.T, ((0, 0), (0, F_PAD - N_HEADS)))
    g_w_main = mm(h1, dz, ta=True, out_dtype=BF16, tm=1024, tn=Z_MAIN // 4, tk=2048, name="grad_w_in")
    g_w_f = mm(h1, dzf, ta=True, out_dtype=BF16, tm=1024, tn=F_PAD, tk=1024, name="grad_w_in_forget")
    tok = mixer_grads_ready(dict(w_main=g_w_main, w_f=g_w_f))
    dh1 = [mm(dz, w_main, tb=True, out_dtype=BF16, tm=512, tn=1024, tk=Z_MAIN, name="d_h1"),
           mm(dzf + tok, w_f, tb=True, out_dtype=F32, tm=2048, tn=1024, tk=F_PAD, name="d_h1_forget")]
    grad_x, gg_pre_mix = _rms_bwd(dh1, x, g_pre_mix, dx2, out_dtype=F32, name="rms_pre_mix_bwd")

    grads = dict(
        b_forget=g_b_forget.reshape(1, N_HEADS), conv_b=_ffn_deinterleave(g_conv_b),
        g_pre_mix=gg_pre_mix, g_post_mix=gg_post_mix, g_pre_ffn=gg_pre_ffn, g_post_ffn=gg_post_ffn)
    return sq_err[0, 0], grad_x, grads


def _exchange(arrays, scatter, *, name):
    n = len(arrays)

    def body(*refs):
        ins, outs = refs[:n], refs[n:2 * n]
        send_sems, recv_sems, local_sems = refs[2 * n:]
        x, y, c = lax.axis_index("x"), lax.axis_index("y"), lax.axis_index("c")
        me = 4 * x + 2 * y + c
        peers = []
        for k in range(1, N_DEV):
            px = 1 - x if k & 4 else x
            py = 1 - y if k & 2 else y
            pc = 1 - c if k & 1 else c
            peers.append(((px, py, pc), 4 * px + 2 * py + pc))

        def remote(a, k):
            dev, slot = peers[k]
            return pltpu.make_async_remote_copy(
                src_ref=ins[a].at[slot] if scatter else ins[a], dst_ref=outs[a].at[me],
                send_sem=send_sems.at[a, k], recv_sem=recv_sems.at[a, k],
                device_id=dev, device_id_type=MESH_ID)

        def landed(a, k):
            dev, slot = peers[k]
            return pltpu.make_async_remote_copy(
                src_ref=outs[a].at[slot], dst_ref=outs[a].at[slot],
                send_sem=send_sems.at[a, k], recv_sem=recv_sems.at[a, k],
                device_id=dev, device_id_type=MESH_ID)

        own = [pltpu.make_async_copy(ins[a].at[me] if scatter else ins[a], outs[a].at[me], local_sems.at[a])
               for a in range(n)]
        copies = [remote(a, k) for k in range(N_DEV - 1) for a in range(n)]
        for cp in own + copies:
            cp.start()
        for k in range(N_DEV - 1):
            for a in range(n):
                landed(a, k).wait_recv()
        for cp in copies:
            cp.wait_send()
        for cp in own:
            cp.wait()

    out_shape = [_sds(((N_DEV,) + a.shape[-2:]), a.dtype) for a in arrays]
    return pl.pallas_call(
        body, name=name, in_specs=[ANY] * n, out_specs=[ANY] * n, out_shape=out_shape,
        scratch_shapes=[pltpu.SemaphoreType.DMA((n, N_DEV - 1)), pltpu.SemaphoreType.DMA((n, N_DEV - 1)),
                        pltpu.SemaphoreType.DMA((n,))],
    )(*arrays)


def _gather_two_level(shard, *, name):
    def body(x_ref, out_ref, send_sems, recv_sems, local_sem):
        x, y, c = lax.axis_index("x"), lax.axis_index("y"), lax.axis_index("c")
        me, sibling = (x, y, c), (x, y, 1 - c)
        chips = [(1 - x, y), (x, 1 - y), (1 - x, 1 - y)]

        def slot(px, py, pc):
            return out_ref.at[4 * px + 2 * py + pc]

        def copy(k, block, to, src=None):
            return pltpu.make_async_remote_copy(
                src_ref=slot(*block) if src is None else src, dst_ref=slot(*block),
                send_sem=send_sems.at[k], recv_sem=recv_sems.at[k], device_id=to, device_id_type=MESH_ID)

        mine = pltpu.make_async_copy(x_ref, slot(*me), local_sem)
        mine.start()
        first = [copy(0, me, sibling, src=x_ref)]
        first += [copy(1 + j, me, (*chip, c), src=x_ref) for j, chip in enumerate(chips)]
        for cp in first:
            cp.start()
        passed = [copy(4 + j, (*chip, c), sibling) for j, chip in enumerate(chips)]
        for j, chip in enumerate(chips):
            copy(1 + j, (*chip, c), me).wait_recv()
            passed[j].start()
        copy(0, sibling, me).wait_recv()
        for j, chip in enumerate(chips):
            copy(4 + j, (*chip, 1 - c), me).wait_recv()
        for cp in first + passed:
            cp.wait_send()
        mine.wait()

    return pl.pallas_call(
        body, name=name, in_specs=[ANY], out_specs=ANY, out_shape=_sds((N_DEV,) + shard.shape, shard.dtype),
        scratch_shapes=[pltpu.SemaphoreType.DMA((N_DEV - 1,)), pltpu.SemaphoreType.DMA((N_DEV - 1,)),
                        pltpu.SemaphoreType.DMA],
    )(shard)


N_CHIPS = N_DEV // 2


def _peers(chips_only=False):
    x, y, c = lax.axis_index("x"), lax.axis_index("y"), lax.axis_index("c")
    out = []
    if chips_only:
        for k in range(1, N_CHIPS):
            px = 1 - x if k & 2 else x
            py = 1 - y if k & 1 else y
            out.append(((px, py, c), 2 * px + py))
        return 2 * x + y, out
    for k in range(1, N_DEV):
        px = 1 - x if k & 4 else x
        py = 1 - y if k & 2 else y
        pc = 1 - c if k & 1 else c
        out.append(((px, py, pc), 4 * px + 2 * py + pc))
    return 4 * x + 2 * y + c, out


def _sibling_swap(slots, *, name):
    def body(in_ref, out_ref, send_sems, recv_sems):
        x, y, c = lax.axis_index("x"), lax.axis_index("y"), lax.axis_index("c")
        copies = [pltpu.make_async_remote_copy(
            src_ref=in_ref.at[2 * q + (1 - c)], dst_ref=out_ref.at[q], send_sem=send_sems.at[q],
            recv_sem=recv_sems.at[q], device_id=(x, y, 1 - c), device_id_type=MESH_ID) for q in range(N_CHIPS)]
        for cp in copies:
            cp.start()
        for cp in copies:
            cp.wait_recv()
        for cp in copies:
            cp.wait_send()

    return pl.pallas_call(
        body, name=name, in_specs=[ANY], out_specs=ANY,
        out_shape=_sds((N_CHIPS,) + slots.shape[1:], slots.dtype),
        scratch_shapes=[pltpu.SemaphoreType.DMA((N_CHIPS,)), pltpu.SemaphoreType.DMA((N_CHIPS,))],
    )(slots)


def _pair_sum(slots, from_sibling, *, name, tm=256):
    _, r, c = slots.shape
    core = lax.axis_index("c").astype(jnp.int32).reshape(1)

    def body(core_ref, a_ref, b_ref, o_ref):
        o_ref[...] = (a_ref[...].astype(F32) + b_ref[...].astype(F32)).astype(o_ref.dtype)

    blk = lambda f: pl.BlockSpec((1, tm, c), f)
    return pl.pallas_call(
        body, name=name,
        grid_spec=pltpu.PrefetchScalarGridSpec(
            num_scalar_prefetch=1, grid=(N_CHIPS, r // tm),
            in_specs=[blk(lambda q, i, core: (2 * q + core[0], i, 0)), blk(lambda q, i, core: (q, i, 0))],
            out_specs=blk(lambda q, i, core: (q, i, 0))),
        out_shape=_sds((N_CHIPS, r, c), slots.dtype),
        compiler_params=_params("parallel", "parallel"),
    )(core, slots, from_sibling)


HBM = pl.BlockSpec(memory_space=pltpu.HBM)
SEM = pl.BlockSpec(memory_space=pltpu.SEMAPHORE)
DATAFLOW = pltpu.SideEffectType.DATAFLOW_SIDE_EFFECTING


def _split_copy(srcs, lands, send_sems, recv_sems, scatter, a, k, me, peers, incoming=False):
    dev, slot = peers[k]
    if incoming:
        src = dst = lands[a].at[slot]
    else:
        src, dst = (srcs[a].at[slot] if scatter else srcs[a]), lands[a].at[me]
    sem = a * len(peers) + k
    return pltpu.make_async_remote_copy(
        src_ref=src, dst_ref=dst, send_sem=send_sems.at[sem], recv_sem=recv_sems.at[sem],
        device_id=dev, device_id_type=MESH_ID)


def _exchange_start(arrays, scatter, *, name, chips_only=False):
    n = len(arrays)
    n_slots = N_CHIPS if chips_only else N_DEV

    def body(*refs):
        srcs, lands = refs[:n], refs[n:2 * n]
        send_sems, recv_sems = refs[2 * n], refs[2 * n + 1]
        token = refs[-1]
        me, peers = _peers(chips_only)
        for k in range(len(peers)):
            for a in range(n):
                _split_copy(srcs, lands, send_sems, recv_sems, scatter, a, k, me, peers).start()
        token[...] = jnp.zeros_like(token)

    land_shapes = [((n_slots,) + a.shape[-2:], a.dtype) for a in arrays]
    sems = pltpu.SemaphoreType.DMA((n * (n_slots - 1),))
    outs = pl.pallas_call(
        body, name=name,
        out_shape=(sems, sems, *[pltpu.HBM(a.shape, a.dtype) for a in arrays],
                   *[pltpu.HBM(s, d) for s, d in land_shapes], _sds((SUBLANE, LANE), F32)),
        in_specs=[HBM] * (2 * n),
        out_specs=(SEM, SEM, *[HBM] * (2 * n), pl.BlockSpec(memory_space=pltpu.VMEM)),
        input_output_aliases={i: 2 + i for i in range(2 * n)},
        compiler_params=pltpu.CompilerParams(has_side_effects=DATAFLOW),
    )(*[pltpu.with_memory_space_constraint(a, pltpu.HBM) for a in arrays],
      *[pltpu.with_memory_space_constraint(lax.empty(s, d), pltpu.HBM) for s, d in land_shapes])
    return (outs[0], outs[1], outs[2:2 + n], outs[2 + n:2 + 2 * n], scatter, chips_only), outs[-1]


def _exchange_wait(handles, after, *, name):
    send_sems, recv_sems, srcs, lands, scatter, chips_only = handles
    n = len(srcs)

    def body(*refs):
        src_refs, land_refs = refs[:n], refs[n:2 * n]
        send_ref, recv_ref = refs[2 * n], refs[2 * n + 1]
        me, peers = _peers(chips_only)
        for k in range(len(peers)):
            for a in range(n):
                _split_copy(src_refs, land_refs, send_ref, recv_ref, scatter, a, k, me, peers).wait_send()
                _split_copy(src_refs, land_refs, send_ref, recv_ref, scatter, a, k, me, peers, True).wait_recv()

    outs = pl.pallas_call(
        body, name=name,
        out_shape=tuple(pltpu.HBM(t.shape, t.dtype) for t in (*srcs, *lands)),
        in_specs=[HBM] * (2 * n) + [SEM, SEM, pl.BlockSpec(memory_space=pl.ANY)],
        out_specs=tuple([HBM] * (2 * n)),
        input_output_aliases={i: i for i in range(2 * n)},
        compiler_params=pltpu.CompilerParams(has_side_effects=DATAFLOW),
    )(*srcs, *lands, send_sems, recv_sems, after)
    return _with_own_slot(outs[n:], outs[:n], scatter, chips_only)


def _with_own_slot(landed, own, scatter, chips_only):
    me = 2 * lax.axis_index("x") + lax.axis_index("y")
    if not chips_only:
        me = 2 * me + lax.axis_index("c")
    out = []
    for buf, src in zip(landed, own):
        mine = lax.dynamic_index_in_dim(src, me, 0, keepdims=False) if scatter else src
        out.append(lax.dynamic_update_index_in_dim(buf, mine, me, 0))
    return out


def _adamw(parts, w, m, v, *, name, tm):
    r, c = w.shape
    assert r % tm == 0

    def body(p_ref, w_ref, m_ref, v_ref, g_ref, d_ref, nm_ref, nv_ref):
        _adamw_update(p_ref, w_ref, m_ref, v_ref, g_ref, d_ref, nm_ref, nv_ref)

    blk = pl.BlockSpec((tm, c), lambda i: (i, 0))
    return pl.pallas_call(
        body, name=name, grid=(r // tm,),
        in_specs=[pl.BlockSpec((parts.shape[0], tm, c), lambda i: (0, i, 0)), blk, blk, blk],
        out_specs=[blk] * 4, out_shape=[_sds((r, c), F32)] * 4,
        compiler_params=_params("parallel"),
    )(parts, w, m, v)


def _adamw_update(p_ref, w_ref, m_ref, v_ref, g_ref, d_ref, nm_ref, nv_ref):
    g = p_ref[0].astype(F32)
    for s in range(1, p_ref.shape[0]):
        g = g + p_ref[s].astype(F32)
    g_ref[...] = g
    m_new = ADAM_B1 * m_ref[...] + (1.0 - ADAM_B1) * g
    v_new = ADAM_B2 * v_ref[...] + (1.0 - ADAM_B2) * (g * g)
    nm_ref[...] = m_new
    nv_ref[...] = v_new
    m_hat = m_new / (1.0 - ADAM_B1 ** ADAM_STEP)
    v_hat = v_new / (1.0 - ADAM_B2 ** ADAM_STEP)
    d_ref[...] = -ADAM_LR * (m_hat / (jnp.sqrt(v_hat) + ADAM_EPS) + ADAM_WD * w_ref[...])


SMALL = ("g_pre_mix", "b_forget", "g_post_mix", "g_pre_ffn", "conv_b", "g_post_ffn")


def _adamw_small(parts, ws, ms, vs):
    n = len(ws)

    def body(*refs):
        ins, outs = refs[:4 * n], refs[4 * n:]
        for i in range(n):
            _adamw_update(ins[i], ins[n + i], ins[2 * n + i], ins[3 * n + i], *outs[4 * i:4 * i + 4])

    res = pl.pallas_call(
        body, name="adamw_small", out_shape=[_sds(w.shape, F32) for w in ws for _ in range(4)],
        compiler_params=pltpu.CompilerParams(vmem_limit_bytes=VMEM_LIMIT),
    )(*parts, *ws, *ms, *vs)
    return [res[4 * i:4 * i + 4] for i in range(n)]


def kernel(x, g_pre_mix, w_in, b_forget, w_o_fox, w_o_dil, w_out, g_post_mix, g_pre_ffn, w_up, conv_w, conv_b, w_down, g_post_ffn, loss_target, m_g_pre_mix, m_w_in, m_b_forget, m_w_o_fox, m_w_o_dil, m_w_out, m_g_post_mix, m_g_pre_ffn, m_w_up, m_conv_w, m_conv_b, m_w_down, m_g_post_ffn, v_g_pre_mix, v_w_in, v_b_forget, v_w_o_fox, v_w_o_dil, v_w_out, v_g_post_mix, v_g_pre_ffn, v_w_up, v_conv_w, v_conv_b, v_w_down, v_g_post_ffn):
    names = ("g_pre_mix", "w_in", "b_forget", "w_o_fox", "w_o_dil", "w_out", "g_post_mix", "g_pre_ffn",
             "w_up", "conv_w", "conv_b", "w_down", "g_post_ffn")
    w = dict(g_pre_mix=g_pre_mix, w_in=w_in, b_forget=b_forget, w_o_fox=w_o_fox, w_o_dil=w_o_dil, w_out=w_out,
             g_post_mix=g_post_mix, g_pre_ffn=g_pre_ffn, w_up=w_up, conv_w=conv_w, conv_b=conv_b, w_down=w_down,
             g_post_ffn=g_post_ffn)
    m = dict(g_pre_mix=m_g_pre_mix, w_in=m_w_in, b_forget=m_b_forget, w_o_fox=m_w_o_fox, w_o_dil=m_w_o_dil,
             w_out=m_w_out, g_post_mix=m_g_post_mix, g_pre_ffn=m_g_pre_ffn, w_up=m_w_up, conv_w=m_conv_w,
             conv_b=m_conv_b, w_down=m_w_down, g_post_ffn=m_g_post_ffn)
    v = dict(g_pre_mix=v_g_pre_mix, w_in=v_w_in, b_forget=v_b_forget, w_o_fox=v_w_o_fox, w_o_dil=v_w_o_dil,
             w_out=v_w_out, g_post_mix=v_g_post_mix, g_pre_ffn=v_g_pre_ffn, w_up=v_w_up, conv_w=v_conv_w,
             conv_b=v_conv_b, w_down=v_w_down, g_post_ffn=v_g_post_ffn)
    sharded = ("w_in", "w_o_fox", "w_o_dil", "w_out", "w_up", "w_down", "conv_w")
    wire = lambda n: F32 if n == "conv_w" else BF16

    by_cols = lambda t: jnp.transpose(t, (1, 0, 2)).reshape(t.shape[1], N_DEV * t.shape[2])
    by_rows = lambda t: t.reshape(N_DEV * t.shape[1], t.shape[2])
    col_slots = lambda t: jnp.transpose(t.reshape(t.shape[0], N_DEV, t.shape[1] // N_DEV), (1, 0, 2))
    row_slots = lambda t: t.reshape(N_DEV, t.shape[0] // N_DEV, t.shape[1])
    to_slots = lambda n, t: (row_slots if n in ("w_out", "w_down") else col_slots)(t).astype(wire(n))
    shard = lambda n: w[n][0].astype(wire(n))
    f_lo, f_hi = 3 * ATT_W, 3 * ATT_W + N_HEADS

    w_in_full = by_cols(_gather_two_level(shard("w_in"), name="gather_w_in"))
    w_main = jnp.concatenate([w_in_full[:, :f_lo], w_in_full[:, f_hi:]], axis=1)
    w_f = jnp.pad(w_in_full[:, f_lo:f_hi], ((0, 0), (0, F_PAD - N_HEADS)))
    late = ("w_o_fox", "w_o_dil", "w_out", "w_up", "conv_w", "w_down")
    order = jnp.minimum(jnp.abs(w_in_full[0, 0].astype(F32)), 0.0)
    late_handles, late_tok = _exchange_start(
        [shard(n) + order.astype(wire(n)) if n == "conv_w" else shard(n) for n in late], False,
        name="gather_late_start")

    def late_weights(after):
        got = dict(zip(late, _exchange_wait(late_handles, after, name="gather_late_wait")))
        return (by_cols(got["w_o_fox"]), by_cols(got["w_o_dil"]), by_rows(got["w_out"]),
                _ffn_move_blocks(by_cols(got["w_up"]), interleave=True, name="w_up_cols"),
                _ffn_interleave(by_cols(got["conv_w"])),
                by_rows(got["w_down"]))

    pending = {}

    def ffn_grads_ready(g):
        pending["ffn"] = _exchange_start([to_slots(n, g[n]) for n in ("w_down", "w_up", "conv_w")], True,
                                         name="scatter_ffn_start")
        return pending["ffn"][1][0, 0]

    def proj_grads_ready(g):
        pending["proj"] = _exchange_start([to_slots(n, g[n]) for n in ("w_o_fox", "w_o_dil", "w_out")], True,
                                          name="scatter_proj_start")
        return pending["proj"][1][0, 0]

    def mixer_grads_ready(g):
        g_w_in = jnp.concatenate([g["w_main"][:, :f_lo], g["w_f"][:, :N_HEADS], g["w_main"][:, f_lo:]], axis=1)
        slots = to_slots("w_in", g_w_in)
        chip_sums = _pair_sum(slots, _sibling_swap(slots, name="scatter_w_in_swap"), name="scatter_w_in_pair_sum")
        pending["w_in"] = _exchange_start([chip_sums], True, name="scatter_w_in_start", chips_only=True)
        return pending["w_in"][1][0, 0]

    sq_err, grad_x, g = _local_step(
        x[0], loss_target[0], w_main, w_f, b_forget, conv_b, g_pre_mix + late_tok[0, 0], g_post_mix, g_pre_ffn,
        g_post_ffn, late_weights, ffn_grads_ready, proj_grads_ready, mixer_grads_ready)
    loss = lax.psum(0.5 * sq_err / D_MODEL, ("x", "y", "c"))

    tiles = dict(w_in=256, w_o_fox=512, w_o_dil=512, w_out=128, w_up=256, w_down=176, conv_w=3)
    adam = lambda n, p: _adamw(p, w[n][0], m[n][0], v[n][0], name=f"adamw_{n}", tm=tiles[n])
    res = {}
    for key, group in (("ffn", ("w_down", "w_up", "conv_w")), ("proj", ("w_o_fox", "w_o_dil", "w_out"))):
        landed = _exchange_wait(pending[key][0], grad_x, name=f"scatter_{key}_wait")
        res.update({n: adam(n, p) for n, p in zip(group, landed)})
    small_parts = _exchange([g[n] for n in SMALL], False, name="gather_small_grads")
    done = res["w_up"][3]
    res["w_in"] = adam("w_in", _exchange_wait(pending["w_in"][0], done, name="scatter_w_in_wait")[0])
    small = dict(zip(SMALL, _adamw_small(small_parts, *[[t[n] for n in SMALL] for t in (w, m, v)])))
    out = [[(res[n][k][None] if n in sharded else small[n][k]) for n in names] for k in range(4)]
    return (loss, grad_x[None], *out[0], *out[1], *out[2], *out[3])
```

```python
import functools
import math

import jax
import jax.numpy as jnp
import numpy as np
from jax import lax
from jax.experimental import pallas as pl
from jax.experimental.pallas import tpu as pltpu

F32 = jnp.float32
BF16 = jnp.bfloat16

SEQ = 4096
D_MODEL = 1024
N_HEADS = 8
HEAD_DIM = 64
ATT_W = N_HEADS * HEAD_DIM
D_FF = 2816
Z_MAIN = 5120
F_PAD = 128
ROPE_DIM = 16
ROPE_THETA = 500000.0
RMS_EPS = 1e-6
NEG_INF = -1e30
SCALE = 1.0 / math.sqrt(HEAD_DIM)
DIL_PATTERNS = ((128, 1), (512, 4), (2048, 16))
DIL_BLK = 128
N_DEV = 8

ADAM_LR = 0.001
ADAM_B1 = 0.9
ADAM_B2 = 0.999
ADAM_EPS = 1e-08
ADAM_WD = 0.01
ADAM_STEP = 10

LANE = 128
SUBLANE = 8
VMEM_LIMIT = 56 * 1024 * 1024
MESH_ID = pl.DeviceIdType.MESH
ANY = pl.BlockSpec(memory_space=pl.ANY)


def _params(*sem):
    return pltpu.CompilerParams(dimension_semantics=sem, vmem_limit_bytes=VMEM_LIMIT)


def _sds(shape, dtype):
    return jax.ShapeDtypeStruct(shape, dtype)


def _matmul(a, b, *, ta=False, tb=False, out_dtype, tm, tn, tk, name, b_k_off=0):
    if ta:
        kk, m = a.shape
    else:
        m, kk = a.shape
    n = b.shape[0] if tb else b.shape[1]
    tm, tn, tk = min(tm, m), min(tn, n), min(tk, kk)
    assert (b.shape[1] if tb else b.shape[0]) >= b_k_off * tk + kk
    assert m % tm == 0 and n % tn == 0 and kk % tk == 0, (name, m, n, kk, tm, tn, tk)
    nk = kk // tk
    dims = (((0 if ta else 1,), (1 if tb else 0,)), ((), ()))

    def body(a_ref, b_ref, o_ref, *scratch):
        p = lax.dot_general(a_ref[...].astype(BF16), b_ref[...].astype(BF16), dims,
                            preferred_element_type=F32)
        if nk == 1:
            o_ref[...] = p.astype(o_ref.dtype)
        else:
            acc = scratch[0]
            k = pl.program_id(2)

            @pl.when(k == 0)
            def _():
                acc[...] = p

            @pl.when(k > 0)
            def _():
                acc[...] += p

            @pl.when(k == nk - 1)
            def _():
                o_ref[...] = acc[...].astype(o_ref.dtype)

    a_spec = (pl.BlockSpec((tk, tm), lambda i, j, k: (k, i)) if ta
              else pl.BlockSpec((tm, tk), lambda i, j, k: (i, k)))
    b_spec = (pl.BlockSpec((tn, tk), lambda i, j, k: (j, k + b_k_off)) if tb
              else pl.BlockSpec((tk, tn), lambda i, j, k: (k + b_k_off, j)))
    return pl.pallas_call(
        body, name=name, grid=(m // tm, n // tn, nk),
        in_specs=[a_spec, b_spec],
        out_specs=pl.BlockSpec((tm, tn), lambda i, j, k: (i, j)),
        out_shape=_sds((m, n), out_dtype),
        scratch_shapes=[pltpu.VMEM((tm, tn), F32)] if nk > 1 else [],
        compiler_params=_params("parallel", "parallel", "arbitrary"),
    )(a, b)


def _rms_fwd(x, g, *, name, tm=512):
    def body(x_ref, g_ref, h_ref):
        xv = x_ref[...]
        r = lax.rsqrt(jnp.mean(xv * xv, axis=-1, keepdims=True) + RMS_EPS)
        h_ref[...] = (xv * r * g_ref[...]).astype(h_ref.dtype)

    return pl.pallas_call(
        body, name=name, grid=(SEQ // tm,),
        in_specs=[pl.BlockSpec((tm, D_MODEL), lambda i: (i, 0)), pl.BlockSpec((1, D_MODEL), lambda i: (0, 0))],
        out_specs=pl.BlockSpec((tm, D_MODEL), lambda i: (i, 0)),
        out_shape=_sds((SEQ, D_MODEL), BF16),
        compiler_params=_params("parallel"),
    )(x, g)


def _rms_bwd(dh_parts, xin, g, dres, *, out_dtype, name, tm=512):
    n_parts = len(dh_parts)
    has_res = dres is not None

    def body(*refs):
        parts = refs[:n_parts]
        x_ref, g_ref = refs[n_parts], refs[n_parts + 1]
        res_ref = refs[n_parts + 2] if has_res else None
        o_ref, gg_ref = refs[-2], refs[-1]
        dh = parts[0][...].astype(F32)
        for p in parts[1:]:
            dh = dh + p[...].astype(F32)
        xv = x_ref[...]
        r = lax.rsqrt(jnp.mean(xv * xv, axis=-1, keepdims=True) + RMS_EPS)
        xn = xv * r

        @pl.when(pl.program_id(0) == 0)
        def _():
            gg_ref[...] = jnp.zeros_like(gg_ref)

        gg_ref[...] += jnp.sum(dh * xn, axis=0, keepdims=True)
        dxn = dh * g_ref[...]
        dx = r * (dxn - xn * jnp.mean(dxn * xn, axis=-1, keepdims=True))
        if has_res:
            dx = dx + res_ref[...]
        o_ref[...] = dx.astype(o_ref.dtype)

    row = pl.BlockSpec((tm, D_MODEL), lambda i: (i, 0))
    vec = pl.BlockSpec((1, D_MODEL), lambda i: (0, 0))
    args = list(dh_parts) + [xin, g] + ([dres] if has_res else [])
    return pl.pallas_call(
        body, name=name, grid=(SEQ // tm,),
        in_specs=[row] * n_parts + [row, vec] + ([row] if has_res else []),
        out_specs=[row, vec],
        out_shape=[_sds((SEQ, D_MODEL), out_dtype), _sds((1, D_MODEL), F32)],
        compiler_params=_params("arbitrary"),
    )(*args)


def _rms_pair_bwd(dh_parts, x2, g_pre, dres, y1, g_post, *, tm=512):
    n_parts = len(dh_parts)

    def norm_bwd(dh, xin, g_ref, gg_ref):
        r = lax.rsqrt(jnp.mean(xin * xin, axis=-1, keepdims=True) + RMS_EPS)
        xn = xin * r
        gg_ref[...] += jnp.sum(dh * xn, axis=0, keepdims=True)
        dxn = dh * g_ref[...]
        return r * (dxn - xn * jnp.mean(dxn * xn, axis=-1, keepdims=True))

    def body(*refs):
        parts = refs[:n_parts]
        x2_ref, gpre_ref, res_ref, y1_ref, gpost_ref, dx2_ref, dy1_ref, ggpre_ref, ggpost_ref = refs[n_parts:]

        @pl.when(pl.program_id(0) == 0)
        def _():
            ggpre_ref[...] = jnp.zeros_like(ggpre_ref)
            ggpost_ref[...] = jnp.zeros_like(ggpost_ref)

        dh = parts[0][...].astype(F32)
        for p in parts[1:]:
            dh = dh + p[...].astype(F32)
        dx2 = res_ref[...] + norm_bwd(dh, x2_ref[...], gpre_ref, ggpre_ref)
        dx2_ref[...] = dx2
        dy1_ref[...] = norm_bwd(dx2, y1_ref[...], gpost_ref, ggpost_ref).astype(dy1_ref.dtype)

    row = pl.BlockSpec((tm, D_MODEL), lambda i: (i, 0))
    vec = pl.BlockSpec((1, D_MODEL), lambda i: (0, 0))
    return pl.pallas_call(
        body, name="rms_pair_bwd", grid=(SEQ // tm,),
        in_specs=[row] * n_parts + [row, vec, row, row, vec],
        out_specs=[row, row, vec, vec],
        out_shape=[_sds((SEQ, D_MODEL), F32), _sds((SEQ, D_MODEL), BF16), _sds((1, D_MODEL), F32),
                   _sds((1, D_MODEL), F32)],
        compiler_params=_params("arbitrary"),
    )(*dh_parts, x2, g_pre, dres, y1, g_post)


SCAN_BLK = 512


def _split_dot(v, tri):
    hi = v.astype(BF16)
    r1 = v - hi.astype(F32)
    mid = r1.astype(BF16)
    lo = (r1 - mid.astype(F32)).astype(BF16)
    dot = functools.partial(jnp.dot, preferred_element_type=F32)
    return dot(hi, tri) + dot(mid, tri) + dot(lo, tri)


def _fox_prep(fa_t, b_col):
    nblk = SEQ // SCAN_BLK

    def body(fa_ref, b_ref, f_ref, sg_ref):
        row = lax.broadcasted_iota(jnp.int32, (SCAN_BLK, SCAN_BLK), 0)
        col = lax.broadcasted_iota(jnp.int32, (SCAN_BLK, SCAN_BLK), 1)
        upper = (row <= col).astype(BF16)
        carry = jnp.zeros((N_HEADS, 1), F32)
        for blk in range(nblk):
            sl = pl.ds(blk * SCAN_BLK, SCAN_BLK)
            xx = fa_ref[:, sl] + b_ref[...]
            e = jnp.exp(-jnp.abs(xx))
            logf = jnp.minimum(xx, 0.0) - jnp.log(1.0 + e)
            sg_ref[:, sl] = jnp.where(xx >= 0.0, e, 1.0) / (1.0 + e)
            c = _split_dot(logf, upper) + carry
            f_ref[:, sl] = c
            carry = c[:, SCAN_BLK - 1:SCAN_BLK]

    return pl.pallas_call(
        body, name="fox_prep",
        out_shape=[_sds((N_HEADS, SEQ), F32), _sds((N_HEADS, SEQ), F32)],
        compiler_params=pltpu.CompilerParams(vmem_limit_bytes=VMEM_LIMIT),
    )(fa_t, b_col)


def _fox_post_bwd(df_t, sg_t):
    nblk = SEQ // SCAN_BLK

    def body(df_ref, sg_ref, dfa_ref, gb_ref):
        row = lax.broadcasted_iota(jnp.int32, (SCAN_BLK, SCAN_BLK), 0)
        col = lax.broadcasted_iota(jnp.int32, (SCAN_BLK, SCAN_BLK), 1)
        lower = (row >= col).astype(BF16)
        carry = jnp.zeros((N_HEADS, 1), F32)
        gb = jnp.zeros((N_HEADS, 1), F32)
        for blk in reversed(range(nblk)):
            sl = pl.ds(blk * SCAN_BLK, SCAN_BLK)
            c = _split_dot(df_ref[:, sl], lower) + carry
            carry = c[:, 0:1]
            dfa = c * sg_ref[:, sl]
            dfa_ref[:, sl] = dfa
            gb = gb + jnp.sum(dfa, axis=1, keepdims=True)
        gb_ref[...] = gb

    return pl.pallas_call(
        body, name="fox_post_bwd",
        out_shape=[_sds((N_HEADS, SEQ), F32), _sds((N_HEADS, 1), F32)],
        compiler_params=pltpu.CompilerParams(vmem_limit_bytes=VMEM_LIMIT),
    )(df_t, sg_t)


FOX_T = 512
NT_DIMS = (((1,), (1,)), ((), ()))
TN_DIMS = (((0,), (0,)), ((), ()))


def _head(ref_or_val, h):
    return ref_or_val[:, h * HEAD_DIM:(h + 1) * HEAD_DIM]


def _split3(v):
    hi = v.astype(BF16).astype(F32)
    r1 = v - hi
    mid = r1.astype(BF16).astype(F32)
    return hi, mid, (r1 - mid).astype(BF16).astype(F32)


ONE_LANE = 3 * N_HEADS


def _pack_terms(v, with_one):
    hi, mid, lo = _split3(v)
    t = hi + pltpu.roll(mid, N_HEADS, 1) + pltpu.roll(lo, 2 * N_HEADS, 1)
    if with_one:
        t = t + (lax.broadcasted_iota(jnp.int32, v.shape, 1) == ONE_LANE).astype(F32)
    return t.astype(BF16)


def _aux_matrices():
    to_q = np.zeros((LANE, N_HEADS * 2 * HEAD_DIM), np.float32)
    to_k = np.zeros_like(to_q)
    for h in range(N_HEADS):
        base = h * 2 * HEAD_DIM + HEAD_DIM
        for s in range(3):
            to_q[s * N_HEADS + h, base + s] = 1.0
            to_q[ONE_LANE, base + 3 + s] = 1.0
            to_k[ONE_LANE, base + s] = 1.0
            to_k[s * N_HEADS + h, base + 3 + s] = -1.0
    return jnp.asarray(to_q, BF16), jnp.asarray(to_k, BF16)


def _head_sums():
    total = np.zeros((N_HEADS * HEAD_DIM, LANE), np.float32)
    first = np.zeros_like(total)
    for h in range(N_HEADS):
        total[h * HEAD_DIM:(h + 1) * HEAD_DIM, h] = 1.0
        first[h * HEAD_DIM, h] = 1.0
    return jnp.asarray(total, BF16), jnp.asarray(first, BF16)


SLOT = 2 * HEAD_DIM
N_SPLIT = 3
FOX_FWD_HEADS = 8
FOX_BWD_HEADS = 4


def _slot(ref, h):
    return ref[:, h * SLOT:(h + 1) * SLOT]


def _fox_pack_fwd(zm, f_cols, *, tm=512):
    def body(q_ref, k_ref, v_ref, f_ref, tq_ref, tk_ref, qs_ref, ks_ref, vs_ref):
        ones = jnp.ones((tm, HEAD_DIM), BF16)
        terms = _pack_terms(f_ref[...], True)
        q_aux = jnp.dot(terms, tq_ref[...], preferred_element_type=F32).astype(BF16)
        k_aux = jnp.dot(terms, tk_ref[...], preferred_element_type=F32).astype(BF16)
        for h in range(N_HEADS):
            aux = slice(h * SLOT + HEAD_DIM, (h + 1) * SLOT)
            qs_ref[:, h * SLOT:(h + 1) * SLOT] = jnp.concatenate(
                [(_head(q_ref, h).astype(F32) * SCALE).astype(BF16), q_aux[:, aux]], axis=1)
            ks_ref[:, h * SLOT:(h + 1) * SLOT] = jnp.concatenate([_head(k_ref, h), k_aux[:, aux]], axis=1)
            vs_ref[:, h * SLOT:(h + 1) * SLOT] = jnp.concatenate([_head(v_ref, h), ones], axis=1)

    col = lambda b: pl.BlockSpec((tm, ATT_W), lambda i: (i, b))
    wide = pl.BlockSpec((tm, N_HEADS * SLOT), lambda i: (i, 0))
    const = pl.BlockSpec((LANE, N_HEADS * SLOT), lambda i: (0, 0))
    return pl.pallas_call(
        body, name="fox_pack_fwd", grid=(SEQ // tm,),
        in_specs=[col(0), col(1), col(2), pl.BlockSpec((tm, LANE), lambda i: (i, 0)), const, const],
        out_specs=[wide] * 3, out_shape=[_sds((SEQ, N_HEADS * SLOT), BF16)] * 3,
        compiler_params=_params("parallel"),
    )(zm, zm, zm, f_cols, *_aux_matrices())


def _fox_pack_bwd(zm, f_cols, lse, o, do, *, tm=512):
    def body(q_ref, f_ref, lse_ref, o_ref, do_ref, tq_ref, total_ref, first_ref, qs_ref, ds_ref):
        delta = _split_dot(o_ref[...].astype(F32) * do_ref[...].astype(F32), total_ref[...])
        lse_h = _split_dot(lse_ref[...], first_ref[...])
        q_aux = jnp.dot(_pack_terms(f_ref[...] - lse_h, True), tq_ref[...], preferred_element_type=F32).astype(BF16)
        d_aux = jnp.dot(_pack_terms(-delta, False), tq_ref[...], preferred_element_type=F32).astype(BF16)
        for h in range(N_HEADS):
            aux = slice(h * SLOT + HEAD_DIM, (h + 1) * SLOT)
            qs_ref[:, h * SLOT:(h + 1) * SLOT] = jnp.concatenate(
                [(_head(q_ref, h).astype(F32) * SCALE).astype(BF16), q_aux[:, aux]], axis=1)
            ds_ref[:, h * SLOT:(h + 1) * SLOT] = jnp.concatenate([_head(do_ref, h), d_aux[:, aux]], axis=1)

    row = pl.BlockSpec((tm, ATT_W), lambda i: (i, 0))
    wide = pl.BlockSpec((tm, N_HEADS * SLOT), lambda i: (i, 0))
    const = lambda r, c: pl.BlockSpec((r, c), lambda i: (0, 0))
    return pl.pallas_call(
        body, name="fox_pack_bwd", grid=(SEQ // tm,),
        in_specs=[row, pl.BlockSpec((tm, LANE), lambda i: (i, 0)), row, row, row,
                  const(LANE, N_HEADS * SLOT), const(ATT_W, LANE), const(ATT_W, LANE)],
        out_specs=[wide] * 2, out_shape=[_sds((SEQ, N_HEADS * SLOT), BF16)] * 2,
        compiler_params=_params("parallel"),
    )(zm, f_cols, lse, o, do, _aux_matrices()[0], *_head_sums())


def _causal_pairs(key_major):
    nb = SEQ // FOX_T
    if key_major:
        pairs = [(i, j) for j in range(nb) for i in range(j, nb)]
    else:
        pairs = [(i, j) for i in range(nb) for j in range(i + 1)]
    return (jnp.array([p[0] for p in pairs], jnp.int32), jnp.array([p[1] for p in pairs], jnp.int32), len(pairs))


def _diag_mask():
    row = lax.broadcasted_iota(jnp.int32, (FOX_T, FOX_T), 0)
    col = lax.broadcasted_iota(jnp.int32, (FOX_T, FOX_T), 1)
    return col <= row


def _fox_fwd(q_slots, k_slots, v_slots):
    i_tab, j_tab, n_pairs = _causal_pairs(False)

    def body(i_tab, j_tab, q_ref, k_ref, v_ref, o_ref, lse_ref, m_s, acc_s):
        t = pl.program_id(1)
        i, j = i_tab[t], j_tab[t]

        @pl.when(j == 0)
        def _():
            m_s[...] = jnp.full_like(m_s, NEG_INF)
            acc_s[...] = jnp.zeros_like(acc_s)

        def step(masked):
            scores = [lax.dot_general(_slot(q_ref, h), _slot(k_ref, h), NT_DIMS, preferred_element_type=F32)
                      for h in range(FOX_FWD_HEADS)]
            probs, alphas = [], []
            for h in range(FOX_FWD_HEADS):
                s = jnp.where(_diag_mask(), scores[h], NEG_INF) if masked else scores[h]
                m_prev = m_s[h]
                m_new = jnp.maximum(m_prev, jnp.max(s, axis=-1, keepdims=True))
                probs.append(jnp.exp(s - jnp.tile(m_new, (1, FOX_T // LANE))).astype(BF16))
                alphas.append(jnp.exp(m_prev - m_new))
                m_s[h] = m_new
            for h in range(FOX_FWD_HEADS):
                acc_s[h] = alphas[h] * acc_s[h] + jnp.dot(probs[h], _slot(v_ref, h), preferred_element_type=F32)

        @pl.when(j < i)
        def _():
            step(False)

        @pl.when(j == i)
        def _():
            step(True)
            outs, lses = [], []
            for h in range(FOX_FWD_HEADS):
                acc = acc_s[h]
                l = acc[:, HEAD_DIM:]
                outs.append(acc[:, :HEAD_DIM] / l)
                lses.append(m_s[h][:, :HEAD_DIM] + jnp.log(l))
            o_ref[...] = jnp.concatenate(outs, axis=1).astype(o_ref.dtype)
            lse_ref[...] = jnp.concatenate(lses, axis=1)

    qspec = pl.BlockSpec((FOX_T, FOX_FWD_HEADS * SLOT), lambda p, t, it, jt: (it[t], p))
    kspec = pl.BlockSpec((FOX_T, FOX_FWD_HEADS * SLOT), lambda p, t, it, jt: (jt[t], p))
    ospec = pl.BlockSpec((FOX_T, FOX_FWD_HEADS * HEAD_DIM), lambda p, t, it, jt: (it[t], p))
    return pl.pallas_call(
        body, name="fox_fwd",
        grid_spec=pltpu.PrefetchScalarGridSpec(
            num_scalar_prefetch=2, grid=(N_HEADS // FOX_FWD_HEADS, n_pairs),
            in_specs=[qspec, kspec, kspec], out_specs=[ospec, ospec],
            scratch_shapes=[pltpu.VMEM((FOX_FWD_HEADS, FOX_T, LANE), F32),
                            pltpu.VMEM((FOX_FWD_HEADS, FOX_T, SLOT), F32)]),
        out_shape=[_sds((SEQ, ATT_W), BF16), _sds((SEQ, ATT_W), F32)],
        compiler_params=_params("parallel", "arbitrary"),
    )(i_tab, j_tab, q_slots, k_slots, v_slots)


def _fox_bwd(q_slots, k_slots, v_slots, do_slots):
    i_tab, j_tab, n_pairs = _causal_pairs(True)

    def body(i_tab, j_tab, q_ref, k_ref, v_ref, do_ref, dq_ref, dk_ref, dv_ref):
        t = pl.program_id(1)
        i, j = i_tab[t], j_tab[t]

        @pl.when(t == 0)
        def _():
            dq_ref[...] = jnp.zeros_like(dq_ref)

        @pl.when(i == j)
        def _():
            dk_ref[...] = jnp.zeros_like(dk_ref)
            dv_ref[...] = jnp.zeros_like(dv_ref)

        def step(masked):
            rows = pl.ds(pl.multiple_of(i * FOX_T, FOX_T), FOX_T)
            heads = range(FOX_BWD_HEADS)
            scores = [lax.dot_general(_slot(q_ref, h), _slot(k_ref, h), NT_DIMS, preferred_element_type=F32)
                      for h in heads]
            dps = [lax.dot_general(_slot(do_ref, h), _slot(v_ref, h), NT_DIMS, preferred_element_type=F32)
                   for h in heads]
            ps, dss = [], []
            for h in heads:
                p = jnp.exp(scores[h])
                if masked:
                    p = jnp.where(_diag_mask(), p, 0.0)
                ps.append(p.astype(BF16))
                dss.append((p * dps[h]).astype(BF16))
            for h in heads:
                cols = slice(h * SLOT, (h + 1) * SLOT)
                dv_ref[:, cols] += lax.dot_general(ps[h], _slot(do_ref, h), TN_DIMS, preferred_element_type=F32)
                dk_ref[:, cols] += lax.dot_general(dss[h], _slot(q_ref, h), TN_DIMS, preferred_element_type=F32)
                dq_ref[rows, cols] += jnp.dot(dss[h], _slot(k_ref, h), preferred_element_type=F32)

        @pl.when(i > j)
        def _():
            step(False)

        @pl.when(i == j)
        def _():
            step(True)

    qspec = pl.BlockSpec((FOX_T, FOX_BWD_HEADS * SLOT), lambda p, t, it, jt: (it[t], p))
    kspec = pl.BlockSpec((FOX_T, FOX_BWD_HEADS * SLOT), lambda p, t, it, jt: (jt[t], p))
    return pl.pallas_call(
        body, name="fox_bwd",
        grid_spec=pltpu.PrefetchScalarGridSpec(
            num_scalar_prefetch=2, grid=(N_HEADS // FOX_BWD_HEADS, n_pairs),
            in_specs=[qspec, kspec, kspec, qspec],
            out_specs=[pl.BlockSpec((SEQ, FOX_BWD_HEADS * SLOT), lambda p, t, it, jt: (0, p)), kspec, kspec]),
        out_shape=[_sds((SEQ, N_HEADS * SLOT), F32)] * 3,
        compiler_params=_params("arbitrary", "arbitrary"),
    )(i_tab, j_tab, q_slots, k_slots, v_slots, do_slots)


def _fox_unpack(dq_slots, dk_slots, dv_slots, *, tm=512):
    def body(dq_ref, dk_ref, dv_ref, o_ref, df_ref):
        lane = lax.broadcasted_iota(jnp.int32, (tm, LANE), 1)
        df = jnp.zeros((tm, LANE), F32)
        for h in range(N_HEADS):
            lo = h * SLOT
            for part, (ref, mult) in enumerate(((dq_ref, SCALE), (dk_ref, 1.0), (dv_ref, 1.0))):
                o_ref[:, part * ATT_W + h * HEAD_DIM:part * ATT_W + (h + 1) * HEAD_DIM] = (
                    ref[:, lo:lo + HEAD_DIM] * mult).astype(o_ref.dtype)
            rows = dq_ref[:, lo + HEAD_DIM:lo + HEAD_DIM + 1]
            cols = dk_ref[:, lo + HEAD_DIM + N_SPLIT:lo + HEAD_DIM + N_SPLIT + 1]
            df = jnp.where(lane == h, rows - cols, df)
        df_ref[...] = df

    wide = pl.BlockSpec((tm, N_HEADS * SLOT), lambda i: (i, 0))
    return pl.pallas_call(
        body, name="fox_unpack", grid=(SEQ // tm,), in_specs=[wide] * 3,
        out_specs=[pl.BlockSpec((tm, 3 * ATT_W), lambda i: (i, 0)), pl.BlockSpec((tm, LANE), lambda i: (i, 0))],
        out_shape=[_sds((SEQ, 3 * ATT_W), BF16), _sds((SEQ, LANE), F32)],
        compiler_params=_params("parallel"),
    )(dq_slots, dk_slots, dv_slots)


def _attn_delta(o, do, *, name, tm=512):
    def body(o_ref, do_ref, d_ref):
        prod = o_ref[...].astype(F32) * do_ref[...].astype(F32)
        lane = lax.broadcasted_iota(jnp.int32, (tm, LANE), 1)
        out = jnp.zeros((tm, LANE), F32)
        for h in range(N_HEADS):
            out = jnp.where(lane == h, jnp.sum(_head(prod, h), axis=1, keepdims=True), out)
        d_ref[...] = out

    row = pl.BlockSpec((tm, ATT_W), lambda i: (i, 0))
    return pl.pallas_call(
        body, name=name, grid=(SEQ // tm,), in_specs=[row, row],
        out_specs=pl.BlockSpec((tm, LANE), lambda i: (i, 0)), out_shape=_sds((SEQ, LANE), F32),
        compiler_params=_params("parallel"),
    )(o, do)


def _rope_tables():
    half = ROPE_DIM // 2
    inv_freq = np.float32(ROPE_THETA) ** (-np.arange(half, dtype=np.float32) * np.float32(2.0) / np.float32(ROPE_DIM))
    ang = np.arange(SEQ, dtype=np.float32)[:, None] * inv_freq.astype(np.float32)[None, :]
    cos, sin = jnp.asarray(np.cos(ang).astype(np.float32)), jnp.asarray(np.sin(ang).astype(np.float32))
    ones = jnp.ones((SEQ, HEAD_DIM - ROPE_DIM), F32)
    zeros = jnp.zeros((SEQ, HEAD_DIM - ROPE_DIM), F32)
    zh = jnp.zeros((SEQ, half), F32)
    c_tab = jnp.concatenate([cos, cos, ones], axis=1)
    a_tab = jnp.concatenate([-sin, zh, zeros], axis=1)
    b_tab = jnp.concatenate([zh, sin, zeros], axis=1)
    two = lambda t: jnp.concatenate([t, t], axis=1)
    return two(c_tab), two(a_tab), two(b_tab)


def _rotate(x, c_tab, a_tab, b_tab):
    return x * c_tab + pltpu.roll(x, LANE - ROPE_DIM // 2, 1) * a_tab + pltpu.roll(x, ROPE_DIM // 2, 1) * b_tab


def _rope_fwd(zm, tabs, *, tm=512):
    def body(q_ref, k_ref, v_ref, c_ref, a_ref, b_ref, o_ref):
        for part, (x_ref, mult) in enumerate(((q_ref, SCALE), (k_ref, 1.0))):
            for cc in range(ATT_W // LANE):
                sl = slice(cc * LANE, (cc + 1) * LANE)
                rot = _rotate(x_ref[:, sl].astype(F32), c_ref[...], a_ref[...], b_ref[...])
                o_ref[:, part * ATT_W + cc * LANE:part * ATT_W + (cc + 1) * LANE] = (rot * mult).astype(o_ref.dtype)
        o_ref[:, 2 * ATT_W:] = v_ref[...]

    tab = pl.BlockSpec((tm, LANE), lambda i: (i, 0))
    col = lambda b: pl.BlockSpec((tm, ATT_W), lambda i: (i, b))
    return pl.pallas_call(
        body, name="rope_fwd", grid=(SEQ // tm,),
        in_specs=[col(3), col(4), col(5), tab, tab, tab],
        out_specs=pl.BlockSpec((tm, 3 * ATT_W), lambda i: (i, 0)),
        out_shape=_sds((SEQ, 3 * ATT_W), BF16),
        compiler_params=_params("parallel"),
    )(zm, zm, zm, *tabs)


def _dil_grad_combine(dqs, dks, dvs, tabs, *, tm=256):
    def body(*refs):
        q_refs, k_refs, v_refs = refs[0:3], refs[3:6], refs[6:9]
        c_ref, a_ref, b_ref, o_ref = refs[9:]
        total = lambda rs, sl: rs[0][:, sl].astype(F32) + rs[1][:, sl].astype(F32) + rs[2][:, sl].astype(F32)
        for cc in range(ATT_W // LANE):
            sl = slice(cc * LANE, (cc + 1) * LANE)
            for part, rs in enumerate((q_refs, k_refs)):
                o_ref[:, part * ATT_W + cc * LANE:part * ATT_W + (cc + 1) * LANE] = _rotate(
                    total(rs, sl), c_ref[...], -a_ref[...], -b_ref[...]).astype(o_ref.dtype)
            o_ref[:, 2 * ATT_W + cc * LANE:2 * ATT_W + (cc + 1) * LANE] = total(v_refs, sl).astype(o_ref.dtype)

    row = pl.BlockSpec((tm, ATT_W), lambda i: (i, 0))
    tab = pl.BlockSpec((tm, LANE), lambda i: (i, 0))
    return pl.pallas_call(
        body, name="dil_grad_combine", grid=(SEQ // tm,),
        in_specs=[row] * 9 + [tab] * 3,
        out_specs=pl.BlockSpec((tm, 3 * ATT_W), lambda i: (i, 0)),
        out_shape=_sds((SEQ, 3 * ATT_W), BF16),
        compiler_params=_params("parallel"),
    )(*dqs, *dks, *dvs, *tabs)


def _dil_valid(n):
    qi = lax.broadcasted_iota(jnp.int32, (DIL_BLK, 2 * DIL_BLK), 0)
    ki = lax.broadcasted_iota(jnp.int32, (DIL_BLK, 2 * DIL_BLK), 1)
    dist = qi + DIL_BLK - ki
    return (dist >= 0) & (dist <= DIL_BLK) & ((n > 0) | (ki >= DIL_BLK))


def _dil_fwd(qkv, d):
    length = SEQ // d
    nb = length // DIL_BLK
    qkv_v = qkv.reshape(length, d * 3 * ATT_W)

    def body(q_ref, kp_ref, kc_ref, vp_ref, vc_ref, o_ref, lse_ref):
        n = pl.program_id(1)
        ok = _dil_valid(n)
        lane = lax.broadcasted_iota(jnp.int32, (DIL_BLK, LANE), 1)
        lse_all = jnp.zeros((DIL_BLK, LANE), F32)
        heads = range(N_HEADS)
        scores = [lax.dot_general(_head(q_ref, h), jnp.concatenate([_head(kp_ref, h), _head(kc_ref, h)], axis=0),
                                  NT_DIMS, preferred_element_type=F32) for h in heads]
        probs, inv_l = [], []
        for h in heads:
            s = jnp.where(ok, scores[h], NEG_INF)
            m = jnp.max(s, axis=-1, keepdims=True)
            p = jnp.exp(s - m)
            l = jnp.sum(p, axis=-1, keepdims=True)
            probs.append(p.astype(BF16))
            inv_l.append(1.0 / l)
            lse_all = jnp.where(lane == h, m + jnp.log(l), lse_all)
        outs = [jnp.dot(probs[h], jnp.concatenate([_head(vp_ref, h), _head(vc_ref, h)], axis=0),
                        preferred_element_type=F32) * inv_l[h] for h in heads]
        o_ref[...] = jnp.concatenate(outs, axis=1).astype(o_ref.dtype)
        lse_ref[...] = lse_all

    blk = lambda f: pl.BlockSpec((DIL_BLK, ATT_W), f)
    prev = lambda n: jnp.maximum(n - 1, 0)
    o, lse = pl.pallas_call(
        body, name=f"dil_fwd_d{d}", grid=(d, nb),
        in_specs=[blk(lambda r, n: (n, 3 * r)),
                  blk(lambda r, n: (prev(n), 3 * r + 1)), blk(lambda r, n: (n, 3 * r + 1)),
                  blk(lambda r, n: (prev(n), 3 * r + 2)), blk(lambda r, n: (n, 3 * r + 2))],
        out_specs=[blk(lambda r, n: (n, r)), pl.BlockSpec((DIL_BLK, LANE), lambda r, n: (n, r))],
        out_shape=[_sds((length, d * ATT_W), BF16), _sds((length, d * LANE), F32)],
        compiler_params=_params("parallel", "arbitrary"),
    )(qkv_v, qkv_v, qkv_v, qkv_v, qkv_v)
    return o.reshape(SEQ, ATT_W), lse.reshape(SEQ, LANE)


def _dil_merge(os_, lses, *, tm=512):
    def body(o0, o1, o2, l0, l1, l2, y_ref, lse_ref):
        ls = [l0[...], l1[...], l2[...]]
        m = jnp.maximum(jnp.maximum(ls[0], ls[1]), ls[2])
        es = [jnp.exp(l - m) for l in ls]
        tot = es[0] + es[1] + es[2]
        lse_ref[...] = m + jnp.log(tot)
        alphas = [e / tot for e in es]
        outs = []
        for h in range(N_HEADS):
            acc = None
            for g, o_ref in enumerate((o0, o1, o2)):
                term = alphas[g][:, h:h + 1] * _head(o_ref, h).astype(F32)
                acc = term if acc is None else acc + term
            outs.append(acc)
        y_ref[...] = jnp.concatenate(outs, axis=1).astype(y_ref.dtype)

    row = pl.BlockSpec((tm, ATT_W), lambda i: (i, 0))
    vec = pl.BlockSpec((tm, LANE), lambda i: (i, 0))
    return pl.pallas_call(
        body, name="dil_merge", grid=(SEQ // tm,),
        in_specs=[row] * 3 + [vec] * 3, out_specs=[row, vec],
        out_shape=[_sds((SEQ, ATT_W), BF16), _sds((SEQ, LANE), F32)],
        compiler_params=_params("parallel"),
    )(*os_, *lses)


def _dil_bwd(qkv, lse, delta, do, d):
    length = SEQ // d
    nb = length // DIL_BLK
    qkv_v = qkv.reshape(length, d * 3 * ATT_W)
    lse_v, dl_v, do_v = lse.reshape(length, d * LANE), delta.reshape(length, d * LANE), do.reshape(length, d * ATT_W)

    def body(q_ref, kp_ref, kc_ref, vp_ref, vc_ref, lse_ref, dl_ref, do_ref,
             dq_ref, dk_ref, dv_ref, ck_s, cv_s):
        n = pl.program_id(1)

        @pl.when(n == 0)
        def _():
            ck_s[...] = jnp.zeros_like(ck_s)
            cv_s[...] = jnp.zeros_like(cv_s)

        @pl.when(n < nb)
        def _():
            ok = _dil_valid(n)
            heads = range(N_HEADS)
            kks = [jnp.concatenate([_head(kp_ref, h), _head(kc_ref, h)], axis=0) for h in heads]
            scores = [lax.dot_general(_head(q_ref, h), kks[h], NT_DIMS, preferred_element_type=F32) for h in heads]
            dps = [lax.dot_general(_head(do_ref, h), jnp.concatenate([_head(vp_ref, h), _head(vc_ref, h)], axis=0),
                                   NT_DIMS, preferred_element_type=F32) for h in heads]
            ps, dss = [], []
            for h in heads:
                p = jnp.where(ok, jnp.exp(scores[h] - lse_ref[:, h:h + 1]), 0.0)
                ps.append(p.astype(BF16))
                dss.append((p * (dps[h] - dl_ref[:, h:h + 1])).astype(BF16))
            dqs = [jnp.dot(dss[h], kks[h], preferred_element_type=F32) * SCALE for h in heads]
            dkks = [lax.dot_general(dss[h], _head(q_ref, h), TN_DIMS, preferred_element_type=F32) for h in heads]
            dvvs = [lax.dot_general(ps[h], _head(do_ref, h), TN_DIMS, preferred_element_type=F32) for h in heads]
            dq_ref[...] = jnp.concatenate(dqs, axis=1).astype(dq_ref.dtype)
            dk_ref[...] = (ck_s[...] + jnp.concatenate([t[:DIL_BLK] for t in dkks], axis=1)).astype(dk_ref.dtype)
            dv_ref[...] = (cv_s[...] + jnp.concatenate([t[:DIL_BLK] for t in dvvs], axis=1)).astype(dv_ref.dtype)
            ck_s[...] = jnp.concatenate([t[DIL_BLK:] for t in dkks], axis=1)
            cv_s[...] = jnp.concatenate([t[DIL_BLK:] for t in dvvs], axis=1)

        @pl.when(n == nb)
        def _():
            dk_ref[...] = ck_s[...].astype(dk_ref.dtype)
            dv_ref[...] = cv_s[...].astype(dv_ref.dtype)

    blk = lambda f: pl.BlockSpec((DIL_BLK, ATT_W), f)
    vec = lambda f: pl.BlockSpec((DIL_BLK, LANE), f)
    cur = lambda n: jnp.minimum(n, nb - 1)
    prev = lambda n: jnp.maximum(cur(n) - 1, 0)
    back = lambda n: jnp.maximum(n - 1, 0)
    outs = pl.pallas_call(
        body, name=f"dil_bwd_d{d}", grid=(d, nb + 1),
        in_specs=[blk(lambda r, n: (cur(n), 3 * r)),
                  blk(lambda r, n: (prev(n), 3 * r + 1)), blk(lambda r, n: (cur(n), 3 * r + 1)),
                  blk(lambda r, n: (prev(n), 3 * r + 2)), blk(lambda r, n: (cur(n), 3 * r + 2)),
                  vec(lambda r, n: (cur(n), r)), vec(lambda r, n: (cur(n), r)),
                  blk(lambda r, n: (cur(n), r))],
        out_specs=[blk(lambda r, n: (cur(n), r)), blk(lambda r, n: (back(n), r)), blk(lambda r, n: (back(n), r))],
        out_shape=[_sds((length, d * ATT_W), BF16)] * 3,
        scratch_shapes=[pltpu.VMEM((DIL_BLK, ATT_W), F32), pltpu.VMEM((DIL_BLK, ATT_W), F32)],
        compiler_params=_params("arbitrary", "arbitrary"),
    )(qkv_v, qkv_v, qkv_v, qkv_v, qkv_v, lse_v, dl_v, do_v)
    return [t.reshape(SEQ, ATT_W) for t in outs]


def _sigmoid(x):
    return 1.0 / (1.0 + jnp.exp(-x))


def _mix_fwd(ya, yb, w_oa, w_ob, zm, *, tm=512):
    def body(ya_ref, yb_ref, wa_ref, wb_ref, ga_ref, gb_ref, pa_ref, pb_ref, mix_ref):
        pa = jnp.dot(ya_ref[...], wa_ref[...], preferred_element_type=F32)
        pb = jnp.dot(yb_ref[...], wb_ref[...], preferred_element_type=F32)
        pa_ref[...] = pa.astype(pa_ref.dtype)
        pb_ref[...] = pb.astype(pb_ref.dtype)
        mix_ref[...] = (_sigmoid(ga_ref[...].astype(F32)) * pa + _sigmoid(gb_ref[...].astype(F32)) * pb
                        ).astype(mix_ref.dtype)

    row = pl.BlockSpec((tm, ATT_W), lambda i: (i, 0))
    wsp = pl.BlockSpec((ATT_W, D_MODEL), lambda i: (0, 0))
    wide = pl.BlockSpec((tm, D_MODEL), lambda i: (i, 0))
    return pl.pallas_call(
        body, name="mix_fwd", grid=(SEQ // tm,),
        in_specs=[row, row, wsp, wsp, pl.BlockSpec((tm, D_MODEL), lambda i: (i, 3)),
                  pl.BlockSpec((tm, D_MODEL), lambda i: (i, 4))],
        out_specs=[wide] * 3, out_shape=[_sds((SEQ, D_MODEL), BF16)] * 3,
        compiler_params=_params("parallel"),
    )(ya, yb, w_oa, w_ob, zm, zm)


def _gate_bwd(dmix, zm, pa, pb, *, tm=512):
    def body(dm_ref, ga_ref, gb_ref, pa_ref, pb_ref, dpa_ref, dpb_ref, dg_ref):
        dm = dm_ref[...].astype(F32)
        sa, sb = _sigmoid(ga_ref[...].astype(F32)), _sigmoid(gb_ref[...].astype(F32))
        dpa_ref[...] = (dm * sa).astype(dpa_ref.dtype)
        dpb_ref[...] = (dm * sb).astype(dpb_ref.dtype)
        dg_ref[:, :D_MODEL] = (dm * pa_ref[...].astype(F32) * sa * (1.0 - sa)).astype(dg_ref.dtype)
        dg_ref[:, D_MODEL:] = (dm * pb_ref[...].astype(F32) * sb * (1.0 - sb)).astype(dg_ref.dtype)

    wide = pl.BlockSpec((tm, D_MODEL), lambda i: (i, 0))
    return pl.pallas_call(
        body, name="gate_bwd", grid=(SEQ // tm,),
        in_specs=[wide, pl.BlockSpec((tm, D_MODEL), lambda i: (i, 3)), pl.BlockSpec((tm, D_MODEL), lambda i: (i, 4)),
                  wide, wide],
        out_specs=[wide, wide, pl.BlockSpec((tm, 2 * D_MODEL), lambda i: (i, 0))],
        out_shape=[_sds((SEQ, D_MODEL), BF16), _sds((SEQ, D_MODEL), BF16), _sds((SEQ, 2 * D_MODEL), BF16)],
        compiler_params=_params("parallel"),
    )(dmix, zm, zm, pa, pb)


def _out_fwd(mixed, w_out, x, g_post, g_pre, *, tm=512):
    def body(m_ref, w_ref, x_ref, gp_ref, gn_ref, y_ref, x2_ref, h_ref):
        y = jnp.dot(m_ref[...], w_ref[...], preferred_element_type=F32)
        y_ref[...] = y
        r = lax.rsqrt(jnp.mean(y * y, axis=-1, keepdims=True) + RMS_EPS)
        x2 = x_ref[...] + y * r * gp_ref[...]
        x2_ref[...] = x2
        r2 = lax.rsqrt(jnp.mean(x2 * x2, axis=-1, keepdims=True) + RMS_EPS)
        h_ref[...] = (x2 * r2 * gn_ref[...]).astype(h_ref.dtype)

    row = pl.BlockSpec((tm, D_MODEL), lambda i: (i, 0))
    vec = pl.BlockSpec((1, D_MODEL), lambda i: (0, 0))
    return pl.pallas_call(
        body, name="out_fwd", grid=(SEQ // tm,),
        in_specs=[row, pl.BlockSpec((D_MODEL, D_MODEL), lambda i: (0, 0)), row, vec, vec],
        out_specs=[row] * 3,
        out_shape=[_sds((SEQ, D_MODEL), F32), _sds((SEQ, D_MODEL), F32), _sds((SEQ, D_MODEL), BF16)],
        compiler_params=_params("parallel"),
    )(mixed, w_out, x, g_post, g_pre)


FFN_TM = 512
FFN_HALF = 256
FFN_TN = 2 * FFN_HALF
FFN_NJ = D_FF // FFN_HALF
FFN_GROUP = 2 * SUBLANE


def _ffn_interleave(t):
    lead = t.shape[:-1]
    return jnp.swapaxes(t.reshape(*lead, 2, FFN_NJ, FFN_HALF), -3, -2).reshape(*lead, 2 * D_FF)


def _ffn_deinterleave(t):
    lead = t.shape[:-1]
    return jnp.swapaxes(t.reshape(*lead, FFN_NJ, 2, FFN_HALF), -3, -2).reshape(*lead, 2 * D_FF)


def _ffn_move_blocks(t, *, interleave, name):
    rows = t.shape[0]
    if interleave:
        src = lambda jb: (0, (jb % 2) * FFN_NJ + jb // 2)
    else:
        src = lambda jb: (0, 2 * (jb % FFN_NJ) + jb // FFN_NJ)

    def body(x_ref, o_ref):
        o_ref[...] = x_ref[...]

    return pl.pallas_call(
        body, name=name, grid=(2 * FFN_NJ,),
        in_specs=[pl.BlockSpec((rows, FFN_HALF), src)],
        out_specs=pl.BlockSpec((rows, FFN_HALF), lambda jb: (0, jb)),
        out_shape=_sds(t.shape, t.dtype),
        compiler_params=_params("parallel"),
    )(t)


def _gelu_parts(a):
    c = math.sqrt(2.0 / math.pi)
    a2 = a * a
    t = jnp.tanh((c * a) * (1.0 + 0.044715 * a2))
    half_a, one_t = 0.5 * a, 1.0 + t
    gelu = half_a * one_t
    dgelu = 0.5 * one_t + half_a * (1.0 - t * t) * (c + (3.0 * 0.044715 * c) * a2)
    return gelu, dgelu


def _row_masks(down):
    row = lax.broadcasted_iota(jnp.int32, (SUBLANE, FFN_TN), 0)
    return (row < 1, row < 2) if down else (row >= SUBLANE - 1, row >= SUBLANE - 2)


def _rolled(x, down):
    return (pltpu.roll(x, 1, 0), pltpu.roll(x, 2, 0)) if down else (
        pltpu.roll(x, SUBLANE - 1, 0), pltpu.roll(x, SUBLANE - 2, 0))


def _shifted(cur_rolled, neighbour_rolled, masks):
    return (jnp.where(masks[0], neighbour_rolled[0], cur_rolled[0]),
            jnp.where(masks[1], neighbour_rolled[1], cur_rolled[1]))


def _conv_consts(w_ref, b_ref):
    shape = (SUBLANE, FFN_TN)
    return [jnp.broadcast_to(w_ref[k:k + 1, :], shape) for k in range(3)] + [jnp.broadcast_to(b_ref[...], shape)]


def _ffn_mid_fwd(u, conv_w, conv_b):
    per = FFN_TM // SUBLANE

    def body(u_ref, h_ref, w_ref, b_ref, m_ref, ab_ref):
        live = (pl.program_id(1) > 0).astype(F32)
        w0, w1, w2, bias = _conv_consts(w_ref, b_ref)
        masks = _row_masks(True)

        def group(g, above):
            rows = pl.ds(pl.multiple_of(g * FFN_GROUP, FFN_GROUP), FFN_GROUP)
            x = u_ref[rows, :].astype(F32)
            convs = []
            for c in range(2):
                cur = x[c * SUBLANE:(c + 1) * SUBLANE]
                cur_rolled = _rolled(cur, True)
                s1, s2 = _shifted(cur_rolled, above, masks)
                convs.append(w0 * s2 + w1 * s1 + w2 * cur + bias)
                above = cur_rolled
            y = jnp.concatenate(convs, axis=0)
            ab_ref[rows, :] = y.astype(ab_ref.dtype)
            m_ref[rows, :] = (_gelu_parts(y[:, :FFN_HALF])[0] * y[:, FFN_HALF:]).astype(m_ref.dtype)
            return above

        lax.fori_loop(0, FFN_TM // (2 * FFN_GROUP), lambda g2, carry: group(2 * g2 + 1, group(2 * g2, carry)),
                      _rolled(h_ref[...].astype(F32) * live, True))

    blk = pl.BlockSpec((FFN_TM, FFN_TN), lambda j, i: (i, j))
    return pl.pallas_call(
        body, name="ffn_mid_fwd", grid=(FFN_NJ, SEQ // FFN_TM),
        in_specs=[blk, pl.BlockSpec((SUBLANE, FFN_TN), lambda j, i: (jnp.maximum(i * per - 1, 0), j)),
                  pl.BlockSpec((3, FFN_TN), lambda j, i: (0, j)), pl.BlockSpec((1, FFN_TN), lambda j, i: (0, j))],
        out_specs=[pl.BlockSpec((FFN_TM, FFN_HALF), lambda j, i: (i, j)), blk],
        out_shape=[_sds((SEQ, D_FF), BF16), _sds((SEQ, 2 * D_FF), BF16)],
        compiler_params=_params("parallel", "arbitrary"),
    )(u, u, conv_w, conv_b)


def _ffn_mid_bwd(dm, u, ab, conv_w):
    nrow = SEQ // FFN_TM
    n_groups = FFN_TM // FFN_GROUP

    def body(dm_ref, u_ref, ab_ref, w_ref, du_ref, gw_ref, gb_ref, c_s):
        @pl.when(pl.program_id(1) == 0)
        def _():
            c_s[...] = jnp.zeros_like(c_s)
            gw_ref[...] = jnp.zeros_like(gw_ref)
            gb_ref[...] = jnp.zeros_like(gb_ref)

        taps = [jnp.broadcast_to(w_ref[k:k + 1, :], (SUBLANE, FFN_TN)) for k in range(3)]
        masks = _row_masks(False)

        def group(t, carry):
            below, acc = carry
            rows = pl.ds(pl.multiple_of((n_groups - 1 - t) * FFN_GROUP, FFN_GROUP), FFN_GROUP)
            x, y, dmv = u_ref[rows, :].astype(F32), ab_ref[rows, :].astype(F32), dm_ref[rows, :].astype(F32)
            gelu, dgelu = _gelu_parts(y[:, :FFN_HALF])
            d = jnp.concatenate([dmv * y[:, FFN_HALF:] * dgelu, dmv * gelu], axis=1)
            acc, pre = list(acc), [None, None]
            for c in (1, 0):
                sl = slice(c * SUBLANE, (c + 1) * SUBLANE)
                cur, xs = d[sl], x[sl]
                cur_rolled = _rolled(cur, False)
                up1, up2 = _shifted(cur_rolled, below, masks)
                acc = [acc[0] + up2 * xs, acc[1] + up1 * xs, acc[2] + cur * xs, acc[3] + cur]
                pre[c] = taps[2] * cur + taps[1] * up1 + taps[0] * up2
                below = cur_rolled
            du_ref[rows, :] = jnp.concatenate(pre, axis=0).astype(du_ref.dtype)
            return below, tuple(acc)

        zeros = jnp.zeros((SUBLANE, FFN_TN), F32)
        below, acc = lax.fori_loop(0, n_groups // 2, lambda t2, carry: group(2 * t2 + 1, group(2 * t2, carry)),
                                   (_rolled(c_s[...], False), (zeros,) * 4))
        c_s[...] = pltpu.roll(below[0], 1, 0)
        for k in range(3):
            gw_ref[k:k + 1, :] += jnp.sum(acc[k], axis=0, keepdims=True)
        gb_ref[...] += jnp.sum(acc[3], axis=0, keepdims=True)

    blk = pl.BlockSpec((FFN_TM, FFN_TN), lambda j, i: (nrow - 1 - i, j))
    return pl.pallas_call(
        body, name="ffn_mid_bwd", grid=(FFN_NJ, nrow),
        in_specs=[pl.BlockSpec((FFN_TM, FFN_HALF), lambda j, i: (nrow - 1 - i, j)), blk, blk,
                  pl.BlockSpec((3, FFN_TN), lambda j, i: (0, j))],
        out_specs=[blk, pl.BlockSpec((3, FFN_TN), lambda j, i: (0, j)), pl.BlockSpec((1, FFN_TN), lambda j, i: (0, j))],
        out_shape=[_sds((SEQ, 2 * D_FF), BF16), _sds((3, 2 * D_FF), F32), _sds((1, 2 * D_FF), F32)],
        scratch_shapes=[pltpu.VMEM((SUBLANE, FFN_TN), F32)],
        compiler_params=_params("parallel", "arbitrary"),
    )(dm, u, ab, conv_w)


def _down_fwd(m, w_down, x2, g_post, target, *, tm=512):
    def body(m_ref, w_ref, x2_ref, g_ref, t_ref, dout_ref, dy_ref, gg_ref, loss_ref):
        @pl.when(pl.program_id(0) == 0)
        def _():
            gg_ref[...] = jnp.zeros_like(gg_ref)
            loss_ref[...] = jnp.zeros_like(loss_ref)

        y = jnp.dot(m_ref[...], w_ref[...], preferred_element_type=F32)
        r = lax.rsqrt(jnp.mean(y * y, axis=-1, keepdims=True) + RMS_EPS)
        yn = y * r
        diff = (x2_ref[...] + yn * g_ref[...]) - t_ref[...]
        loss_ref[...] += jnp.sum(diff * diff)
        dout = diff * (1.0 / D_MODEL)
        dout_ref[...] = dout
        gg_ref[...] += jnp.sum(dout * yn, axis=0, keepdims=True)
        dn = dout * g_ref[...]
        dy_ref[...] = (r * (dn - yn * jnp.mean(dn * yn, axis=-1, keepdims=True))).astype(dy_ref.dtype)

    row = pl.BlockSpec((tm, D_MODEL), lambda i: (i, 0))
    vec = pl.BlockSpec((1, D_MODEL), lambda i: (0, 0))
    return pl.pallas_call(
        body, name="down_fwd", grid=(SEQ // tm,),
        in_specs=[pl.BlockSpec((tm, D_FF), lambda i: (i, 0)), pl.BlockSpec((D_FF, D_MODEL), lambda i: (0, 0)),
                  row, vec, row],
        out_specs=[row, row, vec, pl.BlockSpec((1, LANE), lambda i: (0, 0))],
        out_shape=[_sds((SEQ, D_MODEL), F32), _sds((SEQ, D_MODEL), BF16), _sds((1, D_MODEL), F32),
                   _sds((1, LANE), F32)],
        compiler_params=_params("arbitrary"),
    )(m, w_down, x2, g_post, target)


def _local_step(x, target, w_main, w_f, b_forget, conv_b, g_pre_mix, g_post_mix, g_pre_ffn, g_post_ffn,
                late_weights, ffn_grads_ready, proj_grads_ready, mixer_grads_ready):
    mm = _matmul
    tabs = _rope_tables()

    h1 = _rms_fwd(x, g_pre_mix, name="rms_pre_mix")
    zm = mm(h1, w_main, out_dtype=BF16, tm=2048, tn=512, tk=1024, name="in_proj")
    zf = mm(h1, w_f, out_dtype=F32, tm=2048, tn=F_PAD, tk=1024, name="in_proj_forget")
    f_row, sg_row = _fox_prep(zf[:, :N_HEADS].T, b_forget.reshape(N_HEADS, 1))
    f_cols = jnp.pad(f_row.T, ((0, 0), (0, LANE - N_HEADS)))
    q_slots, k_slots, v_slots = _fox_pack_fwd(zm, f_cols)
    ya, lse_a = _fox_fwd(q_slots, k_slots, v_slots)
    qkv_d = _rope_fwd(zm, tabs)
    dil = [_dil_fwd(qkv_d, d) for _, d in DIL_PATTERNS]
    yb, lse_b = _dil_merge([o for o, _ in dil], [l for _, l in dil])
    w_oa, w_ob, w_out, w_up, conv_w, w_down = late_weights(yb)
    pa, pb, mixed = _mix_fwd(ya, yb, w_oa, w_ob, zm)
    y1, x2, h2 = _out_fwd(mixed, w_out, x, g_post_mix, g_pre_ffn)
    u = mm(h2, w_up, out_dtype=BF16, tm=2048, tn=512, tk=1024, name="up_proj")
    m, ab = _ffn_mid_fwd(u, conv_w, _ffn_interleave(conv_b))
    dout, dy2, gg_post_ffn, sq_err = _down_fwd(m, w_down, x2, g_post_ffn, target)

    g_w_down = mm(m, dy2, ta=True, out_dtype=BF16, tm=D_FF // 2, tn=1024, tk=2048, name="grad_w_down")
    dm = mm(dy2, w_down, tb=True, out_dtype=BF16, tm=2048, tn=D_FF // 2, tk=1024, name="d_ffn_mid")
    du, g_conv_w, g_conv_b = _ffn_mid_bwd(dm, u, ab, conv_w)
    g_w_up = mm(h2, du, ta=True, out_dtype=BF16, tm=1024, tn=D_FF // 2, tk=2048, name="grad_w_up")
    tok = ffn_grads_ready(dict(w_down=g_w_down, w_up=_ffn_move_blocks(g_w_up, interleave=False, name="grad_w_up_cols"),
                               conv_w=_ffn_deinterleave(g_conv_w)))
    dh2 = mm(du, w_up, tb=True, out_dtype=BF16, tm=512, tn=1024, tk=2 * D_FF, name="d_h2")

    dx2, dy1, gg_pre_ffn, gg_post_mix = _rms_pair_bwd([dh2], x2, g_pre_ffn, dout, y1, g_post_mix + tok)
    g_w_out = mm(mixed, dy1, ta=True, out_dtype=BF16, tm=1024, tn=1024, tk=2048, name="grad_w_out")
    dmix = mm(dy1, w_out, tb=True, out_dtype=BF16, tm=2048, tn=1024, tk=1024, name="d_mixed")
    dpa, dpb, dgates = _gate_bwd(dmix, zm, pa, pb)
    g_w_oa = mm(ya, dpa, ta=True, out_dtype=BF16, tm=512, tn=1024, tk=SEQ, name="grad_w_o_fox")
    g_w_ob = mm(yb, dpb, ta=True, out_dtype=BF16, tm=512, tn=1024, tk=SEQ, name="grad_w_o_dil")
    tok = proj_grads_ready(dict(w_o_fox=g_w_oa, w_o_dil=g_w_ob, w_out=g_w_out))
    dya = mm(dpa, w_oa, tb=True, out_dtype=BF16, tm=2048, tn=512, tk=1024, name="d_y_fox")
    dyb = mm(dpb, w_ob, tb=True, out_dtype=BF16, tm=2048, tn=512, tk=1024, name="d_y_dil")

    qb_slots, do_slots = _fox_pack_bwd(zm, f_cols + tok, lse_a, ya, dya)
    d_fox, df_cols = _fox_unpack(*_fox_bwd(qb_slots, k_slots, v_slots, do_slots))
    dfa_t, g_b_forget = _fox_post_bwd(df_cols[:, :N_HEADS].T, sg_row)

    delta_b = _attn_delta(yb, dyb, name="delta_dil")
    dil_g = [_dil_bwd(qkv_d, lse_b, delta_b, dyb, d) for _, d in DIL_PATTERNS]
    d_dil = _dil_grad_combine([g[0] for g in dil_g], [g[1] for g in dil_g], [g[2] for g in dil_g], tabs)

    dz = jnp.concatenate([d_fox, d_dil, dgates], axis=1)
    dzf = jnp.pad(dfa_t.T, ((0, 0), (0, F_PAD - N_HEADS)))
    g_w_main = mm(h1, dz, ta=True, out_dtype=BF16, tm=1024, tn=Z_MAIN // 4, tk=2048, name="grad_w_in")
    g_w_f = mm(h1, dzf, ta=True, out_dtype=BF16, tm=1024, tn=F_PAD, tk=1024, name="grad_w_in_forget")
    tok = mixer_grads_ready(dict(w_main=g_w_main, w_f=g_w_f))
    dh1 = [mm(dz, w_main, tb=True, out_dtype=BF16, tm=512, tn=1024, tk=Z_MAIN, name="d_h1"),
           mm(dzf + tok, w_f, tb=True, out_dtype=F32, tm=2048, tn=1024, tk=F_PAD, name="d_h1_forget")]
    grad_x, gg_pre_mix = _rms_bwd(dh1, x, g_pre_mix, dx2, out_dtype=F32, name="rms_pre_mix_bwd")

    grads = dict(
        b_forget=g_b_forget.reshape(1, N_HEADS), conv_b=_ffn_deinterleave(g_conv_b),
        g_pre_mix=gg_pre_mix, g_post_mix=gg_post_mix, g_pre_ffn=gg_pre_ffn, g_post_ffn=gg_post_ffn)
    return sq_err, grad_x, grads


def _exchange(arrays, scatter, *, name):
    n = len(arrays)

    def body(*refs):
        ins, outs = refs[:n], refs[n:2 * n]
        send_sems, recv_sems, local_sems = refs[2 * n:]
        x, y, c = lax.axis_index("x"), lax.axis_index("y"), lax.axis_index("c")
        me = 4 * x + 2 * y + c
        peers = []
        for k in range(1, N_DEV):
            px = 1 - x if k & 4 else x
            py = 1 - y if k & 2 else y
            pc = 1 - c if k & 1 else c
            peers.append(((px, py, pc), 4 * px + 2 * py + pc))

        def remote(a, k):
            dev, slot = peers[k]
            return pltpu.make_async_remote_copy(
                src_ref=ins[a].at[slot] if scatter else ins[a], dst_ref=outs[a].at[me],
                send_sem=send_sems.at[a, k], recv_sem=recv_sems.at[a, k],
                device_id=dev, device_id_type=MESH_ID)

        def landed(a, k):
            dev, slot = peers[k]
            return pltpu.make_async_remote_copy(
                src_ref=outs[a].at[slot], dst_ref=outs[a].at[slot],
                send_sem=send_sems.at[a, k], recv_sem=recv_sems.at[a, k],
                device_id=dev, device_id_type=MESH_ID)

        own = [pltpu.make_async_copy(ins[a].at[me] if scatter else ins[a], outs[a].at[me], local_sems.at[a])
               for a in range(n)]
        copies = [remote(a, k) for k in range(N_DEV - 1) for a in range(n)]
        for cp in own + copies:
            cp.start()
        for k in range(N_DEV - 1):
            for a in range(n):
                landed(a, k).wait_recv()
        for cp in copies:
            cp.wait_send()
        for cp in own:
            cp.wait()

    out_shape = [_sds(((N_DEV,) + a.shape[-2:]), a.dtype) for a in arrays]
    return pl.pallas_call(
        body, name=name, in_specs=[ANY] * n, out_specs=[ANY] * n, out_shape=out_shape,
        scratch_shapes=[pltpu.SemaphoreType.DMA((n, N_DEV - 1)), pltpu.SemaphoreType.DMA((n, N_DEV - 1)),
                        pltpu.SemaphoreType.DMA((n,))],
    )(*arrays)


def _gather_two_level(shard, *, name):
    def body(x_ref, out_ref, send_sems, recv_sems, local_sem):
        x, y, c = lax.axis_index("x"), lax.axis_index("y"), lax.axis_index("c")
        me, sibling = (x, y, c), (x, y, 1 - c)
        chips = [(1 - x, y), (x, 1 - y), (1 - x, 1 - y)]

        def slot(px, py, pc):
            return out_ref.at[4 * px + 2 * py + pc]

        def copy(k, block, to, src=None):
            return pltpu.make_async_remote_copy(
                src_ref=slot(*block) if src is None else src, dst_ref=slot(*block),
                send_sem=send_sems.at[k], recv_sem=recv_sems.at[k], device_id=to, device_id_type=MESH_ID)

        mine = pltpu.make_async_copy(x_ref, slot(*me), local_sem)
        mine.start()
        first = [copy(0, me, sibling, src=x_ref)]
        first += [copy(1 + j, me, (*chip, c), src=x_ref) for j, chip in enumerate(chips)]
        for cp in first:
            cp.start()
        passed = [copy(4 + j, (*chip, c), sibling) for j, chip in enumerate(chips)]
        for j, chip in enumerate(chips):
            copy(1 + j, (*chip, c), me).wait_recv()
            passed[j].start()
        copy(0, sibling, me).wait_recv()
        for j, chip in enumerate(chips):
            copy(4 + j, (*chip, 1 - c), me).wait_recv()
        for cp in first + passed:
            cp.wait_send()
        mine.wait()

    return pl.pallas_call(
        body, name=name, in_specs=[ANY], out_specs=ANY, out_shape=_sds((N_DEV,) + shard.shape, shard.dtype),
        scratch_shapes=[pltpu.SemaphoreType.DMA((N_DEV - 1,)), pltpu.SemaphoreType.DMA((N_DEV - 1,)),
                        pltpu.SemaphoreType.DMA],
    )(shard)


N_CHIPS = N_DEV // 2


def _peers(chips_only=False):
    x, y, c = lax.axis_index("x"), lax.axis_index("y"), lax.axis_index("c")
    out = []
    if chips_only:
        for k in range(1, N_CHIPS):
            px = 1 - x if k & 2 else x
            py = 1 - y if k & 1 else y
            out.append(((px, py, c), 2 * px + py))
        return 2 * x + y, out
    for k in range(1, N_DEV):
        px = 1 - x if k & 4 else x
        py = 1 - y if k & 2 else y
        pc = 1 - c if k & 1 else c
        out.append(((px, py, pc), 4 * px + 2 * py + pc))
    return 4 * x + 2 * y + c, out


def _sibling_swap(slot_arrays, *, name):
    n = len(slot_arrays)

    def body(*refs):
        ins, outs, send_sems, recv_sems = refs[:n], refs[n:2 * n], refs[2 * n], refs[2 * n + 1]
        x, y, c = lax.axis_index("x"), lax.axis_index("y"), lax.axis_index("c")
        copies = [pltpu.make_async_remote_copy(
            src_ref=ins[a].at[2 * q + (1 - c)], dst_ref=outs[a].at[q], send_sem=send_sems.at[a, q],
            recv_sem=recv_sems.at[a, q], device_id=(x, y, 1 - c), device_id_type=MESH_ID)
            for a in range(n) for q in range(N_CHIPS)]
        for cp in copies:
            cp.start()
        for cp in copies:
            cp.wait_recv()
        for cp in copies:
            cp.wait_send()

    return pl.pallas_call(
        body, name=name, in_specs=[ANY] * n, out_specs=[ANY] * n,
        out_shape=[_sds((N_CHIPS,) + t.shape[1:], t.dtype) for t in slot_arrays],
        scratch_shapes=[pltpu.SemaphoreType.DMA((n, N_CHIPS)), pltpu.SemaphoreType.DMA((n, N_CHIPS))],
    )(*slot_arrays)


def _pair_sum(slots, from_sibling, *, name, tn):
    _, r, c = slots.shape
    core = lax.axis_index("c").astype(jnp.int32).reshape(1)

    def body(core_ref, a_ref, b_ref, o_ref):
        o_ref[...] = (a_ref[...].astype(F32) + b_ref[...].astype(F32)).astype(o_ref.dtype)

    blk = lambda f: pl.BlockSpec((1, r, tn), f)
    return pl.pallas_call(
        body, name=name,
        grid_spec=pltpu.PrefetchScalarGridSpec(
            num_scalar_prefetch=1, grid=(N_CHIPS, c // tn),
            in_specs=[blk(lambda q, j, core: (2 * q + core[0], 0, j)), blk(lambda q, j, core: (q, 0, j))],
            out_specs=blk(lambda q, j, core: (q, 0, j))),
        out_shape=_sds((N_CHIPS, r, c), slots.dtype),
        compiler_params=_params("parallel", "parallel"),
    )(core, slots, from_sibling)


def _sum_parts(parts, *, name, tn):
    n, r, c = parts.shape

    def body(p_ref, o_ref):
        total = p_ref[0].astype(F32)
        for s in range(1, n):
            total = total + p_ref[s].astype(F32)
        o_ref[...] = total

    return pl.pallas_call(
        body, name=name, grid=(c // tn,),
        in_specs=[pl.BlockSpec((n, r, tn), lambda j: (0, 0, j))],
        out_specs=pl.BlockSpec((r, tn), lambda j: (0, j)), out_shape=_sds((r, c), F32),
        compiler_params=_params("parallel"),
    )(parts)


HBM = pl.BlockSpec(memory_space=pltpu.HBM)
SEM = pl.BlockSpec(memory_space=pltpu.SEMAPHORE)
DATAFLOW = pltpu.SideEffectType.DATAFLOW_SIDE_EFFECTING


def _split_copy(srcs, lands, send_sems, recv_sems, scatter, a, k, me, peers, incoming=False):
    dev, slot = peers[k]
    if incoming:
        src = dst = lands[a].at[slot]
    else:
        src, dst = (srcs[a].at[slot] if scatter else srcs[a]), lands[a].at[me]
    sem = a * len(peers) + k
    return pltpu.make_async_remote_copy(
        src_ref=src, dst_ref=dst, send_sem=send_sems.at[sem], recv_sem=recv_sems.at[sem],
        device_id=dev, device_id_type=MESH_ID)


def _exchange_start(arrays, scatter, *, name, chips_only=False):
    n = len(arrays)
    n_slots = N_CHIPS if chips_only else N_DEV

    def body(*refs):
        srcs, lands = refs[:n], refs[n:2 * n]
        send_sems, recv_sems = refs[2 * n], refs[2 * n + 1]
        token = refs[-1]
        me, peers = _peers(chips_only)
        for k in range(len(peers)):
            for a in range(n):
                _split_copy(srcs, lands, send_sems, recv_sems, scatter, a, k, me, peers).start()
        token[...] = jnp.zeros_like(token)

    land_shapes = [((n_slots,) + a.shape[-2:], a.dtype) for a in arrays]
    sems = pltpu.SemaphoreType.DMA((n * (n_slots - 1),))
    outs = pl.pallas_call(
        body, name=name,
        out_shape=(sems, sems, *[pltpu.HBM(a.shape, a.dtype) for a in arrays],
                   *[pltpu.HBM(s, d) for s, d in land_shapes], _sds((SUBLANE, LANE), F32)),
        in_specs=[HBM] * (2 * n),
        out_specs=(SEM, SEM, *[HBM] * (2 * n), pl.BlockSpec(memory_space=pltpu.VMEM)),
        input_output_aliases={i: 2 + i for i in range(2 * n)},
        compiler_params=pltpu.CompilerParams(has_side_effects=DATAFLOW),
    )(*[pltpu.with_memory_space_constraint(a, pltpu.HBM) for a in arrays],
      *[pltpu.with_memory_space_constraint(lax.empty(s, d), pltpu.HBM) for s, d in land_shapes])
    return (outs[0], outs[1], outs[2:2 + n], outs[2 + n:2 + 2 * n], scatter, chips_only), outs[-1]


def _exchange_wait(handles, after, *, name):
    send_sems, recv_sems, srcs, lands, scatter, chips_only = handles
    n = len(srcs)

    def body(*refs):
        src_refs, land_refs = refs[:n], refs[n:2 * n]
        send_ref, recv_ref = refs[2 * n], refs[2 * n + 1]
        me, peers = _peers(chips_only)
        for k in range(len(peers)):
            for a in range(n):
                _split_copy(src_refs, land_refs, send_ref, recv_ref, scatter, a, k, me, peers).wait_send()
                _split_copy(src_refs, land_refs, send_ref, recv_ref, scatter, a, k, me, peers, True).wait_recv()

    outs = pl.pallas_call(
        body, name=name,
        out_shape=tuple(pltpu.HBM(t.shape, t.dtype) for t in (*srcs, *lands)),
        in_specs=[HBM] * (2 * n) + [SEM, SEM, pl.BlockSpec(memory_space=pl.ANY)],
        out_specs=tuple([HBM] * (2 * n)),
        input_output_aliases={i: i for i in range(2 * n)},
        compiler_params=pltpu.CompilerParams(has_side_effects=DATAFLOW),
    )(*srcs, *lands, send_sems, recv_sems, after)
    return _with_own_slot(outs[n:], outs[:n], scatter, chips_only)


def _with_own_slot(landed, own, scatter, chips_only):
    me = 2 * lax.axis_index("x") + lax.axis_index("y")
    if not chips_only:
        me = 2 * me + lax.axis_index("c")
    out = []
    for buf, src in zip(landed, own):
        mine = lax.dynamic_index_in_dim(src, me, 0, keepdims=False) if scatter else src
        out.append(lax.dynamic_update_index_in_dim(buf, mine, me, 0))
    return out


def _adamw(parts, w, m, v, *, name, tm):
    r, c = w.shape
    assert r % tm == 0

    def body(p_ref, w_ref, m_ref, v_ref, g_ref, d_ref, nm_ref, nv_ref):
        _adamw_update(p_ref, w_ref, m_ref, v_ref, g_ref, d_ref, nm_ref, nv_ref)

    blk = pl.BlockSpec((tm, c), lambda i: (i, 0))
    return pl.pallas_call(
        body, name=name, grid=(r // tm,),
        in_specs=[pl.BlockSpec((parts.shape[0], tm, c), lambda i: (0, i, 0)), blk, blk, blk],
        out_specs=[blk] * 4, out_shape=[_sds((r, c), F32)] * 4,
        compiler_params=_params("parallel"),
    )(parts, w, m, v)


def _adamw_update(p_ref, w_ref, m_ref, v_ref, g_ref, d_ref, nm_ref, nv_ref):
    g = p_ref[0].astype(F32)
    for s in range(1, p_ref.shape[0]):
        g = g + p_ref[s].astype(F32)
    g_ref[...] = g
    m_new = ADAM_B1 * m_ref[...] + (1.0 - ADAM_B1) * g
    v_new = ADAM_B2 * v_ref[...] + (1.0 - ADAM_B2) * (g * g)
    nm_ref[...] = m_new
    nv_ref[...] = v_new
    m_hat = m_new / (1.0 - ADAM_B1 ** ADAM_STEP)
    v_hat = v_new / (1.0 - ADAM_B2 ** ADAM_STEP)
    d_ref[...] = -ADAM_LR * (m_hat / (jnp.sqrt(v_hat) + ADAM_EPS) + ADAM_WD * w_ref[...])


SMALL = ("g_pre_mix", "b_forget", "g_post_mix", "g_pre_ffn", "conv_b", "g_post_ffn")


def _adamw_small(parts, ws, ms, vs, sq_err_parts):
    n = len(ws)

    def body(*refs):
        ins, sq_ref, outs, loss_ref = refs[:4 * n], refs[4 * n], refs[4 * n + 1:-1], refs[-1]
        for i in range(n):
            _adamw_update(ins[i], ins[n + i], ins[2 * n + i], ins[3 * n + i], *outs[4 * i:4 * i + 4])
        total = sq_ref[0]
        for s in range(1, N_DEV):
            total = total + sq_ref[s]
        loss_ref[...] = total * (0.5 / D_MODEL)

    res = pl.pallas_call(
        body, name="adamw_small",
        out_shape=[_sds(w.shape, F32) for w in ws for _ in range(4)] + [_sds((1, LANE), F32)],
        compiler_params=pltpu.CompilerParams(vmem_limit_bytes=VMEM_LIMIT),
    )(*parts, *ws, *ms, *vs, sq_err_parts)
    return [res[4 * i:4 * i + 4] for i in range(n)], res[-1][0, 0]


def kernel(x, g_pre_mix, w_in, b_forget, w_o_fox, w_o_dil, w_out, g_post_mix, g_pre_ffn, w_up, conv_w, conv_b, w_down, g_post_ffn, loss_target, m_g_pre_mix, m_w_in, m_b_forget, m_w_o_fox, m_w_o_dil, m_w_out, m_g_post_mix, m_g_pre_ffn, m_w_up, m_conv_w, m_conv_b, m_w_down, m_g_post_ffn, v_g_pre_mix, v_w_in, v_b_forget, v_w_o_fox, v_w_o_dil, v_w_out, v_g_post_mix, v_g_pre_ffn, v_w_up, v_conv_w, v_conv_b, v_w_down, v_g_post_ffn):
    names = ("g_pre_mix", "w_in", "b_forget", "w_o_fox", "w_o_dil", "w_out", "g_post_mix", "g_pre_ffn",
             "w_up", "conv_w", "conv_b", "w_down", "g_post_ffn")
    w = dict(g_pre_mix=g_pre_mix, w_in=w_in, b_forget=b_forget, w_o_fox=w_o_fox, w_o_dil=w_o_dil, w_out=w_out,
             g_post_mix=g_post_mix, g_pre_ffn=g_pre_ffn, w_up=w_up, conv_w=conv_w, conv_b=conv_b, w_down=w_down,
             g_post_ffn=g_post_ffn)
    m = dict(g_pre_mix=m_g_pre_mix, w_in=m_w_in, b_forget=m_b_forget, w_o_fox=m_w_o_fox, w_o_dil=m_w_o_dil,
             w_out=m_w_out, g_post_mix=m_g_post_mix, g_pre_ffn=m_g_pre_ffn, w_up=m_w_up, conv_w=m_conv_w,
             conv_b=m_conv_b, w_down=m_w_down, g_post_ffn=m_g_post_ffn)
    v = dict(g_pre_mix=v_g_pre_mix, w_in=v_w_in, b_forget=v_b_forget, w_o_fox=v_w_o_fox, w_o_dil=v_w_o_dil,
             w_out=v_w_out, g_post_mix=v_g_post_mix, g_pre_ffn=v_g_pre_ffn, w_up=v_w_up, conv_w=v_conv_w,
             conv_b=v_conv_b, w_down=v_w_down, g_post_ffn=v_g_post_ffn)
    sharded = ("w_in", "w_o_fox", "w_o_dil", "w_out", "w_up", "w_down", "conv_w")
    wire = lambda n: F32 if n == "conv_w" else BF16

    by_cols = lambda t: jnp.transpose(t, (1, 0, 2)).reshape(t.shape[1], N_DEV * t.shape[2])
    by_rows = lambda t: t.reshape(N_DEV * t.shape[1], t.shape[2])
    col_slots = lambda t: jnp.transpose(t.reshape(t.shape[0], N_DEV, t.shape[1] // N_DEV), (1, 0, 2))
    row_slots = lambda t: t.reshape(N_DEV, t.shape[0] // N_DEV, t.shape[1])
    to_slots = lambda n, t: (row_slots if n in ("w_out", "w_down") else col_slots)(t).astype(wire(n))
    shard = lambda n: w[n][0].astype(wire(n))
    f_lo, f_hi = 3 * ATT_W, 3 * ATT_W + N_HEADS

    w_in_full = by_cols(_gather_two_level(shard("w_in"), name="gather_w_in"))
    w_main = jnp.concatenate([w_in_full[:, :f_lo], w_in_full[:, f_hi:]], axis=1)
    w_f = jnp.pad(w_in_full[:, f_lo:f_hi], ((0, 0), (0, F_PAD - N_HEADS)))
    late = ("w_o_fox", "w_o_dil", "w_out", "w_up", "conv_w", "w_down")
    order = jnp.minimum(jnp.abs(w_in_full[0, 0].astype(F32)), 0.0)
    late_handles, late_tok = _exchange_start(
        [shard(n) + order.astype(wire(n)) if n == "conv_w" else shard(n) for n in late], False,
        name="gather_late_start")

    def late_weights(after):
        got = dict(zip(late, _exchange_wait(late_handles, after, name="gather_late_wait")))
        return (by_cols(got["w_o_fox"]), by_cols(got["w_o_dil"]), by_rows(got["w_out"]),
                _ffn_move_blocks(by_cols(got["w_up"]), interleave=True, name="w_up_cols"),
                _ffn_interleave(by_cols(got["conv_w"])),
                by_rows(got["w_down"]))

    pending = {}

    def ffn_grads_ready(g):
        pending["ffn"] = _exchange_start([to_slots(n, g[n]) for n in ("w_down", "w_up", "conv_w")], True,
                                         name="scatter_ffn_start")
        return pending["ffn"][1][0, 0]

    def proj_grads_ready(g):
        pending["proj"] = _exchange_start([to_slots(n, g[n]) for n in ("w_o_fox", "w_o_dil", "w_out")], True,
                                          name="scatter_proj_start")
        return pending["proj"][1][0, 0]

    def mixer_grads_ready(g):
        slabs = [g["w_main"].reshape(N_DEV, D_MODEL // N_DEV, Z_MAIN), g["w_f"].reshape(N_DEV, D_MODEL // N_DEV, F_PAD)]
        theirs = _sibling_swap(slabs, name="scatter_w_in_swap")
        chip_sums = [_pair_sum(slabs[0], theirs[0], name="scatter_w_in_pair_sum", tn=Z_MAIN // 4),
                     _pair_sum(slabs[1], theirs[1], name="scatter_w_in_forget_pair_sum", tn=F_PAD)]
        pending["w_in"] = _exchange_start(chip_sums, True, name="scatter_w_in_start", chips_only=True)
        return pending["w_in"][1][0, 0]

    sq_err, grad_x, g = _local_step(
        x[0], loss_target[0], w_main, w_f, b_forget, conv_b, g_pre_mix + late_tok[0, 0], g_post_mix, g_pre_ffn,
        g_post_ffn, late_weights, ffn_grads_ready, proj_grads_ready, mixer_grads_ready)

    tiles = dict(w_in=256, w_o_fox=512, w_o_dil=512, w_out=128, w_up=256, w_down=176, conv_w=3)
    adam = lambda n, p: _adamw(p, w[n][0], m[n][0], v[n][0], name=f"adamw_{n}", tm=tiles[n])
    res = {}
    for key, group in (("ffn", ("w_down", "w_up", "conv_w")), ("proj", ("w_o_fox", "w_o_dil", "w_out"))):
        landed = _exchange_wait(pending[key][0], grad_x, name=f"scatter_{key}_wait")
        res.update({n: adam(n, p) for n, p in zip(group, landed)})
    small_parts = _exchange([g[n] for n in SMALL] + [sq_err], False, name="gather_small_grads")
    done = res["w_up"][3]
    main_parts, f_parts = _exchange_wait(pending["w_in"][0], done, name="scatter_w_in_wait")
    slab_main = _sum_parts(main_parts, name="scatter_w_in_sum", tn=Z_MAIN // 4)
    slab_f = _sum_parts(f_parts, name="scatter_w_in_forget_sum", tn=F_PAD)
    slab = jnp.concatenate([slab_main[:, :f_lo], slab_f[:, :N_HEADS], slab_main[:, f_lo:]], axis=1)
    rows = _exchange([col_slots(slab)], True, name="scatter_w_in_rows")[0]
    res["w_in"] = adam("w_in", rows.reshape(1, D_MODEL, rows.shape[-1]))
    small, loss = _adamw_small(small_parts[:-1], *[[t[n] for n in SMALL] for t in (w, m, v)], small_parts[-1])
    small = dict(zip(SMALL, small))
    out = [[(res[n][k][None] if n in sharded else small[n][k]) for n in names] for k in range(4)]
    return (loss, grad_x[None], *out[0], *out[1], *out[2], *out[3])
```

```python
import functools
import math

import jax
import jax.numpy as jnp
import numpy as np
from jax import lax
from jax.experimental import pallas as pl
from jax.experimental.pallas import tpu as pltpu

F32 = jnp.float32
BF16 = jnp.bfloat16

SEQ = 4096
D_MODEL = 1024
N_HEADS = 8
HEAD_DIM = 64
ATT_W = N_HEADS * HEAD_DIM
D_FF = 2816
Z_MAIN = 5120
F_PAD = 128
ROPE_DIM = 16
ROPE_THETA = 500000.0
RMS_EPS = 1e-6
NEG_INF = -1e30
SCALE = 1.0 / math.sqrt(HEAD_DIM)
DIL_PATTERNS = ((128, 1), (512, 4), (2048, 16))
DIL_BLK = 128
N_DEV = 8

ADAM_LR = 0.001
ADAM_B1 = 0.9
ADAM_B2 = 0.999
ADAM_EPS = 1e-08
ADAM_WD = 0.01
ADAM_STEP = 10

LANE = 128
SUBLANE = 8
VMEM_LIMIT = 56 * 1024 * 1024
MESH_ID = pl.DeviceIdType.MESH
ANY = pl.BlockSpec(memory_space=pl.ANY)


def _params(*sem):
    return pltpu.CompilerParams(dimension_semantics=sem, vmem_limit_bytes=VMEM_LIMIT)


def _sds(shape, dtype):
    return jax.ShapeDtypeStruct(shape, dtype)


def _matmul(a, b, *, ta=False, tb=False, out_dtype, tm, tn, tk, name, b_k_off=0):
    if ta:
        kk, m = a.shape
    else:
        m, kk = a.shape
    n = b.shape[0] if tb else b.shape[1]
    tm, tn, tk = min(tm, m), min(tn, n), min(tk, kk)
    assert (b.shape[1] if tb else b.shape[0]) >= b_k_off * tk + kk
    assert m % tm == 0 and n % tn == 0 and kk % tk == 0, (name, m, n, kk, tm, tn, tk)
    nk = kk // tk
    dims = (((0 if ta else 1,), (1 if tb else 0,)), ((), ()))

    def body(a_ref, b_ref, o_ref, *scratch):
        p = lax.dot_general(a_ref[...].astype(BF16), b_ref[...].astype(BF16), dims,
                            preferred_element_type=F32)
        if nk == 1:
            o_ref[...] = p.astype(o_ref.dtype)
        else:
            acc = scratch[0]
            k = pl.program_id(2)

            @pl.when(k == 0)
            def _():
                acc[...] = p

            @pl.when(k > 0)
            def _():
                acc[...] += p

            @pl.when(k == nk - 1)
            def _():
                o_ref[...] = acc[...].astype(o_ref.dtype)

    a_spec = (pl.BlockSpec((tk, tm), lambda i, j, k: (k, i)) if ta
              else pl.BlockSpec((tm, tk), lambda i, j, k: (i, k)))
    b_spec = (pl.BlockSpec((tn, tk), lambda i, j, k: (j, k + b_k_off)) if tb
              else pl.BlockSpec((tk, tn), lambda i, j, k: (k + b_k_off, j)))
    return pl.pallas_call(
        body, name=name, grid=(m // tm, n // tn, nk),
        in_specs=[a_spec, b_spec],
        out_specs=pl.BlockSpec((tm, tn), lambda i, j, k: (i, j)),
        out_shape=_sds((m, n), out_dtype),
        scratch_shapes=[pltpu.VMEM((tm, tn), F32)] if nk > 1 else [],
        compiler_params=_params("parallel", "parallel", "arbitrary"),
    )(a, b)


def _rms_fwd(x, g, *, name, tm=512):
    def body(x_ref, g_ref, h_ref):
        xv = x_ref[...]
        r = lax.rsqrt(jnp.mean(xv * xv, axis=-1, keepdims=True) + RMS_EPS)
        h_ref[...] = (xv * r * g_ref[...]).astype(h_ref.dtype)

    return pl.pallas_call(
        body, name=name, grid=(SEQ // tm,),
        in_specs=[pl.BlockSpec((tm, D_MODEL), lambda i: (i, 0)), pl.BlockSpec((1, D_MODEL), lambda i: (0, 0))],
        out_specs=pl.BlockSpec((tm, D_MODEL), lambda i: (i, 0)),
        out_shape=_sds((SEQ, D_MODEL), BF16),
        compiler_params=_params("parallel"),
    )(x, g)


def _rms_bwd(dh_parts, xin, g, dres, *, out_dtype, name, tm=512):
    n_parts = len(dh_parts)
    has_res = dres is not None

    def body(*refs):
        parts = refs[:n_parts]
        x_ref, g_ref = refs[n_parts], refs[n_parts + 1]
        res_ref = refs[n_parts + 2] if has_res else None
        o_ref, gg_ref = refs[-2], refs[-1]
        dh = parts[0][...].astype(F32)
        for p in parts[1:]:
            dh = dh + p[...].astype(F32)
        xv = x_ref[...]
        r = lax.rsqrt(jnp.mean(xv * xv, axis=-1, keepdims=True) + RMS_EPS)
        xn = xv * r

        @pl.when(pl.program_id(0) == 0)
        def _():
            gg_ref[...] = jnp.zeros_like(gg_ref)

        gg_ref[...] += jnp.sum(dh * xn, axis=0, keepdims=True)
        dxn = dh * g_ref[...]
        dx = r * (dxn - xn * jnp.mean(dxn * xn, axis=-1, keepdims=True))
        if has_res:
            dx = dx + res_ref[...]
        o_ref[...] = dx.astype(o_ref.dtype)

    row = pl.BlockSpec((tm, D_MODEL), lambda i: (i, 0))
    vec = pl.BlockSpec((1, D_MODEL), lambda i: (0, 0))
    args = list(dh_parts) + [xin, g] + ([dres] if has_res else [])
    return pl.pallas_call(
        body, name=name, grid=(SEQ // tm,),
        in_specs=[row] * n_parts + [row, vec] + ([row] if has_res else []),
        out_specs=[row, vec],
        out_shape=[_sds((SEQ, D_MODEL), out_dtype), _sds((1, D_MODEL), F32)],
        compiler_params=_params("arbitrary"),
    )(*args)


def _rms_pair_bwd(dh_parts, x2, g_pre, dres, y1, g_post, *, tm=512):
    n_parts = len(dh_parts)

    def norm_bwd(dh, xin, g_ref, gg_ref):
        r = lax.rsqrt(jnp.mean(xin * xin, axis=-1, keepdims=True) + RMS_EPS)
        xn = xin * r
        gg_ref[...] += jnp.sum(dh * xn, axis=0, keepdims=True)
        dxn = dh * g_ref[...]
        return r * (dxn - xn * jnp.mean(dxn * xn, axis=-1, keepdims=True))

    def body(*refs):
        parts = refs[:n_parts]
        x2_ref, gpre_ref, res_ref, y1_ref, gpost_ref, dx2_ref, dy1_ref, ggpre_ref, ggpost_ref = refs[n_parts:]

        @pl.when(pl.program_id(0) == 0)
        def _():
            ggpre_ref[...] = jnp.zeros_like(ggpre_ref)
            ggpost_ref[...] = jnp.zeros_like(ggpost_ref)

        dh = parts[0][...].astype(F32)
        for p in parts[1:]:
            dh = dh + p[...].astype(F32)
        dx2 = res_ref[...] + norm_bwd(dh, x2_ref[...], gpre_ref, ggpre_ref)
        dx2_ref[...] = dx2
        dy1_ref[...] = norm_bwd(dx2, y1_ref[...], gpost_ref, ggpost_ref).astype(dy1_ref.dtype)

    row = pl.BlockSpec((tm, D_MODEL), lambda i: (i, 0))
    vec = pl.BlockSpec((1, D_MODEL), lambda i: (0, 0))
    return pl.pallas_call(
        body, name="rms_pair_bwd", grid=(SEQ // tm,),
        in_specs=[row] * n_parts + [row, vec, row, row, vec],
        out_specs=[row, row, vec, vec],
        out_shape=[_sds((SEQ, D_MODEL), F32), _sds((SEQ, D_MODEL), BF16), _sds((1, D_MODEL), F32),
                   _sds((1, D_MODEL), F32)],
        compiler_params=_params("arbitrary"),
    )(*dh_parts, x2, g_pre, dres, y1, g_post)


SCAN_BLK = 512


def _split_dot(v, tri):
    hi = v.astype(BF16)
    r1 = v - hi.astype(F32)
    mid = r1.astype(BF16)
    lo = (r1 - mid.astype(F32)).astype(BF16)
    dot = functools.partial(jnp.dot, preferred_element_type=F32)
    return dot(hi, tri) + dot(mid, tri) + dot(lo, tri)


def _fox_prep(fa_t, b_col):
    nblk = SEQ // SCAN_BLK

    def body(fa_ref, b_ref, f_ref, sg_ref):
        row = lax.broadcasted_iota(jnp.int32, (SCAN_BLK, SCAN_BLK), 0)
        col = lax.broadcasted_iota(jnp.int32, (SCAN_BLK, SCAN_BLK), 1)
        upper = (row <= col).astype(BF16)
        carry = jnp.zeros((N_HEADS, 1), F32)
        for blk in range(nblk):
            sl = pl.ds(blk * SCAN_BLK, SCAN_BLK)
            xx = fa_ref[:, sl] + b_ref[...]
            e = jnp.exp(-jnp.abs(xx))
            logf = jnp.minimum(xx, 0.0) - jnp.log(1.0 + e)
            sg_ref[:, sl] = jnp.where(xx >= 0.0, e, 1.0) / (1.0 + e)
            c = _split_dot(logf, upper) + carry
            f_ref[:, sl] = c
            carry = c[:, SCAN_BLK - 1:SCAN_BLK]

    return pl.pallas_call(
        body, name="fox_prep",
        out_shape=[_sds((N_HEADS, SEQ), F32), _sds((N_HEADS, SEQ), F32)],
        compiler_params=pltpu.CompilerParams(vmem_limit_bytes=VMEM_LIMIT),
    )(fa_t, b_col)


def _fox_post_bwd(df_t, sg_t):
    nblk = SEQ // SCAN_BLK

    def body(df_ref, sg_ref, dfa_ref, gb_ref):
        row = lax.broadcasted_iota(jnp.int32, (SCAN_BLK, SCAN_BLK), 0)
        col = lax.broadcasted_iota(jnp.int32, (SCAN_BLK, SCAN_BLK), 1)
        lower = (row >= col).astype(BF16)
        carry = jnp.zeros((N_HEADS, 1), F32)
        gb = jnp.zeros((N_HEADS, 1), F32)
        for blk in reversed(range(nblk)):
            sl = pl.ds(blk * SCAN_BLK, SCAN_BLK)
            c = _split_dot(df_ref[:, sl], lower) + carry
            carry = c[:, 0:1]
            dfa = c * sg_ref[:, sl]
            dfa_ref[:, sl] = dfa
            gb = gb + jnp.sum(dfa, axis=1, keepdims=True)
        gb_ref[...] = gb

    return pl.pallas_call(
        body, name="fox_post_bwd",
        out_shape=[_sds((N_HEADS, SEQ), F32), _sds((N_HEADS, 1), F32)],
        compiler_params=pltpu.CompilerParams(vmem_limit_bytes=VMEM_LIMIT),
    )(df_t, sg_t)


FOX_T = 512
NT_DIMS = (((1,), (1,)), ((), ()))
TN_DIMS = (((0,), (0,)), ((), ()))


def _head(ref_or_val, h):
    return ref_or_val[:, h * HEAD_DIM:(h + 1) * HEAD_DIM]


def _split3(v):
    hi = v.astype(BF16).astype(F32)
    r1 = v - hi
    mid = r1.astype(BF16).astype(F32)
    return hi, mid, (r1 - mid).astype(BF16).astype(F32)


ONE_LANE = 3 * N_HEADS


def _pack_terms(v, with_one):
    hi, mid, lo = _split3(v)
    t = hi + pltpu.roll(mid, N_HEADS, 1) + pltpu.roll(lo, 2 * N_HEADS, 1)
    if with_one:
        t = t + (lax.broadcasted_iota(jnp.int32, v.shape, 1) == ONE_LANE).astype(F32)
    return t.astype(BF16)


def _aux_matrices():
    to_q = np.zeros((LANE, N_HEADS * 2 * HEAD_DIM), np.float32)
    to_k = np.zeros_like(to_q)
    for h in range(N_HEADS):
        base = h * 2 * HEAD_DIM + HEAD_DIM
        for s in range(3):
            to_q[s * N_HEADS + h, base + s] = 1.0
            to_q[ONE_LANE, base + 3 + s] = 1.0
            to_k[ONE_LANE, base + s] = 1.0
            to_k[s * N_HEADS + h, base + 3 + s] = -1.0
    return jnp.asarray(to_q, BF16), jnp.asarray(to_k, BF16)


def _head_sums():
    total = np.zeros((N_HEADS * HEAD_DIM, LANE), np.float32)
    first = np.zeros_like(total)
    for h in range(N_HEADS):
        total[h * HEAD_DIM:(h + 1) * HEAD_DIM, h] = 1.0
        first[h * HEAD_DIM, h] = 1.0
    return jnp.asarray(total, BF16), jnp.asarray(first, BF16)


SLOT = 2 * HEAD_DIM
N_SPLIT = 3
FOX_FWD_HEADS = 8
FOX_BWD_HEADS = 4


def _slot(ref, h):
    return ref[:, h * SLOT:(h + 1) * SLOT]


def _fox_pack_fwd(zm, f_cols, *, tm=512):
    def body(q_ref, k_ref, v_ref, f_ref, tq_ref, tk_ref, qs_ref, ks_ref, vs_ref):
        ones = jnp.ones((tm, HEAD_DIM), BF16)
        terms = _pack_terms(f_ref[...], True)
        q_aux = jnp.dot(terms, tq_ref[...], preferred_element_type=F32).astype(BF16)
        k_aux = jnp.dot(terms, tk_ref[...], preferred_element_type=F32).astype(BF16)
        for h in range(N_HEADS):
            aux = slice(h * SLOT + HEAD_DIM, (h + 1) * SLOT)
            qs_ref[:, h * SLOT:(h + 1) * SLOT] = jnp.concatenate(
                [(_head(q_ref, h).astype(F32) * SCALE).astype(BF16), q_aux[:, aux]], axis=1)
            ks_ref[:, h * SLOT:(h + 1) * SLOT] = jnp.concatenate([_head(k_ref, h), k_aux[:, aux]], axis=1)
            vs_ref[:, h * SLOT:(h + 1) * SLOT] = jnp.concatenate([_head(v_ref, h), ones], axis=1)

    col = lambda b: pl.BlockSpec((tm, ATT_W), lambda i: (i, b))
    wide = pl.BlockSpec((tm, N_HEADS * SLOT), lambda i: (i, 0))
    const = pl.BlockSpec((LANE, N_HEADS * SLOT), lambda i: (0, 0))
    return pl.pallas_call(
        body, name="fox_pack_fwd", grid=(SEQ // tm,),
        in_specs=[col(0), col(1), col(2), pl.BlockSpec((tm, LANE), lambda i: (i, 0)), const, const],
        out_specs=[wide] * 3, out_shape=[_sds((SEQ, N_HEADS * SLOT), BF16)] * 3,
        compiler_params=_params("parallel"),
    )(zm, zm, zm, f_cols, *_aux_matrices())


def _fox_pack_bwd(zm, f_cols, lse, o, do, *, tm=512):
    def body(q_ref, f_ref, lse_ref, o_ref, do_ref, tq_ref, total_ref, first_ref, qs_ref, ds_ref):
        delta = _split_dot(o_ref[...].astype(F32) * do_ref[...].astype(F32), total_ref[...])
        lse_h = _split_dot(lse_ref[...], first_ref[...])
        q_aux = jnp.dot(_pack_terms(f_ref[...] - lse_h, True), tq_ref[...], preferred_element_type=F32).astype(BF16)
        d_aux = jnp.dot(_pack_terms(-delta, False), tq_ref[...], preferred_element_type=F32).astype(BF16)
        for h in range(N_HEADS):
            aux = slice(h * SLOT + HEAD_DIM, (h + 1) * SLOT)
            qs_ref[:, h * SLOT:(h + 1) * SLOT] = jnp.concatenate(
                [(_head(q_ref, h).astype(F32) * SCALE).astype(BF16), q_aux[:, aux]], axis=1)
            ds_ref[:, h * SLOT:(h + 1) * SLOT] = jnp.concatenate([_head(do_ref, h), d_aux[:, aux]], axis=1)

    row = pl.BlockSpec((tm, ATT_W), lambda i: (i, 0))
    wide = pl.BlockSpec((tm, N_HEADS * SLOT), lambda i: (i, 0))
    const = lambda r, c: pl.BlockSpec((r, c), lambda i: (0, 0))
    return pl.pallas_call(
        body, name="fox_pack_bwd", grid=(SEQ // tm,),
        in_specs=[row, pl.BlockSpec((tm, LANE), lambda i: (i, 0)), row, row, row,
                  const(LANE, N_HEADS * SLOT), const(ATT_W, LANE), const(ATT_W, LANE)],
        out_specs=[wide] * 2, out_shape=[_sds((SEQ, N_HEADS * SLOT), BF16)] * 2,
        compiler_params=_params("parallel"),
    )(zm, f_cols, lse, o, do, _aux_matrices()[0], *_head_sums())


def _causal_pairs(key_major):
    nb = SEQ // FOX_T
    if key_major:
        pairs = [(i, j) for j in range(nb) for i in range(j, nb)]
    else:
        pairs = [(i, j) for i in range(nb) for j in range(i + 1)]
    return (jnp.array([p[0] for p in pairs], jnp.int32), jnp.array([p[1] for p in pairs], jnp.int32), len(pairs))


def _diag_mask():
    row = lax.broadcasted_iota(jnp.int32, (FOX_T, FOX_T), 0)
    col = lax.broadcasted_iota(jnp.int32, (FOX_T, FOX_T), 1)
    return col <= row


def _fox_fwd(q_slots, k_slots, v_slots):
    i_tab, j_tab, n_pairs = _causal_pairs(False)

    def body(i_tab, j_tab, q_ref, k_ref, v_ref, o_ref, lse_ref, m_s, acc_s):
        t = pl.program_id(1)
        i, j = i_tab[t], j_tab[t]

        @pl.when(j == 0)
        def _():
            m_s[...] = jnp.full_like(m_s, NEG_INF)
            acc_s[...] = jnp.zeros_like(acc_s)

        def step(masked):
            scores = [lax.dot_general(_slot(q_ref, h), _slot(k_ref, h), NT_DIMS, preferred_element_type=F32)
                      for h in range(FOX_FWD_HEADS)]
            probs, alphas = [], []
            for h in range(FOX_FWD_HEADS):
                s = jnp.where(_diag_mask(), scores[h], NEG_INF) if masked else scores[h]
                m_prev = m_s[h]
                m_new = jnp.maximum(m_prev, jnp.max(s, axis=-1, keepdims=True))
                probs.append(jnp.exp(s - jnp.tile(m_new, (1, FOX_T // LANE))).astype(BF16))
                alphas.append(jnp.exp(m_prev - m_new))
                m_s[h] = m_new
            for h in range(FOX_FWD_HEADS):
                acc_s[h] = alphas[h] * acc_s[h] + jnp.dot(probs[h], _slot(v_ref, h), preferred_element_type=F32)

        @pl.when(j < i)
        def _():
            step(False)

        @pl.when(j == i)
        def _():
            step(True)
            outs, lses = [], []
            for h in range(FOX_FWD_HEADS):
                acc = acc_s[h]
                l = acc[:, HEAD_DIM:]
                outs.append(acc[:, :HEAD_DIM] / l)
                lses.append(m_s[h][:, :HEAD_DIM] + jnp.log(l))
            o_ref[...] = jnp.concatenate(outs, axis=1).astype(o_ref.dtype)
            lse_ref[...] = jnp.concatenate(lses, axis=1)

    qspec = pl.BlockSpec((FOX_T, FOX_FWD_HEADS * SLOT), lambda p, t, it, jt: (it[t], p))
    kspec = pl.BlockSpec((FOX_T, FOX_FWD_HEADS * SLOT), lambda p, t, it, jt: (jt[t], p))
    ospec = pl.BlockSpec((FOX_T, FOX_FWD_HEADS * HEAD_DIM), lambda p, t, it, jt: (it[t], p))
    return pl.pallas_call(
        body, name="fox_fwd",
        grid_spec=pltpu.PrefetchScalarGridSpec(
            num_scalar_prefetch=2, grid=(N_HEADS // FOX_FWD_HEADS, n_pairs),
            in_specs=[qspec, kspec, kspec], out_specs=[ospec, ospec],
            scratch_shapes=[pltpu.VMEM((FOX_FWD_HEADS, FOX_T, LANE), F32),
                            pltpu.VMEM((FOX_FWD_HEADS, FOX_T, SLOT), F32)]),
        out_shape=[_sds((SEQ, ATT_W), BF16), _sds((SEQ, ATT_W), F32)],
        compiler_params=_params("parallel", "arbitrary"),
    )(i_tab, j_tab, q_slots, k_slots, v_slots)


def _fox_bwd(q_slots, k_slots, v_slots, do_slots):
    i_tab, j_tab, n_pairs = _causal_pairs(True)

    def body(i_tab, j_tab, q_ref, k_ref, v_ref, do_ref, dq_ref, dk_ref, dv_ref):
        t = pl.program_id(1)
        i, j = i_tab[t], j_tab[t]

        @pl.when(t == 0)
        def _():
            dq_ref[...] = jnp.zeros_like(dq_ref)

        @pl.when(i == j)
        def _():
            dk_ref[...] = jnp.zeros_like(dk_ref)
            dv_ref[...] = jnp.zeros_like(dv_ref)

        def step(masked):
            rows = pl.ds(pl.multiple_of(i * FOX_T, FOX_T), FOX_T)
            heads = range(FOX_BWD_HEADS)
            scores = [lax.dot_general(_slot(q_ref, h), _slot(k_ref, h), NT_DIMS, preferred_element_type=F32)
                      for h in heads]
            dps = [lax.dot_general(_slot(do_ref, h), _slot(v_ref, h), NT_DIMS, preferred_element_type=F32)
                   for h in heads]
            ps, dss = [], []
            for h in heads:
                p = jnp.exp(scores[h])
                if masked:
                    p = jnp.where(_diag_mask(), p, 0.0)
                ps.append(p.astype(BF16))
                dss.append((p * dps[h]).astype(BF16))
            for h in heads:
                cols = slice(h * SLOT, (h + 1) * SLOT)
                dv_ref[:, cols] += lax.dot_general(ps[h], _slot(do_ref, h), TN_DIMS, preferred_element_type=F32)
                dk_ref[:, cols] += lax.dot_general(dss[h], _slot(q_ref, h), TN_DIMS, preferred_element_type=F32)
                dq_ref[rows, cols] += jnp.dot(dss[h], _slot(k_ref, h), preferred_element_type=F32)

        @pl.when(i > j)
        def _():
            step(False)

        @pl.when(i == j)
        def _():
            step(True)

    qspec = pl.BlockSpec((FOX_T, FOX_BWD_HEADS * SLOT), lambda p, t, it, jt: (it[t], p))
    kspec = pl.BlockSpec((FOX_T, FOX_BWD_HEADS * SLOT), lambda p, t, it, jt: (jt[t], p))
    return pl.pallas_call(
        body, name="fox_bwd",
        grid_spec=pltpu.PrefetchScalarGridSpec(
            num_scalar_prefetch=2, grid=(N_HEADS // FOX_BWD_HEADS, n_pairs),
            in_specs=[qspec, kspec, kspec, qspec],
            out_specs=[pl.BlockSpec((SEQ, FOX_BWD_HEADS * SLOT), lambda p, t, it, jt: (0, p)), kspec, kspec]),
        out_shape=[_sds((SEQ, N_HEADS * SLOT), F32)] * 3,
        compiler_params=_params("arbitrary", "arbitrary"),
    )(i_tab, j_tab, q_slots, k_slots, v_slots, do_slots)


def _fox_unpack(dq_slots, dk_slots, dv_slots, dz, *, tm=512):
    def body(dq_ref, dk_ref, dv_ref, dz_in, o_ref, df_ref):
        lane = lax.broadcasted_iota(jnp.int32, (tm, LANE), 1)
        df = jnp.zeros((tm, LANE), F32)
        for h in range(N_HEADS):
            lo = h * SLOT
            for part, (ref, mult) in enumerate(((dq_ref, SCALE), (dk_ref, 1.0), (dv_ref, 1.0))):
                o_ref[:, part * ATT_W + h * HEAD_DIM:part * ATT_W + (h + 1) * HEAD_DIM] = (
                    ref[:, lo:lo + HEAD_DIM] * mult).astype(o_ref.dtype)
            rows = dq_ref[:, lo + HEAD_DIM:lo + HEAD_DIM + 1]
            cols = dk_ref[:, lo + HEAD_DIM + N_SPLIT:lo + HEAD_DIM + N_SPLIT + 1]
            df = jnp.where(lane == h, rows - cols, df)
        df_ref[...] = df

    wide = pl.BlockSpec((tm, N_HEADS * SLOT), lambda i: (i, 0))
    return pl.pallas_call(
        body, name="fox_unpack", grid=(SEQ // tm,), in_specs=[wide] * 3 + [ANY],
        out_specs=[pl.BlockSpec((tm, 3 * ATT_W), lambda i: (i, 0)), pl.BlockSpec((tm, LANE), lambda i: (i, 0))],
        out_shape=[_sds((SEQ, Z_MAIN), BF16), _sds((SEQ, LANE), F32)],
        input_output_aliases={3: 0},
        compiler_params=_params("parallel"),
    )(dq_slots, dk_slots, dv_slots, dz)


def _attn_delta(o, do, *, name, tm=512):
    def body(o_ref, do_ref, d_ref):
        prod = o_ref[...].astype(F32) * do_ref[...].astype(F32)
        lane = lax.broadcasted_iota(jnp.int32, (tm, LANE), 1)
        out = jnp.zeros((tm, LANE), F32)
        for h in range(N_HEADS):
            out = jnp.where(lane == h, jnp.sum(_head(prod, h), axis=1, keepdims=True), out)
        d_ref[...] = out

    row = pl.BlockSpec((tm, ATT_W), lambda i: (i, 0))
    return pl.pallas_call(
        body, name=name, grid=(SEQ // tm,), in_specs=[row, row],
        out_specs=pl.BlockSpec((tm, LANE), lambda i: (i, 0)), out_shape=_sds((SEQ, LANE), F32),
        compiler_params=_params("parallel"),
    )(o, do)


def _rope_tables():
    half = ROPE_DIM // 2
    inv_freq = np.float32(ROPE_THETA) ** (-np.arange(half, dtype=np.float32) * np.float32(2.0) / np.float32(ROPE_DIM))
    ang = np.arange(SEQ, dtype=np.float32)[:, None] * inv_freq.astype(np.float32)[None, :]
    cos, sin = jnp.asarray(np.cos(ang).astype(np.float32)), jnp.asarray(np.sin(ang).astype(np.float32))
    ones = jnp.ones((SEQ, HEAD_DIM - ROPE_DIM), F32)
    zeros = jnp.zeros((SEQ, HEAD_DIM - ROPE_DIM), F32)
    zh = jnp.zeros((SEQ, half), F32)
    c_tab = jnp.concatenate([cos, cos, ones], axis=1)
    a_tab = jnp.concatenate([-sin, zh, zeros], axis=1)
    b_tab = jnp.concatenate([zh, sin, zeros], axis=1)
    two = lambda t: jnp.concatenate([t, t], axis=1)
    return two(c_tab), two(a_tab), two(b_tab)


def _rotate(x, c_tab, a_tab, b_tab):
    return x * c_tab + pltpu.roll(x, LANE - ROPE_DIM // 2, 1) * a_tab + pltpu.roll(x, ROPE_DIM // 2, 1) * b_tab


def _rope_fwd(zm, tabs, *, tm=512):
    def body(q_ref, k_ref, v_ref, c_ref, a_ref, b_ref, o_ref):
        for part, (x_ref, mult) in enumerate(((q_ref, SCALE), (k_ref, 1.0))):
            for cc in range(ATT_W // LANE):
                sl = slice(cc * LANE, (cc + 1) * LANE)
                rot = _rotate(x_ref[:, sl].astype(F32), c_ref[...], a_ref[...], b_ref[...])
                o_ref[:, part * ATT_W + cc * LANE:part * ATT_W + (cc + 1) * LANE] = (rot * mult).astype(o_ref.dtype)
        o_ref[:, 2 * ATT_W:] = v_ref[...]

    tab = pl.BlockSpec((tm, LANE), lambda i: (i, 0))
    col = lambda b: pl.BlockSpec((tm, ATT_W), lambda i: (i, b))
    return pl.pallas_call(
        body, name="rope_fwd", grid=(SEQ // tm,),
        in_specs=[col(3), col(4), col(5), tab, tab, tab],
        out_specs=pl.BlockSpec((tm, 3 * ATT_W), lambda i: (i, 0)),
        out_shape=_sds((SEQ, 3 * ATT_W), BF16),
        compiler_params=_params("parallel"),
    )(zm, zm, zm, *tabs)


def _dil_grad_combine(dqs, dks, dvs, tabs, dz, *, tm=256):
    def body(*refs):
        q_refs, k_refs, v_refs = refs[0:3], refs[3:6], refs[6:9]
        c_ref, a_ref, b_ref, _, o_ref = refs[9:]
        total = lambda rs, sl: rs[0][:, sl].astype(F32) + rs[1][:, sl].astype(F32) + rs[2][:, sl].astype(F32)
        for cc in range(ATT_W // LANE):
            sl = slice(cc * LANE, (cc + 1) * LANE)
            for part, rs in enumerate((q_refs, k_refs)):
                o_ref[:, part * ATT_W + cc * LANE:part * ATT_W + (cc + 1) * LANE] = _rotate(
                    total(rs, sl), c_ref[...], -a_ref[...], -b_ref[...]).astype(o_ref.dtype)
            o_ref[:, 2 * ATT_W + cc * LANE:2 * ATT_W + (cc + 1) * LANE] = total(v_refs, sl).astype(o_ref.dtype)

    row = pl.BlockSpec((tm, ATT_W), lambda i: (i, 0))
    tab = pl.BlockSpec((tm, LANE), lambda i: (i, 0))
    return pl.pallas_call(
        body, name="dil_grad_combine", grid=(SEQ // tm,),
        in_specs=[row] * 9 + [tab] * 3 + [ANY],
        out_specs=pl.BlockSpec((tm, 3 * ATT_W), lambda i: (i, 1)),
        out_shape=_sds((SEQ, Z_MAIN), BF16),
        input_output_aliases={12: 0},
        compiler_params=_params("parallel"),
    )(*dqs, *dks, *dvs, *tabs, dz)


def _dil_valid(n):
    qi = lax.broadcasted_iota(jnp.int32, (DIL_BLK, 2 * DIL_BLK), 0)
    ki = lax.broadcasted_iota(jnp.int32, (DIL_BLK, 2 * DIL_BLK), 1)
    dist = qi + DIL_BLK - ki
    return (dist >= 0) & (dist <= DIL_BLK) & ((n > 0) | (ki >= DIL_BLK))


def _dil_fwd(qkv, d):
    length = SEQ // d
    nb = length // DIL_BLK
    qkv_v = qkv.reshape(length, d * 3 * ATT_W)

    def body(q_ref, kp_ref, kc_ref, vp_ref, vc_ref, o_ref, lse_ref):
        n = pl.program_id(1)
        ok = _dil_valid(n)
        lane = lax.broadcasted_iota(jnp.int32, (DIL_BLK, LANE), 1)
        lse_all = jnp.zeros((DIL_BLK, LANE), F32)
        heads = range(N_HEADS)
        scores = [lax.dot_general(_head(q_ref, h), jnp.concatenate([_head(kp_ref, h), _head(kc_ref, h)], axis=0),
                                  NT_DIMS, preferred_element_type=F32) for h in heads]
        probs, inv_l = [], []
        for h in heads:
            s = jnp.where(ok, scores[h], NEG_INF)
            m = jnp.max(s, axis=-1, keepdims=True)
            p = jnp.exp(s - m)
            l = jnp.sum(p, axis=-1, keepdims=True)
            probs.append(p.astype(BF16))
            inv_l.append(1.0 / l)
            lse_all = jnp.where(lane == h, m + jnp.log(l), lse_all)
        outs = [jnp.dot(probs[h], jnp.concatenate([_head(vp_ref, h), _head(vc_ref, h)], axis=0),
                        preferred_element_type=F32) * inv_l[h] for h in heads]
        o_ref[...] = jnp.concatenate(outs, axis=1).astype(o_ref.dtype)
        lse_ref[...] = lse_all

    blk = lambda f: pl.BlockSpec((DIL_BLK, ATT_W), f)
    prev = lambda n: jnp.maximum(n - 1, 0)
    o, lse = pl.pallas_call(
        body, name=f"dil_fwd_d{d}", grid=(d, nb),
        in_specs=[blk(lambda r, n: (n, 3 * r)),
                  blk(lambda r, n: (prev(n), 3 * r + 1)), blk(lambda r, n: (n, 3 * r + 1)),
                  blk(lambda r, n: (prev(n), 3 * r + 2)), blk(lambda r, n: (n, 3 * r + 2))],
        out_specs=[blk(lambda r, n: (n, r)), pl.BlockSpec((DIL_BLK, LANE), lambda r, n: (n, r))],
        out_shape=[_sds((length, d * ATT_W), BF16), _sds((length, d * LANE), F32)],
        compiler_params=_params("parallel", "arbitrary"),
    )(qkv_v, qkv_v, qkv_v, qkv_v, qkv_v)
    return o.reshape(SEQ, ATT_W), lse.reshape(SEQ, LANE)


def _dil_merge(os_, lses, *, tm=512):
    def body(o0, o1, o2, l0, l1, l2, y_ref, lse_ref):
        ls = [l0[...], l1[...], l2[...]]
        m = jnp.maximum(jnp.maximum(ls[0], ls[1]), ls[2])
        es = [jnp.exp(l - m) for l in ls]
        tot = es[0] + es[1] + es[2]
        lse_ref[...] = m + jnp.log(tot)
        alphas = [e / tot for e in es]
        outs = []
        for h in range(N_HEADS):
            acc = None
            for g, o_ref in enumerate((o0, o1, o2)):
                term = alphas[g][:, h:h + 1] * _head(o_ref, h).astype(F32)
                acc = term if acc is None else acc + term
            outs.append(acc)
        y_ref[...] = jnp.concatenate(outs, axis=1).astype(y_ref.dtype)

    row = pl.BlockSpec((tm, ATT_W), lambda i: (i, 0))
    vec = pl.BlockSpec((tm, LANE), lambda i: (i, 0))
    return pl.pallas_call(
        body, name="dil_merge", grid=(SEQ // tm,),
        in_specs=[row] * 3 + [vec] * 3, out_specs=[row, vec],
        out_shape=[_sds((SEQ, ATT_W), BF16), _sds((SEQ, LANE), F32)],
        compiler_params=_params("parallel"),
    )(*os_, *lses)


def _dil_bwd(qkv, lse, delta, do, d):
    length = SEQ // d
    nb = length // DIL_BLK
    qkv_v = qkv.reshape(length, d * 3 * ATT_W)
    lse_v, dl_v, do_v = lse.reshape(length, d * LANE), delta.reshape(length, d * LANE), do.reshape(length, d * ATT_W)

    def body(q_ref, kp_ref, kc_ref, vp_ref, vc_ref, lse_ref, dl_ref, do_ref,
             dq_ref, dk_ref, dv_ref, ck_s, cv_s):
        n = pl.program_id(1)

        @pl.when(n == 0)
        def _():
            ck_s[...] = jnp.zeros_like(ck_s)
            cv_s[...] = jnp.zeros_like(cv_s)

        @pl.when(n < nb)
        def _():
            ok = _dil_valid(n)
            heads = range(N_HEADS)
            kks = [jnp.concatenate([_head(kp_ref, h), _head(kc_ref, h)], axis=0) for h in heads]
            scores = [lax.dot_general(_head(q_ref, h), kks[h], NT_DIMS, preferred_element_type=F32) for h in heads]
            dps = [lax.dot_general(_head(do_ref, h), jnp.concatenate([_head(vp_ref, h), _head(vc_ref, h)], axis=0),
                                   NT_DIMS, preferred_element_type=F32) for h in heads]
            ps, dss = [], []
            for h in heads:
                p = jnp.where(ok, jnp.exp(scores[h] - lse_ref[:, h:h + 1]), 0.0)
                ps.append(p.astype(BF16))
                dss.append((p * (dps[h] - dl_ref[:, h:h + 1])).astype(BF16))
            dqs = [jnp.dot(dss[h], kks[h], preferred_element_type=F32) * SCALE for h in heads]
            dkks = [lax.dot_general(dss[h], _head(q_ref, h), TN_DIMS, preferred_element_type=F32) for h in heads]
            dvvs = [lax.dot_general(ps[h], _head(do_ref, h), TN_DIMS, preferred_element_type=F32) for h in heads]
            dq_ref[...] = jnp.concatenate(dqs, axis=1).astype(dq_ref.dtype)
            dk_ref[...] = (ck_s[...] + jnp.concatenate([t[:DIL_BLK] for t in dkks], axis=1)).astype(dk_ref.dtype)
            dv_ref[...] = (cv_s[...] + jnp.concatenate([t[:DIL_BLK] for t in dvvs], axis=1)).astype(dv_ref.dtype)
            ck_s[...] = jnp.concatenate([t[DIL_BLK:] for t in dkks], axis=1)
            cv_s[...] = jnp.concatenate([t[DIL_BLK:] for t in dvvs], axis=1)

        @pl.when(n == nb)
        def _():
            dk_ref[...] = ck_s[...].astype(dk_ref.dtype)
            dv_ref[...] = cv_s[...].astype(dv_ref.dtype)

    blk = lambda f: pl.BlockSpec((DIL_BLK, ATT_W), f)
    vec = lambda f: pl.BlockSpec((DIL_BLK, LANE), f)
    cur = lambda n: jnp.minimum(n, nb - 1)
    prev = lambda n: jnp.maximum(cur(n) - 1, 0)
    back = lambda n: jnp.maximum(n - 1, 0)
    outs = pl.pallas_call(
        body, name=f"dil_bwd_d{d}", grid=(d, nb + 1),
        in_specs=[blk(lambda r, n: (cur(n), 3 * r)),
                  blk(lambda r, n: (prev(n), 3 * r + 1)), blk(lambda r, n: (cur(n), 3 * r + 1)),
                  blk(lambda r, n: (prev(n), 3 * r + 2)), blk(lambda r, n: (cur(n), 3 * r + 2)),
                  vec(lambda r, n: (cur(n), r)), vec(lambda r, n: (cur(n), r)),
                  blk(lambda r, n: (cur(n), r))],
        out_specs=[blk(lambda r, n: (cur(n), r)), blk(lambda r, n: (back(n), r)), blk(lambda r, n: (back(n), r))],
        out_shape=[_sds((length, d * ATT_W), BF16)] * 3,
        scratch_shapes=[pltpu.VMEM((DIL_BLK, ATT_W), F32), pltpu.VMEM((DIL_BLK, ATT_W), F32)],
        compiler_params=_params("arbitrary", "arbitrary"),
    )(qkv_v, qkv_v, qkv_v, qkv_v, qkv_v, lse_v, dl_v, do_v)
    return [t.reshape(SEQ, ATT_W) for t in outs]


def _sigmoid(x):
    return 1.0 / (1.0 + jnp.exp(-x))


def _mix_fwd(ya, yb, w_oa, w_ob, zm, *, tm=512):
    def body(ya_ref, yb_ref, wa_ref, wb_ref, ga_ref, gb_ref, pa_ref, pb_ref, mix_ref):
        pa = jnp.dot(ya_ref[...], wa_ref[...], preferred_element_type=F32)
        pb = jnp.dot(yb_ref[...], wb_ref[...], preferred_element_type=F32)
        pa_ref[...] = pa.astype(pa_ref.dtype)
        pb_ref[...] = pb.astype(pb_ref.dtype)
        mix_ref[...] = (_sigmoid(ga_ref[...].astype(F32)) * pa + _sigmoid(gb_ref[...].astype(F32)) * pb
                        ).astype(mix_ref.dtype)

    row = pl.BlockSpec((tm, ATT_W), lambda i: (i, 0))
    wsp = pl.BlockSpec((ATT_W, D_MODEL), lambda i: (0, 0))
    wide = pl.BlockSpec((tm, D_MODEL), lambda i: (i, 0))
    return pl.pallas_call(
        body, name="mix_fwd", grid=(SEQ // tm,),
        in_specs=[row, row, wsp, wsp, pl.BlockSpec((tm, D_MODEL), lambda i: (i, 3)),
                  pl.BlockSpec((tm, D_MODEL), lambda i: (i, 4))],
        out_specs=[wide] * 3, out_shape=[_sds((SEQ, D_MODEL), BF16)] * 3,
        compiler_params=_params("parallel"),
    )(ya, yb, w_oa, w_ob, zm, zm)


def _gate_bwd(dmix, zm, p, gate_block, dz, *, name, tm=512):
    def body(dm_ref, g_ref, p_ref, *rest):
        dp_ref, dz_ref = rest[-2], rest[-1]
        dm = dm_ref[...].astype(F32)
        s = _sigmoid(g_ref[...].astype(F32))
        dp_ref[...] = (dm * s).astype(dp_ref.dtype)
        dz_ref[...] = (dm * p_ref[...].astype(F32) * s * (1.0 - s)).astype(dz_ref.dtype)

    wide = pl.BlockSpec((tm, D_MODEL), lambda i: (i, 0))
    gate = pl.BlockSpec((tm, D_MODEL), lambda i: (i, gate_block))
    extra = [] if dz is None else [dz]
    return pl.pallas_call(
        body, name=name, grid=(SEQ // tm,),
        in_specs=[wide, gate, wide] + [ANY] * len(extra),
        out_specs=[wide, gate],
        out_shape=[_sds((SEQ, D_MODEL), BF16), _sds((SEQ, Z_MAIN), BF16)],
        input_output_aliases={3: 1} if extra else {},
        compiler_params=_params("parallel"),
    )(dmix, zm, p, *extra)


def _out_fwd(mixed, w_out, x, g_post, g_pre, *, tm=512):
    def body(m_ref, w_ref, x_ref, gp_ref, gn_ref, y_ref, x2_ref, h_ref):
        y = jnp.dot(m_ref[...], w_ref[...], preferred_element_type=F32)
        y_ref[...] = y
        r = lax.rsqrt(jnp.mean(y * y, axis=-1, keepdims=True) + RMS_EPS)
        x2 = x_ref[...] + y * r * gp_ref[...]
        x2_ref[...] = x2
        r2 = lax.rsqrt(jnp.mean(x2 * x2, axis=-1, keepdims=True) + RMS_EPS)
        h_ref[...] = (x2 * r2 * gn_ref[...]).astype(h_ref.dtype)

    row = pl.BlockSpec((tm, D_MODEL), lambda i: (i, 0))
    vec = pl.BlockSpec((1, D_MODEL), lambda i: (0, 0))
    return pl.pallas_call(
        body, name="out_fwd", grid=(SEQ // tm,),
        in_specs=[row, pl.BlockSpec((D_MODEL, D_MODEL), lambda i: (0, 0)), row, vec, vec],
        out_specs=[row] * 3,
        out_shape=[_sds((SEQ, D_MODEL), F32), _sds((SEQ, D_MODEL), F32), _sds((SEQ, D_MODEL), BF16)],
        compiler_params=_params("parallel"),
    )(mixed, w_out, x, g_post, g_pre)


FFN_TM = 512
FFN_HALF = 256
FFN_TN = 2 * FFN_HALF
FFN_NJ = D_FF // FFN_HALF
FFN_GROUP = 2 * SUBLANE


def _ffn_interleave(t):
    lead = t.shape[:-1]
    return jnp.swapaxes(t.reshape(*lead, 2, FFN_NJ, FFN_HALF), -3, -2).reshape(*lead, 2 * D_FF)


def _ffn_deinterleave(t):
    lead = t.shape[:-1]
    return jnp.swapaxes(t.reshape(*lead, FFN_NJ, 2, FFN_HALF), -3, -2).reshape(*lead, 2 * D_FF)


def _ffn_move_blocks(t, *, interleave, name):
    rows = t.shape[0]
    if interleave:
        src = lambda jb: (0, (jb % 2) * FFN_NJ + jb // 2)
    else:
        src = lambda jb: (0, 2 * (jb % FFN_NJ) + jb // FFN_NJ)

    def body(x_ref, o_ref):
        o_ref[...] = x_ref[...]

    return pl.pallas_call(
        body, name=name, grid=(2 * FFN_NJ,),
        in_specs=[pl.BlockSpec((rows, FFN_HALF), src)],
        out_specs=pl.BlockSpec((rows, FFN_HALF), lambda jb: (0, jb)),
        out_shape=_sds(t.shape, t.dtype),
        compiler_params=_params("parallel"),
    )(t)


def _gelu_parts(a):
    c = math.sqrt(2.0 / math.pi)
    a2 = a * a
    t = jnp.tanh((c * a) * (1.0 + 0.044715 * a2))
    half_a, one_t = 0.5 * a, 1.0 + t
    gelu = half_a * one_t
    dgelu = 0.5 * one_t + half_a * (1.0 - t * t) * (c + (3.0 * 0.044715 * c) * a2)
    return gelu, dgelu


def _row_masks(down):
    row = lax.broadcasted_iota(jnp.int32, (SUBLANE, FFN_TN), 0)
    return (row < 1, row < 2) if down else (row >= SUBLANE - 1, row >= SUBLANE - 2)


def _rolled(x, down):
    return (pltpu.roll(x, 1, 0), pltpu.roll(x, 2, 0)) if down else (
        pltpu.roll(x, SUBLANE - 1, 0), pltpu.roll(x, SUBLANE - 2, 0))


def _shifted(cur_rolled, neighbour_rolled, masks):
    return (jnp.where(masks[0], neighbour_rolled[0], cur_rolled[0]),
            jnp.where(masks[1], neighbour_rolled[1], cur_rolled[1]))


def _conv_consts(w_ref, b_ref):
    shape = (SUBLANE, FFN_TN)
    return [jnp.broadcast_to(w_ref[k:k + 1, :], shape) for k in range(3)] + [jnp.broadcast_to(b_ref[...], shape)]


def _ffn_mid_fwd(u, conv_w, conv_b):
    per = FFN_TM // SUBLANE

    def body(u_ref, h_ref, w_ref, b_ref, m_ref, ab_ref):
        live = (pl.program_id(1) > 0).astype(F32)
        w0, w1, w2, bias = _conv_consts(w_ref, b_ref)
        masks = _row_masks(True)

        def group(g, above):
            rows = pl.ds(pl.multiple_of(g * FFN_GROUP, FFN_GROUP), FFN_GROUP)
            x = u_ref[rows, :].astype(F32)
            convs = []
            for c in range(2):
                cur = x[c * SUBLANE:(c + 1) * SUBLANE]
                cur_rolled = _rolled(cur, True)
                s1, s2 = _shifted(cur_rolled, above, masks)
                convs.append(w0 * s2 + w1 * s1 + w2 * cur + bias)
                above = cur_rolled
            y = jnp.concatenate(convs, axis=0)
            ab_ref[rows, :] = y.astype(ab_ref.dtype)
            m_ref[rows, :] = (_gelu_parts(y[:, :FFN_HALF])[0] * y[:, FFN_HALF:]).astype(m_ref.dtype)
            return above

        lax.fori_loop(0, FFN_TM // (2 * FFN_GROUP), lambda g2, carry: group(2 * g2 + 1, group(2 * g2, carry)),
                      _rolled(h_ref[...].astype(F32) * live, True))

    blk = pl.BlockSpec((FFN_TM, FFN_TN), lambda j, i: (i, j))
    return pl.pallas_call(
        body, name="ffn_mid_fwd", grid=(FFN_NJ, SEQ // FFN_TM),
        in_specs=[blk, pl.BlockSpec((SUBLANE, FFN_TN), lambda j, i: (jnp.maximum(i * per - 1, 0), j)),
                  pl.BlockSpec((3, FFN_TN), lambda j, i: (0, j)), pl.BlockSpec((1, FFN_TN), lambda j, i: (0, j))],
        out_specs=[pl.BlockSpec((FFN_TM, FFN_HALF), lambda j, i: (i, j)), blk],
        out_shape=[_sds((SEQ, D_FF), BF16), _sds((SEQ, 2 * D_FF), BF16)],
        compiler_params=_params("parallel", "arbitrary"),
    )(u, u, conv_w, conv_b)


def _ffn_mid_bwd(dm, u, ab, conv_w):
    nrow = SEQ // FFN_TM
    n_groups = FFN_TM // FFN_GROUP

    def body(dm_ref, u_ref, ab_ref, w_ref, du_ref, gw_ref, gb_ref, c_s):
        @pl.when(pl.program_id(1) == 0)
        def _():
            c_s[...] = jnp.zeros_like(c_s)
            gw_ref[...] = jnp.zeros_like(gw_ref)
            gb_ref[...] = jnp.zeros_like(gb_ref)

        taps = [jnp.broadcast_to(w_ref[k:k + 1, :], (SUBLANE, FFN_TN)) for k in range(3)]
        masks = _row_masks(False)

        def group(t, carry):
            below, acc = carry
            rows = pl.ds(pl.multiple_of((n_groups - 1 - t) * FFN_GROUP, FFN_GROUP), FFN_GROUP)
            x, y, dmv = u_ref[rows, :].astype(F32), ab_ref[rows, :].astype(F32), dm_ref[rows, :].astype(F32)
            gelu, dgelu = _gelu_parts(y[:, :FFN_HALF])
            d = jnp.concatenate([dmv * y[:, FFN_HALF:] * dgelu, dmv * gelu], axis=1)
            acc, pre = list(acc), [None, None]
            for c in (1, 0):
                sl = slice(c * SUBLANE, (c + 1) * SUBLANE)
                cur, xs = d[sl], x[sl]
                cur_rolled = _rolled(cur, False)
                up1, up2 = _shifted(cur_rolled, below, masks)
                acc = [acc[0] + up2 * xs, acc[1] + up1 * xs, acc[2] + cur * xs, acc[3] + cur]
                pre[c] = taps[2] * cur + taps[1] * up1 + taps[0] * up2
                below = cur_rolled
            du_ref[rows, :] = jnp.concatenate(pre, axis=0).astype(du_ref.dtype)
            return below, tuple(acc)

        zeros = jnp.zeros((SUBLANE, FFN_TN), F32)
        below, acc = lax.fori_loop(0, n_groups // 2, lambda t2, carry: group(2 * t2 + 1, group(2 * t2, carry)),
                                   (_rolled(c_s[...], False), (zeros,) * 4))
        c_s[...] = pltpu.roll(below[0], 1, 0)
        for k in range(3):
            gw_ref[k:k + 1, :] += jnp.sum(acc[k], axis=0, keepdims=True)
        gb_ref[...] += jnp.sum(acc[3], axis=0, keepdims=True)

    blk = pl.BlockSpec((FFN_TM, FFN_TN), lambda j, i: (nrow - 1 - i, j))
    return pl.pallas_call(
        body, name="ffn_mid_bwd", grid=(FFN_NJ, nrow),
        in_specs=[pl.BlockSpec((FFN_TM, FFN_HALF), lambda j, i: (nrow - 1 - i, j)), blk, blk,
                  pl.BlockSpec((3, FFN_TN), lambda j, i: (0, j))],
        out_specs=[blk, pl.BlockSpec((3, FFN_TN), lambda j, i: (0, j)), pl.BlockSpec((1, FFN_TN), lambda j, i: (0, j))],
        out_shape=[_sds((SEQ, 2 * D_FF), BF16), _sds((3, 2 * D_FF), F32), _sds((1, 2 * D_FF), F32)],
        scratch_shapes=[pltpu.VMEM((SUBLANE, FFN_TN), F32)],
        compiler_params=_params("parallel", "arbitrary"),
    )(dm, u, ab, conv_w)


def _down_fwd(m, w_down, x2, g_post, target, *, tm=512):
    def body(m_ref, w_ref, x2_ref, g_ref, t_ref, dout_ref, dy_ref, gg_ref, loss_ref):
        @pl.when(pl.program_id(0) == 0)
        def _():
            gg_ref[...] = jnp.zeros_like(gg_ref)
            loss_ref[...] = jnp.zeros_like(loss_ref)

        y = jnp.dot(m_ref[...], w_ref[...], preferred_element_type=F32)
        r = lax.rsqrt(jnp.mean(y * y, axis=-1, keepdims=True) + RMS_EPS)
        yn = y * r
        diff = (x2_ref[...] + yn * g_ref[...]) - t_ref[...]
        loss_ref[...] += jnp.sum(diff * diff)
        dout = diff * (1.0 / D_MODEL)
        dout_ref[...] = dout
        gg_ref[...] += jnp.sum(dout * yn, axis=0, keepdims=True)
        dn = dout * g_ref[...]
        dy_ref[...] = (r * (dn - yn * jnp.mean(dn * yn, axis=-1, keepdims=True))).astype(dy_ref.dtype)

    row = pl.BlockSpec((tm, D_MODEL), lambda i: (i, 0))
    vec = pl.BlockSpec((1, D_MODEL), lambda i: (0, 0))
    return pl.pallas_call(
        body, name="down_fwd", grid=(SEQ // tm,),
        in_specs=[pl.BlockSpec((tm, D_FF), lambda i: (i, 0)), pl.BlockSpec((D_FF, D_MODEL), lambda i: (0, 0)),
                  row, vec, row],
        out_specs=[row, row, vec, pl.BlockSpec((1, LANE), lambda i: (0, 0))],
        out_shape=[_sds((SEQ, D_MODEL), F32), _sds((SEQ, D_MODEL), BF16), _sds((1, D_MODEL), F32),
                   _sds((1, LANE), F32)],
        compiler_params=_params("arbitrary"),
    )(m, w_down, x2, g_post, target)


def _local_step(x, target, w_main, w_f, b_forget, conv_b, g_pre_mix, g_post_mix, g_pre_ffn, g_post_ffn,
                late_weights, ffn_grads_ready, proj_grads_ready, mixer_grads_ready):
    mm = _matmul
    tabs = _rope_tables()

    h1 = _rms_fwd(x, g_pre_mix, name="rms_pre_mix")
    zm = mm(h1, w_main, out_dtype=BF16, tm=2048, tn=512, tk=1024, name="in_proj")
    zf = mm(h1, w_f, out_dtype=F32, tm=2048, tn=F_PAD, tk=1024, name="in_proj_forget")
    f_row, sg_row = _fox_prep(zf[:, :N_HEADS].T, b_forget.reshape(N_HEADS, 1))
    f_cols = jnp.pad(f_row.T, ((0, 0), (0, LANE - N_HEADS)))
    q_slots, k_slots, v_slots = _fox_pack_fwd(zm, f_cols)
    ya, lse_a = _fox_fwd(q_slots, k_slots, v_slots)
    qkv_d = _rope_fwd(zm, tabs)
    dil = [_dil_fwd(qkv_d, d) for _, d in DIL_PATTERNS]
    yb, lse_b = _dil_merge([o for o, _ in dil], [l for _, l in dil])
    w_oa, w_ob, w_out, w_up, conv_w, w_down = late_weights(yb)
    pa, pb, mixed = _mix_fwd(ya, yb, w_oa, w_ob, zm)
    y1, x2, h2 = _out_fwd(mixed, w_out, x, g_post_mix, g_pre_ffn)
    u = mm(h2, w_up, out_dtype=BF16, tm=2048, tn=512, tk=1024, name="up_proj")
    m, ab = _ffn_mid_fwd(u, conv_w, _ffn_interleave(conv_b))
    dout, dy2, gg_post_ffn, sq_err = _down_fwd(m, w_down, x2, g_post_ffn, target)

    g_w_down = mm(m, dy2, ta=True, out_dtype=BF16, tm=D_FF // 2, tn=1024, tk=2048, name="grad_w_down")
    dm = mm(dy2, w_down, tb=True, out_dtype=BF16, tm=2048, tn=D_FF // 2, tk=1024, name="d_ffn_mid")
    du, g_conv_w, g_conv_b = _ffn_mid_bwd(dm, u, ab, conv_w)
    g_w_up = mm(h2, du, ta=True, out_dtype=BF16, tm=1024, tn=D_FF // 2, tk=2048, name="grad_w_up")
    tok = ffn_grads_ready(dict(w_down=g_w_down, w_up=_ffn_move_blocks(g_w_up, interleave=False, name="grad_w_up_cols"),
                               conv_w=_ffn_deinterleave(g_conv_w)))
    dh2 = mm(du, w_up, tb=True, out_dtype=BF16, tm=512, tn=1024, tk=2 * D_FF, name="d_h2")

    dx2, dy1, gg_pre_ffn, gg_post_mix = _rms_pair_bwd([dh2], x2, g_pre_ffn, dout, y1, g_post_mix + tok)
    g_w_out = mm(mixed, dy1, ta=True, out_dtype=BF16, tm=1024, tn=1024, tk=2048, name="grad_w_out")
    dmix = mm(dy1, w_out, tb=True, out_dtype=BF16, tm=2048, tn=1024, tk=1024, name="d_mixed")
    dpa, dz = _gate_bwd(dmix, zm, pa, 3, None, name="gate_bwd_fox")
    dpb, dz = _gate_bwd(dmix, zm, pb, 4, dz, name="gate_bwd_dil")
    g_w_oa = mm(ya, dpa, ta=True, out_dtype=BF16, tm=512, tn=1024, tk=SEQ, name="grad_w_o_fox")
    g_w_ob = mm(yb, dpb, ta=True, out_dtype=BF16, tm=512, tn=1024, tk=SEQ, name="grad_w_o_dil")
    tok = proj_grads_ready(dict(w_o_fox=g_w_oa, w_o_dil=g_w_ob, w_out=g_w_out))
    dya = mm(dpa, w_oa, tb=True, out_dtype=BF16, tm=2048, tn=512, tk=1024, name="d_y_fox")
    dyb = mm(dpb, w_ob, tb=True, out_dtype=BF16, tm=2048, tn=512, tk=1024, name="d_y_dil")

    qb_slots, do_slots = _fox_pack_bwd(zm, f_cols + tok, lse_a, ya, dya)
    dz, df_cols = _fox_unpack(*_fox_bwd(qb_slots, k_slots, v_slots, do_slots), dz)
    dfa_t, g_b_forget = _fox_post_bwd(df_cols[:, :N_HEADS].T, sg_row)

    delta_b = _attn_delta(yb, dyb, name="delta_dil")
    dil_g = [_dil_bwd(qkv_d, lse_b, delta_b, dyb, d) for _, d in DIL_PATTERNS]
    dz = _dil_grad_combine([g[0] for g in dil_g], [g[1] for g in dil_g], [g[2] for g in dil_g], tabs, dz)

    dzf = jnp.pad(dfa_t.T, ((0, 0), (0, F_PAD - N_HEADS)))
    g_w_main = mm(h1, dz, ta=True, out_dtype=BF16, tm=1024, tn=Z_MAIN // 4, tk=2048, name="grad_w_in")
    g_w_f = mm(h1, dzf, ta=True, out_dtype=BF16, tm=1024, tn=F_PAD, tk=1024, name="grad_w_in_forget")
    tok = mixer_grads_ready(dict(w_main=g_w_main, w_f=g_w_f))
    dh1 = [mm(dz, w_main, tb=True, out_dtype=BF16, tm=512, tn=1024, tk=Z_MAIN, name="d_h1"),
           mm(dzf + tok, w_f, tb=True, out_dtype=F32, tm=2048, tn=1024, tk=F_PAD, name="d_h1_forget")]
    grad_x, gg_pre_mix = _rms_bwd(dh1, x, g_pre_mix, dx2, out_dtype=F32, name="rms_pre_mix_bwd")

    grads = dict(
        b_forget=g_b_forget.reshape(1, N_HEADS), conv_b=_ffn_deinterleave(g_conv_b),
        g_pre_mix=gg_pre_mix, g_post_mix=gg_post_mix, g_pre_ffn=gg_pre_ffn, g_post_ffn=gg_post_ffn)
    return sq_err, grad_x, grads


def _exchange(arrays, scatter, *, name):
    n = len(arrays)

    def body(*refs):
        ins, outs = refs[:n], refs[n:2 * n]
        send_sems, recv_sems, local_sems = refs[2 * n:]
        x, y, c = lax.axis_index("x"), lax.axis_index("y"), lax.axis_index("c")
        me = 4 * x + 2 * y + c
        peers = []
        for k in range(1, N_DEV):
            px = 1 - x if k & 4 else x
            py = 1 - y if k & 2 else y
            pc = 1 - c if k & 1 else c
            peers.append(((px, py, pc), 4 * px + 2 * py + pc))

        def remote(a, k):
            dev, slot = peers[k]
            return pltpu.make_async_remote_copy(
                src_ref=ins[a].at[slot] if scatter else ins[a], dst_ref=outs[a].at[me],
                send_sem=send_sems.at[a, k], recv_sem=recv_sems.at[a, k],
                device_id=dev, device_id_type=MESH_ID)

        def landed(a, k):
            dev, slot = peers[k]
            return pltpu.make_async_remote_copy(
                src_ref=outs[a].at[slot], dst_ref=outs[a].at[slot],
                send_sem=send_sems.at[a, k], recv_sem=recv_sems.at[a, k],
                device_id=dev, device_id_type=MESH_ID)

        own = [pltpu.make_async_copy(ins[a].at[me] if scatter else ins[a], outs[a].at[me], local_sems.at[a])
               for a in range(n)]
        copies = [remote(a, k) for k in range(N_DEV - 1) for a in range(n)]
        for cp in own + copies:
            cp.start()
        for k in range(N_DEV - 1):
            for a in range(n):
                landed(a, k).wait_recv()
        for cp in copies:
            cp.wait_send()
        for cp in own:
            cp.wait()

    out_shape = [_sds(((N_DEV,) + a.shape[-2:]), a.dtype) for a in arrays]
    return pl.pallas_call(
        body, name=name, in_specs=[ANY] * n, out_specs=[ANY] * n, out_shape=out_shape,
        scratch_shapes=[pltpu.SemaphoreType.DMA((n, N_DEV - 1)), pltpu.SemaphoreType.DMA((n, N_DEV - 1)),
                        pltpu.SemaphoreType.DMA((n,))],
    )(*arrays)


def _gather_two_level(shard, *, name):
    def body(x_ref, out_ref, send_sems, recv_sems, local_sem):
        x, y, c = lax.axis_index("x"), lax.axis_index("y"), lax.axis_index("c")
        me, sibling = (x, y, c), (x, y, 1 - c)
        chips = [(1 - x, y), (x, 1 - y), (1 - x, 1 - y)]

        def slot(px, py, pc):
            return out_ref.at[4 * px + 2 * py + pc]

        def copy(k, block, to, src=None):
            return pltpu.make_async_remote_copy(
                src_ref=slot(*block) if src is None else src, dst_ref=slot(*block),
                send_sem=send_sems.at[k], recv_sem=recv_sems.at[k], device_id=to, device_id_type=MESH_ID)

        mine = pltpu.make_async_copy(x_ref, slot(*me), local_sem)
        mine.start()
        first = [copy(0, me, sibling, src=x_ref)]
        first += [copy(1 + j, me, (*chip, c), src=x_ref) for j, chip in enumerate(chips)]
        for cp in first:
            cp.start()
        passed = [copy(4 + j, (*chip, c), sibling) for j, chip in enumerate(chips)]
        for j, chip in enumerate(chips):
            copy(1 + j, (*chip, c), me).wait_recv()
            passed[j].start()
        copy(0, sibling, me).wait_recv()
        for j, chip in enumerate(chips):
            copy(4 + j, (*chip, 1 - c), me).wait_recv()
        for cp in first + passed:
            cp.wait_send()
        mine.wait()

    return pl.pallas_call(
        body, name=name, in_specs=[ANY], out_specs=ANY, out_shape=_sds((N_DEV,) + shard.shape, shard.dtype),
        scratch_shapes=[pltpu.SemaphoreType.DMA((N_DEV - 1,)), pltpu.SemaphoreType.DMA((N_DEV - 1,)),
                        pltpu.SemaphoreType.DMA],
    )(shard)


N_CHIPS = N_DEV // 2


def _peers(chips_only=False):
    x, y, c = lax.axis_index("x"), lax.axis_index("y"), lax.axis_index("c")
    out = []
    if chips_only:
        for k in range(1, N_CHIPS):
            px = 1 - x if k & 2 else x
            py = 1 - y if k & 1 else y
            out.append(((px, py, c), 2 * px + py))
        return 2 * x + y, out
    for k in range(1, N_DEV):
        px = 1 - x if k & 4 else x
        py = 1 - y if k & 2 else y
        pc = 1 - c if k & 1 else c
        out.append(((px, py, pc), 4 * px + 2 * py + pc))
    return 4 * x + 2 * y + c, out


def _sibling_swap(slot_arrays, *, name):
    n = len(slot_arrays)

    def body(*refs):
        ins, outs, send_sems, recv_sems = refs[:n], refs[n:2 * n], refs[2 * n], refs[2 * n + 1]
        x, y, c = lax.axis_index("x"), lax.axis_index("y"), lax.axis_index("c")
        copies = [pltpu.make_async_remote_copy(
            src_ref=ins[a].at[2 * q + (1 - c)], dst_ref=outs[a].at[q], send_sem=send_sems.at[a, q],
            recv_sem=recv_sems.at[a, q], device_id=(x, y, 1 - c), device_id_type=MESH_ID)
            for a in range(n) for q in range(N_CHIPS)]
        for cp in copies:
            cp.start()
        for cp in copies:
            cp.wait_recv()
        for cp in copies:
            cp.wait_send()

    return pl.pallas_call(
        body, name=name, in_specs=[ANY] * n, out_specs=[ANY] * n,
        out_shape=[_sds((N_CHIPS,) + t.shape[1:], t.dtype) for t in slot_arrays],
        scratch_shapes=[pltpu.SemaphoreType.DMA((n, N_CHIPS)), pltpu.SemaphoreType.DMA((n, N_CHIPS))],
    )(*slot_arrays)


def _pair_sum(slots, from_sibling, *, name, tn):
    _, r, c = slots.shape
    core = lax.axis_index("c").astype(jnp.int32).reshape(1)

    def body(core_ref, a_ref, b_ref, o_ref):
        o_ref[...] = (a_ref[...].astype(F32) + b_ref[...].astype(F32)).astype(o_ref.dtype)

    blk = lambda f: pl.BlockSpec((1, r, tn), f)
    return pl.pallas_call(
        body, name=name,
        grid_spec=pltpu.PrefetchScalarGridSpec(
            num_scalar_prefetch=1, grid=(N_CHIPS, c // tn),
            in_specs=[blk(lambda q, j, core: (2 * q + core[0], 0, j)), blk(lambda q, j, core: (q, 0, j))],
            out_specs=blk(lambda q, j, core: (q, 0, j))),
        out_shape=_sds((N_CHIPS, r, c), slots.dtype),
        compiler_params=_params("parallel", "parallel"),
    )(core, slots, from_sibling)


def _sum_parts(parts, *, name, tn):
    n, r, c = parts.shape

    def body(p_ref, o_ref):
        total = p_ref[0].astype(F32)
        for s in range(1, n):
            total = total + p_ref[s].astype(F32)
        o_ref[...] = total

    return pl.pallas_call(
        body, name=name, grid=(c // tn,),
        in_specs=[pl.BlockSpec((n, r, tn), lambda j: (0, 0, j))],
        out_specs=pl.BlockSpec((r, tn), lambda j: (0, j)), out_shape=_sds((r, c), F32),
        compiler_params=_params("parallel"),
    )(parts)


HBM = pl.BlockSpec(memory_space=pltpu.HBM)
SEM = pl.BlockSpec(memory_space=pltpu.SEMAPHORE)
DATAFLOW = pltpu.SideEffectType.DATAFLOW_SIDE_EFFECTING


def _split_copy(srcs, lands, send_sems, recv_sems, scatter, a, k, me, peers, incoming=False):
    dev, slot = peers[k]
    if incoming:
        src = dst = lands[a].at[slot]
    else:
        src, dst = (srcs[a].at[slot] if scatter else srcs[a]), lands[a].at[me]
    sem = a * len(peers) + k
    return pltpu.make_async_remote_copy(
        src_ref=src, dst_ref=dst, send_sem=send_sems.at[sem], recv_sem=recv_sems.at[sem],
        device_id=dev, device_id_type=MESH_ID)


def _exchange_start(arrays, scatter, *, name, chips_only=False):
    n = len(arrays)
    n_slots = N_CHIPS if chips_only else N_DEV

    def body(*refs):
        srcs, lands = refs[:n], refs[n:2 * n]
        send_sems, recv_sems = refs[2 * n], refs[2 * n + 1]
        token = refs[-1]
        me, peers = _peers(chips_only)
        for k in range(len(peers)):
            for a in range(n):
                _split_copy(srcs, lands, send_sems, recv_sems, scatter, a, k, me, peers).start()
        token[...] = jnp.zeros_like(token)

    land_shapes = [((n_slots,) + a.shape[-2:], a.dtype) for a in arrays]
    sems = pltpu.SemaphoreType.DMA((n * (n_slots - 1),))
    outs = pl.pallas_call(
        body, name=name,
        out_shape=(sems, sems, *[pltpu.HBM(a.shape, a.dtype) for a in arrays],
                   *[pltpu.HBM(s, d) for s, d in land_shapes], _sds((SUBLANE, LANE), F32)),
        in_specs=[HBM] * (2 * n),
        out_specs=(SEM, SEM, *[HBM] * (2 * n), pl.BlockSpec(memory_space=pltpu.VMEM)),
        input_output_aliases={i: 2 + i for i in range(2 * n)},
        compiler_params=pltpu.CompilerParams(has_side_effects=DATAFLOW),
    )(*[pltpu.with_memory_space_constraint(a, pltpu.HBM) for a in arrays],
      *[pltpu.with_memory_space_constraint(lax.empty(s, d), pltpu.HBM) for s, d in land_shapes])
    return (outs[0], outs[1], outs[2:2 + n], outs[2 + n:2 + 2 * n], scatter, chips_only), outs[-1]


def _exchange_wait(handles, after, *, name):
    send_sems, recv_sems, srcs, lands, scatter, chips_only = handles
    n = len(srcs)

    def body(*refs):
        src_refs, land_refs = refs[:n], refs[n:2 * n]
        send_ref, recv_ref = refs[2 * n], refs[2 * n + 1]
        me, peers = _peers(chips_only)
        for k in range(len(peers)):
            for a in range(n):
                _split_copy(src_refs, land_refs, send_ref, recv_ref, scatter, a, k, me, peers).wait_send()
                _split_copy(src_refs, land_refs, send_ref, recv_ref, scatter, a, k, me, peers, True).wait_recv()

    outs = pl.pallas_call(
        body, name=name,
        out_shape=tuple(pltpu.HBM(t.shape, t.dtype) for t in (*srcs, *lands)),
        in_specs=[HBM] * (2 * n) + [SEM, SEM, pl.BlockSpec(memory_space=pl.ANY)],
        out_specs=tuple([HBM] * (2 * n)),
        input_output_aliases={i: i for i in range(2 * n)},
        compiler_params=pltpu.CompilerParams(has_side_effects=DATAFLOW),
    )(*srcs, *lands, send_sems, recv_sems, after)
    return _with_own_slot(outs[n:], outs[:n], scatter, chips_only)


def _with_own_slot(landed, own, scatter, chips_only):
    me = 2 * lax.axis_index("x") + lax.axis_index("y")
    if not chips_only:
        me = 2 * me + lax.axis_index("c")
    out = []
    for buf, src in zip(landed, own):
        mine = lax.dynamic_index_in_dim(src, me, 0, keepdims=False) if scatter else src
        out.append(lax.dynamic_update_index_in_dim(buf, mine, me, 0))
    return out


def _adamw(parts, w, m, v, *, name, tm):
    r, c = w.shape
    assert r % tm == 0

    def body(p_ref, w_ref, m_ref, v_ref, g_ref, d_ref, nm_ref, nv_ref):
        _adamw_update(p_ref, w_ref, m_ref, v_ref, g_ref, d_ref, nm_ref, nv_ref)

    blk = pl.BlockSpec((tm, c), lambda i: (i, 0))
    return pl.pallas_call(
        body, name=name, grid=(r // tm,),
        in_specs=[pl.BlockSpec((parts.shape[0], tm, c), lambda i: (0, i, 0)), blk, blk, blk],
        out_specs=[blk] * 4, out_shape=[_sds((r, c), F32)] * 4,
        compiler_params=_params("parallel"),
    )(parts, w, m, v)


def _adamw_update(p_ref, w_ref, m_ref, v_ref, g_ref, d_ref, nm_ref, nv_ref):
    g = p_ref[0].astype(F32)
    for s in range(1, p_ref.shape[0]):
        g = g + p_ref[s].astype(F32)
    g_ref[...] = g
    m_new = ADAM_B1 * m_ref[...] + (1.0 - ADAM_B1) * g
    v_new = ADAM_B2 * v_ref[...] + (1.0 - ADAM_B2) * (g * g)
    nm_ref[...] = m_new
    nv_ref[...] = v_new
    m_hat = m_new / (1.0 - ADAM_B1 ** ADAM_STEP)
    v_hat = v_new / (1.0 - ADAM_B2 ** ADAM_STEP)
    d_ref[...] = -ADAM_LR * (m_hat / (jnp.sqrt(v_hat) + ADAM_EPS) + ADAM_WD * w_ref[...])


SMALL = ("g_pre_mix", "b_forget", "g_post_mix", "g_pre_ffn", "conv_b", "g_post_ffn")


def _adamw_small(parts, ws, ms, vs, sq_err_parts):
    n = len(ws)

    def body(*refs):
        ins, sq_ref, outs, loss_ref = refs[:4 * n], refs[4 * n], refs[4 * n + 1:-1], refs[-1]
        for i in range(n):
            _adamw_update(ins[i], ins[n + i], ins[2 * n + i], ins[3 * n + i], *outs[4 * i:4 * i + 4])
        total = sq_ref[0]
        for s in range(1, N_DEV):
            total = total + sq_ref[s]
        loss_ref[...] = total * (0.5 / D_MODEL)

    res = pl.pallas_call(
        body, name="adamw_small",
        out_shape=[_sds(w.shape, F32) for w in ws for _ in range(4)] + [_sds((1, LANE), F32)],
        compiler_params=pltpu.CompilerParams(vmem_limit_bytes=VMEM_LIMIT),
    )(*parts, *ws, *ms, *vs, sq_err_parts)
    return [res[4 * i:4 * i + 4] for i in range(n)], res[-1][0, 0]


def kernel(x, g_pre_mix, w_in, b_forget, w_o_fox, w_o_dil, w_out, g_post_mix, g_pre_ffn, w_up, conv_w, conv_b, w_down, g_post_ffn, loss_target, m_g_pre_mix, m_w_in, m_b_forget, m_w_o_fox, m_w_o_dil, m_w_out, m_g_post_mix, m_g_pre_ffn, m_w_up, m_conv_w, m_conv_b, m_w_down, m_g_post_ffn, v_g_pre_mix, v_w_in, v_b_forget, v_w_o_fox, v_w_o_dil, v_w_out, v_g_post_mix, v_g_pre_ffn, v_w_up, v_conv_w, v_conv_b, v_w_down, v_g_post_ffn):
    names = ("g_pre_mix", "w_in", "b_forget", "w_o_fox", "w_o_dil", "w_out", "g_post_mix", "g_pre_ffn",
             "w_up", "conv_w", "conv_b", "w_down", "g_post_ffn")
    w = dict(g_pre_mix=g_pre_mix, w_in=w_in, b_forget=b_forget, w_o_fox=w_o_fox, w_o_dil=w_o_dil, w_out=w_out,
             g_post_mix=g_post_mix, g_pre_ffn=g_pre_ffn, w_up=w_up, conv_w=conv_w, conv_b=conv_b, w_down=w_down,
             g_post_ffn=g_post_ffn)
    m = dict(g_pre_mix=m_g_pre_mix, w_in=m_w_in, b_forget=m_b_forget, w_o_fox=m_w_o_fox, w_o_dil=m_w_o_dil,
             w_out=m_w_out, g_post_mix=m_g_post_mix, g_pre_ffn=m_g_pre_ffn, w_up=m_w_up, conv_w=m_conv_w,
             conv_b=m_conv_b, w_down=m_w_down, g_post_ffn=m_g_post_ffn)
    v = dict(g_pre_mix=v_g_pre_mix, w_in=v_w_in, b_forget=v_b_forget, w_o_fox=v_w_o_fox, w_o_dil=v_w_o_dil,
             w_out=v_w_out, g_post_mix=v_g_post_mix, g_pre_ffn=v_g_pre_ffn, w_up=v_w_up, conv_w=v_conv_w,
             conv_b=v_conv_b, w_down=v_w_down, g_post_ffn=v_g_post_ffn)
    sharded = ("w_in", "w_o_fox", "w_o_dil", "w_out", "w_up", "w_down", "conv_w")
    wire = lambda n: F32 if n == "conv_w" else BF16

    by_cols = lambda t: jnp.transpose(t, (1, 0, 2)).reshape(t.shape[1], N_DEV * t.shape[2])
    by_rows = lambda t: t.reshape(N_DEV * t.shape[1], t.shape[2])
    col_slots = lambda t: jnp.transpose(t.reshape(t.shape[0], N_DEV, t.shape[1] // N_DEV), (1, 0, 2))
    row_slots = lambda t: t.reshape(N_DEV, t.shape[0] // N_DEV, t.shape[1])
    to_slots = lambda n, t: (row_slots if n in ("w_out", "w_down") else col_slots)(t).astype(wire(n))
    shard = lambda n: w[n][0].astype(wire(n))
    f_lo, f_hi = 3 * ATT_W, 3 * ATT_W + N_HEADS

    w_in_full = by_cols(_gather_two_level(shard("w_in"), name="gather_w_in"))
    w_main = jnp.concatenate([w_in_full[:, :f_lo], w_in_full[:, f_hi:]], axis=1)
    w_f = jnp.pad(w_in_full[:, f_lo:f_hi], ((0, 0), (0, F_PAD - N_HEADS)))
    late = ("w_o_fox", "w_o_dil", "w_out", "w_up", "conv_w", "w_down")
    order = jnp.minimum(jnp.abs(w_in_full[0, 0].astype(F32)), 0.0)
    late_handles, late_tok = _exchange_start(
        [shard(n) + order.astype(wire(n)) if n == "conv_w" else shard(n) for n in late], False,
        name="gather_late_start")

    def late_weights(after):
        got = dict(zip(late, _exchange_wait(late_handles, after, name="gather_late_wait")))
        return (by_cols(got["w_o_fox"]), by_cols(got["w_o_dil"]), by_rows(got["w_out"]),
                _ffn_move_blocks(by_cols(got["w_up"]), interleave=True, name="w_up_cols"),
                _ffn_interleave(by_cols(got["conv_w"])),
                by_rows(got["w_down"]))

    pending = {}

    def ffn_grads_ready(g):
        pending["ffn"] = _exchange_start([to_slots(n, g[n]) for n in ("w_down", "w_up", "conv_w")], True,
                                         name="scatter_ffn_start")
        return pending["ffn"][1][0, 0]

    def proj_grads_ready(g):
        pending["proj"] = _exchange_start([to_slots(n, g[n]) for n in ("w_o_fox", "w_o_dil", "w_out")], True,
                                          name="scatter_proj_start")
        return pending["proj"][1][0, 0]

    def mixer_grads_ready(g):
        slabs = [g["w_main"].reshape(N_DEV, D_MODEL // N_DEV, Z_MAIN), g["w_f"].reshape(N_DEV, D_MODEL // N_DEV, F_PAD)]
        theirs = _sibling_swap(slabs, name="scatter_w_in_swap")
        chip_sums = [_pair_sum(slabs[0], theirs[0], name="scatter_w_in_pair_sum", tn=Z_MAIN // 4),
                     _pair_sum(slabs[1], theirs[1], name="scatter_w_in_forget_pair_sum", tn=F_PAD)]
        pending["w_in"] = _exchange_start(chip_sums, True, name="scatter_w_in_start", chips_only=True)
        return pending["w_in"][1][0, 0]

    sq_err, grad_x, g = _local_step(
        x[0], loss_target[0], w_main, w_f, b_forget, conv_b, g_pre_mix + late_tok[0, 0], g_post_mix, g_pre_ffn,
        g_post_ffn, late_weights, ffn_grads_ready, proj_grads_ready, mixer_grads_ready)

    tiles = dict(w_in=256, w_o_fox=512, w_o_dil=512, w_out=128, w_up=256, w_down=176, conv_w=3)
    adam = lambda n, p: _adamw(p, w[n][0], m[n][0], v[n][0], name=f"adamw_{n}", tm=tiles[n])
    res = {}
    for key, group in (("ffn", ("w_down", "w_up", "conv_w")), ("proj", ("w_o_fox", "w_o_dil", "w_out"))):
        landed = _exchange_wait(pending[key][0], grad_x, name=f"scatter_{key}_wait")
        res.update({n: adam(n, p) for n, p in zip(group, landed)})
    small_parts = _exchange([g[n] for n in SMALL] + [sq_err], False, name="gather_small_grads")
    done = res["w_up"][3]
    main_parts, f_parts = _exchange_wait(pending["w_in"][0], done, name="scatter_w_in_wait")
    slab_main = _sum_parts(main_parts, name="scatter_w_in_sum", tn=Z_MAIN // 4)
    slab_f = _sum_parts(f_parts, name="scatter_w_in_forget_sum", tn=F_PAD)
    slab = jnp.concatenate([slab_main[:, :f_lo], slab_f[:, :N_HEADS], slab_main[:, f_lo:]], axis=1)
    rows = _exchange([col_slots(slab)], True, name="scatter_w_in_rows")[0]
    res["w_in"] = adam("w_in", rows.reshape(1, D_MODEL, rows.shape[-1]))
    small, loss = _adamw_small(small_parts[:-1], *[[t[n] for n in SMALL] for t in (w, m, v)], small_parts[-1])
    small = dict(zip(SMALL, small))
    out = [[(res[n][k][None] if n in sharded else small[n][k]) for n in names] for k in range(4)]
    return (loss, grad_x[None], *out[0], *out[1], *out[2], *out[3])
```

```python
import functools
import math

import jax
import jax.numpy as jnp
import numpy as np
from jax import lax
from jax.experimental import pallas as pl
from jax.experimental.pallas import tpu as pltpu

F32 = jnp.float32
BF16 = jnp.bfloat16

SEQ = 4096
D_MODEL = 1024
N_HEADS = 8
HEAD_DIM = 64
ATT_W = N_HEADS * HEAD_DIM
D_FF = 2816
Z_MAIN = 5120
F_PAD = 128
ROPE_DIM = 16
ROPE_THETA = 500000.0
RMS_EPS = 1e-6
NEG_INF = -1e30
SCALE = 1.0 / math.sqrt(HEAD_DIM)
DIL_PATTERNS = ((128, 1), (512, 4), (2048, 16))
DIL_BLK = 128
N_DEV = 8

ADAM_LR = 0.001
ADAM_B1 = 0.9
ADAM_B2 = 0.999
ADAM_EPS = 1e-08
ADAM_WD = 0.01
ADAM_STEP = 10

LANE = 128
SUBLANE = 8
VMEM_LIMIT = 56 * 1024 * 1024
MESH_ID = pl.DeviceIdType.MESH
ANY = pl.BlockSpec(memory_space=pl.ANY)


def _params(*sem):
    return pltpu.CompilerParams(dimension_semantics=sem, vmem_limit_bytes=VMEM_LIMIT)


def _sds(shape, dtype):
    return jax.ShapeDtypeStruct(shape, dtype)


def _matmul(a, b, *, ta=False, tb=False, out_dtype, tm, tn, tk, name, b_k_off=0):
    if ta:
        kk, m = a.shape
    else:
        m, kk = a.shape
    n = b.shape[0] if tb else b.shape[1]
    tm, tn, tk = min(tm, m), min(tn, n), min(tk, kk)
    assert (b.shape[1] if tb else b.shape[0]) >= b_k_off * tk + kk
    assert m % tm == 0 and n % tn == 0 and kk % tk == 0, (name, m, n, kk, tm, tn, tk)
    nk = kk // tk
    dims = (((0 if ta else 1,), (1 if tb else 0,)), ((), ()))

    def body(a_ref, b_ref, o_ref, *scratch):
        p = lax.dot_general(a_ref[...].astype(BF16), b_ref[...].astype(BF16), dims,
                            preferred_element_type=F32)
        if nk == 1:
            o_ref[...] = p.astype(o_ref.dtype)
        else:
            acc = scratch[0]
            k = pl.program_id(2)

            @pl.when(k == 0)
            def _():
                acc[...] = p

            @pl.when(k > 0)
            def _():
                acc[...] += p

            @pl.when(k == nk - 1)
            def _():
                o_ref[...] = acc[...].astype(o_ref.dtype)

    a_spec = (pl.BlockSpec((tk, tm), lambda i, j, k: (k, i)) if ta
              else pl.BlockSpec((tm, tk), lambda i, j, k: (i, k)))
    b_spec = (pl.BlockSpec((tn, tk), lambda i, j, k: (j, k + b_k_off)) if tb
              else pl.BlockSpec((tk, tn), lambda i, j, k: (k + b_k_off, j)))
    return pl.pallas_call(
        body, name=name, grid=(m // tm, n // tn, nk),
        in_specs=[a_spec, b_spec],
        out_specs=pl.BlockSpec((tm, tn), lambda i, j, k: (i, j)),
        out_shape=_sds((m, n), out_dtype),
        scratch_shapes=[pltpu.VMEM((tm, tn), F32)] if nk > 1 else [],
        compiler_params=_params("parallel", "parallel", "arbitrary"),
    )(a, b)


def _rms_fwd(x, g, *, name, tm=512):
    def body(x_ref, g_ref, h_ref):
        xv = x_ref[...]
        r = lax.rsqrt(jnp.mean(xv * xv, axis=-1, keepdims=True) + RMS_EPS)
        h_ref[...] = (xv * r * g_ref[...]).astype(h_ref.dtype)

    return pl.pallas_call(
        body, name=name, grid=(SEQ // tm,),
        in_specs=[pl.BlockSpec((tm, D_MODEL), lambda i: (i, 0)), pl.BlockSpec((1, D_MODEL), lambda i: (0, 0))],
        out_specs=pl.BlockSpec((tm, D_MODEL), lambda i: (i, 0)),
        out_shape=_sds((SEQ, D_MODEL), BF16),
        compiler_params=_params("parallel"),
    )(x, g)


def _rms_bwd(dh_parts, xin, g, dres, *, out_dtype, name, tm=512):
    n_parts = len(dh_parts)
    has_res = dres is not None

    def body(*refs):
        parts = refs[:n_parts]
        x_ref, g_ref = refs[n_parts], refs[n_parts + 1]
        res_ref = refs[n_parts + 2] if has_res else None
        o_ref, gg_ref = refs[-2], refs[-1]
        dh = parts[0][...].astype(F32)
        for p in parts[1:]:
            dh = dh + p[...].astype(F32)
        xv = x_ref[...]
        r = lax.rsqrt(jnp.mean(xv * xv, axis=-1, keepdims=True) + RMS_EPS)
        xn = xv * r

        @pl.when(pl.program_id(0) == 0)
        def _():
            gg_ref[...] = jnp.zeros_like(gg_ref)

        gg_ref[...] += jnp.sum(dh * xn, axis=0, keepdims=True)
        dxn = dh * g_ref[...]
        dx = r * (dxn - xn * jnp.mean(dxn * xn, axis=-1, keepdims=True))
        if has_res:
            dx = dx + res_ref[...]
        o_ref[...] = dx.astype(o_ref.dtype)

    row = pl.BlockSpec((tm, D_MODEL), lambda i: (i, 0))
    vec = pl.BlockSpec((1, D_MODEL), lambda i: (0, 0))
    args = list(dh_parts) + [xin, g] + ([dres] if has_res else [])
    return pl.pallas_call(
        body, name=name, grid=(SEQ // tm,),
        in_specs=[row] * n_parts + [row, vec] + ([row] if has_res else []),
        out_specs=[row, vec],
        out_shape=[_sds((SEQ, D_MODEL), out_dtype), _sds((1, D_MODEL), F32)],
        compiler_params=_params("arbitrary"),
    )(*args)


def _rms_pair_bwd(dh_parts, x2, g_pre, dres, y1, g_post, *, tm=512):
    n_parts = len(dh_parts)

    def norm_bwd(dh, xin, g_ref, gg_ref):
        r = lax.rsqrt(jnp.mean(xin * xin, axis=-1, keepdims=True) + RMS_EPS)
        xn = xin * r
        gg_ref[...] += jnp.sum(dh * xn, axis=0, keepdims=True)
        dxn = dh * g_ref[...]
        return r * (dxn - xn * jnp.mean(dxn * xn, axis=-1, keepdims=True))

    def body(*refs):
        parts = refs[:n_parts]
        x2_ref, gpre_ref, res_ref, y1_ref, gpost_ref, dx2_ref, dy1_ref, ggpre_ref, ggpost_ref = refs[n_parts:]

        @pl.when(pl.program_id(0) == 0)
        def _():
            ggpre_ref[...] = jnp.zeros_like(ggpre_ref)
            ggpost_ref[...] = jnp.zeros_like(ggpost_ref)

        dh = parts[0][...].astype(F32)
        for p in parts[1:]:
            dh = dh + p[...].astype(F32)
        dx2 = res_ref[...] + norm_bwd(dh, x2_ref[...], gpre_ref, ggpre_ref)
        dx2_ref[...] = dx2
        dy1_ref[...] = norm_bwd(dx2, y1_ref[...], gpost_ref, ggpost_ref).astype(dy1_ref.dtype)

    row = pl.BlockSpec((tm, D_MODEL), lambda i: (i, 0))
    vec = pl.BlockSpec((1, D_MODEL), lambda i: (0, 0))
    return pl.pallas_call(
        body, name="rms_pair_bwd", grid=(SEQ // tm,),
        in_specs=[row] * n_parts + [row, vec, row, row, vec],
        out_specs=[row, row, vec, vec],
        out_shape=[_sds((SEQ, D_MODEL), F32), _sds((SEQ, D_MODEL), BF16), _sds((1, D_MODEL), F32),
                   _sds((1, D_MODEL), F32)],
        compiler_params=_params("arbitrary"),
    )(*dh_parts, x2, g_pre, dres, y1, g_post)


SCAN_BLK = 512


def _split_dot(v, tri):
    hi = v.astype(BF16)
    r1 = v - hi.astype(F32)
    mid = r1.astype(BF16)
    lo = (r1 - mid.astype(F32)).astype(BF16)
    dot = functools.partial(jnp.dot, preferred_element_type=F32)
    return dot(hi, tri) + dot(mid, tri) + dot(lo, tri)


def _fox_prep(fa_t, b_col):
    nblk = SEQ // SCAN_BLK

    def body(fa_ref, b_ref, f_ref, sg_ref):
        row = lax.broadcasted_iota(jnp.int32, (SCAN_BLK, SCAN_BLK), 0)
        col = lax.broadcasted_iota(jnp.int32, (SCAN_BLK, SCAN_BLK), 1)
        upper = (row <= col).astype(BF16)
        carry = jnp.zeros((N_HEADS, 1), F32)
        for blk in range(nblk):
            sl = pl.ds(blk * SCAN_BLK, SCAN_BLK)
            xx = fa_ref[:, sl] + b_ref[...]
            e = jnp.exp(-jnp.abs(xx))
            logf = jnp.minimum(xx, 0.0) - jnp.log(1.0 + e)
            sg_ref[:, sl] = jnp.where(xx >= 0.0, e, 1.0) / (1.0 + e)
            c = _split_dot(logf, upper) + carry
            f_ref[:, sl] = c
            carry = c[:, SCAN_BLK - 1:SCAN_BLK]

    return pl.pallas_call(
        body, name="fox_prep",
        out_shape=[_sds((N_HEADS, SEQ), F32), _sds((N_HEADS, SEQ), F32)],
        compiler_params=pltpu.CompilerParams(vmem_limit_bytes=VMEM_LIMIT),
    )(fa_t, b_col)


def _fox_post_bwd(df_t, sg_t):
    nblk = SEQ // SCAN_BLK

    def body(df_ref, sg_ref, dfa_ref, gb_ref):
        row = lax.broadcasted_iota(jnp.int32, (SCAN_BLK, SCAN_BLK), 0)
        col = lax.broadcasted_iota(jnp.int32, (SCAN_BLK, SCAN_BLK), 1)
        lower = (row >= col).astype(BF16)
        carry = jnp.zeros((N_HEADS, 1), F32)
        gb = jnp.zeros((N_HEADS, 1), F32)
        for blk in reversed(range(nblk)):
            sl = pl.ds(blk * SCAN_BLK, SCAN_BLK)
            c = _split_dot(df_ref[:, sl], lower) + carry
            carry = c[:, 0:1]
            dfa = c * sg_ref[:, sl]
            dfa_ref[:, sl] = dfa
            gb = gb + jnp.sum(dfa, axis=1, keepdims=True)
        gb_ref[...] = gb

    return pl.pallas_call(
        body, name="fox_post_bwd",
        out_shape=[_sds((N_HEADS, SEQ), F32), _sds((N_HEADS, 1), F32)],
        compiler_params=pltpu.CompilerParams(vmem_limit_bytes=VMEM_LIMIT),
    )(df_t, sg_t)


FOX_T = 512
NT_DIMS = (((1,), (1,)), ((), ()))
TN_DIMS = (((0,), (0,)), ((), ()))


def _head(ref_or_val, h):
    return ref_or_val[:, h * HEAD_DIM:(h + 1) * HEAD_DIM]


def _split3(v):
    hi = v.astype(BF16).astype(F32)
    r1 = v - hi
    mid = r1.astype(BF16).astype(F32)
    return hi, mid, (r1 - mid).astype(BF16).astype(F32)


ONE_LANE = 3 * N_HEADS


def _pack_terms(v, with_one):
    hi, mid, lo = _split3(v)
    t = hi + pltpu.roll(mid, N_HEADS, 1) + pltpu.roll(lo, 2 * N_HEADS, 1)
    if with_one:
        t = t + (lax.broadcasted_iota(jnp.int32, v.shape, 1) == ONE_LANE).astype(F32)
    return t.astype(BF16)


def _aux_matrices():
    to_q = np.zeros((LANE, N_HEADS * 2 * HEAD_DIM), np.float32)
    to_k = np.zeros_like(to_q)
    for h in range(N_HEADS):
        base = h * 2 * HEAD_DIM + HEAD_DIM
        for s in range(3):
            to_q[s * N_HEADS + h, base + s] = 1.0
            to_q[ONE_LANE, base + 3 + s] = 1.0
            to_k[ONE_LANE, base + s] = 1.0
            to_k[s * N_HEADS + h, base + 3 + s] = -1.0
    return jnp.asarray(to_q, BF16), jnp.asarray(to_k, BF16)


def _head_sums():
    total = np.zeros((N_HEADS * HEAD_DIM, LANE), np.float32)
    first = np.zeros_like(total)
    for h in range(N_HEADS):
        total[h * HEAD_DIM:(h + 1) * HEAD_DIM, h] = 1.0
        first[h * HEAD_DIM, h] = 1.0
    return jnp.asarray(total, BF16), jnp.asarray(first, BF16)


SLOT = 2 * HEAD_DIM
N_SPLIT = 3
FOX_FWD_HEADS = 8
FOX_BWD_HEADS = 4


def _slot(ref, h):
    return ref[:, h * SLOT:(h + 1) * SLOT]


def _fox_pack_fwd(zm, f_cols, *, tm=512):
    def body(q_ref, k_ref, v_ref, f_ref, tq_ref, tk_ref, qs_ref, ks_ref, vs_ref):
        ones = jnp.ones((tm, HEAD_DIM), BF16)
        terms = _pack_terms(f_ref[...], True)
        q_aux = jnp.dot(terms, tq_ref[...], preferred_element_type=F32).astype(BF16)
        k_aux = jnp.dot(terms, tk_ref[...], preferred_element_type=F32).astype(BF16)
        for h in range(N_HEADS):
            aux = slice(h * SLOT + HEAD_DIM, (h + 1) * SLOT)
            qs_ref[:, h * SLOT:(h + 1) * SLOT] = jnp.concatenate(
                [(_head(q_ref, h).astype(F32) * SCALE).astype(BF16), q_aux[:, aux]], axis=1)
            ks_ref[:, h * SLOT:(h + 1) * SLOT] = jnp.concatenate([_head(k_ref, h), k_aux[:, aux]], axis=1)
            vs_ref[:, h * SLOT:(h + 1) * SLOT] = jnp.concatenate([_head(v_ref, h), ones], axis=1)

    col = lambda b: pl.BlockSpec((tm, ATT_W), lambda i: (i, b))
    wide = pl.BlockSpec((tm, N_HEADS * SLOT), lambda i: (i, 0))
    const = pl.BlockSpec((LANE, N_HEADS * SLOT), lambda i: (0, 0))
    return pl.pallas_call(
        body, name="fox_pack_fwd", grid=(SEQ // tm,),
        in_specs=[col(0), col(1), col(2), pl.BlockSpec((tm, LANE), lambda i: (i, 0)), const, const],
        out_specs=[wide] * 3, out_shape=[_sds((SEQ, N_HEADS * SLOT), BF16)] * 3,
        compiler_params=_params("parallel"),
    )(zm, zm, zm, f_cols, *_aux_matrices())


def _fox_pack_bwd(zm, f_cols, lse, o, do, *, tm=512):
    def body(q_ref, f_ref, lse_ref, o_ref, do_ref, tq_ref, total_ref, first_ref, qs_ref, ds_ref):
        delta = _split_dot(o_ref[...].astype(F32) * do_ref[...].astype(F32), total_ref[...])
        lse_h = _split_dot(lse_ref[...], first_ref[...])
        q_aux = jnp.dot(_pack_terms(f_ref[...] - lse_h, True), tq_ref[...], preferred_element_type=F32).astype(BF16)
        d_aux = jnp.dot(_pack_terms(-delta, False), tq_ref[...], preferred_element_type=F32).astype(BF16)
        for h in range(N_HEADS):
            aux = slice(h * SLOT + HEAD_DIM, (h + 1) * SLOT)
            qs_ref[:, h * SLOT:(h + 1) * SLOT] = jnp.concatenate(
                [(_head(q_ref, h).astype(F32) * SCALE).astype(BF16), q_aux[:, aux]], axis=1)
            ds_ref[:, h * SLOT:(h + 1) * SLOT] = jnp.concatenate([_head(do_ref, h), d_aux[:, aux]], axis=1)

    row = pl.BlockSpec((tm, ATT_W), lambda i: (i, 0))
    wide = pl.BlockSpec((tm, N_HEADS * SLOT), lambda i: (i, 0))
    const = lambda r, c: pl.BlockSpec((r, c), lambda i: (0, 0))
    return pl.pallas_call(
        body, name="fox_pack_bwd", grid=(SEQ // tm,),
        in_specs=[row, pl.BlockSpec((tm, LANE), lambda i: (i, 0)), row, row, row,
                  const(LANE, N_HEADS * SLOT), const(ATT_W, LANE), const(ATT_W, LANE)],
        out_specs=[wide] * 2, out_shape=[_sds((SEQ, N_HEADS * SLOT), BF16)] * 2,
        compiler_params=_params("parallel"),
    )(zm, f_cols, lse, o, do, _aux_matrices()[0], *_head_sums())


def _causal_pairs(key_major):
    nb = SEQ // FOX_T
    if key_major:
        pairs = [(i, j) for j in range(nb) for i in range(j, nb)]
    else:
        pairs = [(i, j) for i in range(nb) for j in range(i + 1)]
    return (jnp.array([p[0] for p in pairs], jnp.int32), jnp.array([p[1] for p in pairs], jnp.int32), len(pairs))


def _diag_mask():
    row = lax.broadcasted_iota(jnp.int32, (FOX_T, FOX_T), 0)
    col = lax.broadcasted_iota(jnp.int32, (FOX_T, FOX_T), 1)
    return col <= row


def _fox_fwd(q_slots, k_slots, v_slots):
    i_tab, j_tab, n_pairs = _causal_pairs(False)

    def body(i_tab, j_tab, q_ref, k_ref, v_ref, o_ref, lse_ref, m_s, acc_s):
        t = pl.program_id(1)
        i, j = i_tab[t], j_tab[t]

        @pl.when(j == 0)
        def _():
            m_s[...] = jnp.full_like(m_s, NEG_INF)
            acc_s[...] = jnp.zeros_like(acc_s)

        def step(masked):
            scores = [lax.dot_general(_slot(q_ref, h), _slot(k_ref, h), NT_DIMS, preferred_element_type=F32)
                      for h in range(FOX_FWD_HEADS)]
            probs, alphas = [], []
            for h in range(FOX_FWD_HEADS):
                s = jnp.where(_diag_mask(), scores[h], NEG_INF) if masked else scores[h]
                m_prev = m_s[h]
                m_new = jnp.maximum(m_prev, jnp.max(s, axis=-1, keepdims=True))
                probs.append(jnp.exp(s - jnp.tile(m_new, (1, FOX_T // LANE))).astype(BF16))
                alphas.append(jnp.exp(m_prev - m_new))
                m_s[h] = m_new
            for h in range(FOX_FWD_HEADS):
                acc_s[h] = alphas[h] * acc_s[h] + jnp.dot(probs[h], _slot(v_ref, h), preferred_element_type=F32)

        @pl.when(j < i)
        def _():
            step(False)

        @pl.when(j == i)
        def _():
            step(True)
            outs, lses = [], []
            for h in range(FOX_FWD_HEADS):
                acc = acc_s[h]
                l = acc[:, HEAD_DIM:]
                outs.append(acc[:, :HEAD_DIM] / l)
                lses.append(m_s[h][:, :HEAD_DIM] + jnp.log(l))
            o_ref[...] = jnp.concatenate(outs, axis=1).astype(o_ref.dtype)
            lse_ref[...] = jnp.concatenate(lses, axis=1)

    qspec = pl.BlockSpec((FOX_T, FOX_FWD_HEADS * SLOT), lambda p, t, it, jt: (it[t], p))
    kspec = pl.BlockSpec((FOX_T, FOX_FWD_HEADS * SLOT), lambda p, t, it, jt: (jt[t], p))
    ospec = pl.BlockSpec((FOX_T, FOX_FWD_HEADS * HEAD_DIM), lambda p, t, it, jt: (it[t], p))
    return pl.pallas_call(
        body, name="fox_fwd",
        grid_spec=pltpu.PrefetchScalarGridSpec(
            num_scalar_prefetch=2, grid=(N_HEADS // FOX_FWD_HEADS, n_pairs),
            in_specs=[qspec, kspec, kspec], out_specs=[ospec, ospec],
            scratch_shapes=[pltpu.VMEM((FOX_FWD_HEADS, FOX_T, LANE), F32),
                            pltpu.VMEM((FOX_FWD_HEADS, FOX_T, SLOT), F32)]),
        out_shape=[_sds((SEQ, ATT_W), BF16), _sds((SEQ, ATT_W), F32)],
        compiler_params=_params("parallel", "arbitrary"),
    )(i_tab, j_tab, q_slots, k_slots, v_slots)


def _fox_bwd(q_slots, k_slots, v_slots, do_slots):
    i_tab, j_tab, n_pairs = _causal_pairs(True)

    def body(i_tab, j_tab, q_ref, k_ref, v_ref, do_ref, dq_ref, dk_ref, dv_ref):
        t = pl.program_id(1)
        i, j = i_tab[t], j_tab[t]

        @pl.when(t == 0)
        def _():
            dq_ref[...] = jnp.zeros_like(dq_ref)

        @pl.when(i == j)
        def _():
            dk_ref[...] = jnp.zeros_like(dk_ref)
            dv_ref[...] = jnp.zeros_like(dv_ref)

        def step(masked):
            rows = pl.ds(pl.multiple_of(i * FOX_T, FOX_T), FOX_T)
            heads = range(FOX_BWD_HEADS)
            scores = [lax.dot_general(_slot(q_ref, h), _slot(k_ref, h), NT_DIMS, preferred_element_type=F32)
                      for h in heads]
            dps = [lax.dot_general(_slot(do_ref, h), _slot(v_ref, h), NT_DIMS, preferred_element_type=F32)
                   for h in heads]
            ps, dss = [], []
            for h in heads:
                p = jnp.exp(scores[h])
                if masked:
                    p = jnp.where(_diag_mask(), p, 0.0)
                ps.append(p.astype(BF16))
                dss.append((p * dps[h]).astype(BF16))
            for h in heads:
                cols = slice(h * SLOT, (h + 1) * SLOT)
                dv_ref[:, cols] += lax.dot_general(ps[h], _slot(do_ref, h), TN_DIMS, preferred_element_type=F32)
                dk_ref[:, cols] += lax.dot_general(dss[h], _slot(q_ref, h), TN_DIMS, preferred_element_type=F32)
                dq_ref[rows, cols] += jnp.dot(dss[h], _slot(k_ref, h), preferred_element_type=F32)

        @pl.when(i > j)
        def _():
            step(False)

        @pl.when(i == j)
        def _():
            step(True)

    qspec = pl.BlockSpec((FOX_T, FOX_BWD_HEADS * SLOT), lambda p, t, it, jt: (it[t], p))
    kspec = pl.BlockSpec((FOX_T, FOX_BWD_HEADS * SLOT), lambda p, t, it, jt: (jt[t], p))
    return pl.pallas_call(
        body, name="fox_bwd",
        grid_spec=pltpu.PrefetchScalarGridSpec(
            num_scalar_prefetch=2, grid=(N_HEADS // FOX_BWD_HEADS, n_pairs),
            in_specs=[qspec, kspec, kspec, qspec],
            out_specs=[pl.BlockSpec((SEQ, FOX_BWD_HEADS * SLOT), lambda p, t, it, jt: (0, p)), kspec, kspec]),
        out_shape=[_sds((SEQ, N_HEADS * SLOT), F32)] * 3,
        compiler_params=_params("arbitrary", "arbitrary"),
    )(i_tab, j_tab, q_slots, k_slots, v_slots, do_slots)


def _fox_unpack(dq_slots, dk_slots, dv_slots, dz, *, tm=512):
    def body(dq_ref, dk_ref, dv_ref, dz_in, o_ref, df_ref):
        lane = lax.broadcasted_iota(jnp.int32, (tm, LANE), 1)
        df = jnp.zeros((tm, LANE), F32)
        for h in range(N_HEADS):
            lo = h * SLOT
            for part, (ref, mult) in enumerate(((dq_ref, SCALE), (dk_ref, 1.0), (dv_ref, 1.0))):
                o_ref[:, part * ATT_W + h * HEAD_DIM:part * ATT_W + (h + 1) * HEAD_DIM] = (
                    ref[:, lo:lo + HEAD_DIM] * mult).astype(o_ref.dtype)
            rows = dq_ref[:, lo + HEAD_DIM:lo + HEAD_DIM + 1]
            cols = dk_ref[:, lo + HEAD_DIM + N_SPLIT:lo + HEAD_DIM + N_SPLIT + 1]
            df = jnp.where(lane == h, rows - cols, df)
        df_ref[...] = df

    wide = pl.BlockSpec((tm, N_HEADS * SLOT), lambda i: (i, 0))
    return pl.pallas_call(
        body, name="fox_unpack", grid=(SEQ // tm,), in_specs=[wide] * 3 + [ANY],
        out_specs=[pl.BlockSpec((tm, 3 * ATT_W), lambda i: (i, 0)), pl.BlockSpec((tm, LANE), lambda i: (i, 0))],
        out_shape=[_sds((SEQ, Z_MAIN), BF16), _sds((SEQ, LANE), F32)],
        input_output_aliases={3: 0},
        compiler_params=_params("parallel"),
    )(dq_slots, dk_slots, dv_slots, dz)


def _attn_delta(o, do, *, name, tm=512):
    def body(o_ref, do_ref, d_ref):
        prod = o_ref[...].astype(F32) * do_ref[...].astype(F32)
        lane = lax.broadcasted_iota(jnp.int32, (tm, LANE), 1)
        out = jnp.zeros((tm, LANE), F32)
        for h in range(N_HEADS):
            out = jnp.where(lane == h, jnp.sum(_head(prod, h), axis=1, keepdims=True), out)
        d_ref[...] = out

    row = pl.BlockSpec((tm, ATT_W), lambda i: (i, 0))
    return pl.pallas_call(
        body, name=name, grid=(SEQ // tm,), in_specs=[row, row],
        out_specs=pl.BlockSpec((tm, LANE), lambda i: (i, 0)), out_shape=_sds((SEQ, LANE), F32),
        compiler_params=_params("parallel"),
    )(o, do)


def _rope_tables():
    half = ROPE_DIM // 2
    inv_freq = np.float32(ROPE_THETA) ** (-np.arange(half, dtype=np.float32) * np.float32(2.0) / np.float32(ROPE_DIM))
    ang = np.arange(SEQ, dtype=np.float32)[:, None] * inv_freq.astype(np.float32)[None, :]
    cos, sin = jnp.asarray(np.cos(ang).astype(np.float32)), jnp.asarray(np.sin(ang).astype(np.float32))
    ones = jnp.ones((SEQ, HEAD_DIM - ROPE_DIM), F32)
    zeros = jnp.zeros((SEQ, HEAD_DIM - ROPE_DIM), F32)
    zh = jnp.zeros((SEQ, half), F32)
    c_tab = jnp.concatenate([cos, cos, ones], axis=1)
    a_tab = jnp.concatenate([-sin, zh, zeros], axis=1)
    b_tab = jnp.concatenate([zh, sin, zeros], axis=1)
    two = lambda t: jnp.concatenate([t, t], axis=1)
    return two(c_tab), two(a_tab), two(b_tab)


def _rotate(x, c_tab, a_tab, b_tab):
    return x * c_tab + pltpu.roll(x, LANE - ROPE_DIM // 2, 1) * a_tab + pltpu.roll(x, ROPE_DIM // 2, 1) * b_tab


def _rope_fwd(zm, tabs, *, tm=512):
    def body(q_ref, k_ref, v_ref, c_ref, a_ref, b_ref, o_ref):
        for part, (x_ref, mult) in enumerate(((q_ref, SCALE), (k_ref, 1.0))):
            for cc in range(ATT_W // LANE):
                sl = slice(cc * LANE, (cc + 1) * LANE)
                rot = _rotate(x_ref[:, sl].astype(F32), c_ref[...], a_ref[...], b_ref[...])
                o_ref[:, part * ATT_W + cc * LANE:part * ATT_W + (cc + 1) * LANE] = (rot * mult).astype(o_ref.dtype)
        o_ref[:, 2 * ATT_W:] = v_ref[...]

    tab = pl.BlockSpec((tm, LANE), lambda i: (i, 0))
    col = lambda b: pl.BlockSpec((tm, ATT_W), lambda i: (i, b))
    return pl.pallas_call(
        body, name="rope_fwd", grid=(SEQ // tm,),
        in_specs=[col(3), col(4), col(5), tab, tab, tab],
        out_specs=pl.BlockSpec((tm, 3 * ATT_W), lambda i: (i, 0)),
        out_shape=_sds((SEQ, 3 * ATT_W), BF16),
        compiler_params=_params("parallel"),
    )(zm, zm, zm, *tabs)


def _dil_grad_combine(dqs, dks, dvs, tabs, dz, *, tm=256):
    def body(*refs):
        q_refs, k_refs, v_refs = refs[0:3], refs[3:6], refs[6:9]
        c_ref, a_ref, b_ref, _, o_ref = refs[9:]
        total = lambda rs, sl: rs[0][:, sl].astype(F32) + rs[1][:, sl].astype(F32) + rs[2][:, sl].astype(F32)
        for cc in range(ATT_W // LANE):
            sl = slice(cc * LANE, (cc + 1) * LANE)
            for part, rs in enumerate((q_refs, k_refs)):
                o_ref[:, part * ATT_W + cc * LANE:part * ATT_W + (cc + 1) * LANE] = _rotate(
                    total(rs, sl), c_ref[...], -a_ref[...], -b_ref[...]).astype(o_ref.dtype)
            o_ref[:, 2 * ATT_W + cc * LANE:2 * ATT_W + (cc + 1) * LANE] = total(v_refs, sl).astype(o_ref.dtype)

    row = pl.BlockSpec((tm, ATT_W), lambda i: (i, 0))
    tab = pl.BlockSpec((tm, LANE), lambda i: (i, 0))
    return pl.pallas_call(
        body, name="dil_grad_combine", grid=(SEQ // tm,),
        in_specs=[row] * 9 + [tab] * 3 + [ANY],
        out_specs=pl.BlockSpec((tm, 3 * ATT_W), lambda i: (i, 1)),
        out_shape=_sds((SEQ, Z_MAIN), BF16),
        input_output_aliases={12: 0},
        compiler_params=_params("parallel"),
    )(*dqs, *dks, *dvs, *tabs, dz)


def _dil_valid(n):
    qi = lax.broadcasted_iota(jnp.int32, (DIL_BLK, 2 * DIL_BLK), 0)
    ki = lax.broadcasted_iota(jnp.int32, (DIL_BLK, 2 * DIL_BLK), 1)
    dist = qi + DIL_BLK - ki
    return (dist >= 0) & (dist <= DIL_BLK) & ((n > 0) | (ki >= DIL_BLK))


def _dil_fwd(qkv, d):
    length = SEQ // d
    nb = length // DIL_BLK
    qkv_v = qkv.reshape(length, d * 3 * ATT_W)

    def body(q_ref, kp_ref, kc_ref, vp_ref, vc_ref, o_ref, lse_ref):
        m_step = pl.program_id(1)
        lane = lax.broadcasted_iota(jnp.int32, (DIL_BLK, LANE), 1)
        jobs = [(sub, h) for sub in range(2) for h in range(N_HEADS)]
        rows = lambda sub: slice(sub * DIL_BLK, (sub + 1) * DIL_BLK)
        cols = lambda h: slice(h * HEAD_DIM, (h + 1) * HEAD_DIM)

        def keys(prev_ref, cur_ref, sub, h):
            before = prev_ref[:, cols(h)] if sub == 0 else cur_ref[rows(0), cols(h)]
            return jnp.concatenate([before, cur_ref[rows(sub), cols(h)]], axis=0)

        scores = [lax.dot_general(q_ref[rows(sub), cols(h)], keys(kp_ref, kc_ref, sub, h), NT_DIMS,
                                  preferred_element_type=F32) for sub, h in jobs]
        ok = [_dil_valid(m_step), _dil_valid(1)]
        probs, inv_l, lse_all = [], [], [jnp.zeros((DIL_BLK, LANE), F32)] * 2
        for idx, (sub, h) in enumerate(jobs):
            s = jnp.where(ok[sub], scores[idx], NEG_INF)
            m = jnp.max(s, axis=-1, keepdims=True)
            p = jnp.exp(s - m)
            l = jnp.sum(p, axis=-1, keepdims=True)
            probs.append(p.astype(BF16))
            inv_l.append(1.0 / l)
            lse_all[sub] = jnp.where(lane == h, m + jnp.log(l), lse_all[sub])
        outs = [jnp.dot(probs[idx], keys(vp_ref, vc_ref, sub, h), preferred_element_type=F32) * inv_l[idx]
                for idx, (sub, h) in enumerate(jobs)]
        for sub in range(2):
            o_ref[rows(sub), :] = jnp.concatenate(outs[sub * N_HEADS:(sub + 1) * N_HEADS], axis=1).astype(o_ref.dtype)
            lse_ref[rows(sub), :] = lse_all[sub]

    pair = lambda f: pl.BlockSpec((2 * DIL_BLK, ATT_W), f)
    one = lambda f: pl.BlockSpec((DIL_BLK, ATT_W), f)
    before = lambda m: jnp.maximum(2 * m - 1, 0)
    o, lse = pl.pallas_call(
        body, name=f"dil_fwd_d{d}", grid=(d, nb // 2),
        in_specs=[pair(lambda r, m: (m, 3 * r)),
                  one(lambda r, m: (before(m), 3 * r + 1)), pair(lambda r, m: (m, 3 * r + 1)),
                  one(lambda r, m: (before(m), 3 * r + 2)), pair(lambda r, m: (m, 3 * r + 2))],
        out_specs=[pair(lambda r, m: (m, r)), pl.BlockSpec((2 * DIL_BLK, LANE), lambda r, m: (m, r))],
        out_shape=[_sds((length, d * ATT_W), BF16), _sds((length, d * LANE), F32)],
        compiler_params=_params("parallel", "arbitrary"),
    )(qkv_v, qkv_v, qkv_v, qkv_v, qkv_v)
    return o.reshape(SEQ, ATT_W), lse.reshape(SEQ, LANE)


def _dil_merge(os_, lses, *, tm=512):
    def body(o0, o1, o2, l0, l1, l2, y_ref, lse_ref):
        ls = [l0[...], l1[...], l2[...]]
        m = jnp.maximum(jnp.maximum(ls[0], ls[1]), ls[2])
        es = [jnp.exp(l - m) for l in ls]
        tot = es[0] + es[1] + es[2]
        lse_ref[...] = m + jnp.log(tot)
        alphas = [e / tot for e in es]
        outs = []
        for h in range(N_HEADS):
            acc = None
            for g, o_ref in enumerate((o0, o1, o2)):
                term = alphas[g][:, h:h + 1] * _head(o_ref, h).astype(F32)
                acc = term if acc is None else acc + term
            outs.append(acc)
        y_ref[...] = jnp.concatenate(outs, axis=1).astype(y_ref.dtype)

    row = pl.BlockSpec((tm, ATT_W), lambda i: (i, 0))
    vec = pl.BlockSpec((tm, LANE), lambda i: (i, 0))
    return pl.pallas_call(
        body, name="dil_merge", grid=(SEQ // tm,),
        in_specs=[row] * 3 + [vec] * 3, out_specs=[row, vec],
        out_shape=[_sds((SEQ, ATT_W), BF16), _sds((SEQ, LANE), F32)],
        compiler_params=_params("parallel"),
    )(*os_, *lses)


def _dil_bwd(qkv, lse, delta, do, d):
    length = SEQ // d
    nb = length // DIL_BLK
    n_steps = nb // 2
    qkv_v = qkv.reshape(length, d * 3 * ATT_W)
    lse_v, dl_v, do_v = lse.reshape(length, d * LANE), delta.reshape(length, d * LANE), do.reshape(length, d * ATT_W)

    def body(q_ref, kp_ref, kc_ref, vp_ref, vc_ref, lse_ref, dl_ref, do_ref, dq_ref, dk_ref, dv_ref, dk_s, dv_s):
        m_step = pl.program_id(1)

        @pl.when(m_step == 0)
        def _():
            dk_s[...] = jnp.zeros_like(dk_s)
            dv_s[...] = jnp.zeros_like(dv_s)

        jobs = [(sub, h) for sub in range(2) for h in range(N_HEADS)]
        rows = lambda sub: slice(sub * DIL_BLK, (sub + 1) * DIL_BLK)
        cols = lambda h: slice(h * HEAD_DIM, (h + 1) * HEAD_DIM)

        def keys(prev_ref, cur_ref, sub, h):
            before = prev_ref[:, cols(h)] if sub == 0 else cur_ref[rows(0), cols(h)]
            return jnp.concatenate([before, cur_ref[rows(sub), cols(h)]], axis=0)

        kks = [keys(kp_ref, kc_ref, sub, h) for sub, h in jobs]
        scores = [lax.dot_general(q_ref[rows(sub), cols(h)], kks[idx], NT_DIMS, preferred_element_type=F32)
                  for idx, (sub, h) in enumerate(jobs)]
        dps = [lax.dot_general(do_ref[rows(sub), cols(h)], keys(vp_ref, vc_ref, sub, h), NT_DIMS,
                               preferred_element_type=F32) for sub, h in jobs]
        ok = [_dil_valid(m_step), _dil_valid(1)]
        ps, dss = [], []
        for idx, (sub, h) in enumerate(jobs):
            p = jnp.where(ok[sub], jnp.exp(scores[idx] - lse_ref[rows(sub), h:h + 1]), 0.0)
            ps.append(p.astype(BF16))
            dss.append((p * (dps[idx] - dl_ref[rows(sub), h:h + 1])).astype(BF16))
        dqs = [jnp.dot(dss[idx], kks[idx], preferred_element_type=F32) * SCALE for idx in range(len(jobs))]
        dkks = [lax.dot_general(dss[idx], q_ref[rows(sub), cols(h)], TN_DIMS, preferred_element_type=F32)
                for idx, (sub, h) in enumerate(jobs)]
        dvvs = [lax.dot_general(ps[idx], do_ref[rows(sub), cols(h)], TN_DIMS, preferred_element_type=F32)
                for idx, (sub, h) in enumerate(jobs)]
        for sub in range(2):
            dq_ref[rows(sub), :] = jnp.concatenate(dqs[sub * N_HEADS:(sub + 1) * N_HEADS], axis=1).astype(dq_ref.dtype)
        base = m_step * (2 * DIL_BLK)
        blocks = [pl.ds(pl.multiple_of(jnp.maximum(base - DIL_BLK, 0), DIL_BLK), DIL_BLK),
                  pl.ds(pl.multiple_of(base, DIL_BLK), DIL_BLK),
                  pl.ds(pl.multiple_of(base + DIL_BLK, DIL_BLK), DIL_BLK)]
        for acc, parts in ((dk_s, dkks), (dv_s, dvvs)):
            top = lambda sub: jnp.concatenate([parts[sub * N_HEADS + h][:DIL_BLK] for h in range(N_HEADS)], axis=1)
            bottom = lambda sub: jnp.concatenate([parts[sub * N_HEADS + h][DIL_BLK:] for h in range(N_HEADS)], axis=1)
            acc[blocks[0], :] += top(0)
            acc[blocks[1], :] += bottom(0) + top(1)
            acc[blocks[2], :] += bottom(1)

        @pl.when(m_step == n_steps - 1)
        def _():
            dk_ref[...] = dk_s[...].astype(dk_ref.dtype)
            dv_ref[...] = dv_s[...].astype(dv_ref.dtype)

    pair = lambda f: pl.BlockSpec((2 * DIL_BLK, ATT_W), f)
    one = lambda f: pl.BlockSpec((DIL_BLK, ATT_W), f)
    vec = lambda f: pl.BlockSpec((2 * DIL_BLK, LANE), f)
    whole = pl.BlockSpec((length, ATT_W), lambda r, m: (0, r))
    before = lambda m: jnp.maximum(2 * m - 1, 0)
    outs = pl.pallas_call(
        body, name=f"dil_bwd_d{d}", grid=(d, n_steps),
        in_specs=[pair(lambda r, m: (m, 3 * r)),
                  one(lambda r, m: (before(m), 3 * r + 1)), pair(lambda r, m: (m, 3 * r + 1)),
                  one(lambda r, m: (before(m), 3 * r + 2)), pair(lambda r, m: (m, 3 * r + 2)),
                  vec(lambda r, m: (m, r)), vec(lambda r, m: (m, r)), pair(lambda r, m: (m, r))],
        out_specs=[pair(lambda r, m: (m, r)), whole, whole],
        out_shape=[_sds((length, d * ATT_W), BF16)] * 3,
        scratch_shapes=[pltpu.VMEM((length, ATT_W), F32), pltpu.VMEM((length, ATT_W), F32)],
        compiler_params=_params("arbitrary", "arbitrary"),
    )(qkv_v, qkv_v, qkv_v, qkv_v, qkv_v, lse_v, dl_v, do_v)
    return [t.reshape(SEQ, ATT_W) for t in outs]


def _sigmoid(x):
    return 1.0 / (1.0 + jnp.exp(-x))


def _mix_fwd(ya, yb, w_oa, w_ob, zm, *, tm=512):
    def body(ya_ref, yb_ref, wa_ref, wb_ref, ga_ref, gb_ref, pa_ref, pb_ref, mix_ref):
        pa = jnp.dot(ya_ref[...], wa_ref[...], preferred_element_type=F32)
        pb = jnp.dot(yb_ref[...], wb_ref[...], preferred_element_type=F32)
        pa_ref[...] = pa.astype(pa_ref.dtype)
        pb_ref[...] = pb.astype(pb_ref.dtype)
        mix_ref[...] = (_sigmoid(ga_ref[...].astype(F32)) * pa + _sigmoid(gb_ref[...].astype(F32)) * pb
                        ).astype(mix_ref.dtype)

    row = pl.BlockSpec((tm, ATT_W), lambda i: (i, 0))
    wsp = pl.BlockSpec((ATT_W, D_MODEL), lambda i: (0, 0))
    wide = pl.BlockSpec((tm, D_MODEL), lambda i: (i, 0))
    return pl.pallas_call(
        body, name="mix_fwd", grid=(SEQ // tm,),
        in_specs=[row, row, wsp, wsp, pl.BlockSpec((tm, D_MODEL), lambda i: (i, 3)),
                  pl.BlockSpec((tm, D_MODEL), lambda i: (i, 4))],
        out_specs=[wide] * 3, out_shape=[_sds((SEQ, D_MODEL), BF16)] * 3,
        compiler_params=_params("parallel"),
    )(ya, yb, w_oa, w_ob, zm, zm)


def _gate_bwd(dmix, zm, p, gate_block, dz, *, name, tm=512):
    def body(dm_ref, g_ref, p_ref, *rest):
        dp_ref, dz_ref = rest[-2], rest[-1]
        dm = dm_ref[...].astype(F32)
        s = _sigmoid(g_ref[...].astype(F32))
        dp_ref[...] = (dm * s).astype(dp_ref.dtype)
        dz_ref[...] = (dm * p_ref[...].astype(F32) * s * (1.0 - s)).astype(dz_ref.dtype)

    wide = pl.BlockSpec((tm, D_MODEL), lambda i: (i, 0))
    gate = pl.BlockSpec((tm, D_MODEL), lambda i: (i, gate_block))
    extra = [] if dz is None else [dz]
    return pl.pallas_call(
        body, name=name, grid=(SEQ // tm,),
        in_specs=[wide, gate, wide] + [ANY] * len(extra),
        out_specs=[wide, gate],
        out_shape=[_sds((SEQ, D_MODEL), BF16), _sds((SEQ, Z_MAIN), BF16)],
        input_output_aliases={3: 1} if extra else {},
        compiler_params=_params("parallel"),
    )(dmix, zm, p, *extra)


def _out_fwd(mixed, w_out, x, g_post, g_pre, *, tm=512):
    def body(m_ref, w_ref, x_ref, gp_ref, gn_ref, y_ref, x2_ref, h_ref):
        y = jnp.dot(m_ref[...], w_ref[...], preferred_element_type=F32)
        y_ref[...] = y
        r = lax.rsqrt(jnp.mean(y * y, axis=-1, keepdims=True) + RMS_EPS)
        x2 = x_ref[...] + y * r * gp_ref[...]
        x2_ref[...] = x2
        r2 = lax.rsqrt(jnp.mean(x2 * x2, axis=-1, keepdims=True) + RMS_EPS)
        h_ref[...] = (x2 * r2 * gn_ref[...]).astype(h_ref.dtype)

    row = pl.BlockSpec((tm, D_MODEL), lambda i: (i, 0))
    vec = pl.BlockSpec((1, D_MODEL), lambda i: (0, 0))
    return pl.pallas_call(
        body, name="out_fwd", grid=(SEQ // tm,),
        in_specs=[row, pl.BlockSpec((D_MODEL, D_MODEL), lambda i: (0, 0)), row, vec, vec],
        out_specs=[row] * 3,
        out_shape=[_sds((SEQ, D_MODEL), F32), _sds((SEQ, D_MODEL), F32), _sds((SEQ, D_MODEL), BF16)],
        compiler_params=_params("parallel"),
    )(mixed, w_out, x, g_post, g_pre)


FFN_TM = 512
FFN_HALF = 256
FFN_TN = 2 * FFN_HALF
FFN_NJ = D_FF // FFN_HALF
FFN_GROUP = 2 * SUBLANE


def _ffn_interleave(t):
    lead = t.shape[:-1]
    return jnp.swapaxes(t.reshape(*lead, 2, FFN_NJ, FFN_HALF), -3, -2).reshape(*lead, 2 * D_FF)


def _ffn_deinterleave(t):
    lead = t.shape[:-1]
    return jnp.swapaxes(t.reshape(*lead, FFN_NJ, 2, FFN_HALF), -3, -2).reshape(*lead, 2 * D_FF)


def _ffn_move_blocks(t, *, interleave, name):
    rows = t.shape[0]
    if interleave:
        src = lambda jb: (0, (jb % 2) * FFN_NJ + jb // 2)
    else:
        src = lambda jb: (0, 2 * (jb % FFN_NJ) + jb // FFN_NJ)

    def body(x_ref, o_ref):
        o_ref[...] = x_ref[...]

    return pl.pallas_call(
        body, name=name, grid=(2 * FFN_NJ,),
        in_specs=[pl.BlockSpec((rows, FFN_HALF), src)],
        out_specs=pl.BlockSpec((rows, FFN_HALF), lambda jb: (0, jb)),
        out_shape=_sds(t.shape, t.dtype),
        compiler_params=_params("parallel"),
    )(t)


def _gelu_parts(a):
    c = math.sqrt(2.0 / math.pi)
    a2 = a * a
    t = jnp.tanh((c * a) * (1.0 + 0.044715 * a2))
    half_a, one_t = 0.5 * a, 1.0 + t
    gelu = half_a * one_t
    dgelu = 0.5 * one_t + half_a * (1.0 - t * t) * (c + (3.0 * 0.044715 * c) * a2)
    return gelu, dgelu


def _row_masks(down):
    row = lax.broadcasted_iota(jnp.int32, (SUBLANE, FFN_TN), 0)
    return (row < 1, row < 2) if down else (row >= SUBLANE - 1, row >= SUBLANE - 2)


def _rolled(x, down):
    return (pltpu.roll(x, 1, 0), pltpu.roll(x, 2, 0)) if down else (
        pltpu.roll(x, SUBLANE - 1, 0), pltpu.roll(x, SUBLANE - 2, 0))


def _shifted(cur_rolled, neighbour_rolled, masks):
    return (jnp.where(masks[0], neighbour_rolled[0], cur_rolled[0]),
            jnp.where(masks[1], neighbour_rolled[1], cur_rolled[1]))


def _conv_consts(w_ref, b_ref):
    shape = (SUBLANE, FFN_TN)
    return [jnp.broadcast_to(w_ref[k:k + 1, :], shape) for k in range(3)] + [jnp.broadcast_to(b_ref[...], shape)]


def _ffn_mid_fwd(u, conv_w, conv_b):
    per = FFN_TM // SUBLANE

    def body(u_ref, h_ref, w_ref, b_ref, m_ref, ab_ref):
        live = (pl.program_id(1) > 0).astype(F32)
        w0, w1, w2, bias = _conv_consts(w_ref, b_ref)
        masks = _row_masks(True)

        def group(g, above):
            rows = pl.ds(pl.multiple_of(g * FFN_GROUP, FFN_GROUP), FFN_GROUP)
            x = u_ref[rows, :].astype(F32)
            convs = []
            for c in range(2):
                cur = x[c * SUBLANE:(c + 1) * SUBLANE]
                cur_rolled = _rolled(cur, True)
                s1, s2 = _shifted(cur_rolled, above, masks)
                convs.append(w0 * s2 + w1 * s1 + w2 * cur + bias)
                above = cur_rolled
            y = jnp.concatenate(convs, axis=0)
            ab_ref[rows, :] = y.astype(ab_ref.dtype)
            m_ref[rows, :] = (_gelu_parts(y[:, :FFN_HALF])[0] * y[:, FFN_HALF:]).astype(m_ref.dtype)
            return above

        lax.fori_loop(0, FFN_TM // (2 * FFN_GROUP), lambda g2, carry: group(2 * g2 + 1, group(2 * g2, carry)),
                      _rolled(h_ref[...].astype(F32) * live, True))

    blk = pl.BlockSpec((FFN_TM, FFN_TN), lambda j, i: (i, j))
    return pl.pallas_call(
        body, name="ffn_mid_fwd", grid=(FFN_NJ, SEQ // FFN_TM),
        in_specs=[blk, pl.BlockSpec((SUBLANE, FFN_TN), lambda j, i: (jnp.maximum(i * per - 1, 0), j)),
                  pl.BlockSpec((3, FFN_TN), lambda j, i: (0, j)), pl.BlockSpec((1, FFN_TN), lambda j, i: (0, j))],
        out_specs=[pl.BlockSpec((FFN_TM, FFN_HALF), lambda j, i: (i, j)), blk],
        out_shape=[_sds((SEQ, D_FF), BF16), _sds((SEQ, 2 * D_FF), BF16)],
        compiler_params=_params("parallel", "arbitrary"),
    )(u, u, conv_w, conv_b)


def _ffn_mid_bwd(dm, u, ab, conv_w):
    nrow = SEQ // FFN_TM
    n_groups = FFN_TM // FFN_GROUP

    def body(dm_ref, u_ref, ab_ref, w_ref, du_ref, gw_ref, gb_ref, c_s):
        @pl.when(pl.program_id(1) == 0)
        def _():
            c_s[...] = jnp.zeros_like(c_s)
            gw_ref[...] = jnp.zeros_like(gw_ref)
            gb_ref[...] = jnp.zeros_like(gb_ref)

        taps = [jnp.broadcast_to(w_ref[k:k + 1, :], (SUBLANE, FFN_TN)) for k in range(3)]
        masks = _row_masks(False)

        def group(t, carry):
            below, acc = carry
            rows = pl.ds(pl.multiple_of((n_groups - 1 - t) * FFN_GROUP, FFN_GROUP), FFN_GROUP)
            x, y, dmv = u_ref[rows, :].astype(F32), ab_ref[rows, :].astype(F32), dm_ref[rows, :].astype(F32)
            gelu, dgelu = _gelu_parts(y[:, :FFN_HALF])
            d = jnp.concatenate([dmv * y[:, FFN_HALF:] * dgelu, dmv * gelu], axis=1)
            acc, pre = list(acc), [None, None]
            for c in (1, 0):
                sl = slice(c * SUBLANE, (c + 1) * SUBLANE)
                cur, xs = d[sl], x[sl]
                cur_rolled = _rolled(cur, False)
                up1, up2 = _shifted(cur_rolled, below, masks)
                acc = [acc[0] + up2 * xs, acc[1] + up1 * xs, acc[2] + cur * xs, acc[3] + cur]
                pre[c] = taps[2] * cur + taps[1] * up1 + taps[0] * up2
                below = cur_rolled
            du_ref[rows, :] = jnp.concatenate(pre, axis=0).astype(du_ref.dtype)
            return below, tuple(acc)

        zeros = jnp.zeros((SUBLANE, FFN_TN), F32)
        below, acc = lax.fori_loop(0, n_groups // 2, lambda t2, carry: group(2 * t2 + 1, group(2 * t2, carry)),
                                   (_rolled(c_s[...], False), (zeros,) * 4))
        c_s[...] = pltpu.roll(below[0], 1, 0)
        for k in range(3):
            gw_ref[k:k + 1, :] += jnp.sum(acc[k], axis=0, keepdims=True)
        gb_ref[...] += jnp.sum(acc[3], axis=0, keepdims=True)

    blk = pl.BlockSpec((FFN_TM, FFN_TN), lambda j, i: (nrow - 1 - i, j))
    return pl.pallas_call(
        body, name="ffn_mid_bwd", grid=(FFN_NJ, nrow),
        in_specs=[pl.BlockSpec((FFN_TM, FFN_HALF), lambda j, i: (nrow - 1 - i, j)), blk, blk,
                  pl.BlockSpec((3, FFN_TN), lambda j, i: (0, j))],
        out_specs=[blk, pl.BlockSpec((3, FFN_TN), lambda j, i: (0, j)), pl.BlockSpec((1, FFN_TN), lambda j, i: (0, j))],
        out_shape=[_sds((SEQ, 2 * D_FF), BF16), _sds((3, 2 * D_FF), F32), _sds((1, 2 * D_FF), F32)],
        scratch_shapes=[pltpu.VMEM((SUBLANE, FFN_TN), F32)],
        compiler_params=_params("parallel", "arbitrary"),
    )(dm, u, ab, conv_w)


def _down_fwd(m, w_down, x2, g_post, target, *, tm=512):
    def body(m_ref, w_ref, x2_ref, g_ref, t_ref, dout_ref, dy_ref, gg_ref, loss_ref):
        @pl.when(pl.program_id(0) == 0)
        def _():
            gg_ref[...] = jnp.zeros_like(gg_ref)
            loss_ref[...] = jnp.zeros_like(loss_ref)

        y = jnp.dot(m_ref[...], w_ref[...], preferred_element_type=F32)
        r = lax.rsqrt(jnp.mean(y * y, axis=-1, keepdims=True) + RMS_EPS)
        yn = y * r
        diff = (x2_ref[...] + yn * g_ref[...]) - t_ref[...]
        loss_ref[...] += jnp.sum(diff * diff)
        dout = diff * (1.0 / D_MODEL)
        dout_ref[...] = dout
        gg_ref[...] += jnp.sum(dout * yn, axis=0, keepdims=True)
        dn = dout * g_ref[...]
        dy_ref[...] = (r * (dn - yn * jnp.mean(dn * yn, axis=-1, keepdims=True))).astype(dy_ref.dtype)

    row = pl.BlockSpec((tm, D_MODEL), lambda i: (i, 0))
    vec = pl.BlockSpec((1, D_MODEL), lambda i: (0, 0))
    return pl.pallas_call(
        body, name="down_fwd", grid=(SEQ // tm,),
        in_specs=[pl.BlockSpec((tm, D_FF), lambda i: (i, 0)), pl.BlockSpec((D_FF, D_MODEL), lambda i: (0, 0)),
                  row, vec, row],
        out_specs=[row, row, vec, pl.BlockSpec((1, LANE), lambda i: (0, 0))],
        out_shape=[_sds((SEQ, D_MODEL), F32), _sds((SEQ, D_MODEL), BF16), _sds((1, D_MODEL), F32),
                   _sds((1, LANE), F32)],
        compiler_params=_params("arbitrary"),
    )(m, w_down, x2, g_post, target)


def _local_step(x, target, w_main, w_f, b_forget, conv_b, g_pre_mix, g_post_mix, g_pre_ffn, g_post_ffn,
                late_weights, ffn_grads_ready, proj_grads_ready, mixer_grads_ready):
    mm = _matmul
    tabs = _rope_tables()

    h1 = _rms_fwd(x, g_pre_mix, name="rms_pre_mix")
    zm = mm(h1, w_main, out_dtype=BF16, tm=2048, tn=512, tk=1024, name="in_proj")
    zf = mm(h1, w_f, out_dtype=F32, tm=2048, tn=F_PAD, tk=1024, name="in_proj_forget")
    f_row, sg_row = _fox_prep(zf[:, :N_HEADS].T, b_forget.reshape(N_HEADS, 1))
    f_cols = jnp.pad(f_row.T, ((0, 0), (0, LANE - N_HEADS)))
    q_slots, k_slots, v_slots = _fox_pack_fwd(zm, f_cols)
    ya, lse_a = _fox_fwd(q_slots, k_slots, v_slots)
    qkv_d = _rope_fwd(zm, tabs)
    dil = [_dil_fwd(qkv_d, d) for _, d in DIL_PATTERNS]
    yb, lse_b = _dil_merge([o for o, _ in dil], [l for _, l in dil])
    w_oa, w_ob, w_out, w_up, conv_w, w_down = late_weights(yb)
    pa, pb, mixed = _mix_fwd(ya, yb, w_oa, w_ob, zm)
    y1, x2, h2 = _out_fwd(mixed, w_out, x, g_post_mix, g_pre_ffn)
    u = mm(h2, w_up, out_dtype=BF16, tm=2048, tn=512, tk=1024, name="up_proj")
    m, ab = _ffn_mid_fwd(u, conv_w, _ffn_interleave(conv_b))
    dout, dy2, gg_post_ffn, sq_err = _down_fwd(m, w_down, x2, g_post_ffn, target)

    g_w_down = mm(m, dy2, ta=True, out_dtype=BF16, tm=D_FF // 2, tn=1024, tk=2048, name="grad_w_down")
    dm = mm(dy2, w_down, tb=True, out_dtype=BF16, tm=2048, tn=D_FF // 2, tk=1024, name="d_ffn_mid")
    du, g_conv_w, g_conv_b = _ffn_mid_bwd(dm, u, ab, conv_w)
    g_w_up = mm(h2, du, ta=True, out_dtype=BF16, tm=1024, tn=D_FF // 2, tk=2048, name="grad_w_up")
    tok = ffn_grads_ready(dict(w_down=g_w_down, w_up=_ffn_move_blocks(g_w_up, interleave=False, name="grad_w_up_cols"),
                               conv_w=_ffn_deinterleave(g_conv_w)))
    dh2 = mm(du, w_up, tb=True, out_dtype=BF16, tm=512, tn=1024, tk=2 * D_FF, name="d_h2")

    dx2, dy1, gg_pre_ffn, gg_post_mix = _rms_pair_bwd([dh2], x2, g_pre_ffn, dout, y1, g_post_mix + tok)
    g_w_out = mm(mixed, dy1, ta=True, out_dtype=BF16, tm=1024, tn=1024, tk=2048, name="grad_w_out")
    dmix = mm(dy1, w_out, tb=True, out_dtype=BF16, tm=2048, tn=1024, tk=1024, name="d_mixed")
    dpa, dz = _gate_bwd(dmix, zm, pa, 3, None, name="gate_bwd_fox")
    dpb, dz = _gate_bwd(dmix, zm, pb, 4, dz, name="gate_bwd_dil")
    g_w_oa = mm(ya, dpa, ta=True, out_dtype=BF16, tm=512, tn=1024, tk=SEQ, name="grad_w_o_fox")
    g_w_ob = mm(yb, dpb, ta=True, out_dtype=BF16, tm=512, tn=1024, tk=SEQ, name="grad_w_o_dil")
    tok = proj_grads_ready(dict(w_o_fox=g_w_oa, w_o_dil=g_w_ob, w_out=g_w_out))
    dya = mm(dpa, w_oa, tb=True, out_dtype=BF16, tm=2048, tn=512, tk=1024, name="d_y_fox")
    dyb = mm(dpb, w_ob, tb=True, out_dtype=BF16, tm=2048, tn=512, tk=1024, name="d_y_dil")

    qb_slots, do_slots = _fox_pack_bwd(zm, f_cols + tok, lse_a, ya, dya)
    dz, df_cols = _fox_unpack(*_fox_bwd(qb_slots, k_slots, v_slots, do_slots), dz)
    dfa_t, g_b_forget = _fox_post_bwd(df_cols[:, :N_HEADS].T, sg_row)

    delta_b = _attn_delta(yb, dyb, name="delta_dil")
    dil_g = [_dil_bwd(qkv_d, lse_b, delta_b, dyb, d) for _, d in DIL_PATTERNS]
    dz = _dil_grad_combine([g[0] for g in dil_g], [g[1] for g in dil_g], [g[2] for g in dil_g], tabs, dz)

    dzf = jnp.pad(dfa_t.T, ((0, 0), (0, F_PAD - N_HEADS)))
    g_w_main = mm(h1, dz, ta=True, out_dtype=BF16, tm=1024, tn=Z_MAIN // 4, tk=2048, name="grad_w_in")
    g_w_f = mm(h1, dzf, ta=True, out_dtype=BF16, tm=1024, tn=F_PAD, tk=1024, name="grad_w_in_forget")
    tok = mixer_grads_ready(dict(w_main=g_w_main, w_f=g_w_f))
    dh1 = [mm(dz, w_main, tb=True, out_dtype=BF16, tm=512, tn=1024, tk=Z_MAIN, name="d_h1"),
           mm(dzf + tok, w_f, tb=True, out_dtype=F32, tm=2048, tn=1024, tk=F_PAD, name="d_h1_forget")]
    grad_x, gg_pre_mix = _rms_bwd(dh1, x, g_pre_mix, dx2, out_dtype=F32, name="rms_pre_mix_bwd")

    grads = dict(
        b_forget=g_b_forget.reshape(1, N_HEADS), conv_b=_ffn_deinterleave(g_conv_b),
        g_pre_mix=gg_pre_mix, g_post_mix=gg_post_mix, g_pre_ffn=gg_pre_ffn, g_post_ffn=gg_post_ffn)
    return sq_err, grad_x, grads


def _exchange(arrays, scatter, *, name):
    n = len(arrays)

    def body(*refs):
        ins, outs = refs[:n], refs[n:2 * n]
        send_sems, recv_sems, local_sems = refs[2 * n:]
        x, y, c = lax.axis_index("x"), lax.axis_index("y"), lax.axis_index("c")
        me = 4 * x + 2 * y + c
        peers = []
        for k in range(1, N_DEV):
            px = 1 - x if k & 4 else x
            py = 1 - y if k & 2 else y
            pc = 1 - c if k & 1 else c
            peers.append(((px, py, pc), 4 * px + 2 * py + pc))

        def remote(a, k):
            dev, slot = peers[k]
            return pltpu.make_async_remote_copy(
                src_ref=ins[a].at[slot] if scatter else ins[a], dst_ref=outs[a].at[me],
                send_sem=send_sems.at[a, k], recv_sem=recv_sems.at[a, k],
                device_id=dev, device_id_type=MESH_ID)

        def landed(a, k):
            dev, slot = peers[k]
            return pltpu.make_async_remote_copy(
                src_ref=outs[a].at[slot], dst_ref=outs[a].at[slot],
                send_sem=send_sems.at[a, k], recv_sem=recv_sems.at[a, k],
                device_id=dev, device_id_type=MESH_ID)

        own = [pltpu.make_async_copy(ins[a].at[me] if scatter else ins[a], outs[a].at[me], local_sems.at[a])
               for a in range(n)]
        copies = [remote(a, k) for k in range(N_DEV - 1) for a in range(n)]
        for cp in own + copies:
            cp.start()
        for k in range(N_DEV - 1):
            for a in range(n):
                landed(a, k).wait_recv()
        for cp in copies:
            cp.wait_send()
        for cp in own:
            cp.wait()

    out_shape = [_sds(((N_DEV,) + a.shape[-2:]), a.dtype) for a in arrays]
    return pl.pallas_call(
        body, name=name, in_specs=[ANY] * n, out_specs=[ANY] * n, out_shape=out_shape,
        scratch_shapes=[pltpu.SemaphoreType.DMA((n, N_DEV - 1)), pltpu.SemaphoreType.DMA((n, N_DEV - 1)),
                        pltpu.SemaphoreType.DMA((n,))],
    )(*arrays)


def _gather_two_level(shard, *, name):
    def body(x_ref, out_ref, send_sems, recv_sems, local_sem):
        x, y, c = lax.axis_index("x"), lax.axis_index("y"), lax.axis_index("c")
        me, sibling = (x, y, c), (x, y, 1 - c)
        chips = [(1 - x, y), (x, 1 - y), (1 - x, 1 - y)]

        def slot(px, py, pc):
            return out_ref.at[4 * px + 2 * py + pc]

        def copy(k, block, to, src=None):
            return pltpu.make_async_remote_copy(
                src_ref=slot(*block) if src is None else src, dst_ref=slot(*block),
                send_sem=send_sems.at[k], recv_sem=recv_sems.at[k], device_id=to, device_id_type=MESH_ID)

        mine = pltpu.make_async_copy(x_ref, slot(*me), local_sem)
        mine.start()
        first = [copy(0, me, sibling, src=x_ref)]
        first += [copy(1 + j, me, (*chip, c), src=x_ref) for j, chip in enumerate(chips)]
        for cp in first:
            cp.start()
        passed = [copy(4 + j, (*chip, c), sibling) for j, chip in enumerate(chips)]
        for j, chip in enumerate(chips):
            copy(1 + j, (*chip, c), me).wait_recv()
            passed[j].start()
        copy(0, sibling, me).wait_recv()
        for j, chip in enumerate(chips):
            copy(4 + j, (*chip, 1 - c), me).wait_recv()
        for cp in first + passed:
            cp.wait_send()
        mine.wait()

    return pl.pallas_call(
        body, name=name, in_specs=[ANY], out_specs=ANY, out_shape=_sds((N_DEV,) + shard.shape, shard.dtype),
        scratch_shapes=[pltpu.SemaphoreType.DMA((N_DEV - 1,)), pltpu.SemaphoreType.DMA((N_DEV - 1,)),
                        pltpu.SemaphoreType.DMA],
    )(shard)


N_CHIPS = N_DEV // 2


def _peers(chips_only=False):
    x, y, c = lax.axis_index("x"), lax.axis_index("y"), lax.axis_index("c")
    out = []
    if chips_only:
        for k in range(1, N_CHIPS):
            px = 1 - x if k & 2 else x
            py = 1 - y if k & 1 else y
            out.append(((px, py, c), 2 * px + py))
        return 2 * x + y, out
    for k in range(1, N_DEV):
        px = 1 - x if k & 4 else x
        py = 1 - y if k & 2 else y
        pc = 1 - c if k & 1 else c
        out.append(((px, py, pc), 4 * px + 2 * py + pc))
    return 4 * x + 2 * y + c, out


def _sibling_swap(slot_arrays, *, name):
    n = len(slot_arrays)

    def body(*refs):
        ins, outs, send_sems, recv_sems = refs[:n], refs[n:2 * n], refs[2 * n], refs[2 * n + 1]
        x, y, c = lax.axis_index("x"), lax.axis_index("y"), lax.axis_index("c")
        copies = [pltpu.make_async_remote_copy(
            src_ref=ins[a].at[2 * q + (1 - c)], dst_ref=outs[a].at[q], send_sem=send_sems.at[a, q],
            recv_sem=recv_sems.at[a, q], device_id=(x, y, 1 - c), device_id_type=MESH_ID)
            for a in range(n) for q in range(N_CHIPS)]
        for cp in copies:
            cp.start()
        for cp in copies:
            cp.wait_recv()
        for cp in copies:
            cp.wait_send()

    return pl.pallas_call(
        body, name=name, in_specs=[ANY] * n, out_specs=[ANY] * n,
        out_shape=[_sds((N_CHIPS,) + t.shape[1:], t.dtype) for t in slot_arrays],
        scratch_shapes=[pltpu.SemaphoreType.DMA((n, N_CHIPS)), pltpu.SemaphoreType.DMA((n, N_CHIPS))],
    )(*slot_arrays)


def _pair_sum(slots, from_sibling, *, name, tn):
    _, r, c = slots.shape
    core = lax.axis_index("c").astype(jnp.int32).reshape(1)

    def body(core_ref, a_ref, b_ref, o_ref):
        o_ref[...] = (a_ref[...].astype(F32) + b_ref[...].astype(F32)).astype(o_ref.dtype)

    blk = lambda f: pl.BlockSpec((1, r, tn), f)
    return pl.pallas_call(
        body, name=name,
        grid_spec=pltpu.PrefetchScalarGridSpec(
            num_scalar_prefetch=1, grid=(N_CHIPS, c // tn),
            in_specs=[blk(lambda q, j, core: (2 * q + core[0], 0, j)), blk(lambda q, j, core: (q, 0, j))],
            out_specs=blk(lambda q, j, core: (q, 0, j))),
        out_shape=_sds((N_CHIPS, r, c), slots.dtype),
        compiler_params=_params("parallel", "parallel"),
    )(core, slots, from_sibling)


def _sum_parts(parts, *, name, tn):
    n, r, c = parts.shape

    def body(p_ref, o_ref):
        total = p_ref[0].astype(F32)
        for s in range(1, n):
            total = total + p_ref[s].astype(F32)
        o_ref[...] = total

    return pl.pallas_call(
        body, name=name, grid=(c // tn,),
        in_specs=[pl.BlockSpec((n, r, tn), lambda j: (0, 0, j))],
        out_specs=pl.BlockSpec((r, tn), lambda j: (0, j)), out_shape=_sds((r, c), F32),
        compiler_params=_params("parallel"),
    )(parts)


HBM = pl.BlockSpec(memory_space=pltpu.HBM)
SEM = pl.BlockSpec(memory_space=pltpu.SEMAPHORE)
DATAFLOW = pltpu.SideEffectType.DATAFLOW_SIDE_EFFECTING


def _split_copy(srcs, lands, send_sems, recv_sems, scatter, a, k, me, peers, incoming=False):
    dev, slot = peers[k]
    if incoming:
        src = dst = lands[a].at[slot]
    else:
        src, dst = (srcs[a].at[slot] if scatter else srcs[a]), lands[a].at[me]
    sem = a * len(peers) + k
    return pltpu.make_async_remote_copy(
        src_ref=src, dst_ref=dst, send_sem=send_sems.at[sem], recv_sem=recv_sems.at[sem],
        device_id=dev, device_id_type=MESH_ID)


def _exchange_start(arrays, scatter, *, name, chips_only=False):
    n = len(arrays)
    n_slots = N_CHIPS if chips_only else N_DEV

    def body(*refs):
        srcs, lands = refs[:n], refs[n:2 * n]
        send_sems, recv_sems = refs[2 * n], refs[2 * n + 1]
        token = refs[-1]
        me, peers = _peers(chips_only)
        for k in range(len(peers)):
            for a in range(n):
                _split_copy(srcs, lands, send_sems, recv_sems, scatter, a, k, me, peers).start()
        token[...] = jnp.zeros_like(token)

    land_shapes = [((n_slots,) + a.shape[-2:], a.dtype) for a in arrays]
    sems = pltpu.SemaphoreType.DMA((n * (n_slots - 1),))
    outs = pl.pallas_call(
        body, name=name,
        out_shape=(sems, sems, *[pltpu.HBM(a.shape, a.dtype) for a in arrays],
                   *[pltpu.HBM(s, d) for s, d in land_shapes], _sds((SUBLANE, LANE), F32)),
        in_specs=[HBM] * (2 * n),
        out_specs=(SEM, SEM, *[HBM] * (2 * n), pl.BlockSpec(memory_space=pltpu.VMEM)),
        input_output_aliases={i: 2 + i for i in range(2 * n)},
        compiler_params=pltpu.CompilerParams(has_side_effects=DATAFLOW),
    )(*[pltpu.with_memory_space_constraint(a, pltpu.HBM) for a in arrays],
      *[pltpu.with_memory_space_constraint(lax.empty(s, d), pltpu.HBM) for s, d in land_shapes])
    return (outs[0], outs[1], outs[2:2 + n], outs[2 + n:2 + 2 * n], scatter, chips_only), outs[-1]


def _exchange_wait(handles, after, *, name):
    send_sems, recv_sems, srcs, lands, scatter, chips_only = handles
    n = len(srcs)

    def body(*refs):
        src_refs, land_refs = refs[:n], refs[n:2 * n]
        send_ref, recv_ref = refs[2 * n], refs[2 * n + 1]
        me, peers = _peers(chips_only)
        for k in range(len(peers)):
            for a in range(n):
                _split_copy(src_refs, land_refs, send_ref, recv_ref, scatter, a, k, me, peers).wait_send()
                _split_copy(src_refs, land_refs, send_ref, recv_ref, scatter, a, k, me, peers, True).wait_recv()

    outs = pl.pallas_call(
        body, name=name,
        out_shape=tuple(pltpu.HBM(t.shape, t.dtype) for t in (*srcs, *lands)),
        in_specs=[HBM] * (2 * n) + [SEM, SEM, pl.BlockSpec(memory_space=pl.ANY)],
        out_specs=tuple([HBM] * (2 * n)),
        input_output_aliases={i: i for i in range(2 * n)},
        compiler_params=pltpu.CompilerParams(has_side_effects=DATAFLOW),
    )(*srcs, *lands, send_sems, recv_sems, after)
    return _with_own_slot(outs[n:], outs[:n], scatter, chips_only)


def _with_own_slot(landed, own, scatter, chips_only):
    me = 2 * lax.axis_index("x") + lax.axis_index("y")
    if not chips_only:
        me = 2 * me + lax.axis_index("c")
    out = []
    for buf, src in zip(landed, own):
        mine = lax.dynamic_index_in_dim(src, me, 0, keepdims=False) if scatter else src
        out.append(lax.dynamic_update_index_in_dim(buf, mine, me, 0))
    return out


def _adamw(parts, w, m, v, *, name, tm):
    r, c = w.shape
    assert r % tm == 0

    def body(p_ref, w_ref, m_ref, v_ref, g_ref, d_ref, nm_ref, nv_ref):
        _adamw_update(p_ref, w_ref, m_ref, v_ref, g_ref, d_ref, nm_ref, nv_ref)

    blk = pl.BlockSpec((tm, c), lambda i: (i, 0))
    return pl.pallas_call(
        body, name=name, grid=(r // tm,),
        in_specs=[pl.BlockSpec((parts.shape[0], tm, c), lambda i: (0, i, 0)), blk, blk, blk],
        out_specs=[blk] * 4, out_shape=[_sds((r, c), F32)] * 4,
        compiler_params=_params("parallel"),
    )(parts, w, m, v)


def _adamw_update(p_ref, w_ref, m_ref, v_ref, g_ref, d_ref, nm_ref, nv_ref):
    g = p_ref[0].astype(F32)
    for s in range(1, p_ref.shape[0]):
        g = g + p_ref[s].astype(F32)
    g_ref[...] = g
    m_new = ADAM_B1 * m_ref[...] + (1.0 - ADAM_B1) * g
    v_new = ADAM_B2 * v_ref[...] + (1.0 - ADAM_B2) * (g * g)
    nm_ref[...] = m_new
    nv_ref[...] = v_new
    m_hat = m_new / (1.0 - ADAM_B1 ** ADAM_STEP)
    v_hat = v_new / (1.0 - ADAM_B2 ** ADAM_STEP)
    d_ref[...] = -ADAM_LR * (m_hat / (jnp.sqrt(v_hat) + ADAM_EPS) + ADAM_WD * w_ref[...])


SMALL = ("g_pre_mix", "b_forget", "g_post_mix", "g_pre_ffn", "conv_b", "g_post_ffn")


def _adamw_small(parts, ws, ms, vs, sq_err_parts):
    n = len(ws)

    def body(*refs):
        ins, sq_ref, outs, loss_ref = refs[:4 * n], refs[4 * n], refs[4 * n + 1:-1], refs[-1]
        for i in range(n):
            _adamw_update(ins[i], ins[n + i], ins[2 * n + i], ins[3 * n + i], *outs[4 * i:4 * i + 4])
        total = sq_ref[0]
        for s in range(1, N_DEV):
            total = total + sq_ref[s]
        loss_ref[...] = total * (0.5 / D_MODEL)

    res = pl.pallas_call(
        body, name="adamw_small",
        out_shape=[_sds(w.shape, F32) for w in ws for _ in range(4)] + [_sds((1, LANE), F32)],
        compiler_params=pltpu.CompilerParams(vmem_limit_bytes=VMEM_LIMIT),
    )(*parts, *ws, *ms, *vs, sq_err_parts)
    return [res[4 * i:4 * i + 4] for i in range(n)], res[-1][0, 0]


def kernel(x, g_pre_mix, w_in, b_forget, w_o_fox, w_o_dil, w_out, g_post_mix, g_pre_ffn, w_up, conv_w, conv_b, w_down, g_post_ffn, loss_target, m_g_pre_mix, m_w_in, m_b_forget, m_w_o_fox, m_w_o_dil, m_w_out, m_g_post_mix, m_g_pre_ffn, m_w_up, m_conv_w, m_conv_b, m_w_down, m_g_post_ffn, v_g_pre_mix, v_w_in, v_b_forget, v_w_o_fox, v_w_o_dil, v_w_out, v_g_post_mix, v_g_pre_ffn, v_w_up, v_conv_w, v_conv_b, v_w_down, v_g_post_ffn):
    names = ("g_pre_mix", "w_in", "b_forget", "w_o_fox", "w_o_dil", "w_out", "g_post_mix", "g_pre_ffn",
             "w_up", "conv_w", "conv_b", "w_down", "g_post_ffn")
    w = dict(g_pre_mix=g_pre_mix, w_in=w_in, b_forget=b_forget, w_o_fox=w_o_fox, w_o_dil=w_o_dil, w_out=w_out,
             g_post_mix=g_post_mix, g_pre_ffn=g_pre_ffn, w_up=w_up, conv_w=conv_w, conv_b=conv_b, w_down=w_down,
             g_post_ffn=g_post_ffn)
    m = dict(g_pre_mix=m_g_pre_mix, w_in=m_w_in, b_forget=m_b_forget, w_o_fox=m_w_o_fox, w_o_dil=m_w_o_dil,
             w_out=m_w_out, g_post_mix=m_g_post_mix, g_pre_ffn=m_g_pre_ffn, w_up=m_w_up, conv_w=m_conv_w,
             conv_b=m_conv_b, w_down=m_w_down, g_post_ffn=m_g_post_ffn)
    v = dict(g_pre_mix=v_g_pre_mix, w_in=v_w_in, b_forget=v_b_forget, w_o_fox=v_w_o_fox, w_o_dil=v_w_o_dil,
             w_out=v_w_out, g_post_mix=v_g_post_mix, g_pre_ffn=v_g_pre_ffn, w_up=v_w_up, conv_w=v_conv_w,
             conv_b=v_conv_b, w_down=v_w_down, g_post_ffn=v_g_post_ffn)
    sharded = ("w_in", "w_o_fox", "w_o_dil", "w_out", "w_up", "w_down", "conv_w")
    wire = lambda n: F32 if n == "conv_w" else BF16

    by_cols = lambda t: jnp.transpose(t, (1, 0, 2)).reshape(t.shape[1], N_DEV * t.shape[2])
    by_rows = lambda t: t.reshape(N_DEV * t.shape[1], t.shape[2])
    col_slots = lambda t: jnp.transpose(t.reshape(t.shape[0], N_DEV, t.shape[1] // N_DEV), (1, 0, 2))
    row_slots = lambda t: t.reshape(N_DEV, t.shape[0] // N_DEV, t.shape[1])
    to_slots = lambda n, t: (row_slots if n in ("w_out", "w_down") else col_slots)(t).astype(wire(n))
    shard = lambda n: w[n][0].astype(wire(n))
    f_lo, f_hi = 3 * ATT_W, 3 * ATT_W + N_HEADS

    w_in_full = by_cols(_gather_two_level(shard("w_in"), name="gather_w_in"))
    w_main = jnp.concatenate([w_in_full[:, :f_lo], w_in_full[:, f_hi:]], axis=1)
    w_f = jnp.pad(w_in_full[:, f_lo:f_hi], ((0, 0), (0, F_PAD - N_HEADS)))
    late = ("w_o_fox", "w_o_dil", "w_out", "w_up", "conv_w", "w_down")
    order = jnp.minimum(jnp.abs(w_in_full[0, 0].astype(F32)), 0.0)
    late_handles, late_tok = _exchange_start(
        [shard(n) + order.astype(wire(n)) if n == "conv_w" else shard(n) for n in late], False,
        name="gather_late_start")

    def late_weights(after):
        got = dict(zip(late, _exchange_wait(late_handles, after, name="gather_late_wait")))
        return (by_cols(got["w_o_fox"]), by_cols(got["w_o_dil"]), by_rows(got["w_out"]),
                _ffn_move_blocks(by_cols(got["w_up"]), interleave=True, name="w_up_cols"),
                _ffn_interleave(by_cols(got["conv_w"])),
                by_rows(got["w_down"]))

    pending = {}

    def ffn_grads_ready(g):
        pending["ffn"] = _exchange_start([to_slots(n, g[n]) for n in ("w_down", "w_up", "conv_w")], True,
                                         name="scatter_ffn_start")
        return pending["ffn"][1][0, 0]

    def proj_grads_ready(g):
        pending["proj"] = _exchange_start([to_slots(n, g[n]) for n in ("w_o_fox", "w_o_dil", "w_out")], True,
                                          name="scatter_proj_start")
        return pending["proj"][1][0, 0]

    def mixer_grads_ready(g):
        slabs = [g["w_main"].reshape(N_DEV, D_MODEL // N_DEV, Z_MAIN), g["w_f"].reshape(N_DEV, D_MODEL // N_DEV, F_PAD)]
        theirs = _sibling_swap(slabs, name="scatter_w_in_swap")
        chip_sums = [_pair_sum(slabs[0], theirs[0], name="scatter_w_in_pair_sum", tn=Z_MAIN // 4),
                     _pair_sum(slabs[1], theirs[1], name="scatter_w_in_forget_pair_sum", tn=F_PAD)]
        pending["w_in"] = _exchange_start(chip_sums, True, name="scatter_w_in_start", chips_only=True)
        return pending["w_in"][1][0, 0]

    sq_err, grad_x, g = _local_step(
        x[0], loss_target[0], w_main, w_f, b_forget, conv_b, g_pre_mix + late_tok[0, 0], g_post_mix, g_pre_ffn,
        g_post_ffn, late_weights, ffn_grads_ready, proj_grads_ready, mixer_grads_ready)

    tiles = dict(w_in=256, w_o_fox=512, w_o_dil=512, w_out=128, w_up=256, w_down=176, conv_w=3)
    adam = lambda n, p: _adamw(p, w[n][0], m[n][0], v[n][0], name=f"adamw_{n}", tm=tiles[n])
    res = {}
    for key, group in (("ffn", ("w_down", "w_up", "conv_w")), ("proj", ("w_o_fox", "w_o_dil", "w_out"))):
        landed = _exchange_wait(pending[key][0], grad_x, name=f"scatter_{key}_wait")
        res.update({n: adam(n, p) for n, p in zip(group, landed)})
    small_parts = _exchange([g[n] for n in SMALL] + [sq_err], False, name="gather_small_grads")
    done = res["w_up"][3]
    main_parts, f_parts = _exchange_wait(pending["w_in"][0], done, name="scatter_w_in_wait")
    slab_main = _sum_parts(main_parts, name="scatter_w_in_sum", tn=Z_MAIN // 4)
    slab_f = _sum_parts(f_parts, name="scatter_w_in_forget_sum", tn=F_PAD)
    slab = jnp.concatenate([slab_main[:, :f_lo], slab_f[:, :N_HEADS], slab_main[:, f_lo:]], axis=1)
    rows = _exchange([col_slots(slab)], True, name="scatter_w_in_rows")[0]
    res["w_in"] = adam("w_in", rows.reshape(1, D_MODEL, rows.shape[-1]))
    small, loss = _adamw_small(small_parts[:-1], *[[t[n] for n in SMALL] for t in (w, m, v)], small_parts[-1])
    small = dict(zip(SMALL, small))
    out = [[(res[n][k][None] if n in sharded else small[n][k]) for n in names] for k in range(4)]
    return (loss, grad_x[None], *out[0], *out[1], *out[2], *out[3])
```

```python
import functools
import math

import jax
import jax.numpy as jnp
import numpy as np
from jax import lax
from jax.experimental import pallas as pl
from jax.experimental.pallas import tpu as pltpu

F32 = jnp.float32
BF16 = jnp.bfloat16

SEQ = 4096
D_MODEL = 1024
N_HEADS = 8
HEAD_DIM = 64
ATT_W = N_HEADS * HEAD_DIM
D_FF = 2816
Z_MAIN = 5120
F_PAD = 128
ROPE_DIM = 16
ROPE_THETA = 500000.0
RMS_EPS = 1e-6
NEG_INF = -1e30
SCALE = 1.0 / math.sqrt(HEAD_DIM)
DIL_PATTERNS = ((128, 1), (512, 4), (2048, 16))
DIL_BLK = 128
N_DEV = 8

ADAM_LR = 0.001
ADAM_B1 = 0.9
ADAM_B2 = 0.999
ADAM_EPS = 1e-08
ADAM_WD = 0.01
ADAM_STEP = 10

LANE = 128
SUBLANE = 8
VMEM_LIMIT = 56 * 1024 * 1024
MESH_ID = pl.DeviceIdType.MESH
ANY = pl.BlockSpec(memory_space=pl.ANY)


def _params(*sem):
    return pltpu.CompilerParams(dimension_semantics=sem, vmem_limit_bytes=VMEM_LIMIT)


def _sds(shape, dtype):
    return jax.ShapeDtypeStruct(shape, dtype)


def _matmul(a, b, *, ta=False, tb=False, out_dtype, tm, tn, tk, name, b_k_off=0):
    if ta:
        kk, m = a.shape
    else:
        m, kk = a.shape
    n = b.shape[0] if tb else b.shape[1]
    tm, tn, tk = min(tm, m), min(tn, n), min(tk, kk)
    assert (b.shape[1] if tb else b.shape[0]) >= b_k_off * tk + kk
    assert m % tm == 0 and n % tn == 0 and kk % tk == 0, (name, m, n, kk, tm, tn, tk)
    nk = kk // tk
    dims = (((0 if ta else 1,), (1 if tb else 0,)), ((), ()))

    def body(a_ref, b_ref, o_ref, *scratch):
        p = lax.dot_general(a_ref[...].astype(BF16), b_ref[...].astype(BF16), dims,
                            preferred_element_type=F32)
        if nk == 1:
            o_ref[...] = p.astype(o_ref.dtype)
        else:
            acc = scratch[0]
            k = pl.program_id(2)

            @pl.when(k == 0)
            def _():
                acc[...] = p

            @pl.when(k > 0)
            def _():
                acc[...] += p

            @pl.when(k == nk - 1)
            def _():
                o_ref[...] = acc[...].astype(o_ref.dtype)

    a_spec = (pl.BlockSpec((tk, tm), lambda i, j, k: (k, i)) if ta
              else pl.BlockSpec((tm, tk), lambda i, j, k: (i, k)))
    b_spec = (pl.BlockSpec((tn, tk), lambda i, j, k: (j, k + b_k_off)) if tb
              else pl.BlockSpec((tk, tn), lambda i, j, k: (k + b_k_off, j)))
    return pl.pallas_call(
        body, name=name, grid=(m // tm, n // tn, nk),
        in_specs=[a_spec, b_spec],
        out_specs=pl.BlockSpec((tm, tn), lambda i, j, k: (i, j)),
        out_shape=_sds((m, n), out_dtype),
        scratch_shapes=[pltpu.VMEM((tm, tn), F32)] if nk > 1 else [],
        compiler_params=_params("parallel", "parallel", "arbitrary"),
    )(a, b)


def _rms_fwd(x, g, *, name, tm=512):
    def body(x_ref, g_ref, h_ref):
        xv = x_ref[...]
        r = lax.rsqrt(jnp.mean(xv * xv, axis=-1, keepdims=True) + RMS_EPS)
        h_ref[...] = (xv * r * g_ref[...]).astype(h_ref.dtype)

    return pl.pallas_call(
        body, name=name, grid=(SEQ // tm,),
        in_specs=[pl.BlockSpec((tm, D_MODEL), lambda i: (i, 0)), pl.BlockSpec((1, D_MODEL), lambda i: (0, 0))],
        out_specs=pl.BlockSpec((tm, D_MODEL), lambda i: (i, 0)),
        out_shape=_sds((SEQ, D_MODEL), BF16),
        compiler_params=_params("parallel"),
    )(x, g)


def _rms_bwd(dh_parts, xin, g, dres, *, out_dtype, name, tm=512):
    n_parts = len(dh_parts)
    has_res = dres is not None

    def body(*refs):
        parts = refs[:n_parts]
        x_ref, g_ref = refs[n_parts], refs[n_parts + 1]
        res_ref = refs[n_parts + 2] if has_res else None
        o_ref, gg_ref = refs[-2], refs[-1]
        dh = parts[0][...].astype(F32)
        for p in parts[1:]:
            dh = dh + p[...].astype(F32)
        xv = x_ref[...]
        r = lax.rsqrt(jnp.mean(xv * xv, axis=-1, keepdims=True) + RMS_EPS)
        xn = xv * r

        @pl.when(pl.program_id(0) == 0)
        def _():
            gg_ref[...] = jnp.zeros_like(gg_ref)

        gg_ref[...] += jnp.sum(dh * xn, axis=0, keepdims=True)
        dxn = dh * g_ref[...]
        dx = r * (dxn - xn * jnp.mean(dxn * xn, axis=-1, keepdims=True))
        if has_res:
            dx = dx + res_ref[...]
        o_ref[...] = dx.astype(o_ref.dtype)

    row = pl.BlockSpec((tm, D_MODEL), lambda i: (i, 0))
    vec = pl.BlockSpec((1, D_MODEL), lambda i: (0, 0))
    args = list(dh_parts) + [xin, g] + ([dres] if has_res else [])
    return pl.pallas_call(
        body, name=name, grid=(SEQ // tm,),
        in_specs=[row] * n_parts + [row, vec] + ([row] if has_res else []),
        out_specs=[row, vec],
        out_shape=[_sds((SEQ, D_MODEL), out_dtype), _sds((1, D_MODEL), F32)],
        compiler_params=_params("arbitrary"),
    )(*args)


def _rms_pair_bwd(dh_parts, x2, g_pre, dres, y1, g_post, *, tm=512):
    n_parts = len(dh_parts)

    def norm_bwd(dh, xin, g_ref, gg_ref):
        r = lax.rsqrt(jnp.mean(xin * xin, axis=-1, keepdims=True) + RMS_EPS)
        xn = xin * r
        gg_ref[...] += jnp.sum(dh * xn, axis=0, keepdims=True)
        dxn = dh * g_ref[...]
        return r * (dxn - xn * jnp.mean(dxn * xn, axis=-1, keepdims=True))

    def body(*refs):
        parts = refs[:n_parts]
        x2_ref, gpre_ref, res_ref, y1_ref, gpost_ref, dx2_ref, dy1_ref, ggpre_ref, ggpost_ref = refs[n_parts:]

        @pl.when(pl.program_id(0) == 0)
        def _():
            ggpre_ref[...] = jnp.zeros_like(ggpre_ref)
            ggpost_ref[...] = jnp.zeros_like(ggpost_ref)

        dh = parts[0][...].astype(F32)
        for p in parts[1:]:
            dh = dh + p[...].astype(F32)
        dx2 = res_ref[...] + norm_bwd(dh, x2_ref[...], gpre_ref, ggpre_ref)
        dx2_ref[...] = dx2
        dy1_ref[...] = norm_bwd(dx2, y1_ref[...], gpost_ref, ggpost_ref).astype(dy1_ref.dtype)

    row = pl.BlockSpec((tm, D_MODEL), lambda i: (i, 0))
    vec = pl.BlockSpec((1, D_MODEL), lambda i: (0, 0))
    return pl.pallas_call(
        body, name="rms_pair_bwd", grid=(SEQ // tm,),
        in_specs=[row] * n_parts + [row, vec, row, row, vec],
        out_specs=[row, row, vec, vec],
        out_shape=[_sds((SEQ, D_MODEL), F32), _sds((SEQ, D_MODEL), BF16), _sds((1, D_MODEL), F32),
                   _sds((1, D_MODEL), F32)],
        compiler_params=_params("arbitrary"),
    )(*dh_parts, x2, g_pre, dres, y1, g_post)


SCAN_BLK = 512


def _split_dot(v, tri):
    hi = v.astype(BF16)
    r1 = v - hi.astype(F32)
    mid = r1.astype(BF16)
    lo = (r1 - mid.astype(F32)).astype(BF16)
    dot = functools.partial(jnp.dot, preferred_element_type=F32)
    return dot(hi, tri) + dot(mid, tri) + dot(lo, tri)


def _fox_prep(fa_t, b_col):
    nblk = SEQ // SCAN_BLK

    def body(fa_ref, b_ref, f_ref, sg_ref):
        row = lax.broadcasted_iota(jnp.int32, (SCAN_BLK, SCAN_BLK), 0)
        col = lax.broadcasted_iota(jnp.int32, (SCAN_BLK, SCAN_BLK), 1)
        upper = (row <= col).astype(BF16)
        carry = jnp.zeros((N_HEADS, 1), F32)
        for blk in range(nblk):
            sl = pl.ds(blk * SCAN_BLK, SCAN_BLK)
            xx = fa_ref[:, sl] + b_ref[...]
            e = jnp.exp(-jnp.abs(xx))
            logf = jnp.minimum(xx, 0.0) - jnp.log(1.0 + e)
            sg_ref[:, sl] = jnp.where(xx >= 0.0, e, 1.0) / (1.0 + e)
            c = _split_dot(logf, upper) + carry
            f_ref[:, sl] = c
            carry = c[:, SCAN_BLK - 1:SCAN_BLK]

    return pl.pallas_call(
        body, name="fox_prep",
        out_shape=[_sds((N_HEADS, SEQ), F32), _sds((N_HEADS, SEQ), F32)],
        compiler_params=pltpu.CompilerParams(vmem_limit_bytes=VMEM_LIMIT),
    )(fa_t, b_col)


def _fox_post_bwd(df_t, sg_t):
    nblk = SEQ // SCAN_BLK

    def body(df_ref, sg_ref, dfa_ref, gb_ref):
        row = lax.broadcasted_iota(jnp.int32, (SCAN_BLK, SCAN_BLK), 0)
        col = lax.broadcasted_iota(jnp.int32, (SCAN_BLK, SCAN_BLK), 1)
        lower = (row >= col).astype(BF16)
        carry = jnp.zeros((N_HEADS, 1), F32)
        gb = jnp.zeros((N_HEADS, 1), F32)
        for blk in reversed(range(nblk)):
            sl = pl.ds(blk * SCAN_BLK, SCAN_BLK)
            c = _split_dot(df_ref[:, sl], lower) + carry
            carry = c[:, 0:1]
            dfa = c * sg_ref[:, sl]
            dfa_ref[:, sl] = dfa
            gb = gb + jnp.sum(dfa, axis=1, keepdims=True)
        gb_ref[...] = gb

    return pl.pallas_call(
        body, name="fox_post_bwd",
        out_shape=[_sds((N_HEADS, SEQ), F32), _sds((N_HEADS, 1), F32)],
        compiler_params=pltpu.CompilerParams(vmem_limit_bytes=VMEM_LIMIT),
    )(df_t, sg_t)


FOX_T = 512
NT_DIMS = (((1,), (1,)), ((), ()))
TN_DIMS = (((0,), (0,)), ((), ()))


def _head(ref_or_val, h):
    return ref_or_val[:, h * HEAD_DIM:(h + 1) * HEAD_DIM]


def _split3(v):
    hi = v.astype(BF16).astype(F32)
    r1 = v - hi
    mid = r1.astype(BF16).astype(F32)
    return hi, mid, (r1 - mid).astype(BF16).astype(F32)


ONE_LANE = 3 * N_HEADS


def _pack_terms(v, with_one):
    hi, mid, lo = _split3(v)
    t = hi + pltpu.roll(mid, N_HEADS, 1) + pltpu.roll(lo, 2 * N_HEADS, 1)
    if with_one:
        t = t + (lax.broadcasted_iota(jnp.int32, v.shape, 1) == ONE_LANE).astype(F32)
    return t.astype(BF16)


def _aux_matrices():
    to_q = np.zeros((LANE, N_HEADS * 2 * HEAD_DIM), np.float32)
    to_k = np.zeros_like(to_q)
    for h in range(N_HEADS):
        base = h * 2 * HEAD_DIM + HEAD_DIM
        for s in range(3):
            to_q[s * N_HEADS + h, base + s] = 1.0
            to_q[ONE_LANE, base + 3 + s] = 1.0
            to_k[ONE_LANE, base + s] = 1.0
            to_k[s * N_HEADS + h, base + 3 + s] = -1.0
    return jnp.asarray(to_q, BF16), jnp.asarray(to_k, BF16)


def _head_sums():
    total = np.zeros((N_HEADS * HEAD_DIM, LANE), np.float32)
    first = np.zeros_like(total)
    for h in range(N_HEADS):
        total[h * HEAD_DIM:(h + 1) * HEAD_DIM, h] = 1.0
        first[h * HEAD_DIM, h] = 1.0
    return jnp.asarray(total, BF16), jnp.asarray(first, BF16)


SLOT = 2 * HEAD_DIM
N_SPLIT = 3
FOX_FWD_HEADS = 8
FOX_BWD_HEADS = 4


def _slot(ref, h):
    return ref[:, h * SLOT:(h + 1) * SLOT]


def _fox_pack_fwd(zm, f_cols, *, tm=512):
    def body(q_ref, k_ref, v_ref, f_ref, tq_ref, tk_ref, qs_ref, ks_ref, vs_ref):
        ones = jnp.ones((tm, HEAD_DIM), BF16)
        terms = _pack_terms(f_ref[...], True)
        q_aux = jnp.dot(terms, tq_ref[...], preferred_element_type=F32).astype(BF16)
        k_aux = jnp.dot(terms, tk_ref[...], preferred_element_type=F32).astype(BF16)
        for h in range(N_HEADS):
            aux = slice(h * SLOT + HEAD_DIM, (h + 1) * SLOT)
            qs_ref[:, h * SLOT:(h + 1) * SLOT] = jnp.concatenate(
                [(_head(q_ref, h).astype(F32) * SCALE).astype(BF16), q_aux[:, aux]], axis=1)
            ks_ref[:, h * SLOT:(h + 1) * SLOT] = jnp.concatenate([_head(k_ref, h), k_aux[:, aux]], axis=1)
            vs_ref[:, h * SLOT:(h + 1) * SLOT] = jnp.concatenate([_head(v_ref, h), ones], axis=1)

    col = lambda b: pl.BlockSpec((tm, ATT_W), lambda i: (i, b))
    wide = pl.BlockSpec((tm, N_HEADS * SLOT), lambda i: (i, 0))
    const = pl.BlockSpec((LANE, N_HEADS * SLOT), lambda i: (0, 0))
    return pl.pallas_call(
        body, name="fox_pack_fwd", grid=(SEQ // tm,),
        in_specs=[col(0), col(1), col(2), pl.BlockSpec((tm, LANE), lambda i: (i, 0)), const, const],
        out_specs=[wide] * 3, out_shape=[_sds((SEQ, N_HEADS * SLOT), BF16)] * 3,
        compiler_params=_params("parallel"),
    )(zm, zm, zm, f_cols, *_aux_matrices())


def _fox_pack_bwd(zm, f_cols, lse, o, do, *, tm=512):
    def body(q_ref, f_ref, lse_ref, o_ref, do_ref, tq_ref, total_ref, first_ref, qs_ref, ds_ref):
        delta = _split_dot(o_ref[...].astype(F32) * do_ref[...].astype(F32), total_ref[...])
        lse_h = _split_dot(lse_ref[...], first_ref[...])
        q_aux = jnp.dot(_pack_terms(f_ref[...] - lse_h, True), tq_ref[...], preferred_element_type=F32).astype(BF16)
        d_aux = jnp.dot(_pack_terms(-delta, False), tq_ref[...], preferred_element_type=F32).astype(BF16)
        for h in range(N_HEADS):
            aux = slice(h * SLOT + HEAD_DIM, (h + 1) * SLOT)
            qs_ref[:, h * SLOT:(h + 1) * SLOT] = jnp.concatenate(
                [(_head(q_ref, h).astype(F32) * SCALE).astype(BF16), q_aux[:, aux]], axis=1)
            ds_ref[:, h * SLOT:(h + 1) * SLOT] = jnp.concatenate([_head(do_ref, h), d_aux[:, aux]], axis=1)

    row = pl.BlockSpec((tm, ATT_W), lambda i: (i, 0))
    wide = pl.BlockSpec((tm, N_HEADS * SLOT), lambda i: (i, 0))
    const = lambda r, c: pl.BlockSpec((r, c), lambda i: (0, 0))
    return pl.pallas_call(
        body, name="fox_pack_bwd", grid=(SEQ // tm,),
        in_specs=[row, pl.BlockSpec((tm, LANE), lambda i: (i, 0)), row, row, row,
                  const(LANE, N_HEADS * SLOT), const(ATT_W, LANE), const(ATT_W, LANE)],
        out_specs=[wide] * 2, out_shape=[_sds((SEQ, N_HEADS * SLOT), BF16)] * 2,
        compiler_params=_params("parallel"),
    )(zm, f_cols, lse, o, do, _aux_matrices()[0], *_head_sums())


def _causal_pairs(key_major):
    nb = SEQ // FOX_T
    if key_major:
        pairs = [(i, j) for j in range(nb) for i in range(j, nb)]
    else:
        pairs = [(i, j) for i in range(nb) for j in range(i + 1)]
    return (jnp.array([p[0] for p in pairs], jnp.int32), jnp.array([p[1] for p in pairs], jnp.int32), len(pairs))


def _diag_mask():
    row = lax.broadcasted_iota(jnp.int32, (FOX_T, FOX_T), 0)
    col = lax.broadcasted_iota(jnp.int32, (FOX_T, FOX_T), 1)
    return col <= row


def _fox_fwd(q_slots, k_slots, v_slots):
    i_tab, j_tab, n_pairs = _causal_pairs(False)

    def body(i_tab, j_tab, q_ref, k_ref, v_ref, o_ref, lse_ref, m_s, acc_s):
        t = pl.program_id(1)
        i, j = i_tab[t], j_tab[t]

        @pl.when(j == 0)
        def _():
            m_s[...] = jnp.full_like(m_s, NEG_INF)
            acc_s[...] = jnp.zeros_like(acc_s)

        def step(masked):
            scores = [lax.dot_general(_slot(q_ref, h), _slot(k_ref, h), NT_DIMS, preferred_element_type=F32)
                      for h in range(FOX_FWD_HEADS)]
            probs, alphas = [], []
            for h in range(FOX_FWD_HEADS):
                s = jnp.where(_diag_mask(), scores[h], NEG_INF) if masked else scores[h]
                m_prev = m_s[h]
                m_new = jnp.maximum(m_prev, jnp.max(s, axis=-1, keepdims=True))
                probs.append(jnp.exp(s - jnp.tile(m_new, (1, FOX_T // LANE))).astype(BF16))
                alphas.append(jnp.exp(m_prev - m_new))
                m_s[h] = m_new
            for h in range(FOX_FWD_HEADS):
                acc_s[h] = alphas[h] * acc_s[h] + jnp.dot(probs[h], _slot(v_ref, h), preferred_element_type=F32)

        @pl.when(j < i)
        def _():
            step(False)

        @pl.when(j == i)
        def _():
            step(True)
            outs, lses = [], []
            for h in range(FOX_FWD_HEADS):
                acc = acc_s[h]
                l = acc[:, HEAD_DIM:]
                outs.append(acc[:, :HEAD_DIM] / l)
                lses.append(m_s[h][:, :HEAD_DIM] + jnp.log(l))
            o_ref[...] = jnp.concatenate(outs, axis=1).astype(o_ref.dtype)
            lse_ref[...] = jnp.concatenate(lses, axis=1)

    qspec = pl.BlockSpec((FOX_T, FOX_FWD_HEADS * SLOT), lambda p, t, it, jt: (it[t], p))
    kspec = pl.BlockSpec((FOX_T, FOX_FWD_HEADS * SLOT), lambda p, t, it, jt: (jt[t], p))
    ospec = pl.BlockSpec((FOX_T, FOX_FWD_HEADS * HEAD_DIM), lambda p, t, it, jt: (it[t], p))
    return pl.pallas_call(
        body, name="fox_fwd",
        grid_spec=pltpu.PrefetchScalarGridSpec(
            num_scalar_prefetch=2, grid=(N_HEADS // FOX_FWD_HEADS, n_pairs),
            in_specs=[qspec, kspec, kspec], out_specs=[ospec, ospec],
            scratch_shapes=[pltpu.VMEM((FOX_FWD_HEADS, FOX_T, LANE), F32),
                            pltpu.VMEM((FOX_FWD_HEADS, FOX_T, SLOT), F32)]),
        out_shape=[_sds((SEQ, ATT_W), BF16), _sds((SEQ, ATT_W), F32)],
        compiler_params=_params("parallel", "arbitrary"),
    )(i_tab, j_tab, q_slots, k_slots, v_slots)


def _fox_bwd(q_slots, k_slots, v_slots, do_slots):
    i_tab, j_tab, n_pairs = _causal_pairs(True)

    def body(i_tab, j_tab, q_ref, k_ref, v_ref, do_ref, dq_ref, dk_ref, dv_ref):
        t = pl.program_id(1)
        i, j = i_tab[t], j_tab[t]

        @pl.when(t == 0)
        def _():
            dq_ref[...] = jnp.zeros_like(dq_ref)

        @pl.when(i == j)
        def _():
            dk_ref[...] = jnp.zeros_like(dk_ref)
            dv_ref[...] = jnp.zeros_like(dv_ref)

        def step(masked):
            rows = pl.ds(pl.multiple_of(i * FOX_T, FOX_T), FOX_T)
            heads = range(FOX_BWD_HEADS)
            scores = [lax.dot_general(_slot(q_ref, h), _slot(k_ref, h), NT_DIMS, preferred_element_type=F32)
                      for h in heads]
            dps = [lax.dot_general(_slot(do_ref, h), _slot(v_ref, h), NT_DIMS, preferred_element_type=F32)
                   for h in heads]
            ps, dss = [], []
            for h in heads:
                p = jnp.exp(scores[h])
                if masked:
                    p = jnp.where(_diag_mask(), p, 0.0)
                ps.append(p.astype(BF16))
                dss.append((p * dps[h]).astype(BF16))
            for h in heads:
                cols = slice(h * SLOT, (h + 1) * SLOT)
                dv_ref[:, cols] += lax.dot_general(ps[h], _slot(do_ref, h), TN_DIMS, preferred_element_type=F32)
                dk_ref[:, cols] += lax.dot_general(dss[h], _slot(q_ref, h), TN_DIMS, preferred_element_type=F32)
                dq_ref[rows, cols] += jnp.dot(dss[h], _slot(k_ref, h), preferred_element_type=F32)

        @pl.when(i > j)
        def _():
            step(False)

        @pl.when(i == j)
        def _():
            step(True)

    qspec = pl.BlockSpec((FOX_T, FOX_BWD_HEADS * SLOT), lambda p, t, it, jt: (it[t], p))
    kspec = pl.BlockSpec((FOX_T, FOX_BWD_HEADS * SLOT), lambda p, t, it, jt: (jt[t], p))
    return pl.pallas_call(
        body, name="fox_bwd",
        grid_spec=pltpu.PrefetchScalarGridSpec(
            num_scalar_prefetch=2, grid=(N_HEADS // FOX_BWD_HEADS, n_pairs),
            in_specs=[qspec, kspec, kspec, qspec],
            out_specs=[pl.BlockSpec((SEQ, FOX_BWD_HEADS * SLOT), lambda p, t, it, jt: (0, p)), kspec, kspec]),
        out_shape=[_sds((SEQ, N_HEADS * SLOT), F32)] * 3,
        compiler_params=_params("arbitrary", "arbitrary"),
    )(i_tab, j_tab, q_slots, k_slots, v_slots, do_slots)


def _fox_unpack(dq_slots, dk_slots, dv_slots, dz, *, tm=512):
    def body(dq_ref, dk_ref, dv_ref, dz_in, o_ref, df_ref):
        lane = lax.broadcasted_iota(jnp.int32, (tm, LANE), 1)
        df = jnp.zeros((tm, LANE), F32)
        for h in range(N_HEADS):
            lo = h * SLOT
            for part, (ref, mult) in enumerate(((dq_ref, SCALE), (dk_ref, 1.0), (dv_ref, 1.0))):
                o_ref[:, part * ATT_W + h * HEAD_DIM:part * ATT_W + (h + 1) * HEAD_DIM] = (
                    ref[:, lo:lo + HEAD_DIM] * mult).astype(o_ref.dtype)
            rows = dq_ref[:, lo + HEAD_DIM:lo + HEAD_DIM + 1]
            cols = dk_ref[:, lo + HEAD_DIM + N_SPLIT:lo + HEAD_DIM + N_SPLIT + 1]
            df = jnp.where(lane == h, rows - cols, df)
        df_ref[...] = df

    wide = pl.BlockSpec((tm, N_HEADS * SLOT), lambda i: (i, 0))
    return pl.pallas_call(
        body, name="fox_unpack", grid=(SEQ // tm,), in_specs=[wide] * 3 + [ANY],
        out_specs=[pl.BlockSpec((tm, 3 * ATT_W), lambda i: (i, 0)), pl.BlockSpec((tm, LANE), lambda i: (i, 0))],
        out_shape=[_sds((SEQ, Z_MAIN), BF16), _sds((SEQ, LANE), F32)],
        input_output_aliases={3: 0},
        compiler_params=_params("parallel"),
    )(dq_slots, dk_slots, dv_slots, dz)


def _attn_delta(o, do, *, name, tm=512):
    def body(o_ref, do_ref, d_ref):
        prod = o_ref[...].astype(F32) * do_ref[...].astype(F32)
        lane = lax.broadcasted_iota(jnp.int32, (tm, LANE), 1)
        out = jnp.zeros((tm, LANE), F32)
        for h in range(N_HEADS):
            out = jnp.where(lane == h, jnp.sum(_head(prod, h), axis=1, keepdims=True), out)
        d_ref[...] = out

    row = pl.BlockSpec((tm, ATT_W), lambda i: (i, 0))
    return pl.pallas_call(
        body, name=name, grid=(SEQ // tm,), in_specs=[row, row],
        out_specs=pl.BlockSpec((tm, LANE), lambda i: (i, 0)), out_shape=_sds((SEQ, LANE), F32),
        compiler_params=_params("parallel"),
    )(o, do)


def _rope_tables():
    half = ROPE_DIM // 2
    inv_freq = np.float32(ROPE_THETA) ** (-np.arange(half, dtype=np.float32) * np.float32(2.0) / np.float32(ROPE_DIM))
    ang = np.arange(SEQ, dtype=np.float32)[:, None] * inv_freq.astype(np.float32)[None, :]
    cos, sin = jnp.asarray(np.cos(ang).astype(np.float32)), jnp.asarray(np.sin(ang).astype(np.float32))
    ones = jnp.ones((SEQ, HEAD_DIM - ROPE_DIM), F32)
    zeros = jnp.zeros((SEQ, HEAD_DIM - ROPE_DIM), F32)
    zh = jnp.zeros((SEQ, half), F32)
    c_tab = jnp.concatenate([cos, cos, ones], axis=1)
    a_tab = jnp.concatenate([-sin, zh, zeros], axis=1)
    b_tab = jnp.concatenate([zh, sin, zeros], axis=1)
    two = lambda t: jnp.concatenate([t, t], axis=1)
    return two(c_tab), two(a_tab), two(b_tab)


def _rotate(x, c_tab, a_tab, b_tab):
    return x * c_tab + pltpu.roll(x, LANE - ROPE_DIM // 2, 1) * a_tab + pltpu.roll(x, ROPE_DIM // 2, 1) * b_tab


def _rope_fwd(zm, tabs, *, tm=512):
    def body(q_ref, k_ref, v_ref, c_ref, a_ref, b_ref, o_ref):
        for part, (x_ref, mult) in enumerate(((q_ref, SCALE), (k_ref, 1.0))):
            for cc in range(ATT_W // LANE):
                sl = slice(cc * LANE, (cc + 1) * LANE)
                rot = _rotate(x_ref[:, sl].astype(F32), c_ref[...], a_ref[...], b_ref[...])
                o_ref[:, part * ATT_W + cc * LANE:part * ATT_W + (cc + 1) * LANE] = (rot * mult).astype(o_ref.dtype)
        o_ref[:, 2 * ATT_W:] = v_ref[...]

    tab = pl.BlockSpec((tm, LANE), lambda i: (i, 0))
    col = lambda b: pl.BlockSpec((tm, ATT_W), lambda i: (i, b))
    return pl.pallas_call(
        body, name="rope_fwd", grid=(SEQ // tm,),
        in_specs=[col(3), col(4), col(5), tab, tab, tab],
        out_specs=pl.BlockSpec((tm, 3 * ATT_W), lambda i: (i, 0)),
        out_shape=_sds((SEQ, 3 * ATT_W), BF16),
        compiler_params=_params("parallel"),
    )(zm, zm, zm, *tabs)


def _dil_grad_combine(dqs, dks, dvs, tabs, dz, *, tm=256):
    def body(*refs):
        q_refs, k_refs, v_refs = refs[0:3], refs[3:6], refs[6:9]
        c_ref, a_ref, b_ref, _, o_ref = refs[9:]
        total = lambda rs, sl: rs[0][:, sl].astype(F32) + rs[1][:, sl].astype(F32) + rs[2][:, sl].astype(F32)
        for cc in range(ATT_W // LANE):
            sl = slice(cc * LANE, (cc + 1) * LANE)
            for part, rs in enumerate((q_refs, k_refs)):
                o_ref[:, part * ATT_W + cc * LANE:part * ATT_W + (cc + 1) * LANE] = _rotate(
                    total(rs, sl), c_ref[...], -a_ref[...], -b_ref[...]).astype(o_ref.dtype)
            o_ref[:, 2 * ATT_W + cc * LANE:2 * ATT_W + (cc + 1) * LANE] = total(v_refs, sl).astype(o_ref.dtype)

    row = pl.BlockSpec((tm, ATT_W), lambda i: (i, 0))
    tab = pl.BlockSpec((tm, LANE), lambda i: (i, 0))
    return pl.pallas_call(
        body, name="dil_grad_combine", grid=(SEQ // tm,),
        in_specs=[row] * 9 + [tab] * 3 + [ANY],
        out_specs=pl.BlockSpec((tm, 3 * ATT_W), lambda i: (i, 1)),
        out_shape=_sds((SEQ, Z_MAIN), BF16),
        input_output_aliases={12: 0},
        compiler_params=_params("parallel"),
    )(*dqs, *dks, *dvs, *tabs, dz)


def _dil_valid(n):
    qi = lax.broadcasted_iota(jnp.int32, (DIL_BLK, 2 * DIL_BLK), 0)
    ki = lax.broadcasted_iota(jnp.int32, (DIL_BLK, 2 * DIL_BLK), 1)
    dist = qi + DIL_BLK - ki
    return (dist >= 0) & (dist <= DIL_BLK) & ((n > 0) | (ki >= DIL_BLK))


def _dil_fwd(qkv, d):
    length = SEQ // d
    nb = length // DIL_BLK
    qkv_v = qkv.reshape(length, d * 3 * ATT_W)

    def body(q_ref, kp_ref, kc_ref, vp_ref, vc_ref, o_ref, lse_ref):
        m_step = pl.program_id(1)
        lane = lax.broadcasted_iota(jnp.int32, (DIL_BLK, LANE), 1)
        jobs = [(sub, h) for sub in range(2) for h in range(N_HEADS)]
        rows = lambda sub: slice(sub * DIL_BLK, (sub + 1) * DIL_BLK)
        cols = lambda h: slice(h * HEAD_DIM, (h + 1) * HEAD_DIM)

        def keys(prev_ref, cur_ref, sub, h):
            before = prev_ref[:, cols(h)] if sub == 0 else cur_ref[rows(0), cols(h)]
            return jnp.concatenate([before, cur_ref[rows(sub), cols(h)]], axis=0)

        scores = [lax.dot_general(q_ref[rows(sub), cols(h)], keys(kp_ref, kc_ref, sub, h), NT_DIMS,
                                  preferred_element_type=F32) for sub, h in jobs]
        ok = [_dil_valid(m_step), _dil_valid(1)]
        probs, inv_l, lse_all = [], [], [jnp.zeros((DIL_BLK, LANE), F32)] * 2
        for idx, (sub, h) in enumerate(jobs):
            s = jnp.where(ok[sub], scores[idx], NEG_INF)
            m = jnp.max(s, axis=-1, keepdims=True)
            p = jnp.exp(s - m)
            l = jnp.sum(p, axis=-1, keepdims=True)
            probs.append(p.astype(BF16))
            inv_l.append(1.0 / l)
            lse_all[sub] = jnp.where(lane == h, m + jnp.log(l), lse_all[sub])
        outs = [jnp.dot(probs[idx], keys(vp_ref, vc_ref, sub, h), preferred_element_type=F32) * inv_l[idx]
                for idx, (sub, h) in enumerate(jobs)]
        for sub in range(2):
            o_ref[rows(sub), :] = jnp.concatenate(outs[sub * N_HEADS:(sub + 1) * N_HEADS], axis=1).astype(o_ref.dtype)
            lse_ref[rows(sub), :] = lse_all[sub]

    pair = lambda f: pl.BlockSpec((2 * DIL_BLK, ATT_W), f)
    one = lambda f: pl.BlockSpec((DIL_BLK, ATT_W), f)
    before = lambda m: jnp.maximum(2 * m - 1, 0)
    o, lse = pl.pallas_call(
        body, name=f"dil_fwd_d{d}", grid=(d, nb // 2),
        in_specs=[pair(lambda r, m: (m, 3 * r)),
                  one(lambda r, m: (before(m), 3 * r + 1)), pair(lambda r, m: (m, 3 * r + 1)),
                  one(lambda r, m: (before(m), 3 * r + 2)), pair(lambda r, m: (m, 3 * r + 2))],
        out_specs=[pair(lambda r, m: (m, r)), pl.BlockSpec((2 * DIL_BLK, LANE), lambda r, m: (m, r))],
        out_shape=[_sds((length, d * ATT_W), BF16), _sds((length, d * LANE), F32)],
        compiler_params=_params("parallel", "arbitrary"),
    )(qkv_v, qkv_v, qkv_v, qkv_v, qkv_v)
    return o.reshape(SEQ, ATT_W), lse.reshape(SEQ, LANE)


def _dil_merge(os_, lses, *, tm=512):
    def body(o0, o1, o2, l0, l1, l2, y_ref, lse_ref):
        ls = [l0[...], l1[...], l2[...]]
        m = jnp.maximum(jnp.maximum(ls[0], ls[1]), ls[2])
        es = [jnp.exp(l - m) for l in ls]
        tot = es[0] + es[1] + es[2]
        lse_ref[...] = m + jnp.log(tot)
        alphas = [e / tot for e in es]
        outs = []
        for h in range(N_HEADS):
            acc = None
            for g, o_ref in enumerate((o0, o1, o2)):
                term = alphas[g][:, h:h + 1] * _head(o_ref, h).astype(F32)
                acc = term if acc is None else acc + term
            outs.append(acc)
        y_ref[...] = jnp.concatenate(outs, axis=1).astype(y_ref.dtype)

    row = pl.BlockSpec((tm, ATT_W), lambda i: (i, 0))
    vec = pl.BlockSpec((tm, LANE), lambda i: (i, 0))
    return pl.pallas_call(
        body, name="dil_merge", grid=(SEQ // tm,),
        in_specs=[row] * 3 + [vec] * 3, out_specs=[row, vec],
        out_shape=[_sds((SEQ, ATT_W), BF16), _sds((SEQ, LANE), F32)],
        compiler_params=_params("parallel"),
    )(*os_, *lses)


def _dil_bwd(qkv, lse, delta, do, d):
    length = SEQ // d
    nb = length // DIL_BLK
    n_steps = nb // 2
    qkv_v = qkv.reshape(length, d * 3 * ATT_W)
    lse_v, dl_v, do_v = lse.reshape(length, d * LANE), delta.reshape(length, d * LANE), do.reshape(length, d * ATT_W)

    def body(q_ref, kp_ref, kc_ref, vp_ref, vc_ref, lse_ref, dl_ref, do_ref, dq_ref, dk_ref, dv_ref, dk_s, dv_s):
        m_step = pl.program_id(1)

        @pl.when(m_step == 0)
        def _():
            dk_s[...] = jnp.zeros_like(dk_s)
            dv_s[...] = jnp.zeros_like(dv_s)

        jobs = [(sub, h) for sub in range(2) for h in range(N_HEADS)]
        rows = lambda sub: slice(sub * DIL_BLK, (sub + 1) * DIL_BLK)
        cols = lambda h: slice(h * HEAD_DIM, (h + 1) * HEAD_DIM)

        def keys(prev_ref, cur_ref, sub, h):
            before = prev_ref[:, cols(h)] if sub == 0 else cur_ref[rows(0), cols(h)]
            return jnp.concatenate([before, cur_ref[rows(sub), cols(h)]], axis=0)

        kks = [keys(kp_ref, kc_ref, sub, h) for sub, h in jobs]
        scores = [lax.dot_general(q_ref[rows(sub), cols(h)], kks[idx], NT_DIMS, preferred_element_type=F32)
                  for idx, (sub, h) in enumerate(jobs)]
        dps = [lax.dot_general(do_ref[rows(sub), cols(h)], keys(vp_ref, vc_ref, sub, h), NT_DIMS,
                               preferred_element_type=F32) for sub, h in jobs]
        ok = [_dil_valid(m_step), _dil_valid(1)]
        ps, dss = [], []
        for idx, (sub, h) in enumerate(jobs):
            p = jnp.where(ok[sub], jnp.exp(scores[idx] - lse_ref[rows(sub), h:h + 1]), 0.0)
            ps.append(p.astype(BF16))
            dss.append((p * (dps[idx] - dl_ref[rows(sub), h:h + 1])).astype(BF16))
        dqs = [jnp.dot(dss[idx], kks[idx], preferred_element_type=F32) * SCALE for idx in range(len(jobs))]
        dkks = [lax.dot_general(dss[idx], q_ref[rows(sub), cols(h)], TN_DIMS, preferred_element_type=F32)
                for idx, (sub, h) in enumerate(jobs)]
        dvvs = [lax.dot_general(ps[idx], do_ref[rows(sub), cols(h)], TN_DIMS, preferred_element_type=F32)
                for idx, (sub, h) in enumerate(jobs)]
        for sub in range(2):
            dq_ref[rows(sub), :] = jnp.concatenate(dqs[sub * N_HEADS:(sub + 1) * N_HEADS], axis=1).astype(dq_ref.dtype)
        base = m_step * (2 * DIL_BLK)
        blocks = [pl.ds(pl.multiple_of(jnp.maximum(base - DIL_BLK, 0), DIL_BLK), DIL_BLK),
                  pl.ds(pl.multiple_of(base, DIL_BLK), DIL_BLK),
                  pl.ds(pl.multiple_of(base + DIL_BLK, DIL_BLK), DIL_BLK)]
        for acc, parts in ((dk_s, dkks), (dv_s, dvvs)):
            top = lambda sub: jnp.concatenate([parts[sub * N_HEADS + h][:DIL_BLK] for h in range(N_HEADS)], axis=1)
            bottom = lambda sub: jnp.concatenate([parts[sub * N_HEADS + h][DIL_BLK:] for h in range(N_HEADS)], axis=1)
            acc[blocks[0], :] += top(0)
            acc[blocks[1], :] += bottom(0) + top(1)
            acc[blocks[2], :] += bottom(1)

        @pl.when(m_step == n_steps - 1)
        def _():
            dk_ref[...] = dk_s[...].astype(dk_ref.dtype)
            dv_ref[...] = dv_s[...].astype(dv_ref.dtype)

    pair = lambda f: pl.BlockSpec((2 * DIL_BLK, ATT_W), f)
    one = lambda f: pl.BlockSpec((DIL_BLK, ATT_W), f)
    vec = lambda f: pl.BlockSpec((2 * DIL_BLK, LANE), f)
    whole = pl.BlockSpec((length, ATT_W), lambda r, m: (0, r))
    before = lambda m: jnp.maximum(2 * m - 1, 0)
    outs = pl.pallas_call(
        body, name=f"dil_bwd_d{d}", grid=(d, n_steps),
        in_specs=[pair(lambda r, m: (m, 3 * r)),
                  one(lambda r, m: (before(m), 3 * r + 1)), pair(lambda r, m: (m, 3 * r + 1)),
                  one(lambda r, m: (before(m), 3 * r + 2)), pair(lambda r, m: (m, 3 * r + 2)),
                  vec(lambda r, m: (m, r)), vec(lambda r, m: (m, r)), pair(lambda r, m: (m, r))],
        out_specs=[pair(lambda r, m: (m, r)), whole, whole],
        out_shape=[_sds((length, d * ATT_W), BF16)] * 3,
        scratch_shapes=[pltpu.VMEM((length, ATT_W), F32), pltpu.VMEM((length, ATT_W), F32)],
        compiler_params=_params("arbitrary", "arbitrary"),
    )(qkv_v, qkv_v, qkv_v, qkv_v, qkv_v, lse_v, dl_v, do_v)
    return [t.reshape(SEQ, ATT_W) for t in outs]


def _sigmoid(x):
    return 1.0 / (1.0 + jnp.exp(-x))


def _mix_fwd(ya, yb, w_oa, w_ob, zm, *, tm=512):
    def body(ya_ref, yb_ref, wa_ref, wb_ref, ga_ref, gb_ref, pa_ref, pb_ref, mix_ref):
        pa = jnp.dot(ya_ref[...], wa_ref[...], preferred_element_type=F32)
        pb = jnp.dot(yb_ref[...], wb_ref[...], preferred_element_type=F32)
        pa_ref[...] = pa.astype(pa_ref.dtype)
        pb_ref[...] = pb.astype(pb_ref.dtype)
        mix_ref[...] = (_sigmoid(ga_ref[...].astype(F32)) * pa + _sigmoid(gb_ref[...].astype(F32)) * pb
                        ).astype(mix_ref.dtype)

    row = pl.BlockSpec((tm, ATT_W), lambda i: (i, 0))
    wsp = pl.BlockSpec((ATT_W, D_MODEL), lambda i: (0, 0))
    wide = pl.BlockSpec((tm, D_MODEL), lambda i: (i, 0))
    return pl.pallas_call(
        body, name="mix_fwd", grid=(SEQ // tm,),
        in_specs=[row, row, wsp, wsp, pl.BlockSpec((tm, D_MODEL), lambda i: (i, 3)),
                  pl.BlockSpec((tm, D_MODEL), lambda i: (i, 4))],
        out_specs=[wide] * 3, out_shape=[_sds((SEQ, D_MODEL), BF16)] * 3,
        compiler_params=_params("parallel"),
    )(ya, yb, w_oa, w_ob, zm, zm)


def _gate_bwd(dmix, zm, p, gate_block, dz, *, name, tm=512):
    def body(dm_ref, g_ref, p_ref, *rest):
        dp_ref, dz_ref = rest[-2], rest[-1]
        dm = dm_ref[...].astype(F32)
        s = _sigmoid(g_ref[...].astype(F32))
        dp_ref[...] = (dm * s).astype(dp_ref.dtype)
        dz_ref[...] = (dm * p_ref[...].astype(F32) * s * (1.0 - s)).astype(dz_ref.dtype)

    wide = pl.BlockSpec((tm, D_MODEL), lambda i: (i, 0))
    gate = pl.BlockSpec((tm, D_MODEL), lambda i: (i, gate_block))
    extra = [] if dz is None else [dz]
    return pl.pallas_call(
        body, name=name, grid=(SEQ // tm,),
        in_specs=[wide, gate, wide] + [ANY] * len(extra),
        out_specs=[wide, gate],
        out_shape=[_sds((SEQ, D_MODEL), BF16), _sds((SEQ, Z_MAIN), BF16)],
        input_output_aliases={3: 1} if extra else {},
        compiler_params=_params("parallel"),
    )(dmix, zm, p, *extra)


def _out_fwd(mixed, w_out, x, g_post, g_pre, *, tm=512):
    def body(m_ref, w_ref, x_ref, gp_ref, gn_ref, y_ref, x2_ref, h_ref):
        y = jnp.dot(m_ref[...], w_ref[...], preferred_element_type=F32)
        y_ref[...] = y
        r = lax.rsqrt(jnp.mean(y * y, axis=-1, keepdims=True) + RMS_EPS)
        x2 = x_ref[...] + y * r * gp_ref[...]
        x2_ref[...] = x2
        r2 = lax.rsqrt(jnp.mean(x2 * x2, axis=-1, keepdims=True) + RMS_EPS)
        h_ref[...] = (x2 * r2 * gn_ref[...]).astype(h_ref.dtype)

    row = pl.BlockSpec((tm, D_MODEL), lambda i: (i, 0))
    vec = pl.BlockSpec((1, D_MODEL), lambda i: (0, 0))
    return pl.pallas_call(
        body, name="out_fwd", grid=(SEQ // tm,),
        in_specs=[row, pl.BlockSpec((D_MODEL, D_MODEL), lambda i: (0, 0)), row, vec, vec],
        out_specs=[row] * 3,
        out_shape=[_sds((SEQ, D_MODEL), F32), _sds((SEQ, D_MODEL), F32), _sds((SEQ, D_MODEL), BF16)],
        compiler_params=_params("parallel"),
    )(mixed, w_out, x, g_post, g_pre)


FFN_TM = 512
FFN_HALF = 256
FFN_TN = 2 * FFN_HALF
FFN_NJ = D_FF // FFN_HALF
FFN_GROUP = 2 * SUBLANE


def _ffn_interleave(t):
    lead = t.shape[:-1]
    return jnp.swapaxes(t.reshape(*lead, 2, FFN_NJ, FFN_HALF), -3, -2).reshape(*lead, 2 * D_FF)


def _ffn_deinterleave(t):
    lead = t.shape[:-1]
    return jnp.swapaxes(t.reshape(*lead, FFN_NJ, 2, FFN_HALF), -3, -2).reshape(*lead, 2 * D_FF)


def _ffn_move_blocks(t, *, interleave, name):
    rows = t.shape[0]
    if interleave:
        src = lambda jb: (0, (jb % 2) * FFN_NJ + jb // 2)
    else:
        src = lambda jb: (0, 2 * (jb % FFN_NJ) + jb // FFN_NJ)

    def body(x_ref, o_ref):
        o_ref[...] = x_ref[...]

    return pl.pallas_call(
        body, name=name, grid=(2 * FFN_NJ,),
        in_specs=[pl.BlockSpec((rows, FFN_HALF), src)],
        out_specs=pl.BlockSpec((rows, FFN_HALF), lambda jb: (0, jb)),
        out_shape=_sds(t.shape, t.dtype),
        compiler_params=_params("parallel"),
    )(t)


def _gelu_parts(a):
    c = math.sqrt(2.0 / math.pi)
    a2 = a * a
    t = jnp.tanh((c * a) * (1.0 + 0.044715 * a2))
    half_a, one_t = 0.5 * a, 1.0 + t
    gelu = half_a * one_t
    dgelu = 0.5 * one_t + half_a * (1.0 - t * t) * (c + (3.0 * 0.044715 * c) * a2)
    return gelu, dgelu


def _row_masks(down):
    row = lax.broadcasted_iota(jnp.int32, (SUBLANE, FFN_TN), 0)
    return (row < 1, row < 2) if down else (row >= SUBLANE - 1, row >= SUBLANE - 2)


def _rolled(x, down):
    return (pltpu.roll(x, 1, 0), pltpu.roll(x, 2, 0)) if down else (
        pltpu.roll(x, SUBLANE - 1, 0), pltpu.roll(x, SUBLANE - 2, 0))


def _shifted(cur_rolled, neighbour_rolled, masks):
    return (jnp.where(masks[0], neighbour_rolled[0], cur_rolled[0]),
            jnp.where(masks[1], neighbour_rolled[1], cur_rolled[1]))


def _conv_consts(w_ref, b_ref):
    shape = (SUBLANE, FFN_TN)
    return [jnp.broadcast_to(w_ref[k:k + 1, :], shape) for k in range(3)] + [jnp.broadcast_to(b_ref[...], shape)]


def _ffn_mid_fwd(u, conv_w, conv_b):
    per = FFN_TM // SUBLANE

    def body(u_ref, h_ref, w_ref, b_ref, m_ref, ab_ref):
        live = (pl.program_id(1) > 0).astype(F32)
        w0, w1, w2, bias = _conv_consts(w_ref, b_ref)
        masks = _row_masks(True)

        def group(g, above):
            rows = pl.ds(pl.multiple_of(g * FFN_GROUP, FFN_GROUP), FFN_GROUP)
            x = u_ref[rows, :].astype(F32)
            convs = []
            for c in range(2):
                cur = x[c * SUBLANE:(c + 1) * SUBLANE]
                cur_rolled = _rolled(cur, True)
                s1, s2 = _shifted(cur_rolled, above, masks)
                convs.append(w0 * s2 + w1 * s1 + w2 * cur + bias)
                above = cur_rolled
            y = jnp.concatenate(convs, axis=0)
            ab_ref[rows, :] = y.astype(ab_ref.dtype)
            m_ref[rows, :] = (_gelu_parts(y[:, :FFN_HALF])[0] * y[:, FFN_HALF:]).astype(m_ref.dtype)
            return above

        lax.fori_loop(0, FFN_TM // (2 * FFN_GROUP), lambda g2, carry: group(2 * g2 + 1, group(2 * g2, carry)),
                      _rolled(h_ref[...].astype(F32) * live, True))

    blk = pl.BlockSpec((FFN_TM, FFN_TN), lambda j, i: (i, j))
    return pl.pallas_call(
        body, name="ffn_mid_fwd", grid=(FFN_NJ, SEQ // FFN_TM),
        in_specs=[blk, pl.BlockSpec((SUBLANE, FFN_TN), lambda j, i: (jnp.maximum(i * per - 1, 0), j)),
                  pl.BlockSpec((3, FFN_TN), lambda j, i: (0, j)), pl.BlockSpec((1, FFN_TN), lambda j, i: (0, j))],
        out_specs=[pl.BlockSpec((FFN_TM, FFN_HALF), lambda j, i: (i, j)), blk],
        out_shape=[_sds((SEQ, D_FF), BF16), _sds((SEQ, 2 * D_FF), BF16)],
        compiler_params=_params("parallel", "arbitrary"),
    )(u, u, conv_w, conv_b)


def _ffn_mid_bwd(dm, u, ab, conv_w):
    nrow = SEQ // FFN_TM
    n_groups = FFN_TM // FFN_GROUP

    def body(dm_ref, u_ref, ab_ref, w_ref, du_ref, gw_ref, gb_ref, c_s):
        @pl.when(pl.program_id(1) == 0)
        def _():
            c_s[...] = jnp.zeros_like(c_s)
            gw_ref[...] = jnp.zeros_like(gw_ref)
            gb_ref[...] = jnp.zeros_like(gb_ref)

        taps = [jnp.broadcast_to(w_ref[k:k + 1, :], (SUBLANE, FFN_TN)) for k in range(3)]
        masks = _row_masks(False)

        def group(t, carry):
            below, acc = carry
            rows = pl.ds(pl.multiple_of((n_groups - 1 - t) * FFN_GROUP, FFN_GROUP), FFN_GROUP)
            x, y, dmv = u_ref[rows, :].astype(F32), ab_ref[rows, :].astype(F32), dm_ref[rows, :].astype(F32)
            gelu, dgelu = _gelu_parts(y[:, :FFN_HALF])
            d = jnp.concatenate([dmv * y[:, FFN_HALF:] * dgelu, dmv * gelu], axis=1)
            acc, pre = list(acc), [None, None]
            for c in (1, 0):
                sl = slice(c * SUBLANE, (c + 1) * SUBLANE)
                cur, xs = d[sl], x[sl]
                cur_rolled = _rolled(cur, False)
                up1, up2 = _shifted(cur_rolled, below, masks)
                acc = [acc[0] + up2 * xs, acc[1] + up1 * xs, acc[2] + cur * xs, acc[3] + cur]
                pre[c] = taps[2] * cur + taps[1] * up1 + taps[0] * up2
                below = cur_rolled
            du_ref[rows, :] = jnp.concatenate(pre, axis=0).astype(du_ref.dtype)
            return below, tuple(acc)

        zeros = jnp.zeros((SUBLANE, FFN_TN), F32)
        below, acc = lax.fori_loop(0, n_groups // 2, lambda t2, carry: group(2 * t2 + 1, group(2 * t2, carry)),
                                   (_rolled(c_s[...], False), (zeros,) * 4))
        c_s[...] = pltpu.roll(below[0], 1, 0)
        for k in range(3):
            gw_ref[k:k + 1, :] += jnp.sum(acc[k], axis=0, keepdims=True)
        gb_ref[...] += jnp.sum(acc[3], axis=0, keepdims=True)

    blk = pl.BlockSpec((FFN_TM, FFN_TN), lambda j, i: (nrow - 1 - i, j))
    return pl.pallas_call(
        body, name="ffn_mid_bwd", grid=(FFN_NJ, nrow),
        in_specs=[pl.BlockSpec((FFN_TM, FFN_HALF), lambda j, i: (nrow - 1 - i, j)), blk, blk,
                  pl.BlockSpec((3, FFN_TN), lambda j, i: (0, j))],
        out_specs=[blk, pl.BlockSpec((3, FFN_TN), lambda j, i: (0, j)), pl.BlockSpec((1, FFN_TN), lambda j, i: (0, j))],
        out_shape=[_sds((SEQ, 2 * D_FF), BF16), _sds((3, 2 * D_FF), F32), _sds((1, 2 * D_FF), F32)],
        scratch_shapes=[pltpu.VMEM((SUBLANE, FFN_TN), F32)],
        compiler_params=_params("parallel", "arbitrary"),
    )(dm, u, ab, conv_w)


def _down_fwd(m, w_down, x2, g_post, target, *, tm=512):
    def body(m_ref, w_ref, x2_ref, g_ref, t_ref, dout_ref, dy_ref, gg_ref, loss_ref):
        @pl.when(pl.program_id(0) == 0)
        def _():
            gg_ref[...] = jnp.zeros_like(gg_ref)
            loss_ref[...] = jnp.zeros_like(loss_ref)

        y = jnp.dot(m_ref[...], w_ref[...], preferred_element_type=F32)
        r = lax.rsqrt(jnp.mean(y * y, axis=-1, keepdims=True) + RMS_EPS)
        yn = y * r
        diff = (x2_ref[...] + yn * g_ref[...]) - t_ref[...]
        loss_ref[...] += jnp.sum(diff * diff)
        dout = diff * (1.0 / D_MODEL)
        dout_ref[...] = dout
        gg_ref[...] += jnp.sum(dout * yn, axis=0, keepdims=True)
        dn = dout * g_ref[...]
        dy_ref[...] = (r * (dn - yn * jnp.mean(dn * yn, axis=-1, keepdims=True))).astype(dy_ref.dtype)

    row = pl.BlockSpec((tm, D_MODEL), lambda i: (i, 0))
    vec = pl.BlockSpec((1, D_MODEL), lambda i: (0, 0))
    return pl.pallas_call(
        body, name="down_fwd", grid=(SEQ // tm,),
        in_specs=[pl.BlockSpec((tm, D_FF), lambda i: (i, 0)), pl.BlockSpec((D_FF, D_MODEL), lambda i: (0, 0)),
                  row, vec, row],
        out_specs=[row, row, vec, pl.BlockSpec((1, LANE), lambda i: (0, 0))],
        out_shape=[_sds((SEQ, D_MODEL), F32), _sds((SEQ, D_MODEL), BF16), _sds((1, D_MODEL), F32),
                   _sds((1, LANE), F32)],
        compiler_params=_params("arbitrary"),
    )(m, w_down, x2, g_post, target)


def _local_step(x, target, w_main, w_f, b_forget, conv_b, g_pre_mix, g_post_mix, g_pre_ffn, g_post_ffn,
                late_weights, ffn_grads_ready, proj_grads_ready, mixer_grads_ready):
    mm = _matmul
    tabs = _rope_tables()

    h1 = _rms_fwd(x, g_pre_mix, name="rms_pre_mix")
    zm = mm(h1, w_main, out_dtype=BF16, tm=2048, tn=512, tk=1024, name="in_proj")
    zf = mm(h1, w_f, out_dtype=F32, tm=2048, tn=F_PAD, tk=1024, name="in_proj_forget")
    f_row, sg_row = _fox_prep(zf[:, :N_HEADS].T, b_forget.reshape(N_HEADS, 1))
    f_cols = jnp.pad(f_row.T, ((0, 0), (0, LANE - N_HEADS)))
    q_slots, k_slots, v_slots = _fox_pack_fwd(zm, f_cols)
    ya, lse_a = _fox_fwd(q_slots, k_slots, v_slots)
    qkv_d = _rope_fwd(zm, tabs)
    dil = [_dil_fwd(qkv_d, d) for _, d in DIL_PATTERNS]
    yb, lse_b = _dil_merge([o for o, _ in dil], [l for _, l in dil])
    w_oa, w_ob, w_out, w_up, conv_w, w_down = late_weights(yb)
    pa, pb, mixed = _mix_fwd(ya, yb, w_oa, w_ob, zm)
    y1, x2, h2 = _out_fwd(mixed, w_out, x, g_post_mix, g_pre_ffn)
    u = mm(h2, w_up, out_dtype=BF16, tm=2048, tn=512, tk=1024, name="up_proj")
    m, ab = _ffn_mid_fwd(u, conv_w, _ffn_interleave(conv_b))
    dout, dy2, gg_post_ffn, sq_err = _down_fwd(m, w_down, x2, g_post_ffn, target)

    g_w_down = mm(m, dy2, ta=True, out_dtype=BF16, tm=D_FF // 2, tn=1024, tk=2048, name="grad_w_down")
    dm = mm(dy2, w_down, tb=True, out_dtype=BF16, tm=2048, tn=D_FF // 2, tk=1024, name="d_ffn_mid")
    du, g_conv_w, g_conv_b = _ffn_mid_bwd(dm, u, ab, conv_w)
    g_w_up = mm(h2, du, ta=True, out_dtype=BF16, tm=1024, tn=D_FF // 2, tk=2048, name="grad_w_up")
    tok = ffn_grads_ready(dict(w_down=g_w_down, w_up=_ffn_move_blocks(g_w_up, interleave=False, name="grad_w_up_cols"),
                               conv_w=_ffn_deinterleave(g_conv_w)))
    dh2 = mm(du, w_up, tb=True, out_dtype=BF16, tm=512, tn=1024, tk=2 * D_FF, name="d_h2")

    dx2, dy1, gg_pre_ffn, gg_post_mix = _rms_pair_bwd([dh2], x2, g_pre_ffn, dout, y1, g_post_mix + tok)
    g_w_out = mm(mixed, dy1, ta=True, out_dtype=BF16, tm=1024, tn=1024, tk=2048, name="grad_w_out")
    dmix = mm(dy1, w_out, tb=True, out_dtype=BF16, tm=2048, tn=1024, tk=1024, name="d_mixed")
    dpa, dz = _gate_bwd(dmix, zm, pa, 3, None, name="gate_bwd_fox")
    dpb, dz = _gate_bwd(dmix, zm, pb, 4, dz, name="gate_bwd_dil")
    g_w_oa = mm(ya, dpa, ta=True, out_dtype=BF16, tm=512, tn=1024, tk=SEQ, name="grad_w_o_fox")
    g_w_ob = mm(yb, dpb, ta=True, out_dtype=BF16, tm=512, tn=1024, tk=SEQ, name="grad_w_o_dil")
    tok = proj_grads_ready(dict(w_o_fox=g_w_oa, w_o_dil=g_w_ob, w_out=g_w_out))
    dya = mm(dpa, w_oa, tb=True, out_dtype=BF16, tm=2048, tn=512, tk=1024, name="d_y_fox")
    dyb = mm(dpb, w_ob, tb=True, out_dtype=BF16, tm=2048, tn=512, tk=1024, name="d_y_dil")

    qb_slots, do_slots = _fox_pack_bwd(zm, f_cols + tok, lse_a, ya, dya)
    dz, df_cols = _fox_unpack(*_fox_bwd(qb_slots, k_slots, v_slots, do_slots), dz)
    dfa_t, g_b_forget = _fox_post_bwd(df_cols[:, :N_HEADS].T, sg_row)

    delta_b = _attn_delta(yb, dyb, name="delta_dil")
    dil_g = [_dil_bwd(qkv_d, lse_b, delta_b, dyb, d) for _, d in DIL_PATTERNS]
    dz = _dil_grad_combine([g[0] for g in dil_g], [g[1] for g in dil_g], [g[2] for g in dil_g], tabs, dz)

    dzf = jnp.pad(dfa_t.T, ((0, 0), (0, F_PAD - N_HEADS)))
    g_w_main = mm(h1, dz, ta=True, out_dtype=BF16, tm=1024, tn=Z_MAIN // 4, tk=2048, name="grad_w_in")
    g_w_f = mm(h1, dzf, ta=True, out_dtype=BF16, tm=1024, tn=F_PAD, tk=1024, name="grad_w_in_forget")
    tok = mixer_grads_ready(dict(w_main=g_w_main, w_f=g_w_f))
    dh1 = [mm(dz, w_main, tb=True, out_dtype=BF16, tm=512, tn=1024, tk=Z_MAIN, name="d_h1"),
           mm(dzf + tok, w_f, tb=True, out_dtype=BF16, tm=2048, tn=1024, tk=F_PAD, name="d_h1_forget")]
    grad_x, gg_pre_mix = _rms_bwd(dh1, x, g_pre_mix, dx2, out_dtype=F32, name="rms_pre_mix_bwd")

    grads = dict(
        b_forget=g_b_forget.reshape(1, N_HEADS), conv_b=_ffn_deinterleave(g_conv_b),
        g_pre_mix=gg_pre_mix, g_post_mix=gg_post_mix, g_pre_ffn=gg_pre_ffn, g_post_ffn=gg_post_ffn)
    return sq_err, grad_x, grads


def _exchange(arrays, scatter, *, name):
    n = len(arrays)
    scatters = [scatter] * n if isinstance(scatter, bool) else list(scatter)

    def body(*refs):
        ins, outs = refs[:n], refs[n:2 * n]
        send_sems, recv_sems, local_sems = refs[2 * n:]
        me, peers = _peers()

        def remote(a, k):
            dev, slot = peers[k]
            return pltpu.make_async_remote_copy(
                src_ref=ins[a].at[slot] if scatters[a] else ins[a], dst_ref=outs[a].at[me],
                send_sem=send_sems.at[a, k], recv_sem=recv_sems.at[a, k],
                device_id=dev, device_id_type=MESH_ID)

        def landed(a, k):
            dev, slot = peers[k]
            return pltpu.make_async_remote_copy(
                src_ref=outs[a].at[slot], dst_ref=outs[a].at[slot],
                send_sem=send_sems.at[a, k], recv_sem=recv_sems.at[a, k],
                device_id=dev, device_id_type=MESH_ID)

        own = [pltpu.make_async_copy(ins[a].at[me] if scatters[a] else ins[a], outs[a].at[me], local_sems.at[a])
               for a in range(n)]
        copies = [remote(a, k) for k in range(N_DEV - 1) for a in range(n)]
        for cp in own + copies:
            cp.start()
        for k in range(N_DEV - 1):
            for a in range(n):
                landed(a, k).wait_recv()
        for cp in copies:
            cp.wait_send()
        for cp in own:
            cp.wait()

    out_shape = [_sds(((N_DEV,) + a.shape[-2:]), a.dtype) for a in arrays]
    return pl.pallas_call(
        body, name=name, in_specs=[ANY] * n, out_specs=[ANY] * n, out_shape=out_shape,
        scratch_shapes=[pltpu.SemaphoreType.DMA((n, N_DEV - 1)), pltpu.SemaphoreType.DMA((n, N_DEV - 1)),
                        pltpu.SemaphoreType.DMA((n,))],
    )(*arrays)


def _gather_two_level(shard, *, name):
    def body(x_ref, out_ref, send_sems, recv_sems, local_sem):
        x, y, c = lax.axis_index("x"), lax.axis_index("y"), lax.axis_index("c")
        me, sibling = (x, y, c), (x, y, 1 - c)
        chips = [(1 - x, y), (x, 1 - y), (1 - x, 1 - y)]

        def slot(px, py, pc):
            return out_ref.at[4 * px + 2 * py + pc]

        def copy(k, block, to, src=None):
            return pltpu.make_async_remote_copy(
                src_ref=slot(*block) if src is None else src, dst_ref=slot(*block),
                send_sem=send_sems.at[k], recv_sem=recv_sems.at[k], device_id=to, device_id_type=MESH_ID)

        mine = pltpu.make_async_copy(x_ref, slot(*me), local_sem)
        mine.start()
        first = [copy(0, me, sibling, src=x_ref)]
        first += [copy(1 + j, me, (*chip, c), src=x_ref) for j, chip in enumerate(chips)]
        for cp in first:
            cp.start()
        passed = [copy(4 + j, (*chip, c), sibling) for j, chip in enumerate(chips)]
        for j, chip in enumerate(chips):
            copy(1 + j, (*chip, c), me).wait_recv()
            passed[j].start()
        copy(0, sibling, me).wait_recv()
        for j, chip in enumerate(chips):
            copy(4 + j, (*chip, 1 - c), me).wait_recv()
        for cp in first + passed:
            cp.wait_send()
        mine.wait()

    return pl.pallas_call(
        body, name=name, in_specs=[ANY], out_specs=ANY, out_shape=_sds((N_DEV,) + shard.shape, shard.dtype),
        scratch_shapes=[pltpu.SemaphoreType.DMA((N_DEV - 1,)), pltpu.SemaphoreType.DMA((N_DEV - 1,)),
                        pltpu.SemaphoreType.DMA],
    )(shard)


N_CHIPS = N_DEV // 2


def _peers(chips_only=False):
    x, y, c = lax.axis_index("x"), lax.axis_index("y"), lax.axis_index("c")
    out = []
    if chips_only:
        for k in range(1, N_CHIPS):
            px = 1 - x if k & 2 else x
            py = 1 - y if k & 1 else y
            out.append(((px, py, c), 2 * px + py))
        return 2 * x + y, out
    for k in range(1, N_DEV):
        px = 1 - x if k & 4 else x
        py = 1 - y if k & 2 else y
        pc = 1 - c if k & 1 else c
        out.append(((px, py, pc), 4 * px + 2 * py + pc))
    return 4 * x + 2 * y + c, out


def _sibling_swap(slot_arrays, *, name):
    n = len(slot_arrays)

    def body(*refs):
        ins, outs, send_sems, recv_sems = refs[:n], refs[n:2 * n], refs[2 * n], refs[2 * n + 1]
        x, y, c = lax.axis_index("x"), lax.axis_index("y"), lax.axis_index("c")
        copies = [pltpu.make_async_remote_copy(
            src_ref=ins[a].at[2 * q + (1 - c)], dst_ref=outs[a].at[q], send_sem=send_sems.at[a, q],
            recv_sem=recv_sems.at[a, q], device_id=(x, y, 1 - c), device_id_type=MESH_ID)
            for a in range(n) for q in range(N_CHIPS)]
        for cp in copies:
            cp.start()
        for cp in copies:
            cp.wait_recv()
        for cp in copies:
            cp.wait_send()

    return pl.pallas_call(
        body, name=name, in_specs=[ANY] * n, out_specs=[ANY] * n,
        out_shape=[_sds((N_CHIPS,) + t.shape[1:], t.dtype) for t in slot_arrays],
        scratch_shapes=[pltpu.SemaphoreType.DMA((n, N_CHIPS)), pltpu.SemaphoreType.DMA((n, N_CHIPS))],
    )(*slot_arrays)


def _pair_sum(slots, from_sibling, *, name, tn):
    _, r, c = slots.shape
    core = lax.axis_index("c").astype(jnp.int32).reshape(1)

    def body(core_ref, a_ref, b_ref, o_ref):
        o_ref[...] = (a_ref[...].astype(F32) + b_ref[...].astype(F32)).astype(o_ref.dtype)

    blk = lambda f: pl.BlockSpec((1, r, tn), f)
    return pl.pallas_call(
        body, name=name,
        grid_spec=pltpu.PrefetchScalarGridSpec(
            num_scalar_prefetch=1, grid=(N_CHIPS, c // tn),
            in_specs=[blk(lambda q, j, core: (2 * q + core[0], 0, j)), blk(lambda q, j, core: (q, 0, j))],
            out_specs=blk(lambda q, j, core: (q, 0, j))),
        out_shape=_sds((N_CHIPS, r, c), slots.dtype),
        compiler_params=_params("parallel", "parallel"),
    )(core, slots, from_sibling)


def _sum_parts(parts, *, name, tn):
    n, r, c = parts.shape

    def body(p_ref, o_ref):
        total = p_ref[0].astype(F32)
        for s in range(1, n):
            total = total + p_ref[s].astype(F32)
        o_ref[...] = total

    return pl.pallas_call(
        body, name=name, grid=(c // tn,),
        in_specs=[pl.BlockSpec((n, r, tn), lambda j: (0, 0, j))],
        out_specs=pl.BlockSpec((r, tn), lambda j: (0, j)), out_shape=_sds((r, c), F32),
        compiler_params=_params("parallel"),
    )(parts)


HBM = pl.BlockSpec(memory_space=pltpu.HBM)
SEM = pl.BlockSpec(memory_space=pltpu.SEMAPHORE)
DATAFLOW = pltpu.SideEffectType.DATAFLOW_SIDE_EFFECTING


def _split_copy(srcs, lands, send_sems, recv_sems, scatter, a, k, me, peers, incoming=False):
    dev, slot = peers[k]
    if incoming:
        src = dst = lands[a].at[slot]
    else:
        src, dst = (srcs[a].at[slot] if scatter else srcs[a]), lands[a].at[me]
    sem = a * len(peers) + k
    return pltpu.make_async_remote_copy(
        src_ref=src, dst_ref=dst, send_sem=send_sems.at[sem], recv_sem=recv_sems.at[sem],
        device_id=dev, device_id_type=MESH_ID)


def _exchange_start(arrays, scatter, *, name, chips_only=False):
    n = len(arrays)
    n_slots = N_CHIPS if chips_only else N_DEV

    def body(*refs):
        srcs, lands = refs[:n], refs[n:2 * n]
        send_sems, recv_sems = refs[2 * n], refs[2 * n + 1]
        token = refs[-1]
        me, peers = _peers(chips_only)
        for k in range(len(peers)):
            for a in range(n):
                _split_copy(srcs, lands, send_sems, recv_sems, scatter, a, k, me, peers).start()
        token[...] = jnp.zeros_like(token)

    land_shapes = [((n_slots,) + a.shape[-2:], a.dtype) for a in arrays]
    sems = pltpu.SemaphoreType.DMA((n * (n_slots - 1),))
    outs = pl.pallas_call(
        body, name=name,
        out_shape=(sems, sems, *[pltpu.HBM(a.shape, a.dtype) for a in arrays],
                   *[pltpu.HBM(s, d) for s, d in land_shapes], _sds((SUBLANE, LANE), F32)),
        in_specs=[HBM] * (2 * n),
        out_specs=(SEM, SEM, *[HBM] * (2 * n), pl.BlockSpec(memory_space=pltpu.VMEM)),
        input_output_aliases={i: 2 + i for i in range(2 * n)},
        compiler_params=pltpu.CompilerParams(has_side_effects=DATAFLOW),
    )(*[pltpu.with_memory_space_constraint(a, pltpu.HBM) for a in arrays],
      *[pltpu.with_memory_space_constraint(lax.empty(s, d), pltpu.HBM) for s, d in land_shapes])
    return (outs[0], outs[1], outs[2:2 + n], outs[2 + n:2 + 2 * n], scatter, chips_only), outs[-1]


def _exchange_wait(handles, after, *, name):
    send_sems, recv_sems, srcs, lands, scatter, chips_only = handles
    n = len(srcs)

    def body(*refs):
        src_refs, land_refs = refs[:n], refs[n:2 * n]
        send_ref, recv_ref = refs[2 * n], refs[2 * n + 1]
        me, peers = _peers(chips_only)
        for k in range(len(peers)):
            for a in range(n):
                _split_copy(src_refs, land_refs, send_ref, recv_ref, scatter, a, k, me, peers).wait_send()
                _split_copy(src_refs, land_refs, send_ref, recv_ref, scatter, a, k, me, peers, True).wait_recv()

    outs = pl.pallas_call(
        body, name=name,
        out_shape=tuple(pltpu.HBM(t.shape, t.dtype) for t in (*srcs, *lands)),
        in_specs=[HBM] * (2 * n) + [SEM, SEM, pl.BlockSpec(memory_space=pl.ANY)],
        out_specs=tuple([HBM] * (2 * n)),
        input_output_aliases={i: i for i in range(2 * n)},
        compiler_params=pltpu.CompilerParams(has_side_effects=DATAFLOW),
    )(*srcs, *lands, send_sems, recv_sems, after)
    return _with_own_slot(outs[n:], outs[:n], scatter, chips_only)


def _with_own_slot(landed, own, scatter, chips_only):
    me = 2 * lax.axis_index("x") + lax.axis_index("y")
    if not chips_only:
        me = 2 * me + lax.axis_index("c")
    out = []
    for buf, src in zip(landed, own):
        mine = lax.dynamic_index_in_dim(src, me, 0, keepdims=False) if scatter else src
        out.append(lax.dynamic_update_index_in_dim(buf, mine, me, 0))
    return out


def _adamw(parts, w, m, v, *, name, tm):
    r, c = w.shape
    assert r % tm == 0

    def body(p_ref, w_ref, m_ref, v_ref, g_ref, d_ref, nm_ref, nv_ref):
        _adamw_update(p_ref, w_ref, m_ref, v_ref, g_ref, d_ref, nm_ref, nv_ref)

    blk = pl.BlockSpec((tm, c), lambda i: (i, 0))
    return pl.pallas_call(
        body, name=name, grid=(r // tm,),
        in_specs=[pl.BlockSpec((parts.shape[0], tm, c), lambda i: (0, i, 0)), blk, blk, blk],
        out_specs=[blk] * 4, out_shape=[_sds((r, c), F32)] * 4,
        compiler_params=_params("parallel"),
    )(parts, w, m, v)


def _adamw_update(p_ref, w_ref, m_ref, v_ref, g_ref, d_ref, nm_ref, nv_ref):
    g = p_ref[0].astype(F32)
    for s in range(1, p_ref.shape[0]):
        g = g + p_ref[s].astype(F32)
    g_ref[...] = g
    m_new = ADAM_B1 * m_ref[...] + (1.0 - ADAM_B1) * g
    v_new = ADAM_B2 * v_ref[...] + (1.0 - ADAM_B2) * (g * g)
    nm_ref[...] = m_new
    nv_ref[...] = v_new
    m_hat = m_new / (1.0 - ADAM_B1 ** ADAM_STEP)
    v_hat = v_new / (1.0 - ADAM_B2 ** ADAM_STEP)
    d_ref[...] = -ADAM_LR * (m_hat / (jnp.sqrt(v_hat) + ADAM_EPS) + ADAM_WD * w_ref[...])


SMALL = ("g_pre_mix", "b_forget", "g_post_mix", "g_pre_ffn", "conv_b", "g_post_ffn")


def _adamw_small(parts, ws, ms, vs, sq_err_parts):
    n = len(ws)

    def body(*refs):
        ins, sq_ref, outs, loss_ref = refs[:4 * n], refs[4 * n], refs[4 * n + 1:-1], refs[-1]
        for i in range(n):
            _adamw_update(ins[i], ins[n + i], ins[2 * n + i], ins[3 * n + i], *outs[4 * i:4 * i + 4])
        total = sq_ref[0]
        for s in range(1, N_DEV):
            total = total + sq_ref[s]
        loss_ref[...] = total * (0.5 / D_MODEL)

    res = pl.pallas_call(
        body, name="adamw_small",
        out_shape=[_sds(w.shape, F32) for w in ws for _ in range(4)] + [_sds((1, LANE), F32)],
        compiler_params=pltpu.CompilerParams(vmem_limit_bytes=VMEM_LIMIT),
    )(*parts, *ws, *ms, *vs, sq_err_parts)
    return [res[4 * i:4 * i + 4] for i in range(n)], res[-1][0, 0]


def kernel(x, g_pre_mix, w_in, b_forget, w_o_fox, w_o_dil, w_out, g_post_mix, g_pre_ffn, w_up, conv_w, conv_b, w_down, g_post_ffn, loss_target, m_g_pre_mix, m_w_in, m_b_forget, m_w_o_fox, m_w_o_dil, m_w_out, m_g_post_mix, m_g_pre_ffn, m_w_up, m_conv_w, m_conv_b, m_w_down, m_g_post_ffn, v_g_pre_mix, v_w_in, v_b_forget, v_w_o_fox, v_w_o_dil, v_w_out, v_g_post_mix, v_g_pre_ffn, v_w_up, v_conv_w, v_conv_b, v_w_down, v_g_post_ffn):
    names = ("g_pre_mix", "w_in", "b_forget", "w_o_fox", "w_o_dil", "w_out", "g_post_mix", "g_pre_ffn",
             "w_up", "conv_w", "conv_b", "w_down", "g_post_ffn")
    w = dict(g_pre_mix=g_pre_mix, w_in=w_in, b_forget=b_forget, w_o_fox=w_o_fox, w_o_dil=w_o_dil, w_out=w_out,
             g_post_mix=g_post_mix, g_pre_ffn=g_pre_ffn, w_up=w_up, conv_w=conv_w, conv_b=conv_b, w_down=w_down,
             g_post_ffn=g_post_ffn)
    m = dict(g_pre_mix=m_g_pre_mix, w_in=m_w_in, b_forget=m_b_forget, w_o_fox=m_w_o_fox, w_o_dil=m_w_o_dil,
             w_out=m_w_out, g_post_mix=m_g_post_mix, g_pre_ffn=m_g_pre_ffn, w_up=m_w_up, conv_w=m_conv_w,
             conv_b=m_conv_b, w_down=m_w_down, g_post_ffn=m_g_post_ffn)
    v = dict(g_pre_mix=v_g_pre_mix, w_in=v_w_in, b_forget=v_b_forget, w_o_fox=v_w_o_fox, w_o_dil=v_w_o_dil,
             w_out=v_w_out, g_post_mix=v_g_post_mix, g_pre_ffn=v_g_pre_ffn, w_up=v_w_up, conv_w=v_conv_w,
             conv_b=v_conv_b, w_down=v_w_down, g_post_ffn=v_g_post_ffn)
    sharded = ("w_in", "w_o_fox", "w_o_dil", "w_out", "w_up", "w_down", "conv_w")
    wire = lambda n: F32 if n == "conv_w" else BF16

    by_cols = lambda t: jnp.transpose(t, (1, 0, 2)).reshape(t.shape[1], N_DEV * t.shape[2])
    by_rows = lambda t: t.reshape(N_DEV * t.shape[1], t.shape[2])
    col_slots = lambda t: jnp.transpose(t.reshape(t.shape[0], N_DEV, t.shape[1] // N_DEV), (1, 0, 2))
    row_slots = lambda t: t.reshape(N_DEV, t.shape[0] // N_DEV, t.shape[1])
    to_slots = lambda n, t: (row_slots if n in ("w_out", "w_down") else col_slots)(t).astype(wire(n))
    shard = lambda n: w[n][0].astype(wire(n))
    f_lo, f_hi = 3 * ATT_W, 3 * ATT_W + N_HEADS

    w_in_full = by_cols(_gather_two_level(shard("w_in"), name="gather_w_in"))
    w_main = jnp.concatenate([w_in_full[:, :f_lo], w_in_full[:, f_hi:]], axis=1)
    w_f = jnp.pad(w_in_full[:, f_lo:f_hi], ((0, 0), (0, F_PAD - N_HEADS)))
    late = ("w_o_fox", "w_o_dil", "w_out", "w_up", "conv_w", "w_down")
    order = jnp.minimum(jnp.abs(w_in_full[0, 0].astype(F32)), 0.0)
    late_handles, late_tok = _exchange_start(
        [shard(n) + order.astype(wire(n)) if n == "conv_w" else shard(n) for n in late], False,
        name="gather_late_start")

    def late_weights(after):
        got = dict(zip(late, _exchange_wait(late_handles, after, name="gather_late_wait")))
        return (by_cols(got["w_o_fox"]), by_cols(got["w_o_dil"]), by_rows(got["w_out"]),
                _ffn_move_blocks(by_cols(got["w_up"]), interleave=True, name="w_up_cols"),
                _ffn_interleave(by_cols(got["conv_w"])),
                by_rows(got["w_down"]))

    pending = {}

    def ffn_grads_ready(g):
        pending["ffn"] = _exchange_start([to_slots(n, g[n]) for n in ("w_down", "w_up", "conv_w")], True,
                                         name="scatter_ffn_start")
        return pending["ffn"][1][0, 0]

    def proj_grads_ready(g):
        pending["proj"] = _exchange_start([to_slots(n, g[n]) for n in ("w_o_fox", "w_o_dil", "w_out")], True,
                                          name="scatter_proj_start")
        return pending["proj"][1][0, 0]

    def mixer_grads_ready(g):
        slabs = [g["w_main"].reshape(N_DEV, D_MODEL // N_DEV, Z_MAIN), g["w_f"].reshape(N_DEV, D_MODEL // N_DEV, F_PAD)]
        theirs = _sibling_swap(slabs, name="scatter_w_in_swap")
        chip_sums = [_pair_sum(slabs[0], theirs[0], name="scatter_w_in_pair_sum", tn=Z_MAIN // 4),
                     _pair_sum(slabs[1], theirs[1], name="scatter_w_in_forget_pair_sum", tn=F_PAD)]
        pending["w_in"] = _exchange_start(chip_sums, True, name="scatter_w_in_start", chips_only=True)
        return pending["w_in"][1][0, 0]

    sq_err, grad_x, g = _local_step(
        x[0], loss_target[0], w_main, w_f, b_forget, conv_b, g_pre_mix + late_tok[0, 0], g_post_mix, g_pre_ffn,
        g_post_ffn, late_weights, ffn_grads_ready, proj_grads_ready, mixer_grads_ready)

    tiles = dict(w_in=256, w_o_fox=512, w_o_dil=512, w_out=128, w_up=256, w_down=176, conv_w=3)
    adam = lambda n, p: _adamw(p, w[n][0], m[n][0], v[n][0], name=f"adamw_{n}", tm=tiles[n])
    res = {}
    for key, group in (("ffn", ("w_down", "w_up", "conv_w")), ("proj", ("w_o_fox", "w_o_dil", "w_out"))):
        landed = _exchange_wait(pending[key][0], grad_x, name=f"scatter_{key}_wait")
        res.update({n: adam(n, p) for n, p in zip(group, landed)})
    done = res["w_up"][3]
    main_parts, f_parts = _exchange_wait(pending["w_in"][0], done, name="scatter_w_in_wait")
    slab_main = _sum_parts(main_parts, name="scatter_w_in_sum", tn=Z_MAIN // 4)
    slab_f = _sum_parts(f_parts, name="scatter_w_in_forget_sum", tn=F_PAD)
    slab = jnp.concatenate([slab_main[:, :f_lo], slab_f[:, :N_HEADS], slab_main[:, f_lo:]], axis=1)
    last = _exchange([col_slots(slab).astype(BF16)] + [g[n] for n in SMALL] + [sq_err],
                     [True] + [False] * (len(SMALL) + 1), name="scatter_w_in_rows_gather_small")
    rows, small_parts = last[0], last[1:]
    res["w_in"] = adam("w_in", rows.reshape(1, D_MODEL, rows.shape[-1]))
    small, loss = _adamw_small(small_parts[:-1], *[[t[n] for n in SMALL] for t in (w, m, v)], small_parts[-1])
    small = dict(zip(SMALL, small))
    out = [[(res[n][k][None] if n in sharded else small[n][k]) for n in names] for k in range(4)]
    return (loss, grad_x[None], *out[0], *out[1], *out[2], *out[3])
```

```python
import functools
import math

import jax
import jax.numpy as jnp
import numpy as np
from jax import lax
from jax.experimental import pallas as pl
from jax.experimental.pallas import tpu as pltpu

F32 = jnp.float32
BF16 = jnp.bfloat16

SEQ = 4096
D_MODEL = 1024
N_HEADS = 8
HEAD_DIM = 64
ATT_W = N_HEADS * HEAD_DIM
D_FF = 2816
Z_MAIN = 5120
F_PAD = 128
ROPE_DIM = 16
ROPE_THETA = 500000.0
RMS_EPS = 1e-6
NEG_INF = -1e30
SCALE = 1.0 / math.sqrt(HEAD_DIM)
DIL_PATTERNS = ((128, 1), (512, 4), (2048, 16))
DIL_BLK = 128
N_DEV = 8

ADAM_LR = 0.001
ADAM_B1 = 0.9
ADAM_B2 = 0.999
ADAM_EPS = 1e-08
ADAM_WD = 0.01
ADAM_STEP = 10

LANE = 128
SUBLANE = 8
VMEM_LIMIT = 56 * 1024 * 1024
MESH_ID = pl.DeviceIdType.MESH
ANY = pl.BlockSpec(memory_space=pl.ANY)


def _params(*sem):
    return pltpu.CompilerParams(dimension_semantics=sem, vmem_limit_bytes=VMEM_LIMIT)


def _sds(shape, dtype):
    return jax.ShapeDtypeStruct(shape, dtype)


def _matmul(a, b, *, ta=False, tb=False, out_dtype, tm, tn, tk, name, b_k_off=0):
    if ta:
        kk, m = a.shape
    else:
        m, kk = a.shape
    n = b.shape[0] if tb else b.shape[1]
    tm, tn, tk = min(tm, m), min(tn, n), min(tk, kk)
    assert (b.shape[1] if tb else b.shape[0]) >= b_k_off * tk + kk
    assert m % tm == 0 and n % tn == 0 and kk % tk == 0, (name, m, n, kk, tm, tn, tk)
    nk = kk // tk
    dims = (((0 if ta else 1,), (1 if tb else 0,)), ((), ()))

    def body(a_ref, b_ref, o_ref, *scratch):
        p = lax.dot_general(a_ref[...].astype(BF16), b_ref[...].astype(BF16), dims,
                            preferred_element_type=F32)
        if nk == 1:
            o_ref[...] = p.astype(o_ref.dtype)
        else:
            acc = scratch[0]
            k = pl.program_id(2)

            @pl.when(k == 0)
            def _():
                acc[...] = p

            @pl.when(k > 0)
            def _():
                acc[...] += p

            @pl.when(k == nk - 1)
            def _():
                o_ref[...] = acc[...].astype(o_ref.dtype)

    a_spec = (pl.BlockSpec((tk, tm), lambda i, j, k: (k, i)) if ta
              else pl.BlockSpec((tm, tk), lambda i, j, k: (i, k)))
    b_spec = (pl.BlockSpec((tn, tk), lambda i, j, k: (j, k + b_k_off)) if tb
              else pl.BlockSpec((tk, tn), lambda i, j, k: (k + b_k_off, j)))
    return pl.pallas_call(
        body, name=name, grid=(m // tm, n // tn, nk),
        in_specs=[a_spec, b_spec],
        out_specs=pl.BlockSpec((tm, tn), lambda i, j, k: (i, j)),
        out_shape=_sds((m, n), out_dtype),
        scratch_shapes=[pltpu.VMEM((tm, tn), F32)] if nk > 1 else [],
        compiler_params=_params("parallel", "parallel", "arbitrary"),
    )(a, b)


def _rms_fwd(x, g, *, name, tm=512):
    def body(x_ref, g_ref, h_ref):
        xv = x_ref[...]
        r = lax.rsqrt(jnp.mean(xv * xv, axis=-1, keepdims=True) + RMS_EPS)
        h_ref[...] = (xv * r * g_ref[...]).astype(h_ref.dtype)

    return pl.pallas_call(
        body, name=name, grid=(SEQ // tm,),
        in_specs=[pl.BlockSpec((tm, D_MODEL), lambda i: (i, 0)), pl.BlockSpec((1, D_MODEL), lambda i: (0, 0))],
        out_specs=pl.BlockSpec((tm, D_MODEL), lambda i: (i, 0)),
        out_shape=_sds((SEQ, D_MODEL), BF16),
        compiler_params=_params("parallel"),
    )(x, g)


def _rms_bwd(dh_parts, xin, g, dres, *, out_dtype, name, tm=512):
    n_parts = len(dh_parts)
    has_res = dres is not None

    def body(*refs):
        parts = refs[:n_parts]
        x_ref, g_ref = refs[n_parts], refs[n_parts + 1]
        res_ref = refs[n_parts + 2] if has_res else None
        o_ref, gg_ref = refs[-2], refs[-1]
        dh = parts[0][...].astype(F32)
        for p in parts[1:]:
            dh = dh + p[...].astype(F32)
        xv = x_ref[...]
        r = lax.rsqrt(jnp.mean(xv * xv, axis=-1, keepdims=True) + RMS_EPS)
        xn = xv * r

        @pl.when(pl.program_id(0) == 0)
        def _():
            gg_ref[...] = jnp.zeros_like(gg_ref)

        gg_ref[...] += jnp.sum(dh * xn, axis=0, keepdims=True)
        dxn = dh * g_ref[...]
        dx = r * (dxn - xn * jnp.mean(dxn * xn, axis=-1, keepdims=True))
        if has_res:
            dx = dx + res_ref[...]
        o_ref[...] = dx.astype(o_ref.dtype)

    row = pl.BlockSpec((tm, D_MODEL), lambda i: (i, 0))
    vec = pl.BlockSpec((1, D_MODEL), lambda i: (0, 0))
    args = list(dh_parts) + [xin, g] + ([dres] if has_res else [])
    return pl.pallas_call(
        body, name=name, grid=(SEQ // tm,),
        in_specs=[row] * n_parts + [row, vec] + ([row] if has_res else []),
        out_specs=[row, vec],
        out_shape=[_sds((SEQ, D_MODEL), out_dtype), _sds((1, D_MODEL), F32)],
        compiler_params=_params("arbitrary"),
    )(*args)


def _rms_pair_bwd(dh_parts, x2, g_pre, dres, y1, g_post, *, tm=512):
    n_parts = len(dh_parts)

    def norm_bwd(dh, xin, g_ref, gg_ref):
        r = lax.rsqrt(jnp.mean(xin * xin, axis=-1, keepdims=True) + RMS_EPS)
        xn = xin * r
        gg_ref[...] += jnp.sum(dh * xn, axis=0, keepdims=True)
        dxn = dh * g_ref[...]
        return r * (dxn - xn * jnp.mean(dxn * xn, axis=-1, keepdims=True))

    def body(*refs):
        parts = refs[:n_parts]
        x2_ref, gpre_ref, res_ref, y1_ref, gpost_ref, dx2_ref, dy1_ref, ggpre_ref, ggpost_ref = refs[n_parts:]

        @pl.when(pl.program_id(0) == 0)
        def _():
            ggpre_ref[...] = jnp.zeros_like(ggpre_ref)
            ggpost_ref[...] = jnp.zeros_like(ggpost_ref)

        dh = parts[0][...].astype(F32)
        for p in parts[1:]:
            dh = dh + p[...].astype(F32)
        dx2 = res_ref[...] + norm_bwd(dh, x2_ref[...], gpre_ref, ggpre_ref)
        dx2_ref[...] = dx2
        dy1_ref[...] = norm_bwd(dx2, y1_ref[...], gpost_ref, ggpost_ref).astype(dy1_ref.dtype)

    row = pl.BlockSpec((tm, D_MODEL), lambda i: (i, 0))
    vec = pl.BlockSpec((1, D_MODEL), lambda i: (0, 0))
    return pl.pallas_call(
        body, name="rms_pair_bwd", grid=(SEQ // tm,),
        in_specs=[row] * n_parts + [row, vec, row, row, vec],
        out_specs=[row, row, vec, vec],
        out_shape=[_sds((SEQ, D_MODEL), F32), _sds((SEQ, D_MODEL), BF16), _sds((1, D_MODEL), F32),
                   _sds((1, D_MODEL), F32)],
        compiler_params=_params("arbitrary"),
    )(*dh_parts, x2, g_pre, dres, y1, g_post)


SCAN_BLK = 512


def _split_dot(v, tri):
    hi = v.astype(BF16)
    r1 = v - hi.astype(F32)
    mid = r1.astype(BF16)
    lo = (r1 - mid.astype(F32)).astype(BF16)
    dot = functools.partial(jnp.dot, preferred_element_type=F32)
    return dot(hi, tri) + dot(mid, tri) + dot(lo, tri)


def _fox_prep(fa_t, b_col):
    nblk = SEQ // SCAN_BLK

    def body(fa_ref, b_ref, f_ref, sg_ref):
        row = lax.broadcasted_iota(jnp.int32, (SCAN_BLK, SCAN_BLK), 0)
        col = lax.broadcasted_iota(jnp.int32, (SCAN_BLK, SCAN_BLK), 1)
        upper = (row <= col).astype(BF16)
        carry = jnp.zeros((N_HEADS, 1), F32)
        for blk in range(nblk):
            sl = pl.ds(blk * SCAN_BLK, SCAN_BLK)
            xx = fa_ref[:, sl] + b_ref[...]
            e = jnp.exp(-jnp.abs(xx))
            logf = jnp.minimum(xx, 0.0) - jnp.log(1.0 + e)
            sg_ref[:, sl] = jnp.where(xx >= 0.0, e, 1.0) / (1.0 + e)
            c = _split_dot(logf, upper) + carry
            f_ref[:, sl] = c
            carry = c[:, SCAN_BLK - 1:SCAN_BLK]

    return pl.pallas_call(
        body, name="fox_prep",
        out_shape=[_sds((N_HEADS, SEQ), F32), _sds((N_HEADS, SEQ), F32)],
        compiler_params=pltpu.CompilerParams(vmem_limit_bytes=VMEM_LIMIT),
    )(fa_t, b_col)


def _fox_post_bwd(df_t, sg_t):
    nblk = SEQ // SCAN_BLK

    def body(df_ref, sg_ref, dfa_ref, gb_ref):
        row = lax.broadcasted_iota(jnp.int32, (SCAN_BLK, SCAN_BLK), 0)
        col = lax.broadcasted_iota(jnp.int32, (SCAN_BLK, SCAN_BLK), 1)
        lower = (row >= col).astype(BF16)
        carry = jnp.zeros((N_HEADS, 1), F32)
        gb = jnp.zeros((N_HEADS, 1), F32)
        for blk in reversed(range(nblk)):
            sl = pl.ds(blk * SCAN_BLK, SCAN_BLK)
            c = _split_dot(df_ref[:, sl], lower) + carry
            carry = c[:, 0:1]
            dfa = c * sg_ref[:, sl]
            dfa_ref[:, sl] = dfa
            gb = gb + jnp.sum(dfa, axis=1, keepdims=True)
        gb_ref[...] = gb

    return pl.pallas_call(
        body, name="fox_post_bwd",
        out_shape=[_sds((N_HEADS, SEQ), F32), _sds((N_HEADS, 1), F32)],
        compiler_params=pltpu.CompilerParams(vmem_limit_bytes=VMEM_LIMIT),
    )(df_t, sg_t)


FOX_T = 512
NT_DIMS = (((1,), (1,)), ((), ()))
TN_DIMS = (((0,), (0,)), ((), ()))


def _head(ref_or_val, h):
    return ref_or_val[:, h * HEAD_DIM:(h + 1) * HEAD_DIM]


def _split3(v):
    hi = v.astype(BF16).astype(F32)
    r1 = v - hi
    mid = r1.astype(BF16).astype(F32)
    return hi, mid, (r1 - mid).astype(BF16).astype(F32)


ONE_LANE = 3 * N_HEADS


def _pack_terms(v, with_one):
    hi, mid, lo = _split3(v)
    t = hi + pltpu.roll(mid, N_HEADS, 1) + pltpu.roll(lo, 2 * N_HEADS, 1)
    if with_one:
        t = t + (lax.broadcasted_iota(jnp.int32, v.shape, 1) == ONE_LANE).astype(F32)
    return t.astype(BF16)


def _aux_matrices():
    to_q = np.zeros((LANE, N_HEADS * 2 * HEAD_DIM), np.float32)
    to_k = np.zeros_like(to_q)
    for h in range(N_HEADS):
        base = h * 2 * HEAD_DIM + HEAD_DIM
        for s in range(3):
            to_q[s * N_HEADS + h, base + s] = 1.0
            to_q[ONE_LANE, base + 3 + s] = 1.0
            to_k[ONE_LANE, base + s] = 1.0
            to_k[s * N_HEADS + h, base + 3 + s] = -1.0
    return jnp.asarray(to_q, BF16), jnp.asarray(to_k, BF16)


def _head_sums():
    total = np.zeros((N_HEADS * HEAD_DIM, LANE), np.float32)
    first = np.zeros_like(total)
    for h in range(N_HEADS):
        total[h * HEAD_DIM:(h + 1) * HEAD_DIM, h] = 1.0
        first[h * HEAD_DIM, h] = 1.0
    return jnp.asarray(total, BF16), jnp.asarray(first, BF16)


SLOT = 2 * HEAD_DIM
N_SPLIT = 3
FOX_FWD_HEADS = 8
FOX_BWD_HEADS = 4


def _slot(ref, h):
    return ref[:, h * SLOT:(h + 1) * SLOT]


def _fox_pack_fwd(zm, f_cols, *, tm=512):
    def body(q_ref, k_ref, v_ref, f_ref, tq_ref, tk_ref, qs_ref, ks_ref, vs_ref):
        ones = jnp.ones((tm, HEAD_DIM), BF16)
        terms = _pack_terms(f_ref[...], True)
        q_aux = jnp.dot(terms, tq_ref[...], preferred_element_type=F32).astype(BF16)
        k_aux = jnp.dot(terms, tk_ref[...], preferred_element_type=F32).astype(BF16)
        for h in range(N_HEADS):
            aux = slice(h * SLOT + HEAD_DIM, (h + 1) * SLOT)
            qs_ref[:, h * SLOT:(h + 1) * SLOT] = jnp.concatenate(
                [(_head(q_ref, h).astype(F32) * SCALE).astype(BF16), q_aux[:, aux]], axis=1)
            ks_ref[:, h * SLOT:(h + 1) * SLOT] = jnp.concatenate([_head(k_ref, h), k_aux[:, aux]], axis=1)
            vs_ref[:, h * SLOT:(h + 1) * SLOT] = jnp.concatenate([_head(v_ref, h), ones], axis=1)

    col = lambda b: pl.BlockSpec((tm, ATT_W), lambda i: (i, b))
    wide = pl.BlockSpec((tm, N_HEADS * SLOT), lambda i: (i, 0))
    const = pl.BlockSpec((LANE, N_HEADS * SLOT), lambda i: (0, 0))
    return pl.pallas_call(
        body, name="fox_pack_fwd", grid=(SEQ // tm,),
        in_specs=[col(0), col(1), col(2), pl.BlockSpec((tm, LANE), lambda i: (i, 0)), const, const],
        out_specs=[wide] * 3, out_shape=[_sds((SEQ, N_HEADS * SLOT), BF16)] * 3,
        compiler_params=_params("parallel"),
    )(zm, zm, zm, f_cols, *_aux_matrices())


def _fox_pack_bwd(zm, f_cols, lse, o, do, *, tm=512):
    def body(q_ref, f_ref, lse_ref, o_ref, do_ref, tq_ref, total_ref, first_ref, qs_ref, ds_ref):
        delta = _split_dot(o_ref[...].astype(F32) * do_ref[...].astype(F32), total_ref[...])
        lse_h = _split_dot(lse_ref[...], first_ref[...])
        q_aux = jnp.dot(_pack_terms(f_ref[...] - lse_h, True), tq_ref[...], preferred_element_type=F32).astype(BF16)
        d_aux = jnp.dot(_pack_terms(-delta, False), tq_ref[...], preferred_element_type=F32).astype(BF16)
        for h in range(N_HEADS):
            aux = slice(h * SLOT + HEAD_DIM, (h + 1) * SLOT)
            qs_ref[:, h * SLOT:(h + 1) * SLOT] = jnp.concatenate(
                [(_head(q_ref, h).astype(F32) * SCALE).astype(BF16), q_aux[:, aux]], axis=1)
            ds_ref[:, h * SLOT:(h + 1) * SLOT] = jnp.concatenate([_head(do_ref, h), d_aux[:, aux]], axis=1)

    row = pl.BlockSpec((tm, ATT_W), lambda i: (i, 0))
    wide = pl.BlockSpec((tm, N_HEADS * SLOT), lambda i: (i, 0))
    const = lambda r, c: pl.BlockSpec((r, c), lambda i: (0, 0))
    return pl.pallas_call(
        body, name="fox_pack_bwd", grid=(SEQ // tm,),
        in_specs=[row, pl.BlockSpec((tm, LANE), lambda i: (i, 0)), row, row, row,
                  const(LANE, N_HEADS * SLOT), const(ATT_W, LANE), const(ATT_W, LANE)],
        out_specs=[wide] * 2, out_shape=[_sds((SEQ, N_HEADS * SLOT), BF16)] * 2,
        compiler_params=_params("parallel"),
    )(zm, f_cols, lse, o, do, _aux_matrices()[0], *_head_sums())


def _causal_pairs(key_major):
    nb = SEQ // FOX_T
    if key_major:
        pairs = [(i, j) for j in range(nb) for i in range(j, nb)]
    else:
        pairs = [(i, j) for i in range(nb) for j in range(i + 1)]
    return (jnp.array([p[0] for p in pairs], jnp.int32), jnp.array([p[1] for p in pairs], jnp.int32), len(pairs))


FOX_HALF = FOX_T // 2
FOX_FULL = ((slice(0, FOX_T), slice(0, FOX_T), None),)
FOX_DIAG = ((slice(0, FOX_HALF), slice(0, FOX_HALF), 0), (slice(FOX_HALF, FOX_T), slice(0, FOX_T), FOX_HALF))


def _causal_piece_mask(q_rows, k_rows, offset):
    shape = (q_rows.stop - q_rows.start, k_rows.stop - k_rows.start)
    row = lax.broadcasted_iota(jnp.int32, shape, 0)
    col = lax.broadcasted_iota(jnp.int32, shape, 1)
    return col <= row + offset


def _fox_fwd(q_slots, k_slots, v_slots):
    i_tab, j_tab, n_pairs = _causal_pairs(False)

    def body(i_tab, j_tab, q_ref, k_ref, v_ref, o_ref, lse_ref, m_s, acc_s):
        t = pl.program_id(1)
        i, j = i_tab[t], j_tab[t]

        @pl.when(j == 0)
        def _():
            m_s[...] = jnp.full_like(m_s, NEG_INF)
            acc_s[...] = jnp.zeros_like(acc_s)

        def step(pieces):
            jobs = [(h, piece) for h in range(FOX_FWD_HEADS) for piece in pieces]
            lanes = lambda h: slice(h * SLOT, (h + 1) * SLOT)
            scores = [lax.dot_general(q_ref[qr, lanes(h)], k_ref[kr, lanes(h)], NT_DIMS, preferred_element_type=F32)
                      for h, (qr, kr, _) in jobs]
            probs, alphas = [], []
            for idx, (h, (qr, kr, offset)) in enumerate(jobs):
                s = scores[idx]
                if offset is not None:
                    s = jnp.where(_causal_piece_mask(qr, kr, offset), s, NEG_INF)
                m_prev = m_s[h, qr, :]
                m_new = jnp.maximum(m_prev, jnp.max(s, axis=-1, keepdims=True))
                probs.append(jnp.exp(s - jnp.tile(m_new, (1, s.shape[1] // LANE))).astype(BF16))
                alphas.append(jnp.exp(m_prev - m_new))
                m_s[h, qr, :] = m_new
            for idx, (h, (qr, kr, _)) in enumerate(jobs):
                acc_s[h, qr, :] = alphas[idx] * acc_s[h, qr, :] + jnp.dot(
                    probs[idx], v_ref[kr, lanes(h)], preferred_element_type=F32)

        @pl.when(j < i)
        def _():
            step(FOX_FULL)

        @pl.when(j == i)
        def _():
            step(FOX_DIAG)
            outs, lses = [], []
            for h in range(FOX_FWD_HEADS):
                acc = acc_s[h]
                l = acc[:, HEAD_DIM:]
                outs.append(acc[:, :HEAD_DIM] / l)
                lses.append(m_s[h][:, :HEAD_DIM] + jnp.log(l))
            o_ref[...] = jnp.concatenate(outs, axis=1).astype(o_ref.dtype)
            lse_ref[...] = jnp.concatenate(lses, axis=1)

    qspec = pl.BlockSpec((FOX_T, FOX_FWD_HEADS * SLOT), lambda p, t, it, jt: (it[t], p))
    kspec = pl.BlockSpec((FOX_T, FOX_FWD_HEADS * SLOT), lambda p, t, it, jt: (jt[t], p))
    ospec = pl.BlockSpec((FOX_T, FOX_FWD_HEADS * HEAD_DIM), lambda p, t, it, jt: (it[t], p))
    return pl.pallas_call(
        body, name="fox_fwd",
        grid_spec=pltpu.PrefetchScalarGridSpec(
            num_scalar_prefetch=2, grid=(N_HEADS // FOX_FWD_HEADS, n_pairs),
            in_specs=[qspec, kspec, kspec], out_specs=[ospec, ospec],
            scratch_shapes=[pltpu.VMEM((FOX_FWD_HEADS, FOX_T, LANE), F32),
                            pltpu.VMEM((FOX_FWD_HEADS, FOX_T, SLOT), F32)]),
        out_shape=[_sds((SEQ, ATT_W), BF16), _sds((SEQ, ATT_W), F32)],
        compiler_params=_params("parallel", "arbitrary"),
    )(i_tab, j_tab, q_slots, k_slots, v_slots)


def _fox_bwd(q_slots, k_slots, v_slots, do_slots):
    i_tab, j_tab, n_pairs = _causal_pairs(True)

    def body(i_tab, j_tab, q_ref, k_ref, v_ref, do_ref, dq_ref, dk_ref, dv_ref):
        t = pl.program_id(1)
        i, j = i_tab[t], j_tab[t]

        @pl.when(t == 0)
        def _():
            dq_ref[...] = jnp.zeros_like(dq_ref)

        @pl.when(i == j)
        def _():
            dk_ref[...] = jnp.zeros_like(dk_ref)
            dv_ref[...] = jnp.zeros_like(dv_ref)

        def step(pieces):
            jobs = [(h, piece) for h in range(FOX_BWD_HEADS) for piece in pieces]
            lanes = lambda h: slice(h * SLOT, (h + 1) * SLOT)
            scores = [lax.dot_general(q_ref[qr, lanes(h)], k_ref[kr, lanes(h)], NT_DIMS, preferred_element_type=F32)
                      for h, (qr, kr, _) in jobs]
            dps = [lax.dot_general(do_ref[qr, lanes(h)], v_ref[kr, lanes(h)], NT_DIMS, preferred_element_type=F32)
                   for h, (qr, kr, _) in jobs]
            ps, dss = [], []
            for idx, (h, (qr, kr, offset)) in enumerate(jobs):
                p = jnp.exp(scores[idx])
                if offset is not None:
                    p = jnp.where(_causal_piece_mask(qr, kr, offset), p, 0.0)
                ps.append(p.astype(BF16))
                dss.append((p * dps[idx]).astype(BF16))
            for idx, (h, (qr, kr, _)) in enumerate(jobs):
                rows = pl.ds(pl.multiple_of(i * FOX_T + qr.start, FOX_HALF), qr.stop - qr.start)
                dv_ref[kr, lanes(h)] += lax.dot_general(ps[idx], do_ref[qr, lanes(h)], TN_DIMS,
                                                        preferred_element_type=F32)
                dk_ref[kr, lanes(h)] += lax.dot_general(dss[idx], q_ref[qr, lanes(h)], TN_DIMS,
                                                        preferred_element_type=F32)
                dq_ref[rows, lanes(h)] += jnp.dot(dss[idx], k_ref[kr, lanes(h)], preferred_element_type=F32)

        @pl.when(i > j)
        def _():
            step(FOX_FULL)

        @pl.when(i == j)
        def _():
            step(FOX_DIAG)

    qspec = pl.BlockSpec((FOX_T, FOX_BWD_HEADS * SLOT), lambda p, t, it, jt: (it[t], p))
    kspec = pl.BlockSpec((FOX_T, FOX_BWD_HEADS * SLOT), lambda p, t, it, jt: (jt[t], p))
    return pl.pallas_call(
        body, name="fox_bwd",
        grid_spec=pltpu.PrefetchScalarGridSpec(
            num_scalar_prefetch=2, grid=(N_HEADS // FOX_BWD_HEADS, n_pairs),
            in_specs=[qspec, kspec, kspec, qspec],
            out_specs=[pl.BlockSpec((SEQ, FOX_BWD_HEADS * SLOT), lambda p, t, it, jt: (0, p)), kspec, kspec]),
        out_shape=[_sds((SEQ, N_HEADS * SLOT), F32)] * 3,
        compiler_params=_params("arbitrary", "arbitrary"),
    )(i_tab, j_tab, q_slots, k_slots, v_slots, do_slots)


def _fox_unpack(dq_slots, dk_slots, dv_slots, dz, *, tm=512):
    def body(dq_ref, dk_ref, dv_ref, dz_in, o_ref, df_ref):
        lane = lax.broadcasted_iota(jnp.int32, (tm, LANE), 1)
        df = jnp.zeros((tm, LANE), F32)
        for h in range(N_HEADS):
            lo = h * SLOT
            for part, (ref, mult) in enumerate(((dq_ref, SCALE), (dk_ref, 1.0), (dv_ref, 1.0))):
                o_ref[:, part * ATT_W + h * HEAD_DIM:part * ATT_W + (h + 1) * HEAD_DIM] = (
                    ref[:, lo:lo + HEAD_DIM] * mult).astype(o_ref.dtype)
            rows = dq_ref[:, lo + HEAD_DIM:lo + HEAD_DIM + 1]
            cols = dk_ref[:, lo + HEAD_DIM + N_SPLIT:lo + HEAD_DIM + N_SPLIT + 1]
            df = jnp.where(lane == h, rows - cols, df)
        df_ref[...] = df

    wide = pl.BlockSpec((tm, N_HEADS * SLOT), lambda i: (i, 0))
    return pl.pallas_call(
        body, name="fox_unpack", grid=(SEQ // tm,), in_specs=[wide] * 3 + [ANY],
        out_specs=[pl.BlockSpec((tm, 3 * ATT_W), lambda i: (i, 0)), pl.BlockSpec((tm, LANE), lambda i: (i, 0))],
        out_shape=[_sds((SEQ, Z_MAIN), BF16), _sds((SEQ, LANE), F32)],
        input_output_aliases={3: 0},
        compiler_params=_params("parallel"),
    )(dq_slots, dk_slots, dv_slots, dz)


def _attn_delta(o, do, *, name, tm=512):
    def body(o_ref, do_ref, d_ref):
        prod = o_ref[...].astype(F32) * do_ref[...].astype(F32)
        lane = lax.broadcasted_iota(jnp.int32, (tm, LANE), 1)
        out = jnp.zeros((tm, LANE), F32)
        for h in range(N_HEADS):
            out = jnp.where(lane == h, jnp.sum(_head(prod, h), axis=1, keepdims=True), out)
        d_ref[...] = out

    row = pl.BlockSpec((tm, ATT_W), lambda i: (i, 0))
    return pl.pallas_call(
        body, name=name, grid=(SEQ // tm,), in_specs=[row, row],
        out_specs=pl.BlockSpec((tm, LANE), lambda i: (i, 0)), out_shape=_sds((SEQ, LANE), F32),
        compiler_params=_params("parallel"),
    )(o, do)


def _rope_tables():
    half = ROPE_DIM // 2
    inv_freq = np.float32(ROPE_THETA) ** (-np.arange(half, dtype=np.float32) * np.float32(2.0) / np.float32(ROPE_DIM))
    ang = np.arange(SEQ, dtype=np.float32)[:, None] * inv_freq.astype(np.float32)[None, :]
    cos, sin = jnp.asarray(np.cos(ang).astype(np.float32)), jnp.asarray(np.sin(ang).astype(np.float32))
    ones = jnp.ones((SEQ, HEAD_DIM - ROPE_DIM), F32)
    zeros = jnp.zeros((SEQ, HEAD_DIM - ROPE_DIM), F32)
    zh = jnp.zeros((SEQ, half), F32)
    c_tab = jnp.concatenate([cos, cos, ones], axis=1)
    a_tab = jnp.concatenate([-sin, zh, zeros], axis=1)
    b_tab = jnp.concatenate([zh, sin, zeros], axis=1)
    two = lambda t: jnp.concatenate([t, t], axis=1)
    return two(c_tab), two(a_tab), two(b_tab)


def _rotate(x, c_tab, a_tab, b_tab):
    return x * c_tab + pltpu.roll(x, LANE - ROPE_DIM // 2, 1) * a_tab + pltpu.roll(x, ROPE_DIM // 2, 1) * b_tab


def _rope_fwd(zm, tabs, *, tm=512):
    def body(q_ref, k_ref, v_ref, c_ref, a_ref, b_ref, o_ref):
        for part, (x_ref, mult) in enumerate(((q_ref, SCALE), (k_ref, 1.0))):
            for cc in range(ATT_W // LANE):
                sl = slice(cc * LANE, (cc + 1) * LANE)
                rot = _rotate(x_ref[:, sl].astype(F32), c_ref[...], a_ref[...], b_ref[...])
                o_ref[:, part * ATT_W + cc * LANE:part * ATT_W + (cc + 1) * LANE] = (rot * mult).astype(o_ref.dtype)
        o_ref[:, 2 * ATT_W:] = v_ref[...]

    tab = pl.BlockSpec((tm, LANE), lambda i: (i, 0))
    col = lambda b: pl.BlockSpec((tm, ATT_W), lambda i: (i, b))
    return pl.pallas_call(
        body, name="rope_fwd", grid=(SEQ // tm,),
        in_specs=[col(3), col(4), col(5), tab, tab, tab],
        out_specs=pl.BlockSpec((tm, 3 * ATT_W), lambda i: (i, 0)),
        out_shape=_sds((SEQ, 3 * ATT_W), BF16),
        compiler_params=_params("parallel"),
    )(zm, zm, zm, *tabs)


def _dil_grad_combine(dqs, dks, dvs, tabs, dz, *, tm=256):
    def body(*refs):
        q_refs, k_refs, v_refs = refs[0:3], refs[3:6], refs[6:9]
        c_ref, a_ref, b_ref, _, o_ref = refs[9:]
        total = lambda rs, sl: rs[0][:, sl].astype(F32) + rs[1][:, sl].astype(F32) + rs[2][:, sl].astype(F32)
        for cc in range(ATT_W // LANE):
            sl = slice(cc * LANE, (cc + 1) * LANE)
            for part, rs in enumerate((q_refs, k_refs)):
                o_ref[:, part * ATT_W + cc * LANE:part * ATT_W + (cc + 1) * LANE] = _rotate(
                    total(rs, sl), c_ref[...], -a_ref[...], -b_ref[...]).astype(o_ref.dtype)
            o_ref[:, 2 * ATT_W + cc * LANE:2 * ATT_W + (cc + 1) * LANE] = total(v_refs, sl).astype(o_ref.dtype)

    row = pl.BlockSpec((tm, ATT_W), lambda i: (i, 0))
    tab = pl.BlockSpec((tm, LANE), lambda i: (i, 0))
    return pl.pallas_call(
        body, name="dil_grad_combine", grid=(SEQ // tm,),
        in_specs=[row] * 9 + [tab] * 3 + [ANY],
        out_specs=pl.BlockSpec((tm, 3 * ATT_W), lambda i: (i, 1)),
        out_shape=_sds((SEQ, Z_MAIN), BF16),
        input_output_aliases={12: 0},
        compiler_params=_params("parallel"),
    )(*dqs, *dks, *dvs, *tabs, dz)


def _dil_valid(n):
    qi = lax.broadcasted_iota(jnp.int32, (DIL_BLK, 2 * DIL_BLK), 0)
    ki = lax.broadcasted_iota(jnp.int32, (DIL_BLK, 2 * DIL_BLK), 1)
    dist = qi + DIL_BLK - ki
    return (dist >= 0) & (dist <= DIL_BLK) & ((n > 0) | (ki >= DIL_BLK))


def _dil_fwd(qkv, d):
    length = SEQ // d
    nb = length // DIL_BLK
    qkv_v = qkv.reshape(length, d * 3 * ATT_W)

    def body(q_ref, kp_ref, kc_ref, vp_ref, vc_ref, o_ref, lse_ref):
        m_step = pl.program_id(1)
        lane = lax.broadcasted_iota(jnp.int32, (DIL_BLK, LANE), 1)
        jobs = [(sub, h) for sub in range(2) for h in range(N_HEADS)]
        rows = lambda sub: slice(sub * DIL_BLK, (sub + 1) * DIL_BLK)
        cols = lambda h: slice(h * HEAD_DIM, (h + 1) * HEAD_DIM)

        def keys(prev_ref, cur_ref, sub, h):
            before = prev_ref[:, cols(h)] if sub == 0 else cur_ref[rows(0), cols(h)]
            return jnp.concatenate([before, cur_ref[rows(sub), cols(h)]], axis=0)

        scores = [lax.dot_general(q_ref[rows(sub), cols(h)], keys(kp_ref, kc_ref, sub, h), NT_DIMS,
                                  preferred_element_type=F32) for sub, h in jobs]
        ok = [_dil_valid(m_step), _dil_valid(1)]
        probs, inv_l, lse_all = [], [], [jnp.zeros((DIL_BLK, LANE), F32)] * 2
        for idx, (sub, h) in enumerate(jobs):
            s = jnp.where(ok[sub], scores[idx], NEG_INF)
            m = jnp.max(s, axis=-1, keepdims=True)
            p = jnp.exp(s - m)
            l = jnp.sum(p, axis=-1, keepdims=True)
            probs.append(p.astype(BF16))
            inv_l.append(1.0 / l)
            lse_all[sub] = jnp.where(lane == h, m + jnp.log(l), lse_all[sub])
        outs = [jnp.dot(probs[idx], keys(vp_ref, vc_ref, sub, h), preferred_element_type=F32) * inv_l[idx]
                for idx, (sub, h) in enumerate(jobs)]
        for sub in range(2):
            o_ref[rows(sub), :] = jnp.concatenate(outs[sub * N_HEADS:(sub + 1) * N_HEADS], axis=1).astype(o_ref.dtype)
            lse_ref[rows(sub), :] = lse_all[sub]

    pair = lambda f: pl.BlockSpec((2 * DIL_BLK, ATT_W), f)
    one = lambda f: pl.BlockSpec((DIL_BLK, ATT_W), f)
    before = lambda m: jnp.maximum(2 * m - 1, 0)
    o, lse = pl.pallas_call(
        body, name=f"dil_fwd_d{d}", grid=(d, nb // 2),
        in_specs=[pair(lambda r, m: (m, 3 * r)),
                  one(lambda r, m: (before(m), 3 * r + 1)), pair(lambda r, m: (m, 3 * r + 1)),
                  one(lambda r, m: (before(m), 3 * r + 2)), pair(lambda r, m: (m, 3 * r + 2))],
        out_specs=[pair(lambda r, m: (m, r)), pl.BlockSpec((2 * DIL_BLK, LANE), lambda r, m: (m, r))],
        out_shape=[_sds((length, d * ATT_W), BF16), _sds((length, d * LANE), F32)],
        compiler_params=_params("parallel", "arbitrary"),
    )(qkv_v, qkv_v, qkv_v, qkv_v, qkv_v)
    return o.reshape(SEQ, ATT_W), lse.reshape(SEQ, LANE)


def _dil_merge(os_, lses, *, tm=512):
    def body(o0, o1, o2, l0, l1, l2, y_ref, lse_ref):
        ls = [l0[...], l1[...], l2[...]]
        m = jnp.maximum(jnp.maximum(ls[0], ls[1]), ls[2])
        es = [jnp.exp(l - m) for l in ls]
        tot = es[0] + es[1] + es[2]
        lse_ref[...] = m + jnp.log(tot)
        alphas = [e / tot for e in es]
        outs = []
        for h in range(N_HEADS):
            acc = None
            for g, o_ref in enumerate((o0, o1, o2)):
                term = alphas[g][:, h:h + 1] * _head(o_ref, h).astype(F32)
                acc = term if acc is None else acc + term
            outs.append(acc)
        y_ref[...] = jnp.concatenate(outs, axis=1).astype(y_ref.dtype)

    row = pl.BlockSpec((tm, ATT_W), lambda i: (i, 0))
    vec = pl.BlockSpec((tm, LANE), lambda i: (i, 0))
    return pl.pallas_call(
        body, name="dil_merge", grid=(SEQ // tm,),
        in_specs=[row] * 3 + [vec] * 3, out_specs=[row, vec],
        out_shape=[_sds((SEQ, ATT_W), BF16), _sds((SEQ, LANE), F32)],
        compiler_params=_params("parallel"),
    )(*os_, *lses)


def _dil_bwd(qkv, lse, delta, do, d):
    length = SEQ // d
    nb = length // DIL_BLK
    n_steps = nb // 2
    qkv_v = qkv.reshape(length, d * 3 * ATT_W)
    lse_v, dl_v, do_v = lse.reshape(length, d * LANE), delta.reshape(length, d * LANE), do.reshape(length, d * ATT_W)

    def body(q_ref, kp_ref, kc_ref, vp_ref, vc_ref, lse_ref, dl_ref, do_ref, dq_ref, dk_ref, dv_ref, dk_s, dv_s):
        m_step = pl.program_id(1)

        @pl.when(m_step == 0)
        def _():
            dk_s[...] = jnp.zeros_like(dk_s)
            dv_s[...] = jnp.zeros_like(dv_s)

        jobs = [(sub, h) for sub in range(2) for h in range(N_HEADS)]
        rows = lambda sub: slice(sub * DIL_BLK, (sub + 1) * DIL_BLK)
        cols = lambda h: slice(h * HEAD_DIM, (h + 1) * HEAD_DIM)

        def keys(prev_ref, cur_ref, sub, h):
            before = prev_ref[:, cols(h)] if sub == 0 else cur_ref[rows(0), cols(h)]
            return jnp.concatenate([before, cur_ref[rows(sub), cols(h)]], axis=0)

        kks = [keys(kp_ref, kc_ref, sub, h) for sub, h in jobs]
        scores = [lax.dot_general(q_ref[rows(sub), cols(h)], kks[idx], NT_DIMS, preferred_element_type=F32)
                  for idx, (sub, h) in enumerate(jobs)]
        dps = [lax.dot_general(do_ref[rows(sub), cols(h)], keys(vp_ref, vc_ref, sub, h), NT_DIMS,
                               preferred_element_type=F32) for sub, h in jobs]
        ok = [_dil_valid(m_step), _dil_valid(1)]
        ps, dss = [], []
        for idx, (sub, h) in enumerate(jobs):
            p = jnp.where(ok[sub], jnp.exp(scores[idx] - lse_ref[rows(sub), h:h + 1]), 0.0)
            ps.append(p.astype(BF16))
            dss.append((p * (dps[idx] - dl_ref[rows(sub), h:h + 1])).astype(BF16))
        dqs = [jnp.dot(dss[idx], kks[idx], preferred_element_type=F32) * SCALE for idx in range(len(jobs))]
        dkks = [lax.dot_general(dss[idx], q_ref[rows(sub), cols(h)], TN_DIMS, preferred_element_type=F32)
                for idx, (sub, h) in enumerate(jobs)]
        dvvs = [lax.dot_general(ps[idx], do_ref[rows(sub), cols(h)], TN_DIMS, preferred_element_type=F32)
                for idx, (sub, h) in enumerate(jobs)]
        for sub in range(2):
            dq_ref[rows(sub), :] = jnp.concatenate(dqs[sub * N_HEADS:(sub + 1) * N_HEADS], axis=1).astype(dq_ref.dtype)
        base = m_step * (2 * DIL_BLK)
        blocks = [pl.ds(pl.multiple_of(jnp.maximum(base - DIL_BLK, 0), DIL_BLK), DIL_BLK),
                  pl.ds(pl.multiple_of(base, DIL_BLK), DIL_BLK),
                  pl.ds(pl.multiple_of(base + DIL_BLK, DIL_BLK), DIL_BLK)]
        for acc, parts in ((dk_s, dkks), (dv_s, dvvs)):
            top = lambda sub: jnp.concatenate([parts[sub * N_HEADS + h][:DIL_BLK] for h in range(N_HEADS)], axis=1)
            bottom = lambda sub: jnp.concatenate([parts[sub * N_HEADS + h][DIL_BLK:] for h in range(N_HEADS)], axis=1)
            acc[blocks[0], :] += top(0)
            acc[blocks[1], :] += bottom(0) + top(1)
            acc[blocks[2], :] += bottom(1)

        @pl.when(m_step == n_steps - 1)
        def _():
            dk_ref[...] = dk_s[...].astype(dk_ref.dtype)
            dv_ref[...] = dv_s[...].astype(dv_ref.dtype)

    pair = lambda f: pl.BlockSpec((2 * DIL_BLK, ATT_W), f)
    one = lambda f: pl.BlockSpec((DIL_BLK, ATT_W), f)
    vec = lambda f: pl.BlockSpec((2 * DIL_BLK, LANE), f)
    whole = pl.BlockSpec((length, ATT_W), lambda r, m: (0, r))
    before = lambda m: jnp.maximum(2 * m - 1, 0)
    outs = pl.pallas_call(
        body, name=f"dil_bwd_d{d}", grid=(d, n_steps),
        in_specs=[pair(lambda r, m: (m, 3 * r)),
                  one(lambda r, m: (before(m), 3 * r + 1)), pair(lambda r, m: (m, 3 * r + 1)),
                  one(lambda r, m: (before(m), 3 * r + 2)), pair(lambda r, m: (m, 3 * r + 2)),
                  vec(lambda r, m: (m, r)), vec(lambda r, m: (m, r)), pair(lambda r, m: (m, r))],
        out_specs=[pair(lambda r, m: (m, r)), whole, whole],
        out_shape=[_sds((length, d * ATT_W), BF16)] * 3,
        scratch_shapes=[pltpu.VMEM((length, ATT_W), F32), pltpu.VMEM((length, ATT_W), F32)],
        compiler_params=_params("arbitrary", "arbitrary"),
    )(qkv_v, qkv_v, qkv_v, qkv_v, qkv_v, lse_v, dl_v, do_v)
    return [t.reshape(SEQ, ATT_W) for t in outs]


def _sigmoid(x):
    return 1.0 / (1.0 + jnp.exp(-x))


def _mix_fwd(ya, yb, w_oa, w_ob, zm, *, tm=512):
    def body(ya_ref, yb_ref, wa_ref, wb_ref, ga_ref, gb_ref, pa_ref, pb_ref, mix_ref):
        pa = jnp.dot(ya_ref[...], wa_ref[...], preferred_element_type=F32)
        pb = jnp.dot(yb_ref[...], wb_ref[...], preferred_element_type=F32)
        pa_ref[...] = pa.astype(pa_ref.dtype)
        pb_ref[...] = pb.astype(pb_ref.dtype)
        mix_ref[...] = (_sigmoid(ga_ref[...].astype(F32)) * pa + _sigmoid(gb_ref[...].astype(F32)) * pb
                        ).astype(mix_ref.dtype)

    row = pl.BlockSpec((tm, ATT_W), lambda i: (i, 0))
    wsp = pl.BlockSpec((ATT_W, D_MODEL), lambda i: (0, 0))
    wide = pl.BlockSpec((tm, D_MODEL), lambda i: (i, 0))
    return pl.pallas_call(
        body, name="mix_fwd", grid=(SEQ // tm,),
        in_specs=[row, row, wsp, wsp, pl.BlockSpec((tm, D_MODEL), lambda i: (i, 3)),
                  pl.BlockSpec((tm, D_MODEL), lambda i: (i, 4))],
        out_specs=[wide] * 3, out_shape=[_sds((SEQ, D_MODEL), BF16)] * 3,
        compiler_params=_params("parallel"),
    )(ya, yb, w_oa, w_ob, zm, zm)


def _gate_bwd(dmix, zm, p, gate_block, dz, *, name, tm=512):
    def body(dm_ref, g_ref, p_ref, *rest):
        dp_ref, dz_ref = rest[-2], rest[-1]
        dm = dm_ref[...].astype(F32)
        s = _sigmoid(g_ref[...].astype(F32))
        dp_ref[...] = (dm * s).astype(dp_ref.dtype)
        dz_ref[...] = (dm * p_ref[...].astype(F32) * s * (1.0 - s)).astype(dz_ref.dtype)

    wide = pl.BlockSpec((tm, D_MODEL), lambda i: (i, 0))
    gate = pl.BlockSpec((tm, D_MODEL), lambda i: (i, gate_block))
    extra = [] if dz is None else [dz]
    return pl.pallas_call(
        body, name=name, grid=(SEQ // tm,),
        in_specs=[wide, gate, wide] + [ANY] * len(extra),
        out_specs=[wide, gate],
        out_shape=[_sds((SEQ, D_MODEL), BF16), _sds((SEQ, Z_MAIN), BF16)],
        input_output_aliases={3: 1} if extra else {},
        compiler_params=_params("parallel"),
    )(dmix, zm, p, *extra)


def _out_fwd(mixed, w_out, x, g_post, g_pre, *, tm=512):
    def body(m_ref, w_ref, x_ref, gp_ref, gn_ref, y_ref, x2_ref, h_ref):
        y = jnp.dot(m_ref[...], w_ref[...], preferred_element_type=F32)
        y_ref[...] = y
        r = lax.rsqrt(jnp.mean(y * y, axis=-1, keepdims=True) + RMS_EPS)
        x2 = x_ref[...] + y * r * gp_ref[...]
        x2_ref[...] = x2
        r2 = lax.rsqrt(jnp.mean(x2 * x2, axis=-1, keepdims=True) + RMS_EPS)
        h_ref[...] = (x2 * r2 * gn_ref[...]).astype(h_ref.dtype)

    row = pl.BlockSpec((tm, D_MODEL), lambda i: (i, 0))
    vec = pl.BlockSpec((1, D_MODEL), lambda i: (0, 0))
    return pl.pallas_call(
        body, name="out_fwd", grid=(SEQ // tm,),
        in_specs=[row, pl.BlockSpec((D_MODEL, D_MODEL), lambda i: (0, 0)), row, vec, vec],
        out_specs=[row] * 3,
        out_shape=[_sds((SEQ, D_MODEL), F32), _sds((SEQ, D_MODEL), F32), _sds((SEQ, D_MODEL), BF16)],
        compiler_params=_params("parallel"),
    )(mixed, w_out, x, g_post, g_pre)


FFN_TM = 1024
FFN_HALF = 256
FFN_TN = 2 * FFN_HALF
FFN_NJ = D_FF // FFN_HALF
FFN_GROUP = 2 * SUBLANE


def _ffn_interleave(t):
    lead = t.shape[:-1]
    return jnp.swapaxes(t.reshape(*lead, 2, FFN_NJ, FFN_HALF), -3, -2).reshape(*lead, 2 * D_FF)


def _ffn_deinterleave(t):
    lead = t.shape[:-1]
    return jnp.swapaxes(t.reshape(*lead, FFN_NJ, 2, FFN_HALF), -3, -2).reshape(*lead, 2 * D_FF)


def _ffn_move_blocks(t, *, interleave, name):
    rows = t.shape[0]
    if interleave:
        src = lambda jb: (0, (jb % 2) * FFN_NJ + jb // 2)
    else:
        src = lambda jb: (0, 2 * (jb % FFN_NJ) + jb // FFN_NJ)

    def body(x_ref, o_ref):
        o_ref[...] = x_ref[...]

    return pl.pallas_call(
        body, name=name, grid=(2 * FFN_NJ,),
        in_specs=[pl.BlockSpec((rows, FFN_HALF), src)],
        out_specs=pl.BlockSpec((rows, FFN_HALF), lambda jb: (0, jb)),
        out_shape=_sds(t.shape, t.dtype),
        compiler_params=_params("parallel"),
    )(t)


def _gelu_parts(a):
    c = math.sqrt(2.0 / math.pi)
    a2 = a * a
    t = jnp.tanh((c * a) * (1.0 + 0.044715 * a2))
    half_a, one_t = 0.5 * a, 1.0 + t
    gelu = half_a * one_t
    dgelu = 0.5 * one_t + half_a * (1.0 - t * t) * (c + (3.0 * 0.044715 * c) * a2)
    return gelu, dgelu


def _row_masks(down):
    row = lax.broadcasted_iota(jnp.int32, (SUBLANE, FFN_TN), 0)
    return (row < 1, row < 2) if down else (row >= SUBLANE - 1, row >= SUBLANE - 2)


def _rolled(x, down):
    return (pltpu.roll(x, 1, 0), pltpu.roll(x, 2, 0)) if down else (
        pltpu.roll(x, SUBLANE - 1, 0), pltpu.roll(x, SUBLANE - 2, 0))


def _shifted(cur_rolled, neighbour_rolled, masks):
    return (jnp.where(masks[0], neighbour_rolled[0], cur_rolled[0]),
            jnp.where(masks[1], neighbour_rolled[1], cur_rolled[1]))


def _conv_consts(w_ref, b_ref):
    shape = (SUBLANE, FFN_TN)
    return [jnp.broadcast_to(w_ref[k:k + 1, :], shape) for k in range(3)] + [jnp.broadcast_to(b_ref[...], shape)]


def _ffn_mid_fwd(u, conv_w, conv_b):
    per = FFN_TM // SUBLANE

    def body(u_ref, h_ref, w_ref, b_ref, m_ref, ab_ref):
        live = (pl.program_id(1) > 0).astype(F32)
        w0, w1, w2, bias = _conv_consts(w_ref, b_ref)
        masks = _row_masks(True)

        def group(g, above):
            rows = pl.ds(pl.multiple_of(g * FFN_GROUP, FFN_GROUP), FFN_GROUP)
            x = u_ref[rows, :].astype(F32)
            convs = []
            for c in range(2):
                cur = x[c * SUBLANE:(c + 1) * SUBLANE]
                cur_rolled = _rolled(cur, True)
                s1, s2 = _shifted(cur_rolled, above, masks)
                convs.append(w0 * s2 + w1 * s1 + w2 * cur + bias)
                above = cur_rolled
            y = jnp.concatenate(convs, axis=0)
            ab_ref[rows, :] = y.astype(ab_ref.dtype)
            m_ref[rows, :] = (_gelu_parts(y[:, :FFN_HALF])[0] * y[:, FFN_HALF:]).astype(m_ref.dtype)
            return above

        lax.fori_loop(0, FFN_TM // (2 * FFN_GROUP), lambda g2, carry: group(2 * g2 + 1, group(2 * g2, carry)),
                      _rolled(h_ref[...].astype(F32) * live, True))

    blk = pl.BlockSpec((FFN_TM, FFN_TN), lambda j, i: (i, j))
    return pl.pallas_call(
        body, name="ffn_mid_fwd", grid=(FFN_NJ, SEQ // FFN_TM),
        in_specs=[blk, pl.BlockSpec((SUBLANE, FFN_TN), lambda j, i: (jnp.maximum(i * per - 1, 0), j)),
                  pl.BlockSpec((3, FFN_TN), lambda j, i: (0, j)), pl.BlockSpec((1, FFN_TN), lambda j, i: (0, j))],
        out_specs=[pl.BlockSpec((FFN_TM, FFN_HALF), lambda j, i: (i, j)), blk],
        out_shape=[_sds((SEQ, D_FF), BF16), _sds((SEQ, 2 * D_FF), BF16)],
        compiler_params=_params("parallel", "arbitrary"),
    )(u, u, conv_w, conv_b)


def _ffn_mid_bwd(dm, u, ab, conv_w):
    nrow = SEQ // FFN_TM
    n_groups = FFN_TM // FFN_GROUP

    def body(dm_ref, u_ref, ab_ref, w_ref, du_ref, gw_ref, gb_ref, c_s):
        @pl.when(pl.program_id(1) == 0)
        def _():
            c_s[...] = jnp.zeros_like(c_s)
            gw_ref[...] = jnp.zeros_like(gw_ref)
            gb_ref[...] = jnp.zeros_like(gb_ref)

        taps = [jnp.broadcast_to(w_ref[k:k + 1, :], (SUBLANE, FFN_TN)) for k in range(3)]
        masks = _row_masks(False)

        def group(t, carry):
            below, acc = carry
            rows = pl.ds(pl.multiple_of((n_groups - 1 - t) * FFN_GROUP, FFN_GROUP), FFN_GROUP)
            x, y, dmv = u_ref[rows, :].astype(F32), ab_ref[rows, :].astype(F32), dm_ref[rows, :].astype(F32)
            gelu, dgelu = _gelu_parts(y[:, :FFN_HALF])
            d = jnp.concatenate([dmv * y[:, FFN_HALF:] * dgelu, dmv * gelu], axis=1)
            acc, pre = list(acc), [None, None]
            for c in (1, 0):
                sl = slice(c * SUBLANE, (c + 1) * SUBLANE)
                cur, xs = d[sl], x[sl]
                cur_rolled = _rolled(cur, False)
                up1, up2 = _shifted(cur_rolled, below, masks)
                acc = [acc[0] + up2 * xs, acc[1] + up1 * xs, acc[2] + cur * xs, acc[3] + cur]
                pre[c] = taps[2] * cur + taps[1] * up1 + taps[0] * up2
                below = cur_rolled
            du_ref[rows, :] = jnp.concatenate(pre, axis=0).astype(du_ref.dtype)
            return below, tuple(acc)

        zeros = jnp.zeros((SUBLANE, FFN_TN), F32)
        below, acc = lax.fori_loop(0, n_groups // 2, lambda t2, carry: group(2 * t2 + 1, group(2 * t2, carry)),
                                   (_rolled(c_s[...], False), (zeros,) * 4))
        c_s[...] = pltpu.roll(below[0], 1, 0)
        for k in range(3):
            gw_ref[k:k + 1, :] += jnp.sum(acc[k], axis=0, keepdims=True)
        gb_ref[...] += jnp.sum(acc[3], axis=0, keepdims=True)

    blk = pl.BlockSpec((FFN_TM, FFN_TN), lambda j, i: (nrow - 1 - i, j))
    return pl.pallas_call(
        body, name="ffn_mid_bwd", grid=(FFN_NJ, nrow),
        in_specs=[pl.BlockSpec((FFN_TM, FFN_HALF), lambda j, i: (nrow - 1 - i, j)), blk, blk,
                  pl.BlockSpec((3, FFN_TN), lambda j, i: (0, j))],
        out_specs=[blk, pl.BlockSpec((3, FFN_TN), lambda j, i: (0, j)), pl.BlockSpec((1, FFN_TN), lambda j, i: (0, j))],
        out_shape=[_sds((SEQ, 2 * D_FF), BF16), _sds((3, 2 * D_FF), F32), _sds((1, 2 * D_FF), F32)],
        scratch_shapes=[pltpu.VMEM((SUBLANE, FFN_TN), F32)],
        compiler_params=_params("parallel", "arbitrary"),
    )(dm, u, ab, conv_w)


def _down_fwd(m, w_down, x2, g_post, target, *, tm=512):
    def body(m_ref, w_ref, x2_ref, g_ref, t_ref, dout_ref, dy_ref, gg_ref, loss_ref):
        @pl.when(pl.program_id(0) == 0)
        def _():
            gg_ref[...] = jnp.zeros_like(gg_ref)
            loss_ref[...] = jnp.zeros_like(loss_ref)

        y = jnp.dot(m_ref[...], w_ref[...], preferred_element_type=F32)
        r = lax.rsqrt(jnp.mean(y * y, axis=-1, keepdims=True) + RMS_EPS)
        yn = y * r
        diff = (x2_ref[...] + yn * g_ref[...]) - t_ref[...]
        loss_ref[...] += jnp.sum(diff * diff)
        dout = diff * (1.0 / D_MODEL)
        dout_ref[...] = dout
        gg_ref[...] += jnp.sum(dout * yn, axis=0, keepdims=True)
        dn = dout * g_ref[...]
        dy_ref[...] = (r * (dn - yn * jnp.mean(dn * yn, axis=-1, keepdims=True))).astype(dy_ref.dtype)

    row = pl.BlockSpec((tm, D_MODEL), lambda i: (i, 0))
    vec = pl.BlockSpec((1, D_MODEL), lambda i: (0, 0))
    return pl.pallas_call(
        body, name="down_fwd", grid=(SEQ // tm,),
        in_specs=[pl.BlockSpec((tm, D_FF), lambda i: (i, 0)), pl.BlockSpec((D_FF, D_MODEL), lambda i: (0, 0)),
                  row, vec, row],
        out_specs=[row, row, vec, pl.BlockSpec((1, LANE), lambda i: (0, 0))],
        out_shape=[_sds((SEQ, D_MODEL), F32), _sds((SEQ, D_MODEL), BF16), _sds((1, D_MODEL), F32),
                   _sds((1, LANE), F32)],
        compiler_params=_params("arbitrary"),
    )(m, w_down, x2, g_post, target)


def _local_step(x, target, w_main, w_f, b_forget, conv_b, g_pre_mix, g_post_mix, g_pre_ffn, g_post_ffn,
                late_weights, ffn_grads_ready, proj_grads_ready, mixer_grads_ready):
    mm = _matmul
    tabs = _rope_tables()

    h1 = _rms_fwd(x, g_pre_mix, name="rms_pre_mix")
    zm = mm(h1, w_main, out_dtype=BF16, tm=2048, tn=512, tk=1024, name="in_proj")
    zf = mm(h1, w_f, out_dtype=F32, tm=2048, tn=F_PAD, tk=1024, name="in_proj_forget")
    f_row, sg_row = _fox_prep(zf[:, :N_HEADS].T, b_forget.reshape(N_HEADS, 1))
    f_cols = jnp.pad(f_row.T, ((0, 0), (0, LANE - N_HEADS)))
    q_slots, k_slots, v_slots = _fox_pack_fwd(zm, f_cols)
    ya, lse_a = _fox_fwd(q_slots, k_slots, v_slots)
    qkv_d = _rope_fwd(zm, tabs)
    dil = [_dil_fwd(qkv_d, d) for _, d in DIL_PATTERNS]
    yb, lse_b = _dil_merge([o for o, _ in dil], [l for _, l in dil])
    w_oa, w_ob, w_out, w_up, conv_w, w_down = late_weights(yb)
    pa, pb, mixed = _mix_fwd(ya, yb, w_oa, w_ob, zm)
    y1, x2, h2 = _out_fwd(mixed, w_out, x, g_post_mix, g_pre_ffn)
    u = mm(h2, w_up, out_dtype=BF16, tm=2048, tn=512, tk=1024, name="up_proj")
    m, ab = _ffn_mid_fwd(u, conv_w, _ffn_interleave(conv_b))
    dout, dy2, gg_post_ffn, sq_err = _down_fwd(m, w_down, x2, g_post_ffn, target)

    g_w_down = mm(m, dy2, ta=True, out_dtype=BF16, tm=D_FF // 2, tn=1024, tk=2048, name="grad_w_down")
    dm = mm(dy2, w_down, tb=True, out_dtype=BF16, tm=2048, tn=D_FF // 2, tk=1024, name="d_ffn_mid")
    du, g_conv_w, g_conv_b = _ffn_mid_bwd(dm, u, ab, conv_w)
    g_w_up = mm(h2, du, ta=True, out_dtype=BF16, tm=1024, tn=D_FF // 2, tk=2048, name="grad_w_up")
    tok = ffn_grads_ready(dict(w_down=g_w_down, w_up=_ffn_move_blocks(g_w_up, interleave=False, name="grad_w_up_cols"),
                               conv_w=_ffn_deinterleave(g_conv_w)))
    dh2 = mm(du, w_up, tb=True, out_dtype=BF16, tm=512, tn=1024, tk=2 * D_FF, name="d_h2")

    dx2, dy1, gg_pre_ffn, gg_post_mix = _rms_pair_bwd([dh2], x2, g_pre_ffn, dout, y1, g_post_mix + tok)
    g_w_out = mm(mixed, dy1, ta=True, out_dtype=BF16, tm=1024, tn=1024, tk=2048, name="grad_w_out")
    dmix = mm(dy1, w_out, tb=True, out_dtype=BF16, tm=2048, tn=1024, tk=1024, name="d_mixed")
    dpa, dz = _gate_bwd(dmix, zm, pa, 3, None, name="gate_bwd_fox")
    dpb, dz = _gate_bwd(dmix, zm, pb, 4, dz, name="gate_bwd_dil")
    g_w_oa = mm(ya, dpa, ta=True, out_dtype=BF16, tm=512, tn=1024, tk=SEQ, name="grad_w_o_fox")
    g_w_ob = mm(yb, dpb, ta=True, out_dtype=BF16, tm=512, tn=1024, tk=SEQ, name="grad_w_o_dil")
    tok = proj_grads_ready(dict(w_o_fox=g_w_oa, w_o_dil=g_w_ob, w_out=g_w_out))
    dya = mm(dpa, w_oa, tb=True, out_dtype=BF16, tm=2048, tn=512, tk=1024, name="d_y_fox")
    dyb = mm(dpb, w_ob, tb=True, out_dtype=BF16, tm=2048, tn=512, tk=1024, name="d_y_dil")

    qb_slots, do_slots = _fox_pack_bwd(zm, f_cols + tok, lse_a, ya, dya)
    dz, df_cols = _fox_unpack(*_fox_bwd(qb_slots, k_slots, v_slots, do_slots), dz)
    dfa_t, g_b_forget = _fox_post_bwd(df_cols[:, :N_HEADS].T, sg_row)

    delta_b = _attn_delta(yb, dyb, name="delta_dil")
    dil_g = [_dil_bwd(qkv_d, lse_b, delta_b, dyb, d) for _, d in DIL_PATTERNS]
    dz = _dil_grad_combine([g[0] for g in dil_g], [g[1] for g in dil_g], [g[2] for g in dil_g], tabs, dz)

    dzf = jnp.pad(dfa_t.T, ((0, 0), (0, F_PAD - N_HEADS)))
    g_w_main = mm(h1, dz, ta=True, out_dtype=BF16, tm=1024, tn=Z_MAIN // 4, tk=2048, name="grad_w_in")
    g_w_f = mm(h1, dzf, ta=True, out_dtype=BF16, tm=1024, tn=F_PAD, tk=1024, name="grad_w_in_forget")
    tok = mixer_grads_ready(dict(w_main=g_w_main, w_f=g_w_f))
    dh1 = [mm(dz, w_main, tb=True, out_dtype=BF16, tm=512, tn=1024, tk=Z_MAIN, name="d_h1"),
           mm(dzf + tok, w_f, tb=True, out_dtype=BF16, tm=2048, tn=1024, tk=F_PAD, name="d_h1_forget")]
    grad_x, gg_pre_mix = _rms_bwd(dh1, x, g_pre_mix, dx2, out_dtype=F32, name="rms_pre_mix_bwd")

    grads = dict(
        b_forget=g_b_forget.reshape(1, N_HEADS), conv_b=_ffn_deinterleave(g_conv_b),
        g_pre_mix=gg_pre_mix, g_post_mix=gg_post_mix, g_pre_ffn=gg_pre_ffn, g_post_ffn=gg_post_ffn)
    return sq_err, grad_x, grads


def _exchange(arrays, scatter, *, name):
    n = len(arrays)
    scatters = [scatter] * n if isinstance(scatter, bool) else list(scatter)

    def body(*refs):
        ins, outs = refs[:n], refs[n:2 * n]
        send_sems, recv_sems, local_sems = refs[2 * n:]
        me, peers = _peers()

        def remote(a, k):
            dev, slot = peers[k]
            return pltpu.make_async_remote_copy(
                src_ref=ins[a].at[slot] if scatters[a] else ins[a], dst_ref=outs[a].at[me],
                send_sem=send_sems.at[a, k], recv_sem=recv_sems.at[a, k],
                device_id=dev, device_id_type=MESH_ID)

        def landed(a, k):
            dev, slot = peers[k]
            return pltpu.make_async_remote_copy(
                src_ref=outs[a].at[slot], dst_ref=outs[a].at[slot],
                send_sem=send_sems.at[a, k], recv_sem=recv_sems.at[a, k],
                device_id=dev, device_id_type=MESH_ID)

        own = [pltpu.make_async_copy(ins[a].at[me] if scatters[a] else ins[a], outs[a].at[me], local_sems.at[a])
               for a in range(n)]
        copies = [remote(a, k) for k in range(N_DEV - 1) for a in range(n)]
        for cp in own + copies:
            cp.start()
        for k in range(N_DEV - 1):
            for a in range(n):
                landed(a, k).wait_recv()
        for cp in copies:
            cp.wait_send()
        for cp in own:
            cp.wait()

    out_shape = [_sds(((N_DEV,) + a.shape[-2:]), a.dtype) for a in arrays]
    return pl.pallas_call(
        body, name=name, in_specs=[ANY] * n, out_specs=[ANY] * n, out_shape=out_shape,
        scratch_shapes=[pltpu.SemaphoreType.DMA((n, N_DEV - 1)), pltpu.SemaphoreType.DMA((n, N_DEV - 1)),
                        pltpu.SemaphoreType.DMA((n,))],
    )(*arrays)


def _gather_two_level(shard, *, name):
    def body(x_ref, out_ref, send_sems, recv_sems, local_sem):
        x, y, c = lax.axis_index("x"), lax.axis_index("y"), lax.axis_index("c")
        me, sibling = (x, y, c), (x, y, 1 - c)
        chips = [(1 - x, y), (x, 1 - y), (1 - x, 1 - y)]

        def slot(px, py, pc):
            return out_ref.at[4 * px + 2 * py + pc]

        def copy(k, block, to, src=None):
            return pltpu.make_async_remote_copy(
                src_ref=slot(*block) if src is None else src, dst_ref=slot(*block),
                send_sem=send_sems.at[k], recv_sem=recv_sems.at[k], device_id=to, device_id_type=MESH_ID)

        mine = pltpu.make_async_copy(x_ref, slot(*me), local_sem)
        mine.start()
        first = [copy(0, me, sibling, src=x_ref)]
        first += [copy(1 + j, me, (*chip, c), src=x_ref) for j, chip in enumerate(chips)]
        for cp in first:
            cp.start()
        passed = [copy(4 + j, (*chip, c), sibling) for j, chip in enumerate(chips)]
        for j, chip in enumerate(chips):
            copy(1 + j, (*chip, c), me).wait_recv()
            passed[j].start()
        copy(0, sibling, me).wait_recv()
        for j, chip in enumerate(chips):
            copy(4 + j, (*chip, 1 - c), me).wait_recv()
        for cp in first + passed:
            cp.wait_send()
        mine.wait()

    return pl.pallas_call(
        body, name=name, in_specs=[ANY], out_specs=ANY, out_shape=_sds((N_DEV,) + shard.shape, shard.dtype),
        scratch_shapes=[pltpu.SemaphoreType.DMA((N_DEV - 1,)), pltpu.SemaphoreType.DMA((N_DEV - 1,)),
                        pltpu.SemaphoreType.DMA],
    )(shard)


N_CHIPS = N_DEV // 2


def _peers(chips_only=False):
    x, y, c = lax.axis_index("x"), lax.axis_index("y"), lax.axis_index("c")
    out = []
    if chips_only:
        for k in range(1, N_CHIPS):
            px = 1 - x if k & 2 else x
            py = 1 - y if k & 1 else y
            out.append(((px, py, c), 2 * px + py))
        return 2 * x + y, out
    for k in range(1, N_DEV):
        px = 1 - x if k & 4 else x
        py = 1 - y if k & 2 else y
        pc = 1 - c if k & 1 else c
        out.append(((px, py, pc), 4 * px + 2 * py + pc))
    return 4 * x + 2 * y + c, out


def _sibling_swap(slot_arrays, *, name):
    n = len(slot_arrays)

    def body(*refs):
        ins, outs, send_sems, recv_sems = refs[:n], refs[n:2 * n], refs[2 * n], refs[2 * n + 1]
        x, y, c = lax.axis_index("x"), lax.axis_index("y"), lax.axis_index("c")
        copies = [pltpu.make_async_remote_copy(
            src_ref=ins[a].at[2 * q + (1 - c)], dst_ref=outs[a].at[q], send_sem=send_sems.at[a, q],
            recv_sem=recv_sems.at[a, q], device_id=(x, y, 1 - c), device_id_type=MESH_ID)
            for a in range(n) for q in range(N_CHIPS)]
        for cp in copies:
            cp.start()
        for cp in copies:
            cp.wait_recv()
        for cp in copies:
            cp.wait_send()

    return pl.pallas_call(
        body, name=name, in_specs=[ANY] * n, out_specs=[ANY] * n,
        out_shape=[_sds((N_CHIPS,) + t.shape[1:], t.dtype) for t in slot_arrays],
        scratch_shapes=[pltpu.SemaphoreType.DMA((n, N_CHIPS)), pltpu.SemaphoreType.DMA((n, N_CHIPS))],
    )(*slot_arrays)


def _pair_sum(slots, from_sibling, *, name, tn):
    _, r, c = slots.shape
    core = lax.axis_index("c").astype(jnp.int32).reshape(1)

    def body(core_ref, a_ref, b_ref, o_ref):
        o_ref[...] = (a_ref[...].astype(F32) + b_ref[...].astype(F32)).astype(o_ref.dtype)

    blk = lambda f: pl.BlockSpec((1, r, tn), f)
    return pl.pallas_call(
        body, name=name,
        grid_spec=pltpu.PrefetchScalarGridSpec(
            num_scalar_prefetch=1, grid=(N_CHIPS, c // tn),
            in_specs=[blk(lambda q, j, core: (2 * q + core[0], 0, j)), blk(lambda q, j, core: (q, 0, j))],
            out_specs=blk(lambda q, j, core: (q, 0, j))),
        out_shape=_sds((N_CHIPS, r, c), slots.dtype),
        compiler_params=_params("parallel", "parallel"),
    )(core, slots, from_sibling)


def _sum_parts(parts, *, name, tn):
    n, r, c = parts.shape

    def body(p_ref, o_ref):
        total = p_ref[0].astype(F32)
        for s in range(1, n):
            total = total + p_ref[s].astype(F32)
        o_ref[...] = total

    return pl.pallas_call(
        body, name=name, grid=(c // tn,),
        in_specs=[pl.BlockSpec((n, r, tn), lambda j: (0, 0, j))],
        out_specs=pl.BlockSpec((r, tn), lambda j: (0, j)), out_shape=_sds((r, c), F32),
        compiler_params=_params("parallel"),
    )(parts)


HBM = pl.BlockSpec(memory_space=pltpu.HBM)
SEM = pl.BlockSpec(memory_space=pltpu.SEMAPHORE)
DATAFLOW = pltpu.SideEffectType.DATAFLOW_SIDE_EFFECTING


def _split_copy(srcs, lands, send_sems, recv_sems, scatter, a, k, me, peers, incoming=False):
    dev, slot = peers[k]
    if incoming:
        src = dst = lands[a].at[slot]
    else:
        src, dst = (srcs[a].at[slot] if scatter else srcs[a]), lands[a].at[me]
    sem = a * len(peers) + k
    return pltpu.make_async_remote_copy(
        src_ref=src, dst_ref=dst, send_sem=send_sems.at[sem], recv_sem=recv_sems.at[sem],
        device_id=dev, device_id_type=MESH_ID)


def _exchange_start(arrays, scatter, *, name, chips_only=False):
    n = len(arrays)
    n_slots = N_CHIPS if chips_only else N_DEV

    def body(*refs):
        srcs, lands = refs[:n], refs[n:2 * n]
        send_sems, recv_sems = refs[2 * n], refs[2 * n + 1]
        token = refs[-1]
        me, peers = _peers(chips_only)
        for k in range(len(peers)):
            for a in range(n):
                _split_copy(srcs, lands, send_sems, recv_sems, scatter, a, k, me, peers).start()
        token[...] = jnp.zeros_like(token)

    land_shapes = [((n_slots,) + a.shape[-2:], a.dtype) for a in arrays]
    sems = pltpu.SemaphoreType.DMA((n * (n_slots - 1),))
    outs = pl.pallas_call(
        body, name=name,
        out_shape=(sems, sems, *[pltpu.HBM(a.shape, a.dtype) for a in arrays],
                   *[pltpu.HBM(s, d) for s, d in land_shapes], _sds((SUBLANE, LANE), F32)),
        in_specs=[HBM] * (2 * n),
        out_specs=(SEM, SEM, *[HBM] * (2 * n), pl.BlockSpec(memory_space=pltpu.VMEM)),
        input_output_aliases={i: 2 + i for i in range(2 * n)},
        compiler_params=pltpu.CompilerParams(has_side_effects=DATAFLOW),
    )(*[pltpu.with_memory_space_constraint(a, pltpu.HBM) for a in arrays],
      *[pltpu.with_memory_space_constraint(lax.empty(s, d), pltpu.HBM) for s, d in land_shapes])
    return (outs[0], outs[1], outs[2:2 + n], outs[2 + n:2 + 2 * n], scatter, chips_only), outs[-1]


def _exchange_wait(handles, after, *, name):
    send_sems, recv_sems, srcs, lands, scatter, chips_only = handles
    n = len(srcs)

    def body(*refs):
        src_refs, land_refs = refs[:n], refs[n:2 * n]
        send_ref, recv_ref = refs[2 * n], refs[2 * n + 1]
        me, peers = _peers(chips_only)
        for k in range(len(peers)):
            for a in range(n):
                _split_copy(src_refs, land_refs, send_ref, recv_ref, scatter, a, k, me, peers).wait_send()
                _split_copy(src_refs, land_refs, send_ref, recv_ref, scatter, a, k, me, peers, True).wait_recv()

    outs = pl.pallas_call(
        body, name=name,
        out_shape=tuple(pltpu.HBM(t.shape, t.dtype) for t in (*srcs, *lands)),
        in_specs=[HBM] * (2 * n) + [SEM, SEM, pl.BlockSpec(memory_space=pl.ANY)],
        out_specs=tuple([HBM] * (2 * n)),
        input_output_aliases={i: i for i in range(2 * n)},
        compiler_params=pltpu.CompilerParams(has_side_effects=DATAFLOW),
    )(*srcs, *lands, send_sems, recv_sems, after)
    return _with_own_slot(outs[n:], outs[:n], scatter, chips_only)


def _with_own_slot(landed, own, scatter, chips_only):
    me = 2 * lax.axis_index("x") + lax.axis_index("y")
    if not chips_only:
        me = 2 * me + lax.axis_index("c")
    out = []
    for buf, src in zip(landed, own):
        mine = lax.dynamic_index_in_dim(src, me, 0, keepdims=False) if scatter else src
        out.append(lax.dynamic_update_index_in_dim(buf, mine, me, 0))
    return out


def _adamw(parts, w, m, v, *, name, tm):
    r, c = w.shape
    assert r % tm == 0

    def body(p_ref, w_ref, m_ref, v_ref, g_ref, d_ref, nm_ref, nv_ref):
        _adamw_update(p_ref, w_ref, m_ref, v_ref, g_ref, d_ref, nm_ref, nv_ref)

    blk = pl.BlockSpec((tm, c), lambda i: (i, 0))
    return pl.pallas_call(
        body, name=name, grid=(r // tm,),
        in_specs=[pl.BlockSpec((parts.shape[0], tm, c), lambda i: (0, i, 0)), blk, blk, blk],
        out_specs=[blk] * 4, out_shape=[_sds((r, c), F32)] * 4,
        compiler_params=_params("parallel"),
    )(parts, w, m, v)


def _adamw_update(p_ref, w_ref, m_ref, v_ref, g_ref, d_ref, nm_ref, nv_ref):
    g = p_ref[0].astype(F32)
    for s in range(1, p_ref.shape[0]):
        g = g + p_ref[s].astype(F32)
    g_ref[...] = g
    m_new = ADAM_B1 * m_ref[...] + (1.0 - ADAM_B1) * g
    v_new = ADAM_B2 * v_ref[...] + (1.0 - ADAM_B2) * (g * g)
    nm_ref[...] = m_new
    nv_ref[...] = v_new
    m_hat = m_new / (1.0 - ADAM_B1 ** ADAM_STEP)
    v_hat = v_new / (1.0 - ADAM_B2 ** ADAM_STEP)
    d_ref[...] = -ADAM_LR * (m_hat / (jnp.sqrt(v_hat) + ADAM_EPS) + ADAM_WD * w_ref[...])


SMALL = ("g_pre_mix", "b_forget", "g_post_mix", "g_pre_ffn", "conv_b", "g_post_ffn")


def _adamw_small(parts, ws, ms, vs, sq_err_parts):
    n = len(ws)

    def body(*refs):
        ins, sq_ref, outs, loss_ref = refs[:4 * n], refs[4 * n], refs[4 * n + 1:-1], refs[-1]
        for i in range(n):
            _adamw_update(ins[i], ins[n + i], ins[2 * n + i], ins[3 * n + i], *outs[4 * i:4 * i + 4])
        total = sq_ref[0]
        for s in range(1, N_DEV):
            total = total + sq_ref[s]
        loss_ref[...] = total * (0.5 / D_MODEL)

    res = pl.pallas_call(
        body, name="adamw_small",
        out_shape=[_sds(w.shape, F32) for w in ws for _ in range(4)] + [_sds((1, LANE), F32)],
        compiler_params=pltpu.CompilerParams(vmem_limit_bytes=VMEM_LIMIT),
    )(*parts, *ws, *ms, *vs, sq_err_parts)
    return [res[4 * i:4 * i + 4] for i in range(n)], res[-1][0, 0]


def kernel(x, g_pre_mix, w_in, b_forget, w_o_fox, w_o_dil, w_out, g_post_mix, g_pre_ffn, w_up, conv_w, conv_b, w_down, g_post_ffn, loss_target, m_g_pre_mix, m_w_in, m_b_forget, m_w_o_fox, m_w_o_dil, m_w_out, m_g_post_mix, m_g_pre_ffn, m_w_up, m_conv_w, m_conv_b, m_w_down, m_g_post_ffn, v_g_pre_mix, v_w_in, v_b_forget, v_w_o_fox, v_w_o_dil, v_w_out, v_g_post_mix, v_g_pre_ffn, v_w_up, v_conv_w, v_conv_b, v_w_down, v_g_post_ffn):
    names = ("g_pre_mix", "w_in", "b_forget", "w_o_fox", "w_o_dil", "w_out", "g_post_mix", "g_pre_ffn",
             "w_up", "conv_w", "conv_b", "w_down", "g_post_ffn")
    w = dict(g_pre_mix=g_pre_mix, w_in=w_in, b_forget=b_forget, w_o_fox=w_o_fox, w_o_dil=w_o_dil, w_out=w_out,
             g_post_mix=g_post_mix, g_pre_ffn=g_pre_ffn, w_up=w_up, conv_w=conv_w, conv_b=conv_b, w_down=w_down,
             g_post_ffn=g_post_ffn)
    m = dict(g_pre_mix=m_g_pre_mix, w_in=m_w_in, b_forget=m_b_forget, w_o_fox=m_w_o_fox, w_o_dil=m_w_o_dil,
             w_out=m_w_out, g_post_mix=m_g_post_mix, g_pre_ffn=m_g_pre_ffn, w_up=m_w_up, conv_w=m_conv_w,
             conv_b=m_conv_b, w_down=m_w_down, g_post_ffn=m_g_post_ffn)
    v = dict(g_pre_mix=v_g_pre_mix, w_in=v_w_in, b_forget=v_b_forget, w_o_fox=v_w_o_fox, w_o_dil=v_w_o_dil,
             w_out=v_w_out, g_post_mix=v_g_post_mix, g_pre_ffn=v_g_pre_ffn, w_up=v_w_up, conv_w=v_conv_w,
             conv_b=v_conv_b, w_down=v_w_down, g_post_ffn=v_g_post_ffn)
    sharded = ("w_in", "w_o_fox", "w_o_dil", "w_out", "w_up", "w_down", "conv_w")
    wire = lambda n: F32 if n == "conv_w" else BF16

    by_cols = lambda t: jnp.transpose(t, (1, 0, 2)).reshape(t.shape[1], N_DEV * t.shape[2])
    by_rows = lambda t: t.reshape(N_DEV * t.shape[1], t.shape[2])
    col_slots = lambda t: jnp.transpose(t.reshape(t.shape[0], N_DEV, t.shape[1] // N_DEV), (1, 0, 2))
    row_slots = lambda t: t.reshape(N_DEV, t.shape[0] // N_DEV, t.shape[1])
    to_slots = lambda n, t: (row_slots if n in ("w_out", "w_down") else col_slots)(t).astype(wire(n))
    shard = lambda n: w[n][0].astype(wire(n))
    f_lo, f_hi = 3 * ATT_W, 3 * ATT_W + N_HEADS

    w_in_full = by_cols(_gather_two_level(shard("w_in"), name="gather_w_in"))
    w_main = jnp.concatenate([w_in_full[:, :f_lo], w_in_full[:, f_hi:]], axis=1)
    w_f = jnp.pad(w_in_full[:, f_lo:f_hi], ((0, 0), (0, F_PAD - N_HEADS)))
    late = ("w_o_fox", "w_o_dil", "w_out", "w_up", "conv_w", "w_down")
    order = jnp.minimum(jnp.abs(w_in_full[0, 0].astype(F32)), 0.0)
    late_handles, late_tok = _exchange_start(
        [shard(n) + order.astype(wire(n)) if n == "conv_w" else shard(n) for n in late], False,
        name="gather_late_start")

    def late_weights(after):
        got = dict(zip(late, _exchange_wait(late_handles, after, name="gather_late_wait")))
        return (by_cols(got["w_o_fox"]), by_cols(got["w_o_dil"]), by_rows(got["w_out"]),
                _ffn_move_blocks(by_cols(got["w_up"]), interleave=True, name="w_up_cols"),
                _ffn_interleave(by_cols(got["conv_w"])),
                by_rows(got["w_down"]))

    pending = {}

    def ffn_grads_ready(g):
        pending["ffn"] = _exchange_start([to_slots(n, g[n]) for n in ("w_down", "w_up", "conv_w")], True,
                                         name="scatter_ffn_start")
        return pending["ffn"][1][0, 0]

    def proj_grads_ready(g):
        pending["proj"] = _exchange_start([to_slots(n, g[n]) for n in ("w_o_fox", "w_o_dil", "w_out")], True,
                                          name="scatter_proj_start")
        return pending["proj"][1][0, 0]

    def mixer_grads_ready(g):
        slabs = [g["w_main"].reshape(N_DEV, D_MODEL // N_DEV, Z_MAIN), g["w_f"].reshape(N_DEV, D_MODEL // N_DEV, F_PAD)]
        theirs = _sibling_swap(slabs, name="scatter_w_in_swap")
        chip_sums = [_pair_sum(slabs[0], theirs[0], name="scatter_w_in_pair_sum", tn=Z_MAIN // 4),
                     _pair_sum(slabs[1], theirs[1], name="scatter_w_in_forget_pair_sum", tn=F_PAD)]
        pending["w_in"] = _exchange_start(chip_sums, True, name="scatter_w_in_start", chips_only=True)
        return pending["w_in"][1][0, 0]

    sq_err, grad_x, g = _local_step(
        x[0], loss_target[0], w_main, w_f, b_forget, conv_b, g_pre_mix + late_tok[0, 0], g_post_mix, g_pre_ffn,
        g_post_ffn, late_weights, ffn_grads_ready, proj_grads_ready, mixer_grads_ready)

    tiles = dict(w_in=256, w_o_fox=512, w_o_dil=512, w_out=128, w_up=256, w_down=176, conv_w=3)
    adam = lambda n, p: _adamw(p, w[n][0], m[n][0], v[n][0], name=f"adamw_{n}", tm=tiles[n])
    res = {}
    for key, group in (("ffn", ("w_down", "w_up", "conv_w")), ("proj", ("w_o_fox", "w_o_dil", "w_out"))):
        landed = _exchange_wait(pending[key][0], grad_x, name=f"scatter_{key}_wait")
        res.update({n: adam(n, p) for n, p in zip(group, landed)})
    done = res["w_up"][3]
    main_parts, f_parts = _exchange_wait(pending["w_in"][0], done, name="scatter_w_in_wait")
    slab_main = _sum_parts(main_parts, name="scatter_w_in_sum", tn=Z_MAIN // 4)
    slab_f = _sum_parts(f_parts, name="scatter_w_in_forget_sum", tn=F_PAD)
    slab = jnp.concatenate([slab_main[:, :f_lo], slab_f[:, :N_HEADS], slab_main[:, f_lo:]], axis=1)
    last = _exchange([col_slots(slab).astype(BF16)] + [g[n] for n in SMALL] + [sq_err],
                     [True] + [False] * (len(SMALL) + 1), name="scatter_w_in_rows_gather_small")
    rows, small_parts = last[0], last[1:]
    res["w_in"] = adam("w_in", rows.reshape(1, D_MODEL, rows.shape[-1]))
    small, loss = _adamw_small(small_parts[:-1], *[[t[n] for n in SMALL] for t in (w, m, v)], small_parts[-1])
    small = dict(zip(SMALL, small))
    out = [[(res[n][k][None] if n in sharded else small[n][k]) for n in names] for k in range(4)]
    return (loss, grad_x[None], *out[0], *out[1], *out[2], *out[3])
```

```python
import functools
import math

import jax
import jax.numpy as jnp
import numpy as np
from jax import lax
from jax.experimental import pallas as pl
from jax.experimental.pallas import tpu as pltpu

F32 = jnp.float32
BF16 = jnp.bfloat16

SEQ = 4096
D_MODEL = 1024
N_HEADS = 8
HEAD_DIM = 64
ATT_W = N_HEADS * HEAD_DIM
D_FF = 2816
Z_MAIN = 5120
F_PAD = 128
ROPE_DIM = 16
ROPE_THETA = 500000.0
RMS_EPS = 1e-6
NEG_INF = -1e30
SCALE = 1.0 / math.sqrt(HEAD_DIM)
DIL_PATTERNS = ((128, 1), (512, 4), (2048, 16))
DIL_BLK = 128
N_DEV = 8

ADAM_LR = 0.001
ADAM_B1 = 0.9
ADAM_B2 = 0.999
ADAM_EPS = 1e-08
ADAM_WD = 0.01
ADAM_STEP = 10

LANE = 128
SUBLANE = 8
VMEM_LIMIT = 56 * 1024 * 1024
MESH_ID = pl.DeviceIdType.MESH
ANY = pl.BlockSpec(memory_space=pl.ANY)


def _params(*sem):
    return pltpu.CompilerParams(dimension_semantics=sem, vmem_limit_bytes=VMEM_LIMIT)


def _sds(shape, dtype):
    return jax.ShapeDtypeStruct(shape, dtype)


def _matmul(a, b, *, ta=False, tb=False, out_dtype, tm, tn, tk, name, b_k_off=0):
    if ta:
        kk, m = a.shape
    else:
        m, kk = a.shape
    n = b.shape[0] if tb else b.shape[1]
    tm, tn, tk = min(tm, m), min(tn, n), min(tk, kk)
    assert (b.shape[1] if tb else b.shape[0]) >= b_k_off * tk + kk
    assert m % tm == 0 and n % tn == 0 and kk % tk == 0, (name, m, n, kk, tm, tn, tk)
    nk = kk // tk
    dims = (((0 if ta else 1,), (1 if tb else 0,)), ((), ()))

    def body(a_ref, b_ref, o_ref, *scratch):
        p = lax.dot_general(a_ref[...].astype(BF16), b_ref[...].astype(BF16), dims,
                            preferred_element_type=F32)
        if nk == 1:
            o_ref[...] = p.astype(o_ref.dtype)
        else:
            acc = scratch[0]
            k = pl.program_id(2)

            @pl.when(k == 0)
            def _():
                acc[...] = p

            @pl.when(k > 0)
            def _():
                acc[...] += p

            @pl.when(k == nk - 1)
            def _():
                o_ref[...] = acc[...].astype(o_ref.dtype)

    a_spec = (pl.BlockSpec((tk, tm), lambda i, j, k: (k, i)) if ta
              else pl.BlockSpec((tm, tk), lambda i, j, k: (i, k)))
    b_spec = (pl.BlockSpec((tn, tk), lambda i, j, k: (j, k + b_k_off)) if tb
              else pl.BlockSpec((tk, tn), lambda i, j, k: (k + b_k_off, j)))
    return pl.pallas_call(
        body, name=name, grid=(m // tm, n // tn, nk),
        in_specs=[a_spec, b_spec],
        out_specs=pl.BlockSpec((tm, tn), lambda i, j, k: (i, j)),
        out_shape=_sds((m, n), out_dtype),
        scratch_shapes=[pltpu.VMEM((tm, tn), F32)] if nk > 1 else [],
        compiler_params=_params("parallel", "parallel", "arbitrary"),
    )(a, b)


def _rms_fwd(x, g, *, name, tm=512):
    def body(x_ref, g_ref, h_ref):
        xv = x_ref[...]
        r = lax.rsqrt(jnp.mean(xv * xv, axis=-1, keepdims=True) + RMS_EPS)
        h_ref[...] = (xv * r * g_ref[...]).astype(h_ref.dtype)

    return pl.pallas_call(
        body, name=name, grid=(SEQ // tm,),
        in_specs=[pl.BlockSpec((tm, D_MODEL), lambda i: (i, 0)), pl.BlockSpec((1, D_MODEL), lambda i: (0, 0))],
        out_specs=pl.BlockSpec((tm, D_MODEL), lambda i: (i, 0)),
        out_shape=_sds((SEQ, D_MODEL), BF16),
        compiler_params=_params("parallel"),
    )(x, g)


def _rms_bwd(dh_parts, xin, g, dres, *, out_dtype, name, tm=512):
    n_parts = len(dh_parts)
    has_res = dres is not None

    def body(*refs):
        parts = refs[:n_parts]
        x_ref, g_ref = refs[n_parts], refs[n_parts + 1]
        res_ref = refs[n_parts + 2] if has_res else None
        o_ref, gg_ref = refs[-2], refs[-1]
        dh = parts[0][...].astype(F32)
        for p in parts[1:]:
            dh = dh + p[...].astype(F32)
        xv = x_ref[...]
        r = lax.rsqrt(jnp.mean(xv * xv, axis=-1, keepdims=True) + RMS_EPS)
        xn = xv * r

        @pl.when(pl.program_id(0) == 0)
        def _():
            gg_ref[...] = jnp.zeros_like(gg_ref)

        gg_ref[...] += jnp.sum(dh * xn, axis=0, keepdims=True)
        dxn = dh * g_ref[...]
        dx = r * (dxn - xn * jnp.mean(dxn * xn, axis=-1, keepdims=True))
        if has_res:
            dx = dx + res_ref[...]
        o_ref[...] = dx.astype(o_ref.dtype)

    row = pl.BlockSpec((tm, D_MODEL), lambda i: (i, 0))
    vec = pl.BlockSpec((1, D_MODEL), lambda i: (0, 0))
    args = list(dh_parts) + [xin, g] + ([dres] if has_res else [])
    return pl.pallas_call(
        body, name=name, grid=(SEQ // tm,),
        in_specs=[row] * n_parts + [row, vec] + ([row] if has_res else []),
        out_specs=[row, vec],
        out_shape=[_sds((SEQ, D_MODEL), out_dtype), _sds((1, D_MODEL), F32)],
        compiler_params=_params("arbitrary"),
    )(*args)


def _rms_pair_bwd(dh_parts, x2, g_pre, dres, y1, g_post, *, tm=512):
    n_parts = len(dh_parts)

    def norm_bwd(dh, xin, g_ref, gg_ref):
        r = lax.rsqrt(jnp.mean(xin * xin, axis=-1, keepdims=True) + RMS_EPS)
        xn = xin * r
        gg_ref[...] += jnp.sum(dh * xn, axis=0, keepdims=True)
        dxn = dh * g_ref[...]
        return r * (dxn - xn * jnp.mean(dxn * xn, axis=-1, keepdims=True))

    def body(*refs):
        parts = refs[:n_parts]
        x2_ref, gpre_ref, res_ref, y1_ref, gpost_ref, dx2_ref, dy1_ref, ggpre_ref, ggpost_ref = refs[n_parts:]

        @pl.when(pl.program_id(0) == 0)
        def _():
            ggpre_ref[...] = jnp.zeros_like(ggpre_ref)
            ggpost_ref[...] = jnp.zeros_like(ggpost_ref)

        dh = parts[0][...].astype(F32)
        for p in parts[1:]:
            dh = dh + p[...].astype(F32)
        dx2 = res_ref[...] + norm_bwd(dh, x2_ref[...], gpre_ref, ggpre_ref)
        dx2_ref[...] = dx2
        dy1_ref[...] = norm_bwd(dx2, y1_ref[...], gpost_ref, ggpost_ref).astype(dy1_ref.dtype)

    row = pl.BlockSpec((tm, D_MODEL), lambda i: (i, 0))
    vec = pl.BlockSpec((1, D_MODEL), lambda i: (0, 0))
    return pl.pallas_call(
        body, name="rms_pair_bwd", grid=(SEQ // tm,),
        in_specs=[row] * n_parts + [row, vec, row, row, vec],
        out_specs=[row, row, vec, vec],
        out_shape=[_sds((SEQ, D_MODEL), F32), _sds((SEQ, D_MODEL), BF16), _sds((1, D_MODEL), F32),
                   _sds((1, D_MODEL), F32)],
        compiler_params=_params("arbitrary"),
    )(*dh_parts, x2, g_pre, dres, y1, g_post)


SCAN_BLK = 512


def _split_dot(v, tri):
    hi = v.astype(BF16)
    r1 = v - hi.astype(F32)
    mid = r1.astype(BF16)
    lo = (r1 - mid.astype(F32)).astype(BF16)
    dot = functools.partial(jnp.dot, preferred_element_type=F32)
    return dot(hi, tri) + dot(mid, tri) + dot(lo, tri)


def _fox_prep(fa_t, b_col):
    nblk = SEQ // SCAN_BLK

    def body(fa_ref, b_ref, f_ref, sg_ref):
        row = lax.broadcasted_iota(jnp.int32, (SCAN_BLK, SCAN_BLK), 0)
        col = lax.broadcasted_iota(jnp.int32, (SCAN_BLK, SCAN_BLK), 1)
        upper = (row <= col).astype(BF16)
        carry = jnp.zeros((N_HEADS, 1), F32)
        for blk in range(nblk):
            sl = pl.ds(blk * SCAN_BLK, SCAN_BLK)
            xx = fa_ref[:, sl] + b_ref[...]
            e = jnp.exp(-jnp.abs(xx))
            logf = jnp.minimum(xx, 0.0) - jnp.log(1.0 + e)
            sg_ref[:, sl] = jnp.where(xx >= 0.0, e, 1.0) / (1.0 + e)
            c = _split_dot(logf, upper) + carry
            f_ref[:, sl] = c
            carry = c[:, SCAN_BLK - 1:SCAN_BLK]

    return pl.pallas_call(
        body, name="fox_prep",
        out_shape=[_sds((N_HEADS, SEQ), F32), _sds((N_HEADS, SEQ), F32)],
        compiler_params=pltpu.CompilerParams(vmem_limit_bytes=VMEM_LIMIT),
    )(fa_t, b_col)


def _fox_post_bwd(df_t, sg_t):
    nblk = SEQ // SCAN_BLK

    def body(df_ref, sg_ref, dfa_ref, gb_ref):
        row = lax.broadcasted_iota(jnp.int32, (SCAN_BLK, SCAN_BLK), 0)
        col = lax.broadcasted_iota(jnp.int32, (SCAN_BLK, SCAN_BLK), 1)
        lower = (row >= col).astype(BF16)
        carry = jnp.zeros((N_HEADS, 1), F32)
        gb = jnp.zeros((N_HEADS, 1), F32)
        for blk in reversed(range(nblk)):
            sl = pl.ds(blk * SCAN_BLK, SCAN_BLK)
            c = _split_dot(df_ref[:, sl], lower) + carry
            carry = c[:, 0:1]
            dfa = c * sg_ref[:, sl]
            dfa_ref[:, sl] = dfa
            gb = gb + jnp.sum(dfa, axis=1, keepdims=True)
        gb_ref[...] = gb

    return pl.pallas_call(
        body, name="fox_post_bwd",
        out_shape=[_sds((N_HEADS, SEQ), F32), _sds((N_HEADS, 1), F32)],
        compiler_params=pltpu.CompilerParams(vmem_limit_bytes=VMEM_LIMIT),
    )(df_t, sg_t)


FOX_T = 512
NT_DIMS = (((1,), (1,)), ((), ()))
TN_DIMS = (((0,), (0,)), ((), ()))


def _head(ref_or_val, h):
    return ref_or_val[:, h * HEAD_DIM:(h + 1) * HEAD_DIM]


def _split3(v):
    hi = v.astype(BF16).astype(F32)
    r1 = v - hi
    mid = r1.astype(BF16).astype(F32)
    return hi, mid, (r1 - mid).astype(BF16).astype(F32)


ONE_LANE = 3 * N_HEADS


def _pack_terms(v, with_one):
    hi, mid, lo = _split3(v)
    t = hi + pltpu.roll(mid, N_HEADS, 1) + pltpu.roll(lo, 2 * N_HEADS, 1)
    if with_one:
        t = t + (lax.broadcasted_iota(jnp.int32, v.shape, 1) == ONE_LANE).astype(F32)
    return t.astype(BF16)


def _aux_matrices():
    to_q = np.zeros((LANE, N_HEADS * 2 * HEAD_DIM), np.float32)
    to_k = np.zeros_like(to_q)
    for h in range(N_HEADS):
        base = h * 2 * HEAD_DIM + HEAD_DIM
        for s in range(3):
            to_q[s * N_HEADS + h, base + s] = 1.0
            to_q[ONE_LANE, base + 3 + s] = 1.0
            to_k[ONE_LANE, base + s] = 1.0
            to_k[s * N_HEADS + h, base + 3 + s] = -1.0
    return jnp.asarray(to_q, BF16), jnp.asarray(to_k, BF16)


def _head_sums():
    total = np.zeros((N_HEADS * HEAD_DIM, LANE), np.float32)
    first = np.zeros_like(total)
    for h in range(N_HEADS):
        total[h * HEAD_DIM:(h + 1) * HEAD_DIM, h] = 1.0
        first[h * HEAD_DIM, h] = 1.0
    return jnp.asarray(total, BF16), jnp.asarray(first, BF16)


SLOT = 2 * HEAD_DIM
N_SPLIT = 3
FOX_FWD_HEADS = 8
FOX_BWD_HEADS = 4


def _slot(ref, h):
    return ref[:, h * SLOT:(h + 1) * SLOT]


def _fox_pack_fwd(zm, f_cols, *, tm=512):
    def body(q_ref, k_ref, v_ref, f_ref, tq_ref, tk_ref, qs_ref, ks_ref, vs_ref):
        ones = jnp.ones((tm, HEAD_DIM), BF16)
        terms = _pack_terms(f_ref[...], True)
        q_aux = jnp.dot(terms, tq_ref[...], preferred_element_type=F32).astype(BF16)
        k_aux = jnp.dot(terms, tk_ref[...], preferred_element_type=F32).astype(BF16)
        for h in range(N_HEADS):
            aux = slice(h * SLOT + HEAD_DIM, (h + 1) * SLOT)
            qs_ref[:, h * SLOT:(h + 1) * SLOT] = jnp.concatenate(
                [(_head(q_ref, h).astype(F32) * SCALE).astype(BF16), q_aux[:, aux]], axis=1)
            ks_ref[:, h * SLOT:(h + 1) * SLOT] = jnp.concatenate([_head(k_ref, h), k_aux[:, aux]], axis=1)
            vs_ref[:, h * SLOT:(h + 1) * SLOT] = jnp.concatenate([_head(v_ref, h), ones], axis=1)

    col = lambda b: pl.BlockSpec((tm, ATT_W), lambda i: (i, b))
    wide = pl.BlockSpec((tm, N_HEADS * SLOT), lambda i: (i, 0))
    const = pl.BlockSpec((LANE, N_HEADS * SLOT), lambda i: (0, 0))
    return pl.pallas_call(
        body, name="fox_pack_fwd", grid=(SEQ // tm,),
        in_specs=[col(0), col(1), col(2), pl.BlockSpec((tm, LANE), lambda i: (i, 0)), const, const],
        out_specs=[wide] * 3, out_shape=[_sds((SEQ, N_HEADS * SLOT), BF16)] * 3,
        compiler_params=_params("parallel"),
    )(zm, zm, zm, f_cols, *_aux_matrices())


def _fox_pack_bwd(zm, f_cols, lse, o, do, *, tm=512):
    def body(q_ref, f_ref, lse_ref, o_ref, do_ref, tq_ref, total_ref, first_ref, qs_ref, ds_ref):
        delta = _split_dot(o_ref[...].astype(F32) * do_ref[...].astype(F32), total_ref[...])
        lse_h = _split_dot(lse_ref[...], first_ref[...])
        q_aux = jnp.dot(_pack_terms(f_ref[...] - lse_h, True), tq_ref[...], preferred_element_type=F32).astype(BF16)
        d_aux = jnp.dot(_pack_terms(-delta, False), tq_ref[...], preferred_element_type=F32).astype(BF16)
        for h in range(N_HEADS):
            aux = slice(h * SLOT + HEAD_DIM, (h + 1) * SLOT)
            qs_ref[:, h * SLOT:(h + 1) * SLOT] = jnp.concatenate(
                [(_head(q_ref, h).astype(F32) * SCALE).astype(BF16), q_aux[:, aux]], axis=1)
            ds_ref[:, h * SLOT:(h + 1) * SLOT] = jnp.concatenate([_head(do_ref, h), d_aux[:, aux]], axis=1)

    row = pl.BlockSpec((tm, ATT_W), lambda i: (i, 0))
    wide = pl.BlockSpec((tm, N_HEADS * SLOT), lambda i: (i, 0))
    const = lambda r, c: pl.BlockSpec((r, c), lambda i: (0, 0))
    return pl.pallas_call(
        body, name="fox_pack_bwd", grid=(SEQ // tm,),
        in_specs=[row, pl.BlockSpec((tm, LANE), lambda i: (i, 0)), row, row, row,
                  const(LANE, N_HEADS * SLOT), const(ATT_W, LANE), const(ATT_W, LANE)],
        out_specs=[wide] * 2, out_shape=[_sds((SEQ, N_HEADS * SLOT), BF16)] * 2,
        compiler_params=_params("parallel"),
    )(zm, f_cols, lse, o, do, _aux_matrices()[0], *_head_sums())


def _causal_pairs(key_major):
    nb = SEQ // FOX_T
    if key_major:
        pairs = [(i, j) for j in range(nb) for i in range(j, nb)]
    else:
        pairs = [(i, j) for i in range(nb) for j in range(i + 1)]
    return (jnp.array([p[0] for p in pairs], jnp.int32), jnp.array([p[1] for p in pairs], jnp.int32), len(pairs))


FOX_HALF = FOX_T // 2
FOX_FULL = ((slice(0, FOX_T), slice(0, FOX_T), None),)
FOX_DIAG = ((slice(0, FOX_HALF), slice(0, FOX_HALF), 0), (slice(FOX_HALF, FOX_T), slice(0, FOX_T), FOX_HALF))


def _causal_piece_mask(q_rows, k_rows, offset):
    shape = (q_rows.stop - q_rows.start, k_rows.stop - k_rows.start)
    row = lax.broadcasted_iota(jnp.int32, shape, 0)
    col = lax.broadcasted_iota(jnp.int32, shape, 1)
    return col <= row + offset


def _fox_fwd(q_slots, k_slots, v_slots):
    i_tab, j_tab, n_pairs = _causal_pairs(False)

    def body(i_tab, j_tab, q_ref, k_ref, v_ref, o_ref, lse_ref, m_s, acc_s):
        t = pl.program_id(1)
        i, j = i_tab[t], j_tab[t]

        @pl.when(j == 0)
        def _():
            m_s[...] = jnp.full_like(m_s, NEG_INF)
            acc_s[...] = jnp.zeros_like(acc_s)

        def step(pieces):
            jobs = [(h, piece) for h in range(FOX_FWD_HEADS) for piece in pieces]
            lanes = lambda h: slice(h * SLOT, (h + 1) * SLOT)
            scores = [lax.dot_general(q_ref[qr, lanes(h)], k_ref[kr, lanes(h)], NT_DIMS, preferred_element_type=F32)
                      for h, (qr, kr, _) in jobs]
            probs, alphas = [], []
            for idx, (h, (qr, kr, offset)) in enumerate(jobs):
                s = scores[idx]
                if offset is not None:
                    s = jnp.where(_causal_piece_mask(qr, kr, offset), s, NEG_INF)
                m_prev = m_s[h, qr, :]
                m_new = jnp.maximum(m_prev, jnp.max(s, axis=-1, keepdims=True))
                probs.append(jnp.exp(s - jnp.tile(m_new, (1, s.shape[1] // LANE))).astype(BF16))
                alphas.append(jnp.exp(m_prev - m_new))
                m_s[h, qr, :] = m_new
            for idx, (h, (qr, kr, _)) in enumerate(jobs):
                acc_s[h, qr, :] = alphas[idx] * acc_s[h, qr, :] + jnp.dot(
                    probs[idx], v_ref[kr, lanes(h)], preferred_element_type=F32)

        @pl.when(j < i)
        def _():
            step(FOX_FULL)

        @pl.when(j == i)
        def _():
            step(FOX_DIAG)
            outs, lses = [], []
            for h in range(FOX_FWD_HEADS):
                acc = acc_s[h]
                l = acc[:, HEAD_DIM:]
                outs.append(acc[:, :HEAD_DIM] / l)
                lses.append(m_s[h][:, :HEAD_DIM] + jnp.log(l))
            o_ref[...] = jnp.concatenate(outs, axis=1).astype(o_ref.dtype)
            lse_ref[...] = jnp.concatenate(lses, axis=1)

    qspec = pl.BlockSpec((FOX_T, FOX_FWD_HEADS * SLOT), lambda p, t, it, jt: (it[t], p))
    kspec = pl.BlockSpec((FOX_T, FOX_FWD_HEADS * SLOT), lambda p, t, it, jt: (jt[t], p))
    ospec = pl.BlockSpec((FOX_T, FOX_FWD_HEADS * HEAD_DIM), lambda p, t, it, jt: (it[t], p))
    return pl.pallas_call(
        body, name="fox_fwd",
        grid_spec=pltpu.PrefetchScalarGridSpec(
            num_scalar_prefetch=2, grid=(N_HEADS // FOX_FWD_HEADS, n_pairs),
            in_specs=[qspec, kspec, kspec], out_specs=[ospec, ospec],
            scratch_shapes=[pltpu.VMEM((FOX_FWD_HEADS, FOX_T, LANE), F32),
                            pltpu.VMEM((FOX_FWD_HEADS, FOX_T, SLOT), F32)]),
        out_shape=[_sds((SEQ, ATT_W), BF16), _sds((SEQ, ATT_W), F32)],
        compiler_params=_params("parallel", "arbitrary"),
    )(i_tab, j_tab, q_slots, k_slots, v_slots)


def _fox_bwd(q_slots, k_slots, v_slots, do_slots):
    i_tab, j_tab, n_pairs = _causal_pairs(True)

    def body(i_tab, j_tab, q_ref, k_ref, v_ref, do_ref, dq_ref, dk_ref, dv_ref):
        t = pl.program_id(1)
        i, j = i_tab[t], j_tab[t]

        @pl.when(t == 0)
        def _():
            dq_ref[...] = jnp.zeros_like(dq_ref)

        @pl.when(i == j)
        def _():
            dk_ref[...] = jnp.zeros_like(dk_ref)
            dv_ref[...] = jnp.zeros_like(dv_ref)

        def step(pieces):
            jobs = [(h, piece) for h in range(FOX_BWD_HEADS) for piece in pieces]
            lanes = lambda h: slice(h * SLOT, (h + 1) * SLOT)
            scores = [lax.dot_general(q_ref[qr, lanes(h)], k_ref[kr, lanes(h)], NT_DIMS, preferred_element_type=F32)
                      for h, (qr, kr, _) in jobs]
            dps = [lax.dot_general(do_ref[qr, lanes(h)], v_ref[kr, lanes(h)], NT_DIMS, preferred_element_type=F32)
                   for h, (qr, kr, _) in jobs]
            ps, dss = [], []
            for idx, (h, (qr, kr, offset)) in enumerate(jobs):
                p = jnp.exp(scores[idx])
                if offset is not None:
                    p = jnp.where(_causal_piece_mask(qr, kr, offset), p, 0.0)
                ps.append(p.astype(BF16))
                dss.append((p * dps[idx]).astype(BF16))
            for idx, (h, (qr, kr, _)) in enumerate(jobs):
                rows = pl.ds(pl.multiple_of(i * FOX_T + qr.start, FOX_HALF), qr.stop - qr.start)
                dv_ref[kr, lanes(h)] += lax.dot_general(ps[idx], do_ref[qr, lanes(h)], TN_DIMS,
                                                        preferred_element_type=F32)
                dk_ref[kr, lanes(h)] += lax.dot_general(dss[idx], q_ref[qr, lanes(h)], TN_DIMS,
                                                        preferred_element_type=F32)
                dq_ref[rows, lanes(h)] += jnp.dot(dss[idx], k_ref[kr, lanes(h)], preferred_element_type=F32)

        @pl.when(i > j)
        def _():
            step(FOX_FULL)

        @pl.when(i == j)
        def _():
            step(FOX_DIAG)

    qspec = pl.BlockSpec((FOX_T, FOX_BWD_HEADS * SLOT), lambda p, t, it, jt: (it[t], p))
    kspec = pl.BlockSpec((FOX_T, FOX_BWD_HEADS * SLOT), lambda p, t, it, jt: (jt[t], p))
    return pl.pallas_call(
        body, name="fox_bwd",
        grid_spec=pltpu.PrefetchScalarGridSpec(
            num_scalar_prefetch=2, grid=(N_HEADS // FOX_BWD_HEADS, n_pairs),
            in_specs=[qspec, kspec, kspec, qspec],
            out_specs=[pl.BlockSpec((SEQ, FOX_BWD_HEADS * SLOT), lambda p, t, it, jt: (0, p)), kspec, kspec]),
        out_shape=[_sds((SEQ, N_HEADS * SLOT), F32)] * 3,
        compiler_params=_params("arbitrary", "arbitrary"),
    )(i_tab, j_tab, q_slots, k_slots, v_slots, do_slots)


def _fox_unpack(dq_slots, dk_slots, dv_slots, dz, *, tm=512):
    def body(dq_ref, dk_ref, dv_ref, dz_in, o_ref, df_ref):
        lane = lax.broadcasted_iota(jnp.int32, (tm, LANE), 1)
        df = jnp.zeros((tm, LANE), F32)
        for h in range(N_HEADS):
            lo = h * SLOT
            for part, (ref, mult) in enumerate(((dq_ref, SCALE), (dk_ref, 1.0), (dv_ref, 1.0))):
                o_ref[:, part * ATT_W + h * HEAD_DIM:part * ATT_W + (h + 1) * HEAD_DIM] = (
                    ref[:, lo:lo + HEAD_DIM] * mult).astype(o_ref.dtype)
            rows = dq_ref[:, lo + HEAD_DIM:lo + HEAD_DIM + 1]
            cols = dk_ref[:, lo + HEAD_DIM + N_SPLIT:lo + HEAD_DIM + N_SPLIT + 1]
            df = jnp.where(lane == h, rows - cols, df)
        df_ref[...] = df

    wide = pl.BlockSpec((tm, N_HEADS * SLOT), lambda i: (i, 0))
    return pl.pallas_call(
        body, name="fox_unpack", grid=(SEQ // tm,), in_specs=[wide] * 3 + [ANY],
        out_specs=[pl.BlockSpec((tm, 3 * ATT_W), lambda i: (i, 0)), pl.BlockSpec((tm, LANE), lambda i: (i, 0))],
        out_shape=[_sds((SEQ, Z_MAIN), BF16), _sds((SEQ, LANE), F32)],
        input_output_aliases={3: 0},
        compiler_params=_params("parallel"),
    )(dq_slots, dk_slots, dv_slots, dz)


def _dil_bwd_prep(o, do, lse, *, tm=512):
    dilations = [d for _, d in DIL_PATTERNS]
    o_chunks = ATT_W // LANE

    def body(o_ref, do_ref, lse_ref, *rest):
        outs, (do_scr, lse_scr, dl_scr) = rest[:-3], rest[-3:]
        dov = do_ref[...].astype(F32)
        prod = o_ref[...].astype(F32) * dov
        lane = lax.broadcasted_iota(jnp.int32, (tm, LANE), 1)
        delta = jnp.zeros((tm, LANE), F32)
        for h in range(N_HEADS):
            delta = jnp.where(lane == h, jnp.sum(_head(prod, h), axis=1, keepdims=True), delta)
        for ch in range(o_chunks):
            do_scr[ch] = dov[:, ch * LANE:(ch + 1) * LANE]
        lse_scr[0] = lse_ref[...]
        dl_scr[0] = delta
        for k, d in enumerate(dilations):
            for scr, out in zip((do_scr, lse_scr, dl_scr), outs[3 * k:3 * k + 3]):
                _slabs_from_rows(scr, out, d)

    row = pl.BlockSpec((tm, ATT_W), lambda i: (i, 0))
    view = lambda d, w: pl.BlockSpec((tm // d, d * w), lambda i: (i, 0))
    outs = pl.pallas_call(
        body, name="dil_bwd_prep", grid=(SEQ // tm,),
        in_specs=[row, row, pl.BlockSpec((tm, LANE), lambda i: (i, 0))],
        out_specs=[view(d, w) for d in dilations for w in (ATT_W, LANE, LANE)],
        out_shape=[_sds((SEQ // d, d * w), t) for d in dilations for w, t in ((ATT_W, BF16), (LANE, F32), (LANE, F32))],
        scratch_shapes=[pltpu.VMEM((o_chunks, tm, LANE), F32), pltpu.VMEM((1, tm, LANE), F32),
                        pltpu.VMEM((1, tm, LANE), F32)],
        compiler_params=_params("parallel"),
    )(o, do, lse)
    return [outs[3 * k:3 * k + 3] for k in range(len(dilations))]


def _rope_tables():
    half = ROPE_DIM // 2
    inv_freq = np.float32(ROPE_THETA) ** (-np.arange(half, dtype=np.float32) * np.float32(2.0) / np.float32(ROPE_DIM))
    ang = np.arange(SEQ, dtype=np.float32)[:, None] * inv_freq.astype(np.float32)[None, :]
    cos, sin = jnp.asarray(np.cos(ang).astype(np.float32)), jnp.asarray(np.sin(ang).astype(np.float32))
    ones = jnp.ones((SEQ, HEAD_DIM - ROPE_DIM), F32)
    zeros = jnp.zeros((SEQ, HEAD_DIM - ROPE_DIM), F32)
    zh = jnp.zeros((SEQ, half), F32)
    c_tab = jnp.concatenate([cos, cos, ones], axis=1)
    a_tab = jnp.concatenate([-sin, zh, zeros], axis=1)
    b_tab = jnp.concatenate([zh, sin, zeros], axis=1)
    two = lambda t: jnp.concatenate([t, t], axis=1)
    return two(c_tab), two(a_tab), two(b_tab)


def _rotate(x, c_tab, a_tab, b_tab):
    return x * c_tab + pltpu.roll(x, LANE - ROPE_DIM // 2, 1) * a_tab + pltpu.roll(x, ROPE_DIM // 2, 1) * b_tab


def _rope_fwd(zm, tabs, *, tm=512):
    width = 3 * ATT_W
    dilations = [d for _, d in DIL_PATTERNS]

    def body(q_ref, k_ref, v_ref, c_ref, a_ref, b_ref, *rest):
        outs, scr = rest[:-1], rest[-1]
        per_part = ATT_W // LANE
        for part, (x_ref, mult) in enumerate(((q_ref, SCALE), (k_ref, 1.0))):
            for cc in range(per_part):
                sl = slice(cc * LANE, (cc + 1) * LANE)
                scr[part * per_part + cc] = _rotate(x_ref[:, sl].astype(F32), c_ref[...], a_ref[...], b_ref[...]) * mult
        for cc in range(per_part):
            scr[2 * per_part + cc] = v_ref[:, cc * LANE:(cc + 1) * LANE].astype(F32)
        for o_ref, d in zip(outs, dilations):
            for r in range(d):
                for ch in range(width // LANE):
                    o_ref[:, r * width + ch * LANE:r * width + (ch + 1) * LANE] = (
                        scr.at[ch][pl.ds(r, tm // d, stride=d), :].astype(o_ref.dtype))

    tab = pl.BlockSpec((tm, LANE), lambda i: (i, 0))
    col = lambda b: pl.BlockSpec((tm, ATT_W), lambda i: (i, b))
    return pl.pallas_call(
        body, name="rope_fwd", grid=(SEQ // tm,),
        in_specs=[col(3), col(4), col(5), tab, tab, tab],
        out_specs=[pl.BlockSpec((tm // d, d * width), lambda i: (i, 0)) for d in dilations],
        out_shape=[_sds((SEQ // d, d * width), BF16) for d in dilations],
        scratch_shapes=[pltpu.VMEM((width // LANE, tm, LANE), F32)],
        compiler_params=_params("parallel"),
    )(zm, zm, zm, *tabs)


def _dil_grad_combine(dqs, dks, dvs, tabs, dz, *, tm=256):
    dilations = [d for _, d in DIL_PATTERNS]
    chunks = ATT_W // LANE

    def body(*refs):
        groups = (refs[0:3], refs[3:6], refs[6:9])
        c_ref, a_ref, b_ref, _, o_ref, scr = refs[9:]

        def total(part, cc):
            acc = None
            for g, (ref, d) in enumerate(zip(groups[part], dilations)):
                term = ref[:, cc * LANE:(cc + 1) * LANE].astype(F32) if d == 1 else scr[part, g, cc]
                acc = term if acc is None else acc + term
            return acc

        for part in range(3):
            for g, (ref, d) in enumerate(zip(groups[part], dilations)):
                if d > 1:
                    _rows_from_slabs(ref, scr.at[part, g], d)
        for cc in range(chunks):
            for part in range(2):
                o_ref[:, part * ATT_W + cc * LANE:part * ATT_W + (cc + 1) * LANE] = _rotate(
                    total(part, cc), c_ref[...], -a_ref[...], -b_ref[...]).astype(o_ref.dtype)
            o_ref[:, 2 * ATT_W + cc * LANE:2 * ATT_W + (cc + 1) * LANE] = total(2, cc).astype(o_ref.dtype)

    view = lambda d: pl.BlockSpec((tm // d, d * ATT_W), lambda i: (i, 0))
    tab = pl.BlockSpec((tm, LANE), lambda i: (i, 0))
    return pl.pallas_call(
        body, name="dil_grad_combine", grid=(SEQ // tm,),
        in_specs=[view(d) for d in dilations] * 3 + [tab] * 3 + [ANY],
        out_specs=pl.BlockSpec((tm, 3 * ATT_W), lambda i: (i, 1)),
        out_shape=_sds((SEQ, Z_MAIN), BF16),
        input_output_aliases={12: 0},
        scratch_shapes=[pltpu.VMEM((3, len(dilations), chunks, tm, LANE), F32)],
        compiler_params=_params("parallel"),
    )(*dqs, *dks, *dvs, *tabs, dz)


def _dil_valid(n):
    qi = lax.broadcasted_iota(jnp.int32, (DIL_BLK, 2 * DIL_BLK), 0)
    ki = lax.broadcasted_iota(jnp.int32, (DIL_BLK, 2 * DIL_BLK), 1)
    dist = qi + DIL_BLK - ki
    return (dist >= 0) & (dist <= DIL_BLK) & ((n > 0) | (ki >= DIL_BLK))


def _dil_fwd(qkv_v, d):
    length = SEQ // d
    nb = length // DIL_BLK

    def body(q_ref, kp_ref, kc_ref, vp_ref, vc_ref, o_ref, lse_ref):
        m_step = pl.program_id(1)
        lane = lax.broadcasted_iota(jnp.int32, (DIL_BLK, LANE), 1)
        jobs = [(sub, h) for sub in range(2) for h in range(N_HEADS)]
        rows = lambda sub: slice(sub * DIL_BLK, (sub + 1) * DIL_BLK)
        cols = lambda h: slice(h * HEAD_DIM, (h + 1) * HEAD_DIM)

        def keys(prev_ref, cur_ref, sub, h):
            before = prev_ref[:, cols(h)] if sub == 0 else cur_ref[rows(0), cols(h)]
            return jnp.concatenate([before, cur_ref[rows(sub), cols(h)]], axis=0)

        scores = [lax.dot_general(q_ref[rows(sub), cols(h)], keys(kp_ref, kc_ref, sub, h), NT_DIMS,
                                  preferred_element_type=F32) for sub, h in jobs]
        ok = [_dil_valid(m_step), _dil_valid(1)]
        probs, inv_l, lse_all = [], [], [jnp.zeros((DIL_BLK, LANE), F32)] * 2
        for idx, (sub, h) in enumerate(jobs):
            s = jnp.where(ok[sub], scores[idx], NEG_INF)
            m = jnp.max(s, axis=-1, keepdims=True)
            p = jnp.exp(s - m)
            l = jnp.sum(p, axis=-1, keepdims=True)
            probs.append(p.astype(BF16))
            inv_l.append(1.0 / l)
            lse_all[sub] = jnp.where(lane == h, m + jnp.log(l), lse_all[sub])
        outs = [jnp.dot(probs[idx], keys(vp_ref, vc_ref, sub, h), preferred_element_type=F32) * inv_l[idx]
                for idx, (sub, h) in enumerate(jobs)]
        for sub in range(2):
            o_ref[rows(sub), :] = jnp.concatenate(outs[sub * N_HEADS:(sub + 1) * N_HEADS], axis=1).astype(o_ref.dtype)
            lse_ref[rows(sub), :] = lse_all[sub]

    pair = lambda f: pl.BlockSpec((2 * DIL_BLK, ATT_W), f)
    one = lambda f: pl.BlockSpec((DIL_BLK, ATT_W), f)
    before = lambda m: jnp.maximum(2 * m - 1, 0)
    o, lse = pl.pallas_call(
        body, name=f"dil_fwd_d{d}", grid=(d, nb // 2),
        in_specs=[pair(lambda r, m: (m, 3 * r)),
                  one(lambda r, m: (before(m), 3 * r + 1)), pair(lambda r, m: (m, 3 * r + 1)),
                  one(lambda r, m: (before(m), 3 * r + 2)), pair(lambda r, m: (m, 3 * r + 2))],
        out_specs=[pair(lambda r, m: (m, r)), pl.BlockSpec((2 * DIL_BLK, LANE), lambda r, m: (m, r))],
        out_shape=[_sds((length, d * ATT_W), BF16), _sds((length, d * LANE), F32)],
        compiler_params=_params("parallel", "arbitrary"),
    )(qkv_v, qkv_v, qkv_v, qkv_v, qkv_v)
    return o, lse


def _rows_from_slabs(view_ref, scr, d):
    chunks, rows = scr.shape[0], scr.shape[1]
    for r in range(d):
        for ch in range(chunks):
            lo = (r * chunks + ch) * LANE
            scr.at[ch][pl.ds(r, rows // d, stride=d), :] = view_ref[:, lo:lo + LANE].astype(F32)


def _slabs_from_rows(scr, view_ref, d):
    chunks, rows = scr.shape[0], scr.shape[1]
    for r in range(d):
        for ch in range(chunks):
            lo = (r * chunks + ch) * LANE
            view_ref[:, lo:lo + LANE] = scr.at[ch][pl.ds(r, rows // d, stride=d), :].astype(view_ref.dtype)


def _dil_merge(os_, lses, *, tm=512):
    dilations = [d for _, d in DIL_PATTERNS]
    o_chunks = ATT_W // LANE

    def body(o0, o1, o2, l0, l1, l2, y_ref, lse_ref, o_scr, l_scr):
        os_nat, ls = [], []
        for g, (o_ref, l_ref, d) in enumerate(zip((o0, o1, o2), (l0, l1, l2), dilations)):
            if d == 1:
                os_nat.append(o_ref[...].astype(F32))
                ls.append(l_ref[...])
            else:
                _rows_from_slabs(o_ref, o_scr.at[g], d)
                _rows_from_slabs(l_ref, l_scr.at[g], d)
                os_nat.append(jnp.concatenate([o_scr[g, ch] for ch in range(o_chunks)], axis=1))
                ls.append(l_scr[g, 0])
        m = jnp.maximum(jnp.maximum(ls[0], ls[1]), ls[2])
        es = [jnp.exp(l - m) for l in ls]
        tot = es[0] + es[1] + es[2]
        lse_ref[...] = m + jnp.log(tot)
        alphas = [e / tot for e in es]
        outs = []
        for h in range(N_HEADS):
            acc = None
            for g in range(3):
                term = alphas[g][:, h:h + 1] * _head(os_nat[g], h)
                acc = term if acc is None else acc + term
            outs.append(acc)
        y_ref[...] = jnp.concatenate(outs, axis=1).astype(y_ref.dtype)

    row = pl.BlockSpec((tm, ATT_W), lambda i: (i, 0))
    vec = pl.BlockSpec((tm, LANE), lambda i: (i, 0))
    view = lambda d, w: pl.BlockSpec((tm // d, d * w), lambda i: (i, 0))
    return pl.pallas_call(
        body, name="dil_merge", grid=(SEQ // tm,),
        in_specs=[view(d, ATT_W) for d in dilations] + [view(d, LANE) for d in dilations], out_specs=[row, vec],
        out_shape=[_sds((SEQ, ATT_W), BF16), _sds((SEQ, LANE), F32)],
        scratch_shapes=[pltpu.VMEM((3, o_chunks, tm, LANE), F32), pltpu.VMEM((3, 1, tm, LANE), F32)],
        compiler_params=_params("parallel"),
    )(*os_, *lses)


def _dil_bwd(qkv_v, do_v, lse_v, dl_v, d):
    length = SEQ // d
    nb = length // DIL_BLK
    n_steps = nb // 2

    def body(q_ref, kp_ref, kc_ref, vp_ref, vc_ref, lse_ref, dl_ref, do_ref, dq_ref, dk_ref, dv_ref, dk_s, dv_s):
        m_step = pl.program_id(1)

        @pl.when(m_step == 0)
        def _():
            dk_s[...] = jnp.zeros_like(dk_s)
            dv_s[...] = jnp.zeros_like(dv_s)

        jobs = [(sub, h) for sub in range(2) for h in range(N_HEADS)]
        rows = lambda sub: slice(sub * DIL_BLK, (sub + 1) * DIL_BLK)
        cols = lambda h: slice(h * HEAD_DIM, (h + 1) * HEAD_DIM)

        def keys(prev_ref, cur_ref, sub, h):
            before = prev_ref[:, cols(h)] if sub == 0 else cur_ref[rows(0), cols(h)]
            return jnp.concatenate([before, cur_ref[rows(sub), cols(h)]], axis=0)

        kks = [keys(kp_ref, kc_ref, sub, h) for sub, h in jobs]
        scores = [lax.dot_general(q_ref[rows(sub), cols(h)], kks[idx], NT_DIMS, preferred_element_type=F32)
                  for idx, (sub, h) in enumerate(jobs)]
        dps = [lax.dot_general(do_ref[rows(sub), cols(h)], keys(vp_ref, vc_ref, sub, h), NT_DIMS,
                               preferred_element_type=F32) for sub, h in jobs]
        ok = [_dil_valid(m_step), _dil_valid(1)]
        ps, dss = [], []
        for idx, (sub, h) in enumerate(jobs):
            p = jnp.where(ok[sub], jnp.exp(scores[idx] - lse_ref[rows(sub), h:h + 1]), 0.0)
            ps.append(p.astype(BF16))
            dss.append((p * (dps[idx] - dl_ref[rows(sub), h:h + 1])).astype(BF16))
        dqs = [jnp.dot(dss[idx], kks[idx], preferred_element_type=F32) * SCALE for idx in range(len(jobs))]
        dkks = [lax.dot_general(dss[idx], q_ref[rows(sub), cols(h)], TN_DIMS, preferred_element_type=F32)
                for idx, (sub, h) in enumerate(jobs)]
        dvvs = [lax.dot_general(ps[idx], do_ref[rows(sub), cols(h)], TN_DIMS, preferred_element_type=F32)
                for idx, (sub, h) in enumerate(jobs)]
        for sub in range(2):
            dq_ref[rows(sub), :] = jnp.concatenate(dqs[sub * N_HEADS:(sub + 1) * N_HEADS], axis=1).astype(dq_ref.dtype)
        base = m_step * (2 * DIL_BLK)
        blocks = [pl.ds(pl.multiple_of(jnp.maximum(base - DIL_BLK, 0), DIL_BLK), DIL_BLK),
                  pl.ds(pl.multiple_of(base, DIL_BLK), DIL_BLK),
                  pl.ds(pl.multiple_of(base + DIL_BLK, DIL_BLK), DIL_BLK)]
        for acc, parts in ((dk_s, dkks), (dv_s, dvvs)):
            top = lambda sub: jnp.concatenate([parts[sub * N_HEADS + h][:DIL_BLK] for h in range(N_HEADS)], axis=1)
            bottom = lambda sub: jnp.concatenate([parts[sub * N_HEADS + h][DIL_BLK:] for h in range(N_HEADS)], axis=1)
            acc[blocks[0], :] += top(0)
            acc[blocks[1], :] += bottom(0) + top(1)
            acc[blocks[2], :] += bottom(1)

        @pl.when(m_step == n_steps - 1)
        def _():
            dk_ref[...] = dk_s[...].astype(dk_ref.dtype)
            dv_ref[...] = dv_s[...].astype(dv_ref.dtype)

    pair = lambda f: pl.BlockSpec((2 * DIL_BLK, ATT_W), f)
    one = lambda f: pl.BlockSpec((DIL_BLK, ATT_W), f)
    vec = lambda f: pl.BlockSpec((2 * DIL_BLK, LANE), f)
    whole = pl.BlockSpec((length, ATT_W), lambda r, m: (0, r))
    before = lambda m: jnp.maximum(2 * m - 1, 0)
    outs = pl.pallas_call(
        body, name=f"dil_bwd_d{d}", grid=(d, n_steps),
        in_specs=[pair(lambda r, m: (m, 3 * r)),
                  one(lambda r, m: (before(m), 3 * r + 1)), pair(lambda r, m: (m, 3 * r + 1)),
                  one(lambda r, m: (before(m), 3 * r + 2)), pair(lambda r, m: (m, 3 * r + 2)),
                  vec(lambda r, m: (m, r)), vec(lambda r, m: (m, r)), pair(lambda r, m: (m, r))],
        out_specs=[pair(lambda r, m: (m, r)), whole, whole],
        out_shape=[_sds((length, d * ATT_W), BF16)] * 3,
        scratch_shapes=[pltpu.VMEM((length, ATT_W), F32), pltpu.VMEM((length, ATT_W), F32)],
        compiler_params=_params("arbitrary", "arbitrary"),
    )(qkv_v, qkv_v, qkv_v, qkv_v, qkv_v, lse_v, dl_v, do_v)
    return outs


def _sigmoid(x):
    return 1.0 / (1.0 + jnp.exp(-x))


def _mix_fwd(ya, yb, w_oa, w_ob, zm, *, tm=512):
    def body(ya_ref, yb_ref, wa_ref, wb_ref, ga_ref, gb_ref, pa_ref, pb_ref, mix_ref):
        pa = jnp.dot(ya_ref[...], wa_ref[...], preferred_element_type=F32)
        pb = jnp.dot(yb_ref[...], wb_ref[...], preferred_element_type=F32)
        pa_ref[...] = pa.astype(pa_ref.dtype)
        pb_ref[...] = pb.astype(pb_ref.dtype)
        mix_ref[...] = (_sigmoid(ga_ref[...].astype(F32)) * pa + _sigmoid(gb_ref[...].astype(F32)) * pb
                        ).astype(mix_ref.dtype)

    row = pl.BlockSpec((tm, ATT_W), lambda i: (i, 0))
    wsp = pl.BlockSpec((ATT_W, D_MODEL), lambda i: (0, 0))
    wide = pl.BlockSpec((tm, D_MODEL), lambda i: (i, 0))
    return pl.pallas_call(
        body, name="mix_fwd", grid=(SEQ // tm,),
        in_specs=[row, row, wsp, wsp, pl.BlockSpec((tm, D_MODEL), lambda i: (i, 3)),
                  pl.BlockSpec((tm, D_MODEL), lambda i: (i, 4))],
        out_specs=[wide] * 3, out_shape=[_sds((SEQ, D_MODEL), BF16)] * 3,
        compiler_params=_params("parallel"),
    )(ya, yb, w_oa, w_ob, zm, zm)


def _gate_bwd(dmix, zm, p, gate_block, dz, *, name, tm=512):
    def body(dm_ref, g_ref, p_ref, *rest):
        dp_ref, dz_ref = rest[-2], rest[-1]
        dm = dm_ref[...].astype(F32)
        s = _sigmoid(g_ref[...].astype(F32))
        dp_ref[...] = (dm * s).astype(dp_ref.dtype)
        dz_ref[...] = (dm * p_ref[...].astype(F32) * s * (1.0 - s)).astype(dz_ref.dtype)

    wide = pl.BlockSpec((tm, D_MODEL), lambda i: (i, 0))
    gate = pl.BlockSpec((tm, D_MODEL), lambda i: (i, gate_block))
    extra = [] if dz is None else [dz]
    return pl.pallas_call(
        body, name=name, grid=(SEQ // tm,),
        in_specs=[wide, gate, wide] + [ANY] * len(extra),
        out_specs=[wide, gate],
        out_shape=[_sds((SEQ, D_MODEL), BF16), _sds((SEQ, Z_MAIN), BF16)],
        input_output_aliases={3: 1} if extra else {},
        compiler_params=_params("parallel"),
    )(dmix, zm, p, *extra)


def _out_fwd(mixed, w_out, x, g_post, g_pre, *, tm=512):
    def body(m_ref, w_ref, x_ref, gp_ref, gn_ref, y_ref, x2_ref, h_ref):
        y = jnp.dot(m_ref[...], w_ref[...], preferred_element_type=F32)
        y_ref[...] = y
        r = lax.rsqrt(jnp.mean(y * y, axis=-1, keepdims=True) + RMS_EPS)
        x2 = x_ref[...] + y * r * gp_ref[...]
        x2_ref[...] = x2
        r2 = lax.rsqrt(jnp.mean(x2 * x2, axis=-1, keepdims=True) + RMS_EPS)
        h_ref[...] = (x2 * r2 * gn_ref[...]).astype(h_ref.dtype)

    row = pl.BlockSpec((tm, D_MODEL), lambda i: (i, 0))
    vec = pl.BlockSpec((1, D_MODEL), lambda i: (0, 0))
    return pl.pallas_call(
        body, name="out_fwd", grid=(SEQ // tm,),
        in_specs=[row, pl.BlockSpec((D_MODEL, D_MODEL), lambda i: (0, 0)), row, vec, vec],
        out_specs=[row] * 3,
        out_shape=[_sds((SEQ, D_MODEL), F32), _sds((SEQ, D_MODEL), F32), _sds((SEQ, D_MODEL), BF16)],
        compiler_params=_params("parallel"),
    )(mixed, w_out, x, g_post, g_pre)


FFN_TM = 2048
FFN_HALF = 256
FFN_TN = 2 * FFN_HALF
FFN_NJ = D_FF // FFN_HALF
FFN_GROUP = 2 * SUBLANE


def _ffn_interleave(t):
    lead = t.shape[:-1]
    return jnp.swapaxes(t.reshape(*lead, 2, FFN_NJ, FFN_HALF), -3, -2).reshape(*lead, 2 * D_FF)


def _ffn_deinterleave(t):
    lead = t.shape[:-1]
    return jnp.swapaxes(t.reshape(*lead, FFN_NJ, 2, FFN_HALF), -3, -2).reshape(*lead, 2 * D_FF)


def _ffn_move_blocks(t, *, interleave, name):
    rows = t.shape[0]
    if interleave:
        src = lambda jb: (0, (jb % 2) * FFN_NJ + jb // 2)
    else:
        src = lambda jb: (0, 2 * (jb % FFN_NJ) + jb // FFN_NJ)

    def body(x_ref, o_ref):
        o_ref[...] = x_ref[...]

    return pl.pallas_call(
        body, name=name, grid=(2 * FFN_NJ,),
        in_specs=[pl.BlockSpec((rows, FFN_HALF), src)],
        out_specs=pl.BlockSpec((rows, FFN_HALF), lambda jb: (0, jb)),
        out_shape=_sds(t.shape, t.dtype),
        compiler_params=_params("parallel"),
    )(t)


def _gelu_parts(a):
    c = math.sqrt(2.0 / math.pi)
    a2 = a * a
    t = jnp.tanh((c * a) * (1.0 + 0.044715 * a2))
    half_a, one_t = 0.5 * a, 1.0 + t
    gelu = half_a * one_t
    dgelu = 0.5 * one_t + half_a * (1.0 - t * t) * (c + (3.0 * 0.044715 * c) * a2)
    return gelu, dgelu


def _row_masks(down):
    row = lax.broadcasted_iota(jnp.int32, (SUBLANE, FFN_TN), 0)
    return (row < 1, row < 2) if down else (row >= SUBLANE - 1, row >= SUBLANE - 2)


def _rolled(x, down):
    return (pltpu.roll(x, 1, 0), pltpu.roll(x, 2, 0)) if down else (
        pltpu.roll(x, SUBLANE - 1, 0), pltpu.roll(x, SUBLANE - 2, 0))


def _shifted(cur_rolled, neighbour_rolled, masks):
    return (jnp.where(masks[0], neighbour_rolled[0], cur_rolled[0]),
            jnp.where(masks[1], neighbour_rolled[1], cur_rolled[1]))


def _conv_consts(w_ref, b_ref):
    shape = (SUBLANE, FFN_TN)
    return [jnp.broadcast_to(w_ref[k:k + 1, :], shape) for k in range(3)] + [jnp.broadcast_to(b_ref[...], shape)]


def _ffn_mid_fwd(u, conv_w, conv_b):
    per = FFN_TM // SUBLANE

    def body(u_ref, h_ref, w_ref, b_ref, m_ref, ab_ref):
        live = (pl.program_id(1) > 0).astype(F32)
        w0, w1, w2, bias = _conv_consts(w_ref, b_ref)
        masks = _row_masks(True)

        def group(g, above):
            rows = pl.ds(pl.multiple_of(g * FFN_GROUP, FFN_GROUP), FFN_GROUP)
            x = u_ref[rows, :].astype(F32)
            convs = []
            for c in range(2):
                cur = x[c * SUBLANE:(c + 1) * SUBLANE]
                cur_rolled = _rolled(cur, True)
                s1, s2 = _shifted(cur_rolled, above, masks)
                convs.append(w0 * s2 + w1 * s1 + w2 * cur + bias)
                above = cur_rolled
            y = jnp.concatenate(convs, axis=0)
            ab_ref[rows, :] = y.astype(ab_ref.dtype)
            m_ref[rows, :] = (_gelu_parts(y[:, :FFN_HALF])[0] * y[:, FFN_HALF:]).astype(m_ref.dtype)
            return above

        lax.fori_loop(0, FFN_TM // (2 * FFN_GROUP), lambda g2, carry: group(2 * g2 + 1, group(2 * g2, carry)),
                      _rolled(h_ref[...].astype(F32) * live, True))

    blk = pl.BlockSpec((FFN_TM, FFN_TN), lambda j, i: (i, j))
    return pl.pallas_call(
        body, name="ffn_mid_fwd", grid=(FFN_NJ, SEQ // FFN_TM),
        in_specs=[blk, pl.BlockSpec((SUBLANE, FFN_TN), lambda j, i: (jnp.maximum(i * per - 1, 0), j)),
                  pl.BlockSpec((3, FFN_TN), lambda j, i: (0, j)), pl.BlockSpec((1, FFN_TN), lambda j, i: (0, j))],
        out_specs=[pl.BlockSpec((FFN_TM, FFN_HALF), lambda j, i: (i, j)), blk],
        out_shape=[_sds((SEQ, D_FF), BF16), _sds((SEQ, 2 * D_FF), BF16)],
        compiler_params=_params("parallel", "arbitrary"),
    )(u, u, conv_w, conv_b)


def _ffn_mid_bwd(dm, u, ab, conv_w):
    nrow = SEQ // FFN_TM
    n_groups = FFN_TM // FFN_GROUP

    def body(dm_ref, u_ref, ab_ref, w_ref, du_ref, gw_ref, gb_ref, c_s):
        @pl.when(pl.program_id(1) == 0)
        def _():
            c_s[...] = jnp.zeros_like(c_s)
            gw_ref[...] = jnp.zeros_like(gw_ref)
            gb_ref[...] = jnp.zeros_like(gb_ref)

        taps = [jnp.broadcast_to(w_ref[k:k + 1, :], (SUBLANE, FFN_TN)) for k in range(3)]
        masks = _row_masks(False)

        def group(t, carry):
            below, acc = carry
            rows = pl.ds(pl.multiple_of((n_groups - 1 - t) * FFN_GROUP, FFN_GROUP), FFN_GROUP)
            x, y, dmv = u_ref[rows, :].astype(F32), ab_ref[rows, :].astype(F32), dm_ref[rows, :].astype(F32)
            gelu, dgelu = _gelu_parts(y[:, :FFN_HALF])
            d = jnp.concatenate([dmv * y[:, FFN_HALF:] * dgelu, dmv * gelu], axis=1)
            acc, pre = list(acc), [None, None]
            for c in (1, 0):
                sl = slice(c * SUBLANE, (c + 1) * SUBLANE)
                cur, xs = d[sl], x[sl]
                cur_rolled = _rolled(cur, False)
                up1, up2 = _shifted(cur_rolled, below, masks)
                acc = [acc[0] + up2 * xs, acc[1] + up1 * xs, acc[2] + cur * xs, acc[3] + cur]
                pre[c] = taps[2] * cur + taps[1] * up1 + taps[0] * up2
                below = cur_rolled
            du_ref[rows, :] = jnp.concatenate(pre, axis=0).astype(du_ref.dtype)
            return below, tuple(acc)

        zeros = jnp.zeros((SUBLANE, FFN_TN), F32)
        below, acc = lax.fori_loop(0, n_groups // 2, lambda t2, carry: group(2 * t2 + 1, group(2 * t2, carry)),
                                   (_rolled(c_s[...], False), (zeros,) * 4))
        c_s[...] = pltpu.roll(below[0], 1, 0)
        for k in range(3):
            gw_ref[k:k + 1, :] += jnp.sum(acc[k], axis=0, keepdims=True)
        gb_ref[...] += jnp.sum(acc[3], axis=0, keepdims=True)

    blk = pl.BlockSpec((FFN_TM, FFN_TN), lambda j, i: (nrow - 1 - i, j))
    return pl.pallas_call(
        body, name="ffn_mid_bwd", grid=(FFN_NJ, nrow),
        in_specs=[pl.BlockSpec((FFN_TM, FFN_HALF), lambda j, i: (nrow - 1 - i, j)), blk, blk,
                  pl.BlockSpec((3, FFN_TN), lambda j, i: (0, j))],
        out_specs=[blk, pl.BlockSpec((3, FFN_TN), lambda j, i: (0, j)), pl.BlockSpec((1, FFN_TN), lambda j, i: (0, j))],
        out_shape=[_sds((SEQ, 2 * D_FF), BF16), _sds((3, 2 * D_FF), F32), _sds((1, 2 * D_FF), F32)],
        scratch_shapes=[pltpu.VMEM((SUBLANE, FFN_TN), F32)],
        compiler_params=_params("parallel", "arbitrary"),
    )(dm, u, ab, conv_w)


def _down_fwd(m, w_down, x2, g_post, target, *, tm=512):
    def body(m_ref, w_ref, x2_ref, g_ref, t_ref, dout_ref, dy_ref, gg_ref, loss_ref):
        @pl.when(pl.program_id(0) == 0)
        def _():
            gg_ref[...] = jnp.zeros_like(gg_ref)
            loss_ref[...] = jnp.zeros_like(loss_ref)

        y = jnp.dot(m_ref[...], w_ref[...], preferred_element_type=F32)
        r = lax.rsqrt(jnp.mean(y * y, axis=-1, keepdims=True) + RMS_EPS)
        yn = y * r
        diff = (x2_ref[...] + yn * g_ref[...]) - t_ref[...]
        loss_ref[...] += jnp.sum(diff * diff)
        dout = diff * (1.0 / D_MODEL)
        dout_ref[...] = dout
        gg_ref[...] += jnp.sum(dout * yn, axis=0, keepdims=True)
        dn = dout * g_ref[...]
        dy_ref[...] = (r * (dn - yn * jnp.mean(dn * yn, axis=-1, keepdims=True))).astype(dy_ref.dtype)

    row = pl.BlockSpec((tm, D_MODEL), lambda i: (i, 0))
    vec = pl.BlockSpec((1, D_MODEL), lambda i: (0, 0))
    return pl.pallas_call(
        body, name="down_fwd", grid=(SEQ // tm,),
        in_specs=[pl.BlockSpec((tm, D_FF), lambda i: (i, 0)), pl.BlockSpec((D_FF, D_MODEL), lambda i: (0, 0)),
                  row, vec, row],
        out_specs=[row, row, vec, pl.BlockSpec((1, LANE), lambda i: (0, 0))],
        out_shape=[_sds((SEQ, D_MODEL), F32), _sds((SEQ, D_MODEL), BF16), _sds((1, D_MODEL), F32),
                   _sds((1, LANE), F32)],
        compiler_params=_params("arbitrary"),
    )(m, w_down, x2, g_post, target)


def _local_step(x, target, w_main, w_f, b_forget, conv_b, g_pre_mix, g_post_mix, g_pre_ffn, g_post_ffn,
                late_weights, ffn_grads_ready, proj_grads_ready, mixer_grads_ready):
    mm = _matmul
    tabs = _rope_tables()

    h1 = _rms_fwd(x, g_pre_mix, name="rms_pre_mix")
    zm = mm(h1, w_main, out_dtype=BF16, tm=2048, tn=512, tk=1024, name="in_proj")
    zf = mm(h1, w_f, out_dtype=F32, tm=2048, tn=F_PAD, tk=1024, name="in_proj_forget")
    f_row, sg_row = _fox_prep(zf[:, :N_HEADS].T, b_forget.reshape(N_HEADS, 1))
    f_cols = jnp.pad(f_row.T, ((0, 0), (0, LANE - N_HEADS)))
    q_slots, k_slots, v_slots = _fox_pack_fwd(zm, f_cols)
    ya, lse_a = _fox_fwd(q_slots, k_slots, v_slots)
    qkv_d = dict(zip([d for _, d in DIL_PATTERNS], _rope_fwd(zm, tabs)))
    dil = [_dil_fwd(qkv_d[d], d) for _, d in DIL_PATTERNS]
    yb, lse_b = _dil_merge([o for o, _ in dil], [l for _, l in dil])
    w_oa, w_ob, w_out, w_up, conv_w, w_down = late_weights(yb)
    pa, pb, mixed = _mix_fwd(ya, yb, w_oa, w_ob, zm)
    y1, x2, h2 = _out_fwd(mixed, w_out, x, g_post_mix, g_pre_ffn)
    u = mm(h2, w_up, out_dtype=BF16, tm=2048, tn=512, tk=1024, name="up_proj")
    m, ab = _ffn_mid_fwd(u, conv_w, _ffn_interleave(conv_b))
    dout, dy2, gg_post_ffn, sq_err = _down_fwd(m, w_down, x2, g_post_ffn, target)

    g_w_down = mm(m, dy2, ta=True, out_dtype=BF16, tm=D_FF // 2, tn=1024, tk=2048, name="grad_w_down")
    dm = mm(dy2, w_down, tb=True, out_dtype=BF16, tm=2048, tn=D_FF // 2, tk=1024, name="d_ffn_mid")
    du, g_conv_w, g_conv_b = _ffn_mid_bwd(dm, u, ab, conv_w)
    g_w_up = mm(h2, du, ta=True, out_dtype=BF16, tm=1024, tn=D_FF // 2, tk=2048, name="grad_w_up")
    tok = ffn_grads_ready(dict(w_down=g_w_down, w_up=_ffn_move_blocks(g_w_up, interleave=False, name="grad_w_up_cols"),
                               conv_w=_ffn_deinterleave(g_conv_w)))
    dh2 = mm(du, w_up, tb=True, out_dtype=BF16, tm=512, tn=1024, tk=2 * D_FF, name="d_h2")

    dx2, dy1, gg_pre_ffn, gg_post_mix = _rms_pair_bwd([dh2], x2, g_pre_ffn, dout, y1, g_post_mix + tok)
    g_w_out = mm(mixed, dy1, ta=True, out_dtype=BF16, tm=1024, tn=1024, tk=2048, name="grad_w_out")
    dmix = mm(dy1, w_out, tb=True, out_dtype=BF16, tm=2048, tn=1024, tk=1024, name="d_mixed")
    dpa, dz = _gate_bwd(dmix, zm, pa, 3, None, name="gate_bwd_fox")
    dpb, dz = _gate_bwd(dmix, zm, pb, 4, dz, name="gate_bwd_dil")
    g_w_oa = mm(ya, dpa, ta=True, out_dtype=BF16, tm=512, tn=1024, tk=SEQ, name="grad_w_o_fox")
    g_w_ob = mm(yb, dpb, ta=True, out_dtype=BF16, tm=512, tn=1024, tk=SEQ, name="grad_w_o_dil")
    tok = proj_grads_ready(dict(w_o_fox=g_w_oa, w_o_dil=g_w_ob, w_out=g_w_out))
    dya = mm(dpa, w_oa, tb=True, out_dtype=BF16, tm=2048, tn=512, tk=1024, name="d_y_fox")
    dyb = mm(dpb, w_ob, tb=True, out_dtype=BF16, tm=2048, tn=512, tk=1024, name="d_y_dil")

    qb_slots, do_slots = _fox_pack_bwd(zm, f_cols + tok, lse_a, ya, dya)
    dz, df_cols = _fox_unpack(*_fox_bwd(qb_slots, k_slots, v_slots, do_slots), dz)
    dfa_t, g_b_forget = _fox_post_bwd(df_cols[:, :N_HEADS].T, sg_row)

    rows_d = _dil_bwd_prep(yb, dyb, lse_b)
    dil_g = [_dil_bwd(qkv_d[d], *rows_d[k], d) for k, (_, d) in enumerate(DIL_PATTERNS)]
    dz = _dil_grad_combine([g[0] for g in dil_g], [g[1] for g in dil_g], [g[2] for g in dil_g], tabs, dz)

    dzf = jnp.pad(dfa_t.T, ((0, 0), (0, F_PAD - N_HEADS)))
    g_w_main = mm(h1, dz, ta=True, out_dtype=BF16, tm=1024, tn=Z_MAIN // 4, tk=2048, name="grad_w_in")
    g_w_f = mm(h1, dzf, ta=True, out_dtype=BF16, tm=1024, tn=F_PAD, tk=1024, name="grad_w_in_forget")
    tok = mixer_grads_ready(dict(w_main=g_w_main, w_f=g_w_f))
    dh1 = [mm(dz, w_main, tb=True, out_dtype=BF16, tm=512, tn=1024, tk=Z_MAIN, name="d_h1"),
           mm(dzf + tok, w_f, tb=True, out_dtype=BF16, tm=2048, tn=1024, tk=F_PAD, name="d_h1_forget")]
    grad_x, gg_pre_mix = _rms_bwd(dh1, x, g_pre_mix, dx2, out_dtype=F32, name="rms_pre_mix_bwd")

    grads = dict(
        b_forget=g_b_forget.reshape(1, N_HEADS), conv_b=_ffn_deinterleave(g_conv_b),
        g_pre_mix=gg_pre_mix, g_post_mix=gg_post_mix, g_pre_ffn=gg_pre_ffn, g_post_ffn=gg_post_ffn)
    return sq_err, grad_x, grads


def _exchange(arrays, scatter, *, name):
    n = len(arrays)
    scatters = [scatter] * n if isinstance(scatter, bool) else list(scatter)

    def body(*refs):
        ins, outs = refs[:n], refs[n:2 * n]
        send_sems, recv_sems, local_sems = refs[2 * n:]
        me, peers = _peers()

        def remote(a, k):
            dev, slot = peers[k]
            return pltpu.make_async_remote_copy(
                src_ref=ins[a].at[slot] if scatters[a] else ins[a], dst_ref=outs[a].at[me],
                send_sem=send_sems.at[a, k], recv_sem=recv_sems.at[a, k],
                device_id=dev, device_id_type=MESH_ID)

        def landed(a, k):
            dev, slot = peers[k]
            return pltpu.make_async_remote_copy(
                src_ref=outs[a].at[slot], dst_ref=outs[a].at[slot],
                send_sem=send_sems.at[a, k], recv_sem=recv_sems.at[a, k],
                device_id=dev, device_id_type=MESH_ID)

        own = [pltpu.make_async_copy(ins[a].at[me] if scatters[a] else ins[a], outs[a].at[me], local_sems.at[a])
               for a in range(n)]
        copies = [remote(a, k) for k in range(N_DEV - 1) for a in range(n)]
        for cp in own + copies:
            cp.start()
        for k in range(N_DEV - 1):
            for a in range(n):
                landed(a, k).wait_recv()
        for cp in copies:
            cp.wait_send()
        for cp in own:
            cp.wait()

    out_shape = [_sds(((N_DEV,) + a.shape[-2:]), a.dtype) for a in arrays]
    return pl.pallas_call(
        body, name=name, in_specs=[ANY] * n, out_specs=[ANY] * n, out_shape=out_shape,
        scratch_shapes=[pltpu.SemaphoreType.DMA((n, N_DEV - 1)), pltpu.SemaphoreType.DMA((n, N_DEV - 1)),
                        pltpu.SemaphoreType.DMA((n,))],
    )(*arrays)


def _gather_two_level(shard, *, name):
    def body(x_ref, out_ref, send_sems, recv_sems, local_sem):
        x, y, c = lax.axis_index("x"), lax.axis_index("y"), lax.axis_index("c")
        me, sibling = (x, y, c), (x, y, 1 - c)
        chips = [(1 - x, y), (x, 1 - y), (1 - x, 1 - y)]

        def slot(px, py, pc):
            return out_ref.at[4 * px + 2 * py + pc]

        def copy(k, block, to, src=None):
            return pltpu.make_async_remote_copy(
                src_ref=slot(*block) if src is None else src, dst_ref=slot(*block),
                send_sem=send_sems.at[k], recv_sem=recv_sems.at[k], device_id=to, device_id_type=MESH_ID)

        mine = pltpu.make_async_copy(x_ref, slot(*me), local_sem)
        mine.start()
        first = [copy(0, me, sibling, src=x_ref)]
        first += [copy(1 + j, me, (*chip, c), src=x_ref) for j, chip in enumerate(chips)]
        for cp in first:
            cp.start()
        passed = [copy(4 + j, (*chip, c), sibling) for j, chip in enumerate(chips)]
        for j, chip in enumerate(chips):
            copy(1 + j, (*chip, c), me).wait_recv()
            passed[j].start()
        copy(0, sibling, me).wait_recv()
        for j, chip in enumerate(chips):
            copy(4 + j, (*chip, 1 - c), me).wait_recv()
        for cp in first + passed:
            cp.wait_send()
        mine.wait()

    return pl.pallas_call(
        body, name=name, in_specs=[ANY], out_specs=ANY, out_shape=_sds((N_DEV,) + shard.shape, shard.dtype),
        scratch_shapes=[pltpu.SemaphoreType.DMA((N_DEV - 1,)), pltpu.SemaphoreType.DMA((N_DEV - 1,)),
                        pltpu.SemaphoreType.DMA],
    )(shard)


N_CHIPS = N_DEV // 2


def _peers(chips_only=False):
    x, y, c = lax.axis_index("x"), lax.axis_index("y"), lax.axis_index("c")
    out = []
    if chips_only:
        for k in range(1, N_CHIPS):
            px = 1 - x if k & 2 else x
            py = 1 - y if k & 1 else y
            out.append(((px, py, c), 2 * px + py))
        return 2 * x + y, out
    for k in range(1, N_DEV):
        px = 1 - x if k & 4 else x
        py = 1 - y if k & 2 else y
        pc = 1 - c if k & 1 else c
        out.append(((px, py, pc), 4 * px + 2 * py + pc))
    return 4 * x + 2 * y + c, out


def _sibling_swap(slot_arrays, *, name):
    n = len(slot_arrays)

    def body(*refs):
        ins, outs, send_sems, recv_sems = refs[:n], refs[n:2 * n], refs[2 * n], refs[2 * n + 1]
        x, y, c = lax.axis_index("x"), lax.axis_index("y"), lax.axis_index("c")
        copies = [pltpu.make_async_remote_copy(
            src_ref=ins[a].at[2 * q + (1 - c)], dst_ref=outs[a].at[q], send_sem=send_sems.at[a, q],
            recv_sem=recv_sems.at[a, q], device_id=(x, y, 1 - c), device_id_type=MESH_ID)
            for a in range(n) for q in range(N_CHIPS)]
        for cp in copies:
            cp.start()
        for cp in copies:
            cp.wait_recv()
        for cp in copies:
            cp.wait_send()

    return pl.pallas_call(
        body, name=name, in_specs=[ANY] * n, out_specs=[ANY] * n,
        out_shape=[_sds((N_CHIPS,) + t.shape[1:], t.dtype) for t in slot_arrays],
        scratch_shapes=[pltpu.SemaphoreType.DMA((n, N_CHIPS)), pltpu.SemaphoreType.DMA((n, N_CHIPS))],
    )(*slot_arrays)


def _pair_sum(slots, from_sibling, *, name, tn):
    _, r, c = slots.shape
    core = lax.axis_index("c").astype(jnp.int32).reshape(1)

    def body(core_ref, a_ref, b_ref, o_ref):
        o_ref[...] = (a_ref[...].astype(F32) + b_ref[...].astype(F32)).astype(o_ref.dtype)

    blk = lambda f: pl.BlockSpec((1, r, tn), f)
    return pl.pallas_call(
        body, name=name,
        grid_spec=pltpu.PrefetchScalarGridSpec(
            num_scalar_prefetch=1, grid=(N_CHIPS, c // tn),
            in_specs=[blk(lambda q, j, core: (2 * q + core[0], 0, j)), blk(lambda q, j, core: (q, 0, j))],
            out_specs=blk(lambda q, j, core: (q, 0, j))),
        out_shape=_sds((N_CHIPS, r, c), slots.dtype),
        compiler_params=_params("parallel", "parallel"),
    )(core, slots, from_sibling)


def _sum_parts(parts, *, name, tn):
    n, r, c = parts.shape

    def body(p_ref, o_ref):
        total = p_ref[0].astype(F32)
        for s in range(1, n):
            total = total + p_ref[s].astype(F32)
        o_ref[...] = total

    return pl.pallas_call(
        body, name=name, grid=(c // tn,),
        in_specs=[pl.BlockSpec((n, r, tn), lambda j: (0, 0, j))],
        out_specs=pl.BlockSpec((r, tn), lambda j: (0, j)), out_shape=_sds((r, c), F32),
        compiler_params=_params("parallel"),
    )(parts)


HBM = pl.BlockSpec(memory_space=pltpu.HBM)
SEM = pl.BlockSpec(memory_space=pltpu.SEMAPHORE)
DATAFLOW = pltpu.SideEffectType.DATAFLOW_SIDE_EFFECTING


def _split_copy(srcs, lands, send_sems, recv_sems, scatter, a, k, me, peers, incoming=False):
    dev, slot = peers[k]
    if incoming:
        src = dst = lands[a].at[slot]
    else:
        src, dst = (srcs[a].at[slot] if scatter else srcs[a]), lands[a].at[me]
    sem = a * len(peers) + k
    return pltpu.make_async_remote_copy(
        src_ref=src, dst_ref=dst, send_sem=send_sems.at[sem], recv_sem=recv_sems.at[sem],
        device_id=dev, device_id_type=MESH_ID)


def _exchange_start(arrays, scatter, *, name, chips_only=False):
    n = len(arrays)
    n_slots = N_CHIPS if chips_only else N_DEV

    def body(*refs):
        srcs, lands = refs[:n], refs[n:2 * n]
        send_sems, recv_sems = refs[2 * n], refs[2 * n + 1]
        token = refs[-1]
        me, peers = _peers(chips_only)
        for k in range(len(peers)):
            for a in range(n):
                _split_copy(srcs, lands, send_sems, recv_sems, scatter, a, k, me, peers).start()
        token[...] = jnp.zeros_like(token)

    land_shapes = [((n_slots,) + a.shape[-2:], a.dtype) for a in arrays]
    sems = pltpu.SemaphoreType.DMA((n * (n_slots - 1),))
    outs = pl.pallas_call(
        body, name=name,
        out_shape=(sems, sems, *[pltpu.HBM(a.shape, a.dtype) for a in arrays],
                   *[pltpu.HBM(s, d) for s, d in land_shapes], _sds((SUBLANE, LANE), F32)),
        in_specs=[HBM] * (2 * n),
        out_specs=(SEM, SEM, *[HBM] * (2 * n), pl.BlockSpec(memory_space=pltpu.VMEM)),
        input_output_aliases={i: 2 + i for i in range(2 * n)},
        compiler_params=pltpu.CompilerParams(has_side_effects=DATAFLOW),
    )(*[pltpu.with_memory_space_constraint(a, pltpu.HBM) for a in arrays],
      *[pltpu.with_memory_space_constraint(lax.empty(s, d), pltpu.HBM) for s, d in land_shapes])
    return (outs[0], outs[1], outs[2:2 + n], outs[2 + n:2 + 2 * n], scatter, chips_only), outs[-1]


def _exchange_wait(handles, after, *, name):
    send_sems, recv_sems, srcs, lands, scatter, chips_only = handles
    n = len(srcs)

    def body(*refs):
        src_refs, land_refs = refs[:n], refs[n:2 * n]
        send_ref, recv_ref = refs[2 * n], refs[2 * n + 1]
        me, peers = _peers(chips_only)
        for k in range(len(peers)):
            for a in range(n):
                _split_copy(src_refs, land_refs, send_ref, recv_ref, scatter, a, k, me, peers).wait_send()
                _split_copy(src_refs, land_refs, send_ref, recv_ref, scatter, a, k, me, peers, True).wait_recv()

    outs = pl.pallas_call(
        body, name=name,
        out_shape=tuple(pltpu.HBM(t.shape, t.dtype) for t in (*srcs, *lands)),
        in_specs=[HBM] * (2 * n) + [SEM, SEM, pl.BlockSpec(memory_space=pl.ANY)],
        out_specs=tuple([HBM] * (2 * n)),
        input_output_aliases={i: i for i in range(2 * n)},
        compiler_params=pltpu.CompilerParams(has_side_effects=DATAFLOW),
    )(*srcs, *lands, send_sems, recv_sems, after)
    return _with_own_slot(outs[n:], outs[:n], scatter, chips_only)


def _with_own_slot(landed, own, scatter, chips_only):
    me = 2 * lax.axis_index("x") + lax.axis_index("y")
    if not chips_only:
        me = 2 * me + lax.axis_index("c")
    out = []
    for buf, src in zip(landed, own):
        mine = lax.dynamic_index_in_dim(src, me, 0, keepdims=False) if scatter else src
        out.append(lax.dynamic_update_index_in_dim(buf, mine, me, 0))
    return out


def _adamw(parts, w, m, v, *, name, tm):
    r, c = w.shape
    assert r % tm == 0

    def body(p_ref, w_ref, m_ref, v_ref, g_ref, d_ref, nm_ref, nv_ref):
        _adamw_update(p_ref, w_ref, m_ref, v_ref, g_ref, d_ref, nm_ref, nv_ref)

    blk = pl.BlockSpec((tm, c), lambda i: (i, 0))
    return pl.pallas_call(
        body, name=name, grid=(r // tm,),
        in_specs=[pl.BlockSpec((parts.shape[0], tm, c), lambda i: (0, i, 0)), blk, blk, blk],
        out_specs=[blk] * 4, out_shape=[_sds((r, c), F32)] * 4,
        compiler_params=_params("parallel"),
    )(parts, w, m, v)


def _adamw_update(p_ref, w_ref, m_ref, v_ref, g_ref, d_ref, nm_ref, nv_ref):
    g = p_ref[0].astype(F32)
    for s in range(1, p_ref.shape[0]):
        g = g + p_ref[s].astype(F32)
    g_ref[...] = g
    m_new = ADAM_B1 * m_ref[...] + (1.0 - ADAM_B1) * g
    v_new = ADAM_B2 * v_ref[...] + (1.0 - ADAM_B2) * (g * g)
    nm_ref[...] = m_new
    nv_ref[...] = v_new
    m_hat = m_new / (1.0 - ADAM_B1 ** ADAM_STEP)
    v_hat = v_new / (1.0 - ADAM_B2 ** ADAM_STEP)
    d_ref[...] = -ADAM_LR * (m_hat / (jnp.sqrt(v_hat) + ADAM_EPS) + ADAM_WD * w_ref[...])


SMALL = ("g_pre_mix", "b_forget", "g_post_mix", "g_pre_ffn", "conv_b", "g_post_ffn")


def _adamw_small(parts, ws, ms, vs, sq_err_parts):
    n = len(ws)

    def body(*refs):
        ins, sq_ref, outs, loss_ref = refs[:4 * n], refs[4 * n], refs[4 * n + 1:-1], refs[-1]
        for i in range(n):
            _adamw_update(ins[i], ins[n + i], ins[2 * n + i], ins[3 * n + i], *outs[4 * i:4 * i + 4])
        total = sq_ref[0]
        for s in range(1, N_DEV):
            total = total + sq_ref[s]
        loss_ref[...] = total * (0.5 / D_MODEL)

    res = pl.pallas_call(
        body, name="adamw_small",
        out_shape=[_sds(w.shape, F32) for w in ws for _ in range(4)] + [_sds((1, LANE), F32)],
        compiler_params=pltpu.CompilerParams(vmem_limit_bytes=VMEM_LIMIT),
    )(*parts, *ws, *ms, *vs, sq_err_parts)
    return [res[4 * i:4 * i + 4] for i in range(n)], res[-1][0, 0]


def kernel(x, g_pre_mix, w_in, b_forget, w_o_fox, w_o_dil, w_out, g_post_mix, g_pre_ffn, w_up, conv_w, conv_b, w_down, g_post_ffn, loss_target, m_g_pre_mix, m_w_in, m_b_forget, m_w_o_fox, m_w_o_dil, m_w_out, m_g_post_mix, m_g_pre_ffn, m_w_up, m_conv_w, m_conv_b, m_w_down, m_g_post_ffn, v_g_pre_mix, v_w_in, v_b_forget, v_w_o_fox, v_w_o_dil, v_w_out, v_g_post_mix, v_g_pre_ffn, v_w_up, v_conv_w, v_conv_b, v_w_down, v_g_post_ffn):
    names = ("g_pre_mix", "w_in", "b_forget", "w_o_fox", "w_o_dil", "w_out", "g_post_mix", "g_pre_ffn",
             "w_up", "conv_w", "conv_b", "w_down", "g_post_ffn")
    w = dict(g_pre_mix=g_pre_mix, w_in=w_in, b_forget=b_forget, w_o_fox=w_o_fox, w_o_dil=w_o_dil, w_out=w_out,
             g_post_mix=g_post_mix, g_pre_ffn=g_pre_ffn, w_up=w_up, conv_w=conv_w, conv_b=conv_b, w_down=w_down,
             g_post_ffn=g_post_ffn)
    m = dict(g_pre_mix=m_g_pre_mix, w_in=m_w_in, b_forget=m_b_forget, w_o_fox=m_w_o_fox, w_o_dil=m_w_o_dil,
             w_out=m_w_out, g_post_mix=m_g_post_mix, g_pre_ffn=m_g_pre_ffn, w_up=m_w_up, conv_w=m_conv_w,
             conv_b=m_conv_b, w_down=m_w_down, g_post_ffn=m_g_post_ffn)
    v = dict(g_pre_mix=v_g_pre_mix, w_in=v_w_in, b_forget=v_b_forget, w_o_fox=v_w_o_fox, w_o_dil=v_w_o_dil,
             w_out=v_w_out, g_post_mix=v_g_post_mix, g_pre_ffn=v_g_pre_ffn, w_up=v_w_up, conv_w=v_conv_w,
             conv_b=v_conv_b, w_down=v_w_down, g_post_ffn=v_g_post_ffn)
    sharded = ("w_in", "w_o_fox", "w_o_dil", "w_out", "w_up", "w_down", "conv_w")
    wire = lambda n: F32 if n == "conv_w" else BF16

    by_cols = lambda t: jnp.transpose(t, (1, 0, 2)).reshape(t.shape[1], N_DEV * t.shape[2])
    by_rows = lambda t: t.reshape(N_DEV * t.shape[1], t.shape[2])
    col_slots = lambda t: jnp.transpose(t.reshape(t.shape[0], N_DEV, t.shape[1] // N_DEV), (1, 0, 2))
    row_slots = lambda t: t.reshape(N_DEV, t.shape[0] // N_DEV, t.shape[1])
    to_slots = lambda n, t: (row_slots if n in ("w_out", "w_down") else col_slots)(t).astype(wire(n))
    shard = lambda n: w[n][0].astype(wire(n))
    f_lo, f_hi = 3 * ATT_W, 3 * ATT_W + N_HEADS

    w_in_full = by_cols(_gather_two_level(shard("w_in"), name="gather_w_in"))
    w_main = jnp.concatenate([w_in_full[:, :f_lo], w_in_full[:, f_hi:]], axis=1)
    w_f = jnp.pad(w_in_full[:, f_lo:f_hi], ((0, 0), (0, F_PAD - N_HEADS)))
    late = ("w_o_fox", "w_o_dil", "w_out", "w_up", "conv_w", "w_down")
    order = jnp.minimum(jnp.abs(w_in_full[0, 0].astype(F32)), 0.0)
    late_handles, late_tok = _exchange_start(
        [shard(n) + order.astype(wire(n)) if n == "conv_w" else shard(n) for n in late], False,
        name="gather_late_start")

    def late_weights(after):
        got = dict(zip(late, _exchange_wait(late_handles, after, name="gather_late_wait")))
        return (by_cols(got["w_o_fox"]), by_cols(got["w_o_dil"]), by_rows(got["w_out"]),
                _ffn_move_blocks(by_cols(got["w_up"]), interleave=True, name="w_up_cols"),
                _ffn_interleave(by_cols(got["conv_w"])),
                by_rows(got["w_down"]))

    pending = {}

    def ffn_grads_ready(g):
        pending["ffn"] = _exchange_start([to_slots(n, g[n]) for n in ("w_down", "w_up", "conv_w")], True,
                                         name="scatter_ffn_start")
        return pending["ffn"][1][0, 0]

    def proj_grads_ready(g):
        pending["proj"] = _exchange_start([to_slots(n, g[n]) for n in ("w_o_fox", "w_o_dil", "w_out")], True,
                                          name="scatter_proj_start")
        return pending["proj"][1][0, 0]

    def mixer_grads_ready(g):
        slabs = [g["w_main"].reshape(N_DEV, D_MODEL // N_DEV, Z_MAIN), g["w_f"].reshape(N_DEV, D_MODEL // N_DEV, F_PAD)]
        theirs = _sibling_swap(slabs, name="scatter_w_in_swap")
        chip_sums = [_pair_sum(slabs[0], theirs[0], name="scatter_w_in_pair_sum", tn=Z_MAIN // 4),
                     _pair_sum(slabs[1], theirs[1], name="scatter_w_in_forget_pair_sum", tn=F_PAD)]
        pending["w_in"] = _exchange_start(chip_sums, True, name="scatter_w_in_start", chips_only=True)
        return pending["w_in"][1][0, 0]

    sq_err, grad_x, g = _local_step(
        x[0], loss_target[0], w_main, w_f, b_forget, conv_b, g_pre_mix + late_tok[0, 0], g_post_mix, g_pre_ffn,
        g_post_ffn, late_weights, ffn_grads_ready, proj_grads_ready, mixer_grads_ready)

    tiles = dict(w_in=256, w_o_fox=512, w_o_dil=512, w_out=128, w_up=256, w_down=176, conv_w=3)
    adam = lambda n, p: _adamw(p, w[n][0], m[n][0], v[n][0], name=f"adamw_{n}", tm=tiles[n])
    res = {}
    for key, group in (("ffn", ("w_down", "w_up", "conv_w")), ("proj", ("w_o_fox", "w_o_dil", "w_out"))):
        landed = _exchange_wait(pending[key][0], grad_x, name=f"scatter_{key}_wait")
        res.update({n: adam(n, p) for n, p in zip(group, landed)})
    done = res["w_up"][3]
    main_parts, f_parts = _exchange_wait(pending["w_in"][0], done, name="scatter_w_in_wait")
    slab_main = _sum_parts(main_parts, name="scatter_w_in_sum", tn=Z_MAIN // 4)
    slab_f = _sum_parts(f_parts, name="scatter_w_in_forget_sum", tn=F_PAD)
    slab = jnp.concatenate([slab_main[:, :f_lo], slab_f[:, :N_HEADS], slab_main[:, f_lo:]], axis=1)
    last = _exchange([col_slots(slab).astype(BF16)] + [g[n] for n in SMALL] + [sq_err],
                     [True] + [False] * (len(SMALL) + 1), name="scatter_w_in_rows_gather_small")
    rows, small_parts = last[0], last[1:]
    res["w_in"] = adam("w_in", rows.reshape(1, D_MODEL, rows.shape[-1]))
    small, loss = _adamw_small(small_parts[:-1], *[[t[n] for n in SMALL] for t in (w, m, v)], small_parts[-1])
    small = dict(zip(SMALL, small))
    out = [[(res[n][k][None] if n in sharded else small[n][k]) for n in names] for k in range(4)]
    return (loss, grad_x[None], *out[0], *out[1], *out[2], *out[3])
```

```python
import functools
import math

import jax
import jax.numpy as jnp
import numpy as np
from jax import lax
from jax.experimental import pallas as pl
from jax.experimental.pallas import tpu as pltpu

F32 = jnp.float32
BF16 = jnp.bfloat16

SEQ = 4096
D_MODEL = 1024
N_HEADS = 8
HEAD_DIM = 64
ATT_W = N_HEADS * HEAD_DIM
D_FF = 2816
Z_MAIN = 5120
F_PAD = 128
ROPE_DIM = 16
ROPE_THETA = 500000.0
RMS_EPS = 1e-6
NEG_INF = -1e30
SCALE = 1.0 / math.sqrt(HEAD_DIM)
DIL_PATTERNS = ((128, 1), (512, 4), (2048, 16))
DIL_BLK = 128
N_DEV = 8

ADAM_LR = 0.001
ADAM_B1 = 0.9
ADAM_B2 = 0.999
ADAM_EPS = 1e-08
ADAM_WD = 0.01
ADAM_STEP = 10

LANE = 128
SUBLANE = 8
VMEM_LIMIT = 56 * 1024 * 1024
MESH_ID = pl.DeviceIdType.MESH
ANY = pl.BlockSpec(memory_space=pl.ANY)


def _params(*sem):
    return pltpu.CompilerParams(dimension_semantics=sem, vmem_limit_bytes=VMEM_LIMIT)


def _sds(shape, dtype):
    return jax.ShapeDtypeStruct(shape, dtype)


def _matmul(a, b, *, ta=False, tb=False, out_dtype, tm, tn, tk, name, b_k_off=0):
    if ta:
        kk, m = a.shape
    else:
        m, kk = a.shape
    n = b.shape[0] if tb else b.shape[1]
    tm, tn, tk = min(tm, m), min(tn, n), min(tk, kk)
    assert (b.shape[1] if tb else b.shape[0]) >= b_k_off * tk + kk
    assert m % tm == 0 and n % tn == 0 and kk % tk == 0, (name, m, n, kk, tm, tn, tk)
    nk = kk // tk
    dims = (((0 if ta else 1,), (1 if tb else 0,)), ((), ()))

    def body(a_ref, b_ref, o_ref, *scratch):
        p = lax.dot_general(a_ref[...].astype(BF16), b_ref[...].astype(BF16), dims,
                            preferred_element_type=F32)
        if nk == 1:
            o_ref[...] = p.astype(o_ref.dtype)
        else:
            acc = scratch[0]
            k = pl.program_id(2)

            @pl.when(k == 0)
            def _():
                acc[...] = p

            @pl.when(k > 0)
            def _():
                acc[...] += p

            @pl.when(k == nk - 1)
            def _():
                o_ref[...] = acc[...].astype(o_ref.dtype)

    a_spec = (pl.BlockSpec((tk, tm), lambda i, j, k: (k, i)) if ta
              else pl.BlockSpec((tm, tk), lambda i, j, k: (i, k)))
    b_spec = (pl.BlockSpec((tn, tk), lambda i, j, k: (j, k + b_k_off)) if tb
              else pl.BlockSpec((tk, tn), lambda i, j, k: (k + b_k_off, j)))
    return pl.pallas_call(
        body, name=name, grid=(m // tm, n // tn, nk),
        in_specs=[a_spec, b_spec],
        out_specs=pl.BlockSpec((tm, tn), lambda i, j, k: (i, j)),
        out_shape=_sds((m, n), out_dtype),
        scratch_shapes=[pltpu.VMEM((tm, tn), F32)] if nk > 1 else [],
        compiler_params=_params("parallel", "parallel", "arbitrary"),
    )(a, b)


def _rms_fwd(x, g, *, name, tm=512):
    def body(x_ref, g_ref, h_ref):
        xv = x_ref[...]
        r = lax.rsqrt(jnp.mean(xv * xv, axis=-1, keepdims=True) + RMS_EPS)
        h_ref[...] = (xv * r * g_ref[...]).astype(h_ref.dtype)

    return pl.pallas_call(
        body, name=name, grid=(SEQ // tm,),
        in_specs=[pl.BlockSpec((tm, D_MODEL), lambda i: (i, 0)), pl.BlockSpec((1, D_MODEL), lambda i: (0, 0))],
        out_specs=pl.BlockSpec((tm, D_MODEL), lambda i: (i, 0)),
        out_shape=_sds((SEQ, D_MODEL), BF16),
        compiler_params=_params("parallel"),
    )(x, g)


def _rms_bwd(dh_parts, xin, g, dres, *, out_dtype, name, tm=512):
    n_parts = len(dh_parts)
    has_res = dres is not None

    def body(*refs):
        parts = refs[:n_parts]
        x_ref, g_ref = refs[n_parts], refs[n_parts + 1]
        res_ref = refs[n_parts + 2] if has_res else None
        o_ref, gg_ref = refs[-2], refs[-1]
        dh = parts[0][...].astype(F32)
        for p in parts[1:]:
            dh = dh + p[...].astype(F32)
        xv = x_ref[...]
        r = lax.rsqrt(jnp.mean(xv * xv, axis=-1, keepdims=True) + RMS_EPS)
        xn = xv * r

        @pl.when(pl.program_id(0) == 0)
        def _():
            gg_ref[...] = jnp.zeros_like(gg_ref)

        gg_ref[...] += jnp.sum(dh * xn, axis=0, keepdims=True)
        dxn = dh * g_ref[...]
        dx = r * (dxn - xn * jnp.mean(dxn * xn, axis=-1, keepdims=True))
        if has_res:
            dx = dx + res_ref[...]
        o_ref[...] = dx.astype(o_ref.dtype)

    row = pl.BlockSpec((tm, D_MODEL), lambda i: (i, 0))
    vec = pl.BlockSpec((1, D_MODEL), lambda i: (0, 0))
    args = list(dh_parts) + [xin, g] + ([dres] if has_res else [])
    return pl.pallas_call(
        body, name=name, grid=(SEQ // tm,),
        in_specs=[row] * n_parts + [row, vec] + ([row] if has_res else []),
        out_specs=[row, vec],
        out_shape=[_sds((SEQ, D_MODEL), out_dtype), _sds((1, D_MODEL), F32)],
        compiler_params=_params("arbitrary"),
    )(*args)


def _rms_pair_bwd(dh_parts, x2, g_pre, dres, y1, g_post, *, tm=512):
    n_parts = len(dh_parts)

    def norm_bwd(dh, xin, g_ref, gg_ref):
        r = lax.rsqrt(jnp.mean(xin * xin, axis=-1, keepdims=True) + RMS_EPS)
        xn = xin * r
        gg_ref[...] += jnp.sum(dh * xn, axis=0, keepdims=True)
        dxn = dh * g_ref[...]
        return r * (dxn - xn * jnp.mean(dxn * xn, axis=-1, keepdims=True))

    def body(*refs):
        parts = refs[:n_parts]
        x2_ref, gpre_ref, res_ref, y1_ref, gpost_ref, dx2_ref, dy1_ref, ggpre_ref, ggpost_ref = refs[n_parts:]

        @pl.when(pl.program_id(0) == 0)
        def _():
            ggpre_ref[...] = jnp.zeros_like(ggpre_ref)
            ggpost_ref[...] = jnp.zeros_like(ggpost_ref)

        dh = parts[0][...].astype(F32)
        for p in parts[1:]:
            dh = dh + p[...].astype(F32)
        dx2 = res_ref[...] + norm_bwd(dh, x2_ref[...], gpre_ref, ggpre_ref)
        dx2_ref[...] = dx2
        dy1_ref[...] = norm_bwd(dx2, y1_ref[...], gpost_ref, ggpost_ref).astype(dy1_ref.dtype)

    row = pl.BlockSpec((tm, D_MODEL), lambda i: (i, 0))
    vec = pl.BlockSpec((1, D_MODEL), lambda i: (0, 0))
    return pl.pallas_call(
        body, name="rms_pair_bwd", grid=(SEQ // tm,),
        in_specs=[row] * n_parts + [row, vec, row, row, vec],
        out_specs=[row, row, vec, vec],
        out_shape=[_sds((SEQ, D_MODEL), F32), _sds((SEQ, D_MODEL), BF16), _sds((1, D_MODEL), F32),
                   _sds((1, D_MODEL), F32)],
        compiler_params=_params("arbitrary"),
    )(*dh_parts, x2, g_pre, dres, y1, g_post)


SCAN_BLK = 512


def _split_dot(v, tri):
    hi = v.astype(BF16)
    r1 = v - hi.astype(F32)
    mid = r1.astype(BF16)
    lo = (r1 - mid.astype(F32)).astype(BF16)
    dot = functools.partial(jnp.dot, preferred_element_type=F32)
    return dot(hi, tri) + dot(mid, tri) + dot(lo, tri)


def _fox_prep(fa_t, b_col):
    nblk = SEQ // SCAN_BLK

    def body(fa_ref, b_ref, f_ref, sg_ref):
        row = lax.broadcasted_iota(jnp.int32, (SCAN_BLK, SCAN_BLK), 0)
        col = lax.broadcasted_iota(jnp.int32, (SCAN_BLK, SCAN_BLK), 1)
        upper = (row <= col).astype(BF16)
        carry = jnp.zeros((N_HEADS, 1), F32)
        for blk in range(nblk):
            sl = pl.ds(blk * SCAN_BLK, SCAN_BLK)
            xx = fa_ref[:, sl] + b_ref[...]
            e = jnp.exp(-jnp.abs(xx))
            logf = jnp.minimum(xx, 0.0) - jnp.log(1.0 + e)
            sg_ref[:, sl] = jnp.where(xx >= 0.0, e, 1.0) / (1.0 + e)
            c = _split_dot(logf, upper) + carry
            f_ref[:, sl] = c
            carry = c[:, SCAN_BLK - 1:SCAN_BLK]

    return pl.pallas_call(
        body, name="fox_prep",
        out_shape=[_sds((N_HEADS, SEQ), F32), _sds((N_HEADS, SEQ), F32)],
        compiler_params=pltpu.CompilerParams(vmem_limit_bytes=VMEM_LIMIT),
    )(fa_t, b_col)


def _fox_post_bwd(df_t, sg_t):
    nblk = SEQ // SCAN_BLK

    def body(df_ref, sg_ref, dfa_ref, gb_ref):
        row = lax.broadcasted_iota(jnp.int32, (SCAN_BLK, SCAN_BLK), 0)
        col = lax.broadcasted_iota(jnp.int32, (SCAN_BLK, SCAN_BLK), 1)
        lower = (row >= col).astype(BF16)
        carry = jnp.zeros((N_HEADS, 1), F32)
        gb = jnp.zeros((N_HEADS, 1), F32)
        for blk in reversed(range(nblk)):
            sl = pl.ds(blk * SCAN_BLK, SCAN_BLK)
            c = _split_dot(df_ref[:, sl], lower) + carry
            carry = c[:, 0:1]
            dfa = c * sg_ref[:, sl]
            dfa_ref[:, sl] = dfa
            gb = gb + jnp.sum(dfa, axis=1, keepdims=True)
        gb_ref[...] = gb

    return pl.pallas_call(
        body, name="fox_post_bwd",
        out_shape=[_sds((N_HEADS, SEQ), F32), _sds((N_HEADS, 1), F32)],
        compiler_params=pltpu.CompilerParams(vmem_limit_bytes=VMEM_LIMIT),
    )(df_t, sg_t)


FOX_T = 512
NT_DIMS = (((1,), (1,)), ((), ()))
TN_DIMS = (((0,), (0,)), ((), ()))


def _head(ref_or_val, h):
    return ref_or_val[:, h * HEAD_DIM:(h + 1) * HEAD_DIM]


def _split3(v):
    hi = v.astype(BF16).astype(F32)
    r1 = v - hi
    mid = r1.astype(BF16).astype(F32)
    return hi, mid, (r1 - mid).astype(BF16).astype(F32)


ONE_LANE = 3 * N_HEADS


def _pack_terms(v, with_one):
    hi, mid, lo = _split3(v)
    t = hi + pltpu.roll(mid, N_HEADS, 1) + pltpu.roll(lo, 2 * N_HEADS, 1)
    if with_one:
        t = t + (lax.broadcasted_iota(jnp.int32, v.shape, 1) == ONE_LANE).astype(F32)
    return t.astype(BF16)


def _aux_matrices():
    to_q = np.zeros((LANE, N_HEADS * 2 * HEAD_DIM), np.float32)
    to_k = np.zeros_like(to_q)
    for h in range(N_HEADS):
        base = h * 2 * HEAD_DIM + HEAD_DIM
        for s in range(3):
            to_q[s * N_HEADS + h, base + s] = 1.0
            to_q[ONE_LANE, base + 3 + s] = 1.0
            to_k[ONE_LANE, base + s] = 1.0
            to_k[s * N_HEADS + h, base + 3 + s] = -1.0
    return jnp.asarray(to_q, BF16), jnp.asarray(to_k, BF16)


def _head_sums():
    total = np.zeros((N_HEADS * HEAD_DIM, LANE), np.float32)
    first = np.zeros_like(total)
    for h in range(N_HEADS):
        total[h * HEAD_DIM:(h + 1) * HEAD_DIM, h] = 1.0
        first[h * HEAD_DIM, h] = 1.0
    return jnp.asarray(total, BF16), jnp.asarray(first, BF16)


SLOT = 2 * HEAD_DIM
N_SPLIT = 3
FOX_FWD_HEADS = 8
FOX_BWD_HEADS = 4


def _slot(ref, h):
    return ref[:, h * SLOT:(h + 1) * SLOT]


def _fox_pack_fwd(zm, f_cols, *, tm=512):
    def body(q_ref, k_ref, v_ref, f_ref, tq_ref, tk_ref, qs_ref, ks_ref, vs_ref):
        ones = jnp.ones((tm, HEAD_DIM), BF16)
        terms = _pack_terms(f_ref[...], True)
        q_aux = jnp.dot(terms, tq_ref[...], preferred_element_type=F32).astype(BF16)
        k_aux = jnp.dot(terms, tk_ref[...], preferred_element_type=F32).astype(BF16)
        for h in range(N_HEADS):
            aux = slice(h * SLOT + HEAD_DIM, (h + 1) * SLOT)
            qs_ref[:, h * SLOT:(h + 1) * SLOT] = jnp.concatenate(
                [(_head(q_ref, h).astype(F32) * SCALE).astype(BF16), q_aux[:, aux]], axis=1)
            ks_ref[:, h * SLOT:(h + 1) * SLOT] = jnp.concatenate([_head(k_ref, h), k_aux[:, aux]], axis=1)
            vs_ref[:, h * SLOT:(h + 1) * SLOT] = jnp.concatenate([_head(v_ref, h), ones], axis=1)

    col = lambda b: pl.BlockSpec((tm, ATT_W), lambda i: (i, b))
    wide = pl.BlockSpec((tm, N_HEADS * SLOT), lambda i: (i, 0))
    const = pl.BlockSpec((LANE, N_HEADS * SLOT), lambda i: (0, 0))
    return pl.pallas_call(
        body, name="fox_pack_fwd", grid=(SEQ // tm,),
        in_specs=[col(0), col(1), col(2), pl.BlockSpec((tm, LANE), lambda i: (i, 0)), const, const],
        out_specs=[wide] * 3, out_shape=[_sds((SEQ, N_HEADS * SLOT), BF16)] * 3,
        compiler_params=_params("parallel"),
    )(zm, zm, zm, f_cols, *_aux_matrices())


def _fox_pack_bwd(zm, f_cols, lse, o, do, *, tm=512):
    def body(q_ref, f_ref, lse_ref, o_ref, do_ref, tq_ref, total_ref, first_ref, qs_ref, ds_ref):
        delta = _split_dot(o_ref[...].astype(F32) * do_ref[...].astype(F32), total_ref[...])
        lse_h = _split_dot(lse_ref[...], first_ref[...])
        q_aux = jnp.dot(_pack_terms(f_ref[...] - lse_h, True), tq_ref[...], preferred_element_type=F32).astype(BF16)
        d_aux = jnp.dot(_pack_terms(-delta, False), tq_ref[...], preferred_element_type=F32).astype(BF16)
        for h in range(N_HEADS):
            aux = slice(h * SLOT + HEAD_DIM, (h + 1) * SLOT)
            qs_ref[:, h * SLOT:(h + 1) * SLOT] = jnp.concatenate(
                [(_head(q_ref, h).astype(F32) * SCALE).astype(BF16), q_aux[:, aux]], axis=1)
            ds_ref[:, h * SLOT:(h + 1) * SLOT] = jnp.concatenate([_head(do_ref, h), d_aux[:, aux]], axis=1)

    row = pl.BlockSpec((tm, ATT_W), lambda i: (i, 0))
    wide = pl.BlockSpec((tm, N_HEADS * SLOT), lambda i: (i, 0))
    const = lambda r, c: pl.BlockSpec((r, c), lambda i: (0, 0))
    return pl.pallas_call(
        body, name="fox_pack_bwd", grid=(SEQ // tm,),
        in_specs=[row, pl.BlockSpec((tm, LANE), lambda i: (i, 0)), row, row, row,
                  const(LANE, N_HEADS * SLOT), const(ATT_W, LANE), const(ATT_W, LANE)],
        out_specs=[wide] * 2, out_shape=[_sds((SEQ, N_HEADS * SLOT), BF16)] * 2,
        compiler_params=_params("parallel"),
    )(zm, f_cols, lse, o, do, _aux_matrices()[0], *_head_sums())


def _causal_pairs(key_major):
    nb = SEQ // FOX_T
    if key_major:
        pairs = [(i, j) for j in range(nb) for i in range(j, nb)]
    else:
        pairs = [(i, j) for i in range(nb) for j in range(i + 1)]
    return (jnp.array([p[0] for p in pairs], jnp.int32), jnp.array([p[1] for p in pairs], jnp.int32), len(pairs))


FOX_HALF = FOX_T // 2
FOX_FULL = ((slice(0, FOX_T), slice(0, FOX_T), None),)
FOX_DIAG = ((slice(0, FOX_HALF), slice(0, FOX_HALF), 0), (slice(FOX_HALF, FOX_T), slice(0, FOX_T), FOX_HALF))


def _causal_piece_mask(q_rows, k_rows, offset):
    shape = (q_rows.stop - q_rows.start, k_rows.stop - k_rows.start)
    row = lax.broadcasted_iota(jnp.int32, shape, 0)
    col = lax.broadcasted_iota(jnp.int32, shape, 1)
    return col <= row + offset


def _fox_fwd(q_slots, k_slots, v_slots):
    i_tab, j_tab, n_pairs = _causal_pairs(False)

    def body(i_tab, j_tab, q_ref, k_ref, v_ref, o_ref, lse_ref, m_s, acc_s):
        t = pl.program_id(1)
        i, j = i_tab[t], j_tab[t]

        @pl.when(j == 0)
        def _():
            m_s[...] = jnp.full_like(m_s, NEG_INF)
            acc_s[...] = jnp.zeros_like(acc_s)

        def step(pieces):
            jobs = [(h, piece) for h in range(FOX_FWD_HEADS) for piece in pieces]
            lanes = lambda h: slice(h * SLOT, (h + 1) * SLOT)
            scores = [lax.dot_general(q_ref[qr, lanes(h)], k_ref[kr, lanes(h)], NT_DIMS, preferred_element_type=F32)
                      for h, (qr, kr, _) in jobs]
            probs, alphas = [], []
            for idx, (h, (qr, kr, offset)) in enumerate(jobs):
                s = scores[idx]
                if offset is not None:
                    s = jnp.where(_causal_piece_mask(qr, kr, offset), s, NEG_INF)
                m_prev = m_s[h, qr, :]
                m_new = jnp.maximum(m_prev, jnp.max(s, axis=-1, keepdims=True))
                probs.append(jnp.exp(s - jnp.tile(m_new, (1, s.shape[1] // LANE))).astype(BF16))
                alphas.append(jnp.exp(m_prev - m_new))
                m_s[h, qr, :] = m_new
            for idx, (h, (qr, kr, _)) in enumerate(jobs):
                acc_s[h, qr, :] = alphas[idx] * acc_s[h, qr, :] + jnp.dot(
                    probs[idx], v_ref[kr, lanes(h)], preferred_element_type=F32)

        @pl.when(j < i)
        def _():
            step(FOX_FULL)

        @pl.when(j == i)
        def _():
            step(FOX_DIAG)
            outs, lses = [], []
            for h in range(FOX_FWD_HEADS):
                acc = acc_s[h]
                l = acc[:, HEAD_DIM:]
                outs.append(acc[:, :HEAD_DIM] / l)
                lses.append(m_s[h][:, :HEAD_DIM] + jnp.log(l))
            o_ref[...] = jnp.concatenate(outs, axis=1).astype(o_ref.dtype)
            lse_ref[...] = jnp.concatenate(lses, axis=1)

    qspec = pl.BlockSpec((FOX_T, FOX_FWD_HEADS * SLOT), lambda p, t, it, jt: (it[t], p))
    kspec = pl.BlockSpec((FOX_T, FOX_FWD_HEADS * SLOT), lambda p, t, it, jt: (jt[t], p))
    ospec = pl.BlockSpec((FOX_T, FOX_FWD_HEADS * HEAD_DIM), lambda p, t, it, jt: (it[t], p))
    return pl.pallas_call(
        body, name="fox_fwd",
        grid_spec=pltpu.PrefetchScalarGridSpec(
            num_scalar_prefetch=2, grid=(N_HEADS // FOX_FWD_HEADS, n_pairs),
            in_specs=[qspec, kspec, kspec], out_specs=[ospec, ospec],
            scratch_shapes=[pltpu.VMEM((FOX_FWD_HEADS, FOX_T, LANE), F32),
                            pltpu.VMEM((FOX_FWD_HEADS, FOX_T, SLOT), F32)]),
        out_shape=[_sds((SEQ, ATT_W), BF16), _sds((SEQ, ATT_W), F32)],
        compiler_params=_params("parallel", "arbitrary"),
    )(i_tab, j_tab, q_slots, k_slots, v_slots)


def _fox_bwd(q_slots, k_slots, v_slots, do_slots):
    i_tab, j_tab, n_pairs = _causal_pairs(True)

    def body(i_tab, j_tab, q_ref, k_ref, v_ref, do_ref, dq_ref, dk_ref, dv_ref):
        t = pl.program_id(1)
        i, j = i_tab[t], j_tab[t]

        @pl.when(t == 0)
        def _():
            dq_ref[...] = jnp.zeros_like(dq_ref)

        @pl.when(i == j)
        def _():
            dk_ref[...] = jnp.zeros_like(dk_ref)
            dv_ref[...] = jnp.zeros_like(dv_ref)

        def step(pieces):
            jobs = [(h, piece) for h in range(FOX_BWD_HEADS) for piece in pieces]
            lanes = lambda h: slice(h * SLOT, (h + 1) * SLOT)
            scores = [lax.dot_general(q_ref[qr, lanes(h)], k_ref[kr, lanes(h)], NT_DIMS, preferred_element_type=F32)
                      for h, (qr, kr, _) in jobs]
            dps = [lax.dot_general(do_ref[qr, lanes(h)], v_ref[kr, lanes(h)], NT_DIMS, preferred_element_type=F32)
                   for h, (qr, kr, _) in jobs]
            ps, dss = [], []
            for idx, (h, (qr, kr, offset)) in enumerate(jobs):
                p = jnp.exp(scores[idx])
                if offset is not None:
                    p = jnp.where(_causal_piece_mask(qr, kr, offset), p, 0.0)
                ps.append(p.astype(BF16))
                dss.append((p * dps[idx]).astype(BF16))
            for idx, (h, (qr, kr, _)) in enumerate(jobs):
                rows = pl.ds(pl.multiple_of(i * FOX_T + qr.start, FOX_HALF), qr.stop - qr.start)
                dv_ref[kr, lanes(h)] += lax.dot_general(ps[idx], do_ref[qr, lanes(h)], TN_DIMS,
                                                        preferred_element_type=F32)
                dk_ref[kr, lanes(h)] += lax.dot_general(dss[idx], q_ref[qr, lanes(h)], TN_DIMS,
                                                        preferred_element_type=F32)
                dq_ref[rows, lanes(h)] += jnp.dot(dss[idx], k_ref[kr, lanes(h)], preferred_element_type=F32)

        @pl.when(i > j)
        def _():
            step(FOX_FULL)

        @pl.when(i == j)
        def _():
            step(FOX_DIAG)

    qspec = pl.BlockSpec((FOX_T, FOX_BWD_HEADS * SLOT), lambda p, t, it, jt: (it[t], p))
    kspec = pl.BlockSpec((FOX_T, FOX_BWD_HEADS * SLOT), lambda p, t, it, jt: (jt[t], p))
    return pl.pallas_call(
        body, name="fox_bwd",
        grid_spec=pltpu.PrefetchScalarGridSpec(
            num_scalar_prefetch=2, grid=(N_HEADS // FOX_BWD_HEADS, n_pairs),
            in_specs=[qspec, kspec, kspec, qspec],
            out_specs=[pl.BlockSpec((SEQ, FOX_BWD_HEADS * SLOT), lambda p, t, it, jt: (0, p)), kspec, kspec]),
        out_shape=[_sds((SEQ, N_HEADS * SLOT), F32)] * 3,
        compiler_params=_params("arbitrary", "arbitrary"),
    )(i_tab, j_tab, q_slots, k_slots, v_slots, do_slots)


def _fox_unpack(dq_slots, dk_slots, dv_slots, dz, *, tm=512):
    def body(dq_ref, dk_ref, dv_ref, dz_in, o_ref, df_ref):
        lane = lax.broadcasted_iota(jnp.int32, (tm, LANE), 1)
        df = jnp.zeros((tm, LANE), F32)
        for h in range(N_HEADS):
            lo = h * SLOT
            for part, (ref, mult) in enumerate(((dq_ref, SCALE), (dk_ref, 1.0), (dv_ref, 1.0))):
                o_ref[:, part * ATT_W + h * HEAD_DIM:part * ATT_W + (h + 1) * HEAD_DIM] = (
                    ref[:, lo:lo + HEAD_DIM] * mult).astype(o_ref.dtype)
            rows = dq_ref[:, lo + HEAD_DIM:lo + HEAD_DIM + 1]
            cols = dk_ref[:, lo + HEAD_DIM + N_SPLIT:lo + HEAD_DIM + N_SPLIT + 1]
            df = jnp.where(lane == h, rows - cols, df)
        df_ref[...] = df

    wide = pl.BlockSpec((tm, N_HEADS * SLOT), lambda i: (i, 0))
    return pl.pallas_call(
        body, name="fox_unpack", grid=(SEQ // tm,), in_specs=[wide] * 3 + [ANY],
        out_specs=[pl.BlockSpec((tm, 3 * ATT_W), lambda i: (i, 0)), pl.BlockSpec((tm, LANE), lambda i: (i, 0))],
        out_shape=[_sds((SEQ, Z_MAIN), BF16), _sds((SEQ, LANE), F32)],
        input_output_aliases={3: 0},
        compiler_params=_params("parallel"),
    )(dq_slots, dk_slots, dv_slots, dz)


def _dil_bwd_prep(o, do, lse, *, tm=512):
    dilations = [d for _, d in DIL_PATTERNS]
    o_chunks = ATT_W // LANE

    def body(o_ref, do_ref, lse_ref, *rest):
        outs, (do_scr, lse_scr, dl_scr) = rest[:-3], rest[-3:]
        dov = do_ref[...].astype(F32)
        prod = o_ref[...].astype(F32) * dov
        lane = lax.broadcasted_iota(jnp.int32, (tm, LANE), 1)
        delta = jnp.zeros((tm, LANE), F32)
        for h in range(N_HEADS):
            delta = jnp.where(lane == h, jnp.sum(_head(prod, h), axis=1, keepdims=True), delta)
        for ch in range(o_chunks):
            do_scr[ch] = dov[:, ch * LANE:(ch + 1) * LANE]
        lse_scr[0] = lse_ref[...]
        dl_scr[0] = delta
        for k, d in enumerate(dilations):
            for scr, out in zip((do_scr, lse_scr, dl_scr), outs[3 * k:3 * k + 3]):
                _slabs_from_rows(scr, out, d)

    row = pl.BlockSpec((tm, ATT_W), lambda i: (i, 0))
    view = lambda d, w: pl.BlockSpec((tm // d, d * w), lambda i: (i, 0))
    outs = pl.pallas_call(
        body, name="dil_bwd_prep", grid=(SEQ // tm,),
        in_specs=[row, row, pl.BlockSpec((tm, LANE), lambda i: (i, 0))],
        out_specs=[view(d, w) for d in dilations for w in (ATT_W, LANE, LANE)],
        out_shape=[_sds((SEQ // d, d * w), t) for d in dilations for w, t in ((ATT_W, BF16), (LANE, F32), (LANE, F32))],
        scratch_shapes=[pltpu.VMEM((o_chunks, tm, LANE), F32), pltpu.VMEM((1, tm, LANE), F32),
                        pltpu.VMEM((1, tm, LANE), F32)],
        compiler_params=_params("parallel"),
    )(o, do, lse)
    return [outs[3 * k:3 * k + 3] for k in range(len(dilations))]


def _rope_tables():
    half = ROPE_DIM // 2
    inv_freq = np.float32(ROPE_THETA) ** (-np.arange(half, dtype=np.float32) * np.float32(2.0) / np.float32(ROPE_DIM))
    ang = np.arange(SEQ, dtype=np.float32)[:, None] * inv_freq.astype(np.float32)[None, :]
    cos, sin = jnp.asarray(np.cos(ang).astype(np.float32)), jnp.asarray(np.sin(ang).astype(np.float32))
    ones = jnp.ones((SEQ, HEAD_DIM - ROPE_DIM), F32)
    zeros = jnp.zeros((SEQ, HEAD_DIM - ROPE_DIM), F32)
    zh = jnp.zeros((SEQ, half), F32)
    c_tab = jnp.concatenate([cos, cos, ones], axis=1)
    a_tab = jnp.concatenate([-sin, zh, zeros], axis=1)
    b_tab = jnp.concatenate([zh, sin, zeros], axis=1)
    two = lambda t: jnp.concatenate([t, t], axis=1)
    return two(c_tab), two(a_tab), two(b_tab)


def _rotate(x, c_tab, a_tab, b_tab):
    return x * c_tab + pltpu.roll(x, LANE - ROPE_DIM // 2, 1) * a_tab + pltpu.roll(x, ROPE_DIM // 2, 1) * b_tab


def _rope_fwd(zm, tabs, *, tm=512):
    width = 3 * ATT_W
    dilations = [d for _, d in DIL_PATTERNS]

    def body(q_ref, k_ref, v_ref, c_ref, a_ref, b_ref, *rest):
        outs, scr = rest[:-1], rest[-1]
        per_part = ATT_W // LANE
        for part, (x_ref, mult) in enumerate(((q_ref, SCALE), (k_ref, 1.0))):
            for cc in range(per_part):
                sl = slice(cc * LANE, (cc + 1) * LANE)
                scr[part * per_part + cc] = _rotate(x_ref[:, sl].astype(F32), c_ref[...], a_ref[...], b_ref[...]) * mult
        for cc in range(per_part):
            scr[2 * per_part + cc] = v_ref[:, cc * LANE:(cc + 1) * LANE].astype(F32)
        for o_ref, d in zip(outs, dilations):
            for r in range(d):
                for ch in range(width // LANE):
                    o_ref[:, r * width + ch * LANE:r * width + (ch + 1) * LANE] = (
                        scr.at[ch][pl.ds(r, tm // d, stride=d), :].astype(o_ref.dtype))

    tab = pl.BlockSpec((tm, LANE), lambda i: (i, 0))
    col = lambda b: pl.BlockSpec((tm, ATT_W), lambda i: (i, b))
    return pl.pallas_call(
        body, name="rope_fwd", grid=(SEQ // tm,),
        in_specs=[col(3), col(4), col(5), tab, tab, tab],
        out_specs=[pl.BlockSpec((tm // d, d * width), lambda i: (i, 0)) for d in dilations],
        out_shape=[_sds((SEQ // d, d * width), BF16) for d in dilations],
        scratch_shapes=[pltpu.VMEM((width // LANE, tm, LANE), F32)],
        compiler_params=_params("parallel"),
    )(zm, zm, zm, *tabs)


def _dil_grad_combine(dqs, dks, dvs, tabs, dz, *, tm=256):
    dilations = [d for _, d in DIL_PATTERNS]
    chunks = ATT_W // LANE

    def body(*refs):
        groups = (refs[0:3], refs[3:6], refs[6:9])
        c_ref, a_ref, b_ref, _, o_ref, scr = refs[9:]

        def total(part, cc):
            acc = None
            for g, (ref, d) in enumerate(zip(groups[part], dilations)):
                term = ref[:, cc * LANE:(cc + 1) * LANE].astype(F32) if d == 1 else scr[part, g, cc]
                acc = term if acc is None else acc + term
            return acc

        for part in range(3):
            for g, (ref, d) in enumerate(zip(groups[part], dilations)):
                if d > 1:
                    _rows_from_slabs(ref, scr.at[part, g], d)
        for cc in range(chunks):
            for part in range(2):
                o_ref[:, part * ATT_W + cc * LANE:part * ATT_W + (cc + 1) * LANE] = _rotate(
                    total(part, cc), c_ref[...], -a_ref[...], -b_ref[...]).astype(o_ref.dtype)
            o_ref[:, 2 * ATT_W + cc * LANE:2 * ATT_W + (cc + 1) * LANE] = total(2, cc).astype(o_ref.dtype)

    view = lambda d: pl.BlockSpec((tm // d, d * ATT_W), lambda i: (i, 0))
    tab = pl.BlockSpec((tm, LANE), lambda i: (i, 0))
    return pl.pallas_call(
        body, name="dil_grad_combine", grid=(SEQ // tm,),
        in_specs=[view(d) for d in dilations] * 3 + [tab] * 3 + [ANY],
        out_specs=pl.BlockSpec((tm, 3 * ATT_W), lambda i: (i, 1)),
        out_shape=_sds((SEQ, Z_MAIN), BF16),
        input_output_aliases={12: 0},
        scratch_shapes=[pltpu.VMEM((3, len(dilations), chunks, tm, LANE), F32)],
        compiler_params=_params("parallel"),
    )(*dqs, *dks, *dvs, *tabs, dz)


def _dil_valid(n):
    qi = lax.broadcasted_iota(jnp.int32, (DIL_BLK, 2 * DIL_BLK), 0)
    ki = lax.broadcasted_iota(jnp.int32, (DIL_BLK, 2 * DIL_BLK), 1)
    dist = qi + DIL_BLK - ki
    return (dist >= 0) & (dist <= DIL_BLK) & ((n > 0) | (ki >= DIL_BLK))


def _dil_fwd(qkv_v, d):
    length = SEQ // d
    nb = length // DIL_BLK

    def body(q_ref, kp_ref, kc_ref, vp_ref, vc_ref, o_ref, lse_ref):
        m_step = pl.program_id(1)
        lane = lax.broadcasted_iota(jnp.int32, (DIL_BLK, LANE), 1)
        jobs = [(sub, h) for sub in range(2) for h in range(N_HEADS)]
        rows = lambda sub: slice(sub * DIL_BLK, (sub + 1) * DIL_BLK)
        cols = lambda h: slice(h * HEAD_DIM, (h + 1) * HEAD_DIM)

        def keys(prev_ref, cur_ref, sub, h):
            before = prev_ref[:, cols(h)] if sub == 0 else cur_ref[rows(0), cols(h)]
            return jnp.concatenate([before, cur_ref[rows(sub), cols(h)]], axis=0)

        scores = [lax.dot_general(q_ref[rows(sub), cols(h)], keys(kp_ref, kc_ref, sub, h), NT_DIMS,
                                  preferred_element_type=F32) for sub, h in jobs]
        ok = [_dil_valid(m_step), _dil_valid(1)]
        probs, inv_l, lse_all = [], [], [jnp.zeros((DIL_BLK, LANE), F32)] * 2
        for idx, (sub, h) in enumerate(jobs):
            s = jnp.where(ok[sub], scores[idx], NEG_INF)
            m = jnp.max(s, axis=-1, keepdims=True)
            p = jnp.exp(s - m)
            l = jnp.sum(p, axis=-1, keepdims=True)
            probs.append(p.astype(BF16))
            inv_l.append(1.0 / l)
            lse_all[sub] = jnp.where(lane == h, m + jnp.log(l), lse_all[sub])
        outs = [jnp.dot(probs[idx], keys(vp_ref, vc_ref, sub, h), preferred_element_type=F32) * inv_l[idx]
                for idx, (sub, h) in enumerate(jobs)]
        for sub in range(2):
            o_ref[rows(sub), :] = jnp.concatenate(outs[sub * N_HEADS:(sub + 1) * N_HEADS], axis=1).astype(o_ref.dtype)
            lse_ref[rows(sub), :] = lse_all[sub]

    pair = lambda f: pl.BlockSpec((2 * DIL_BLK, ATT_W), f)
    one = lambda f: pl.BlockSpec((DIL_BLK, ATT_W), f)
    before = lambda m: jnp.maximum(2 * m - 1, 0)
    o, lse = pl.pallas_call(
        body, name=f"dil_fwd_d{d}", grid=(d, nb // 2),
        in_specs=[pair(lambda r, m: (m, 3 * r)),
                  one(lambda r, m: (before(m), 3 * r + 1)), pair(lambda r, m: (m, 3 * r + 1)),
                  one(lambda r, m: (before(m), 3 * r + 2)), pair(lambda r, m: (m, 3 * r + 2))],
        out_specs=[pair(lambda r, m: (m, r)), pl.BlockSpec((2 * DIL_BLK, LANE), lambda r, m: (m, r))],
        out_shape=[_sds((length, d * ATT_W), BF16), _sds((length, d * LANE), F32)],
        compiler_params=_params("parallel", "arbitrary"),
    )(qkv_v, qkv_v, qkv_v, qkv_v, qkv_v)
    return o, lse


def _rows_from_slabs(view_ref, scr, d):
    chunks, rows = scr.shape[0], scr.shape[1]
    for r in range(d):
        for ch in range(chunks):
            lo = (r * chunks + ch) * LANE
            scr.at[ch][pl.ds(r, rows // d, stride=d), :] = view_ref[:, lo:lo + LANE].astype(F32)


def _slabs_from_rows(scr, view_ref, d):
    chunks, rows = scr.shape[0], scr.shape[1]
    for r in range(d):
        for ch in range(chunks):
            lo = (r * chunks + ch) * LANE
            view_ref[:, lo:lo + LANE] = scr.at[ch][pl.ds(r, rows // d, stride=d), :].astype(view_ref.dtype)


def _dil_merge(os_, lses, *, tm=512):
    dilations = [d for _, d in DIL_PATTERNS]
    o_chunks = ATT_W // LANE

    def body(o0, o1, o2, l0, l1, l2, y_ref, lse_ref, o_scr, l_scr):
        os_nat, ls = [], []
        for g, (o_ref, l_ref, d) in enumerate(zip((o0, o1, o2), (l0, l1, l2), dilations)):
            if d == 1:
                os_nat.append(o_ref[...].astype(F32))
                ls.append(l_ref[...])
            else:
                _rows_from_slabs(o_ref, o_scr.at[g], d)
                _rows_from_slabs(l_ref, l_scr.at[g], d)
                os_nat.append(jnp.concatenate([o_scr[g, ch] for ch in range(o_chunks)], axis=1))
                ls.append(l_scr[g, 0])
        m = jnp.maximum(jnp.maximum(ls[0], ls[1]), ls[2])
        es = [jnp.exp(l - m) for l in ls]
        tot = es[0] + es[1] + es[2]
        lse_ref[...] = m + jnp.log(tot)
        alphas = [e / tot for e in es]
        outs = []
        for h in range(N_HEADS):
            acc = None
            for g in range(3):
                term = alphas[g][:, h:h + 1] * _head(os_nat[g], h)
                acc = term if acc is None else acc + term
            outs.append(acc)
        y_ref[...] = jnp.concatenate(outs, axis=1).astype(y_ref.dtype)

    row = pl.BlockSpec((tm, ATT_W), lambda i: (i, 0))
    vec = pl.BlockSpec((tm, LANE), lambda i: (i, 0))
    view = lambda d, w: pl.BlockSpec((tm // d, d * w), lambda i: (i, 0))
    return pl.pallas_call(
        body, name="dil_merge", grid=(SEQ // tm,),
        in_specs=[view(d, ATT_W) for d in dilations] + [view(d, LANE) for d in dilations], out_specs=[row, vec],
        out_shape=[_sds((SEQ, ATT_W), BF16), _sds((SEQ, LANE), F32)],
        scratch_shapes=[pltpu.VMEM((3, o_chunks, tm, LANE), F32), pltpu.VMEM((3, 1, tm, LANE), F32)],
        compiler_params=_params("parallel"),
    )(*os_, *lses)


def _dil_bwd(qkv_v, do_v, lse_v, dl_v, d):
    length = SEQ // d
    nb = length // DIL_BLK
    n_steps = nb // 2

    def body(q_ref, kp_ref, kc_ref, vp_ref, vc_ref, lse_ref, dl_ref, do_ref, dq_ref, dk_ref, dv_ref, dk_s, dv_s):
        m_step = pl.program_id(1)

        @pl.when(m_step == 0)
        def _():
            dk_s[...] = jnp.zeros_like(dk_s)
            dv_s[...] = jnp.zeros_like(dv_s)

        jobs = [(sub, h) for sub in range(2) for h in range(N_HEADS)]
        rows = lambda sub: slice(sub * DIL_BLK, (sub + 1) * DIL_BLK)
        cols = lambda h: slice(h * HEAD_DIM, (h + 1) * HEAD_DIM)

        def keys(prev_ref, cur_ref, sub, h):
            before = prev_ref[:, cols(h)] if sub == 0 else cur_ref[rows(0), cols(h)]
            return jnp.concatenate([before, cur_ref[rows(sub), cols(h)]], axis=0)

        kks = [keys(kp_ref, kc_ref, sub, h) for sub, h in jobs]
        scores = [lax.dot_general(q_ref[rows(sub), cols(h)], kks[idx], NT_DIMS, preferred_element_type=F32)
                  for idx, (sub, h) in enumerate(jobs)]
        dps = [lax.dot_general(do_ref[rows(sub), cols(h)], keys(vp_ref, vc_ref, sub, h), NT_DIMS,
                               preferred_element_type=F32) for sub, h in jobs]
        ok = [_dil_valid(m_step), _dil_valid(1)]
        ps, dss = [], []
        for idx, (sub, h) in enumerate(jobs):
            p = jnp.where(ok[sub], jnp.exp(scores[idx] - lse_ref[rows(sub), h:h + 1]), 0.0)
            ps.append(p.astype(BF16))
            dss.append((p * (dps[idx] - dl_ref[rows(sub), h:h + 1])).astype(BF16))
        dqs = [jnp.dot(dss[idx], kks[idx], preferred_element_type=F32) * SCALE for idx in range(len(jobs))]
        dkks = [lax.dot_general(dss[idx], q_ref[rows(sub), cols(h)], TN_DIMS, preferred_element_type=F32)
                for idx, (sub, h) in enumerate(jobs)]
        dvvs = [lax.dot_general(ps[idx], do_ref[rows(sub), cols(h)], TN_DIMS, preferred_element_type=F32)
                for idx, (sub, h) in enumerate(jobs)]
        for sub in range(2):
            dq_ref[rows(sub), :] = jnp.concatenate(dqs[sub * N_HEADS:(sub + 1) * N_HEADS], axis=1).astype(dq_ref.dtype)
        base = m_step * (2 * DIL_BLK)
        blocks = [pl.ds(pl.multiple_of(jnp.maximum(base - DIL_BLK, 0), DIL_BLK), DIL_BLK),
                  pl.ds(pl.multiple_of(base, DIL_BLK), DIL_BLK),
                  pl.ds(pl.multiple_of(base + DIL_BLK, DIL_BLK), DIL_BLK)]
        for acc, parts in ((dk_s, dkks), (dv_s, dvvs)):
            top = lambda sub: jnp.concatenate([parts[sub * N_HEADS + h][:DIL_BLK] for h in range(N_HEADS)], axis=1)
            bottom = lambda sub: jnp.concatenate([parts[sub * N_HEADS + h][DIL_BLK:] for h in range(N_HEADS)], axis=1)
            acc[blocks[0], :] += top(0)
            acc[blocks[1], :] += bottom(0) + top(1)
            acc[blocks[2], :] += bottom(1)

        @pl.when(m_step == n_steps - 1)
        def _():
            dk_ref[...] = dk_s[...].astype(dk_ref.dtype)
            dv_ref[...] = dv_s[...].astype(dv_ref.dtype)

    pair = lambda f: pl.BlockSpec((2 * DIL_BLK, ATT_W), f)
    one = lambda f: pl.BlockSpec((DIL_BLK, ATT_W), f)
    vec = lambda f: pl.BlockSpec((2 * DIL_BLK, LANE), f)
    whole = pl.BlockSpec((length, ATT_W), lambda r, m: (0, r))
    before = lambda m: jnp.maximum(2 * m - 1, 0)
    outs = pl.pallas_call(
        body, name=f"dil_bwd_d{d}", grid=(d, n_steps),
        in_specs=[pair(lambda r, m: (m, 3 * r)),
                  one(lambda r, m: (before(m), 3 * r + 1)), pair(lambda r, m: (m, 3 * r + 1)),
                  one(lambda r, m: (before(m), 3 * r + 2)), pair(lambda r, m: (m, 3 * r + 2)),
                  vec(lambda r, m: (m, r)), vec(lambda r, m: (m, r)), pair(lambda r, m: (m, r))],
        out_specs=[pair(lambda r, m: (m, r)), whole, whole],
        out_shape=[_sds((length, d * ATT_W), BF16)] * 3,
        scratch_shapes=[pltpu.VMEM((length, ATT_W), F32), pltpu.VMEM((length, ATT_W), F32)],
        compiler_params=_params("arbitrary", "arbitrary"),
    )(qkv_v, qkv_v, qkv_v, qkv_v, qkv_v, lse_v, dl_v, do_v)
    return outs


def _sigmoid(x):
    return 1.0 / (1.0 + jnp.exp(-x))


def _mix_fwd(ya, yb, w_oa, w_ob, zm, *, tm=512):
    def body(ya_ref, yb_ref, wa_ref, wb_ref, ga_ref, gb_ref, pa_ref, pb_ref, mix_ref):
        pa = jnp.dot(ya_ref[...], wa_ref[...], preferred_element_type=F32)
        pb = jnp.dot(yb_ref[...], wb_ref[...], preferred_element_type=F32)
        pa_ref[...] = pa.astype(pa_ref.dtype)
        pb_ref[...] = pb.astype(pb_ref.dtype)
        mix_ref[...] = (_sigmoid(ga_ref[...].astype(F32)) * pa + _sigmoid(gb_ref[...].astype(F32)) * pb
                        ).astype(mix_ref.dtype)

    row = pl.BlockSpec((tm, ATT_W), lambda i: (i, 0))
    wsp = pl.BlockSpec((ATT_W, D_MODEL), lambda i: (0, 0))
    wide = pl.BlockSpec((tm, D_MODEL), lambda i: (i, 0))
    return pl.pallas_call(
        body, name="mix_fwd", grid=(SEQ // tm,),
        in_specs=[row, row, wsp, wsp, pl.BlockSpec((tm, D_MODEL), lambda i: (i, 3)),
                  pl.BlockSpec((tm, D_MODEL), lambda i: (i, 4))],
        out_specs=[wide] * 3, out_shape=[_sds((SEQ, D_MODEL), BF16)] * 3,
        compiler_params=_params("parallel"),
    )(ya, yb, w_oa, w_ob, zm, zm)


def _gate_bwd(dmix, zm, p, gate_block, dz, *, name, tm=512):
    def body(dm_ref, g_ref, p_ref, *rest):
        dp_ref, dz_ref = rest[-2], rest[-1]
        dm = dm_ref[...].astype(F32)
        s = _sigmoid(g_ref[...].astype(F32))
        dp_ref[...] = (dm * s).astype(dp_ref.dtype)
        dz_ref[...] = (dm * p_ref[...].astype(F32) * s * (1.0 - s)).astype(dz_ref.dtype)

    wide = pl.BlockSpec((tm, D_MODEL), lambda i: (i, 0))
    gate = pl.BlockSpec((tm, D_MODEL), lambda i: (i, gate_block))
    extra = [] if dz is None else [dz]
    return pl.pallas_call(
        body, name=name, grid=(SEQ // tm,),
        in_specs=[wide, gate, wide] + [ANY] * len(extra),
        out_specs=[wide, gate],
        out_shape=[_sds((SEQ, D_MODEL), BF16), _sds((SEQ, Z_MAIN), BF16)],
        input_output_aliases={3: 1} if extra else {},
        compiler_params=_params("parallel"),
    )(dmix, zm, p, *extra)


def _out_fwd(mixed, w_out, x, g_post, g_pre, *, tm=512):
    def body(m_ref, w_ref, x_ref, gp_ref, gn_ref, y_ref, x2_ref, h_ref):
        y = jnp.dot(m_ref[...], w_ref[...], preferred_element_type=F32)
        y_ref[...] = y
        r = lax.rsqrt(jnp.mean(y * y, axis=-1, keepdims=True) + RMS_EPS)
        x2 = x_ref[...] + y * r * gp_ref[...]
        x2_ref[...] = x2
        r2 = lax.rsqrt(jnp.mean(x2 * x2, axis=-1, keepdims=True) + RMS_EPS)
        h_ref[...] = (x2 * r2 * gn_ref[...]).astype(h_ref.dtype)

    row = pl.BlockSpec((tm, D_MODEL), lambda i: (i, 0))
    vec = pl.BlockSpec((1, D_MODEL), lambda i: (0, 0))
    return pl.pallas_call(
        body, name="out_fwd", grid=(SEQ // tm,),
        in_specs=[row, pl.BlockSpec((D_MODEL, D_MODEL), lambda i: (0, 0)), row, vec, vec],
        out_specs=[row] * 3,
        out_shape=[_sds((SEQ, D_MODEL), F32), _sds((SEQ, D_MODEL), F32), _sds((SEQ, D_MODEL), BF16)],
        compiler_params=_params("parallel"),
    )(mixed, w_out, x, g_post, g_pre)


FFN_TM = 2048
FFN_HALF = 256
FFN_TN = 2 * FFN_HALF
FFN_NJ = D_FF // FFN_HALF
FFN_GROUP = 2 * SUBLANE


def _ffn_interleave(t):
    lead = t.shape[:-1]
    return jnp.swapaxes(t.reshape(*lead, 2, FFN_NJ, FFN_HALF), -3, -2).reshape(*lead, 2 * D_FF)


def _ffn_deinterleave(t):
    lead = t.shape[:-1]
    return jnp.swapaxes(t.reshape(*lead, FFN_NJ, 2, FFN_HALF), -3, -2).reshape(*lead, 2 * D_FF)


W_IN_SHARD = (Z_MAIN + N_HEADS) // N_DEV
FORGET_LO = 3 * ATT_W


def _w_in_from_shards(shards, *, tm=256):
    def columns(g_ref, lo, width):
        p, off = divmod(lo, W_IN_SHARD)
        if off + width <= W_IN_SHARD:
            return g_ref[p, :, off:off + width]
        first = W_IN_SHARD - off
        return jnp.concatenate([g_ref[p, :, off:], g_ref[p + 1, :, :width - first]], axis=1)

    def body(g_ref, main_ref, f_ref):
        for t in range(Z_MAIN // LANE):
            lo = t * LANE
            main_ref[:, lo:lo + LANE] = columns(g_ref, lo if lo < FORGET_LO else lo + N_HEADS, LANE)
        f_ref[...] = jnp.concatenate([columns(g_ref, FORGET_LO, N_HEADS),
                                      jnp.zeros((tm, F_PAD - N_HEADS), f_ref.dtype)], axis=1)

    return pl.pallas_call(
        body, name="w_in_from_shards", grid=(D_MODEL // tm,),
        in_specs=[pl.BlockSpec((N_DEV, tm, W_IN_SHARD), lambda i: (0, i, 0))],
        out_specs=[pl.BlockSpec((tm, Z_MAIN), lambda i: (i, 0)), pl.BlockSpec((tm, F_PAD), lambda i: (i, 0))],
        out_shape=[_sds((D_MODEL, Z_MAIN), shards.dtype), _sds((D_MODEL, F_PAD), shards.dtype)],
        compiler_params=_params("parallel"),
    )(shards)


W_UP_SHARD = 2 * D_FF // N_DEV


def _w_up_lane_tile(k):
    block = k // 2
    return (2 * (block % FFN_NJ) + block // FFN_NJ) * FFN_HALF + (k % 2) * LANE


def _w_up_from_shards(shards, *, tm=256):
    def body(g_ref, o_ref):
        for k in range(2 * D_FF // LANE):
            p, off = divmod(k * LANE, W_UP_SHARD)
            if off + LANE <= W_UP_SHARD:
                tile = g_ref[p, :, off:off + LANE]
            else:
                tile = jnp.concatenate([g_ref[p, :, off:], g_ref[p + 1, :, :off + LANE - W_UP_SHARD]], axis=1)
            dst = _w_up_lane_tile(k)
            o_ref[:, dst:dst + LANE] = tile

    return pl.pallas_call(
        body, name="w_up_from_shards", grid=(D_MODEL // tm,),
        in_specs=[pl.BlockSpec((N_DEV, tm, W_UP_SHARD), lambda i: (0, i, 0))],
        out_specs=pl.BlockSpec((tm, 2 * D_FF), lambda i: (i, 0)),
        out_shape=_sds((D_MODEL, 2 * D_FF), shards.dtype),
        compiler_params=_params("parallel"),
    )(shards)


def _w_up_to_shards(t, *, tm=256):
    def body(x_ref, o_ref):
        for p in range(N_DEV):
            for q in range(-(-W_UP_SHARD // LANE)):
                width = min(LANE, W_UP_SHARD - q * LANE)
                k, off = divmod(p * W_UP_SHARD + q * LANE, LANE)
                src = _w_up_lane_tile(k)
                if off == 0:
                    tile = x_ref[:, src:src + width]
                else:
                    tile = x_ref[:, src + off:src + LANE]
                    if width > LANE - off:
                        nxt = _w_up_lane_tile(k + 1)
                        tile = jnp.concatenate([tile, x_ref[:, nxt:nxt + width - (LANE - off)]], axis=1)
                o_ref[p, :, q * LANE:q * LANE + width] = tile

    return pl.pallas_call(
        body, name="w_up_to_shards", grid=(D_MODEL // tm,),
        in_specs=[pl.BlockSpec((tm, 2 * D_FF), lambda i: (i, 0))],
        out_specs=pl.BlockSpec((N_DEV, tm, W_UP_SHARD), lambda i: (0, i, 0)),
        out_shape=_sds((N_DEV, D_MODEL, W_UP_SHARD), t.dtype),
        compiler_params=_params("parallel"),
    )(t)


def _gelu_parts(a):
    c = math.sqrt(2.0 / math.pi)
    a2 = a * a
    t = jnp.tanh((c * a) * (1.0 + 0.044715 * a2))
    half_a, one_t = 0.5 * a, 1.0 + t
    gelu = half_a * one_t
    dgelu = 0.5 * one_t + half_a * (1.0 - t * t) * (c + (3.0 * 0.044715 * c) * a2)
    return gelu, dgelu


def _row_masks(down):
    row = lax.broadcasted_iota(jnp.int32, (SUBLANE, FFN_TN), 0)
    return (row < 1, row < 2) if down else (row >= SUBLANE - 1, row >= SUBLANE - 2)


def _rolled(x, down):
    return (pltpu.roll(x, 1, 0), pltpu.roll(x, 2, 0)) if down else (
        pltpu.roll(x, SUBLANE - 1, 0), pltpu.roll(x, SUBLANE - 2, 0))


def _shifted(cur_rolled, neighbour_rolled, masks):
    return (jnp.where(masks[0], neighbour_rolled[0], cur_rolled[0]),
            jnp.where(masks[1], neighbour_rolled[1], cur_rolled[1]))


def _conv_consts(w_ref, b_ref):
    shape = (SUBLANE, FFN_TN)
    return [jnp.broadcast_to(w_ref[k:k + 1, :], shape) for k in range(3)] + [jnp.broadcast_to(b_ref[...], shape)]


def _ffn_mid_fwd(u, conv_w, conv_b):
    per = FFN_TM // SUBLANE

    def body(u_ref, h_ref, w_ref, b_ref, m_ref, ab_ref):
        live = (pl.program_id(1) > 0).astype(F32)
        w0, w1, w2, bias = _conv_consts(w_ref, b_ref)
        masks = _row_masks(True)

        def group(g, above):
            rows = pl.ds(pl.multiple_of(g * FFN_GROUP, FFN_GROUP), FFN_GROUP)
            x = u_ref[rows, :].astype(F32)
            convs = []
            for c in range(2):
                cur = x[c * SUBLANE:(c + 1) * SUBLANE]
                cur_rolled = _rolled(cur, True)
                s1, s2 = _shifted(cur_rolled, above, masks)
                convs.append(w0 * s2 + w1 * s1 + w2 * cur + bias)
                above = cur_rolled
            y = jnp.concatenate(convs, axis=0)
            ab_ref[rows, :] = y.astype(ab_ref.dtype)
            m_ref[rows, :] = (_gelu_parts(y[:, :FFN_HALF])[0] * y[:, FFN_HALF:]).astype(m_ref.dtype)
            return above

        lax.fori_loop(0, FFN_TM // (2 * FFN_GROUP), lambda g2, carry: group(2 * g2 + 1, group(2 * g2, carry)),
                      _rolled(h_ref[...].astype(F32) * live, True))

    blk = pl.BlockSpec((FFN_TM, FFN_TN), lambda j, i: (i, j))
    return pl.pallas_call(
        body, name="ffn_mid_fwd", grid=(FFN_NJ, SEQ // FFN_TM),
        in_specs=[blk, pl.BlockSpec((SUBLANE, FFN_TN), lambda j, i: (jnp.maximum(i * per - 1, 0), j)),
                  pl.BlockSpec((3, FFN_TN), lambda j, i: (0, j)), pl.BlockSpec((1, FFN_TN), lambda j, i: (0, j))],
        out_specs=[pl.BlockSpec((FFN_TM, FFN_HALF), lambda j, i: (i, j)), blk],
        out_shape=[_sds((SEQ, D_FF), BF16), _sds((SEQ, 2 * D_FF), BF16)],
        compiler_params=_params("parallel", "arbitrary"),
    )(u, u, conv_w, conv_b)


def _ffn_mid_bwd(dm, u, ab, conv_w):
    nrow = SEQ // FFN_TM
    n_groups = FFN_TM // FFN_GROUP

    def body(dm_ref, u_ref, ab_ref, w_ref, du_ref, gw_ref, gb_ref, c_s):
        @pl.when(pl.program_id(1) == 0)
        def _():
            c_s[...] = jnp.zeros_like(c_s)
            gw_ref[...] = jnp.zeros_like(gw_ref)
            gb_ref[...] = jnp.zeros_like(gb_ref)

        taps = [jnp.broadcast_to(w_ref[k:k + 1, :], (SUBLANE, FFN_TN)) for k in range(3)]
        masks = _row_masks(False)

        def group(t, carry):
            below, acc = carry
            rows = pl.ds(pl.multiple_of((n_groups - 1 - t) * FFN_GROUP, FFN_GROUP), FFN_GROUP)
            x, y, dmv = u_ref[rows, :].astype(F32), ab_ref[rows, :].astype(F32), dm_ref[rows, :].astype(F32)
            gelu, dgelu = _gelu_parts(y[:, :FFN_HALF])
            d = jnp.concatenate([dmv * y[:, FFN_HALF:] * dgelu, dmv * gelu], axis=1)
            acc, pre = list(acc), [None, None]
            for c in (1, 0):
                sl = slice(c * SUBLANE, (c + 1) * SUBLANE)
                cur, xs = d[sl], x[sl]
                cur_rolled = _rolled(cur, False)
                up1, up2 = _shifted(cur_rolled, below, masks)
                acc = [acc[0] + up2 * xs, acc[1] + up1 * xs, acc[2] + cur * xs, acc[3] + cur]
                pre[c] = taps[2] * cur + taps[1] * up1 + taps[0] * up2
                below = cur_rolled
            du_ref[rows, :] = jnp.concatenate(pre, axis=0).astype(du_ref.dtype)
            return below, tuple(acc)

        zeros = jnp.zeros((SUBLANE, FFN_TN), F32)
        below, acc = lax.fori_loop(0, n_groups // 2, lambda t2, carry: group(2 * t2 + 1, group(2 * t2, carry)),
                                   (_rolled(c_s[...], False), (zeros,) * 4))
        c_s[...] = pltpu.roll(below[0], 1, 0)
        for k in range(3):
            gw_ref[k:k + 1, :] += jnp.sum(acc[k], axis=0, keepdims=True)
        gb_ref[...] += jnp.sum(acc[3], axis=0, keepdims=True)

    blk = pl.BlockSpec((FFN_TM, FFN_TN), lambda j, i: (nrow - 1 - i, j))
    return pl.pallas_call(
        body, name="ffn_mid_bwd", grid=(FFN_NJ, nrow),
        in_specs=[pl.BlockSpec((FFN_TM, FFN_HALF), lambda j, i: (nrow - 1 - i, j)), blk, blk,
                  pl.BlockSpec((3, FFN_TN), lambda j, i: (0, j))],
        out_specs=[blk, pl.BlockSpec((3, FFN_TN), lambda j, i: (0, j)), pl.BlockSpec((1, FFN_TN), lambda j, i: (0, j))],
        out_shape=[_sds((SEQ, 2 * D_FF), BF16), _sds((3, 2 * D_FF), F32), _sds((1, 2 * D_FF), F32)],
        scratch_shapes=[pltpu.VMEM((SUBLANE, FFN_TN), F32)],
        compiler_params=_params("parallel", "arbitrary"),
    )(dm, u, ab, conv_w)


def _down_fwd(m, w_down, x2, g_post, target, *, tm=512):
    def body(m_ref, w_ref, x2_ref, g_ref, t_ref, dout_ref, dy_ref, gg_ref, loss_ref):
        @pl.when(pl.program_id(0) == 0)
        def _():
            gg_ref[...] = jnp.zeros_like(gg_ref)
            loss_ref[...] = jnp.zeros_like(loss_ref)

        y = jnp.dot(m_ref[...], w_ref[...], preferred_element_type=F32)
        r = lax.rsqrt(jnp.mean(y * y, axis=-1, keepdims=True) + RMS_EPS)
        yn = y * r
        diff = (x2_ref[...] + yn * g_ref[...]) - t_ref[...]
        loss_ref[...] += jnp.sum(diff * diff)
        dout = diff * (1.0 / D_MODEL)
        dout_ref[...] = dout
        gg_ref[...] += jnp.sum(dout * yn, axis=0, keepdims=True)
        dn = dout * g_ref[...]
        dy_ref[...] = (r * (dn - yn * jnp.mean(dn * yn, axis=-1, keepdims=True))).astype(dy_ref.dtype)

    row = pl.BlockSpec((tm, D_MODEL), lambda i: (i, 0))
    vec = pl.BlockSpec((1, D_MODEL), lambda i: (0, 0))
    return pl.pallas_call(
        body, name="down_fwd", grid=(SEQ // tm,),
        in_specs=[pl.BlockSpec((tm, D_FF), lambda i: (i, 0)), pl.BlockSpec((D_FF, D_MODEL), lambda i: (0, 0)),
                  row, vec, row],
        out_specs=[row, row, vec, pl.BlockSpec((1, LANE), lambda i: (0, 0))],
        out_shape=[_sds((SEQ, D_MODEL), F32), _sds((SEQ, D_MODEL), BF16), _sds((1, D_MODEL), F32),
                   _sds((1, LANE), F32)],
        compiler_params=_params("arbitrary"),
    )(m, w_down, x2, g_post, target)


def _local_step(x, target, w_main, w_f, b_forget, conv_b, g_pre_mix, g_post_mix, g_pre_ffn, g_post_ffn,
                late_weights, ffn_grads_ready, proj_grads_ready, mixer_grads_ready):
    mm = _matmul
    tabs = _rope_tables()

    h1 = _rms_fwd(x, g_pre_mix, name="rms_pre_mix")
    zm = mm(h1, w_main, out_dtype=BF16, tm=2048, tn=512, tk=1024, name="in_proj")
    zf = mm(h1, w_f, out_dtype=F32, tm=2048, tn=F_PAD, tk=1024, name="in_proj_forget")
    f_row, sg_row = _fox_prep(zf[:, :N_HEADS].T, b_forget.reshape(N_HEADS, 1))
    f_cols = jnp.pad(f_row.T, ((0, 0), (0, LANE - N_HEADS)))
    q_slots, k_slots, v_slots = _fox_pack_fwd(zm, f_cols)
    ya, lse_a = _fox_fwd(q_slots, k_slots, v_slots)
    qkv_d = dict(zip([d for _, d in DIL_PATTERNS], _rope_fwd(zm, tabs)))
    dil = [_dil_fwd(qkv_d[d], d) for _, d in DIL_PATTERNS]
    yb, lse_b = _dil_merge([o for o, _ in dil], [l for _, l in dil])
    w_oa, w_ob, w_out, w_up, conv_w, w_down = late_weights(yb)
    pa, pb, mixed = _mix_fwd(ya, yb, w_oa, w_ob, zm)
    y1, x2, h2 = _out_fwd(mixed, w_out, x, g_post_mix, g_pre_ffn)
    u = mm(h2, w_up, out_dtype=BF16, tm=2048, tn=512, tk=1024, name="up_proj")
    m, ab = _ffn_mid_fwd(u, conv_w, _ffn_interleave(conv_b))
    dout, dy2, gg_post_ffn, sq_err = _down_fwd(m, w_down, x2, g_post_ffn, target)

    g_w_down = mm(m, dy2, ta=True, out_dtype=BF16, tm=D_FF // 2, tn=1024, tk=2048, name="grad_w_down")
    dm = mm(dy2, w_down, tb=True, out_dtype=BF16, tm=2048, tn=D_FF // 2, tk=1024, name="d_ffn_mid")
    du, g_conv_w, g_conv_b = _ffn_mid_bwd(dm, u, ab, conv_w)
    g_w_up = mm(h2, du, ta=True, out_dtype=BF16, tm=1024, tn=D_FF // 2, tk=2048, name="grad_w_up")
    tok = ffn_grads_ready(dict(w_down=g_w_down, w_up_blocks=g_w_up, conv_w=_ffn_deinterleave(g_conv_w)))
    dh2 = mm(du, w_up, tb=True, out_dtype=BF16, tm=512, tn=1024, tk=2 * D_FF, name="d_h2")

    dx2, dy1, gg_pre_ffn, gg_post_mix = _rms_pair_bwd([dh2], x2, g_pre_ffn, dout, y1, g_post_mix + tok)
    g_w_out = mm(mixed, dy1, ta=True, out_dtype=BF16, tm=1024, tn=1024, tk=2048, name="grad_w_out")
    dmix = mm(dy1, w_out, tb=True, out_dtype=BF16, tm=2048, tn=1024, tk=1024, name="d_mixed")
    dpa, dz = _gate_bwd(dmix, zm, pa, 3, None, name="gate_bwd_fox")
    dpb, dz = _gate_bwd(dmix, zm, pb, 4, dz, name="gate_bwd_dil")
    g_w_oa = mm(ya, dpa, ta=True, out_dtype=BF16, tm=512, tn=1024, tk=SEQ, name="grad_w_o_fox")
    g_w_ob = mm(yb, dpb, ta=True, out_dtype=BF16, tm=512, tn=1024, tk=SEQ, name="grad_w_o_dil")
    tok = proj_grads_ready(dict(w_o_fox=g_w_oa, w_o_dil=g_w_ob, w_out=g_w_out))
    dya = mm(dpa, w_oa, tb=True, out_dtype=BF16, tm=2048, tn=512, tk=1024, name="d_y_fox")
    dyb = mm(dpb, w_ob, tb=True, out_dtype=BF16, tm=2048, tn=512, tk=1024, name="d_y_dil")

    qb_slots, do_slots = _fox_pack_bwd(zm, f_cols + tok, lse_a, ya, dya)
    dz, df_cols = _fox_unpack(*_fox_bwd(qb_slots, k_slots, v_slots, do_slots), dz)
    dfa_t, g_b_forget = _fox_post_bwd(df_cols[:, :N_HEADS].T, sg_row)

    rows_d = _dil_bwd_prep(yb, dyb, lse_b)
    dil_g = [_dil_bwd(qkv_d[d], *rows_d[k], d) for k, (_, d) in enumerate(DIL_PATTERNS)]
    dz = _dil_grad_combine([g[0] for g in dil_g], [g[1] for g in dil_g], [g[2] for g in dil_g], tabs, dz)

    dzf = jnp.pad(dfa_t.T, ((0, 0), (0, F_PAD - N_HEADS)))
    g_w_main = mm(h1, dz, ta=True, out_dtype=BF16, tm=1024, tn=Z_MAIN // 4, tk=2048, name="grad_w_in")
    g_w_f = mm(h1, dzf, ta=True, out_dtype=BF16, tm=1024, tn=F_PAD, tk=1024, name="grad_w_in_forget")
    tok = mixer_grads_ready(dict(w_main=g_w_main, w_f=g_w_f))
    dh1 = [mm(dz, w_main, tb=True, out_dtype=BF16, tm=512, tn=1024, tk=Z_MAIN, name="d_h1"),
           mm(dzf + tok, w_f, tb=True, out_dtype=BF16, tm=2048, tn=1024, tk=F_PAD, name="d_h1_forget")]
    grad_x, gg_pre_mix = _rms_bwd(dh1, x, g_pre_mix, dx2, out_dtype=F32, name="rms_pre_mix_bwd")

    grads = dict(
        b_forget=g_b_forget.reshape(1, N_HEADS), conv_b=_ffn_deinterleave(g_conv_b),
        g_pre_mix=gg_pre_mix, g_post_mix=gg_post_mix, g_pre_ffn=gg_pre_ffn, g_post_ffn=gg_post_ffn)
    return sq_err, grad_x, grads


def _exchange(arrays, scatter, *, name):
    n = len(arrays)
    scatters = [scatter] * n if isinstance(scatter, bool) else list(scatter)

    def body(*refs):
        ins, outs = refs[:n], refs[n:2 * n]
        send_sems, recv_sems, local_sems = refs[2 * n:]
        me, peers = _peers()

        def remote(a, k):
            dev, slot = peers[k]
            return pltpu.make_async_remote_copy(
                src_ref=ins[a].at[slot] if scatters[a] else ins[a], dst_ref=outs[a].at[me],
                send_sem=send_sems.at[a, k], recv_sem=recv_sems.at[a, k],
                device_id=dev, device_id_type=MESH_ID)

        def landed(a, k):
            dev, slot = peers[k]
            return pltpu.make_async_remote_copy(
                src_ref=outs[a].at[slot], dst_ref=outs[a].at[slot],
                send_sem=send_sems.at[a, k], recv_sem=recv_sems.at[a, k],
                device_id=dev, device_id_type=MESH_ID)

        own = [pltpu.make_async_copy(ins[a].at[me] if scatters[a] else ins[a], outs[a].at[me], local_sems.at[a])
               for a in range(n)]
        copies = [remote(a, k) for k in range(N_DEV - 1) for a in range(n)]
        for cp in own + copies:
            cp.start()
        for k in range(N_DEV - 1):
            for a in range(n):
                landed(a, k).wait_recv()
        for cp in copies:
            cp.wait_send()
        for cp in own:
            cp.wait()

    out_shape = [_sds(((N_DEV,) + a.shape[-2:]), a.dtype) for a in arrays]
    return pl.pallas_call(
        body, name=name, in_specs=[ANY] * n, out_specs=[ANY] * n, out_shape=out_shape,
        scratch_shapes=[pltpu.SemaphoreType.DMA((n, N_DEV - 1)), pltpu.SemaphoreType.DMA((n, N_DEV - 1)),
                        pltpu.SemaphoreType.DMA((n,))],
    )(*arrays)


def _gather_two_level(shard, *, name):
    def body(x_ref, out_ref, send_sems, recv_sems, local_sem):
        x, y, c = lax.axis_index("x"), lax.axis_index("y"), lax.axis_index("c")
        me, sibling = (x, y, c), (x, y, 1 - c)
        chips = [(1 - x, y), (x, 1 - y), (1 - x, 1 - y)]

        def slot(px, py, pc):
            return out_ref.at[4 * px + 2 * py + pc]

        def copy(k, block, to, src=None):
            return pltpu.make_async_remote_copy(
                src_ref=slot(*block) if src is None else src, dst_ref=slot(*block),
                send_sem=send_sems.at[k], recv_sem=recv_sems.at[k], device_id=to, device_id_type=MESH_ID)

        mine = pltpu.make_async_copy(x_ref, slot(*me), local_sem)
        mine.start()
        first = [copy(0, me, sibling, src=x_ref)]
        first += [copy(1 + j, me, (*chip, c), src=x_ref) for j, chip in enumerate(chips)]
        for cp in first:
            cp.start()
        passed = [copy(4 + j, (*chip, c), sibling) for j, chip in enumerate(chips)]
        for j, chip in enumerate(chips):
            copy(1 + j, (*chip, c), me).wait_recv()
            passed[j].start()
        copy(0, sibling, me).wait_recv()
        for j, chip in enumerate(chips):
            copy(4 + j, (*chip, 1 - c), me).wait_recv()
        for cp in first + passed:
            cp.wait_send()
        mine.wait()

    return pl.pallas_call(
        body, name=name, in_specs=[ANY], out_specs=ANY, out_shape=_sds((N_DEV,) + shard.shape, shard.dtype),
        scratch_shapes=[pltpu.SemaphoreType.DMA((N_DEV - 1,)), pltpu.SemaphoreType.DMA((N_DEV - 1,)),
                        pltpu.SemaphoreType.DMA],
    )(shard)


N_CHIPS = N_DEV // 2


def _peers(chips_only=False):
    x, y, c = lax.axis_index("x"), lax.axis_index("y"), lax.axis_index("c")
    out = []
    if chips_only:
        for k in range(1, N_CHIPS):
            px = 1 - x if k & 2 else x
            py = 1 - y if k & 1 else y
            out.append(((px, py, c), 2 * px + py))
        return 2 * x + y, out
    for k in range(1, N_DEV):
        px = 1 - x if k & 4 else x
        py = 1 - y if k & 2 else y
        pc = 1 - c if k & 1 else c
        out.append(((px, py, pc), 4 * px + 2 * py + pc))
    return 4 * x + 2 * y + c, out


def _sibling_swap(slot_arrays, *, name):
    n = len(slot_arrays)

    def body(*refs):
        ins, outs, send_sems, recv_sems = refs[:n], refs[n:2 * n], refs[2 * n], refs[2 * n + 1]
        x, y, c = lax.axis_index("x"), lax.axis_index("y"), lax.axis_index("c")
        copies = [pltpu.make_async_remote_copy(
            src_ref=ins[a].at[2 * q + (1 - c)], dst_ref=outs[a].at[q], send_sem=send_sems.at[a, q],
            recv_sem=recv_sems.at[a, q], device_id=(x, y, 1 - c), device_id_type=MESH_ID)
            for a in range(n) for q in range(N_CHIPS)]
        for cp in copies:
            cp.start()
        for cp in copies:
            cp.wait_recv()
        for cp in copies:
            cp.wait_send()

    return pl.pallas_call(
        body, name=name, in_specs=[ANY] * n, out_specs=[ANY] * n,
        out_shape=[_sds((N_CHIPS,) + t.shape[1:], t.dtype) for t in slot_arrays],
        scratch_shapes=[pltpu.SemaphoreType.DMA((n, N_CHIPS)), pltpu.SemaphoreType.DMA((n, N_CHIPS))],
    )(*slot_arrays)


def _pair_sum(slots, from_sibling, *, name, tn):
    _, r, c = slots.shape
    core = lax.axis_index("c").astype(jnp.int32).reshape(1)

    def body(core_ref, a_ref, b_ref, o_ref):
        o_ref[...] = (a_ref[...].astype(F32) + b_ref[...].astype(F32)).astype(o_ref.dtype)

    blk = lambda f: pl.BlockSpec((1, r, tn), f)
    return pl.pallas_call(
        body, name=name,
        grid_spec=pltpu.PrefetchScalarGridSpec(
            num_scalar_prefetch=1, grid=(N_CHIPS, c // tn),
            in_specs=[blk(lambda q, j, core: (2 * q + core[0], 0, j)), blk(lambda q, j, core: (q, 0, j))],
            out_specs=blk(lambda q, j, core: (q, 0, j))),
        out_shape=_sds((N_CHIPS, r, c), slots.dtype),
        compiler_params=_params("parallel", "parallel"),
    )(core, slots, from_sibling)


def _sum_parts(parts, *, name, tn):
    n, r, c = parts.shape

    def body(p_ref, o_ref):
        total = p_ref[0].astype(F32)
        for s in range(1, n):
            total = total + p_ref[s].astype(F32)
        o_ref[...] = total

    return pl.pallas_call(
        body, name=name, grid=(c // tn,),
        in_specs=[pl.BlockSpec((n, r, tn), lambda j: (0, 0, j))],
        out_specs=pl.BlockSpec((r, tn), lambda j: (0, j)), out_shape=_sds((r, c), F32),
        compiler_params=_params("parallel"),
    )(parts)


HBM = pl.BlockSpec(memory_space=pltpu.HBM)
SEM = pl.BlockSpec(memory_space=pltpu.SEMAPHORE)
DATAFLOW = pltpu.SideEffectType.DATAFLOW_SIDE_EFFECTING


def _split_copy(srcs, lands, send_sems, recv_sems, scatter, a, k, me, peers, incoming=False):
    dev, slot = peers[k]
    if incoming:
        src = dst = lands[a].at[slot]
    else:
        src, dst = (srcs[a].at[slot] if scatter else srcs[a]), lands[a].at[me]
    sem = a * len(peers) + k
    return pltpu.make_async_remote_copy(
        src_ref=src, dst_ref=dst, send_sem=send_sems.at[sem], recv_sem=recv_sems.at[sem],
        device_id=dev, device_id_type=MESH_ID)


def _exchange_start(arrays, scatter, *, name, chips_only=False):
    n = len(arrays)
    n_slots = N_CHIPS if chips_only else N_DEV

    def body(*refs):
        srcs, lands = refs[:n], refs[n:2 * n]
        send_sems, recv_sems = refs[2 * n], refs[2 * n + 1]
        token = refs[-1]
        me, peers = _peers(chips_only)
        for k in range(len(peers)):
            for a in range(n):
                _split_copy(srcs, lands, send_sems, recv_sems, scatter, a, k, me, peers).start()
        token[...] = jnp.zeros_like(token)

    land_shapes = [((n_slots,) + a.shape[-2:], a.dtype) for a in arrays]
    sems = pltpu.SemaphoreType.DMA((n * (n_slots - 1),))
    outs = pl.pallas_call(
        body, name=name,
        out_shape=(sems, sems, *[pltpu.HBM(a.shape, a.dtype) for a in arrays],
                   *[pltpu.HBM(s, d) for s, d in land_shapes], _sds((SUBLANE, LANE), F32)),
        in_specs=[HBM] * (2 * n),
        out_specs=(SEM, SEM, *[HBM] * (2 * n), pl.BlockSpec(memory_space=pltpu.VMEM)),
        input_output_aliases={i: 2 + i for i in range(2 * n)},
        compiler_params=pltpu.CompilerParams(has_side_effects=DATAFLOW),
    )(*[pltpu.with_memory_space_constraint(a, pltpu.HBM) for a in arrays],
      *[pltpu.with_memory_space_constraint(lax.empty(s, d), pltpu.HBM) for s, d in land_shapes])
    return (outs[0], outs[1], outs[2:2 + n], outs[2 + n:2 + 2 * n], scatter, chips_only), outs[-1]


def _exchange_wait(handles, after, *, name):
    send_sems, recv_sems, srcs, lands, scatter, chips_only = handles
    n = len(srcs)

    def body(*refs):
        src_refs, land_refs = refs[:n], refs[n:2 * n]
        send_ref, recv_ref = refs[2 * n], refs[2 * n + 1]
        me, peers = _peers(chips_only)
        for k in range(len(peers)):
            for a in range(n):
                _split_copy(src_refs, land_refs, send_ref, recv_ref, scatter, a, k, me, peers).wait_send()
                _split_copy(src_refs, land_refs, send_ref, recv_ref, scatter, a, k, me, peers, True).wait_recv()

    outs = pl.pallas_call(
        body, name=name,
        out_shape=tuple(pltpu.HBM(t.shape, t.dtype) for t in (*srcs, *lands)),
        in_specs=[HBM] * (2 * n) + [SEM, SEM, pl.BlockSpec(memory_space=pl.ANY)],
        out_specs=tuple([HBM] * (2 * n)),
        input_output_aliases={i: i for i in range(2 * n)},
        compiler_params=pltpu.CompilerParams(has_side_effects=DATAFLOW),
    )(*srcs, *lands, send_sems, recv_sems, after)
    return _with_own_slot(outs[n:], outs[:n], scatter, chips_only)


def _with_own_slot(landed, own, scatter, chips_only):
    me = 2 * lax.axis_index("x") + lax.axis_index("y")
    if not chips_only:
        me = 2 * me + lax.axis_index("c")
    out = []
    for buf, src in zip(landed, own):
        mine = lax.dynamic_index_in_dim(src, me, 0, keepdims=False) if scatter else src
        out.append(lax.dynamic_update_index_in_dim(buf, mine, me, 0))
    return out


def _adamw(parts, w, m, v, *, name, tm):
    r, c = w.shape
    assert r % tm == 0

    def body(p_ref, w_ref, m_ref, v_ref, g_ref, d_ref, nm_ref, nv_ref):
        _adamw_update(p_ref, w_ref, m_ref, v_ref, g_ref, d_ref, nm_ref, nv_ref)

    blk = pl.BlockSpec((tm, c), lambda i: (i, 0))
    return pl.pallas_call(
        body, name=name, grid=(r // tm,),
        in_specs=[pl.BlockSpec((parts.shape[0], tm, c), lambda i: (0, i, 0)), blk, blk, blk],
        out_specs=[blk] * 4, out_shape=[_sds((r, c), F32)] * 4,
        compiler_params=_params("parallel"),
    )(parts, w, m, v)


def _adamw_update(p_ref, w_ref, m_ref, v_ref, g_ref, d_ref, nm_ref, nv_ref):
    g = p_ref[0].astype(F32)
    for s in range(1, p_ref.shape[0]):
        g = g + p_ref[s].astype(F32)
    g_ref[...] = g
    m_new = ADAM_B1 * m_ref[...] + (1.0 - ADAM_B1) * g
    v_new = ADAM_B2 * v_ref[...] + (1.0 - ADAM_B2) * (g * g)
    nm_ref[...] = m_new
    nv_ref[...] = v_new
    m_hat = m_new / (1.0 - ADAM_B1 ** ADAM_STEP)
    v_hat = v_new / (1.0 - ADAM_B2 ** ADAM_STEP)
    d_ref[...] = -ADAM_LR * (m_hat / (jnp.sqrt(v_hat) + ADAM_EPS) + ADAM_WD * w_ref[...])


SMALL = ("g_pre_mix", "b_forget", "g_post_mix", "g_pre_ffn", "conv_b", "g_post_ffn")


def _adamw_small(parts, ws, ms, vs, sq_err_parts):
    n = len(ws)

    def body(*refs):
        ins, sq_ref, outs, loss_ref = refs[:4 * n], refs[4 * n], refs[4 * n + 1:-1], refs[-1]
        for i in range(n):
            _adamw_update(ins[i], ins[n + i], ins[2 * n + i], ins[3 * n + i], *outs[4 * i:4 * i + 4])
        total = sq_ref[0]
        for s in range(1, N_DEV):
            total = total + sq_ref[s]
        loss_ref[...] = total * (0.5 / D_MODEL)

    res = pl.pallas_call(
        body, name="adamw_small",
        out_shape=[_sds(w.shape, F32) for w in ws for _ in range(4)] + [_sds((1, LANE), F32)],
        compiler_params=pltpu.CompilerParams(vmem_limit_bytes=VMEM_LIMIT),
    )(*parts, *ws, *ms, *vs, sq_err_parts)
    return [res[4 * i:4 * i + 4] for i in range(n)], res[-1][0, 0]


def kernel(x, g_pre_mix, w_in, b_forget, w_o_fox, w_o_dil, w_out, g_post_mix, g_pre_ffn, w_up, conv_w, conv_b, w_down, g_post_ffn, loss_target, m_g_pre_mix, m_w_in, m_b_forget, m_w_o_fox, m_w_o_dil, m_w_out, m_g_post_mix, m_g_pre_ffn, m_w_up, m_conv_w, m_conv_b, m_w_down, m_g_post_ffn, v_g_pre_mix, v_w_in, v_b_forget, v_w_o_fox, v_w_o_dil, v_w_out, v_g_post_mix, v_g_pre_ffn, v_w_up, v_conv_w, v_conv_b, v_w_down, v_g_post_ffn):
    names = ("g_pre_mix", "w_in", "b_forget", "w_o_fox", "w_o_dil", "w_out", "g_post_mix", "g_pre_ffn",
             "w_up", "conv_w", "conv_b", "w_down", "g_post_ffn")
    w = dict(g_pre_mix=g_pre_mix, w_in=w_in, b_forget=b_forget, w_o_fox=w_o_fox, w_o_dil=w_o_dil, w_out=w_out,
             g_post_mix=g_post_mix, g_pre_ffn=g_pre_ffn, w_up=w_up, conv_w=conv_w, conv_b=conv_b, w_down=w_down,
             g_post_ffn=g_post_ffn)
    m = dict(g_pre_mix=m_g_pre_mix, w_in=m_w_in, b_forget=m_b_forget, w_o_fox=m_w_o_fox, w_o_dil=m_w_o_dil,
             w_out=m_w_out, g_post_mix=m_g_post_mix, g_pre_ffn=m_g_pre_ffn, w_up=m_w_up, conv_w=m_conv_w,
             conv_b=m_conv_b, w_down=m_w_down, g_post_ffn=m_g_post_ffn)
    v = dict(g_pre_mix=v_g_pre_mix, w_in=v_w_in, b_forget=v_b_forget, w_o_fox=v_w_o_fox, w_o_dil=v_w_o_dil,
             w_out=v_w_out, g_post_mix=v_g_post_mix, g_pre_ffn=v_g_pre_ffn, w_up=v_w_up, conv_w=v_conv_w,
             conv_b=v_conv_b, w_down=v_w_down, g_post_ffn=v_g_post_ffn)
    sharded = ("w_in", "w_o_fox", "w_o_dil", "w_out", "w_up", "w_down", "conv_w")
    wire = lambda n: F32 if n == "conv_w" else BF16

    by_cols = lambda t: jnp.transpose(t, (1, 0, 2)).reshape(t.shape[1], N_DEV * t.shape[2])
    by_rows = lambda t: t.reshape(N_DEV * t.shape[1], t.shape[2])
    col_slots = lambda t: jnp.transpose(t.reshape(t.shape[0], N_DEV, t.shape[1] // N_DEV), (1, 0, 2))
    row_slots = lambda t: t.reshape(N_DEV, t.shape[0] // N_DEV, t.shape[1])
    to_slots = lambda n, t: (row_slots if n in ("w_out", "w_down") else col_slots)(t).astype(wire(n))
    shard = lambda n: w[n][0].astype(wire(n))
    f_lo = FORGET_LO

    w_main, w_f = _w_in_from_shards(_gather_two_level(shard("w_in"), name="gather_w_in"))
    late = ("w_o_fox", "w_o_dil", "w_out", "w_up", "conv_w", "w_down")
    order = jnp.minimum(jnp.abs(w_f[0, 0].astype(F32)), 0.0)
    late_handles, late_tok = _exchange_start(
        [shard(n) + order.astype(wire(n)) if n == "conv_w" else shard(n) for n in late], False,
        name="gather_late_start")

    def late_weights(after):
        got = dict(zip(late, _exchange_wait(late_handles, after, name="gather_late_wait")))
        return (by_cols(got["w_o_fox"]), by_cols(got["w_o_dil"]), by_rows(got["w_out"]),
                _w_up_from_shards(got["w_up"]),
                _ffn_interleave(by_cols(got["conv_w"])),
                by_rows(got["w_down"]))

    pending = {}

    def ffn_grads_ready(g):
        slots = [to_slots("w_down", g["w_down"]), _w_up_to_shards(g["w_up_blocks"]), to_slots("conv_w", g["conv_w"])]
        pending["ffn"] = _exchange_start(slots, True, name="scatter_ffn_start")
        return pending["ffn"][1][0, 0]

    def proj_grads_ready(g):
        pending["proj"] = _exchange_start([to_slots(n, g[n]) for n in ("w_o_fox", "w_o_dil", "w_out")], True,
                                          name="scatter_proj_start")
        return pending["proj"][1][0, 0]

    def mixer_grads_ready(g):
        slabs = [g["w_main"].reshape(N_DEV, D_MODEL // N_DEV, Z_MAIN), g["w_f"].reshape(N_DEV, D_MODEL // N_DEV, F_PAD)]
        theirs = _sibling_swap(slabs, name="scatter_w_in_swap")
        chip_sums = [_pair_sum(slabs[0], theirs[0], name="scatter_w_in_pair_sum", tn=Z_MAIN // 4),
                     _pair_sum(slabs[1], theirs[1], name="scatter_w_in_forget_pair_sum", tn=F_PAD)]
        pending["w_in"] = _exchange_start(chip_sums, True, name="scatter_w_in_start", chips_only=True)
        return pending["w_in"][1][0, 0]

    sq_err, grad_x, g = _local_step(
        x[0], loss_target[0], w_main, w_f, b_forget, conv_b, g_pre_mix + late_tok[0, 0], g_post_mix, g_pre_ffn,
        g_post_ffn, late_weights, ffn_grads_ready, proj_grads_ready, mixer_grads_ready)

    tiles = dict(w_in=256, w_o_fox=512, w_o_dil=512, w_out=128, w_up=256, w_down=176, conv_w=3)
    adam = lambda n, p: _adamw(p, w[n][0], m[n][0], v[n][0], name=f"adamw_{n}", tm=tiles[n])
    res = {}
    for key, group in (("ffn", ("w_down", "w_up", "conv_w")), ("proj", ("w_o_fox", "w_o_dil", "w_out"))):
        landed = _exchange_wait(pending[key][0], grad_x, name=f"scatter_{key}_wait")
        res.update({n: adam(n, p) for n, p in zip(group, landed)})
    done = res["w_up"][3]
    main_parts, f_parts = _exchange_wait(pending["w_in"][0], done, name="scatter_w_in_wait")
    slab_main = _sum_parts(main_parts, name="scatter_w_in_sum", tn=Z_MAIN // 4)
    slab_f = _sum_parts(f_parts, name="scatter_w_in_forget_sum", tn=F_PAD)
    slab = jnp.concatenate([slab_main[:, :f_lo], slab_f[:, :N_HEADS], slab_main[:, f_lo:]], axis=1)
    last = _exchange([col_slots(slab).astype(BF16)] + [g[n] for n in SMALL] + [sq_err],
                     [True] + [False] * (len(SMALL) + 1), name="scatter_w_in_rows_gather_small")
    rows, small_parts = last[0], last[1:]
    res["w_in"] = adam("w_in", rows.reshape(1, D_MODEL, rows.shape[-1]))
    small, loss = _adamw_small(small_parts[:-1], *[[t[n] for n in SMALL] for t in (w, m, v)], small_parts[-1])
    small = dict(zip(SMALL, small))
    out = [[(res[n][k][None] if n in sharded else small[n][k]) for n in names] for k in range(4)]
    return (loss, grad_x[None], *out[0], *out[1], *out[2], *out[3])
```

```python
import functools
import math

import jax
import jax.numpy as jnp
import numpy as np
from jax import lax
from jax.experimental import pallas as pl
from jax.experimental.pallas import tpu as pltpu

F32 = jnp.float32
BF16 = jnp.bfloat16

SEQ = 4096
D_MODEL = 1024
N_HEADS = 8
HEAD_DIM = 64
ATT_W = N_HEADS * HEAD_DIM
D_FF = 2816
Z_MAIN = 5120
F_PAD = 128
ROPE_DIM = 16
ROPE_THETA = 500000.0
RMS_EPS = 1e-6
NEG_INF = -1e30
SCALE = 1.0 / math.sqrt(HEAD_DIM)
DIL_PATTERNS = ((128, 1), (512, 4), (2048, 16))
DIL_BLK = 128
N_DEV = 8

ADAM_LR = 0.001
ADAM_B1 = 0.9
ADAM_B2 = 0.999
ADAM_EPS = 1e-08
ADAM_WD = 0.01
ADAM_STEP = 10

LANE = 128
SUBLANE = 8
VMEM_LIMIT = 56 * 1024 * 1024
MESH_ID = pl.DeviceIdType.MESH
ANY = pl.BlockSpec(memory_space=pl.ANY)


def _params(*sem):
    return pltpu.CompilerParams(dimension_semantics=sem, vmem_limit_bytes=VMEM_LIMIT)


def _sds(shape, dtype):
    return jax.ShapeDtypeStruct(shape, dtype)


def _matmul(a, b, *, ta=False, tb=False, out_dtype, tm, tn, tk, name, b_k_off=0):
    if ta:
        kk, m = a.shape
    else:
        m, kk = a.shape
    n = b.shape[0] if tb else b.shape[1]
    tm, tn, tk = min(tm, m), min(tn, n), min(tk, kk)
    assert (b.shape[1] if tb else b.shape[0]) >= b_k_off * tk + kk
    assert m % tm == 0 and n % tn == 0 and kk % tk == 0, (name, m, n, kk, tm, tn, tk)
    nk = kk // tk
    dims = (((0 if ta else 1,), (1 if tb else 0,)), ((), ()))

    def body(a_ref, b_ref, o_ref, *scratch):
        p = lax.dot_general(a_ref[...].astype(BF16), b_ref[...].astype(BF16), dims,
                            preferred_element_type=F32)
        if nk == 1:
            o_ref[...] = p.astype(o_ref.dtype)
        else:
            acc = scratch[0]
            k = pl.program_id(2)

            @pl.when(k == 0)
            def _():
                acc[...] = p

            @pl.when(k > 0)
            def _():
                acc[...] += p

            @pl.when(k == nk - 1)
            def _():
                o_ref[...] = acc[...].astype(o_ref.dtype)

    a_spec = (pl.BlockSpec((tk, tm), lambda i, j, k: (k, i)) if ta
              else pl.BlockSpec((tm, tk), lambda i, j, k: (i, k)))
    b_spec = (pl.BlockSpec((tn, tk), lambda i, j, k: (j, k + b_k_off)) if tb
              else pl.BlockSpec((tk, tn), lambda i, j, k: (k + b_k_off, j)))
    return pl.pallas_call(
        body, name=name, grid=(m // tm, n // tn, nk),
        in_specs=[a_spec, b_spec],
        out_specs=pl.BlockSpec((tm, tn), lambda i, j, k: (i, j)),
        out_shape=_sds((m, n), out_dtype),
        scratch_shapes=[pltpu.VMEM((tm, tn), F32)] if nk > 1 else [],
        compiler_params=_params("parallel", "parallel", "arbitrary"),
    )(a, b)


def _rms_fwd(x, g, *, name, tm=512):
    def body(x_ref, g_ref, h_ref):
        xv = x_ref[...]
        r = lax.rsqrt(jnp.mean(xv * xv, axis=-1, keepdims=True) + RMS_EPS)
        h_ref[...] = (xv * r * g_ref[...]).astype(h_ref.dtype)

    return pl.pallas_call(
        body, name=name, grid=(SEQ // tm,),
        in_specs=[pl.BlockSpec((tm, D_MODEL), lambda i: (i, 0)), pl.BlockSpec((1, D_MODEL), lambda i: (0, 0))],
        out_specs=pl.BlockSpec((tm, D_MODEL), lambda i: (i, 0)),
        out_shape=_sds((SEQ, D_MODEL), BF16),
        compiler_params=_params("parallel"),
    )(x, g)


def _rms_bwd(dh_parts, xin, g, dres, *, out_dtype, name, tm=512):
    n_parts = len(dh_parts)
    has_res = dres is not None

    def body(*refs):
        parts = refs[:n_parts]
        x_ref, g_ref = refs[n_parts], refs[n_parts + 1]
        res_ref = refs[n_parts + 2] if has_res else None
        o_ref, gg_ref = refs[-2], refs[-1]
        dh = parts[0][...].astype(F32)
        for p in parts[1:]:
            dh = dh + p[...].astype(F32)
        xv = x_ref[...]
        r = lax.rsqrt(jnp.mean(xv * xv, axis=-1, keepdims=True) + RMS_EPS)
        xn = xv * r

        @pl.when(pl.program_id(0) == 0)
        def _():
            gg_ref[...] = jnp.zeros_like(gg_ref)

        gg_ref[...] += jnp.sum(dh * xn, axis=0, keepdims=True)
        dxn = dh * g_ref[...]
        dx = r * (dxn - xn * jnp.mean(dxn * xn, axis=-1, keepdims=True))
        if has_res:
            dx = dx + res_ref[...]
        o_ref[...] = dx.astype(o_ref.dtype)

    row = pl.BlockSpec((tm, D_MODEL), lambda i: (i, 0))
    vec = pl.BlockSpec((1, D_MODEL), lambda i: (0, 0))
    args = list(dh_parts) + [xin, g] + ([dres] if has_res else [])
    return pl.pallas_call(
        body, name=name, grid=(SEQ // tm,),
        in_specs=[row] * n_parts + [row, vec] + ([row] if has_res else []),
        out_specs=[row, vec],
        out_shape=[_sds((SEQ, D_MODEL), out_dtype), _sds((1, D_MODEL), F32)],
        compiler_params=_params("arbitrary"),
    )(*args)


def _rms_pair_bwd(dh_parts, x2, g_pre, dres, y1, g_post, *, tm=512):
    n_parts = len(dh_parts)

    def norm_bwd(dh, xin, g_ref, gg_ref):
        r = lax.rsqrt(jnp.mean(xin * xin, axis=-1, keepdims=True) + RMS_EPS)
        xn = xin * r
        gg_ref[...] += jnp.sum(dh * xn, axis=0, keepdims=True)
        dxn = dh * g_ref[...]
        return r * (dxn - xn * jnp.mean(dxn * xn, axis=-1, keepdims=True))

    def body(*refs):
        parts = refs[:n_parts]
        x2_ref, gpre_ref, res_ref, y1_ref, gpost_ref, dx2_ref, dy1_ref, ggpre_ref, ggpost_ref = refs[n_parts:]

        @pl.when(pl.program_id(0) == 0)
        def _():
            ggpre_ref[...] = jnp.zeros_like(ggpre_ref)
            ggpost_ref[...] = jnp.zeros_like(ggpost_ref)

        dh = parts[0][...].astype(F32)
        for p in parts[1:]:
            dh = dh + p[...].astype(F32)
        dx2 = res_ref[...] + norm_bwd(dh, x2_ref[...], gpre_ref, ggpre_ref)
        dx2_ref[...] = dx2
        dy1_ref[...] = norm_bwd(dx2, y1_ref[...], gpost_ref, ggpost_ref).astype(dy1_ref.dtype)

    row = pl.BlockSpec((tm, D_MODEL), lambda i: (i, 0))
    vec = pl.BlockSpec((1, D_MODEL), lambda i: (0, 0))
    return pl.pallas_call(
        body, name="rms_pair_bwd", grid=(SEQ // tm,),
        in_specs=[row] * n_parts + [row, vec, row, row, vec],
        out_specs=[row, row, vec, vec],
        out_shape=[_sds((SEQ, D_MODEL), F32), _sds((SEQ, D_MODEL), BF16), _sds((1, D_MODEL), F32),
                   _sds((1, D_MODEL), F32)],
        compiler_params=_params("arbitrary"),
    )(*dh_parts, x2, g_pre, dres, y1, g_post)


SCAN_BLK = 512


def _split_dot(v, tri):
    hi = v.astype(BF16)
    r1 = v - hi.astype(F32)
    mid = r1.astype(BF16)
    lo = (r1 - mid.astype(F32)).astype(BF16)
    dot = functools.partial(jnp.dot, preferred_element_type=F32)
    return dot(hi, tri) + dot(mid, tri) + dot(lo, tri)


def _fox_prep(fa_t, b_col):
    nblk = SEQ // SCAN_BLK

    def body(fa_ref, b_ref, f_ref, sg_ref):
        row = lax.broadcasted_iota(jnp.int32, (SCAN_BLK, SCAN_BLK), 0)
        col = lax.broadcasted_iota(jnp.int32, (SCAN_BLK, SCAN_BLK), 1)
        upper = (row <= col).astype(BF16)
        carry = jnp.zeros((N_HEADS, 1), F32)
        for blk in range(nblk):
            sl = pl.ds(blk * SCAN_BLK, SCAN_BLK)
            xx = fa_ref[:, sl] + b_ref[...]
            e = jnp.exp(-jnp.abs(xx))
            logf = jnp.minimum(xx, 0.0) - jnp.log(1.0 + e)
            sg_ref[:, sl] = jnp.where(xx >= 0.0, e, 1.0) / (1.0 + e)
            c = _split_dot(logf, upper) + carry
            f_ref[:, sl] = c
            carry = c[:, SCAN_BLK - 1:SCAN_BLK]

    return pl.pallas_call(
        body, name="fox_prep",
        out_shape=[_sds((N_HEADS, SEQ), F32), _sds((N_HEADS, SEQ), F32)],
        compiler_params=pltpu.CompilerParams(vmem_limit_bytes=VMEM_LIMIT),
    )(fa_t, b_col)


def _fox_post_bwd(df_t, sg_t):
    nblk = SEQ // SCAN_BLK

    def body(df_ref, sg_ref, dfa_ref, gb_ref):
        row = lax.broadcasted_iota(jnp.int32, (SCAN_BLK, SCAN_BLK), 0)
        col = lax.broadcasted_iota(jnp.int32, (SCAN_BLK, SCAN_BLK), 1)
        lower = (row >= col).astype(BF16)
        carry = jnp.zeros((N_HEADS, 1), F32)
        gb = jnp.zeros((N_HEADS, 1), F32)
        for blk in reversed(range(nblk)):
            sl = pl.ds(blk * SCAN_BLK, SCAN_BLK)
            c = _split_dot(df_ref[:, sl], lower) + carry
            carry = c[:, 0:1]
            dfa = c * sg_ref[:, sl]
            dfa_ref[:, sl] = dfa
            gb = gb + jnp.sum(dfa, axis=1, keepdims=True)
        gb_ref[...] = gb

    return pl.pallas_call(
        body, name="fox_post_bwd",
        out_shape=[_sds((N_HEADS, SEQ), F32), _sds((N_HEADS, 1), F32)],
        compiler_params=pltpu.CompilerParams(vmem_limit_bytes=VMEM_LIMIT),
    )(df_t, sg_t)


FOX_T = 512
NT_DIMS = (((1,), (1,)), ((), ()))
TN_DIMS = (((0,), (0,)), ((), ()))


def _head(ref_or_val, h):
    return ref_or_val[:, h * HEAD_DIM:(h + 1) * HEAD_DIM]


def _split3(v):
    hi = v.astype(BF16).astype(F32)
    r1 = v - hi
    mid = r1.astype(BF16).astype(F32)
    return hi, mid, (r1 - mid).astype(BF16).astype(F32)


ONE_LANE = 3 * N_HEADS


def _pack_terms(v, with_one):
    hi, mid, lo = _split3(v)
    t = hi + pltpu.roll(mid, N_HEADS, 1) + pltpu.roll(lo, 2 * N_HEADS, 1)
    if with_one:
        t = t + (lax.broadcasted_iota(jnp.int32, v.shape, 1) == ONE_LANE).astype(F32)
    return t.astype(BF16)


def _aux_matrices():
    to_q = np.zeros((LANE, N_HEADS * 2 * HEAD_DIM), np.float32)
    to_k = np.zeros_like(to_q)
    for h in range(N_HEADS):
        base = h * 2 * HEAD_DIM + HEAD_DIM
        for s in range(3):
            to_q[s * N_HEADS + h, base + s] = 1.0
            to_q[ONE_LANE, base + 3 + s] = 1.0
            to_k[ONE_LANE, base + s] = 1.0
            to_k[s * N_HEADS + h, base + 3 + s] = -1.0
    return jnp.asarray(to_q, BF16), jnp.asarray(to_k, BF16)


def _head_sums():
    total = np.zeros((N_HEADS * HEAD_DIM, LANE), np.float32)
    first = np.zeros_like(total)
    for h in range(N_HEADS):
        total[h * HEAD_DIM:(h + 1) * HEAD_DIM, h] = 1.0
        first[h * HEAD_DIM, h] = 1.0
    return jnp.asarray(total, BF16), jnp.asarray(first, BF16)


SLOT = 2 * HEAD_DIM
N_SPLIT = 3
FOX_FWD_HEADS = 8
FOX_BWD_HEADS = 4


def _slot(ref, h):
    return ref[:, h * SLOT:(h + 1) * SLOT]


def _fox_pack_fwd(zm, f_cols, *, tm=512):
    def body(q_ref, k_ref, v_ref, f_ref, tq_ref, tk_ref, qs_ref, ks_ref, vs_ref):
        ones = jnp.ones((tm, HEAD_DIM), BF16)
        terms = _pack_terms(f_ref[...], True)
        q_aux = jnp.dot(terms, tq_ref[...], preferred_element_type=F32).astype(BF16)
        k_aux = jnp.dot(terms, tk_ref[...], preferred_element_type=F32).astype(BF16)
        for h in range(N_HEADS):
            aux = slice(h * SLOT + HEAD_DIM, (h + 1) * SLOT)
            qs_ref[:, h * SLOT:(h + 1) * SLOT] = jnp.concatenate(
                [(_head(q_ref, h).astype(F32) * SCALE).astype(BF16), q_aux[:, aux]], axis=1)
            ks_ref[:, h * SLOT:(h + 1) * SLOT] = jnp.concatenate([_head(k_ref, h), k_aux[:, aux]], axis=1)
            vs_ref[:, h * SLOT:(h + 1) * SLOT] = jnp.concatenate([_head(v_ref, h), ones], axis=1)

    col = lambda b: pl.BlockSpec((tm, ATT_W), lambda i: (i, b))
    wide = pl.BlockSpec((tm, N_HEADS * SLOT), lambda i: (i, 0))
    const = pl.BlockSpec((LANE, N_HEADS * SLOT), lambda i: (0, 0))
    return pl.pallas_call(
        body, name="fox_pack_fwd", grid=(SEQ // tm,),
        in_specs=[col(0), col(1), col(2), pl.BlockSpec((tm, LANE), lambda i: (i, 0)), const, const],
        out_specs=[wide] * 3, out_shape=[_sds((SEQ, N_HEADS * SLOT), BF16)] * 3,
        compiler_params=_params("parallel"),
    )(zm, zm, zm, f_cols, *_aux_matrices())


def _fox_pack_bwd(zm, f_cols, lse, o, do, *, tm=512):
    def body(q_ref, f_ref, lse_ref, o_ref, do_ref, tq_ref, total_ref, first_ref, qs_ref, ds_ref):
        delta = _split_dot(o_ref[...].astype(F32) * do_ref[...].astype(F32), total_ref[...])
        lse_h = _split_dot(lse_ref[...], first_ref[...])
        q_aux = jnp.dot(_pack_terms(f_ref[...] - lse_h, True), tq_ref[...], preferred_element_type=F32).astype(BF16)
        d_aux = jnp.dot(_pack_terms(-delta, False), tq_ref[...], preferred_element_type=F32).astype(BF16)
        for h in range(N_HEADS):
            aux = slice(h * SLOT + HEAD_DIM, (h + 1) * SLOT)
            qs_ref[:, h * SLOT:(h + 1) * SLOT] = jnp.concatenate(
                [(_head(q_ref, h).astype(F32) * SCALE).astype(BF16), q_aux[:, aux]], axis=1)
            ds_ref[:, h * SLOT:(h + 1) * SLOT] = jnp.concatenate([_head(do_ref, h), d_aux[:, aux]], axis=1)

    row = pl.BlockSpec((tm, ATT_W), lambda i: (i, 0))
    wide = pl.BlockSpec((tm, N_HEADS * SLOT), lambda i: (i, 0))
    const = lambda r, c: pl.BlockSpec((r, c), lambda i: (0, 0))
    return pl.pallas_call(
        body, name="fox_pack_bwd", grid=(SEQ // tm,),
        in_specs=[row, pl.BlockSpec((tm, LANE), lambda i: (i, 0)), row, row, row,
                  const(LANE, N_HEADS * SLOT), const(ATT_W, LANE), const(ATT_W, LANE)],
        out_specs=[wide] * 2, out_shape=[_sds((SEQ, N_HEADS * SLOT), BF16)] * 2,
        compiler_params=_params("parallel"),
    )(zm, f_cols, lse, o, do, _aux_matrices()[0], *_head_sums())


def _causal_pairs(key_major):
    nb = SEQ // FOX_T
    if key_major:
        pairs = [(i, j) for j in range(nb) for i in range(j, nb)]
    else:
        pairs = [(i, j) for i in range(nb) for j in range(i + 1)]
    return (jnp.array([p[0] for p in pairs], jnp.int32), jnp.array([p[1] for p in pairs], jnp.int32), len(pairs))


FOX_HALF = FOX_T // 2
FOX_FULL = ((slice(0, FOX_T), slice(0, FOX_T), None),)
FOX_DIAG = ((slice(0, FOX_HALF), slice(0, FOX_HALF), 0), (slice(FOX_HALF, FOX_T), slice(0, FOX_T), FOX_HALF))


def _causal_piece_mask(q_rows, k_rows, offset):
    shape = (q_rows.stop - q_rows.start, k_rows.stop - k_rows.start)
    row = lax.broadcasted_iota(jnp.int32, shape, 0)
    col = lax.broadcasted_iota(jnp.int32, shape, 1)
    return col <= row + offset


def _fox_fwd(q_slots, k_slots, v_slots):
    i_tab, j_tab, n_pairs = _causal_pairs(False)

    def body(i_tab, j_tab, q_ref, k_ref, v_ref, o_ref, lse_ref, m_s, acc_s):
        t = pl.program_id(1)
        i, j = i_tab[t], j_tab[t]

        @pl.when(j == 0)
        def _():
            m_s[...] = jnp.full_like(m_s, NEG_INF)
            acc_s[...] = jnp.zeros_like(acc_s)

        def step(pieces):
            jobs = [(h, piece) for h in range(FOX_FWD_HEADS) for piece in pieces]
            lanes = lambda h: slice(h * SLOT, (h + 1) * SLOT)
            scores = [lax.dot_general(q_ref[qr, lanes(h)], k_ref[kr, lanes(h)], NT_DIMS, preferred_element_type=F32)
                      for h, (qr, kr, _) in jobs]
            probs, alphas = [], []
            for idx, (h, (qr, kr, offset)) in enumerate(jobs):
                s = scores[idx]
                if offset is not None:
                    s = jnp.where(_causal_piece_mask(qr, kr, offset), s, NEG_INF)
                m_prev = m_s[h, qr, :]
                m_new = jnp.maximum(m_prev, jnp.max(s, axis=-1, keepdims=True))
                probs.append(jnp.exp(s - jnp.tile(m_new, (1, s.shape[1] // LANE))).astype(BF16))
                alphas.append(jnp.exp(m_prev - m_new))
                m_s[h, qr, :] = m_new
            for idx, (h, (qr, kr, _)) in enumerate(jobs):
                acc_s[h, qr, :] = alphas[idx] * acc_s[h, qr, :] + jnp.dot(
                    probs[idx], v_ref[kr, lanes(h)], preferred_element_type=F32)

        @pl.when(j < i)
        def _():
            step(FOX_FULL)

        @pl.when(j == i)
        def _():
            step(FOX_DIAG)
            outs, lses = [], []
            for h in range(FOX_FWD_HEADS):
                acc = acc_s[h]
                l = acc[:, HEAD_DIM:]
                outs.append(acc[:, :HEAD_DIM] / l)
                lses.append(m_s[h][:, :HEAD_DIM] + jnp.log(l))
            o_ref[...] = jnp.concatenate(outs, axis=1).astype(o_ref.dtype)
            lse_ref[...] = jnp.concatenate(lses, axis=1)

    qspec = pl.BlockSpec((FOX_T, FOX_FWD_HEADS * SLOT), lambda p, t, it, jt: (it[t], p))
    kspec = pl.BlockSpec((FOX_T, FOX_FWD_HEADS * SLOT), lambda p, t, it, jt: (jt[t], p))
    ospec = pl.BlockSpec((FOX_T, FOX_FWD_HEADS * HEAD_DIM), lambda p, t, it, jt: (it[t], p))
    return pl.pallas_call(
        body, name="fox_fwd",
        grid_spec=pltpu.PrefetchScalarGridSpec(
            num_scalar_prefetch=2, grid=(N_HEADS // FOX_FWD_HEADS, n_pairs),
            in_specs=[qspec, kspec, kspec], out_specs=[ospec, ospec],
            scratch_shapes=[pltpu.VMEM((FOX_FWD_HEADS, FOX_T, LANE), F32),
                            pltpu.VMEM((FOX_FWD_HEADS, FOX_T, SLOT), F32)]),
        out_shape=[_sds((SEQ, ATT_W), BF16), _sds((SEQ, ATT_W), F32)],
        compiler_params=_params("parallel", "arbitrary"),
    )(i_tab, j_tab, q_slots, k_slots, v_slots)


def _fox_bwd(q_slots, k_slots, v_slots, do_slots):
    i_tab, j_tab, n_pairs = _causal_pairs(True)

    def body(i_tab, j_tab, q_ref, k_ref, v_ref, do_ref, dq_ref, dk_ref, dv_ref):
        t = pl.program_id(1)
        i, j = i_tab[t], j_tab[t]

        @pl.when(t == 0)
        def _():
            dq_ref[...] = jnp.zeros_like(dq_ref)

        @pl.when(i == j)
        def _():
            dk_ref[...] = jnp.zeros_like(dk_ref)
            dv_ref[...] = jnp.zeros_like(dv_ref)

        def step(pieces):
            jobs = [(h, piece) for h in range(FOX_BWD_HEADS) for piece in pieces]
            lanes = lambda h: slice(h * SLOT, (h + 1) * SLOT)
            scores = [lax.dot_general(q_ref[qr, lanes(h)], k_ref[kr, lanes(h)], NT_DIMS, preferred_element_type=F32)
                      for h, (qr, kr, _) in jobs]
            dps = [lax.dot_general(do_ref[qr, lanes(h)], v_ref[kr, lanes(h)], NT_DIMS, preferred_element_type=F32)
                   for h, (qr, kr, _) in jobs]
            ps, dss = [], []
            for idx, (h, (qr, kr, offset)) in enumerate(jobs):
                p = jnp.exp(scores[idx])
                if offset is not None:
                    p = jnp.where(_causal_piece_mask(qr, kr, offset), p, 0.0)
                ps.append(p.astype(BF16))
                dss.append((p * dps[idx]).astype(BF16))
            for idx, (h, (qr, kr, _)) in enumerate(jobs):
                rows = pl.ds(pl.multiple_of(i * FOX_T + qr.start, FOX_HALF), qr.stop - qr.start)
                dv_ref[kr, lanes(h)] += lax.dot_general(ps[idx], do_ref[qr, lanes(h)], TN_DIMS,
                                                        preferred_element_type=F32)
                dk_ref[kr, lanes(h)] += lax.dot_general(dss[idx], q_ref[qr, lanes(h)], TN_DIMS,
                                                        preferred_element_type=F32)
                dq_ref[rows, lanes(h)] += jnp.dot(dss[idx], k_ref[kr, lanes(h)], preferred_element_type=F32)

        @pl.when(i > j)
        def _():
            step(FOX_FULL)

        @pl.when(i == j)
        def _():
            step(FOX_DIAG)

    qspec = pl.BlockSpec((FOX_T, FOX_BWD_HEADS * SLOT), lambda p, t, it, jt: (it[t], p))
    kspec = pl.BlockSpec((FOX_T, FOX_BWD_HEADS * SLOT), lambda p, t, it, jt: (jt[t], p))
    return pl.pallas_call(
        body, name="fox_bwd",
        grid_spec=pltpu.PrefetchScalarGridSpec(
            num_scalar_prefetch=2, grid=(N_HEADS // FOX_BWD_HEADS, n_pairs),
            in_specs=[qspec, kspec, kspec, qspec],
            out_specs=[pl.BlockSpec((SEQ, FOX_BWD_HEADS * SLOT), lambda p, t, it, jt: (0, p)), kspec, kspec]),
        out_shape=[_sds((SEQ, N_HEADS * SLOT), F32)] * 3,
        compiler_params=_params("arbitrary", "arbitrary"),
    )(i_tab, j_tab, q_slots, k_slots, v_slots, do_slots)


def _fox_unpack(dq_slots, dk_slots, dv_slots, dz, *, tm=512):
    def body(dq_ref, dk_ref, dv_ref, dz_in, o_ref, df_ref):
        lane = lax.broadcasted_iota(jnp.int32, (tm, LANE), 1)
        df = jnp.zeros((tm, LANE), F32)
        for h in range(N_HEADS):
            lo = h * SLOT
            for part, (ref, mult) in enumerate(((dq_ref, SCALE), (dk_ref, 1.0), (dv_ref, 1.0))):
                o_ref[:, part * ATT_W + h * HEAD_DIM:part * ATT_W + (h + 1) * HEAD_DIM] = (
                    ref[:, lo:lo + HEAD_DIM] * mult).astype(o_ref.dtype)
            rows = dq_ref[:, lo + HEAD_DIM:lo + HEAD_DIM + 1]
            cols = dk_ref[:, lo + HEAD_DIM + N_SPLIT:lo + HEAD_DIM + N_SPLIT + 1]
            df = jnp.where(lane == h, rows - cols, df)
        df_ref[...] = df

    wide = pl.BlockSpec((tm, N_HEADS * SLOT), lambda i: (i, 0))
    return pl.pallas_call(
        body, name="fox_unpack", grid=(SEQ // tm,), in_specs=[wide] * 3 + [ANY],
        out_specs=[pl.BlockSpec((tm, 3 * ATT_W), lambda i: (i, 0)), pl.BlockSpec((tm, LANE), lambda i: (i, 0))],
        out_shape=[_sds((SEQ, Z_MAIN), BF16), _sds((SEQ, LANE), F32)],
        input_output_aliases={3: 0},
        compiler_params=_params("parallel"),
    )(dq_slots, dk_slots, dv_slots, dz)


def _dil_bwd_prep(o, do, lse, *, tm=512):
    dilations = [d for _, d in DIL_PATTERNS]
    o_chunks = ATT_W // LANE

    def body(o_ref, do_ref, lse_ref, *rest):
        outs, (do_scr, lse_scr, dl_scr) = rest[:-3], rest[-3:]
        dov = do_ref[...].astype(F32)
        prod = o_ref[...].astype(F32) * dov
        lane = lax.broadcasted_iota(jnp.int32, (tm, LANE), 1)
        delta = jnp.zeros((tm, LANE), F32)
        for h in range(N_HEADS):
            delta = jnp.where(lane == h, jnp.sum(_head(prod, h), axis=1, keepdims=True), delta)
        for ch in range(o_chunks):
            do_scr[ch] = dov[:, ch * LANE:(ch + 1) * LANE]
        lse_scr[0] = lse_ref[...]
        dl_scr[0] = delta
        for k, d in enumerate(dilations):
            for scr, out in zip((do_scr, lse_scr, dl_scr), outs[3 * k:3 * k + 3]):
                _slabs_from_rows(scr, out, d)

    row = pl.BlockSpec((tm, ATT_W), lambda i: (i, 0))
    view = lambda d, w: pl.BlockSpec((tm // d, d * w), lambda i: (i, 0))
    outs = pl.pallas_call(
        body, name="dil_bwd_prep", grid=(SEQ // tm,),
        in_specs=[row, row, pl.BlockSpec((tm, LANE), lambda i: (i, 0))],
        out_specs=[view(d, w) for d in dilations for w in (ATT_W, LANE, LANE)],
        out_shape=[_sds((SEQ // d, d * w), t) for d in dilations for w, t in ((ATT_W, BF16), (LANE, F32), (LANE, F32))],
        scratch_shapes=[pltpu.VMEM((o_chunks, tm, LANE), F32), pltpu.VMEM((1, tm, LANE), F32),
                        pltpu.VMEM((1, tm, LANE), F32)],
        compiler_params=_params("parallel"),
    )(o, do, lse)
    return [outs[3 * k:3 * k + 3] for k in range(len(dilations))]


def _rope_tables():
    half = ROPE_DIM // 2
    inv_freq = np.float32(ROPE_THETA) ** (-np.arange(half, dtype=np.float32) * np.float32(2.0) / np.float32(ROPE_DIM))
    ang = np.arange(SEQ, dtype=np.float32)[:, None] * inv_freq.astype(np.float32)[None, :]
    cos, sin = jnp.asarray(np.cos(ang).astype(np.float32)), jnp.asarray(np.sin(ang).astype(np.float32))
    ones = jnp.ones((SEQ, HEAD_DIM - ROPE_DIM), F32)
    zeros = jnp.zeros((SEQ, HEAD_DIM - ROPE_DIM), F32)
    zh = jnp.zeros((SEQ, half), F32)
    c_tab = jnp.concatenate([cos, cos, ones], axis=1)
    a_tab = jnp.concatenate([-sin, zh, zeros], axis=1)
    b_tab = jnp.concatenate([zh, sin, zeros], axis=1)
    two = lambda t: jnp.concatenate([t, t], axis=1)
    return two(c_tab), two(a_tab), two(b_tab)


def _rotate(x, c_tab, a_tab, b_tab):
    return x * c_tab + pltpu.roll(x, LANE - ROPE_DIM // 2, 1) * a_tab + pltpu.roll(x, ROPE_DIM // 2, 1) * b_tab


def _rope_fwd(zm, tabs, *, tm=512):
    width = 3 * ATT_W
    dilations = [d for _, d in DIL_PATTERNS]

    def body(q_ref, k_ref, v_ref, c_ref, a_ref, b_ref, *rest):
        outs, scr = rest[:-1], rest[-1]
        per_part = ATT_W // LANE
        for part, (x_ref, mult) in enumerate(((q_ref, SCALE), (k_ref, 1.0))):
            for cc in range(per_part):
                sl = slice(cc * LANE, (cc + 1) * LANE)
                scr[part * per_part + cc] = _rotate(x_ref[:, sl].astype(F32), c_ref[...], a_ref[...], b_ref[...]) * mult
        for cc in range(per_part):
            scr[2 * per_part + cc] = v_ref[:, cc * LANE:(cc + 1) * LANE].astype(F32)
        for o_ref, d in zip(outs, dilations):
            for r in range(d):
                for ch in range(width // LANE):
                    o_ref[:, r * width + ch * LANE:r * width + (ch + 1) * LANE] = (
                        scr.at[ch][pl.ds(r, tm // d, stride=d), :].astype(o_ref.dtype))

    tab = pl.BlockSpec((tm, LANE), lambda i: (i, 0))
    col = lambda b: pl.BlockSpec((tm, ATT_W), lambda i: (i, b))
    return pl.pallas_call(
        body, name="rope_fwd", grid=(SEQ // tm,),
        in_specs=[col(3), col(4), col(5), tab, tab, tab],
        out_specs=[pl.BlockSpec((tm // d, d * width), lambda i: (i, 0)) for d in dilations],
        out_shape=[_sds((SEQ // d, d * width), BF16) for d in dilations],
        scratch_shapes=[pltpu.VMEM((width // LANE, tm, LANE), F32)],
        compiler_params=_params("parallel"),
    )(zm, zm, zm, *tabs)


def _dil_grad_combine(dqs, dks, dvs, tabs, dz, *, tm=256):
    dilations = [d for _, d in DIL_PATTERNS]
    chunks = ATT_W // LANE

    def body(*refs):
        groups = (refs[0:3], refs[3:6], refs[6:9])
        c_ref, a_ref, b_ref, _, o_ref, scr = refs[9:]

        def total(part, cc):
            acc = None
            for g, (ref, d) in enumerate(zip(groups[part], dilations)):
                term = ref[:, cc * LANE:(cc + 1) * LANE].astype(F32) if d == 1 else scr[part, g, cc]
                acc = term if acc is None else acc + term
            return acc

        for part in range(3):
            for g, (ref, d) in enumerate(zip(groups[part], dilations)):
                if d > 1:
                    _rows_from_slabs(ref, scr.at[part, g], d)
        for cc in range(chunks):
            for part in range(2):
                o_ref[:, part * ATT_W + cc * LANE:part * ATT_W + (cc + 1) * LANE] = _rotate(
                    total(part, cc), c_ref[...], -a_ref[...], -b_ref[...]).astype(o_ref.dtype)
            o_ref[:, 2 * ATT_W + cc * LANE:2 * ATT_W + (cc + 1) * LANE] = total(2, cc).astype(o_ref.dtype)

    view = lambda d: pl.BlockSpec((tm // d, d * ATT_W), lambda i: (i, 0))
    tab = pl.BlockSpec((tm, LANE), lambda i: (i, 0))
    return pl.pallas_call(
        body, name="dil_grad_combine", grid=(SEQ // tm,),
        in_specs=[view(d) for d in dilations] * 3 + [tab] * 3 + [ANY],
        out_specs=pl.BlockSpec((tm, 3 * ATT_W), lambda i: (i, 1)),
        out_shape=_sds((SEQ, Z_MAIN), BF16),
        input_output_aliases={12: 0},
        scratch_shapes=[pltpu.VMEM((3, len(dilations), chunks, tm, LANE), F32)],
        compiler_params=_params("parallel"),
    )(*dqs, *dks, *dvs, *tabs, dz)


def _dil_valid(n):
    qi = lax.broadcasted_iota(jnp.int32, (DIL_BLK, 2 * DIL_BLK), 0)
    ki = lax.broadcasted_iota(jnp.int32, (DIL_BLK, 2 * DIL_BLK), 1)
    dist = qi + DIL_BLK - ki
    return (dist >= 0) & (dist <= DIL_BLK) & ((n > 0) | (ki >= DIL_BLK))


def _dil_fwd(qkv_v, d):
    length = SEQ // d
    nb = length // DIL_BLK

    def body(q_ref, kp_ref, kc_ref, vp_ref, vc_ref, o_ref, lse_ref):
        m_step = pl.program_id(1)
        lane = lax.broadcasted_iota(jnp.int32, (DIL_BLK, LANE), 1)
        jobs = [(sub, h) for sub in range(2) for h in range(N_HEADS)]
        rows = lambda sub: slice(sub * DIL_BLK, (sub + 1) * DIL_BLK)
        cols = lambda h: slice(h * HEAD_DIM, (h + 1) * HEAD_DIM)

        def keys(prev_ref, cur_ref, sub, h):
            before = prev_ref[:, cols(h)] if sub == 0 else cur_ref[rows(0), cols(h)]
            return jnp.concatenate([before, cur_ref[rows(sub), cols(h)]], axis=0)

        scores = [lax.dot_general(q_ref[rows(sub), cols(h)], keys(kp_ref, kc_ref, sub, h), NT_DIMS,
                                  preferred_element_type=F32) for sub, h in jobs]
        ok = [_dil_valid(m_step), _dil_valid(1)]
        probs, inv_l, lse_all = [], [], [jnp.zeros((DIL_BLK, LANE), F32)] * 2
        for idx, (sub, h) in enumerate(jobs):
            s = jnp.where(ok[sub], scores[idx], NEG_INF)
            m = jnp.max(s, axis=-1, keepdims=True)
            p = jnp.exp(s - m)
            l = jnp.sum(p, axis=-1, keepdims=True)
            probs.append(p.astype(BF16))
            inv_l.append(1.0 / l)
            lse_all[sub] = jnp.where(lane == h, m + jnp.log(l), lse_all[sub])
        outs = [jnp.dot(probs[idx], keys(vp_ref, vc_ref, sub, h), preferred_element_type=F32) * inv_l[idx]
                for idx, (sub, h) in enumerate(jobs)]
        for sub in range(2):
            o_ref[rows(sub), :] = jnp.concatenate(outs[sub * N_HEADS:(sub + 1) * N_HEADS], axis=1).astype(o_ref.dtype)
            lse_ref[rows(sub), :] = lse_all[sub]

    pair = lambda f: pl.BlockSpec((2 * DIL_BLK, ATT_W), f)
    one = lambda f: pl.BlockSpec((DIL_BLK, ATT_W), f)
    before = lambda m: jnp.maximum(2 * m - 1, 0)
    o, lse = pl.pallas_call(
        body, name=f"dil_fwd_d{d}", grid=(d, nb // 2),
        in_specs=[pair(lambda r, m: (m, 3 * r)),
                  one(lambda r, m: (before(m), 3 * r + 1)), pair(lambda r, m: (m, 3 * r + 1)),
                  one(lambda r, m: (before(m), 3 * r + 2)), pair(lambda r, m: (m, 3 * r + 2))],
        out_specs=[pair(lambda r, m: (m, r)), pl.BlockSpec((2 * DIL_BLK, LANE), lambda r, m: (m, r))],
        out_shape=[_sds((length, d * ATT_W), BF16), _sds((length, d * LANE), F32)],
        compiler_params=_params("parallel", "arbitrary"),
    )(qkv_v, qkv_v, qkv_v, qkv_v, qkv_v)
    return o, lse


def _rows_from_slabs(view_ref, scr, d):
    chunks, rows = scr.shape[0], scr.shape[1]
    for r in range(d):
        for ch in range(chunks):
            lo = (r * chunks + ch) * LANE
            scr.at[ch][pl.ds(r, rows // d, stride=d), :] = view_ref[:, lo:lo + LANE].astype(F32)


def _slabs_from_rows(scr, view_ref, d):
    chunks, rows = scr.shape[0], scr.shape[1]
    for r in range(d):
        for ch in range(chunks):
            lo = (r * chunks + ch) * LANE
            view_ref[:, lo:lo + LANE] = scr.at[ch][pl.ds(r, rows // d, stride=d), :].astype(view_ref.dtype)


def _dil_merge(os_, lses, *, tm=512):
    dilations = [d for _, d in DIL_PATTERNS]
    o_chunks = ATT_W // LANE

    def body(o0, o1, o2, l0, l1, l2, y_ref, lse_ref, o_scr, l_scr):
        os_nat, ls = [], []
        for g, (o_ref, l_ref, d) in enumerate(zip((o0, o1, o2), (l0, l1, l2), dilations)):
            if d == 1:
                os_nat.append(o_ref[...].astype(F32))
                ls.append(l_ref[...])
            else:
                _rows_from_slabs(o_ref, o_scr.at[g], d)
                _rows_from_slabs(l_ref, l_scr.at[g], d)
                os_nat.append(jnp.concatenate([o_scr[g, ch] for ch in range(o_chunks)], axis=1))
                ls.append(l_scr[g, 0])
        m = jnp.maximum(jnp.maximum(ls[0], ls[1]), ls[2])
        es = [jnp.exp(l - m) for l in ls]
        tot = es[0] + es[1] + es[2]
        lse_ref[...] = m + jnp.log(tot)
        alphas = [e / tot for e in es]
        outs = []
        for h in range(N_HEADS):
            acc = None
            for g in range(3):
                term = alphas[g][:, h:h + 1] * _head(os_nat[g], h)
                acc = term if acc is None else acc + term
            outs.append(acc)
        y_ref[...] = jnp.concatenate(outs, axis=1).astype(y_ref.dtype)

    row = pl.BlockSpec((tm, ATT_W), lambda i: (i, 0))
    vec = pl.BlockSpec((tm, LANE), lambda i: (i, 0))
    view = lambda d, w: pl.BlockSpec((tm // d, d * w), lambda i: (i, 0))
    return pl.pallas_call(
        body, name="dil_merge", grid=(SEQ // tm,),
        in_specs=[view(d, ATT_W) for d in dilations] + [view(d, LANE) for d in dilations], out_specs=[row, vec],
        out_shape=[_sds((SEQ, ATT_W), BF16), _sds((SEQ, LANE), F32)],
        scratch_shapes=[pltpu.VMEM((3, o_chunks, tm, LANE), F32), pltpu.VMEM((3, 1, tm, LANE), F32)],
        compiler_params=_params("parallel"),
    )(*os_, *lses)


def _dil_bwd(qkv_v, do_v, lse_v, dl_v, d):
    length = SEQ // d
    nb = length // DIL_BLK
    n_steps = nb // 2

    def body(q_ref, kp_ref, kc_ref, vp_ref, vc_ref, lse_ref, dl_ref, do_ref, dq_ref, dk_ref, dv_ref, dk_s, dv_s):
        m_step = pl.program_id(1)

        @pl.when(m_step == 0)
        def _():
            dk_s[...] = jnp.zeros_like(dk_s)
            dv_s[...] = jnp.zeros_like(dv_s)

        jobs = [(sub, h) for sub in range(2) for h in range(N_HEADS)]
        rows = lambda sub: slice(sub * DIL_BLK, (sub + 1) * DIL_BLK)
        cols = lambda h: slice(h * HEAD_DIM, (h + 1) * HEAD_DIM)

        def keys(prev_ref, cur_ref, sub, h):
            before = prev_ref[:, cols(h)] if sub == 0 else cur_ref[rows(0), cols(h)]
            return jnp.concatenate([before, cur_ref[rows(sub), cols(h)]], axis=0)

        kks = [keys(kp_ref, kc_ref, sub, h) for sub, h in jobs]
        scores = [lax.dot_general(q_ref[rows(sub), cols(h)], kks[idx], NT_DIMS, preferred_element_type=F32)
                  for idx, (sub, h) in enumerate(jobs)]
        dps = [lax.dot_general(do_ref[rows(sub), cols(h)], keys(vp_ref, vc_ref, sub, h), NT_DIMS,
                               preferred_element_type=F32) for sub, h in jobs]
        ok = [_dil_valid(m_step), _dil_valid(1)]
        ps, dss = [], []
        for idx, (sub, h) in enumerate(jobs):
            p = jnp.where(ok[sub], jnp.exp(scores[idx] - lse_ref[rows(sub), h:h + 1]), 0.0)
            ps.append(p.astype(BF16))
            dss.append((p * (dps[idx] - dl_ref[rows(sub), h:h + 1])).astype(BF16))
        dqs = [jnp.dot(dss[idx], kks[idx], preferred_element_type=F32) * SCALE for idx in range(len(jobs))]
        dkks = [lax.dot_general(dss[idx], q_ref[rows(sub), cols(h)], TN_DIMS, preferred_element_type=F32)
                for idx, (sub, h) in enumerate(jobs)]
        dvvs = [lax.dot_general(ps[idx], do_ref[rows(sub), cols(h)], TN_DIMS, preferred_element_type=F32)
                for idx, (sub, h) in enumerate(jobs)]
        for sub in range(2):
            dq_ref[rows(sub), :] = jnp.concatenate(dqs[sub * N_HEADS:(sub + 1) * N_HEADS], axis=1).astype(dq_ref.dtype)
        base = m_step * (2 * DIL_BLK)
        blocks = [pl.ds(pl.multiple_of(jnp.maximum(base - DIL_BLK, 0), DIL_BLK), DIL_BLK),
                  pl.ds(pl.multiple_of(base, DIL_BLK), DIL_BLK),
                  pl.ds(pl.multiple_of(base + DIL_BLK, DIL_BLK), DIL_BLK)]
        for acc, parts in ((dk_s, dkks), (dv_s, dvvs)):
            top = lambda sub: jnp.concatenate([parts[sub * N_HEADS + h][:DIL_BLK] for h in range(N_HEADS)], axis=1)
            bottom = lambda sub: jnp.concatenate([parts[sub * N_HEADS + h][DIL_BLK:] for h in range(N_HEADS)], axis=1)
            acc[blocks[0], :] += top(0)
            acc[blocks[1], :] += bottom(0) + top(1)
            acc[blocks[2], :] += bottom(1)

        @pl.when(m_step == n_steps - 1)
        def _():
            dk_ref[...] = dk_s[...].astype(dk_ref.dtype)
            dv_ref[...] = dv_s[...].astype(dv_ref.dtype)

    pair = lambda f: pl.BlockSpec((2 * DIL_BLK, ATT_W), f)
    one = lambda f: pl.BlockSpec((DIL_BLK, ATT_W), f)
    vec = lambda f: pl.BlockSpec((2 * DIL_BLK, LANE), f)
    whole = pl.BlockSpec((length, ATT_W), lambda r, m: (0, r))
    before = lambda m: jnp.maximum(2 * m - 1, 0)
    outs = pl.pallas_call(
        body, name=f"dil_bwd_d{d}", grid=(d, n_steps),
        in_specs=[pair(lambda r, m: (m, 3 * r)),
                  one(lambda r, m: (before(m), 3 * r + 1)), pair(lambda r, m: (m, 3 * r + 1)),
                  one(lambda r, m: (before(m), 3 * r + 2)), pair(lambda r, m: (m, 3 * r + 2)),
                  vec(lambda r, m: (m, r)), vec(lambda r, m: (m, r)), pair(lambda r, m: (m, r))],
        out_specs=[pair(lambda r, m: (m, r)), whole, whole],
        out_shape=[_sds((length, d * ATT_W), BF16)] * 3,
        scratch_shapes=[pltpu.VMEM((length, ATT_W), F32), pltpu.VMEM((length, ATT_W), F32)],
        compiler_params=_params("arbitrary", "arbitrary"),
    )(qkv_v, qkv_v, qkv_v, qkv_v, qkv_v, lse_v, dl_v, do_v)
    return outs


def _sigmoid(x):
    return 1.0 / (1.0 + jnp.exp(-x))


def _mix_fwd(ya, yb, w_oa, w_ob, zm, *, tm=512):
    def body(ya_ref, yb_ref, wa_ref, wb_ref, ga_ref, gb_ref, pa_ref, pb_ref, mix_ref):
        pa = jnp.dot(ya_ref[...], wa_ref[...], preferred_element_type=F32)
        pb = jnp.dot(yb_ref[...], wb_ref[...], preferred_element_type=F32)
        pa_ref[...] = pa.astype(pa_ref.dtype)
        pb_ref[...] = pb.astype(pb_ref.dtype)
        mix_ref[...] = (_sigmoid(ga_ref[...].astype(F32)) * pa + _sigmoid(gb_ref[...].astype(F32)) * pb
                        ).astype(mix_ref.dtype)

    row = pl.BlockSpec((tm, ATT_W), lambda i: (i, 0))
    wsp = pl.BlockSpec((ATT_W, D_MODEL), lambda i: (0, 0))
    wide = pl.BlockSpec((tm, D_MODEL), lambda i: (i, 0))
    return pl.pallas_call(
        body, name="mix_fwd", grid=(SEQ // tm,),
        in_specs=[row, row, wsp, wsp, pl.BlockSpec((tm, D_MODEL), lambda i: (i, 3)),
                  pl.BlockSpec((tm, D_MODEL), lambda i: (i, 4))],
        out_specs=[wide] * 3, out_shape=[_sds((SEQ, D_MODEL), BF16)] * 3,
        compiler_params=_params("parallel"),
    )(ya, yb, w_oa, w_ob, zm, zm)


def _gate_bwd(dmix, zm, p, gate_block, dz, *, name, tm=512):
    def body(dm_ref, g_ref, p_ref, *rest):
        dp_ref, dz_ref = rest[-2], rest[-1]
        dm = dm_ref[...].astype(F32)
        s = _sigmoid(g_ref[...].astype(F32))
        dp_ref[...] = (dm * s).astype(dp_ref.dtype)
        dz_ref[...] = (dm * p_ref[...].astype(F32) * s * (1.0 - s)).astype(dz_ref.dtype)

    wide = pl.BlockSpec((tm, D_MODEL), lambda i: (i, 0))
    gate = pl.BlockSpec((tm, D_MODEL), lambda i: (i, gate_block))
    extra = [] if dz is None else [dz]
    return pl.pallas_call(
        body, name=name, grid=(SEQ // tm,),
        in_specs=[wide, gate, wide] + [ANY] * len(extra),
        out_specs=[wide, gate],
        out_shape=[_sds((SEQ, D_MODEL), BF16), _sds((SEQ, Z_MAIN), BF16)],
        input_output_aliases={3: 1} if extra else {},
        compiler_params=_params("parallel"),
    )(dmix, zm, p, *extra)


def _out_fwd(mixed, w_out, x, g_post, g_pre, *, tm=512):
    def body(m_ref, w_ref, x_ref, gp_ref, gn_ref, y_ref, x2_ref, h_ref):
        y = jnp.dot(m_ref[...], w_ref[...], preferred_element_type=F32)
        y_ref[...] = y
        r = lax.rsqrt(jnp.mean(y * y, axis=-1, keepdims=True) + RMS_EPS)
        x2 = x_ref[...] + y * r * gp_ref[...]
        x2_ref[...] = x2
        r2 = lax.rsqrt(jnp.mean(x2 * x2, axis=-1, keepdims=True) + RMS_EPS)
        h_ref[...] = (x2 * r2 * gn_ref[...]).astype(h_ref.dtype)

    row = pl.BlockSpec((tm, D_MODEL), lambda i: (i, 0))
    vec = pl.BlockSpec((1, D_MODEL), lambda i: (0, 0))
    return pl.pallas_call(
        body, name="out_fwd", grid=(SEQ // tm,),
        in_specs=[row, pl.BlockSpec((D_MODEL, D_MODEL), lambda i: (0, 0)), row, vec, vec],
        out_specs=[row] * 3,
        out_shape=[_sds((SEQ, D_MODEL), F32), _sds((SEQ, D_MODEL), F32), _sds((SEQ, D_MODEL), BF16)],
        compiler_params=_params("parallel"),
    )(mixed, w_out, x, g_post, g_pre)


FFN_TM = 2048
FFN_HALF = 256
FFN_TN = 2 * FFN_HALF
FFN_NJ = D_FF // FFN_HALF
FFN_GROUP = 2 * SUBLANE


def _ffn_interleave(t):
    lead = t.shape[:-1]
    return jnp.swapaxes(t.reshape(*lead, 2, FFN_NJ, FFN_HALF), -3, -2).reshape(*lead, 2 * D_FF)


def _ffn_deinterleave(t):
    lead = t.shape[:-1]
    return jnp.swapaxes(t.reshape(*lead, FFN_NJ, 2, FFN_HALF), -3, -2).reshape(*lead, 2 * D_FF)


W_IN_SHARD = (Z_MAIN + N_HEADS) // N_DEV
FORGET_LO = 3 * ATT_W


def _w_in_from_shards(shards, *, tm=256):
    def columns(g_ref, lo, width):
        p, off = divmod(lo, W_IN_SHARD)
        if off + width <= W_IN_SHARD:
            return g_ref[p, :, off:off + width]
        first = W_IN_SHARD - off
        return jnp.concatenate([g_ref[p, :, off:], g_ref[p + 1, :, :width - first]], axis=1)

    def body(g_ref, main_ref, f_ref):
        for t in range(Z_MAIN // LANE):
            lo = t * LANE
            main_ref[:, lo:lo + LANE] = columns(g_ref, lo if lo < FORGET_LO else lo + N_HEADS, LANE)
        f_ref[...] = jnp.concatenate([columns(g_ref, FORGET_LO, N_HEADS),
                                      jnp.zeros((tm, F_PAD - N_HEADS), f_ref.dtype)], axis=1)

    return pl.pallas_call(
        body, name="w_in_from_shards", grid=(D_MODEL // tm,),
        in_specs=[pl.BlockSpec((N_DEV, tm, W_IN_SHARD), lambda i: (0, i, 0))],
        out_specs=[pl.BlockSpec((tm, Z_MAIN), lambda i: (i, 0)), pl.BlockSpec((tm, F_PAD), lambda i: (i, 0))],
        out_shape=[_sds((D_MODEL, Z_MAIN), shards.dtype), _sds((D_MODEL, F_PAD), shards.dtype)],
        compiler_params=_params("parallel"),
    )(shards)


def _w_in_to_shards(g_main, g_f, *, tm=256):
    def natural(main_ref, f_ref, lo, width):
        pieces, hi = [], lo + width
        for ref, start, stop, shift in ((main_ref, 0, FORGET_LO, 0), (f_ref, FORGET_LO, FORGET_LO + N_HEADS, FORGET_LO),
                                        (main_ref, FORGET_LO + N_HEADS, Z_MAIN + N_HEADS, N_HEADS)):
            a, b = max(lo, start), min(hi, stop)
            if a < b:
                pieces.append(ref[:, a - shift:b - shift])
        return pieces[0] if len(pieces) == 1 else jnp.concatenate(pieces, axis=1)

    def body(main_ref, f_ref, o_ref):
        for p in range(N_DEV):
            for q in range(-(-W_IN_SHARD // LANE)):
                width = min(LANE, W_IN_SHARD - q * LANE)
                o_ref[p, :, q * LANE:q * LANE + width] = natural(main_ref, f_ref, p * W_IN_SHARD + q * LANE, width)

    return pl.pallas_call(
        body, name="w_in_to_shards", grid=(D_MODEL // tm,),
        in_specs=[pl.BlockSpec((tm, Z_MAIN), lambda i: (i, 0)), pl.BlockSpec((tm, F_PAD), lambda i: (i, 0))],
        out_specs=pl.BlockSpec((N_DEV, tm, W_IN_SHARD), lambda i: (0, i, 0)),
        out_shape=_sds((N_DEV, D_MODEL, W_IN_SHARD), g_main.dtype),
        compiler_params=_params("parallel"),
    )(g_main, g_f)


W_UP_SHARD = 2 * D_FF // N_DEV


def _w_up_lane_tile(k):
    block = k // 2
    return (2 * (block % FFN_NJ) + block // FFN_NJ) * FFN_HALF + (k % 2) * LANE


def _w_up_from_shards(shards, *, tm=256):
    def body(g_ref, o_ref):
        for k in range(2 * D_FF // LANE):
            p, off = divmod(k * LANE, W_UP_SHARD)
            if off + LANE <= W_UP_SHARD:
                tile = g_ref[p, :, off:off + LANE]
            else:
                tile = jnp.concatenate([g_ref[p, :, off:], g_ref[p + 1, :, :off + LANE - W_UP_SHARD]], axis=1)
            dst = _w_up_lane_tile(k)
            o_ref[:, dst:dst + LANE] = tile

    return pl.pallas_call(
        body, name="w_up_from_shards", grid=(D_MODEL // tm,),
        in_specs=[pl.BlockSpec((N_DEV, tm, W_UP_SHARD), lambda i: (0, i, 0))],
        out_specs=pl.BlockSpec((tm, 2 * D_FF), lambda i: (i, 0)),
        out_shape=_sds((D_MODEL, 2 * D_FF), shards.dtype),
        compiler_params=_params("parallel"),
    )(shards)


def _w_up_to_shards(t, *, tm=256):
    def body(x_ref, o_ref):
        for p in range(N_DEV):
            for q in range(-(-W_UP_SHARD // LANE)):
                width = min(LANE, W_UP_SHARD - q * LANE)
                k, off = divmod(p * W_UP_SHARD + q * LANE, LANE)
                src = _w_up_lane_tile(k)
                if off == 0:
                    tile = x_ref[:, src:src + width]
                else:
                    tile = x_ref[:, src + off:src + LANE]
                    if width > LANE - off:
                        nxt = _w_up_lane_tile(k + 1)
                        tile = jnp.concatenate([tile, x_ref[:, nxt:nxt + width - (LANE - off)]], axis=1)
                o_ref[p, :, q * LANE:q * LANE + width] = tile

    return pl.pallas_call(
        body, name="w_up_to_shards", grid=(D_MODEL // tm,),
        in_specs=[pl.BlockSpec((tm, 2 * D_FF), lambda i: (i, 0))],
        out_specs=pl.BlockSpec((N_DEV, tm, W_UP_SHARD), lambda i: (0, i, 0)),
        out_shape=_sds((N_DEV, D_MODEL, W_UP_SHARD), t.dtype),
        compiler_params=_params("parallel"),
    )(t)


def _gelu_parts(a):
    c = math.sqrt(2.0 / math.pi)
    a2 = a * a
    t = jnp.tanh((c * a) * (1.0 + 0.044715 * a2))
    half_a, one_t = 0.5 * a, 1.0 + t
    gelu = half_a * one_t
    dgelu = 0.5 * one_t + half_a * (1.0 - t * t) * (c + (3.0 * 0.044715 * c) * a2)
    return gelu, dgelu


def _row_masks(down):
    row = lax.broadcasted_iota(jnp.int32, (SUBLANE, FFN_TN), 0)
    return (row < 1, row < 2) if down else (row >= SUBLANE - 1, row >= SUBLANE - 2)


def _rolled(x, down):
    return (pltpu.roll(x, 1, 0), pltpu.roll(x, 2, 0)) if down else (
        pltpu.roll(x, SUBLANE - 1, 0), pltpu.roll(x, SUBLANE - 2, 0))


def _shifted(cur_rolled, neighbour_rolled, masks):
    return (jnp.where(masks[0], neighbour_rolled[0], cur_rolled[0]),
            jnp.where(masks[1], neighbour_rolled[1], cur_rolled[1]))


def _conv_consts(w_ref, b_ref):
    shape = (SUBLANE, FFN_TN)
    return [jnp.broadcast_to(w_ref[k:k + 1, :], shape) for k in range(3)] + [jnp.broadcast_to(b_ref[...], shape)]


def _ffn_mid_fwd(u, conv_w, conv_b):
    per = FFN_TM // SUBLANE

    def body(u_ref, h_ref, w_ref, b_ref, m_ref, ab_ref):
        live = (pl.program_id(1) > 0).astype(F32)
        w0, w1, w2, bias = _conv_consts(w_ref, b_ref)
        masks = _row_masks(True)

        def group(g, above):
            rows = pl.ds(pl.multiple_of(g * FFN_GROUP, FFN_GROUP), FFN_GROUP)
            x = u_ref[rows, :].astype(F32)
            convs = []
            for c in range(2):
                cur = x[c * SUBLANE:(c + 1) * SUBLANE]
                cur_rolled = _rolled(cur, True)
                s1, s2 = _shifted(cur_rolled, above, masks)
                convs.append(w0 * s2 + w1 * s1 + w2 * cur + bias)
                above = cur_rolled
            y = jnp.concatenate(convs, axis=0)
            ab_ref[rows, :] = y.astype(ab_ref.dtype)
            m_ref[rows, :] = (_gelu_parts(y[:, :FFN_HALF])[0] * y[:, FFN_HALF:]).astype(m_ref.dtype)
            return above

        lax.fori_loop(0, FFN_TM // (2 * FFN_GROUP), lambda g2, carry: group(2 * g2 + 1, group(2 * g2, carry)),
                      _rolled(h_ref[...].astype(F32) * live, True))

    blk = pl.BlockSpec((FFN_TM, FFN_TN), lambda j, i: (i, j))
    return pl.pallas_call(
        body, name="ffn_mid_fwd", grid=(FFN_NJ, SEQ // FFN_TM),
        in_specs=[blk, pl.BlockSpec((SUBLANE, FFN_TN), lambda j, i: (jnp.maximum(i * per - 1, 0), j)),
                  pl.BlockSpec((3, FFN_TN), lambda j, i: (0, j)), pl.BlockSpec((1, FFN_TN), lambda j, i: (0, j))],
        out_specs=[pl.BlockSpec((FFN_TM, FFN_HALF), lambda j, i: (i, j)), blk],
        out_shape=[_sds((SEQ, D_FF), BF16), _sds((SEQ, 2 * D_FF), BF16)],
        compiler_params=_params("parallel", "arbitrary"),
    )(u, u, conv_w, conv_b)


def _ffn_mid_bwd(dm, u, ab, conv_w):
    nrow = SEQ // FFN_TM
    n_groups = FFN_TM // FFN_GROUP

    def body(dm_ref, u_ref, ab_ref, w_ref, du_ref, gw_ref, gb_ref, c_s):
        @pl.when(pl.program_id(1) == 0)
        def _():
            c_s[...] = jnp.zeros_like(c_s)
            gw_ref[...] = jnp.zeros_like(gw_ref)
            gb_ref[...] = jnp.zeros_like(gb_ref)

        taps = [jnp.broadcast_to(w_ref[k:k + 1, :], (SUBLANE, FFN_TN)) for k in range(3)]
        masks = _row_masks(False)

        def group(t, carry):
            below, acc = carry
            rows = pl.ds(pl.multiple_of((n_groups - 1 - t) * FFN_GROUP, FFN_GROUP), FFN_GROUP)
            x, y, dmv = u_ref[rows, :].astype(F32), ab_ref[rows, :].astype(F32), dm_ref[rows, :].astype(F32)
            gelu, dgelu = _gelu_parts(y[:, :FFN_HALF])
            d = jnp.concatenate([dmv * y[:, FFN_HALF:] * dgelu, dmv * gelu], axis=1)
            acc, pre = list(acc), [None, None]
            for c in (1, 0):
                sl = slice(c * SUBLANE, (c + 1) * SUBLANE)
                cur, xs = d[sl], x[sl]
                cur_rolled = _rolled(cur, False)
                up1, up2 = _shifted(cur_rolled, below, masks)
                acc = [acc[0] + up2 * xs, acc[1] + up1 * xs, acc[2] + cur * xs, acc[3] + cur]
                pre[c] = taps[2] * cur + taps[1] * up1 + taps[0] * up2
                below = cur_rolled
            du_ref[rows, :] = jnp.concatenate(pre, axis=0).astype(du_ref.dtype)
            return below, tuple(acc)

        zeros = jnp.zeros((SUBLANE, FFN_TN), F32)
        below, acc = lax.fori_loop(0, n_groups // 2, lambda t2, carry: group(2 * t2 + 1, group(2 * t2, carry)),
                                   (_rolled(c_s[...], False), (zeros,) * 4))
        c_s[...] = pltpu.roll(below[0], 1, 0)
        for k in range(3):
            gw_ref[k:k + 1, :] += jnp.sum(acc[k], axis=0, keepdims=True)
        gb_ref[...] += jnp.sum(acc[3], axis=0, keepdims=True)

    blk = pl.BlockSpec((FFN_TM, FFN_TN), lambda j, i: (nrow - 1 - i, j))
    return pl.pallas_call(
        body, name="ffn_mid_bwd", grid=(FFN_NJ, nrow),
        in_specs=[pl.BlockSpec((FFN_TM, FFN_HALF), lambda j, i: (nrow - 1 - i, j)), blk, blk,
                  pl.BlockSpec((3, FFN_TN), lambda j, i: (0, j))],
        out_specs=[blk, pl.BlockSpec((3, FFN_TN), lambda j, i: (0, j)), pl.BlockSpec((1, FFN_TN), lambda j, i: (0, j))],
        out_shape=[_sds((SEQ, 2 * D_FF), BF16), _sds((3, 2 * D_FF), F32), _sds((1, 2 * D_FF), F32)],
        scratch_shapes=[pltpu.VMEM((SUBLANE, FFN_TN), F32)],
        compiler_params=_params("parallel", "arbitrary"),
    )(dm, u, ab, conv_w)


def _down_fwd(m, w_down, x2, g_post, target, *, tm=512):
    def body(m_ref, w_ref, x2_ref, g_ref, t_ref, dout_ref, dy_ref, gg_ref, loss_ref):
        @pl.when(pl.program_id(0) == 0)
        def _():
            gg_ref[...] = jnp.zeros_like(gg_ref)
            loss_ref[...] = jnp.zeros_like(loss_ref)

        y = jnp.dot(m_ref[...], w_ref[...], preferred_element_type=F32)
        r = lax.rsqrt(jnp.mean(y * y, axis=-1, keepdims=True) + RMS_EPS)
        yn = y * r
        diff = (x2_ref[...] + yn * g_ref[...]) - t_ref[...]
        loss_ref[...] += jnp.sum(diff * diff)
        dout = diff * (1.0 / D_MODEL)
        dout_ref[...] = dout
        gg_ref[...] += jnp.sum(dout * yn, axis=0, keepdims=True)
        dn = dout * g_ref[...]
        dy_ref[...] = (r * (dn - yn * jnp.mean(dn * yn, axis=-1, keepdims=True))).astype(dy_ref.dtype)

    row = pl.BlockSpec((tm, D_MODEL), lambda i: (i, 0))
    vec = pl.BlockSpec((1, D_MODEL), lambda i: (0, 0))
    return pl.pallas_call(
        body, name="down_fwd", grid=(SEQ // tm,),
        in_specs=[pl.BlockSpec((tm, D_FF), lambda i: (i, 0)), pl.BlockSpec((D_FF, D_MODEL), lambda i: (0, 0)),
                  row, vec, row],
        out_specs=[row, row, vec, pl.BlockSpec((1, LANE), lambda i: (0, 0))],
        out_shape=[_sds((SEQ, D_MODEL), F32), _sds((SEQ, D_MODEL), BF16), _sds((1, D_MODEL), F32),
                   _sds((1, LANE), F32)],
        compiler_params=_params("arbitrary"),
    )(m, w_down, x2, g_post, target)


def _local_step(x, target, w_main, w_f, b_forget, conv_b, g_pre_mix, g_post_mix, g_pre_ffn, g_post_ffn,
                late_weights, ffn_grads_ready, proj_grads_ready, mixer_grads_ready):
    mm = _matmul
    tabs = _rope_tables()

    h1 = _rms_fwd(x, g_pre_mix, name="rms_pre_mix")
    zm = mm(h1, w_main, out_dtype=BF16, tm=2048, tn=512, tk=1024, name="in_proj")
    zf = mm(h1, w_f, out_dtype=F32, tm=2048, tn=F_PAD, tk=1024, name="in_proj_forget")
    f_row, sg_row = _fox_prep(zf[:, :N_HEADS].T, b_forget.reshape(N_HEADS, 1))
    f_cols = jnp.pad(f_row.T, ((0, 0), (0, LANE - N_HEADS)))
    q_slots, k_slots, v_slots = _fox_pack_fwd(zm, f_cols)
    ya, lse_a = _fox_fwd(q_slots, k_slots, v_slots)
    qkv_d = dict(zip([d for _, d in DIL_PATTERNS], _rope_fwd(zm, tabs)))
    dil = [_dil_fwd(qkv_d[d], d) for _, d in DIL_PATTERNS]
    yb, lse_b = _dil_merge([o for o, _ in dil], [l for _, l in dil])
    w_oa, w_ob, w_out, w_up, conv_w, w_down = late_weights(yb)
    pa, pb, mixed = _mix_fwd(ya, yb, w_oa, w_ob, zm)
    y1, x2, h2 = _out_fwd(mixed, w_out, x, g_post_mix, g_pre_ffn)
    u = mm(h2, w_up, out_dtype=BF16, tm=2048, tn=512, tk=1024, name="up_proj")
    m, ab = _ffn_mid_fwd(u, conv_w, _ffn_interleave(conv_b))
    dout, dy2, gg_post_ffn, sq_err = _down_fwd(m, w_down, x2, g_post_ffn, target)

    g_w_down = mm(m, dy2, ta=True, out_dtype=BF16, tm=D_FF // 2, tn=1024, tk=2048, name="grad_w_down")
    dm = mm(dy2, w_down, tb=True, out_dtype=BF16, tm=2048, tn=D_FF // 2, tk=1024, name="d_ffn_mid")
    du, g_conv_w, g_conv_b = _ffn_mid_bwd(dm, u, ab, conv_w)
    g_w_up = mm(h2, du, ta=True, out_dtype=BF16, tm=1024, tn=D_FF // 2, tk=2048, name="grad_w_up")
    tok = ffn_grads_ready(dict(w_down=g_w_down, w_up_blocks=g_w_up, conv_w=_ffn_deinterleave(g_conv_w)))
    dh2 = mm(du, w_up, tb=True, out_dtype=BF16, tm=512, tn=1024, tk=2 * D_FF, name="d_h2")

    dx2, dy1, gg_pre_ffn, gg_post_mix = _rms_pair_bwd([dh2], x2, g_pre_ffn, dout, y1, g_post_mix + tok)
    g_w_out = mm(mixed, dy1, ta=True, out_dtype=BF16, tm=1024, tn=1024, tk=2048, name="grad_w_out")
    dmix = mm(dy1, w_out, tb=True, out_dtype=BF16, tm=2048, tn=1024, tk=1024, name="d_mixed")
    dpa, dz = _gate_bwd(dmix, zm, pa, 3, None, name="gate_bwd_fox")
    dpb, dz = _gate_bwd(dmix, zm, pb, 4, dz, name="gate_bwd_dil")
    g_w_oa = mm(ya, dpa, ta=True, out_dtype=BF16, tm=512, tn=1024, tk=SEQ, name="grad_w_o_fox")
    g_w_ob = mm(yb, dpb, ta=True, out_dtype=BF16, tm=512, tn=1024, tk=SEQ, name="grad_w_o_dil")
    tok = proj_grads_ready(dict(w_o_fox=g_w_oa, w_o_dil=g_w_ob, w_out=g_w_out))
    dya = mm(dpa, w_oa, tb=True, out_dtype=BF16, tm=2048, tn=512, tk=1024, name="d_y_fox")
    dyb = mm(dpb, w_ob, tb=True, out_dtype=BF16, tm=2048, tn=512, tk=1024, name="d_y_dil")

    qb_slots, do_slots = _fox_pack_bwd(zm, f_cols + tok, lse_a, ya, dya)
    dz, df_cols = _fox_unpack(*_fox_bwd(qb_slots, k_slots, v_slots, do_slots), dz)
    dfa_t, g_b_forget = _fox_post_bwd(df_cols[:, :N_HEADS].T, sg_row)

    rows_d = _dil_bwd_prep(yb, dyb, lse_b)
    dil_g = [_dil_bwd(qkv_d[d], *rows_d[k], d) for k, (_, d) in enumerate(DIL_PATTERNS)]
    dz = _dil_grad_combine([g[0] for g in dil_g], [g[1] for g in dil_g], [g[2] for g in dil_g], tabs, dz)

    dzf = jnp.pad(dfa_t.T, ((0, 0), (0, F_PAD - N_HEADS)))
    g_w_main = mm(h1, dz, ta=True, out_dtype=BF16, tm=1024, tn=Z_MAIN // 4, tk=2048, name="grad_w_in")
    g_w_f = mm(h1, dzf, ta=True, out_dtype=BF16, tm=1024, tn=F_PAD, tk=1024, name="grad_w_in_forget")
    tok = mixer_grads_ready(dict(w_main=g_w_main, w_f=g_w_f))
    dh1 = [mm(dz, w_main, tb=True, out_dtype=BF16, tm=512, tn=1024, tk=Z_MAIN, name="d_h1"),
           mm(dzf + tok, w_f, tb=True, out_dtype=BF16, tm=2048, tn=1024, tk=F_PAD, name="d_h1_forget")]
    grad_x, gg_pre_mix = _rms_bwd(dh1, x, g_pre_mix, dx2, out_dtype=F32, name="rms_pre_mix_bwd")

    grads = dict(
        b_forget=g_b_forget.reshape(1, N_HEADS), conv_b=_ffn_deinterleave(g_conv_b),
        g_pre_mix=gg_pre_mix, g_post_mix=gg_post_mix, g_pre_ffn=gg_pre_ffn, g_post_ffn=gg_post_ffn)
    return sq_err, grad_x, grads


def _exchange(arrays, scatter, *, name):
    n = len(arrays)
    scatters = [scatter] * n if isinstance(scatter, bool) else list(scatter)

    def body(*refs):
        ins, outs = refs[:n], refs[n:2 * n]
        send_sems, recv_sems, local_sems = refs[2 * n:]
        me, peers = _peers()

        def remote(a, k):
            dev, slot = peers[k]
            return pltpu.make_async_remote_copy(
                src_ref=ins[a].at[slot] if scatters[a] else ins[a], dst_ref=outs[a].at[me],
                send_sem=send_sems.at[a, k], recv_sem=recv_sems.at[a, k],
                device_id=dev, device_id_type=MESH_ID)

        def landed(a, k):
            dev, slot = peers[k]
            return pltpu.make_async_remote_copy(
                src_ref=outs[a].at[slot], dst_ref=outs[a].at[slot],
                send_sem=send_sems.at[a, k], recv_sem=recv_sems.at[a, k],
                device_id=dev, device_id_type=MESH_ID)

        own = [pltpu.make_async_copy(ins[a].at[me] if scatters[a] else ins[a], outs[a].at[me], local_sems.at[a])
               for a in range(n)]
        copies = [remote(a, k) for k in range(N_DEV - 1) for a in range(n)]
        for cp in own + copies:
            cp.start()
        for k in range(N_DEV - 1):
            for a in range(n):
                landed(a, k).wait_recv()
        for cp in copies:
            cp.wait_send()
        for cp in own:
            cp.wait()

    out_shape = [_sds(((N_DEV,) + a.shape[-2:]), a.dtype) for a in arrays]
    return pl.pallas_call(
        body, name=name, in_specs=[ANY] * n, out_specs=[ANY] * n, out_shape=out_shape,
        scratch_shapes=[pltpu.SemaphoreType.DMA((n, N_DEV - 1)), pltpu.SemaphoreType.DMA((n, N_DEV - 1)),
                        pltpu.SemaphoreType.DMA((n,))],
    )(*arrays)


def _gather_two_level(shard, *, name):
    def body(x_ref, out_ref, send_sems, recv_sems, local_sem):
        x, y, c = lax.axis_index("x"), lax.axis_index("y"), lax.axis_index("c")
        me, sibling = (x, y, c), (x, y, 1 - c)
        chips = [(1 - x, y), (x, 1 - y), (1 - x, 1 - y)]

        def slot(px, py, pc):
            return out_ref.at[4 * px + 2 * py + pc]

        def copy(k, block, to, src=None):
            return pltpu.make_async_remote_copy(
                src_ref=slot(*block) if src is None else src, dst_ref=slot(*block),
                send_sem=send_sems.at[k], recv_sem=recv_sems.at[k], device_id=to, device_id_type=MESH_ID)

        mine = pltpu.make_async_copy(x_ref, slot(*me), local_sem)
        mine.start()
        first = [copy(0, me, sibling, src=x_ref)]
        first += [copy(1 + j, me, (*chip, c), src=x_ref) for j, chip in enumerate(chips)]
        for cp in first:
            cp.start()
        passed = [copy(4 + j, (*chip, c), sibling) for j, chip in enumerate(chips)]
        for j, chip in enumerate(chips):
            copy(1 + j, (*chip, c), me).wait_recv()
            passed[j].start()
        copy(0, sibling, me).wait_recv()
        for j, chip in enumerate(chips):
            copy(4 + j, (*chip, 1 - c), me).wait_recv()
        for cp in first + passed:
            cp.wait_send()
        mine.wait()

    return pl.pallas_call(
        body, name=name, in_specs=[ANY], out_specs=ANY, out_shape=_sds((N_DEV,) + shard.shape, shard.dtype),
        scratch_shapes=[pltpu.SemaphoreType.DMA((N_DEV - 1,)), pltpu.SemaphoreType.DMA((N_DEV - 1,)),
                        pltpu.SemaphoreType.DMA],
    )(shard)


N_CHIPS = N_DEV // 2


def _peers(chips_only=False):
    x, y, c = lax.axis_index("x"), lax.axis_index("y"), lax.axis_index("c")
    out = []
    if chips_only:
        for k in range(1, N_CHIPS):
            px = 1 - x if k & 2 else x
            py = 1 - y if k & 1 else y
            out.append(((px, py, c), 2 * px + py))
        return 2 * x + y, out
    for k in range(1, N_DEV):
        px = 1 - x if k & 4 else x
        py = 1 - y if k & 2 else y
        pc = 1 - c if k & 1 else c
        out.append(((px, py, pc), 4 * px + 2 * py + pc))
    return 4 * x + 2 * y + c, out


def _sibling_swap(slot_arrays, *, name):
    n = len(slot_arrays)

    def body(*refs):
        ins, outs, send_sems, recv_sems = refs[:n], refs[n:2 * n], refs[2 * n], refs[2 * n + 1]
        x, y, c = lax.axis_index("x"), lax.axis_index("y"), lax.axis_index("c")
        copies = [pltpu.make_async_remote_copy(
            src_ref=ins[a].at[2 * q + (1 - c)], dst_ref=outs[a].at[q], send_sem=send_sems.at[a, q],
            recv_sem=recv_sems.at[a, q], device_id=(x, y, 1 - c), device_id_type=MESH_ID)
            for a in range(n) for q in range(N_CHIPS)]
        for cp in copies:
            cp.start()
        for cp in copies:
            cp.wait_recv()
        for cp in copies:
            cp.wait_send()

    return pl.pallas_call(
        body, name=name, in_specs=[ANY] * n, out_specs=[ANY] * n,
        out_shape=[_sds((N_CHIPS,) + t.shape[1:], t.dtype) for t in slot_arrays],
        scratch_shapes=[pltpu.SemaphoreType.DMA((n, N_CHIPS)), pltpu.SemaphoreType.DMA((n, N_CHIPS))],
    )(*slot_arrays)


def _pair_sum(slots, from_sibling, *, name, tn):
    _, r, c = slots.shape
    core = lax.axis_index("c").astype(jnp.int32).reshape(1)

    def body(core_ref, a_ref, b_ref, o_ref):
        o_ref[...] = (a_ref[...].astype(F32) + b_ref[...].astype(F32)).astype(o_ref.dtype)

    blk = lambda f: pl.BlockSpec((1, r, tn), f)
    return pl.pallas_call(
        body, name=name,
        grid_spec=pltpu.PrefetchScalarGridSpec(
            num_scalar_prefetch=1, grid=(N_CHIPS, c // tn),
            in_specs=[blk(lambda q, j, core: (2 * q + core[0], 0, j)), blk(lambda q, j, core: (q, 0, j))],
            out_specs=blk(lambda q, j, core: (q, 0, j))),
        out_shape=_sds((N_CHIPS, r, c), slots.dtype),
        compiler_params=_params("parallel", "parallel"),
    )(core, slots, from_sibling)


HBM = pl.BlockSpec(memory_space=pltpu.HBM)
SEM = pl.BlockSpec(memory_space=pltpu.SEMAPHORE)
DATAFLOW = pltpu.SideEffectType.DATAFLOW_SIDE_EFFECTING


def _split_copy(srcs, lands, send_sems, recv_sems, scatter, a, k, me, peers, incoming=False):
    dev, slot = peers[k]
    if incoming:
        src = dst = lands[a].at[slot]
    else:
        src, dst = (srcs[a].at[slot] if scatter else srcs[a]), lands[a].at[me]
    sem = a * len(peers) + k
    return pltpu.make_async_remote_copy(
        src_ref=src, dst_ref=dst, send_sem=send_sems.at[sem], recv_sem=recv_sems.at[sem],
        device_id=dev, device_id_type=MESH_ID)


def _exchange_start(arrays, scatter, *, name, chips_only=False):
    n = len(arrays)
    n_slots = N_CHIPS if chips_only else N_DEV

    def body(*refs):
        srcs, lands = refs[:n], refs[n:2 * n]
        send_sems, recv_sems = refs[2 * n], refs[2 * n + 1]
        token = refs[-1]
        me, peers = _peers(chips_only)
        for k in range(len(peers)):
            for a in range(n):
                _split_copy(srcs, lands, send_sems, recv_sems, scatter, a, k, me, peers).start()
        token[...] = jnp.zeros_like(token)

    land_shapes = [((n_slots,) + a.shape[-2:], a.dtype) for a in arrays]
    sems = pltpu.SemaphoreType.DMA((n * (n_slots - 1),))
    outs = pl.pallas_call(
        body, name=name,
        out_shape=(sems, sems, *[pltpu.HBM(a.shape, a.dtype) for a in arrays],
                   *[pltpu.HBM(s, d) for s, d in land_shapes], _sds((SUBLANE, LANE), F32)),
        in_specs=[HBM] * (2 * n),
        out_specs=(SEM, SEM, *[HBM] * (2 * n), pl.BlockSpec(memory_space=pltpu.VMEM)),
        input_output_aliases={i: 2 + i for i in range(2 * n)},
        compiler_params=pltpu.CompilerParams(has_side_effects=DATAFLOW),
    )(*[pltpu.with_memory_space_constraint(a, pltpu.HBM) for a in arrays],
      *[pltpu.with_memory_space_constraint(lax.empty(s, d), pltpu.HBM) for s, d in land_shapes])
    return (outs[0], outs[1], outs[2:2 + n], outs[2 + n:2 + 2 * n], scatter, chips_only), outs[-1]


def _exchange_wait(handles, after, *, name):
    send_sems, recv_sems, srcs, lands, scatter, chips_only = handles
    n = len(srcs)

    def body(*refs):
        src_refs, land_refs = refs[:n], refs[n:2 * n]
        send_ref, recv_ref = refs[2 * n], refs[2 * n + 1]
        me, peers = _peers(chips_only)
        for k in range(len(peers)):
            for a in range(n):
                _split_copy(src_refs, land_refs, send_ref, recv_ref, scatter, a, k, me, peers).wait_send()
                _split_copy(src_refs, land_refs, send_ref, recv_ref, scatter, a, k, me, peers, True).wait_recv()

    outs = pl.pallas_call(
        body, name=name,
        out_shape=tuple(pltpu.HBM(t.shape, t.dtype) for t in (*srcs, *lands)),
        in_specs=[HBM] * (2 * n) + [SEM, SEM, pl.BlockSpec(memory_space=pl.ANY)],
        out_specs=tuple([HBM] * (2 * n)),
        input_output_aliases={i: i for i in range(2 * n)},
        compiler_params=pltpu.CompilerParams(has_side_effects=DATAFLOW),
    )(*srcs, *lands, send_sems, recv_sems, after)
    return _with_own_slot(outs[n:], outs[:n], scatter, chips_only)


def _with_own_slot(landed, own, scatter, chips_only):
    me = 2 * lax.axis_index("x") + lax.axis_index("y")
    if not chips_only:
        me = 2 * me + lax.axis_index("c")
    out = []
    for buf, src in zip(landed, own):
        mine = lax.dynamic_index_in_dim(src, me, 0, keepdims=False) if scatter else src
        out.append(lax.dynamic_update_index_in_dim(buf, mine, me, 0))
    return out


def _adamw(parts, w, m, v, *, name, tm):
    r, c = w.shape
    assert r % tm == 0

    def body(p_ref, w_ref, m_ref, v_ref, g_ref, d_ref, nm_ref, nv_ref):
        _adamw_update(p_ref, w_ref, m_ref, v_ref, g_ref, d_ref, nm_ref, nv_ref)

    blk = pl.BlockSpec((tm, c), lambda i: (i, 0))
    return pl.pallas_call(
        body, name=name, grid=(r // tm,),
        in_specs=[pl.BlockSpec((parts.shape[0], tm, c), lambda i: (0, i, 0)), blk, blk, blk],
        out_specs=[blk] * 4, out_shape=[_sds((r, c), F32)] * 4,
        compiler_params=_params("parallel"),
    )(parts, w, m, v)


def _adamw_update(p_ref, w_ref, m_ref, v_ref, g_ref, d_ref, nm_ref, nv_ref):
    g = p_ref[0].astype(F32)
    for s in range(1, p_ref.shape[0]):
        g = g + p_ref[s].astype(F32)
    g_ref[...] = g
    m_new = ADAM_B1 * m_ref[...] + (1.0 - ADAM_B1) * g
    v_new = ADAM_B2 * v_ref[...] + (1.0 - ADAM_B2) * (g * g)
    nm_ref[...] = m_new
    nv_ref[...] = v_new
    m_hat = m_new / (1.0 - ADAM_B1 ** ADAM_STEP)
    v_hat = v_new / (1.0 - ADAM_B2 ** ADAM_STEP)
    d_ref[...] = -ADAM_LR * (m_hat / (jnp.sqrt(v_hat) + ADAM_EPS) + ADAM_WD * w_ref[...])


SMALL = ("g_pre_mix", "b_forget", "g_post_mix", "g_pre_ffn", "conv_b", "g_post_ffn")


def _adamw_small(parts, ws, ms, vs, sq_err_parts):
    n = len(ws)

    def body(*refs):
        ins, sq_ref, outs, loss_ref = refs[:4 * n], refs[4 * n], refs[4 * n + 1:-1], refs[-1]
        for i in range(n):
            _adamw_update(ins[i], ins[n + i], ins[2 * n + i], ins[3 * n + i], *outs[4 * i:4 * i + 4])
        total = sq_ref[0]
        for s in range(1, N_DEV):
            total = total + sq_ref[s]
        loss_ref[...] = total * (0.5 / D_MODEL)

    res = pl.pallas_call(
        body, name="adamw_small",
        out_shape=[_sds(w.shape, F32) for w in ws for _ in range(4)] + [_sds((1, LANE), F32)],
        compiler_params=pltpu.CompilerParams(vmem_limit_bytes=VMEM_LIMIT),
    )(*parts, *ws, *ms, *vs, sq_err_parts)
    return [res[4 * i:4 * i + 4] for i in range(n)], res[-1][0, 0]


def kernel(x, g_pre_mix, w_in, b_forget, w_o_fox, w_o_dil, w_out, g_post_mix, g_pre_ffn, w_up, conv_w, conv_b, w_down, g_post_ffn, loss_target, m_g_pre_mix, m_w_in, m_b_forget, m_w_o_fox, m_w_o_dil, m_w_out, m_g_post_mix, m_g_pre_ffn, m_w_up, m_conv_w, m_conv_b, m_w_down, m_g_post_ffn, v_g_pre_mix, v_w_in, v_b_forget, v_w_o_fox, v_w_o_dil, v_w_out, v_g_post_mix, v_g_pre_ffn, v_w_up, v_conv_w, v_conv_b, v_w_down, v_g_post_ffn):
    names = ("g_pre_mix", "w_in", "b_forget", "w_o_fox", "w_o_dil", "w_out", "g_post_mix", "g_pre_ffn",
             "w_up", "conv_w", "conv_b", "w_down", "g_post_ffn")
    w = dict(g_pre_mix=g_pre_mix, w_in=w_in, b_forget=b_forget, w_o_fox=w_o_fox, w_o_dil=w_o_dil, w_out=w_out,
             g_post_mix=g_post_mix, g_pre_ffn=g_pre_ffn, w_up=w_up, conv_w=conv_w, conv_b=conv_b, w_down=w_down,
             g_post_ffn=g_post_ffn)
    m = dict(g_pre_mix=m_g_pre_mix, w_in=m_w_in, b_forget=m_b_forget, w_o_fox=m_w_o_fox, w_o_dil=m_w_o_dil,
             w_out=m_w_out, g_post_mix=m_g_post_mix, g_pre_ffn=m_g_pre_ffn, w_up=m_w_up, conv_w=m_conv_w,
             conv_b=m_conv_b, w_down=m_w_down, g_post_ffn=m_g_post_ffn)
    v = dict(g_pre_mix=v_g_pre_mix, w_in=v_w_in, b_forget=v_b_forget, w_o_fox=v_w_o_fox, w_o_dil=v_w_o_dil,
             w_out=v_w_out, g_post_mix=v_g_post_mix, g_pre_ffn=v_g_pre_ffn, w_up=v_w_up, conv_w=v_conv_w,
             conv_b=v_conv_b, w_down=v_w_down, g_post_ffn=v_g_post_ffn)
    sharded = ("w_in", "w_o_fox", "w_o_dil", "w_out", "w_up", "w_down", "conv_w")
    wire = lambda n: F32 if n == "conv_w" else BF16

    by_cols = lambda t: jnp.transpose(t, (1, 0, 2)).reshape(t.shape[1], N_DEV * t.shape[2])
    by_rows = lambda t: t.reshape(N_DEV * t.shape[1], t.shape[2])
    col_slots = lambda t: jnp.transpose(t.reshape(t.shape[0], N_DEV, t.shape[1] // N_DEV), (1, 0, 2))
    row_slots = lambda t: t.reshape(N_DEV, t.shape[0] // N_DEV, t.shape[1])
    to_slots = lambda n, t: (row_slots if n in ("w_out", "w_down") else col_slots)(t).astype(wire(n))
    shard = lambda n: w[n][0].astype(wire(n))

    w_main, w_f = _w_in_from_shards(_gather_two_level(shard("w_in"), name="gather_w_in"))
    late = ("w_o_fox", "w_o_dil", "w_out", "w_up", "conv_w", "w_down")
    order = jnp.minimum(jnp.abs(w_f[0, 0].astype(F32)), 0.0)
    late_handles, late_tok = _exchange_start(
        [shard(n) + order.astype(wire(n)) if n == "conv_w" else shard(n) for n in late], False,
        name="gather_late_start")

    def late_weights(after):
        got = dict(zip(late, _exchange_wait(late_handles, after, name="gather_late_wait")))
        return (by_cols(got["w_o_fox"]), by_cols(got["w_o_dil"]), by_rows(got["w_out"]),
                _w_up_from_shards(got["w_up"]),
                _ffn_interleave(by_cols(got["conv_w"])),
                by_rows(got["w_down"]))

    pending = {}

    def ffn_grads_ready(g):
        slots = [to_slots("w_down", g["w_down"]), _w_up_to_shards(g["w_up_blocks"]), to_slots("conv_w", g["conv_w"])]
        pending["ffn"] = _exchange_start(slots, True, name="scatter_ffn_start")
        return pending["ffn"][1][0, 0]

    def proj_grads_ready(g):
        pending["proj"] = _exchange_start([to_slots(n, g[n]) for n in ("w_o_fox", "w_o_dil", "w_out")], True,
                                          name="scatter_proj_start")
        return pending["proj"][1][0, 0]

    def mixer_grads_ready(g):
        slots = _w_in_to_shards(g["w_main"], g["w_f"])
        theirs = _sibling_swap([slots], name="scatter_w_in_swap")[0]
        chip_sums = _pair_sum(slots, theirs, name="scatter_w_in_pair_sum", tn=W_IN_SHARD)
        pending["w_in"] = _exchange_start([chip_sums], True, name="scatter_w_in_start", chips_only=True)
        return pending["w_in"][1][0, 0]

    sq_err, grad_x, g = _local_step(
        x[0], loss_target[0], w_main, w_f, b_forget, conv_b, g_pre_mix + late_tok[0, 0], g_post_mix, g_pre_ffn,
        g_post_ffn, late_weights, ffn_grads_ready, proj_grads_ready, mixer_grads_ready)

    tiles = dict(w_in=256, w_o_fox=512, w_o_dil=512, w_out=128, w_up=256, w_down=176, conv_w=3)
    adam = lambda n, p: _adamw(p, w[n][0], m[n][0], v[n][0], name=f"adamw_{n}", tm=tiles[n])
    res = {}
    for key, group in (("ffn", ("w_down", "w_up", "conv_w")), ("proj", ("w_o_fox", "w_o_dil", "w_out"))):
        landed = _exchange_wait(pending[key][0], grad_x, name=f"scatter_{key}_wait")
        res.update({n: adam(n, p) for n, p in zip(group, landed)})
    done = res["w_up"][3]
    res["w_in"] = adam("w_in", _exchange_wait(pending["w_in"][0], done, name="scatter_w_in_wait")[0])
    small_parts = _exchange([g[n] for n in SMALL] + [sq_err], False, name="gather_small_grads")
    small, loss = _adamw_small(small_parts[:-1], *[[t[n] for n in SMALL] for t in (w, m, v)], small_parts[-1])
    small = dict(zip(SMALL, small))
    out = [[(res[n][k][None] if n in sharded else small[n][k]) for n in names] for k in range(4)]
    return (loss, grad_x[None], *out[0], *out[1], *out[2], *out[3])
```

```python
import functools
import math

import jax
import jax.numpy as jnp
import numpy as np
from jax import lax
from jax.experimental import pallas as pl
from jax.experimental.pallas import tpu as pltpu

F32 = jnp.float32
BF16 = jnp.bfloat16

SEQ = 4096
D_MODEL = 1024
N_HEADS = 8
HEAD_DIM = 64
ATT_W = N_HEADS * HEAD_DIM
D_FF = 2816
Z_MAIN = 5120
F_PAD = 128
ROPE_DIM = 16
ROPE_THETA = 500000.0
RMS_EPS = 1e-6
NEG_INF = -1e30
SCALE = 1.0 / math.sqrt(HEAD_DIM)
DIL_PATTERNS = ((128, 1), (512, 4), (2048, 16))
DIL_BLK = 128
DIL_STEP_BLOCKS = 2
N_DEV = 8

ADAM_LR = 0.001
ADAM_B1 = 0.9
ADAM_B2 = 0.999
ADAM_EPS = 1e-08
ADAM_WD = 0.01
ADAM_STEP = 10

LANE = 128
SUBLANE = 8
VMEM_LIMIT = 56 * 1024 * 1024
MESH_ID = pl.DeviceIdType.MESH
ANY = pl.BlockSpec(memory_space=pl.ANY)


def _params(*sem):
    return pltpu.CompilerParams(dimension_semantics=sem, vmem_limit_bytes=VMEM_LIMIT)


def _sds(shape, dtype):
    return jax.ShapeDtypeStruct(shape, dtype)


def _matmul(a, b, *, ta=False, tb=False, out_dtype, tm, tn, tk, name, b_k_off=0):
    if ta:
        kk, m = a.shape
    else:
        m, kk = a.shape
    n = b.shape[0] if tb else b.shape[1]
    tm, tn, tk = min(tm, m), min(tn, n), min(tk, kk)
    assert (b.shape[1] if tb else b.shape[0]) >= b_k_off * tk + kk
    assert m % tm == 0 and n % tn == 0 and kk % tk == 0, (name, m, n, kk, tm, tn, tk)
    nk = kk // tk
    dims = (((0 if ta else 1,), (1 if tb else 0,)), ((), ()))

    def body(a_ref, b_ref, o_ref, *scratch):
        p = lax.dot_general(a_ref[...].astype(BF16), b_ref[...].astype(BF16), dims,
                            preferred_element_type=F32)
        if nk == 1:
            o_ref[...] = p.astype(o_ref.dtype)
        else:
            acc = scratch[0]
            k = pl.program_id(2)

            @pl.when(k == 0)
            def _():
                acc[...] = p

            @pl.when(k > 0)
            def _():
                acc[...] += p

            @pl.when(k == nk - 1)
            def _():
                o_ref[...] = acc[...].astype(o_ref.dtype)

    a_spec = (pl.BlockSpec((tk, tm), lambda i, j, k: (k, i)) if ta
              else pl.BlockSpec((tm, tk), lambda i, j, k: (i, k)))
    b_spec = (pl.BlockSpec((tn, tk), lambda i, j, k: (j, k + b_k_off)) if tb
              else pl.BlockSpec((tk, tn), lambda i, j, k: (k + b_k_off, j)))
    return pl.pallas_call(
        body, name=name, grid=(m // tm, n // tn, nk),
        in_specs=[a_spec, b_spec],
        out_specs=pl.BlockSpec((tm, tn), lambda i, j, k: (i, j)),
        out_shape=_sds((m, n), out_dtype),
        scratch_shapes=[pltpu.VMEM((tm, tn), F32)] if nk > 1 else [],
        compiler_params=_params("parallel", "parallel", "arbitrary"),
    )(a, b)


def _rms_fwd(x, g, *, name, tm=512):
    def body(x_ref, g_ref, h_ref):
        xv = x_ref[...]
        r = lax.rsqrt(jnp.mean(xv * xv, axis=-1, keepdims=True) + RMS_EPS)
        h_ref[...] = (xv * r * g_ref[...]).astype(h_ref.dtype)

    return pl.pallas_call(
        body, name=name, grid=(SEQ // tm,),
        in_specs=[pl.BlockSpec((tm, D_MODEL), lambda i: (i, 0)), pl.BlockSpec((1, D_MODEL), lambda i: (0, 0))],
        out_specs=pl.BlockSpec((tm, D_MODEL), lambda i: (i, 0)),
        out_shape=_sds((SEQ, D_MODEL), BF16),
        compiler_params=_params("parallel"),
    )(x, g)


def _rms_bwd(dh_parts, xin, g, dres, *, out_dtype, name, tm=512):
    n_parts = len(dh_parts)
    has_res = dres is not None

    def body(*refs):
        parts = refs[:n_parts]
        x_ref, g_ref = refs[n_parts], refs[n_parts + 1]
        res_ref = refs[n_parts + 2] if has_res else None
        o_ref, gg_ref = refs[-2], refs[-1]
        dh = parts[0][...].astype(F32)
        for p in parts[1:]:
            dh = dh + p[...].astype(F32)
        xv = x_ref[...]
        r = lax.rsqrt(jnp.mean(xv * xv, axis=-1, keepdims=True) + RMS_EPS)
        xn = xv * r

        @pl.when(pl.program_id(0) == 0)
        def _():
            gg_ref[...] = jnp.zeros_like(gg_ref)

        gg_ref[...] += jnp.sum(dh * xn, axis=0, keepdims=True)
        dxn = dh * g_ref[...]
        dx = r * (dxn - xn * jnp.mean(dxn * xn, axis=-1, keepdims=True))
        if has_res:
            dx = dx + res_ref[...]
        o_ref[...] = dx.astype(o_ref.dtype)

    row = pl.BlockSpec((tm, D_MODEL), lambda i: (i, 0))
    vec = pl.BlockSpec((1, D_MODEL), lambda i: (0, 0))
    args = list(dh_parts) + [xin, g] + ([dres] if has_res else [])
    return pl.pallas_call(
        body, name=name, grid=(SEQ // tm,),
        in_specs=[row] * n_parts + [row, vec] + ([row] if has_res else []),
        out_specs=[row, vec],
        out_shape=[_sds((SEQ, D_MODEL), out_dtype), _sds((1, D_MODEL), F32)],
        compiler_params=_params("arbitrary"),
    )(*args)


def _rms_pair_bwd(dh_parts, x2, g_pre, dres, y1, g_post, *, tm=512):
    n_parts = len(dh_parts)

    def norm_bwd(dh, xin, g_ref, gg_ref):
        r = lax.rsqrt(jnp.mean(xin * xin, axis=-1, keepdims=True) + RMS_EPS)
        xn = xin * r
        gg_ref[...] += jnp.sum(dh * xn, axis=0, keepdims=True)
        dxn = dh * g_ref[...]
        return r * (dxn - xn * jnp.mean(dxn * xn, axis=-1, keepdims=True))

    def body(*refs):
        parts = refs[:n_parts]
        x2_ref, gpre_ref, res_ref, y1_ref, gpost_ref, dx2_ref, dy1_ref, ggpre_ref, ggpost_ref = refs[n_parts:]

        @pl.when(pl.program_id(0) == 0)
        def _():
            ggpre_ref[...] = jnp.zeros_like(ggpre_ref)
            ggpost_ref[...] = jnp.zeros_like(ggpost_ref)

        dh = parts[0][...].astype(F32)
        for p in parts[1:]:
            dh = dh + p[...].astype(F32)
        dx2 = res_ref[...] + norm_bwd(dh, x2_ref[...], gpre_ref, ggpre_ref)
        dx2_ref[...] = dx2
        dy1_ref[...] = norm_bwd(dx2, y1_ref[...], gpost_ref, ggpost_ref).astype(dy1_ref.dtype)

    row = pl.BlockSpec((tm, D_MODEL), lambda i: (i, 0))
    vec = pl.BlockSpec((1, D_MODEL), lambda i: (0, 0))
    return pl.pallas_call(
        body, name="rms_pair_bwd", grid=(SEQ // tm,),
        in_specs=[row] * n_parts + [row, vec, row, row, vec],
        out_specs=[row, row, vec, vec],
        out_shape=[_sds((SEQ, D_MODEL), F32), _sds((SEQ, D_MODEL), BF16), _sds((1, D_MODEL), F32),
                   _sds((1, D_MODEL), F32)],
        compiler_params=_params("arbitrary"),
    )(*dh_parts, x2, g_pre, dres, y1, g_post)


SCAN_BLK = 512


def _split_dot(v, tri):
    hi = v.astype(BF16)
    r1 = v - hi.astype(F32)
    mid = r1.astype(BF16)
    lo = (r1 - mid.astype(F32)).astype(BF16)
    dot = functools.partial(jnp.dot, preferred_element_type=F32)
    return dot(hi, tri) + dot(mid, tri) + dot(lo, tri)


def _fox_prep(fa_t, b_col):
    nblk = SEQ // SCAN_BLK

    def body(fa_ref, b_ref, f_ref, sg_ref):
        row = lax.broadcasted_iota(jnp.int32, (SCAN_BLK, SCAN_BLK), 0)
        col = lax.broadcasted_iota(jnp.int32, (SCAN_BLK, SCAN_BLK), 1)
        upper = (row <= col).astype(BF16)
        carry = jnp.zeros((N_HEADS, 1), F32)
        for blk in range(nblk):
            sl = pl.ds(blk * SCAN_BLK, SCAN_BLK)
            xx = fa_ref[:, sl] + b_ref[...]
            e = jnp.exp(-jnp.abs(xx))
            logf = jnp.minimum(xx, 0.0) - jnp.log(1.0 + e)
            sg_ref[:, sl] = jnp.where(xx >= 0.0, e, 1.0) / (1.0 + e)
            c = _split_dot(logf, upper) + carry
            f_ref[:, sl] = c
            carry = c[:, SCAN_BLK - 1:SCAN_BLK]

    return pl.pallas_call(
        body, name="fox_prep",
        out_shape=[_sds((N_HEADS, SEQ), F32), _sds((N_HEADS, SEQ), F32)],
        compiler_params=pltpu.CompilerParams(vmem_limit_bytes=VMEM_LIMIT),
    )(fa_t, b_col)


def _fox_post_bwd(df_t, sg_t):
    nblk = SEQ // SCAN_BLK

    def body(df_ref, sg_ref, dfa_ref, gb_ref):
        row = lax.broadcasted_iota(jnp.int32, (SCAN_BLK, SCAN_BLK), 0)
        col = lax.broadcasted_iota(jnp.int32, (SCAN_BLK, SCAN_BLK), 1)
        lower = (row >= col).astype(BF16)
        carry = jnp.zeros((N_HEADS, 1), F32)
        gb = jnp.zeros((N_HEADS, 1), F32)
        for blk in reversed(range(nblk)):
            sl = pl.ds(blk * SCAN_BLK, SCAN_BLK)
            c = _split_dot(df_ref[:, sl], lower) + carry
            carry = c[:, 0:1]
            dfa = c * sg_ref[:, sl]
            dfa_ref[:, sl] = dfa
            gb = gb + jnp.sum(dfa, axis=1, keepdims=True)
        gb_ref[...] = gb

    return pl.pallas_call(
        body, name="fox_post_bwd",
        out_shape=[_sds((N_HEADS, SEQ), F32), _sds((N_HEADS, 1), F32)],
        compiler_params=pltpu.CompilerParams(vmem_limit_bytes=VMEM_LIMIT),
    )(df_t, sg_t)


FOX_T = 512
NT_DIMS = (((1,), (1,)), ((), ()))
TN_DIMS = (((0,), (0,)), ((), ()))


def _head(ref_or_val, h):
    return ref_or_val[:, h * HEAD_DIM:(h + 1) * HEAD_DIM]


def _split3(v):
    hi = v.astype(BF16).astype(F32)
    r1 = v - hi
    mid = r1.astype(BF16).astype(F32)
    return hi, mid, (r1 - mid).astype(BF16).astype(F32)


ONE_LANE = 3 * N_HEADS


def _pack_terms(v, with_one):
    hi, mid, lo = _split3(v)
    t = hi + pltpu.roll(mid, N_HEADS, 1) + pltpu.roll(lo, 2 * N_HEADS, 1)
    if with_one:
        t = t + (lax.broadcasted_iota(jnp.int32, v.shape, 1) == ONE_LANE).astype(F32)
    return t.astype(BF16)


def _aux_matrices():
    to_q = np.zeros((LANE, N_HEADS * 2 * HEAD_DIM), np.float32)
    to_k = np.zeros_like(to_q)
    for h in range(N_HEADS):
        base = h * 2 * HEAD_DIM + HEAD_DIM
        for s in range(3):
            to_q[s * N_HEADS + h, base + s] = 1.0
            to_q[ONE_LANE, base + 3 + s] = 1.0
            to_k[ONE_LANE, base + s] = 1.0
            to_k[s * N_HEADS + h, base + 3 + s] = -1.0
    return jnp.asarray(to_q, BF16), jnp.asarray(to_k, BF16)


def _head_sums():
    total = np.zeros((N_HEADS * HEAD_DIM, LANE), np.float32)
    first = np.zeros_like(total)
    for h in range(N_HEADS):
        total[h * HEAD_DIM:(h + 1) * HEAD_DIM, h] = 1.0
        first[h * HEAD_DIM, h] = 1.0
    return jnp.asarray(total, BF16), jnp.asarray(first, BF16)


SLOT = 2 * HEAD_DIM
N_SPLIT = 3
FOX_FWD_HEADS = 8
FOX_BWD_HEADS = 4


def _slot(ref, h):
    return ref[:, h * SLOT:(h + 1) * SLOT]


def _fox_pack_fwd(zm, f_cols, *, tm=512):
    def body(q_ref, k_ref, v_ref, f_ref, tq_ref, tk_ref, qs_ref, ks_ref, vs_ref):
        ones = jnp.ones((tm, HEAD_DIM), BF16)
        terms = _pack_terms(f_ref[...], True)
        q_aux = jnp.dot(terms, tq_ref[...], preferred_element_type=F32).astype(BF16)
        k_aux = jnp.dot(terms, tk_ref[...], preferred_element_type=F32).astype(BF16)
        for h in range(N_HEADS):
            aux = slice(h * SLOT + HEAD_DIM, (h + 1) * SLOT)
            qs_ref[:, h * SLOT:(h + 1) * SLOT] = jnp.concatenate(
                [(_head(q_ref, h).astype(F32) * SCALE).astype(BF16), q_aux[:, aux]], axis=1)
            ks_ref[:, h * SLOT:(h + 1) * SLOT] = jnp.concatenate([_head(k_ref, h), k_aux[:, aux]], axis=1)
            vs_ref[:, h * SLOT:(h + 1) * SLOT] = jnp.concatenate([_head(v_ref, h), ones], axis=1)

    col = lambda b: pl.BlockSpec((tm, ATT_W), lambda i: (i, b))
    wide = pl.BlockSpec((tm, N_HEADS * SLOT), lambda i: (i, 0))
    const = pl.BlockSpec((LANE, N_HEADS * SLOT), lambda i: (0, 0))
    return pl.pallas_call(
        body, name="fox_pack_fwd", grid=(SEQ // tm,),
        in_specs=[col(0), col(1), col(2), pl.BlockSpec((tm, LANE), lambda i: (i, 0)), const, const],
        out_specs=[wide] * 3, out_shape=[_sds((SEQ, N_HEADS * SLOT), BF16)] * 3,
        compiler_params=_params("parallel"),
    )(zm, zm, zm, f_cols, *_aux_matrices())


def _fox_pack_bwd(zm, f_cols, lse, o, do, *, tm=512):
    def body(q_ref, f_ref, lse_ref, o_ref, do_ref, tq_ref, total_ref, first_ref, qs_ref, ds_ref):
        delta = _split_dot(o_ref[...].astype(F32) * do_ref[...].astype(F32), total_ref[...])
        lse_h = _split_dot(lse_ref[...], first_ref[...])
        q_aux = jnp.dot(_pack_terms(f_ref[...] - lse_h, True), tq_ref[...], preferred_element_type=F32).astype(BF16)
        d_aux = jnp.dot(_pack_terms(-delta, False), tq_ref[...], preferred_element_type=F32).astype(BF16)
        for h in range(N_HEADS):
            aux = slice(h * SLOT + HEAD_DIM, (h + 1) * SLOT)
            qs_ref[:, h * SLOT:(h + 1) * SLOT] = jnp.concatenate(
                [(_head(q_ref, h).astype(F32) * SCALE).astype(BF16), q_aux[:, aux]], axis=1)
            ds_ref[:, h * SLOT:(h + 1) * SLOT] = jnp.concatenate([_head(do_ref, h), d_aux[:, aux]], axis=1)

    row = pl.BlockSpec((tm, ATT_W), lambda i: (i, 0))
    wide = pl.BlockSpec((tm, N_HEADS * SLOT), lambda i: (i, 0))
    const = lambda r, c: pl.BlockSpec((r, c), lambda i: (0, 0))
    return pl.pallas_call(
        body, name="fox_pack_bwd", grid=(SEQ // tm,),
        in_specs=[row, pl.BlockSpec((tm, LANE), lambda i: (i, 0)), row, row, row,
                  const(LANE, N_HEADS * SLOT), const(ATT_W, LANE), const(ATT_W, LANE)],
        out_specs=[wide] * 2, out_shape=[_sds((SEQ, N_HEADS * SLOT), BF16)] * 2,
        compiler_params=_params("parallel"),
    )(zm, f_cols, lse, o, do, _aux_matrices()[0], *_head_sums())


def _causal_pairs(key_major):
    nb = SEQ // FOX_T
    if key_major:
        pairs = [(i, j) for j in range(nb) for i in range(j, nb)]
    else:
        pairs = [(i, j) for i in range(nb) for j in range(i + 1)]
    return (jnp.array([p[0] for p in pairs], jnp.int32), jnp.array([p[1] for p in pairs], jnp.int32), len(pairs))


FOX_HALF = FOX_T // 2
FOX_FULL = ((slice(0, FOX_T), slice(0, FOX_T), None),)
FOX_DIAG = ((slice(0, FOX_HALF), slice(0, FOX_HALF), 0), (slice(FOX_HALF, FOX_T), slice(0, FOX_T), FOX_HALF))


def _causal_piece_mask(q_rows, k_rows, offset):
    shape = (q_rows.stop - q_rows.start, k_rows.stop - k_rows.start)
    row = lax.broadcasted_iota(jnp.int32, shape, 0)
    col = lax.broadcasted_iota(jnp.int32, shape, 1)
    return col <= row + offset


def _fox_fwd(q_slots, k_slots, v_slots):
    i_tab, j_tab, n_pairs = _causal_pairs(False)

    def body(i_tab, j_tab, q_ref, k_ref, v_ref, o_ref, lse_ref, m_s, acc_s):
        t = pl.program_id(1)
        i, j = i_tab[t], j_tab[t]

        @pl.when(j == 0)
        def _():
            m_s[...] = jnp.full_like(m_s, NEG_INF)
            acc_s[...] = jnp.zeros_like(acc_s)

        def step(pieces):
            jobs = [(h, piece) for h in range(FOX_FWD_HEADS) for piece in pieces]
            lanes = lambda h: slice(h * SLOT, (h + 1) * SLOT)
            scores = [lax.dot_general(q_ref[qr, lanes(h)], k_ref[kr, lanes(h)], NT_DIMS, preferred_element_type=F32)
                      for h, (qr, kr, _) in jobs]
            probs, alphas = [], []
            for idx, (h, (qr, kr, offset)) in enumerate(jobs):
                s = scores[idx]
                if offset is not None:
                    s = jnp.where(_causal_piece_mask(qr, kr, offset), s, NEG_INF)
                m_prev = m_s[h, qr, :]
                m_new = jnp.maximum(m_prev, jnp.max(s, axis=-1, keepdims=True))
                probs.append(jnp.exp(s - jnp.tile(m_new, (1, s.shape[1] // LANE))).astype(BF16))
                alphas.append(jnp.exp(m_prev - m_new))
                m_s[h, qr, :] = m_new
            for idx, (h, (qr, kr, _)) in enumerate(jobs):
                acc_s[h, qr, :] = alphas[idx] * acc_s[h, qr, :] + jnp.dot(
                    probs[idx], v_ref[kr, lanes(h)], preferred_element_type=F32)

        @pl.when(j < i)
        def _():
            step(FOX_FULL)

        @pl.when(j == i)
        def _():
            step(FOX_DIAG)
            outs, lses = [], []
            for h in range(FOX_FWD_HEADS):
                acc = acc_s[h]
                l = acc[:, HEAD_DIM:]
                outs.append(acc[:, :HEAD_DIM] / l)
                lses.append(m_s[h][:, :HEAD_DIM] + jnp.log(l))
            o_ref[...] = jnp.concatenate(outs, axis=1).astype(o_ref.dtype)
            lse_ref[...] = jnp.concatenate(lses, axis=1)

    qspec = pl.BlockSpec((FOX_T, FOX_FWD_HEADS * SLOT), lambda p, t, it, jt: (it[t], p))
    kspec = pl.BlockSpec((FOX_T, FOX_FWD_HEADS * SLOT), lambda p, t, it, jt: (jt[t], p))
    ospec = pl.BlockSpec((FOX_T, FOX_FWD_HEADS * HEAD_DIM), lambda p, t, it, jt: (it[t], p))
    return pl.pallas_call(
        body, name="fox_fwd",
        grid_spec=pltpu.PrefetchScalarGridSpec(
            num_scalar_prefetch=2, grid=(N_HEADS // FOX_FWD_HEADS, n_pairs),
            in_specs=[qspec, kspec, kspec], out_specs=[ospec, ospec],
            scratch_shapes=[pltpu.VMEM((FOX_FWD_HEADS, FOX_T, LANE), F32),
                            pltpu.VMEM((FOX_FWD_HEADS, FOX_T, SLOT), F32)]),
        out_shape=[_sds((SEQ, ATT_W), BF16), _sds((SEQ, ATT_W), F32)],
        compiler_params=_params("parallel", "arbitrary"),
    )(i_tab, j_tab, q_slots, k_slots, v_slots)


def _fox_bwd(q_slots, k_slots, v_slots, do_slots):
    i_tab, j_tab, n_pairs = _causal_pairs(True)

    def body(i_tab, j_tab, q_ref, k_ref, v_ref, do_ref, dq_ref, dk_ref, dv_ref):
        t = pl.program_id(1)
        i, j = i_tab[t], j_tab[t]

        @pl.when(t == 0)
        def _():
            dq_ref[...] = jnp.zeros_like(dq_ref)

        @pl.when(i == j)
        def _():
            dk_ref[...] = jnp.zeros_like(dk_ref)
            dv_ref[...] = jnp.zeros_like(dv_ref)

        def step(pieces):
            jobs = [(h, piece) for h in range(FOX_BWD_HEADS) for piece in pieces]
            lanes = lambda h: slice(h * SLOT, (h + 1) * SLOT)
            scores = [lax.dot_general(q_ref[qr, lanes(h)], k_ref[kr, lanes(h)], NT_DIMS, preferred_element_type=F32)
                      for h, (qr, kr, _) in jobs]
            dps = [lax.dot_general(do_ref[qr, lanes(h)], v_ref[kr, lanes(h)], NT_DIMS, preferred_element_type=F32)
                   for h, (qr, kr, _) in jobs]
            ps, dss = [], []
            for idx, (h, (qr, kr, offset)) in enumerate(jobs):
                p = jnp.exp(scores[idx])
                if offset is not None:
                    p = jnp.where(_causal_piece_mask(qr, kr, offset), p, 0.0)
                ps.append(p.astype(BF16))
                dss.append((p * dps[idx]).astype(BF16))
            for idx, (h, (qr, kr, _)) in enumerate(jobs):
                rows = pl.ds(pl.multiple_of(i * FOX_T + qr.start, FOX_HALF), qr.stop - qr.start)
                dv_ref[kr, lanes(h)] += lax.dot_general(ps[idx], do_ref[qr, lanes(h)], TN_DIMS,
                                                        preferred_element_type=F32)
                dk_ref[kr, lanes(h)] += lax.dot_general(dss[idx], q_ref[qr, lanes(h)], TN_DIMS,
                                                        preferred_element_type=F32)
                dq_ref[rows, lanes(h)] += jnp.dot(dss[idx], k_ref[kr, lanes(h)], preferred_element_type=F32)

        @pl.when(i > j)
        def _():
            step(FOX_FULL)

        @pl.when(i == j)
        def _():
            step(FOX_DIAG)

    qspec = pl.BlockSpec((FOX_T, FOX_BWD_HEADS * SLOT), lambda p, t, it, jt: (it[t], p))
    kspec = pl.BlockSpec((FOX_T, FOX_BWD_HEADS * SLOT), lambda p, t, it, jt: (jt[t], p))
    return pl.pallas_call(
        body, name="fox_bwd",
        grid_spec=pltpu.PrefetchScalarGridSpec(
            num_scalar_prefetch=2, grid=(N_HEADS // FOX_BWD_HEADS, n_pairs),
            in_specs=[qspec, kspec, kspec, qspec],
            out_specs=[pl.BlockSpec((SEQ, FOX_BWD_HEADS * SLOT), lambda p, t, it, jt: (0, p)), kspec, kspec]),
        out_shape=[_sds((SEQ, N_HEADS * SLOT), F32)] * 3,
        compiler_params=_params("arbitrary", "arbitrary"),
    )(i_tab, j_tab, q_slots, k_slots, v_slots, do_slots)


def _fox_unpack(dq_slots, dk_slots, dv_slots, dz, *, tm=512):
    def body(dq_ref, dk_ref, dv_ref, dz_in, o_ref, df_ref):
        lane = lax.broadcasted_iota(jnp.int32, (tm, LANE), 1)
        df = jnp.zeros((tm, LANE), F32)
        for h in range(N_HEADS):
            lo = h * SLOT
            for part, (ref, mult) in enumerate(((dq_ref, SCALE), (dk_ref, 1.0), (dv_ref, 1.0))):
                o_ref[:, part * ATT_W + h * HEAD_DIM:part * ATT_W + (h + 1) * HEAD_DIM] = (
                    ref[:, lo:lo + HEAD_DIM] * mult).astype(o_ref.dtype)
            rows = dq_ref[:, lo + HEAD_DIM:lo + HEAD_DIM + 1]
            cols = dk_ref[:, lo + HEAD_DIM + N_SPLIT:lo + HEAD_DIM + N_SPLIT + 1]
            df = jnp.where(lane == h, rows - cols, df)
        df_ref[...] = df

    wide = pl.BlockSpec((tm, N_HEADS * SLOT), lambda i: (i, 0))
    return pl.pallas_call(
        body, name="fox_unpack", grid=(SEQ // tm,), in_specs=[wide] * 3 + [ANY],
        out_specs=[pl.BlockSpec((tm, 3 * ATT_W), lambda i: (i, 0)), pl.BlockSpec((tm, LANE), lambda i: (i, 0))],
        out_shape=[_sds((SEQ, Z_MAIN), BF16), _sds((SEQ, LANE), F32)],
        input_output_aliases={3: 0},
        compiler_params=_params("parallel"),
    )(dq_slots, dk_slots, dv_slots, dz)


def _dil_bwd_prep(o, do, lse, *, tm=512):
    dilations = [d for _, d in DIL_PATTERNS]
    o_chunks = ATT_W // LANE

    def body(o_ref, do_ref, lse_ref, *rest):
        outs, (do_scr, lse_scr, dl_scr) = rest[:-3], rest[-3:]
        dov = do_ref[...].astype(F32)
        prod = o_ref[...].astype(F32) * dov
        lane = lax.broadcasted_iota(jnp.int32, (tm, LANE), 1)
        delta = jnp.zeros((tm, LANE), F32)
        for h in range(N_HEADS):
            delta = jnp.where(lane == h, jnp.sum(_head(prod, h), axis=1, keepdims=True), delta)
        for ch in range(o_chunks):
            do_scr[ch] = dov[:, ch * LANE:(ch + 1) * LANE]
        lse_scr[0] = lse_ref[...]
        dl_scr[0] = delta
        for k, d in enumerate(dilations):
            for scr, out in zip((do_scr, lse_scr, dl_scr), outs[3 * k:3 * k + 3]):
                _slabs_from_rows(scr, out, d)

    row = pl.BlockSpec((tm, ATT_W), lambda i: (i, 0))
    view = lambda d, w: pl.BlockSpec((tm // d, d * w), lambda i: (i, 0))
    outs = pl.pallas_call(
        body, name="dil_bwd_prep", grid=(SEQ // tm,),
        in_specs=[row, row, pl.BlockSpec((tm, LANE), lambda i: (i, 0))],
        out_specs=[view(d, w) for d in dilations for w in (ATT_W, LANE, LANE)],
        out_shape=[_sds((SEQ // d, d * w), t) for d in dilations for w, t in ((ATT_W, BF16), (LANE, F32), (LANE, F32))],
        scratch_shapes=[pltpu.VMEM((o_chunks, tm, LANE), F32), pltpu.VMEM((1, tm, LANE), F32),
                        pltpu.VMEM((1, tm, LANE), F32)],
        compiler_params=_params("parallel"),
    )(o, do, lse)
    return [outs[3 * k:3 * k + 3] for k in range(len(dilations))]


def _rope_tables():
    half = ROPE_DIM // 2
    inv_freq = np.float32(ROPE_THETA) ** (-np.arange(half, dtype=np.float32) * np.float32(2.0) / np.float32(ROPE_DIM))
    ang = np.arange(SEQ, dtype=np.float32)[:, None] * inv_freq.astype(np.float32)[None, :]
    cos, sin = jnp.asarray(np.cos(ang).astype(np.float32)), jnp.asarray(np.sin(ang).astype(np.float32))
    ones = jnp.ones((SEQ, HEAD_DIM - ROPE_DIM), F32)
    zeros = jnp.zeros((SEQ, HEAD_DIM - ROPE_DIM), F32)
    zh = jnp.zeros((SEQ, half), F32)
    c_tab = jnp.concatenate([cos, cos, ones], axis=1)
    a_tab = jnp.concatenate([-sin, zh, zeros], axis=1)
    b_tab = jnp.concatenate([zh, sin, zeros], axis=1)
    two = lambda t: jnp.concatenate([t, t], axis=1)
    return two(c_tab), two(a_tab), two(b_tab)


def _rotate(x, c_tab, a_tab, b_tab):
    return x * c_tab + pltpu.roll(x, LANE - ROPE_DIM // 2, 1) * a_tab + pltpu.roll(x, ROPE_DIM // 2, 1) * b_tab


def _rope_fwd(zm, tabs, *, tm=512):
    width = 3 * ATT_W
    dilations = [d for _, d in DIL_PATTERNS]

    def body(q_ref, k_ref, v_ref, c_ref, a_ref, b_ref, *rest):
        outs, scr = rest[:-1], rest[-1]
        per_part = ATT_W // LANE
        for part, (x_ref, mult) in enumerate(((q_ref, SCALE), (k_ref, 1.0))):
            for cc in range(per_part):
                sl = slice(cc * LANE, (cc + 1) * LANE)
                scr[part * per_part + cc] = _rotate(x_ref[:, sl].astype(F32), c_ref[...], a_ref[...], b_ref[...]) * mult
        for cc in range(per_part):
            scr[2 * per_part + cc] = v_ref[:, cc * LANE:(cc + 1) * LANE].astype(F32)
        for o_ref, d in zip(outs, dilations):
            for r in range(d):
                for ch in range(width // LANE):
                    o_ref[:, r * width + ch * LANE:r * width + (ch + 1) * LANE] = (
                        scr.at[ch][pl.ds(r, tm // d, stride=d), :].astype(o_ref.dtype))

    tab = pl.BlockSpec((tm, LANE), lambda i: (i, 0))
    col = lambda b: pl.BlockSpec((tm, ATT_W), lambda i: (i, b))
    return pl.pallas_call(
        body, name="rope_fwd", grid=(SEQ // tm,),
        in_specs=[col(3), col(4), col(5), tab, tab, tab],
        out_specs=[pl.BlockSpec((tm // d, d * width), lambda i: (i, 0)) for d in dilations],
        out_shape=[_sds((SEQ // d, d * width), BF16) for d in dilations],
        scratch_shapes=[pltpu.VMEM((width // LANE, tm, LANE), F32)],
        compiler_params=_params("parallel"),
    )(zm, zm, zm, *tabs)


def _dil_grad_combine(dqs, dks, dvs, tabs, dz, *, tm=256):
    dilations = [d for _, d in DIL_PATTERNS]
    chunks = ATT_W // LANE

    def body(*refs):
        groups = (refs[0:3], refs[3:6], refs[6:9])
        c_ref, a_ref, b_ref, _, o_ref, scr = refs[9:]

        def total(part, cc):
            acc = None
            for g, (ref, d) in enumerate(zip(groups[part], dilations)):
                term = ref[:, cc * LANE:(cc + 1) * LANE].astype(F32) if d == 1 else scr[part, g, cc]
                acc = term if acc is None else acc + term
            return acc

        for part in range(3):
            for g, (ref, d) in enumerate(zip(groups[part], dilations)):
                if d > 1:
                    _rows_from_slabs(ref, scr.at[part, g], d)
        for cc in range(chunks):
            for part in range(2):
                o_ref[:, part * ATT_W + cc * LANE:part * ATT_W + (cc + 1) * LANE] = _rotate(
                    total(part, cc), c_ref[...], -a_ref[...], -b_ref[...]).astype(o_ref.dtype)
            o_ref[:, 2 * ATT_W + cc * LANE:2 * ATT_W + (cc + 1) * LANE] = total(2, cc).astype(o_ref.dtype)

    view = lambda d: pl.BlockSpec((tm // d, d * ATT_W), lambda i: (i, 0))
    tab = pl.BlockSpec((tm, LANE), lambda i: (i, 0))
    return pl.pallas_call(
        body, name="dil_grad_combine", grid=(SEQ // tm,),
        in_specs=[view(d) for d in dilations] * 3 + [tab] * 3 + [ANY],
        out_specs=pl.BlockSpec((tm, 3 * ATT_W), lambda i: (i, 1)),
        out_shape=_sds((SEQ, Z_MAIN), BF16),
        input_output_aliases={12: 0},
        scratch_shapes=[pltpu.VMEM((3, len(dilations), chunks, tm, LANE), F32)],
        compiler_params=_params("parallel"),
    )(*dqs, *dks, *dvs, *tabs, dz)


def _dil_valid(n):
    qi = lax.broadcasted_iota(jnp.int32, (DIL_BLK, 2 * DIL_BLK), 0)
    ki = lax.broadcasted_iota(jnp.int32, (DIL_BLK, 2 * DIL_BLK), 1)
    dist = qi + DIL_BLK - ki
    return (dist >= 0) & (dist <= DIL_BLK) & ((n > 0) | (ki >= DIL_BLK))


def _dil_fwd(qkv_v, d):
    length = SEQ // d
    nb = length // DIL_BLK
    nsub = min(DIL_STEP_BLOCKS, nb)

    def body(q_ref, kp_ref, kc_ref, vp_ref, vc_ref, o_ref, lse_ref):
        m_step = pl.program_id(1)
        lane = lax.broadcasted_iota(jnp.int32, (DIL_BLK, LANE), 1)
        jobs = [(sub, h) for sub in range(nsub) for h in range(N_HEADS)]
        rows = lambda sub: slice(sub * DIL_BLK, (sub + 1) * DIL_BLK)
        cols = lambda h: slice(h * HEAD_DIM, (h + 1) * HEAD_DIM)

        def keys(prev_ref, cur_ref, sub, h):
            before = prev_ref[:, cols(h)] if sub == 0 else cur_ref[rows(sub - 1), cols(h)]
            return jnp.concatenate([before, cur_ref[rows(sub), cols(h)]], axis=0)

        scores = [lax.dot_general(q_ref[rows(sub), cols(h)], keys(kp_ref, kc_ref, sub, h), NT_DIMS,
                                  preferred_element_type=F32) for sub, h in jobs]
        ok = [_dil_valid(m_step)] + [_dil_valid(1)] * (nsub - 1)
        probs, inv_l, lse_all = [], [], [jnp.zeros((DIL_BLK, LANE), F32)] * nsub
        for idx, (sub, h) in enumerate(jobs):
            s = jnp.where(ok[sub], scores[idx], NEG_INF)
            m = jnp.max(s, axis=-1, keepdims=True)
            p = jnp.exp(s - m)
            l = jnp.sum(p, axis=-1, keepdims=True)
            probs.append(p.astype(BF16))
            inv_l.append(1.0 / l)
            lse_all[sub] = jnp.where(lane == h, m + jnp.log(l), lse_all[sub])
        outs = [jnp.dot(probs[idx], keys(vp_ref, vc_ref, sub, h), preferred_element_type=F32) * inv_l[idx]
                for idx, (sub, h) in enumerate(jobs)]
        for sub in range(nsub):
            o_ref[rows(sub), :] = jnp.concatenate(outs[sub * N_HEADS:(sub + 1) * N_HEADS], axis=1).astype(o_ref.dtype)
            lse_ref[rows(sub), :] = lse_all[sub]

    pair = lambda f: pl.BlockSpec((nsub * DIL_BLK, ATT_W), f)
    one = lambda f: pl.BlockSpec((DIL_BLK, ATT_W), f)
    before = lambda m: jnp.maximum(nsub * m - 1, 0)
    o, lse = pl.pallas_call(
        body, name=f"dil_fwd_d{d}", grid=(d, nb // nsub),
        in_specs=[pair(lambda r, m: (m, 3 * r)),
                  one(lambda r, m: (before(m), 3 * r + 1)), pair(lambda r, m: (m, 3 * r + 1)),
                  one(lambda r, m: (before(m), 3 * r + 2)), pair(lambda r, m: (m, 3 * r + 2))],
        out_specs=[pair(lambda r, m: (m, r)), pl.BlockSpec((nsub * DIL_BLK, LANE), lambda r, m: (m, r))],
        out_shape=[_sds((length, d * ATT_W), BF16), _sds((length, d * LANE), F32)],
        compiler_params=_params("parallel", "arbitrary"),
    )(qkv_v, qkv_v, qkv_v, qkv_v, qkv_v)
    return o, lse


def _rows_from_slabs(view_ref, scr, d):
    chunks, rows = scr.shape[0], scr.shape[1]
    for r in range(d):
        for ch in range(chunks):
            lo = (r * chunks + ch) * LANE
            scr.at[ch][pl.ds(r, rows // d, stride=d), :] = view_ref[:, lo:lo + LANE].astype(F32)


def _slabs_from_rows(scr, view_ref, d):
    chunks, rows = scr.shape[0], scr.shape[1]
    for r in range(d):
        for ch in range(chunks):
            lo = (r * chunks + ch) * LANE
            view_ref[:, lo:lo + LANE] = scr.at[ch][pl.ds(r, rows // d, stride=d), :].astype(view_ref.dtype)


def _dil_merge(os_, lses, *, tm=512):
    dilations = [d for _, d in DIL_PATTERNS]
    o_chunks = ATT_W // LANE

    def body(o0, o1, o2, l0, l1, l2, y_ref, lse_ref, o_scr, l_scr):
        os_nat, ls = [], []
        for g, (o_ref, l_ref, d) in enumerate(zip((o0, o1, o2), (l0, l1, l2), dilations)):
            if d == 1:
                os_nat.append(o_ref[...].astype(F32))
                ls.append(l_ref[...])
            else:
                _rows_from_slabs(o_ref, o_scr.at[g], d)
                _rows_from_slabs(l_ref, l_scr.at[g], d)
                os_nat.append(jnp.concatenate([o_scr[g, ch] for ch in range(o_chunks)], axis=1))
                ls.append(l_scr[g, 0])
        m = jnp.maximum(jnp.maximum(ls[0], ls[1]), ls[2])
        es = [jnp.exp(l - m) for l in ls]
        tot = es[0] + es[1] + es[2]
        lse_ref[...] = m + jnp.log(tot)
        alphas = [e / tot for e in es]
        outs = []
        for h in range(N_HEADS):
            acc = None
            for g in range(3):
                term = alphas[g][:, h:h + 1] * _head(os_nat[g], h)
                acc = term if acc is None else acc + term
            outs.append(acc)
        y_ref[...] = jnp.concatenate(outs, axis=1).astype(y_ref.dtype)

    row = pl.BlockSpec((tm, ATT_W), lambda i: (i, 0))
    vec = pl.BlockSpec((tm, LANE), lambda i: (i, 0))
    view = lambda d, w: pl.BlockSpec((tm // d, d * w), lambda i: (i, 0))
    return pl.pallas_call(
        body, name="dil_merge", grid=(SEQ // tm,),
        in_specs=[view(d, ATT_W) for d in dilations] + [view(d, LANE) for d in dilations], out_specs=[row, vec],
        out_shape=[_sds((SEQ, ATT_W), BF16), _sds((SEQ, LANE), F32)],
        scratch_shapes=[pltpu.VMEM((3, o_chunks, tm, LANE), F32), pltpu.VMEM((3, 1, tm, LANE), F32)],
        compiler_params=_params("parallel"),
    )(*os_, *lses)


def _dil_bwd(qkv_v, do_v, lse_v, dl_v, d):
    length = SEQ // d
    nb = length // DIL_BLK
    nsub = min(DIL_STEP_BLOCKS, nb)
    n_steps = nb // nsub

    def body(q_ref, kp_ref, kc_ref, vp_ref, vc_ref, lse_ref, dl_ref, do_ref, dq_ref, dk_ref, dv_ref, dk_s, dv_s):
        m_step = pl.program_id(1)

        @pl.when(m_step == 0)
        def _():
            dk_s[...] = jnp.zeros_like(dk_s)
            dv_s[...] = jnp.zeros_like(dv_s)

        jobs = [(sub, h) for sub in range(nsub) for h in range(N_HEADS)]
        rows = lambda sub: slice(sub * DIL_BLK, (sub + 1) * DIL_BLK)
        cols = lambda h: slice(h * HEAD_DIM, (h + 1) * HEAD_DIM)

        def keys(prev_ref, cur_ref, sub, h):
            before = prev_ref[:, cols(h)] if sub == 0 else cur_ref[rows(sub - 1), cols(h)]
            return jnp.concatenate([before, cur_ref[rows(sub), cols(h)]], axis=0)

        kks = [keys(kp_ref, kc_ref, sub, h) for sub, h in jobs]
        scores = [lax.dot_general(q_ref[rows(sub), cols(h)], kks[idx], NT_DIMS, preferred_element_type=F32)
                  for idx, (sub, h) in enumerate(jobs)]
        dps = [lax.dot_general(do_ref[rows(sub), cols(h)], keys(vp_ref, vc_ref, sub, h), NT_DIMS,
                               preferred_element_type=F32) for sub, h in jobs]
        ok = [_dil_valid(m_step)] + [_dil_valid(1)] * (nsub - 1)
        ps, dss = [], []
        for idx, (sub, h) in enumerate(jobs):
            p = jnp.where(ok[sub], jnp.exp(scores[idx] - lse_ref[rows(sub), h:h + 1]), 0.0)
            ps.append(p.astype(BF16))
            dss.append((p * (dps[idx] - dl_ref[rows(sub), h:h + 1])).astype(BF16))
        dqs = [jnp.dot(dss[idx], kks[idx], preferred_element_type=F32) * SCALE for idx in range(len(jobs))]
        dkks = [lax.dot_general(dss[idx], q_ref[rows(sub), cols(h)], TN_DIMS, preferred_element_type=F32)
                for idx, (sub, h) in enumerate(jobs)]
        dvvs = [lax.dot_general(ps[idx], do_ref[rows(sub), cols(h)], TN_DIMS, preferred_element_type=F32)
                for idx, (sub, h) in enumerate(jobs)]
        for sub in range(nsub):
            dq_ref[rows(sub), :] = jnp.concatenate(dqs[sub * N_HEADS:(sub + 1) * N_HEADS], axis=1).astype(dq_ref.dtype)
        base = m_step * (nsub * DIL_BLK)
        blocks = [pl.ds(pl.multiple_of(jnp.maximum(base - DIL_BLK, 0), DIL_BLK), DIL_BLK)]
        blocks += [pl.ds(pl.multiple_of(base + s * DIL_BLK, DIL_BLK), DIL_BLK) for s in range(nsub)]
        for acc, parts in ((dk_s, dkks), (dv_s, dvvs)):
            top = lambda sub: jnp.concatenate([parts[sub * N_HEADS + h][:DIL_BLK] for h in range(N_HEADS)], axis=1)
            bottom = lambda sub: jnp.concatenate([parts[sub * N_HEADS + h][DIL_BLK:] for h in range(N_HEADS)], axis=1)
            acc[blocks[0], :] += top(0)
            for s in range(nsub):
                acc[blocks[s + 1], :] += bottom(s) + top(s + 1) if s + 1 < nsub else bottom(s)

        @pl.when(m_step == n_steps - 1)
        def _():
            dk_ref[...] = dk_s[...].astype(dk_ref.dtype)
            dv_ref[...] = dv_s[...].astype(dv_ref.dtype)

    pair = lambda f: pl.BlockSpec((nsub * DIL_BLK, ATT_W), f)
    one = lambda f: pl.BlockSpec((DIL_BLK, ATT_W), f)
    vec = lambda f: pl.BlockSpec((nsub * DIL_BLK, LANE), f)
    whole = pl.BlockSpec((length, ATT_W), lambda r, m: (0, r))
    before = lambda m: jnp.maximum(nsub * m - 1, 0)
    outs = pl.pallas_call(
        body, name=f"dil_bwd_d{d}", grid=(d, n_steps),
        in_specs=[pair(lambda r, m: (m, 3 * r)),
                  one(lambda r, m: (before(m), 3 * r + 1)), pair(lambda r, m: (m, 3 * r + 1)),
                  one(lambda r, m: (before(m), 3 * r + 2)), pair(lambda r, m: (m, 3 * r + 2)),
                  vec(lambda r, m: (m, r)), vec(lambda r, m: (m, r)), pair(lambda r, m: (m, r))],
        out_specs=[pair(lambda r, m: (m, r)), whole, whole],
        out_shape=[_sds((length, d * ATT_W), BF16)] * 3,
        scratch_shapes=[pltpu.VMEM((length, ATT_W), F32), pltpu.VMEM((length, ATT_W), F32)],
        compiler_params=_params("arbitrary", "arbitrary"),
    )(qkv_v, qkv_v, qkv_v, qkv_v, qkv_v, lse_v, dl_v, do_v)
    return outs


def _sigmoid(x):
    return 1.0 / (1.0 + jnp.exp(-x))


def _mix_fwd(ya, yb, w_oa, w_ob, zm, *, tm=512):
    def body(ya_ref, yb_ref, wa_ref, wb_ref, ga_ref, gb_ref, pa_ref, pb_ref, mix_ref):
        pa = jnp.dot(ya_ref[...], wa_ref[...], preferred_element_type=F32)
        pb = jnp.dot(yb_ref[...], wb_ref[...], preferred_element_type=F32)
        pa_ref[...] = pa.astype(pa_ref.dtype)
        pb_ref[...] = pb.astype(pb_ref.dtype)
        mix_ref[...] = (_sigmoid(ga_ref[...].astype(F32)) * pa + _sigmoid(gb_ref[...].astype(F32)) * pb
                        ).astype(mix_ref.dtype)

    row = pl.BlockSpec((tm, ATT_W), lambda i: (i, 0))
    wsp = pl.BlockSpec((ATT_W, D_MODEL), lambda i: (0, 0))
    wide = pl.BlockSpec((tm, D_MODEL), lambda i: (i, 0))
    return pl.pallas_call(
        body, name="mix_fwd", grid=(SEQ // tm,),
        in_specs=[row, row, wsp, wsp, pl.BlockSpec((tm, D_MODEL), lambda i: (i, 3)),
                  pl.BlockSpec((tm, D_MODEL), lambda i: (i, 4))],
        out_specs=[wide] * 3, out_shape=[_sds((SEQ, D_MODEL), BF16)] * 3,
        compiler_params=_params("parallel"),
    )(ya, yb, w_oa, w_ob, zm, zm)


def _gate_bwd(dmix, zm, p, gate_block, dz, *, name, tm=512):
    def body(dm_ref, g_ref, p_ref, *rest):
        dp_ref, dz_ref = rest[-2], rest[-1]
        dm = dm_ref[...].astype(F32)
        s = _sigmoid(g_ref[...].astype(F32))
        dp_ref[...] = (dm * s).astype(dp_ref.dtype)
        dz_ref[...] = (dm * p_ref[...].astype(F32) * s * (1.0 - s)).astype(dz_ref.dtype)

    wide = pl.BlockSpec((tm, D_MODEL), lambda i: (i, 0))
    gate = pl.BlockSpec((tm, D_MODEL), lambda i: (i, gate_block))
    extra = [] if dz is None else [dz]
    return pl.pallas_call(
        body, name=name, grid=(SEQ // tm,),
        in_specs=[wide, gate, wide] + [ANY] * len(extra),
        out_specs=[wide, gate],
        out_shape=[_sds((SEQ, D_MODEL), BF16), _sds((SEQ, Z_MAIN), BF16)],
        input_output_aliases={3: 1} if extra else {},
        compiler_params=_params("parallel"),
    )(dmix, zm, p, *extra)


def _out_fwd(mixed, w_out, x, g_post, g_pre, *, tm=512):
    def body(m_ref, w_ref, x_ref, gp_ref, gn_ref, y_ref, x2_ref, h_ref):
        y = jnp.dot(m_ref[...], w_ref[...], preferred_element_type=F32)
        y_ref[...] = y
        r = lax.rsqrt(jnp.mean(y * y, axis=-1, keepdims=True) + RMS_EPS)
        x2 = x_ref[...] + y * r * gp_ref[...]
        x2_ref[...] = x2
        r2 = lax.rsqrt(jnp.mean(x2 * x2, axis=-1, keepdims=True) + RMS_EPS)
        h_ref[...] = (x2 * r2 * gn_ref[...]).astype(h_ref.dtype)

    row = pl.BlockSpec((tm, D_MODEL), lambda i: (i, 0))
    vec = pl.BlockSpec((1, D_MODEL), lambda i: (0, 0))
    return pl.pallas_call(
        body, name="out_fwd", grid=(SEQ // tm,),
        in_specs=[row, pl.BlockSpec((D_MODEL, D_MODEL), lambda i: (0, 0)), row, vec, vec],
        out_specs=[row] * 3,
        out_shape=[_sds((SEQ, D_MODEL), F32), _sds((SEQ, D_MODEL), F32), _sds((SEQ, D_MODEL), BF16)],
        compiler_params=_params("parallel"),
    )(mixed, w_out, x, g_post, g_pre)


FFN_TM = 2048
FFN_HALF = 256
FFN_TN = 2 * FFN_HALF
FFN_NJ = D_FF // FFN_HALF
FFN_GROUP = 2 * SUBLANE


def _ffn_interleave(t):
    lead = t.shape[:-1]
    return jnp.swapaxes(t.reshape(*lead, 2, FFN_NJ, FFN_HALF), -3, -2).reshape(*lead, 2 * D_FF)


def _ffn_deinterleave(t):
    lead = t.shape[:-1]
    return jnp.swapaxes(t.reshape(*lead, FFN_NJ, 2, FFN_HALF), -3, -2).reshape(*lead, 2 * D_FF)


W_IN_SHARD = (Z_MAIN + N_HEADS) // N_DEV
FORGET_LO = 3 * ATT_W


def _w_in_from_shards(shards, *, tm=256):
    def columns(g_ref, lo, width):
        p, off = divmod(lo, W_IN_SHARD)
        if off + width <= W_IN_SHARD:
            return g_ref[p, :, off:off + width]
        first = W_IN_SHARD - off
        return jnp.concatenate([g_ref[p, :, off:], g_ref[p + 1, :, :width - first]], axis=1)

    def body(g_ref, main_ref, f_ref):
        for t in range(Z_MAIN // LANE):
            lo = t * LANE
            main_ref[:, lo:lo + LANE] = columns(g_ref, lo if lo < FORGET_LO else lo + N_HEADS, LANE)
        f_ref[...] = jnp.concatenate([columns(g_ref, FORGET_LO, N_HEADS),
                                      jnp.zeros((tm, F_PAD - N_HEADS), f_ref.dtype)], axis=1)

    return pl.pallas_call(
        body, name="w_in_from_shards", grid=(D_MODEL // tm,),
        in_specs=[pl.BlockSpec((N_DEV, tm, W_IN_SHARD), lambda i: (0, i, 0))],
        out_specs=[pl.BlockSpec((tm, Z_MAIN), lambda i: (i, 0)), pl.BlockSpec((tm, F_PAD), lambda i: (i, 0))],
        out_shape=[_sds((D_MODEL, Z_MAIN), shards.dtype), _sds((D_MODEL, F_PAD), shards.dtype)],
        compiler_params=_params("parallel"),
    )(shards)


def _w_in_to_shards(g_main, g_f, *, tm=256):
    def natural(main_ref, f_ref, lo, width):
        pieces, hi = [], lo + width
        for ref, start, stop, shift in ((main_ref, 0, FORGET_LO, 0), (f_ref, FORGET_LO, FORGET_LO + N_HEADS, FORGET_LO),
                                        (main_ref, FORGET_LO + N_HEADS, Z_MAIN + N_HEADS, N_HEADS)):
            a, b = max(lo, start), min(hi, stop)
            if a < b:
                pieces.append(ref[:, a - shift:b - shift])
        return pieces[0] if len(pieces) == 1 else jnp.concatenate(pieces, axis=1)

    def body(main_ref, f_ref, o_ref):
        for p in range(N_DEV):
            for q in range(-(-W_IN_SHARD // LANE)):
                width = min(LANE, W_IN_SHARD - q * LANE)
                o_ref[p, :, q * LANE:q * LANE + width] = natural(main_ref, f_ref, p * W_IN_SHARD + q * LANE, width)

    return pl.pallas_call(
        body, name="w_in_to_shards", grid=(D_MODEL // tm,),
        in_specs=[pl.BlockSpec((tm, Z_MAIN), lambda i: (i, 0)), pl.BlockSpec((tm, F_PAD), lambda i: (i, 0))],
        out_specs=pl.BlockSpec((N_DEV, tm, W_IN_SHARD), lambda i: (0, i, 0)),
        out_shape=_sds((N_DEV, D_MODEL, W_IN_SHARD), g_main.dtype),
        compiler_params=_params("parallel"),
    )(g_main, g_f)


W_UP_SHARD = 2 * D_FF // N_DEV


def _w_up_lane_tile(k):
    block = k // 2
    return (2 * (block % FFN_NJ) + block // FFN_NJ) * FFN_HALF + (k % 2) * LANE


def _w_up_from_shards(shards, *, tm=256):
    def body(g_ref, o_ref):
        for k in range(2 * D_FF // LANE):
            p, off = divmod(k * LANE, W_UP_SHARD)
            if off + LANE <= W_UP_SHARD:
                tile = g_ref[p, :, off:off + LANE]
            else:
                tile = jnp.concatenate([g_ref[p, :, off:], g_ref[p + 1, :, :off + LANE - W_UP_SHARD]], axis=1)
            dst = _w_up_lane_tile(k)
            o_ref[:, dst:dst + LANE] = tile

    return pl.pallas_call(
        body, name="w_up_from_shards", grid=(D_MODEL // tm,),
        in_specs=[pl.BlockSpec((N_DEV, tm, W_UP_SHARD), lambda i: (0, i, 0))],
        out_specs=pl.BlockSpec((tm, 2 * D_FF), lambda i: (i, 0)),
        out_shape=_sds((D_MODEL, 2 * D_FF), shards.dtype),
        compiler_params=_params("parallel"),
    )(shards)


def _w_up_to_shards(t, *, tm=256):
    def body(x_ref, o_ref):
        for p in range(N_DEV):
            for q in range(-(-W_UP_SHARD // LANE)):
                width = min(LANE, W_UP_SHARD - q * LANE)
                k, off = divmod(p * W_UP_SHARD + q * LANE, LANE)
                src = _w_up_lane_tile(k)
                if off == 0:
                    tile = x_ref[:, src:src + width]
                else:
                    tile = x_ref[:, src + off:src + LANE]
                    if width > LANE - off:
                        nxt = _w_up_lane_tile(k + 1)
                        tile = jnp.concatenate([tile, x_ref[:, nxt:nxt + width - (LANE - off)]], axis=1)
                o_ref[p, :, q * LANE:q * LANE + width] = tile

    return pl.pallas_call(
        body, name="w_up_to_shards", grid=(D_MODEL // tm,),
        in_specs=[pl.BlockSpec((tm, 2 * D_FF), lambda i: (i, 0))],
        out_specs=pl.BlockSpec((N_DEV, tm, W_UP_SHARD), lambda i: (0, i, 0)),
        out_shape=_sds((N_DEV, D_MODEL, W_UP_SHARD), t.dtype),
        compiler_params=_params("parallel"),
    )(t)


def _gelu_parts(a):
    c = math.sqrt(2.0 / math.pi)
    a2 = a * a
    t = jnp.tanh((c * a) * (1.0 + 0.044715 * a2))
    half_a, one_t = 0.5 * a, 1.0 + t
    gelu = half_a * one_t
    dgelu = 0.5 * one_t + half_a * (1.0 - t * t) * (c + (3.0 * 0.044715 * c) * a2)
    return gelu, dgelu


def _row_masks(down):
    row = lax.broadcasted_iota(jnp.int32, (SUBLANE, FFN_TN), 0)
    return (row < 1, row < 2) if down else (row >= SUBLANE - 1, row >= SUBLANE - 2)


def _rolled(x, down):
    return (pltpu.roll(x, 1, 0), pltpu.roll(x, 2, 0)) if down else (
        pltpu.roll(x, SUBLANE - 1, 0), pltpu.roll(x, SUBLANE - 2, 0))


def _shifted(cur_rolled, neighbour_rolled, masks):
    return (jnp.where(masks[0], neighbour_rolled[0], cur_rolled[0]),
            jnp.where(masks[1], neighbour_rolled[1], cur_rolled[1]))


def _conv_consts(w_ref, b_ref):
    shape = (SUBLANE, FFN_TN)
    return [jnp.broadcast_to(w_ref[k:k + 1, :], shape) for k in range(3)] + [jnp.broadcast_to(b_ref[...], shape)]


def _ffn_mid_fwd(u, conv_w, conv_b):
    per = FFN_TM // SUBLANE

    def body(u_ref, h_ref, w_ref, b_ref, m_ref, ab_ref):
        live = (pl.program_id(1) > 0).astype(F32)
        w0, w1, w2, bias = _conv_consts(w_ref, b_ref)
        masks = _row_masks(True)

        def group(g, above):
            rows = pl.ds(pl.multiple_of(g * FFN_GROUP, FFN_GROUP), FFN_GROUP)
            x = u_ref[rows, :].astype(F32)
            convs = []
            for c in range(2):
                cur = x[c * SUBLANE:(c + 1) * SUBLANE]
                cur_rolled = _rolled(cur, True)
                s1, s2 = _shifted(cur_rolled, above, masks)
                convs.append(w0 * s2 + w1 * s1 + w2 * cur + bias)
                above = cur_rolled
            y = jnp.concatenate(convs, axis=0)
            ab_ref[rows, :] = y.astype(ab_ref.dtype)
            m_ref[rows, :] = (_gelu_parts(y[:, :FFN_HALF])[0] * y[:, FFN_HALF:]).astype(m_ref.dtype)
            return above

        lax.fori_loop(0, FFN_TM // (2 * FFN_GROUP), lambda g2, carry: group(2 * g2 + 1, group(2 * g2, carry)),
                      _rolled(h_ref[...].astype(F32) * live, True))

    blk = pl.BlockSpec((FFN_TM, FFN_TN), lambda j, i: (i, j))
    return pl.pallas_call(
        body, name="ffn_mid_fwd", grid=(FFN_NJ, SEQ // FFN_TM),
        in_specs=[blk, pl.BlockSpec((SUBLANE, FFN_TN), lambda j, i: (jnp.maximum(i * per - 1, 0), j)),
                  pl.BlockSpec((3, FFN_TN), lambda j, i: (0, j)), pl.BlockSpec((1, FFN_TN), lambda j, i: (0, j))],
        out_specs=[pl.BlockSpec((FFN_TM, FFN_HALF), lambda j, i: (i, j)), blk],
        out_shape=[_sds((SEQ, D_FF), BF16), _sds((SEQ, 2 * D_FF), BF16)],
        compiler_params=_params("parallel", "arbitrary"),
    )(u, u, conv_w, conv_b)


def _ffn_mid_bwd(dm, u, ab, conv_w):
    nrow = SEQ // FFN_TM
    n_groups = FFN_TM // FFN_GROUP

    def body(dm_ref, u_ref, ab_ref, w_ref, du_ref, gw_ref, gb_ref, c_s):
        @pl.when(pl.program_id(1) == 0)
        def _():
            c_s[...] = jnp.zeros_like(c_s)
            gw_ref[...] = jnp.zeros_like(gw_ref)
            gb_ref[...] = jnp.zeros_like(gb_ref)

        taps = [jnp.broadcast_to(w_ref[k:k + 1, :], (SUBLANE, FFN_TN)) for k in range(3)]
        masks = _row_masks(False)

        def group(t, carry):
            below, acc = carry
            rows = pl.ds(pl.multiple_of((n_groups - 1 - t) * FFN_GROUP, FFN_GROUP), FFN_GROUP)
            x, y, dmv = u_ref[rows, :].astype(F32), ab_ref[rows, :].astype(F32), dm_ref[rows, :].astype(F32)
            gelu, dgelu = _gelu_parts(y[:, :FFN_HALF])
            d = jnp.concatenate([dmv * y[:, FFN_HALF:] * dgelu, dmv * gelu], axis=1)
            acc, pre = list(acc), [None, None]
            for c in (1, 0):
                sl = slice(c * SUBLANE, (c + 1) * SUBLANE)
                cur, xs = d[sl], x[sl]
                cur_rolled = _rolled(cur, False)
                up1, up2 = _shifted(cur_rolled, below, masks)
                acc = [acc[0] + up2 * xs, acc[1] + up1 * xs, acc[2] + cur * xs, acc[3] + cur]
                pre[c] = taps[2] * cur + taps[1] * up1 + taps[0] * up2
                below = cur_rolled
            du_ref[rows, :] = jnp.concatenate(pre, axis=0).astype(du_ref.dtype)
            return below, tuple(acc)

        zeros = jnp.zeros((SUBLANE, FFN_TN), F32)
        below, acc = lax.fori_loop(0, n_groups // 2, lambda t2, carry: group(2 * t2 + 1, group(2 * t2, carry)),
                                   (_rolled(c_s[...], False), (zeros,) * 4))
        c_s[...] = pltpu.roll(below[0], 1, 0)
        for k in range(3):
            gw_ref[k:k + 1, :] += jnp.sum(acc[k], axis=0, keepdims=True)
        gb_ref[...] += jnp.sum(acc[3], axis=0, keepdims=True)

    blk = pl.BlockSpec((FFN_TM, FFN_TN), lambda j, i: (nrow - 1 - i, j))
    return pl.pallas_call(
        body, name="ffn_mid_bwd", grid=(FFN_NJ, nrow),
        in_specs=[pl.BlockSpec((FFN_TM, FFN_HALF), lambda j, i: (nrow - 1 - i, j)), blk, blk,
                  pl.BlockSpec((3, FFN_TN), lambda j, i: (0, j))],
        out_specs=[blk, pl.BlockSpec((3, FFN_TN), lambda j, i: (0, j)), pl.BlockSpec((1, FFN_TN), lambda j, i: (0, j))],
        out_shape=[_sds((SEQ, 2 * D_FF), BF16), _sds((3, 2 * D_FF), F32), _sds((1, 2 * D_FF), F32)],
        scratch_shapes=[pltpu.VMEM((SUBLANE, FFN_TN), F32)],
        compiler_params=_params("parallel", "arbitrary"),
    )(dm, u, ab, conv_w)


def _down_fwd(m, w_down, x2, g_post, target, *, tm=512):
    def body(m_ref, w_ref, x2_ref, g_ref, t_ref, dout_ref, dy_ref, gg_ref, loss_ref):
        @pl.when(pl.program_id(0) == 0)
        def _():
            gg_ref[...] = jnp.zeros_like(gg_ref)
            loss_ref[...] = jnp.zeros_like(loss_ref)

        y = jnp.dot(m_ref[...], w_ref[...], preferred_element_type=F32)
        r = lax.rsqrt(jnp.mean(y * y, axis=-1, keepdims=True) + RMS_EPS)
        yn = y * r
        diff = (x2_ref[...] + yn * g_ref[...]) - t_ref[...]
        loss_ref[...] += jnp.sum(diff * diff)
        dout = diff * (1.0 / D_MODEL)
        dout_ref[...] = dout
        gg_ref[...] += jnp.sum(dout * yn, axis=0, keepdims=True)
        dn = dout * g_ref[...]
        dy_ref[...] = (r * (dn - yn * jnp.mean(dn * yn, axis=-1, keepdims=True))).astype(dy_ref.dtype)

    row = pl.BlockSpec((tm, D_MODEL), lambda i: (i, 0))
    vec = pl.BlockSpec((1, D_MODEL), lambda i: (0, 0))
    return pl.pallas_call(
        body, name="down_fwd", grid=(SEQ // tm,),
        in_specs=[pl.BlockSpec((tm, D_FF), lambda i: (i, 0)), pl.BlockSpec((D_FF, D_MODEL), lambda i: (0, 0)),
                  row, vec, row],
        out_specs=[row, row, vec, pl.BlockSpec((1, LANE), lambda i: (0, 0))],
        out_shape=[_sds((SEQ, D_MODEL), F32), _sds((SEQ, D_MODEL), BF16), _sds((1, D_MODEL), F32),
                   _sds((1, LANE), F32)],
        compiler_params=_params("arbitrary"),
    )(m, w_down, x2, g_post, target)


def _local_step(x, target, w_main, w_f, b_forget, conv_b, g_pre_mix, g_post_mix, g_pre_ffn, g_post_ffn,
                late_weights, ffn_grads_ready, proj_grads_ready, mixer_grads_ready):
    mm = _matmul
    tabs = _rope_tables()

    h1 = _rms_fwd(x, g_pre_mix, name="rms_pre_mix")
    zm = mm(h1, w_main, out_dtype=BF16, tm=2048, tn=1024, tk=1024, name="in_proj")
    zf = mm(h1, w_f, out_dtype=F32, tm=2048, tn=F_PAD, tk=1024, name="in_proj_forget")
    f_row, sg_row = _fox_prep(zf[:, :N_HEADS].T, b_forget.reshape(N_HEADS, 1))
    f_cols = jnp.pad(f_row.T, ((0, 0), (0, LANE - N_HEADS)))
    q_slots, k_slots, v_slots = _fox_pack_fwd(zm, f_cols)
    ya, lse_a = _fox_fwd(q_slots, k_slots, v_slots)
    qkv_d = dict(zip([d for _, d in DIL_PATTERNS], _rope_fwd(zm, tabs)))
    dil = [_dil_fwd(qkv_d[d], d) for _, d in DIL_PATTERNS]
    yb, lse_b = _dil_merge([o for o, _ in dil], [l for _, l in dil])
    w_oa, w_ob, w_out, w_up, conv_w, w_down = late_weights(yb)
    pa, pb, mixed = _mix_fwd(ya, yb, w_oa, w_ob, zm)
    y1, x2, h2 = _out_fwd(mixed, w_out, x, g_post_mix, g_pre_ffn)
    u = mm(h2, w_up, out_dtype=BF16, tm=2048, tn=D_FF // 2, tk=1024, name="up_proj")
    m, ab = _ffn_mid_fwd(u, conv_w, _ffn_interleave(conv_b))
    dout, dy2, gg_post_ffn, sq_err = _down_fwd(m, w_down, x2, g_post_ffn, target)

    g_w_down = mm(m, dy2, ta=True, out_dtype=BF16, tm=D_FF // 2, tn=1024, tk=2048, name="grad_w_down")
    dm = mm(dy2, w_down, tb=True, out_dtype=BF16, tm=2048, tn=D_FF // 2, tk=1024, name="d_ffn_mid")
    du, g_conv_w, g_conv_b = _ffn_mid_bwd(dm, u, ab, conv_w)
    g_w_up = mm(h2, du, ta=True, out_dtype=BF16, tm=1024, tn=D_FF // 2, tk=2048, name="grad_w_up")
    tok = ffn_grads_ready(dict(w_down=g_w_down, w_up_blocks=g_w_up, conv_w=_ffn_deinterleave(g_conv_w)))
    dh2 = mm(du, w_up, tb=True, out_dtype=BF16, tm=512, tn=1024, tk=2 * D_FF, name="d_h2")

    dx2, dy1, gg_pre_ffn, gg_post_mix = _rms_pair_bwd([dh2], x2, g_pre_ffn, dout, y1, g_post_mix + tok)
    g_w_out = mm(mixed, dy1, ta=True, out_dtype=BF16, tm=1024, tn=1024, tk=2048, name="grad_w_out")
    dmix = mm(dy1, w_out, tb=True, out_dtype=BF16, tm=2048, tn=1024, tk=1024, name="d_mixed")
    dpa, dz = _gate_bwd(dmix, zm, pa, 3, None, name="gate_bwd_fox")
    dpb, dz = _gate_bwd(dmix, zm, pb, 4, dz, name="gate_bwd_dil")
    g_w_oa = mm(ya, dpa, ta=True, out_dtype=BF16, tm=512, tn=1024, tk=SEQ, name="grad_w_o_fox")
    g_w_ob = mm(yb, dpb, ta=True, out_dtype=BF16, tm=512, tn=1024, tk=SEQ, name="grad_w_o_dil")
    tok = proj_grads_ready(dict(w_o_fox=g_w_oa, w_o_dil=g_w_ob, w_out=g_w_out))
    dya = mm(dpa, w_oa, tb=True, out_dtype=BF16, tm=2048, tn=512, tk=1024, name="d_y_fox")
    dyb = mm(dpb, w_ob, tb=True, out_dtype=BF16, tm=2048, tn=512, tk=1024, name="d_y_dil")

    qb_slots, do_slots = _fox_pack_bwd(zm, f_cols + tok, lse_a, ya, dya)
    dz, df_cols = _fox_unpack(*_fox_bwd(qb_slots, k_slots, v_slots, do_slots), dz)
    dfa_t, g_b_forget = _fox_post_bwd(df_cols[:, :N_HEADS].T, sg_row)

    rows_d = _dil_bwd_prep(yb, dyb, lse_b)
    dil_g = [_dil_bwd(qkv_d[d], *rows_d[k], d) for k, (_, d) in enumerate(DIL_PATTERNS)]
    dz = _dil_grad_combine([g[0] for g in dil_g], [g[1] for g in dil_g], [g[2] for g in dil_g], tabs, dz)

    dzf = jnp.pad(dfa_t.T, ((0, 0), (0, F_PAD - N_HEADS)))
    g_w_main = mm(h1, dz, ta=True, out_dtype=BF16, tm=1024, tn=Z_MAIN // 4, tk=2048, name="grad_w_in")
    g_w_f = mm(h1, dzf, ta=True, out_dtype=BF16, tm=1024, tn=F_PAD, tk=1024, name="grad_w_in_forget")
    tok = mixer_grads_ready(dict(w_main=g_w_main, w_f=g_w_f))
    dh1 = [mm(dz, w_main, tb=True, out_dtype=BF16, tm=512, tn=1024, tk=Z_MAIN, name="d_h1"),
           mm(dzf + tok, w_f, tb=True, out_dtype=BF16, tm=2048, tn=1024, tk=F_PAD, name="d_h1_forget")]
    grad_x, gg_pre_mix = _rms_bwd(dh1, x, g_pre_mix, dx2, out_dtype=F32, name="rms_pre_mix_bwd")

    grads = dict(
        b_forget=g_b_forget.reshape(1, N_HEADS), conv_b=_ffn_deinterleave(g_conv_b),
        g_pre_mix=gg_pre_mix, g_post_mix=gg_post_mix, g_pre_ffn=gg_pre_ffn, g_post_ffn=gg_post_ffn)
    return sq_err, grad_x, grads


def _exchange(arrays, scatter, *, name):
    n = len(arrays)
    scatters = [scatter] * n if isinstance(scatter, bool) else list(scatter)

    def body(*refs):
        ins, outs = refs[:n], refs[n:2 * n]
        send_sems, recv_sems, local_sems = refs[2 * n:]
        me, peers = _peers()

        def remote(a, k):
            dev, slot = peers[k]
            return pltpu.make_async_remote_copy(
                src_ref=ins[a].at[slot] if scatters[a] else ins[a], dst_ref=outs[a].at[me],
                send_sem=send_sems.at[a, k], recv_sem=recv_sems.at[a, k],
                device_id=dev, device_id_type=MESH_ID)

        def landed(a, k):
            dev, slot = peers[k]
            return pltpu.make_async_remote_copy(
                src_ref=outs[a].at[slot], dst_ref=outs[a].at[slot],
                send_sem=send_sems.at[a, k], recv_sem=recv_sems.at[a, k],
                device_id=dev, device_id_type=MESH_ID)

        own = [pltpu.make_async_copy(ins[a].at[me] if scatters[a] else ins[a], outs[a].at[me], local_sems.at[a])
               for a in range(n)]
        copies = [remote(a, k) for k in range(N_DEV - 1) for a in range(n)]
        for cp in own + copies:
            cp.start()
        for k in range(N_DEV - 1):
            for a in range(n):
                landed(a, k).wait_recv()
        for cp in copies:
            cp.wait_send()
        for cp in own:
            cp.wait()

    out_shape = [_sds(((N_DEV,) + a.shape[-2:]), a.dtype) for a in arrays]
    return pl.pallas_call(
        body, name=name, in_specs=[ANY] * n, out_specs=[ANY] * n, out_shape=out_shape,
        scratch_shapes=[pltpu.SemaphoreType.DMA((n, N_DEV - 1)), pltpu.SemaphoreType.DMA((n, N_DEV - 1)),
                        pltpu.SemaphoreType.DMA((n,))],
    )(*arrays)


def _gather_two_level(shard, *, name):
    def body(x_ref, out_ref, send_sems, recv_sems, local_sem):
        x, y, c = lax.axis_index("x"), lax.axis_index("y"), lax.axis_index("c")
        me, sibling = (x, y, c), (x, y, 1 - c)
        chips = [(1 - x, y), (x, 1 - y), (1 - x, 1 - y)]

        def slot(px, py, pc):
            return out_ref.at[4 * px + 2 * py + pc]

        def copy(k, block, to, src=None):
            return pltpu.make_async_remote_copy(
                src_ref=slot(*block) if src is None else src, dst_ref=slot(*block),
                send_sem=send_sems.at[k], recv_sem=recv_sems.at[k], device_id=to, device_id_type=MESH_ID)

        mine = pltpu.make_async_copy(x_ref, slot(*me), local_sem)
        mine.start()
        first = [copy(0, me, sibling, src=x_ref)]
        first += [copy(1 + j, me, (*chip, c), src=x_ref) for j, chip in enumerate(chips)]
        for cp in first:
            cp.start()
        passed = [copy(4 + j, (*chip, c), sibling) for j, chip in enumerate(chips)]
        for j, chip in enumerate(chips):
            copy(1 + j, (*chip, c), me).wait_recv()
            passed[j].start()
        copy(0, sibling, me).wait_recv()
        for j, chip in enumerate(chips):
            copy(4 + j, (*chip, 1 - c), me).wait_recv()
        for cp in first + passed:
            cp.wait_send()
        mine.wait()

    return pl.pallas_call(
        body, name=name, in_specs=[ANY], out_specs=ANY, out_shape=_sds((N_DEV,) + shard.shape, shard.dtype),
        scratch_shapes=[pltpu.SemaphoreType.DMA((N_DEV - 1,)), pltpu.SemaphoreType.DMA((N_DEV - 1,)),
                        pltpu.SemaphoreType.DMA],
    )(shard)


N_CHIPS = N_DEV // 2


def _peers(chips_only=False):
    x, y, c = lax.axis_index("x"), lax.axis_index("y"), lax.axis_index("c")
    out = []
    if chips_only:
        for k in range(1, N_CHIPS):
            px = 1 - x if k & 2 else x
            py = 1 - y if k & 1 else y
            out.append(((px, py, c), 2 * px + py))
        return 2 * x + y, out
    for k in range(1, N_DEV):
        px = 1 - x if k & 4 else x
        py = 1 - y if k & 2 else y
        pc = 1 - c if k & 1 else c
        out.append(((px, py, pc), 4 * px + 2 * py + pc))
    return 4 * x + 2 * y + c, out


def _sibling_swap(slot_arrays, *, name):
    n = len(slot_arrays)

    def body(*refs):
        ins, outs, send_sems, recv_sems = refs[:n], refs[n:2 * n], refs[2 * n], refs[2 * n + 1]
        x, y, c = lax.axis_index("x"), lax.axis_index("y"), lax.axis_index("c")
        copies = [pltpu.make_async_remote_copy(
            src_ref=ins[a].at[2 * q + (1 - c)], dst_ref=outs[a].at[q], send_sem=send_sems.at[a, q],
            recv_sem=recv_sems.at[a, q], device_id=(x, y, 1 - c), device_id_type=MESH_ID)
            for a in range(n) for q in range(N_CHIPS)]
        for cp in copies:
            cp.start()
        for cp in copies:
            cp.wait_recv()
        for cp in copies:
            cp.wait_send()

    return pl.pallas_call(
        body, name=name, in_specs=[ANY] * n, out_specs=[ANY] * n,
        out_shape=[_sds((N_CHIPS,) + t.shape[1:], t.dtype) for t in slot_arrays],
        scratch_shapes=[pltpu.SemaphoreType.DMA((n, N_CHIPS)), pltpu.SemaphoreType.DMA((n, N_CHIPS))],
    )(*slot_arrays)


def _pair_sum(slots, from_sibling, *, name, tn):
    _, r, c = slots.shape
    core = lax.axis_index("c").astype(jnp.int32).reshape(1)

    def body(core_ref, a_ref, b_ref, o_ref):
        o_ref[...] = (a_ref[...].astype(F32) + b_ref[...].astype(F32)).astype(o_ref.dtype)

    blk = lambda f: pl.BlockSpec((1, r, tn), f)
    return pl.pallas_call(
        body, name=name,
        grid_spec=pltpu.PrefetchScalarGridSpec(
            num_scalar_prefetch=1, grid=(N_CHIPS, c // tn),
            in_specs=[blk(lambda q, j, core: (2 * q + core[0], 0, j)), blk(lambda q, j, core: (q, 0, j))],
            out_specs=blk(lambda q, j, core: (q, 0, j))),
        out_shape=_sds((N_CHIPS, r, c), slots.dtype),
        compiler_params=_params("parallel", "parallel"),
    )(core, slots, from_sibling)


HBM = pl.BlockSpec(memory_space=pltpu.HBM)
SEM = pl.BlockSpec(memory_space=pltpu.SEMAPHORE)
DATAFLOW = pltpu.SideEffectType.DATAFLOW_SIDE_EFFECTING


def _split_copy(srcs, lands, send_sems, recv_sems, scatter, a, k, me, peers, incoming=False):
    dev, slot = peers[k]
    if incoming:
        src = dst = lands[a].at[slot]
    else:
        src, dst = (srcs[a].at[slot] if scatter else srcs[a]), lands[a].at[me]
    sem = a * len(peers) + k
    return pltpu.make_async_remote_copy(
        src_ref=src, dst_ref=dst, send_sem=send_sems.at[sem], recv_sem=recv_sems.at[sem],
        device_id=dev, device_id_type=MESH_ID)


def _exchange_start(arrays, scatter, *, name, chips_only=False):
    n = len(arrays)
    n_slots = N_CHIPS if chips_only else N_DEV

    def body(*refs):
        srcs, lands = refs[:n], refs[n:2 * n]
        send_sems, recv_sems = refs[2 * n], refs[2 * n + 1]
        token = refs[-1]
        me, peers = _peers(chips_only)
        for k in range(len(peers)):
            for a in range(n):
                _split_copy(srcs, lands, send_sems, recv_sems, scatter, a, k, me, peers).start()
        token[...] = jnp.zeros_like(token)

    land_shapes = [((n_slots,) + a.shape[-2:], a.dtype) for a in arrays]
    sems = pltpu.SemaphoreType.DMA((n * (n_slots - 1),))
    outs = pl.pallas_call(
        body, name=name,
        out_shape=(sems, sems, *[pltpu.HBM(a.shape, a.dtype) for a in arrays],
                   *[pltpu.HBM(s, d) for s, d in land_shapes], _sds((SUBLANE, LANE), F32)),
        in_specs=[HBM] * (2 * n),
        out_specs=(SEM, SEM, *[HBM] * (2 * n), pl.BlockSpec(memory_space=pltpu.VMEM)),
        input_output_aliases={i: 2 + i for i in range(2 * n)},
        compiler_params=pltpu.CompilerParams(has_side_effects=DATAFLOW),
    )(*[pltpu.with_memory_space_constraint(a, pltpu.HBM) for a in arrays],
      *[pltpu.with_memory_space_constraint(lax.empty(s, d), pltpu.HBM) for s, d in land_shapes])
    return (outs[0], outs[1], outs[2:2 + n], outs[2 + n:2 + 2 * n], scatter, chips_only), outs[-1]


def _exchange_wait(handles, after, *, name):
    send_sems, recv_sems, srcs, lands, scatter, chips_only = handles
    n = len(srcs)

    def body(*refs):
        src_refs, land_refs = refs[:n], refs[n:2 * n]
        send_ref, recv_ref = refs[2 * n], refs[2 * n + 1]
        me, peers = _peers(chips_only)
        for k in range(len(peers)):
            for a in range(n):
                _split_copy(src_refs, land_refs, send_ref, recv_ref, scatter, a, k, me, peers).wait_send()
                _split_copy(src_refs, land_refs, send_ref, recv_ref, scatter, a, k, me, peers, True).wait_recv()

    outs = pl.pallas_call(
        body, name=name,
        out_shape=tuple(pltpu.HBM(t.shape, t.dtype) for t in (*srcs, *lands)),
        in_specs=[HBM] * (2 * n) + [SEM, SEM, pl.BlockSpec(memory_space=pl.ANY)],
        out_specs=tuple([HBM] * (2 * n)),
        input_output_aliases={i: i for i in range(2 * n)},
        compiler_params=pltpu.CompilerParams(has_side_effects=DATAFLOW),
    )(*srcs, *lands, send_sems, recv_sems, after)
    return _with_own_slot(outs[n:], outs[:n], scatter, chips_only)


def _with_own_slot(landed, own, scatter, chips_only):
    me = 2 * lax.axis_index("x") + lax.axis_index("y")
    if not chips_only:
        me = 2 * me + lax.axis_index("c")
    out = []
    for buf, src in zip(landed, own):
        mine = lax.dynamic_index_in_dim(src, me, 0, keepdims=False) if scatter else src
        out.append(lax.dynamic_update_index_in_dim(buf, mine, me, 0))
    return out


def _adamw(parts, w, m, v, *, name, tm):
    r, c = w.shape
    assert r % tm == 0

    def body(p_ref, w_ref, m_ref, v_ref, g_ref, d_ref, nm_ref, nv_ref):
        _adamw_update(p_ref, w_ref, m_ref, v_ref, g_ref, d_ref, nm_ref, nv_ref)

    blk = pl.BlockSpec((tm, c), lambda i: (i, 0))
    return pl.pallas_call(
        body, name=name, grid=(r // tm,),
        in_specs=[pl.BlockSpec((parts.shape[0], tm, c), lambda i: (0, i, 0)), blk, blk, blk],
        out_specs=[blk] * 4, out_shape=[_sds((r, c), F32)] * 4,
        compiler_params=_params("parallel"),
    )(parts, w, m, v)


def _adamw_update(p_ref, w_ref, m_ref, v_ref, g_ref, d_ref, nm_ref, nv_ref):
    g = p_ref[0].astype(F32)
    for s in range(1, p_ref.shape[0]):
        g = g + p_ref[s].astype(F32)
    g_ref[...] = g
    m_new = ADAM_B1 * m_ref[...] + (1.0 - ADAM_B1) * g
    v_new = ADAM_B2 * v_ref[...] + (1.0 - ADAM_B2) * (g * g)
    nm_ref[...] = m_new
    nv_ref[...] = v_new
    m_hat = m_new / (1.0 - ADAM_B1 ** ADAM_STEP)
    v_hat = v_new / (1.0 - ADAM_B2 ** ADAM_STEP)
    d_ref[...] = -ADAM_LR * (m_hat / (jnp.sqrt(v_hat) + ADAM_EPS) + ADAM_WD * w_ref[...])


SMALL = ("g_pre_mix", "b_forget", "g_post_mix", "g_pre_ffn", "conv_b", "g_post_ffn")


def _adamw_small(parts, ws, ms, vs, sq_err_parts):
    n = len(ws)

    def body(*refs):
        ins, sq_ref, outs, loss_ref = refs[:4 * n], refs[4 * n], refs[4 * n + 1:-1], refs[-1]
        for i in range(n):
            _adamw_update(ins[i], ins[n + i], ins[2 * n + i], ins[3 * n + i], *outs[4 * i:4 * i + 4])
        total = sq_ref[0]
        for s in range(1, N_DEV):
            total = total + sq_ref[s]
        loss_ref[...] = total * (0.5 / D_MODEL)

    res = pl.pallas_call(
        body, name="adamw_small",
        out_shape=[_sds(w.shape, F32) for w in ws for _ in range(4)] + [_sds((1, LANE), F32)],
        compiler_params=pltpu.CompilerParams(vmem_limit_bytes=VMEM_LIMIT),
    )(*parts, *ws, *ms, *vs, sq_err_parts)
    return [res[4 * i:4 * i + 4] for i in range(n)], res[-1][0, 0]


def kernel(x, g_pre_mix, w_in, b_forget, w_o_fox, w_o_dil, w_out, g_post_mix, g_pre_ffn, w_up, conv_w, conv_b, w_down, g_post_ffn, loss_target, m_g_pre_mix, m_w_in, m_b_forget, m_w_o_fox, m_w_o_dil, m_w_out, m_g_post_mix, m_g_pre_ffn, m_w_up, m_conv_w, m_conv_b, m_w_down, m_g_post_ffn, v_g_pre_mix, v_w_in, v_b_forget, v_w_o_fox, v_w_o_dil, v_w_out, v_g_post_mix, v_g_pre_ffn, v_w_up, v_conv_w, v_conv_b, v_w_down, v_g_post_ffn):
    names = ("g_pre_mix", "w_in", "b_forget", "w_o_fox", "w_o_dil", "w_out", "g_post_mix", "g_pre_ffn",
             "w_up", "conv_w", "conv_b", "w_down", "g_post_ffn")
    w = dict(g_pre_mix=g_pre_mix, w_in=w_in, b_forget=b_forget, w_o_fox=w_o_fox, w_o_dil=w_o_dil, w_out=w_out,
             g_post_mix=g_post_mix, g_pre_ffn=g_pre_ffn, w_up=w_up, conv_w=conv_w, conv_b=conv_b, w_down=w_down,
             g_post_ffn=g_post_ffn)
    m = dict(g_pre_mix=m_g_pre_mix, w_in=m_w_in, b_forget=m_b_forget, w_o_fox=m_w_o_fox, w_o_dil=m_w_o_dil,
             w_out=m_w_out, g_post_mix=m_g_post_mix, g_pre_ffn=m_g_pre_ffn, w_up=m_w_up, conv_w=m_conv_w,
             conv_b=m_conv_b, w_down=m_w_down, g_post_ffn=m_g_post_ffn)
    v = dict(g_pre_mix=v_g_pre_mix, w_in=v_w_in, b_forget=v_b_forget, w_o_fox=v_w_o_fox, w_o_dil=v_w_o_dil,
             w_out=v_w_out, g_post_mix=v_g_post_mix, g_pre_ffn=v_g_pre_ffn, w_up=v_w_up, conv_w=v_conv_w,
             conv_b=v_conv_b, w_down=v_w_down, g_post_ffn=v_g_post_ffn)
    sharded = ("w_in", "w_o_fox", "w_o_dil", "w_out", "w_up", "w_down", "conv_w")
    wire = lambda n: F32 if n == "conv_w" else BF16

    by_cols = lambda t: jnp.transpose(t, (1, 0, 2)).reshape(t.shape[1], N_DEV * t.shape[2])
    by_rows = lambda t: t.reshape(N_DEV * t.shape[1], t.shape[2])
    col_slots = lambda t: jnp.transpose(t.reshape(t.shape[0], N_DEV, t.shape[1] // N_DEV), (1, 0, 2))
    row_slots = lambda t: t.reshape(N_DEV, t.shape[0] // N_DEV, t.shape[1])
    to_slots = lambda n, t: (row_slots if n in ("w_out", "w_down") else col_slots)(t).astype(wire(n))
    shard = lambda n: w[n][0].astype(wire(n))

    w_main, w_f = _w_in_from_shards(_gather_two_level(shard("w_in"), name="gather_w_in"))
    late = ("w_o_fox", "w_o_dil", "w_out", "w_up", "conv_w", "w_down")
    order = jnp.minimum(jnp.abs(w_f[0, 0].astype(F32)), 0.0)
    late_handles, late_tok = _exchange_start(
        [shard(n) + order.astype(wire(n)) if n == "conv_w" else shard(n) for n in late], False,
        name="gather_late_start")

    def late_weights(after):
        got = dict(zip(late, _exchange_wait(late_handles, after, name="gather_late_wait")))
        return (by_cols(got["w_o_fox"]), by_cols(got["w_o_dil"]), by_rows(got["w_out"]),
                _w_up_from_shards(got["w_up"]),
                _ffn_interleave(by_cols(got["conv_w"])),
                by_rows(got["w_down"]))

    pending = {}

    def ffn_grads_ready(g):
        slots = [to_slots("w_down", g["w_down"]), _w_up_to_shards(g["w_up_blocks"]), to_slots("conv_w", g["conv_w"])]
        pending["ffn"] = _exchange_start(slots, True, name="scatter_ffn_start")
        return pending["ffn"][1][0, 0]

    def proj_grads_ready(g):
        pending["proj"] = _exchange_start([to_slots(n, g[n]) for n in ("w_o_fox", "w_o_dil", "w_out")], True,
                                          name="scatter_proj_start")
        return pending["proj"][1][0, 0]

    def mixer_grads_ready(g):
        slots = _w_in_to_shards(g["w_main"], g["w_f"])
        theirs = _sibling_swap([slots], name="scatter_w_in_swap")[0]
        chip_sums = _pair_sum(slots, theirs, name="scatter_w_in_pair_sum", tn=W_IN_SHARD)
        pending["w_in"] = _exchange_start([chip_sums], True, name="scatter_w_in_start", chips_only=True)
        return pending["w_in"][1][0, 0]

    sq_err, grad_x, g = _local_step(
        x[0], loss_target[0], w_main, w_f, b_forget, conv_b, g_pre_mix + late_tok[0, 0], g_post_mix, g_pre_ffn,
        g_post_ffn, late_weights, ffn_grads_ready, proj_grads_ready, mixer_grads_ready)

    tiles = dict(w_in=256, w_o_fox=512, w_o_dil=512, w_out=128, w_up=256, w_down=176, conv_w=3)
    adam = lambda n, p: _adamw(p, w[n][0], m[n][0], v[n][0], name=f"adamw_{n}", tm=tiles[n])
    res = {}
    for key, group in (("ffn", ("w_down", "w_up", "conv_w")), ("proj", ("w_o_fox", "w_o_dil", "w_out"))):
        landed = _exchange_wait(pending[key][0], grad_x, name=f"scatter_{key}_wait")
        res.update({n: adam(n, p) for n, p in zip(group, landed)})
    done = res["w_up"][3]
    res["w_in"] = adam("w_in", _exchange_wait(pending["w_in"][0], done, name="scatter_w_in_wait")[0])
    small_parts = _exchange([g[n] for n in SMALL] + [sq_err], False, name="gather_small_grads")
    small, loss = _adamw_small(small_parts[:-1], *[[t[n] for n in SMALL] for t in (w, m, v)], small_parts[-1])
    small = dict(zip(SMALL, small))
    out = [[(res[n][k][None] if n in sharded else small[n][k]) for n in names] for k in range(4)]
    return (loss, grad_x[None], *out[0], *out[1], *out[2], *out[3])
```

```python
import functools
import math

import jax
import jax.numpy as jnp
import numpy as np
from jax import lax
from jax.experimental import pallas as pl
from jax.experimental.pallas import tpu as pltpu

F32 = jnp.float32
BF16 = jnp.bfloat16

SEQ = 4096
D_MODEL = 1024
N_HEADS = 8
HEAD_DIM = 64
ATT_W = N_HEADS * HEAD_DIM
D_FF = 2816
Z_MAIN = 5120
F_PAD = 128
ROPE_DIM = 16
ROPE_THETA = 500000.0
RMS_EPS = 1e-6
NEG_INF = -1e30
SCALE = 1.0 / math.sqrt(HEAD_DIM)
DIL_PATTERNS = ((128, 1), (512, 4), (2048, 16))
DIL_BLK = 128
DIL_STEP_BLOCKS = 2
N_DEV = 8

ADAM_LR = 0.001
ADAM_B1 = 0.9
ADAM_B2 = 0.999
ADAM_EPS = 1e-08
ADAM_WD = 0.01
ADAM_STEP = 10

LANE = 128
SUBLANE = 8
VMEM_LIMIT = 56 * 1024 * 1024
MESH_ID = pl.DeviceIdType.MESH
ANY = pl.BlockSpec(memory_space=pl.ANY)


def _params(*sem):
    return pltpu.CompilerParams(dimension_semantics=sem, vmem_limit_bytes=VMEM_LIMIT)


def _sds(shape, dtype):
    return jax.ShapeDtypeStruct(shape, dtype)


def _matmul(a, b, *, ta=False, tb=False, out_dtype, tm, tn, tk, name, b_k_off=0):
    if ta:
        kk, m = a.shape
    else:
        m, kk = a.shape
    n = b.shape[0] if tb else b.shape[1]
    tm, tn, tk = min(tm, m), min(tn, n), min(tk, kk)
    assert (b.shape[1] if tb else b.shape[0]) >= b_k_off * tk + kk
    assert m % tm == 0 and n % tn == 0 and kk % tk == 0, (name, m, n, kk, tm, tn, tk)
    nk = kk // tk
    dims = (((0 if ta else 1,), (1 if tb else 0,)), ((), ()))

    def body(a_ref, b_ref, o_ref, *scratch):
        p = lax.dot_general(a_ref[...].astype(BF16), b_ref[...].astype(BF16), dims,
                            preferred_element_type=F32)
        if nk == 1:
            o_ref[...] = p.astype(o_ref.dtype)
        else:
            acc = scratch[0]
            k = pl.program_id(2)

            @pl.when(k == 0)
            def _():
                acc[...] = p

            @pl.when(k > 0)
            def _():
                acc[...] += p

            @pl.when(k == nk - 1)
            def _():
                o_ref[...] = acc[...].astype(o_ref.dtype)

    a_spec = (pl.BlockSpec((tk, tm), lambda i, j, k: (k, i)) if ta
              else pl.BlockSpec((tm, tk), lambda i, j, k: (i, k)))
    b_spec = (pl.BlockSpec((tn, tk), lambda i, j, k: (j, k + b_k_off)) if tb
              else pl.BlockSpec((tk, tn), lambda i, j, k: (k + b_k_off, j)))
    return pl.pallas_call(
        body, name=name, grid=(m // tm, n // tn, nk),
        in_specs=[a_spec, b_spec],
        out_specs=pl.BlockSpec((tm, tn), lambda i, j, k: (i, j)),
        out_shape=_sds((m, n), out_dtype),
        scratch_shapes=[pltpu.VMEM((tm, tn), F32)] if nk > 1 else [],
        compiler_params=_params("parallel", "parallel", "arbitrary"),
    )(a, b)


def _rms_fwd(x, g, *, name, tm=512):
    def body(x_ref, g_ref, h_ref):
        xv = x_ref[...]
        r = lax.rsqrt(jnp.mean(xv * xv, axis=-1, keepdims=True) + RMS_EPS)
        h_ref[...] = (xv * r * g_ref[...]).astype(h_ref.dtype)

    return pl.pallas_call(
        body, name=name, grid=(SEQ // tm,),
        in_specs=[pl.BlockSpec((tm, D_MODEL), lambda i: (i, 0)), pl.BlockSpec((1, D_MODEL), lambda i: (0, 0))],
        out_specs=pl.BlockSpec((tm, D_MODEL), lambda i: (i, 0)),
        out_shape=_sds((SEQ, D_MODEL), BF16),
        compiler_params=_params("parallel"),
    )(x, g)


def _rms_bwd(dh_parts, xin, g, dres, *, out_dtype, name, tm=512):
    n_parts = len(dh_parts)
    has_res = dres is not None

    def body(*refs):
        parts = refs[:n_parts]
        x_ref, g_ref = refs[n_parts], refs[n_parts + 1]
        res_ref = refs[n_parts + 2] if has_res else None
        o_ref, gg_ref = refs[-2], refs[-1]
        dh = parts[0][...].astype(F32)
        for p in parts[1:]:
            dh = dh + p[...].astype(F32)
        xv = x_ref[...]
        r = lax.rsqrt(jnp.mean(xv * xv, axis=-1, keepdims=True) + RMS_EPS)
        xn = xv * r

        @pl.when(pl.program_id(0) == 0)
        def _():
            gg_ref[...] = jnp.zeros_like(gg_ref)

        gg_ref[...] += jnp.sum(dh * xn, axis=0, keepdims=True)
        dxn = dh * g_ref[...]
        dx = r * (dxn - xn * jnp.mean(dxn * xn, axis=-1, keepdims=True))
        if has_res:
            dx = dx + res_ref[...]
        o_ref[...] = dx.astype(o_ref.dtype)

    row = pl.BlockSpec((tm, D_MODEL), lambda i: (i, 0))
    vec = pl.BlockSpec((1, D_MODEL), lambda i: (0, 0))
    args = list(dh_parts) + [xin, g] + ([dres] if has_res else [])
    return pl.pallas_call(
        body, name=name, grid=(SEQ // tm,),
        in_specs=[row] * n_parts + [row, vec] + ([row] if has_res else []),
        out_specs=[row, vec],
        out_shape=[_sds((SEQ, D_MODEL), out_dtype), _sds((1, D_MODEL), F32)],
        compiler_params=_params("arbitrary"),
    )(*args)


def _rms_pair_bwd(dh_parts, x2, g_pre, dres, y1, g_post, *, tm=512):
    n_parts = len(dh_parts)

    def norm_bwd(dh, xin, g_ref, gg_ref):
        r = lax.rsqrt(jnp.mean(xin * xin, axis=-1, keepdims=True) + RMS_EPS)
        xn = xin * r
        gg_ref[...] += jnp.sum(dh * xn, axis=0, keepdims=True)
        dxn = dh * g_ref[...]
        return r * (dxn - xn * jnp.mean(dxn * xn, axis=-1, keepdims=True))

    def body(*refs):
        parts = refs[:n_parts]
        x2_ref, gpre_ref, res_ref, y1_ref, gpost_ref, dx2_ref, dy1_ref, ggpre_ref, ggpost_ref = refs[n_parts:]

        @pl.when(pl.program_id(0) == 0)
        def _():
            ggpre_ref[...] = jnp.zeros_like(ggpre_ref)
            ggpost_ref[...] = jnp.zeros_like(ggpost_ref)

        dh = parts[0][...].astype(F32)
        for p in parts[1:]:
            dh = dh + p[...].astype(F32)
        dx2 = res_ref[...] + norm_bwd(dh, x2_ref[...], gpre_ref, ggpre_ref)
        dx2_ref[...] = dx2
        dy1_ref[...] = norm_bwd(dx2, y1_ref[...], gpost_ref, ggpost_ref).astype(dy1_ref.dtype)

    row = pl.BlockSpec((tm, D_MODEL), lambda i: (i, 0))
    vec = pl.BlockSpec((1, D_MODEL), lambda i: (0, 0))
    return pl.pallas_call(
        body, name="rms_pair_bwd", grid=(SEQ // tm,),
        in_specs=[row] * n_parts + [row, vec, row, row, vec],
        out_specs=[row, row, vec, vec],
        out_shape=[_sds((SEQ, D_MODEL), F32), _sds((SEQ, D_MODEL), BF16), _sds((1, D_MODEL), F32),
                   _sds((1, D_MODEL), F32)],
        compiler_params=_params("arbitrary"),
    )(*dh_parts, x2, g_pre, dres, y1, g_post)


SCAN_BLK = 512


def _split_dot(v, tri):
    hi = v.astype(BF16)
    r1 = v - hi.astype(F32)
    mid = r1.astype(BF16)
    lo = (r1 - mid.astype(F32)).astype(BF16)
    dot = functools.partial(jnp.dot, preferred_element_type=F32)
    return dot(hi, tri) + dot(mid, tri) + dot(lo, tri)


def _fox_prep(fa_t, b_col):
    nblk = SEQ // SCAN_BLK

    def body(fa_ref, b_ref, f_ref, sg_ref):
        row = lax.broadcasted_iota(jnp.int32, (SCAN_BLK, SCAN_BLK), 0)
        col = lax.broadcasted_iota(jnp.int32, (SCAN_BLK, SCAN_BLK), 1)
        upper = (row <= col).astype(BF16)
        carry = jnp.zeros((N_HEADS, 1), F32)
        for blk in range(nblk):
            sl = pl.ds(blk * SCAN_BLK, SCAN_BLK)
            xx = fa_ref[:, sl] + b_ref[...]
            e = jnp.exp(-jnp.abs(xx))
            logf = jnp.minimum(xx, 0.0) - jnp.log(1.0 + e)
            sg_ref[:, sl] = jnp.where(xx >= 0.0, e, 1.0) / (1.0 + e)
            c = _split_dot(logf, upper) + carry
            f_ref[:, sl] = c
            carry = c[:, SCAN_BLK - 1:SCAN_BLK]

    return pl.pallas_call(
        body, name="fox_prep",
        out_shape=[_sds((N_HEADS, SEQ), F32), _sds((N_HEADS, SEQ), F32)],
        compiler_params=pltpu.CompilerParams(vmem_limit_bytes=VMEM_LIMIT),
    )(fa_t, b_col)


def _fox_post_bwd(df_t, sg_t):
    nblk = SEQ // SCAN_BLK

    def body(df_ref, sg_ref, dfa_ref, gb_ref):
        row = lax.broadcasted_iota(jnp.int32, (SCAN_BLK, SCAN_BLK), 0)
        col = lax.broadcasted_iota(jnp.int32, (SCAN_BLK, SCAN_BLK), 1)
        lower = (row >= col).astype(BF16)
        carry = jnp.zeros((N_HEADS, 1), F32)
        gb = jnp.zeros((N_HEADS, 1), F32)
        for blk in reversed(range(nblk)):
            sl = pl.ds(blk * SCAN_BLK, SCAN_BLK)
            c = _split_dot(df_ref[:, sl], lower) + carry
            carry = c[:, 0:1]
            dfa = c * sg_ref[:, sl]
            dfa_ref[:, sl] = dfa
            gb = gb + jnp.sum(dfa, axis=1, keepdims=True)
        gb_ref[...] = gb

    return pl.pallas_call(
        body, name="fox_post_bwd",
        out_shape=[_sds((N_HEADS, SEQ), F32), _sds((N_HEADS, 1), F32)],
        compiler_params=pltpu.CompilerParams(vmem_limit_bytes=VMEM_LIMIT),
    )(df_t, sg_t)


FOX_T = 512
NT_DIMS = (((1,), (1,)), ((), ()))
TN_DIMS = (((0,), (0,)), ((), ()))


def _head(ref_or_val, h):
    return ref_or_val[:, h * HEAD_DIM:(h + 1) * HEAD_DIM]


def _split3(v):
    hi = v.astype(BF16).astype(F32)
    r1 = v - hi
    mid = r1.astype(BF16).astype(F32)
    return hi, mid, (r1 - mid).astype(BF16).astype(F32)


ONE_LANE = 3 * N_HEADS


def _pack_terms(v, with_one):
    hi, mid, lo = _split3(v)
    t = hi + pltpu.roll(mid, N_HEADS, 1) + pltpu.roll(lo, 2 * N_HEADS, 1)
    if with_one:
        t = t + (lax.broadcasted_iota(jnp.int32, v.shape, 1) == ONE_LANE).astype(F32)
    return t.astype(BF16)


def _aux_matrices():
    to_q = np.zeros((LANE, N_HEADS * 2 * HEAD_DIM), np.float32)
    to_k = np.zeros_like(to_q)
    for h in range(N_HEADS):
        base = h * 2 * HEAD_DIM + HEAD_DIM
        for s in range(3):
            to_q[s * N_HEADS + h, base + s] = 1.0
            to_q[ONE_LANE, base + 3 + s] = 1.0
            to_k[ONE_LANE, base + s] = 1.0
            to_k[s * N_HEADS + h, base + 3 + s] = -1.0
    return jnp.asarray(to_q, BF16), jnp.asarray(to_k, BF16)


def _head_sums():
    total = np.zeros((N_HEADS * HEAD_DIM, LANE), np.float32)
    first = np.zeros_like(total)
    for h in range(N_HEADS):
        total[h * HEAD_DIM:(h + 1) * HEAD_DIM, h] = 1.0
        first[h * HEAD_DIM, h] = 1.0
    return jnp.asarray(total, BF16), jnp.asarray(first, BF16)


SLOT = 2 * HEAD_DIM
N_SPLIT = 3
FOX_FWD_HEADS = 8
FOX_BWD_HEADS = 4


def _slot(ref, h):
    return ref[:, h * SLOT:(h + 1) * SLOT]


def _fox_pack_fwd(zm, f_cols, *, tm=512):
    def body(q_ref, k_ref, v_ref, f_ref, tq_ref, tk_ref, qs_ref, ks_ref, vs_ref):
        ones = jnp.ones((tm, HEAD_DIM), BF16)
        terms = _pack_terms(f_ref[...], True)
        q_aux = jnp.dot(terms, tq_ref[...], preferred_element_type=F32).astype(BF16)
        k_aux = jnp.dot(terms, tk_ref[...], preferred_element_type=F32).astype(BF16)
        for h in range(N_HEADS):
            aux = slice(h * SLOT + HEAD_DIM, (h + 1) * SLOT)
            qs_ref[:, h * SLOT:(h + 1) * SLOT] = jnp.concatenate(
                [(_head(q_ref, h).astype(F32) * SCALE).astype(BF16), q_aux[:, aux]], axis=1)
            ks_ref[:, h * SLOT:(h + 1) * SLOT] = jnp.concatenate([_head(k_ref, h), k_aux[:, aux]], axis=1)
            vs_ref[:, h * SLOT:(h + 1) * SLOT] = jnp.concatenate([_head(v_ref, h), ones], axis=1)

    col = lambda b: pl.BlockSpec((tm, ATT_W), lambda i: (i, b))
    wide = pl.BlockSpec((tm, N_HEADS * SLOT), lambda i: (i, 0))
    const = pl.BlockSpec((LANE, N_HEADS * SLOT), lambda i: (0, 0))
    return pl.pallas_call(
        body, name="fox_pack_fwd", grid=(SEQ // tm,),
        in_specs=[col(0), col(1), col(2), pl.BlockSpec((tm, LANE), lambda i: (i, 0)), const, const],
        out_specs=[wide] * 3, out_shape=[_sds((SEQ, N_HEADS * SLOT), BF16)] * 3,
        compiler_params=_params("parallel"),
    )(zm, zm, zm, f_cols, *_aux_matrices())


def _fox_pack_bwd(zm, f_cols, lse, o, do, *, tm=512):
    def body(q_ref, f_ref, lse_ref, o_ref, do_ref, tq_ref, total_ref, first_ref, qs_ref, ds_ref):
        delta = _split_dot(o_ref[...].astype(F32) * do_ref[...].astype(F32), total_ref[...])
        lse_h = _split_dot(lse_ref[...], first_ref[...])
        q_aux = jnp.dot(_pack_terms(f_ref[...] - lse_h, True), tq_ref[...], preferred_element_type=F32).astype(BF16)
        d_aux = jnp.dot(_pack_terms(-delta, False), tq_ref[...], preferred_element_type=F32).astype(BF16)
        for h in range(N_HEADS):
            aux = slice(h * SLOT + HEAD_DIM, (h + 1) * SLOT)
            qs_ref[:, h * SLOT:(h + 1) * SLOT] = jnp.concatenate(
                [(_head(q_ref, h).astype(F32) * SCALE).astype(BF16), q_aux[:, aux]], axis=1)
            ds_ref[:, h * SLOT:(h + 1) * SLOT] = jnp.concatenate([_head(do_ref, h), d_aux[:, aux]], axis=1)

    row = pl.BlockSpec((tm, ATT_W), lambda i: (i, 0))
    wide = pl.BlockSpec((tm, N_HEADS * SLOT), lambda i: (i, 0))
    const = lambda r, c: pl.BlockSpec((r, c), lambda i: (0, 0))
    return pl.pallas_call(
        body, name="fox_pack_bwd", grid=(SEQ // tm,),
        in_specs=[row, pl.BlockSpec((tm, LANE), lambda i: (i, 0)), row, row, row,
                  const(LANE, N_HEADS * SLOT), const(ATT_W, LANE), const(ATT_W, LANE)],
        out_specs=[wide] * 2, out_shape=[_sds((SEQ, N_HEADS * SLOT), BF16)] * 2,
        compiler_params=_params("parallel"),
    )(zm, f_cols, lse, o, do, _aux_matrices()[0], *_head_sums())


def _causal_pairs(key_major):
    nb = SEQ // FOX_T
    if key_major:
        pairs = [(i, j) for j in range(nb) for i in range(j, nb)]
    else:
        pairs = [(i, j) for i in range(nb) for j in range(i + 1)]
    return (jnp.array([p[0] for p in pairs], jnp.int32), jnp.array([p[1] for p in pairs], jnp.int32), len(pairs))


FOX_HALF = FOX_T // 2
FOX_FULL = ((slice(0, FOX_T), slice(0, FOX_T), None),)
FOX_DIAG = ((slice(0, FOX_HALF), slice(0, FOX_HALF), 0), (slice(FOX_HALF, FOX_T), slice(0, FOX_T), FOX_HALF))


def _causal_piece_mask(q_rows, k_rows, offset):
    shape = (q_rows.stop - q_rows.start, k_rows.stop - k_rows.start)
    row = lax.broadcasted_iota(jnp.int32, shape, 0)
    col = lax.broadcasted_iota(jnp.int32, shape, 1)
    return col <= row + offset


def _fox_fwd(q_slots, k_slots, v_slots):
    i_tab, j_tab, n_pairs = _causal_pairs(False)

    def body(i_tab, j_tab, q_ref, k_ref, v_ref, o_ref, lse_ref, m_s, acc_s):
        t = pl.program_id(1)
        i, j = i_tab[t], j_tab[t]

        @pl.when(j == 0)
        def _():
            m_s[...] = jnp.full_like(m_s, NEG_INF)
            acc_s[...] = jnp.zeros_like(acc_s)

        def step(pieces):
            jobs = [(h, piece) for h in range(FOX_FWD_HEADS) for piece in pieces]
            lanes = lambda h: slice(h * SLOT, (h + 1) * SLOT)
            scores = [lax.dot_general(q_ref[qr, lanes(h)], k_ref[kr, lanes(h)], NT_DIMS, preferred_element_type=F32)
                      for h, (qr, kr, _) in jobs]
            probs, alphas = [], []
            for idx, (h, (qr, kr, offset)) in enumerate(jobs):
                s = scores[idx]
                if offset is not None:
                    s = jnp.where(_causal_piece_mask(qr, kr, offset), s, NEG_INF)
                m_prev = m_s[h, qr, :]
                m_new = jnp.maximum(m_prev, jnp.max(s, axis=-1, keepdims=True))
                probs.append(jnp.exp(s - jnp.tile(m_new, (1, s.shape[1] // LANE))).astype(BF16))
                alphas.append(jnp.exp(m_prev - m_new))
                m_s[h, qr, :] = m_new
            for idx, (h, (qr, kr, _)) in enumerate(jobs):
                acc_s[h, qr, :] = alphas[idx] * acc_s[h, qr, :] + jnp.dot(
                    probs[idx], v_ref[kr, lanes(h)], preferred_element_type=F32)

        @pl.when(j < i)
        def _():
            step(FOX_FULL)

        @pl.when(j == i)
        def _():
            step(FOX_DIAG)
            outs, lses = [], []
            for h in range(FOX_FWD_HEADS):
                acc = acc_s[h]
                l = acc[:, HEAD_DIM:]
                outs.append(acc[:, :HEAD_DIM] / l)
                lses.append(m_s[h][:, :HEAD_DIM] + jnp.log(l))
            o_ref[...] = jnp.concatenate(outs, axis=1).astype(o_ref.dtype)
            lse_ref[...] = jnp.concatenate(lses, axis=1)

    qspec = pl.BlockSpec((FOX_T, FOX_FWD_HEADS * SLOT), lambda p, t, it, jt: (it[t], p))
    kspec = pl.BlockSpec((FOX_T, FOX_FWD_HEADS * SLOT), lambda p, t, it, jt: (jt[t], p))
    ospec = pl.BlockSpec((FOX_T, FOX_FWD_HEADS * HEAD_DIM), lambda p, t, it, jt: (it[t], p))
    return pl.pallas_call(
        body, name="fox_fwd",
        grid_spec=pltpu.PrefetchScalarGridSpec(
            num_scalar_prefetch=2, grid=(N_HEADS // FOX_FWD_HEADS, n_pairs),
            in_specs=[qspec, kspec, kspec], out_specs=[ospec, ospec],
            scratch_shapes=[pltpu.VMEM((FOX_FWD_HEADS, FOX_T, LANE), F32),
                            pltpu.VMEM((FOX_FWD_HEADS, FOX_T, SLOT), F32)]),
        out_shape=[_sds((SEQ, ATT_W), BF16), _sds((SEQ, ATT_W), F32)],
        compiler_params=_params("parallel", "arbitrary"),
    )(i_tab, j_tab, q_slots, k_slots, v_slots)


def _fox_bwd(q_slots, k_slots, v_slots, do_slots):
    i_tab, j_tab, n_pairs = _causal_pairs(True)

    def body(i_tab, j_tab, q_ref, k_ref, v_ref, do_ref, dq_ref, dk_ref, dv_ref):
        t = pl.program_id(1)
        i, j = i_tab[t], j_tab[t]

        @pl.when(t == 0)
        def _():
            dq_ref[...] = jnp.zeros_like(dq_ref)

        @pl.when(i == j)
        def _():
            dk_ref[...] = jnp.zeros_like(dk_ref)
            dv_ref[...] = jnp.zeros_like(dv_ref)

        def step(pieces):
            jobs = [(h, piece) for h in range(FOX_BWD_HEADS) for piece in pieces]
            lanes = lambda h: slice(h * SLOT, (h + 1) * SLOT)
            scores = [lax.dot_general(q_ref[qr, lanes(h)], k_ref[kr, lanes(h)], NT_DIMS, preferred_element_type=F32)
                      for h, (qr, kr, _) in jobs]
            dps = [lax.dot_general(do_ref[qr, lanes(h)], v_ref[kr, lanes(h)], NT_DIMS, preferred_element_type=F32)
                   for h, (qr, kr, _) in jobs]
            ps, dss = [], []
            for idx, (h, (qr, kr, offset)) in enumerate(jobs):
                p = jnp.exp(scores[idx])
                if offset is not None:
                    p = jnp.where(_causal_piece_mask(qr, kr, offset), p, 0.0)
                ps.append(p.astype(BF16))
                dss.append((p * dps[idx]).astype(BF16))
            for idx, (h, (qr, kr, _)) in enumerate(jobs):
                rows = pl.ds(pl.multiple_of(i * FOX_T + qr.start, FOX_HALF), qr.stop - qr.start)
                dv_ref[kr, lanes(h)] += lax.dot_general(ps[idx], do_ref[qr, lanes(h)], TN_DIMS,
                                                        preferred_element_type=F32)
                dk_ref[kr, lanes(h)] += lax.dot_general(dss[idx], q_ref[qr, lanes(h)], TN_DIMS,
                                                        preferred_element_type=F32)
                dq_ref[rows, lanes(h)] += jnp.dot(dss[idx], k_ref[kr, lanes(h)], preferred_element_type=F32)

        @pl.when(i > j)
        def _():
            step(FOX_FULL)

        @pl.when(i == j)
        def _():
            step(FOX_DIAG)

    qspec = pl.BlockSpec((FOX_T, FOX_BWD_HEADS * SLOT), lambda p, t, it, jt: (it[t], p))
    kspec = pl.BlockSpec((FOX_T, FOX_BWD_HEADS * SLOT), lambda p, t, it, jt: (jt[t], p))
    return pl.pallas_call(
        body, name="fox_bwd",
        grid_spec=pltpu.PrefetchScalarGridSpec(
            num_scalar_prefetch=2, grid=(N_HEADS // FOX_BWD_HEADS, n_pairs),
            in_specs=[qspec, kspec, kspec, qspec],
            out_specs=[pl.BlockSpec((SEQ, FOX_BWD_HEADS * SLOT), lambda p, t, it, jt: (0, p)), kspec, kspec]),
        out_shape=[_sds((SEQ, N_HEADS * SLOT), F32)] * 3,
        compiler_params=_params("arbitrary", "arbitrary"),
    )(i_tab, j_tab, q_slots, k_slots, v_slots, do_slots)


def _fox_unpack(dq_slots, dk_slots, dv_slots, dz, *, tm=512):
    def body(dq_ref, dk_ref, dv_ref, dz_in, o_ref, df_ref):
        lane = lax.broadcasted_iota(jnp.int32, (tm, LANE), 1)
        df = jnp.zeros((tm, LANE), F32)
        for h in range(N_HEADS):
            lo = h * SLOT
            for part, (ref, mult) in enumerate(((dq_ref, SCALE), (dk_ref, 1.0), (dv_ref, 1.0))):
                o_ref[:, part * ATT_W + h * HEAD_DIM:part * ATT_W + (h + 1) * HEAD_DIM] = (
                    ref[:, lo:lo + HEAD_DIM] * mult).astype(o_ref.dtype)
            rows = dq_ref[:, lo + HEAD_DIM:lo + HEAD_DIM + 1]
            cols = dk_ref[:, lo + HEAD_DIM + N_SPLIT:lo + HEAD_DIM + N_SPLIT + 1]
            df = jnp.where(lane == h, rows - cols, df)
        df_ref[...] = df

    wide = pl.BlockSpec((tm, N_HEADS * SLOT), lambda i: (i, 0))
    return pl.pallas_call(
        body, name="fox_unpack", grid=(SEQ // tm,), in_specs=[wide] * 3 + [ANY],
        out_specs=[pl.BlockSpec((tm, 3 * ATT_W), lambda i: (i, 0)), pl.BlockSpec((tm, LANE), lambda i: (i, 0))],
        out_shape=[_sds((SEQ, Z_MAIN), BF16), _sds((SEQ, LANE), F32)],
        input_output_aliases={3: 0},
        compiler_params=_params("parallel"),
    )(dq_slots, dk_slots, dv_slots, dz)


def _dil_bwd_prep(o, do, lse, *, tm=512):
    dilations = [d for _, d in DIL_PATTERNS]
    o_chunks = ATT_W // LANE

    def body(o_ref, do_ref, lse_ref, *rest):
        outs, (do_scr, lse_scr, dl_scr) = rest[:-3], rest[-3:]
        dov = do_ref[...].astype(F32)
        prod = o_ref[...].astype(F32) * dov
        lane = lax.broadcasted_iota(jnp.int32, (tm, LANE), 1)
        delta = jnp.zeros((tm, LANE), F32)
        for h in range(N_HEADS):
            delta = jnp.where(lane == h, jnp.sum(_head(prod, h), axis=1, keepdims=True), delta)
        for ch in range(o_chunks):
            do_scr[ch] = dov[:, ch * LANE:(ch + 1) * LANE]
        lse_scr[0] = lse_ref[...]
        dl_scr[0] = delta
        for k, d in enumerate(dilations):
            for scr, out in zip((do_scr, lse_scr, dl_scr), outs[3 * k:3 * k + 3]):
                _slabs_from_rows(scr, out, d)

    row = pl.BlockSpec((tm, ATT_W), lambda i: (i, 0))
    view = lambda d, w: pl.BlockSpec((tm // d, d * w), lambda i: (i, 0))
    outs = pl.pallas_call(
        body, name="dil_bwd_prep", grid=(SEQ // tm,),
        in_specs=[row, row, pl.BlockSpec((tm, LANE), lambda i: (i, 0))],
        out_specs=[view(d, w) for d in dilations for w in (ATT_W, LANE, LANE)],
        out_shape=[_sds((SEQ // d, d * w), t) for d in dilations for w, t in ((ATT_W, BF16), (LANE, F32), (LANE, F32))],
        scratch_shapes=[pltpu.VMEM((o_chunks, tm, LANE), F32), pltpu.VMEM((1, tm, LANE), F32),
                        pltpu.VMEM((1, tm, LANE), F32)],
        compiler_params=_params("parallel"),
    )(o, do, lse)
    return [outs[3 * k:3 * k + 3] for k in range(len(dilations))]


def _rope_tables():
    half = ROPE_DIM // 2
    inv_freq = np.float32(ROPE_THETA) ** (-np.arange(half, dtype=np.float32) * np.float32(2.0) / np.float32(ROPE_DIM))
    ang = np.arange(SEQ, dtype=np.float32)[:, None] * inv_freq.astype(np.float32)[None, :]
    cos, sin = jnp.asarray(np.cos(ang).astype(np.float32)), jnp.asarray(np.sin(ang).astype(np.float32))
    ones = jnp.ones((SEQ, HEAD_DIM - ROPE_DIM), F32)
    zeros = jnp.zeros((SEQ, HEAD_DIM - ROPE_DIM), F32)
    zh = jnp.zeros((SEQ, half), F32)
    c_tab = jnp.concatenate([cos, cos, ones], axis=1)
    a_tab = jnp.concatenate([-sin, zh, zeros], axis=1)
    b_tab = jnp.concatenate([zh, sin, zeros], axis=1)
    two = lambda t: jnp.concatenate([t, t], axis=1)
    return two(c_tab), two(a_tab), two(b_tab)


def _rotate(x, c_tab, a_tab, b_tab):
    return x * c_tab + pltpu.roll(x, LANE - ROPE_DIM // 2, 1) * a_tab + pltpu.roll(x, ROPE_DIM // 2, 1) * b_tab


def _rope_fwd(zm, tabs, *, tm=512):
    width = 3 * ATT_W
    dilations = [d for _, d in DIL_PATTERNS]

    def body(q_ref, k_ref, v_ref, c_ref, a_ref, b_ref, *rest):
        outs, scr = rest[:-1], rest[-1]
        per_part = ATT_W // LANE
        for part, (x_ref, mult) in enumerate(((q_ref, SCALE), (k_ref, 1.0))):
            for cc in range(per_part):
                sl = slice(cc * LANE, (cc + 1) * LANE)
                scr[part * per_part + cc] = _rotate(x_ref[:, sl].astype(F32), c_ref[...], a_ref[...], b_ref[...]) * mult
        for cc in range(per_part):
            scr[2 * per_part + cc] = v_ref[:, cc * LANE:(cc + 1) * LANE].astype(F32)
        for o_ref, d in zip(outs, dilations):
            for r in range(d):
                for ch in range(width // LANE):
                    o_ref[:, r * width + ch * LANE:r * width + (ch + 1) * LANE] = (
                        scr.at[ch][pl.ds(r, tm // d, stride=d), :].astype(o_ref.dtype))

    tab = pl.BlockSpec((tm, LANE), lambda i: (i, 0))
    col = lambda b: pl.BlockSpec((tm, ATT_W), lambda i: (i, b))
    return pl.pallas_call(
        body, name="rope_fwd", grid=(SEQ // tm,),
        in_specs=[col(3), col(4), col(5), tab, tab, tab],
        out_specs=[pl.BlockSpec((tm // d, d * width), lambda i: (i, 0)) for d in dilations],
        out_shape=[_sds((SEQ // d, d * width), BF16) for d in dilations],
        scratch_shapes=[pltpu.VMEM((width // LANE, tm, LANE), F32)],
        compiler_params=_params("parallel"),
    )(zm, zm, zm, *tabs)


def _dil_grad_combine(dqs, dks, dvs, tabs, dz, *, tm=256):
    dilations = [d for _, d in DIL_PATTERNS]
    chunks = ATT_W // LANE

    def body(*refs):
        groups = (refs[0:3], refs[3:6], refs[6:9])
        c_ref, a_ref, b_ref, _, o_ref, scr = refs[9:]

        def total(part, cc):
            acc = None
            for g, (ref, d) in enumerate(zip(groups[part], dilations)):
                term = ref[:, cc * LANE:(cc + 1) * LANE].astype(F32) if d == 1 else scr[part, g, cc]
                acc = term if acc is None else acc + term
            return acc

        for part in range(3):
            for g, (ref, d) in enumerate(zip(groups[part], dilations)):
                if d > 1:
                    _rows_from_slabs(ref, scr.at[part, g], d)
        for cc in range(chunks):
            for part in range(2):
                o_ref[:, part * ATT_W + cc * LANE:part * ATT_W + (cc + 1) * LANE] = _rotate(
                    total(part, cc), c_ref[...], -a_ref[...], -b_ref[...]).astype(o_ref.dtype)
            o_ref[:, 2 * ATT_W + cc * LANE:2 * ATT_W + (cc + 1) * LANE] = total(2, cc).astype(o_ref.dtype)

    view = lambda d: pl.BlockSpec((tm // d, d * ATT_W), lambda i: (i, 0))
    tab = pl.BlockSpec((tm, LANE), lambda i: (i, 0))
    return pl.pallas_call(
        body, name="dil_grad_combine", grid=(SEQ // tm,),
        in_specs=[view(d) for d in dilations] * 3 + [tab] * 3 + [ANY],
        out_specs=pl.BlockSpec((tm, 3 * ATT_W), lambda i: (i, 1)),
        out_shape=_sds((SEQ, Z_MAIN), BF16),
        input_output_aliases={12: 0},
        scratch_shapes=[pltpu.VMEM((3, len(dilations), chunks, tm, LANE), F32)],
        compiler_params=_params("parallel"),
    )(*dqs, *dks, *dvs, *tabs, dz)


def _dil_valid(n):
    qi = lax.broadcasted_iota(jnp.int32, (DIL_BLK, 2 * DIL_BLK), 0)
    ki = lax.broadcasted_iota(jnp.int32, (DIL_BLK, 2 * DIL_BLK), 1)
    dist = qi + DIL_BLK - ki
    return (dist >= 0) & (dist <= DIL_BLK) & ((n > 0) | (ki >= DIL_BLK))


def _dil_fwd(qkv_v, d):
    length = SEQ // d
    nb = length // DIL_BLK
    nsub = min(DIL_STEP_BLOCKS, nb)

    def body(q_ref, kp_ref, kc_ref, vp_ref, vc_ref, o_ref, lse_ref):
        m_step = pl.program_id(1)
        lane = lax.broadcasted_iota(jnp.int32, (DIL_BLK, LANE), 1)
        jobs = [(sub, h) for sub in range(nsub) for h in range(N_HEADS)]
        rows = lambda sub: slice(sub * DIL_BLK, (sub + 1) * DIL_BLK)
        cols = lambda h: slice(h * HEAD_DIM, (h + 1) * HEAD_DIM)

        def keys(prev_ref, cur_ref, sub, h):
            before = prev_ref[:, cols(h)] if sub == 0 else cur_ref[rows(sub - 1), cols(h)]
            return jnp.concatenate([before, cur_ref[rows(sub), cols(h)]], axis=0)

        scores = [lax.dot_general(q_ref[rows(sub), cols(h)], keys(kp_ref, kc_ref, sub, h), NT_DIMS,
                                  preferred_element_type=F32) for sub, h in jobs]
        ok = [_dil_valid(m_step)] + [_dil_valid(1)] * (nsub - 1)
        probs, inv_l, lse_all = [], [], [jnp.zeros((DIL_BLK, LANE), F32)] * nsub
        for idx, (sub, h) in enumerate(jobs):
            s = jnp.where(ok[sub], scores[idx], NEG_INF)
            m = jnp.max(s, axis=-1, keepdims=True)
            p = jnp.exp(s - m)
            l = jnp.sum(p, axis=-1, keepdims=True)
            probs.append(p.astype(BF16))
            inv_l.append(1.0 / l)
            lse_all[sub] = jnp.where(lane == h, m + jnp.log(l), lse_all[sub])
        outs = [jnp.dot(probs[idx], keys(vp_ref, vc_ref, sub, h), preferred_element_type=F32) * inv_l[idx]
                for idx, (sub, h) in enumerate(jobs)]
        for sub in range(nsub):
            o_ref[rows(sub), :] = jnp.concatenate(outs[sub * N_HEADS:(sub + 1) * N_HEADS], axis=1).astype(o_ref.dtype)
            lse_ref[rows(sub), :] = lse_all[sub]

    pair = lambda f: pl.BlockSpec((nsub * DIL_BLK, ATT_W), f)
    one = lambda f: pl.BlockSpec((DIL_BLK, ATT_W), f)
    before = lambda m: jnp.maximum(nsub * m - 1, 0)
    o, lse = pl.pallas_call(
        body, name=f"dil_fwd_d{d}", grid=(d, nb // nsub),
        in_specs=[pair(lambda r, m: (m, 3 * r)),
                  one(lambda r, m: (before(m), 3 * r + 1)), pair(lambda r, m: (m, 3 * r + 1)),
                  one(lambda r, m: (before(m), 3 * r + 2)), pair(lambda r, m: (m, 3 * r + 2))],
        out_specs=[pair(lambda r, m: (m, r)), pl.BlockSpec((nsub * DIL_BLK, LANE), lambda r, m: (m, r))],
        out_shape=[_sds((length, d * ATT_W), BF16), _sds((length, d * LANE), F32)],
        compiler_params=_params("parallel", "arbitrary"),
    )(qkv_v, qkv_v, qkv_v, qkv_v, qkv_v)
    return o, lse


def _rows_from_slabs(view_ref, scr, d):
    chunks, rows = scr.shape[0], scr.shape[1]
    for r in range(d):
        for ch in range(chunks):
            lo = (r * chunks + ch) * LANE
            scr.at[ch][pl.ds(r, rows // d, stride=d), :] = view_ref[:, lo:lo + LANE].astype(F32)


def _slabs_from_rows(scr, view_ref, d):
    chunks, rows = scr.shape[0], scr.shape[1]
    for r in range(d):
        for ch in range(chunks):
            lo = (r * chunks + ch) * LANE
            view_ref[:, lo:lo + LANE] = scr.at[ch][pl.ds(r, rows // d, stride=d), :].astype(view_ref.dtype)


def _dil_merge(os_, lses, *, tm=512):
    dilations = [d for _, d in DIL_PATTERNS]
    o_chunks = ATT_W // LANE

    def body(o0, o1, o2, l0, l1, l2, y_ref, lse_ref, o_scr, l_scr):
        os_nat, ls = [], []
        for g, (o_ref, l_ref, d) in enumerate(zip((o0, o1, o2), (l0, l1, l2), dilations)):
            if d == 1:
                os_nat.append(o_ref[...].astype(F32))
                ls.append(l_ref[...])
            else:
                _rows_from_slabs(o_ref, o_scr.at[g], d)
                _rows_from_slabs(l_ref, l_scr.at[g], d)
                os_nat.append(jnp.concatenate([o_scr[g, ch] for ch in range(o_chunks)], axis=1))
                ls.append(l_scr[g, 0])
        m = jnp.maximum(jnp.maximum(ls[0], ls[1]), ls[2])
        es = [jnp.exp(l - m) for l in ls]
        tot = es[0] + es[1] + es[2]
        lse_ref[...] = m + jnp.log(tot)
        alphas = [e / tot for e in es]
        outs = []
        for h in range(N_HEADS):
            acc = None
            for g in range(3):
                term = alphas[g][:, h:h + 1] * _head(os_nat[g], h)
                acc = term if acc is None else acc + term
            outs.append(acc)
        y_ref[...] = jnp.concatenate(outs, axis=1).astype(y_ref.dtype)

    row = pl.BlockSpec((tm, ATT_W), lambda i: (i, 0))
    vec = pl.BlockSpec((tm, LANE), lambda i: (i, 0))
    view = lambda d, w: pl.BlockSpec((tm // d, d * w), lambda i: (i, 0))
    return pl.pallas_call(
        body, name="dil_merge", grid=(SEQ // tm,),
        in_specs=[view(d, ATT_W) for d in dilations] + [view(d, LANE) for d in dilations], out_specs=[row, vec],
        out_shape=[_sds((SEQ, ATT_W), BF16), _sds((SEQ, LANE), F32)],
        scratch_shapes=[pltpu.VMEM((3, o_chunks, tm, LANE), F32), pltpu.VMEM((3, 1, tm, LANE), F32)],
        compiler_params=_params("parallel"),
    )(*os_, *lses)


def _dil_bwd(qkv_v, do_v, lse_v, dl_v, d):
    length = SEQ // d
    nb = length // DIL_BLK
    nsub = min(DIL_STEP_BLOCKS, nb)
    n_steps = nb // nsub

    def body(q_ref, kp_ref, kc_ref, vp_ref, vc_ref, lse_ref, dl_ref, do_ref, dq_ref, dk_ref, dv_ref, dk_s, dv_s):
        m_step = pl.program_id(1)

        @pl.when(m_step == 0)
        def _():
            dk_s[...] = jnp.zeros_like(dk_s)
            dv_s[...] = jnp.zeros_like(dv_s)

        jobs = [(sub, h) for sub in range(nsub) for h in range(N_HEADS)]
        rows = lambda sub: slice(sub * DIL_BLK, (sub + 1) * DIL_BLK)
        cols = lambda h: slice(h * HEAD_DIM, (h + 1) * HEAD_DIM)

        def keys(prev_ref, cur_ref, sub, h):
            before = prev_ref[:, cols(h)] if sub == 0 else cur_ref[rows(sub - 1), cols(h)]
            return jnp.concatenate([before, cur_ref[rows(sub), cols(h)]], axis=0)

        kks = [keys(kp_ref, kc_ref, sub, h) for sub, h in jobs]
        scores = [lax.dot_general(q_ref[rows(sub), cols(h)], kks[idx], NT_DIMS, preferred_element_type=F32)
                  for idx, (sub, h) in enumerate(jobs)]
        dps = [lax.dot_general(do_ref[rows(sub), cols(h)], keys(vp_ref, vc_ref, sub, h), NT_DIMS,
                               preferred_element_type=F32) for sub, h in jobs]
        ok = [_dil_valid(m_step)] + [_dil_valid(1)] * (nsub - 1)
        ps, dss = [], []
        for idx, (sub, h) in enumerate(jobs):
            p = jnp.where(ok[sub], jnp.exp(scores[idx] - lse_ref[rows(sub), h:h + 1]), 0.0)
            ps.append(p.astype(BF16))
            dss.append((p * (dps[idx] - dl_ref[rows(sub), h:h + 1])).astype(BF16))
        dqs = [jnp.dot(dss[idx], kks[idx], preferred_element_type=F32) * SCALE for idx in range(len(jobs))]
        dkks = [lax.dot_general(dss[idx], q_ref[rows(sub), cols(h)], TN_DIMS, preferred_element_type=F32)
                for idx, (sub, h) in enumerate(jobs)]
        dvvs = [lax.dot_general(ps[idx], do_ref[rows(sub), cols(h)], TN_DIMS, preferred_element_type=F32)
                for idx, (sub, h) in enumerate(jobs)]
        for sub in range(nsub):
            dq_ref[rows(sub), :] = jnp.concatenate(dqs[sub * N_HEADS:(sub + 1) * N_HEADS], axis=1).astype(dq_ref.dtype)
        base = m_step * (nsub * DIL_BLK)
        blocks = [pl.ds(pl.multiple_of(jnp.maximum(base - DIL_BLK, 0), DIL_BLK), DIL_BLK)]
        blocks += [pl.ds(pl.multiple_of(base + s * DIL_BLK, DIL_BLK), DIL_BLK) for s in range(nsub)]
        for acc, parts in ((dk_s, dkks), (dv_s, dvvs)):
            top = lambda sub: jnp.concatenate([parts[sub * N_HEADS + h][:DIL_BLK] for h in range(N_HEADS)], axis=1)
            bottom = lambda sub: jnp.concatenate([parts[sub * N_HEADS + h][DIL_BLK:] for h in range(N_HEADS)], axis=1)
            acc[blocks[0], :] += top(0)
            for s in range(nsub):
                acc[blocks[s + 1], :] += bottom(s) + top(s + 1) if s + 1 < nsub else bottom(s)

        @pl.when(m_step == n_steps - 1)
        def _():
            dk_ref[...] = dk_s[...].astype(dk_ref.dtype)
            dv_ref[...] = dv_s[...].astype(dv_ref.dtype)

    pair = lambda f: pl.BlockSpec((nsub * DIL_BLK, ATT_W), f)
    one = lambda f: pl.BlockSpec((DIL_BLK, ATT_W), f)
    vec = lambda f: pl.BlockSpec((nsub * DIL_BLK, LANE), f)
    whole = pl.BlockSpec((length, ATT_W), lambda r, m: (0, r))
    before = lambda m: jnp.maximum(nsub * m - 1, 0)
    outs = pl.pallas_call(
        body, name=f"dil_bwd_d{d}", grid=(d, n_steps),
        in_specs=[pair(lambda r, m: (m, 3 * r)),
                  one(lambda r, m: (before(m), 3 * r + 1)), pair(lambda r, m: (m, 3 * r + 1)),
                  one(lambda r, m: (before(m), 3 * r + 2)), pair(lambda r, m: (m, 3 * r + 2)),
                  vec(lambda r, m: (m, r)), vec(lambda r, m: (m, r)), pair(lambda r, m: (m, r))],
        out_specs=[pair(lambda r, m: (m, r)), whole, whole],
        out_shape=[_sds((length, d * ATT_W), BF16)] * 3,
        scratch_shapes=[pltpu.VMEM((length, ATT_W), F32), pltpu.VMEM((length, ATT_W), F32)],
        compiler_params=_params("arbitrary", "arbitrary"),
    )(qkv_v, qkv_v, qkv_v, qkv_v, qkv_v, lse_v, dl_v, do_v)
    return outs


def _sigmoid(x):
    return 1.0 / (1.0 + jnp.exp(-x))


def _mix_fwd(ya, yb, w_oa, w_ob, zm, *, tm=512):
    def body(ya_ref, yb_ref, wa_ref, wb_ref, ga_ref, gb_ref, pa_ref, pb_ref, mix_ref):
        pa = jnp.dot(ya_ref[...], wa_ref[...], preferred_element_type=F32)
        pb = jnp.dot(yb_ref[...], wb_ref[...], preferred_element_type=F32)
        pa_ref[...] = pa.astype(pa_ref.dtype)
        pb_ref[...] = pb.astype(pb_ref.dtype)
        mix_ref[...] = (_sigmoid(ga_ref[...].astype(F32)) * pa + _sigmoid(gb_ref[...].astype(F32)) * pb
                        ).astype(mix_ref.dtype)

    row = pl.BlockSpec((tm, ATT_W), lambda i: (i, 0))
    wsp = pl.BlockSpec((ATT_W, D_MODEL), lambda i: (0, 0))
    wide = pl.BlockSpec((tm, D_MODEL), lambda i: (i, 0))
    return pl.pallas_call(
        body, name="mix_fwd", grid=(SEQ // tm,),
        in_specs=[row, row, wsp, wsp, pl.BlockSpec((tm, D_MODEL), lambda i: (i, 3)),
                  pl.BlockSpec((tm, D_MODEL), lambda i: (i, 4))],
        out_specs=[wide] * 3, out_shape=[_sds((SEQ, D_MODEL), BF16)] * 3,
        compiler_params=_params("parallel"),
    )(ya, yb, w_oa, w_ob, zm, zm)


def _gate_bwd(dmix, zm, p, gate_block, dz, *, name, tm=512):
    def body(dm_ref, g_ref, p_ref, *rest):
        dp_ref, dz_ref = rest[-2], rest[-1]
        dm = dm_ref[...].astype(F32)
        s = _sigmoid(g_ref[...].astype(F32))
        dp_ref[...] = (dm * s).astype(dp_ref.dtype)
        dz_ref[...] = (dm * p_ref[...].astype(F32) * s * (1.0 - s)).astype(dz_ref.dtype)

    wide = pl.BlockSpec((tm, D_MODEL), lambda i: (i, 0))
    gate = pl.BlockSpec((tm, D_MODEL), lambda i: (i, gate_block))
    extra = [] if dz is None else [dz]
    return pl.pallas_call(
        body, name=name, grid=(SEQ // tm,),
        in_specs=[wide, gate, wide] + [ANY] * len(extra),
        out_specs=[wide, gate],
        out_shape=[_sds((SEQ, D_MODEL), BF16), _sds((SEQ, Z_MAIN), BF16)],
        input_output_aliases={3: 1} if extra else {},
        compiler_params=_params("parallel"),
    )(dmix, zm, p, *extra)


def _out_fwd(mixed, w_out, x, g_post, g_pre, *, tm=512):
    def body(m_ref, w_ref, x_ref, gp_ref, gn_ref, y_ref, x2_ref, h_ref):
        y = jnp.dot(m_ref[...], w_ref[...], preferred_element_type=F32)
        y_ref[...] = y
        r = lax.rsqrt(jnp.mean(y * y, axis=-1, keepdims=True) + RMS_EPS)
        x2 = x_ref[...] + y * r * gp_ref[...]
        x2_ref[...] = x2
        r2 = lax.rsqrt(jnp.mean(x2 * x2, axis=-1, keepdims=True) + RMS_EPS)
        h_ref[...] = (x2 * r2 * gn_ref[...]).astype(h_ref.dtype)

    row = pl.BlockSpec((tm, D_MODEL), lambda i: (i, 0))
    vec = pl.BlockSpec((1, D_MODEL), lambda i: (0, 0))
    return pl.pallas_call(
        body, name="out_fwd", grid=(SEQ // tm,),
        in_specs=[row, pl.BlockSpec((D_MODEL, D_MODEL), lambda i: (0, 0)), row, vec, vec],
        out_specs=[row] * 3,
        out_shape=[_sds((SEQ, D_MODEL), F32), _sds((SEQ, D_MODEL), F32), _sds((SEQ, D_MODEL), BF16)],
        compiler_params=_params("parallel"),
    )(mixed, w_out, x, g_post, g_pre)


FFN_TM = 2048
FFN_HALF = 256
FFN_TN = 2 * FFN_HALF
FFN_NJ = D_FF // FFN_HALF
FFN_GROUP = 2 * SUBLANE


def _ffn_interleave(t):
    lead = t.shape[:-1]
    return jnp.swapaxes(t.reshape(*lead, 2, FFN_NJ, FFN_HALF), -3, -2).reshape(*lead, 2 * D_FF)


def _ffn_deinterleave(t):
    lead = t.shape[:-1]
    return jnp.swapaxes(t.reshape(*lead, FFN_NJ, 2, FFN_HALF), -3, -2).reshape(*lead, 2 * D_FF)


W_IN_SHARD = (Z_MAIN + N_HEADS) // N_DEV
FORGET_LO = 3 * ATT_W


def _w_in_from_shards(shards, *, tm=256):
    def columns(g_ref, lo, width):
        p, off = divmod(lo, W_IN_SHARD)
        if off + width <= W_IN_SHARD:
            return g_ref[p, :, off:off + width]
        first = W_IN_SHARD - off
        return jnp.concatenate([g_ref[p, :, off:], g_ref[p + 1, :, :width - first]], axis=1)

    def body(g_ref, main_ref, f_ref):
        for t in range(Z_MAIN // LANE):
            lo = t * LANE
            main_ref[:, lo:lo + LANE] = columns(g_ref, lo if lo < FORGET_LO else lo + N_HEADS, LANE)
        f_ref[...] = jnp.concatenate([columns(g_ref, FORGET_LO, N_HEADS),
                                      jnp.zeros((tm, F_PAD - N_HEADS), f_ref.dtype)], axis=1)

    return pl.pallas_call(
        body, name="w_in_from_shards", grid=(D_MODEL // tm,),
        in_specs=[pl.BlockSpec((N_DEV, tm, W_IN_SHARD), lambda i: (0, i, 0))],
        out_specs=[pl.BlockSpec((tm, Z_MAIN), lambda i: (i, 0)), pl.BlockSpec((tm, F_PAD), lambda i: (i, 0))],
        out_shape=[_sds((D_MODEL, Z_MAIN), shards.dtype), _sds((D_MODEL, F_PAD), shards.dtype)],
        compiler_params=_params("parallel"),
    )(shards)


def _w_in_to_shards(g_main, g_f, *, tm=256):
    def natural(main_ref, f_ref, lo, width):
        pieces, hi = [], lo + width
        for ref, start, stop, shift in ((main_ref, 0, FORGET_LO, 0), (f_ref, FORGET_LO, FORGET_LO + N_HEADS, FORGET_LO),
                                        (main_ref, FORGET_LO + N_HEADS, Z_MAIN + N_HEADS, N_HEADS)):
            a, b = max(lo, start), min(hi, stop)
            if a < b:
                pieces.append(ref[:, a - shift:b - shift])
        return pieces[0] if len(pieces) == 1 else jnp.concatenate(pieces, axis=1)

    def body(main_ref, f_ref, o_ref):
        for p in range(N_DEV):
            for q in range(-(-W_IN_SHARD // LANE)):
                width = min(LANE, W_IN_SHARD - q * LANE)
                o_ref[p, :, q * LANE:q * LANE + width] = natural(main_ref, f_ref, p * W_IN_SHARD + q * LANE, width)

    return pl.pallas_call(
        body, name="w_in_to_shards", grid=(D_MODEL // tm,),
        in_specs=[pl.BlockSpec((tm, Z_MAIN), lambda i: (i, 0)), pl.BlockSpec((tm, F_PAD), lambda i: (i, 0))],
        out_specs=pl.BlockSpec((N_DEV, tm, W_IN_SHARD), lambda i: (0, i, 0)),
        out_shape=_sds((N_DEV, D_MODEL, W_IN_SHARD), g_main.dtype),
        compiler_params=_params("parallel"),
    )(g_main, g_f)


W_UP_SHARD = 2 * D_FF // N_DEV


def _w_up_lane_tile(k):
    block = k // 2
    return (2 * (block % FFN_NJ) + block // FFN_NJ) * FFN_HALF + (k % 2) * LANE


def _w_up_from_shards(shards, *, tm=256):
    def body(g_ref, o_ref):
        for k in range(2 * D_FF // LANE):
            p, off = divmod(k * LANE, W_UP_SHARD)
            if off + LANE <= W_UP_SHARD:
                tile = g_ref[p, :, off:off + LANE]
            else:
                tile = jnp.concatenate([g_ref[p, :, off:], g_ref[p + 1, :, :off + LANE - W_UP_SHARD]], axis=1)
            dst = _w_up_lane_tile(k)
            o_ref[:, dst:dst + LANE] = tile

    return pl.pallas_call(
        body, name="w_up_from_shards", grid=(D_MODEL // tm,),
        in_specs=[pl.BlockSpec((N_DEV, tm, W_UP_SHARD), lambda i: (0, i, 0))],
        out_specs=pl.BlockSpec((tm, 2 * D_FF), lambda i: (i, 0)),
        out_shape=_sds((D_MODEL, 2 * D_FF), shards.dtype),
        compiler_params=_params("parallel"),
    )(shards)


def _w_up_to_shards(t, *, tm=256):
    def body(x_ref, o_ref):
        for p in range(N_DEV):
            for q in range(-(-W_UP_SHARD // LANE)):
                width = min(LANE, W_UP_SHARD - q * LANE)
                k, off = divmod(p * W_UP_SHARD + q * LANE, LANE)
                src = _w_up_lane_tile(k)
                if off == 0:
                    tile = x_ref[:, src:src + width]
                else:
                    tile = x_ref[:, src + off:src + LANE]
                    if width > LANE - off:
                        nxt = _w_up_lane_tile(k + 1)
                        tile = jnp.concatenate([tile, x_ref[:, nxt:nxt + width - (LANE - off)]], axis=1)
                o_ref[p, :, q * LANE:q * LANE + width] = tile

    return pl.pallas_call(
        body, name="w_up_to_shards", grid=(D_MODEL // tm,),
        in_specs=[pl.BlockSpec((tm, 2 * D_FF), lambda i: (i, 0))],
        out_specs=pl.BlockSpec((N_DEV, tm, W_UP_SHARD), lambda i: (0, i, 0)),
        out_shape=_sds((N_DEV, D_MODEL, W_UP_SHARD), t.dtype),
        compiler_params=_params("parallel"),
    )(t)


def _gelu_parts(a):
    c = math.sqrt(2.0 / math.pi)
    a2 = a * a
    t = jnp.tanh((c * a) * (1.0 + 0.044715 * a2))
    half_a, one_t = 0.5 * a, 1.0 + t
    gelu = half_a * one_t
    dgelu = 0.5 * one_t + half_a * (1.0 - t * t) * (c + (3.0 * 0.044715 * c) * a2)
    return gelu, dgelu


def _row_masks(down):
    row = lax.broadcasted_iota(jnp.int32, (SUBLANE, FFN_TN), 0)
    return (row < 1, row < 2) if down else (row >= SUBLANE - 1, row >= SUBLANE - 2)


def _rolled(x, down):
    return (pltpu.roll(x, 1, 0), pltpu.roll(x, 2, 0)) if down else (
        pltpu.roll(x, SUBLANE - 1, 0), pltpu.roll(x, SUBLANE - 2, 0))


def _shifted(cur_rolled, neighbour_rolled, masks):
    return (jnp.where(masks[0], neighbour_rolled[0], cur_rolled[0]),
            jnp.where(masks[1], neighbour_rolled[1], cur_rolled[1]))


def _conv_consts(w_ref, b_ref):
    shape = (SUBLANE, FFN_TN)
    return [jnp.broadcast_to(w_ref[k:k + 1, :], shape) for k in range(3)] + [jnp.broadcast_to(b_ref[...], shape)]


def _up_conv_fwd(h2, w_up, conv_w, conv_b):
    def body(h_ref, wu_ref, w_ref, b_ref, u_ref, ab_ref, m_ref, u_s, c1_s, c2_s):
        @pl.when(pl.program_id(1) == 0)
        def _():
            c1_s[...] = jnp.zeros_like(c1_s)
            c2_s[...] = jnp.zeros_like(c2_s)

        u = jnp.dot(h_ref[...], wu_ref[...], preferred_element_type=F32)
        u_s[...] = u
        u_ref[...] = u.astype(u_ref.dtype)
        w0, w1, w2, bias = _conv_consts(w_ref, b_ref)
        masks = _row_masks(True)

        def group(g, above):
            rows = pl.ds(pl.multiple_of(g * FFN_GROUP, FFN_GROUP), FFN_GROUP)
            x = u_s[rows, :]
            convs = []
            for c in range(2):
                cur = x[c * SUBLANE:(c + 1) * SUBLANE]
                cur_rolled = _rolled(cur, True)
                s1, s2 = _shifted(cur_rolled, above, masks)
                convs.append(w0 * s2 + w1 * s1 + w2 * cur + bias)
                above = cur_rolled
            y = jnp.concatenate(convs, axis=0)
            ab_ref[rows, :] = y.astype(ab_ref.dtype)
            m_ref[rows, :] = (_gelu_parts(y[:, :FFN_HALF])[0] * y[:, FFN_HALF:]).astype(m_ref.dtype)
            return above

        above = lax.fori_loop(0, FFN_TM // (2 * FFN_GROUP), lambda g2, carry: group(2 * g2 + 1, group(2 * g2, carry)),
                              (c1_s[...], c2_s[...]))
        c1_s[...], c2_s[...] = above

    blk = pl.BlockSpec((FFN_TM, FFN_TN), lambda j, i: (i, j))
    return pl.pallas_call(
        body, name="up_conv_fwd", grid=(FFN_NJ, SEQ // FFN_TM),
        in_specs=[pl.BlockSpec((FFN_TM, D_MODEL), lambda j, i: (i, 0)), pl.BlockSpec((D_MODEL, FFN_TN), lambda j, i: (0, j)),
                  pl.BlockSpec((3, FFN_TN), lambda j, i: (0, j)), pl.BlockSpec((1, FFN_TN), lambda j, i: (0, j))],
        out_specs=[blk, blk, pl.BlockSpec((FFN_TM, FFN_HALF), lambda j, i: (i, j))],
        out_shape=[_sds((SEQ, 2 * D_FF), BF16), _sds((SEQ, 2 * D_FF), BF16), _sds((SEQ, D_FF), BF16)],
        scratch_shapes=[pltpu.VMEM((FFN_TM, FFN_TN), F32), pltpu.VMEM((SUBLANE, FFN_TN), F32),
                        pltpu.VMEM((SUBLANE, FFN_TN), F32)],
        compiler_params=_params("parallel", "arbitrary"),
    )(h2, w_up, conv_w, conv_b)


def _ffn_mid_bwd(dy2, w_down, u, ab, conv_w):
    nrow = SEQ // FFN_TM
    n_groups = FFN_TM // FFN_GROUP

    def body(dy_ref, wd_ref, u_ref, ab_ref, w_ref, du_ref, gw_ref, gb_ref, c_s, dm_s):
        @pl.when(pl.program_id(1) == 0)
        def _():
            c_s[...] = jnp.zeros_like(c_s)
            gw_ref[...] = jnp.zeros_like(gw_ref)
            gb_ref[...] = jnp.zeros_like(gb_ref)

        dm_s[...] = lax.dot_general(dy_ref[...], wd_ref[...], NT_DIMS, preferred_element_type=F32)
        taps = [jnp.broadcast_to(w_ref[k:k + 1, :], (SUBLANE, FFN_TN)) for k in range(3)]
        masks = _row_masks(False)

        def group(t, carry):
            below, acc = carry
            rows = pl.ds(pl.multiple_of((n_groups - 1 - t) * FFN_GROUP, FFN_GROUP), FFN_GROUP)
            x, y, dmv = u_ref[rows, :].astype(F32), ab_ref[rows, :].astype(F32), dm_s[rows, :]
            gelu, dgelu = _gelu_parts(y[:, :FFN_HALF])
            d = jnp.concatenate([dmv * y[:, FFN_HALF:] * dgelu, dmv * gelu], axis=1)
            acc, pre = list(acc), [None, None]
            for c in (1, 0):
                sl = slice(c * SUBLANE, (c + 1) * SUBLANE)
                cur, xs = d[sl], x[sl]
                cur_rolled = _rolled(cur, False)
                up1, up2 = _shifted(cur_rolled, below, masks)
                acc = [acc[0] + up2 * xs, acc[1] + up1 * xs, acc[2] + cur * xs, acc[3] + cur]
                pre[c] = taps[2] * cur + taps[1] * up1 + taps[0] * up2
                below = cur_rolled
            du_ref[rows, :] = jnp.concatenate(pre, axis=0).astype(du_ref.dtype)
            return below, tuple(acc)

        zeros = jnp.zeros((SUBLANE, FFN_TN), F32)
        below, acc = lax.fori_loop(0, n_groups // 2, lambda t2, carry: group(2 * t2 + 1, group(2 * t2, carry)),
                                   (_rolled(c_s[...], False), (zeros,) * 4))
        c_s[...] = pltpu.roll(below[0], 1, 0)
        for k in range(3):
            gw_ref[k:k + 1, :] += jnp.sum(acc[k], axis=0, keepdims=True)
        gb_ref[...] += jnp.sum(acc[3], axis=0, keepdims=True)

    blk = pl.BlockSpec((FFN_TM, FFN_TN), lambda j, i: (nrow - 1 - i, j))
    return pl.pallas_call(
        body, name="ffn_mid_bwd", grid=(FFN_NJ, nrow),
        in_specs=[pl.BlockSpec((FFN_TM, D_MODEL), lambda j, i: (nrow - 1 - i, 0)),
                  pl.BlockSpec((FFN_HALF, D_MODEL), lambda j, i: (j, 0)), blk, blk,
                  pl.BlockSpec((3, FFN_TN), lambda j, i: (0, j))],
        out_specs=[blk, pl.BlockSpec((3, FFN_TN), lambda j, i: (0, j)), pl.BlockSpec((1, FFN_TN), lambda j, i: (0, j))],
        out_shape=[_sds((SEQ, 2 * D_FF), BF16), _sds((3, 2 * D_FF), F32), _sds((1, 2 * D_FF), F32)],
        scratch_shapes=[pltpu.VMEM((SUBLANE, FFN_TN), F32), pltpu.VMEM((FFN_TM, FFN_HALF), F32)],
        compiler_params=_params("parallel", "arbitrary"),
    )(dy2, w_down, u, ab, conv_w)


def _down_fwd(m, w_down, x2, g_post, target, *, tm=512):
    def body(m_ref, w_ref, x2_ref, g_ref, t_ref, dout_ref, dy_ref, gg_ref, loss_ref):
        @pl.when(pl.program_id(0) == 0)
        def _():
            gg_ref[...] = jnp.zeros_like(gg_ref)
            loss_ref[...] = jnp.zeros_like(loss_ref)

        y = jnp.dot(m_ref[...], w_ref[...], preferred_element_type=F32)
        r = lax.rsqrt(jnp.mean(y * y, axis=-1, keepdims=True) + RMS_EPS)
        yn = y * r
        diff = (x2_ref[...] + yn * g_ref[...]) - t_ref[...]
        loss_ref[...] += jnp.sum(diff * diff)
        dout = diff * (1.0 / D_MODEL)
        dout_ref[...] = dout
        gg_ref[...] += jnp.sum(dout * yn, axis=0, keepdims=True)
        dn = dout * g_ref[...]
        dy_ref[...] = (r * (dn - yn * jnp.mean(dn * yn, axis=-1, keepdims=True))).astype(dy_ref.dtype)

    row = pl.BlockSpec((tm, D_MODEL), lambda i: (i, 0))
    vec = pl.BlockSpec((1, D_MODEL), lambda i: (0, 0))
    return pl.pallas_call(
        body, name="down_fwd", grid=(SEQ // tm,),
        in_specs=[pl.BlockSpec((tm, D_FF), lambda i: (i, 0)), pl.BlockSpec((D_FF, D_MODEL), lambda i: (0, 0)),
                  row, vec, row],
        out_specs=[row, row, vec, pl.BlockSpec((1, LANE), lambda i: (0, 0))],
        out_shape=[_sds((SEQ, D_MODEL), F32), _sds((SEQ, D_MODEL), BF16), _sds((1, D_MODEL), F32),
                   _sds((1, LANE), F32)],
        compiler_params=_params("arbitrary"),
    )(m, w_down, x2, g_post, target)


def _local_step(x, target, w_main, w_f, b_forget, conv_b, g_pre_mix, g_post_mix, g_pre_ffn, g_post_ffn,
                late_weights, ffn_grads_ready, proj_grads_ready, mixer_grads_ready):
    mm = _matmul
    tabs = _rope_tables()

    h1 = _rms_fwd(x, g_pre_mix, name="rms_pre_mix")
    zm = mm(h1, w_main, out_dtype=BF16, tm=2048, tn=1024, tk=1024, name="in_proj")
    zf = mm(h1, w_f, out_dtype=F32, tm=2048, tn=F_PAD, tk=1024, name="in_proj_forget")
    f_row, sg_row = _fox_prep(zf[:, :N_HEADS].T, b_forget.reshape(N_HEADS, 1))
    f_cols = jnp.pad(f_row.T, ((0, 0), (0, LANE - N_HEADS)))
    q_slots, k_slots, v_slots = _fox_pack_fwd(zm, f_cols)
    ya, lse_a = _fox_fwd(q_slots, k_slots, v_slots)
    qkv_d = dict(zip([d for _, d in DIL_PATTERNS], _rope_fwd(zm, tabs)))
    dil = [_dil_fwd(qkv_d[d], d) for _, d in DIL_PATTERNS]
    yb, lse_b = _dil_merge([o for o, _ in dil], [l for _, l in dil])
    w_oa, w_ob, w_out, w_up, conv_w, w_down = late_weights(yb)
    pa, pb, mixed = _mix_fwd(ya, yb, w_oa, w_ob, zm)
    y1, x2, h2 = _out_fwd(mixed, w_out, x, g_post_mix, g_pre_ffn)
    u, ab, m = _up_conv_fwd(h2, w_up, conv_w, _ffn_interleave(conv_b))
    dout, dy2, gg_post_ffn, sq_err = _down_fwd(m, w_down, x2, g_post_ffn, target)

    g_w_down = mm(m, dy2, ta=True, out_dtype=BF16, tm=D_FF // 2, tn=1024, tk=2048, name="grad_w_down")
    du, g_conv_w, g_conv_b = _ffn_mid_bwd(dy2, w_down, u, ab, conv_w)
    g_w_up = mm(h2, du, ta=True, out_dtype=BF16, tm=1024, tn=D_FF // 2, tk=2048, name="grad_w_up")
    tok = ffn_grads_ready(dict(w_down=g_w_down, w_up_blocks=g_w_up, conv_w=_ffn_deinterleave(g_conv_w)))
    dh2 = mm(du, w_up, tb=True, out_dtype=BF16, tm=512, tn=1024, tk=2 * D_FF, name="d_h2")

    dx2, dy1, gg_pre_ffn, gg_post_mix = _rms_pair_bwd([dh2], x2, g_pre_ffn, dout, y1, g_post_mix + tok)
    g_w_out = mm(mixed, dy1, ta=True, out_dtype=BF16, tm=1024, tn=1024, tk=2048, name="grad_w_out")
    dmix = mm(dy1, w_out, tb=True, out_dtype=BF16, tm=2048, tn=1024, tk=1024, name="d_mixed")
    dpa, dz = _gate_bwd(dmix, zm, pa, 3, None, name="gate_bwd_fox")
    dpb, dz = _gate_bwd(dmix, zm, pb, 4, dz, name="gate_bwd_dil")
    g_w_oa = mm(ya, dpa, ta=True, out_dtype=BF16, tm=512, tn=1024, tk=SEQ, name="grad_w_o_fox")
    g_w_ob = mm(yb, dpb, ta=True, out_dtype=BF16, tm=512, tn=1024, tk=SEQ, name="grad_w_o_dil")
    tok = proj_grads_ready(dict(w_o_fox=g_w_oa, w_o_dil=g_w_ob, w_out=g_w_out))
    dya = mm(dpa, w_oa, tb=True, out_dtype=BF16, tm=2048, tn=512, tk=1024, name="d_y_fox")
    dyb = mm(dpb, w_ob, tb=True, out_dtype=BF16, tm=2048, tn=512, tk=1024, name="d_y_dil")

    qb_slots, do_slots = _fox_pack_bwd(zm, f_cols + tok, lse_a, ya, dya)
    dz, df_cols = _fox_unpack(*_fox_bwd(qb_slots, k_slots, v_slots, do_slots), dz)
    dfa_t, g_b_forget = _fox_post_bwd(df_cols[:, :N_HEADS].T, sg_row)

    rows_d = _dil_bwd_prep(yb, dyb, lse_b)
    dil_g = [_dil_bwd(qkv_d[d], *rows_d[k], d) for k, (_, d) in enumerate(DIL_PATTERNS)]
    dz = _dil_grad_combine([g[0] for g in dil_g], [g[1] for g in dil_g], [g[2] for g in dil_g], tabs, dz)

    dzf = jnp.pad(dfa_t.T, ((0, 0), (0, F_PAD - N_HEADS)))
    g_w_main = mm(h1, dz, ta=True, out_dtype=BF16, tm=1024, tn=Z_MAIN // 4, tk=2048, name="grad_w_in")
    g_w_f = mm(h1, dzf, ta=True, out_dtype=BF16, tm=1024, tn=F_PAD, tk=1024, name="grad_w_in_forget")
    tok = mixer_grads_ready(dict(w_main=g_w_main, w_f=g_w_f))
    dh1 = [mm(dz, w_main, tb=True, out_dtype=BF16, tm=512, tn=1024, tk=Z_MAIN, name="d_h1"),
           mm(dzf + tok, w_f, tb=True, out_dtype=BF16, tm=2048, tn=1024, tk=F_PAD, name="d_h1_forget")]
    grad_x, gg_pre_mix = _rms_bwd(dh1, x, g_pre_mix, dx2, out_dtype=F32, name="rms_pre_mix_bwd")

    grads = dict(
        b_forget=g_b_forget.reshape(1, N_HEADS), conv_b=_ffn_deinterleave(g_conv_b),
        g_pre_mix=gg_pre_mix, g_post_mix=gg_post_mix, g_pre_ffn=gg_pre_ffn, g_post_ffn=gg_post_ffn)
    return sq_err, grad_x, grads


def _exchange(arrays, scatter, *, name):
    n = len(arrays)
    scatters = [scatter] * n if isinstance(scatter, bool) else list(scatter)

    def body(*refs):
        ins, outs = refs[:n], refs[n:2 * n]
        send_sems, recv_sems, local_sems = refs[2 * n:]
        me, peers = _peers()

        def remote(a, k):
            dev, slot = peers[k]
            return pltpu.make_async_remote_copy(
                src_ref=ins[a].at[slot] if scatters[a] else ins[a], dst_ref=outs[a].at[me],
                send_sem=send_sems.at[a, k], recv_sem=recv_sems.at[a, k],
                device_id=dev, device_id_type=MESH_ID)

        def landed(a, k):
            dev, slot = peers[k]
            return pltpu.make_async_remote_copy(
                src_ref=outs[a].at[slot], dst_ref=outs[a].at[slot],
                send_sem=send_sems.at[a, k], recv_sem=recv_sems.at[a, k],
                device_id=dev, device_id_type=MESH_ID)

        own = [pltpu.make_async_copy(ins[a].at[me] if scatters[a] else ins[a], outs[a].at[me], local_sems.at[a])
               for a in range(n)]
        copies = [remote(a, k) for k in range(N_DEV - 1) for a in range(n)]
        for cp in own + copies:
            cp.start()
        for k in range(N_DEV - 1):
            for a in range(n):
                landed(a, k).wait_recv()
        for cp in copies:
            cp.wait_send()
        for cp in own:
            cp.wait()

    out_shape = [_sds(((N_DEV,) + a.shape[-2:]), a.dtype) for a in arrays]
    return pl.pallas_call(
        body, name=name, in_specs=[ANY] * n, out_specs=[ANY] * n, out_shape=out_shape,
        scratch_shapes=[pltpu.SemaphoreType.DMA((n, N_DEV - 1)), pltpu.SemaphoreType.DMA((n, N_DEV - 1)),
                        pltpu.SemaphoreType.DMA((n,))],
    )(*arrays)


def _gather_two_level(shard, *, name):
    def body(x_ref, out_ref, send_sems, recv_sems, local_sem):
        x, y, c = lax.axis_index("x"), lax.axis_index("y"), lax.axis_index("c")
        me, sibling = (x, y, c), (x, y, 1 - c)
        chips = [(1 - x, y), (x, 1 - y), (1 - x, 1 - y)]

        def slot(px, py, pc):
            return out_ref.at[4 * px + 2 * py + pc]

        def copy(k, block, to, src=None):
            return pltpu.make_async_remote_copy(
                src_ref=slot(*block) if src is None else src, dst_ref=slot(*block),
                send_sem=send_sems.at[k], recv_sem=recv_sems.at[k], device_id=to, device_id_type=MESH_ID)

        mine = pltpu.make_async_copy(x_ref, slot(*me), local_sem)
        mine.start()
        first = [copy(0, me, sibling, src=x_ref)]
        first += [copy(1 + j, me, (*chip, c), src=x_ref) for j, chip in enumerate(chips)]
        for cp in first:
            cp.start()
        passed = [copy(4 + j, (*chip, c), sibling) for j, chip in enumerate(chips)]
        for j, chip in enumerate(chips):
            copy(1 + j, (*chip, c), me).wait_recv()
            passed[j].start()
        copy(0, sibling, me).wait_recv()
        for j, chip in enumerate(chips):
            copy(4 + j, (*chip, 1 - c), me).wait_recv()
        for cp in first + passed:
            cp.wait_send()
        mine.wait()

    return pl.pallas_call(
        body, name=name, in_specs=[ANY], out_specs=ANY, out_shape=_sds((N_DEV,) + shard.shape, shard.dtype),
        scratch_shapes=[pltpu.SemaphoreType.DMA((N_DEV - 1,)), pltpu.SemaphoreType.DMA((N_DEV - 1,)),
                        pltpu.SemaphoreType.DMA],
    )(shard)


N_CHIPS = N_DEV // 2


def _peers(chips_only=False):
    x, y, c = lax.axis_index("x"), lax.axis_index("y"), lax.axis_index("c")
    out = []
    if chips_only:
        for k in range(1, N_CHIPS):
            px = 1 - x if k & 2 else x
            py = 1 - y if k & 1 else y
            out.append(((px, py, c), 2 * px + py))
        return 2 * x + y, out
    for k in range(1, N_DEV):
        px = 1 - x if k & 4 else x
        py = 1 - y if k & 2 else y
        pc = 1 - c if k & 1 else c
        out.append(((px, py, pc), 4 * px + 2 * py + pc))
    return 4 * x + 2 * y + c, out


def _sibling_swap(slot_arrays, *, name):
    n = len(slot_arrays)

    def body(*refs):
        ins, outs, send_sems, recv_sems = refs[:n], refs[n:2 * n], refs[2 * n], refs[2 * n + 1]
        x, y, c = lax.axis_index("x"), lax.axis_index("y"), lax.axis_index("c")
        copies = [pltpu.make_async_remote_copy(
            src_ref=ins[a].at[2 * q + (1 - c)], dst_ref=outs[a].at[q], send_sem=send_sems.at[a, q],
            recv_sem=recv_sems.at[a, q], device_id=(x, y, 1 - c), device_id_type=MESH_ID)
            for a in range(n) for q in range(N_CHIPS)]
        for cp in copies:
            cp.start()
        for cp in copies:
            cp.wait_recv()
        for cp in copies:
            cp.wait_send()

    return pl.pallas_call(
        body, name=name, in_specs=[ANY] * n, out_specs=[ANY] * n,
        out_shape=[_sds((N_CHIPS,) + t.shape[1:], t.dtype) for t in slot_arrays],
        scratch_shapes=[pltpu.SemaphoreType.DMA((n, N_CHIPS)), pltpu.SemaphoreType.DMA((n, N_CHIPS))],
    )(*slot_arrays)


def _pair_sum(slots, from_sibling, *, name, tn):
    _, r, c = slots.shape
    core = lax.axis_index("c").astype(jnp.int32).reshape(1)

    def body(core_ref, a_ref, b_ref, o_ref):
        o_ref[...] = (a_ref[...].astype(F32) + b_ref[...].astype(F32)).astype(o_ref.dtype)

    blk = lambda f: pl.BlockSpec((1, r, tn), f)
    return pl.pallas_call(
        body, name=name,
        grid_spec=pltpu.PrefetchScalarGridSpec(
            num_scalar_prefetch=1, grid=(N_CHIPS, c // tn),
            in_specs=[blk(lambda q, j, core: (2 * q + core[0], 0, j)), blk(lambda q, j, core: (q, 0, j))],
            out_specs=blk(lambda q, j, core: (q, 0, j))),
        out_shape=_sds((N_CHIPS, r, c), slots.dtype),
        compiler_params=_params("parallel", "parallel"),
    )(core, slots, from_sibling)


HBM = pl.BlockSpec(memory_space=pltpu.HBM)
SEM = pl.BlockSpec(memory_space=pltpu.SEMAPHORE)
DATAFLOW = pltpu.SideEffectType.DATAFLOW_SIDE_EFFECTING


def _split_copy(srcs, lands, send_sems, recv_sems, scatter, a, k, me, peers, incoming=False):
    dev, slot = peers[k]
    if incoming:
        src = dst = lands[a].at[slot]
    else:
        src, dst = (srcs[a].at[slot] if scatter else srcs[a]), lands[a].at[me]
    sem = a * len(peers) + k
    return pltpu.make_async_remote_copy(
        src_ref=src, dst_ref=dst, send_sem=send_sems.at[sem], recv_sem=recv_sems.at[sem],
        device_id=dev, device_id_type=MESH_ID)


def _exchange_start(arrays, scatter, *, name, chips_only=False):
    n = len(arrays)
    n_slots = N_CHIPS if chips_only else N_DEV

    def body(*refs):
        srcs, lands = refs[:n], refs[n:2 * n]
        send_sems, recv_sems = refs[2 * n], refs[2 * n + 1]
        token = refs[-1]
        me, peers = _peers(chips_only)
        for k in range(len(peers)):
            for a in range(n):
                _split_copy(srcs, lands, send_sems, recv_sems, scatter, a, k, me, peers).start()
        token[...] = jnp.zeros_like(token)

    land_shapes = [((n_slots,) + a.shape[-2:], a.dtype) for a in arrays]
    sems = pltpu.SemaphoreType.DMA((n * (n_slots - 1),))
    outs = pl.pallas_call(
        body, name=name,
        out_shape=(sems, sems, *[pltpu.HBM(a.shape, a.dtype) for a in arrays],
                   *[pltpu.HBM(s, d) for s, d in land_shapes], _sds((SUBLANE, LANE), F32)),
        in_specs=[HBM] * (2 * n),
        out_specs=(SEM, SEM, *[HBM] * (2 * n), pl.BlockSpec(memory_space=pltpu.VMEM)),
        input_output_aliases={i: 2 + i for i in range(2 * n)},
        compiler_params=pltpu.CompilerParams(has_side_effects=DATAFLOW),
    )(*[pltpu.with_memory_space_constraint(a, pltpu.HBM) for a in arrays],
      *[pltpu.with_memory_space_constraint(lax.empty(s, d), pltpu.HBM) for s, d in land_shapes])
    return (outs[0], outs[1], outs[2:2 + n], outs[2 + n:2 + 2 * n], scatter, chips_only), outs[-1]


def _exchange_wait(handles, after, *, name):
    send_sems, recv_sems, srcs, lands, scatter, chips_only = handles
    n = len(srcs)

    def body(*refs):
        src_refs, land_refs = refs[:n], refs[n:2 * n]
        send_ref, recv_ref = refs[2 * n], refs[2 * n + 1]
        me, peers = _peers(chips_only)
        for k in range(len(peers)):
            for a in range(n):
                _split_copy(src_refs, land_refs, send_ref, recv_ref, scatter, a, k, me, peers).wait_send()
                _split_copy(src_refs, land_refs, send_ref, recv_ref, scatter, a, k, me, peers, True).wait_recv()

    outs = pl.pallas_call(
        body, name=name,
        out_shape=tuple(pltpu.HBM(t.shape, t.dtype) for t in (*srcs, *lands)),
        in_specs=[HBM] * (2 * n) + [SEM, SEM, pl.BlockSpec(memory_space=pl.ANY)],
        out_specs=tuple([HBM] * (2 * n)),
        input_output_aliases={i: i for i in range(2 * n)},
        compiler_params=pltpu.CompilerParams(has_side_effects=DATAFLOW),
    )(*srcs, *lands, send_sems, recv_sems, after)
    return _with_own_slot(outs[n:], outs[:n], scatter, chips_only)


def _with_own_slot(landed, own, scatter, chips_only):
    me = 2 * lax.axis_index("x") + lax.axis_index("y")
    if not chips_only:
        me = 2 * me + lax.axis_index("c")
    out = []
    for buf, src in zip(landed, own):
        mine = lax.dynamic_index_in_dim(src, me, 0, keepdims=False) if scatter else src
        out.append(lax.dynamic_update_index_in_dim(buf, mine, me, 0))
    return out


def _adamw(parts, w, m, v, *, name, tm):
    r, c = w.shape
    assert r % tm == 0

    def body(p_ref, w_ref, m_ref, v_ref, g_ref, d_ref, nm_ref, nv_ref):
        _adamw_update(p_ref, w_ref, m_ref, v_ref, g_ref, d_ref, nm_ref, nv_ref)

    blk = pl.BlockSpec((tm, c), lambda i: (i, 0))
    return pl.pallas_call(
        body, name=name, grid=(r // tm,),
        in_specs=[pl.BlockSpec((parts.shape[0], tm, c), lambda i: (0, i, 0)), blk, blk, blk],
        out_specs=[blk] * 4, out_shape=[_sds((r, c), F32)] * 4,
        compiler_params=_params("parallel"),
    )(parts, w, m, v)


def _adamw_update(p_ref, w_ref, m_ref, v_ref, g_ref, d_ref, nm_ref, nv_ref):
    g = p_ref[0].astype(F32)
    for s in range(1, p_ref.shape[0]):
        g = g + p_ref[s].astype(F32)
    g_ref[...] = g
    m_new = ADAM_B1 * m_ref[...] + (1.0 - ADAM_B1) * g
    v_new = ADAM_B2 * v_ref[...] + (1.0 - ADAM_B2) * (g * g)
    nm_ref[...] = m_new
    nv_ref[...] = v_new
    m_hat = m_new / (1.0 - ADAM_B1 ** ADAM_STEP)
    v_hat = v_new / (1.0 - ADAM_B2 ** ADAM_STEP)
    d_ref[...] = -ADAM_LR * (m_hat / (jnp.sqrt(v_hat) + ADAM_EPS) + ADAM_WD * w_ref[...])


SMALL = ("g_pre_mix", "b_forget", "g_post_mix", "g_pre_ffn", "conv_b", "g_post_ffn")


def _adamw_small(parts, ws, ms, vs, sq_err_parts):
    n = len(ws)

    def body(*refs):
        ins, sq_ref, outs, loss_ref = refs[:4 * n], refs[4 * n], refs[4 * n + 1:-1], refs[-1]
        for i in range(n):
            _adamw_update(ins[i], ins[n + i], ins[2 * n + i], ins[3 * n + i], *outs[4 * i:4 * i + 4])
        total = sq_ref[0]
        for s in range(1, N_DEV):
            total = total + sq_ref[s]
        loss_ref[...] = total * (0.5 / D_MODEL)

    res = pl.pallas_call(
        body, name="adamw_small",
        out_shape=[_sds(w.shape, F32) for w in ws for _ in range(4)] + [_sds((1, LANE), F32)],
        compiler_params=pltpu.CompilerParams(vmem_limit_bytes=VMEM_LIMIT),
    )(*parts, *ws, *ms, *vs, sq_err_parts)
    return [res[4 * i:4 * i + 4] for i in range(n)], res[-1][0, 0]


def kernel(x, g_pre_mix, w_in, b_forget, w_o_fox, w_o_dil, w_out, g_post_mix, g_pre_ffn, w_up, conv_w, conv_b, w_down, g_post_ffn, loss_target, m_g_pre_mix, m_w_in, m_b_forget, m_w_o_fox, m_w_o_dil, m_w_out, m_g_post_mix, m_g_pre_ffn, m_w_up, m_conv_w, m_conv_b, m_w_down, m_g_post_ffn, v_g_pre_mix, v_w_in, v_b_forget, v_w_o_fox, v_w_o_dil, v_w_out, v_g_post_mix, v_g_pre_ffn, v_w_up, v_conv_w, v_conv_b, v_w_down, v_g_post_ffn):
    names = ("g_pre_mix", "w_in", "b_forget", "w_o_fox", "w_o_dil", "w_out", "g_post_mix", "g_pre_ffn",
             "w_up", "conv_w", "conv_b", "w_down", "g_post_ffn")
    w = dict(g_pre_mix=g_pre_mix, w_in=w_in, b_forget=b_forget, w_o_fox=w_o_fox, w_o_dil=w_o_dil, w_out=w_out,
             g_post_mix=g_post_mix, g_pre_ffn=g_pre_ffn, w_up=w_up, conv_w=conv_w, conv_b=conv_b, w_down=w_down,
             g_post_ffn=g_post_ffn)
    m = dict(g_pre_mix=m_g_pre_mix, w_in=m_w_in, b_forget=m_b_forget, w_o_fox=m_w_o_fox, w_o_dil=m_w_o_dil,
             w_out=m_w_out, g_post_mix=m_g_post_mix, g_pre_ffn=m_g_pre_ffn, w_up=m_w_up, conv_w=m_conv_w,
             conv_b=m_conv_b, w_down=m_w_down, g_post_ffn=m_g_post_ffn)
    v = dict(g_pre_mix=v_g_pre_mix, w_in=v_w_in, b_forget=v_b_forget, w_o_fox=v_w_o_fox, w_o_dil=v_w_o_dil,
             w_out=v_w_out, g_post_mix=v_g_post_mix, g_pre_ffn=v_g_pre_ffn, w_up=v_w_up, conv_w=v_conv_w,
             conv_b=v_conv_b, w_down=v_w_down, g_post_ffn=v_g_post_ffn)
    sharded = ("w_in", "w_o_fox", "w_o_dil", "w_out", "w_up", "w_down", "conv_w")
    wire = lambda n: F32 if n == "conv_w" else BF16

    by_cols = lambda t: jnp.transpose(t, (1, 0, 2)).reshape(t.shape[1], N_DEV * t.shape[2])
    by_rows = lambda t: t.reshape(N_DEV * t.shape[1], t.shape[2])
    col_slots = lambda t: jnp.transpose(t.reshape(t.shape[0], N_DEV, t.shape[1] // N_DEV), (1, 0, 2))
    row_slots = lambda t: t.reshape(N_DEV, t.shape[0] // N_DEV, t.shape[1])
    to_slots = lambda n, t: (row_slots if n in ("w_out", "w_down") else col_slots)(t).astype(wire(n))
    shard = lambda n: w[n][0].astype(wire(n))

    w_main, w_f = _w_in_from_shards(_gather_two_level(shard("w_in"), name="gather_w_in"))
    late = ("w_o_fox", "w_o_dil", "w_out", "w_up", "conv_w", "w_down")
    order = jnp.minimum(jnp.abs(w_f[0, 0].astype(F32)), 0.0)
    late_handles, late_tok = _exchange_start(
        [shard(n) + order.astype(wire(n)) if n == "conv_w" else shard(n) for n in late], False,
        name="gather_late_start")

    def late_weights(after):
        got = dict(zip(late, _exchange_wait(late_handles, after, name="gather_late_wait")))
        return (by_cols(got["w_o_fox"]), by_cols(got["w_o_dil"]), by_rows(got["w_out"]),
                _w_up_from_shards(got["w_up"]),
                _ffn_interleave(by_cols(got["conv_w"])),
                by_rows(got["w_down"]))

    pending = {}

    def ffn_grads_ready(g):
        slots = [to_slots("w_down", g["w_down"]), _w_up_to_shards(g["w_up_blocks"]), to_slots("conv_w", g["conv_w"])]
        pending["ffn"] = _exchange_start(slots, True, name="scatter_ffn_start")
        return pending["ffn"][1][0, 0]

    def proj_grads_ready(g):
        pending["proj"] = _exchange_start([to_slots(n, g[n]) for n in ("w_o_fox", "w_o_dil", "w_out")], True,
                                          name="scatter_proj_start")
        return pending["proj"][1][0, 0]

    def mixer_grads_ready(g):
        slots = _w_in_to_shards(g["w_main"], g["w_f"])
        theirs = _sibling_swap([slots], name="scatter_w_in_swap")[0]
        chip_sums = _pair_sum(slots, theirs, name="scatter_w_in_pair_sum", tn=W_IN_SHARD)
        pending["w_in"] = _exchange_start([chip_sums], True, name="scatter_w_in_start", chips_only=True)
        return pending["w_in"][1][0, 0]

    sq_err, grad_x, g = _local_step(
        x[0], loss_target[0], w_main, w_f, b_forget, conv_b, g_pre_mix + late_tok[0, 0], g_post_mix, g_pre_ffn,
        g_post_ffn, late_weights, ffn_grads_ready, proj_grads_ready, mixer_grads_ready)

    tiles = dict(w_in=256, w_o_fox=512, w_o_dil=512, w_out=128, w_up=256, w_down=176, conv_w=3)
    adam = lambda n, p: _adamw(p, w[n][0], m[n][0], v[n][0], name=f"adamw_{n}", tm=tiles[n])
    res = {}
    for key, group in (("ffn", ("w_down", "w_up", "conv_w")), ("proj", ("w_o_fox", "w_o_dil", "w_out"))):
        landed = _exchange_wait(pending[key][0], grad_x, name=f"scatter_{key}_wait")
        res.update({n: adam(n, p) for n, p in zip(group, landed)})
    done = res["w_up"][3]
    res["w_in"] = adam("w_in", _exchange_wait(pending["w_in"][0], done, name="scatter_w_in_wait")[0])
    small_parts = _exchange([g[n] for n in SMALL] + [sq_err], False, name="gather_small_grads")
    small, loss = _adamw_small(small_parts[:-1], *[[t[n] for n in SMALL] for t in (w, m, v)], small_parts[-1])
    small = dict(zip(SMALL, small))
    out = [[(res[n][k][None] if n in sharded else small[n][k]) for n in names] for k in range(4)]
    return (loss, grad_x[None], *out[0], *out[1], *out[2], *out[3])
```

```python
import functools
import math

import jax
import jax.numpy as jnp
import numpy as np
from jax import lax
from jax.experimental import pallas as pl
from jax.experimental.pallas import tpu as pltpu

F32 = jnp.float32
BF16 = jnp.bfloat16

SEQ = 4096
D_MODEL = 1024
N_HEADS = 8
HEAD_DIM = 64
ATT_W = N_HEADS * HEAD_DIM
D_FF = 2816
Z_MAIN = 5120
F_PAD = 128
ROPE_DIM = 16
ROPE_THETA = 500000.0
RMS_EPS = 1e-6
NEG_INF = -1e30
SCALE = 1.0 / math.sqrt(HEAD_DIM)
DIL_PATTERNS = ((128, 1), (512, 4), (2048, 16))
DIL_BLK = 128
DIL_STEP_BLOCKS = 2
N_DEV = 8

ADAM_LR = 0.001
ADAM_B1 = 0.9
ADAM_B2 = 0.999
ADAM_EPS = 1e-08
ADAM_WD = 0.01
ADAM_STEP = 10

LANE = 128
SUBLANE = 8
VMEM_LIMIT = 56 * 1024 * 1024
MESH_ID = pl.DeviceIdType.MESH
ANY = pl.BlockSpec(memory_space=pl.ANY)


def _params(*sem):
    return pltpu.CompilerParams(dimension_semantics=sem, vmem_limit_bytes=VMEM_LIMIT)


def _sds(shape, dtype):
    return jax.ShapeDtypeStruct(shape, dtype)


def _matmul(a, b, *, ta=False, tb=False, out_dtype, tm, tn, tk, name, b_k_off=0):
    if ta:
        kk, m = a.shape
    else:
        m, kk = a.shape
    n = b.shape[0] if tb else b.shape[1]
    tm, tn, tk = min(tm, m), min(tn, n), min(tk, kk)
    assert (b.shape[1] if tb else b.shape[0]) >= b_k_off * tk + kk
    assert m % tm == 0 and n % tn == 0 and kk % tk == 0, (name, m, n, kk, tm, tn, tk)
    nk = kk // tk
    dims = (((0 if ta else 1,), (1 if tb else 0,)), ((), ()))

    def body(a_ref, b_ref, o_ref, *scratch):
        p = lax.dot_general(a_ref[...].astype(BF16), b_ref[...].astype(BF16), dims,
                            preferred_element_type=F32)
        if nk == 1:
            o_ref[...] = p.astype(o_ref.dtype)
        else:
            acc = scratch[0]
            k = pl.program_id(2)

            @pl.when(k == 0)
            def _():
                acc[...] = p

            @pl.when(k > 0)
            def _():
                acc[...] += p

            @pl.when(k == nk - 1)
            def _():
                o_ref[...] = acc[...].astype(o_ref.dtype)

    a_spec = (pl.BlockSpec((tk, tm), lambda i, j, k: (k, i)) if ta
              else pl.BlockSpec((tm, tk), lambda i, j, k: (i, k)))
    b_spec = (pl.BlockSpec((tn, tk), lambda i, j, k: (j, k + b_k_off)) if tb
              else pl.BlockSpec((tk, tn), lambda i, j, k: (k + b_k_off, j)))
    return pl.pallas_call(
        body, name=name, grid=(m // tm, n // tn, nk),
        in_specs=[a_spec, b_spec],
        out_specs=pl.BlockSpec((tm, tn), lambda i, j, k: (i, j)),
        out_shape=_sds((m, n), out_dtype),
        scratch_shapes=[pltpu.VMEM((tm, tn), F32)] if nk > 1 else [],
        compiler_params=_params("parallel", "parallel", "arbitrary"),
    )(a, b)


def _rms_fwd(x, g, *, name, tm=512):
    def body(x_ref, g_ref, h_ref):
        xv = x_ref[...]
        r = lax.rsqrt(jnp.mean(xv * xv, axis=-1, keepdims=True) + RMS_EPS)
        h_ref[...] = (xv * r * g_ref[...]).astype(h_ref.dtype)

    return pl.pallas_call(
        body, name=name, grid=(SEQ // tm,),
        in_specs=[pl.BlockSpec((tm, D_MODEL), lambda i: (i, 0)), pl.BlockSpec((1, D_MODEL), lambda i: (0, 0))],
        out_specs=pl.BlockSpec((tm, D_MODEL), lambda i: (i, 0)),
        out_shape=_sds((SEQ, D_MODEL), BF16),
        compiler_params=_params("parallel"),
    )(x, g)


def _rms_bwd(dh_parts, xin, g, dres, *, out_dtype, name, tm=512):
    n_parts = len(dh_parts)
    has_res = dres is not None

    def body(*refs):
        parts = refs[:n_parts]
        x_ref, g_ref = refs[n_parts], refs[n_parts + 1]
        res_ref = refs[n_parts + 2] if has_res else None
        o_ref, gg_ref = refs[-2], refs[-1]
        dh = parts[0][...].astype(F32)
        for p in parts[1:]:
            dh = dh + p[...].astype(F32)
        xv = x_ref[...]
        r = lax.rsqrt(jnp.mean(xv * xv, axis=-1, keepdims=True) + RMS_EPS)
        xn = xv * r

        @pl.when(pl.program_id(0) == 0)
        def _():
            gg_ref[...] = jnp.zeros_like(gg_ref)

        gg_ref[...] += jnp.sum(dh * xn, axis=0, keepdims=True)
        dxn = dh * g_ref[...]
        dx = r * (dxn - xn * jnp.mean(dxn * xn, axis=-1, keepdims=True))
        if has_res:
            dx = dx + res_ref[...]
        o_ref[...] = dx.astype(o_ref.dtype)

    row = pl.BlockSpec((tm, D_MODEL), lambda i: (i, 0))
    vec = pl.BlockSpec((1, D_MODEL), lambda i: (0, 0))
    args = list(dh_parts) + [xin, g] + ([dres] if has_res else [])
    return pl.pallas_call(
        body, name=name, grid=(SEQ // tm,),
        in_specs=[row] * n_parts + [row, vec] + ([row] if has_res else []),
        out_specs=[row, vec],
        out_shape=[_sds((SEQ, D_MODEL), out_dtype), _sds((1, D_MODEL), F32)],
        compiler_params=_params("arbitrary"),
    )(*args)


def _rms_pair_bwd(dh_parts, x2, g_pre, dres, y1, g_post, *, tm=512):
    n_parts = len(dh_parts)

    def norm_bwd(dh, xin, g_ref, gg_ref):
        r = lax.rsqrt(jnp.mean(xin * xin, axis=-1, keepdims=True) + RMS_EPS)
        xn = xin * r
        gg_ref[...] += jnp.sum(dh * xn, axis=0, keepdims=True)
        dxn = dh * g_ref[...]
        return r * (dxn - xn * jnp.mean(dxn * xn, axis=-1, keepdims=True))

    def body(*refs):
        parts = refs[:n_parts]
        x2_ref, gpre_ref, res_ref, y1_ref, gpost_ref, dx2_ref, dy1_ref, ggpre_ref, ggpost_ref = refs[n_parts:]

        @pl.when(pl.program_id(0) == 0)
        def _():
            ggpre_ref[...] = jnp.zeros_like(ggpre_ref)
            ggpost_ref[...] = jnp.zeros_like(ggpost_ref)

        dh = parts[0][...].astype(F32)
        for p in parts[1:]:
            dh = dh + p[...].astype(F32)
        dx2 = res_ref[...] + norm_bwd(dh, x2_ref[...], gpre_ref, ggpre_ref)
        dx2_ref[...] = dx2
        dy1_ref[...] = norm_bwd(dx2, y1_ref[...], gpost_ref, ggpost_ref).astype(dy1_ref.dtype)

    row = pl.BlockSpec((tm, D_MODEL), lambda i: (i, 0))
    vec = pl.BlockSpec((1, D_MODEL), lambda i: (0, 0))
    return pl.pallas_call(
        body, name="rms_pair_bwd", grid=(SEQ // tm,),
        in_specs=[row] * n_parts + [row, vec, row, row, vec],
        out_specs=[row, row, vec, vec],
        out_shape=[_sds((SEQ, D_MODEL), F32), _sds((SEQ, D_MODEL), BF16), _sds((1, D_MODEL), F32),
                   _sds((1, D_MODEL), F32)],
        compiler_params=_params("arbitrary"),
    )(*dh_parts, x2, g_pre, dres, y1, g_post)


SCAN_BLK = 512


def _split_dot(v, tri):
    hi = v.astype(BF16)
    r1 = v - hi.astype(F32)
    mid = r1.astype(BF16)
    lo = (r1 - mid.astype(F32)).astype(BF16)
    dot = functools.partial(jnp.dot, preferred_element_type=F32)
    return dot(hi, tri) + dot(mid, tri) + dot(lo, tri)


def _fox_prep(fa_t, b_col):
    nblk = SEQ // SCAN_BLK

    def body(fa_ref, b_ref, f_ref, sg_ref):
        row = lax.broadcasted_iota(jnp.int32, (SCAN_BLK, SCAN_BLK), 0)
        col = lax.broadcasted_iota(jnp.int32, (SCAN_BLK, SCAN_BLK), 1)
        upper = (row <= col).astype(BF16)
        carry = jnp.zeros((N_HEADS, 1), F32)
        for blk in range(nblk):
            sl = pl.ds(blk * SCAN_BLK, SCAN_BLK)
            xx = fa_ref[:, sl] + b_ref[...]
            e = jnp.exp(-jnp.abs(xx))
            logf = jnp.minimum(xx, 0.0) - jnp.log(1.0 + e)
            sg_ref[:, sl] = jnp.where(xx >= 0.0, e, 1.0) / (1.0 + e)
            c = _split_dot(logf, upper) + carry
            f_ref[:, sl] = c
            carry = c[:, SCAN_BLK - 1:SCAN_BLK]

    return pl.pallas_call(
        body, name="fox_prep",
        out_shape=[_sds((N_HEADS, SEQ), F32), _sds((N_HEADS, SEQ), F32)],
        compiler_params=pltpu.CompilerParams(vmem_limit_bytes=VMEM_LIMIT),
    )(fa_t, b_col)


def _fox_post_bwd(df_t, sg_t):
    nblk = SEQ // SCAN_BLK

    def body(df_ref, sg_ref, dfa_ref, gb_ref):
        row = lax.broadcasted_iota(jnp.int32, (SCAN_BLK, SCAN_BLK), 0)
        col = lax.broadcasted_iota(jnp.int32, (SCAN_BLK, SCAN_BLK), 1)
        lower = (row >= col).astype(BF16)
        carry = jnp.zeros((N_HEADS, 1), F32)
        gb = jnp.zeros((N_HEADS, 1), F32)
        for blk in reversed(range(nblk)):
            sl = pl.ds(blk * SCAN_BLK, SCAN_BLK)
            c = _split_dot(df_ref[:, sl], lower) + carry
            carry = c[:, 0:1]
            dfa = c * sg_ref[:, sl]
            dfa_ref[:, sl] = dfa
            gb = gb + jnp.sum(dfa, axis=1, keepdims=True)
        gb_ref[...] = gb

    return pl.pallas_call(
        body, name="fox_post_bwd",
        out_shape=[_sds((N_HEADS, SEQ), F32), _sds((N_HEADS, 1), F32)],
        compiler_params=pltpu.CompilerParams(vmem_limit_bytes=VMEM_LIMIT),
    )(df_t, sg_t)


FOX_T = 512
NT_DIMS = (((1,), (1,)), ((), ()))
TN_DIMS = (((0,), (0,)), ((), ()))


def _head(ref_or_val, h):
    return ref_or_val[:, h * HEAD_DIM:(h + 1) * HEAD_DIM]


def _split3(v):
    hi = v.astype(BF16).astype(F32)
    r1 = v - hi
    mid = r1.astype(BF16).astype(F32)
    return hi, mid, (r1 - mid).astype(BF16).astype(F32)


ONE_LANE = 3 * N_HEADS


def _pack_terms(v, with_one):
    hi, mid, lo = _split3(v)
    t = hi + pltpu.roll(mid, N_HEADS, 1) + pltpu.roll(lo, 2 * N_HEADS, 1)
    if with_one:
        t = t + (lax.broadcasted_iota(jnp.int32, v.shape, 1) == ONE_LANE).astype(F32)
    return t.astype(BF16)


def _aux_matrices():
    to_q = np.zeros((LANE, N_HEADS * 2 * HEAD_DIM), np.float32)
    to_k = np.zeros_like(to_q)
    for h in range(N_HEADS):
        base = h * 2 * HEAD_DIM + HEAD_DIM
        for s in range(3):
            to_q[s * N_HEADS + h, base + s] = 1.0
            to_q[ONE_LANE, base + 3 + s] = 1.0
            to_k[ONE_LANE, base + s] = 1.0
            to_k[s * N_HEADS + h, base + 3 + s] = -1.0
    return jnp.asarray(to_q, BF16), jnp.asarray(to_k, BF16)


def _head_sums():
    total = np.zeros((N_HEADS * HEAD_DIM, LANE), np.float32)
    first = np.zeros_like(total)
    for h in range(N_HEADS):
        total[h * HEAD_DIM:(h + 1) * HEAD_DIM, h] = 1.0
        first[h * HEAD_DIM, h] = 1.0
    return jnp.asarray(total, BF16), jnp.asarray(first, BF16)


SLOT = 2 * HEAD_DIM
N_SPLIT = 3
FOX_FWD_HEADS = 8
FOX_BWD_HEADS = 4


def _slot(ref, h):
    return ref[:, h * SLOT:(h + 1) * SLOT]


def _fox_pack_fwd(zm, f_cols, *, tm=512):
    def body(q_ref, k_ref, v_ref, f_ref, tq_ref, tk_ref, qs_ref, ks_ref, vs_ref):
        ones = jnp.ones((tm, HEAD_DIM), BF16)
        terms = _pack_terms(f_ref[...], True)
        q_aux = jnp.dot(terms, tq_ref[...], preferred_element_type=F32).astype(BF16)
        k_aux = jnp.dot(terms, tk_ref[...], preferred_element_type=F32).astype(BF16)
        for h in range(N_HEADS):
            aux = slice(h * SLOT + HEAD_DIM, (h + 1) * SLOT)
            qs_ref[:, h * SLOT:(h + 1) * SLOT] = jnp.concatenate(
                [(_head(q_ref, h).astype(F32) * SCALE).astype(BF16), q_aux[:, aux]], axis=1)
            ks_ref[:, h * SLOT:(h + 1) * SLOT] = jnp.concatenate([_head(k_ref, h), k_aux[:, aux]], axis=1)
            vs_ref[:, h * SLOT:(h + 1) * SLOT] = jnp.concatenate([_head(v_ref, h), ones], axis=1)

    col = lambda b: pl.BlockSpec((tm, ATT_W), lambda i: (i, b))
    wide = pl.BlockSpec((tm, N_HEADS * SLOT), lambda i: (i, 0))
    const = pl.BlockSpec((LANE, N_HEADS * SLOT), lambda i: (0, 0))
    return pl.pallas_call(
        body, name="fox_pack_fwd", grid=(SEQ // tm,),
        in_specs=[col(0), col(1), col(2), pl.BlockSpec((tm, LANE), lambda i: (i, 0)), const, const],
        out_specs=[wide] * 3, out_shape=[_sds((SEQ, N_HEADS * SLOT), BF16)] * 3,
        compiler_params=_params("parallel"),
    )(zm, zm, zm, f_cols, *_aux_matrices())


def _fox_pack_bwd(zm, f_cols, lse, o, do, *, tm=512):
    def body(q_ref, f_ref, lse_ref, o_ref, do_ref, tq_ref, total_ref, first_ref, qs_ref, ds_ref):
        delta = _split_dot(o_ref[...].astype(F32) * do_ref[...].astype(F32), total_ref[...])
        lse_h = _split_dot(lse_ref[...], first_ref[...])
        q_aux = jnp.dot(_pack_terms(f_ref[...] - lse_h, True), tq_ref[...], preferred_element_type=F32).astype(BF16)
        d_aux = jnp.dot(_pack_terms(-delta, False), tq_ref[...], preferred_element_type=F32).astype(BF16)
        for h in range(N_HEADS):
            aux = slice(h * SLOT + HEAD_DIM, (h + 1) * SLOT)
            qs_ref[:, h * SLOT:(h + 1) * SLOT] = jnp.concatenate(
                [(_head(q_ref, h).astype(F32) * SCALE).astype(BF16), q_aux[:, aux]], axis=1)
            ds_ref[:, h * SLOT:(h + 1) * SLOT] = jnp.concatenate([_head(do_ref, h), d_aux[:, aux]], axis=1)

    row = pl.BlockSpec((tm, ATT_W), lambda i: (i, 0))
    wide = pl.BlockSpec((tm, N_HEADS * SLOT), lambda i: (i, 0))
    const = lambda r, c: pl.BlockSpec((r, c), lambda i: (0, 0))
    return pl.pallas_call(
        body, name="fox_pack_bwd", grid=(SEQ // tm,),
        in_specs=[row, pl.BlockSpec((tm, LANE), lambda i: (i, 0)), row, row, row,
                  const(LANE, N_HEADS * SLOT), const(ATT_W, LANE), const(ATT_W, LANE)],
        out_specs=[wide] * 2, out_shape=[_sds((SEQ, N_HEADS * SLOT), BF16)] * 2,
        compiler_params=_params("parallel"),
    )(zm, f_cols, lse, o, do, _aux_matrices()[0], *_head_sums())


def _causal_pairs(key_major):
    nb = SEQ // FOX_T
    if key_major:
        pairs = [(i, j) for j in range(nb) for i in range(j, nb)]
    else:
        pairs = [(i, j) for i in range(nb) for j in range(i + 1)]
    return (jnp.array([p[0] for p in pairs], jnp.int32), jnp.array([p[1] for p in pairs], jnp.int32), len(pairs))


FOX_HALF = FOX_T // 2
FOX_FULL = ((slice(0, FOX_T), slice(0, FOX_T), None),)
FOX_DIAG = ((slice(0, FOX_HALF), slice(0, FOX_HALF), 0), (slice(FOX_HALF, FOX_T), slice(0, FOX_T), FOX_HALF))


def _causal_piece_mask(q_rows, k_rows, offset):
    shape = (q_rows.stop - q_rows.start, k_rows.stop - k_rows.start)
    row = lax.broadcasted_iota(jnp.int32, shape, 0)
    col = lax.broadcasted_iota(jnp.int32, shape, 1)
    return col <= row + offset


def _fox_fwd(q_slots, k_slots, v_slots):
    i_tab, j_tab, n_pairs = _causal_pairs(False)

    def body(i_tab, j_tab, q_ref, k_ref, v_ref, o_ref, lse_ref, m_s, acc_s):
        t = pl.program_id(1)
        i, j = i_tab[t], j_tab[t]

        @pl.when(j == 0)
        def _():
            m_s[...] = jnp.full_like(m_s, NEG_INF)
            acc_s[...] = jnp.zeros_like(acc_s)

        def step(pieces):
            jobs = [(h, piece) for h in range(FOX_FWD_HEADS) for piece in pieces]
            lanes = lambda h: slice(h * SLOT, (h + 1) * SLOT)
            scores = [lax.dot_general(q_ref[qr, lanes(h)], k_ref[kr, lanes(h)], NT_DIMS, preferred_element_type=F32)
                      for h, (qr, kr, _) in jobs]
            probs, alphas = [], []
            for idx, (h, (qr, kr, offset)) in enumerate(jobs):
                s = scores[idx]
                if offset is not None:
                    s = jnp.where(_causal_piece_mask(qr, kr, offset), s, NEG_INF)
                m_prev = m_s[h, qr, :]
                m_new = jnp.maximum(m_prev, jnp.max(s, axis=-1, keepdims=True))
                probs.append(jnp.exp(s - jnp.tile(m_new, (1, s.shape[1] // LANE))).astype(BF16))
                alphas.append(jnp.exp(m_prev - m_new))
                m_s[h, qr, :] = m_new
            for idx, (h, (qr, kr, _)) in enumerate(jobs):
                acc_s[h, qr, :] = alphas[idx] * acc_s[h, qr, :] + jnp.dot(
                    probs[idx], v_ref[kr, lanes(h)], preferred_element_type=F32)

        @pl.when(j < i)
        def _():
            step(FOX_FULL)

        @pl.when(j == i)
        def _():
            step(FOX_DIAG)
            outs, lses = [], []
            for h in range(FOX_FWD_HEADS):
                acc = acc_s[h]
                l = acc[:, HEAD_DIM:]
                outs.append(acc[:, :HEAD_DIM] / l)
                lses.append(m_s[h][:, :HEAD_DIM] + jnp.log(l))
            o_ref[...] = jnp.concatenate(outs, axis=1).astype(o_ref.dtype)
            lse_ref[...] = jnp.concatenate(lses, axis=1)

    qspec = pl.BlockSpec((FOX_T, FOX_FWD_HEADS * SLOT), lambda p, t, it, jt: (it[t], p))
    kspec = pl.BlockSpec((FOX_T, FOX_FWD_HEADS * SLOT), lambda p, t, it, jt: (jt[t], p))
    ospec = pl.BlockSpec((FOX_T, FOX_FWD_HEADS * HEAD_DIM), lambda p, t, it, jt: (it[t], p))
    return pl.pallas_call(
        body, name="fox_fwd",
        grid_spec=pltpu.PrefetchScalarGridSpec(
            num_scalar_prefetch=2, grid=(N_HEADS // FOX_FWD_HEADS, n_pairs),
            in_specs=[qspec, kspec, kspec], out_specs=[ospec, ospec],
            scratch_shapes=[pltpu.VMEM((FOX_FWD_HEADS, FOX_T, LANE), F32),
                            pltpu.VMEM((FOX_FWD_HEADS, FOX_T, SLOT), F32)]),
        out_shape=[_sds((SEQ, ATT_W), BF16), _sds((SEQ, ATT_W), F32)],
        compiler_params=_params("parallel", "arbitrary"),
    )(i_tab, j_tab, q_slots, k_slots, v_slots)


def _fox_bwd(q_slots, k_slots, v_slots, do_slots):
    i_tab, j_tab, n_pairs = _causal_pairs(True)

    def body(i_tab, j_tab, q_ref, k_ref, v_ref, do_ref, dq_ref, dk_ref, dv_ref):
        t = pl.program_id(1)
        i, j = i_tab[t], j_tab[t]

        @pl.when(t == 0)
        def _():
            dq_ref[...] = jnp.zeros_like(dq_ref)

        @pl.when(i == j)
        def _():
            dk_ref[...] = jnp.zeros_like(dk_ref)
            dv_ref[...] = jnp.zeros_like(dv_ref)

        def step(pieces):
            jobs = [(h, piece) for h in range(FOX_BWD_HEADS) for piece in pieces]
            lanes = lambda h: slice(h * SLOT, (h + 1) * SLOT)
            scores = [lax.dot_general(q_ref[qr, lanes(h)], k_ref[kr, lanes(h)], NT_DIMS, preferred_element_type=F32)
                      for h, (qr, kr, _) in jobs]
            dps = [lax.dot_general(do_ref[qr, lanes(h)], v_ref[kr, lanes(h)], NT_DIMS, preferred_element_type=F32)
                   for h, (qr, kr, _) in jobs]
            ps, dss = [], []
            for idx, (h, (qr, kr, offset)) in enumerate(jobs):
                p = jnp.exp(scores[idx])
                if offset is not None:
                    p = jnp.where(_causal_piece_mask(qr, kr, offset), p, 0.0)
                ps.append(p.astype(BF16))
                dss.append((p * dps[idx]).astype(BF16))
            for idx, (h, (qr, kr, _)) in enumerate(jobs):
                rows = pl.ds(pl.multiple_of(i * FOX_T + qr.start, FOX_HALF), qr.stop - qr.start)
                dv_ref[kr, lanes(h)] += lax.dot_general(ps[idx], do_ref[qr, lanes(h)], TN_DIMS,
                                                        preferred_element_type=F32)
                dk_ref[kr, lanes(h)] += lax.dot_general(dss[idx], q_ref[qr, lanes(h)], TN_DIMS,
                                                        preferred_element_type=F32)
                dq_ref[rows, lanes(h)] += jnp.dot(dss[idx], k_ref[kr, lanes(h)], preferred_element_type=F32)

        @pl.when(i > j)
        def _():
            step(FOX_FULL)

        @pl.when(i == j)
        def _():
            step(FOX_DIAG)

    qspec = pl.BlockSpec((FOX_T, FOX_BWD_HEADS * SLOT), lambda p, t, it, jt: (it[t], p))
    kspec = pl.BlockSpec((FOX_T, FOX_BWD_HEADS * SLOT), lambda p, t, it, jt: (jt[t], p))
    return pl.pallas_call(
        body, name="fox_bwd",
        grid_spec=pltpu.PrefetchScalarGridSpec(
            num_scalar_prefetch=2, grid=(N_HEADS // FOX_BWD_HEADS, n_pairs),
            in_specs=[qspec, kspec, kspec, qspec],
            out_specs=[pl.BlockSpec((SEQ, FOX_BWD_HEADS * SLOT), lambda p, t, it, jt: (0, p)), kspec, kspec]),
        out_shape=[_sds((SEQ, N_HEADS * SLOT), F32)] * 3,
        compiler_params=_params("arbitrary", "arbitrary"),
    )(i_tab, j_tab, q_slots, k_slots, v_slots, do_slots)


def _fox_unpack(dq_slots, dk_slots, dv_slots, dz, *, tm=512):
    def body(dq_ref, dk_ref, dv_ref, dz_in, o_ref, df_ref):
        lane = lax.broadcasted_iota(jnp.int32, (tm, LANE), 1)
        df = jnp.zeros((tm, LANE), F32)
        for h in range(N_HEADS):
            lo = h * SLOT
            for part, (ref, mult) in enumerate(((dq_ref, SCALE), (dk_ref, 1.0), (dv_ref, 1.0))):
                o_ref[:, part * ATT_W + h * HEAD_DIM:part * ATT_W + (h + 1) * HEAD_DIM] = (
                    ref[:, lo:lo + HEAD_DIM] * mult).astype(o_ref.dtype)
            rows = dq_ref[:, lo + HEAD_DIM:lo + HEAD_DIM + 1]
            cols = dk_ref[:, lo + HEAD_DIM + N_SPLIT:lo + HEAD_DIM + N_SPLIT + 1]
            df = jnp.where(lane == h, rows - cols, df)
        df_ref[...] = df

    wide = pl.BlockSpec((tm, N_HEADS * SLOT), lambda i: (i, 0))
    return pl.pallas_call(
        body, name="fox_unpack", grid=(SEQ // tm,), in_specs=[wide] * 3 + [ANY],
        out_specs=[pl.BlockSpec((tm, 3 * ATT_W), lambda i: (i, 0)), pl.BlockSpec((tm, LANE), lambda i: (i, 0))],
        out_shape=[_sds((SEQ, Z_MAIN), BF16), _sds((SEQ, LANE), F32)],
        input_output_aliases={3: 0},
        compiler_params=_params("parallel"),
    )(dq_slots, dk_slots, dv_slots, dz)


def _dil_bwd_prep(o, do, lse, *, tm=512):
    dilations = [d for _, d in DIL_PATTERNS]
    o_chunks = ATT_W // LANE

    def body(o_ref, do_ref, lse_ref, *rest):
        outs, (do_scr, lse_scr, dl_scr) = rest[:-3], rest[-3:]
        dov = do_ref[...].astype(F32)
        prod = o_ref[...].astype(F32) * dov
        lane = lax.broadcasted_iota(jnp.int32, (tm, LANE), 1)
        delta = jnp.zeros((tm, LANE), F32)
        for h in range(N_HEADS):
            delta = jnp.where(lane == h, jnp.sum(_head(prod, h), axis=1, keepdims=True), delta)
        for ch in range(o_chunks):
            do_scr[ch] = dov[:, ch * LANE:(ch + 1) * LANE]
        lse_scr[0] = lse_ref[...]
        dl_scr[0] = delta
        for k, d in enumerate(dilations):
            for scr, out in zip((do_scr, lse_scr, dl_scr), outs[3 * k:3 * k + 3]):
                _slabs_from_rows(scr, out, d)

    row = pl.BlockSpec((tm, ATT_W), lambda i: (i, 0))
    view = lambda d, w: pl.BlockSpec((tm // d, d * w), lambda i: (i, 0))
    outs = pl.pallas_call(
        body, name="dil_bwd_prep", grid=(SEQ // tm,),
        in_specs=[row, row, pl.BlockSpec((tm, LANE), lambda i: (i, 0))],
        out_specs=[view(d, w) for d in dilations for w in (ATT_W, LANE, LANE)],
        out_shape=[_sds((SEQ // d, d * w), t) for d in dilations for w, t in ((ATT_W, BF16), (LANE, F32), (LANE, F32))],
        scratch_shapes=[pltpu.VMEM((o_chunks, tm, LANE), F32), pltpu.VMEM((1, tm, LANE), F32),
                        pltpu.VMEM((1, tm, LANE), F32)],
        compiler_params=_params("parallel"),
    )(o, do, lse)
    return [outs[3 * k:3 * k + 3] for k in range(len(dilations))]


def _rope_tables():
    half = ROPE_DIM // 2
    inv_freq = np.float32(ROPE_THETA) ** (-np.arange(half, dtype=np.float32) * np.float32(2.0) / np.float32(ROPE_DIM))
    ang = np.arange(SEQ, dtype=np.float32)[:, None] * inv_freq.astype(np.float32)[None, :]
    cos, sin = jnp.asarray(np.cos(ang).astype(np.float32)), jnp.asarray(np.sin(ang).astype(np.float32))
    ones = jnp.ones((SEQ, HEAD_DIM - ROPE_DIM), F32)
    zeros = jnp.zeros((SEQ, HEAD_DIM - ROPE_DIM), F32)
    zh = jnp.zeros((SEQ, half), F32)
    c_tab = jnp.concatenate([cos, cos, ones], axis=1)
    a_tab = jnp.concatenate([-sin, zh, zeros], axis=1)
    b_tab = jnp.concatenate([zh, sin, zeros], axis=1)
    two = lambda t: jnp.concatenate([t, t], axis=1)
    return two(c_tab), two(a_tab), two(b_tab)


def _rotate(x, c_tab, a_tab, b_tab):
    return x * c_tab + pltpu.roll(x, LANE - ROPE_DIM // 2, 1) * a_tab + pltpu.roll(x, ROPE_DIM // 2, 1) * b_tab


def _rope_fwd(zm, tabs, *, tm=512):
    width = 3 * ATT_W
    dilations = [d for _, d in DIL_PATTERNS]

    def body(q_ref, k_ref, v_ref, c_ref, a_ref, b_ref, *rest):
        outs, scr = rest[:-1], rest[-1]
        per_part = ATT_W // LANE
        for part, (x_ref, mult) in enumerate(((q_ref, SCALE), (k_ref, 1.0))):
            for cc in range(per_part):
                sl = slice(cc * LANE, (cc + 1) * LANE)
                scr[part * per_part + cc] = _rotate(x_ref[:, sl].astype(F32), c_ref[...], a_ref[...], b_ref[...]) * mult
        for cc in range(per_part):
            scr[2 * per_part + cc] = v_ref[:, cc * LANE:(cc + 1) * LANE].astype(F32)
        for o_ref, d in zip(outs, dilations):
            for r in range(d):
                for ch in range(width // LANE):
                    o_ref[:, r * width + ch * LANE:r * width + (ch + 1) * LANE] = (
                        scr.at[ch][pl.ds(r, tm // d, stride=d), :].astype(o_ref.dtype))

    tab = pl.BlockSpec((tm, LANE), lambda i: (i, 0))
    col = lambda b: pl.BlockSpec((tm, ATT_W), lambda i: (i, b))
    return pl.pallas_call(
        body, name="rope_fwd", grid=(SEQ // tm,),
        in_specs=[col(3), col(4), col(5), tab, tab, tab],
        out_specs=[pl.BlockSpec((tm // d, d * width), lambda i: (i, 0)) for d in dilations],
        out_shape=[_sds((SEQ // d, d * width), BF16) for d in dilations],
        scratch_shapes=[pltpu.VMEM((width // LANE, tm, LANE), F32)],
        compiler_params=_params("parallel"),
    )(zm, zm, zm, *tabs)


def _dil_grad_combine(dqs, dks, dvs, tabs, dz, *, tm=256):
    dilations = [d for _, d in DIL_PATTERNS]
    chunks = ATT_W // LANE

    def body(*refs):
        groups = (refs[0:3], refs[3:6], refs[6:9])
        c_ref, a_ref, b_ref, _, o_ref, scr = refs[9:]

        def total(part, cc):
            acc = None
            for g, (ref, d) in enumerate(zip(groups[part], dilations)):
                term = ref[:, cc * LANE:(cc + 1) * LANE].astype(F32) if d == 1 else scr[part, g, cc]
                acc = term if acc is None else acc + term
            return acc

        for part in range(3):
            for g, (ref, d) in enumerate(zip(groups[part], dilations)):
                if d > 1:
                    _rows_from_slabs(ref, scr.at[part, g], d)
        for cc in range(chunks):
            for part in range(2):
                o_ref[:, part * ATT_W + cc * LANE:part * ATT_W + (cc + 1) * LANE] = _rotate(
                    total(part, cc), c_ref[...], -a_ref[...], -b_ref[...]).astype(o_ref.dtype)
            o_ref[:, 2 * ATT_W + cc * LANE:2 * ATT_W + (cc + 1) * LANE] = total(2, cc).astype(o_ref.dtype)

    view = lambda d: pl.BlockSpec((tm // d, d * ATT_W), lambda i: (i, 0))
    tab = pl.BlockSpec((tm, LANE), lambda i: (i, 0))
    return pl.pallas_call(
        body, name="dil_grad_combine", grid=(SEQ // tm,),
        in_specs=[view(d) for d in dilations] * 3 + [tab] * 3 + [ANY],
        out_specs=pl.BlockSpec((tm, 3 * ATT_W), lambda i: (i, 1)),
        out_shape=_sds((SEQ, Z_MAIN), BF16),
        input_output_aliases={12: 0},
        scratch_shapes=[pltpu.VMEM((3, len(dilations), chunks, tm, LANE), F32)],
        compiler_params=_params("parallel"),
    )(*dqs, *dks, *dvs, *tabs, dz)


def _dil_valid(n):
    qi = lax.broadcasted_iota(jnp.int32, (DIL_BLK, 2 * DIL_BLK), 0)
    ki = lax.broadcasted_iota(jnp.int32, (DIL_BLK, 2 * DIL_BLK), 1)
    dist = qi + DIL_BLK - ki
    return (dist >= 0) & (dist <= DIL_BLK) & ((n > 0) | (ki >= DIL_BLK))


def _dil_fwd(qkv_v, d):
    length = SEQ // d
    nb = length // DIL_BLK
    nsub = min(DIL_STEP_BLOCKS, nb)

    def body(q_ref, kp_ref, kc_ref, vp_ref, vc_ref, o_ref, lse_ref):
        m_step = pl.program_id(1)
        lane = lax.broadcasted_iota(jnp.int32, (DIL_BLK, LANE), 1)
        jobs = [(sub, h) for sub in range(nsub) for h in range(N_HEADS)]
        rows = lambda sub: slice(sub * DIL_BLK, (sub + 1) * DIL_BLK)
        cols = lambda h: slice(h * HEAD_DIM, (h + 1) * HEAD_DIM)

        def keys(prev_ref, cur_ref, sub, h):
            before = prev_ref[:, cols(h)] if sub == 0 else cur_ref[rows(sub - 1), cols(h)]
            return jnp.concatenate([before, cur_ref[rows(sub), cols(h)]], axis=0)

        scores = [lax.dot_general(q_ref[rows(sub), cols(h)], keys(kp_ref, kc_ref, sub, h), NT_DIMS,
                                  preferred_element_type=F32) for sub, h in jobs]
        ok = [_dil_valid(m_step)] + [_dil_valid(1)] * (nsub - 1)
        probs, inv_l, lse_all = [], [], [jnp.zeros((DIL_BLK, LANE), F32)] * nsub
        for idx, (sub, h) in enumerate(jobs):
            s = jnp.where(ok[sub], scores[idx], NEG_INF)
            m = jnp.max(s, axis=-1, keepdims=True)
            p = jnp.exp(s - m)
            l = jnp.sum(p, axis=-1, keepdims=True)
            probs.append(p.astype(BF16))
            inv_l.append(1.0 / l)
            lse_all[sub] = jnp.where(lane == h, m + jnp.log(l), lse_all[sub])
        outs = [jnp.dot(probs[idx], keys(vp_ref, vc_ref, sub, h), preferred_element_type=F32) * inv_l[idx]
                for idx, (sub, h) in enumerate(jobs)]
        for sub in range(nsub):
            o_ref[rows(sub), :] = jnp.concatenate(outs[sub * N_HEADS:(sub + 1) * N_HEADS], axis=1).astype(o_ref.dtype)
            lse_ref[rows(sub), :] = lse_all[sub]

    pair = lambda f: pl.BlockSpec((nsub * DIL_BLK, ATT_W), f)
    one = lambda f: pl.BlockSpec((DIL_BLK, ATT_W), f)
    before = lambda m: jnp.maximum(nsub * m - 1, 0)
    o, lse = pl.pallas_call(
        body, name=f"dil_fwd_d{d}", grid=(d, nb // nsub),
        in_specs=[pair(lambda r, m: (m, 3 * r)),
                  one(lambda r, m: (before(m), 3 * r + 1)), pair(lambda r, m: (m, 3 * r + 1)),
                  one(lambda r, m: (before(m), 3 * r + 2)), pair(lambda r, m: (m, 3 * r + 2))],
        out_specs=[pair(lambda r, m: (m, r)), pl.BlockSpec((nsub * DIL_BLK, LANE), lambda r, m: (m, r))],
        out_shape=[_sds((length, d * ATT_W), BF16), _sds((length, d * LANE), F32)],
        compiler_params=_params("parallel", "arbitrary"),
    )(qkv_v, qkv_v, qkv_v, qkv_v, qkv_v)
    return o, lse


def _rows_from_slabs(view_ref, scr, d):
    chunks, rows = scr.shape[0], scr.shape[1]
    for r in range(d):
        for ch in range(chunks):
            lo = (r * chunks + ch) * LANE
            scr.at[ch][pl.ds(r, rows // d, stride=d), :] = view_ref[:, lo:lo + LANE].astype(F32)


def _slabs_from_rows(scr, view_ref, d):
    chunks, rows = scr.shape[0], scr.shape[1]
    for r in range(d):
        for ch in range(chunks):
            lo = (r * chunks + ch) * LANE
            view_ref[:, lo:lo + LANE] = scr.at[ch][pl.ds(r, rows // d, stride=d), :].astype(view_ref.dtype)


def _dil_merge(os_, lses, *, tm=512):
    dilations = [d for _, d in DIL_PATTERNS]
    o_chunks = ATT_W // LANE

    def body(o0, o1, o2, l0, l1, l2, y_ref, lse_ref, o_scr, l_scr):
        os_nat, ls = [], []
        for g, (o_ref, l_ref, d) in enumerate(zip((o0, o1, o2), (l0, l1, l2), dilations)):
            if d == 1:
                os_nat.append(o_ref[...].astype(F32))
                ls.append(l_ref[...])
            else:
                _rows_from_slabs(o_ref, o_scr.at[g], d)
                _rows_from_slabs(l_ref, l_scr.at[g], d)
                os_nat.append(jnp.concatenate([o_scr[g, ch] for ch in range(o_chunks)], axis=1))
                ls.append(l_scr[g, 0])
        m = jnp.maximum(jnp.maximum(ls[0], ls[1]), ls[2])
        es = [jnp.exp(l - m) for l in ls]
        tot = es[0] + es[1] + es[2]
        lse_ref[...] = m + jnp.log(tot)
        alphas = [e / tot for e in es]
        outs = []
        for h in range(N_HEADS):
            acc = None
            for g in range(3):
                term = alphas[g][:, h:h + 1] * _head(os_nat[g], h)
                acc = term if acc is None else acc + term
            outs.append(acc)
        y_ref[...] = jnp.concatenate(outs, axis=1).astype(y_ref.dtype)

    row = pl.BlockSpec((tm, ATT_W), lambda i: (i, 0))
    vec = pl.BlockSpec((tm, LANE), lambda i: (i, 0))
    view = lambda d, w: pl.BlockSpec((tm // d, d * w), lambda i: (i, 0))
    return pl.pallas_call(
        body, name="dil_merge", grid=(SEQ // tm,),
        in_specs=[view(d, ATT_W) for d in dilations] + [view(d, LANE) for d in dilations], out_specs=[row, vec],
        out_shape=[_sds((SEQ, ATT_W), BF16), _sds((SEQ, LANE), F32)],
        scratch_shapes=[pltpu.VMEM((3, o_chunks, tm, LANE), F32), pltpu.VMEM((3, 1, tm, LANE), F32)],
        compiler_params=_params("parallel"),
    )(*os_, *lses)


def _dil_bwd(qkv_v, do_v, lse_v, dl_v, d):
    length = SEQ // d
    nb = length // DIL_BLK
    nsub = min(DIL_STEP_BLOCKS, nb)
    n_steps = nb // nsub

    def body(q_ref, kp_ref, kc_ref, vp_ref, vc_ref, lse_ref, dl_ref, do_ref, dq_ref, dk_ref, dv_ref, dk_s, dv_s):
        m_step = pl.program_id(1)

        @pl.when(m_step == 0)
        def _():
            dk_s[...] = jnp.zeros_like(dk_s)
            dv_s[...] = jnp.zeros_like(dv_s)

        jobs = [(sub, h) for sub in range(nsub) for h in range(N_HEADS)]
        rows = lambda sub: slice(sub * DIL_BLK, (sub + 1) * DIL_BLK)
        cols = lambda h: slice(h * HEAD_DIM, (h + 1) * HEAD_DIM)

        def keys(prev_ref, cur_ref, sub, h):
            before = prev_ref[:, cols(h)] if sub == 0 else cur_ref[rows(sub - 1), cols(h)]
            return jnp.concatenate([before, cur_ref[rows(sub), cols(h)]], axis=0)

        kks = [keys(kp_ref, kc_ref, sub, h) for sub, h in jobs]
        scores = [lax.dot_general(q_ref[rows(sub), cols(h)], kks[idx], NT_DIMS, preferred_element_type=F32)
                  for idx, (sub, h) in enumerate(jobs)]
        dps = [lax.dot_general(do_ref[rows(sub), cols(h)], keys(vp_ref, vc_ref, sub, h), NT_DIMS,
                               preferred_element_type=F32) for sub, h in jobs]
        ok = [_dil_valid(m_step)] + [_dil_valid(1)] * (nsub - 1)
        ps, dss = [], []
        for idx, (sub, h) in enumerate(jobs):
            p = jnp.where(ok[sub], jnp.exp(scores[idx] - lse_ref[rows(sub), h:h + 1]), 0.0)
            ps.append(p.astype(BF16))
            dss.append((p * (dps[idx] - dl_ref[rows(sub), h:h + 1])).astype(BF16))
        dqs = [jnp.dot(dss[idx], kks[idx], preferred_element_type=F32) * SCALE for idx in range(len(jobs))]
        dkks = [lax.dot_general(dss[idx], q_ref[rows(sub), cols(h)], TN_DIMS, preferred_element_type=F32)
                for idx, (sub, h) in enumerate(jobs)]
        dvvs = [lax.dot_general(ps[idx], do_ref[rows(sub), cols(h)], TN_DIMS, preferred_element_type=F32)
                for idx, (sub, h) in enumerate(jobs)]
        for sub in range(nsub):
            dq_ref[rows(sub), :] = jnp.concatenate(dqs[sub * N_HEADS:(sub + 1) * N_HEADS], axis=1).astype(dq_ref.dtype)
        base = m_step * (nsub * DIL_BLK)
        blocks = [pl.ds(pl.multiple_of(jnp.maximum(base - DIL_BLK, 0), DIL_BLK), DIL_BLK)]
        blocks += [pl.ds(pl.multiple_of(base + s * DIL_BLK, DIL_BLK), DIL_BLK) for s in range(nsub)]
        for acc, parts in ((dk_s, dkks), (dv_s, dvvs)):
            top = lambda sub: jnp.concatenate([parts[sub * N_HEADS + h][:DIL_BLK] for h in range(N_HEADS)], axis=1)
            bottom = lambda sub: jnp.concatenate([parts[sub * N_HEADS + h][DIL_BLK:] for h in range(N_HEADS)], axis=1)
            acc[blocks[0], :] += top(0)
            for s in range(nsub):
                acc[blocks[s + 1], :] += bottom(s) + top(s + 1) if s + 1 < nsub else bottom(s)

        @pl.when(m_step == n_steps - 1)
        def _():
            dk_ref[...] = dk_s[...].astype(dk_ref.dtype)
            dv_ref[...] = dv_s[...].astype(dv_ref.dtype)

    pair = lambda f: pl.BlockSpec((nsub * DIL_BLK, ATT_W), f)
    one = lambda f: pl.BlockSpec((DIL_BLK, ATT_W), f)
    vec = lambda f: pl.BlockSpec((nsub * DIL_BLK, LANE), f)
    whole = pl.BlockSpec((length, ATT_W), lambda r, m: (0, r))
    before = lambda m: jnp.maximum(nsub * m - 1, 0)
    outs = pl.pallas_call(
        body, name=f"dil_bwd_d{d}", grid=(d, n_steps),
        in_specs=[pair(lambda r, m: (m, 3 * r)),
                  one(lambda r, m: (before(m), 3 * r + 1)), pair(lambda r, m: (m, 3 * r + 1)),
                  one(lambda r, m: (before(m), 3 * r + 2)), pair(lambda r, m: (m, 3 * r + 2)),
                  vec(lambda r, m: (m, r)), vec(lambda r, m: (m, r)), pair(lambda r, m: (m, r))],
        out_specs=[pair(lambda r, m: (m, r)), whole, whole],
        out_shape=[_sds((length, d * ATT_W), BF16)] * 3,
        scratch_shapes=[pltpu.VMEM((length, ATT_W), F32), pltpu.VMEM((length, ATT_W), F32)],
        compiler_params=_params("arbitrary", "arbitrary"),
    )(qkv_v, qkv_v, qkv_v, qkv_v, qkv_v, lse_v, dl_v, do_v)
    return outs


def _sigmoid(x):
    return 1.0 / (1.0 + jnp.exp(-x))


def _mix_fwd(ya, yb, w_oa, w_ob, zm, *, tm=512):
    def body(ya_ref, yb_ref, wa_ref, wb_ref, ga_ref, gb_ref, pa_ref, pb_ref, mix_ref):
        pa = jnp.dot(ya_ref[...], wa_ref[...], preferred_element_type=F32)
        pb = jnp.dot(yb_ref[...], wb_ref[...], preferred_element_type=F32)
        pa_ref[...] = pa.astype(pa_ref.dtype)
        pb_ref[...] = pb.astype(pb_ref.dtype)
        mix_ref[...] = (_sigmoid(ga_ref[...].astype(F32)) * pa + _sigmoid(gb_ref[...].astype(F32)) * pb
                        ).astype(mix_ref.dtype)

    row = pl.BlockSpec((tm, ATT_W), lambda i: (i, 0))
    wsp = pl.BlockSpec((ATT_W, D_MODEL), lambda i: (0, 0))
    wide = pl.BlockSpec((tm, D_MODEL), lambda i: (i, 0))
    return pl.pallas_call(
        body, name="mix_fwd", grid=(SEQ // tm,),
        in_specs=[row, row, wsp, wsp, pl.BlockSpec((tm, D_MODEL), lambda i: (i, 3)),
                  pl.BlockSpec((tm, D_MODEL), lambda i: (i, 4))],
        out_specs=[wide] * 3, out_shape=[_sds((SEQ, D_MODEL), BF16)] * 3,
        compiler_params=_params("parallel"),
    )(ya, yb, w_oa, w_ob, zm, zm)


def _gate_bwd(dmix, zm, p, gate_block, dz, *, name, tm=512):
    def body(dm_ref, g_ref, p_ref, *rest):
        dp_ref, dz_ref = rest[-2], rest[-1]
        dm = dm_ref[...].astype(F32)
        s = _sigmoid(g_ref[...].astype(F32))
        dp_ref[...] = (dm * s).astype(dp_ref.dtype)
        dz_ref[...] = (dm * p_ref[...].astype(F32) * s * (1.0 - s)).astype(dz_ref.dtype)

    wide = pl.BlockSpec((tm, D_MODEL), lambda i: (i, 0))
    gate = pl.BlockSpec((tm, D_MODEL), lambda i: (i, gate_block))
    extra = [] if dz is None else [dz]
    return pl.pallas_call(
        body, name=name, grid=(SEQ // tm,),
        in_specs=[wide, gate, wide] + [ANY] * len(extra),
        out_specs=[wide, gate],
        out_shape=[_sds((SEQ, D_MODEL), BF16), _sds((SEQ, Z_MAIN), BF16)],
        input_output_aliases={3: 1} if extra else {},
        compiler_params=_params("parallel"),
    )(dmix, zm, p, *extra)


def _out_fwd(mixed, w_out, x, g_post, g_pre, *, tm=512):
    def body(m_ref, w_ref, x_ref, gp_ref, gn_ref, y_ref, x2_ref, h_ref):
        y = jnp.dot(m_ref[...], w_ref[...], preferred_element_type=F32)
        y_ref[...] = y
        r = lax.rsqrt(jnp.mean(y * y, axis=-1, keepdims=True) + RMS_EPS)
        x2 = x_ref[...] + y * r * gp_ref[...]
        x2_ref[...] = x2
        r2 = lax.rsqrt(jnp.mean(x2 * x2, axis=-1, keepdims=True) + RMS_EPS)
        h_ref[...] = (x2 * r2 * gn_ref[...]).astype(h_ref.dtype)

    row = pl.BlockSpec((tm, D_MODEL), lambda i: (i, 0))
    vec = pl.BlockSpec((1, D_MODEL), lambda i: (0, 0))
    return pl.pallas_call(
        body, name="out_fwd", grid=(SEQ // tm,),
        in_specs=[row, pl.BlockSpec((D_MODEL, D_MODEL), lambda i: (0, 0)), row, vec, vec],
        out_specs=[row] * 3,
        out_shape=[_sds((SEQ, D_MODEL), F32), _sds((SEQ, D_MODEL), F32), _sds((SEQ, D_MODEL), BF16)],
        compiler_params=_params("parallel"),
    )(mixed, w_out, x, g_post, g_pre)


FFN_HALF = 256
FFN_TN = 2 * FFN_HALF
FFN_NJ = D_FF // FFN_HALF
FFN_GROUP = 2 * SUBLANE
UP_TM = 1024


def _ffn_interleave(t):
    lead = t.shape[:-1]
    return jnp.swapaxes(t.reshape(*lead, 2, FFN_NJ, FFN_HALF), -3, -2).reshape(*lead, 2 * D_FF)


def _ffn_deinterleave(t):
    lead = t.shape[:-1]
    return jnp.swapaxes(t.reshape(*lead, FFN_NJ, 2, FFN_HALF), -3, -2).reshape(*lead, 2 * D_FF)


W_IN_SHARD = (Z_MAIN + N_HEADS) // N_DEV
FORGET_LO = 3 * ATT_W


def _w_in_from_shards(shards, *, tm=256):
    def columns(g_ref, lo, width):
        p, off = divmod(lo, W_IN_SHARD)
        if off + width <= W_IN_SHARD:
            return g_ref[p, :, off:off + width]
        first = W_IN_SHARD - off
        return jnp.concatenate([g_ref[p, :, off:], g_ref[p + 1, :, :width - first]], axis=1)

    def body(g_ref, main_ref, f_ref):
        for t in range(Z_MAIN // LANE):
            lo = t * LANE
            main_ref[:, lo:lo + LANE] = columns(g_ref, lo if lo < FORGET_LO else lo + N_HEADS, LANE)
        f_ref[...] = jnp.concatenate([columns(g_ref, FORGET_LO, N_HEADS),
                                      jnp.zeros((tm, F_PAD - N_HEADS), f_ref.dtype)], axis=1)

    return pl.pallas_call(
        body, name="w_in_from_shards", grid=(D_MODEL // tm,),
        in_specs=[pl.BlockSpec((N_DEV, tm, W_IN_SHARD), lambda i: (0, i, 0))],
        out_specs=[pl.BlockSpec((tm, Z_MAIN), lambda i: (i, 0)), pl.BlockSpec((tm, F_PAD), lambda i: (i, 0))],
        out_shape=[_sds((D_MODEL, Z_MAIN), shards.dtype), _sds((D_MODEL, F_PAD), shards.dtype)],
        compiler_params=_params("parallel"),
    )(shards)


def _w_in_to_shards(g_main, g_f, *, tm=256):
    def natural(main_ref, f_ref, lo, width):
        pieces, hi = [], lo + width
        for ref, start, stop, shift in ((main_ref, 0, FORGET_LO, 0), (f_ref, FORGET_LO, FORGET_LO + N_HEADS, FORGET_LO),
                                        (main_ref, FORGET_LO + N_HEADS, Z_MAIN + N_HEADS, N_HEADS)):
            a, b = max(lo, start), min(hi, stop)
            if a < b:
                pieces.append(ref[:, a - shift:b - shift])
        return pieces[0] if len(pieces) == 1 else jnp.concatenate(pieces, axis=1)

    def body(main_ref, f_ref, o_ref):
        for p in range(N_DEV):
            for q in range(-(-W_IN_SHARD // LANE)):
                width = min(LANE, W_IN_SHARD - q * LANE)
                o_ref[p, :, q * LANE:q * LANE + width] = natural(main_ref, f_ref, p * W_IN_SHARD + q * LANE, width)

    return pl.pallas_call(
        body, name="w_in_to_shards", grid=(D_MODEL // tm,),
        in_specs=[pl.BlockSpec((tm, Z_MAIN), lambda i: (i, 0)), pl.BlockSpec((tm, F_PAD), lambda i: (i, 0))],
        out_specs=pl.BlockSpec((N_DEV, tm, W_IN_SHARD), lambda i: (0, i, 0)),
        out_shape=_sds((N_DEV, D_MODEL, W_IN_SHARD), g_main.dtype),
        compiler_params=_params("parallel"),
    )(g_main, g_f)


W_UP_SHARD = 2 * D_FF // N_DEV


def _w_up_lane_tile(k):
    block = k // 2
    return (2 * (block % FFN_NJ) + block // FFN_NJ) * FFN_HALF + (k % 2) * LANE


def _w_up_from_shards(shards, *, tm=256):
    def body(g_ref, o_ref):
        for k in range(2 * D_FF // LANE):
            p, off = divmod(k * LANE, W_UP_SHARD)
            if off + LANE <= W_UP_SHARD:
                tile = g_ref[p, :, off:off + LANE]
            else:
                tile = jnp.concatenate([g_ref[p, :, off:], g_ref[p + 1, :, :off + LANE - W_UP_SHARD]], axis=1)
            dst = _w_up_lane_tile(k)
            o_ref[:, dst:dst + LANE] = tile

    return pl.pallas_call(
        body, name="w_up_from_shards", grid=(D_MODEL // tm,),
        in_specs=[pl.BlockSpec((N_DEV, tm, W_UP_SHARD), lambda i: (0, i, 0))],
        out_specs=pl.BlockSpec((tm, 2 * D_FF), lambda i: (i, 0)),
        out_shape=_sds((D_MODEL, 2 * D_FF), shards.dtype),
        compiler_params=_params("parallel"),
    )(shards)


def _w_up_to_shards(t, *, tm=256):
    def body(x_ref, o_ref):
        for p in range(N_DEV):
            for q in range(-(-W_UP_SHARD // LANE)):
                width = min(LANE, W_UP_SHARD - q * LANE)
                k, off = divmod(p * W_UP_SHARD + q * LANE, LANE)
                src = _w_up_lane_tile(k)
                if off == 0:
                    tile = x_ref[:, src:src + width]
                else:
                    tile = x_ref[:, src + off:src + LANE]
                    if width > LANE - off:
                        nxt = _w_up_lane_tile(k + 1)
                        tile = jnp.concatenate([tile, x_ref[:, nxt:nxt + width - (LANE - off)]], axis=1)
                o_ref[p, :, q * LANE:q * LANE + width] = tile

    return pl.pallas_call(
        body, name="w_up_to_shards", grid=(D_MODEL // tm,),
        in_specs=[pl.BlockSpec((tm, 2 * D_FF), lambda i: (i, 0))],
        out_specs=pl.BlockSpec((N_DEV, tm, W_UP_SHARD), lambda i: (0, i, 0)),
        out_shape=_sds((N_DEV, D_MODEL, W_UP_SHARD), t.dtype),
        compiler_params=_params("parallel"),
    )(t)


def _gelu_parts(a):
    c = math.sqrt(2.0 / math.pi)
    a2 = a * a
    t = jnp.tanh((c * a) * (1.0 + 0.044715 * a2))
    half_a, one_t = 0.5 * a, 1.0 + t
    gelu = half_a * one_t
    dgelu = 0.5 * one_t + half_a * (1.0 - t * t) * (c + (3.0 * 0.044715 * c) * a2)
    return gelu, dgelu


def _row_masks(down):
    row = lax.broadcasted_iota(jnp.int32, (SUBLANE, FFN_TN), 0)
    return (row < 1, row < 2) if down else (row >= SUBLANE - 1, row >= SUBLANE - 2)


def _rolled(x, down):
    return (pltpu.roll(x, 1, 0), pltpu.roll(x, 2, 0)) if down else (
        pltpu.roll(x, SUBLANE - 1, 0), pltpu.roll(x, SUBLANE - 2, 0))


def _shifted(cur_rolled, neighbour_rolled, masks):
    return (jnp.where(masks[0], neighbour_rolled[0], cur_rolled[0]),
            jnp.where(masks[1], neighbour_rolled[1], cur_rolled[1]))


def _conv_consts(w_ref, b_ref):
    shape = (SUBLANE, FFN_TN)
    return [jnp.broadcast_to(w_ref[k:k + 1, :], shape) for k in range(3)] + [jnp.broadcast_to(b_ref[...], shape)]


def _up_conv_fwd(h2, w_up, conv_w, conv_b):
    nrow = SEQ // UP_TM
    n_tiles = FFN_NJ * nrow
    n_groups = UP_TM // FFN_GROUP

    def body(h_ref, wu_ref, w_ref, b_ref, u_ref, ab_ref, m_ref, ua_s, ub_s, c1_s, c2_s):
        k = pl.program_id(0)

        @pl.when(k == 0)
        def _():
            ub_s[...] = jnp.zeros_like(ub_s)

        @pl.when(jnp.maximum(k - 1, 0) % nrow == 0)
        def _():
            c1_s[...] = jnp.zeros_like(c1_s)
            c2_s[...] = jnp.zeros_like(c2_s)

        def step(write_s, read_s):
            u = jnp.dot(h_ref[...], wu_ref[...], preferred_element_type=F32)
            write_s[...] = u
            u_ref[...] = u.astype(u_ref.dtype)
            w0, w1, w2, bias = _conv_consts(w_ref, b_ref)
            masks = _row_masks(True)
            above = (c1_s[...], c2_s[...])
            for g in range(n_groups):
                rows = slice(g * FFN_GROUP, (g + 1) * FFN_GROUP)
                x = read_s[rows, :]
                convs = []
                for c in range(2):
                    cur = x[c * SUBLANE:(c + 1) * SUBLANE]
                    cur_rolled = _rolled(cur, True)
                    s1, s2 = _shifted(cur_rolled, above, masks)
                    convs.append(w0 * s2 + w1 * s1 + w2 * cur + bias)
                    above = cur_rolled
                y = jnp.concatenate(convs, axis=0)
                ab_ref[rows, :] = y.astype(ab_ref.dtype)
                m_ref[rows, :] = (_gelu_parts(y[:, :FFN_HALF])[0] * y[:, FFN_HALF:]).astype(m_ref.dtype)
            c1_s[...], c2_s[...] = above

        @pl.when(k % 2 == 0)
        def _():
            step(ua_s, ub_s)

        @pl.when(k % 2 == 1)
        def _():
            step(ub_s, ua_s)

    this = lambda k: jnp.minimum(k, n_tiles - 1)
    last = lambda k: jnp.maximum(k - 1, 0)
    blk = lambda tile: pl.BlockSpec((UP_TM, FFN_TN), lambda k: (tile(k) % nrow, tile(k) // nrow))
    return pl.pallas_call(
        body, name="up_conv_fwd", grid=(n_tiles + 1,),
        in_specs=[pl.BlockSpec((UP_TM, D_MODEL), lambda k: (this(k) % nrow, 0)),
                  pl.BlockSpec((D_MODEL, FFN_TN), lambda k: (0, this(k) // nrow)),
                  pl.BlockSpec((3, FFN_TN), lambda k: (0, last(k) // nrow)),
                  pl.BlockSpec((1, FFN_TN), lambda k: (0, last(k) // nrow))],
        out_specs=[blk(this), blk(last), pl.BlockSpec((UP_TM, FFN_HALF), lambda k: (last(k) % nrow, last(k) // nrow))],
        out_shape=[_sds((SEQ, 2 * D_FF), BF16), _sds((SEQ, 2 * D_FF), BF16), _sds((SEQ, D_FF), BF16)],
        scratch_shapes=[pltpu.VMEM((UP_TM, FFN_TN), F32), pltpu.VMEM((UP_TM, FFN_TN), F32),
                        pltpu.VMEM((SUBLANE, FFN_TN), F32), pltpu.VMEM((SUBLANE, FFN_TN), F32)],
        compiler_params=_params("arbitrary"),
    )(h2, w_up, conv_w, conv_b)


def _ffn_mid_bwd(dy2, w_down, u, ab, conv_w):
    nrow = SEQ // UP_TM
    n_tiles = FFN_NJ * nrow
    n_groups = UP_TM // FFN_GROUP
    this = lambda k: jnp.minimum(k, n_tiles - 1)
    last = lambda k: jnp.maximum(k - 1, 0)
    row_of = lambda tile: nrow - 1 - tile % nrow

    def body(dy_ref, wd_ref, u_ref, ab_ref, w_ref, du_ref, gw_ref, gb_ref, c_s, dma_s, dmb_s):
        k = pl.program_id(0)

        @pl.when(k == 0)
        def _():
            dmb_s[...] = jnp.zeros_like(dmb_s)

        @pl.when(last(k) % nrow == 0)
        def _():
            c_s[...] = jnp.zeros_like(c_s)
            gw_ref[...] = jnp.zeros_like(gw_ref)
            gb_ref[...] = jnp.zeros_like(gb_ref)

        def step(write_s, read_s):
            write_s[...] = lax.dot_general(dy_ref[...], wd_ref[...], NT_DIMS, preferred_element_type=F32)
            taps = [jnp.broadcast_to(w_ref[t:t + 1, :], (SUBLANE, FFN_TN)) for t in range(3)]
            masks = _row_masks(False)
            below = _rolled(c_s[...], False)
            acc = [jnp.zeros((SUBLANE, FFN_TN), F32)] * 4
            for g in reversed(range(n_groups)):
                rows = slice(g * FFN_GROUP, (g + 1) * FFN_GROUP)
                x, y, dmv = u_ref[rows, :].astype(F32), ab_ref[rows, :].astype(F32), read_s[rows, :]
                gelu, dgelu = _gelu_parts(y[:, :FFN_HALF])
                d = jnp.concatenate([dmv * y[:, FFN_HALF:] * dgelu, dmv * gelu], axis=1)
                pre = [None, None]
                for c in (1, 0):
                    sl = slice(c * SUBLANE, (c + 1) * SUBLANE)
                    cur, xs = d[sl], x[sl]
                    cur_rolled = _rolled(cur, False)
                    up1, up2 = _shifted(cur_rolled, below, masks)
                    acc = [acc[0] + up2 * xs, acc[1] + up1 * xs, acc[2] + cur * xs, acc[3] + cur]
                    pre[c] = taps[2] * cur + taps[1] * up1 + taps[0] * up2
                    below = cur_rolled
                du_ref[rows, :] = jnp.concatenate(pre, axis=0).astype(du_ref.dtype)
            c_s[...] = pltpu.roll(below[0], 1, 0)
            for t in range(3):
                gw_ref[t:t + 1, :] += jnp.sum(acc[t], axis=0, keepdims=True)
            gb_ref[...] += jnp.sum(acc[3], axis=0, keepdims=True)

        @pl.when(k % 2 == 0)
        def _():
            step(dma_s, dmb_s)

        @pl.when(k % 2 == 1)
        def _():
            step(dmb_s, dma_s)

    blk = pl.BlockSpec((UP_TM, FFN_TN), lambda k: (row_of(last(k)), last(k) // nrow))
    col = lambda rows: pl.BlockSpec((rows, FFN_TN), lambda k: (0, last(k) // nrow))
    return pl.pallas_call(
        body, name="ffn_mid_bwd", grid=(n_tiles + 1,),
        in_specs=[pl.BlockSpec((UP_TM, D_MODEL), lambda k: (row_of(this(k)), 0)),
                  pl.BlockSpec((FFN_HALF, D_MODEL), lambda k: (this(k) // nrow, 0)), blk, blk, col(3)],
        out_specs=[blk, col(3), col(1)],
        out_shape=[_sds((SEQ, 2 * D_FF), BF16), _sds((3, 2 * D_FF), F32), _sds((1, 2 * D_FF), F32)],
        scratch_shapes=[pltpu.VMEM((SUBLANE, FFN_TN), F32), pltpu.VMEM((UP_TM, FFN_HALF), F32),
                        pltpu.VMEM((UP_TM, FFN_HALF), F32)],
        compiler_params=_params("arbitrary"),
    )(dy2, w_down, u, ab, conv_w)


def _down_fwd(m, w_down, x2, g_post, target, *, tm=512):
    def body(m_ref, w_ref, x2_ref, g_ref, t_ref, dout_ref, dy_ref, gg_ref, loss_ref):
        @pl.when(pl.program_id(0) == 0)
        def _():
            gg_ref[...] = jnp.zeros_like(gg_ref)
            loss_ref[...] = jnp.zeros_like(loss_ref)

        y = jnp.dot(m_ref[...], w_ref[...], preferred_element_type=F32)
        r = lax.rsqrt(jnp.mean(y * y, axis=-1, keepdims=True) + RMS_EPS)
        yn = y * r
        diff = (x2_ref[...] + yn * g_ref[...]) - t_ref[...]
        loss_ref[...] += jnp.sum(diff * diff)
        dout = diff * (1.0 / D_MODEL)
        dout_ref[...] = dout
        gg_ref[...] += jnp.sum(dout * yn, axis=0, keepdims=True)
        dn = dout * g_ref[...]
        dy_ref[...] = (r * (dn - yn * jnp.mean(dn * yn, axis=-1, keepdims=True))).astype(dy_ref.dtype)

    row = pl.BlockSpec((tm, D_MODEL), lambda i: (i, 0))
    vec = pl.BlockSpec((1, D_MODEL), lambda i: (0, 0))
    return pl.pallas_call(
        body, name="down_fwd", grid=(SEQ // tm,),
        in_specs=[pl.BlockSpec((tm, D_FF), lambda i: (i, 0)), pl.BlockSpec((D_FF, D_MODEL), lambda i: (0, 0)),
                  row, vec, row],
        out_specs=[row, row, vec, pl.BlockSpec((1, LANE), lambda i: (0, 0))],
        out_shape=[_sds((SEQ, D_MODEL), F32), _sds((SEQ, D_MODEL), BF16), _sds((1, D_MODEL), F32),
                   _sds((1, LANE), F32)],
        compiler_params=_params("arbitrary"),
    )(m, w_down, x2, g_post, target)


def _local_step(x, target, w_main, w_f, b_forget, conv_b, g_pre_mix, g_post_mix, g_pre_ffn, g_post_ffn,
                late_weights, ffn_grads_ready, proj_grads_ready, mixer_grads_ready):
    mm = _matmul
    tabs = _rope_tables()

    h1 = _rms_fwd(x, g_pre_mix, name="rms_pre_mix")
    zm = mm(h1, w_main, out_dtype=BF16, tm=2048, tn=1024, tk=1024, name="in_proj")
    zf = mm(h1, w_f, out_dtype=F32, tm=2048, tn=F_PAD, tk=1024, name="in_proj_forget")
    f_row, sg_row = _fox_prep(zf[:, :N_HEADS].T, b_forget.reshape(N_HEADS, 1))
    f_cols = jnp.pad(f_row.T, ((0, 0), (0, LANE - N_HEADS)))
    q_slots, k_slots, v_slots = _fox_pack_fwd(zm, f_cols)
    ya, lse_a = _fox_fwd(q_slots, k_slots, v_slots)
    qkv_d = dict(zip([d for _, d in DIL_PATTERNS], _rope_fwd(zm, tabs)))
    dil = [_dil_fwd(qkv_d[d], d) for _, d in DIL_PATTERNS]
    yb, lse_b = _dil_merge([o for o, _ in dil], [l for _, l in dil])
    w_oa, w_ob, w_out, w_up, conv_w, w_down = late_weights(yb)
    pa, pb, mixed = _mix_fwd(ya, yb, w_oa, w_ob, zm)
    y1, x2, h2 = _out_fwd(mixed, w_out, x, g_post_mix, g_pre_ffn)
    u, ab, m = _up_conv_fwd(h2, w_up, conv_w, _ffn_interleave(conv_b))
    dout, dy2, gg_post_ffn, sq_err = _down_fwd(m, w_down, x2, g_post_ffn, target)

    g_w_down = mm(m, dy2, ta=True, out_dtype=BF16, tm=D_FF // 2, tn=1024, tk=2048, name="grad_w_down")
    du, g_conv_w, g_conv_b = _ffn_mid_bwd(dy2, w_down, u, ab, conv_w)
    g_w_up = mm(h2, du, ta=True, out_dtype=BF16, tm=1024, tn=D_FF // 2, tk=2048, name="grad_w_up")
    tok = ffn_grads_ready(dict(w_down=g_w_down, w_up_blocks=g_w_up, conv_w=_ffn_deinterleave(g_conv_w)))
    dh2 = mm(du, w_up, tb=True, out_dtype=BF16, tm=512, tn=1024, tk=2 * D_FF, name="d_h2")

    dx2, dy1, gg_pre_ffn, gg_post_mix = _rms_pair_bwd([dh2], x2, g_pre_ffn, dout, y1, g_post_mix + tok)
    g_w_out = mm(mixed, dy1, ta=True, out_dtype=BF16, tm=1024, tn=1024, tk=2048, name="grad_w_out")
    dmix = mm(dy1, w_out, tb=True, out_dtype=BF16, tm=2048, tn=1024, tk=1024, name="d_mixed")
    dpa, dz = _gate_bwd(dmix, zm, pa, 3, None, name="gate_bwd_fox")
    dpb, dz = _gate_bwd(dmix, zm, pb, 4, dz, name="gate_bwd_dil")
    g_w_oa = mm(ya, dpa, ta=True, out_dtype=BF16, tm=512, tn=1024, tk=SEQ, name="grad_w_o_fox")
    g_w_ob = mm(yb, dpb, ta=True, out_dtype=BF16, tm=512, tn=1024, tk=SEQ, name="grad_w_o_dil")
    tok = proj_grads_ready(dict(w_o_fox=g_w_oa, w_o_dil=g_w_ob, w_out=g_w_out))
    dya = mm(dpa, w_oa, tb=True, out_dtype=BF16, tm=2048, tn=512, tk=1024, name="d_y_fox")
    dyb = mm(dpb, w_ob, tb=True, out_dtype=BF16, tm=2048, tn=512, tk=1024, name="d_y_dil")

    qb_slots, do_slots = _fox_pack_bwd(zm, f_cols + tok, lse_a, ya, dya)
    dz, df_cols = _fox_unpack(*_fox_bwd(qb_slots, k_slots, v_slots, do_slots), dz)
    dfa_t, g_b_forget = _fox_post_bwd(df_cols[:, :N_HEADS].T, sg_row)

    rows_d = _dil_bwd_prep(yb, dyb, lse_b)
    dil_g = [_dil_bwd(qkv_d[d], *rows_d[k], d) for k, (_, d) in enumerate(DIL_PATTERNS)]
    dz = _dil_grad_combine([g[0] for g in dil_g], [g[1] for g in dil_g], [g[2] for g in dil_g], tabs, dz)

    dzf = jnp.pad(dfa_t.T, ((0, 0), (0, F_PAD - N_HEADS)))
    g_w_main = mm(h1, dz, ta=True, out_dtype=BF16, tm=1024, tn=Z_MAIN // 4, tk=2048, name="grad_w_in")
    g_w_f = mm(h1, dzf, ta=True, out_dtype=BF16, tm=1024, tn=F_PAD, tk=1024, name="grad_w_in_forget")
    tok = mixer_grads_ready(dict(w_main=g_w_main, w_f=g_w_f))
    dh1 = [mm(dz, w_main, tb=True, out_dtype=BF16, tm=512, tn=1024, tk=Z_MAIN, name="d_h1"),
           mm(dzf + tok, w_f, tb=True, out_dtype=BF16, tm=2048, tn=1024, tk=F_PAD, name="d_h1_forget")]
    grad_x, gg_pre_mix = _rms_bwd(dh1, x, g_pre_mix, dx2, out_dtype=F32, name="rms_pre_mix_bwd")

    grads = dict(
        b_forget=g_b_forget.reshape(1, N_HEADS), conv_b=_ffn_deinterleave(g_conv_b),
        g_pre_mix=gg_pre_mix, g_post_mix=gg_post_mix, g_pre_ffn=gg_pre_ffn, g_post_ffn=gg_post_ffn)
    return sq_err, grad_x, grads


def _exchange(arrays, scatter, *, name):
    n = len(arrays)
    scatters = [scatter] * n if isinstance(scatter, bool) else list(scatter)

    def body(*refs):
        ins, outs = refs[:n], refs[n:2 * n]
        send_sems, recv_sems, local_sems = refs[2 * n:]
        me, peers = _peers()

        def remote(a, k):
            dev, slot = peers[k]
            return pltpu.make_async_remote_copy(
                src_ref=ins[a].at[slot] if scatters[a] else ins[a], dst_ref=outs[a].at[me],
                send_sem=send_sems.at[a, k], recv_sem=recv_sems.at[a, k],
                device_id=dev, device_id_type=MESH_ID)

        def landed(a, k):
            dev, slot = peers[k]
            return pltpu.make_async_remote_copy(
                src_ref=outs[a].at[slot], dst_ref=outs[a].at[slot],
                send_sem=send_sems.at[a, k], recv_sem=recv_sems.at[a, k],
                device_id=dev, device_id_type=MESH_ID)

        own = [pltpu.make_async_copy(ins[a].at[me] if scatters[a] else ins[a], outs[a].at[me], local_sems.at[a])
               for a in range(n)]
        copies = [remote(a, k) for k in range(N_DEV - 1) for a in range(n)]
        for cp in own + copies:
            cp.start()
        for k in range(N_DEV - 1):
            for a in range(n):
                landed(a, k).wait_recv()
        for cp in copies:
            cp.wait_send()
        for cp in own:
            cp.wait()

    out_shape = [_sds(((N_DEV,) + a.shape[-2:]), a.dtype) for a in arrays]
    return pl.pallas_call(
        body, name=name, in_specs=[ANY] * n, out_specs=[ANY] * n, out_shape=out_shape,
        scratch_shapes=[pltpu.SemaphoreType.DMA((n, N_DEV - 1)), pltpu.SemaphoreType.DMA((n, N_DEV - 1)),
                        pltpu.SemaphoreType.DMA((n,))],
    )(*arrays)


def _gather_two_level(shard, *, name):
    def body(x_ref, out_ref, send_sems, recv_sems, local_sem):
        x, y, c = lax.axis_index("x"), lax.axis_index("y"), lax.axis_index("c")
        me, sibling = (x, y, c), (x, y, 1 - c)
        chips = [(1 - x, y), (x, 1 - y), (1 - x, 1 - y)]

        def slot(px, py, pc):
            return out_ref.at[4 * px + 2 * py + pc]

        def copy(k, block, to, src=None):
            return pltpu.make_async_remote_copy(
                src_ref=slot(*block) if src is None else src, dst_ref=slot(*block),
                send_sem=send_sems.at[k], recv_sem=recv_sems.at[k], device_id=to, device_id_type=MESH_ID)

        mine = pltpu.make_async_copy(x_ref, slot(*me), local_sem)
        mine.start()
        first = [copy(0, me, sibling, src=x_ref)]
        first += [copy(1 + j, me, (*chip, c), src=x_ref) for j, chip in enumerate(chips)]
        for cp in first:
            cp.start()
        passed = [copy(4 + j, (*chip, c), sibling) for j, chip in enumerate(chips)]
        for j, chip in enumerate(chips):
            copy(1 + j, (*chip, c), me).wait_recv()
            passed[j].start()
        copy(0, sibling, me).wait_recv()
        for j, chip in enumerate(chips):
            copy(4 + j, (*chip, 1 - c), me).wait_recv()
        for cp in first + passed:
            cp.wait_send()
        mine.wait()

    return pl.pallas_call(
        body, name=name, in_specs=[ANY], out_specs=ANY, out_shape=_sds((N_DEV,) + shard.shape, shard.dtype),
        scratch_shapes=[pltpu.SemaphoreType.DMA((N_DEV - 1,)), pltpu.SemaphoreType.DMA((N_DEV - 1,)),
                        pltpu.SemaphoreType.DMA],
    )(shard)


N_CHIPS = N_DEV // 2


def _peers(chips_only=False):
    x, y, c = lax.axis_index("x"), lax.axis_index("y"), lax.axis_index("c")
    out = []
    if chips_only:
        for k in range(1, N_CHIPS):
            px = 1 - x if k & 2 else x
            py = 1 - y if k & 1 else y
            out.append(((px, py, c), 2 * px + py))
        return 2 * x + y, out
    for k in range(1, N_DEV):
        px = 1 - x if k & 4 else x
        py = 1 - y if k & 2 else y
        pc = 1 - c if k & 1 else c
        out.append(((px, py, pc), 4 * px + 2 * py + pc))
    return 4 * x + 2 * y + c, out


def _sibling_swap(slot_arrays, *, name):
    n = len(slot_arrays)

    def body(*refs):
        ins, outs, send_sems, recv_sems = refs[:n], refs[n:2 * n], refs[2 * n], refs[2 * n + 1]
        x, y, c = lax.axis_index("x"), lax.axis_index("y"), lax.axis_index("c")
        copies = [pltpu.make_async_remote_copy(
            src_ref=ins[a].at[2 * q + (1 - c)], dst_ref=outs[a].at[q], send_sem=send_sems.at[a, q],
            recv_sem=recv_sems.at[a, q], device_id=(x, y, 1 - c), device_id_type=MESH_ID)
            for a in range(n) for q in range(N_CHIPS)]
        for cp in copies:
            cp.start()
        for cp in copies:
            cp.wait_recv()
        for cp in copies:
            cp.wait_send()

    return pl.pallas_call(
        body, name=name, in_specs=[ANY] * n, out_specs=[ANY] * n,
        out_shape=[_sds((N_CHIPS,) + t.shape[1:], t.dtype) for t in slot_arrays],
        scratch_shapes=[pltpu.SemaphoreType.DMA((n, N_CHIPS)), pltpu.SemaphoreType.DMA((n, N_CHIPS))],
    )(*slot_arrays)


def _pair_sum(slots, from_sibling, *, name, tn):
    _, r, c = slots.shape
    core = lax.axis_index("c").astype(jnp.int32).reshape(1)

    def body(core_ref, a_ref, b_ref, o_ref):
        o_ref[...] = (a_ref[...].astype(F32) + b_ref[...].astype(F32)).astype(o_ref.dtype)

    blk = lambda f: pl.BlockSpec((1, r, tn), f)
    return pl.pallas_call(
        body, name=name,
        grid_spec=pltpu.PrefetchScalarGridSpec(
            num_scalar_prefetch=1, grid=(N_CHIPS, c // tn),
            in_specs=[blk(lambda q, j, core: (2 * q + core[0], 0, j)), blk(lambda q, j, core: (q, 0, j))],
            out_specs=blk(lambda q, j, core: (q, 0, j))),
        out_shape=_sds((N_CHIPS, r, c), slots.dtype),
        compiler_params=_params("parallel", "parallel"),
    )(core, slots, from_sibling)


HBM = pl.BlockSpec(memory_space=pltpu.HBM)
SEM = pl.BlockSpec(memory_space=pltpu.SEMAPHORE)
DATAFLOW = pltpu.SideEffectType.DATAFLOW_SIDE_EFFECTING


def _split_copy(srcs, lands, send_sems, recv_sems, scatter, a, k, me, peers, incoming=False):
    dev, slot = peers[k]
    if incoming:
        src = dst = lands[a].at[slot]
    else:
        src, dst = (srcs[a].at[slot] if scatter else srcs[a]), lands[a].at[me]
    sem = a * len(peers) + k
    return pltpu.make_async_remote_copy(
        src_ref=src, dst_ref=dst, send_sem=send_sems.at[sem], recv_sem=recv_sems.at[sem],
        device_id=dev, device_id_type=MESH_ID)


def _exchange_start(arrays, scatter, *, name, chips_only=False):
    n = len(arrays)
    n_slots = N_CHIPS if chips_only else N_DEV

    def body(*refs):
        srcs, lands = refs[:n], refs[n:2 * n]
        send_sems, recv_sems = refs[2 * n], refs[2 * n + 1]
        token = refs[-1]
        me, peers = _peers(chips_only)
        for k in range(len(peers)):
            for a in range(n):
                _split_copy(srcs, lands, send_sems, recv_sems, scatter, a, k, me, peers).start()
        token[...] = jnp.zeros_like(token)

    land_shapes = [((n_slots,) + a.shape[-2:], a.dtype) for a in arrays]
    sems = pltpu.SemaphoreType.DMA((n * (n_slots - 1),))
    outs = pl.pallas_call(
        body, name=name,
        out_shape=(sems, sems, *[pltpu.HBM(a.shape, a.dtype) for a in arrays],
                   *[pltpu.HBM(s, d) for s, d in land_shapes], _sds((SUBLANE, LANE), F32)),
        in_specs=[HBM] * (2 * n),
        out_specs=(SEM, SEM, *[HBM] * (2 * n), pl.BlockSpec(memory_space=pltpu.VMEM)),
        input_output_aliases={i: 2 + i for i in range(2 * n)},
        compiler_params=pltpu.CompilerParams(has_side_effects=DATAFLOW),
    )(*[pltpu.with_memory_space_constraint(a, pltpu.HBM) for a in arrays],
      *[pltpu.with_memory_space_constraint(lax.empty(s, d), pltpu.HBM) for s, d in land_shapes])
    return (outs[0], outs[1], outs[2:2 + n], outs[2 + n:2 + 2 * n], scatter, chips_only), outs[-1]


def _exchange_wait(handles, after, *, name):
    send_sems, recv_sems, srcs, lands, scatter, chips_only = handles
    n = len(srcs)

    def body(*refs):
        src_refs, land_refs = refs[:n], refs[n:2 * n]
        send_ref, recv_ref = refs[2 * n], refs[2 * n + 1]
        me, peers = _peers(chips_only)
        for k in range(len(peers)):
            for a in range(n):
                _split_copy(src_refs, land_refs, send_ref, recv_ref, scatter, a, k, me, peers).wait_send()
                _split_copy(src_refs, land_refs, send_ref, recv_ref, scatter, a, k, me, peers, True).wait_recv()

    outs = pl.pallas_call(
        body, name=name,
        out_shape=tuple(pltpu.HBM(t.shape, t.dtype) for t in (*srcs, *lands)),
        in_specs=[HBM] * (2 * n) + [SEM, SEM, pl.BlockSpec(memory_space=pl.ANY)],
        out_specs=tuple([HBM] * (2 * n)),
        input_output_aliases={i: i for i in range(2 * n)},
        compiler_params=pltpu.CompilerParams(has_side_effects=DATAFLOW),
    )(*srcs, *lands, send_sems, recv_sems, after)
    return _with_own_slot(outs[n:], outs[:n], scatter, chips_only)


def _with_own_slot(landed, own, scatter, chips_only):
    me = 2 * lax.axis_index("x") + lax.axis_index("y")
    if not chips_only:
        me = 2 * me + lax.axis_index("c")
    out = []
    for buf, src in zip(landed, own):
        mine = lax.dynamic_index_in_dim(src, me, 0, keepdims=False) if scatter else src
        out.append(lax.dynamic_update_index_in_dim(buf, mine, me, 0))
    return out


def _adamw(parts, w, m, v, *, name, tm):
    r, c = w.shape
    assert r % tm == 0

    def body(p_ref, w_ref, m_ref, v_ref, g_ref, d_ref, nm_ref, nv_ref):
        _adamw_update(p_ref, w_ref, m_ref, v_ref, g_ref, d_ref, nm_ref, nv_ref)

    blk = pl.BlockSpec((tm, c), lambda i: (i, 0))
    return pl.pallas_call(
        body, name=name, grid=(r // tm,),
        in_specs=[pl.BlockSpec((parts.shape[0], tm, c), lambda i: (0, i, 0)), blk, blk, blk],
        out_specs=[blk] * 4, out_shape=[_sds((r, c), F32)] * 4,
        compiler_params=_params("parallel"),
    )(parts, w, m, v)


def _adamw_update(p_ref, w_ref, m_ref, v_ref, g_ref, d_ref, nm_ref, nv_ref):
    g = p_ref[0].astype(F32)
    for s in range(1, p_ref.shape[0]):
        g = g + p_ref[s].astype(F32)
    g_ref[...] = g
    m_new = ADAM_B1 * m_ref[...] + (1.0 - ADAM_B1) * g
    v_new = ADAM_B2 * v_ref[...] + (1.0 - ADAM_B2) * (g * g)
    nm_ref[...] = m_new
    nv_ref[...] = v_new
    m_hat = m_new / (1.0 - ADAM_B1 ** ADAM_STEP)
    v_hat = v_new / (1.0 - ADAM_B2 ** ADAM_STEP)
    d_ref[...] = -ADAM_LR * (m_hat / (jnp.sqrt(v_hat) + ADAM_EPS) + ADAM_WD * w_ref[...])


SMALL = ("g_pre_mix", "b_forget", "g_post_mix", "g_pre_ffn", "conv_b", "g_post_ffn")


def _adamw_small(parts, ws, ms, vs, sq_err_parts):
    n = len(ws)

    def body(*refs):
        ins, sq_ref, outs, loss_ref = refs[:4 * n], refs[4 * n], refs[4 * n + 1:-1], refs[-1]
        for i in range(n):
            _adamw_update(ins[i], ins[n + i], ins[2 * n + i], ins[3 * n + i], *outs[4 * i:4 * i + 4])
        total = sq_ref[0]
        for s in range(1, N_DEV):
            total = total + sq_ref[s]
        loss_ref[...] = total * (0.5 / D_MODEL)

    res = pl.pallas_call(
        body, name="adamw_small",
        out_shape=[_sds(w.shape, F32) for w in ws for _ in range(4)] + [_sds((1, LANE), F32)],
        compiler_params=pltpu.CompilerParams(vmem_limit_bytes=VMEM_LIMIT),
    )(*parts, *ws, *ms, *vs, sq_err_parts)
    return [res[4 * i:4 * i + 4] for i in range(n)], res[-1][0, 0]


def kernel(x, g_pre_mix, w_in, b_forget, w_o_fox, w_o_dil, w_out, g_post_mix, g_pre_ffn, w_up, conv_w, conv_b, w_down, g_post_ffn, loss_target, m_g_pre_mix, m_w_in, m_b_forget, m_w_o_fox, m_w_o_dil, m_w_out, m_g_post_mix, m_g_pre_ffn, m_w_up, m_conv_w, m_conv_b, m_w_down, m_g_post_ffn, v_g_pre_mix, v_w_in, v_b_forget, v_w_o_fox, v_w_o_dil, v_w_out, v_g_post_mix, v_g_pre_ffn, v_w_up, v_conv_w, v_conv_b, v_w_down, v_g_post_ffn):
    names = ("g_pre_mix", "w_in", "b_forget", "w_o_fox", "w_o_dil", "w_out", "g_post_mix", "g_pre_ffn",
             "w_up", "conv_w", "conv_b", "w_down", "g_post_ffn")
    w = dict(g_pre_mix=g_pre_mix, w_in=w_in, b_forget=b_forget, w_o_fox=w_o_fox, w_o_dil=w_o_dil, w_out=w_out,
             g_post_mix=g_post_mix, g_pre_ffn=g_pre_ffn, w_up=w_up, conv_w=conv_w, conv_b=conv_b, w_down=w_down,
             g_post_ffn=g_post_ffn)
    m = dict(g_pre_mix=m_g_pre_mix, w_in=m_w_in, b_forget=m_b_forget, w_o_fox=m_w_o_fox, w_o_dil=m_w_o_dil,
             w_out=m_w_out, g_post_mix=m_g_post_mix, g_pre_ffn=m_g_pre_ffn, w_up=m_w_up, conv_w=m_conv_w,
             conv_b=m_conv_b, w_down=m_w_down, g_post_ffn=m_g_post_ffn)
    v = dict(g_pre_mix=v_g_pre_mix, w_in=v_w_in, b_forget=v_b_forget, w_o_fox=v_w_o_fox, w_o_dil=v_w_o_dil,
             w_out=v_w_out, g_post_mix=v_g_post_mix, g_pre_ffn=v_g_pre_ffn, w_up=v_w_up, conv_w=v_conv_w,
             conv_b=v_conv_b, w_down=v_w_down, g_post_ffn=v_g_post_ffn)
    sharded = ("w_in", "w_o_fox", "w_o_dil", "w_out", "w_up", "w_down", "conv_w")
    wire = lambda n: F32 if n == "conv_w" else BF16

    by_cols = lambda t: jnp.transpose(t, (1, 0, 2)).reshape(t.shape[1], N_DEV * t.shape[2])
    by_rows = lambda t: t.reshape(N_DEV * t.shape[1], t.shape[2])
    col_slots = lambda t: jnp.transpose(t.reshape(t.shape[0], N_DEV, t.shape[1] // N_DEV), (1, 0, 2))
    row_slots = lambda t: t.reshape(N_DEV, t.shape[0] // N_DEV, t.shape[1])
    to_slots = lambda n, t: (row_slots if n in ("w_out", "w_down") else col_slots)(t).astype(wire(n))
    shard = lambda n: w[n][0].astype(wire(n))

    w_main, w_f = _w_in_from_shards(_gather_two_level(shard("w_in"), name="gather_w_in"))
    late = ("w_o_fox", "w_o_dil", "w_out", "w_up", "conv_w", "w_down")
    order = jnp.minimum(jnp.abs(w_f[0, 0].astype(F32)), 0.0)
    late_handles, late_tok = _exchange_start(
        [shard(n) + order.astype(wire(n)) if n == "conv_w" else shard(n) for n in late], False,
        name="gather_late_start")

    def late_weights(after):
        got = dict(zip(late, _exchange_wait(late_handles, after, name="gather_late_wait")))
        return (by_cols(got["w_o_fox"]), by_cols(got["w_o_dil"]), by_rows(got["w_out"]),
                _w_up_from_shards(got["w_up"]),
                _ffn_interleave(by_cols(got["conv_w"])),
                by_rows(got["w_down"]))

    pending = {}

    def ffn_grads_ready(g):
        slots = [to_slots("w_down", g["w_down"]), _w_up_to_shards(g["w_up_blocks"]), to_slots("conv_w", g["conv_w"])]
        pending["ffn"] = _exchange_start(slots, True, name="scatter_ffn_start")
        return pending["ffn"][1][0, 0]

    def proj_grads_ready(g):
        pending["proj"] = _exchange_start([to_slots(n, g[n]) for n in ("w_o_fox", "w_o_dil", "w_out")], True,
                                          name="scatter_proj_start")
        return pending["proj"][1][0, 0]

    def mixer_grads_ready(g):
        slots = _w_in_to_shards(g["w_main"], g["w_f"])
        theirs = _sibling_swap([slots], name="scatter_w_in_swap")[0]
        chip_sums = _pair_sum(slots, theirs, name="scatter_w_in_pair_sum", tn=W_IN_SHARD)
        pending["w_in"] = _exchange_start([chip_sums], True, name="scatter_w_in_start", chips_only=True)
        return pending["w_in"][1][0, 0]

    sq_err, grad_x, g = _local_step(
        x[0], loss_target[0], w_main, w_f, b_forget, conv_b, g_pre_mix + late_tok[0, 0], g_post_mix, g_pre_ffn,
        g_post_ffn, late_weights, ffn_grads_ready, proj_grads_ready, mixer_grads_ready)

    tiles = dict(w_in=256, w_o_fox=512, w_o_dil=512, w_out=128, w_up=256, w_down=176, conv_w=3)
    adam = lambda n, p: _adamw(p, w[n][0], m[n][0], v[n][0], name=f"adamw_{n}", tm=tiles[n])
    res = {}
    for key, group in (("ffn", ("w_down", "w_up", "conv_w")), ("proj", ("w_o_fox", "w_o_dil", "w_out"))):
        landed = _exchange_wait(pending[key][0], grad_x, name=f"scatter_{key}_wait")
        res.update({n: adam(n, p) for n, p in zip(group, landed)})
    done = res["w_up"][3]
    res["w_in"] = adam("w_in", _exchange_wait(pending["w_in"][0], done, name="scatter_w_in_wait")[0])
    small_parts = _exchange([g[n] for n in SMALL] + [sq_err], False, name="gather_small_grads")
    small, loss = _adamw_small(small_parts[:-1], *[[t[n] for n in SMALL] for t in (w, m, v)], small_parts[-1])
    small = dict(zip(SMALL, small))
    out = [[(res[n][k][None] if n in sharded else small[n][k]) for n in names] for k in range(4)]
    return (loss, grad_x[None], *out[0], *out[1], *out[2], *out[3])
```

```python
import functools
import math

import jax
import jax.numpy as jnp
import numpy as np
from jax import lax
from jax.experimental import pallas as pl
from jax.experimental.pallas import tpu as pltpu

F32 = jnp.float32
BF16 = jnp.bfloat16

SEQ = 4096
D_MODEL = 1024
N_HEADS = 8
HEAD_DIM = 64
ATT_W = N_HEADS * HEAD_DIM
D_FF = 2816
Z_MAIN = 5120
F_PAD = 128
ROPE_DIM = 16
ROPE_THETA = 500000.0
RMS_EPS = 1e-6
NEG_INF = -1e30
SCALE = 1.0 / math.sqrt(HEAD_DIM)
DIL_PATTERNS = ((128, 1), (512, 4), (2048, 16))
DIL_BLK = 128
DIL_STEP_BLOCKS = 2
N_DEV = 8

ADAM_LR = 0.001
ADAM_B1 = 0.9
ADAM_B2 = 0.999
ADAM_EPS = 1e-08
ADAM_WD = 0.01
ADAM_STEP = 10

LANE = 128
SUBLANE = 8
VMEM_LIMIT = 56 * 1024 * 1024
MESH_ID = pl.DeviceIdType.MESH
ANY = pl.BlockSpec(memory_space=pl.ANY)


def _params(*sem):
    return pltpu.CompilerParams(dimension_semantics=sem, vmem_limit_bytes=VMEM_LIMIT)


def _sds(shape, dtype):
    return jax.ShapeDtypeStruct(shape, dtype)


def _matmul(a, b, *, ta=False, tb=False, out_dtype, tm, tn, tk, name, b_k_off=0):
    if ta:
        kk, m = a.shape
    else:
        m, kk = a.shape
    n = b.shape[0] if tb else b.shape[1]
    tm, tn, tk = min(tm, m), min(tn, n), min(tk, kk)
    assert (b.shape[1] if tb else b.shape[0]) >= b_k_off * tk + kk
    assert m % tm == 0 and n % tn == 0 and kk % tk == 0, (name, m, n, kk, tm, tn, tk)
    nk = kk // tk
    dims = (((0 if ta else 1,), (1 if tb else 0,)), ((), ()))

    def body(a_ref, b_ref, o_ref, *scratch):
        p = lax.dot_general(a_ref[...].astype(BF16), b_ref[...].astype(BF16), dims,
                            preferred_element_type=F32)
        if nk == 1:
            o_ref[...] = p.astype(o_ref.dtype)
        else:
            acc = scratch[0]
            k = pl.program_id(2)

            @pl.when(k == 0)
            def _():
                acc[...] = p

            @pl.when(k > 0)
            def _():
                acc[...] += p

            @pl.when(k == nk - 1)
            def _():
                o_ref[...] = acc[...].astype(o_ref.dtype)

    a_spec = (pl.BlockSpec((tk, tm), lambda i, j, k: (k, i)) if ta
              else pl.BlockSpec((tm, tk), lambda i, j, k: (i, k)))
    b_spec = (pl.BlockSpec((tn, tk), lambda i, j, k: (j, k + b_k_off)) if tb
              else pl.BlockSpec((tk, tn), lambda i, j, k: (k + b_k_off, j)))
    return pl.pallas_call(
        body, name=name, grid=(m // tm, n // tn, nk),
        in_specs=[a_spec, b_spec],
        out_specs=pl.BlockSpec((tm, tn), lambda i, j, k: (i, j)),
        out_shape=_sds((m, n), out_dtype),
        scratch_shapes=[pltpu.VMEM((tm, tn), F32)] if nk > 1 else [],
        compiler_params=_params("parallel", "parallel", "arbitrary"),
    )(a, b)


def _rms_fwd(x, g, *, name, tm=512):
    def body(x_ref, g_ref, h_ref):
        xv = x_ref[...]
        r = lax.rsqrt(jnp.mean(xv * xv, axis=-1, keepdims=True) + RMS_EPS)
        h_ref[...] = (xv * r * g_ref[...]).astype(h_ref.dtype)

    return pl.pallas_call(
        body, name=name, grid=(SEQ // tm,),
        in_specs=[pl.BlockSpec((tm, D_MODEL), lambda i: (i, 0)), pl.BlockSpec((1, D_MODEL), lambda i: (0, 0))],
        out_specs=pl.BlockSpec((tm, D_MODEL), lambda i: (i, 0)),
        out_shape=_sds((SEQ, D_MODEL), BF16),
        compiler_params=_params("parallel"),
    )(x, g)


def _rms_bwd(dh_parts, xin, g, dres, *, out_dtype, name, tm=512):
    n_parts = len(dh_parts)
    has_res = dres is not None

    def body(*refs):
        parts = refs[:n_parts]
        x_ref, g_ref = refs[n_parts], refs[n_parts + 1]
        res_ref = refs[n_parts + 2] if has_res else None
        o_ref, gg_ref = refs[-2], refs[-1]
        dh = parts[0][...].astype(F32)
        for p in parts[1:]:
            dh = dh + p[...].astype(F32)
        xv = x_ref[...]
        r = lax.rsqrt(jnp.mean(xv * xv, axis=-1, keepdims=True) + RMS_EPS)
        xn = xv * r

        @pl.when(pl.program_id(0) == 0)
        def _():
            gg_ref[...] = jnp.zeros_like(gg_ref)

        gg_ref[...] += jnp.sum(dh * xn, axis=0, keepdims=True)
        dxn = dh * g_ref[...]
        dx = r * (dxn - xn * jnp.mean(dxn * xn, axis=-1, keepdims=True))
        if has_res:
            dx = dx + res_ref[...]
        o_ref[...] = dx.astype(o_ref.dtype)

    row = pl.BlockSpec((tm, D_MODEL), lambda i: (i, 0))
    vec = pl.BlockSpec((1, D_MODEL), lambda i: (0, 0))
    args = list(dh_parts) + [xin, g] + ([dres] if has_res else [])
    return pl.pallas_call(
        body, name=name, grid=(SEQ // tm,),
        in_specs=[row] * n_parts + [row, vec] + ([row] if has_res else []),
        out_specs=[row, vec],
        out_shape=[_sds((SEQ, D_MODEL), out_dtype), _sds((1, D_MODEL), F32)],
        compiler_params=_params("arbitrary"),
    )(*args)


def _rms_pair_bwd(dh_parts, x2, g_pre, dres, y1, g_post, *, tm=512):
    n_parts = len(dh_parts)

    def norm_bwd(dh, xin, g_ref, gg_ref):
        r = lax.rsqrt(jnp.mean(xin * xin, axis=-1, keepdims=True) + RMS_EPS)
        xn = xin * r
        gg_ref[...] += jnp.sum(dh * xn, axis=0, keepdims=True)
        dxn = dh * g_ref[...]
        return r * (dxn - xn * jnp.mean(dxn * xn, axis=-1, keepdims=True))

    def body(*refs):
        parts = refs[:n_parts]
        x2_ref, gpre_ref, res_ref, y1_ref, gpost_ref, dx2_ref, dy1_ref, ggpre_ref, ggpost_ref = refs[n_parts:]

        @pl.when(pl.program_id(0) == 0)
        def _():
            ggpre_ref[...] = jnp.zeros_like(ggpre_ref)
            ggpost_ref[...] = jnp.zeros_like(ggpost_ref)

        dh = parts[0][...].astype(F32)
        for p in parts[1:]:
            dh = dh + p[...].astype(F32)
        dx2 = res_ref[...] + norm_bwd(dh, x2_ref[...], gpre_ref, ggpre_ref)
        dx2_ref[...] = dx2
        dy1_ref[...] = norm_bwd(dx2, y1_ref[...], gpost_ref, ggpost_ref).astype(dy1_ref.dtype)

    row = pl.BlockSpec((tm, D_MODEL), lambda i: (i, 0))
    vec = pl.BlockSpec((1, D_MODEL), lambda i: (0, 0))
    return pl.pallas_call(
        body, name="rms_pair_bwd", grid=(SEQ // tm,),
        in_specs=[row] * n_parts + [row, vec, row, row, vec],
        out_specs=[row, row, vec, vec],
        out_shape=[_sds((SEQ, D_MODEL), F32), _sds((SEQ, D_MODEL), BF16), _sds((1, D_MODEL), F32),
                   _sds((1, D_MODEL), F32)],
        compiler_params=_params("arbitrary"),
    )(*dh_parts, x2, g_pre, dres, y1, g_post)


SCAN_BLK = 512


def _split_dot(v, tri):
    hi = v.astype(BF16)
    r1 = v - hi.astype(F32)
    mid = r1.astype(BF16)
    lo = (r1 - mid.astype(F32)).astype(BF16)
    dot = functools.partial(jnp.dot, preferred_element_type=F32)
    return dot(hi, tri) + dot(mid, tri) + dot(lo, tri)


def _fox_prep(fa_t, b_col):
    nblk = SEQ // SCAN_BLK

    def body(fa_ref, b_ref, f_ref, sg_ref):
        row = lax.broadcasted_iota(jnp.int32, (SCAN_BLK, SCAN_BLK), 0)
        col = lax.broadcasted_iota(jnp.int32, (SCAN_BLK, SCAN_BLK), 1)
        upper = (row <= col).astype(BF16)
        carry = jnp.zeros((N_HEADS, 1), F32)
        for blk in range(nblk):
            sl = pl.ds(blk * SCAN_BLK, SCAN_BLK)
            xx = fa_ref[:, sl] + b_ref[...]
            e = jnp.exp(-jnp.abs(xx))
            logf = jnp.minimum(xx, 0.0) - jnp.log(1.0 + e)
            sg_ref[:, sl] = jnp.where(xx >= 0.0, e, 1.0) / (1.0 + e)
            c = _split_dot(logf, upper) + carry
            f_ref[:, sl] = c
            carry = c[:, SCAN_BLK - 1:SCAN_BLK]

    return pl.pallas_call(
        body, name="fox_prep",
        out_shape=[_sds((N_HEADS, SEQ), F32), _sds((N_HEADS, SEQ), F32)],
        compiler_params=pltpu.CompilerParams(vmem_limit_bytes=VMEM_LIMIT),
    )(fa_t, b_col)


def _fox_post_bwd(df_t, sg_t):
    nblk = SEQ // SCAN_BLK

    def body(df_ref, sg_ref, dfa_ref, gb_ref):
        row = lax.broadcasted_iota(jnp.int32, (SCAN_BLK, SCAN_BLK), 0)
        col = lax.broadcasted_iota(jnp.int32, (SCAN_BLK, SCAN_BLK), 1)
        lower = (row >= col).astype(BF16)
        carry = jnp.zeros((N_HEADS, 1), F32)
        gb = jnp.zeros((N_HEADS, 1), F32)
        for blk in reversed(range(nblk)):
            sl = pl.ds(blk * SCAN_BLK, SCAN_BLK)
            c = _split_dot(df_ref[:, sl], lower) + carry
            carry = c[:, 0:1]
            dfa = c * sg_ref[:, sl]
            dfa_ref[:, sl] = dfa
            gb = gb + jnp.sum(dfa, axis=1, keepdims=True)
        gb_ref[...] = gb

    return pl.pallas_call(
        body, name="fox_post_bwd",
        out_shape=[_sds((N_HEADS, SEQ), F32), _sds((N_HEADS, 1), F32)],
        compiler_params=pltpu.CompilerParams(vmem_limit_bytes=VMEM_LIMIT),
    )(df_t, sg_t)


FOX_T = 512
NT_DIMS = (((1,), (1,)), ((), ()))
TN_DIMS = (((0,), (0,)), ((), ()))


def _head(ref_or_val, h):
    return ref_or_val[:, h * HEAD_DIM:(h + 1) * HEAD_DIM]


def _split3(v):
    hi = v.astype(BF16).astype(F32)
    r1 = v - hi
    mid = r1.astype(BF16).astype(F32)
    return hi, mid, (r1 - mid).astype(BF16).astype(F32)


ONE_LANE = 3 * N_HEADS


def _pack_terms(v, with_one):
    hi, mid, lo = _split3(v)
    t = hi + pltpu.roll(mid, N_HEADS, 1) + pltpu.roll(lo, 2 * N_HEADS, 1)
    if with_one:
        t = t + (lax.broadcasted_iota(jnp.int32, v.shape, 1) == ONE_LANE).astype(F32)
    return t.astype(BF16)


def _aux_matrices():
    to_q = np.zeros((LANE, N_HEADS * 2 * HEAD_DIM), np.float32)
    to_k = np.zeros_like(to_q)
    for h in range(N_HEADS):
        base = h * 2 * HEAD_DIM + HEAD_DIM
        for s in range(3):
            to_q[s * N_HEADS + h, base + s] = 1.0
            to_q[ONE_LANE, base + 3 + s] = 1.0
            to_k[ONE_LANE, base + s] = 1.0
            to_k[s * N_HEADS + h, base + 3 + s] = -1.0
    return jnp.asarray(to_q, BF16), jnp.asarray(to_k, BF16)


def _head_sums():
    total = np.zeros((N_HEADS * HEAD_DIM, LANE), np.float32)
    first = np.zeros_like(total)
    for h in range(N_HEADS):
        total[h * HEAD_DIM:(h + 1) * HEAD_DIM, h] = 1.0
        first[h * HEAD_DIM, h] = 1.0
    return jnp.asarray(total, BF16), jnp.asarray(first, BF16)


SLOT = 2 * HEAD_DIM
N_SPLIT = 3
FOX_FWD_HEADS = 8
FOX_BWD_HEADS = 4


def _slot(ref, h):
    return ref[:, h * SLOT:(h + 1) * SLOT]


def _fox_pack_fwd(zm, f_cols, *, tm=512):
    def body(q_ref, k_ref, v_ref, f_ref, tq_ref, tk_ref, qs_ref, ks_ref, vs_ref):
        ones = jnp.ones((tm, HEAD_DIM), BF16)
        terms = _pack_terms(f_ref[...], True)
        q_aux = jnp.dot(terms, tq_ref[...], preferred_element_type=F32).astype(BF16)
        k_aux = jnp.dot(terms, tk_ref[...], preferred_element_type=F32).astype(BF16)
        for h in range(N_HEADS):
            aux = slice(h * SLOT + HEAD_DIM, (h + 1) * SLOT)
            qs_ref[:, h * SLOT:(h + 1) * SLOT] = jnp.concatenate(
                [(_head(q_ref, h).astype(F32) * SCALE).astype(BF16), q_aux[:, aux]], axis=1)
            ks_ref[:, h * SLOT:(h + 1) * SLOT] = jnp.concatenate([_head(k_ref, h), k_aux[:, aux]], axis=1)
            vs_ref[:, h * SLOT:(h + 1) * SLOT] = jnp.concatenate([_head(v_ref, h), ones], axis=1)

    col = lambda b: pl.BlockSpec((tm, ATT_W), lambda i: (i, b))
    wide = pl.BlockSpec((tm, N_HEADS * SLOT), lambda i: (i, 0))
    const = pl.BlockSpec((LANE, N_HEADS * SLOT), lambda i: (0, 0))
    return pl.pallas_call(
        body, name="fox_pack_fwd", grid=(SEQ // tm,),
        in_specs=[col(0), col(1), col(2), pl.BlockSpec((tm, LANE), lambda i: (i, 0)), const, const],
        out_specs=[wide] * 3, out_shape=[_sds((SEQ, N_HEADS * SLOT), BF16)] * 3,
        compiler_params=_params("parallel"),
    )(zm, zm, zm, f_cols, *_aux_matrices())


def _fox_pack_bwd(zm, f_cols, lse, o, do, *, tm=512):
    def body(q_ref, f_ref, lse_ref, o_ref, do_ref, tq_ref, total_ref, first_ref, qs_ref, ds_ref):
        delta = _split_dot(o_ref[...].astype(F32) * do_ref[...].astype(F32), total_ref[...])
        lse_h = _split_dot(lse_ref[...], first_ref[...])
        q_aux = jnp.dot(_pack_terms(f_ref[...] - lse_h, True), tq_ref[...], preferred_element_type=F32).astype(BF16)
        d_aux = jnp.dot(_pack_terms(-delta, False), tq_ref[...], preferred_element_type=F32).astype(BF16)
        for h in range(N_HEADS):
            aux = slice(h * SLOT + HEAD_DIM, (h + 1) * SLOT)
            qs_ref[:, h * SLOT:(h + 1) * SLOT] = jnp.concatenate(
                [(_head(q_ref, h).astype(F32) * SCALE).astype(BF16), q_aux[:, aux]], axis=1)
            ds_ref[:, h * SLOT:(h + 1) * SLOT] = jnp.concatenate([_head(do_ref, h), d_aux[:, aux]], axis=1)

    row = pl.BlockSpec((tm, ATT_W), lambda i: (i, 0))
    wide = pl.BlockSpec((tm, N_HEADS * SLOT), lambda i: (i, 0))
    const = lambda r, c: pl.BlockSpec((r, c), lambda i: (0, 0))
    return pl.pallas_call(
        body, name="fox_pack_bwd", grid=(SEQ // tm,),
        in_specs=[row, pl.BlockSpec((tm, LANE), lambda i: (i, 0)), row, row, row,
                  const(LANE, N_HEADS * SLOT), const(ATT_W, LANE), const(ATT_W, LANE)],
        out_specs=[wide] * 2, out_shape=[_sds((SEQ, N_HEADS * SLOT), BF16)] * 2,
        compiler_params=_params("parallel"),
    )(zm, f_cols, lse, o, do, _aux_matrices()[0], *_head_sums())


def _causal_pairs(key_major):
    nb = SEQ // FOX_T
    if key_major:
        pairs = [(i, j) for j in range(nb) for i in range(j, nb)]
    else:
        pairs = [(i, j) for i in range(nb) for j in range(i + 1)]
    return (jnp.array([p[0] for p in pairs], jnp.int32), jnp.array([p[1] for p in pairs], jnp.int32), len(pairs))


FOX_HALF = FOX_T // 2
FOX_FULL = ((slice(0, FOX_T), slice(0, FOX_T), None),)
FOX_DIAG = ((slice(0, FOX_HALF), slice(0, FOX_HALF), 0), (slice(FOX_HALF, FOX_T), slice(0, FOX_T), FOX_HALF))


def _causal_piece_mask(q_rows, k_rows, offset):
    shape = (q_rows.stop - q_rows.start, k_rows.stop - k_rows.start)
    row = lax.broadcasted_iota(jnp.int32, shape, 0)
    col = lax.broadcasted_iota(jnp.int32, shape, 1)
    return col <= row + offset


def _fox_fwd(q_slots, k_slots, v_slots):
    i_tab, j_tab, n_pairs = _causal_pairs(False)

    def body(i_tab, j_tab, q_ref, k_ref, v_ref, o_ref, lse_ref, m_s, acc_s):
        t = pl.program_id(1)
        i, j = i_tab[t], j_tab[t]

        @pl.when(j == 0)
        def _():
            m_s[...] = jnp.full_like(m_s, NEG_INF)
            acc_s[...] = jnp.zeros_like(acc_s)

        def step(pieces):
            jobs = [(h, piece) for h in range(FOX_FWD_HEADS) for piece in pieces]
            lanes = lambda h: slice(h * SLOT, (h + 1) * SLOT)
            scores = [lax.dot_general(q_ref[qr, lanes(h)], k_ref[kr, lanes(h)], NT_DIMS, preferred_element_type=F32)
                      for h, (qr, kr, _) in jobs]
            probs, alphas = [], []
            for idx, (h, (qr, kr, offset)) in enumerate(jobs):
                s = scores[idx]
                if offset is not None:
                    s = jnp.where(_causal_piece_mask(qr, kr, offset), s, NEG_INF)
                m_prev = m_s[h, qr, :]
                m_new = jnp.maximum(m_prev, jnp.max(s, axis=-1, keepdims=True))
                probs.append(jnp.exp(s - jnp.tile(m_new, (1, s.shape[1] // LANE))).astype(BF16))
                alphas.append(jnp.exp(m_prev - m_new))
                m_s[h, qr, :] = m_new
            for idx, (h, (qr, kr, _)) in enumerate(jobs):
                acc_s[h, qr, :] = alphas[idx] * acc_s[h, qr, :] + jnp.dot(
                    probs[idx], v_ref[kr, lanes(h)], preferred_element_type=F32)

        @pl.when(j < i)
        def _():
            step(FOX_FULL)

        @pl.when(j == i)
        def _():
            step(FOX_DIAG)
            outs, lses = [], []
            for h in range(FOX_FWD_HEADS):
                acc = acc_s[h]
                l = acc[:, HEAD_DIM:]
                outs.append(acc[:, :HEAD_DIM] / l)
                lses.append(m_s[h][:, :HEAD_DIM] + jnp.log(l))
            o_ref[...] = jnp.concatenate(outs, axis=1).astype(o_ref.dtype)
            lse_ref[...] = jnp.concatenate(lses, axis=1)

    qspec = pl.BlockSpec((FOX_T, FOX_FWD_HEADS * SLOT), lambda p, t, it, jt: (it[t], p))
    kspec = pl.BlockSpec((FOX_T, FOX_FWD_HEADS * SLOT), lambda p, t, it, jt: (jt[t], p))
    ospec = pl.BlockSpec((FOX_T, FOX_FWD_HEADS * HEAD_DIM), lambda p, t, it, jt: (it[t], p))
    return pl.pallas_call(
        body, name="fox_fwd",
        grid_spec=pltpu.PrefetchScalarGridSpec(
            num_scalar_prefetch=2, grid=(N_HEADS // FOX_FWD_HEADS, n_pairs),
            in_specs=[qspec, kspec, kspec], out_specs=[ospec, ospec],
            scratch_shapes=[pltpu.VMEM((FOX_FWD_HEADS, FOX_T, LANE), F32),
                            pltpu.VMEM((FOX_FWD_HEADS, FOX_T, SLOT), F32)]),
        out_shape=[_sds((SEQ, ATT_W), BF16), _sds((SEQ, ATT_W), F32)],
        compiler_params=_params("parallel", "arbitrary"),
    )(i_tab, j_tab, q_slots, k_slots, v_slots)


def _fox_bwd(q_slots, k_slots, v_slots, do_slots):
    i_tab, j_tab, n_pairs = _causal_pairs(True)

    def body(i_tab, j_tab, q_ref, k_ref, v_ref, do_ref, dq_ref, dk_ref, dv_ref):
        t = pl.program_id(1)
        i, j = i_tab[t], j_tab[t]

        @pl.when(t == 0)
        def _():
            dq_ref[...] = jnp.zeros_like(dq_ref)

        @pl.when(i == j)
        def _():
            dk_ref[...] = jnp.zeros_like(dk_ref)
            dv_ref[...] = jnp.zeros_like(dv_ref)

        def step(pieces):
            jobs = [(h, piece) for h in range(FOX_BWD_HEADS) for piece in pieces]
            lanes = lambda h: slice(h * SLOT, (h + 1) * SLOT)
            scores = [lax.dot_general(q_ref[qr, lanes(h)], k_ref[kr, lanes(h)], NT_DIMS, preferred_element_type=F32)
                      for h, (qr, kr, _) in jobs]
            dps = [lax.dot_general(do_ref[qr, lanes(h)], v_ref[kr, lanes(h)], NT_DIMS, preferred_element_type=F32)
                   for h, (qr, kr, _) in jobs]
            ps, dss = [], []
            for idx, (h, (qr, kr, offset)) in enumerate(jobs):
                p = jnp.exp(scores[idx])
                if offset is not None:
                    p = jnp.where(_causal_piece_mask(qr, kr, offset), p, 0.0)
                ps.append(p.astype(BF16))
                dss.append((p * dps[idx]).astype(BF16))
            for idx, (h, (qr, kr, _)) in enumerate(jobs):
                rows = pl.ds(pl.multiple_of(i * FOX_T + qr.start, FOX_HALF), qr.stop - qr.start)
                dv_ref[kr, lanes(h)] += lax.dot_general(ps[idx], do_ref[qr, lanes(h)], TN_DIMS,
                                                        preferred_element_type=F32)
                dk_ref[kr, lanes(h)] += lax.dot_general(dss[idx], q_ref[qr, lanes(h)], TN_DIMS,
                                                        preferred_element_type=F32)
                dq_ref[rows, lanes(h)] += jnp.dot(dss[idx], k_ref[kr, lanes(h)], preferred_element_type=F32)

        @pl.when(i > j)
        def _():
            step(FOX_FULL)

        @pl.when(i == j)
        def _():
            step(FOX_DIAG)

    qspec = pl.BlockSpec((FOX_T, FOX_BWD_HEADS * SLOT), lambda p, t, it, jt: (it[t], p))
    kspec = pl.BlockSpec((FOX_T, FOX_BWD_HEADS * SLOT), lambda p, t, it, jt: (jt[t], p))
    return pl.pallas_call(
        body, name="fox_bwd",
        grid_spec=pltpu.PrefetchScalarGridSpec(
            num_scalar_prefetch=2, grid=(N_HEADS // FOX_BWD_HEADS, n_pairs),
            in_specs=[qspec, kspec, kspec, qspec],
            out_specs=[pl.BlockSpec((SEQ, FOX_BWD_HEADS * SLOT), lambda p, t, it, jt: (0, p)), kspec, kspec]),
        out_shape=[_sds((SEQ, N_HEADS * SLOT), F32)] * 3,
        compiler_params=_params("arbitrary", "arbitrary"),
    )(i_tab, j_tab, q_slots, k_slots, v_slots, do_slots)


def _fox_unpack(dq_slots, dk_slots, dv_slots, dz, *, tm=512):
    def body(dq_ref, dk_ref, dv_ref, dz_in, o_ref, df_ref):
        lane = lax.broadcasted_iota(jnp.int32, (tm, LANE), 1)
        df = jnp.zeros((tm, LANE), F32)
        for h in range(N_HEADS):
            lo = h * SLOT
            for part, (ref, mult) in enumerate(((dq_ref, SCALE), (dk_ref, 1.0), (dv_ref, 1.0))):
                o_ref[:, part * ATT_W + h * HEAD_DIM:part * ATT_W + (h + 1) * HEAD_DIM] = (
                    ref[:, lo:lo + HEAD_DIM] * mult).astype(o_ref.dtype)
            rows = dq_ref[:, lo + HEAD_DIM:lo + HEAD_DIM + 1]
            cols = dk_ref[:, lo + HEAD_DIM + N_SPLIT:lo + HEAD_DIM + N_SPLIT + 1]
            df = jnp.where(lane == h, rows - cols, df)
        df_ref[...] = df

    wide = pl.BlockSpec((tm, N_HEADS * SLOT), lambda i: (i, 0))
    return pl.pallas_call(
        body, name="fox_unpack", grid=(SEQ // tm,), in_specs=[wide] * 3 + [ANY],
        out_specs=[pl.BlockSpec((tm, 3 * ATT_W), lambda i: (i, 0)), pl.BlockSpec((tm, LANE), lambda i: (i, 0))],
        out_shape=[_sds((SEQ, Z_MAIN), BF16), _sds((SEQ, LANE), F32)],
        input_output_aliases={3: 0},
        compiler_params=_params("parallel"),
    )(dq_slots, dk_slots, dv_slots, dz)


def _dil_bwd_prep(o, do, lse, *, tm=512):
    dilations = [d for _, d in DIL_PATTERNS]
    o_chunks = ATT_W // LANE

    def body(o_ref, do_ref, lse_ref, *rest):
        outs, (do_scr, lse_scr, dl_scr) = rest[:-3], rest[-3:]
        dov = do_ref[...].astype(F32)
        prod = o_ref[...].astype(F32) * dov
        lane = lax.broadcasted_iota(jnp.int32, (tm, LANE), 1)
        delta = jnp.zeros((tm, LANE), F32)
        for h in range(N_HEADS):
            delta = jnp.where(lane == h, jnp.sum(_head(prod, h), axis=1, keepdims=True), delta)
        for ch in range(o_chunks):
            do_scr[ch] = dov[:, ch * LANE:(ch + 1) * LANE]
        lse_scr[0] = lse_ref[...]
        dl_scr[0] = delta
        for k, d in enumerate(dilations):
            for scr, out in zip((do_scr, lse_scr, dl_scr), outs[3 * k:3 * k + 3]):
                _slabs_from_rows(scr, out, d)

    row = pl.BlockSpec((tm, ATT_W), lambda i: (i, 0))
    view = lambda d, w: pl.BlockSpec((tm // d, d * w), lambda i: (i, 0))
    outs = pl.pallas_call(
        body, name="dil_bwd_prep", grid=(SEQ // tm,),
        in_specs=[row, row, pl.BlockSpec((tm, LANE), lambda i: (i, 0))],
        out_specs=[view(d, w) for d in dilations for w in (ATT_W, LANE, LANE)],
        out_shape=[_sds((SEQ // d, d * w), t) for d in dilations for w, t in ((ATT_W, BF16), (LANE, F32), (LANE, F32))],
        scratch_shapes=[pltpu.VMEM((o_chunks, tm, LANE), F32), pltpu.VMEM((1, tm, LANE), F32),
                        pltpu.VMEM((1, tm, LANE), F32)],
        compiler_params=_params("parallel"),
    )(o, do, lse)
    return [outs[3 * k:3 * k + 3] for k in range(len(dilations))]


def _rope_tables():
    half = ROPE_DIM // 2
    inv_freq = np.float32(ROPE_THETA) ** (-np.arange(half, dtype=np.float32) * np.float32(2.0) / np.float32(ROPE_DIM))
    ang = np.arange(SEQ, dtype=np.float32)[:, None] * inv_freq.astype(np.float32)[None, :]
    cos, sin = jnp.asarray(np.cos(ang).astype(np.float32)), jnp.asarray(np.sin(ang).astype(np.float32))
    ones = jnp.ones((SEQ, HEAD_DIM - ROPE_DIM), F32)
    zeros = jnp.zeros((SEQ, HEAD_DIM - ROPE_DIM), F32)
    zh = jnp.zeros((SEQ, half), F32)
    c_tab = jnp.concatenate([cos, cos, ones], axis=1)
    a_tab = jnp.concatenate([-sin, zh, zeros], axis=1)
    b_tab = jnp.concatenate([zh, sin, zeros], axis=1)
    two = lambda t: jnp.concatenate([t, t], axis=1)
    return two(c_tab), two(a_tab), two(b_tab)


def _rotate(x, c_tab, a_tab, b_tab):
    return x * c_tab + pltpu.roll(x, LANE - ROPE_DIM // 2, 1) * a_tab + pltpu.roll(x, ROPE_DIM // 2, 1) * b_tab


def _rope_fwd(zm, tabs, *, tm=512):
    width = 3 * ATT_W
    dilations = [d for _, d in DIL_PATTERNS]

    def body(q_ref, k_ref, v_ref, c_ref, a_ref, b_ref, *rest):
        outs, scr = rest[:-1], rest[-1]
        per_part = ATT_W // LANE
        for part, (x_ref, mult) in enumerate(((q_ref, SCALE), (k_ref, 1.0))):
            for cc in range(per_part):
                sl = slice(cc * LANE, (cc + 1) * LANE)
                scr[part * per_part + cc] = _rotate(x_ref[:, sl].astype(F32), c_ref[...], a_ref[...], b_ref[...]) * mult
        for cc in range(per_part):
            scr[2 * per_part + cc] = v_ref[:, cc * LANE:(cc + 1) * LANE].astype(F32)
        for o_ref, d in zip(outs, dilations):
            for r in range(d):
                for ch in range(width // LANE):
                    o_ref[:, r * width + ch * LANE:r * width + (ch + 1) * LANE] = (
                        scr.at[ch][pl.ds(r, tm // d, stride=d), :].astype(o_ref.dtype))

    tab = pl.BlockSpec((tm, LANE), lambda i: (i, 0))
    col = lambda b: pl.BlockSpec((tm, ATT_W), lambda i: (i, b))
    return pl.pallas_call(
        body, name="rope_fwd", grid=(SEQ // tm,),
        in_specs=[col(3), col(4), col(5), tab, tab, tab],
        out_specs=[pl.BlockSpec((tm // d, d * width), lambda i: (i, 0)) for d in dilations],
        out_shape=[_sds((SEQ // d, d * width), BF16) for d in dilations],
        scratch_shapes=[pltpu.VMEM((width // LANE, tm, LANE), F32)],
        compiler_params=_params("parallel"),
    )(zm, zm, zm, *tabs)


def _dil_grad_combine(dqs, dks, dvs, tabs, dz, *, tm=256):
    dilations = [d for _, d in DIL_PATTERNS]
    chunks = ATT_W // LANE

    def body(*refs):
        groups = (refs[0:3], refs[3:6], refs[6:9])
        c_ref, a_ref, b_ref, _, o_ref, scr = refs[9:]

        def total(part, cc):
            acc = None
            for g, (ref, d) in enumerate(zip(groups[part], dilations)):
                term = ref[:, cc * LANE:(cc + 1) * LANE].astype(F32) if d == 1 else scr[part, g, cc]
                acc = term if acc is None else acc + term
            return acc

        for part in range(3):
            for g, (ref, d) in enumerate(zip(groups[part], dilations)):
                if d > 1:
                    _rows_from_slabs(ref, scr.at[part, g], d)
        for cc in range(chunks):
            for part in range(2):
                o_ref[:, part * ATT_W + cc * LANE:part * ATT_W + (cc + 1) * LANE] = _rotate(
                    total(part, cc), c_ref[...], -a_ref[...], -b_ref[...]).astype(o_ref.dtype)
            o_ref[:, 2 * ATT_W + cc * LANE:2 * ATT_W + (cc + 1) * LANE] = total(2, cc).astype(o_ref.dtype)

    view = lambda d: pl.BlockSpec((tm // d, d * ATT_W), lambda i: (i, 0))
    tab = pl.BlockSpec((tm, LANE), lambda i: (i, 0))
    return pl.pallas_call(
        body, name="dil_grad_combine", grid=(SEQ // tm,),
        in_specs=[view(d) for d in dilations] * 3 + [tab] * 3 + [ANY],
        out_specs=pl.BlockSpec((tm, 3 * ATT_W), lambda i: (i, 1)),
        out_shape=_sds((SEQ, Z_MAIN), BF16),
        input_output_aliases={12: 0},
        scratch_shapes=[pltpu.VMEM((3, len(dilations), chunks, tm, LANE), F32)],
        compiler_params=_params("parallel"),
    )(*dqs, *dks, *dvs, *tabs, dz)


def _dil_valid(n):
    qi = lax.broadcasted_iota(jnp.int32, (DIL_BLK, 2 * DIL_BLK), 0)
    ki = lax.broadcasted_iota(jnp.int32, (DIL_BLK, 2 * DIL_BLK), 1)
    dist = qi + DIL_BLK - ki
    return (dist >= 0) & (dist <= DIL_BLK) & ((n > 0) | (ki >= DIL_BLK))


def _dil_fwd(qkv_v, d):
    length = SEQ // d
    nb = length // DIL_BLK
    nsub = min(DIL_STEP_BLOCKS, nb)

    def body(q_ref, kp_ref, kc_ref, vp_ref, vc_ref, o_ref, lse_ref):
        m_step = pl.program_id(1)
        lane = lax.broadcasted_iota(jnp.int32, (DIL_BLK, LANE), 1)
        jobs = [(sub, h) for sub in range(nsub) for h in range(N_HEADS)]
        rows = lambda sub: slice(sub * DIL_BLK, (sub + 1) * DIL_BLK)
        cols = lambda h: slice(h * HEAD_DIM, (h + 1) * HEAD_DIM)

        def keys(prev_ref, cur_ref, sub, h):
            before = prev_ref[:, cols(h)] if sub == 0 else cur_ref[rows(sub - 1), cols(h)]
            return jnp.concatenate([before, cur_ref[rows(sub), cols(h)]], axis=0)

        scores = [lax.dot_general(q_ref[rows(sub), cols(h)], keys(kp_ref, kc_ref, sub, h), NT_DIMS,
                                  preferred_element_type=F32) for sub, h in jobs]
        ok = [_dil_valid(m_step)] + [_dil_valid(1)] * (nsub - 1)
        probs, inv_l, lse_all = [], [], [jnp.zeros((DIL_BLK, LANE), F32)] * nsub
        for idx, (sub, h) in enumerate(jobs):
            s = jnp.where(ok[sub], scores[idx], NEG_INF)
            m = jnp.max(s, axis=-1, keepdims=True)
            p = jnp.exp(s - m)
            l = jnp.sum(p, axis=-1, keepdims=True)
            probs.append(p.astype(BF16))
            inv_l.append(1.0 / l)
            lse_all[sub] = jnp.where(lane == h, m + jnp.log(l), lse_all[sub])
        outs = [jnp.dot(probs[idx], keys(vp_ref, vc_ref, sub, h), preferred_element_type=F32) * inv_l[idx]
                for idx, (sub, h) in enumerate(jobs)]
        for sub in range(nsub):
            o_ref[rows(sub), :] = jnp.concatenate(outs[sub * N_HEADS:(sub + 1) * N_HEADS], axis=1).astype(o_ref.dtype)
            lse_ref[rows(sub), :] = lse_all[sub]

    pair = lambda f: pl.BlockSpec((nsub * DIL_BLK, ATT_W), f)
    one = lambda f: pl.BlockSpec((DIL_BLK, ATT_W), f)
    before = lambda m: jnp.maximum(nsub * m - 1, 0)
    o, lse = pl.pallas_call(
        body, name=f"dil_fwd_d{d}", grid=(d, nb // nsub),
        in_specs=[pair(lambda r, m: (m, 3 * r)),
                  one(lambda r, m: (before(m), 3 * r + 1)), pair(lambda r, m: (m, 3 * r + 1)),
                  one(lambda r, m: (before(m), 3 * r + 2)), pair(lambda r, m: (m, 3 * r + 2))],
        out_specs=[pair(lambda r, m: (m, r)), pl.BlockSpec((nsub * DIL_BLK, LANE), lambda r, m: (m, r))],
        out_shape=[_sds((length, d * ATT_W), BF16), _sds((length, d * LANE), F32)],
        compiler_params=_params("parallel", "arbitrary"),
    )(qkv_v, qkv_v, qkv_v, qkv_v, qkv_v)
    return o, lse


def _rows_from_slabs(view_ref, scr, d):
    chunks, rows = scr.shape[0], scr.shape[1]
    for r in range(d):
        for ch in range(chunks):
            lo = (r * chunks + ch) * LANE
            scr.at[ch][pl.ds(r, rows // d, stride=d), :] = view_ref[:, lo:lo + LANE].astype(F32)


def _slabs_from_rows(scr, view_ref, d):
    chunks, rows = scr.shape[0], scr.shape[1]
    for r in range(d):
        for ch in range(chunks):
            lo = (r * chunks + ch) * LANE
            view_ref[:, lo:lo + LANE] = scr.at[ch][pl.ds(r, rows // d, stride=d), :].astype(view_ref.dtype)


def _dil_merge(os_, lses, *, tm=512):
    dilations = [d for _, d in DIL_PATTERNS]
    o_chunks = ATT_W // LANE

    def body(o0, o1, o2, l0, l1, l2, y_ref, lse_ref, o_scr, l_scr):
        os_nat, ls = [], []
        for g, (o_ref, l_ref, d) in enumerate(zip((o0, o1, o2), (l0, l1, l2), dilations)):
            if d == 1:
                os_nat.append(o_ref[...].astype(F32))
                ls.append(l_ref[...])
            else:
                _rows_from_slabs(o_ref, o_scr.at[g], d)
                _rows_from_slabs(l_ref, l_scr.at[g], d)
                os_nat.append(jnp.concatenate([o_scr[g, ch] for ch in range(o_chunks)], axis=1))
                ls.append(l_scr[g, 0])
        m = jnp.maximum(jnp.maximum(ls[0], ls[1]), ls[2])
        es = [jnp.exp(l - m) for l in ls]
        tot = es[0] + es[1] + es[2]
        lse_ref[...] = m + jnp.log(tot)
        alphas = [e / tot for e in es]
        outs = []
        for h in range(N_HEADS):
            acc = None
            for g in range(3):
                term = alphas[g][:, h:h + 1] * _head(os_nat[g], h)
                acc = term if acc is None else acc + term
            outs.append(acc)
        y_ref[...] = jnp.concatenate(outs, axis=1).astype(y_ref.dtype)

    row = pl.BlockSpec((tm, ATT_W), lambda i: (i, 0))
    vec = pl.BlockSpec((tm, LANE), lambda i: (i, 0))
    view = lambda d, w: pl.BlockSpec((tm // d, d * w), lambda i: (i, 0))
    return pl.pallas_call(
        body, name="dil_merge", grid=(SEQ // tm,),
        in_specs=[view(d, ATT_W) for d in dilations] + [view(d, LANE) for d in dilations], out_specs=[row, vec],
        out_shape=[_sds((SEQ, ATT_W), BF16), _sds((SEQ, LANE), F32)],
        scratch_shapes=[pltpu.VMEM((3, o_chunks, tm, LANE), F32), pltpu.VMEM((3, 1, tm, LANE), F32)],
        compiler_params=_params("parallel"),
    )(*os_, *lses)


def _dil_bwd(qkv_v, do_v, lse_v, dl_v, d):
    length = SEQ // d
    nb = length // DIL_BLK
    nsub = min(DIL_STEP_BLOCKS, nb)
    n_steps = nb // nsub

    def body(q_ref, kp_ref, kc_ref, vp_ref, vc_ref, lse_ref, dl_ref, do_ref, dq_ref, dk_ref, dv_ref, dk_s, dv_s):
        m_step = pl.program_id(1)

        @pl.when(m_step == 0)
        def _():
            dk_s[...] = jnp.zeros_like(dk_s)
            dv_s[...] = jnp.zeros_like(dv_s)

        jobs = [(sub, h) for sub in range(nsub) for h in range(N_HEADS)]
        rows = lambda sub: slice(sub * DIL_BLK, (sub + 1) * DIL_BLK)
        cols = lambda h: slice(h * HEAD_DIM, (h + 1) * HEAD_DIM)

        def keys(prev_ref, cur_ref, sub, h):
            before = prev_ref[:, cols(h)] if sub == 0 else cur_ref[rows(sub - 1), cols(h)]
            return jnp.concatenate([before, cur_ref[rows(sub), cols(h)]], axis=0)

        kks = [keys(kp_ref, kc_ref, sub, h) for sub, h in jobs]
        scores = [lax.dot_general(q_ref[rows(sub), cols(h)], kks[idx], NT_DIMS, preferred_element_type=F32)
                  for idx, (sub, h) in enumerate(jobs)]
        dps = [lax.dot_general(do_ref[rows(sub), cols(h)], keys(vp_ref, vc_ref, sub, h), NT_DIMS,
                               preferred_element_type=F32) for sub, h in jobs]
        ok = [_dil_valid(m_step)] + [_dil_valid(1)] * (nsub - 1)
        ps, dss = [], []
        for idx, (sub, h) in enumerate(jobs):
            p = jnp.where(ok[sub], jnp.exp(scores[idx] - lse_ref[rows(sub), h:h + 1]), 0.0)
            ps.append(p.astype(BF16))
            dss.append((p * (dps[idx] - dl_ref[rows(sub), h:h + 1])).astype(BF16))
        dqs = [jnp.dot(dss[idx], kks[idx], preferred_element_type=F32) * SCALE for idx in range(len(jobs))]
        dkks = [lax.dot_general(dss[idx], q_ref[rows(sub), cols(h)], TN_DIMS, preferred_element_type=F32)
                for idx, (sub, h) in enumerate(jobs)]
        dvvs = [lax.dot_general(ps[idx], do_ref[rows(sub), cols(h)], TN_DIMS, preferred_element_type=F32)
                for idx, (sub, h) in enumerate(jobs)]
        for sub in range(nsub):
            dq_ref[rows(sub), :] = jnp.concatenate(dqs[sub * N_HEADS:(sub + 1) * N_HEADS], axis=1).astype(dq_ref.dtype)
        base = m_step * (nsub * DIL_BLK)
        blocks = [pl.ds(pl.multiple_of(jnp.maximum(base - DIL_BLK, 0), DIL_BLK), DIL_BLK)]
        blocks += [pl.ds(pl.multiple_of(base + s * DIL_BLK, DIL_BLK), DIL_BLK) for s in range(nsub)]
        for acc, parts in ((dk_s, dkks), (dv_s, dvvs)):
            top = lambda sub: jnp.concatenate([parts[sub * N_HEADS + h][:DIL_BLK] for h in range(N_HEADS)], axis=1)
            bottom = lambda sub: jnp.concatenate([parts[sub * N_HEADS + h][DIL_BLK:] for h in range(N_HEADS)], axis=1)
            acc[blocks[0], :] += top(0)
            for s in range(nsub):
                acc[blocks[s + 1], :] += bottom(s) + top(s + 1) if s + 1 < nsub else bottom(s)

        @pl.when(m_step == n_steps - 1)
        def _():
            dk_ref[...] = dk_s[...].astype(dk_ref.dtype)
            dv_ref[...] = dv_s[...].astype(dv_ref.dtype)

    pair = lambda f: pl.BlockSpec((nsub * DIL_BLK, ATT_W), f)
    one = lambda f: pl.BlockSpec((DIL_BLK, ATT_W), f)
    vec = lambda f: pl.BlockSpec((nsub * DIL_BLK, LANE), f)
    whole = pl.BlockSpec((length, ATT_W), lambda r, m: (0, r))
    before = lambda m: jnp.maximum(nsub * m - 1, 0)
    outs = pl.pallas_call(
        body, name=f"dil_bwd_d{d}", grid=(d, n_steps),
        in_specs=[pair(lambda r, m: (m, 3 * r)),
                  one(lambda r, m: (before(m), 3 * r + 1)), pair(lambda r, m: (m, 3 * r + 1)),
                  one(lambda r, m: (before(m), 3 * r + 2)), pair(lambda r, m: (m, 3 * r + 2)),
                  vec(lambda r, m: (m, r)), vec(lambda r, m: (m, r)), pair(lambda r, m: (m, r))],
        out_specs=[pair(lambda r, m: (m, r)), whole, whole],
        out_shape=[_sds((length, d * ATT_W), BF16)] * 3,
        scratch_shapes=[pltpu.VMEM((length, ATT_W), F32), pltpu.VMEM((length, ATT_W), F32)],
        compiler_params=_params("arbitrary", "arbitrary"),
    )(qkv_v, qkv_v, qkv_v, qkv_v, qkv_v, lse_v, dl_v, do_v)
    return outs


def _sigmoid(x):
    return 1.0 / (1.0 + jnp.exp(-x))


def _mix_fwd(ya, yb, w_oa, w_ob, zm, *, tm=512):
    def body(ya_ref, yb_ref, wa_ref, wb_ref, ga_ref, gb_ref, pa_ref, pb_ref, mix_ref):
        pa = jnp.dot(ya_ref[...], wa_ref[...], preferred_element_type=F32)
        pb = jnp.dot(yb_ref[...], wb_ref[...], preferred_element_type=F32)
        pa_ref[...] = pa.astype(pa_ref.dtype)
        pb_ref[...] = pb.astype(pb_ref.dtype)
        mix_ref[...] = (_sigmoid(ga_ref[...].astype(F32)) * pa + _sigmoid(gb_ref[...].astype(F32)) * pb
                        ).astype(mix_ref.dtype)

    row = pl.BlockSpec((tm, ATT_W), lambda i: (i, 0))
    wsp = pl.BlockSpec((ATT_W, D_MODEL), lambda i: (0, 0))
    wide = pl.BlockSpec((tm, D_MODEL), lambda i: (i, 0))
    return pl.pallas_call(
        body, name="mix_fwd", grid=(SEQ // tm,),
        in_specs=[row, row, wsp, wsp, pl.BlockSpec((tm, D_MODEL), lambda i: (i, 3)),
                  pl.BlockSpec((tm, D_MODEL), lambda i: (i, 4))],
        out_specs=[wide] * 3, out_shape=[_sds((SEQ, D_MODEL), BF16)] * 3,
        compiler_params=_params("parallel"),
    )(ya, yb, w_oa, w_ob, zm, zm)


def _gate_bwd(dmix, zm, p, gate_block, dz, *, name, tm=512):
    def body(dm_ref, g_ref, p_ref, *rest):
        dp_ref, dz_ref = rest[-2], rest[-1]
        dm = dm_ref[...].astype(F32)
        s = _sigmoid(g_ref[...].astype(F32))
        dp_ref[...] = (dm * s).astype(dp_ref.dtype)
        dz_ref[...] = (dm * p_ref[...].astype(F32) * s * (1.0 - s)).astype(dz_ref.dtype)

    wide = pl.BlockSpec((tm, D_MODEL), lambda i: (i, 0))
    gate = pl.BlockSpec((tm, D_MODEL), lambda i: (i, gate_block))
    extra = [] if dz is None else [dz]
    return pl.pallas_call(
        body, name=name, grid=(SEQ // tm,),
        in_specs=[wide, gate, wide] + [ANY] * len(extra),
        out_specs=[wide, gate],
        out_shape=[_sds((SEQ, D_MODEL), BF16), _sds((SEQ, Z_MAIN), BF16)],
        input_output_aliases={3: 1} if extra else {},
        compiler_params=_params("parallel"),
    )(dmix, zm, p, *extra)


def _out_fwd(mixed, w_out, x, g_post, g_pre, *, tm=512):
    def body(m_ref, w_ref, x_ref, gp_ref, gn_ref, y_ref, x2_ref, h_ref):
        y = jnp.dot(m_ref[...], w_ref[...], preferred_element_type=F32)
        y_ref[...] = y
        r = lax.rsqrt(jnp.mean(y * y, axis=-1, keepdims=True) + RMS_EPS)
        x2 = x_ref[...] + y * r * gp_ref[...]
        x2_ref[...] = x2
        r2 = lax.rsqrt(jnp.mean(x2 * x2, axis=-1, keepdims=True) + RMS_EPS)
        h_ref[...] = (x2 * r2 * gn_ref[...]).astype(h_ref.dtype)

    row = pl.BlockSpec((tm, D_MODEL), lambda i: (i, 0))
    vec = pl.BlockSpec((1, D_MODEL), lambda i: (0, 0))
    return pl.pallas_call(
        body, name="out_fwd", grid=(SEQ // tm,),
        in_specs=[row, pl.BlockSpec((D_MODEL, D_MODEL), lambda i: (0, 0)), row, vec, vec],
        out_specs=[row] * 3,
        out_shape=[_sds((SEQ, D_MODEL), F32), _sds((SEQ, D_MODEL), F32), _sds((SEQ, D_MODEL), BF16)],
        compiler_params=_params("parallel"),
    )(mixed, w_out, x, g_post, g_pre)


FFN_HALF = 256
FFN_TN = 2 * FFN_HALF
FFN_NJ = D_FF // FFN_HALF
FFN_GROUP = 2 * SUBLANE
UP_TM = 1024


def _ffn_interleave(t):
    lead = t.shape[:-1]
    return jnp.swapaxes(t.reshape(*lead, 2, FFN_NJ, FFN_HALF), -3, -2).reshape(*lead, 2 * D_FF)


def _ffn_deinterleave(t):
    lead = t.shape[:-1]
    return jnp.swapaxes(t.reshape(*lead, FFN_NJ, 2, FFN_HALF), -3, -2).reshape(*lead, 2 * D_FF)


W_IN_SHARD = (Z_MAIN + N_HEADS) // N_DEV
FORGET_LO = 3 * ATT_W


def _w_in_from_shards(shards, *, tm=256):
    def columns(g_ref, lo, width):
        p, off = divmod(lo, W_IN_SHARD)
        if off + width <= W_IN_SHARD:
            return g_ref[p, :, off:off + width]
        first = W_IN_SHARD - off
        return jnp.concatenate([g_ref[p, :, off:], g_ref[p + 1, :, :width - first]], axis=1)

    def body(g_ref, main_ref, f_ref):
        for t in range(Z_MAIN // LANE):
            lo = t * LANE
            main_ref[:, lo:lo + LANE] = columns(g_ref, lo if lo < FORGET_LO else lo + N_HEADS, LANE)
        f_ref[...] = jnp.concatenate([columns(g_ref, FORGET_LO, N_HEADS),
                                      jnp.zeros((tm, F_PAD - N_HEADS), f_ref.dtype)], axis=1)

    return pl.pallas_call(
        body, name="w_in_from_shards", grid=(D_MODEL // tm,),
        in_specs=[pl.BlockSpec((N_DEV, tm, W_IN_SHARD), lambda i: (0, i, 0))],
        out_specs=[pl.BlockSpec((tm, Z_MAIN), lambda i: (i, 0)), pl.BlockSpec((tm, F_PAD), lambda i: (i, 0))],
        out_shape=[_sds((D_MODEL, Z_MAIN), shards.dtype), _sds((D_MODEL, F_PAD), shards.dtype)],
        compiler_params=_params("parallel"),
    )(shards)


def _w_in_to_shards(g_main, g_f, *, tm=256):
    def natural(main_ref, f_ref, lo, width):
        pieces, hi = [], lo + width
        for ref, start, stop, shift in ((main_ref, 0, FORGET_LO, 0), (f_ref, FORGET_LO, FORGET_LO + N_HEADS, FORGET_LO),
                                        (main_ref, FORGET_LO + N_HEADS, Z_MAIN + N_HEADS, N_HEADS)):
            a, b = max(lo, start), min(hi, stop)
            if a < b:
                pieces.append(ref[:, a - shift:b - shift])
        return pieces[0] if len(pieces) == 1 else jnp.concatenate(pieces, axis=1)

    def body(main_ref, f_ref, o_ref):
        for p in range(N_DEV):
            for q in range(-(-W_IN_SHARD // LANE)):
                width = min(LANE, W_IN_SHARD - q * LANE)
                o_ref[p, :, q * LANE:q * LANE + width] = natural(main_ref, f_ref, p * W_IN_SHARD + q * LANE, width)

    return pl.pallas_call(
        body, name="w_in_to_shards", grid=(D_MODEL // tm,),
        in_specs=[pl.BlockSpec((tm, Z_MAIN), lambda i: (i, 0)), pl.BlockSpec((tm, F_PAD), lambda i: (i, 0))],
        out_specs=pl.BlockSpec((N_DEV, tm, W_IN_SHARD), lambda i: (0, i, 0)),
        out_shape=_sds((N_DEV, D_MODEL, W_IN_SHARD), g_main.dtype),
        compiler_params=_params("parallel"),
    )(g_main, g_f)


W_UP_SHARD = 2 * D_FF // N_DEV


def _w_up_lane_tile(k):
    block = k // 2
    return (2 * (block % FFN_NJ) + block // FFN_NJ) * FFN_HALF + (k % 2) * LANE


def _w_up_from_shards(shards, *, tm=256):
    def body(g_ref, o_ref):
        for k in range(2 * D_FF // LANE):
            p, off = divmod(k * LANE, W_UP_SHARD)
            if off + LANE <= W_UP_SHARD:
                tile = g_ref[p, :, off:off + LANE]
            else:
                tile = jnp.concatenate([g_ref[p, :, off:], g_ref[p + 1, :, :off + LANE - W_UP_SHARD]], axis=1)
            dst = _w_up_lane_tile(k)
            o_ref[:, dst:dst + LANE] = tile

    return pl.pallas_call(
        body, name="w_up_from_shards", grid=(D_MODEL // tm,),
        in_specs=[pl.BlockSpec((N_DEV, tm, W_UP_SHARD), lambda i: (0, i, 0))],
        out_specs=pl.BlockSpec((tm, 2 * D_FF), lambda i: (i, 0)),
        out_shape=_sds((D_MODEL, 2 * D_FF), shards.dtype),
        compiler_params=_params("parallel"),
    )(shards)


def _w_up_to_shards(t, *, tm=256):
    def body(x_ref, o_ref):
        for p in range(N_DEV):
            for q in range(-(-W_UP_SHARD // LANE)):
                width = min(LANE, W_UP_SHARD - q * LANE)
                k, off = divmod(p * W_UP_SHARD + q * LANE, LANE)
                src = _w_up_lane_tile(k)
                if off == 0:
                    tile = x_ref[:, src:src + width]
                else:
                    tile = x_ref[:, src + off:src + LANE]
                    if width > LANE - off:
                        nxt = _w_up_lane_tile(k + 1)
                        tile = jnp.concatenate([tile, x_ref[:, nxt:nxt + width - (LANE - off)]], axis=1)
                o_ref[p, :, q * LANE:q * LANE + width] = tile

    return pl.pallas_call(
        body, name="w_up_to_shards", grid=(D_MODEL // tm,),
        in_specs=[pl.BlockSpec((tm, 2 * D_FF), lambda i: (i, 0))],
        out_specs=pl.BlockSpec((N_DEV, tm, W_UP_SHARD), lambda i: (0, i, 0)),
        out_shape=_sds((N_DEV, D_MODEL, W_UP_SHARD), t.dtype),
        compiler_params=_params("parallel"),
    )(t)


def _gelu_parts(a):
    c = math.sqrt(2.0 / math.pi)
    a2 = a * a
    t = jnp.tanh((c * a) * (1.0 + 0.044715 * a2))
    half_a, one_t = 0.5 * a, 1.0 + t
    gelu = half_a * one_t
    dgelu = 0.5 * one_t + half_a * (1.0 - t * t) * (c + (3.0 * 0.044715 * c) * a2)
    return gelu, dgelu


def _row_masks(down):
    row = lax.broadcasted_iota(jnp.int32, (SUBLANE, FFN_TN), 0)
    return (row < 1, row < 2) if down else (row >= SUBLANE - 1, row >= SUBLANE - 2)


def _rolled(x, down):
    return (pltpu.roll(x, 1, 0), pltpu.roll(x, 2, 0)) if down else (
        pltpu.roll(x, SUBLANE - 1, 0), pltpu.roll(x, SUBLANE - 2, 0))


def _shifted(cur_rolled, neighbour_rolled, masks):
    return (jnp.where(masks[0], neighbour_rolled[0], cur_rolled[0]),
            jnp.where(masks[1], neighbour_rolled[1], cur_rolled[1]))


def _conv_consts(w_ref, b_ref):
    shape = (SUBLANE, FFN_TN)
    return [jnp.broadcast_to(w_ref[k:k + 1, :], shape) for k in range(3)] + [jnp.broadcast_to(b_ref[...], shape)]


def _up_conv_fwd(h2, w_up, conv_w, conv_b):
    nrow = SEQ // UP_TM
    n_tiles = FFN_NJ * nrow
    n_groups = UP_TM // FFN_GROUP

    def body(h_ref, wu_ref, w_ref, b_ref, u_ref, ab_ref, m_ref, ua_s, ub_s, c1_s, c2_s):
        k = pl.program_id(0)

        @pl.when(k == 0)
        def _():
            ub_s[...] = jnp.zeros_like(ub_s)

        @pl.when(jnp.maximum(k - 1, 0) % nrow == 0)
        def _():
            c1_s[...] = jnp.zeros_like(c1_s)
            c2_s[...] = jnp.zeros_like(c2_s)

        def step(write_s, read_s):
            h_rows = pl.ds(pl.multiple_of((this(k) % nrow) * UP_TM, UP_TM), UP_TM)
            u = jnp.dot(h_ref[h_rows, :], wu_ref[...], preferred_element_type=F32)
            write_s[...] = u
            u_ref[...] = u.astype(u_ref.dtype)
            w0, w1, w2, bias = _conv_consts(w_ref, b_ref)
            masks = _row_masks(True)
            above = (c1_s[...], c2_s[...])
            for g in range(n_groups):
                rows = slice(g * FFN_GROUP, (g + 1) * FFN_GROUP)
                x = read_s[rows, :]
                convs = []
                for c in range(2):
                    cur = x[c * SUBLANE:(c + 1) * SUBLANE]
                    cur_rolled = _rolled(cur, True)
                    s1, s2 = _shifted(cur_rolled, above, masks)
                    convs.append(w0 * s2 + w1 * s1 + w2 * cur + bias)
                    above = cur_rolled
                y = jnp.concatenate(convs, axis=0)
                ab_ref[rows, :] = y.astype(ab_ref.dtype)
                m_ref[rows, :] = (_gelu_parts(y[:, :FFN_HALF])[0] * y[:, FFN_HALF:]).astype(m_ref.dtype)
            c1_s[...], c2_s[...] = above

        @pl.when(k % 2 == 0)
        def _():
            step(ua_s, ub_s)

        @pl.when(k % 2 == 1)
        def _():
            step(ub_s, ua_s)

    this = lambda k: jnp.minimum(k, n_tiles - 1)
    last = lambda k: jnp.maximum(k - 1, 0)
    blk = lambda tile: pl.BlockSpec((UP_TM, FFN_TN), lambda k: (tile(k) % nrow, tile(k) // nrow))
    return pl.pallas_call(
        body, name="up_conv_fwd", grid=(n_tiles + 1,),
        in_specs=[pl.BlockSpec((SEQ, D_MODEL), lambda k: (0, 0)),
                  pl.BlockSpec((D_MODEL, FFN_TN), lambda k: (0, this(k) // nrow)),
                  pl.BlockSpec((3, FFN_TN), lambda k: (0, last(k) // nrow)),
                  pl.BlockSpec((1, FFN_TN), lambda k: (0, last(k) // nrow))],
        out_specs=[blk(this), blk(last), pl.BlockSpec((UP_TM, FFN_HALF), lambda k: (last(k) % nrow, last(k) // nrow))],
        out_shape=[_sds((SEQ, 2 * D_FF), BF16), _sds((SEQ, 2 * D_FF), BF16), _sds((SEQ, D_FF), BF16)],
        scratch_shapes=[pltpu.VMEM((UP_TM, FFN_TN), F32), pltpu.VMEM((UP_TM, FFN_TN), F32),
                        pltpu.VMEM((SUBLANE, FFN_TN), F32), pltpu.VMEM((SUBLANE, FFN_TN), F32)],
        compiler_params=_params("arbitrary"),
    )(h2, w_up, conv_w, conv_b)


def _ffn_mid_bwd(dy2, w_down, u, ab, conv_w):
    nrow = SEQ // UP_TM
    n_tiles = FFN_NJ * nrow
    n_groups = UP_TM // FFN_GROUP
    this = lambda k: jnp.minimum(k, n_tiles - 1)
    last = lambda k: jnp.maximum(k - 1, 0)
    row_of = lambda tile: nrow - 1 - tile % nrow

    def body(dy_ref, wd_ref, u_ref, ab_ref, w_ref, du_ref, gw_ref, gb_ref, c_s, dma_s, dmb_s):
        k = pl.program_id(0)

        @pl.when(k == 0)
        def _():
            dmb_s[...] = jnp.zeros_like(dmb_s)

        @pl.when(last(k) % nrow == 0)
        def _():
            c_s[...] = jnp.zeros_like(c_s)
            gw_ref[...] = jnp.zeros_like(gw_ref)
            gb_ref[...] = jnp.zeros_like(gb_ref)

        def step(write_s, read_s):
            dy_rows = pl.ds(pl.multiple_of(row_of(this(k)) * UP_TM, UP_TM), UP_TM)
            write_s[...] = lax.dot_general(dy_ref[dy_rows, :], wd_ref[...], NT_DIMS,
                                           preferred_element_type=F32)
            taps = [jnp.broadcast_to(w_ref[t:t + 1, :], (SUBLANE, FFN_TN)) for t in range(3)]
            masks = _row_masks(False)
            below = _rolled(c_s[...], False)
            acc = [jnp.zeros((SUBLANE, FFN_TN), F32)] * 4
            for g in reversed(range(n_groups)):
                rows = slice(g * FFN_GROUP, (g + 1) * FFN_GROUP)
                x, y, dmv = u_ref[rows, :].astype(F32), ab_ref[rows, :].astype(F32), read_s[rows, :]
                gelu, dgelu = _gelu_parts(y[:, :FFN_HALF])
                d = jnp.concatenate([dmv * y[:, FFN_HALF:] * dgelu, dmv * gelu], axis=1)
                pre = [None, None]
                for c in (1, 0):
                    sl = slice(c * SUBLANE, (c + 1) * SUBLANE)
                    cur, xs = d[sl], x[sl]
                    cur_rolled = _rolled(cur, False)
                    up1, up2 = _shifted(cur_rolled, below, masks)
                    acc = [acc[0] + up2 * xs, acc[1] + up1 * xs, acc[2] + cur * xs, acc[3] + cur]
                    pre[c] = taps[2] * cur + taps[1] * up1 + taps[0] * up2
                    below = cur_rolled
                du_ref[rows, :] = jnp.concatenate(pre, axis=0).astype(du_ref.dtype)
            c_s[...] = pltpu.roll(below[0], 1, 0)
            for t in range(3):
                gw_ref[t:t + 1, :] += jnp.sum(acc[t], axis=0, keepdims=True)
            gb_ref[...] += jnp.sum(acc[3], axis=0, keepdims=True)

        @pl.when(k % 2 == 0)
        def _():
            step(dma_s, dmb_s)

        @pl.when(k % 2 == 1)
        def _():
            step(dmb_s, dma_s)

    blk = pl.BlockSpec((UP_TM, FFN_TN), lambda k: (row_of(last(k)), last(k) // nrow))
    col = lambda rows: pl.BlockSpec((rows, FFN_TN), lambda k: (0, last(k) // nrow))
    return pl.pallas_call(
        body, name="ffn_mid_bwd", grid=(n_tiles + 1,),
        in_specs=[pl.BlockSpec((SEQ, D_MODEL), lambda k: (0, 0)),
                  pl.BlockSpec((FFN_HALF, D_MODEL), lambda k: (this(k) // nrow, 0)), blk, blk, col(3)],
        out_specs=[blk, col(3), col(1)],
        out_shape=[_sds((SEQ, 2 * D_FF), BF16), _sds((3, 2 * D_FF), F32), _sds((1, 2 * D_FF), F32)],
        scratch_shapes=[pltpu.VMEM((SUBLANE, FFN_TN), F32), pltpu.VMEM((UP_TM, FFN_HALF), F32),
                        pltpu.VMEM((UP_TM, FFN_HALF), F32)],
        compiler_params=_params("arbitrary"),
    )(dy2, w_down, u, ab, conv_w)


def _down_fwd(m, w_down, x2, g_post, target, *, tm=512):
    def body(m_ref, w_ref, x2_ref, g_ref, t_ref, dout_ref, dy_ref, gg_ref, loss_ref):
        @pl.when(pl.program_id(0) == 0)
        def _():
            gg_ref[...] = jnp.zeros_like(gg_ref)
            loss_ref[...] = jnp.zeros_like(loss_ref)

        y = jnp.dot(m_ref[...], w_ref[...], preferred_element_type=F32)
        r = lax.rsqrt(jnp.mean(y * y, axis=-1, keepdims=True) + RMS_EPS)
        yn = y * r
        diff = (x2_ref[...] + yn * g_ref[...]) - t_ref[...]
        loss_ref[...] += jnp.sum(diff * diff)
        dout = diff * (1.0 / D_MODEL)
        dout_ref[...] = dout
        gg_ref[...] += jnp.sum(dout * yn, axis=0, keepdims=True)
        dn = dout * g_ref[...]
        dy_ref[...] = (r * (dn - yn * jnp.mean(dn * yn, axis=-1, keepdims=True))).astype(dy_ref.dtype)

    row = pl.BlockSpec((tm, D_MODEL), lambda i: (i, 0))
    vec = pl.BlockSpec((1, D_MODEL), lambda i: (0, 0))
    return pl.pallas_call(
        body, name="down_fwd", grid=(SEQ // tm,),
        in_specs=[pl.BlockSpec((tm, D_FF), lambda i: (i, 0)), pl.BlockSpec((D_FF, D_MODEL), lambda i: (0, 0)),
                  row, vec, row],
        out_specs=[row, row, vec, pl.BlockSpec((1, LANE), lambda i: (0, 0))],
        out_shape=[_sds((SEQ, D_MODEL), F32), _sds((SEQ, D_MODEL), BF16), _sds((1, D_MODEL), F32),
                   _sds((1, LANE), F32)],
        compiler_params=_params("arbitrary"),
    )(m, w_down, x2, g_post, target)


def _local_step(x, target, w_main, w_f, b_forget, conv_b, g_pre_mix, g_post_mix, g_pre_ffn, g_post_ffn,
                late_weights, ffn_grads_ready, proj_grads_ready, mixer_grads_ready):
    mm = _matmul
    tabs = _rope_tables()

    h1 = _rms_fwd(x, g_pre_mix, name="rms_pre_mix")
    zm = mm(h1, w_main, out_dtype=BF16, tm=2048, tn=1024, tk=1024, name="in_proj")
    zf = mm(h1, w_f, out_dtype=F32, tm=2048, tn=F_PAD, tk=1024, name="in_proj_forget")
    f_row, sg_row = _fox_prep(zf[:, :N_HEADS].T, b_forget.reshape(N_HEADS, 1))
    f_cols = jnp.pad(f_row.T, ((0, 0), (0, LANE - N_HEADS)))
    q_slots, k_slots, v_slots = _fox_pack_fwd(zm, f_cols)
    ya, lse_a = _fox_fwd(q_slots, k_slots, v_slots)
    qkv_d = dict(zip([d for _, d in DIL_PATTERNS], _rope_fwd(zm, tabs)))
    dil = [_dil_fwd(qkv_d[d], d) for _, d in DIL_PATTERNS]
    yb, lse_b = _dil_merge([o for o, _ in dil], [l for _, l in dil])
    w_oa, w_ob, w_out, w_up, conv_w, w_down = late_weights(yb)
    pa, pb, mixed = _mix_fwd(ya, yb, w_oa, w_ob, zm)
    y1, x2, h2 = _out_fwd(mixed, w_out, x, g_post_mix, g_pre_ffn)
    u, ab, m = _up_conv_fwd(h2, w_up, conv_w, _ffn_interleave(conv_b))
    dout, dy2, gg_post_ffn, sq_err = _down_fwd(m, w_down, x2, g_post_ffn, target)

    g_w_down = mm(m, dy2, ta=True, out_dtype=BF16, tm=D_FF // 2, tn=1024, tk=2048, name="grad_w_down")
    du, g_conv_w, g_conv_b = _ffn_mid_bwd(dy2, w_down, u, ab, conv_w)
    g_w_up = mm(h2, du, ta=True, out_dtype=BF16, tm=1024, tn=D_FF // 2, tk=2048, name="grad_w_up")
    tok = ffn_grads_ready(dict(w_down=g_w_down, w_up_blocks=g_w_up, conv_w=_ffn_deinterleave(g_conv_w)))
    dh2 = mm(du, w_up, tb=True, out_dtype=BF16, tm=512, tn=1024, tk=2 * D_FF, name="d_h2")

    dx2, dy1, gg_pre_ffn, gg_post_mix = _rms_pair_bwd([dh2], x2, g_pre_ffn, dout, y1, g_post_mix + tok)
    g_w_out = mm(mixed, dy1, ta=True, out_dtype=BF16, tm=1024, tn=1024, tk=2048, name="grad_w_out")
    dmix = mm(dy1, w_out, tb=True, out_dtype=BF16, tm=2048, tn=1024, tk=1024, name="d_mixed")
    dpa, dz = _gate_bwd(dmix, zm, pa, 3, None, name="gate_bwd_fox")
    dpb, dz = _gate_bwd(dmix, zm, pb, 4, dz, name="gate_bwd_dil")
    g_w_oa = mm(ya, dpa, ta=True, out_dtype=BF16, tm=512, tn=1024, tk=SEQ, name="grad_w_o_fox")
    g_w_ob = mm(yb, dpb, ta=True, out_dtype=BF16, tm=512, tn=1024, tk=SEQ, name="grad_w_o_dil")
    tok = proj_grads_ready(dict(w_o_fox=g_w_oa, w_o_dil=g_w_ob, w_out=g_w_out))
    dya = mm(dpa, w_oa, tb=True, out_dtype=BF16, tm=2048, tn=512, tk=1024, name="d_y_fox")
    dyb = mm(dpb, w_ob, tb=True, out_dtype=BF16, tm=2048, tn=512, tk=1024, name="d_y_dil")

    qb_slots, do_slots = _fox_pack_bwd(zm, f_cols + tok, lse_a, ya, dya)
    dz, df_cols = _fox_unpack(*_fox_bwd(qb_slots, k_slots, v_slots, do_slots), dz)
    dfa_t, g_b_forget = _fox_post_bwd(df_cols[:, :N_HEADS].T, sg_row)

    rows_d = _dil_bwd_prep(yb, dyb, lse_b)
    dil_g = [_dil_bwd(qkv_d[d], *rows_d[k], d) for k, (_, d) in enumerate(DIL_PATTERNS)]
    dz = _dil_grad_combine([g[0] for g in dil_g], [g[1] for g in dil_g], [g[2] for g in dil_g], tabs, dz)

    dzf = jnp.pad(dfa_t.T, ((0, 0), (0, F_PAD - N_HEADS)))
    g_w_main = mm(h1, dz, ta=True, out_dtype=BF16, tm=1024, tn=Z_MAIN // 4, tk=2048, name="grad_w_in")
    g_w_f = mm(h1, dzf, ta=True, out_dtype=BF16, tm=1024, tn=F_PAD, tk=1024, name="grad_w_in_forget")
    tok = mixer_grads_ready(dict(w_main=g_w_main, w_f=g_w_f))
    dh1 = [mm(dz, w_main, tb=True, out_dtype=BF16, tm=512, tn=1024, tk=Z_MAIN, name="d_h1"),
           mm(dzf + tok, w_f, tb=True, out_dtype=BF16, tm=2048, tn=1024, tk=F_PAD, name="d_h1_forget")]
    grad_x, gg_pre_mix = _rms_bwd(dh1, x, g_pre_mix, dx2, out_dtype=F32, name="rms_pre_mix_bwd")

    grads = dict(
        b_forget=g_b_forget.reshape(1, N_HEADS), conv_b=_ffn_deinterleave(g_conv_b),
        g_pre_mix=gg_pre_mix, g_post_mix=gg_post_mix, g_pre_ffn=gg_pre_ffn, g_post_ffn=gg_post_ffn)
    return sq_err, grad_x, grads


def _exchange(arrays, scatter, *, name):
    n = len(arrays)
    scatters = [scatter] * n if isinstance(scatter, bool) else list(scatter)

    def body(*refs):
        ins, outs = refs[:n], refs[n:2 * n]
        send_sems, recv_sems, local_sems = refs[2 * n:]
        me, peers = _peers()

        def remote(a, k):
            dev, slot = peers[k]
            return pltpu.make_async_remote_copy(
                src_ref=ins[a].at[slot] if scatters[a] else ins[a], dst_ref=outs[a].at[me],
                send_sem=send_sems.at[a, k], recv_sem=recv_sems.at[a, k],
                device_id=dev, device_id_type=MESH_ID)

        def landed(a, k):
            dev, slot = peers[k]
            return pltpu.make_async_remote_copy(
                src_ref=outs[a].at[slot], dst_ref=outs[a].at[slot],
                send_sem=send_sems.at[a, k], recv_sem=recv_sems.at[a, k],
                device_id=dev, device_id_type=MESH_ID)

        own = [pltpu.make_async_copy(ins[a].at[me] if scatters[a] else ins[a], outs[a].at[me], local_sems.at[a])
               for a in range(n)]
        copies = [remote(a, k) for k in range(N_DEV - 1) for a in range(n)]
        for cp in own + copies:
            cp.start()
        for k in range(N_DEV - 1):
            for a in range(n):
                landed(a, k).wait_recv()
        for cp in copies:
            cp.wait_send()
        for cp in own:
            cp.wait()

    out_shape = [_sds(((N_DEV,) + a.shape[-2:]), a.dtype) for a in arrays]
    return pl.pallas_call(
        body, name=name, in_specs=[ANY] * n, out_specs=[ANY] * n, out_shape=out_shape,
        scratch_shapes=[pltpu.SemaphoreType.DMA((n, N_DEV - 1)), pltpu.SemaphoreType.DMA((n, N_DEV - 1)),
                        pltpu.SemaphoreType.DMA((n,))],
    )(*arrays)


def _gather_two_level(shard, *, name):
    def body(x_ref, out_ref, send_sems, recv_sems, local_sem):
        x, y, c = lax.axis_index("x"), lax.axis_index("y"), lax.axis_index("c")
        me, sibling = (x, y, c), (x, y, 1 - c)
        chips = [(1 - x, y), (x, 1 - y), (1 - x, 1 - y)]

        def slot(px, py, pc):
            return out_ref.at[4 * px + 2 * py + pc]

        def copy(k, block, to, src=None):
            return pltpu.make_async_remote_copy(
                src_ref=slot(*block) if src is None else src, dst_ref=slot(*block),
                send_sem=send_sems.at[k], recv_sem=recv_sems.at[k], device_id=to, device_id_type=MESH_ID)

        mine = pltpu.make_async_copy(x_ref, slot(*me), local_sem)
        mine.start()
        first = [copy(0, me, sibling, src=x_ref)]
        first += [copy(1 + j, me, (*chip, c), src=x_ref) for j, chip in enumerate(chips)]
        for cp in first:
            cp.start()
        passed = [copy(4 + j, (*chip, c), sibling) for j, chip in enumerate(chips)]
        for j, chip in enumerate(chips):
            copy(1 + j, (*chip, c), me).wait_recv()
            passed[j].start()
        copy(0, sibling, me).wait_recv()
        for j, chip in enumerate(chips):
            copy(4 + j, (*chip, 1 - c), me).wait_recv()
        for cp in first + passed:
            cp.wait_send()
        mine.wait()

    return pl.pallas_call(
        body, name=name, in_specs=[ANY], out_specs=ANY, out_shape=_sds((N_DEV,) + shard.shape, shard.dtype),
        scratch_shapes=[pltpu.SemaphoreType.DMA((N_DEV - 1,)), pltpu.SemaphoreType.DMA((N_DEV - 1,)),
                        pltpu.SemaphoreType.DMA],
    )(shard)


N_CHIPS = N_DEV // 2


def _peers(chips_only=False):
    x, y, c = lax.axis_index("x"), lax.axis_index("y"), lax.axis_index("c")
    out = []
    if chips_only:
        for k in range(1, N_CHIPS):
            px = 1 - x if k & 2 else x
            py = 1 - y if k & 1 else y
            out.append(((px, py, c), 2 * px + py))
        return 2 * x + y, out
    for k in range(1, N_DEV):
        px = 1 - x if k & 4 else x
        py = 1 - y if k & 2 else y
        pc = 1 - c if k & 1 else c
        out.append(((px, py, pc), 4 * px + 2 * py + pc))
    return 4 * x + 2 * y + c, out


def _sibling_swap(slot_arrays, *, name):
    n = len(slot_arrays)

    def body(*refs):
        ins, outs, send_sems, recv_sems = refs[:n], refs[n:2 * n], refs[2 * n], refs[2 * n + 1]
        x, y, c = lax.axis_index("x"), lax.axis_index("y"), lax.axis_index("c")
        copies = [pltpu.make_async_remote_copy(
            src_ref=ins[a].at[2 * q + (1 - c)], dst_ref=outs[a].at[q], send_sem=send_sems.at[a, q],
            recv_sem=recv_sems.at[a, q], device_id=(x, y, 1 - c), device_id_type=MESH_ID)
            for a in range(n) for q in range(N_CHIPS)]
        for cp in copies:
            cp.start()
        for cp in copies:
            cp.wait_recv()
        for cp in copies:
            cp.wait_send()

    return pl.pallas_call(
        body, name=name, in_specs=[ANY] * n, out_specs=[ANY] * n,
        out_shape=[_sds((N_CHIPS,) + t.shape[1:], t.dtype) for t in slot_arrays],
        scratch_shapes=[pltpu.SemaphoreType.DMA((n, N_CHIPS)), pltpu.SemaphoreType.DMA((n, N_CHIPS))],
    )(*slot_arrays)


def _pair_sum(slots, from_sibling, *, name, tn):
    _, r, c = slots.shape
    core = lax.axis_index("c").astype(jnp.int32).reshape(1)

    def body(core_ref, a_ref, b_ref, o_ref):
        o_ref[...] = (a_ref[...].astype(F32) + b_ref[...].astype(F32)).astype(o_ref.dtype)

    blk = lambda f: pl.BlockSpec((1, r, tn), f)
    return pl.pallas_call(
        body, name=name,
        grid_spec=pltpu.PrefetchScalarGridSpec(
            num_scalar_prefetch=1, grid=(N_CHIPS, c // tn),
            in_specs=[blk(lambda q, j, core: (2 * q + core[0], 0, j)), blk(lambda q, j, core: (q, 0, j))],
            out_specs=blk(lambda q, j, core: (q, 0, j))),
        out_shape=_sds((N_CHIPS, r, c), slots.dtype),
        compiler_params=_params("parallel", "parallel"),
    )(core, slots, from_sibling)


HBM = pl.BlockSpec(memory_space=pltpu.HBM)
SEM = pl.BlockSpec(memory_space=pltpu.SEMAPHORE)
DATAFLOW = pltpu.SideEffectType.DATAFLOW_SIDE_EFFECTING


def _split_copy(srcs, lands, send_sems, recv_sems, scatter, a, k, me, peers, incoming=False):
    dev, slot = peers[k]
    if incoming:
        src = dst = lands[a].at[slot]
    else:
        src, dst = (srcs[a].at[slot] if scatter else srcs[a]), lands[a].at[me]
    sem = a * len(peers) + k
    return pltpu.make_async_remote_copy(
        src_ref=src, dst_ref=dst, send_sem=send_sems.at[sem], recv_sem=recv_sems.at[sem],
        device_id=dev, device_id_type=MESH_ID)


def _exchange_start(arrays, scatter, *, name, chips_only=False):
    n = len(arrays)
    n_slots = N_CHIPS if chips_only else N_DEV

    def body(*refs):
        srcs, lands = refs[:n], refs[n:2 * n]
        send_sems, recv_sems = refs[2 * n], refs[2 * n + 1]
        token = refs[-1]
        me, peers = _peers(chips_only)
        for k in range(len(peers)):
            for a in range(n):
                _split_copy(srcs, lands, send_sems, recv_sems, scatter, a, k, me, peers).start()
        token[...] = jnp.zeros_like(token)

    land_shapes = [((n_slots,) + a.shape[-2:], a.dtype) for a in arrays]
    sems = pltpu.SemaphoreType.DMA((n * (n_slots - 1),))
    outs = pl.pallas_call(
        body, name=name,
        out_shape=(sems, sems, *[pltpu.HBM(a.shape, a.dtype) for a in arrays],
                   *[pltpu.HBM(s, d) for s, d in land_shapes], _sds((SUBLANE, LANE), F32)),
        in_specs=[HBM] * (2 * n),
        out_specs=(SEM, SEM, *[HBM] * (2 * n), pl.BlockSpec(memory_space=pltpu.VMEM)),
        input_output_aliases={i: 2 + i for i in range(2 * n)},
        compiler_params=pltpu.CompilerParams(has_side_effects=DATAFLOW),
    )(*[pltpu.with_memory_space_constraint(a, pltpu.HBM) for a in arrays],
      *[pltpu.with_memory_space_constraint(lax.empty(s, d), pltpu.HBM) for s, d in land_shapes])
    return (outs[0], outs[1], outs[2:2 + n], outs[2 + n:2 + 2 * n], scatter, chips_only), outs[-1]


def _exchange_wait(handles, after, *, name):
    send_sems, recv_sems, srcs, lands, scatter, chips_only = handles
    n = len(srcs)

    def body(*refs):
        src_refs, land_refs = refs[:n], refs[n:2 * n]
        send_ref, recv_ref = refs[2 * n], refs[2 * n + 1]
        me, peers = _peers(chips_only)
        for k in range(len(peers)):
            for a in range(n):
                _split_copy(src_refs, land_refs, send_ref, recv_ref, scatter, a, k, me, peers).wait_send()
                _split_copy(src_refs, land_refs, send_ref, recv_ref, scatter, a, k, me, peers, True).wait_recv()

    outs = pl.pallas_call(
        body, name=name,
        out_shape=tuple(pltpu.HBM(t.shape, t.dtype) for t in (*srcs, *lands)),
        in_specs=[HBM] * (2 * n) + [SEM, SEM, pl.BlockSpec(memory_space=pl.ANY)],
        out_specs=tuple([HBM] * (2 * n)),
        input_output_aliases={i: i for i in range(2 * n)},
        compiler_params=pltpu.CompilerParams(has_side_effects=DATAFLOW),
    )(*srcs, *lands, send_sems, recv_sems, after)
    return _with_own_slot(outs[n:], outs[:n], scatter, chips_only)


def _with_own_slot(landed, own, scatter, chips_only):
    me = 2 * lax.axis_index("x") + lax.axis_index("y")
    if not chips_only:
        me = 2 * me + lax.axis_index("c")
    out = []
    for buf, src in zip(landed, own):
        mine = lax.dynamic_index_in_dim(src, me, 0, keepdims=False) if scatter else src
        out.append(lax.dynamic_update_index_in_dim(buf, mine, me, 0))
    return out


def _adamw(parts, w, m, v, *, name, tm):
    r, c = w.shape
    assert r % tm == 0

    def body(p_ref, w_ref, m_ref, v_ref, g_ref, d_ref, nm_ref, nv_ref):
        _adamw_update(p_ref, w_ref, m_ref, v_ref, g_ref, d_ref, nm_ref, nv_ref)

    blk = pl.BlockSpec((tm, c), lambda i: (i, 0))
    return pl.pallas_call(
        body, name=name, grid=(r // tm,),
        in_specs=[pl.BlockSpec((parts.shape[0], tm, c), lambda i: (0, i, 0)), blk, blk, blk],
        out_specs=[blk] * 4, out_shape=[_sds((r, c), F32)] * 4,
        compiler_params=_params("parallel"),
    )(parts, w, m, v)


def _adamw_update(p_ref, w_ref, m_ref, v_ref, g_ref, d_ref, nm_ref, nv_ref):
    g = p_ref[0].astype(F32)
    for s in range(1, p_ref.shape[0]):
        g = g + p_ref[s].astype(F32)
    g_ref[...] = g
    m_new = ADAM_B1 * m_ref[...] + (1.0 - ADAM_B1) * g
    v_new = ADAM_B2 * v_ref[...] + (1.0 - ADAM_B2) * (g * g)
    nm_ref[...] = m_new
    nv_ref[...] = v_new
    m_hat = m_new / (1.0 - ADAM_B1 ** ADAM_STEP)
    v_hat = v_new / (1.0 - ADAM_B2 ** ADAM_STEP)
    d_ref[...] = -ADAM_LR * (m_hat / (jnp.sqrt(v_hat) + ADAM_EPS) + ADAM_WD * w_ref[...])


SMALL = ("g_pre_mix", "b_forget", "g_post_mix", "g_pre_ffn", "conv_b", "g_post_ffn")


def _adamw_small(parts, ws, ms, vs, sq_err_parts):
    n = len(ws)

    def body(*refs):
        ins, sq_ref, outs, loss_ref = refs[:4 * n], refs[4 * n], refs[4 * n + 1:-1], refs[-1]
        for i in range(n):
            _adamw_update(ins[i], ins[n + i], ins[2 * n + i], ins[3 * n + i], *outs[4 * i:4 * i + 4])
        total = sq_ref[0]
        for s in range(1, N_DEV):
            total = total + sq_ref[s]
        loss_ref[...] = total * (0.5 / D_MODEL)

    res = pl.pallas_call(
        body, name="adamw_small",
        out_shape=[_sds(w.shape, F32) for w in ws for _ in range(4)] + [_sds((1, LANE), F32)],
        compiler_params=pltpu.CompilerParams(vmem_limit_bytes=VMEM_LIMIT),
    )(*parts, *ws, *ms, *vs, sq_err_parts)
    return [res[4 * i:4 * i + 4] for i in range(n)], res[-1][0, 0]


def kernel(x, g_pre_mix, w_in, b_forget, w_o_fox, w_o_dil, w_out, g_post_mix, g_pre_ffn, w_up, conv_w, conv_b, w_down, g_post_ffn, loss_target, m_g_pre_mix, m_w_in, m_b_forget, m_w_o_fox, m_w_o_dil, m_w_out, m_g_post_mix, m_g_pre_ffn, m_w_up, m_conv_w, m_conv_b, m_w_down, m_g_post_ffn, v_g_pre_mix, v_w_in, v_b_forget, v_w_o_fox, v_w_o_dil, v_w_out, v_g_post_mix, v_g_pre_ffn, v_w_up, v_conv_w, v_conv_b, v_w_down, v_g_post_ffn):
    names = ("g_pre_mix", "w_in", "b_forget", "w_o_fox", "w_o_dil", "w_out", "g_post_mix", "g_pre_ffn",
             "w_up", "conv_w", "conv_b", "w_down", "g_post_ffn")
    w = dict(g_pre_mix=g_pre_mix, w_in=w_in, b_forget=b_forget, w_o_fox=w_o_fox, w_o_dil=w_o_dil, w_out=w_out,
             g_post_mix=g_post_mix, g_pre_ffn=g_pre_ffn, w_up=w_up, conv_w=conv_w, conv_b=conv_b, w_down=w_down,
             g_post_ffn=g_post_ffn)
    m = dict(g_pre_mix=m_g_pre_mix, w_in=m_w_in, b_forget=m_b_forget, w_o_fox=m_w_o_fox, w_o_dil=m_w_o_dil,
             w_out=m_w_out, g_post_mix=m_g_post_mix, g_pre_ffn=m_g_pre_ffn, w_up=m_w_up, conv_w=m_conv_w,
             conv_b=m_conv_b, w_down=m_w_down, g_post_ffn=m_g_post_ffn)
    v = dict(g_pre_mix=v_g_pre_mix, w_in=v_w_in, b_forget=v_b_forget, w_o_fox=v_w_o_fox, w_o_dil=v_w_o_dil,
             w_out=v_w_out, g_post_mix=v_g_post_mix, g_pre_ffn=v_g_pre_ffn, w_up=v_w_up, conv_w=v_conv_w,
             conv_b=v_conv_b, w_down=v_w_down, g_post_ffn=v_g_post_ffn)
    sharded = ("w_in", "w_o_fox", "w_o_dil", "w_out", "w_up", "w_down", "conv_w")
    wire = lambda n: F32 if n == "conv_w" else BF16

    by_cols = lambda t: jnp.transpose(t, (1, 0, 2)).reshape(t.shape[1], N_DEV * t.shape[2])
    by_rows = lambda t: t.reshape(N_DEV * t.shape[1], t.shape[2])
    col_slots = lambda t: jnp.transpose(t.reshape(t.shape[0], N_DEV, t.shape[1] // N_DEV), (1, 0, 2))
    row_slots = lambda t: t.reshape(N_DEV, t.shape[0] // N_DEV, t.shape[1])
    to_slots = lambda n, t: (row_slots if n in ("w_out", "w_down") else col_slots)(t).astype(wire(n))
    shard = lambda n: w[n][0].astype(wire(n))

    w_main, w_f = _w_in_from_shards(_gather_two_level(shard("w_in"), name="gather_w_in"))
    late = ("w_o_fox", "w_o_dil", "w_out", "w_up", "conv_w", "w_down")
    order = jnp.minimum(jnp.abs(w_f[0, 0].astype(F32)), 0.0)
    late_handles, late_tok = _exchange_start(
        [shard(n) + order.astype(wire(n)) if n == "conv_w" else shard(n) for n in late], False,
        name="gather_late_start")

    def late_weights(after):
        got = dict(zip(late, _exchange_wait(late_handles, after, name="gather_late_wait")))
        return (by_cols(got["w_o_fox"]), by_cols(got["w_o_dil"]), by_rows(got["w_out"]),
                _w_up_from_shards(got["w_up"]),
                _ffn_interleave(by_cols(got["conv_w"])),
                by_rows(got["w_down"]))

    pending = {}

    def ffn_grads_ready(g):
        slots = [to_slots("w_down", g["w_down"]), _w_up_to_shards(g["w_up_blocks"]), to_slots("conv_w", g["conv_w"])]
        pending["ffn"] = _exchange_start(slots, True, name="scatter_ffn_start")
        return pending["ffn"][1][0, 0]

    def proj_grads_ready(g):
        pending["proj"] = _exchange_start([to_slots(n, g[n]) for n in ("w_o_fox", "w_o_dil", "w_out")], True,
                                          name="scatter_proj_start")
        return pending["proj"][1][0, 0]

    def mixer_grads_ready(g):
        slots = _w_in_to_shards(g["w_main"], g["w_f"])
        theirs = _sibling_swap([slots], name="scatter_w_in_swap")[0]
        chip_sums = _pair_sum(slots, theirs, name="scatter_w_in_pair_sum", tn=W_IN_SHARD)
        pending["w_in"] = _exchange_start([chip_sums], True, name="scatter_w_in_start", chips_only=True)
        return pending["w_in"][1][0, 0]

    sq_err, grad_x, g = _local_step(
        x[0], loss_target[0], w_main, w_f, b_forget, conv_b, g_pre_mix + late_tok[0, 0], g_post_mix, g_pre_ffn,
        g_post_ffn, late_weights, ffn_grads_ready, proj_grads_ready, mixer_grads_ready)

    tiles = dict(w_in=256, w_o_fox=512, w_o_dil=512, w_out=128, w_up=256, w_down=176, conv_w=3)
    adam = lambda n, p: _adamw(p, w[n][0], m[n][0], v[n][0], name=f"adamw_{n}", tm=tiles[n])
    res = {}
    for key, group in (("ffn", ("w_down", "w_up", "conv_w")), ("proj", ("w_o_fox", "w_o_dil", "w_out"))):
        landed = _exchange_wait(pending[key][0], grad_x, name=f"scatter_{key}_wait")
        res.update({n: adam(n, p) for n, p in zip(group, landed)})
    done = res["w_up"][3]
    res["w_in"] = adam("w_in", _exchange_wait(pending["w_in"][0], done, name="scatter_w_in_wait")[0])
    small_parts = _exchange([g[n] for n in SMALL] + [sq_err], False, name="gather_small_grads")
    small, loss = _adamw_small(small_parts[:-1], *[[t[n] for n in SMALL] for t in (w, m, v)], small_parts[-1])
    small = dict(zip(SMALL, small))
    out = [[(res[n][k][None] if n in sharded else small[n][k]) for n in names] for k in range(4)]
    return (loss, grad_x[None], *out[0], *out[1], *out[2], *out[3])
```

```python
import functools
import math

import jax
import jax.numpy as jnp
import numpy as np
from jax import lax
from jax.experimental import pallas as pl
from jax.experimental.pallas import tpu as pltpu

F32 = jnp.float32
BF16 = jnp.bfloat16

SEQ = 4096
D_MODEL = 1024
N_HEADS = 8
HEAD_DIM = 64
ATT_W = N_HEADS * HEAD_DIM
D_FF = 2816
Z_MAIN = 5120
F_PAD = 128
ROPE_DIM = 16
ROPE_THETA = 500000.0
RMS_EPS = 1e-6
NEG_INF = -1e30
SCALE = 1.0 / math.sqrt(HEAD_DIM)
DIL_PATTERNS = ((128, 1), (512, 4), (2048, 16))
DIL_BLK = 128
DIL_STEP_BLOCKS = 2
N_DEV = 8

ADAM_LR = 0.001
ADAM_B1 = 0.9
ADAM_B2 = 0.999
ADAM_EPS = 1e-08
ADAM_WD = 0.01
ADAM_STEP = 10

LANE = 128
SUBLANE = 8
VMEM_LIMIT = 56 * 1024 * 1024
MESH_ID = pl.DeviceIdType.MESH
ANY = pl.BlockSpec(memory_space=pl.ANY)


def _params(*sem):
    return pltpu.CompilerParams(dimension_semantics=sem, vmem_limit_bytes=VMEM_LIMIT)


def _sds(shape, dtype):
    return jax.ShapeDtypeStruct(shape, dtype)


def _matmul(a, b, *, ta=False, tb=False, out_dtype, tm, tn, tk, name, b_k_off=0):
    if ta:
        kk, m = a.shape
    else:
        m, kk = a.shape
    n = b.shape[0] if tb else b.shape[1]
    tm, tn, tk = min(tm, m), min(tn, n), min(tk, kk)
    assert (b.shape[1] if tb else b.shape[0]) >= b_k_off * tk + kk
    assert m % tm == 0 and n % tn == 0 and kk % tk == 0, (name, m, n, kk, tm, tn, tk)
    nk = kk // tk
    dims = (((0 if ta else 1,), (1 if tb else 0,)), ((), ()))

    def body(a_ref, b_ref, o_ref, *scratch):
        p = lax.dot_general(a_ref[...].astype(BF16), b_ref[...].astype(BF16), dims,
                            preferred_element_type=F32)
        if nk == 1:
            o_ref[...] = p.astype(o_ref.dtype)
        else:
            acc = scratch[0]
            k = pl.program_id(2)

            @pl.when(k == 0)
            def _():
                acc[...] = p

            @pl.when(k > 0)
            def _():
                acc[...] += p

            @pl.when(k == nk - 1)
            def _():
                o_ref[...] = acc[...].astype(o_ref.dtype)

    a_spec = (pl.BlockSpec((tk, tm), lambda i, j, k: (k, i)) if ta
              else pl.BlockSpec((tm, tk), lambda i, j, k: (i, k)))
    b_spec = (pl.BlockSpec((tn, tk), lambda i, j, k: (j, k + b_k_off)) if tb
              else pl.BlockSpec((tk, tn), lambda i, j, k: (k + b_k_off, j)))
    return pl.pallas_call(
        body, name=name, grid=(m // tm, n // tn, nk),
        in_specs=[a_spec, b_spec],
        out_specs=pl.BlockSpec((tm, tn), lambda i, j, k: (i, j)),
        out_shape=_sds((m, n), out_dtype),
        scratch_shapes=[pltpu.VMEM((tm, tn), F32)] if nk > 1 else [],
        compiler_params=_params("parallel", "parallel", "arbitrary"),
    )(a, b)


def _rms_fwd(x, g, *, name, tm=512):
    def body(x_ref, g_ref, h_ref):
        xv = x_ref[...]
        r = lax.rsqrt(jnp.mean(xv * xv, axis=-1, keepdims=True) + RMS_EPS)
        h_ref[...] = (xv * r * g_ref[...]).astype(h_ref.dtype)

    return pl.pallas_call(
        body, name=name, grid=(SEQ // tm,),
        in_specs=[pl.BlockSpec((tm, D_MODEL), lambda i: (i, 0)), pl.BlockSpec((1, D_MODEL), lambda i: (0, 0))],
        out_specs=pl.BlockSpec((tm, D_MODEL), lambda i: (i, 0)),
        out_shape=_sds((SEQ, D_MODEL), BF16),
        compiler_params=_params("parallel"),
    )(x, g)


def _rms_bwd(dh_parts, xin, g, dres, *, out_dtype, name, tm=512):
    n_parts = len(dh_parts)
    has_res = dres is not None

    def body(*refs):
        parts = refs[:n_parts]
        x_ref, g_ref = refs[n_parts], refs[n_parts + 1]
        res_ref = refs[n_parts + 2] if has_res else None
        o_ref, gg_ref = refs[-2], refs[-1]
        dh = parts[0][...].astype(F32)
        for p in parts[1:]:
            dh = dh + p[...].astype(F32)
        xv = x_ref[...]
        r = lax.rsqrt(jnp.mean(xv * xv, axis=-1, keepdims=True) + RMS_EPS)
        xn = xv * r

        @pl.when(pl.program_id(0) == 0)
        def _():
            gg_ref[...] = jnp.zeros_like(gg_ref)

        gg_ref[...] += jnp.sum(dh * xn, axis=0, keepdims=True)
        dxn = dh * g_ref[...]
        dx = r * (dxn - xn * jnp.mean(dxn * xn, axis=-1, keepdims=True))
        if has_res:
            dx = dx + res_ref[...]
        o_ref[...] = dx.astype(o_ref.dtype)

    row = pl.BlockSpec((tm, D_MODEL), lambda i: (i, 0))
    vec = pl.BlockSpec((1, D_MODEL), lambda i: (0, 0))
    args = list(dh_parts) + [xin, g] + ([dres] if has_res else [])
    return pl.pallas_call(
        body, name=name, grid=(SEQ // tm,),
        in_specs=[row] * n_parts + [row, vec] + ([row] if has_res else []),
        out_specs=[row, vec],
        out_shape=[_sds((SEQ, D_MODEL), out_dtype), _sds((1, D_MODEL), F32)],
        compiler_params=_params("arbitrary"),
    )(*args)


def _rms_pair_bwd(dh_parts, x2, g_pre, dres, y1, g_post, *, tm=512):
    n_parts = len(dh_parts)

    def norm_bwd(dh, xin, g_ref, gg_ref):
        r = lax.rsqrt(jnp.mean(xin * xin, axis=-1, keepdims=True) + RMS_EPS)
        xn = xin * r
        gg_ref[...] += jnp.sum(dh * xn, axis=0, keepdims=True)
        dxn = dh * g_ref[...]
        return r * (dxn - xn * jnp.mean(dxn * xn, axis=-1, keepdims=True))

    def body(*refs):
        parts = refs[:n_parts]
        x2_ref, gpre_ref, res_ref, y1_ref, gpost_ref, dx2_ref, dy1_ref, ggpre_ref, ggpost_ref = refs[n_parts:]

        @pl.when(pl.program_id(0) == 0)
        def _():
            ggpre_ref[...] = jnp.zeros_like(ggpre_ref)
            ggpost_ref[...] = jnp.zeros_like(ggpost_ref)

        dh = parts[0][...].astype(F32)
        for p in parts[1:]:
            dh = dh + p[...].astype(F32)
        dx2 = res_ref[...] + norm_bwd(dh, x2_ref[...], gpre_ref, ggpre_ref)
        dx2_ref[...] = dx2
        dy1_ref[...] = norm_bwd(dx2, y1_ref[...], gpost_ref, ggpost_ref).astype(dy1_ref.dtype)

    row = pl.BlockSpec((tm, D_MODEL), lambda i: (i, 0))
    vec = pl.BlockSpec((1, D_MODEL), lambda i: (0, 0))
    return pl.pallas_call(
        body, name="rms_pair_bwd", grid=(SEQ // tm,),
        in_specs=[row] * n_parts + [row, vec, row, row, vec],
        out_specs=[row, row, vec, vec],
        out_shape=[_sds((SEQ, D_MODEL), F32), _sds((SEQ, D_MODEL), BF16), _sds((1, D_MODEL), F32),
                   _sds((1, D_MODEL), F32)],
        compiler_params=_params("arbitrary"),
    )(*dh_parts, x2, g_pre, dres, y1, g_post)


SCAN_BLK = 512


def _split_dot(v, tri):
    hi = v.astype(BF16)
    r1 = v - hi.astype(F32)
    mid = r1.astype(BF16)
    lo = (r1 - mid.astype(F32)).astype(BF16)
    dot = functools.partial(jnp.dot, preferred_element_type=F32)
    return dot(hi, tri) + dot(mid, tri) + dot(lo, tri)


def _fox_prep(fa_t, b_col):
    nblk = SEQ // SCAN_BLK

    def body(fa_ref, b_ref, f_ref, sg_ref):
        row = lax.broadcasted_iota(jnp.int32, (SCAN_BLK, SCAN_BLK), 0)
        col = lax.broadcasted_iota(jnp.int32, (SCAN_BLK, SCAN_BLK), 1)
        upper = (row <= col).astype(BF16)
        carry = jnp.zeros((N_HEADS, 1), F32)
        for blk in range(nblk):
            sl = pl.ds(blk * SCAN_BLK, SCAN_BLK)
            xx = fa_ref[:, sl] + b_ref[...]
            e = jnp.exp(-jnp.abs(xx))
            logf = jnp.minimum(xx, 0.0) - jnp.log(1.0 + e)
            sg_ref[:, sl] = jnp.where(xx >= 0.0, e, 1.0) / (1.0 + e)
            c = _split_dot(logf, upper) + carry
            f_ref[:, sl] = c
            carry = c[:, SCAN_BLK - 1:SCAN_BLK]

    return pl.pallas_call(
        body, name="fox_prep",
        out_shape=[_sds((N_HEADS, SEQ), F32), _sds((N_HEADS, SEQ), F32)],
        compiler_params=pltpu.CompilerParams(vmem_limit_bytes=VMEM_LIMIT),
    )(fa_t, b_col)


def _fox_post_bwd(df_t, sg_t):
    nblk = SEQ // SCAN_BLK

    def body(df_ref, sg_ref, dfa_ref, gb_ref):
        row = lax.broadcasted_iota(jnp.int32, (SCAN_BLK, SCAN_BLK), 0)
        col = lax.broadcasted_iota(jnp.int32, (SCAN_BLK, SCAN_BLK), 1)
        lower = (row >= col).astype(BF16)
        carry = jnp.zeros((N_HEADS, 1), F32)
        gb = jnp.zeros((N_HEADS, 1), F32)
        for blk in reversed(range(nblk)):
            sl = pl.ds(blk * SCAN_BLK, SCAN_BLK)
            c = _split_dot(df_ref[:, sl], lower) + carry
            carry = c[:, 0:1]
            dfa = c * sg_ref[:, sl]
            dfa_ref[:, sl] = dfa
            gb = gb + jnp.sum(dfa, axis=1, keepdims=True)
        gb_ref[...] = gb

    return pl.pallas_call(
        body, name="fox_post_bwd",
        out_shape=[_sds((N_HEADS, SEQ), F32), _sds((N_HEADS, 1), F32)],
        compiler_params=pltpu.CompilerParams(vmem_limit_bytes=VMEM_LIMIT),
    )(df_t, sg_t)


FOX_T = 512
NT_DIMS = (((1,), (1,)), ((), ()))
TN_DIMS = (((0,), (0,)), ((), ()))


def _head(ref_or_val, h):
    return ref_or_val[:, h * HEAD_DIM:(h + 1) * HEAD_DIM]


def _split3(v):
    hi = v.astype(BF16).astype(F32)
    r1 = v - hi
    mid = r1.astype(BF16).astype(F32)
    return hi, mid, (r1 - mid).astype(BF16).astype(F32)


ONE_LANE = 3 * N_HEADS


def _pack_terms(v, with_one):
    hi, mid, lo = _split3(v)
    t = hi + pltpu.roll(mid, N_HEADS, 1) + pltpu.roll(lo, 2 * N_HEADS, 1)
    if with_one:
        t = t + (lax.broadcasted_iota(jnp.int32, v.shape, 1) == ONE_LANE).astype(F32)
    return t.astype(BF16)


def _aux_matrices():
    to_q = np.zeros((LANE, N_HEADS * 2 * HEAD_DIM), np.float32)
    to_k = np.zeros_like(to_q)
    for h in range(N_HEADS):
        base = h * 2 * HEAD_DIM + HEAD_DIM
        for s in range(3):
            to_q[s * N_HEADS + h, base + s] = 1.0
            to_q[ONE_LANE, base + 3 + s] = 1.0
            to_k[ONE_LANE, base + s] = 1.0
            to_k[s * N_HEADS + h, base + 3 + s] = -1.0
    return jnp.asarray(to_q, BF16), jnp.asarray(to_k, BF16)


def _head_sums():
    total = np.zeros((N_HEADS * HEAD_DIM, LANE), np.float32)
    first = np.zeros_like(total)
    for h in range(N_HEADS):
        total[h * HEAD_DIM:(h + 1) * HEAD_DIM, h] = 1.0
        first[h * HEAD_DIM, h] = 1.0
    return jnp.asarray(total, BF16), jnp.asarray(first, BF16)


SLOT = 2 * HEAD_DIM
N_SPLIT = 3
FOX_FWD_HEADS = 8
FOX_BWD_HEADS = 4


def _slot(ref, h):
    return ref[:, h * SLOT:(h + 1) * SLOT]


def _fox_pack_fwd(zm, f_cols, *, tm=512):
    def body(q_ref, k_ref, v_ref, f_ref, tq_ref, tk_ref, qs_ref, ks_ref, vs_ref):
        ones = jnp.ones((tm, HEAD_DIM), BF16)
        terms = _pack_terms(f_ref[...], True)
        q_aux = jnp.dot(terms, tq_ref[...], preferred_element_type=F32).astype(BF16)
        k_aux = jnp.dot(terms, tk_ref[...], preferred_element_type=F32).astype(BF16)
        for h in range(N_HEADS):
            aux = slice(h * SLOT + HEAD_DIM, (h + 1) * SLOT)
            qs_ref[:, h * SLOT:(h + 1) * SLOT] = jnp.concatenate(
                [(_head(q_ref, h).astype(F32) * SCALE).astype(BF16), q_aux[:, aux]], axis=1)
            ks_ref[:, h * SLOT:(h + 1) * SLOT] = jnp.concatenate([_head(k_ref, h), k_aux[:, aux]], axis=1)
            vs_ref[:, h * SLOT:(h + 1) * SLOT] = jnp.concatenate([_head(v_ref, h), ones], axis=1)

    col = lambda b: pl.BlockSpec((tm, ATT_W), lambda i: (i, b))
    wide = pl.BlockSpec((tm, N_HEADS * SLOT), lambda i: (i, 0))
    const = pl.BlockSpec((LANE, N_HEADS * SLOT), lambda i: (0, 0))
    return pl.pallas_call(
        body, name="fox_pack_fwd", grid=(SEQ // tm,),
        in_specs=[col(0), col(1), col(2), pl.BlockSpec((tm, LANE), lambda i: (i, 0)), const, const],
        out_specs=[wide] * 3, out_shape=[_sds((SEQ, N_HEADS * SLOT), BF16)] * 3,
        compiler_params=_params("parallel"),
    )(zm, zm, zm, f_cols, *_aux_matrices())


def _fox_pack_bwd(zm, f_cols, lse, o, do, *, tm=512):
    def body(q_ref, f_ref, lse_ref, o_ref, do_ref, tq_ref, total_ref, first_ref, qs_ref, ds_ref):
        delta = _split_dot(o_ref[...].astype(F32) * do_ref[...].astype(F32), total_ref[...])
        lse_h = _split_dot(lse_ref[...], first_ref[...])
        q_aux = jnp.dot(_pack_terms(f_ref[...] - lse_h, True), tq_ref[...], preferred_element_type=F32).astype(BF16)
        d_aux = jnp.dot(_pack_terms(-delta, False), tq_ref[...], preferred_element_type=F32).astype(BF16)
        for h in range(N_HEADS):
            aux = slice(h * SLOT + HEAD_DIM, (h + 1) * SLOT)
            qs_ref[:, h * SLOT:(h + 1) * SLOT] = jnp.concatenate(
                [(_head(q_ref, h).astype(F32) * SCALE).astype(BF16), q_aux[:, aux]], axis=1)
            ds_ref[:, h * SLOT:(h + 1) * SLOT] = jnp.concatenate([_head(do_ref, h), d_aux[:, aux]], axis=1)

    row = pl.BlockSpec((tm, ATT_W), lambda i: (i, 0))
    wide = pl.BlockSpec((tm, N_HEADS * SLOT), lambda i: (i, 0))
    const = lambda r, c: pl.BlockSpec((r, c), lambda i: (0, 0))
    return pl.pallas_call(
        body, name="fox_pack_bwd", grid=(SEQ // tm,),
        in_specs=[row, pl.BlockSpec((tm, LANE), lambda i: (i, 0)), row, row, row,
                  const(LANE, N_HEADS * SLOT), const(ATT_W, LANE), const(ATT_W, LANE)],
        out_specs=[wide] * 2, out_shape=[_sds((SEQ, N_HEADS * SLOT), BF16)] * 2,
        compiler_params=_params("parallel"),
    )(zm, f_cols, lse, o, do, _aux_matrices()[0], *_head_sums())


def _causal_pairs(key_major):
    nb = SEQ // FOX_T
    if key_major:
        pairs = [(i, j) for j in range(nb) for i in range(j, nb)]
    else:
        pairs = [(i, j) for i in range(nb) for j in range(i + 1)]
    return (jnp.array([p[0] for p in pairs], jnp.int32), jnp.array([p[1] for p in pairs], jnp.int32), len(pairs))


FOX_HALF = FOX_T // 2
FOX_FULL = ((slice(0, FOX_T), slice(0, FOX_T), None),)
FOX_DIAG = ((slice(0, FOX_HALF), slice(0, FOX_HALF), 0), (slice(FOX_HALF, FOX_T), slice(0, FOX_T), FOX_HALF))


def _causal_piece_mask(q_rows, k_rows, offset):
    shape = (q_rows.stop - q_rows.start, k_rows.stop - k_rows.start)
    row = lax.broadcasted_iota(jnp.int32, shape, 0)
    col = lax.broadcasted_iota(jnp.int32, shape, 1)
    return col <= row + offset


def _fox_fwd(q_slots, k_slots, v_slots):
    i_tab, j_tab, n_pairs = _causal_pairs(False)

    def body(i_tab, j_tab, q_ref, k_ref, v_ref, o_ref, lse_ref, m_s, acc_s):
        t = pl.program_id(1)
        i, j = i_tab[t], j_tab[t]

        @pl.when(j == 0)
        def _():
            m_s[...] = jnp.full_like(m_s, NEG_INF)
            acc_s[...] = jnp.zeros_like(acc_s)

        def step(pieces):
            jobs = [(h, piece) for h in range(FOX_FWD_HEADS) for piece in pieces]
            lanes = lambda h: slice(h * SLOT, (h + 1) * SLOT)
            scores = [lax.dot_general(q_ref[qr, lanes(h)], k_ref[kr, lanes(h)], NT_DIMS, preferred_element_type=F32)
                      for h, (qr, kr, _) in jobs]
            probs, alphas = [], []
            for idx, (h, (qr, kr, offset)) in enumerate(jobs):
                s = scores[idx]
                if offset is not None:
                    s = jnp.where(_causal_piece_mask(qr, kr, offset), s, NEG_INF)
                m_prev = m_s[h, qr, :]
                m_new = jnp.maximum(m_prev, jnp.max(s, axis=-1, keepdims=True))
                probs.append(jnp.exp(s - jnp.tile(m_new, (1, s.shape[1] // LANE))).astype(BF16))
                alphas.append(jnp.exp(m_prev - m_new))
                m_s[h, qr, :] = m_new
            for idx, (h, (qr, kr, _)) in enumerate(jobs):
                acc_s[h, qr, :] = alphas[idx] * acc_s[h, qr, :] + jnp.dot(
                    probs[idx], v_ref[kr, lanes(h)], preferred_element_type=F32)

        @pl.when(j < i)
        def _():
            step(FOX_FULL)

        @pl.when(j == i)
        def _():
            step(FOX_DIAG)
            outs, lses = [], []
            for h in range(FOX_FWD_HEADS):
                acc = acc_s[h]
                l = acc[:, HEAD_DIM:]
                outs.append(acc[:, :HEAD_DIM] / l)
                lses.append(m_s[h][:, :HEAD_DIM] + jnp.log(l))
            o_ref[...] = jnp.concatenate(outs, axis=1).astype(o_ref.dtype)
            lse_ref[...] = jnp.concatenate(lses, axis=1)

    qspec = pl.BlockSpec((FOX_T, FOX_FWD_HEADS * SLOT), lambda p, t, it, jt: (it[t], p))
    kspec = pl.BlockSpec((FOX_T, FOX_FWD_HEADS * SLOT), lambda p, t, it, jt: (jt[t], p))
    ospec = pl.BlockSpec((FOX_T, FOX_FWD_HEADS * HEAD_DIM), lambda p, t, it, jt: (it[t], p))
    return pl.pallas_call(
        body, name="fox_fwd",
        grid_spec=pltpu.PrefetchScalarGridSpec(
            num_scalar_prefetch=2, grid=(N_HEADS // FOX_FWD_HEADS, n_pairs),
            in_specs=[qspec, kspec, kspec], out_specs=[ospec, ospec],
            scratch_shapes=[pltpu.VMEM((FOX_FWD_HEADS, FOX_T, LANE), F32),
                            pltpu.VMEM((FOX_FWD_HEADS, FOX_T, SLOT), F32)]),
        out_shape=[_sds((SEQ, ATT_W), BF16), _sds((SEQ, ATT_W), F32)],
        compiler_params=_params("parallel", "arbitrary"),
    )(i_tab, j_tab, q_slots, k_slots, v_slots)


def _fox_bwd(q_slots, k_slots, v_slots, do_slots):
    i_tab, j_tab, n_pairs = _causal_pairs(True)

    def body(i_tab, j_tab, q_ref, k_ref, v_ref, do_ref, dq_ref, dk_ref, dv_ref):
        t = pl.program_id(1)
        i, j = i_tab[t], j_tab[t]

        @pl.when(t == 0)
        def _():
            dq_ref[...] = jnp.zeros_like(dq_ref)

        @pl.when(i == j)
        def _():
            dk_ref[...] = jnp.zeros_like(dk_ref)
            dv_ref[...] = jnp.zeros_like(dv_ref)

        def step(pieces):
            jobs = [(h, piece) for h in range(FOX_BWD_HEADS) for piece in pieces]
            lanes = lambda h: slice(h * SLOT, (h + 1) * SLOT)
            scores = [lax.dot_general(q_ref[qr, lanes(h)], k_ref[kr, lanes(h)], NT_DIMS, preferred_element_type=F32)
                      for h, (qr, kr, _) in jobs]
            dps = [lax.dot_general(do_ref[qr, lanes(h)], v_ref[kr, lanes(h)], NT_DIMS, preferred_element_type=F32)
                   for h, (qr, kr, _) in jobs]
            ps, dss = [], []
            for idx, (h, (qr, kr, offset)) in enumerate(jobs):
                p = jnp.exp(scores[idx])
                if offset is not None:
                    p = jnp.where(_causal_piece_mask(qr, kr, offset), p, 0.0)
                ps.append(p.astype(BF16))
                dss.append((p * dps[idx]).astype(BF16))
            for idx, (h, (qr, kr, _)) in enumerate(jobs):
                rows = pl.ds(pl.multiple_of(i * FOX_T + qr.start, FOX_HALF), qr.stop - qr.start)
                dv_ref[kr, lanes(h)] += lax.dot_general(ps[idx], do_ref[qr, lanes(h)], TN_DIMS,
                                                        preferred_element_type=F32)
                dk_ref[kr, lanes(h)] += lax.dot_general(dss[idx], q_ref[qr, lanes(h)], TN_DIMS,
                                                        preferred_element_type=F32)
                dq_ref[rows, lanes(h)] += jnp.dot(dss[idx], k_ref[kr, lanes(h)], preferred_element_type=F32)

        @pl.when(i > j)
        def _():
            step(FOX_FULL)

        @pl.when(i == j)
        def _():
            step(FOX_DIAG)

    qspec = pl.BlockSpec((FOX_T, FOX_BWD_HEADS * SLOT), lambda p, t, it, jt: (it[t], p))
    kspec = pl.BlockSpec((FOX_T, FOX_BWD_HEADS * SLOT), lambda p, t, it, jt: (jt[t], p))
    return pl.pallas_call(
        body, name="fox_bwd",
        grid_spec=pltpu.PrefetchScalarGridSpec(
            num_scalar_prefetch=2, grid=(N_HEADS // FOX_BWD_HEADS, n_pairs),
            in_specs=[qspec, kspec, kspec, qspec],
            out_specs=[pl.BlockSpec((SEQ, FOX_BWD_HEADS * SLOT), lambda p, t, it, jt: (0, p)), kspec, kspec]),
        out_shape=[_sds((SEQ, N_HEADS * SLOT), F32)] * 3,
        compiler_params=_params("arbitrary", "arbitrary"),
    )(i_tab, j_tab, q_slots, k_slots, v_slots, do_slots)


def _fox_unpack(dq_slots, dk_slots, dv_slots, dz, *, tm=512):
    def body(dq_ref, dk_ref, dv_ref, dz_in, o_ref, df_ref):
        lane = lax.broadcasted_iota(jnp.int32, (tm, LANE), 1)
        df = jnp.zeros((tm, LANE), F32)
        for h in range(N_HEADS):
            lo = h * SLOT
            for part, (ref, mult) in enumerate(((dq_ref, SCALE), (dk_ref, 1.0), (dv_ref, 1.0))):
                o_ref[:, part * ATT_W + h * HEAD_DIM:part * ATT_W + (h + 1) * HEAD_DIM] = (
                    ref[:, lo:lo + HEAD_DIM] * mult).astype(o_ref.dtype)
            rows = dq_ref[:, lo + HEAD_DIM:lo + HEAD_DIM + 1]
            cols = dk_ref[:, lo + HEAD_DIM + N_SPLIT:lo + HEAD_DIM + N_SPLIT + 1]
            df = jnp.where(lane == h, rows - cols, df)
        df_ref[...] = df

    wide = pl.BlockSpec((tm, N_HEADS * SLOT), lambda i: (i, 0))
    return pl.pallas_call(
        body, name="fox_unpack", grid=(SEQ // tm,), in_specs=[wide] * 3 + [ANY],
        out_specs=[pl.BlockSpec((tm, 3 * ATT_W), lambda i: (i, 0)), pl.BlockSpec((tm, LANE), lambda i: (i, 0))],
        out_shape=[_sds((SEQ, Z_MAIN), BF16), _sds((SEQ, LANE), F32)],
        input_output_aliases={3: 0},
        compiler_params=_params("parallel"),
    )(dq_slots, dk_slots, dv_slots, dz)


def _dil_bwd_prep(o, do, lse, *, tm=512):
    dilations = [d for _, d in DIL_PATTERNS]
    o_chunks = ATT_W // LANE

    def body(o_ref, do_ref, lse_ref, *rest):
        outs, (do_scr, lse_scr, dl_scr) = rest[:-3], rest[-3:]
        dov = do_ref[...].astype(F32)
        prod = o_ref[...].astype(F32) * dov
        lane = lax.broadcasted_iota(jnp.int32, (tm, LANE), 1)
        delta = jnp.zeros((tm, LANE), F32)
        for h in range(N_HEADS):
            delta = jnp.where(lane == h, jnp.sum(_head(prod, h), axis=1, keepdims=True), delta)
        for ch in range(o_chunks):
            do_scr[ch] = dov[:, ch * LANE:(ch + 1) * LANE]
        lse_scr[0] = lse_ref[...]
        dl_scr[0] = delta
        for k, d in enumerate(dilations):
            for scr, out in zip((do_scr, lse_scr, dl_scr), outs[3 * k:3 * k + 3]):
                _slabs_from_rows(scr, out, d)

    row = pl.BlockSpec((tm, ATT_W), lambda i: (i, 0))
    view = lambda d, w: pl.BlockSpec((tm // d, d * w), lambda i: (i, 0))
    outs = pl.pallas_call(
        body, name="dil_bwd_prep", grid=(SEQ // tm,),
        in_specs=[row, row, pl.BlockSpec((tm, LANE), lambda i: (i, 0))],
        out_specs=[view(d, w) for d in dilations for w in (ATT_W, LANE, LANE)],
        out_shape=[_sds((SEQ // d, d * w), t) for d in dilations for w, t in ((ATT_W, BF16), (LANE, F32), (LANE, F32))],
        scratch_shapes=[pltpu.VMEM((o_chunks, tm, LANE), F32), pltpu.VMEM((1, tm, LANE), F32),
                        pltpu.VMEM((1, tm, LANE), F32)],
        compiler_params=_params("parallel"),
    )(o, do, lse)
    return [outs[3 * k:3 * k + 3] for k in range(len(dilations))]


def _rope_tables():
    half = ROPE_DIM // 2
    inv_freq = np.float32(ROPE_THETA) ** (-np.arange(half, dtype=np.float32) * np.float32(2.0) / np.float32(ROPE_DIM))
    ang = np.arange(SEQ, dtype=np.float32)[:, None] * inv_freq.astype(np.float32)[None, :]
    cos, sin = jnp.asarray(np.cos(ang).astype(np.float32)), jnp.asarray(np.sin(ang).astype(np.float32))
    ones = jnp.ones((SEQ, HEAD_DIM - ROPE_DIM), F32)
    zeros = jnp.zeros((SEQ, HEAD_DIM - ROPE_DIM), F32)
    zh = jnp.zeros((SEQ, half), F32)
    c_tab = jnp.concatenate([cos, cos, ones], axis=1)
    a_tab = jnp.concatenate([-sin, zh, zeros], axis=1)
    b_tab = jnp.concatenate([zh, sin, zeros], axis=1)
    two = lambda t: jnp.concatenate([t, t], axis=1)
    return two(c_tab), two(a_tab), two(b_tab)


def _rotate(x, c_tab, a_tab, b_tab):
    return x * c_tab + pltpu.roll(x, LANE - ROPE_DIM // 2, 1) * a_tab + pltpu.roll(x, ROPE_DIM // 2, 1) * b_tab


def _rope_fwd(zm, tabs, *, tm=512):
    width = 3 * ATT_W
    dilations = [d for _, d in DIL_PATTERNS]

    def body(q_ref, k_ref, v_ref, c_ref, a_ref, b_ref, *rest):
        outs, scr = rest[:-1], rest[-1]
        per_part = ATT_W // LANE
        for part, (x_ref, mult) in enumerate(((q_ref, SCALE), (k_ref, 1.0))):
            for cc in range(per_part):
                sl = slice(cc * LANE, (cc + 1) * LANE)
                scr[part * per_part + cc] = _rotate(x_ref[:, sl].astype(F32), c_ref[...], a_ref[...], b_ref[...]) * mult
        for cc in range(per_part):
            scr[2 * per_part + cc] = v_ref[:, cc * LANE:(cc + 1) * LANE].astype(F32)
        for o_ref, d in zip(outs, dilations):
            for r in range(d):
                for ch in range(width // LANE):
                    o_ref[:, r * width + ch * LANE:r * width + (ch + 1) * LANE] = (
                        scr.at[ch][pl.ds(r, tm // d, stride=d), :].astype(o_ref.dtype))

    tab = pl.BlockSpec((tm, LANE), lambda i: (i, 0))
    col = lambda b: pl.BlockSpec((tm, ATT_W), lambda i: (i, b))
    return pl.pallas_call(
        body, name="rope_fwd", grid=(SEQ // tm,),
        in_specs=[col(3), col(4), col(5), tab, tab, tab],
        out_specs=[pl.BlockSpec((tm // d, d * width), lambda i: (i, 0)) for d in dilations],
        out_shape=[_sds((SEQ // d, d * width), BF16) for d in dilations],
        scratch_shapes=[pltpu.VMEM((width // LANE, tm, LANE), F32)],
        compiler_params=_params("parallel"),
    )(zm, zm, zm, *tabs)


def _dil_grad_combine(dqs, dks, dvs, tabs, dz, *, tm=256):
    dilations = [d for _, d in DIL_PATTERNS]
    chunks = ATT_W // LANE

    def body(*refs):
        groups = (refs[0:3], refs[3:6], refs[6:9])
        c_ref, a_ref, b_ref, _, o_ref, scr = refs[9:]

        def total(part, cc):
            acc = None
            for g, (ref, d) in enumerate(zip(groups[part], dilations)):
                term = ref[:, cc * LANE:(cc + 1) * LANE].astype(F32) if d == 1 else scr[part, g, cc]
                acc = term if acc is None else acc + term
            return acc

        for part in range(3):
            for g, (ref, d) in enumerate(zip(groups[part], dilations)):
                if d > 1:
                    _rows_from_slabs(ref, scr.at[part, g], d)
        for cc in range(chunks):
            for part in range(2):
                o_ref[:, part * ATT_W + cc * LANE:part * ATT_W + (cc + 1) * LANE] = _rotate(
                    total(part, cc), c_ref[...], -a_ref[...], -b_ref[...]).astype(o_ref.dtype)
            o_ref[:, 2 * ATT_W + cc * LANE:2 * ATT_W + (cc + 1) * LANE] = total(2, cc).astype(o_ref.dtype)

    view = lambda d: pl.BlockSpec((tm // d, d * ATT_W), lambda i: (i, 0))
    tab = pl.BlockSpec((tm, LANE), lambda i: (i, 0))
    return pl.pallas_call(
        body, name="dil_grad_combine", grid=(SEQ // tm,),
        in_specs=[view(d) for d in dilations] * 3 + [tab] * 3 + [ANY],
        out_specs=pl.BlockSpec((tm, 3 * ATT_W), lambda i: (i, 1)),
        out_shape=_sds((SEQ, Z_MAIN), BF16),
        input_output_aliases={12: 0},
        scratch_shapes=[pltpu.VMEM((3, len(dilations), chunks, tm, LANE), F32)],
        compiler_params=_params("parallel"),
    )(*dqs, *dks, *dvs, *tabs, dz)


def _dil_valid(n):
    qi = lax.broadcasted_iota(jnp.int32, (DIL_BLK, 2 * DIL_BLK), 0)
    ki = lax.broadcasted_iota(jnp.int32, (DIL_BLK, 2 * DIL_BLK), 1)
    dist = qi + DIL_BLK - ki
    return (dist >= 0) & (dist <= DIL_BLK) & ((n > 0) | (ki >= DIL_BLK))


def _dil_fwd(qkv_v, d):
    length = SEQ // d
    nb = length // DIL_BLK
    nsub = min(DIL_STEP_BLOCKS, nb)

    def body(q_ref, kp_ref, kc_ref, vp_ref, vc_ref, o_ref, lse_ref):
        m_step = pl.program_id(1)
        lane = lax.broadcasted_iota(jnp.int32, (DIL_BLK, LANE), 1)
        jobs = [(sub, h) for sub in range(nsub) for h in range(N_HEADS)]
        rows = lambda sub: slice(sub * DIL_BLK, (sub + 1) * DIL_BLK)
        cols = lambda h: slice(h * HEAD_DIM, (h + 1) * HEAD_DIM)

        def keys(prev_ref, cur_ref, sub, h):
            before = prev_ref[:, cols(h)] if sub == 0 else cur_ref[rows(sub - 1), cols(h)]
            return jnp.concatenate([before, cur_ref[rows(sub), cols(h)]], axis=0)

        scores = [lax.dot_general(q_ref[rows(sub), cols(h)], keys(kp_ref, kc_ref, sub, h), NT_DIMS,
                                  preferred_element_type=F32) for sub, h in jobs]
        ok = [_dil_valid(m_step)] + [_dil_valid(1)] * (nsub - 1)
        probs, inv_l, lse_all = [], [], [jnp.zeros((DIL_BLK, LANE), F32)] * nsub
        for idx, (sub, h) in enumerate(jobs):
            s = jnp.where(ok[sub], scores[idx], NEG_INF)
            m = jnp.max(s, axis=-1, keepdims=True)
            p = jnp.exp(s - m)
            l = jnp.sum(p, axis=-1, keepdims=True)
            probs.append(p.astype(BF16))
            inv_l.append(1.0 / l)
            lse_all[sub] = jnp.where(lane == h, m + jnp.log(l), lse_all[sub])
        outs = [jnp.dot(probs[idx], keys(vp_ref, vc_ref, sub, h), preferred_element_type=F32) * inv_l[idx]
                for idx, (sub, h) in enumerate(jobs)]
        for sub in range(nsub):
            o_ref[rows(sub), :] = jnp.concatenate(outs[sub * N_HEADS:(sub + 1) * N_HEADS], axis=1).astype(o_ref.dtype)
            lse_ref[rows(sub), :] = lse_all[sub]

    pair = lambda f: pl.BlockSpec((nsub * DIL_BLK, ATT_W), f)
    one = lambda f: pl.BlockSpec((DIL_BLK, ATT_W), f)
    before = lambda m: jnp.maximum(nsub * m - 1, 0)
    o, lse = pl.pallas_call(
        body, name=f"dil_fwd_d{d}", grid=(d, nb // nsub),
        in_specs=[pair(lambda r, m: (m, 3 * r)),
                  one(lambda r, m: (before(m), 3 * r + 1)), pair(lambda r, m: (m, 3 * r + 1)),
                  one(lambda r, m: (before(m), 3 * r + 2)), pair(lambda r, m: (m, 3 * r + 2))],
        out_specs=[pair(lambda r, m: (m, r)), pl.BlockSpec((nsub * DIL_BLK, LANE), lambda r, m: (m, r))],
        out_shape=[_sds((length, d * ATT_W), BF16), _sds((length, d * LANE), F32)],
        compiler_params=_params("parallel", "arbitrary"),
    )(qkv_v, qkv_v, qkv_v, qkv_v, qkv_v)
    return o, lse


def _rows_from_slabs(view_ref, scr, d):
    chunks, rows = scr.shape[0], scr.shape[1]
    for r in range(d):
        for ch in range(chunks):
            lo = (r * chunks + ch) * LANE
            scr.at[ch][pl.ds(r, rows // d, stride=d), :] = view_ref[:, lo:lo + LANE].astype(F32)


def _slabs_from_rows(scr, view_ref, d):
    chunks, rows = scr.shape[0], scr.shape[1]
    for r in range(d):
        for ch in range(chunks):
            lo = (r * chunks + ch) * LANE
            view_ref[:, lo:lo + LANE] = scr.at[ch][pl.ds(r, rows // d, stride=d), :].astype(view_ref.dtype)


def _dil_merge(os_, lses, *, tm=512):
    dilations = [d for _, d in DIL_PATTERNS]
    o_chunks = ATT_W // LANE

    def body(o0, o1, o2, l0, l1, l2, y_ref, lse_ref, o_scr, l_scr):
        os_nat, ls = [], []
        for g, (o_ref, l_ref, d) in enumerate(zip((o0, o1, o2), (l0, l1, l2), dilations)):
            if d == 1:
                os_nat.append(o_ref[...].astype(F32))
                ls.append(l_ref[...])
            else:
                _rows_from_slabs(o_ref, o_scr.at[g], d)
                _rows_from_slabs(l_ref, l_scr.at[g], d)
                os_nat.append(jnp.concatenate([o_scr[g, ch] for ch in range(o_chunks)], axis=1))
                ls.append(l_scr[g, 0])
        m = jnp.maximum(jnp.maximum(ls[0], ls[1]), ls[2])
        es = [jnp.exp(l - m) for l in ls]
        tot = es[0] + es[1] + es[2]
        lse_ref[...] = m + jnp.log(tot)
        alphas = [e / tot for e in es]
        outs = []
        for h in range(N_HEADS):
            acc = None
            for g in range(3):
                term = alphas[g][:, h:h + 1] * _head(os_nat[g], h)
                acc = term if acc is None else acc + term
            outs.append(acc)
        y_ref[...] = jnp.concatenate(outs, axis=1).astype(y_ref.dtype)

    row = pl.BlockSpec((tm, ATT_W), lambda i: (i, 0))
    vec = pl.BlockSpec((tm, LANE), lambda i: (i, 0))
    view = lambda d, w: pl.BlockSpec((tm // d, d * w), lambda i: (i, 0))
    return pl.pallas_call(
        body, name="dil_merge", grid=(SEQ // tm,),
        in_specs=[view(d, ATT_W) for d in dilations] + [view(d, LANE) for d in dilations], out_specs=[row, vec],
        out_shape=[_sds((SEQ, ATT_W), BF16), _sds((SEQ, LANE), F32)],
        scratch_shapes=[pltpu.VMEM((3, o_chunks, tm, LANE), F32), pltpu.VMEM((3, 1, tm, LANE), F32)],
        compiler_params=_params("parallel"),
    )(*os_, *lses)


def _dil_bwd(qkv_v, do_v, lse_v, dl_v, d):
    length = SEQ // d
    nb = length // DIL_BLK
    nsub = min(DIL_STEP_BLOCKS, nb)
    n_steps = nb // nsub

    def body(q_ref, kp_ref, kc_ref, vp_ref, vc_ref, lse_ref, dl_ref, do_ref, dq_ref, dk_ref, dv_ref, dk_s, dv_s):
        m_step = pl.program_id(1)

        @pl.when(m_step == 0)
        def _():
            dk_s[...] = jnp.zeros_like(dk_s)
            dv_s[...] = jnp.zeros_like(dv_s)

        jobs = [(sub, h) for sub in range(nsub) for h in range(N_HEADS)]
        rows = lambda sub: slice(sub * DIL_BLK, (sub + 1) * DIL_BLK)
        cols = lambda h: slice(h * HEAD_DIM, (h + 1) * HEAD_DIM)

        def keys(prev_ref, cur_ref, sub, h):
            before = prev_ref[:, cols(h)] if sub == 0 else cur_ref[rows(sub - 1), cols(h)]
            return jnp.concatenate([before, cur_ref[rows(sub), cols(h)]], axis=0)

        kks = [keys(kp_ref, kc_ref, sub, h) for sub, h in jobs]
        scores = [lax.dot_general(q_ref[rows(sub), cols(h)], kks[idx], NT_DIMS, preferred_element_type=F32)
                  for idx, (sub, h) in enumerate(jobs)]
        dps = [lax.dot_general(do_ref[rows(sub), cols(h)], keys(vp_ref, vc_ref, sub, h), NT_DIMS,
                               preferred_element_type=F32) for sub, h in jobs]
        ok = [_dil_valid(m_step)] + [_dil_valid(1)] * (nsub - 1)
        ps, dss = [], []
        for idx, (sub, h) in enumerate(jobs):
            p = jnp.where(ok[sub], jnp.exp(scores[idx] - lse_ref[rows(sub), h:h + 1]), 0.0)
            ps.append(p.astype(BF16))
            dss.append((p * (dps[idx] - dl_ref[rows(sub), h:h + 1])).astype(BF16))
        dqs = [jnp.dot(dss[idx], kks[idx], preferred_element_type=F32) * SCALE for idx in range(len(jobs))]
        dkks = [lax.dot_general(dss[idx], q_ref[rows(sub), cols(h)], TN_DIMS, preferred_element_type=F32)
                for idx, (sub, h) in enumerate(jobs)]
        dvvs = [lax.dot_general(ps[idx], do_ref[rows(sub), cols(h)], TN_DIMS, preferred_element_type=F32)
                for idx, (sub, h) in enumerate(jobs)]
        for sub in range(nsub):
            dq_ref[rows(sub), :] = jnp.concatenate(dqs[sub * N_HEADS:(sub + 1) * N_HEADS], axis=1).astype(dq_ref.dtype)
        base = m_step * (nsub * DIL_BLK)
        blocks = [pl.ds(pl.multiple_of(jnp.maximum(base - DIL_BLK, 0), DIL_BLK), DIL_BLK)]
        blocks += [pl.ds(pl.multiple_of(base + s * DIL_BLK, DIL_BLK), DIL_BLK) for s in range(nsub)]
        for acc, parts in ((dk_s, dkks), (dv_s, dvvs)):
            top = lambda sub: jnp.concatenate([parts[sub * N_HEADS + h][:DIL_BLK] for h in range(N_HEADS)], axis=1)
            bottom = lambda sub: jnp.concatenate([parts[sub * N_HEADS + h][DIL_BLK:] for h in range(N_HEADS)], axis=1)
            acc[blocks[0], :] += top(0)
            for s in range(nsub):
                acc[blocks[s + 1], :] += bottom(s) + top(s + 1) if s + 1 < nsub else bottom(s)

        @pl.when(m_step == n_steps - 1)
        def _():
            dk_ref[...] = dk_s[...].astype(dk_ref.dtype)
            dv_ref[...] = dv_s[...].astype(dv_ref.dtype)

    pair = lambda f: pl.BlockSpec((nsub * DIL_BLK, ATT_W), f)
    one = lambda f: pl.BlockSpec((DIL_BLK, ATT_W), f)
    vec = lambda f: pl.BlockSpec((nsub * DIL_BLK, LANE), f)
    whole = pl.BlockSpec((length, ATT_W), lambda r, m: (0, r))
    before = lambda m: jnp.maximum(nsub * m - 1, 0)
    outs = pl.pallas_call(
        body, name=f"dil_bwd_d{d}", grid=(d, n_steps),
        in_specs=[pair(lambda r, m: (m, 3 * r)),
                  one(lambda r, m: (before(m), 3 * r + 1)), pair(lambda r, m: (m, 3 * r + 1)),
                  one(lambda r, m: (before(m), 3 * r + 2)), pair(lambda r, m: (m, 3 * r + 2)),
                  vec(lambda r, m: (m, r)), vec(lambda r, m: (m, r)), pair(lambda r, m: (m, r))],
        out_specs=[pair(lambda r, m: (m, r)), whole, whole],
        out_shape=[_sds((length, d * ATT_W), BF16)] * 3,
        scratch_shapes=[pltpu.VMEM((length, ATT_W), F32), pltpu.VMEM((length, ATT_W), F32)],
        compiler_params=_params("arbitrary", "arbitrary"),
    )(qkv_v, qkv_v, qkv_v, qkv_v, qkv_v, lse_v, dl_v, do_v)
    return outs


def _sigmoid(x):
    return 1.0 / (1.0 + jnp.exp(-x))


def _mix_fwd(ya, yb, w_oa, w_ob, zm, *, tm=512):
    def body(ya_ref, yb_ref, wa_ref, wb_ref, ga_ref, gb_ref, pa_ref, pb_ref, mix_ref):
        pa = jnp.dot(ya_ref[...], wa_ref[...], preferred_element_type=F32)
        pb = jnp.dot(yb_ref[...], wb_ref[...], preferred_element_type=F32)
        pa_ref[...] = pa.astype(pa_ref.dtype)
        pb_ref[...] = pb.astype(pb_ref.dtype)
        mix_ref[...] = (_sigmoid(ga_ref[...].astype(F32)) * pa + _sigmoid(gb_ref[...].astype(F32)) * pb
                        ).astype(mix_ref.dtype)

    row = pl.BlockSpec((tm, ATT_W), lambda i: (i, 0))
    wsp = pl.BlockSpec((ATT_W, D_MODEL), lambda i: (0, 0))
    wide = pl.BlockSpec((tm, D_MODEL), lambda i: (i, 0))
    return pl.pallas_call(
        body, name="mix_fwd", grid=(SEQ // tm,),
        in_specs=[row, row, wsp, wsp, pl.BlockSpec((tm, D_MODEL), lambda i: (i, 3)),
                  pl.BlockSpec((tm, D_MODEL), lambda i: (i, 4))],
        out_specs=[wide] * 3, out_shape=[_sds((SEQ, D_MODEL), BF16)] * 3,
        compiler_params=_params("parallel"),
    )(ya, yb, w_oa, w_ob, zm, zm)


def _gate_bwd(dmix, zm, p, gate_block, dz, *, name, tm=512):
    def body(dm_ref, g_ref, p_ref, *rest):
        dp_ref, dz_ref = rest[-2], rest[-1]
        dm = dm_ref[...].astype(F32)
        s = _sigmoid(g_ref[...].astype(F32))
        dp_ref[...] = (dm * s).astype(dp_ref.dtype)
        dz_ref[...] = (dm * p_ref[...].astype(F32) * s * (1.0 - s)).astype(dz_ref.dtype)

    wide = pl.BlockSpec((tm, D_MODEL), lambda i: (i, 0))
    gate = pl.BlockSpec((tm, D_MODEL), lambda i: (i, gate_block))
    extra = [] if dz is None else [dz]
    return pl.pallas_call(
        body, name=name, grid=(SEQ // tm,),
        in_specs=[wide, gate, wide] + [ANY] * len(extra),
        out_specs=[wide, gate],
        out_shape=[_sds((SEQ, D_MODEL), BF16), _sds((SEQ, Z_MAIN), BF16)],
        input_output_aliases={3: 1} if extra else {},
        compiler_params=_params("parallel"),
    )(dmix, zm, p, *extra)


def _out_fwd(mixed, w_out, x, g_post, g_pre, *, tm=512):
    def body(m_ref, w_ref, x_ref, gp_ref, gn_ref, y_ref, x2_ref, h_ref):
        y = jnp.dot(m_ref[...], w_ref[...], preferred_element_type=F32)
        y_ref[...] = y
        r = lax.rsqrt(jnp.mean(y * y, axis=-1, keepdims=True) + RMS_EPS)
        x2 = x_ref[...] + y * r * gp_ref[...]
        x2_ref[...] = x2
        r2 = lax.rsqrt(jnp.mean(x2 * x2, axis=-1, keepdims=True) + RMS_EPS)
        h_ref[...] = (x2 * r2 * gn_ref[...]).astype(h_ref.dtype)

    row = pl.BlockSpec((tm, D_MODEL), lambda i: (i, 0))
    vec = pl.BlockSpec((1, D_MODEL), lambda i: (0, 0))
    return pl.pallas_call(
        body, name="out_fwd", grid=(SEQ // tm,),
        in_specs=[row, pl.BlockSpec((D_MODEL, D_MODEL), lambda i: (0, 0)), row, vec, vec],
        out_specs=[row] * 3,
        out_shape=[_sds((SEQ, D_MODEL), F32), _sds((SEQ, D_MODEL), F32), _sds((SEQ, D_MODEL), BF16)],
        compiler_params=_params("parallel"),
    )(mixed, w_out, x, g_post, g_pre)


FFN_HALF = 256
FFN_TN = 2 * FFN_HALF
FFN_NJ = D_FF // FFN_HALF
FFN_GROUP = 2 * SUBLANE
UP_TM = 1024


def _ffn_interleave(t):
    lead = t.shape[:-1]
    return jnp.swapaxes(t.reshape(*lead, 2, FFN_NJ, FFN_HALF), -3, -2).reshape(*lead, 2 * D_FF)


def _ffn_deinterleave(t):
    lead = t.shape[:-1]
    return jnp.swapaxes(t.reshape(*lead, FFN_NJ, 2, FFN_HALF), -3, -2).reshape(*lead, 2 * D_FF)


W_IN_SHARD = (Z_MAIN + N_HEADS) // N_DEV
FORGET_LO = 3 * ATT_W


def _w_in_from_shards(shards, *, tm=256):
    def columns(g_ref, lo, width):
        p, off = divmod(lo, W_IN_SHARD)
        if off + width <= W_IN_SHARD:
            return g_ref[p, :, off:off + width]
        first = W_IN_SHARD - off
        return jnp.concatenate([g_ref[p, :, off:], g_ref[p + 1, :, :width - first]], axis=1)

    def body(g_ref, main_ref, f_ref):
        for t in range(Z_MAIN // LANE):
            lo = t * LANE
            main_ref[:, lo:lo + LANE] = columns(g_ref, lo if lo < FORGET_LO else lo + N_HEADS, LANE)
        f_ref[...] = jnp.concatenate([columns(g_ref, FORGET_LO, N_HEADS),
                                      jnp.zeros((tm, F_PAD - N_HEADS), f_ref.dtype)], axis=1)

    return pl.pallas_call(
        body, name="w_in_from_shards", grid=(D_MODEL // tm,),
        in_specs=[pl.BlockSpec((N_DEV, tm, W_IN_SHARD), lambda i: (0, i, 0))],
        out_specs=[pl.BlockSpec((tm, Z_MAIN), lambda i: (i, 0)), pl.BlockSpec((tm, F_PAD), lambda i: (i, 0))],
        out_shape=[_sds((D_MODEL, Z_MAIN), shards.dtype), _sds((D_MODEL, F_PAD), shards.dtype)],
        compiler_params=_params("parallel"),
    )(shards)


def _w_in_to_shards(g_main, g_f, *, tm=256):
    def natural(main_ref, f_ref, lo, width):
        pieces, hi = [], lo + width
        for ref, start, stop, shift in ((main_ref, 0, FORGET_LO, 0), (f_ref, FORGET_LO, FORGET_LO + N_HEADS, FORGET_LO),
                                        (main_ref, FORGET_LO + N_HEADS, Z_MAIN + N_HEADS, N_HEADS)):
            a, b = max(lo, start), min(hi, stop)
            if a < b:
                pieces.append(ref[:, a - shift:b - shift])
        return pieces[0] if len(pieces) == 1 else jnp.concatenate(pieces, axis=1)

    def body(main_ref, f_ref, o_ref):
        for p in range(N_DEV):
            for q in range(-(-W_IN_SHARD // LANE)):
                width = min(LANE, W_IN_SHARD - q * LANE)
                o_ref[p, :, q * LANE:q * LANE + width] = natural(main_ref, f_ref, p * W_IN_SHARD + q * LANE, width)

    return pl.pallas_call(
        body, name="w_in_to_shards", grid=(D_MODEL // tm,),
        in_specs=[pl.BlockSpec((tm, Z_MAIN), lambda i: (i, 0)), pl.BlockSpec((tm, F_PAD), lambda i: (i, 0))],
        out_specs=pl.BlockSpec((N_DEV, tm, W_IN_SHARD), lambda i: (0, i, 0)),
        out_shape=_sds((N_DEV, D_MODEL, W_IN_SHARD), g_main.dtype),
        compiler_params=_params("parallel"),
    )(g_main, g_f)


W_UP_SHARD = 2 * D_FF // N_DEV


def _w_up_lane_tile(k):
    block = k // 2
    return (2 * (block % FFN_NJ) + block // FFN_NJ) * FFN_HALF + (k % 2) * LANE


def _w_up_from_shards(shards, *, tm=256):
    def body(g_ref, o_ref):
        for k in range(2 * D_FF // LANE):
            p, off = divmod(k * LANE, W_UP_SHARD)
            if off + LANE <= W_UP_SHARD:
                tile = g_ref[p, :, off:off + LANE]
            else:
                tile = jnp.concatenate([g_ref[p, :, off:], g_ref[p + 1, :, :off + LANE - W_UP_SHARD]], axis=1)
            dst = _w_up_lane_tile(k)
            o_ref[:, dst:dst + LANE] = tile

    return pl.pallas_call(
        body, name="w_up_from_shards", grid=(D_MODEL // tm,),
        in_specs=[pl.BlockSpec((N_DEV, tm, W_UP_SHARD), lambda i: (0, i, 0))],
        out_specs=pl.BlockSpec((tm, 2 * D_FF), lambda i: (i, 0)),
        out_shape=_sds((D_MODEL, 2 * D_FF), shards.dtype),
        compiler_params=_params("parallel"),
    )(shards)


def _w_up_to_shards(t, *, tm=256):
    def body(x_ref, o_ref):
        for p in range(N_DEV):
            for q in range(-(-W_UP_SHARD // LANE)):
                width = min(LANE, W_UP_SHARD - q * LANE)
                k, off = divmod(p * W_UP_SHARD + q * LANE, LANE)
                src = _w_up_lane_tile(k)
                if off == 0:
                    tile = x_ref[:, src:src + width]
                else:
                    tile = x_ref[:, src + off:src + LANE]
                    if width > LANE - off:
                        nxt = _w_up_lane_tile(k + 1)
                        tile = jnp.concatenate([tile, x_ref[:, nxt:nxt + width - (LANE - off)]], axis=1)
                o_ref[p, :, q * LANE:q * LANE + width] = tile

    return pl.pallas_call(
        body, name="w_up_to_shards", grid=(D_MODEL // tm,),
        in_specs=[pl.BlockSpec((tm, 2 * D_FF), lambda i: (i, 0))],
        out_specs=pl.BlockSpec((N_DEV, tm, W_UP_SHARD), lambda i: (0, i, 0)),
        out_shape=_sds((N_DEV, D_MODEL, W_UP_SHARD), t.dtype),
        compiler_params=_params("parallel"),
    )(t)


def _gelu_parts(a):
    c = math.sqrt(2.0 / math.pi)
    a2 = a * a
    t = jnp.tanh((c * a) * (1.0 + 0.044715 * a2))
    half_a, one_t = 0.5 * a, 1.0 + t
    gelu = half_a * one_t
    dgelu = 0.5 * one_t + half_a * (1.0 - t * t) * (c + (3.0 * 0.044715 * c) * a2)
    return gelu, dgelu


def _row_masks(down):
    row = lax.broadcasted_iota(jnp.int32, (SUBLANE, FFN_TN), 0)
    return (row < 1, row < 2) if down else (row >= SUBLANE - 1, row >= SUBLANE - 2)


def _rolled(x, down):
    return (pltpu.roll(x, 1, 0), pltpu.roll(x, 2, 0)) if down else (
        pltpu.roll(x, SUBLANE - 1, 0), pltpu.roll(x, SUBLANE - 2, 0))


def _shifted(cur_rolled, neighbour_rolled, masks):
    return (jnp.where(masks[0], neighbour_rolled[0], cur_rolled[0]),
            jnp.where(masks[1], neighbour_rolled[1], cur_rolled[1]))


def _conv_consts(w_ref, b_ref):
    shape = (SUBLANE, FFN_TN)
    return [jnp.broadcast_to(w_ref[k:k + 1, :], shape) for k in range(3)] + [jnp.broadcast_to(b_ref[...], shape)]


def _up_conv_fwd(h2, w_up, conv_w, conv_b):
    nrow = SEQ // UP_TM
    n_tiles = FFN_NJ * nrow
    n_groups = UP_TM // FFN_GROUP

    def body(h_ref, wu_ref, w_ref, b_ref, u_ref, ab_ref, m_ref, ua_s, ub_s, c1_s, c2_s):
        k = pl.program_id(0)

        @pl.when(k == 0)
        def _():
            ub_s[...] = jnp.zeros_like(ub_s)

        @pl.when(jnp.maximum(k - 1, 0) % nrow == 0)
        def _():
            c1_s[...] = jnp.zeros_like(c1_s)
            c2_s[...] = jnp.zeros_like(c2_s)

        def step(write_s, read_s):
            h_rows = pl.ds(pl.multiple_of((this(k) % nrow) * UP_TM, UP_TM), UP_TM)
            u = jnp.dot(h_ref[h_rows, :], wu_ref[...], preferred_element_type=F32)
            write_s[...] = u
            u_ref[...] = u.astype(u_ref.dtype)
            w0, w1, w2, bias = _conv_consts(w_ref, b_ref)
            masks = _row_masks(True)
            above = (c1_s[...], c2_s[...])
            for g in range(n_groups):
                rows = slice(g * FFN_GROUP, (g + 1) * FFN_GROUP)
                x = read_s[rows, :]
                convs = []
                for c in range(2):
                    cur = x[c * SUBLANE:(c + 1) * SUBLANE]
                    cur_rolled = _rolled(cur, True)
                    s1, s2 = _shifted(cur_rolled, above, masks)
                    convs.append(w0 * s2 + w1 * s1 + w2 * cur + bias)
                    above = cur_rolled
                y = jnp.concatenate(convs, axis=0)
                ab_ref[rows, :] = y.astype(ab_ref.dtype)
                m_ref[rows, :] = (_gelu_parts(y[:, :FFN_HALF])[0] * y[:, FFN_HALF:]).astype(m_ref.dtype)
            c1_s[...], c2_s[...] = above

        @pl.when(k % 2 == 0)
        def _():
            step(ua_s, ub_s)

        @pl.when(k % 2 == 1)
        def _():
            step(ub_s, ua_s)

    this = lambda k: jnp.minimum(k, n_tiles - 1)
    last = lambda k: jnp.maximum(k - 1, 0)
    blk = lambda tile: pl.BlockSpec((UP_TM, FFN_TN), lambda k: (tile(k) % nrow, tile(k) // nrow))
    return pl.pallas_call(
        body, name="up_conv_fwd", grid=(n_tiles + 1,),
        in_specs=[pl.BlockSpec((SEQ, D_MODEL), lambda k: (0, 0)),
                  pl.BlockSpec((D_MODEL, FFN_TN), lambda k: (0, this(k) // nrow)),
                  pl.BlockSpec((3, FFN_TN), lambda k: (0, last(k) // nrow)),
                  pl.BlockSpec((1, FFN_TN), lambda k: (0, last(k) // nrow))],
        out_specs=[blk(this), blk(last), pl.BlockSpec((UP_TM, FFN_HALF), lambda k: (last(k) % nrow, last(k) // nrow))],
        out_shape=[_sds((SEQ, 2 * D_FF), BF16), _sds((SEQ, 2 * D_FF), BF16), _sds((SEQ, D_FF), BF16)],
        scratch_shapes=[pltpu.VMEM((UP_TM, FFN_TN), F32), pltpu.VMEM((UP_TM, FFN_TN), F32),
                        pltpu.VMEM((SUBLANE, FFN_TN), F32), pltpu.VMEM((SUBLANE, FFN_TN), F32)],
        compiler_params=_params("arbitrary"),
    )(h2, w_up, conv_w, conv_b)


def _ffn_mid_bwd(dy2, w_down, u, ab, conv_w):
    nrow = SEQ // UP_TM
    n_tiles = FFN_NJ * nrow
    n_groups = UP_TM // FFN_GROUP
    this = lambda k: jnp.minimum(k, n_tiles - 1)
    last = lambda k: jnp.maximum(k - 1, 0)
    row_of = lambda tile: nrow - 1 - tile % nrow

    def body(dy_ref, wd_ref, u_ref, ab_ref, w_ref, du_ref, gw_ref, gb_ref, c_s, dma_s, dmb_s):
        k = pl.program_id(0)

        @pl.when(k == 0)
        def _():
            dmb_s[...] = jnp.zeros_like(dmb_s)

        @pl.when(last(k) % nrow == 0)
        def _():
            c_s[...] = jnp.zeros_like(c_s)
            gw_ref[...] = jnp.zeros_like(gw_ref)
            gb_ref[...] = jnp.zeros_like(gb_ref)

        def step(write_s, read_s):
            dy_rows = pl.ds(pl.multiple_of(row_of(this(k)) * UP_TM, UP_TM), UP_TM)
            write_s[...] = lax.dot_general(dy_ref[dy_rows, :], wd_ref[...], NT_DIMS,
                                           preferred_element_type=F32)
            taps = [jnp.broadcast_to(w_ref[t:t + 1, :], (SUBLANE, FFN_TN)) for t in range(3)]
            masks = _row_masks(False)
            below = _rolled(c_s[...], False)
            acc = [jnp.zeros((SUBLANE, FFN_TN), F32)] * 4
            for g in reversed(range(n_groups)):
                rows = slice(g * FFN_GROUP, (g + 1) * FFN_GROUP)
                x, y, dmv = u_ref[rows, :].astype(F32), ab_ref[rows, :].astype(F32), read_s[rows, :]
                gelu, dgelu = _gelu_parts(y[:, :FFN_HALF])
                d = jnp.concatenate([dmv * y[:, FFN_HALF:] * dgelu, dmv * gelu], axis=1)
                pre = [None, None]
                for c in (1, 0):
                    sl = slice(c * SUBLANE, (c + 1) * SUBLANE)
                    cur, xs = d[sl], x[sl]
                    cur_rolled = _rolled(cur, False)
                    up1, up2 = _shifted(cur_rolled, below, masks)
                    acc = [acc[0] + up2 * xs, acc[1] + up1 * xs, acc[2] + cur * xs, acc[3] + cur]
                    pre[c] = taps[2] * cur + taps[1] * up1 + taps[0] * up2
                    below = cur_rolled
                du_ref[rows, :] = jnp.concatenate(pre, axis=0).astype(du_ref.dtype)
            c_s[...] = pltpu.roll(below[0], 1, 0)
            for t in range(3):
                gw_ref[t:t + 1, :] += jnp.sum(acc[t], axis=0, keepdims=True)
            gb_ref[...] += jnp.sum(acc[3], axis=0, keepdims=True)

        @pl.when(k % 2 == 0)
        def _():
            step(dma_s, dmb_s)

        @pl.when(k % 2 == 1)
        def _():
            step(dmb_s, dma_s)

    blk = pl.BlockSpec((UP_TM, FFN_TN), lambda k: (row_of(last(k)), last(k) // nrow))
    col = lambda rows: pl.BlockSpec((rows, FFN_TN), lambda k: (0, last(k) // nrow))
    return pl.pallas_call(
        body, name="ffn_mid_bwd", grid=(n_tiles + 1,),
        in_specs=[pl.BlockSpec((SEQ, D_MODEL), lambda k: (0, 0)),
                  pl.BlockSpec((FFN_HALF, D_MODEL), lambda k: (this(k) // nrow, 0)), blk, blk, col(3)],
        out_specs=[blk, col(3), col(1)],
        out_shape=[_sds((SEQ, 2 * D_FF), BF16), _sds((3, 2 * D_FF), F32), _sds((1, 2 * D_FF), F32)],
        scratch_shapes=[pltpu.VMEM((SUBLANE, FFN_TN), F32), pltpu.VMEM((UP_TM, FFN_HALF), F32),
                        pltpu.VMEM((UP_TM, FFN_HALF), F32)],
        compiler_params=_params("arbitrary"),
    )(dy2, w_down, u, ab, conv_w)


def _down_fwd(m, w_down, x2, g_post, target, *, tm=512):
    def body(m_ref, w_ref, x2_ref, g_ref, t_ref, dout_ref, dy_ref, gg_ref, loss_ref):
        @pl.when(pl.program_id(0) == 0)
        def _():
            gg_ref[...] = jnp.zeros_like(gg_ref)
            loss_ref[...] = jnp.zeros_like(loss_ref)

        y = jnp.dot(m_ref[...], w_ref[...], preferred_element_type=F32)
        r = lax.rsqrt(jnp.mean(y * y, axis=-1, keepdims=True) + RMS_EPS)
        yn = y * r
        diff = (x2_ref[...] + yn * g_ref[...]) - t_ref[...]
        loss_ref[...] += jnp.sum(diff * diff)
        dout = diff * (1.0 / D_MODEL)
        dout_ref[...] = dout
        gg_ref[...] += jnp.sum(dout * yn, axis=0, keepdims=True)
        dn = dout * g_ref[...]
        dy_ref[...] = (r * (dn - yn * jnp.mean(dn * yn, axis=-1, keepdims=True))).astype(dy_ref.dtype)

    row = pl.BlockSpec((tm, D_MODEL), lambda i: (i, 0))
    vec = pl.BlockSpec((1, D_MODEL), lambda i: (0, 0))
    return pl.pallas_call(
        body, name="down_fwd", grid=(SEQ // tm,),
        in_specs=[pl.BlockSpec((tm, D_FF), lambda i: (i, 0)), pl.BlockSpec((D_FF, D_MODEL), lambda i: (0, 0)),
                  row, vec, row],
        out_specs=[row, row, vec, pl.BlockSpec((1, LANE), lambda i: (0, 0))],
        out_shape=[_sds((SEQ, D_MODEL), F32), _sds((SEQ, D_MODEL), BF16), _sds((1, D_MODEL), F32),
                   _sds((1, LANE), F32)],
        compiler_params=_params("arbitrary"),
    )(m, w_down, x2, g_post, target)


def _local_step(x, target, w_main, w_f, b_forget, conv_b, g_pre_mix, g_post_mix, g_pre_ffn, g_post_ffn,
                proj_weights, ffn_weights, ffn_grads_ready, proj_grads_ready, mixer_grads_ready):
    mm = _matmul
    tabs = _rope_tables()

    h1 = _rms_fwd(x, g_pre_mix, name="rms_pre_mix")
    zm = mm(h1, w_main, out_dtype=BF16, tm=2048, tn=1024, tk=1024, name="in_proj")
    zf = mm(h1, w_f, out_dtype=F32, tm=2048, tn=F_PAD, tk=1024, name="in_proj_forget")
    f_row, sg_row = _fox_prep(zf[:, :N_HEADS].T, b_forget.reshape(N_HEADS, 1))
    f_cols = jnp.pad(f_row.T, ((0, 0), (0, LANE - N_HEADS)))
    q_slots, k_slots, v_slots = _fox_pack_fwd(zm, f_cols)
    ya, lse_a = _fox_fwd(q_slots, k_slots, v_slots)
    qkv_d = dict(zip([d for _, d in DIL_PATTERNS], _rope_fwd(zm, tabs)))
    dil = [_dil_fwd(qkv_d[d], d) for _, d in DIL_PATTERNS]
    yb, lse_b = _dil_merge([o for o, _ in dil], [l for _, l in dil])
    w_oa, w_ob, w_out = proj_weights(yb)
    pa, pb, mixed = _mix_fwd(ya, yb, w_oa, w_ob, zm)
    y1, x2, h2 = _out_fwd(mixed, w_out, x, g_post_mix, g_pre_ffn)
    w_up, conv_w, w_down = ffn_weights(h2)
    u, ab, m = _up_conv_fwd(h2, w_up, conv_w, _ffn_interleave(conv_b))
    dout, dy2, gg_post_ffn, sq_err = _down_fwd(m, w_down, x2, g_post_ffn, target)

    g_w_down = mm(m, dy2, ta=True, out_dtype=BF16, tm=D_FF // 2, tn=1024, tk=2048, name="grad_w_down")
    du, g_conv_w, g_conv_b = _ffn_mid_bwd(dy2, w_down, u, ab, conv_w)
    g_w_up = mm(h2, du, ta=True, out_dtype=BF16, tm=1024, tn=D_FF // 2, tk=2048, name="grad_w_up")
    tok = ffn_grads_ready(dict(w_down=g_w_down, w_up_blocks=g_w_up, conv_w=_ffn_deinterleave(g_conv_w)))
    dh2 = mm(du, w_up, tb=True, out_dtype=BF16, tm=512, tn=1024, tk=2 * D_FF, name="d_h2")

    dx2, dy1, gg_pre_ffn, gg_post_mix = _rms_pair_bwd([dh2], x2, g_pre_ffn, dout, y1, g_post_mix + tok)
    g_w_out = mm(mixed, dy1, ta=True, out_dtype=BF16, tm=1024, tn=1024, tk=2048, name="grad_w_out")
    dmix = mm(dy1, w_out, tb=True, out_dtype=BF16, tm=2048, tn=1024, tk=1024, name="d_mixed")
    dpa, dz = _gate_bwd(dmix, zm, pa, 3, None, name="gate_bwd_fox")
    dpb, dz = _gate_bwd(dmix, zm, pb, 4, dz, name="gate_bwd_dil")
    g_w_oa = mm(ya, dpa, ta=True, out_dtype=BF16, tm=512, tn=1024, tk=SEQ, name="grad_w_o_fox")
    g_w_ob = mm(yb, dpb, ta=True, out_dtype=BF16, tm=512, tn=1024, tk=SEQ, name="grad_w_o_dil")
    tok = proj_grads_ready(dict(w_o_fox=g_w_oa, w_o_dil=g_w_ob, w_out=g_w_out))
    dya = mm(dpa, w_oa, tb=True, out_dtype=BF16, tm=2048, tn=512, tk=1024, name="d_y_fox")
    dyb = mm(dpb, w_ob, tb=True, out_dtype=BF16, tm=2048, tn=512, tk=1024, name="d_y_dil")

    qb_slots, do_slots = _fox_pack_bwd(zm, f_cols + tok, lse_a, ya, dya)
    dz, df_cols = _fox_unpack(*_fox_bwd(qb_slots, k_slots, v_slots, do_slots), dz)
    dfa_t, g_b_forget = _fox_post_bwd(df_cols[:, :N_HEADS].T, sg_row)

    rows_d = _dil_bwd_prep(yb, dyb, lse_b)
    dil_g = [_dil_bwd(qkv_d[d], *rows_d[k], d) for k, (_, d) in enumerate(DIL_PATTERNS)]
    dz = _dil_grad_combine([g[0] for g in dil_g], [g[1] for g in dil_g], [g[2] for g in dil_g], tabs, dz)

    dzf = jnp.pad(dfa_t.T, ((0, 0), (0, F_PAD - N_HEADS)))
    g_w_main = mm(h1, dz, ta=True, out_dtype=BF16, tm=1024, tn=Z_MAIN // 4, tk=2048, name="grad_w_in")
    g_w_f = mm(h1, dzf, ta=True, out_dtype=BF16, tm=1024, tn=F_PAD, tk=1024, name="grad_w_in_forget")
    tok = mixer_grads_ready(dict(w_main=g_w_main, w_f=g_w_f))
    dh1 = [mm(dz, w_main, tb=True, out_dtype=BF16, tm=512, tn=1024, tk=Z_MAIN, name="d_h1"),
           mm(dzf + tok, w_f, tb=True, out_dtype=BF16, tm=2048, tn=1024, tk=F_PAD, name="d_h1_forget")]
    grad_x, gg_pre_mix = _rms_bwd(dh1, x, g_pre_mix, dx2, out_dtype=F32, name="rms_pre_mix_bwd")

    grads = dict(
        b_forget=g_b_forget.reshape(1, N_HEADS), conv_b=_ffn_deinterleave(g_conv_b),
        g_pre_mix=gg_pre_mix, g_post_mix=gg_post_mix, g_pre_ffn=gg_pre_ffn, g_post_ffn=gg_post_ffn)
    return sq_err, grad_x, grads


def _exchange(arrays, scatter, *, name):
    n = len(arrays)
    scatters = [scatter] * n if isinstance(scatter, bool) else list(scatter)

    def body(*refs):
        ins, outs = refs[:n], refs[n:2 * n]
        send_sems, recv_sems, local_sems = refs[2 * n:]
        me, peers = _peers()

        def remote(a, k):
            dev, slot = peers[k]
            return pltpu.make_async_remote_copy(
                src_ref=ins[a].at[slot] if scatters[a] else ins[a], dst_ref=outs[a].at[me],
                send_sem=send_sems.at[a, k], recv_sem=recv_sems.at[a, k],
                device_id=dev, device_id_type=MESH_ID)

        def landed(a, k):
            dev, slot = peers[k]
            return pltpu.make_async_remote_copy(
                src_ref=outs[a].at[slot], dst_ref=outs[a].at[slot],
                send_sem=send_sems.at[a, k], recv_sem=recv_sems.at[a, k],
                device_id=dev, device_id_type=MESH_ID)

        own = [pltpu.make_async_copy(ins[a].at[me] if scatters[a] else ins[a], outs[a].at[me], local_sems.at[a])
               for a in range(n)]
        copies = [remote(a, k) for k in range(N_DEV - 1) for a in range(n)]
        for cp in own + copies:
            cp.start()
        for k in range(N_DEV - 1):
            for a in range(n):
                landed(a, k).wait_recv()
        for cp in copies:
            cp.wait_send()
        for cp in own:
            cp.wait()

    out_shape = [_sds(((N_DEV,) + a.shape[-2:]), a.dtype) for a in arrays]
    return pl.pallas_call(
        body, name=name, in_specs=[ANY] * n, out_specs=[ANY] * n, out_shape=out_shape,
        scratch_shapes=[pltpu.SemaphoreType.DMA((n, N_DEV - 1)), pltpu.SemaphoreType.DMA((n, N_DEV - 1)),
                        pltpu.SemaphoreType.DMA((n,))],
    )(*arrays)


def _gather_two_level(shard, *, name):
    def body(x_ref, out_ref, send_sems, recv_sems, local_sem):
        x, y, c = lax.axis_index("x"), lax.axis_index("y"), lax.axis_index("c")
        me, sibling = (x, y, c), (x, y, 1 - c)
        chips = [(1 - x, y), (x, 1 - y), (1 - x, 1 - y)]

        def slot(px, py, pc):
            return out_ref.at[4 * px + 2 * py + pc]

        def copy(k, block, to, src=None):
            return pltpu.make_async_remote_copy(
                src_ref=slot(*block) if src is None else src, dst_ref=slot(*block),
                send_sem=send_sems.at[k], recv_sem=recv_sems.at[k], device_id=to, device_id_type=MESH_ID)

        mine = pltpu.make_async_copy(x_ref, slot(*me), local_sem)
        mine.start()
        first = [copy(0, me, sibling, src=x_ref)]
        first += [copy(1 + j, me, (*chip, c), src=x_ref) for j, chip in enumerate(chips)]
        for cp in first:
            cp.start()
        passed = [copy(4 + j, (*chip, c), sibling) for j, chip in enumerate(chips)]
        for j, chip in enumerate(chips):
            copy(1 + j, (*chip, c), me).wait_recv()
            passed[j].start()
        copy(0, sibling, me).wait_recv()
        for j, chip in enumerate(chips):
            copy(4 + j, (*chip, 1 - c), me).wait_recv()
        for cp in first + passed:
            cp.wait_send()
        mine.wait()

    return pl.pallas_call(
        body, name=name, in_specs=[ANY], out_specs=ANY, out_shape=_sds((N_DEV,) + shard.shape, shard.dtype),
        scratch_shapes=[pltpu.SemaphoreType.DMA((N_DEV - 1,)), pltpu.SemaphoreType.DMA((N_DEV - 1,)),
                        pltpu.SemaphoreType.DMA],
    )(shard)


N_CHIPS = N_DEV // 2


def _peers(chips_only=False):
    x, y, c = lax.axis_index("x"), lax.axis_index("y"), lax.axis_index("c")
    out = []
    if chips_only:
        for k in range(1, N_CHIPS):
            px = 1 - x if k & 2 else x
            py = 1 - y if k & 1 else y
            out.append(((px, py, c), 2 * px + py))
        return 2 * x + y, out
    for k in range(1, N_DEV):
        px = 1 - x if k & 4 else x
        py = 1 - y if k & 2 else y
        pc = 1 - c if k & 1 else c
        out.append(((px, py, pc), 4 * px + 2 * py + pc))
    return 4 * x + 2 * y + c, out


def _sibling_swap(slot_arrays, *, name):
    n = len(slot_arrays)

    def body(*refs):
        ins, outs, send_sems, recv_sems = refs[:n], refs[n:2 * n], refs[2 * n], refs[2 * n + 1]
        x, y, c = lax.axis_index("x"), lax.axis_index("y"), lax.axis_index("c")
        copies = [pltpu.make_async_remote_copy(
            src_ref=ins[a].at[2 * q + (1 - c)], dst_ref=outs[a].at[q], send_sem=send_sems.at[a, q],
            recv_sem=recv_sems.at[a, q], device_id=(x, y, 1 - c), device_id_type=MESH_ID)
            for a in range(n) for q in range(N_CHIPS)]
        for cp in copies:
            cp.start()
        for cp in copies:
            cp.wait_recv()
        for cp in copies:
            cp.wait_send()

    return pl.pallas_call(
        body, name=name, in_specs=[ANY] * n, out_specs=[ANY] * n,
        out_shape=[_sds((N_CHIPS,) + t.shape[1:], t.dtype) for t in slot_arrays],
        scratch_shapes=[pltpu.SemaphoreType.DMA((n, N_CHIPS)), pltpu.SemaphoreType.DMA((n, N_CHIPS))],
    )(*slot_arrays)


def _pair_sum(slots, from_sibling, *, name, tn):
    _, r, c = slots.shape
    core = lax.axis_index("c").astype(jnp.int32).reshape(1)

    def body(core_ref, a_ref, b_ref, o_ref):
        o_ref[...] = (a_ref[...].astype(F32) + b_ref[...].astype(F32)).astype(o_ref.dtype)

    blk = lambda f: pl.BlockSpec((1, r, tn), f)
    return pl.pallas_call(
        body, name=name,
        grid_spec=pltpu.PrefetchScalarGridSpec(
            num_scalar_prefetch=1, grid=(N_CHIPS, c // tn),
            in_specs=[blk(lambda q, j, core: (2 * q + core[0], 0, j)), blk(lambda q, j, core: (q, 0, j))],
            out_specs=blk(lambda q, j, core: (q, 0, j))),
        out_shape=_sds((N_CHIPS, r, c), slots.dtype),
        compiler_params=_params("parallel", "parallel"),
    )(core, slots, from_sibling)


HBM = pl.BlockSpec(memory_space=pltpu.HBM)
SEM = pl.BlockSpec(memory_space=pltpu.SEMAPHORE)
DATAFLOW = pltpu.SideEffectType.DATAFLOW_SIDE_EFFECTING


def _split_copy(srcs, lands, send_sems, recv_sems, scatter, a, k, me, peers, incoming=False):
    dev, slot = peers[k]
    if incoming:
        src = dst = lands[a].at[slot]
    else:
        src, dst = (srcs[a].at[slot] if scatter else srcs[a]), lands[a].at[me]
    sem = a * len(peers) + k
    return pltpu.make_async_remote_copy(
        src_ref=src, dst_ref=dst, send_sem=send_sems.at[sem], recv_sem=recv_sems.at[sem],
        device_id=dev, device_id_type=MESH_ID)


def _exchange_start(arrays, scatter, *, name, chips_only=False):
    n = len(arrays)
    n_slots = N_CHIPS if chips_only else N_DEV

    def body(*refs):
        srcs, lands = refs[:n], refs[n:2 * n]
        send_sems, recv_sems = refs[2 * n], refs[2 * n + 1]
        token = refs[-1]
        me, peers = _peers(chips_only)
        for k in range(len(peers)):
            for a in range(n):
                _split_copy(srcs, lands, send_sems, recv_sems, scatter, a, k, me, peers).start()
        token[...] = jnp.zeros_like(token)

    land_shapes = [((n_slots,) + a.shape[-2:], a.dtype) for a in arrays]
    sems = pltpu.SemaphoreType.DMA((n * (n_slots - 1),))
    outs = pl.pallas_call(
        body, name=name,
        out_shape=(sems, sems, *[pltpu.HBM(a.shape, a.dtype) for a in arrays],
                   *[pltpu.HBM(s, d) for s, d in land_shapes], _sds((SUBLANE, LANE), F32)),
        in_specs=[HBM] * (2 * n),
        out_specs=(SEM, SEM, *[HBM] * (2 * n), pl.BlockSpec(memory_space=pltpu.VMEM)),
        input_output_aliases={i: 2 + i for i in range(2 * n)},
        compiler_params=pltpu.CompilerParams(has_side_effects=DATAFLOW),
    )(*[pltpu.with_memory_space_constraint(a, pltpu.HBM) for a in arrays],
      *[pltpu.with_memory_space_constraint(lax.empty(s, d), pltpu.HBM) for s, d in land_shapes])
    return (outs[0], outs[1], outs[2:2 + n], outs[2 + n:2 + 2 * n], scatter, chips_only), outs[-1]


def _exchange_wait(handles, after, *, name):
    send_sems, recv_sems, srcs, lands, scatter, chips_only = handles
    n = len(srcs)

    def body(*refs):
        src_refs, land_refs = refs[:n], refs[n:2 * n]
        send_ref, recv_ref = refs[2 * n], refs[2 * n + 1]
        me, peers = _peers(chips_only)
        for k in range(len(peers)):
            for a in range(n):
                _split_copy(src_refs, land_refs, send_ref, recv_ref, scatter, a, k, me, peers).wait_send()
                _split_copy(src_refs, land_refs, send_ref, recv_ref, scatter, a, k, me, peers, True).wait_recv()

    outs = pl.pallas_call(
        body, name=name,
        out_shape=tuple(pltpu.HBM(t.shape, t.dtype) for t in (*srcs, *lands)),
        in_specs=[HBM] * (2 * n) + [SEM, SEM, pl.BlockSpec(memory_space=pl.ANY)],
        out_specs=tuple([HBM] * (2 * n)),
        input_output_aliases={i: i for i in range(2 * n)},
        compiler_params=pltpu.CompilerParams(has_side_effects=DATAFLOW),
    )(*srcs, *lands, send_sems, recv_sems, after)
    return _with_own_slot(outs[n:], outs[:n], scatter, chips_only)


def _with_own_slot(landed, own, scatter, chips_only):
    me = 2 * lax.axis_index("x") + lax.axis_index("y")
    if not chips_only:
        me = 2 * me + lax.axis_index("c")
    out = []
    for buf, src in zip(landed, own):
        mine = lax.dynamic_index_in_dim(src, me, 0, keepdims=False) if scatter else src
        out.append(lax.dynamic_update_index_in_dim(buf, mine, me, 0))
    return out


def _adamw(parts, w, m, v, *, name, tm):
    r, c = w.shape
    assert r % tm == 0

    def body(p_ref, w_ref, m_ref, v_ref, g_ref, d_ref, nm_ref, nv_ref):
        _adamw_update(p_ref, w_ref, m_ref, v_ref, g_ref, d_ref, nm_ref, nv_ref)

    blk = pl.BlockSpec((tm, c), lambda i: (i, 0))
    return pl.pallas_call(
        body, name=name, grid=(r // tm,),
        in_specs=[pl.BlockSpec((parts.shape[0], tm, c), lambda i: (0, i, 0)), blk, blk, blk],
        out_specs=[blk] * 4, out_shape=[_sds((r, c), F32)] * 4,
        compiler_params=_params("parallel"),
    )(parts, w, m, v)


def _adamw_update(p_ref, w_ref, m_ref, v_ref, g_ref, d_ref, nm_ref, nv_ref):
    g = p_ref[0].astype(F32)
    for s in range(1, p_ref.shape[0]):
        g = g + p_ref[s].astype(F32)
    g_ref[...] = g
    m_new = ADAM_B1 * m_ref[...] + (1.0 - ADAM_B1) * g
    v_new = ADAM_B2 * v_ref[...] + (1.0 - ADAM_B2) * (g * g)
    nm_ref[...] = m_new
    nv_ref[...] = v_new
    m_hat = m_new / (1.0 - ADAM_B1 ** ADAM_STEP)
    v_hat = v_new / (1.0 - ADAM_B2 ** ADAM_STEP)
    d_ref[...] = -ADAM_LR * (m_hat / (jnp.sqrt(v_hat) + ADAM_EPS) + ADAM_WD * w_ref[...])


SMALL = ("g_pre_mix", "b_forget", "g_post_mix", "g_pre_ffn", "conv_b", "g_post_ffn")


def _adamw_small(parts, ws, ms, vs, sq_err_parts):
    n = len(ws)

    def body(*refs):
        ins, sq_ref, outs, loss_ref = refs[:4 * n], refs[4 * n], refs[4 * n + 1:-1], refs[-1]
        for i in range(n):
            _adamw_update(ins[i], ins[n + i], ins[2 * n + i], ins[3 * n + i], *outs[4 * i:4 * i + 4])
        total = sq_ref[0]
        for s in range(1, N_DEV):
            total = total + sq_ref[s]
        loss_ref[...] = total * (0.5 / D_MODEL)

    res = pl.pallas_call(
        body, name="adamw_small",
        out_shape=[_sds(w.shape, F32) for w in ws for _ in range(4)] + [_sds((1, LANE), F32)],
        compiler_params=pltpu.CompilerParams(vmem_limit_bytes=VMEM_LIMIT),
    )(*parts, *ws, *ms, *vs, sq_err_parts)
    return [res[4 * i:4 * i + 4] for i in range(n)], res[-1][0, 0]


def kernel(x, g_pre_mix, w_in, b_forget, w_o_fox, w_o_dil, w_out, g_post_mix, g_pre_ffn, w_up, conv_w, conv_b, w_down, g_post_ffn, loss_target, m_g_pre_mix, m_w_in, m_b_forget, m_w_o_fox, m_w_o_dil, m_w_out, m_g_post_mix, m_g_pre_ffn, m_w_up, m_conv_w, m_conv_b, m_w_down, m_g_post_ffn, v_g_pre_mix, v_w_in, v_b_forget, v_w_o_fox, v_w_o_dil, v_w_out, v_g_post_mix, v_g_pre_ffn, v_w_up, v_conv_w, v_conv_b, v_w_down, v_g_post_ffn):
    names = ("g_pre_mix", "w_in", "b_forget", "w_o_fox", "w_o_dil", "w_out", "g_post_mix", "g_pre_ffn",
             "w_up", "conv_w", "conv_b", "w_down", "g_post_ffn")
    w = dict(g_pre_mix=g_pre_mix, w_in=w_in, b_forget=b_forget, w_o_fox=w_o_fox, w_o_dil=w_o_dil, w_out=w_out,
             g_post_mix=g_post_mix, g_pre_ffn=g_pre_ffn, w_up=w_up, conv_w=conv_w, conv_b=conv_b, w_down=w_down,
             g_post_ffn=g_post_ffn)
    m = dict(g_pre_mix=m_g_pre_mix, w_in=m_w_in, b_forget=m_b_forget, w_o_fox=m_w_o_fox, w_o_dil=m_w_o_dil,
             w_out=m_w_out, g_post_mix=m_g_post_mix, g_pre_ffn=m_g_pre_ffn, w_up=m_w_up, conv_w=m_conv_w,
             conv_b=m_conv_b, w_down=m_w_down, g_post_ffn=m_g_post_ffn)
    v = dict(g_pre_mix=v_g_pre_mix, w_in=v_w_in, b_forget=v_b_forget, w_o_fox=v_w_o_fox, w_o_dil=v_w_o_dil,
             w_out=v_w_out, g_post_mix=v_g_post_mix, g_pre_ffn=v_g_pre_ffn, w_up=v_w_up, conv_w=v_conv_w,
             conv_b=v_conv_b, w_down=v_w_down, g_post_ffn=v_g_post_ffn)
    sharded = ("w_in", "w_o_fox", "w_o_dil", "w_out", "w_up", "w_down", "conv_w")
    wire = lambda n: F32 if n == "conv_w" else BF16

    by_cols = lambda t: jnp.transpose(t, (1, 0, 2)).reshape(t.shape[1], N_DEV * t.shape[2])
    by_rows = lambda t: t.reshape(N_DEV * t.shape[1], t.shape[2])
    col_slots = lambda t: jnp.transpose(t.reshape(t.shape[0], N_DEV, t.shape[1] // N_DEV), (1, 0, 2))
    row_slots = lambda t: t.reshape(N_DEV, t.shape[0] // N_DEV, t.shape[1])
    to_slots = lambda n, t: (row_slots if n in ("w_out", "w_down") else col_slots)(t).astype(wire(n))
    shard = lambda n: w[n][0].astype(wire(n))

    w_main, w_f = _w_in_from_shards(_gather_two_level(shard("w_in"), name="gather_w_in"))
    order = jnp.minimum(jnp.abs(w_f[0, 0].astype(F32)), 0.0)
    proj_handles, proj_tok = _exchange_start(
        [shard("w_o_fox") + order.astype(BF16), shard("w_o_dil"), shard("w_out")], False, name="gather_proj_start")
    ffn_handles, ffn_tok = _exchange_start(
        [shard("w_up"), shard("conv_w") + proj_tok[0, 0], shard("w_down")], False, name="gather_ffn_start")

    def proj_weights(after):
        w_oa, w_ob, w_o = _exchange_wait(proj_handles, after, name="gather_proj_wait")
        return by_cols(w_oa), by_cols(w_ob), by_rows(w_o)

    def ffn_weights(after):
        w_u, conv, w_d = _exchange_wait(ffn_handles, after, name="gather_ffn_wait")
        return _w_up_from_shards(w_u), _ffn_interleave(by_cols(conv)), by_rows(w_d)

    pending = {}

    def ffn_grads_ready(g):
        slots = [to_slots("w_down", g["w_down"]), _w_up_to_shards(g["w_up_blocks"]), to_slots("conv_w", g["conv_w"])]
        pending["ffn"] = _exchange_start(slots, True, name="scatter_ffn_start")
        return pending["ffn"][1][0, 0]

    def proj_grads_ready(g):
        pending["proj"] = _exchange_start([to_slots(n, g[n]) for n in ("w_o_fox", "w_o_dil", "w_out")], True,
                                          name="scatter_proj_start")
        return pending["proj"][1][0, 0]

    def mixer_grads_ready(g):
        slots = _w_in_to_shards(g["w_main"], g["w_f"])
        theirs = _sibling_swap([slots], name="scatter_w_in_swap")[0]
        chip_sums = _pair_sum(slots, theirs, name="scatter_w_in_pair_sum", tn=W_IN_SHARD)
        pending["w_in"] = _exchange_start([chip_sums], True, name="scatter_w_in_start", chips_only=True)
        return pending["w_in"][1][0, 0]

    sq_err, grad_x, g = _local_step(
        x[0], loss_target[0], w_main, w_f, b_forget, conv_b, g_pre_mix + ffn_tok[0, 0], g_post_mix, g_pre_ffn,
        g_post_ffn, proj_weights, ffn_weights, ffn_grads_ready, proj_grads_ready, mixer_grads_ready)

    tiles = dict(w_in=256, w_o_fox=512, w_o_dil=512, w_out=128, w_up=256, w_down=176, conv_w=3)
    adam = lambda n, p: _adamw(p, w[n][0], m[n][0], v[n][0], name=f"adamw_{n}", tm=tiles[n])
    res = {}
    for key, group in (("ffn", ("w_down", "w_up", "conv_w")), ("proj", ("w_o_fox", "w_o_dil", "w_out"))):
        landed = _exchange_wait(pending[key][0], grad_x, name=f"scatter_{key}_wait")
        res.update({n: adam(n, p) for n, p in zip(group, landed)})
    done = res["w_up"][3]
    res["w_in"] = adam("w_in", _exchange_wait(pending["w_in"][0], done, name="scatter_w_in_wait")[0])
    small_parts = _exchange([g[n] for n in SMALL] + [sq_err], False, name="gather_small_grads")
    small, loss = _adamw_small(small_parts[:-1], *[[t[n] for n in SMALL] for t in (w, m, v)], small_parts[-1])
    small = dict(zip(SMALL, small))
    out = [[(res[n][k][None] if n in sharded else small[n][k]) for n in names] for k in range(4)]
    return (loss, grad_x[None], *out[0], *out[1], *out[2], *out[3])
```

```python
import functools
import math

import jax
import jax.numpy as jnp
import numpy as np
from jax import lax
from jax.experimental import pallas as pl
from jax.experimental.pallas import tpu as pltpu

F32 = jnp.float32
BF16 = jnp.bfloat16

SEQ = 4096
D_MODEL = 1024
N_HEADS = 8
HEAD_DIM = 64
ATT_W = N_HEADS * HEAD_DIM
D_FF = 2816
Z_MAIN = 5120
F_PAD = 128
ROPE_DIM = 16
ROPE_THETA = 500000.0
RMS_EPS = 1e-6
NEG_INF = -1e30
SCALE = 1.0 / math.sqrt(HEAD_DIM)
DIL_PATTERNS = ((128, 1), (512, 4), (2048, 16))
DIL_BLK = 128
DIL_STEP_BLOCKS = 2
N_DEV = 8

ADAM_LR = 0.001
ADAM_B1 = 0.9
ADAM_B2 = 0.999
ADAM_EPS = 1e-08
ADAM_WD = 0.01
ADAM_STEP = 10

LANE = 128
SUBLANE = 8
VMEM_LIMIT = 56 * 1024 * 1024
MESH_ID = pl.DeviceIdType.MESH
ANY = pl.BlockSpec(memory_space=pl.ANY)


def _params(*sem):
    return pltpu.CompilerParams(dimension_semantics=sem, vmem_limit_bytes=VMEM_LIMIT)


def _sds(shape, dtype):
    return jax.ShapeDtypeStruct(shape, dtype)


def _also(after):
    return [] if after is None else [after]


def _matmul(a, b, *, ta=False, tb=False, out_dtype, tm, tn, tk, name, b_k_off=0, after=None):
    n_after = len(_also(after))
    if ta:
        kk, m = a.shape
    else:
        m, kk = a.shape
    n = b.shape[0] if tb else b.shape[1]
    tm, tn, tk = min(tm, m), min(tn, n), min(tk, kk)
    assert (b.shape[1] if tb else b.shape[0]) >= b_k_off * tk + kk
    assert m % tm == 0 and n % tn == 0 and kk % tk == 0, (name, m, n, kk, tm, tn, tk)
    nk = kk // tk
    dims = (((0 if ta else 1,), (1 if tb else 0,)), ((), ()))

    def body(a_ref, b_ref, *rest):
        o_ref, scratch = rest[n_after], rest[n_after + 1:]
        p = lax.dot_general(a_ref[...].astype(BF16), b_ref[...].astype(BF16), dims,
                            preferred_element_type=F32)
        if nk == 1:
            o_ref[...] = p.astype(o_ref.dtype)
        else:
            acc = scratch[0]
            k = pl.program_id(2)

            @pl.when(k == 0)
            def _():
                acc[...] = p

            @pl.when(k > 0)
            def _():
                acc[...] += p

            @pl.when(k == nk - 1)
            def _():
                o_ref[...] = acc[...].astype(o_ref.dtype)

    a_spec = (pl.BlockSpec((tk, tm), lambda i, j, k: (k, i)) if ta
              else pl.BlockSpec((tm, tk), lambda i, j, k: (i, k)))
    b_spec = (pl.BlockSpec((tn, tk), lambda i, j, k: (j, k + b_k_off)) if tb
              else pl.BlockSpec((tk, tn), lambda i, j, k: (k + b_k_off, j)))
    return pl.pallas_call(
        body, name=name, grid=(m // tm, n // tn, nk),
        in_specs=[a_spec, b_spec] + [ANY] * n_after,
        out_specs=pl.BlockSpec((tm, tn), lambda i, j, k: (i, j)),
        out_shape=_sds((m, n), out_dtype),
        scratch_shapes=[pltpu.VMEM((tm, tn), F32)] if nk > 1 else [],
        compiler_params=_params("parallel", "parallel", "arbitrary"),
    )(a, b, *_also(after))


def _rms_fwd(x, g, *, name, tm=512, after=None):
    def body(x_ref, g_ref, *rest):
        h_ref = rest[-1]
        xv = x_ref[...]
        r = lax.rsqrt(jnp.mean(xv * xv, axis=-1, keepdims=True) + RMS_EPS)
        h_ref[...] = (xv * r * g_ref[...]).astype(h_ref.dtype)

    return pl.pallas_call(
        body, name=name, grid=(SEQ // tm,),
        in_specs=[pl.BlockSpec((tm, D_MODEL), lambda i: (i, 0)), pl.BlockSpec((1, D_MODEL), lambda i: (0, 0))]
        + [ANY] * len(_also(after)),
        out_specs=pl.BlockSpec((tm, D_MODEL), lambda i: (i, 0)),
        out_shape=_sds((SEQ, D_MODEL), BF16),
        compiler_params=_params("parallel"),
    )(x, g, *_also(after))


def _rms_bwd(dh_parts, xin, g, dres, *, out_dtype, name, tm=512):
    n_parts = len(dh_parts)
    has_res = dres is not None

    def body(*refs):
        parts = refs[:n_parts]
        x_ref, g_ref = refs[n_parts], refs[n_parts + 1]
        res_ref = refs[n_parts + 2] if has_res else None
        o_ref, gg_ref = refs[-2], refs[-1]
        dh = parts[0][...].astype(F32)
        for p in parts[1:]:
            dh = dh + p[...].astype(F32)
        xv = x_ref[...]
        r = lax.rsqrt(jnp.mean(xv * xv, axis=-1, keepdims=True) + RMS_EPS)
        xn = xv * r

        @pl.when(pl.program_id(0) == 0)
        def _():
            gg_ref[...] = jnp.zeros_like(gg_ref)

        gg_ref[...] += jnp.sum(dh * xn, axis=0, keepdims=True)
        dxn = dh * g_ref[...]
        dx = r * (dxn - xn * jnp.mean(dxn * xn, axis=-1, keepdims=True))
        if has_res:
            dx = dx + res_ref[...]
        o_ref[...] = dx.astype(o_ref.dtype)

    row = pl.BlockSpec((tm, D_MODEL), lambda i: (i, 0))
    vec = pl.BlockSpec((1, D_MODEL), lambda i: (0, 0))
    args = list(dh_parts) + [xin, g] + ([dres] if has_res else [])
    return pl.pallas_call(
        body, name=name, grid=(SEQ // tm,),
        in_specs=[row] * n_parts + [row, vec] + ([row] if has_res else []),
        out_specs=[row, vec],
        out_shape=[_sds((SEQ, D_MODEL), out_dtype), _sds((1, D_MODEL), F32)],
        compiler_params=_params("arbitrary"),
    )(*args)


def _rms_pair_bwd(dh_parts, x2, g_pre, dres, y1, g_post, *, tm=512, after=None):
    n_parts = len(dh_parts)

    def norm_bwd(dh, xin, g_ref, gg_ref):
        r = lax.rsqrt(jnp.mean(xin * xin, axis=-1, keepdims=True) + RMS_EPS)
        xn = xin * r
        gg_ref[...] += jnp.sum(dh * xn, axis=0, keepdims=True)
        dxn = dh * g_ref[...]
        return r * (dxn - xn * jnp.mean(dxn * xn, axis=-1, keepdims=True))

    def body(*refs):
        parts = refs[:n_parts]
        x2_ref, gpre_ref, res_ref, y1_ref, gpost_ref = refs[n_parts:n_parts + 5]
        dx2_ref, dy1_ref, ggpre_ref, ggpost_ref = refs[-4:]

        @pl.when(pl.program_id(0) == 0)
        def _():
            ggpre_ref[...] = jnp.zeros_like(ggpre_ref)
            ggpost_ref[...] = jnp.zeros_like(ggpost_ref)

        dh = parts[0][...].astype(F32)
        for p in parts[1:]:
            dh = dh + p[...].astype(F32)
        dx2 = res_ref[...] + norm_bwd(dh, x2_ref[...], gpre_ref, ggpre_ref)
        dx2_ref[...] = dx2
        dy1_ref[...] = norm_bwd(dx2, y1_ref[...], gpost_ref, ggpost_ref).astype(dy1_ref.dtype)

    row = pl.BlockSpec((tm, D_MODEL), lambda i: (i, 0))
    vec = pl.BlockSpec((1, D_MODEL), lambda i: (0, 0))
    return pl.pallas_call(
        body, name="rms_pair_bwd", grid=(SEQ // tm,),
        in_specs=[row] * n_parts + [row, vec, row, row, vec] + [ANY] * len(_also(after)),
        out_specs=[row, row, vec, vec],
        out_shape=[_sds((SEQ, D_MODEL), F32), _sds((SEQ, D_MODEL), BF16), _sds((1, D_MODEL), F32),
                   _sds((1, D_MODEL), F32)],
        compiler_params=_params("arbitrary"),
    )(*dh_parts, x2, g_pre, dres, y1, g_post, *_also(after))


SCAN_BLK = 512


def _split_dot(v, tri):
    hi = v.astype(BF16)
    r1 = v - hi.astype(F32)
    mid = r1.astype(BF16)
    lo = (r1 - mid.astype(F32)).astype(BF16)
    dot = functools.partial(jnp.dot, preferred_element_type=F32)
    return dot(hi, tri) + dot(mid, tri) + dot(lo, tri)


def _fox_prep(fa_t, b_col):
    nblk = SEQ // SCAN_BLK

    def body(fa_ref, b_ref, f_ref, sg_ref):
        row = lax.broadcasted_iota(jnp.int32, (SCAN_BLK, SCAN_BLK), 0)
        col = lax.broadcasted_iota(jnp.int32, (SCAN_BLK, SCAN_BLK), 1)
        upper = (row <= col).astype(BF16)
        carry = jnp.zeros((N_HEADS, 1), F32)
        for blk in range(nblk):
            sl = pl.ds(blk * SCAN_BLK, SCAN_BLK)
            xx = fa_ref[:, sl] + b_ref[...]
            e = jnp.exp(-jnp.abs(xx))
            logf = jnp.minimum(xx, 0.0) - jnp.log(1.0 + e)
            sg_ref[:, sl] = jnp.where(xx >= 0.0, e, 1.0) / (1.0 + e)
            c = _split_dot(logf, upper) + carry
            f_ref[:, sl] = c
            carry = c[:, SCAN_BLK - 1:SCAN_BLK]

    return pl.pallas_call(
        body, name="fox_prep",
        out_shape=[_sds((N_HEADS, SEQ), F32), _sds((N_HEADS, SEQ), F32)],
        compiler_params=pltpu.CompilerParams(vmem_limit_bytes=VMEM_LIMIT),
    )(fa_t, b_col)


def _fox_post_bwd(df_t, sg_t):
    nblk = SEQ // SCAN_BLK

    def body(df_ref, sg_ref, dfa_ref, gb_ref):
        row = lax.broadcasted_iota(jnp.int32, (SCAN_BLK, SCAN_BLK), 0)
        col = lax.broadcasted_iota(jnp.int32, (SCAN_BLK, SCAN_BLK), 1)
        lower = (row >= col).astype(BF16)
        carry = jnp.zeros((N_HEADS, 1), F32)
        gb = jnp.zeros((N_HEADS, 1), F32)
        for blk in reversed(range(nblk)):
            sl = pl.ds(blk * SCAN_BLK, SCAN_BLK)
            c = _split_dot(df_ref[:, sl], lower) + carry
            carry = c[:, 0:1]
            dfa = c * sg_ref[:, sl]
            dfa_ref[:, sl] = dfa
            gb = gb + jnp.sum(dfa, axis=1, keepdims=True)
        gb_ref[...] = gb

    return pl.pallas_call(
        body, name="fox_post_bwd",
        out_shape=[_sds((N_HEADS, SEQ), F32), _sds((N_HEADS, 1), F32)],
        compiler_params=pltpu.CompilerParams(vmem_limit_bytes=VMEM_LIMIT),
    )(df_t, sg_t)


FOX_T = 512
NT_DIMS = (((1,), (1,)), ((), ()))
TN_DIMS = (((0,), (0,)), ((), ()))


def _head(ref_or_val, h):
    return ref_or_val[:, h * HEAD_DIM:(h + 1) * HEAD_DIM]


def _split3(v):
    hi = v.astype(BF16).astype(F32)
    r1 = v - hi
    mid = r1.astype(BF16).astype(F32)
    return hi, mid, (r1 - mid).astype(BF16).astype(F32)


ONE_LANE = 3 * N_HEADS


def _pack_terms(v, with_one):
    hi, mid, lo = _split3(v)
    t = hi + pltpu.roll(mid, N_HEADS, 1) + pltpu.roll(lo, 2 * N_HEADS, 1)
    if with_one:
        t = t + (lax.broadcasted_iota(jnp.int32, v.shape, 1) == ONE_LANE).astype(F32)
    return t.astype(BF16)


def _aux_matrices():
    to_q = np.zeros((LANE, N_HEADS * 2 * HEAD_DIM), np.float32)
    to_k = np.zeros_like(to_q)
    for h in range(N_HEADS):
        base = h * 2 * HEAD_DIM + HEAD_DIM
        for s in range(3):
            to_q[s * N_HEADS + h, base + s] = 1.0
            to_q[ONE_LANE, base + 3 + s] = 1.0
            to_k[ONE_LANE, base + s] = 1.0
            to_k[s * N_HEADS + h, base + 3 + s] = -1.0
    return jnp.asarray(to_q, BF16), jnp.asarray(to_k, BF16)


def _head_sums():
    total = np.zeros((N_HEADS * HEAD_DIM, LANE), np.float32)
    first = np.zeros_like(total)
    for h in range(N_HEADS):
        total[h * HEAD_DIM:(h + 1) * HEAD_DIM, h] = 1.0
        first[h * HEAD_DIM, h] = 1.0
    return jnp.asarray(total, BF16), jnp.asarray(first, BF16)


SLOT = 2 * HEAD_DIM
N_SPLIT = 3
FOX_FWD_HEADS = 8
FOX_BWD_HEADS = 4


def _slot(ref, h):
    return ref[:, h * SLOT:(h + 1) * SLOT]


def _fox_pack_fwd(zm, f_cols, *, tm=512):
    def body(q_ref, k_ref, v_ref, f_ref, tq_ref, tk_ref, qs_ref, ks_ref, vs_ref):
        ones = jnp.ones((tm, HEAD_DIM), BF16)
        terms = _pack_terms(f_ref[...], True)
        q_aux = jnp.dot(terms, tq_ref[...], preferred_element_type=F32).astype(BF16)
        k_aux = jnp.dot(terms, tk_ref[...], preferred_element_type=F32).astype(BF16)
        for h in range(N_HEADS):
            aux = slice(h * SLOT + HEAD_DIM, (h + 1) * SLOT)
            qs_ref[:, h * SLOT:(h + 1) * SLOT] = jnp.concatenate(
                [(_head(q_ref, h).astype(F32) * SCALE).astype(BF16), q_aux[:, aux]], axis=1)
            ks_ref[:, h * SLOT:(h + 1) * SLOT] = jnp.concatenate([_head(k_ref, h), k_aux[:, aux]], axis=1)
            vs_ref[:, h * SLOT:(h + 1) * SLOT] = jnp.concatenate([_head(v_ref, h), ones], axis=1)

    col = lambda b: pl.BlockSpec((tm, ATT_W), lambda i: (i, b))
    wide = pl.BlockSpec((tm, N_HEADS * SLOT), lambda i: (i, 0))
    const = pl.BlockSpec((LANE, N_HEADS * SLOT), lambda i: (0, 0))
    return pl.pallas_call(
        body, name="fox_pack_fwd", grid=(SEQ // tm,),
        in_specs=[col(0), col(1), col(2), pl.BlockSpec((tm, LANE), lambda i: (i, 0)), const, const],
        out_specs=[wide] * 3, out_shape=[_sds((SEQ, N_HEADS * SLOT), BF16)] * 3,
        compiler_params=_params("parallel"),
    )(zm, zm, zm, f_cols, *_aux_matrices())


def _fox_pack_bwd(zm, f_cols, lse, o, do, *, tm=512, after=None):
    def body(q_ref, f_ref, lse_ref, o_ref, do_ref, tq_ref, total_ref, first_ref, *rest):
        qs_ref, ds_ref = rest[-2:]
        delta = _split_dot(o_ref[...].astype(F32) * do_ref[...].astype(F32), total_ref[...])
        lse_h = _split_dot(lse_ref[...], first_ref[...])
        q_aux = jnp.dot(_pack_terms(f_ref[...] - lse_h, True), tq_ref[...], preferred_element_type=F32).astype(BF16)
        d_aux = jnp.dot(_pack_terms(-delta, False), tq_ref[...], preferred_element_type=F32).astype(BF16)
        for h in range(N_HEADS):
            aux = slice(h * SLOT + HEAD_DIM, (h + 1) * SLOT)
            qs_ref[:, h * SLOT:(h + 1) * SLOT] = jnp.concatenate(
                [(_head(q_ref, h).astype(F32) * SCALE).astype(BF16), q_aux[:, aux]], axis=1)
            ds_ref[:, h * SLOT:(h + 1) * SLOT] = jnp.concatenate([_head(do_ref, h), d_aux[:, aux]], axis=1)

    row = pl.BlockSpec((tm, ATT_W), lambda i: (i, 0))
    wide = pl.BlockSpec((tm, N_HEADS * SLOT), lambda i: (i, 0))
    const = lambda r, c: pl.BlockSpec((r, c), lambda i: (0, 0))
    return pl.pallas_call(
        body, name="fox_pack_bwd", grid=(SEQ // tm,),
        in_specs=[row, pl.BlockSpec((tm, LANE), lambda i: (i, 0)), row, row, row,
                  const(LANE, N_HEADS * SLOT), const(ATT_W, LANE), const(ATT_W, LANE)] + [ANY] * len(_also(after)),
        out_specs=[wide] * 2, out_shape=[_sds((SEQ, N_HEADS * SLOT), BF16)] * 2,
        compiler_params=_params("parallel"),
    )(zm, f_cols, lse, o, do, _aux_matrices()[0], *_head_sums(), *_also(after))


def _causal_pairs(key_major):
    nb = SEQ // FOX_T
    if key_major:
        pairs = [(i, j) for j in range(nb) for i in range(j, nb)]
    else:
        pairs = [(i, j) for i in range(nb) for j in range(i + 1)]
    return (jnp.array([p[0] for p in pairs], jnp.int32), jnp.array([p[1] for p in pairs], jnp.int32), len(pairs))


FOX_HALF = FOX_T // 2
FOX_FULL = ((slice(0, FOX_T), slice(0, FOX_T), None),)
FOX_DIAG = ((slice(0, FOX_HALF), slice(0, FOX_HALF), 0), (slice(FOX_HALF, FOX_T), slice(0, FOX_T), FOX_HALF))


def _causal_piece_mask(q_rows, k_rows, offset):
    shape = (q_rows.stop - q_rows.start, k_rows.stop - k_rows.start)
    row = lax.broadcasted_iota(jnp.int32, shape, 0)
    col = lax.broadcasted_iota(jnp.int32, shape, 1)
    return col <= row + offset


def _fox_fwd(q_slots, k_slots, v_slots):
    i_tab, j_tab, n_pairs = _causal_pairs(False)

    def body(i_tab, j_tab, q_ref, k_ref, v_ref, o_ref, lse_ref, m_s, acc_s):
        t = pl.program_id(1)
        i, j = i_tab[t], j_tab[t]

        @pl.when(j == 0)
        def _():
            m_s[...] = jnp.full_like(m_s, NEG_INF)
            acc_s[...] = jnp.zeros_like(acc_s)

        def step(pieces):
            jobs = [(h, piece) for h in range(FOX_FWD_HEADS) for piece in pieces]
            lanes = lambda h: slice(h * SLOT, (h + 1) * SLOT)
            scores = [lax.dot_general(q_ref[qr, lanes(h)], k_ref[kr, lanes(h)], NT_DIMS, preferred_element_type=F32)
                      for h, (qr, kr, _) in jobs]
            probs, alphas = [], []
            for idx, (h, (qr, kr, offset)) in enumerate(jobs):
                s = scores[idx]
                if offset is not None:
                    s = jnp.where(_causal_piece_mask(qr, kr, offset), s, NEG_INF)
                m_prev = m_s[h, qr, :]
                m_new = jnp.maximum(m_prev, jnp.max(s, axis=-1, keepdims=True))
                probs.append(jnp.exp(s - jnp.tile(m_new, (1, s.shape[1] // LANE))).astype(BF16))
                alphas.append(jnp.exp(m_prev - m_new))
                m_s[h, qr, :] = m_new
            for idx, (h, (qr, kr, _)) in enumerate(jobs):
                acc_s[h, qr, :] = alphas[idx] * acc_s[h, qr, :] + jnp.dot(
                    probs[idx], v_ref[kr, lanes(h)], preferred_element_type=F32)

        @pl.when(j < i)
        def _():
            step(FOX_FULL)

        @pl.when(j == i)
        def _():
            step(FOX_DIAG)
            outs, lses = [], []
            for h in range(FOX_FWD_HEADS):
                acc = acc_s[h]
                l = acc[:, HEAD_DIM:]
                outs.append(acc[:, :HEAD_DIM] / l)
                lses.append(m_s[h][:, :HEAD_DIM] + jnp.log(l))
            o_ref[...] = jnp.concatenate(outs, axis=1).astype(o_ref.dtype)
            lse_ref[...] = jnp.concatenate(lses, axis=1)

    qspec = pl.BlockSpec((FOX_T, FOX_FWD_HEADS * SLOT), lambda p, t, it, jt: (it[t], p))
    kspec = pl.BlockSpec((FOX_T, FOX_FWD_HEADS * SLOT), lambda p, t, it, jt: (jt[t], p))
    ospec = pl.BlockSpec((FOX_T, FOX_FWD_HEADS * HEAD_DIM), lambda p, t, it, jt: (it[t], p))
    return pl.pallas_call(
        body, name="fox_fwd",
        grid_spec=pltpu.PrefetchScalarGridSpec(
            num_scalar_prefetch=2, grid=(N_HEADS // FOX_FWD_HEADS, n_pairs),
            in_specs=[qspec, kspec, kspec], out_specs=[ospec, ospec],
            scratch_shapes=[pltpu.VMEM((FOX_FWD_HEADS, FOX_T, LANE), F32),
                            pltpu.VMEM((FOX_FWD_HEADS, FOX_T, SLOT), F32)]),
        out_shape=[_sds((SEQ, ATT_W), BF16), _sds((SEQ, ATT_W), F32)],
        compiler_params=_params("parallel", "arbitrary"),
    )(i_tab, j_tab, q_slots, k_slots, v_slots)


def _fox_bwd(q_slots, k_slots, v_slots, do_slots):
    i_tab, j_tab, n_pairs = _causal_pairs(True)

    def body(i_tab, j_tab, q_ref, k_ref, v_ref, do_ref, dq_ref, dk_ref, dv_ref):
        t = pl.program_id(1)
        i, j = i_tab[t], j_tab[t]

        @pl.when(t == 0)
        def _():
            dq_ref[...] = jnp.zeros_like(dq_ref)

        @pl.when(i == j)
        def _():
            dk_ref[...] = jnp.zeros_like(dk_ref)
            dv_ref[...] = jnp.zeros_like(dv_ref)

        def step(pieces):
            jobs = [(h, piece) for h in range(FOX_BWD_HEADS) for piece in pieces]
            lanes = lambda h: slice(h * SLOT, (h + 1) * SLOT)
            scores = [lax.dot_general(q_ref[qr, lanes(h)], k_ref[kr, lanes(h)], NT_DIMS, preferred_element_type=F32)
                      for h, (qr, kr, _) in jobs]
            dps = [lax.dot_general(do_ref[qr, lanes(h)], v_ref[kr, lanes(h)], NT_DIMS, preferred_element_type=F32)
                   for h, (qr, kr, _) in jobs]
            ps, dss = [], []
            for idx, (h, (qr, kr, offset)) in enumerate(jobs):
                p = jnp.exp(scores[idx])
                if offset is not None:
                    p = jnp.where(_causal_piece_mask(qr, kr, offset), p, 0.0)
                ps.append(p.astype(BF16))
                dss.append((p * dps[idx]).astype(BF16))
            for idx, (h, (qr, kr, _)) in enumerate(jobs):
                rows = pl.ds(pl.multiple_of(i * FOX_T + qr.start, FOX_HALF), qr.stop - qr.start)
                dv_ref[kr, lanes(h)] += lax.dot_general(ps[idx], do_ref[qr, lanes(h)], TN_DIMS,
                                                        preferred_element_type=F32)
                dk_ref[kr, lanes(h)] += lax.dot_general(dss[idx], q_ref[qr, lanes(h)], TN_DIMS,
                                                        preferred_element_type=F32)
                dq_ref[rows, lanes(h)] += jnp.dot(dss[idx], k_ref[kr, lanes(h)], preferred_element_type=F32)

        @pl.when(i > j)
        def _():
            step(FOX_FULL)

        @pl.when(i == j)
        def _():
            step(FOX_DIAG)

    qspec = pl.BlockSpec((FOX_T, FOX_BWD_HEADS * SLOT), lambda p, t, it, jt: (it[t], p))
    kspec = pl.BlockSpec((FOX_T, FOX_BWD_HEADS * SLOT), lambda p, t, it, jt: (jt[t], p))
    return pl.pallas_call(
        body, name="fox_bwd",
        grid_spec=pltpu.PrefetchScalarGridSpec(
            num_scalar_prefetch=2, grid=(N_HEADS // FOX_BWD_HEADS, n_pairs),
            in_specs=[qspec, kspec, kspec, qspec],
            out_specs=[pl.BlockSpec((SEQ, FOX_BWD_HEADS * SLOT), lambda p, t, it, jt: (0, p)), kspec, kspec]),
        out_shape=[_sds((SEQ, N_HEADS * SLOT), F32)] * 3,
        compiler_params=_params("arbitrary", "arbitrary"),
    )(i_tab, j_tab, q_slots, k_slots, v_slots, do_slots)


def _fox_unpack(dq_slots, dk_slots, dv_slots, dz, *, tm=512):
    def body(dq_ref, dk_ref, dv_ref, dz_in, o_ref, df_ref):
        lane = lax.broadcasted_iota(jnp.int32, (tm, LANE), 1)
        df = jnp.zeros((tm, LANE), F32)
        for h in range(N_HEADS):
            lo = h * SLOT
            for part, (ref, mult) in enumerate(((dq_ref, SCALE), (dk_ref, 1.0), (dv_ref, 1.0))):
                o_ref[:, part * ATT_W + h * HEAD_DIM:part * ATT_W + (h + 1) * HEAD_DIM] = (
                    ref[:, lo:lo + HEAD_DIM] * mult).astype(o_ref.dtype)
            rows = dq_ref[:, lo + HEAD_DIM:lo + HEAD_DIM + 1]
            cols = dk_ref[:, lo + HEAD_DIM + N_SPLIT:lo + HEAD_DIM + N_SPLIT + 1]
            df = jnp.where(lane == h, rows - cols, df)
        df_ref[...] = df

    wide = pl.BlockSpec((tm, N_HEADS * SLOT), lambda i: (i, 0))
    return pl.pallas_call(
        body, name="fox_unpack", grid=(SEQ // tm,), in_specs=[wide] * 3 + [ANY],
        out_specs=[pl.BlockSpec((tm, 3 * ATT_W), lambda i: (i, 0)), pl.BlockSpec((tm, LANE), lambda i: (i, 0))],
        out_shape=[_sds((SEQ, Z_MAIN), BF16), _sds((SEQ, LANE), F32)],
        input_output_aliases={3: 0},
        compiler_params=_params("parallel"),
    )(dq_slots, dk_slots, dv_slots, dz)


def _dil_bwd_prep(o, do, lse, *, tm=512):
    dilations = [d for _, d in DIL_PATTERNS]
    o_chunks = ATT_W // LANE

    def body(o_ref, do_ref, lse_ref, *rest):
        outs, (do_scr, lse_scr, dl_scr) = rest[:-3], rest[-3:]
        dov = do_ref[...].astype(F32)
        prod = o_ref[...].astype(F32) * dov
        lane = lax.broadcasted_iota(jnp.int32, (tm, LANE), 1)
        delta = jnp.zeros((tm, LANE), F32)
        for h in range(N_HEADS):
            delta = jnp.where(lane == h, jnp.sum(_head(prod, h), axis=1, keepdims=True), delta)
        for ch in range(o_chunks):
            do_scr[ch] = dov[:, ch * LANE:(ch + 1) * LANE]
        lse_scr[0] = lse_ref[...]
        dl_scr[0] = delta
        for k, d in enumerate(dilations):
            for scr, out in zip((do_scr, lse_scr, dl_scr), outs[3 * k:3 * k + 3]):
                _slabs_from_rows(scr, out, d)

    row = pl.BlockSpec((tm, ATT_W), lambda i: (i, 0))
    view = lambda d, w: pl.BlockSpec((tm // d, d * w), lambda i: (i, 0))
    outs = pl.pallas_call(
        body, name="dil_bwd_prep", grid=(SEQ // tm,),
        in_specs=[row, row, pl.BlockSpec((tm, LANE), lambda i: (i, 0))],
        out_specs=[view(d, w) for d in dilations for w in (ATT_W, LANE, LANE)],
        out_shape=[_sds((SEQ // d, d * w), t) for d in dilations for w, t in ((ATT_W, BF16), (LANE, F32), (LANE, F32))],
        scratch_shapes=[pltpu.VMEM((o_chunks, tm, LANE), F32), pltpu.VMEM((1, tm, LANE), F32),
                        pltpu.VMEM((1, tm, LANE), F32)],
        compiler_params=_params("parallel"),
    )(o, do, lse)
    return [outs[3 * k:3 * k + 3] for k in range(len(dilations))]


def _rope_tables():
    half = ROPE_DIM // 2
    inv_freq = np.float32(ROPE_THETA) ** (-np.arange(half, dtype=np.float32) * np.float32(2.0) / np.float32(ROPE_DIM))
    ang = np.arange(SEQ, dtype=np.float32)[:, None] * inv_freq.astype(np.float32)[None, :]
    cos, sin = jnp.asarray(np.cos(ang).astype(np.float32)), jnp.asarray(np.sin(ang).astype(np.float32))
    ones = jnp.ones((SEQ, HEAD_DIM - ROPE_DIM), F32)
    zeros = jnp.zeros((SEQ, HEAD_DIM - ROPE_DIM), F32)
    zh = jnp.zeros((SEQ, half), F32)
    c_tab = jnp.concatenate([cos, cos, ones], axis=1)
    a_tab = jnp.concatenate([-sin, zh, zeros], axis=1)
    b_tab = jnp.concatenate([zh, sin, zeros], axis=1)
    two = lambda t: jnp.concatenate([t, t], axis=1)
    return two(c_tab), two(a_tab), two(b_tab)


def _rotate(x, c_tab, a_tab, b_tab):
    return x * c_tab + pltpu.roll(x, LANE - ROPE_DIM // 2, 1) * a_tab + pltpu.roll(x, ROPE_DIM // 2, 1) * b_tab


def _rope_fwd(zm, tabs, *, tm=512):
    width = 3 * ATT_W
    dilations = [d for _, d in DIL_PATTERNS]

    def body(q_ref, k_ref, v_ref, c_ref, a_ref, b_ref, *rest):
        outs, scr = rest[:-1], rest[-1]
        per_part = ATT_W // LANE
        for part, (x_ref, mult) in enumerate(((q_ref, SCALE), (k_ref, 1.0))):
            for cc in range(per_part):
                sl = slice(cc * LANE, (cc + 1) * LANE)
                scr[part * per_part + cc] = _rotate(x_ref[:, sl].astype(F32), c_ref[...], a_ref[...], b_ref[...]) * mult
        for cc in range(per_part):
            scr[2 * per_part + cc] = v_ref[:, cc * LANE:(cc + 1) * LANE].astype(F32)
        for o_ref, d in zip(outs, dilations):
            for r in range(d):
                for ch in range(width // LANE):
                    o_ref[:, r * width + ch * LANE:r * width + (ch + 1) * LANE] = (
                        scr.at[ch][pl.ds(r, tm // d, stride=d), :].astype(o_ref.dtype))

    tab = pl.BlockSpec((tm, LANE), lambda i: (i, 0))
    col = lambda b: pl.BlockSpec((tm, ATT_W), lambda i: (i, b))
    return pl.pallas_call(
        body, name="rope_fwd", grid=(SEQ // tm,),
        in_specs=[col(3), col(4), col(5), tab, tab, tab],
        out_specs=[pl.BlockSpec((tm // d, d * width), lambda i: (i, 0)) for d in dilations],
        out_shape=[_sds((SEQ // d, d * width), BF16) for d in dilations],
        scratch_shapes=[pltpu.VMEM((width // LANE, tm, LANE), F32)],
        compiler_params=_params("parallel"),
    )(zm, zm, zm, *tabs)


def _dil_grad_combine(dqs, dks, dvs, tabs, dz, *, tm=256):
    dilations = [d for _, d in DIL_PATTERNS]
    chunks = ATT_W // LANE

    def body(*refs):
        groups = (refs[0:3], refs[3:6], refs[6:9])
        c_ref, a_ref, b_ref, _, o_ref, scr = refs[9:]

        def total(part, cc):
            acc = None
            for g, (ref, d) in enumerate(zip(groups[part], dilations)):
                term = ref[:, cc * LANE:(cc + 1) * LANE].astype(F32) if d == 1 else scr[part, g, cc]
                acc = term if acc is None else acc + term
            return acc

        for part in range(3):
            for g, (ref, d) in enumerate(zip(groups[part], dilations)):
                if d > 1:
                    _rows_from_slabs(ref, scr.at[part, g], d)
        for cc in range(chunks):
            for part in range(2):
                o_ref[:, part * ATT_W + cc * LANE:part * ATT_W + (cc + 1) * LANE] = _rotate(
                    total(part, cc), c_ref[...], -a_ref[...], -b_ref[...]).astype(o_ref.dtype)
            o_ref[:, 2 * ATT_W + cc * LANE:2 * ATT_W + (cc + 1) * LANE] = total(2, cc).astype(o_ref.dtype)

    view = lambda d: pl.BlockSpec((tm // d, d * ATT_W), lambda i: (i, 0))
    tab = pl.BlockSpec((tm, LANE), lambda i: (i, 0))
    return pl.pallas_call(
        body, name="dil_grad_combine", grid=(SEQ // tm,),
        in_specs=[view(d) for d in dilations] * 3 + [tab] * 3 + [ANY],
        out_specs=pl.BlockSpec((tm, 3 * ATT_W), lambda i: (i, 1)),
        out_shape=_sds((SEQ, Z_MAIN), BF16),
        input_output_aliases={12: 0},
        scratch_shapes=[pltpu.VMEM((3, len(dilations), chunks, tm, LANE), F32)],
        compiler_params=_params("parallel"),
    )(*dqs, *dks, *dvs, *tabs, dz)


def _dil_valid(n):
    qi = lax.broadcasted_iota(jnp.int32, (DIL_BLK, 2 * DIL_BLK), 0)
    ki = lax.broadcasted_iota(jnp.int32, (DIL_BLK, 2 * DIL_BLK), 1)
    dist = qi + DIL_BLK - ki
    return (dist >= 0) & (dist <= DIL_BLK) & ((n > 0) | (ki >= DIL_BLK))


def _dil_fwd(qkv_v, d):
    length = SEQ // d
    nb = length // DIL_BLK
    nsub = min(DIL_STEP_BLOCKS, nb)

    def body(q_ref, kp_ref, kc_ref, vp_ref, vc_ref, o_ref, lse_ref):
        m_step = pl.program_id(1)
        lane = lax.broadcasted_iota(jnp.int32, (DIL_BLK, LANE), 1)
        jobs = [(sub, h) for sub in range(nsub) for h in range(N_HEADS)]
        rows = lambda sub: slice(sub * DIL_BLK, (sub + 1) * DIL_BLK)
        cols = lambda h: slice(h * HEAD_DIM, (h + 1) * HEAD_DIM)

        def keys(prev_ref, cur_ref, sub, h):
            before = prev_ref[:, cols(h)] if sub == 0 else cur_ref[rows(sub - 1), cols(h)]
            return jnp.concatenate([before, cur_ref[rows(sub), cols(h)]], axis=0)

        scores = [lax.dot_general(q_ref[rows(sub), cols(h)], keys(kp_ref, kc_ref, sub, h), NT_DIMS,
                                  preferred_element_type=F32) for sub, h in jobs]
        ok = [_dil_valid(m_step)] + [_dil_valid(1)] * (nsub - 1)
        probs, inv_l, lse_all = [], [], [jnp.zeros((DIL_BLK, LANE), F32)] * nsub
        for idx, (sub, h) in enumerate(jobs):
            s = jnp.where(ok[sub], scores[idx], NEG_INF)
            m = jnp.max(s, axis=-1, keepdims=True)
            p = jnp.exp(s - m)
            l = jnp.sum(p, axis=-1, keepdims=True)
            probs.append(p.astype(BF16))
            inv_l.append(1.0 / l)
            lse_all[sub] = jnp.where(lane == h, m + jnp.log(l), lse_all[sub])
        outs = [jnp.dot(probs[idx], keys(vp_ref, vc_ref, sub, h), preferred_element_type=F32) * inv_l[idx]
                for idx, (sub, h) in enumerate(jobs)]
        for sub in range(nsub):
            o_ref[rows(sub), :] = jnp.concatenate(outs[sub * N_HEADS:(sub + 1) * N_HEADS], axis=1).astype(o_ref.dtype)
            lse_ref[rows(sub), :] = lse_all[sub]

    pair = lambda f: pl.BlockSpec((nsub * DIL_BLK, ATT_W), f)
    one = lambda f: pl.BlockSpec((DIL_BLK, ATT_W), f)
    before = lambda m: jnp.maximum(nsub * m - 1, 0)
    o, lse = pl.pallas_call(
        body, name=f"dil_fwd_d{d}", grid=(d, nb // nsub),
        in_specs=[pair(lambda r, m: (m, 3 * r)),
                  one(lambda r, m: (before(m), 3 * r + 1)), pair(lambda r, m: (m, 3 * r + 1)),
                  one(lambda r, m: (before(m), 3 * r + 2)), pair(lambda r, m: (m, 3 * r + 2))],
        out_specs=[pair(lambda r, m: (m, r)), pl.BlockSpec((nsub * DIL_BLK, LANE), lambda r, m: (m, r))],
        out_shape=[_sds((length, d * ATT_W), BF16), _sds((length, d * LANE), F32)],
        compiler_params=_params("parallel", "arbitrary"),
    )(qkv_v, qkv_v, qkv_v, qkv_v, qkv_v)
    return o, lse


def _rows_from_slabs(view_ref, scr, d):
    chunks, rows = scr.shape[0], scr.shape[1]
    for r in range(d):
        for ch in range(chunks):
            lo = (r * chunks + ch) * LANE
            scr.at[ch][pl.ds(r, rows // d, stride=d), :] = view_ref[:, lo:lo + LANE].astype(F32)


def _slabs_from_rows(scr, view_ref, d):
    chunks, rows = scr.shape[0], scr.shape[1]
    for r in range(d):
        for ch in range(chunks):
            lo = (r * chunks + ch) * LANE
            view_ref[:, lo:lo + LANE] = scr.at[ch][pl.ds(r, rows // d, stride=d), :].astype(view_ref.dtype)


def _dil_merge(os_, lses, *, tm=512):
    dilations = [d for _, d in DIL_PATTERNS]
    o_chunks = ATT_W // LANE

    def body(o0, o1, o2, l0, l1, l2, y_ref, lse_ref, o_scr, l_scr):
        os_nat, ls = [], []
        for g, (o_ref, l_ref, d) in enumerate(zip((o0, o1, o2), (l0, l1, l2), dilations)):
            if d == 1:
                os_nat.append(o_ref[...].astype(F32))
                ls.append(l_ref[...])
            else:
                _rows_from_slabs(o_ref, o_scr.at[g], d)
                _rows_from_slabs(l_ref, l_scr.at[g], d)
                os_nat.append(jnp.concatenate([o_scr[g, ch] for ch in range(o_chunks)], axis=1))
                ls.append(l_scr[g, 0])
        m = jnp.maximum(jnp.maximum(ls[0], ls[1]), ls[2])
        es = [jnp.exp(l - m) for l in ls]
        tot = es[0] + es[1] + es[2]
        lse_ref[...] = m + jnp.log(tot)
        alphas = [e / tot for e in es]
        outs = []
        for h in range(N_HEADS):
            acc = None
            for g in range(3):
                term = alphas[g][:, h:h + 1] * _head(os_nat[g], h)
                acc = term if acc is None else acc + term
            outs.append(acc)
        y_ref[...] = jnp.concatenate(outs, axis=1).astype(y_ref.dtype)

    row = pl.BlockSpec((tm, ATT_W), lambda i: (i, 0))
    vec = pl.BlockSpec((tm, LANE), lambda i: (i, 0))
    view = lambda d, w: pl.BlockSpec((tm // d, d * w), lambda i: (i, 0))
    return pl.pallas_call(
        body, name="dil_merge", grid=(SEQ // tm,),
        in_specs=[view(d, ATT_W) for d in dilations] + [view(d, LANE) for d in dilations], out_specs=[row, vec],
        out_shape=[_sds((SEQ, ATT_W), BF16), _sds((SEQ, LANE), F32)],
        scratch_shapes=[pltpu.VMEM((3, o_chunks, tm, LANE), F32), pltpu.VMEM((3, 1, tm, LANE), F32)],
        compiler_params=_params("parallel"),
    )(*os_, *lses)


def _dil_bwd(qkv_v, do_v, lse_v, dl_v, d):
    length = SEQ // d
    nb = length // DIL_BLK
    nsub = min(DIL_STEP_BLOCKS, nb)
    n_steps = nb // nsub

    def body(q_ref, kp_ref, kc_ref, vp_ref, vc_ref, lse_ref, dl_ref, do_ref, dq_ref, dk_ref, dv_ref, dk_s, dv_s):
        m_step = pl.program_id(1)

        @pl.when(m_step == 0)
        def _():
            dk_s[...] = jnp.zeros_like(dk_s)
            dv_s[...] = jnp.zeros_like(dv_s)

        jobs = [(sub, h) for sub in range(nsub) for h in range(N_HEADS)]
        rows = lambda sub: slice(sub * DIL_BLK, (sub + 1) * DIL_BLK)
        cols = lambda h: slice(h * HEAD_DIM, (h + 1) * HEAD_DIM)

        def keys(prev_ref, cur_ref, sub, h):
            before = prev_ref[:, cols(h)] if sub == 0 else cur_ref[rows(sub - 1), cols(h)]
            return jnp.concatenate([before, cur_ref[rows(sub), cols(h)]], axis=0)

        kks = [keys(kp_ref, kc_ref, sub, h) for sub, h in jobs]
        scores = [lax.dot_general(q_ref[rows(sub), cols(h)], kks[idx], NT_DIMS, preferred_element_type=F32)
                  for idx, (sub, h) in enumerate(jobs)]
        dps = [lax.dot_general(do_ref[rows(sub), cols(h)], keys(vp_ref, vc_ref, sub, h), NT_DIMS,
                               preferred_element_type=F32) for sub, h in jobs]
        ok = [_dil_valid(m_step)] + [_dil_valid(1)] * (nsub - 1)
        ps, dss = [], []
        for idx, (sub, h) in enumerate(jobs):
            p = jnp.where(ok[sub], jnp.exp(scores[idx] - lse_ref[rows(sub), h:h + 1]), 0.0)
            ps.append(p.astype(BF16))
            dss.append((p * (dps[idx] - dl_ref[rows(sub), h:h + 1])).astype(BF16))
        dqs = [jnp.dot(dss[idx], kks[idx], preferred_element_type=F32) * SCALE for idx in range(len(jobs))]
        dkks = [lax.dot_general(dss[idx], q_ref[rows(sub), cols(h)], TN_DIMS, preferred_element_type=F32)
                for idx, (sub, h) in enumerate(jobs)]
        dvvs = [lax.dot_general(ps[idx], do_ref[rows(sub), cols(h)], TN_DIMS, preferred_element_type=F32)
                for idx, (sub, h) in enumerate(jobs)]
        for sub in range(nsub):
            dq_ref[rows(sub), :] = jnp.concatenate(dqs[sub * N_HEADS:(sub + 1) * N_HEADS], axis=1).astype(dq_ref.dtype)
        base = m_step * (nsub * DIL_BLK)
        blocks = [pl.ds(pl.multiple_of(jnp.maximum(base - DIL_BLK, 0), DIL_BLK), DIL_BLK)]
        blocks += [pl.ds(pl.multiple_of(base + s * DIL_BLK, DIL_BLK), DIL_BLK) for s in range(nsub)]
        for acc, parts in ((dk_s, dkks), (dv_s, dvvs)):
            top = lambda sub: jnp.concatenate([parts[sub * N_HEADS + h][:DIL_BLK] for h in range(N_HEADS)], axis=1)
            bottom = lambda sub: jnp.concatenate([parts[sub * N_HEADS + h][DIL_BLK:] for h in range(N_HEADS)], axis=1)
            acc[blocks[0], :] += top(0)
            for s in range(nsub):
                acc[blocks[s + 1], :] += bottom(s) + top(s + 1) if s + 1 < nsub else bottom(s)

        @pl.when(m_step == n_steps - 1)
        def _():
            dk_ref[...] = dk_s[...].astype(dk_ref.dtype)
            dv_ref[...] = dv_s[...].astype(dv_ref.dtype)

    pair = lambda f: pl.BlockSpec((nsub * DIL_BLK, ATT_W), f)
    one = lambda f: pl.BlockSpec((DIL_BLK, ATT_W), f)
    vec = lambda f: pl.BlockSpec((nsub * DIL_BLK, LANE), f)
    whole = pl.BlockSpec((length, ATT_W), lambda r, m: (0, r))
    before = lambda m: jnp.maximum(nsub * m - 1, 0)
    outs = pl.pallas_call(
        body, name=f"dil_bwd_d{d}", grid=(d, n_steps),
        in_specs=[pair(lambda r, m: (m, 3 * r)),
                  one(lambda r, m: (before(m), 3 * r + 1)), pair(lambda r, m: (m, 3 * r + 1)),
                  one(lambda r, m: (before(m), 3 * r + 2)), pair(lambda r, m: (m, 3 * r + 2)),
                  vec(lambda r, m: (m, r)), vec(lambda r, m: (m, r)), pair(lambda r, m: (m, r))],
        out_specs=[pair(lambda r, m: (m, r)), whole, whole],
        out_shape=[_sds((length, d * ATT_W), BF16)] * 3,
        scratch_shapes=[pltpu.VMEM((length, ATT_W), F32), pltpu.VMEM((length, ATT_W), F32)],
        compiler_params=_params("arbitrary", "arbitrary"),
    )(qkv_v, qkv_v, qkv_v, qkv_v, qkv_v, lse_v, dl_v, do_v)
    return outs


def _sigmoid(x):
    return 1.0 / (1.0 + jnp.exp(-x))


def _mix_fwd(ya, yb, w_oa, w_ob, zm, *, tm=512):
    def body(ya_ref, yb_ref, wa_ref, wb_ref, ga_ref, gb_ref, pa_ref, pb_ref, mix_ref):
        pa = jnp.dot(ya_ref[...], wa_ref[...], preferred_element_type=F32)
        pb = jnp.dot(yb_ref[...], wb_ref[...], preferred_element_type=F32)
        pa_ref[...] = pa.astype(pa_ref.dtype)
        pb_ref[...] = pb.astype(pb_ref.dtype)
        mix_ref[...] = (_sigmoid(ga_ref[...].astype(F32)) * pa + _sigmoid(gb_ref[...].astype(F32)) * pb
                        ).astype(mix_ref.dtype)

    row = pl.BlockSpec((tm, ATT_W), lambda i: (i, 0))
    wsp = pl.BlockSpec((ATT_W, D_MODEL), lambda i: (0, 0))
    wide = pl.BlockSpec((tm, D_MODEL), lambda i: (i, 0))
    return pl.pallas_call(
        body, name="mix_fwd", grid=(SEQ // tm,),
        in_specs=[row, row, wsp, wsp, pl.BlockSpec((tm, D_MODEL), lambda i: (i, 3)),
                  pl.BlockSpec((tm, D_MODEL), lambda i: (i, 4))],
        out_specs=[wide] * 3, out_shape=[_sds((SEQ, D_MODEL), BF16)] * 3,
        compiler_params=_params("parallel"),
    )(ya, yb, w_oa, w_ob, zm, zm)


def _gate_bwd(dmix, zm, p, gate_block, dz, *, name, tm=512):
    def body(dm_ref, g_ref, p_ref, *rest):
        dp_ref, dz_ref = rest[-2], rest[-1]
        dm = dm_ref[...].astype(F32)
        s = _sigmoid(g_ref[...].astype(F32))
        dp_ref[...] = (dm * s).astype(dp_ref.dtype)
        dz_ref[...] = (dm * p_ref[...].astype(F32) * s * (1.0 - s)).astype(dz_ref.dtype)

    wide = pl.BlockSpec((tm, D_MODEL), lambda i: (i, 0))
    gate = pl.BlockSpec((tm, D_MODEL), lambda i: (i, gate_block))
    extra = [] if dz is None else [dz]
    return pl.pallas_call(
        body, name=name, grid=(SEQ // tm,),
        in_specs=[wide, gate, wide] + [ANY] * len(extra),
        out_specs=[wide, gate],
        out_shape=[_sds((SEQ, D_MODEL), BF16), _sds((SEQ, Z_MAIN), BF16)],
        input_output_aliases={3: 1} if extra else {},
        compiler_params=_params("parallel"),
    )(dmix, zm, p, *extra)


def _out_fwd(mixed, w_out, x, g_post, g_pre, *, tm=512):
    def body(m_ref, w_ref, x_ref, gp_ref, gn_ref, y_ref, x2_ref, h_ref):
        y = jnp.dot(m_ref[...], w_ref[...], preferred_element_type=F32)
        y_ref[...] = y
        r = lax.rsqrt(jnp.mean(y * y, axis=-1, keepdims=True) + RMS_EPS)
        x2 = x_ref[...] + y * r * gp_ref[...]
        x2_ref[...] = x2
        r2 = lax.rsqrt(jnp.mean(x2 * x2, axis=-1, keepdims=True) + RMS_EPS)
        h_ref[...] = (x2 * r2 * gn_ref[...]).astype(h_ref.dtype)

    row = pl.BlockSpec((tm, D_MODEL), lambda i: (i, 0))
    vec = pl.BlockSpec((1, D_MODEL), lambda i: (0, 0))
    return pl.pallas_call(
        body, name="out_fwd", grid=(SEQ // tm,),
        in_specs=[row, pl.BlockSpec((D_MODEL, D_MODEL), lambda i: (0, 0)), row, vec, vec],
        out_specs=[row] * 3,
        out_shape=[_sds((SEQ, D_MODEL), F32), _sds((SEQ, D_MODEL), F32), _sds((SEQ, D_MODEL), BF16)],
        compiler_params=_params("parallel"),
    )(mixed, w_out, x, g_post, g_pre)


FFN_HALF = 256
FFN_TN = 2 * FFN_HALF
FFN_NJ = D_FF // FFN_HALF
FFN_GROUP = 2 * SUBLANE
UP_TM = 1024


def _ffn_interleave(t):
    lead = t.shape[:-1]
    return jnp.swapaxes(t.reshape(*lead, 2, FFN_NJ, FFN_HALF), -3, -2).reshape(*lead, 2 * D_FF)


def _ffn_deinterleave(t):
    lead = t.shape[:-1]
    return jnp.swapaxes(t.reshape(*lead, FFN_NJ, 2, FFN_HALF), -3, -2).reshape(*lead, 2 * D_FF)


W_IN_SHARD = (Z_MAIN + N_HEADS) // N_DEV
FORGET_LO = 3 * ATT_W


def _w_in_from_shards(shards, *, tm=256):
    def columns(g_ref, lo, width):
        p, off = divmod(lo, W_IN_SHARD)
        if off + width <= W_IN_SHARD:
            return g_ref[p, :, off:off + width]
        first = W_IN_SHARD - off
        return jnp.concatenate([g_ref[p, :, off:], g_ref[p + 1, :, :width - first]], axis=1)

    def body(g_ref, main_ref, f_ref):
        for t in range(Z_MAIN // LANE):
            lo = t * LANE
            main_ref[:, lo:lo + LANE] = columns(g_ref, lo if lo < FORGET_LO else lo + N_HEADS, LANE)
        f_ref[...] = jnp.concatenate([columns(g_ref, FORGET_LO, N_HEADS),
                                      jnp.zeros((tm, F_PAD - N_HEADS), f_ref.dtype)], axis=1)

    return pl.pallas_call(
        body, name="w_in_from_shards", grid=(D_MODEL // tm,),
        in_specs=[pl.BlockSpec((N_DEV, tm, W_IN_SHARD), lambda i: (0, i, 0))],
        out_specs=[pl.BlockSpec((tm, Z_MAIN), lambda i: (i, 0)), pl.BlockSpec((tm, F_PAD), lambda i: (i, 0))],
        out_shape=[_sds((D_MODEL, Z_MAIN), shards.dtype), _sds((D_MODEL, F_PAD), shards.dtype)],
        compiler_params=_params("parallel"),
    )(shards)


def _w_in_to_shards(g_main, g_f, *, tm=256):
    def natural(main_ref, f_ref, lo, width):
        pieces, hi = [], lo + width
        for ref, start, stop, shift in ((main_ref, 0, FORGET_LO, 0), (f_ref, FORGET_LO, FORGET_LO + N_HEADS, FORGET_LO),
                                        (main_ref, FORGET_LO + N_HEADS, Z_MAIN + N_HEADS, N_HEADS)):
            a, b = max(lo, start), min(hi, stop)
            if a < b:
                pieces.append(ref[:, a - shift:b - shift])
        return pieces[0] if len(pieces) == 1 else jnp.concatenate(pieces, axis=1)

    def body(main_ref, f_ref, o_ref):
        for p in range(N_DEV):
            for q in range(-(-W_IN_SHARD // LANE)):
                width = min(LANE, W_IN_SHARD - q * LANE)
                o_ref[p, :, q * LANE:q * LANE + width] = natural(main_ref, f_ref, p * W_IN_SHARD + q * LANE, width)

    return pl.pallas_call(
        body, name="w_in_to_shards", grid=(D_MODEL // tm,),
        in_specs=[pl.BlockSpec((tm, Z_MAIN), lambda i: (i, 0)), pl.BlockSpec((tm, F_PAD), lambda i: (i, 0))],
        out_specs=pl.BlockSpec((N_DEV, tm, W_IN_SHARD), lambda i: (0, i, 0)),
        out_shape=_sds((N_DEV, D_MODEL, W_IN_SHARD), g_main.dtype),
        compiler_params=_params("parallel"),
    )(g_main, g_f)


W_UP_SHARD = 2 * D_FF // N_DEV


def _w_up_lane_tile(k):
    block = k // 2
    return (2 * (block % FFN_NJ) + block // FFN_NJ) * FFN_HALF + (k % 2) * LANE


def _w_up_from_shards(shards, *, tm=256):
    def body(g_ref, o_ref):
        for k in range(2 * D_FF // LANE):
            p, off = divmod(k * LANE, W_UP_SHARD)
            if off + LANE <= W_UP_SHARD:
                tile = g_ref[p, :, off:off + LANE]
            else:
                tile = jnp.concatenate([g_ref[p, :, off:], g_ref[p + 1, :, :off + LANE - W_UP_SHARD]], axis=1)
            dst = _w_up_lane_tile(k)
            o_ref[:, dst:dst + LANE] = tile

    return pl.pallas_call(
        body, name="w_up_from_shards", grid=(D_MODEL // tm,),
        in_specs=[pl.BlockSpec((N_DEV, tm, W_UP_SHARD), lambda i: (0, i, 0))],
        out_specs=pl.BlockSpec((tm, 2 * D_FF), lambda i: (i, 0)),
        out_shape=_sds((D_MODEL, 2 * D_FF), shards.dtype),
        compiler_params=_params("parallel"),
    )(shards)


def _w_up_to_shards(t, *, tm=256):
    def body(x_ref, o_ref):
        for p in range(N_DEV):
            for q in range(-(-W_UP_SHARD // LANE)):
                width = min(LANE, W_UP_SHARD - q * LANE)
                k, off = divmod(p * W_UP_SHARD + q * LANE, LANE)
                src = _w_up_lane_tile(k)
                if off == 0:
                    tile = x_ref[:, src:src + width]
                else:
                    tile = x_ref[:, src + off:src + LANE]
                    if width > LANE - off:
                        nxt = _w_up_lane_tile(k + 1)
                        tile = jnp.concatenate([tile, x_ref[:, nxt:nxt + width - (LANE - off)]], axis=1)
                o_ref[p, :, q * LANE:q * LANE + width] = tile

    return pl.pallas_call(
        body, name="w_up_to_shards", grid=(D_MODEL // tm,),
        in_specs=[pl.BlockSpec((tm, 2 * D_FF), lambda i: (i, 0))],
        out_specs=pl.BlockSpec((N_DEV, tm, W_UP_SHARD), lambda i: (0, i, 0)),
        out_shape=_sds((N_DEV, D_MODEL, W_UP_SHARD), t.dtype),
        compiler_params=_params("parallel"),
    )(t)


def _gelu_parts(a):
    c = math.sqrt(2.0 / math.pi)
    a2 = a * a
    t = jnp.tanh((c * a) * (1.0 + 0.044715 * a2))
    half_a, one_t = 0.5 * a, 1.0 + t
    gelu = half_a * one_t
    dgelu = 0.5 * one_t + half_a * (1.0 - t * t) * (c + (3.0 * 0.044715 * c) * a2)
    return gelu, dgelu


def _row_masks(down):
    row = lax.broadcasted_iota(jnp.int32, (SUBLANE, FFN_TN), 0)
    return (row < 1, row < 2) if down else (row >= SUBLANE - 1, row >= SUBLANE - 2)


def _rolled(x, down):
    return (pltpu.roll(x, 1, 0), pltpu.roll(x, 2, 0)) if down else (
        pltpu.roll(x, SUBLANE - 1, 0), pltpu.roll(x, SUBLANE - 2, 0))


def _shifted(cur_rolled, neighbour_rolled, masks):
    return (jnp.where(masks[0], neighbour_rolled[0], cur_rolled[0]),
            jnp.where(masks[1], neighbour_rolled[1], cur_rolled[1]))


def _conv_consts(w_ref, b_ref):
    shape = (SUBLANE, FFN_TN)
    return [jnp.broadcast_to(w_ref[k:k + 1, :], shape) for k in range(3)] + [jnp.broadcast_to(b_ref[...], shape)]


def _up_conv_fwd(h2, w_up, conv_w, conv_b):
    nrow = SEQ // UP_TM
    n_tiles = FFN_NJ * nrow
    n_groups = UP_TM // FFN_GROUP

    def body(h_ref, wu_ref, w_ref, b_ref, u_ref, ab_ref, m_ref, ua_s, ub_s, c1_s, c2_s):
        k = pl.program_id(0)

        @pl.when(k == 0)
        def _():
            ub_s[...] = jnp.zeros_like(ub_s)

        @pl.when(jnp.maximum(k - 1, 0) % nrow == 0)
        def _():
            c1_s[...] = jnp.zeros_like(c1_s)
            c2_s[...] = jnp.zeros_like(c2_s)

        def step(write_s, read_s):
            h_rows = pl.ds(pl.multiple_of((this(k) % nrow) * UP_TM, UP_TM), UP_TM)
            u = jnp.dot(h_ref[h_rows, :], wu_ref[...], preferred_element_type=F32)
            write_s[...] = u
            u_ref[...] = u.astype(u_ref.dtype)
            w0, w1, w2, bias = _conv_consts(w_ref, b_ref)
            masks = _row_masks(True)
            above = (c1_s[...], c2_s[...])
            for g in range(n_groups):
                rows = slice(g * FFN_GROUP, (g + 1) * FFN_GROUP)
                x = read_s[rows, :]
                convs = []
                for c in range(2):
                    cur = x[c * SUBLANE:(c + 1) * SUBLANE]
                    cur_rolled = _rolled(cur, True)
                    s1, s2 = _shifted(cur_rolled, above, masks)
                    convs.append(w0 * s2 + w1 * s1 + w2 * cur + bias)
                    above = cur_rolled
                y = jnp.concatenate(convs, axis=0)
                ab_ref[rows, :] = y.astype(ab_ref.dtype)
                m_ref[rows, :] = (_gelu_parts(y[:, :FFN_HALF])[0] * y[:, FFN_HALF:]).astype(m_ref.dtype)
            c1_s[...], c2_s[...] = above

        @pl.when(k % 2 == 0)
        def _():
            step(ua_s, ub_s)

        @pl.when(k % 2 == 1)
        def _():
            step(ub_s, ua_s)

    this = lambda k: jnp.minimum(k, n_tiles - 1)
    last = lambda k: jnp.maximum(k - 1, 0)
    blk = lambda tile: pl.BlockSpec((UP_TM, FFN_TN), lambda k: (tile(k) % nrow, tile(k) // nrow))
    return pl.pallas_call(
        body, name="up_conv_fwd", grid=(n_tiles + 1,),
        in_specs=[pl.BlockSpec((SEQ, D_MODEL), lambda k: (0, 0)),
                  pl.BlockSpec((D_MODEL, FFN_TN), lambda k: (0, this(k) // nrow)),
                  pl.BlockSpec((3, FFN_TN), lambda k: (0, last(k) // nrow)),
                  pl.BlockSpec((1, FFN_TN), lambda k: (0, last(k) // nrow))],
        out_specs=[blk(this), blk(last), pl.BlockSpec((UP_TM, FFN_HALF), lambda k: (last(k) % nrow, last(k) // nrow))],
        out_shape=[_sds((SEQ, 2 * D_FF), BF16), _sds((SEQ, 2 * D_FF), BF16), _sds((SEQ, D_FF), BF16)],
        scratch_shapes=[pltpu.VMEM((UP_TM, FFN_TN), F32), pltpu.VMEM((UP_TM, FFN_TN), F32),
                        pltpu.VMEM((SUBLANE, FFN_TN), F32), pltpu.VMEM((SUBLANE, FFN_TN), F32)],
        compiler_params=_params("arbitrary"),
    )(h2, w_up, conv_w, conv_b)


def _ffn_mid_bwd(dy2, w_down, u, ab, conv_w):
    nrow = SEQ // UP_TM
    n_tiles = FFN_NJ * nrow
    n_groups = UP_TM // FFN_GROUP
    this = lambda k: jnp.minimum(k, n_tiles - 1)
    last = lambda k: jnp.maximum(k - 1, 0)
    row_of = lambda tile: nrow - 1 - tile % nrow

    def body(dy_ref, wd_ref, u_ref, ab_ref, w_ref, du_ref, gw_ref, gb_ref, c_s, dma_s, dmb_s):
        k = pl.program_id(0)

        @pl.when(k == 0)
        def _():
            dmb_s[...] = jnp.zeros_like(dmb_s)

        @pl.when(last(k) % nrow == 0)
        def _():
            c_s[...] = jnp.zeros_like(c_s)
            gw_ref[...] = jnp.zeros_like(gw_ref)
            gb_ref[...] = jnp.zeros_like(gb_ref)

        def step(write_s, read_s):
            dy_rows = pl.ds(pl.multiple_of(row_of(this(k)) * UP_TM, UP_TM), UP_TM)
            write_s[...] = lax.dot_general(dy_ref[dy_rows, :], wd_ref[...], NT_DIMS,
                                           preferred_element_type=F32)
            taps = [jnp.broadcast_to(w_ref[t:t + 1, :], (SUBLANE, FFN_TN)) for t in range(3)]
            masks = _row_masks(False)
            below = _rolled(c_s[...], False)
            acc = [jnp.zeros((SUBLANE, FFN_TN), F32)] * 4
            for g in reversed(range(n_groups)):
                rows = slice(g * FFN_GROUP, (g + 1) * FFN_GROUP)
                x, y, dmv = u_ref[rows, :].astype(F32), ab_ref[rows, :].astype(F32), read_s[rows, :]
                gelu, dgelu = _gelu_parts(y[:, :FFN_HALF])
                d = jnp.concatenate([dmv * y[:, FFN_HALF:] * dgelu, dmv * gelu], axis=1)
                pre = [None, None]
                for c in (1, 0):
                    sl = slice(c * SUBLANE, (c + 1) * SUBLANE)
                    cur, xs = d[sl], x[sl]
                    cur_rolled = _rolled(cur, False)
                    up1, up2 = _shifted(cur_rolled, below, masks)
                    acc = [acc[0] + up2 * xs, acc[1] + up1 * xs, acc[2] + cur * xs, acc[3] + cur]
                    pre[c] = taps[2] * cur + taps[1] * up1 + taps[0] * up2
                    below = cur_rolled
                du_ref[rows, :] = jnp.concatenate(pre, axis=0).astype(du_ref.dtype)
            c_s[...] = pltpu.roll(below[0], 1, 0)
            for t in range(3):
                gw_ref[t:t + 1, :] += jnp.sum(acc[t], axis=0, keepdims=True)
            gb_ref[...] += jnp.sum(acc[3], axis=0, keepdims=True)

        @pl.when(k % 2 == 0)
        def _():
            step(dma_s, dmb_s)

        @pl.when(k % 2 == 1)
        def _():
            step(dmb_s, dma_s)

    blk = pl.BlockSpec((UP_TM, FFN_TN), lambda k: (row_of(last(k)), last(k) // nrow))
    col = lambda rows: pl.BlockSpec((rows, FFN_TN), lambda k: (0, last(k) // nrow))
    return pl.pallas_call(
        body, name="ffn_mid_bwd", grid=(n_tiles + 1,),
        in_specs=[pl.BlockSpec((SEQ, D_MODEL), lambda k: (0, 0)),
                  pl.BlockSpec((FFN_HALF, D_MODEL), lambda k: (this(k) // nrow, 0)), blk, blk, col(3)],
        out_specs=[blk, col(3), col(1)],
        out_shape=[_sds((SEQ, 2 * D_FF), BF16), _sds((3, 2 * D_FF), F32), _sds((1, 2 * D_FF), F32)],
        scratch_shapes=[pltpu.VMEM((SUBLANE, FFN_TN), F32), pltpu.VMEM((UP_TM, FFN_HALF), F32),
                        pltpu.VMEM((UP_TM, FFN_HALF), F32)],
        compiler_params=_params("arbitrary"),
    )(dy2, w_down, u, ab, conv_w)


def _down_fwd(m, w_down, x2, g_post, target, *, tm=512):
    def body(m_ref, w_ref, x2_ref, g_ref, t_ref, dout_ref, dy_ref, gg_ref, loss_ref):
        @pl.when(pl.program_id(0) == 0)
        def _():
            gg_ref[...] = jnp.zeros_like(gg_ref)
            loss_ref[...] = jnp.zeros_like(loss_ref)

        y = jnp.dot(m_ref[...], w_ref[...], preferred_element_type=F32)
        r = lax.rsqrt(jnp.mean(y * y, axis=-1, keepdims=True) + RMS_EPS)
        yn = y * r
        diff = (x2_ref[...] + yn * g_ref[...]) - t_ref[...]
        loss_ref[...] += jnp.sum(diff * diff)
        dout = diff * (1.0 / D_MODEL)
        dout_ref[...] = dout
        gg_ref[...] += jnp.sum(dout * yn, axis=0, keepdims=True)
        dn = dout * g_ref[...]
        dy_ref[...] = (r * (dn - yn * jnp.mean(dn * yn, axis=-1, keepdims=True))).astype(dy_ref.dtype)

    row = pl.BlockSpec((tm, D_MODEL), lambda i: (i, 0))
    vec = pl.BlockSpec((1, D_MODEL), lambda i: (0, 0))
    return pl.pallas_call(
        body, name="down_fwd", grid=(SEQ // tm,),
        in_specs=[pl.BlockSpec((tm, D_FF), lambda i: (i, 0)), pl.BlockSpec((D_FF, D_MODEL), lambda i: (0, 0)),
                  row, vec, row],
        out_specs=[row, row, vec, pl.BlockSpec((1, LANE), lambda i: (0, 0))],
        out_shape=[_sds((SEQ, D_MODEL), F32), _sds((SEQ, D_MODEL), BF16), _sds((1, D_MODEL), F32),
                   _sds((1, LANE), F32)],
        compiler_params=_params("arbitrary"),
    )(m, w_down, x2, g_post, target)


def _local_step(x, target, w_main, w_f, b_forget, conv_b, g_pre_mix, g_post_mix, g_pre_ffn, g_post_ffn,
                proj_weights, ffn_weights, ffn_grads_ready, proj_grads_ready, mixer_grads_ready, after=None):
    mm = _matmul
    tabs = _rope_tables()

    h1 = _rms_fwd(x, g_pre_mix, name="rms_pre_mix", after=after)
    zm = mm(h1, w_main, out_dtype=BF16, tm=2048, tn=1024, tk=1024, name="in_proj")
    zf = mm(h1, w_f, out_dtype=F32, tm=2048, tn=F_PAD, tk=1024, name="in_proj_forget")
    f_row, sg_row = _fox_prep(zf[:, :N_HEADS].T, b_forget.reshape(N_HEADS, 1))
    f_cols = jnp.pad(f_row.T, ((0, 0), (0, LANE - N_HEADS)))
    q_slots, k_slots, v_slots = _fox_pack_fwd(zm, f_cols)
    ya, lse_a = _fox_fwd(q_slots, k_slots, v_slots)
    qkv_d = dict(zip([d for _, d in DIL_PATTERNS], _rope_fwd(zm, tabs)))
    dil = [_dil_fwd(qkv_d[d], d) for _, d in DIL_PATTERNS]
    yb, lse_b = _dil_merge([o for o, _ in dil], [l for _, l in dil])
    w_oa, w_ob, w_out = proj_weights(yb)
    pa, pb, mixed = _mix_fwd(ya, yb, w_oa, w_ob, zm)
    y1, x2, h2 = _out_fwd(mixed, w_out, x, g_post_mix, g_pre_ffn)
    w_up, conv_w, w_down = ffn_weights(h2)
    u, ab, m = _up_conv_fwd(h2, w_up, conv_w, _ffn_interleave(conv_b))
    dout, dy2, gg_post_ffn, sq_err = _down_fwd(m, w_down, x2, g_post_ffn, target)

    g_w_down = mm(m, dy2, ta=True, out_dtype=BF16, tm=D_FF // 2, tn=1024, tk=2048, name="grad_w_down")
    du, g_conv_w, g_conv_b = _ffn_mid_bwd(dy2, w_down, u, ab, conv_w)
    g_w_up = mm(h2, du, ta=True, out_dtype=BF16, tm=1024, tn=D_FF // 2, tk=2048, name="grad_w_up")
    tok = ffn_grads_ready(dict(w_down=g_w_down, w_up_blocks=g_w_up, conv_w=_ffn_deinterleave(g_conv_w)))
    dh2 = mm(du, w_up, tb=True, out_dtype=BF16, tm=512, tn=1024, tk=2 * D_FF, name="d_h2")

    dx2, dy1, gg_pre_ffn, gg_post_mix = _rms_pair_bwd([dh2], x2, g_pre_ffn, dout, y1, g_post_mix, after=tok)
    g_w_out = mm(mixed, dy1, ta=True, out_dtype=BF16, tm=1024, tn=1024, tk=2048, name="grad_w_out")
    dmix = mm(dy1, w_out, tb=True, out_dtype=BF16, tm=2048, tn=1024, tk=1024, name="d_mixed")
    dpa, dz = _gate_bwd(dmix, zm, pa, 3, None, name="gate_bwd_fox")
    dpb, dz = _gate_bwd(dmix, zm, pb, 4, dz, name="gate_bwd_dil")
    g_w_oa = mm(ya, dpa, ta=True, out_dtype=BF16, tm=512, tn=1024, tk=SEQ, name="grad_w_o_fox")
    g_w_ob = mm(yb, dpb, ta=True, out_dtype=BF16, tm=512, tn=1024, tk=SEQ, name="grad_w_o_dil")
    tok = proj_grads_ready(dict(w_o_fox=g_w_oa, w_o_dil=g_w_ob, w_out=g_w_out))
    dya = mm(dpa, w_oa, tb=True, out_dtype=BF16, tm=2048, tn=512, tk=1024, name="d_y_fox")
    dyb = mm(dpb, w_ob, tb=True, out_dtype=BF16, tm=2048, tn=512, tk=1024, name="d_y_dil")

    qb_slots, do_slots = _fox_pack_bwd(zm, f_cols, lse_a, ya, dya, after=tok)
    dz, df_cols = _fox_unpack(*_fox_bwd(qb_slots, k_slots, v_slots, do_slots), dz)
    dfa_t, g_b_forget = _fox_post_bwd(df_cols[:, :N_HEADS].T, sg_row)

    rows_d = _dil_bwd_prep(yb, dyb, lse_b)
    dil_g = [_dil_bwd(qkv_d[d], *rows_d[k], d) for k, (_, d) in enumerate(DIL_PATTERNS)]
    dz = _dil_grad_combine([g[0] for g in dil_g], [g[1] for g in dil_g], [g[2] for g in dil_g], tabs, dz)

    dzf = jnp.pad(dfa_t.T, ((0, 0), (0, F_PAD - N_HEADS)))
    g_w_main = mm(h1, dz, ta=True, out_dtype=BF16, tm=1024, tn=Z_MAIN // 4, tk=2048, name="grad_w_in")
    g_w_f = mm(h1, dzf, ta=True, out_dtype=BF16, tm=1024, tn=F_PAD, tk=1024, name="grad_w_in_forget")
    tok = mixer_grads_ready(dict(w_main=g_w_main, w_f=g_w_f))
    dh1 = [mm(dz, w_main, tb=True, out_dtype=BF16, tm=512, tn=1024, tk=Z_MAIN, name="d_h1", after=tok),
           mm(dzf, w_f, tb=True, out_dtype=BF16, tm=2048, tn=1024, tk=F_PAD, name="d_h1_forget", after=tok)]
    grad_x, gg_pre_mix = _rms_bwd(dh1, x, g_pre_mix, dx2, out_dtype=F32, name="rms_pre_mix_bwd")

    grads = dict(
        b_forget=g_b_forget.reshape(1, N_HEADS), conv_b=_ffn_deinterleave(g_conv_b),
        g_pre_mix=gg_pre_mix, g_post_mix=gg_post_mix, g_pre_ffn=gg_pre_ffn, g_post_ffn=gg_post_ffn)
    return sq_err, grad_x, grads


def _exchange(arrays, scatter, *, name):
    n = len(arrays)
    scatters = [scatter] * n if isinstance(scatter, bool) else list(scatter)

    def body(*refs):
        ins, outs = refs[:n], refs[n:2 * n]
        send_sems, recv_sems, local_sems = refs[2 * n:]
        me, peers = _peers()

        def remote(a, k):
            dev, slot = peers[k]
            return pltpu.make_async_remote_copy(
                src_ref=ins[a].at[slot] if scatters[a] else ins[a], dst_ref=outs[a].at[me],
                send_sem=send_sems.at[a, k], recv_sem=recv_sems.at[a, k],
                device_id=dev, device_id_type=MESH_ID)

        def landed(a, k):
            dev, slot = peers[k]
            return pltpu.make_async_remote_copy(
                src_ref=outs[a].at[slot], dst_ref=outs[a].at[slot],
                send_sem=send_sems.at[a, k], recv_sem=recv_sems.at[a, k],
                device_id=dev, device_id_type=MESH_ID)

        own = [pltpu.make_async_copy(ins[a].at[me] if scatters[a] else ins[a], outs[a].at[me], local_sems.at[a])
               for a in range(n)]
        copies = [remote(a, k) for k in range(N_DEV - 1) for a in range(n)]
        for cp in own + copies:
            cp.start()
        for k in range(N_DEV - 1):
            for a in range(n):
                landed(a, k).wait_recv()
        for cp in copies:
            cp.wait_send()
        for cp in own:
            cp.wait()

    out_shape = [_sds(((N_DEV,) + a.shape[-2:]), a.dtype) for a in arrays]
    return pl.pallas_call(
        body, name=name, in_specs=[ANY] * n, out_specs=[ANY] * n, out_shape=out_shape,
        scratch_shapes=[pltpu.SemaphoreType.DMA((n, N_DEV - 1)), pltpu.SemaphoreType.DMA((n, N_DEV - 1)),
                        pltpu.SemaphoreType.DMA((n,))],
    )(*arrays)


def _gather_two_level(shard, *, name):
    def body(x_ref, out_ref, send_sems, recv_sems, local_sem):
        x, y, c = lax.axis_index("x"), lax.axis_index("y"), lax.axis_index("c")
        me, sibling = (x, y, c), (x, y, 1 - c)
        chips = [(1 - x, y), (x, 1 - y), (1 - x, 1 - y)]

        def slot(px, py, pc):
            return out_ref.at[4 * px + 2 * py + pc]

        def copy(k, block, to, src=None):
            return pltpu.make_async_remote_copy(
                src_ref=slot(*block) if src is None else src, dst_ref=slot(*block),
                send_sem=send_sems.at[k], recv_sem=recv_sems.at[k], device_id=to, device_id_type=MESH_ID)

        mine = pltpu.make_async_copy(x_ref, slot(*me), local_sem)
        mine.start()
        first = [copy(0, me, sibling, src=x_ref)]
        first += [copy(1 + j, me, (*chip, c), src=x_ref) for j, chip in enumerate(chips)]
        for cp in first:
            cp.start()
        passed = [copy(4 + j, (*chip, c), sibling) for j, chip in enumerate(chips)]
        for j, chip in enumerate(chips):
            copy(1 + j, (*chip, c), me).wait_recv()
            passed[j].start()
        copy(0, sibling, me).wait_recv()
        for j, chip in enumerate(chips):
            copy(4 + j, (*chip, 1 - c), me).wait_recv()
        for cp in first + passed:
            cp.wait_send()
        mine.wait()

    return pl.pallas_call(
        body, name=name, in_specs=[ANY], out_specs=ANY, out_shape=_sds((N_DEV,) + shard.shape, shard.dtype),
        scratch_shapes=[pltpu.SemaphoreType.DMA((N_DEV - 1,)), pltpu.SemaphoreType.DMA((N_DEV - 1,)),
                        pltpu.SemaphoreType.DMA],
    )(shard)


N_CHIPS = N_DEV // 2


def _peers(chips_only=False):
    x, y, c = lax.axis_index("x"), lax.axis_index("y"), lax.axis_index("c")
    out = []
    if chips_only:
        for k in range(1, N_CHIPS):
            px = 1 - x if k & 2 else x
            py = 1 - y if k & 1 else y
            out.append(((px, py, c), 2 * px + py))
        return 2 * x + y, out
    for k in range(1, N_DEV):
        px = 1 - x if k & 4 else x
        py = 1 - y if k & 2 else y
        pc = 1 - c if k & 1 else c
        out.append(((px, py, pc), 4 * px + 2 * py + pc))
    return 4 * x + 2 * y + c, out


def _sibling_swap(slot_arrays, *, name):
    n = len(slot_arrays)

    def body(*refs):
        ins, outs, send_sems, recv_sems = refs[:n], refs[n:2 * n], refs[2 * n], refs[2 * n + 1]
        x, y, c = lax.axis_index("x"), lax.axis_index("y"), lax.axis_index("c")
        copies = [pltpu.make_async_remote_copy(
            src_ref=ins[a].at[2 * q + (1 - c)], dst_ref=outs[a].at[q], send_sem=send_sems.at[a, q],
            recv_sem=recv_sems.at[a, q], device_id=(x, y, 1 - c), device_id_type=MESH_ID)
            for a in range(n) for q in range(N_CHIPS)]
        for cp in copies:
            cp.start()
        for cp in copies:
            cp.wait_recv()
        for cp in copies:
            cp.wait_send()

    return pl.pallas_call(
        body, name=name, in_specs=[ANY] * n, out_specs=[ANY] * n,
        out_shape=[_sds((N_CHIPS,) + t.shape[1:], t.dtype) for t in slot_arrays],
        scratch_shapes=[pltpu.SemaphoreType.DMA((n, N_CHIPS)), pltpu.SemaphoreType.DMA((n, N_CHIPS))],
    )(*slot_arrays)


def _pair_sum(slots, from_sibling, *, name, tn):
    _, r, c = slots.shape
    core = lax.axis_index("c").astype(jnp.int32).reshape(1)

    def body(core_ref, a_ref, b_ref, o_ref):
        o_ref[...] = (a_ref[...].astype(F32) + b_ref[...].astype(F32)).astype(o_ref.dtype)

    blk = lambda f: pl.BlockSpec((1, r, tn), f)
    return pl.pallas_call(
        body, name=name,
        grid_spec=pltpu.PrefetchScalarGridSpec(
            num_scalar_prefetch=1, grid=(N_CHIPS, c // tn),
            in_specs=[blk(lambda q, j, core: (2 * q + core[0], 0, j)), blk(lambda q, j, core: (q, 0, j))],
            out_specs=blk(lambda q, j, core: (q, 0, j))),
        out_shape=_sds((N_CHIPS, r, c), slots.dtype),
        compiler_params=_params("parallel", "parallel"),
    )(core, slots, from_sibling)


HBM = pl.BlockSpec(memory_space=pltpu.HBM)
SEM = pl.BlockSpec(memory_space=pltpu.SEMAPHORE)
DATAFLOW = pltpu.SideEffectType.DATAFLOW_SIDE_EFFECTING


def _split_copy(srcs, lands, send_sems, recv_sems, scatter, a, k, me, peers, incoming=False):
    dev, slot = peers[k]
    if incoming:
        src = dst = lands[a].at[slot]
    else:
        src, dst = (srcs[a].at[slot] if scatter else srcs[a]), lands[a].at[me]
    sem = a * len(peers) + k
    return pltpu.make_async_remote_copy(
        src_ref=src, dst_ref=dst, send_sem=send_sems.at[sem], recv_sem=recv_sems.at[sem],
        device_id=dev, device_id_type=MESH_ID)


def _exchange_start(arrays, scatter, *, name, chips_only=False, after=None):
    n = len(arrays)
    n_slots = N_CHIPS if chips_only else N_DEV
    n_in = 2 * n + len(_also(after))

    def body(*refs):
        srcs, lands = refs[:n], refs[n:2 * n]
        send_sems, recv_sems = refs[n_in], refs[n_in + 1]
        token = refs[-1]
        me, peers = _peers(chips_only)
        for k in range(len(peers)):
            for a in range(n):
                _split_copy(srcs, lands, send_sems, recv_sems, scatter, a, k, me, peers).start()
        token[...] = jnp.zeros_like(token)

    land_shapes = [((n_slots,) + a.shape[-2:], a.dtype) for a in arrays]
    sems = pltpu.SemaphoreType.DMA((n * (n_slots - 1),))
    outs = pl.pallas_call(
        body, name=name,
        out_shape=(sems, sems, *[pltpu.HBM(a.shape, a.dtype) for a in arrays],
                   *[pltpu.HBM(s, d) for s, d in land_shapes], _sds((SUBLANE, LANE), F32)),
        in_specs=[HBM] * (2 * n) + [ANY] * len(_also(after)),
        out_specs=(SEM, SEM, *[HBM] * (2 * n), pl.BlockSpec(memory_space=pltpu.VMEM)),
        input_output_aliases={i: 2 + i for i in range(2 * n)},
        compiler_params=pltpu.CompilerParams(has_side_effects=DATAFLOW),
    )(*[pltpu.with_memory_space_constraint(a, pltpu.HBM) for a in arrays],
      *[pltpu.with_memory_space_constraint(lax.empty(s, d), pltpu.HBM) for s, d in land_shapes], *_also(after))
    return (outs[0], outs[1], outs[2:2 + n], outs[2 + n:2 + 2 * n], scatter, chips_only), outs[-1]


def _exchange_wait(handles, after, *, name):
    send_sems, recv_sems, srcs, lands, scatter, chips_only = handles
    n = len(srcs)

    def body(*refs):
        src_refs, land_refs = refs[:n], refs[n:2 * n]
        send_ref, recv_ref = refs[2 * n], refs[2 * n + 1]
        local_sems = refs[-1]
        me, peers = _peers(chips_only)
        own = [pltpu.make_async_copy(src_refs[a].at[me] if scatter else src_refs[a], land_refs[a].at[me],
                                     local_sems.at[a]) for a in range(n)]
        for cp in own:
            cp.start()
        for k in range(len(peers)):
            for a in range(n):
                _split_copy(src_refs, land_refs, send_ref, recv_ref, scatter, a, k, me, peers).wait_send()
                _split_copy(src_refs, land_refs, send_ref, recv_ref, scatter, a, k, me, peers, True).wait_recv()
        for cp in own:
            cp.wait()

    outs = pl.pallas_call(
        body, name=name,
        out_shape=tuple(pltpu.HBM(t.shape, t.dtype) for t in (*srcs, *lands)),
        in_specs=[HBM] * (2 * n) + [SEM, SEM, pl.BlockSpec(memory_space=pl.ANY)],
        out_specs=tuple([HBM] * (2 * n)),
        scratch_shapes=[pltpu.SemaphoreType.DMA((n,))],
        input_output_aliases={i: i for i in range(2 * n)},
        compiler_params=pltpu.CompilerParams(has_side_effects=DATAFLOW),
    )(*srcs, *lands, send_sems, recv_sems, after)
    return outs[n:]


def _adamw(parts, w, m, v, *, name, tm):
    r, c = w.shape
    assert r % tm == 0

    def body(p_ref, w_ref, m_ref, v_ref, g_ref, d_ref, nm_ref, nv_ref):
        _adamw_update(p_ref, w_ref, m_ref, v_ref, g_ref, d_ref, nm_ref, nv_ref)

    blk = pl.BlockSpec((tm, c), lambda i: (i, 0))
    return pl.pallas_call(
        body, name=name, grid=(r // tm,),
        in_specs=[pl.BlockSpec((parts.shape[0], tm, c), lambda i: (0, i, 0)), blk, blk, blk],
        out_specs=[blk] * 4, out_shape=[_sds((r, c), F32)] * 4,
        compiler_params=_params("parallel"),
    )(parts, w, m, v)


def _adamw_update(p_ref, w_ref, m_ref, v_ref, g_ref, d_ref, nm_ref, nv_ref):
    g = p_ref[0].astype(F32)
    for s in range(1, p_ref.shape[0]):
        g = g + p_ref[s].astype(F32)
    g_ref[...] = g
    m_new = ADAM_B1 * m_ref[...] + (1.0 - ADAM_B1) * g
    v_new = ADAM_B2 * v_ref[...] + (1.0 - ADAM_B2) * (g * g)
    nm_ref[...] = m_new
    nv_ref[...] = v_new
    m_hat = m_new / (1.0 - ADAM_B1 ** ADAM_STEP)
    v_hat = v_new / (1.0 - ADAM_B2 ** ADAM_STEP)
    d_ref[...] = -ADAM_LR * (m_hat / (jnp.sqrt(v_hat) + ADAM_EPS) + ADAM_WD * w_ref[...])


SMALL = ("g_pre_mix", "b_forget", "g_post_mix", "g_pre_ffn", "conv_b", "g_post_ffn")


def _adamw_small(parts, ws, ms, vs, sq_err_parts):
    n = len(ws)

    def body(*refs):
        ins, sq_ref, outs, loss_ref = refs[:4 * n], refs[4 * n], refs[4 * n + 1:-1], refs[-1]
        for i in range(n):
            _adamw_update(ins[i], ins[n + i], ins[2 * n + i], ins[3 * n + i], *outs[4 * i:4 * i + 4])
        total = sq_ref[0]
        for s in range(1, N_DEV):
            total = total + sq_ref[s]
        loss_ref[...] = total * (0.5 / D_MODEL)

    res = pl.pallas_call(
        body, name="adamw_small",
        out_shape=[_sds(w.shape, F32) for w in ws for _ in range(4)] + [_sds((1, LANE), F32)],
        compiler_params=pltpu.CompilerParams(vmem_limit_bytes=VMEM_LIMIT),
    )(*parts, *ws, *ms, *vs, sq_err_parts)
    return [res[4 * i:4 * i + 4] for i in range(n)], res[-1][0, 0]


def kernel(x, g_pre_mix, w_in, b_forget, w_o_fox, w_o_dil, w_out, g_post_mix, g_pre_ffn, w_up, conv_w, conv_b, w_down, g_post_ffn, loss_target, m_g_pre_mix, m_w_in, m_b_forget, m_w_o_fox, m_w_o_dil, m_w_out, m_g_post_mix, m_g_pre_ffn, m_w_up, m_conv_w, m_conv_b, m_w_down, m_g_post_ffn, v_g_pre_mix, v_w_in, v_b_forget, v_w_o_fox, v_w_o_dil, v_w_out, v_g_post_mix, v_g_pre_ffn, v_w_up, v_conv_w, v_conv_b, v_w_down, v_g_post_ffn):
    names = ("g_pre_mix", "w_in", "b_forget", "w_o_fox", "w_o_dil", "w_out", "g_post_mix", "g_pre_ffn",
             "w_up", "conv_w", "conv_b", "w_down", "g_post_ffn")
    w = dict(g_pre_mix=g_pre_mix, w_in=w_in, b_forget=b_forget, w_o_fox=w_o_fox, w_o_dil=w_o_dil, w_out=w_out,
             g_post_mix=g_post_mix, g_pre_ffn=g_pre_ffn, w_up=w_up, conv_w=conv_w, conv_b=conv_b, w_down=w_down,
             g_post_ffn=g_post_ffn)
    m = dict(g_pre_mix=m_g_pre_mix, w_in=m_w_in, b_forget=m_b_forget, w_o_fox=m_w_o_fox, w_o_dil=m_w_o_dil,
             w_out=m_w_out, g_post_mix=m_g_post_mix, g_pre_ffn=m_g_pre_ffn, w_up=m_w_up, conv_w=m_conv_w,
             conv_b=m_conv_b, w_down=m_w_down, g_post_ffn=m_g_post_ffn)
    v = dict(g_pre_mix=v_g_pre_mix, w_in=v_w_in, b_forget=v_b_forget, w_o_fox=v_w_o_fox, w_o_dil=v_w_o_dil,
             w_out=v_w_out, g_post_mix=v_g_post_mix, g_pre_ffn=v_g_pre_ffn, w_up=v_w_up, conv_w=v_conv_w,
             conv_b=v_conv_b, w_down=v_w_down, g_post_ffn=v_g_post_ffn)
    sharded = ("w_in", "w_o_fox", "w_o_dil", "w_out", "w_up", "w_down", "conv_w")
    wire = lambda n: F32 if n == "conv_w" else BF16

    by_cols = lambda t: jnp.transpose(t, (1, 0, 2)).reshape(t.shape[1], N_DEV * t.shape[2])
    by_rows = lambda t: t.reshape(N_DEV * t.shape[1], t.shape[2])
    col_slots = lambda t: jnp.transpose(t.reshape(t.shape[0], N_DEV, t.shape[1] // N_DEV), (1, 0, 2))
    row_slots = lambda t: t.reshape(N_DEV, t.shape[0] // N_DEV, t.shape[1])
    to_slots = lambda n, t: (row_slots if n in ("w_out", "w_down") else col_slots)(t).astype(wire(n))
    shard = lambda n: w[n][0].astype(wire(n))

    w_main, w_f = _w_in_from_shards(_gather_two_level(shard("w_in"), name="gather_w_in"))
    proj_handles, proj_tok = _exchange_start(
        [shard("w_o_fox"), shard("w_o_dil"), shard("w_out")], False, name="gather_proj_start", after=w_f)
    ffn_handles, ffn_tok = _exchange_start(
        [shard("w_up"), shard("conv_w"), shard("w_down")], False, name="gather_ffn_start", after=proj_tok)

    def proj_weights(after):
        w_oa, w_ob, w_o = _exchange_wait(proj_handles, after, name="gather_proj_wait")
        return by_cols(w_oa), by_cols(w_ob), by_rows(w_o)

    def ffn_weights(after):
        w_u, conv, w_d = _exchange_wait(ffn_handles, after, name="gather_ffn_wait")
        return _w_up_from_shards(w_u), _ffn_interleave(by_cols(conv)), by_rows(w_d)

    pending = {}

    def ffn_grads_ready(g):
        slots = [to_slots("w_down", g["w_down"]), _w_up_to_shards(g["w_up_blocks"]), to_slots("conv_w", g["conv_w"])]
        pending["ffn"] = _exchange_start(slots, True, name="scatter_ffn_start")
        return pending["ffn"][1]

    def proj_grads_ready(g):
        pending["proj"] = _exchange_start([to_slots(n, g[n]) for n in ("w_o_fox", "w_o_dil", "w_out")], True,
                                          name="scatter_proj_start")
        return pending["proj"][1]

    def mixer_grads_ready(g):
        slots = _w_in_to_shards(g["w_main"], g["w_f"])
        theirs = _sibling_swap([slots], name="scatter_w_in_swap")[0]
        chip_sums = _pair_sum(slots, theirs, name="scatter_w_in_pair_sum", tn=W_IN_SHARD)
        pending["w_in"] = _exchange_start([chip_sums], True, name="scatter_w_in_start", chips_only=True)
        return pending["w_in"][1]

    sq_err, grad_x, g = _local_step(
        x[0], loss_target[0], w_main, w_f, b_forget, conv_b, g_pre_mix, g_post_mix, g_pre_ffn,
        g_post_ffn, proj_weights, ffn_weights, ffn_grads_ready, proj_grads_ready, mixer_grads_ready, after=ffn_tok)

    tiles = dict(w_in=256, w_o_fox=512, w_o_dil=512, w_out=128, w_up=256, w_down=176, conv_w=3)
    adam = lambda n, p: _adamw(p, w[n][0], m[n][0], v[n][0], name=f"adamw_{n}", tm=tiles[n])
    res = {}
    for key, group in (("ffn", ("w_down", "w_up", "conv_w")), ("proj", ("w_o_fox", "w_o_dil", "w_out"))):
        landed = _exchange_wait(pending[key][0], grad_x, name=f"scatter_{key}_wait")
        res.update({n: adam(n, p) for n, p in zip(group, landed)})
    done = res["w_up"][3]
    res["w_in"] = adam("w_in", _exchange_wait(pending["w_in"][0], done, name="scatter_w_in_wait")[0])
    small_parts = _exchange([g[n] for n in SMALL] + [sq_err], False, name="gather_small_grads")
    small, loss = _adamw_small(small_parts[:-1], *[[t[n] for n in SMALL] for t in (w, m, v)], small_parts[-1])
    small = dict(zip(SMALL, small))
    out = [[(res[n][k][None] if n in sharded else small[n][k]) for n in names] for k in range(4)]
    return (loss, grad_x[None], *out[0], *out[1], *out[2], *out[3])
```

```python
import functools
import math

import jax
import jax.numpy as jnp
import numpy as np
from jax import lax
from jax.experimental import pallas as pl
from jax.experimental.pallas import tpu as pltpu

F32 = jnp.float32
BF16 = jnp.bfloat16

SEQ = 4096
D_MODEL = 1024
N_HEADS = 8
HEAD_DIM = 64
ATT_W = N_HEADS * HEAD_DIM
D_FF = 2816
Z_MAIN = 5120
F_PAD = 128
ROPE_DIM = 16
ROPE_THETA = 500000.0
RMS_EPS = 1e-6
NEG_INF = -1e30
SCALE = 1.0 / math.sqrt(HEAD_DIM)
DIL_PATTERNS = ((128, 1), (512, 4), (2048, 16))
DIL_BLK = 128
DIL_STEP_BLOCKS = 2
N_DEV = 8

ADAM_LR = 0.001
ADAM_B1 = 0.9
ADAM_B2 = 0.999
ADAM_EPS = 1e-08
ADAM_WD = 0.01
ADAM_STEP = 10

LANE = 128
SUBLANE = 8
VMEM_LIMIT = 56 * 1024 * 1024
MESH_ID = pl.DeviceIdType.MESH
ANY = pl.BlockSpec(memory_space=pl.ANY)


def _params(*sem):
    return pltpu.CompilerParams(dimension_semantics=sem, vmem_limit_bytes=VMEM_LIMIT)


def _sds(shape, dtype):
    return jax.ShapeDtypeStruct(shape, dtype)


def _also(after):
    return [] if after is None else [after]


def _matmul(a, b, *, ta=False, tb=False, out_dtype, tm, tn, tk, name, b_k_off=0, after=None):
    n_after = len(_also(after))
    if ta:
        kk, m = a.shape
    else:
        m, kk = a.shape
    n = b.shape[0] if tb else b.shape[1]
    tm, tn, tk = min(tm, m), min(tn, n), min(tk, kk)
    assert (b.shape[1] if tb else b.shape[0]) >= b_k_off * tk + kk
    assert m % tm == 0 and n % tn == 0 and kk % tk == 0, (name, m, n, kk, tm, tn, tk)
    nk = kk // tk
    dims = (((0 if ta else 1,), (1 if tb else 0,)), ((), ()))

    def body(a_ref, b_ref, *rest):
        o_ref, scratch = rest[n_after], rest[n_after + 1:]
        p = lax.dot_general(a_ref[...].astype(BF16), b_ref[...].astype(BF16), dims,
                            preferred_element_type=F32)
        if nk == 1:
            o_ref[...] = p.astype(o_ref.dtype)
        else:
            acc = scratch[0]
            k = pl.program_id(2)

            @pl.when(k == 0)
            def _():
                acc[...] = p

            @pl.when(k > 0)
            def _():
                acc[...] += p

            @pl.when(k == nk - 1)
            def _():
                o_ref[...] = acc[...].astype(o_ref.dtype)

    a_spec = (pl.BlockSpec((tk, tm), lambda i, j, k: (k, i)) if ta
              else pl.BlockSpec((tm, tk), lambda i, j, k: (i, k)))
    b_spec = (pl.BlockSpec((tn, tk), lambda i, j, k: (j, k + b_k_off)) if tb
              else pl.BlockSpec((tk, tn), lambda i, j, k: (k + b_k_off, j)))
    return pl.pallas_call(
        body, name=name, grid=(m // tm, n // tn, nk),
        in_specs=[a_spec, b_spec] + [ANY] * n_after,
        out_specs=pl.BlockSpec((tm, tn), lambda i, j, k: (i, j)),
        out_shape=_sds((m, n), out_dtype),
        scratch_shapes=[pltpu.VMEM((tm, tn), F32)] if nk > 1 else [],
        compiler_params=_params("parallel", "parallel", "arbitrary"),
    )(a, b, *_also(after))


def _rms_fwd(x, g, *, name, tm=512, after=None):
    def body(x_ref, g_ref, *rest):
        h_ref = rest[-1]
        xv = x_ref[...]
        r = lax.rsqrt(jnp.mean(xv * xv, axis=-1, keepdims=True) + RMS_EPS)
        h_ref[...] = (xv * r * g_ref[...]).astype(h_ref.dtype)

    return pl.pallas_call(
        body, name=name, grid=(SEQ // tm,),
        in_specs=[pl.BlockSpec((tm, D_MODEL), lambda i: (i, 0)), pl.BlockSpec((1, D_MODEL), lambda i: (0, 0))]
        + [ANY] * len(_also(after)),
        out_specs=pl.BlockSpec((tm, D_MODEL), lambda i: (i, 0)),
        out_shape=_sds((SEQ, D_MODEL), BF16),
        compiler_params=_params("parallel"),
    )(x, g, *_also(after))


def _rms_bwd(dh_parts, xin, g, dres, *, out_dtype, name, tm=512):
    n_parts = len(dh_parts)
    has_res = dres is not None

    def body(*refs):
        parts = refs[:n_parts]
        x_ref, g_ref = refs[n_parts], refs[n_parts + 1]
        res_ref = refs[n_parts + 2] if has_res else None
        o_ref, gg_ref = refs[-2], refs[-1]
        dh = parts[0][...].astype(F32)
        for p in parts[1:]:
            dh = dh + p[...].astype(F32)
        xv = x_ref[...]
        r = lax.rsqrt(jnp.mean(xv * xv, axis=-1, keepdims=True) + RMS_EPS)
        xn = xv * r

        @pl.when(pl.program_id(0) == 0)
        def _():
            gg_ref[...] = jnp.zeros_like(gg_ref)

        gg_ref[...] += jnp.sum(dh * xn, axis=0, keepdims=True)
        dxn = dh * g_ref[...]
        dx = r * (dxn - xn * jnp.mean(dxn * xn, axis=-1, keepdims=True))
        if has_res:
            dx = dx + res_ref[...]
        o_ref[...] = dx.astype(o_ref.dtype)

    row = pl.BlockSpec((tm, D_MODEL), lambda i: (i, 0))
    vec = pl.BlockSpec((1, D_MODEL), lambda i: (0, 0))
    args = list(dh_parts) + [xin, g] + ([dres] if has_res else [])
    return pl.pallas_call(
        body, name=name, grid=(SEQ // tm,),
        in_specs=[row] * n_parts + [row, vec] + ([row] if has_res else []),
        out_specs=[row, vec],
        out_shape=[_sds((SEQ, D_MODEL), out_dtype), _sds((1, D_MODEL), F32)],
        compiler_params=_params("arbitrary"),
    )(*args)


def _rms_pair_bwd(dh_parts, x2, g_pre, dres, y1, g_post, *, tm=512, after=None):
    n_parts = len(dh_parts)

    def norm_bwd(dh, xin, g_ref, gg_ref):
        r = lax.rsqrt(jnp.mean(xin * xin, axis=-1, keepdims=True) + RMS_EPS)
        xn = xin * r
        gg_ref[...] += jnp.sum(dh * xn, axis=0, keepdims=True)
        dxn = dh * g_ref[...]
        return r * (dxn - xn * jnp.mean(dxn * xn, axis=-1, keepdims=True))

    def body(*refs):
        parts = refs[:n_parts]
        x2_ref, gpre_ref, res_ref, y1_ref, gpost_ref = refs[n_parts:n_parts + 5]
        dx2_ref, dy1_ref, ggpre_ref, ggpost_ref = refs[-4:]

        @pl.when(pl.program_id(0) == 0)
        def _():
            ggpre_ref[...] = jnp.zeros_like(ggpre_ref)
            ggpost_ref[...] = jnp.zeros_like(ggpost_ref)

        dh = parts[0][...].astype(F32)
        for p in parts[1:]:
            dh = dh + p[...].astype(F32)
        dx2 = res_ref[...] + norm_bwd(dh, x2_ref[...], gpre_ref, ggpre_ref)
        dx2_ref[...] = dx2
        dy1_ref[...] = norm_bwd(dx2, y1_ref[...], gpost_ref, ggpost_ref).astype(dy1_ref.dtype)

    row = pl.BlockSpec((tm, D_MODEL), lambda i: (i, 0))
    vec = pl.BlockSpec((1, D_MODEL), lambda i: (0, 0))
    return pl.pallas_call(
        body, name="rms_pair_bwd", grid=(SEQ // tm,),
        in_specs=[row] * n_parts + [row, vec, row, row, vec] + [ANY] * len(_also(after)),
        out_specs=[row, row, vec, vec],
        out_shape=[_sds((SEQ, D_MODEL), F32), _sds((SEQ, D_MODEL), BF16), _sds((1, D_MODEL), F32),
                   _sds((1, D_MODEL), F32)],
        compiler_params=_params("arbitrary"),
    )(*dh_parts, x2, g_pre, dres, y1, g_post, *_also(after))


SCAN_BLK = 512


def _split_dot(v, tri):
    hi = v.astype(BF16)
    r1 = v - hi.astype(F32)
    mid = r1.astype(BF16)
    lo = (r1 - mid.astype(F32)).astype(BF16)
    dot = functools.partial(jnp.dot, preferred_element_type=F32)
    return dot(hi, tri) + dot(mid, tri) + dot(lo, tri)


def _fox_prep(fa_t, b_col):
    nblk = SEQ // SCAN_BLK

    def body(fa_ref, b_ref, f_ref, sg_ref):
        row = lax.broadcasted_iota(jnp.int32, (SCAN_BLK, SCAN_BLK), 0)
        col = lax.broadcasted_iota(jnp.int32, (SCAN_BLK, SCAN_BLK), 1)
        upper = (row <= col).astype(BF16)
        carry = jnp.zeros((N_HEADS, 1), F32)
        for blk in range(nblk):
            sl = pl.ds(blk * SCAN_BLK, SCAN_BLK)
            xx = fa_ref[:, sl] + b_ref[...]
            e = jnp.exp(-jnp.abs(xx))
            logf = jnp.minimum(xx, 0.0) - jnp.log(1.0 + e)
            sg_ref[:, sl] = jnp.where(xx >= 0.0, e, 1.0) / (1.0 + e)
            c = _split_dot(logf, upper) + carry
            f_ref[:, sl] = c
            carry = c[:, SCAN_BLK - 1:SCAN_BLK]

    return pl.pallas_call(
        body, name="fox_prep",
        out_shape=[_sds((N_HEADS, SEQ), F32), _sds((N_HEADS, SEQ), F32)],
        compiler_params=pltpu.CompilerParams(vmem_limit_bytes=VMEM_LIMIT),
    )(fa_t, b_col)


def _fox_post_bwd(df_t, sg_t):
    nblk = SEQ // SCAN_BLK

    def body(df_ref, sg_ref, dfa_ref, gb_ref):
        row = lax.broadcasted_iota(jnp.int32, (SCAN_BLK, SCAN_BLK), 0)
        col = lax.broadcasted_iota(jnp.int32, (SCAN_BLK, SCAN_BLK), 1)
        lower = (row >= col).astype(BF16)
        carry = jnp.zeros((N_HEADS, 1), F32)
        gb = jnp.zeros((N_HEADS, 1), F32)
        for blk in reversed(range(nblk)):
            sl = pl.ds(blk * SCAN_BLK, SCAN_BLK)
            c = _split_dot(df_ref[:, sl], lower) + carry
            carry = c[:, 0:1]
            dfa = c * sg_ref[:, sl]
            dfa_ref[:, sl] = dfa
            gb = gb + jnp.sum(dfa, axis=1, keepdims=True)
        gb_ref[...] = gb

    return pl.pallas_call(
        body, name="fox_post_bwd",
        out_shape=[_sds((N_HEADS, SEQ), F32), _sds((N_HEADS, 1), F32)],
        compiler_params=pltpu.CompilerParams(vmem_limit_bytes=VMEM_LIMIT),
    )(df_t, sg_t)


FOX_T = 512
NT_DIMS = (((1,), (1,)), ((), ()))
TN_DIMS = (((0,), (0,)), ((), ()))


def _head(ref_or_val, h):
    return ref_or_val[:, h * HEAD_DIM:(h + 1) * HEAD_DIM]


def _split3(v):
    hi = v.astype(BF16).astype(F32)
    r1 = v - hi
    mid = r1.astype(BF16).astype(F32)
    return hi, mid, (r1 - mid).astype(BF16).astype(F32)


ONE_LANE = 3 * N_HEADS


def _pack_terms(v, with_one):
    hi, mid, lo = _split3(v)
    t = hi + pltpu.roll(mid, N_HEADS, 1) + pltpu.roll(lo, 2 * N_HEADS, 1)
    if with_one:
        t = t + (lax.broadcasted_iota(jnp.int32, v.shape, 1) == ONE_LANE).astype(F32)
    return t.astype(BF16)


def _aux_matrices():
    to_q = np.zeros((LANE, N_HEADS * 2 * HEAD_DIM), np.float32)
    to_k = np.zeros_like(to_q)
    for h in range(N_HEADS):
        base = h * 2 * HEAD_DIM + HEAD_DIM
        for s in range(3):
            to_q[s * N_HEADS + h, base + s] = 1.0
            to_q[ONE_LANE, base + 3 + s] = 1.0
            to_k[ONE_LANE, base + s] = 1.0
            to_k[s * N_HEADS + h, base + 3 + s] = -1.0
    return jnp.asarray(to_q, BF16), jnp.asarray(to_k, BF16)


def _head_sums():
    total = np.zeros((N_HEADS * HEAD_DIM, LANE), np.float32)
    first = np.zeros_like(total)
    for h in range(N_HEADS):
        total[h * HEAD_DIM:(h + 1) * HEAD_DIM, h] = 1.0
        first[h * HEAD_DIM, h] = 1.0
    return jnp.asarray(total, BF16), jnp.asarray(first, BF16)


SLOT = 2 * HEAD_DIM
N_SPLIT = 3
FOX_FWD_HEADS = 8
FOX_BWD_HEADS = 4


def _slot(ref, h):
    return ref[:, h * SLOT:(h + 1) * SLOT]


def _fox_pack_fwd(zm, f_cols, *, tm=512):
    def body(q_ref, k_ref, v_ref, f_ref, tq_ref, tk_ref, qs_ref, ks_ref, vs_ref):
        ones = jnp.ones((tm, HEAD_DIM), BF16)
        terms = _pack_terms(f_ref[...], True)
        q_aux = jnp.dot(terms, tq_ref[...], preferred_element_type=F32).astype(BF16)
        k_aux = jnp.dot(terms, tk_ref[...], preferred_element_type=F32).astype(BF16)
        for h in range(N_HEADS):
            aux = slice(h * SLOT + HEAD_DIM, (h + 1) * SLOT)
            qs_ref[:, h * SLOT:(h + 1) * SLOT] = jnp.concatenate(
                [(_head(q_ref, h).astype(F32) * SCALE).astype(BF16), q_aux[:, aux]], axis=1)
            ks_ref[:, h * SLOT:(h + 1) * SLOT] = jnp.concatenate([_head(k_ref, h), k_aux[:, aux]], axis=1)
            vs_ref[:, h * SLOT:(h + 1) * SLOT] = jnp.concatenate([_head(v_ref, h), ones], axis=1)

    col = lambda b: pl.BlockSpec((tm, ATT_W), lambda i: (i, b))
    wide = pl.BlockSpec((tm, N_HEADS * SLOT), lambda i: (i, 0))
    const = pl.BlockSpec((LANE, N_HEADS * SLOT), lambda i: (0, 0))
    return pl.pallas_call(
        body, name="fox_pack_fwd", grid=(SEQ // tm,),
        in_specs=[col(0), col(1), col(2), pl.BlockSpec((tm, LANE), lambda i: (i, 0)), const, const],
        out_specs=[wide] * 3, out_shape=[_sds((SEQ, N_HEADS * SLOT), BF16)] * 3,
        compiler_params=_params("parallel"),
    )(zm, zm, zm, f_cols, *_aux_matrices())


def _fox_pack_bwd(zm, f_cols, lse, o, do, *, tm=512, after=None):
    def body(q_ref, f_ref, lse_ref, o_ref, do_ref, tq_ref, total_ref, first_ref, *rest):
        qs_ref, ds_ref = rest[-2:]
        delta = _split_dot(o_ref[...].astype(F32) * do_ref[...].astype(F32), total_ref[...])
        lse_h = _split_dot(lse_ref[...], first_ref[...])
        q_aux = jnp.dot(_pack_terms(f_ref[...] - lse_h, True), tq_ref[...], preferred_element_type=F32).astype(BF16)
        d_aux = jnp.dot(_pack_terms(-delta, False), tq_ref[...], preferred_element_type=F32).astype(BF16)
        for h in range(N_HEADS):
            aux = slice(h * SLOT + HEAD_DIM, (h + 1) * SLOT)
            qs_ref[:, h * SLOT:(h + 1) * SLOT] = jnp.concatenate(
                [(_head(q_ref, h).astype(F32) * SCALE).astype(BF16), q_aux[:, aux]], axis=1)
            ds_ref[:, h * SLOT:(h + 1) * SLOT] = jnp.concatenate([_head(do_ref, h), d_aux[:, aux]], axis=1)

    row = pl.BlockSpec((tm, ATT_W), lambda i: (i, 0))
    wide = pl.BlockSpec((tm, N_HEADS * SLOT), lambda i: (i, 0))
    const = lambda r, c: pl.BlockSpec((r, c), lambda i: (0, 0))
    return pl.pallas_call(
        body, name="fox_pack_bwd", grid=(SEQ // tm,),
        in_specs=[row, pl.BlockSpec((tm, LANE), lambda i: (i, 0)), row, row, row,
                  const(LANE, N_HEADS * SLOT), const(ATT_W, LANE), const(ATT_W, LANE)] + [ANY] * len(_also(after)),
        out_specs=[wide] * 2, out_shape=[_sds((SEQ, N_HEADS * SLOT), BF16)] * 2,
        compiler_params=_params("parallel"),
    )(zm, f_cols, lse, o, do, _aux_matrices()[0], *_head_sums(), *_also(after))


def _causal_pairs(key_major):
    nb = SEQ // FOX_T
    if key_major:
        pairs = [(i, j) for j in range(nb) for i in range(j, nb)]
    else:
        pairs = [(i, j) for i in range(nb) for j in range(i + 1)]
    return (jnp.array([p[0] for p in pairs], jnp.int32), jnp.array([p[1] for p in pairs], jnp.int32), len(pairs))


FOX_HALF = FOX_T // 2
FOX_FULL = ((slice(0, FOX_T), slice(0, FOX_T), None),)
FOX_DIAG = ((slice(0, FOX_HALF), slice(0, FOX_HALF), 0), (slice(FOX_HALF, FOX_T), slice(0, FOX_T), FOX_HALF))


def _causal_piece_mask(q_rows, k_rows, offset):
    shape = (q_rows.stop - q_rows.start, k_rows.stop - k_rows.start)
    row = lax.broadcasted_iota(jnp.int32, shape, 0)
    col = lax.broadcasted_iota(jnp.int32, shape, 1)
    return col <= row + offset


def _fox_fwd(q_slots, k_slots, v_slots):
    i_tab, j_tab, n_pairs = _causal_pairs(False)

    def body(i_tab, j_tab, q_ref, k_ref, v_ref, o_ref, lse_ref, m_s, acc_s):
        t = pl.program_id(1)
        i, j = i_tab[t], j_tab[t]

        @pl.when(j == 0)
        def _():
            m_s[...] = jnp.full_like(m_s, NEG_INF)
            acc_s[...] = jnp.zeros_like(acc_s)

        def step(pieces):
            jobs = [(h, piece) for h in range(FOX_FWD_HEADS) for piece in pieces]
            lanes = lambda h: slice(h * SLOT, (h + 1) * SLOT)
            scores = [lax.dot_general(q_ref[qr, lanes(h)], k_ref[kr, lanes(h)], NT_DIMS, preferred_element_type=F32)
                      for h, (qr, kr, _) in jobs]
            probs, alphas = [], []
            for idx, (h, (qr, kr, offset)) in enumerate(jobs):
                s = scores[idx]
                if offset is not None:
                    s = jnp.where(_causal_piece_mask(qr, kr, offset), s, NEG_INF)
                m_prev = m_s[h, qr, :]
                m_new = jnp.maximum(m_prev, jnp.max(s, axis=-1, keepdims=True))
                probs.append(jnp.exp(s - jnp.tile(m_new, (1, s.shape[1] // LANE))).astype(BF16))
                alphas.append(jnp.exp(m_prev - m_new))
                m_s[h, qr, :] = m_new
            for idx, (h, (qr, kr, _)) in enumerate(jobs):
                acc_s[h, qr, :] = alphas[idx] * acc_s[h, qr, :] + jnp.dot(
                    probs[idx], v_ref[kr, lanes(h)], preferred_element_type=F32)

        @pl.when(j < i)
        def _():
            step(FOX_FULL)

        @pl.when(j == i)
        def _():
            step(FOX_DIAG)
            outs, lses = [], []
            for h in range(FOX_FWD_HEADS):
                acc = acc_s[h]
                l = acc[:, HEAD_DIM:]
                outs.append(acc[:, :HEAD_DIM] / l)
                lses.append(m_s[h][:, :HEAD_DIM] + jnp.log(l))
            o_ref[...] = jnp.concatenate(outs, axis=1).astype(o_ref.dtype)
            lse_ref[...] = jnp.concatenate(lses, axis=1)

    qspec = pl.BlockSpec((FOX_T, FOX_FWD_HEADS * SLOT), lambda p, t, it, jt: (it[t], p))
    kspec = pl.BlockSpec((FOX_T, FOX_FWD_HEADS * SLOT), lambda p, t, it, jt: (jt[t], p))
    ospec = pl.BlockSpec((FOX_T, FOX_FWD_HEADS * HEAD_DIM), lambda p, t, it, jt: (it[t], p))
    return pl.pallas_call(
        body, name="fox_fwd",
        grid_spec=pltpu.PrefetchScalarGridSpec(
            num_scalar_prefetch=2, grid=(N_HEADS // FOX_FWD_HEADS, n_pairs),
            in_specs=[qspec, kspec, kspec], out_specs=[ospec, ospec],
            scratch_shapes=[pltpu.VMEM((FOX_FWD_HEADS, FOX_T, LANE), F32),
                            pltpu.VMEM((FOX_FWD_HEADS, FOX_T, SLOT), F32)]),
        out_shape=[_sds((SEQ, ATT_W), BF16), _sds((SEQ, ATT_W), F32)],
        compiler_params=_params("parallel", "arbitrary"),
    )(i_tab, j_tab, q_slots, k_slots, v_slots)


def _fox_bwd(q_slots, k_slots, v_slots, do_slots):
    i_tab, j_tab, n_pairs = _causal_pairs(True)

    def body(i_tab, j_tab, q_ref, k_ref, v_ref, do_ref, dq_ref, dk_ref, dv_ref):
        t = pl.program_id(1)
        i, j = i_tab[t], j_tab[t]

        @pl.when(t == 0)
        def _():
            dq_ref[...] = jnp.zeros_like(dq_ref)

        @pl.when(i == j)
        def _():
            dk_ref[...] = jnp.zeros_like(dk_ref)
            dv_ref[...] = jnp.zeros_like(dv_ref)

        def step(pieces):
            jobs = [(h, piece) for h in range(FOX_BWD_HEADS) for piece in pieces]
            lanes = lambda h: slice(h * SLOT, (h + 1) * SLOT)
            scores = [lax.dot_general(q_ref[qr, lanes(h)], k_ref[kr, lanes(h)], NT_DIMS, preferred_element_type=F32)
                      for h, (qr, kr, _) in jobs]
            dps = [lax.dot_general(do_ref[qr, lanes(h)], v_ref[kr, lanes(h)], NT_DIMS, preferred_element_type=F32)
                   for h, (qr, kr, _) in jobs]
            ps, dss = [], []
            for idx, (h, (qr, kr, offset)) in enumerate(jobs):
                p = jnp.exp(scores[idx])
                if offset is not None:
                    p = jnp.where(_causal_piece_mask(qr, kr, offset), p, 0.0)
                ps.append(p.astype(BF16))
                dss.append((p * dps[idx]).astype(BF16))
            for idx, (h, (qr, kr, _)) in enumerate(jobs):
                rows = pl.ds(pl.multiple_of(i * FOX_T + qr.start, FOX_HALF), qr.stop - qr.start)
                dv_ref[kr, lanes(h)] += lax.dot_general(ps[idx], do_ref[qr, lanes(h)], TN_DIMS,
                                                        preferred_element_type=F32)
                dk_ref[kr, lanes(h)] += lax.dot_general(dss[idx], q_ref[qr, lanes(h)], TN_DIMS,
                                                        preferred_element_type=F32)
                dq_ref[rows, lanes(h)] += jnp.dot(dss[idx], k_ref[kr, lanes(h)], preferred_element_type=F32)

        @pl.when(i > j)
        def _():
            step(FOX_FULL)

        @pl.when(i == j)
        def _():
            step(FOX_DIAG)

    qspec = pl.BlockSpec((FOX_T, FOX_BWD_HEADS * SLOT), lambda p, t, it, jt: (it[t], p))
    kspec = pl.BlockSpec((FOX_T, FOX_BWD_HEADS * SLOT), lambda p, t, it, jt: (jt[t], p))
    return pl.pallas_call(
        body, name="fox_bwd",
        grid_spec=pltpu.PrefetchScalarGridSpec(
            num_scalar_prefetch=2, grid=(N_HEADS // FOX_BWD_HEADS, n_pairs),
            in_specs=[qspec, kspec, kspec, qspec],
            out_specs=[pl.BlockSpec((SEQ, FOX_BWD_HEADS * SLOT), lambda p, t, it, jt: (0, p)), kspec, kspec]),
        out_shape=[_sds((SEQ, N_HEADS * SLOT), F32)] * 3,
        compiler_params=_params("arbitrary", "arbitrary"),
    )(i_tab, j_tab, q_slots, k_slots, v_slots, do_slots)


def _fox_unpack(dq_slots, dk_slots, dv_slots, dz, *, tm=512):
    def body(dq_ref, dk_ref, dv_ref, dz_in, o_ref, df_ref):
        lane = lax.broadcasted_iota(jnp.int32, (tm, LANE), 1)
        df = jnp.zeros((tm, LANE), F32)
        for h in range(N_HEADS):
            lo = h * SLOT
            for part, (ref, mult) in enumerate(((dq_ref, SCALE), (dk_ref, 1.0), (dv_ref, 1.0))):
                o_ref[:, part * ATT_W + h * HEAD_DIM:part * ATT_W + (h + 1) * HEAD_DIM] = (
                    ref[:, lo:lo + HEAD_DIM] * mult).astype(o_ref.dtype)
            rows = dq_ref[:, lo + HEAD_DIM:lo + HEAD_DIM + 1]
            cols = dk_ref[:, lo + HEAD_DIM + N_SPLIT:lo + HEAD_DIM + N_SPLIT + 1]
            df = jnp.where(lane == h, rows - cols, df)
        df_ref[...] = df

    wide = pl.BlockSpec((tm, N_HEADS * SLOT), lambda i: (i, 0))
    return pl.pallas_call(
        body, name="fox_unpack", grid=(SEQ // tm,), in_specs=[wide] * 3 + [ANY],
        out_specs=[pl.BlockSpec((tm, 3 * ATT_W), lambda i: (i, 0)), pl.BlockSpec((tm, LANE), lambda i: (i, 0))],
        out_shape=[_sds((SEQ, Z_MAIN), BF16), _sds((SEQ, LANE), F32)],
        input_output_aliases={3: 0},
        compiler_params=_params("parallel"),
    )(dq_slots, dk_slots, dv_slots, dz)


def _dil_bwd_prep(o, do, lse, *, tm=512):
    dilations = [d for _, d in DIL_PATTERNS]
    o_chunks = ATT_W // LANE

    def body(o_ref, do_ref, lse_ref, *rest):
        outs, (do_scr, lse_scr, dl_scr) = rest[:-3], rest[-3:]
        dov = do_ref[...].astype(F32)
        prod = o_ref[...].astype(F32) * dov
        lane = lax.broadcasted_iota(jnp.int32, (tm, LANE), 1)
        delta = jnp.zeros((tm, LANE), F32)
        for h in range(N_HEADS):
            delta = jnp.where(lane == h, jnp.sum(_head(prod, h), axis=1, keepdims=True), delta)
        for ch in range(o_chunks):
            do_scr[ch] = dov[:, ch * LANE:(ch + 1) * LANE]
        lse_scr[0] = lse_ref[...]
        dl_scr[0] = delta
        for k, d in enumerate(dilations):
            for scr, out in zip((do_scr, lse_scr, dl_scr), outs[3 * k:3 * k + 3]):
                _slabs_from_rows(scr, out, d)

    row = pl.BlockSpec((tm, ATT_W), lambda i: (i, 0))
    view = lambda d, w: pl.BlockSpec((tm // d, d * w), lambda i: (i, 0))
    outs = pl.pallas_call(
        body, name="dil_bwd_prep", grid=(SEQ // tm,),
        in_specs=[row, row, pl.BlockSpec((tm, LANE), lambda i: (i, 0))],
        out_specs=[view(d, w) for d in dilations for w in (ATT_W, LANE, LANE)],
        out_shape=[_sds((SEQ // d, d * w), t) for d in dilations for w, t in ((ATT_W, BF16), (LANE, F32), (LANE, F32))],
        scratch_shapes=[pltpu.VMEM((o_chunks, tm, LANE), F32), pltpu.VMEM((1, tm, LANE), F32),
                        pltpu.VMEM((1, tm, LANE), F32)],
        compiler_params=_params("parallel"),
    )(o, do, lse)
    return [outs[3 * k:3 * k + 3] for k in range(len(dilations))]


def _rope_tables():
    half = ROPE_DIM // 2
    inv_freq = np.float32(ROPE_THETA) ** (-np.arange(half, dtype=np.float32) * np.float32(2.0) / np.float32(ROPE_DIM))
    ang = np.arange(SEQ, dtype=np.float32)[:, None] * inv_freq.astype(np.float32)[None, :]
    cos, sin = jnp.asarray(np.cos(ang).astype(np.float32)), jnp.asarray(np.sin(ang).astype(np.float32))
    ones = jnp.ones((SEQ, HEAD_DIM - ROPE_DIM), F32)
    zeros = jnp.zeros((SEQ, HEAD_DIM - ROPE_DIM), F32)
    zh = jnp.zeros((SEQ, half), F32)
    c_tab = jnp.concatenate([cos, cos, ones], axis=1)
    a_tab = jnp.concatenate([-sin, zh, zeros], axis=1)
    b_tab = jnp.concatenate([zh, sin, zeros], axis=1)
    two = lambda t: jnp.concatenate([t, t], axis=1)
    return two(c_tab), two(a_tab), two(b_tab)


def _rotate(x, c_tab, a_tab, b_tab):
    return x * c_tab + pltpu.roll(x, LANE - ROPE_DIM // 2, 1) * a_tab + pltpu.roll(x, ROPE_DIM // 2, 1) * b_tab


def _rope_fwd(zm, tabs, *, tm=512):
    width = 3 * ATT_W
    dilations = [d for _, d in DIL_PATTERNS]

    def body(q_ref, k_ref, v_ref, c_ref, a_ref, b_ref, *rest):
        outs, scr = rest[:-1], rest[-1]
        per_part = ATT_W // LANE
        for part, (x_ref, mult) in enumerate(((q_ref, SCALE), (k_ref, 1.0))):
            for cc in range(per_part):
                sl = slice(cc * LANE, (cc + 1) * LANE)
                scr[part * per_part + cc] = _rotate(x_ref[:, sl].astype(F32), c_ref[...], a_ref[...], b_ref[...]) * mult
        for cc in range(per_part):
            scr[2 * per_part + cc] = v_ref[:, cc * LANE:(cc + 1) * LANE].astype(F32)
        for o_ref, d in zip(outs, dilations):
            for r in range(d):
                for ch in range(width // LANE):
                    o_ref[:, r * width + ch * LANE:r * width + (ch + 1) * LANE] = (
                        scr.at[ch][pl.ds(r, tm // d, stride=d), :].astype(o_ref.dtype))

    tab = pl.BlockSpec((tm, LANE), lambda i: (i, 0))
    col = lambda b: pl.BlockSpec((tm, ATT_W), lambda i: (i, b))
    return pl.pallas_call(
        body, name="rope_fwd", grid=(SEQ // tm,),
        in_specs=[col(3), col(4), col(5), tab, tab, tab],
        out_specs=[pl.BlockSpec((tm // d, d * width), lambda i: (i, 0)) for d in dilations],
        out_shape=[_sds((SEQ // d, d * width), BF16) for d in dilations],
        scratch_shapes=[pltpu.VMEM((width // LANE, tm, LANE), F32)],
        compiler_params=_params("parallel"),
    )(zm, zm, zm, *tabs)


def _dil_grad_combine(dqs, dks, dvs, tabs, dz, *, tm=256):
    dilations = [d for _, d in DIL_PATTERNS]
    chunks = ATT_W // LANE

    def body(*refs):
        groups = (refs[0:3], refs[3:6], refs[6:9])
        c_ref, a_ref, b_ref, _, o_ref, scr = refs[9:]

        def total(part, cc):
            acc = None
            for g, (ref, d) in enumerate(zip(groups[part], dilations)):
                term = ref[:, cc * LANE:(cc + 1) * LANE].astype(F32) if d == 1 else scr[part, g, cc]
                acc = term if acc is None else acc + term
            return acc

        for part in range(3):
            for g, (ref, d) in enumerate(zip(groups[part], dilations)):
                if d > 1:
                    _rows_from_slabs(ref, scr.at[part, g], d)
        for cc in range(chunks):
            for part in range(2):
                o_ref[:, part * ATT_W + cc * LANE:part * ATT_W + (cc + 1) * LANE] = _rotate(
                    total(part, cc), c_ref[...], -a_ref[...], -b_ref[...]).astype(o_ref.dtype)
            o_ref[:, 2 * ATT_W + cc * LANE:2 * ATT_W + (cc + 1) * LANE] = total(2, cc).astype(o_ref.dtype)

    view = lambda d: pl.BlockSpec((tm // d, d * ATT_W), lambda i: (i, 0))
    tab = pl.BlockSpec((tm, LANE), lambda i: (i, 0))
    return pl.pallas_call(
        body, name="dil_grad_combine", grid=(SEQ // tm,),
        in_specs=[view(d) for d in dilations] * 3 + [tab] * 3 + [ANY],
        out_specs=pl.BlockSpec((tm, 3 * ATT_W), lambda i: (i, 1)),
        out_shape=_sds((SEQ, Z_MAIN), BF16),
        input_output_aliases={12: 0},
        scratch_shapes=[pltpu.VMEM((3, len(dilations), chunks, tm, LANE), F32)],
        compiler_params=_params("parallel"),
    )(*dqs, *dks, *dvs, *tabs, dz)


def _dil_valid(n):
    qi = lax.broadcasted_iota(jnp.int32, (DIL_BLK, 2 * DIL_BLK), 0)
    ki = lax.broadcasted_iota(jnp.int32, (DIL_BLK, 2 * DIL_BLK), 1)
    dist = qi + DIL_BLK - ki
    return (dist >= 0) & (dist <= DIL_BLK) & ((n > 0) | (ki >= DIL_BLK))


def _dil_fwd(qkv_v, d):
    length = SEQ // d
    nb = length // DIL_BLK
    nsub = min(DIL_STEP_BLOCKS, nb)

    def body(q_ref, kp_ref, kc_ref, vp_ref, vc_ref, o_ref, lse_ref):
        m_step = pl.program_id(1)
        lane = lax.broadcasted_iota(jnp.int32, (DIL_BLK, LANE), 1)
        jobs = [(sub, h) for sub in range(nsub) for h in range(N_HEADS)]
        rows = lambda sub: slice(sub * DIL_BLK, (sub + 1) * DIL_BLK)
        cols = lambda h: slice(h * HEAD_DIM, (h + 1) * HEAD_DIM)

        def keys(prev_ref, cur_ref, sub, h):
            before = prev_ref[:, cols(h)] if sub == 0 else cur_ref[rows(sub - 1), cols(h)]
            return jnp.concatenate([before, cur_ref[rows(sub), cols(h)]], axis=0)

        scores = [lax.dot_general(q_ref[rows(sub), cols(h)], keys(kp_ref, kc_ref, sub, h), NT_DIMS,
                                  preferred_element_type=F32) for sub, h in jobs]
        ok = [_dil_valid(m_step)] + [_dil_valid(1)] * (nsub - 1)
        probs, inv_l, lse_all = [], [], [jnp.zeros((DIL_BLK, LANE), F32)] * nsub
        for idx, (sub, h) in enumerate(jobs):
            s = jnp.where(ok[sub], scores[idx], NEG_INF)
            m = jnp.max(s, axis=-1, keepdims=True)
            p = jnp.exp(s - m)
            l = jnp.sum(p, axis=-1, keepdims=True)
            probs.append(p.astype(BF16))
            inv_l.append(1.0 / l)
            lse_all[sub] = jnp.where(lane == h, m + jnp.log(l), lse_all[sub])
        outs = [jnp.dot(probs[idx], keys(vp_ref, vc_ref, sub, h), preferred_element_type=F32) * inv_l[idx]
                for idx, (sub, h) in enumerate(jobs)]
        for sub in range(nsub):
            o_ref[rows(sub), :] = jnp.concatenate(outs[sub * N_HEADS:(sub + 1) * N_HEADS], axis=1).astype(o_ref.dtype)
            lse_ref[rows(sub), :] = lse_all[sub]

    pair = lambda f: pl.BlockSpec((nsub * DIL_BLK, ATT_W), f)
    one = lambda f: pl.BlockSpec((DIL_BLK, ATT_W), f)
    before = lambda m: jnp.maximum(nsub * m - 1, 0)
    o, lse = pl.pallas_call(
        body, name=f"dil_fwd_d{d}", grid=(d, nb // nsub),
        in_specs=[pair(lambda r, m: (m, 3 * r)),
                  one(lambda r, m: (before(m), 3 * r + 1)), pair(lambda r, m: (m, 3 * r + 1)),
                  one(lambda r, m: (before(m), 3 * r + 2)), pair(lambda r, m: (m, 3 * r + 2))],
        out_specs=[pair(lambda r, m: (m, r)), pl.BlockSpec((nsub * DIL_BLK, LANE), lambda r, m: (m, r))],
        out_shape=[_sds((length, d * ATT_W), BF16), _sds((length, d * LANE), F32)],
        compiler_params=_params("parallel", "arbitrary"),
    )(qkv_v, qkv_v, qkv_v, qkv_v, qkv_v)
    return o, lse


def _rows_from_slabs(view_ref, scr, d):
    chunks, rows = scr.shape[0], scr.shape[1]
    for r in range(d):
        for ch in range(chunks):
            lo = (r * chunks + ch) * LANE
            scr.at[ch][pl.ds(r, rows // d, stride=d), :] = view_ref[:, lo:lo + LANE].astype(F32)


def _slabs_from_rows(scr, view_ref, d):
    chunks, rows = scr.shape[0], scr.shape[1]
    for r in range(d):
        for ch in range(chunks):
            lo = (r * chunks + ch) * LANE
            view_ref[:, lo:lo + LANE] = scr.at[ch][pl.ds(r, rows // d, stride=d), :].astype(view_ref.dtype)


def _dil_merge(os_, lses, *, tm=512):
    dilations = [d for _, d in DIL_PATTERNS]
    o_chunks = ATT_W // LANE

    def body(o0, o1, o2, l0, l1, l2, y_ref, lse_ref, o_scr, l_scr):
        os_nat, ls = [], []
        for g, (o_ref, l_ref, d) in enumerate(zip((o0, o1, o2), (l0, l1, l2), dilations)):
            if d == 1:
                os_nat.append(o_ref[...].astype(F32))
                ls.append(l_ref[...])
            else:
                _rows_from_slabs(o_ref, o_scr.at[g], d)
                _rows_from_slabs(l_ref, l_scr.at[g], d)
                os_nat.append(jnp.concatenate([o_scr[g, ch] for ch in range(o_chunks)], axis=1))
                ls.append(l_scr[g, 0])
        m = jnp.maximum(jnp.maximum(ls[0], ls[1]), ls[2])
        es = [jnp.exp(l - m) for l in ls]
        tot = es[0] + es[1] + es[2]
        lse_ref[...] = m + jnp.log(tot)
        alphas = [e / tot for e in es]
        outs = []
        for h in range(N_HEADS):
            acc = None
            for g in range(3):
                term = alphas[g][:, h:h + 1] * _head(os_nat[g], h)
                acc = term if acc is None else acc + term
            outs.append(acc)
        y_ref[...] = jnp.concatenate(outs, axis=1).astype(y_ref.dtype)

    row = pl.BlockSpec((tm, ATT_W), lambda i: (i, 0))
    vec = pl.BlockSpec((tm, LANE), lambda i: (i, 0))
    view = lambda d, w: pl.BlockSpec((tm // d, d * w), lambda i: (i, 0))
    return pl.pallas_call(
        body, name="dil_merge", grid=(SEQ // tm,),
        in_specs=[view(d, ATT_W) for d in dilations] + [view(d, LANE) for d in dilations], out_specs=[row, vec],
        out_shape=[_sds((SEQ, ATT_W), BF16), _sds((SEQ, LANE), F32)],
        scratch_shapes=[pltpu.VMEM((3, o_chunks, tm, LANE), F32), pltpu.VMEM((3, 1, tm, LANE), F32)],
        compiler_params=_params("parallel"),
    )(*os_, *lses)


def _dil_bwd(qkv_v, do_v, lse_v, dl_v, d):
    length = SEQ // d
    nb = length // DIL_BLK
    nsub = min(DIL_STEP_BLOCKS, nb)
    n_steps = nb // nsub

    def body(q_ref, kp_ref, kc_ref, vp_ref, vc_ref, lse_ref, dl_ref, do_ref, dq_ref, dk_ref, dv_ref, dk_s, dv_s):
        m_step = pl.program_id(1)

        @pl.when(m_step == 0)
        def _():
            dk_s[...] = jnp.zeros_like(dk_s)
            dv_s[...] = jnp.zeros_like(dv_s)

        jobs = [(sub, h) for sub in range(nsub) for h in range(N_HEADS)]
        rows = lambda sub: slice(sub * DIL_BLK, (sub + 1) * DIL_BLK)
        cols = lambda h: slice(h * HEAD_DIM, (h + 1) * HEAD_DIM)

        def keys(prev_ref, cur_ref, sub, h):
            before = prev_ref[:, cols(h)] if sub == 0 else cur_ref[rows(sub - 1), cols(h)]
            return jnp.concatenate([before, cur_ref[rows(sub), cols(h)]], axis=0)

        kks = [keys(kp_ref, kc_ref, sub, h) for sub, h in jobs]
        scores = [lax.dot_general(q_ref[rows(sub), cols(h)], kks[idx], NT_DIMS, preferred_element_type=F32)
                  for idx, (sub, h) in enumerate(jobs)]
        dps = [lax.dot_general(do_ref[rows(sub), cols(h)], keys(vp_ref, vc_ref, sub, h), NT_DIMS,
                               preferred_element_type=F32) for sub, h in jobs]
        ok = [_dil_valid(m_step)] + [_dil_valid(1)] * (nsub - 1)
        ps, dss = [], []
        for idx, (sub, h) in enumerate(jobs):
            p = jnp.where(ok[sub], jnp.exp(scores[idx] - lse_ref[rows(sub), h:h + 1]), 0.0)
            ps.append(p.astype(BF16))
            dss.append((p * (dps[idx] - dl_ref[rows(sub), h:h + 1])).astype(BF16))
        dqs = [jnp.dot(dss[idx], kks[idx], preferred_element_type=F32) * SCALE for idx in range(len(jobs))]
        dkks = [lax.dot_general(dss[idx], q_ref[rows(sub), cols(h)], TN_DIMS, preferred_element_type=F32)
                for idx, (sub, h) in enumerate(jobs)]
        dvvs = [lax.dot_general(ps[idx], do_ref[rows(sub), cols(h)], TN_DIMS, preferred_element_type=F32)
                for idx, (sub, h) in enumerate(jobs)]
        for sub in range(nsub):
            dq_ref[rows(sub), :] = jnp.concatenate(dqs[sub * N_HEADS:(sub + 1) * N_HEADS], axis=1).astype(dq_ref.dtype)
        base = m_step * (nsub * DIL_BLK)
        blocks = [pl.ds(pl.multiple_of(jnp.maximum(base - DIL_BLK, 0), DIL_BLK), DIL_BLK)]
        blocks += [pl.ds(pl.multiple_of(base + s * DIL_BLK, DIL_BLK), DIL_BLK) for s in range(nsub)]
        for acc, parts in ((dk_s, dkks), (dv_s, dvvs)):
            top = lambda sub: jnp.concatenate([parts[sub * N_HEADS + h][:DIL_BLK] for h in range(N_HEADS)], axis=1)
            bottom = lambda sub: jnp.concatenate([parts[sub * N_HEADS + h][DIL_BLK:] for h in range(N_HEADS)], axis=1)
            acc[blocks[0], :] += top(0)
            for s in range(nsub):
                acc[blocks[s + 1], :] += bottom(s) + top(s + 1) if s + 1 < nsub else bottom(s)

        @pl.when(m_step == n_steps - 1)
        def _():
            dk_ref[...] = dk_s[...].astype(dk_ref.dtype)
            dv_ref[...] = dv_s[...].astype(dv_ref.dtype)

    pair = lambda f: pl.BlockSpec((nsub * DIL_BLK, ATT_W), f)
    one = lambda f: pl.BlockSpec((DIL_BLK, ATT_W), f)
    vec = lambda f: pl.BlockSpec((nsub * DIL_BLK, LANE), f)
    whole = pl.BlockSpec((length, ATT_W), lambda r, m: (0, r))
    before = lambda m: jnp.maximum(nsub * m - 1, 0)
    outs = pl.pallas_call(
        body, name=f"dil_bwd_d{d}", grid=(d, n_steps),
        in_specs=[pair(lambda r, m: (m, 3 * r)),
                  one(lambda r, m: (before(m), 3 * r + 1)), pair(lambda r, m: (m, 3 * r + 1)),
                  one(lambda r, m: (before(m), 3 * r + 2)), pair(lambda r, m: (m, 3 * r + 2)),
                  vec(lambda r, m: (m, r)), vec(lambda r, m: (m, r)), pair(lambda r, m: (m, r))],
        out_specs=[pair(lambda r, m: (m, r)), whole, whole],
        out_shape=[_sds((length, d * ATT_W), BF16)] * 3,
        scratch_shapes=[pltpu.VMEM((length, ATT_W), F32), pltpu.VMEM((length, ATT_W), F32)],
        compiler_params=_params("arbitrary", "arbitrary"),
    )(qkv_v, qkv_v, qkv_v, qkv_v, qkv_v, lse_v, dl_v, do_v)
    return outs


def _sigmoid(x):
    return 1.0 / (1.0 + jnp.exp(-x))


def _mix_fwd(ya, yb, w_oa, w_ob, zm, *, tm=512):
    def body(ya_ref, yb_ref, wa_ref, wb_ref, ga_ref, gb_ref, pa_ref, pb_ref, mix_ref):
        pa = jnp.dot(ya_ref[...], wa_ref[...], preferred_element_type=F32)
        pb = jnp.dot(yb_ref[...], wb_ref[...], preferred_element_type=F32)
        pa_ref[...] = pa.astype(pa_ref.dtype)
        pb_ref[...] = pb.astype(pb_ref.dtype)
        mix_ref[...] = (_sigmoid(ga_ref[...].astype(F32)) * pa + _sigmoid(gb_ref[...].astype(F32)) * pb
                        ).astype(mix_ref.dtype)

    row = pl.BlockSpec((tm, ATT_W), lambda i: (i, 0))
    wsp = pl.BlockSpec((ATT_W, D_MODEL), lambda i: (0, 0))
    wide = pl.BlockSpec((tm, D_MODEL), lambda i: (i, 0))
    return pl.pallas_call(
        body, name="mix_fwd", grid=(SEQ // tm,),
        in_specs=[row, row, wsp, wsp, pl.BlockSpec((tm, D_MODEL), lambda i: (i, 3)),
                  pl.BlockSpec((tm, D_MODEL), lambda i: (i, 4))],
        out_specs=[wide] * 3, out_shape=[_sds((SEQ, D_MODEL), BF16)] * 3,
        compiler_params=_params("parallel"),
    )(ya, yb, w_oa, w_ob, zm, zm)


def _gate_bwd(dmix, zm, p, gate_block, dz, *, name, tm=512):
    def body(dm_ref, g_ref, p_ref, *rest):
        dp_ref, dz_ref = rest[-2], rest[-1]
        dm = dm_ref[...].astype(F32)
        s = _sigmoid(g_ref[...].astype(F32))
        dp_ref[...] = (dm * s).astype(dp_ref.dtype)
        dz_ref[...] = (dm * p_ref[...].astype(F32) * s * (1.0 - s)).astype(dz_ref.dtype)

    wide = pl.BlockSpec((tm, D_MODEL), lambda i: (i, 0))
    gate = pl.BlockSpec((tm, D_MODEL), lambda i: (i, gate_block))
    extra = [] if dz is None else [dz]
    return pl.pallas_call(
        body, name=name, grid=(SEQ // tm,),
        in_specs=[wide, gate, wide] + [ANY] * len(extra),
        out_specs=[wide, gate],
        out_shape=[_sds((SEQ, D_MODEL), BF16), _sds((SEQ, Z_MAIN), BF16)],
        input_output_aliases={3: 1} if extra else {},
        compiler_params=_params("parallel"),
    )(dmix, zm, p, *extra)


def _out_fwd(mixed, w_out, x, g_post, g_pre, *, tm=512):
    def body(m_ref, w_ref, x_ref, gp_ref, gn_ref, y_ref, x2_ref, h_ref):
        y = jnp.dot(m_ref[...], w_ref[...], preferred_element_type=F32)
        y_ref[...] = y
        r = lax.rsqrt(jnp.mean(y * y, axis=-1, keepdims=True) + RMS_EPS)
        x2 = x_ref[...] + y * r * gp_ref[...]
        x2_ref[...] = x2
        r2 = lax.rsqrt(jnp.mean(x2 * x2, axis=-1, keepdims=True) + RMS_EPS)
        h_ref[...] = (x2 * r2 * gn_ref[...]).astype(h_ref.dtype)

    row = pl.BlockSpec((tm, D_MODEL), lambda i: (i, 0))
    vec = pl.BlockSpec((1, D_MODEL), lambda i: (0, 0))
    return pl.pallas_call(
        body, name="out_fwd", grid=(SEQ // tm,),
        in_specs=[row, pl.BlockSpec((D_MODEL, D_MODEL), lambda i: (0, 0)), row, vec, vec],
        out_specs=[row] * 3,
        out_shape=[_sds((SEQ, D_MODEL), F32), _sds((SEQ, D_MODEL), F32), _sds((SEQ, D_MODEL), BF16)],
        compiler_params=_params("parallel"),
    )(mixed, w_out, x, g_post, g_pre)


FFN_HALF = 256
FFN_TN = 2 * FFN_HALF
FFN_NJ = D_FF // FFN_HALF
FFN_GROUP = 2 * SUBLANE
UP_TM = 1024


def _ffn_interleave(t):
    lead = t.shape[:-1]
    return jnp.swapaxes(t.reshape(*lead, 2, FFN_NJ, FFN_HALF), -3, -2).reshape(*lead, 2 * D_FF)


def _ffn_deinterleave(t):
    lead = t.shape[:-1]
    return jnp.swapaxes(t.reshape(*lead, FFN_NJ, 2, FFN_HALF), -3, -2).reshape(*lead, 2 * D_FF)


W_IN_SHARD = (Z_MAIN + N_HEADS) // N_DEV
FORGET_LO = 3 * ATT_W


def _w_in_from_shards(shards, *, tm=256):
    def columns(g_ref, lo, width):
        p, off = divmod(lo, W_IN_SHARD)
        if off + width <= W_IN_SHARD:
            return g_ref[p, :, off:off + width]
        first = W_IN_SHARD - off
        return jnp.concatenate([g_ref[p, :, off:], g_ref[p + 1, :, :width - first]], axis=1)

    def body(g_ref, main_ref, f_ref):
        for t in range(Z_MAIN // LANE):
            lo = t * LANE
            main_ref[:, lo:lo + LANE] = columns(g_ref, lo if lo < FORGET_LO else lo + N_HEADS, LANE)
        f_ref[...] = jnp.concatenate([columns(g_ref, FORGET_LO, N_HEADS),
                                      jnp.zeros((tm, F_PAD - N_HEADS), f_ref.dtype)], axis=1)

    return pl.pallas_call(
        body, name="w_in_from_shards", grid=(D_MODEL // tm,),
        in_specs=[pl.BlockSpec((N_DEV, tm, W_IN_SHARD), lambda i: (0, i, 0))],
        out_specs=[pl.BlockSpec((tm, Z_MAIN), lambda i: (i, 0)), pl.BlockSpec((tm, F_PAD), lambda i: (i, 0))],
        out_shape=[_sds((D_MODEL, Z_MAIN), shards.dtype), _sds((D_MODEL, F_PAD), shards.dtype)],
        compiler_params=_params("parallel"),
    )(shards)


def _w_in_to_shards(g_main, g_f, *, tm=256):
    def natural(main_ref, f_ref, lo, width):
        pieces, hi = [], lo + width
        for ref, start, stop, shift in ((main_ref, 0, FORGET_LO, 0), (f_ref, FORGET_LO, FORGET_LO + N_HEADS, FORGET_LO),
                                        (main_ref, FORGET_LO + N_HEADS, Z_MAIN + N_HEADS, N_HEADS)):
            a, b = max(lo, start), min(hi, stop)
            if a < b:
                pieces.append(ref[:, a - shift:b - shift])
        return pieces[0] if len(pieces) == 1 else jnp.concatenate(pieces, axis=1)

    def body(main_ref, f_ref, o_ref):
        for p in range(N_DEV):
            for q in range(-(-W_IN_SHARD // LANE)):
                width = min(LANE, W_IN_SHARD - q * LANE)
                o_ref[p, :, q * LANE:q * LANE + width] = natural(main_ref, f_ref, p * W_IN_SHARD + q * LANE, width)

    return pl.pallas_call(
        body, name="w_in_to_shards", grid=(D_MODEL // tm,),
        in_specs=[pl.BlockSpec((tm, Z_MAIN), lambda i: (i, 0)), pl.BlockSpec((tm, F_PAD), lambda i: (i, 0))],
        out_specs=pl.BlockSpec((N_DEV, tm, W_IN_SHARD), lambda i: (0, i, 0)),
        out_shape=_sds((N_DEV, D_MODEL, W_IN_SHARD), g_main.dtype),
        compiler_params=_params("parallel"),
    )(g_main, g_f)


W_UP_SHARD = 2 * D_FF // N_DEV


def _w_up_lane_tile(k):
    block = k // 2
    return (2 * (block % FFN_NJ) + block // FFN_NJ) * FFN_HALF + (k % 2) * LANE


def _w_up_from_shards(shards, *, tm=256):
    def body(g_ref, o_ref):
        for k in range(2 * D_FF // LANE):
            p, off = divmod(k * LANE, W_UP_SHARD)
            if off + LANE <= W_UP_SHARD:
                tile = g_ref[p, :, off:off + LANE]
            else:
                tile = jnp.concatenate([g_ref[p, :, off:], g_ref[p + 1, :, :off + LANE - W_UP_SHARD]], axis=1)
            dst = _w_up_lane_tile(k)
            o_ref[:, dst:dst + LANE] = tile

    return pl.pallas_call(
        body, name="w_up_from_shards", grid=(D_MODEL // tm,),
        in_specs=[pl.BlockSpec((N_DEV, tm, W_UP_SHARD), lambda i: (0, i, 0))],
        out_specs=pl.BlockSpec((tm, 2 * D_FF), lambda i: (i, 0)),
        out_shape=_sds((D_MODEL, 2 * D_FF), shards.dtype),
        compiler_params=_params("parallel"),
    )(shards)


def _w_up_to_shards(t, *, tm=256):
    def body(x_ref, o_ref):
        for p in range(N_DEV):
            for q in range(-(-W_UP_SHARD // LANE)):
                width = min(LANE, W_UP_SHARD - q * LANE)
                k, off = divmod(p * W_UP_SHARD + q * LANE, LANE)
                src = _w_up_lane_tile(k)
                if off == 0:
                    tile = x_ref[:, src:src + width]
                else:
                    tile = x_ref[:, src + off:src + LANE]
                    if width > LANE - off:
                        nxt = _w_up_lane_tile(k + 1)
                        tile = jnp.concatenate([tile, x_ref[:, nxt:nxt + width - (LANE - off)]], axis=1)
                o_ref[p, :, q * LANE:q * LANE + width] = tile

    return pl.pallas_call(
        body, name="w_up_to_shards", grid=(D_MODEL // tm,),
        in_specs=[pl.BlockSpec((tm, 2 * D_FF), lambda i: (i, 0))],
        out_specs=pl.BlockSpec((N_DEV, tm, W_UP_SHARD), lambda i: (0, i, 0)),
        out_shape=_sds((N_DEV, D_MODEL, W_UP_SHARD), t.dtype),
        compiler_params=_params("parallel"),
    )(t)


def _gelu_parts(a):
    c = math.sqrt(2.0 / math.pi)
    a2 = a * a
    t = jnp.tanh((c * a) * (1.0 + 0.044715 * a2))
    half_a, one_t = 0.5 * a, 1.0 + t
    gelu = half_a * one_t
    dgelu = 0.5 * one_t + half_a * (1.0 - t * t) * (c + (3.0 * 0.044715 * c) * a2)
    return gelu, dgelu


def _row_masks(down):
    row = lax.broadcasted_iota(jnp.int32, (SUBLANE, FFN_TN), 0)
    return (row < 1, row < 2) if down else (row >= SUBLANE - 1, row >= SUBLANE - 2)


def _rolled(x, down):
    return (pltpu.roll(x, 1, 0), pltpu.roll(x, 2, 0)) if down else (
        pltpu.roll(x, SUBLANE - 1, 0), pltpu.roll(x, SUBLANE - 2, 0))


def _shifted(cur_rolled, neighbour_rolled, masks):
    return (jnp.where(masks[0], neighbour_rolled[0], cur_rolled[0]),
            jnp.where(masks[1], neighbour_rolled[1], cur_rolled[1]))


def _conv_consts(w_ref, b_ref):
    shape = (SUBLANE, FFN_TN)
    return [jnp.broadcast_to(w_ref[k:k + 1, :], shape) for k in range(3)] + [jnp.broadcast_to(b_ref[...], shape)]


def _up_conv_fwd(h2, w_up, conv_w, conv_b):
    nrow = SEQ // UP_TM
    n_tiles = FFN_NJ * nrow
    n_groups = UP_TM // FFN_GROUP

    def body(h_ref, wu_ref, w_ref, b_ref, u_ref, ab_ref, m_ref, ua_s, ub_s, c1_s, c2_s):
        k = pl.program_id(0)

        @pl.when(k == 0)
        def _():
            ub_s[...] = jnp.zeros_like(ub_s)

        @pl.when(jnp.maximum(k - 1, 0) % nrow == 0)
        def _():
            c1_s[...] = jnp.zeros_like(c1_s)
            c2_s[...] = jnp.zeros_like(c2_s)

        def step(write_s, read_s):
            h_rows = pl.ds(pl.multiple_of((this(k) % nrow) * UP_TM, UP_TM), UP_TM)
            u = jnp.dot(h_ref[h_rows, :], wu_ref[...], preferred_element_type=F32)
            write_s[...] = u
            u_ref[...] = u.astype(u_ref.dtype)
            w0, w1, w2, bias = _conv_consts(w_ref, b_ref)
            masks = _row_masks(True)
            above = (c1_s[...], c2_s[...])
            for g in range(n_groups):
                rows = slice(g * FFN_GROUP, (g + 1) * FFN_GROUP)
                x = read_s[rows, :]
                convs = []
                for c in range(2):
                    cur = x[c * SUBLANE:(c + 1) * SUBLANE]
                    cur_rolled = _rolled(cur, True)
                    s1, s2 = _shifted(cur_rolled, above, masks)
                    convs.append(w0 * s2 + w1 * s1 + w2 * cur + bias)
                    above = cur_rolled
                y = jnp.concatenate(convs, axis=0)
                ab_ref[rows, :] = y.astype(ab_ref.dtype)
                m_ref[rows, :] = (_gelu_parts(y[:, :FFN_HALF])[0] * y[:, FFN_HALF:]).astype(m_ref.dtype)
            c1_s[...], c2_s[...] = above

        @pl.when(k % 2 == 0)
        def _():
            step(ua_s, ub_s)

        @pl.when(k % 2 == 1)
        def _():
            step(ub_s, ua_s)

    this = lambda k: jnp.minimum(k, n_tiles - 1)
    last = lambda k: jnp.maximum(k - 1, 0)
    blk = lambda tile: pl.BlockSpec((UP_TM, FFN_TN), lambda k: (tile(k) % nrow, tile(k) // nrow))
    return pl.pallas_call(
        body, name="up_conv_fwd", grid=(n_tiles + 1,),
        in_specs=[pl.BlockSpec((SEQ, D_MODEL), lambda k: (0, 0)),
                  pl.BlockSpec((D_MODEL, FFN_TN), lambda k: (0, this(k) // nrow)),
                  pl.BlockSpec((3, FFN_TN), lambda k: (0, last(k) // nrow)),
                  pl.BlockSpec((1, FFN_TN), lambda k: (0, last(k) // nrow))],
        out_specs=[blk(this), blk(last), pl.BlockSpec((UP_TM, FFN_HALF), lambda k: (last(k) % nrow, last(k) // nrow))],
        out_shape=[_sds((SEQ, 2 * D_FF), BF16), _sds((SEQ, 2 * D_FF), BF16), _sds((SEQ, D_FF), BF16)],
        scratch_shapes=[pltpu.VMEM((UP_TM, FFN_TN), F32), pltpu.VMEM((UP_TM, FFN_TN), F32),
                        pltpu.VMEM((SUBLANE, FFN_TN), F32), pltpu.VMEM((SUBLANE, FFN_TN), F32)],
        compiler_params=_params("arbitrary"),
    )(h2, w_up, conv_w, conv_b)


def _ffn_mid_bwd(dy2, w_down, u, ab, conv_w):
    nrow = SEQ // UP_TM
    n_tiles = FFN_NJ * nrow
    n_groups = UP_TM // FFN_GROUP
    this = lambda k: jnp.minimum(k, n_tiles - 1)
    last = lambda k: jnp.maximum(k - 1, 0)
    row_of = lambda tile: nrow - 1 - tile % nrow

    def body(dy_ref, wd_ref, u_ref, ab_ref, w_ref, du_ref, gw_ref, gb_ref, c_s, dma_s, dmb_s):
        k = pl.program_id(0)

        @pl.when(k == 0)
        def _():
            dmb_s[...] = jnp.zeros_like(dmb_s)

        @pl.when(last(k) % nrow == 0)
        def _():
            c_s[...] = jnp.zeros_like(c_s)
            gw_ref[...] = jnp.zeros_like(gw_ref)
            gb_ref[...] = jnp.zeros_like(gb_ref)

        def step(write_s, read_s):
            dy_rows = pl.ds(pl.multiple_of(row_of(this(k)) * UP_TM, UP_TM), UP_TM)
            write_s[...] = lax.dot_general(dy_ref[dy_rows, :], wd_ref[...], NT_DIMS,
                                           preferred_element_type=F32)
            taps = [jnp.broadcast_to(w_ref[t:t + 1, :], (SUBLANE, FFN_TN)) for t in range(3)]
            masks = _row_masks(False)
            below = _rolled(c_s[...], False)
            acc = [jnp.zeros((SUBLANE, FFN_TN), F32)] * 4
            for g in reversed(range(n_groups)):
                rows = slice(g * FFN_GROUP, (g + 1) * FFN_GROUP)
                x, y, dmv = u_ref[rows, :].astype(F32), ab_ref[rows, :].astype(F32), read_s[rows, :]
                gelu, dgelu = _gelu_parts(y[:, :FFN_HALF])
                d = jnp.concatenate([dmv * y[:, FFN_HALF:] * dgelu, dmv * gelu], axis=1)
                pre = [None, None]
                for c in (1, 0):
                    sl = slice(c * SUBLANE, (c + 1) * SUBLANE)
                    cur, xs = d[sl], x[sl]
                    cur_rolled = _rolled(cur, False)
                    up1, up2 = _shifted(cur_rolled, below, masks)
                    acc = [acc[0] + up2 * xs, acc[1] + up1 * xs, acc[2] + cur * xs, acc[3] + cur]
                    pre[c] = taps[2] * cur + taps[1] * up1 + taps[0] * up2
                    below = cur_rolled
                du_ref[rows, :] = jnp.concatenate(pre, axis=0).astype(du_ref.dtype)
            c_s[...] = pltpu.roll(below[0], 1, 0)
            for t in range(3):
                gw_ref[t:t + 1, :] += jnp.sum(acc[t], axis=0, keepdims=True)
            gb_ref[...] += jnp.sum(acc[3], axis=0, keepdims=True)

        @pl.when(k % 2 == 0)
        def _():
            step(dma_s, dmb_s)

        @pl.when(k % 2 == 1)
        def _():
            step(dmb_s, dma_s)

    blk = pl.BlockSpec((UP_TM, FFN_TN), lambda k: (row_of(last(k)), last(k) // nrow))
    col = lambda rows: pl.BlockSpec((rows, FFN_TN), lambda k: (0, last(k) // nrow))
    return pl.pallas_call(
        body, name="ffn_mid_bwd", grid=(n_tiles + 1,),
        in_specs=[pl.BlockSpec((SEQ, D_MODEL), lambda k: (0, 0)),
                  pl.BlockSpec((FFN_HALF, D_MODEL), lambda k: (this(k) // nrow, 0)), blk, blk, col(3)],
        out_specs=[blk, col(3), col(1)],
        out_shape=[_sds((SEQ, 2 * D_FF), BF16), _sds((3, 2 * D_FF), F32), _sds((1, 2 * D_FF), F32)],
        scratch_shapes=[pltpu.VMEM((SUBLANE, FFN_TN), F32), pltpu.VMEM((UP_TM, FFN_HALF), F32),
                        pltpu.VMEM((UP_TM, FFN_HALF), F32)],
        compiler_params=_params("arbitrary"),
    )(dy2, w_down, u, ab, conv_w)


def _down_fwd(m, w_down, x2, g_post, target, *, tm=512):
    def body(m_ref, w_ref, x2_ref, g_ref, t_ref, dout_ref, dy_ref, gg_ref, loss_ref):
        @pl.when(pl.program_id(0) == 0)
        def _():
            gg_ref[...] = jnp.zeros_like(gg_ref)
            loss_ref[...] = jnp.zeros_like(loss_ref)

        y = jnp.dot(m_ref[...], w_ref[...], preferred_element_type=F32)
        r = lax.rsqrt(jnp.mean(y * y, axis=-1, keepdims=True) + RMS_EPS)
        yn = y * r
        diff = (x2_ref[...] + yn * g_ref[...]) - t_ref[...]
        loss_ref[...] += jnp.sum(diff * diff)
        dout = diff * (1.0 / D_MODEL)
        dout_ref[...] = dout
        gg_ref[...] += jnp.sum(dout * yn, axis=0, keepdims=True)
        dn = dout * g_ref[...]
        dy_ref[...] = (r * (dn - yn * jnp.mean(dn * yn, axis=-1, keepdims=True))).astype(dy_ref.dtype)

    row = pl.BlockSpec((tm, D_MODEL), lambda i: (i, 0))
    vec = pl.BlockSpec((1, D_MODEL), lambda i: (0, 0))
    return pl.pallas_call(
        body, name="down_fwd", grid=(SEQ // tm,),
        in_specs=[pl.BlockSpec((tm, D_FF), lambda i: (i, 0)), pl.BlockSpec((D_FF, D_MODEL), lambda i: (0, 0)),
                  row, vec, row],
        out_specs=[row, row, vec, pl.BlockSpec((1, LANE), lambda i: (0, 0))],
        out_shape=[_sds((SEQ, D_MODEL), F32), _sds((SEQ, D_MODEL), BF16), _sds((1, D_MODEL), F32),
                   _sds((1, LANE), F32)],
        compiler_params=_params("arbitrary"),
    )(m, w_down, x2, g_post, target)


def _local_step(x, target, w_main, w_f, b_forget, conv_b, g_pre_mix, g_post_mix, g_pre_ffn, g_post_ffn,
                proj_weights, ffn_weights, ffn_grads_ready, proj_grads_ready, mixer_grads_ready, after=None):
    mm = _matmul
    tabs = _rope_tables()

    h1 = _rms_fwd(x, g_pre_mix, name="rms_pre_mix", after=after)
    zm = mm(h1, w_main, out_dtype=BF16, tm=2048, tn=1024, tk=1024, name="in_proj")
    zf = mm(h1, w_f, out_dtype=F32, tm=2048, tn=F_PAD, tk=1024, name="in_proj_forget")
    f_row, sg_row = _fox_prep(zf[:, :N_HEADS].T, b_forget.reshape(N_HEADS, 1))
    f_cols = jnp.pad(f_row.T, ((0, 0), (0, LANE - N_HEADS)))
    q_slots, k_slots, v_slots = _fox_pack_fwd(zm, f_cols)
    ya, lse_a = _fox_fwd(q_slots, k_slots, v_slots)
    qkv_d = dict(zip([d for _, d in DIL_PATTERNS], _rope_fwd(zm, tabs)))
    dil = [_dil_fwd(qkv_d[d], d) for _, d in DIL_PATTERNS]
    yb, lse_b = _dil_merge([o for o, _ in dil], [l for _, l in dil])
    w_oa, w_ob, w_out = proj_weights(yb)
    pa, pb, mixed = _mix_fwd(ya, yb, w_oa, w_ob, zm)
    y1, x2, h2 = _out_fwd(mixed, w_out, x, g_post_mix, g_pre_ffn)
    w_up, conv_w, w_down = ffn_weights(h2)
    u, ab, m = _up_conv_fwd(h2, w_up, conv_w, _ffn_interleave(conv_b))
    dout, dy2, gg_post_ffn, sq_err = _down_fwd(m, w_down, x2, g_post_ffn, target)

    g_w_down = mm(m, dy2, ta=True, out_dtype=BF16, tm=D_FF // 2, tn=1024, tk=2048, name="grad_w_down")
    du, g_conv_w, g_conv_b = _ffn_mid_bwd(dy2, w_down, u, ab, conv_w)
    g_w_up = mm(h2, du, ta=True, out_dtype=BF16, tm=1024, tn=D_FF // 2, tk=2048, name="grad_w_up")
    tok = ffn_grads_ready(dict(w_down=g_w_down, w_up_blocks=g_w_up, conv_w=_ffn_deinterleave(g_conv_w)))
    dh2 = mm(du, w_up, tb=True, out_dtype=BF16, tm=512, tn=1024, tk=2 * D_FF, name="d_h2")

    dx2, dy1, gg_pre_ffn, gg_post_mix = _rms_pair_bwd([dh2], x2, g_pre_ffn, dout, y1, g_post_mix, after=tok)
    g_w_out = mm(mixed, dy1, ta=True, out_dtype=BF16, tm=1024, tn=1024, tk=2048, name="grad_w_out")
    dmix = mm(dy1, w_out, tb=True, out_dtype=BF16, tm=2048, tn=1024, tk=1024, name="d_mixed")
    dpa, dz = _gate_bwd(dmix, zm, pa, 3, None, name="gate_bwd_fox")
    dpb, dz = _gate_bwd(dmix, zm, pb, 4, dz, name="gate_bwd_dil")
    g_w_oa = mm(ya, dpa, ta=True, out_dtype=BF16, tm=512, tn=1024, tk=SEQ, name="grad_w_o_fox")
    g_w_ob = mm(yb, dpb, ta=True, out_dtype=BF16, tm=512, tn=1024, tk=SEQ, name="grad_w_o_dil")
    tok = proj_grads_ready(dict(w_o_fox=g_w_oa, w_o_dil=g_w_ob, w_out=g_w_out))
    dya = mm(dpa, w_oa, tb=True, out_dtype=BF16, tm=2048, tn=512, tk=1024, name="d_y_fox")
    dyb = mm(dpb, w_ob, tb=True, out_dtype=BF16, tm=2048, tn=512, tk=1024, name="d_y_dil")

    qb_slots, do_slots = _fox_pack_bwd(zm, f_cols, lse_a, ya, dya, after=tok)
    dz, df_cols = _fox_unpack(*_fox_bwd(qb_slots, k_slots, v_slots, do_slots), dz)
    dfa_t, g_b_forget = _fox_post_bwd(df_cols[:, :N_HEADS].T, sg_row)

    rows_d = _dil_bwd_prep(yb, dyb, lse_b)
    dil_g = [_dil_bwd(qkv_d[d], *rows_d[k], d) for k, (_, d) in enumerate(DIL_PATTERNS)]
    dz = _dil_grad_combine([g[0] for g in dil_g], [g[1] for g in dil_g], [g[2] for g in dil_g], tabs, dz)

    dzf = jnp.pad(dfa_t.T, ((0, 0), (0, F_PAD - N_HEADS)))
    g_w_main = mm(h1, dz, ta=True, out_dtype=BF16, tm=1024, tn=Z_MAIN // 4, tk=2048, name="grad_w_in")
    g_w_f = mm(h1, dzf, ta=True, out_dtype=BF16, tm=1024, tn=F_PAD, tk=1024, name="grad_w_in_forget")
    tok = mixer_grads_ready(dict(w_main=g_w_main, w_f=g_w_f))
    dh1 = [mm(dz, w_main, tb=True, out_dtype=BF16, tm=512, tn=1024, tk=Z_MAIN, name="d_h1", after=tok),
           mm(dzf, w_f, tb=True, out_dtype=BF16, tm=2048, tn=1024, tk=F_PAD, name="d_h1_forget", after=tok)]
    grad_x, gg_pre_mix = _rms_bwd(dh1, x, g_pre_mix, dx2, out_dtype=F32, name="rms_pre_mix_bwd")

    grads = dict(
        b_forget=g_b_forget.reshape(1, N_HEADS), conv_b=_ffn_deinterleave(g_conv_b),
        g_pre_mix=gg_pre_mix, g_post_mix=gg_post_mix, g_pre_ffn=gg_pre_ffn, g_post_ffn=gg_post_ffn)
    return sq_err, grad_x, grads


def _exchange(arrays, scatter, *, name):
    n = len(arrays)
    scatters = [scatter] * n if isinstance(scatter, bool) else list(scatter)

    def body(*refs):
        ins, outs = refs[:n], refs[n:2 * n]
        send_sems, recv_sems, local_sems = refs[2 * n:]
        me, peers = _peers()

        def remote(a, k):
            dev, slot = peers[k]
            return pltpu.make_async_remote_copy(
                src_ref=ins[a].at[slot] if scatters[a] else ins[a], dst_ref=outs[a].at[me],
                send_sem=send_sems.at[a, k], recv_sem=recv_sems.at[a, k],
                device_id=dev, device_id_type=MESH_ID)

        def landed(a, k):
            dev, slot = peers[k]
            return pltpu.make_async_remote_copy(
                src_ref=outs[a].at[slot], dst_ref=outs[a].at[slot],
                send_sem=send_sems.at[a, k], recv_sem=recv_sems.at[a, k],
                device_id=dev, device_id_type=MESH_ID)

        own = [pltpu.make_async_copy(ins[a].at[me] if scatters[a] else ins[a], outs[a].at[me], local_sems.at[a])
               for a in range(n)]
        copies = [remote(a, k) for k in range(N_DEV - 1) for a in range(n)]
        for cp in own + copies:
            cp.start()
        for k in range(N_DEV - 1):
            for a in range(n):
                landed(a, k).wait_recv()
        for cp in copies:
            cp.wait_send()
        for cp in own:
            cp.wait()

    out_shape = [_sds(((N_DEV,) + a.shape[-2:]), a.dtype) for a in arrays]
    return pl.pallas_call(
        body, name=name, in_specs=[ANY] * n, out_specs=[ANY] * n, out_shape=out_shape,
        scratch_shapes=[pltpu.SemaphoreType.DMA((n, N_DEV - 1)), pltpu.SemaphoreType.DMA((n, N_DEV - 1)),
                        pltpu.SemaphoreType.DMA((n,))],
    )(*arrays)


def _gather_two_level(shard, *, name):
    def body(x_ref, out_ref, send_sems, recv_sems, local_sem):
        x, y, c = lax.axis_index("x"), lax.axis_index("y"), lax.axis_index("c")
        me, sibling = (x, y, c), (x, y, 1 - c)
        chips = [(1 - x, y), (x, 1 - y), (1 - x, 1 - y)]

        def slot(px, py, pc):
            return out_ref.at[4 * px + 2 * py + pc]

        def copy(k, block, to, src=None):
            return pltpu.make_async_remote_copy(
                src_ref=slot(*block) if src is None else src, dst_ref=slot(*block),
                send_sem=send_sems.at[k], recv_sem=recv_sems.at[k], device_id=to, device_id_type=MESH_ID)

        mine = pltpu.make_async_copy(x_ref, slot(*me), local_sem)
        mine.start()
        first = [copy(0, me, sibling, src=x_ref)]
        first += [copy(1 + j, me, (*chip, c), src=x_ref) for j, chip in enumerate(chips)]
        for cp in first:
            cp.start()
        passed = [copy(4 + j, (*chip, c), sibling) for j, chip in enumerate(chips)]
        for j, chip in enumerate(chips):
            copy(1 + j, (*chip, c), me).wait_recv()
            passed[j].start()
        copy(0, sibling, me).wait_recv()
        for j, chip in enumerate(chips):
            copy(4 + j, (*chip, 1 - c), me).wait_recv()
        for cp in first + passed:
            cp.wait_send()
        mine.wait()

    return pl.pallas_call(
        body, name=name, in_specs=[ANY], out_specs=ANY, out_shape=_sds((N_DEV,) + shard.shape, shard.dtype),
        scratch_shapes=[pltpu.SemaphoreType.DMA((N_DEV - 1,)), pltpu.SemaphoreType.DMA((N_DEV - 1,)),
                        pltpu.SemaphoreType.DMA],
    )(shard)


N_CHIPS = N_DEV // 2


def _peers(chips_only=False):
    x, y, c = lax.axis_index("x"), lax.axis_index("y"), lax.axis_index("c")
    out = []
    if chips_only:
        for k in range(1, N_CHIPS):
            px = 1 - x if k & 2 else x
            py = 1 - y if k & 1 else y
            out.append(((px, py, c), 2 * px + py))
        return 2 * x + y, out
    for k in range(1, N_DEV):
        px = 1 - x if k & 4 else x
        py = 1 - y if k & 2 else y
        pc = 1 - c if k & 1 else c
        out.append(((px, py, pc), 4 * px + 2 * py + pc))
    return 4 * x + 2 * y + c, out


def _sibling_swap(slot_arrays, *, name):
    n = len(slot_arrays)

    def body(*refs):
        ins, outs, send_sems, recv_sems = refs[:n], refs[n:2 * n], refs[2 * n], refs[2 * n + 1]
        x, y, c = lax.axis_index("x"), lax.axis_index("y"), lax.axis_index("c")
        copies = [pltpu.make_async_remote_copy(
            src_ref=ins[a].at[2 * q + (1 - c)], dst_ref=outs[a].at[q], send_sem=send_sems.at[a, q],
            recv_sem=recv_sems.at[a, q], device_id=(x, y, 1 - c), device_id_type=MESH_ID)
            for a in range(n) for q in range(N_CHIPS)]
        for cp in copies:
            cp.start()
        for cp in copies:
            cp.wait_recv()
        for cp in copies:
            cp.wait_send()

    return pl.pallas_call(
        body, name=name, in_specs=[ANY] * n, out_specs=[ANY] * n,
        out_shape=[_sds((N_CHIPS,) + t.shape[1:], t.dtype) for t in slot_arrays],
        scratch_shapes=[pltpu.SemaphoreType.DMA((n, N_CHIPS)), pltpu.SemaphoreType.DMA((n, N_CHIPS))],
    )(*slot_arrays)


def _pair_sum(slots, from_sibling, *, name, tn):
    _, r, c = slots.shape
    core = lax.axis_index("c").astype(jnp.int32).reshape(1)

    def body(core_ref, a_ref, b_ref, o_ref):
        o_ref[...] = (a_ref[...].astype(F32) + b_ref[...].astype(F32)).astype(o_ref.dtype)

    blk = lambda f: pl.BlockSpec((1, r, tn), f)
    return pl.pallas_call(
        body, name=name,
        grid_spec=pltpu.PrefetchScalarGridSpec(
            num_scalar_prefetch=1, grid=(N_CHIPS, c // tn),
            in_specs=[blk(lambda q, j, core: (2 * q + core[0], 0, j)), blk(lambda q, j, core: (q, 0, j))],
            out_specs=blk(lambda q, j, core: (q, 0, j))),
        out_shape=_sds((N_CHIPS, r, c), slots.dtype),
        compiler_params=_params("parallel", "parallel"),
    )(core, slots, from_sibling)


HBM = pl.BlockSpec(memory_space=pltpu.HBM)
SEM = pl.BlockSpec(memory_space=pltpu.SEMAPHORE)
DATAFLOW = pltpu.SideEffectType.DATAFLOW_SIDE_EFFECTING


def _split_copy(srcs, lands, send_sems, recv_sems, scatter, a, k, me, peers, incoming=False):
    dev, slot = peers[k]
    if incoming:
        src = dst = lands[a].at[slot]
    else:
        src, dst = (srcs[a].at[slot] if scatter else srcs[a]), lands[a].at[me]
    sem = a * len(peers) + k
    return pltpu.make_async_remote_copy(
        src_ref=src, dst_ref=dst, send_sem=send_sems.at[sem], recv_sem=recv_sems.at[sem],
        device_id=dev, device_id_type=MESH_ID)


def _own_copy(srcs, lands, own_sems, scatter, a, me):
    return pltpu.make_async_copy(srcs[a].at[me] if scatter else srcs[a], lands[a].at[me], own_sems.at[a])


def _exchange_start(arrays, scatter, *, name, chips_only=False, after=None):
    n = len(arrays)
    n_slots = N_CHIPS if chips_only else N_DEV
    n_in = 2 * n + len(_also(after))

    def body(*refs):
        srcs, lands = refs[:n], refs[n:2 * n]
        send_sems, recv_sems, own_sems = refs[n_in:n_in + 3]
        token = refs[-1]
        me, peers = _peers(chips_only)
        for k in range(len(peers)):
            for a in range(n):
                _split_copy(srcs, lands, send_sems, recv_sems, scatter, a, k, me, peers).start()
        for a in range(n):
            _own_copy(srcs, lands, own_sems, scatter, a, me).start()
        token[...] = jnp.zeros_like(token)

    land_shapes = [((n_slots,) + a.shape[-2:], a.dtype) for a in arrays]
    sems = pltpu.SemaphoreType.DMA((n * (n_slots - 1),))
    outs = pl.pallas_call(
        body, name=name,
        out_shape=(sems, sems, pltpu.SemaphoreType.DMA((n,)), *[pltpu.HBM(a.shape, a.dtype) for a in arrays],
                   *[pltpu.HBM(s, d) for s, d in land_shapes], _sds((SUBLANE, LANE), F32)),
        in_specs=[HBM] * (2 * n) + [ANY] * len(_also(after)),
        out_specs=(SEM, SEM, SEM, *[HBM] * (2 * n), pl.BlockSpec(memory_space=pltpu.VMEM)),
        input_output_aliases={i: 3 + i for i in range(2 * n)},
        compiler_params=pltpu.CompilerParams(has_side_effects=DATAFLOW),
    )(*[pltpu.with_memory_space_constraint(a, pltpu.HBM) for a in arrays],
      *[pltpu.with_memory_space_constraint(lax.empty(s, d), pltpu.HBM) for s, d in land_shapes], *_also(after))
    return (outs[:3], outs[3:3 + n], outs[3 + n:3 + 2 * n], scatter, chips_only), outs[-1]


def _exchange_wait(handles, after, *, name):
    sems, srcs, lands, scatter, chips_only = handles
    n = len(srcs)

    def body(*refs):
        src_refs, land_refs = refs[:n], refs[n:2 * n]
        send_ref, recv_ref, own_ref = refs[2 * n:2 * n + 3]
        me, peers = _peers(chips_only)
        for k in range(len(peers)):
            for a in range(n):
                _split_copy(src_refs, land_refs, send_ref, recv_ref, scatter, a, k, me, peers).wait_send()
                _split_copy(src_refs, land_refs, send_ref, recv_ref, scatter, a, k, me, peers, True).wait_recv()
        for a in range(n):
            _own_copy(src_refs, land_refs, own_ref, scatter, a, me).wait()

    outs = pl.pallas_call(
        body, name=name,
        out_shape=tuple(pltpu.HBM(t.shape, t.dtype) for t in (*srcs, *lands)),
        in_specs=[HBM] * (2 * n) + [SEM, SEM, SEM, pl.BlockSpec(memory_space=pl.ANY)],
        out_specs=tuple([HBM] * (2 * n)),
        input_output_aliases={i: i for i in range(2 * n)},
        compiler_params=pltpu.CompilerParams(has_side_effects=DATAFLOW),
    )(*srcs, *lands, *sems, after)
    return outs[n:]


def _adamw(parts, w, m, v, *, name, tm):
    r, c = w.shape
    assert r % tm == 0

    def body(p_ref, w_ref, m_ref, v_ref, g_ref, d_ref, nm_ref, nv_ref):
        _adamw_update(p_ref, w_ref, m_ref, v_ref, g_ref, d_ref, nm_ref, nv_ref)

    blk = pl.BlockSpec((tm, c), lambda i: (i, 0))
    return pl.pallas_call(
        body, name=name, grid=(r // tm,),
        in_specs=[pl.BlockSpec((parts.shape[0], tm, c), lambda i: (0, i, 0)), blk, blk, blk],
        out_specs=[blk] * 4, out_shape=[_sds((r, c), F32)] * 4,
        compiler_params=_params("parallel"),
    )(parts, w, m, v)


def _adamw_update(p_ref, w_ref, m_ref, v_ref, g_ref, d_ref, nm_ref, nv_ref):
    g = p_ref[0].astype(F32)
    for s in range(1, p_ref.shape[0]):
        g = g + p_ref[s].astype(F32)
    g_ref[...] = g
    m_new = ADAM_B1 * m_ref[...] + (1.0 - ADAM_B1) * g
    v_new = ADAM_B2 * v_ref[...] + (1.0 - ADAM_B2) * (g * g)
    nm_ref[...] = m_new
    nv_ref[...] = v_new
    m_hat = m_new / (1.0 - ADAM_B1 ** ADAM_STEP)
    v_hat = v_new / (1.0 - ADAM_B2 ** ADAM_STEP)
    d_ref[...] = -ADAM_LR * (m_hat / (jnp.sqrt(v_hat) + ADAM_EPS) + ADAM_WD * w_ref[...])


SMALL = ("g_pre_mix", "b_forget", "g_post_mix", "g_pre_ffn", "conv_b", "g_post_ffn")


def _adamw_small(parts, ws, ms, vs, sq_err_parts):
    n = len(ws)

    def body(*refs):
        ins, sq_ref, outs, loss_ref = refs[:4 * n], refs[4 * n], refs[4 * n + 1:-1], refs[-1]
        for i in range(n):
            _adamw_update(ins[i], ins[n + i], ins[2 * n + i], ins[3 * n + i], *outs[4 * i:4 * i + 4])
        total = sq_ref[0]
        for s in range(1, N_DEV):
            total = total + sq_ref[s]
        loss_ref[...] = total * (0.5 / D_MODEL)

    res = pl.pallas_call(
        body, name="adamw_small",
        out_shape=[_sds(w.shape, F32) for w in ws for _ in range(4)] + [_sds((1, LANE), F32)],
        compiler_params=pltpu.CompilerParams(vmem_limit_bytes=VMEM_LIMIT),
    )(*parts, *ws, *ms, *vs, sq_err_parts)
    return [res[4 * i:4 * i + 4] for i in range(n)], res[-1][0, 0]


def kernel(x, g_pre_mix, w_in, b_forget, w_o_fox, w_o_dil, w_out, g_post_mix, g_pre_ffn, w_up, conv_w, conv_b, w_down, g_post_ffn, loss_target, m_g_pre_mix, m_w_in, m_b_forget, m_w_o_fox, m_w_o_dil, m_w_out, m_g_post_mix, m_g_pre_ffn, m_w_up, m_conv_w, m_conv_b, m_w_down, m_g_post_ffn, v_g_pre_mix, v_w_in, v_b_forget, v_w_o_fox, v_w_o_dil, v_w_out, v_g_post_mix, v_g_pre_ffn, v_w_up, v_conv_w, v_conv_b, v_w_down, v_g_post_ffn):
    names = ("g_pre_mix", "w_in", "b_forget", "w_o_fox", "w_o_dil", "w_out", "g_post_mix", "g_pre_ffn",
             "w_up", "conv_w", "conv_b", "w_down", "g_post_ffn")
    w = dict(g_pre_mix=g_pre_mix, w_in=w_in, b_forget=b_forget, w_o_fox=w_o_fox, w_o_dil=w_o_dil, w_out=w_out,
             g_post_mix=g_post_mix, g_pre_ffn=g_pre_ffn, w_up=w_up, conv_w=conv_w, conv_b=conv_b, w_down=w_down,
             g_post_ffn=g_post_ffn)
    m = dict(g_pre_mix=m_g_pre_mix, w_in=m_w_in, b_forget=m_b_forget, w_o_fox=m_w_o_fox, w_o_dil=m_w_o_dil,
             w_out=m_w_out, g_post_mix=m_g_post_mix, g_pre_ffn=m_g_pre_ffn, w_up=m_w_up, conv_w=m_conv_w,
             conv_b=m_conv_b, w_down=m_w_down, g_post_ffn=m_g_post_ffn)
    v = dict(g_pre_mix=v_g_pre_mix, w_in=v_w_in, b_forget=v_b_forget, w_o_fox=v_w_o_fox, w_o_dil=v_w_o_dil,
             w_out=v_w_out, g_post_mix=v_g_post_mix, g_pre_ffn=v_g_pre_ffn, w_up=v_w_up, conv_w=v_conv_w,
             conv_b=v_conv_b, w_down=v_w_down, g_post_ffn=v_g_post_ffn)
    sharded = ("w_in", "w_o_fox", "w_o_dil", "w_out", "w_up", "w_down", "conv_w")
    wire = lambda n: F32 if n == "conv_w" else BF16

    by_cols = lambda t: jnp.transpose(t, (1, 0, 2)).reshape(t.shape[1], N_DEV * t.shape[2])
    by_rows = lambda t: t.reshape(N_DEV * t.shape[1], t.shape[2])
    col_slots = lambda t: jnp.transpose(t.reshape(t.shape[0], N_DEV, t.shape[1] // N_DEV), (1, 0, 2))
    row_slots = lambda t: t.reshape(N_DEV, t.shape[0] // N_DEV, t.shape[1])
    to_slots = lambda n, t: (row_slots if n in ("w_out", "w_down") else col_slots)(t).astype(wire(n))
    shard = lambda n: w[n][0].astype(wire(n))

    w_main, w_f = _w_in_from_shards(_gather_two_level(shard("w_in"), name="gather_w_in"))
    proj_handles, proj_tok = _exchange_start(
        [shard("w_o_fox"), shard("w_o_dil"), shard("w_out")], False, name="gather_proj_start", after=w_f)
    ffn_handles, ffn_tok = _exchange_start(
        [shard("w_up"), shard("conv_w"), shard("w_down")], False, name="gather_ffn_start", after=proj_tok)

    def proj_weights(after):
        w_oa, w_ob, w_o = _exchange_wait(proj_handles, after, name="gather_proj_wait")
        return by_cols(w_oa), by_cols(w_ob), by_rows(w_o)

    def ffn_weights(after):
        w_u, conv, w_d = _exchange_wait(ffn_handles, after, name="gather_ffn_wait")
        return _w_up_from_shards(w_u), _ffn_interleave(by_cols(conv)), by_rows(w_d)

    pending = {}

    def ffn_grads_ready(g):
        slots = [to_slots("w_down", g["w_down"]), _w_up_to_shards(g["w_up_blocks"]), to_slots("conv_w", g["conv_w"])]
        pending["ffn"] = _exchange_start(slots, True, name="scatter_ffn_start")
        return pending["ffn"][1]

    def proj_grads_ready(g):
        pending["proj"] = _exchange_start([to_slots(n, g[n]) for n in ("w_o_fox", "w_o_dil", "w_out")], True,
                                          name="scatter_proj_start")
        return pending["proj"][1]

    def mixer_grads_ready(g):
        slots = _w_in_to_shards(g["w_main"], g["w_f"])
        theirs = _sibling_swap([slots], name="scatter_w_in_swap")[0]
        chip_sums = _pair_sum(slots, theirs, name="scatter_w_in_pair_sum", tn=W_IN_SHARD)
        pending["w_in"] = _exchange_start([chip_sums], True, name="scatter_w_in_start", chips_only=True)
        return pending["w_in"][1]

    sq_err, grad_x, g = _local_step(
        x[0], loss_target[0], w_main, w_f, b_forget, conv_b, g_pre_mix, g_post_mix, g_pre_ffn,
        g_post_ffn, proj_weights, ffn_weights, ffn_grads_ready, proj_grads_ready, mixer_grads_ready, after=ffn_tok)

    tiles = dict(w_in=256, w_o_fox=512, w_o_dil=512, w_out=128, w_up=256, w_down=176, conv_w=3)
    adam = lambda n, p: _adamw(p, w[n][0], m[n][0], v[n][0], name=f"adamw_{n}", tm=tiles[n])
    res = {}
    for key, group in (("ffn", ("w_down", "w_up", "conv_w")), ("proj", ("w_o_fox", "w_o_dil", "w_out"))):
        landed = _exchange_wait(pending[key][0], grad_x, name=f"scatter_{key}_wait")
        res.update({n: adam(n, p) for n, p in zip(group, landed)})
    done = res["w_up"][3]
    res["w_in"] = adam("w_in", _exchange_wait(pending["w_in"][0], done, name="scatter_w_in_wait")[0])
    small_parts = _exchange([g[n] for n in SMALL] + [sq_err], False, name="gather_small_grads")
    small, loss = _adamw_small(small_parts[:-1], *[[t[n] for n in SMALL] for t in (w, m, v)], small_parts[-1])
    small = dict(zip(SMALL, small))
    out = [[(res[n][k][None] if n in sharded else small[n][k]) for n in names] for k in range(4)]
    return (loss, grad_x[None], *out[0], *out[1], *out[2], *out[3])
```

```python
import functools
import math

import jax
import jax.numpy as jnp
import numpy as np
from jax import lax
from jax.experimental import pallas as pl
from jax.experimental.pallas import tpu as pltpu

F32 = jnp.float32
BF16 = jnp.bfloat16

SEQ = 4096
D_MODEL = 1024
N_HEADS = 8
HEAD_DIM = 64
ATT_W = N_HEADS * HEAD_DIM
D_FF = 2816
Z_MAIN = 5120
F_PAD = 128
ROPE_DIM = 16
ROPE_THETA = 500000.0
RMS_EPS = 1e-6
NEG_INF = -1e30
SCALE = 1.0 / math.sqrt(HEAD_DIM)
DIL_PATTERNS = ((128, 1), (512, 4), (2048, 16))
DIL_BLK = 128
DIL_STEP_BLOCKS = 2
N_DEV = 8

ADAM_LR = 0.001
ADAM_B1 = 0.9
ADAM_B2 = 0.999
ADAM_EPS = 1e-08
ADAM_WD = 0.01
ADAM_STEP = 10

LANE = 128
SUBLANE = 8
VMEM_LIMIT = 56 * 1024 * 1024
MESH_ID = pl.DeviceIdType.MESH
ANY = pl.BlockSpec(memory_space=pl.ANY)


def _params(*sem):
    return pltpu.CompilerParams(dimension_semantics=sem, vmem_limit_bytes=VMEM_LIMIT)


def _sds(shape, dtype):
    return jax.ShapeDtypeStruct(shape, dtype)


def _also(after):
    return [] if after is None else [after]


def _matmul(a, b, *, ta=False, tb=False, out_dtype, tm, tn, tk, name, b_k_off=0, after=None, col_slots=1):
    n_after = len(_also(after))
    if ta:
        kk, m = a.shape
    else:
        m, kk = a.shape
    n = b.shape[0] if tb else b.shape[1]
    tm, tn, tk = min(tm, m), min(tn, n), min(tk, kk)
    assert (b.shape[1] if tb else b.shape[0]) >= b_k_off * tk + kk
    assert m % tm == 0 and n % tn == 0 and kk % tk == 0, (name, m, n, kk, tm, tn, tk)
    nk = kk // tk
    dims = (((0 if ta else 1,), (1 if tb else 0,)), ((), ()))
    slot_w = n // col_slots
    assert col_slots == 1 or (nk == 1 and tn == n and slot_w % LANE == 0), name

    def body(a_ref, b_ref, *rest):
        o_ref, scratch = rest[n_after], rest[n_after + 1:]
        p = lax.dot_general(a_ref[...].astype(BF16), b_ref[...].astype(BF16), dims,
                            preferred_element_type=F32)
        if col_slots > 1:
            for s in range(col_slots):
                o_ref[s] = p[:, s * slot_w:(s + 1) * slot_w].astype(o_ref.dtype)
        elif nk == 1:
            o_ref[...] = p.astype(o_ref.dtype)
        else:
            acc = scratch[0]
            k = pl.program_id(2)

            @pl.when(k == 0)
            def _():
                acc[...] = p

            @pl.when(k > 0)
            def _():
                acc[...] += p

            @pl.when(k == nk - 1)
            def _():
                o_ref[...] = acc[...].astype(o_ref.dtype)

    a_spec = (pl.BlockSpec((tk, tm), lambda i, j, k: (k, i)) if ta
              else pl.BlockSpec((tm, tk), lambda i, j, k: (i, k)))
    b_spec = (pl.BlockSpec((tn, tk), lambda i, j, k: (j, k + b_k_off)) if tb
              else pl.BlockSpec((tk, tn), lambda i, j, k: (k + b_k_off, j)))
    return pl.pallas_call(
        body, name=name, grid=(m // tm, n // tn, nk),
        in_specs=[a_spec, b_spec] + [ANY] * n_after,
        out_specs=(pl.BlockSpec((tm, tn), lambda i, j, k: (i, j)) if col_slots == 1
                   else pl.BlockSpec((col_slots, tm, slot_w), lambda i, j, k: (0, i, 0))),
        out_shape=_sds((m, n) if col_slots == 1 else (col_slots, m, slot_w), out_dtype),
        scratch_shapes=[pltpu.VMEM((tm, tn), F32)] if nk > 1 else [],
        compiler_params=_params("parallel", "parallel", "arbitrary"),
    )(a, b, *_also(after))


def _rms_fwd(x, g, *, name, tm=512, after=None):
    def body(x_ref, g_ref, *rest):
        h_ref = rest[-1]
        xv = x_ref[...]
        r = lax.rsqrt(jnp.mean(xv * xv, axis=-1, keepdims=True) + RMS_EPS)
        h_ref[...] = (xv * r * g_ref[...]).astype(h_ref.dtype)

    return pl.pallas_call(
        body, name=name, grid=(SEQ // tm,),
        in_specs=[pl.BlockSpec((tm, D_MODEL), lambda i: (i, 0)), pl.BlockSpec((1, D_MODEL), lambda i: (0, 0))]
        + [ANY] * len(_also(after)),
        out_specs=pl.BlockSpec((tm, D_MODEL), lambda i: (i, 0)),
        out_shape=_sds((SEQ, D_MODEL), BF16),
        compiler_params=_params("parallel"),
    )(x, g, *_also(after))


def _rms_bwd(dh_parts, xin, g, dres, *, out_dtype, name, tm=512):
    n_parts = len(dh_parts)
    has_res = dres is not None

    def body(*refs):
        parts = refs[:n_parts]
        x_ref, g_ref = refs[n_parts], refs[n_parts + 1]
        res_ref = refs[n_parts + 2] if has_res else None
        o_ref, gg_ref = refs[-2], refs[-1]
        dh = parts[0][...].astype(F32)
        for p in parts[1:]:
            dh = dh + p[...].astype(F32)
        xv = x_ref[...]
        r = lax.rsqrt(jnp.mean(xv * xv, axis=-1, keepdims=True) + RMS_EPS)
        xn = xv * r

        @pl.when(pl.program_id(0) == 0)
        def _():
            gg_ref[...] = jnp.zeros_like(gg_ref)

        gg_ref[...] += jnp.sum(dh * xn, axis=0, keepdims=True)
        dxn = dh * g_ref[...]
        dx = r * (dxn - xn * jnp.mean(dxn * xn, axis=-1, keepdims=True))
        if has_res:
            dx = dx + res_ref[...]
        o_ref[...] = dx.astype(o_ref.dtype)

    row = pl.BlockSpec((tm, D_MODEL), lambda i: (i, 0))
    vec = pl.BlockSpec((1, D_MODEL), lambda i: (0, 0))
    args = list(dh_parts) + [xin, g] + ([dres] if has_res else [])
    return pl.pallas_call(
        body, name=name, grid=(SEQ // tm,),
        in_specs=[row] * n_parts + [row, vec] + ([row] if has_res else []),
        out_specs=[row, vec],
        out_shape=[_sds((SEQ, D_MODEL), out_dtype), _sds((1, D_MODEL), F32)],
        compiler_params=_params("arbitrary"),
    )(*args)


def _rms_pair_bwd(dh_parts, x2, g_pre, dres, y1, g_post, *, tm=512, after=None):
    n_parts = len(dh_parts)

    def norm_bwd(dh, xin, g_ref, gg_ref):
        r = lax.rsqrt(jnp.mean(xin * xin, axis=-1, keepdims=True) + RMS_EPS)
        xn = xin * r
        gg_ref[...] += jnp.sum(dh * xn, axis=0, keepdims=True)
        dxn = dh * g_ref[...]
        return r * (dxn - xn * jnp.mean(dxn * xn, axis=-1, keepdims=True))

    def body(*refs):
        parts = refs[:n_parts]
        x2_ref, gpre_ref, res_ref, y1_ref, gpost_ref = refs[n_parts:n_parts + 5]
        dx2_ref, dy1_ref, ggpre_ref, ggpost_ref = refs[-4:]

        @pl.when(pl.program_id(0) == 0)
        def _():
            ggpre_ref[...] = jnp.zeros_like(ggpre_ref)
            ggpost_ref[...] = jnp.zeros_like(ggpost_ref)

        dh = parts[0][...].astype(F32)
        for p in parts[1:]:
            dh = dh + p[...].astype(F32)
        dx2 = res_ref[...] + norm_bwd(dh, x2_ref[...], gpre_ref, ggpre_ref)
        dx2_ref[...] = dx2
        dy1_ref[...] = norm_bwd(dx2, y1_ref[...], gpost_ref, ggpost_ref).astype(dy1_ref.dtype)

    row = pl.BlockSpec((tm, D_MODEL), lambda i: (i, 0))
    vec = pl.BlockSpec((1, D_MODEL), lambda i: (0, 0))
    return pl.pallas_call(
        body, name="rms_pair_bwd", grid=(SEQ // tm,),
        in_specs=[row] * n_parts + [row, vec, row, row, vec] + [ANY] * len(_also(after)),
        out_specs=[row, row, vec, vec],
        out_shape=[_sds((SEQ, D_MODEL), F32), _sds((SEQ, D_MODEL), BF16), _sds((1, D_MODEL), F32),
                   _sds((1, D_MODEL), F32)],
        compiler_params=_params("arbitrary"),
    )(*dh_parts, x2, g_pre, dres, y1, g_post, *_also(after))


SCAN_BLK = 512


def _split_dot(v, tri):
    hi = v.astype(BF16)
    r1 = v - hi.astype(F32)
    mid = r1.astype(BF16)
    lo = (r1 - mid.astype(F32)).astype(BF16)
    dot = functools.partial(jnp.dot, preferred_element_type=F32)
    return dot(hi, tri) + dot(mid, tri) + dot(lo, tri)


def _fox_prep(fa_t, b_col):
    nblk = SEQ // SCAN_BLK

    def body(fa_ref, b_ref, f_ref, sg_ref):
        row = lax.broadcasted_iota(jnp.int32, (SCAN_BLK, SCAN_BLK), 0)
        col = lax.broadcasted_iota(jnp.int32, (SCAN_BLK, SCAN_BLK), 1)
        upper = (row <= col).astype(BF16)
        carry = jnp.zeros((N_HEADS, 1), F32)
        for blk in range(nblk):
            sl = pl.ds(blk * SCAN_BLK, SCAN_BLK)
            xx = fa_ref[:, sl] + b_ref[...]
            e = jnp.exp(-jnp.abs(xx))
            logf = jnp.minimum(xx, 0.0) - jnp.log(1.0 + e)
            sg_ref[:, sl] = jnp.where(xx >= 0.0, e, 1.0) / (1.0 + e)
            c = _split_dot(logf, upper) + carry
            f_ref[:, sl] = c
            carry = c[:, SCAN_BLK - 1:SCAN_BLK]

    return pl.pallas_call(
        body, name="fox_prep",
        out_shape=[_sds((N_HEADS, SEQ), F32), _sds((N_HEADS, SEQ), F32)],
        compiler_params=pltpu.CompilerParams(vmem_limit_bytes=VMEM_LIMIT),
    )(fa_t, b_col)


def _fox_post_bwd(df_t, sg_t):
    nblk = SEQ // SCAN_BLK

    def body(df_ref, sg_ref, dfa_ref, gb_ref):
        row = lax.broadcasted_iota(jnp.int32, (SCAN_BLK, SCAN_BLK), 0)
        col = lax.broadcasted_iota(jnp.int32, (SCAN_BLK, SCAN_BLK), 1)
        lower = (row >= col).astype(BF16)
        carry = jnp.zeros((N_HEADS, 1), F32)
        gb = jnp.zeros((N_HEADS, 1), F32)
        for blk in reversed(range(nblk)):
            sl = pl.ds(blk * SCAN_BLK, SCAN_BLK)
            c = _split_dot(df_ref[:, sl], lower) + carry
            carry = c[:, 0:1]
            dfa = c * sg_ref[:, sl]
            dfa_ref[:, sl] = dfa
            gb = gb + jnp.sum(dfa, axis=1, keepdims=True)
        gb_ref[...] = gb

    return pl.pallas_call(
        body, name="fox_post_bwd",
        out_shape=[_sds((N_HEADS, SEQ), F32), _sds((N_HEADS, 1), F32)],
        compiler_params=pltpu.CompilerParams(vmem_limit_bytes=VMEM_LIMIT),
    )(df_t, sg_t)


FOX_T = 512
NT_DIMS = (((1,), (1,)), ((), ()))
TN_DIMS = (((0,), (0,)), ((), ()))


def _head(ref_or_val, h):
    return ref_or_val[:, h * HEAD_DIM:(h + 1) * HEAD_DIM]


def _split3(v):
    hi = v.astype(BF16).astype(F32)
    r1 = v - hi
    mid = r1.astype(BF16).astype(F32)
    return hi, mid, (r1 - mid).astype(BF16).astype(F32)


ONE_LANE = 3 * N_HEADS


def _pack_terms(v, with_one):
    hi, mid, lo = _split3(v)
    t = hi + pltpu.roll(mid, N_HEADS, 1) + pltpu.roll(lo, 2 * N_HEADS, 1)
    if with_one:
        t = t + (lax.broadcasted_iota(jnp.int32, v.shape, 1) == ONE_LANE).astype(F32)
    return t.astype(BF16)


def _aux_matrices():
    to_q = np.zeros((LANE, N_HEADS * 2 * HEAD_DIM), np.float32)
    to_k = np.zeros_like(to_q)
    for h in range(N_HEADS):
        base = h * 2 * HEAD_DIM + HEAD_DIM
        for s in range(3):
            to_q[s * N_HEADS + h, base + s] = 1.0
            to_q[ONE_LANE, base + 3 + s] = 1.0
            to_k[ONE_LANE, base + s] = 1.0
            to_k[s * N_HEADS + h, base + 3 + s] = -1.0
    return jnp.asarray(to_q, BF16), jnp.asarray(to_k, BF16)


def _head_sums():
    total = np.zeros((N_HEADS * HEAD_DIM, LANE), np.float32)
    first = np.zeros_like(total)
    for h in range(N_HEADS):
        total[h * HEAD_DIM:(h + 1) * HEAD_DIM, h] = 1.0
        first[h * HEAD_DIM, h] = 1.0
    return jnp.asarray(total, BF16), jnp.asarray(first, BF16)


SLOT = 2 * HEAD_DIM
N_SPLIT = 3
FOX_FWD_HEADS = 8
FOX_BWD_HEADS = 4


def _slot(ref, h):
    return ref[:, h * SLOT:(h + 1) * SLOT]


def _fox_pack_fwd(zm, f_cols, *, tm=512):
    def body(q_ref, k_ref, v_ref, f_ref, tq_ref, tk_ref, qs_ref, ks_ref, vs_ref):
        ones = jnp.ones((tm, HEAD_DIM), BF16)
        terms = _pack_terms(f_ref[...], True)
        q_aux = jnp.dot(terms, tq_ref[...], preferred_element_type=F32).astype(BF16)
        k_aux = jnp.dot(terms, tk_ref[...], preferred_element_type=F32).astype(BF16)
        for h in range(N_HEADS):
            aux = slice(h * SLOT + HEAD_DIM, (h + 1) * SLOT)
            qs_ref[:, h * SLOT:(h + 1) * SLOT] = jnp.concatenate(
                [(_head(q_ref, h).astype(F32) * SCALE).astype(BF16), q_aux[:, aux]], axis=1)
            ks_ref[:, h * SLOT:(h + 1) * SLOT] = jnp.concatenate([_head(k_ref, h), k_aux[:, aux]], axis=1)
            vs_ref[:, h * SLOT:(h + 1) * SLOT] = jnp.concatenate([_head(v_ref, h), ones], axis=1)

    col = lambda b: pl.BlockSpec((tm, ATT_W), lambda i: (i, b))
    wide = pl.BlockSpec((tm, N_HEADS * SLOT), lambda i: (i, 0))
    const = pl.BlockSpec((LANE, N_HEADS * SLOT), lambda i: (0, 0))
    return pl.pallas_call(
        body, name="fox_pack_fwd", grid=(SEQ // tm,),
        in_specs=[col(0), col(1), col(2), pl.BlockSpec((tm, LANE), lambda i: (i, 0)), const, const],
        out_specs=[wide] * 3, out_shape=[_sds((SEQ, N_HEADS * SLOT), BF16)] * 3,
        compiler_params=_params("parallel"),
    )(zm, zm, zm, f_cols, *_aux_matrices())


def _fox_pack_bwd(zm, f_cols, lse, o, do, *, tm=512, after=None):
    def body(q_ref, f_ref, lse_ref, o_ref, do_ref, tq_ref, total_ref, first_ref, *rest):
        qs_ref, ds_ref = rest[-2:]
        delta = _split_dot(o_ref[...].astype(F32) * do_ref[...].astype(F32), total_ref[...])
        lse_h = _split_dot(lse_ref[...], first_ref[...])
        q_aux = jnp.dot(_pack_terms(f_ref[...] - lse_h, True), tq_ref[...], preferred_element_type=F32).astype(BF16)
        d_aux = jnp.dot(_pack_terms(-delta, False), tq_ref[...], preferred_element_type=F32).astype(BF16)
        for h in range(N_HEADS):
            aux = slice(h * SLOT + HEAD_DIM, (h + 1) * SLOT)
            qs_ref[:, h * SLOT:(h + 1) * SLOT] = jnp.concatenate(
                [(_head(q_ref, h).astype(F32) * SCALE).astype(BF16), q_aux[:, aux]], axis=1)
            ds_ref[:, h * SLOT:(h + 1) * SLOT] = jnp.concatenate([_head(do_ref, h), d_aux[:, aux]], axis=1)

    row = pl.BlockSpec((tm, ATT_W), lambda i: (i, 0))
    wide = pl.BlockSpec((tm, N_HEADS * SLOT), lambda i: (i, 0))
    const = lambda r, c: pl.BlockSpec((r, c), lambda i: (0, 0))
    return pl.pallas_call(
        body, name="fox_pack_bwd", grid=(SEQ // tm,),
        in_specs=[row, pl.BlockSpec((tm, LANE), lambda i: (i, 0)), row, row, row,
                  const(LANE, N_HEADS * SLOT), const(ATT_W, LANE), const(ATT_W, LANE)] + [ANY] * len(_also(after)),
        out_specs=[wide] * 2, out_shape=[_sds((SEQ, N_HEADS * SLOT), BF16)] * 2,
        compiler_params=_params("parallel"),
    )(zm, f_cols, lse, o, do, _aux_matrices()[0], *_head_sums(), *_also(after))


def _causal_pairs(key_major):
    nb = SEQ // FOX_T
    if key_major:
        pairs = [(i, j) for j in range(nb) for i in range(j, nb)]
    else:
        pairs = [(i, j) for i in range(nb) for j in range(i + 1)]
    return (jnp.array([p[0] for p in pairs], jnp.int32), jnp.array([p[1] for p in pairs], jnp.int32), len(pairs))


FOX_HALF = FOX_T // 2
FOX_FULL = ((slice(0, FOX_T), slice(0, FOX_T), None),)
FOX_DIAG = ((slice(0, FOX_HALF), slice(0, FOX_HALF), 0), (slice(FOX_HALF, FOX_T), slice(0, FOX_T), FOX_HALF))


def _causal_piece_mask(q_rows, k_rows, offset):
    shape = (q_rows.stop - q_rows.start, k_rows.stop - k_rows.start)
    row = lax.broadcasted_iota(jnp.int32, shape, 0)
    col = lax.broadcasted_iota(jnp.int32, shape, 1)
    return col <= row + offset


def _fox_fwd(q_slots, k_slots, v_slots):
    i_tab, j_tab, n_pairs = _causal_pairs(False)

    def body(i_tab, j_tab, q_ref, k_ref, v_ref, o_ref, lse_ref, m_s, acc_s):
        t = pl.program_id(1)
        i, j = i_tab[t], j_tab[t]

        @pl.when(j == 0)
        def _():
            m_s[...] = jnp.full_like(m_s, NEG_INF)
            acc_s[...] = jnp.zeros_like(acc_s)

        def step(pieces):
            jobs = [(h, piece) for h in range(FOX_FWD_HEADS) for piece in pieces]
            lanes = lambda h: slice(h * SLOT, (h + 1) * SLOT)
            scores = [lax.dot_general(q_ref[qr, lanes(h)], k_ref[kr, lanes(h)], NT_DIMS, preferred_element_type=F32)
                      for h, (qr, kr, _) in jobs]
            probs, alphas = [], []
            for idx, (h, (qr, kr, offset)) in enumerate(jobs):
                s = scores[idx]
                if offset is not None:
                    s = jnp.where(_causal_piece_mask(qr, kr, offset), s, NEG_INF)
                m_prev = m_s[h, qr, :]
                m_new = jnp.maximum(m_prev, jnp.max(s, axis=-1, keepdims=True))
                probs.append(jnp.exp(s - jnp.tile(m_new, (1, s.shape[1] // LANE))).astype(BF16))
                alphas.append(jnp.exp(m_prev - m_new))
                m_s[h, qr, :] = m_new
            for idx, (h, (qr, kr, _)) in enumerate(jobs):
                acc_s[h, qr, :] = alphas[idx] * acc_s[h, qr, :] + jnp.dot(
                    probs[idx], v_ref[kr, lanes(h)], preferred_element_type=F32)

        @pl.when(j < i)
        def _():
            step(FOX_FULL)

        @pl.when(j == i)
        def _():
            step(FOX_DIAG)
            outs, lses = [], []
            for h in range(FOX_FWD_HEADS):
                acc = acc_s[h]
                l = acc[:, HEAD_DIM:]
                outs.append(acc[:, :HEAD_DIM] / l)
                lses.append(m_s[h][:, :HEAD_DIM] + jnp.log(l))
            o_ref[...] = jnp.concatenate(outs, axis=1).astype(o_ref.dtype)
            lse_ref[...] = jnp.concatenate(lses, axis=1)

    qspec = pl.BlockSpec((FOX_T, FOX_FWD_HEADS * SLOT), lambda p, t, it, jt: (it[t], p))
    kspec = pl.BlockSpec((FOX_T, FOX_FWD_HEADS * SLOT), lambda p, t, it, jt: (jt[t], p))
    ospec = pl.BlockSpec((FOX_T, FOX_FWD_HEADS * HEAD_DIM), lambda p, t, it, jt: (it[t], p))
    return pl.pallas_call(
        body, name="fox_fwd",
        grid_spec=pltpu.PrefetchScalarGridSpec(
            num_scalar_prefetch=2, grid=(N_HEADS // FOX_FWD_HEADS, n_pairs),
            in_specs=[qspec, kspec, kspec], out_specs=[ospec, ospec],
            scratch_shapes=[pltpu.VMEM((FOX_FWD_HEADS, FOX_T, LANE), F32),
                            pltpu.VMEM((FOX_FWD_HEADS, FOX_T, SLOT), F32)]),
        out_shape=[_sds((SEQ, ATT_W), BF16), _sds((SEQ, ATT_W), F32)],
        compiler_params=_params("parallel", "arbitrary"),
    )(i_tab, j_tab, q_slots, k_slots, v_slots)


def _fox_bwd(q_slots, k_slots, v_slots, do_slots):
    i_tab, j_tab, n_pairs = _causal_pairs(True)

    def body(i_tab, j_tab, q_ref, k_ref, v_ref, do_ref, dq_ref, dk_ref, dv_ref):
        t = pl.program_id(1)
        i, j = i_tab[t], j_tab[t]

        @pl.when(t == 0)
        def _():
            dq_ref[...] = jnp.zeros_like(dq_ref)

        @pl.when(i == j)
        def _():
            dk_ref[...] = jnp.zeros_like(dk_ref)
            dv_ref[...] = jnp.zeros_like(dv_ref)

        def step(pieces):
            jobs = [(h, piece) for h in range(FOX_BWD_HEADS) for piece in pieces]
            lanes = lambda h: slice(h * SLOT, (h + 1) * SLOT)
            scores = [lax.dot_general(q_ref[qr, lanes(h)], k_ref[kr, lanes(h)], NT_DIMS, preferred_element_type=F32)
                      for h, (qr, kr, _) in jobs]
            dps = [lax.dot_general(do_ref[qr, lanes(h)], v_ref[kr, lanes(h)], NT_DIMS, preferred_element_type=F32)
                   for h, (qr, kr, _) in jobs]
            ps, dss = [], []
            for idx, (h, (qr, kr, offset)) in enumerate(jobs):
                p = jnp.exp(scores[idx])
                if offset is not None:
                    p = jnp.where(_causal_piece_mask(qr, kr, offset), p, 0.0)
                ps.append(p.astype(BF16))
                dss.append((p * dps[idx]).astype(BF16))
            for idx, (h, (qr, kr, _)) in enumerate(jobs):
                rows = pl.ds(pl.multiple_of(i * FOX_T + qr.start, FOX_HALF), qr.stop - qr.start)
                dv_ref[kr, lanes(h)] += lax.dot_general(ps[idx], do_ref[qr, lanes(h)], TN_DIMS,
                                                        preferred_element_type=F32)
                dk_ref[kr, lanes(h)] += lax.dot_general(dss[idx], q_ref[qr, lanes(h)], TN_DIMS,
                                                        preferred_element_type=F32)
                dq_ref[rows, lanes(h)] += jnp.dot(dss[idx], k_ref[kr, lanes(h)], preferred_element_type=F32)

        @pl.when(i > j)
        def _():
            step(FOX_FULL)

        @pl.when(i == j)
        def _():
            step(FOX_DIAG)

    qspec = pl.BlockSpec((FOX_T, FOX_BWD_HEADS * SLOT), lambda p, t, it, jt: (it[t], p))
    kspec = pl.BlockSpec((FOX_T, FOX_BWD_HEADS * SLOT), lambda p, t, it, jt: (jt[t], p))
    return pl.pallas_call(
        body, name="fox_bwd",
        grid_spec=pltpu.PrefetchScalarGridSpec(
            num_scalar_prefetch=2, grid=(N_HEADS // FOX_BWD_HEADS, n_pairs),
            in_specs=[qspec, kspec, kspec, qspec],
            out_specs=[pl.BlockSpec((SEQ, FOX_BWD_HEADS * SLOT), lambda p, t, it, jt: (0, p)), kspec, kspec]),
        out_shape=[_sds((SEQ, N_HEADS * SLOT), F32)] * 3,
        compiler_params=_params("arbitrary", "arbitrary"),
    )(i_tab, j_tab, q_slots, k_slots, v_slots, do_slots)


def _fox_unpack(dq_slots, dk_slots, dv_slots, dz, *, tm=512):
    def body(dq_ref, dk_ref, dv_ref, dz_in, o_ref, df_ref):
        lane = lax.broadcasted_iota(jnp.int32, (tm, LANE), 1)
        df = jnp.zeros((tm, LANE), F32)
        for h in range(N_HEADS):
            lo = h * SLOT
            for part, (ref, mult) in enumerate(((dq_ref, SCALE), (dk_ref, 1.0), (dv_ref, 1.0))):
                o_ref[:, part * ATT_W + h * HEAD_DIM:part * ATT_W + (h + 1) * HEAD_DIM] = (
                    ref[:, lo:lo + HEAD_DIM] * mult).astype(o_ref.dtype)
            rows = dq_ref[:, lo + HEAD_DIM:lo + HEAD_DIM + 1]
            cols = dk_ref[:, lo + HEAD_DIM + N_SPLIT:lo + HEAD_DIM + N_SPLIT + 1]
            df = jnp.where(lane == h, rows - cols, df)
        df_ref[...] = df

    wide = pl.BlockSpec((tm, N_HEADS * SLOT), lambda i: (i, 0))
    return pl.pallas_call(
        body, name="fox_unpack", grid=(SEQ // tm,), in_specs=[wide] * 3 + [ANY],
        out_specs=[pl.BlockSpec((tm, 3 * ATT_W), lambda i: (i, 0)), pl.BlockSpec((tm, LANE), lambda i: (i, 0))],
        out_shape=[_sds((SEQ, Z_MAIN), BF16), _sds((SEQ, LANE), F32)],
        input_output_aliases={3: 0},
        compiler_params=_params("parallel"),
    )(dq_slots, dk_slots, dv_slots, dz)


def _dil_bwd_prep(o, do, lse, *, tm=512):
    dilations = [d for _, d in DIL_PATTERNS]
    o_chunks = ATT_W // LANE

    def body(o_ref, do_ref, lse_ref, *rest):
        outs, (do_scr, lse_scr, dl_scr) = rest[:-3], rest[-3:]
        dov = do_ref[...].astype(F32)
        prod = o_ref[...].astype(F32) * dov
        lane = lax.broadcasted_iota(jnp.int32, (tm, LANE), 1)
        delta = jnp.zeros((tm, LANE), F32)
        for h in range(N_HEADS):
            delta = jnp.where(lane == h, jnp.sum(_head(prod, h), axis=1, keepdims=True), delta)
        for ch in range(o_chunks):
            do_scr[ch] = dov[:, ch * LANE:(ch + 1) * LANE]
        lse_scr[0] = lse_ref[...]
        dl_scr[0] = delta
        for k, d in enumerate(dilations):
            for scr, out in zip((do_scr, lse_scr, dl_scr), outs[3 * k:3 * k + 3]):
                _slabs_from_rows(scr, out, d)

    row = pl.BlockSpec((tm, ATT_W), lambda i: (i, 0))
    view = lambda d, w: pl.BlockSpec((tm // d, d * w), lambda i: (i, 0))
    outs = pl.pallas_call(
        body, name="dil_bwd_prep", grid=(SEQ // tm,),
        in_specs=[row, row, pl.BlockSpec((tm, LANE), lambda i: (i, 0))],
        out_specs=[view(d, w) for d in dilations for w in (ATT_W, LANE, LANE)],
        out_shape=[_sds((SEQ // d, d * w), t) for d in dilations for w, t in ((ATT_W, BF16), (LANE, F32), (LANE, F32))],
        scratch_shapes=[pltpu.VMEM((o_chunks, tm, LANE), F32), pltpu.VMEM((1, tm, LANE), F32),
                        pltpu.VMEM((1, tm, LANE), F32)],
        compiler_params=_params("parallel"),
    )(o, do, lse)
    return [outs[3 * k:3 * k + 3] for k in range(len(dilations))]


def _rope_tables():
    half = ROPE_DIM // 2
    inv_freq = np.float32(ROPE_THETA) ** (-np.arange(half, dtype=np.float32) * np.float32(2.0) / np.float32(ROPE_DIM))
    ang = np.arange(SEQ, dtype=np.float32)[:, None] * inv_freq.astype(np.float32)[None, :]
    cos, sin = jnp.asarray(np.cos(ang).astype(np.float32)), jnp.asarray(np.sin(ang).astype(np.float32))
    ones = jnp.ones((SEQ, HEAD_DIM - ROPE_DIM), F32)
    zeros = jnp.zeros((SEQ, HEAD_DIM - ROPE_DIM), F32)
    zh = jnp.zeros((SEQ, half), F32)
    c_tab = jnp.concatenate([cos, cos, ones], axis=1)
    a_tab = jnp.concatenate([-sin, zh, zeros], axis=1)
    b_tab = jnp.concatenate([zh, sin, zeros], axis=1)
    two = lambda t: jnp.concatenate([t, t], axis=1)
    return two(c_tab), two(a_tab), two(b_tab)


def _rotate(x, c_tab, a_tab, b_tab):
    return x * c_tab + pltpu.roll(x, LANE - ROPE_DIM // 2, 1) * a_tab + pltpu.roll(x, ROPE_DIM // 2, 1) * b_tab


def _rope_fwd(zm, tabs, *, tm=512):
    width = 3 * ATT_W
    dilations = [d for _, d in DIL_PATTERNS]

    def body(q_ref, k_ref, v_ref, c_ref, a_ref, b_ref, *rest):
        outs, scr = rest[:-1], rest[-1]
        per_part = ATT_W // LANE
        for part, (x_ref, mult) in enumerate(((q_ref, SCALE), (k_ref, 1.0))):
            for cc in range(per_part):
                sl = slice(cc * LANE, (cc + 1) * LANE)
                scr[part * per_part + cc] = _rotate(x_ref[:, sl].astype(F32), c_ref[...], a_ref[...], b_ref[...]) * mult
        for cc in range(per_part):
            scr[2 * per_part + cc] = v_ref[:, cc * LANE:(cc + 1) * LANE].astype(F32)
        for o_ref, d in zip(outs, dilations):
            for r in range(d):
                for ch in range(width // LANE):
                    o_ref[:, r * width + ch * LANE:r * width + (ch + 1) * LANE] = (
                        scr.at[ch][pl.ds(r, tm // d, stride=d), :].astype(o_ref.dtype))

    tab = pl.BlockSpec((tm, LANE), lambda i: (i, 0))
    col = lambda b: pl.BlockSpec((tm, ATT_W), lambda i: (i, b))
    return pl.pallas_call(
        body, name="rope_fwd", grid=(SEQ // tm,),
        in_specs=[col(3), col(4), col(5), tab, tab, tab],
        out_specs=[pl.BlockSpec((tm // d, d * width), lambda i: (i, 0)) for d in dilations],
        out_shape=[_sds((SEQ // d, d * width), BF16) for d in dilations],
        scratch_shapes=[pltpu.VMEM((width // LANE, tm, LANE), F32)],
        compiler_params=_params("parallel"),
    )(zm, zm, zm, *tabs)


def _dil_grad_combine(dqs, dks, dvs, tabs, dz, *, tm=256):
    dilations = [d for _, d in DIL_PATTERNS]
    chunks = ATT_W // LANE

    def body(*refs):
        groups = (refs[0:3], refs[3:6], refs[6:9])
        c_ref, a_ref, b_ref, _, o_ref, scr = refs[9:]

        def total(part, cc):
            acc = None
            for g, (ref, d) in enumerate(zip(groups[part], dilations)):
                term = ref[:, cc * LANE:(cc + 1) * LANE].astype(F32) if d == 1 else scr[part, g, cc]
                acc = term if acc is None else acc + term
            return acc

        for part in range(3):
            for g, (ref, d) in enumerate(zip(groups[part], dilations)):
                if d > 1:
                    _rows_from_slabs(ref, scr.at[part, g], d)
        for cc in range(chunks):
            for part in range(2):
                o_ref[:, part * ATT_W + cc * LANE:part * ATT_W + (cc + 1) * LANE] = _rotate(
                    total(part, cc), c_ref[...], -a_ref[...], -b_ref[...]).astype(o_ref.dtype)
            o_ref[:, 2 * ATT_W + cc * LANE:2 * ATT_W + (cc + 1) * LANE] = total(2, cc).astype(o_ref.dtype)

    view = lambda d: pl.BlockSpec((tm // d, d * ATT_W), lambda i: (i, 0))
    tab = pl.BlockSpec((tm, LANE), lambda i: (i, 0))
    return pl.pallas_call(
        body, name="dil_grad_combine", grid=(SEQ // tm,),
        in_specs=[view(d) for d in dilations] * 3 + [tab] * 3 + [ANY],
        out_specs=pl.BlockSpec((tm, 3 * ATT_W), lambda i: (i, 1)),
        out_shape=_sds((SEQ, Z_MAIN), BF16),
        input_output_aliases={12: 0},
        scratch_shapes=[pltpu.VMEM((3, len(dilations), chunks, tm, LANE), F32)],
        compiler_params=_params("parallel"),
    )(*dqs, *dks, *dvs, *tabs, dz)


def _dil_valid(n):
    qi = lax.broadcasted_iota(jnp.int32, (DIL_BLK, 2 * DIL_BLK), 0)
    ki = lax.broadcasted_iota(jnp.int32, (DIL_BLK, 2 * DIL_BLK), 1)
    dist = qi + DIL_BLK - ki
    return (dist >= 0) & (dist <= DIL_BLK) & ((n > 0) | (ki >= DIL_BLK))


def _dil_fwd(qkv_v, d):
    length = SEQ // d
    nb = length // DIL_BLK
    nsub = min(DIL_STEP_BLOCKS, nb)

    def body(q_ref, kp_ref, kc_ref, vp_ref, vc_ref, o_ref, lse_ref):
        m_step = pl.program_id(1)
        lane = lax.broadcasted_iota(jnp.int32, (DIL_BLK, LANE), 1)
        jobs = [(sub, h) for sub in range(nsub) for h in range(N_HEADS)]
        rows = lambda sub: slice(sub * DIL_BLK, (sub + 1) * DIL_BLK)
        cols = lambda h: slice(h * HEAD_DIM, (h + 1) * HEAD_DIM)

        def keys(prev_ref, cur_ref, sub, h):
            before = prev_ref[:, cols(h)] if sub == 0 else cur_ref[rows(sub - 1), cols(h)]
            return jnp.concatenate([before, cur_ref[rows(sub), cols(h)]], axis=0)

        scores = [lax.dot_general(q_ref[rows(sub), cols(h)], keys(kp_ref, kc_ref, sub, h), NT_DIMS,
                                  preferred_element_type=F32) for sub, h in jobs]
        ok = [_dil_valid(m_step)] + [_dil_valid(1)] * (nsub - 1)
        probs, inv_l, lse_all = [], [], [jnp.zeros((DIL_BLK, LANE), F32)] * nsub
        for idx, (sub, h) in enumerate(jobs):
            s = jnp.where(ok[sub], scores[idx], NEG_INF)
            m = jnp.max(s, axis=-1, keepdims=True)
            p = jnp.exp(s - m)
            l = jnp.sum(p, axis=-1, keepdims=True)
            probs.append(p.astype(BF16))
            inv_l.append(1.0 / l)
            lse_all[sub] = jnp.where(lane == h, m + jnp.log(l), lse_all[sub])
        outs = [jnp.dot(probs[idx], keys(vp_ref, vc_ref, sub, h), preferred_element_type=F32) * inv_l[idx]
                for idx, (sub, h) in enumerate(jobs)]
        for sub in range(nsub):
            o_ref[rows(sub), :] = jnp.concatenate(outs[sub * N_HEADS:(sub + 1) * N_HEADS], axis=1).astype(o_ref.dtype)
            lse_ref[rows(sub), :] = lse_all[sub]

    pair = lambda f: pl.BlockSpec((nsub * DIL_BLK, ATT_W), f)
    one = lambda f: pl.BlockSpec((DIL_BLK, ATT_W), f)
    before = lambda m: jnp.maximum(nsub * m - 1, 0)
    o, lse = pl.pallas_call(
        body, name=f"dil_fwd_d{d}", grid=(d, nb // nsub),
        in_specs=[pair(lambda r, m: (m, 3 * r)),
                  one(lambda r, m: (before(m), 3 * r + 1)), pair(lambda r, m: (m, 3 * r + 1)),
                  one(lambda r, m: (before(m), 3 * r + 2)), pair(lambda r, m: (m, 3 * r + 2))],
        out_specs=[pair(lambda r, m: (m, r)), pl.BlockSpec((nsub * DIL_BLK, LANE), lambda r, m: (m, r))],
        out_shape=[_sds((length, d * ATT_W), BF16), _sds((length, d * LANE), F32)],
        compiler_params=_params("parallel", "arbitrary"),
    )(qkv_v, qkv_v, qkv_v, qkv_v, qkv_v)
    return o, lse


def _rows_from_slabs(view_ref, scr, d):
    chunks, rows = scr.shape[0], scr.shape[1]
    for r in range(d):
        for ch in range(chunks):
            lo = (r * chunks + ch) * LANE
            scr.at[ch][pl.ds(r, rows // d, stride=d), :] = view_ref[:, lo:lo + LANE].astype(F32)


def _slabs_from_rows(scr, view_ref, d):
    chunks, rows = scr.shape[0], scr.shape[1]
    for r in range(d):
        for ch in range(chunks):
            lo = (r * chunks + ch) * LANE
            view_ref[:, lo:lo + LANE] = scr.at[ch][pl.ds(r, rows // d, stride=d), :].astype(view_ref.dtype)


def _dil_merge(os_, lses, *, tm=512):
    dilations = [d for _, d in DIL_PATTERNS]
    o_chunks = ATT_W // LANE

    def body(o0, o1, o2, l0, l1, l2, y_ref, lse_ref, o_scr, l_scr):
        os_nat, ls = [], []
        for g, (o_ref, l_ref, d) in enumerate(zip((o0, o1, o2), (l0, l1, l2), dilations)):
            if d == 1:
                os_nat.append(o_ref[...].astype(F32))
                ls.append(l_ref[...])
            else:
                _rows_from_slabs(o_ref, o_scr.at[g], d)
                _rows_from_slabs(l_ref, l_scr.at[g], d)
                os_nat.append(jnp.concatenate([o_scr[g, ch] for ch in range(o_chunks)], axis=1))
                ls.append(l_scr[g, 0])
        m = jnp.maximum(jnp.maximum(ls[0], ls[1]), ls[2])
        es = [jnp.exp(l - m) for l in ls]
        tot = es[0] + es[1] + es[2]
        lse_ref[...] = m + jnp.log(tot)
        alphas = [e / tot for e in es]
        outs = []
        for h in range(N_HEADS):
            acc = None
            for g in range(3):
                term = alphas[g][:, h:h + 1] * _head(os_nat[g], h)
                acc = term if acc is None else acc + term
            outs.append(acc)
        y_ref[...] = jnp.concatenate(outs, axis=1).astype(y_ref.dtype)

    row = pl.BlockSpec((tm, ATT_W), lambda i: (i, 0))
    vec = pl.BlockSpec((tm, LANE), lambda i: (i, 0))
    view = lambda d, w: pl.BlockSpec((tm // d, d * w), lambda i: (i, 0))
    return pl.pallas_call(
        body, name="dil_merge", grid=(SEQ // tm,),
        in_specs=[view(d, ATT_W) for d in dilations] + [view(d, LANE) for d in dilations], out_specs=[row, vec],
        out_shape=[_sds((SEQ, ATT_W), BF16), _sds((SEQ, LANE), F32)],
        scratch_shapes=[pltpu.VMEM((3, o_chunks, tm, LANE), F32), pltpu.VMEM((3, 1, tm, LANE), F32)],
        compiler_params=_params("parallel"),
    )(*os_, *lses)


def _dil_bwd(qkv_v, do_v, lse_v, dl_v, d):
    length = SEQ // d
    nb = length // DIL_BLK
    nsub = min(DIL_STEP_BLOCKS, nb)
    n_steps = nb // nsub

    def body(q_ref, kp_ref, kc_ref, vp_ref, vc_ref, lse_ref, dl_ref, do_ref, dq_ref, dk_ref, dv_ref, dk_s, dv_s):
        m_step = pl.program_id(1)

        @pl.when(m_step == 0)
        def _():
            dk_s[...] = jnp.zeros_like(dk_s)
            dv_s[...] = jnp.zeros_like(dv_s)

        jobs = [(sub, h) for sub in range(nsub) for h in range(N_HEADS)]
        rows = lambda sub: slice(sub * DIL_BLK, (sub + 1) * DIL_BLK)
        cols = lambda h: slice(h * HEAD_DIM, (h + 1) * HEAD_DIM)

        def keys(prev_ref, cur_ref, sub, h):
            before = prev_ref[:, cols(h)] if sub == 0 else cur_ref[rows(sub - 1), cols(h)]
            return jnp.concatenate([before, cur_ref[rows(sub), cols(h)]], axis=0)

        kks = [keys(kp_ref, kc_ref, sub, h) for sub, h in jobs]
        scores = [lax.dot_general(q_ref[rows(sub), cols(h)], kks[idx], NT_DIMS, preferred_element_type=F32)
                  for idx, (sub, h) in enumerate(jobs)]
        dps = [lax.dot_general(do_ref[rows(sub), cols(h)], keys(vp_ref, vc_ref, sub, h), NT_DIMS,
                               preferred_element_type=F32) for sub, h in jobs]
        ok = [_dil_valid(m_step)] + [_dil_valid(1)] * (nsub - 1)
        ps, dss = [], []
        for idx, (sub, h) in enumerate(jobs):
            p = jnp.where(ok[sub], jnp.exp(scores[idx] - lse_ref[rows(sub), h:h + 1]), 0.0)
            ps.append(p.astype(BF16))
            dss.append((p * (dps[idx] - dl_ref[rows(sub), h:h + 1])).astype(BF16))
        dqs = [jnp.dot(dss[idx], kks[idx], preferred_element_type=F32) * SCALE for idx in range(len(jobs))]
        dkks = [lax.dot_general(dss[idx], q_ref[rows(sub), cols(h)], TN_DIMS, preferred_element_type=F32)
                for idx, (sub, h) in enumerate(jobs)]
        dvvs = [lax.dot_general(ps[idx], do_ref[rows(sub), cols(h)], TN_DIMS, preferred_element_type=F32)
                for idx, (sub, h) in enumerate(jobs)]
        for sub in range(nsub):
            dq_ref[rows(sub), :] = jnp.concatenate(dqs[sub * N_HEADS:(sub + 1) * N_HEADS], axis=1).astype(dq_ref.dtype)
        base = m_step * (nsub * DIL_BLK)
        blocks = [pl.ds(pl.multiple_of(jnp.maximum(base - DIL_BLK, 0), DIL_BLK), DIL_BLK)]
        blocks += [pl.ds(pl.multiple_of(base + s * DIL_BLK, DIL_BLK), DIL_BLK) for s in range(nsub)]
        for acc, parts in ((dk_s, dkks), (dv_s, dvvs)):
            top = lambda sub: jnp.concatenate([parts[sub * N_HEADS + h][:DIL_BLK] for h in range(N_HEADS)], axis=1)
            bottom = lambda sub: jnp.concatenate([parts[sub * N_HEADS + h][DIL_BLK:] for h in range(N_HEADS)], axis=1)
            acc[blocks[0], :] += top(0)
            for s in range(nsub):
                acc[blocks[s + 1], :] += bottom(s) + top(s + 1) if s + 1 < nsub else bottom(s)

        @pl.when(m_step == n_steps - 1)
        def _():
            dk_ref[...] = dk_s[...].astype(dk_ref.dtype)
            dv_ref[...] = dv_s[...].astype(dv_ref.dtype)

    pair = lambda f: pl.BlockSpec((nsub * DIL_BLK, ATT_W), f)
    one = lambda f: pl.BlockSpec((DIL_BLK, ATT_W), f)
    vec = lambda f: pl.BlockSpec((nsub * DIL_BLK, LANE), f)
    whole = pl.BlockSpec((length, ATT_W), lambda r, m: (0, r))
    before = lambda m: jnp.maximum(nsub * m - 1, 0)
    outs = pl.pallas_call(
        body, name=f"dil_bwd_d{d}", grid=(d, n_steps),
        in_specs=[pair(lambda r, m: (m, 3 * r)),
                  one(lambda r, m: (before(m), 3 * r + 1)), pair(lambda r, m: (m, 3 * r + 1)),
                  one(lambda r, m: (before(m), 3 * r + 2)), pair(lambda r, m: (m, 3 * r + 2)),
                  vec(lambda r, m: (m, r)), vec(lambda r, m: (m, r)), pair(lambda r, m: (m, r))],
        out_specs=[pair(lambda r, m: (m, r)), whole, whole],
        out_shape=[_sds((length, d * ATT_W), BF16)] * 3,
        scratch_shapes=[pltpu.VMEM((length, ATT_W), F32), pltpu.VMEM((length, ATT_W), F32)],
        compiler_params=_params("arbitrary", "arbitrary"),
    )(qkv_v, qkv_v, qkv_v, qkv_v, qkv_v, lse_v, dl_v, do_v)
    return outs


def _sigmoid(x):
    return 1.0 / (1.0 + jnp.exp(-x))


def _mix_fwd(ya, yb, w_oa, w_ob, zm, *, tm=512):
    def body(ya_ref, yb_ref, wa_ref, wb_ref, ga_ref, gb_ref, pa_ref, pb_ref, mix_ref):
        pa = jnp.dot(ya_ref[...], wa_ref[...], preferred_element_type=F32)
        pb = jnp.dot(yb_ref[...], wb_ref[...], preferred_element_type=F32)
        pa_ref[...] = pa.astype(pa_ref.dtype)
        pb_ref[...] = pb.astype(pb_ref.dtype)
        mix_ref[...] = (_sigmoid(ga_ref[...].astype(F32)) * pa + _sigmoid(gb_ref[...].astype(F32)) * pb
                        ).astype(mix_ref.dtype)

    row = pl.BlockSpec((tm, ATT_W), lambda i: (i, 0))
    wsp = pl.BlockSpec((ATT_W, D_MODEL), lambda i: (0, 0))
    wide = pl.BlockSpec((tm, D_MODEL), lambda i: (i, 0))
    return pl.pallas_call(
        body, name="mix_fwd", grid=(SEQ // tm,),
        in_specs=[row, row, wsp, wsp, pl.BlockSpec((tm, D_MODEL), lambda i: (i, 3)),
                  pl.BlockSpec((tm, D_MODEL), lambda i: (i, 4))],
        out_specs=[wide] * 3, out_shape=[_sds((SEQ, D_MODEL), BF16)] * 3,
        compiler_params=_params("parallel"),
    )(ya, yb, w_oa, w_ob, zm, zm)


def _gate_bwd(dmix, zm, p, gate_block, dz, *, name, tm=512):
    def body(dm_ref, g_ref, p_ref, *rest):
        dp_ref, dz_ref = rest[-2], rest[-1]
        dm = dm_ref[...].astype(F32)
        s = _sigmoid(g_ref[...].astype(F32))
        dp_ref[...] = (dm * s).astype(dp_ref.dtype)
        dz_ref[...] = (dm * p_ref[...].astype(F32) * s * (1.0 - s)).astype(dz_ref.dtype)

    wide = pl.BlockSpec((tm, D_MODEL), lambda i: (i, 0))
    gate = pl.BlockSpec((tm, D_MODEL), lambda i: (i, gate_block))
    extra = [] if dz is None else [dz]
    return pl.pallas_call(
        body, name=name, grid=(SEQ // tm,),
        in_specs=[wide, gate, wide] + [ANY] * len(extra),
        out_specs=[wide, gate],
        out_shape=[_sds((SEQ, D_MODEL), BF16), _sds((SEQ, Z_MAIN), BF16)],
        input_output_aliases={3: 1} if extra else {},
        compiler_params=_params("parallel"),
    )(dmix, zm, p, *extra)


def _out_fwd(mixed, w_out, x, g_post, g_pre, *, tm=512):
    def body(m_ref, w_ref, x_ref, gp_ref, gn_ref, y_ref, x2_ref, h_ref):
        y = jnp.dot(m_ref[...], w_ref[...], preferred_element_type=F32)
        y_ref[...] = y
        r = lax.rsqrt(jnp.mean(y * y, axis=-1, keepdims=True) + RMS_EPS)
        x2 = x_ref[...] + y * r * gp_ref[...]
        x2_ref[...] = x2
        r2 = lax.rsqrt(jnp.mean(x2 * x2, axis=-1, keepdims=True) + RMS_EPS)
        h_ref[...] = (x2 * r2 * gn_ref[...]).astype(h_ref.dtype)

    row = pl.BlockSpec((tm, D_MODEL), lambda i: (i, 0))
    vec = pl.BlockSpec((1, D_MODEL), lambda i: (0, 0))
    return pl.pallas_call(
        body, name="out_fwd", grid=(SEQ // tm,),
        in_specs=[row, pl.BlockSpec((D_MODEL, D_MODEL), lambda i: (0, 0)), row, vec, vec],
        out_specs=[row] * 3,
        out_shape=[_sds((SEQ, D_MODEL), F32), _sds((SEQ, D_MODEL), F32), _sds((SEQ, D_MODEL), BF16)],
        compiler_params=_params("parallel"),
    )(mixed, w_out, x, g_post, g_pre)


FFN_HALF = 256
FFN_TN = 2 * FFN_HALF
FFN_NJ = D_FF // FFN_HALF
FFN_GROUP = 2 * SUBLANE
UP_TM = 1024


def _ffn_interleave(t):
    lead = t.shape[:-1]
    return jnp.swapaxes(t.reshape(*lead, 2, FFN_NJ, FFN_HALF), -3, -2).reshape(*lead, 2 * D_FF)


def _ffn_deinterleave(t):
    lead = t.shape[:-1]
    return jnp.swapaxes(t.reshape(*lead, FFN_NJ, 2, FFN_HALF), -3, -2).reshape(*lead, 2 * D_FF)


W_IN_SHARD = (Z_MAIN + N_HEADS) // N_DEV
FORGET_LO = 3 * ATT_W


def _w_in_from_shards(shards, *, tm=256):
    def columns(g_ref, lo, width):
        p, off = divmod(lo, W_IN_SHARD)
        if off + width <= W_IN_SHARD:
            return g_ref[p, :, off:off + width]
        first = W_IN_SHARD - off
        return jnp.concatenate([g_ref[p, :, off:], g_ref[p + 1, :, :width - first]], axis=1)

    def body(g_ref, main_ref, f_ref):
        for t in range(Z_MAIN // LANE):
            lo = t * LANE
            main_ref[:, lo:lo + LANE] = columns(g_ref, lo if lo < FORGET_LO else lo + N_HEADS, LANE)
        f_ref[...] = jnp.concatenate([columns(g_ref, FORGET_LO, N_HEADS),
                                      jnp.zeros((tm, F_PAD - N_HEADS), f_ref.dtype)], axis=1)

    return pl.pallas_call(
        body, name="w_in_from_shards", grid=(D_MODEL // tm,),
        in_specs=[pl.BlockSpec((N_DEV, tm, W_IN_SHARD), lambda i: (0, i, 0))],
        out_specs=[pl.BlockSpec((tm, Z_MAIN), lambda i: (i, 0)), pl.BlockSpec((tm, F_PAD), lambda i: (i, 0))],
        out_shape=[_sds((D_MODEL, Z_MAIN), shards.dtype), _sds((D_MODEL, F_PAD), shards.dtype)],
        compiler_params=_params("parallel"),
    )(shards)


def _w_in_to_shards(g_main, g_f, *, tm=256):
    def natural(main_ref, f_ref, lo, width):
        pieces, hi = [], lo + width
        for ref, start, stop, shift in ((main_ref, 0, FORGET_LO, 0), (f_ref, FORGET_LO, FORGET_LO + N_HEADS, FORGET_LO),
                                        (main_ref, FORGET_LO + N_HEADS, Z_MAIN + N_HEADS, N_HEADS)):
            a, b = max(lo, start), min(hi, stop)
            if a < b:
                pieces.append(ref[:, a - shift:b - shift])
        return pieces[0] if len(pieces) == 1 else jnp.concatenate(pieces, axis=1)

    def body(main_ref, f_ref, o_ref):
        for p in range(N_DEV):
            for q in range(-(-W_IN_SHARD // LANE)):
                width = min(LANE, W_IN_SHARD - q * LANE)
                o_ref[p, :, q * LANE:q * LANE + width] = natural(main_ref, f_ref, p * W_IN_SHARD + q * LANE, width)

    return pl.pallas_call(
        body, name="w_in_to_shards", grid=(D_MODEL // tm,),
        in_specs=[pl.BlockSpec((tm, Z_MAIN), lambda i: (i, 0)), pl.BlockSpec((tm, F_PAD), lambda i: (i, 0))],
        out_specs=pl.BlockSpec((N_DEV, tm, W_IN_SHARD), lambda i: (0, i, 0)),
        out_shape=_sds((N_DEV, D_MODEL, W_IN_SHARD), g_main.dtype),
        compiler_params=_params("parallel"),
    )(g_main, g_f)


W_UP_SHARD = 2 * D_FF // N_DEV


def _w_up_lane_tile(k):
    block = k // 2
    return (2 * (block % FFN_NJ) + block // FFN_NJ) * FFN_HALF + (k % 2) * LANE


def _w_up_from_shards(shards, *, tm=256):
    def body(g_ref, o_ref):
        for k in range(2 * D_FF // LANE):
            p, off = divmod(k * LANE, W_UP_SHARD)
            if off + LANE <= W_UP_SHARD:
                tile = g_ref[p, :, off:off + LANE]
            else:
                tile = jnp.concatenate([g_ref[p, :, off:], g_ref[p + 1, :, :off + LANE - W_UP_SHARD]], axis=1)
            dst = _w_up_lane_tile(k)
            o_ref[:, dst:dst + LANE] = tile

    return pl.pallas_call(
        body, name="w_up_from_shards", grid=(D_MODEL // tm,),
        in_specs=[pl.BlockSpec((N_DEV, tm, W_UP_SHARD), lambda i: (0, i, 0))],
        out_specs=pl.BlockSpec((tm, 2 * D_FF), lambda i: (i, 0)),
        out_shape=_sds((D_MODEL, 2 * D_FF), shards.dtype),
        compiler_params=_params("parallel"),
    )(shards)


def _w_up_to_shards(t, *, tm=256):
    def body(x_ref, o_ref):
        for p in range(N_DEV):
            for q in range(-(-W_UP_SHARD // LANE)):
                width = min(LANE, W_UP_SHARD - q * LANE)
                k, off = divmod(p * W_UP_SHARD + q * LANE, LANE)
                src = _w_up_lane_tile(k)
                if off == 0:
                    tile = x_ref[:, src:src + width]
                else:
                    tile = x_ref[:, src + off:src + LANE]
                    if width > LANE - off:
                        nxt = _w_up_lane_tile(k + 1)
                        tile = jnp.concatenate([tile, x_ref[:, nxt:nxt + width - (LANE - off)]], axis=1)
                o_ref[p, :, q * LANE:q * LANE + width] = tile

    return pl.pallas_call(
        body, name="w_up_to_shards", grid=(D_MODEL // tm,),
        in_specs=[pl.BlockSpec((tm, 2 * D_FF), lambda i: (i, 0))],
        out_specs=pl.BlockSpec((N_DEV, tm, W_UP_SHARD), lambda i: (0, i, 0)),
        out_shape=_sds((N_DEV, D_MODEL, W_UP_SHARD), t.dtype),
        compiler_params=_params("parallel"),
    )(t)


def _gelu_parts(a):
    c = math.sqrt(2.0 / math.pi)
    a2 = a * a
    t = jnp.tanh((c * a) * (1.0 + 0.044715 * a2))
    half_a, one_t = 0.5 * a, 1.0 + t
    gelu = half_a * one_t
    dgelu = 0.5 * one_t + half_a * (1.0 - t * t) * (c + (3.0 * 0.044715 * c) * a2)
    return gelu, dgelu


def _row_masks(down):
    row = lax.broadcasted_iota(jnp.int32, (SUBLANE, FFN_TN), 0)
    return (row < 1, row < 2) if down else (row >= SUBLANE - 1, row >= SUBLANE - 2)


def _rolled(x, down):
    return (pltpu.roll(x, 1, 0), pltpu.roll(x, 2, 0)) if down else (
        pltpu.roll(x, SUBLANE - 1, 0), pltpu.roll(x, SUBLANE - 2, 0))


def _shifted(cur_rolled, neighbour_rolled, masks):
    return (jnp.where(masks[0], neighbour_rolled[0], cur_rolled[0]),
            jnp.where(masks[1], neighbour_rolled[1], cur_rolled[1]))


def _conv_consts(w_ref, b_ref):
    shape = (SUBLANE, FFN_TN)
    return [jnp.broadcast_to(w_ref[k:k + 1, :], shape) for k in range(3)] + [jnp.broadcast_to(b_ref[...], shape)]


def _up_conv_fwd(h2, w_up, conv_w, conv_b):
    nrow = SEQ // UP_TM
    n_tiles = FFN_NJ * nrow
    n_groups = UP_TM // FFN_GROUP

    def body(h_ref, wu_ref, w_ref, b_ref, u_ref, ab_ref, m_ref, ua_s, ub_s, c1_s, c2_s):
        k = pl.program_id(0)

        @pl.when(k == 0)
        def _():
            ub_s[...] = jnp.zeros_like(ub_s)

        @pl.when(jnp.maximum(k - 1, 0) % nrow == 0)
        def _():
            c1_s[...] = jnp.zeros_like(c1_s)
            c2_s[...] = jnp.zeros_like(c2_s)

        def step(write_s, read_s):
            h_rows = pl.ds(pl.multiple_of((this(k) % nrow) * UP_TM, UP_TM), UP_TM)
            u = jnp.dot(h_ref[h_rows, :], wu_ref[...], preferred_element_type=F32)
            write_s[...] = u
            u_ref[...] = u.astype(u_ref.dtype)
            w0, w1, w2, bias = _conv_consts(w_ref, b_ref)
            masks = _row_masks(True)
            above = (c1_s[...], c2_s[...])
            for g in range(n_groups):
                rows = slice(g * FFN_GROUP, (g + 1) * FFN_GROUP)
                x = read_s[rows, :]
                convs = []
                for c in range(2):
                    cur = x[c * SUBLANE:(c + 1) * SUBLANE]
                    cur_rolled = _rolled(cur, True)
                    s1, s2 = _shifted(cur_rolled, above, masks)
                    convs.append(w0 * s2 + w1 * s1 + w2 * cur + bias)
                    above = cur_rolled
                y = jnp.concatenate(convs, axis=0)
                ab_ref[rows, :] = y.astype(ab_ref.dtype)
                m_ref[rows, :] = (_gelu_parts(y[:, :FFN_HALF])[0] * y[:, FFN_HALF:]).astype(m_ref.dtype)
            c1_s[...], c2_s[...] = above

        @pl.when(k % 2 == 0)
        def _():
            step(ua_s, ub_s)

        @pl.when(k % 2 == 1)
        def _():
            step(ub_s, ua_s)

    this = lambda k: jnp.minimum(k, n_tiles - 1)
    last = lambda k: jnp.maximum(k - 1, 0)
    blk = lambda tile: pl.BlockSpec((UP_TM, FFN_TN), lambda k: (tile(k) % nrow, tile(k) // nrow))
    return pl.pallas_call(
        body, name="up_conv_fwd", grid=(n_tiles + 1,),
        in_specs=[pl.BlockSpec((SEQ, D_MODEL), lambda k: (0, 0)),
                  pl.BlockSpec((D_MODEL, FFN_TN), lambda k: (0, this(k) // nrow)),
                  pl.BlockSpec((3, FFN_TN), lambda k: (0, last(k) // nrow)),
                  pl.BlockSpec((1, FFN_TN), lambda k: (0, last(k) // nrow))],
        out_specs=[blk(this), blk(last), pl.BlockSpec((UP_TM, FFN_HALF), lambda k: (last(k) % nrow, last(k) // nrow))],
        out_shape=[_sds((SEQ, 2 * D_FF), BF16), _sds((SEQ, 2 * D_FF), BF16), _sds((SEQ, D_FF), BF16)],
        scratch_shapes=[pltpu.VMEM((UP_TM, FFN_TN), F32), pltpu.VMEM((UP_TM, FFN_TN), F32),
                        pltpu.VMEM((SUBLANE, FFN_TN), F32), pltpu.VMEM((SUBLANE, FFN_TN), F32)],
        compiler_params=_params("arbitrary"),
    )(h2, w_up, conv_w, conv_b)


def _ffn_mid_bwd(dy2, w_down, u, ab, conv_w):
    nrow = SEQ // UP_TM
    n_tiles = FFN_NJ * nrow
    n_groups = UP_TM // FFN_GROUP
    this = lambda k: jnp.minimum(k, n_tiles - 1)
    last = lambda k: jnp.maximum(k - 1, 0)
    row_of = lambda tile: nrow - 1 - tile % nrow

    def body(dy_ref, wd_ref, u_ref, ab_ref, w_ref, du_ref, gw_ref, gb_ref, c_s, dma_s, dmb_s):
        k = pl.program_id(0)

        @pl.when(k == 0)
        def _():
            dmb_s[...] = jnp.zeros_like(dmb_s)

        @pl.when(last(k) % nrow == 0)
        def _():
            c_s[...] = jnp.zeros_like(c_s)
            gw_ref[...] = jnp.zeros_like(gw_ref)
            gb_ref[...] = jnp.zeros_like(gb_ref)

        def step(write_s, read_s):
            dy_rows = pl.ds(pl.multiple_of(row_of(this(k)) * UP_TM, UP_TM), UP_TM)
            write_s[...] = lax.dot_general(dy_ref[dy_rows, :], wd_ref[...], NT_DIMS,
                                           preferred_element_type=F32)
            taps = [jnp.broadcast_to(w_ref[t:t + 1, :], (SUBLANE, FFN_TN)) for t in range(3)]
            masks = _row_masks(False)
            below = _rolled(c_s[...], False)
            acc = [jnp.zeros((SUBLANE, FFN_TN), F32)] * 4
            for g in reversed(range(n_groups)):
                rows = slice(g * FFN_GROUP, (g + 1) * FFN_GROUP)
                x, y, dmv = u_ref[rows, :].astype(F32), ab_ref[rows, :].astype(F32), read_s[rows, :]
                gelu, dgelu = _gelu_parts(y[:, :FFN_HALF])
                d = jnp.concatenate([dmv * y[:, FFN_HALF:] * dgelu, dmv * gelu], axis=1)
                pre = [None, None]
                for c in (1, 0):
                    sl = slice(c * SUBLANE, (c + 1) * SUBLANE)
                    cur, xs = d[sl], x[sl]
                    cur_rolled = _rolled(cur, False)
                    up1, up2 = _shifted(cur_rolled, below, masks)
                    acc = [acc[0] + up2 * xs, acc[1] + up1 * xs, acc[2] + cur * xs, acc[3] + cur]
                    pre[c] = taps[2] * cur + taps[1] * up1 + taps[0] * up2
                    below = cur_rolled
                du_ref[rows, :] = jnp.concatenate(pre, axis=0).astype(du_ref.dtype)
            c_s[...] = pltpu.roll(below[0], 1, 0)
            for t in range(3):
                gw_ref[t:t + 1, :] += jnp.sum(acc[t], axis=0, keepdims=True)
            gb_ref[...] += jnp.sum(acc[3], axis=0, keepdims=True)

        @pl.when(k % 2 == 0)
        def _():
            step(dma_s, dmb_s)

        @pl.when(k % 2 == 1)
        def _():
            step(dmb_s, dma_s)

    blk = pl.BlockSpec((UP_TM, FFN_TN), lambda k: (row_of(last(k)), last(k) // nrow))
    col = lambda rows: pl.BlockSpec((rows, FFN_TN), lambda k: (0, last(k) // nrow))
    return pl.pallas_call(
        body, name="ffn_mid_bwd", grid=(n_tiles + 1,),
        in_specs=[pl.BlockSpec((SEQ, D_MODEL), lambda k: (0, 0)),
                  pl.BlockSpec((FFN_HALF, D_MODEL), lambda k: (this(k) // nrow, 0)), blk, blk, col(3)],
        out_specs=[blk, col(3), col(1)],
        out_shape=[_sds((SEQ, 2 * D_FF), BF16), _sds((3, 2 * D_FF), F32), _sds((1, 2 * D_FF), F32)],
        scratch_shapes=[pltpu.VMEM((SUBLANE, FFN_TN), F32), pltpu.VMEM((UP_TM, FFN_HALF), F32),
                        pltpu.VMEM((UP_TM, FFN_HALF), F32)],
        compiler_params=_params("arbitrary"),
    )(dy2, w_down, u, ab, conv_w)


def _down_fwd(m, w_down, x2, g_post, target, *, tm=512):
    def body(m_ref, w_ref, x2_ref, g_ref, t_ref, dout_ref, dy_ref, gg_ref, loss_ref):
        @pl.when(pl.program_id(0) == 0)
        def _():
            gg_ref[...] = jnp.zeros_like(gg_ref)
            loss_ref[...] = jnp.zeros_like(loss_ref)

        y = jnp.dot(m_ref[...], w_ref[...], preferred_element_type=F32)
        r = lax.rsqrt(jnp.mean(y * y, axis=-1, keepdims=True) + RMS_EPS)
        yn = y * r
        diff = (x2_ref[...] + yn * g_ref[...]) - t_ref[...]
        loss_ref[...] += jnp.sum(diff * diff)
        dout = diff * (1.0 / D_MODEL)
        dout_ref[...] = dout
        gg_ref[...] += jnp.sum(dout * yn, axis=0, keepdims=True)
        dn = dout * g_ref[...]
        dy_ref[...] = (r * (dn - yn * jnp.mean(dn * yn, axis=-1, keepdims=True))).astype(dy_ref.dtype)

    row = pl.BlockSpec((tm, D_MODEL), lambda i: (i, 0))
    vec = pl.BlockSpec((1, D_MODEL), lambda i: (0, 0))
    return pl.pallas_call(
        body, name="down_fwd", grid=(SEQ // tm,),
        in_specs=[pl.BlockSpec((tm, D_FF), lambda i: (i, 0)), pl.BlockSpec((D_FF, D_MODEL), lambda i: (0, 0)),
                  row, vec, row],
        out_specs=[row, row, vec, pl.BlockSpec((1, LANE), lambda i: (0, 0))],
        out_shape=[_sds((SEQ, D_MODEL), F32), _sds((SEQ, D_MODEL), BF16), _sds((1, D_MODEL), F32),
                   _sds((1, LANE), F32)],
        compiler_params=_params("arbitrary"),
    )(m, w_down, x2, g_post, target)


def _local_step(x, target, w_main, w_f, b_forget, conv_b, g_pre_mix, g_post_mix, g_pre_ffn, g_post_ffn,
                proj_weights, ffn_weights, ffn_grads_ready, proj_grads_ready, mixer_grads_ready, after=None):
    mm = _matmul
    tabs = _rope_tables()

    h1 = _rms_fwd(x, g_pre_mix, name="rms_pre_mix", after=after)
    zm = mm(h1, w_main, out_dtype=BF16, tm=2048, tn=1024, tk=1024, name="in_proj")
    zf = mm(h1, w_f, out_dtype=F32, tm=2048, tn=F_PAD, tk=1024, name="in_proj_forget")
    f_row, sg_row = _fox_prep(zf[:, :N_HEADS].T, b_forget.reshape(N_HEADS, 1))
    f_cols = jnp.pad(f_row.T, ((0, 0), (0, LANE - N_HEADS)))
    q_slots, k_slots, v_slots = _fox_pack_fwd(zm, f_cols)
    ya, lse_a = _fox_fwd(q_slots, k_slots, v_slots)
    qkv_d = dict(zip([d for _, d in DIL_PATTERNS], _rope_fwd(zm, tabs)))
    dil = [_dil_fwd(qkv_d[d], d) for _, d in DIL_PATTERNS]
    yb, lse_b = _dil_merge([o for o, _ in dil], [l for _, l in dil])
    w_oa, w_ob, w_out = proj_weights(yb)
    pa, pb, mixed = _mix_fwd(ya, yb, w_oa, w_ob, zm)
    y1, x2, h2 = _out_fwd(mixed, w_out, x, g_post_mix, g_pre_ffn)
    w_up, conv_w, w_down = ffn_weights(h2)
    u, ab, m = _up_conv_fwd(h2, w_up, conv_w, _ffn_interleave(conv_b))
    dout, dy2, gg_post_ffn, sq_err = _down_fwd(m, w_down, x2, g_post_ffn, target)

    g_w_down = mm(m, dy2, ta=True, out_dtype=BF16, tm=D_FF // 2, tn=1024, tk=2048, name="grad_w_down")
    du, g_conv_w, g_conv_b = _ffn_mid_bwd(dy2, w_down, u, ab, conv_w)
    g_w_up = mm(h2, du, ta=True, out_dtype=BF16, tm=1024, tn=D_FF // 2, tk=2048, name="grad_w_up")
    tok = ffn_grads_ready(dict(w_down=g_w_down, w_up_blocks=g_w_up, conv_w=_ffn_deinterleave(g_conv_w)))
    dh2 = mm(du, w_up, tb=True, out_dtype=BF16, tm=512, tn=1024, tk=2 * D_FF, name="d_h2")

    dx2, dy1, gg_pre_ffn, gg_post_mix = _rms_pair_bwd([dh2], x2, g_pre_ffn, dout, y1, g_post_mix, after=tok)
    g_w_out = mm(mixed, dy1, ta=True, out_dtype=BF16, tm=1024, tn=1024, tk=2048, name="grad_w_out")
    dmix = mm(dy1, w_out, tb=True, out_dtype=BF16, tm=2048, tn=1024, tk=1024, name="d_mixed")
    dpa, dz = _gate_bwd(dmix, zm, pa, 3, None, name="gate_bwd_fox")
    dpb, dz = _gate_bwd(dmix, zm, pb, 4, dz, name="gate_bwd_dil")
    g_w_oa = mm(ya, dpa, ta=True, out_dtype=BF16, tm=512, tn=1024, tk=SEQ, name="grad_w_o_fox", col_slots=N_DEV)
    g_w_ob = mm(yb, dpb, ta=True, out_dtype=BF16, tm=512, tn=1024, tk=SEQ, name="grad_w_o_dil", col_slots=N_DEV)
    tok = proj_grads_ready(dict(w_o_fox=g_w_oa, w_o_dil=g_w_ob, w_out=g_w_out))
    dya = mm(dpa, w_oa, tb=True, out_dtype=BF16, tm=2048, tn=512, tk=1024, name="d_y_fox")
    dyb = mm(dpb, w_ob, tb=True, out_dtype=BF16, tm=2048, tn=512, tk=1024, name="d_y_dil")

    qb_slots, do_slots = _fox_pack_bwd(zm, f_cols, lse_a, ya, dya, after=tok)
    dz, df_cols = _fox_unpack(*_fox_bwd(qb_slots, k_slots, v_slots, do_slots), dz)
    dfa_t, g_b_forget = _fox_post_bwd(df_cols[:, :N_HEADS].T, sg_row)

    rows_d = _dil_bwd_prep(yb, dyb, lse_b)
    dil_g = [_dil_bwd(qkv_d[d], *rows_d[k], d) for k, (_, d) in enumerate(DIL_PATTERNS)]
    dz = _dil_grad_combine([g[0] for g in dil_g], [g[1] for g in dil_g], [g[2] for g in dil_g], tabs, dz)

    dzf = jnp.pad(dfa_t.T, ((0, 0), (0, F_PAD - N_HEADS)))
    g_w_main = mm(h1, dz, ta=True, out_dtype=BF16, tm=1024, tn=Z_MAIN // 4, tk=2048, name="grad_w_in")
    g_w_f = mm(h1, dzf, ta=True, out_dtype=BF16, tm=1024, tn=F_PAD, tk=1024, name="grad_w_in_forget")
    tok = mixer_grads_ready(dict(w_main=g_w_main, w_f=g_w_f))
    dh1 = [mm(dz, w_main, tb=True, out_dtype=BF16, tm=512, tn=1024, tk=Z_MAIN, name="d_h1", after=tok),
           mm(dzf, w_f, tb=True, out_dtype=BF16, tm=2048, tn=1024, tk=F_PAD, name="d_h1_forget", after=tok)]
    grad_x, gg_pre_mix = _rms_bwd(dh1, x, g_pre_mix, dx2, out_dtype=F32, name="rms_pre_mix_bwd")

    grads = dict(
        b_forget=g_b_forget.reshape(1, N_HEADS), conv_b=_ffn_deinterleave(g_conv_b),
        g_pre_mix=gg_pre_mix, g_post_mix=gg_post_mix, g_pre_ffn=gg_pre_ffn, g_post_ffn=gg_post_ffn)
    return sq_err, grad_x, grads


def _gather_two_level(shard, *, name):
    def body(x_ref, out_ref, send_sems, recv_sems, local_sem):
        x, y, c = lax.axis_index("x"), lax.axis_index("y"), lax.axis_index("c")
        me, sibling = (x, y, c), (x, y, 1 - c)
        chips = [(1 - x, y), (x, 1 - y), (1 - x, 1 - y)]

        def slot(px, py, pc):
            return out_ref.at[4 * px + 2 * py + pc]

        def copy(k, block, to, src=None):
            return pltpu.make_async_remote_copy(
                src_ref=slot(*block) if src is None else src, dst_ref=slot(*block),
                send_sem=send_sems.at[k], recv_sem=recv_sems.at[k], device_id=to, device_id_type=MESH_ID)

        mine = pltpu.make_async_copy(x_ref, slot(*me), local_sem)
        mine.start()
        first = [copy(0, me, sibling, src=x_ref)]
        first += [copy(1 + j, me, (*chip, c), src=x_ref) for j, chip in enumerate(chips)]
        for cp in first:
            cp.start()
        passed = [copy(4 + j, (*chip, c), sibling) for j, chip in enumerate(chips)]
        for j, chip in enumerate(chips):
            copy(1 + j, (*chip, c), me).wait_recv()
            passed[j].start()
        copy(0, sibling, me).wait_recv()
        for j, chip in enumerate(chips):
            copy(4 + j, (*chip, 1 - c), me).wait_recv()
        for cp in first + passed:
            cp.wait_send()
        mine.wait()

    return pl.pallas_call(
        body, name=name, in_specs=[ANY], out_specs=ANY, out_shape=_sds((N_DEV,) + shard.shape, shard.dtype),
        scratch_shapes=[pltpu.SemaphoreType.DMA((N_DEV - 1,)), pltpu.SemaphoreType.DMA((N_DEV - 1,)),
                        pltpu.SemaphoreType.DMA],
    )(shard)


N_CHIPS = N_DEV // 2


def _peers(chips_only=False):
    x, y, c = lax.axis_index("x"), lax.axis_index("y"), lax.axis_index("c")
    out = []
    if chips_only:
        for k in range(1, N_CHIPS):
            px = 1 - x if k & 2 else x
            py = 1 - y if k & 1 else y
            out.append(((px, py, c), 2 * px + py))
        return 2 * x + y, out
    for k in range(1, N_DEV):
        px = 1 - x if k & 4 else x
        py = 1 - y if k & 2 else y
        pc = 1 - c if k & 1 else c
        out.append(((px, py, pc), 4 * px + 2 * py + pc))
    return 4 * x + 2 * y + c, out


def _sibling_swap(slot_arrays, *, name):
    n = len(slot_arrays)

    def body(*refs):
        ins, outs, send_sems, recv_sems = refs[:n], refs[n:2 * n], refs[2 * n], refs[2 * n + 1]
        x, y, c = lax.axis_index("x"), lax.axis_index("y"), lax.axis_index("c")
        copies = [pltpu.make_async_remote_copy(
            src_ref=ins[a].at[2 * q + (1 - c)], dst_ref=outs[a].at[q], send_sem=send_sems.at[a, q],
            recv_sem=recv_sems.at[a, q], device_id=(x, y, 1 - c), device_id_type=MESH_ID)
            for a in range(n) for q in range(N_CHIPS)]
        for cp in copies:
            cp.start()
        for cp in copies:
            cp.wait_recv()
        for cp in copies:
            cp.wait_send()

    return pl.pallas_call(
        body, name=name, in_specs=[ANY] * n, out_specs=[ANY] * n,
        out_shape=[_sds((N_CHIPS,) + t.shape[1:], t.dtype) for t in slot_arrays],
        scratch_shapes=[pltpu.SemaphoreType.DMA((n, N_CHIPS)), pltpu.SemaphoreType.DMA((n, N_CHIPS))],
    )(*slot_arrays)


def _pair_sum(slots, from_sibling, *, name, tn):
    _, r, c = slots.shape
    core = lax.axis_index("c").astype(jnp.int32).reshape(1)

    def body(core_ref, a_ref, b_ref, o_ref):
        o_ref[...] = (a_ref[...].astype(F32) + b_ref[...].astype(F32)).astype(o_ref.dtype)

    blk = lambda f: pl.BlockSpec((1, r, tn), f)
    return pl.pallas_call(
        body, name=name,
        grid_spec=pltpu.PrefetchScalarGridSpec(
            num_scalar_prefetch=1, grid=(N_CHIPS, c // tn),
            in_specs=[blk(lambda q, j, core: (2 * q + core[0], 0, j)), blk(lambda q, j, core: (q, 0, j))],
            out_specs=blk(lambda q, j, core: (q, 0, j))),
        out_shape=_sds((N_CHIPS, r, c), slots.dtype),
        compiler_params=_params("parallel", "parallel"),
    )(core, slots, from_sibling)


HBM = pl.BlockSpec(memory_space=pltpu.HBM)
SEM = pl.BlockSpec(memory_space=pltpu.SEMAPHORE)
DATAFLOW = pltpu.SideEffectType.DATAFLOW_SIDE_EFFECTING


def _split_copy(srcs, lands, send_sems, recv_sems, scatter, a, k, me, peers, incoming=False):
    dev, slot = peers[k]
    if incoming:
        src = dst = lands[a].at[slot]
    else:
        src, dst = (srcs[a].at[slot] if scatter else srcs[a]), lands[a].at[me]
    sem = a * len(peers) + k
    return pltpu.make_async_remote_copy(
        src_ref=src, dst_ref=dst, send_sem=send_sems.at[sem], recv_sem=recv_sems.at[sem],
        device_id=dev, device_id_type=MESH_ID)


def _own_copy(srcs, lands, own_sems, scatter, a, me):
    return pltpu.make_async_copy(srcs[a].at[me] if scatter else srcs[a], lands[a].at[me], own_sems.at[a])


def _exchange_start(arrays, scatter, *, name, chips_only=False, after=None):
    n = len(arrays)
    n_slots = N_CHIPS if chips_only else N_DEV
    n_in = 2 * n + len(_also(after))

    def body(*refs):
        srcs, lands = refs[:n], refs[n:2 * n]
        send_sems, recv_sems, own_sems = refs[n_in:n_in + 3]
        token = refs[-1]
        me, peers = _peers(chips_only)
        for k in range(len(peers)):
            for a in range(n):
                _split_copy(srcs, lands, send_sems, recv_sems, scatter, a, k, me, peers).start()
        for a in range(n):
            _own_copy(srcs, lands, own_sems, scatter, a, me).start()
        token[...] = jnp.zeros_like(token)

    land_shapes = [((n_slots,) + a.shape[-2:], a.dtype) for a in arrays]
    sems = pltpu.SemaphoreType.DMA((n * (n_slots - 1),))
    outs = pl.pallas_call(
        body, name=name,
        out_shape=(sems, sems, pltpu.SemaphoreType.DMA((n,)), *[pltpu.HBM(a.shape, a.dtype) for a in arrays],
                   *[pltpu.HBM(s, d) for s, d in land_shapes], _sds((SUBLANE, LANE), F32)),
        in_specs=[HBM] * (2 * n) + [ANY] * len(_also(after)),
        out_specs=(SEM, SEM, SEM, *[HBM] * (2 * n), pl.BlockSpec(memory_space=pltpu.VMEM)),
        input_output_aliases={i: 3 + i for i in range(2 * n)},
        compiler_params=pltpu.CompilerParams(has_side_effects=DATAFLOW),
    )(*[pltpu.with_memory_space_constraint(a, pltpu.HBM) for a in arrays],
      *[pltpu.with_memory_space_constraint(lax.empty(s, d), pltpu.HBM) for s, d in land_shapes], *_also(after))
    return (outs[:3], outs[3:3 + n], outs[3 + n:3 + 2 * n], scatter, chips_only), outs[-1]


def _exchange_wait(handles, after, *, name):
    sems, srcs, lands, scatter, chips_only = handles
    n = len(srcs)

    def body(*refs):
        src_refs, land_refs = refs[:n], refs[n:2 * n]
        send_ref, recv_ref, own_ref = refs[2 * n:2 * n + 3]
        me, peers = _peers(chips_only)
        for k in range(len(peers)):
            for a in range(n):
                _split_copy(src_refs, land_refs, send_ref, recv_ref, scatter, a, k, me, peers).wait_send()
                _split_copy(src_refs, land_refs, send_ref, recv_ref, scatter, a, k, me, peers, True).wait_recv()
        for a in range(n):
            _own_copy(src_refs, land_refs, own_ref, scatter, a, me).wait()

    outs = pl.pallas_call(
        body, name=name,
        out_shape=tuple(pltpu.HBM(t.shape, t.dtype) for t in (*srcs, *lands)),
        in_specs=[HBM] * (2 * n) + [SEM, SEM, SEM, pl.BlockSpec(memory_space=pl.ANY)],
        out_specs=tuple([HBM] * (2 * n)),
        input_output_aliases={i: i for i in range(2 * n)},
        compiler_params=pltpu.CompilerParams(has_side_effects=DATAFLOW),
    )(*srcs, *lands, *sems, after)
    return outs[n:]


def _adamw(parts, w, m, v, *, name, tm):
    r, c = w.shape
    assert r % tm == 0

    def body(p_ref, w_ref, m_ref, v_ref, g_ref, d_ref, nm_ref, nv_ref):
        _adamw_update(p_ref, w_ref, m_ref, v_ref, g_ref, d_ref, nm_ref, nv_ref)

    blk = pl.BlockSpec((tm, c), lambda i: (i, 0))
    return pl.pallas_call(
        body, name=name, grid=(r // tm,),
        in_specs=[pl.BlockSpec((parts.shape[0], tm, c), lambda i: (0, i, 0)), blk, blk, blk],
        out_specs=[blk] * 4, out_shape=[_sds((r, c), F32)] * 4,
        compiler_params=_params("parallel"),
    )(parts, w, m, v)


def _adamw_update(p_ref, w_ref, m_ref, v_ref, g_ref, d_ref, nm_ref, nv_ref):
    g = p_ref[0].astype(F32)
    for s in range(1, p_ref.shape[0]):
        g = g + p_ref[s].astype(F32)
    g_ref[...] = g
    m_new = ADAM_B1 * m_ref[...] + (1.0 - ADAM_B1) * g
    v_new = ADAM_B2 * v_ref[...] + (1.0 - ADAM_B2) * (g * g)
    nm_ref[...] = m_new
    nv_ref[...] = v_new
    m_hat = m_new / (1.0 - ADAM_B1 ** ADAM_STEP)
    v_hat = v_new / (1.0 - ADAM_B2 ** ADAM_STEP)
    d_ref[...] = -ADAM_LR * (m_hat / (jnp.sqrt(v_hat) + ADAM_EPS) + ADAM_WD * w_ref[...])


SMALL = ("g_pre_mix", "b_forget", "g_post_mix", "g_pre_ffn", "conv_b", "g_post_ffn")


def _adamw_small(parts, ws, ms, vs, sq_err_parts):
    n = len(ws)

    def body(*refs):
        ins, sq_ref, outs, loss_ref = refs[:4 * n], refs[4 * n], refs[4 * n + 1:-1], refs[-1]
        for i in range(n):
            _adamw_update(ins[i], ins[n + i], ins[2 * n + i], ins[3 * n + i], *outs[4 * i:4 * i + 4])
        total = sq_ref[0]
        for s in range(1, N_DEV):
            total = total + sq_ref[s]
        loss_ref[...] = total * (0.5 / D_MODEL)

    res = pl.pallas_call(
        body, name="adamw_small",
        out_shape=[_sds(w.shape, F32) for w in ws for _ in range(4)] + [_sds((1, LANE), F32)],
        compiler_params=pltpu.CompilerParams(vmem_limit_bytes=VMEM_LIMIT),
    )(*parts, *ws, *ms, *vs, sq_err_parts)
    return [res[4 * i:4 * i + 4] for i in range(n)], res[-1][0, 0]


def kernel(x, g_pre_mix, w_in, b_forget, w_o_fox, w_o_dil, w_out, g_post_mix, g_pre_ffn, w_up, conv_w, conv_b, w_down, g_post_ffn, loss_target, m_g_pre_mix, m_w_in, m_b_forget, m_w_o_fox, m_w_o_dil, m_w_out, m_g_post_mix, m_g_pre_ffn, m_w_up, m_conv_w, m_conv_b, m_w_down, m_g_post_ffn, v_g_pre_mix, v_w_in, v_b_forget, v_w_o_fox, v_w_o_dil, v_w_out, v_g_post_mix, v_g_pre_ffn, v_w_up, v_conv_w, v_conv_b, v_w_down, v_g_post_ffn):
    names = ("g_pre_mix", "w_in", "b_forget", "w_o_fox", "w_o_dil", "w_out", "g_post_mix", "g_pre_ffn",
             "w_up", "conv_w", "conv_b", "w_down", "g_post_ffn")
    w = dict(g_pre_mix=g_pre_mix, w_in=w_in, b_forget=b_forget, w_o_fox=w_o_fox, w_o_dil=w_o_dil, w_out=w_out,
             g_post_mix=g_post_mix, g_pre_ffn=g_pre_ffn, w_up=w_up, conv_w=conv_w, conv_b=conv_b, w_down=w_down,
             g_post_ffn=g_post_ffn)
    m = dict(g_pre_mix=m_g_pre_mix, w_in=m_w_in, b_forget=m_b_forget, w_o_fox=m_w_o_fox, w_o_dil=m_w_o_dil,
             w_out=m_w_out, g_post_mix=m_g_post_mix, g_pre_ffn=m_g_pre_ffn, w_up=m_w_up, conv_w=m_conv_w,
             conv_b=m_conv_b, w_down=m_w_down, g_post_ffn=m_g_post_ffn)
    v = dict(g_pre_mix=v_g_pre_mix, w_in=v_w_in, b_forget=v_b_forget, w_o_fox=v_w_o_fox, w_o_dil=v_w_o_dil,
             w_out=v_w_out, g_post_mix=v_g_post_mix, g_pre_ffn=v_g_pre_ffn, w_up=v_w_up, conv_w=v_conv_w,
             conv_b=v_conv_b, w_down=v_w_down, g_post_ffn=v_g_post_ffn)
    sharded = ("w_in", "w_o_fox", "w_o_dil", "w_out", "w_up", "w_down", "conv_w")
    wire = lambda n: F32 if n == "conv_w" else BF16

    by_cols = lambda t: jnp.transpose(t, (1, 0, 2)).reshape(t.shape[1], N_DEV * t.shape[2])
    by_rows = lambda t: t.reshape(N_DEV * t.shape[1], t.shape[2])
    col_slots = lambda t: jnp.transpose(t.reshape(t.shape[0], N_DEV, t.shape[1] // N_DEV), (1, 0, 2))
    row_slots = lambda t: t.reshape(N_DEV, t.shape[0] // N_DEV, t.shape[1])
    to_slots = lambda n, t: (row_slots if n in ("w_out", "w_down") else col_slots)(t).astype(wire(n))
    shard = lambda n: w[n][0].astype(wire(n))

    w_main, w_f = _w_in_from_shards(_gather_two_level(shard("w_in"), name="gather_w_in"))
    proj_handles, proj_tok = _exchange_start(
        [shard("w_o_fox"), shard("w_o_dil"), shard("w_out")], False, name="gather_proj_start", after=w_f)
    ffn_handles, ffn_tok = _exchange_start(
        [shard("w_up"), shard("conv_w"), shard("w_down")], False, name="gather_ffn_start", after=proj_tok)

    def proj_weights(after):
        w_oa, w_ob, w_o = _exchange_wait(proj_handles, after, name="gather_proj_wait")
        return by_cols(w_oa), by_cols(w_ob), by_rows(w_o)

    def ffn_weights(after):
        w_u, conv, w_d = _exchange_wait(ffn_handles, after, name="gather_ffn_wait")
        return _w_up_from_shards(w_u), _ffn_interleave(by_cols(conv)), by_rows(w_d)

    pending = {}

    def ffn_grads_ready(g):
        slots = [to_slots("w_down", g["w_down"]), _w_up_to_shards(g["w_up_blocks"]), to_slots("conv_w", g["conv_w"])]
        pending["ffn"] = _exchange_start(slots, True, name="scatter_ffn_start")
        return pending["ffn"][1]

    def proj_grads_ready(g):
        pending["proj"] = _exchange_start([g["w_o_fox"], g["w_o_dil"], to_slots("w_out", g["w_out"])], True,
                                          name="scatter_proj_start")
        return pending["proj"][1]

    def mixer_grads_ready(g):
        slots = _w_in_to_shards(g["w_main"], g["w_f"])
        theirs = _sibling_swap([slots], name="scatter_w_in_swap")[0]
        chip_sums = _pair_sum(slots, theirs, name="scatter_w_in_pair_sum", tn=W_IN_SHARD)
        pending["w_in"] = _exchange_start([chip_sums], True, name="scatter_w_in_start", chips_only=True)
        return pending["w_in"][1]

    sq_err, grad_x, g = _local_step(
        x[0], loss_target[0], w_main, w_f, b_forget, conv_b, g_pre_mix, g_post_mix, g_pre_ffn,
        g_post_ffn, proj_weights, ffn_weights, ffn_grads_ready, proj_grads_ready, mixer_grads_ready, after=ffn_tok)

    small_handles, small_tok = _exchange_start([g[n] for n in SMALL] + [sq_err], False, name="gather_small_start")
    tiles = dict(w_in=256, w_o_fox=512, w_o_dil=512, w_out=128, w_up=256, w_down=176, conv_w=3)
    adam = lambda n, p: _adamw(p, w[n][0], m[n][0], v[n][0], name=f"adamw_{n}", tm=tiles[n])
    res = {}
    for key, group in (("ffn", ("w_down", "w_up", "conv_w")), ("proj", ("w_o_fox", "w_o_dil", "w_out"))):
        landed = _exchange_wait(pending[key][0], small_tok, name=f"scatter_{key}_wait")
        res.update({n: adam(n, p) for n, p in zip(group, landed)})
    done = res["w_up"][3]
    res["w_in"] = adam("w_in", _exchange_wait(pending["w_in"][0], done, name="scatter_w_in_wait")[0])
    small_parts = _exchange_wait(small_handles, res["w_in"][3], name="gather_small_wait")
    small, loss = _adamw_small(small_parts[:-1], *[[t[n] for n in SMALL] for t in (w, m, v)], small_parts[-1])
    small = dict(zip(SMALL, small))
    out = [[(res[n][k][None] if n in sharded else small[n][k]) for n in names] for k in range(4)]
    return (loss, grad_x[None], *out[0], *out[1], *out[2], *out[3])
```

```python
import functools
import math

import jax
import jax.numpy as jnp
import numpy as np
from jax import lax
from jax.experimental import pallas as pl
from jax.experimental.pallas import tpu as pltpu

F32 = jnp.float32
BF16 = jnp.bfloat16

SEQ = 4096
D_MODEL = 1024
N_HEADS = 8
HEAD_DIM = 64
ATT_W = N_HEADS * HEAD_DIM
D_FF = 2816
Z_MAIN = 5120
F_PAD = 128
ROPE_DIM = 16
ROPE_THETA = 500000.0
RMS_EPS = 1e-6
NEG_INF = -1e30
SCALE = 1.0 / math.sqrt(HEAD_DIM)
DIL_PATTERNS = ((128, 1), (512, 4), (2048, 16))
DIL_BLK = 128
DIL_STEP_BLOCKS = 2
N_DEV = 8

ADAM_LR = 0.001
ADAM_B1 = 0.9
ADAM_B2 = 0.999
ADAM_EPS = 1e-08
ADAM_WD = 0.01
ADAM_STEP = 10

LANE = 128
SUBLANE = 8
VMEM_LIMIT = 56 * 1024 * 1024
MESH_ID = pl.DeviceIdType.MESH
ANY = pl.BlockSpec(memory_space=pl.ANY)


def _params(*sem):
    return pltpu.CompilerParams(dimension_semantics=sem, vmem_limit_bytes=VMEM_LIMIT)


def _sds(shape, dtype):
    return jax.ShapeDtypeStruct(shape, dtype)


def _also(after):
    return [] if after is None else [after]


def _matmul(a, b, *, ta=False, tb=False, out_dtype, tm, tn, tk, name, b_k_off=0, after=None, col_slots=1):
    n_after = len(_also(after))
    if ta:
        kk, m = a.shape
    else:
        m, kk = a.shape
    n = b.shape[0] if tb else b.shape[1]
    tm, tn, tk = min(tm, m), min(tn, n), min(tk, kk)
    assert (b.shape[1] if tb else b.shape[0]) >= b_k_off * tk + kk
    assert m % tm == 0 and n % tn == 0 and kk % tk == 0, (name, m, n, kk, tm, tn, tk)
    nk = kk // tk
    dims = (((0 if ta else 1,), (1 if tb else 0,)), ((), ()))
    slot_w = n // col_slots
    assert col_slots == 1 or (nk == 1 and tn == n and slot_w % LANE == 0), name

    def body(a_ref, b_ref, *rest):
        o_ref, scratch = rest[n_after], rest[n_after + 1:]
        p = lax.dot_general(a_ref[...].astype(BF16), b_ref[...].astype(BF16), dims,
                            preferred_element_type=F32)
        if col_slots > 1:
            for s in range(col_slots):
                o_ref[s] = p[:, s * slot_w:(s + 1) * slot_w].astype(o_ref.dtype)
        elif nk == 1:
            o_ref[...] = p.astype(o_ref.dtype)
        else:
            acc = scratch[0]
            k = pl.program_id(2)

            @pl.when(k == 0)
            def _():
                acc[...] = p

            @pl.when(k > 0)
            def _():
                acc[...] += p

            @pl.when(k == nk - 1)
            def _():
                o_ref[...] = acc[...].astype(o_ref.dtype)

    a_spec = (pl.BlockSpec((tk, tm), lambda i, j, k: (k, i)) if ta
              else pl.BlockSpec((tm, tk), lambda i, j, k: (i, k)))
    b_spec = (pl.BlockSpec((tn, tk), lambda i, j, k: (j, k + b_k_off)) if tb
              else pl.BlockSpec((tk, tn), lambda i, j, k: (k + b_k_off, j)))
    return pl.pallas_call(
        body, name=name, grid=(m // tm, n // tn, nk),
        in_specs=[a_spec, b_spec] + [ANY] * n_after,
        out_specs=(pl.BlockSpec((tm, tn), lambda i, j, k: (i, j)) if col_slots == 1
                   else pl.BlockSpec((col_slots, tm, slot_w), lambda i, j, k: (0, i, 0))),
        out_shape=_sds((m, n) if col_slots == 1 else (col_slots, m, slot_w), out_dtype),
        scratch_shapes=[pltpu.VMEM((tm, tn), F32)] if nk > 1 else [],
        compiler_params=_params("parallel", "parallel", "arbitrary"),
    )(a, b, *_also(after))


def _rms_fwd(x, g, *, name, tm=512, after=None):
    def body(x_ref, g_ref, *rest):
        h_ref = rest[-1]
        xv = x_ref[...]
        r = lax.rsqrt(jnp.mean(xv * xv, axis=-1, keepdims=True) + RMS_EPS)
        h_ref[...] = (xv * r * g_ref[...]).astype(h_ref.dtype)

    return pl.pallas_call(
        body, name=name, grid=(SEQ // tm,),
        in_specs=[pl.BlockSpec((tm, D_MODEL), lambda i: (i, 0)), pl.BlockSpec((1, D_MODEL), lambda i: (0, 0))]
        + [ANY] * len(_also(after)),
        out_specs=pl.BlockSpec((tm, D_MODEL), lambda i: (i, 0)),
        out_shape=_sds((SEQ, D_MODEL), BF16),
        compiler_params=_params("parallel"),
    )(x, g, *_also(after))


def _rms_bwd(dh_parts, xin, g, dres, *, out_dtype, name, tm=512):
    n_parts = len(dh_parts)
    has_res = dres is not None

    def body(*refs):
        parts = refs[:n_parts]
        x_ref, g_ref = refs[n_parts], refs[n_parts + 1]
        res_ref = refs[n_parts + 2] if has_res else None
        o_ref, gg_ref = refs[-2], refs[-1]
        dh = parts[0][...].astype(F32)
        for p in parts[1:]:
            dh = dh + p[...].astype(F32)
        xv = x_ref[...]
        r = lax.rsqrt(jnp.mean(xv * xv, axis=-1, keepdims=True) + RMS_EPS)
        xn = xv * r

        @pl.when(pl.program_id(0) == 0)
        def _():
            gg_ref[...] = jnp.zeros_like(gg_ref)

        gg_ref[...] += jnp.sum(dh * xn, axis=0, keepdims=True)
        dxn = dh * g_ref[...]
        dx = r * (dxn - xn * jnp.mean(dxn * xn, axis=-1, keepdims=True))
        if has_res:
            dx = dx + res_ref[...]
        o_ref[...] = dx.astype(o_ref.dtype)

    row = pl.BlockSpec((tm, D_MODEL), lambda i: (i, 0))
    vec = pl.BlockSpec((1, D_MODEL), lambda i: (0, 0))
    args = list(dh_parts) + [xin, g] + ([dres] if has_res else [])
    return pl.pallas_call(
        body, name=name, grid=(SEQ // tm,),
        in_specs=[row] * n_parts + [row, vec] + ([row] if has_res else []),
        out_specs=[row, vec],
        out_shape=[_sds((SEQ, D_MODEL), out_dtype), _sds((1, D_MODEL), F32)],
        compiler_params=_params("arbitrary"),
    )(*args)


def _rms_pair_bwd(dh_parts, x2, g_pre, dres, y1, g_post, *, tm=512, after=None):
    n_parts = len(dh_parts)

    def norm_bwd(dh, xin, g_ref, gg_ref):
        r = lax.rsqrt(jnp.mean(xin * xin, axis=-1, keepdims=True) + RMS_EPS)
        xn = xin * r
        gg_ref[...] += jnp.sum(dh * xn, axis=0, keepdims=True)
        dxn = dh * g_ref[...]
        return r * (dxn - xn * jnp.mean(dxn * xn, axis=-1, keepdims=True))

    def body(*refs):
        parts = refs[:n_parts]
        x2_ref, gpre_ref, res_ref, y1_ref, gpost_ref = refs[n_parts:n_parts + 5]
        dx2_ref, dy1_ref, ggpre_ref, ggpost_ref = refs[-4:]

        @pl.when(pl.program_id(0) == 0)
        def _():
            ggpre_ref[...] = jnp.zeros_like(ggpre_ref)
            ggpost_ref[...] = jnp.zeros_like(ggpost_ref)

        dh = parts[0][...].astype(F32)
        for p in parts[1:]:
            dh = dh + p[...].astype(F32)
        dx2 = res_ref[...] + norm_bwd(dh, x2_ref[...], gpre_ref, ggpre_ref)
        dx2_ref[...] = dx2
        dy1_ref[...] = norm_bwd(dx2, y1_ref[...], gpost_ref, ggpost_ref).astype(dy1_ref.dtype)

    row = pl.BlockSpec((tm, D_MODEL), lambda i: (i, 0))
    vec = pl.BlockSpec((1, D_MODEL), lambda i: (0, 0))
    return pl.pallas_call(
        body, name="rms_pair_bwd", grid=(SEQ // tm,),
        in_specs=[row] * n_parts + [row, vec, row, row, vec] + [ANY] * len(_also(after)),
        out_specs=[row, row, vec, vec],
        out_shape=[_sds((SEQ, D_MODEL), F32), _sds((SEQ, D_MODEL), BF16), _sds((1, D_MODEL), F32),
                   _sds((1, D_MODEL), F32)],
        compiler_params=_params("arbitrary"),
    )(*dh_parts, x2, g_pre, dres, y1, g_post, *_also(after))


SCAN_BLK = 512


def _split_dot(v, tri):
    hi = v.astype(BF16)
    r1 = v - hi.astype(F32)
    mid = r1.astype(BF16)
    lo = (r1 - mid.astype(F32)).astype(BF16)
    dot = functools.partial(jnp.dot, preferred_element_type=F32)
    return dot(hi, tri) + dot(mid, tri) + dot(lo, tri)


def _terms(v):
    hi = v.astype(BF16)
    r1 = v - hi.astype(F32)
    mid = r1.astype(BF16)
    return hi, mid, (r1 - mid.astype(F32)).astype(BF16)


def _lanes_to_rows(cols):
    pick = (lax.broadcasted_iota(jnp.int32, (N_HEADS, LANE), 0)
            == lax.broadcasted_iota(jnp.int32, (N_HEADS, LANE), 1)).astype(BF16)
    return sum(lax.dot_general(pick, t, (((1,), (1,)), ((), ())), preferred_element_type=F32) for t in _terms(cols))


def _rows_to_lanes(rows):
    eye = (lax.broadcasted_iota(jnp.int32, (LANE, LANE), 0)
           == lax.broadcasted_iota(jnp.int32, (LANE, LANE), 1)).astype(BF16)
    padded = jnp.concatenate([rows, jnp.zeros((LANE - N_HEADS, rows.shape[1]), F32)], axis=0)
    return sum(lax.dot_general(t, eye, (((0,), (0,)), ((), ())), preferred_element_type=F32) for t in _terms(padded))


def _fox_prep(zf, b_col):
    nblk = SEQ // SCAN_BLK

    def body(zf_ref, b_ref, f_ref, sg_ref):
        row = lax.broadcasted_iota(jnp.int32, (SCAN_BLK, SCAN_BLK), 0)
        col = lax.broadcasted_iota(jnp.int32, (SCAN_BLK, SCAN_BLK), 1)
        upper = (row <= col).astype(BF16)
        carry = jnp.zeros((N_HEADS, 1), F32)
        for blk in range(nblk):
            sl = pl.ds(blk * SCAN_BLK, SCAN_BLK)
            xx = _lanes_to_rows(zf_ref[sl, :]) + b_ref[...]
            e = jnp.exp(-jnp.abs(xx))
            logf = jnp.minimum(xx, 0.0) - jnp.log(1.0 + e)
            sg_ref[:, sl] = jnp.where(xx >= 0.0, e, 1.0) / (1.0 + e)
            c = _split_dot(logf, upper) + carry
            f_ref[sl, :] = _rows_to_lanes(c)
            carry = c[:, SCAN_BLK - 1:SCAN_BLK]

    return pl.pallas_call(
        body, name="fox_prep",
        out_shape=[_sds((SEQ, LANE), F32), _sds((N_HEADS, SEQ), F32)],
        compiler_params=pltpu.CompilerParams(vmem_limit_bytes=VMEM_LIMIT),
    )(zf, b_col)


def _fox_post_bwd(df_cols, sg_t):
    nblk = SEQ // SCAN_BLK

    def body(df_ref, sg_ref, dfa_ref, gb_ref):
        row = lax.broadcasted_iota(jnp.int32, (SCAN_BLK, SCAN_BLK), 0)
        col = lax.broadcasted_iota(jnp.int32, (SCAN_BLK, SCAN_BLK), 1)
        lower = (row >= col).astype(BF16)
        carry = jnp.zeros((N_HEADS, 1), F32)
        gb = jnp.zeros((N_HEADS, 1), F32)
        for blk in reversed(range(nblk)):
            sl = pl.ds(blk * SCAN_BLK, SCAN_BLK)
            c = _split_dot(_lanes_to_rows(df_ref[sl, :]), lower) + carry
            carry = c[:, 0:1]
            dfa = c * sg_ref[:, sl]
            dfa_ref[sl, :] = _rows_to_lanes(dfa)
            gb = gb + jnp.sum(dfa, axis=1, keepdims=True)
        gb_ref[...] = gb

    return pl.pallas_call(
        body, name="fox_post_bwd",
        out_shape=[_sds((SEQ, LANE), F32), _sds((N_HEADS, 1), F32)],
        compiler_params=pltpu.CompilerParams(vmem_limit_bytes=VMEM_LIMIT),
    )(df_cols, sg_t)


FOX_T = 512
NT_DIMS = (((1,), (1,)), ((), ()))
TN_DIMS = (((0,), (0,)), ((), ()))


def _head(ref_or_val, h):
    return ref_or_val[:, h * HEAD_DIM:(h + 1) * HEAD_DIM]


def _split3(v):
    hi = v.astype(BF16).astype(F32)
    r1 = v - hi
    mid = r1.astype(BF16).astype(F32)
    return hi, mid, (r1 - mid).astype(BF16).astype(F32)


ONE_LANE = 3 * N_HEADS


def _pack_terms(v, with_one):
    hi, mid, lo = _split3(v)
    t = hi + pltpu.roll(mid, N_HEADS, 1) + pltpu.roll(lo, 2 * N_HEADS, 1)
    if with_one:
        t = t + (lax.broadcasted_iota(jnp.int32, v.shape, 1) == ONE_LANE).astype(F32)
    return t.astype(BF16)


def _aux_matrices():
    to_q = np.zeros((LANE, N_HEADS * 2 * HEAD_DIM), np.float32)
    to_k = np.zeros_like(to_q)
    for h in range(N_HEADS):
        base = h * 2 * HEAD_DIM + HEAD_DIM
        for s in range(3):
            to_q[s * N_HEADS + h, base + s] = 1.0
            to_q[ONE_LANE, base + 3 + s] = 1.0
            to_k[ONE_LANE, base + s] = 1.0
            to_k[s * N_HEADS + h, base + 3 + s] = -1.0
    return jnp.asarray(to_q, BF16), jnp.asarray(to_k, BF16)


def _head_sums():
    total = np.zeros((N_HEADS * HEAD_DIM, LANE), np.float32)
    first = np.zeros_like(total)
    for h in range(N_HEADS):
        total[h * HEAD_DIM:(h + 1) * HEAD_DIM, h] = 1.0
        first[h * HEAD_DIM, h] = 1.0
    return jnp.asarray(total, BF16), jnp.asarray(first, BF16)


SLOT = 2 * HEAD_DIM
N_SPLIT = 3
FOX_FWD_HEADS = 8
FOX_BWD_HEADS = 4


def _slot(ref, h):
    return ref[:, h * SLOT:(h + 1) * SLOT]


def _fox_pack_fwd(zm, f_cols, *, tm=512):
    def body(q_ref, k_ref, v_ref, f_ref, tq_ref, tk_ref, qs_ref, ks_ref, vs_ref):
        ones = jnp.ones((tm, HEAD_DIM), BF16)
        terms = _pack_terms(f_ref[...], True)
        q_aux = jnp.dot(terms, tq_ref[...], preferred_element_type=F32).astype(BF16)
        k_aux = jnp.dot(terms, tk_ref[...], preferred_element_type=F32).astype(BF16)
        for h in range(N_HEADS):
            aux = slice(h * SLOT + HEAD_DIM, (h + 1) * SLOT)
            qs_ref[:, h * SLOT:(h + 1) * SLOT] = jnp.concatenate(
                [(_head(q_ref, h).astype(F32) * SCALE).astype(BF16), q_aux[:, aux]], axis=1)
            ks_ref[:, h * SLOT:(h + 1) * SLOT] = jnp.concatenate([_head(k_ref, h), k_aux[:, aux]], axis=1)
            vs_ref[:, h * SLOT:(h + 1) * SLOT] = jnp.concatenate([_head(v_ref, h), ones], axis=1)

    col = lambda b: pl.BlockSpec((tm, ATT_W), lambda i: (i, b))
    wide = pl.BlockSpec((tm, N_HEADS * SLOT), lambda i: (i, 0))
    const = pl.BlockSpec((LANE, N_HEADS * SLOT), lambda i: (0, 0))
    return pl.pallas_call(
        body, name="fox_pack_fwd", grid=(SEQ // tm,),
        in_specs=[col(0), col(1), col(2), pl.BlockSpec((tm, LANE), lambda i: (i, 0)), const, const],
        out_specs=[wide] * 3, out_shape=[_sds((SEQ, N_HEADS * SLOT), BF16)] * 3,
        compiler_params=_params("parallel"),
    )(zm, zm, zm, f_cols, *_aux_matrices())


def _fox_pack_bwd(zm, f_cols, lse, o, do, *, tm=512, after=None):
    def body(q_ref, f_ref, lse_ref, o_ref, do_ref, tq_ref, total_ref, first_ref, *rest):
        qs_ref, ds_ref = rest[-2:]
        delta = _split_dot(o_ref[...].astype(F32) * do_ref[...].astype(F32), total_ref[...])
        lse_h = _split_dot(lse_ref[...], first_ref[...])
        q_aux = jnp.dot(_pack_terms(f_ref[...] - lse_h, True), tq_ref[...], preferred_element_type=F32).astype(BF16)
        d_aux = jnp.dot(_pack_terms(-delta, False), tq_ref[...], preferred_element_type=F32).astype(BF16)
        for h in range(N_HEADS):
            aux = slice(h * SLOT + HEAD_DIM, (h + 1) * SLOT)
            qs_ref[:, h * SLOT:(h + 1) * SLOT] = jnp.concatenate(
                [(_head(q_ref, h).astype(F32) * SCALE).astype(BF16), q_aux[:, aux]], axis=1)
            ds_ref[:, h * SLOT:(h + 1) * SLOT] = jnp.concatenate([_head(do_ref, h), d_aux[:, aux]], axis=1)

    row = pl.BlockSpec((tm, ATT_W), lambda i: (i, 0))
    wide = pl.BlockSpec((tm, N_HEADS * SLOT), lambda i: (i, 0))
    const = lambda r, c: pl.BlockSpec((r, c), lambda i: (0, 0))
    return pl.pallas_call(
        body, name="fox_pack_bwd", grid=(SEQ // tm,),
        in_specs=[row, pl.BlockSpec((tm, LANE), lambda i: (i, 0)), row, row, row,
                  const(LANE, N_HEADS * SLOT), const(ATT_W, LANE), const(ATT_W, LANE)] + [ANY] * len(_also(after)),
        out_specs=[wide] * 2, out_shape=[_sds((SEQ, N_HEADS * SLOT), BF16)] * 2,
        compiler_params=_params("parallel"),
    )(zm, f_cols, lse, o, do, _aux_matrices()[0], *_head_sums(), *_also(after))


def _causal_pairs(key_major):
    nb = SEQ // FOX_T
    if key_major:
        pairs = [(i, j) for j in range(nb) for i in range(j, nb)]
    else:
        pairs = [(i, j) for i in range(nb) for j in range(i + 1)]
    return (jnp.array([p[0] for p in pairs], jnp.int32), jnp.array([p[1] for p in pairs], jnp.int32), len(pairs))


FOX_HALF = FOX_T // 2
FOX_FULL = ((slice(0, FOX_T), slice(0, FOX_T), None),)
FOX_DIAG = ((slice(0, FOX_HALF), slice(0, FOX_HALF), 0), (slice(FOX_HALF, FOX_T), slice(0, FOX_T), FOX_HALF))


def _causal_piece_mask(q_rows, k_rows, offset):
    shape = (q_rows.stop - q_rows.start, k_rows.stop - k_rows.start)
    row = lax.broadcasted_iota(jnp.int32, shape, 0)
    col = lax.broadcasted_iota(jnp.int32, shape, 1)
    return col <= row + offset


def _fox_fwd(q_slots, k_slots, v_slots):
    i_tab, j_tab, n_pairs = _causal_pairs(False)

    def body(i_tab, j_tab, q_ref, k_ref, v_ref, o_ref, lse_ref, m_s, acc_s):
        t = pl.program_id(1)
        i, j = i_tab[t], j_tab[t]

        @pl.when(j == 0)
        def _():
            m_s[...] = jnp.full_like(m_s, NEG_INF)
            acc_s[...] = jnp.zeros_like(acc_s)

        def step(pieces):
            jobs = [(h, piece) for h in range(FOX_FWD_HEADS) for piece in pieces]
            lanes = lambda h: slice(h * SLOT, (h + 1) * SLOT)
            scores = [lax.dot_general(q_ref[qr, lanes(h)], k_ref[kr, lanes(h)], NT_DIMS, preferred_element_type=F32)
                      for h, (qr, kr, _) in jobs]
            probs, alphas = [], []
            for idx, (h, (qr, kr, offset)) in enumerate(jobs):
                s = scores[idx]
                if offset is not None:
                    s = jnp.where(_causal_piece_mask(qr, kr, offset), s, NEG_INF)
                m_prev = m_s[h, qr, :]
                m_new = jnp.maximum(m_prev, jnp.max(s, axis=-1, keepdims=True))
                probs.append(jnp.exp(s - jnp.tile(m_new, (1, s.shape[1] // LANE))).astype(BF16))
                alphas.append(jnp.exp(m_prev - m_new))
                m_s[h, qr, :] = m_new
            for idx, (h, (qr, kr, _)) in enumerate(jobs):
                acc_s[h, qr, :] = alphas[idx] * acc_s[h, qr, :] + jnp.dot(
                    probs[idx], v_ref[kr, lanes(h)], preferred_element_type=F32)

        @pl.when(j < i)
        def _():
            step(FOX_FULL)

        @pl.when(j == i)
        def _():
            step(FOX_DIAG)
            outs, lses = [], []
            for h in range(FOX_FWD_HEADS):
                acc = acc_s[h]
                l = acc[:, HEAD_DIM:]
                outs.append(acc[:, :HEAD_DIM] / l)
                lses.append(m_s[h][:, :HEAD_DIM] + jnp.log(l))
            o_ref[...] = jnp.concatenate(outs, axis=1).astype(o_ref.dtype)
            lse_ref[...] = jnp.concatenate(lses, axis=1)

    qspec = pl.BlockSpec((FOX_T, FOX_FWD_HEADS * SLOT), lambda p, t, it, jt: (it[t], p))
    kspec = pl.BlockSpec((FOX_T, FOX_FWD_HEADS * SLOT), lambda p, t, it, jt: (jt[t], p))
    ospec = pl.BlockSpec((FOX_T, FOX_FWD_HEADS * HEAD_DIM), lambda p, t, it, jt: (it[t], p))
    return pl.pallas_call(
        body, name="fox_fwd",
        grid_spec=pltpu.PrefetchScalarGridSpec(
            num_scalar_prefetch=2, grid=(N_HEADS // FOX_FWD_HEADS, n_pairs),
            in_specs=[qspec, kspec, kspec], out_specs=[ospec, ospec],
            scratch_shapes=[pltpu.VMEM((FOX_FWD_HEADS, FOX_T, LANE), F32),
                            pltpu.VMEM((FOX_FWD_HEADS, FOX_T, SLOT), F32)]),
        out_shape=[_sds((SEQ, ATT_W), BF16), _sds((SEQ, ATT_W), F32)],
        compiler_params=_params("parallel", "arbitrary"),
    )(i_tab, j_tab, q_slots, k_slots, v_slots)


def _fox_bwd(q_slots, k_slots, v_slots, do_slots):
    i_tab, j_tab, n_pairs = _causal_pairs(True)

    def body(i_tab, j_tab, q_ref, k_ref, v_ref, do_ref, dq_ref, dk_ref, dv_ref):
        t = pl.program_id(1)
        i, j = i_tab[t], j_tab[t]

        @pl.when(t == 0)
        def _():
            dq_ref[...] = jnp.zeros_like(dq_ref)

        @pl.when(i == j)
        def _():
            dk_ref[...] = jnp.zeros_like(dk_ref)
            dv_ref[...] = jnp.zeros_like(dv_ref)

        def step(pieces):
            jobs = [(h, piece) for h in range(FOX_BWD_HEADS) for piece in pieces]
            lanes = lambda h: slice(h * SLOT, (h + 1) * SLOT)
            scores = [lax.dot_general(q_ref[qr, lanes(h)], k_ref[kr, lanes(h)], NT_DIMS, preferred_element_type=F32)
                      for h, (qr, kr, _) in jobs]
            dps = [lax.dot_general(do_ref[qr, lanes(h)], v_ref[kr, lanes(h)], NT_DIMS, preferred_element_type=F32)
                   for h, (qr, kr, _) in jobs]
            ps, dss = [], []
            for idx, (h, (qr, kr, offset)) in enumerate(jobs):
                p = jnp.exp(scores[idx])
                if offset is not None:
                    p = jnp.where(_causal_piece_mask(qr, kr, offset), p, 0.0)
                ps.append(p.astype(BF16))
                dss.append((p * dps[idx]).astype(BF16))
            for idx, (h, (qr, kr, _)) in enumerate(jobs):
                rows = pl.ds(pl.multiple_of(i * FOX_T + qr.start, FOX_HALF), qr.stop - qr.start)
                dv_ref[kr, lanes(h)] += lax.dot_general(ps[idx], do_ref[qr, lanes(h)], TN_DIMS,
                                                        preferred_element_type=F32)
                dk_ref[kr, lanes(h)] += lax.dot_general(dss[idx], q_ref[qr, lanes(h)], TN_DIMS,
                                                        preferred_element_type=F32)
                dq_ref[rows, lanes(h)] += jnp.dot(dss[idx], k_ref[kr, lanes(h)], preferred_element_type=F32)

        @pl.when(i > j)
        def _():
            step(FOX_FULL)

        @pl.when(i == j)
        def _():
            step(FOX_DIAG)

    qspec = pl.BlockSpec((FOX_T, FOX_BWD_HEADS * SLOT), lambda p, t, it, jt: (it[t], p))
    kspec = pl.BlockSpec((FOX_T, FOX_BWD_HEADS * SLOT), lambda p, t, it, jt: (jt[t], p))
    return pl.pallas_call(
        body, name="fox_bwd",
        grid_spec=pltpu.PrefetchScalarGridSpec(
            num_scalar_prefetch=2, grid=(N_HEADS // FOX_BWD_HEADS, n_pairs),
            in_specs=[qspec, kspec, kspec, qspec],
            out_specs=[pl.BlockSpec((SEQ, FOX_BWD_HEADS * SLOT), lambda p, t, it, jt: (0, p)), kspec, kspec]),
        out_shape=[_sds((SEQ, N_HEADS * SLOT), F32)] * 3,
        compiler_params=_params("arbitrary", "arbitrary"),
    )(i_tab, j_tab, q_slots, k_slots, v_slots, do_slots)


def _fox_unpack(dq_slots, dk_slots, dv_slots, dz, *, tm=512):
    def body(dq_ref, dk_ref, dv_ref, dz_in, o_ref, df_ref):
        lane = lax.broadcasted_iota(jnp.int32, (tm, LANE), 1)
        df = jnp.zeros((tm, LANE), F32)
        for h in range(N_HEADS):
            lo = h * SLOT
            for part, (ref, mult) in enumerate(((dq_ref, SCALE), (dk_ref, 1.0), (dv_ref, 1.0))):
                o_ref[:, part * ATT_W + h * HEAD_DIM:part * ATT_W + (h + 1) * HEAD_DIM] = (
                    ref[:, lo:lo + HEAD_DIM] * mult).astype(o_ref.dtype)
            rows = dq_ref[:, lo + HEAD_DIM:lo + HEAD_DIM + 1]
            cols = dk_ref[:, lo + HEAD_DIM + N_SPLIT:lo + HEAD_DIM + N_SPLIT + 1]
            df = jnp.where(lane == h, rows - cols, df)
        df_ref[...] = df

    wide = pl.BlockSpec((tm, N_HEADS * SLOT), lambda i: (i, 0))
    return pl.pallas_call(
        body, name="fox_unpack", grid=(SEQ // tm,), in_specs=[wide] * 3 + [ANY],
        out_specs=[pl.BlockSpec((tm, 3 * ATT_W), lambda i: (i, 0)), pl.BlockSpec((tm, LANE), lambda i: (i, 0))],
        out_shape=[_sds((SEQ, Z_MAIN), BF16), _sds((SEQ, LANE), F32)],
        input_output_aliases={3: 0},
        compiler_params=_params("parallel"),
    )(dq_slots, dk_slots, dv_slots, dz)


def _dil_bwd_prep(o, do, lse, *, tm=512):
    dilations = [d for _, d in DIL_PATTERNS]
    o_chunks = ATT_W // LANE

    def body(o_ref, do_ref, lse_ref, *rest):
        outs, (do_scr, lse_scr, dl_scr) = rest[:-3], rest[-3:]
        dov = do_ref[...].astype(F32)
        prod = o_ref[...].astype(F32) * dov
        lane = lax.broadcasted_iota(jnp.int32, (tm, LANE), 1)
        delta = jnp.zeros((tm, LANE), F32)
        for h in range(N_HEADS):
            delta = jnp.where(lane == h, jnp.sum(_head(prod, h), axis=1, keepdims=True), delta)
        for ch in range(o_chunks):
            do_scr[ch] = dov[:, ch * LANE:(ch + 1) * LANE]
        lse_scr[0] = lse_ref[...]
        dl_scr[0] = delta
        for k, d in enumerate(dilations):
            for scr, out in zip((do_scr, lse_scr, dl_scr), outs[3 * k:3 * k + 3]):
                _slabs_from_rows(scr, out, d)

    row = pl.BlockSpec((tm, ATT_W), lambda i: (i, 0))
    view = lambda d, w: pl.BlockSpec((tm // d, d * w), lambda i: (i, 0))
    outs = pl.pallas_call(
        body, name="dil_bwd_prep", grid=(SEQ // tm,),
        in_specs=[row, row, pl.BlockSpec((tm, LANE), lambda i: (i, 0))],
        out_specs=[view(d, w) for d in dilations for w in (ATT_W, LANE, LANE)],
        out_shape=[_sds((SEQ // d, d * w), t) for d in dilations for w, t in ((ATT_W, BF16), (LANE, F32), (LANE, F32))],
        scratch_shapes=[pltpu.VMEM((o_chunks, tm, LANE), F32), pltpu.VMEM((1, tm, LANE), F32),
                        pltpu.VMEM((1, tm, LANE), F32)],
        compiler_params=_params("parallel"),
    )(o, do, lse)
    return [outs[3 * k:3 * k + 3] for k in range(len(dilations))]


def _rope_tables():
    half = ROPE_DIM // 2
    inv_freq = np.float32(ROPE_THETA) ** (-np.arange(half, dtype=np.float32) * np.float32(2.0) / np.float32(ROPE_DIM))
    ang = np.arange(SEQ, dtype=np.float32)[:, None] * inv_freq.astype(np.float32)[None, :]
    cos, sin = jnp.asarray(np.cos(ang).astype(np.float32)), jnp.asarray(np.sin(ang).astype(np.float32))
    ones = jnp.ones((SEQ, HEAD_DIM - ROPE_DIM), F32)
    zeros = jnp.zeros((SEQ, HEAD_DIM - ROPE_DIM), F32)
    zh = jnp.zeros((SEQ, half), F32)
    c_tab = jnp.concatenate([cos, cos, ones], axis=1)
    a_tab = jnp.concatenate([-sin, zh, zeros], axis=1)
    b_tab = jnp.concatenate([zh, sin, zeros], axis=1)
    two = lambda t: jnp.concatenate([t, t], axis=1)
    return two(c_tab), two(a_tab), two(b_tab)


def _rotate(x, c_tab, a_tab, b_tab):
    return x * c_tab + pltpu.roll(x, LANE - ROPE_DIM // 2, 1) * a_tab + pltpu.roll(x, ROPE_DIM // 2, 1) * b_tab


def _rope_fwd(zm, tabs, *, tm=512):
    width = 3 * ATT_W
    dilations = [d for _, d in DIL_PATTERNS]

    def body(q_ref, k_ref, v_ref, c_ref, a_ref, b_ref, *rest):
        outs, scr = rest[:-1], rest[-1]
        per_part = ATT_W // LANE
        for part, (x_ref, mult) in enumerate(((q_ref, SCALE), (k_ref, 1.0))):
            for cc in range(per_part):
                sl = slice(cc * LANE, (cc + 1) * LANE)
                scr[part * per_part + cc] = _rotate(x_ref[:, sl].astype(F32), c_ref[...], a_ref[...], b_ref[...]) * mult
        for cc in range(per_part):
            scr[2 * per_part + cc] = v_ref[:, cc * LANE:(cc + 1) * LANE].astype(F32)
        for o_ref, d in zip(outs, dilations):
            for r in range(d):
                for ch in range(width // LANE):
                    o_ref[:, r * width + ch * LANE:r * width + (ch + 1) * LANE] = (
                        scr.at[ch][pl.ds(r, tm // d, stride=d), :].astype(o_ref.dtype))

    tab = pl.BlockSpec((tm, LANE), lambda i: (i, 0))
    col = lambda b: pl.BlockSpec((tm, ATT_W), lambda i: (i, b))
    return pl.pallas_call(
        body, name="rope_fwd", grid=(SEQ // tm,),
        in_specs=[col(3), col(4), col(5), tab, tab, tab],
        out_specs=[pl.BlockSpec((tm // d, d * width), lambda i: (i, 0)) for d in dilations],
        out_shape=[_sds((SEQ // d, d * width), BF16) for d in dilations],
        scratch_shapes=[pltpu.VMEM((width // LANE, tm, LANE), F32)],
        compiler_params=_params("parallel"),
    )(zm, zm, zm, *tabs)


def _dil_grad_combine(dqs, dks, dvs, tabs, dz, *, tm=256):
    dilations = [d for _, d in DIL_PATTERNS]
    chunks = ATT_W // LANE

    def body(*refs):
        groups = (refs[0:3], refs[3:6], refs[6:9])
        c_ref, a_ref, b_ref, _, o_ref, scr = refs[9:]

        def total(part, cc):
            acc = None
            for g, (ref, d) in enumerate(zip(groups[part], dilations)):
                term = ref[:, cc * LANE:(cc + 1) * LANE].astype(F32) if d == 1 else scr[part, g, cc]
                acc = term if acc is None else acc + term
            return acc

        for part in range(3):
            for g, (ref, d) in enumerate(zip(groups[part], dilations)):
                if d > 1:
                    _rows_from_slabs(ref, scr.at[part, g], d)
        for cc in range(chunks):
            for part in range(2):
                o_ref[:, part * ATT_W + cc * LANE:part * ATT_W + (cc + 1) * LANE] = _rotate(
                    total(part, cc), c_ref[...], -a_ref[...], -b_ref[...]).astype(o_ref.dtype)
            o_ref[:, 2 * ATT_W + cc * LANE:2 * ATT_W + (cc + 1) * LANE] = total(2, cc).astype(o_ref.dtype)

    view = lambda d: pl.BlockSpec((tm // d, d * ATT_W), lambda i: (i, 0))
    tab = pl.BlockSpec((tm, LANE), lambda i: (i, 0))
    return pl.pallas_call(
        body, name="dil_grad_combine", grid=(SEQ // tm,),
        in_specs=[view(d) for d in dilations] * 3 + [tab] * 3 + [ANY],
        out_specs=pl.BlockSpec((tm, 3 * ATT_W), lambda i: (i, 1)),
        out_shape=_sds((SEQ, Z_MAIN), BF16),
        input_output_aliases={12: 0},
        scratch_shapes=[pltpu.VMEM((3, len(dilations), chunks, tm, LANE), F32)],
        compiler_params=_params("parallel"),
    )(*dqs, *dks, *dvs, *tabs, dz)


def _dil_valid(n):
    qi = lax.broadcasted_iota(jnp.int32, (DIL_BLK, 2 * DIL_BLK), 0)
    ki = lax.broadcasted_iota(jnp.int32, (DIL_BLK, 2 * DIL_BLK), 1)
    dist = qi + DIL_BLK - ki
    return (dist >= 0) & (dist <= DIL_BLK) & ((n > 0) | (ki >= DIL_BLK))


def _dil_fwd(qkv_v, d):
    length = SEQ // d
    nb = length // DIL_BLK
    nsub = min(DIL_STEP_BLOCKS, nb)

    def body(q_ref, kp_ref, kc_ref, vp_ref, vc_ref, o_ref, lse_ref):
        m_step = pl.program_id(1)
        lane = lax.broadcasted_iota(jnp.int32, (DIL_BLK, LANE), 1)
        jobs = [(sub, h) for sub in range(nsub) for h in range(N_HEADS)]
        rows = lambda sub: slice(sub * DIL_BLK, (sub + 1) * DIL_BLK)
        cols = lambda h: slice(h * HEAD_DIM, (h + 1) * HEAD_DIM)

        def keys(prev_ref, cur_ref, sub, h):
            before = prev_ref[:, cols(h)] if sub == 0 else cur_ref[rows(sub - 1), cols(h)]
            return jnp.concatenate([before, cur_ref[rows(sub), cols(h)]], axis=0)

        scores = [lax.dot_general(q_ref[rows(sub), cols(h)], keys(kp_ref, kc_ref, sub, h), NT_DIMS,
                                  preferred_element_type=F32) for sub, h in jobs]
        ok = [_dil_valid(m_step)] + [_dil_valid(1)] * (nsub - 1)
        probs, inv_l, lse_all = [], [], [jnp.zeros((DIL_BLK, LANE), F32)] * nsub
        for idx, (sub, h) in enumerate(jobs):
            s = jnp.where(ok[sub], scores[idx], NEG_INF)
            m = jnp.max(s, axis=-1, keepdims=True)
            p = jnp.exp(s - m)
            l = jnp.sum(p, axis=-1, keepdims=True)
            probs.append(p.astype(BF16))
            inv_l.append(1.0 / l)
            lse_all[sub] = jnp.where(lane == h, m + jnp.log(l), lse_all[sub])
        outs = [jnp.dot(probs[idx], keys(vp_ref, vc_ref, sub, h), preferred_element_type=F32) * inv_l[idx]
                for idx, (sub, h) in enumerate(jobs)]
        for sub in range(nsub):
            o_ref[rows(sub), :] = jnp.concatenate(outs[sub * N_HEADS:(sub + 1) * N_HEADS], axis=1).astype(o_ref.dtype)
            lse_ref[rows(sub), :] = lse_all[sub]

    pair = lambda f: pl.BlockSpec((nsub * DIL_BLK, ATT_W), f)
    one = lambda f: pl.BlockSpec((DIL_BLK, ATT_W), f)
    before = lambda m: jnp.maximum(nsub * m - 1, 0)
    o, lse = pl.pallas_call(
        body, name=f"dil_fwd_d{d}", grid=(d, nb // nsub),
        in_specs=[pair(lambda r, m: (m, 3 * r)),
                  one(lambda r, m: (before(m), 3 * r + 1)), pair(lambda r, m: (m, 3 * r + 1)),
                  one(lambda r, m: (before(m), 3 * r + 2)), pair(lambda r, m: (m, 3 * r + 2))],
        out_specs=[pair(lambda r, m: (m, r)), pl.BlockSpec((nsub * DIL_BLK, LANE), lambda r, m: (m, r))],
        out_shape=[_sds((length, d * ATT_W), BF16), _sds((length, d * LANE), F32)],
        compiler_params=_params("parallel", "arbitrary"),
    )(qkv_v, qkv_v, qkv_v, qkv_v, qkv_v)
    return o, lse


def _rows_from_slabs(view_ref, scr, d):
    chunks, rows = scr.shape[0], scr.shape[1]
    for r in range(d):
        for ch in range(chunks):
            lo = (r * chunks + ch) * LANE
            scr.at[ch][pl.ds(r, rows // d, stride=d), :] = view_ref[:, lo:lo + LANE].astype(F32)


def _slabs_from_rows(scr, view_ref, d):
    chunks, rows = scr.shape[0], scr.shape[1]
    for r in range(d):
        for ch in range(chunks):
            lo = (r * chunks + ch) * LANE
            view_ref[:, lo:lo + LANE] = scr.at[ch][pl.ds(r, rows // d, stride=d), :].astype(view_ref.dtype)


def _dil_merge(os_, lses, *, tm=512):
    dilations = [d for _, d in DIL_PATTERNS]
    o_chunks = ATT_W // LANE

    def body(o0, o1, o2, l0, l1, l2, y_ref, lse_ref, o_scr, l_scr):
        os_nat, ls = [], []
        for g, (o_ref, l_ref, d) in enumerate(zip((o0, o1, o2), (l0, l1, l2), dilations)):
            if d == 1:
                os_nat.append(o_ref[...].astype(F32))
                ls.append(l_ref[...])
            else:
                _rows_from_slabs(o_ref, o_scr.at[g], d)
                _rows_from_slabs(l_ref, l_scr.at[g], d)
                os_nat.append(jnp.concatenate([o_scr[g, ch] for ch in range(o_chunks)], axis=1))
                ls.append(l_scr[g, 0])
        m = jnp.maximum(jnp.maximum(ls[0], ls[1]), ls[2])
        es = [jnp.exp(l - m) for l in ls]
        tot = es[0] + es[1] + es[2]
        lse_ref[...] = m + jnp.log(tot)
        alphas = [e / tot for e in es]
        outs = []
        for h in range(N_HEADS):
            acc = None
            for g in range(3):
                term = alphas[g][:, h:h + 1] * _head(os_nat[g], h)
                acc = term if acc is None else acc + term
            outs.append(acc)
        y_ref[...] = jnp.concatenate(outs, axis=1).astype(y_ref.dtype)

    row = pl.BlockSpec((tm, ATT_W), lambda i: (i, 0))
    vec = pl.BlockSpec((tm, LANE), lambda i: (i, 0))
    view = lambda d, w: pl.BlockSpec((tm // d, d * w), lambda i: (i, 0))
    return pl.pallas_call(
        body, name="dil_merge", grid=(SEQ // tm,),
        in_specs=[view(d, ATT_W) for d in dilations] + [view(d, LANE) for d in dilations], out_specs=[row, vec],
        out_shape=[_sds((SEQ, ATT_W), BF16), _sds((SEQ, LANE), F32)],
        scratch_shapes=[pltpu.VMEM((3, o_chunks, tm, LANE), F32), pltpu.VMEM((3, 1, tm, LANE), F32)],
        compiler_params=_params("parallel"),
    )(*os_, *lses)


def _dil_bwd(qkv_v, do_v, lse_v, dl_v, d):
    length = SEQ // d
    nb = length // DIL_BLK
    nsub = min(DIL_STEP_BLOCKS, nb)
    n_steps = nb // nsub

    def body(q_ref, kp_ref, kc_ref, vp_ref, vc_ref, lse_ref, dl_ref, do_ref, dq_ref, dk_ref, dv_ref, dk_s, dv_s):
        m_step = pl.program_id(1)

        @pl.when(m_step == 0)
        def _():
            dk_s[...] = jnp.zeros_like(dk_s)
            dv_s[...] = jnp.zeros_like(dv_s)

        jobs = [(sub, h) for sub in range(nsub) for h in range(N_HEADS)]
        rows = lambda sub: slice(sub * DIL_BLK, (sub + 1) * DIL_BLK)
        cols = lambda h: slice(h * HEAD_DIM, (h + 1) * HEAD_DIM)

        def keys(prev_ref, cur_ref, sub, h):
            before = prev_ref[:, cols(h)] if sub == 0 else cur_ref[rows(sub - 1), cols(h)]
            return jnp.concatenate([before, cur_ref[rows(sub), cols(h)]], axis=0)

        kks = [keys(kp_ref, kc_ref, sub, h) for sub, h in jobs]
        scores = [lax.dot_general(q_ref[rows(sub), cols(h)], kks[idx], NT_DIMS, preferred_element_type=F32)
                  for idx, (sub, h) in enumerate(jobs)]
        dps = [lax.dot_general(do_ref[rows(sub), cols(h)], keys(vp_ref, vc_ref, sub, h), NT_DIMS,
                               preferred_element_type=F32) for sub, h in jobs]
        ok = [_dil_valid(m_step)] + [_dil_valid(1)] * (nsub - 1)
        ps, dss = [], []
        for idx, (sub, h) in enumerate(jobs):
            p = jnp.where(ok[sub], jnp.exp(scores[idx] - lse_ref[rows(sub), h:h + 1]), 0.0)
            ps.append(p.astype(BF16))
            dss.append((p * (dps[idx] - dl_ref[rows(sub), h:h + 1])).astype(BF16))
        dqs = [jnp.dot(dss[idx], kks[idx], preferred_element_type=F32) * SCALE for idx in range(len(jobs))]
        dkks = [lax.dot_general(dss[idx], q_ref[rows(sub), cols(h)], TN_DIMS, preferred_element_type=F32)
                for idx, (sub, h) in enumerate(jobs)]
        dvvs = [lax.dot_general(ps[idx], do_ref[rows(sub), cols(h)], TN_DIMS, preferred_element_type=F32)
                for idx, (sub, h) in enumerate(jobs)]
        for sub in range(nsub):
            dq_ref[rows(sub), :] = jnp.concatenate(dqs[sub * N_HEADS:(sub + 1) * N_HEADS], axis=1).astype(dq_ref.dtype)
        base = m_step * (nsub * DIL_BLK)
        blocks = [pl.ds(pl.multiple_of(jnp.maximum(base - DIL_BLK, 0), DIL_BLK), DIL_BLK)]
        blocks += [pl.ds(pl.multiple_of(base + s * DIL_BLK, DIL_BLK), DIL_BLK) for s in range(nsub)]
        for acc, parts in ((dk_s, dkks), (dv_s, dvvs)):
            top = lambda sub: jnp.concatenate([parts[sub * N_HEADS + h][:DIL_BLK] for h in range(N_HEADS)], axis=1)
            bottom = lambda sub: jnp.concatenate([parts[sub * N_HEADS + h][DIL_BLK:] for h in range(N_HEADS)], axis=1)
            acc[blocks[0], :] += top(0)
            for s in range(nsub):
                acc[blocks[s + 1], :] += bottom(s) + top(s + 1) if s + 1 < nsub else bottom(s)

        @pl.when(m_step == n_steps - 1)
        def _():
            dk_ref[...] = dk_s[...].astype(dk_ref.dtype)
            dv_ref[...] = dv_s[...].astype(dv_ref.dtype)

    pair = lambda f: pl.BlockSpec((nsub * DIL_BLK, ATT_W), f)
    one = lambda f: pl.BlockSpec((DIL_BLK, ATT_W), f)
    vec = lambda f: pl.BlockSpec((nsub * DIL_BLK, LANE), f)
    whole = pl.BlockSpec((length, ATT_W), lambda r, m: (0, r))
    before = lambda m: jnp.maximum(nsub * m - 1, 0)
    outs = pl.pallas_call(
        body, name=f"dil_bwd_d{d}", grid=(d, n_steps),
        in_specs=[pair(lambda r, m: (m, 3 * r)),
                  one(lambda r, m: (before(m), 3 * r + 1)), pair(lambda r, m: (m, 3 * r + 1)),
                  one(lambda r, m: (before(m), 3 * r + 2)), pair(lambda r, m: (m, 3 * r + 2)),
                  vec(lambda r, m: (m, r)), vec(lambda r, m: (m, r)), pair(lambda r, m: (m, r))],
        out_specs=[pair(lambda r, m: (m, r)), whole, whole],
        out_shape=[_sds((length, d * ATT_W), BF16)] * 3,
        scratch_shapes=[pltpu.VMEM((length, ATT_W), F32), pltpu.VMEM((length, ATT_W), F32)],
        compiler_params=_params("arbitrary", "arbitrary"),
    )(qkv_v, qkv_v, qkv_v, qkv_v, qkv_v, lse_v, dl_v, do_v)
    return outs


def _sigmoid(x):
    return 1.0 / (1.0 + jnp.exp(-x))


def _mix_fwd(ya, yb, w_oa, w_ob, zm, *, tm=512):
    def body(ya_ref, yb_ref, wa_ref, wb_ref, ga_ref, gb_ref, pa_ref, pb_ref, mix_ref):
        pa = jnp.dot(ya_ref[...], wa_ref[...], preferred_element_type=F32)
        pb = jnp.dot(yb_ref[...], wb_ref[...], preferred_element_type=F32)
        pa_ref[...] = pa.astype(pa_ref.dtype)
        pb_ref[...] = pb.astype(pb_ref.dtype)
        mix_ref[...] = (_sigmoid(ga_ref[...].astype(F32)) * pa + _sigmoid(gb_ref[...].astype(F32)) * pb
                        ).astype(mix_ref.dtype)

    row = pl.BlockSpec((tm, ATT_W), lambda i: (i, 0))
    wsp = pl.BlockSpec((ATT_W, D_MODEL), lambda i: (0, 0))
    wide = pl.BlockSpec((tm, D_MODEL), lambda i: (i, 0))
    return pl.pallas_call(
        body, name="mix_fwd", grid=(SEQ // tm,),
        in_specs=[row, row, wsp, wsp, pl.BlockSpec((tm, D_MODEL), lambda i: (i, 3)),
                  pl.BlockSpec((tm, D_MODEL), lambda i: (i, 4))],
        out_specs=[wide] * 3, out_shape=[_sds((SEQ, D_MODEL), BF16)] * 3,
        compiler_params=_params("parallel"),
    )(ya, yb, w_oa, w_ob, zm, zm)


def _gate_bwd(dmix, zm, p, gate_block, dz, *, name, tm=512):
    def body(dm_ref, g_ref, p_ref, *rest):
        dp_ref, dz_ref = rest[-2], rest[-1]
        dm = dm_ref[...].astype(F32)
        s = _sigmoid(g_ref[...].astype(F32))
        dp_ref[...] = (dm * s).astype(dp_ref.dtype)
        dz_ref[...] = (dm * p_ref[...].astype(F32) * s * (1.0 - s)).astype(dz_ref.dtype)

    wide = pl.BlockSpec((tm, D_MODEL), lambda i: (i, 0))
    gate = pl.BlockSpec((tm, D_MODEL), lambda i: (i, gate_block))
    extra = [] if dz is None else [dz]
    return pl.pallas_call(
        body, name=name, grid=(SEQ // tm,),
        in_specs=[wide, gate, wide] + [ANY] * len(extra),
        out_specs=[wide, gate],
        out_shape=[_sds((SEQ, D_MODEL), BF16), _sds((SEQ, Z_MAIN), BF16)],
        input_output_aliases={3: 1} if extra else {},
        compiler_params=_params("parallel"),
    )(dmix, zm, p, *extra)


def _out_fwd(mixed, w_out, x, g_post, g_pre, *, tm=512):
    def body(m_ref, w_ref, x_ref, gp_ref, gn_ref, y_ref, x2_ref, h_ref):
        y = jnp.dot(m_ref[...], w_ref[...], preferred_element_type=F32)
        y_ref[...] = y
        r = lax.rsqrt(jnp.mean(y * y, axis=-1, keepdims=True) + RMS_EPS)
        x2 = x_ref[...] + y * r * gp_ref[...]
        x2_ref[...] = x2
        r2 = lax.rsqrt(jnp.mean(x2 * x2, axis=-1, keepdims=True) + RMS_EPS)
        h_ref[...] = (x2 * r2 * gn_ref[...]).astype(h_ref.dtype)

    row = pl.BlockSpec((tm, D_MODEL), lambda i: (i, 0))
    vec = pl.BlockSpec((1, D_MODEL), lambda i: (0, 0))
    return pl.pallas_call(
        body, name="out_fwd", grid=(SEQ // tm,),
        in_specs=[row, pl.BlockSpec((D_MODEL, D_MODEL), lambda i: (0, 0)), row, vec, vec],
        out_specs=[row] * 3,
        out_shape=[_sds((SEQ, D_MODEL), F32), _sds((SEQ, D_MODEL), F32), _sds((SEQ, D_MODEL), BF16)],
        compiler_params=_params("parallel"),
    )(mixed, w_out, x, g_post, g_pre)


FFN_HALF = 256
FFN_TN = 2 * FFN_HALF
FFN_NJ = D_FF // FFN_HALF
FFN_GROUP = 2 * SUBLANE
UP_TM = 1024


def _ffn_interleave(t):
    lead = t.shape[:-1]
    return jnp.swapaxes(t.reshape(*lead, 2, FFN_NJ, FFN_HALF), -3, -2).reshape(*lead, 2 * D_FF)


def _ffn_deinterleave(t):
    lead = t.shape[:-1]
    return jnp.swapaxes(t.reshape(*lead, FFN_NJ, 2, FFN_HALF), -3, -2).reshape(*lead, 2 * D_FF)


W_IN_SHARD = (Z_MAIN + N_HEADS) // N_DEV
FORGET_LO = 3 * ATT_W


def _w_in_from_shards(shards, *, tm=256):
    def columns(g_ref, lo, width):
        p, off = divmod(lo, W_IN_SHARD)
        if off + width <= W_IN_SHARD:
            return g_ref[p, :, off:off + width]
        first = W_IN_SHARD - off
        return jnp.concatenate([g_ref[p, :, off:], g_ref[p + 1, :, :width - first]], axis=1)

    def body(g_ref, main_ref, f_ref):
        for t in range(Z_MAIN // LANE):
            lo = t * LANE
            main_ref[:, lo:lo + LANE] = columns(g_ref, lo if lo < FORGET_LO else lo + N_HEADS, LANE)
        f_ref[...] = jnp.concatenate([columns(g_ref, FORGET_LO, N_HEADS),
                                      jnp.zeros((tm, F_PAD - N_HEADS), f_ref.dtype)], axis=1)

    return pl.pallas_call(
        body, name="w_in_from_shards", grid=(D_MODEL // tm,),
        in_specs=[pl.BlockSpec((N_DEV, tm, W_IN_SHARD), lambda i: (0, i, 0))],
        out_specs=[pl.BlockSpec((tm, Z_MAIN), lambda i: (i, 0)), pl.BlockSpec((tm, F_PAD), lambda i: (i, 0))],
        out_shape=[_sds((D_MODEL, Z_MAIN), shards.dtype), _sds((D_MODEL, F_PAD), shards.dtype)],
        compiler_params=_params("parallel"),
    )(shards)


def _w_in_to_shards(g_main, g_f, *, tm=256):
    def natural(main_ref, f_ref, lo, width):
        pieces, hi = [], lo + width
        for ref, start, stop, shift in ((main_ref, 0, FORGET_LO, 0), (f_ref, FORGET_LO, FORGET_LO + N_HEADS, FORGET_LO),
                                        (main_ref, FORGET_LO + N_HEADS, Z_MAIN + N_HEADS, N_HEADS)):
            a, b = max(lo, start), min(hi, stop)
            if a < b:
                pieces.append(ref[:, a - shift:b - shift])
        return pieces[0] if len(pieces) == 1 else jnp.concatenate(pieces, axis=1)

    def body(main_ref, f_ref, o_ref):
        for p in range(N_DEV):
            for q in range(-(-W_IN_SHARD // LANE)):
                width = min(LANE, W_IN_SHARD - q * LANE)
                o_ref[p, :, q * LANE:q * LANE + width] = natural(main_ref, f_ref, p * W_IN_SHARD + q * LANE, width)

    return pl.pallas_call(
        body, name="w_in_to_shards", grid=(D_MODEL // tm,),
        in_specs=[pl.BlockSpec((tm, Z_MAIN), lambda i: (i, 0)), pl.BlockSpec((tm, F_PAD), lambda i: (i, 0))],
        out_specs=pl.BlockSpec((N_DEV, tm, W_IN_SHARD), lambda i: (0, i, 0)),
        out_shape=_sds((N_DEV, D_MODEL, W_IN_SHARD), g_main.dtype),
        compiler_params=_params("parallel"),
    )(g_main, g_f)


W_UP_SHARD = 2 * D_FF // N_DEV


def _w_up_lane_tile(k):
    block = k // 2
    return (2 * (block % FFN_NJ) + block // FFN_NJ) * FFN_HALF + (k % 2) * LANE


def _w_up_from_shards(shards, *, tm=256):
    def body(g_ref, o_ref):
        for k in range(2 * D_FF // LANE):
            p, off = divmod(k * LANE, W_UP_SHARD)
            if off + LANE <= W_UP_SHARD:
                tile = g_ref[p, :, off:off + LANE]
            else:
                tile = jnp.concatenate([g_ref[p, :, off:], g_ref[p + 1, :, :off + LANE - W_UP_SHARD]], axis=1)
            dst = _w_up_lane_tile(k)
            o_ref[:, dst:dst + LANE] = tile

    return pl.pallas_call(
        body, name="w_up_from_shards", grid=(D_MODEL // tm,),
        in_specs=[pl.BlockSpec((N_DEV, tm, W_UP_SHARD), lambda i: (0, i, 0))],
        out_specs=pl.BlockSpec((tm, 2 * D_FF), lambda i: (i, 0)),
        out_shape=_sds((D_MODEL, 2 * D_FF), shards.dtype),
        compiler_params=_params("parallel"),
    )(shards)


def _w_up_to_shards(t, *, tm=256):
    def body(x_ref, o_ref):
        for p in range(N_DEV):
            for q in range(-(-W_UP_SHARD // LANE)):
                width = min(LANE, W_UP_SHARD - q * LANE)
                k, off = divmod(p * W_UP_SHARD + q * LANE, LANE)
                src = _w_up_lane_tile(k)
                if off == 0:
                    tile = x_ref[:, src:src + width]
                else:
                    tile = x_ref[:, src + off:src + LANE]
                    if width > LANE - off:
                        nxt = _w_up_lane_tile(k + 1)
                        tile = jnp.concatenate([tile, x_ref[:, nxt:nxt + width - (LANE - off)]], axis=1)
                o_ref[p, :, q * LANE:q * LANE + width] = tile

    return pl.pallas_call(
        body, name="w_up_to_shards", grid=(D_MODEL // tm,),
        in_specs=[pl.BlockSpec((tm, 2 * D_FF), lambda i: (i, 0))],
        out_specs=pl.BlockSpec((N_DEV, tm, W_UP_SHARD), lambda i: (0, i, 0)),
        out_shape=_sds((N_DEV, D_MODEL, W_UP_SHARD), t.dtype),
        compiler_params=_params("parallel"),
    )(t)


def _gelu_parts(a):
    c = math.sqrt(2.0 / math.pi)
    a2 = a * a
    t = jnp.tanh((c * a) * (1.0 + 0.044715 * a2))
    half_a, one_t = 0.5 * a, 1.0 + t
    gelu = half_a * one_t
    dgelu = 0.5 * one_t + half_a * (1.0 - t * t) * (c + (3.0 * 0.044715 * c) * a2)
    return gelu, dgelu


def _row_masks(down):
    row = lax.broadcasted_iota(jnp.int32, (SUBLANE, FFN_TN), 0)
    return (row < 1, row < 2) if down else (row >= SUBLANE - 1, row >= SUBLANE - 2)


def _rolled(x, down):
    return (pltpu.roll(x, 1, 0), pltpu.roll(x, 2, 0)) if down else (
        pltpu.roll(x, SUBLANE - 1, 0), pltpu.roll(x, SUBLANE - 2, 0))


def _shifted(cur_rolled, neighbour_rolled, masks):
    return (jnp.where(masks[0], neighbour_rolled[0], cur_rolled[0]),
            jnp.where(masks[1], neighbour_rolled[1], cur_rolled[1]))


def _conv_consts(w_ref, b_ref):
    shape = (SUBLANE, FFN_TN)
    return [jnp.broadcast_to(w_ref[k:k + 1, :], shape) for k in range(3)] + [jnp.broadcast_to(b_ref[...], shape)]


def _up_conv_fwd(h2, w_up, conv_w, conv_b):
    nrow = SEQ // UP_TM
    n_tiles = FFN_NJ * nrow
    n_groups = UP_TM // FFN_GROUP

    def body(h_ref, wu_ref, w_ref, b_ref, u_ref, ab_ref, m_ref, ua_s, ub_s, c1_s, c2_s):
        k = pl.program_id(0)

        @pl.when(k == 0)
        def _():
            ub_s[...] = jnp.zeros_like(ub_s)

        @pl.when(jnp.maximum(k - 1, 0) % nrow == 0)
        def _():
            c1_s[...] = jnp.zeros_like(c1_s)
            c2_s[...] = jnp.zeros_like(c2_s)

        def step(write_s, read_s):
            h_rows = pl.ds(pl.multiple_of((this(k) % nrow) * UP_TM, UP_TM), UP_TM)
            u = jnp.dot(h_ref[h_rows, :], wu_ref[...], preferred_element_type=F32)
            write_s[...] = u
            u_ref[...] = u.astype(u_ref.dtype)
            w0, w1, w2, bias = _conv_consts(w_ref, b_ref)
            masks = _row_masks(True)
            above = (c1_s[...], c2_s[...])
            for g in range(n_groups):
                rows = slice(g * FFN_GROUP, (g + 1) * FFN_GROUP)
                x = read_s[rows, :]
                convs = []
                for c in range(2):
                    cur = x[c * SUBLANE:(c + 1) * SUBLANE]
                    cur_rolled = _rolled(cur, True)
                    s1, s2 = _shifted(cur_rolled, above, masks)
                    convs.append(w0 * s2 + w1 * s1 + w2 * cur + bias)
                    above = cur_rolled
                y = jnp.concatenate(convs, axis=0)
                ab_ref[rows, :] = y.astype(ab_ref.dtype)
                m_ref[rows, :] = (_gelu_parts(y[:, :FFN_HALF])[0] * y[:, FFN_HALF:]).astype(m_ref.dtype)
            c1_s[...], c2_s[...] = above

        @pl.when(k % 2 == 0)
        def _():
            step(ua_s, ub_s)

        @pl.when(k % 2 == 1)
        def _():
            step(ub_s, ua_s)

    this = lambda k: jnp.minimum(k, n_tiles - 1)
    last = lambda k: jnp.maximum(k - 1, 0)
    blk = lambda tile: pl.BlockSpec((UP_TM, FFN_TN), lambda k: (tile(k) % nrow, tile(k) // nrow))
    return pl.pallas_call(
        body, name="up_conv_fwd", grid=(n_tiles + 1,),
        in_specs=[pl.BlockSpec((SEQ, D_MODEL), lambda k: (0, 0)),
                  pl.BlockSpec((D_MODEL, FFN_TN), lambda k: (0, this(k) // nrow)),
                  pl.BlockSpec((3, FFN_TN), lambda k: (0, last(k) // nrow)),
                  pl.BlockSpec((1, FFN_TN), lambda k: (0, last(k) // nrow))],
        out_specs=[blk(this), blk(last), pl.BlockSpec((UP_TM, FFN_HALF), lambda k: (last(k) % nrow, last(k) // nrow))],
        out_shape=[_sds((SEQ, 2 * D_FF), BF16), _sds((SEQ, 2 * D_FF), BF16), _sds((SEQ, D_FF), BF16)],
        scratch_shapes=[pltpu.VMEM((UP_TM, FFN_TN), F32), pltpu.VMEM((UP_TM, FFN_TN), F32),
                        pltpu.VMEM((SUBLANE, FFN_TN), F32), pltpu.VMEM((SUBLANE, FFN_TN), F32)],
        compiler_params=_params("arbitrary"),
    )(h2, w_up, conv_w, conv_b)


def _ffn_mid_bwd(dy2, w_down, u, ab, conv_w):
    nrow = SEQ // UP_TM
    n_tiles = FFN_NJ * nrow
    n_groups = UP_TM // FFN_GROUP
    this = lambda k: jnp.minimum(k, n_tiles - 1)
    last = lambda k: jnp.maximum(k - 1, 0)
    row_of = lambda tile: nrow - 1 - tile % nrow

    def body(dy_ref, wd_ref, u_ref, ab_ref, w_ref, du_ref, gw_ref, gb_ref, c_s, dma_s, dmb_s):
        k = pl.program_id(0)

        @pl.when(k == 0)
        def _():
            dmb_s[...] = jnp.zeros_like(dmb_s)

        @pl.when(last(k) % nrow == 0)
        def _():
            c_s[...] = jnp.zeros_like(c_s)
            gw_ref[...] = jnp.zeros_like(gw_ref)
            gb_ref[...] = jnp.zeros_like(gb_ref)

        def step(write_s, read_s):
            dy_rows = pl.ds(pl.multiple_of(row_of(this(k)) * UP_TM, UP_TM), UP_TM)
            write_s[...] = lax.dot_general(dy_ref[dy_rows, :], wd_ref[...], NT_DIMS,
                                           preferred_element_type=F32)
            taps = [jnp.broadcast_to(w_ref[t:t + 1, :], (SUBLANE, FFN_TN)) for t in range(3)]
            masks = _row_masks(False)
            below = _rolled(c_s[...], False)
            acc = [jnp.zeros((SUBLANE, FFN_TN), F32)] * 4
            for g in reversed(range(n_groups)):
                rows = slice(g * FFN_GROUP, (g + 1) * FFN_GROUP)
                x, y, dmv = u_ref[rows, :].astype(F32), ab_ref[rows, :].astype(F32), read_s[rows, :]
                gelu, dgelu = _gelu_parts(y[:, :FFN_HALF])
                d = jnp.concatenate([dmv * y[:, FFN_HALF:] * dgelu, dmv * gelu], axis=1)
                pre = [None, None]
                for c in (1, 0):
                    sl = slice(c * SUBLANE, (c + 1) * SUBLANE)
                    cur, xs = d[sl], x[sl]
                    cur_rolled = _rolled(cur, False)
                    up1, up2 = _shifted(cur_rolled, below, masks)
                    acc = [acc[0] + up2 * xs, acc[1] + up1 * xs, acc[2] + cur * xs, acc[3] + cur]
                    pre[c] = taps[2] * cur + taps[1] * up1 + taps[0] * up2
                    below = cur_rolled
                du_ref[rows, :] = jnp.concatenate(pre, axis=0).astype(du_ref.dtype)
            c_s[...] = pltpu.roll(below[0], 1, 0)
            for t in range(3):
                gw_ref[t:t + 1, :] += jnp.sum(acc[t], axis=0, keepdims=True)
            gb_ref[...] += jnp.sum(acc[3], axis=0, keepdims=True)

        @pl.when(k % 2 == 0)
        def _():
            step(dma_s, dmb_s)

        @pl.when(k % 2 == 1)
        def _():
            step(dmb_s, dma_s)

    blk = pl.BlockSpec((UP_TM, FFN_TN), lambda k: (row_of(last(k)), last(k) // nrow))
    col = lambda rows: pl.BlockSpec((rows, FFN_TN), lambda k: (0, last(k) // nrow))
    return pl.pallas_call(
        body, name="ffn_mid_bwd", grid=(n_tiles + 1,),
        in_specs=[pl.BlockSpec((SEQ, D_MODEL), lambda k: (0, 0)),
                  pl.BlockSpec((FFN_HALF, D_MODEL), lambda k: (this(k) // nrow, 0)), blk, blk, col(3)],
        out_specs=[blk, col(3), col(1)],
        out_shape=[_sds((SEQ, 2 * D_FF), BF16), _sds((3, 2 * D_FF), F32), _sds((1, 2 * D_FF), F32)],
        scratch_shapes=[pltpu.VMEM((SUBLANE, FFN_TN), F32), pltpu.VMEM((UP_TM, FFN_HALF), F32),
                        pltpu.VMEM((UP_TM, FFN_HALF), F32)],
        compiler_params=_params("arbitrary"),
    )(dy2, w_down, u, ab, conv_w)


def _down_fwd(m, w_down, x2, g_post, target, *, tm=512):
    def body(m_ref, w_ref, x2_ref, g_ref, t_ref, dout_ref, dy_ref, gg_ref, loss_ref):
        @pl.when(pl.program_id(0) == 0)
        def _():
            gg_ref[...] = jnp.zeros_like(gg_ref)
            loss_ref[...] = jnp.zeros_like(loss_ref)

        y = jnp.dot(m_ref[...], w_ref[...], preferred_element_type=F32)
        r = lax.rsqrt(jnp.mean(y * y, axis=-1, keepdims=True) + RMS_EPS)
        yn = y * r
        diff = (x2_ref[...] + yn * g_ref[...]) - t_ref[...]
        loss_ref[...] += jnp.sum(diff * diff)
        dout = diff * (1.0 / D_MODEL)
        dout_ref[...] = dout
        gg_ref[...] += jnp.sum(dout * yn, axis=0, keepdims=True)
        dn = dout * g_ref[...]
        dy_ref[...] = (r * (dn - yn * jnp.mean(dn * yn, axis=-1, keepdims=True))).astype(dy_ref.dtype)

    row = pl.BlockSpec((tm, D_MODEL), lambda i: (i, 0))
    vec = pl.BlockSpec((1, D_MODEL), lambda i: (0, 0))
    return pl.pallas_call(
        body, name="down_fwd", grid=(SEQ // tm,),
        in_specs=[pl.BlockSpec((tm, D_FF), lambda i: (i, 0)), pl.BlockSpec((D_FF, D_MODEL), lambda i: (0, 0)),
                  row, vec, row],
        out_specs=[row, row, vec, pl.BlockSpec((1, LANE), lambda i: (0, 0))],
        out_shape=[_sds((SEQ, D_MODEL), F32), _sds((SEQ, D_MODEL), BF16), _sds((1, D_MODEL), F32),
                   _sds((1, LANE), F32)],
        compiler_params=_params("arbitrary"),
    )(m, w_down, x2, g_post, target)


def _local_step(x, target, w_main, w_f, b_forget, conv_b, g_pre_mix, g_post_mix, g_pre_ffn, g_post_ffn,
                proj_weights, ffn_weights, ffn_grads_ready, proj_grads_ready, mixer_grads_ready, after=None):
    mm = _matmul
    tabs = _rope_tables()

    h1 = _rms_fwd(x, g_pre_mix, name="rms_pre_mix", after=after)
    zm = mm(h1, w_main, out_dtype=BF16, tm=2048, tn=1024, tk=1024, name="in_proj")
    zf = mm(h1, w_f, out_dtype=F32, tm=2048, tn=F_PAD, tk=1024, name="in_proj_forget")
    f_cols, sg_row = _fox_prep(zf, b_forget.reshape(N_HEADS, 1))
    q_slots, k_slots, v_slots = _fox_pack_fwd(zm, f_cols)
    ya, lse_a = _fox_fwd(q_slots, k_slots, v_slots)
    qkv_d = dict(zip([d for _, d in DIL_PATTERNS], _rope_fwd(zm, tabs)))
    dil = [_dil_fwd(qkv_d[d], d) for _, d in DIL_PATTERNS]
    yb, lse_b = _dil_merge([o for o, _ in dil], [l for _, l in dil])
    w_oa, w_ob, w_out = proj_weights(yb)
    pa, pb, mixed = _mix_fwd(ya, yb, w_oa, w_ob, zm)
    y1, x2, h2 = _out_fwd(mixed, w_out, x, g_post_mix, g_pre_ffn)
    w_up, conv_w, w_down = ffn_weights(h2)
    u, ab, m = _up_conv_fwd(h2, w_up, conv_w, _ffn_interleave(conv_b))
    dout, dy2, gg_post_ffn, sq_err = _down_fwd(m, w_down, x2, g_post_ffn, target)

    g_w_down = mm(m, dy2, ta=True, out_dtype=BF16, tm=D_FF // 2, tn=1024, tk=2048, name="grad_w_down")
    du, g_conv_w, g_conv_b = _ffn_mid_bwd(dy2, w_down, u, ab, conv_w)
    g_w_up = mm(h2, du, ta=True, out_dtype=BF16, tm=1024, tn=D_FF // 2, tk=2048, name="grad_w_up")
    tok = ffn_grads_ready(dict(w_down=g_w_down, w_up_blocks=g_w_up, conv_w=_ffn_deinterleave(g_conv_w)))
    dh2 = mm(du, w_up, tb=True, out_dtype=BF16, tm=512, tn=1024, tk=2 * D_FF, name="d_h2")

    dx2, dy1, gg_pre_ffn, gg_post_mix = _rms_pair_bwd([dh2], x2, g_pre_ffn, dout, y1, g_post_mix, after=tok)
    g_w_out = mm(mixed, dy1, ta=True, out_dtype=BF16, tm=1024, tn=1024, tk=2048, name="grad_w_out")
    dmix = mm(dy1, w_out, tb=True, out_dtype=BF16, tm=2048, tn=1024, tk=1024, name="d_mixed")
    dpa, dz = _gate_bwd(dmix, zm, pa, 3, None, name="gate_bwd_fox")
    dpb, dz = _gate_bwd(dmix, zm, pb, 4, dz, name="gate_bwd_dil")
    g_w_oa = mm(ya, dpa, ta=True, out_dtype=BF16, tm=512, tn=1024, tk=SEQ, name="grad_w_o_fox", col_slots=N_DEV)
    g_w_ob = mm(yb, dpb, ta=True, out_dtype=BF16, tm=512, tn=1024, tk=SEQ, name="grad_w_o_dil", col_slots=N_DEV)
    tok = proj_grads_ready(dict(w_o_fox=g_w_oa, w_o_dil=g_w_ob, w_out=g_w_out))
    dya = mm(dpa, w_oa, tb=True, out_dtype=BF16, tm=2048, tn=512, tk=1024, name="d_y_fox")
    dyb = mm(dpb, w_ob, tb=True, out_dtype=BF16, tm=2048, tn=512, tk=1024, name="d_y_dil")

    qb_slots, do_slots = _fox_pack_bwd(zm, f_cols, lse_a, ya, dya, after=tok)
    dz, df_cols = _fox_unpack(*_fox_bwd(qb_slots, k_slots, v_slots, do_slots), dz)
    dzf, g_b_forget = _fox_post_bwd(df_cols, sg_row)

    rows_d = _dil_bwd_prep(yb, dyb, lse_b)
    dil_g = [_dil_bwd(qkv_d[d], *rows_d[k], d) for k, (_, d) in enumerate(DIL_PATTERNS)]
    dz = _dil_grad_combine([g[0] for g in dil_g], [g[1] for g in dil_g], [g[2] for g in dil_g], tabs, dz)

    g_w_main = mm(h1, dz, ta=True, out_dtype=BF16, tm=1024, tn=Z_MAIN // 4, tk=2048, name="grad_w_in")
    g_w_f = mm(h1, dzf, ta=True, out_dtype=BF16, tm=1024, tn=F_PAD, tk=1024, name="grad_w_in_forget")
    tok = mixer_grads_ready(dict(w_main=g_w_main, w_f=g_w_f))
    dh1 = [mm(dz, w_main, tb=True, out_dtype=BF16, tm=512, tn=1024, tk=Z_MAIN, name="d_h1", after=tok),
           mm(dzf, w_f, tb=True, out_dtype=BF16, tm=2048, tn=1024, tk=F_PAD, name="d_h1_forget", after=tok)]
    grad_x, gg_pre_mix = _rms_bwd(dh1, x, g_pre_mix, dx2, out_dtype=F32, name="rms_pre_mix_bwd")

    grads = dict(
        b_forget=g_b_forget.reshape(1, N_HEADS), conv_b=_ffn_deinterleave(g_conv_b),
        g_pre_mix=gg_pre_mix, g_post_mix=gg_post_mix, g_pre_ffn=gg_pre_ffn, g_post_ffn=gg_post_ffn)
    return sq_err, grad_x, grads


def _gather_two_level(shard, *, name):
    def body(x_ref, out_ref, send_sems, recv_sems, local_sem):
        x, y, c = lax.axis_index("x"), lax.axis_index("y"), lax.axis_index("c")
        me, sibling = (x, y, c), (x, y, 1 - c)
        chips = [(1 - x, y), (x, 1 - y), (1 - x, 1 - y)]

        def slot(px, py, pc):
            return out_ref.at[4 * px + 2 * py + pc]

        def copy(k, block, to, src=None):
            return pltpu.make_async_remote_copy(
                src_ref=slot(*block) if src is None else src, dst_ref=slot(*block),
                send_sem=send_sems.at[k], recv_sem=recv_sems.at[k], device_id=to, device_id_type=MESH_ID)

        mine = pltpu.make_async_copy(x_ref, slot(*me), local_sem)
        mine.start()
        first = [copy(0, me, sibling, src=x_ref)]
        first += [copy(1 + j, me, (*chip, c), src=x_ref) for j, chip in enumerate(chips)]
        for cp in first:
            cp.start()
        passed = [copy(4 + j, (*chip, c), sibling) for j, chip in enumerate(chips)]
        for j, chip in enumerate(chips):
            copy(1 + j, (*chip, c), me).wait_recv()
            passed[j].start()
        copy(0, sibling, me).wait_recv()
        for j, chip in enumerate(chips):
            copy(4 + j, (*chip, 1 - c), me).wait_recv()
        for cp in first + passed:
            cp.wait_send()
        mine.wait()

    return pl.pallas_call(
        body, name=name, in_specs=[ANY], out_specs=ANY, out_shape=_sds((N_DEV,) + shard.shape, shard.dtype),
        scratch_shapes=[pltpu.SemaphoreType.DMA((N_DEV - 1,)), pltpu.SemaphoreType.DMA((N_DEV - 1,)),
                        pltpu.SemaphoreType.DMA],
    )(shard)


N_CHIPS = N_DEV // 2


def _peers(chips_only=False):
    x, y, c = lax.axis_index("x"), lax.axis_index("y"), lax.axis_index("c")
    out = []
    if chips_only:
        for k in range(1, N_CHIPS):
            px = 1 - x if k & 2 else x
            py = 1 - y if k & 1 else y
            out.append(((px, py, c), 2 * px + py))
        return 2 * x + y, out
    for k in range(1, N_DEV):
        px = 1 - x if k & 4 else x
        py = 1 - y if k & 2 else y
        pc = 1 - c if k & 1 else c
        out.append(((px, py, pc), 4 * px + 2 * py + pc))
    return 4 * x + 2 * y + c, out


def _sibling_swap(slot_arrays, *, name):
    n = len(slot_arrays)

    def body(*refs):
        ins, outs, send_sems, recv_sems = refs[:n], refs[n:2 * n], refs[2 * n], refs[2 * n + 1]
        x, y, c = lax.axis_index("x"), lax.axis_index("y"), lax.axis_index("c")
        copies = [pltpu.make_async_remote_copy(
            src_ref=ins[a].at[2 * q + (1 - c)], dst_ref=outs[a].at[q], send_sem=send_sems.at[a, q],
            recv_sem=recv_sems.at[a, q], device_id=(x, y, 1 - c), device_id_type=MESH_ID)
            for a in range(n) for q in range(N_CHIPS)]
        for cp in copies:
            cp.start()
        for cp in copies:
            cp.wait_recv()
        for cp in copies:
            cp.wait_send()

    return pl.pallas_call(
        body, name=name, in_specs=[ANY] * n, out_specs=[ANY] * n,
        out_shape=[_sds((N_CHIPS,) + t.shape[1:], t.dtype) for t in slot_arrays],
        scratch_shapes=[pltpu.SemaphoreType.DMA((n, N_CHIPS)), pltpu.SemaphoreType.DMA((n, N_CHIPS))],
    )(*slot_arrays)


def _pair_sum(slots, from_sibling, *, name, tn):
    _, r, c = slots.shape
    core = lax.axis_index("c").astype(jnp.int32).reshape(1)

    def body(core_ref, a_ref, b_ref, o_ref):
        o_ref[...] = (a_ref[...].astype(F32) + b_ref[...].astype(F32)).astype(o_ref.dtype)

    blk = lambda f: pl.BlockSpec((1, r, tn), f)
    return pl.pallas_call(
        body, name=name,
        grid_spec=pltpu.PrefetchScalarGridSpec(
            num_scalar_prefetch=1, grid=(N_CHIPS, c // tn),
            in_specs=[blk(lambda q, j, core: (2 * q + core[0], 0, j)), blk(lambda q, j, core: (q, 0, j))],
            out_specs=blk(lambda q, j, core: (q, 0, j))),
        out_shape=_sds((N_CHIPS, r, c), slots.dtype),
        compiler_params=_params("parallel", "parallel"),
    )(core, slots, from_sibling)


HBM = pl.BlockSpec(memory_space=pltpu.HBM)
SEM = pl.BlockSpec(memory_space=pltpu.SEMAPHORE)
DATAFLOW = pltpu.SideEffectType.DATAFLOW_SIDE_EFFECTING


def _split_copy(srcs, lands, send_sems, recv_sems, scatter, a, k, me, peers, incoming=False):
    dev, slot = peers[k]
    if incoming:
        src = dst = lands[a].at[slot]
    else:
        src, dst = (srcs[a].at[slot] if scatter else srcs[a]), lands[a].at[me]
    sem = a * len(peers) + k
    return pltpu.make_async_remote_copy(
        src_ref=src, dst_ref=dst, send_sem=send_sems.at[sem], recv_sem=recv_sems.at[sem],
        device_id=dev, device_id_type=MESH_ID)


def _own_copy(srcs, lands, own_sems, scatter, a, me):
    return pltpu.make_async_copy(srcs[a].at[me] if scatter else srcs[a], lands[a].at[me], own_sems.at[a])


def _exchange_start(arrays, scatter, *, name, chips_only=False, after=None):
    n = len(arrays)
    n_slots = N_CHIPS if chips_only else N_DEV
    n_in = 2 * n + len(_also(after))

    def body(*refs):
        srcs, lands = refs[:n], refs[n:2 * n]
        send_sems, recv_sems, own_sems = refs[n_in:n_in + 3]
        token = refs[-1]
        me, peers = _peers(chips_only)
        for k in range(len(peers)):
            for a in range(n):
                _split_copy(srcs, lands, send_sems, recv_sems, scatter, a, k, me, peers).start()
        for a in range(n):
            _own_copy(srcs, lands, own_sems, scatter, a, me).start()
        token[...] = jnp.zeros_like(token)

    land_shapes = [((n_slots,) + a.shape[-2:], a.dtype) for a in arrays]
    sems = pltpu.SemaphoreType.DMA((n * (n_slots - 1),))
    outs = pl.pallas_call(
        body, name=name,
        out_shape=(sems, sems, pltpu.SemaphoreType.DMA((n,)), *[pltpu.HBM(a.shape, a.dtype) for a in arrays],
                   *[pltpu.HBM(s, d) for s, d in land_shapes], _sds((SUBLANE, LANE), F32)),
        in_specs=[HBM] * (2 * n) + [ANY] * len(_also(after)),
        out_specs=(SEM, SEM, SEM, *[HBM] * (2 * n), pl.BlockSpec(memory_space=pltpu.VMEM)),
        input_output_aliases={i: 3 + i for i in range(2 * n)},
        compiler_params=pltpu.CompilerParams(has_side_effects=DATAFLOW),
    )(*[pltpu.with_memory_space_constraint(a, pltpu.HBM) for a in arrays],
      *[pltpu.with_memory_space_constraint(lax.empty(s, d), pltpu.HBM) for s, d in land_shapes], *_also(after))
    return (outs[:3], outs[3:3 + n], outs[3 + n:3 + 2 * n], scatter, chips_only), outs[-1]


def _exchange_wait(handles, after, *, name):
    sems, srcs, lands, scatter, chips_only = handles
    n = len(srcs)

    def body(*refs):
        src_refs, land_refs = refs[:n], refs[n:2 * n]
        send_ref, recv_ref, own_ref = refs[2 * n:2 * n + 3]
        me, peers = _peers(chips_only)
        for k in range(len(peers)):
            for a in range(n):
                _split_copy(src_refs, land_refs, send_ref, recv_ref, scatter, a, k, me, peers).wait_send()
                _split_copy(src_refs, land_refs, send_ref, recv_ref, scatter, a, k, me, peers, True).wait_recv()
        for a in range(n):
            _own_copy(src_refs, land_refs, own_ref, scatter, a, me).wait()

    outs = pl.pallas_call(
        body, name=name,
        out_shape=tuple(pltpu.HBM(t.shape, t.dtype) for t in (*srcs, *lands)),
        in_specs=[HBM] * (2 * n) + [SEM, SEM, SEM, pl.BlockSpec(memory_space=pl.ANY)],
        out_specs=tuple([HBM] * (2 * n)),
        input_output_aliases={i: i for i in range(2 * n)},
        compiler_params=pltpu.CompilerParams(has_side_effects=DATAFLOW),
    )(*srcs, *lands, *sems, after)
    return outs[n:]


def _adamw(parts, w, m, v, *, name, tm):
    r, c = w.shape
    assert r % tm == 0

    def body(p_ref, w_ref, m_ref, v_ref, g_ref, d_ref, nm_ref, nv_ref):
        _adamw_update(p_ref, w_ref, m_ref, v_ref, g_ref, d_ref, nm_ref, nv_ref)

    blk = pl.BlockSpec((tm, c), lambda i: (i, 0))
    return pl.pallas_call(
        body, name=name, grid=(r // tm,),
        in_specs=[pl.BlockSpec((parts.shape[0], tm, c), lambda i: (0, i, 0)), blk, blk, blk],
        out_specs=[blk] * 4, out_shape=[_sds((r, c), F32)] * 4,
        compiler_params=_params("parallel"),
    )(parts, w, m, v)


def _adamw_update(p_ref, w_ref, m_ref, v_ref, g_ref, d_ref, nm_ref, nv_ref):
    g = p_ref[0].astype(F32)
    for s in range(1, p_ref.shape[0]):
        g = g + p_ref[s].astype(F32)
    g_ref[...] = g
    m_new = ADAM_B1 * m_ref[...] + (1.0 - ADAM_B1) * g
    v_new = ADAM_B2 * v_ref[...] + (1.0 - ADAM_B2) * (g * g)
    nm_ref[...] = m_new
    nv_ref[...] = v_new
    m_hat = m_new / (1.0 - ADAM_B1 ** ADAM_STEP)
    v_hat = v_new / (1.0 - ADAM_B2 ** ADAM_STEP)
    d_ref[...] = -ADAM_LR * (m_hat / (jnp.sqrt(v_hat) + ADAM_EPS) + ADAM_WD * w_ref[...])


SMALL = ("g_pre_mix", "b_forget", "g_post_mix", "g_pre_ffn", "conv_b", "g_post_ffn")


def _adamw_small(parts, ws, ms, vs, sq_err_parts):
    n = len(ws)

    def body(*refs):
        ins, sq_ref, outs, loss_ref = refs[:4 * n], refs[4 * n], refs[4 * n + 1:-1], refs[-1]
        for i in range(n):
            _adamw_update(ins[i], ins[n + i], ins[2 * n + i], ins[3 * n + i], *outs[4 * i:4 * i + 4])
        total = sq_ref[0]
        for s in range(1, N_DEV):
            total = total + sq_ref[s]
        loss_ref[...] = total * (0.5 / D_MODEL)

    res = pl.pallas_call(
        body, name="adamw_small",
        out_shape=[_sds(w.shape, F32) for w in ws for _ in range(4)] + [_sds((1, LANE), F32)],
        compiler_params=pltpu.CompilerParams(vmem_limit_bytes=VMEM_LIMIT),
    )(*parts, *ws, *ms, *vs, sq_err_parts)
    return [res[4 * i:4 * i + 4] for i in range(n)], res[-1][0, 0]


def kernel(x, g_pre_mix, w_in, b_forget, w_o_fox, w_o_dil, w_out, g_post_mix, g_pre_ffn, w_up, conv_w, conv_b, w_down, g_post_ffn, loss_target, m_g_pre_mix, m_w_in, m_b_forget, m_w_o_fox, m_w_o_dil, m_w_out, m_g_post_mix, m_g_pre_ffn, m_w_up, m_conv_w, m_conv_b, m_w_down, m_g_post_ffn, v_g_pre_mix, v_w_in, v_b_forget, v_w_o_fox, v_w_o_dil, v_w_out, v_g_post_mix, v_g_pre_ffn, v_w_up, v_conv_w, v_conv_b, v_w_down, v_g_post_ffn):
    names = ("g_pre_mix", "w_in", "b_forget", "w_o_fox", "w_o_dil", "w_out", "g_post_mix", "g_pre_ffn",
             "w_up", "conv_w", "conv_b", "w_down", "g_post_ffn")
    w = dict(g_pre_mix=g_pre_mix, w_in=w_in, b_forget=b_forget, w_o_fox=w_o_fox, w_o_dil=w_o_dil, w_out=w_out,
             g_post_mix=g_post_mix, g_pre_ffn=g_pre_ffn, w_up=w_up, conv_w=conv_w, conv_b=conv_b, w_down=w_down,
             g_post_ffn=g_post_ffn)
    m = dict(g_pre_mix=m_g_pre_mix, w_in=m_w_in, b_forget=m_b_forget, w_o_fox=m_w_o_fox, w_o_dil=m_w_o_dil,
             w_out=m_w_out, g_post_mix=m_g_post_mix, g_pre_ffn=m_g_pre_ffn, w_up=m_w_up, conv_w=m_conv_w,
             conv_b=m_conv_b, w_down=m_w_down, g_post_ffn=m_g_post_ffn)
    v = dict(g_pre_mix=v_g_pre_mix, w_in=v_w_in, b_forget=v_b_forget, w_o_fox=v_w_o_fox, w_o_dil=v_w_o_dil,
             w_out=v_w_out, g_post_mix=v_g_post_mix, g_pre_ffn=v_g_pre_ffn, w_up=v_w_up, conv_w=v_conv_w,
             conv_b=v_conv_b, w_down=v_w_down, g_post_ffn=v_g_post_ffn)
    sharded = ("w_in", "w_o_fox", "w_o_dil", "w_out", "w_up", "w_down", "conv_w")
    wire = lambda n: F32 if n == "conv_w" else BF16

    by_cols = lambda t: jnp.transpose(t, (1, 0, 2)).reshape(t.shape[1], N_DEV * t.shape[2])
    by_rows = lambda t: t.reshape(N_DEV * t.shape[1], t.shape[2])
    col_slots = lambda t: jnp.transpose(t.reshape(t.shape[0], N_DEV, t.shape[1] // N_DEV), (1, 0, 2))
    row_slots = lambda t: t.reshape(N_DEV, t.shape[0] // N_DEV, t.shape[1])
    to_slots = lambda n, t: (row_slots if n in ("w_out", "w_down") else col_slots)(t).astype(wire(n))
    shard = lambda n: w[n][0].astype(wire(n))

    w_main, w_f = _w_in_from_shards(_gather_two_level(shard("w_in"), name="gather_w_in"))
    proj_handles, proj_tok = _exchange_start(
        [shard("w_o_fox"), shard("w_o_dil"), shard("w_out")], False, name="gather_proj_start", after=w_f)
    ffn_handles, ffn_tok = _exchange_start(
        [shard("w_up"), shard("conv_w"), shard("w_down")], False, name="gather_ffn_start", after=proj_tok)

    def proj_weights(after):
        w_oa, w_ob, w_o = _exchange_wait(proj_handles, after, name="gather_proj_wait")
        return by_cols(w_oa), by_cols(w_ob), by_rows(w_o)

    def ffn_weights(after):
        w_u, conv, w_d = _exchange_wait(ffn_handles, after, name="gather_ffn_wait")
        return _w_up_from_shards(w_u), _ffn_interleave(by_cols(conv)), by_rows(w_d)

    pending = {}

    def ffn_grads_ready(g):
        slots = [to_slots("w_down", g["w_down"]), _w_up_to_shards(g["w_up_blocks"]), to_slots("conv_w", g["conv_w"])]
        pending["ffn"] = _exchange_start(slots, True, name="scatter_ffn_start")
        return pending["ffn"][1]

    def proj_grads_ready(g):
        pending["proj"] = _exchange_start([g["w_o_fox"], g["w_o_dil"], to_slots("w_out", g["w_out"])], True,
                                          name="scatter_proj_start")
        return pending["proj"][1]

    def mixer_grads_ready(g):
        slots = _w_in_to_shards(g["w_main"], g["w_f"])
        theirs = _sibling_swap([slots], name="scatter_w_in_swap")[0]
        chip_sums = _pair_sum(slots, theirs, name="scatter_w_in_pair_sum", tn=W_IN_SHARD)
        pending["w_in"] = _exchange_start([chip_sums], True, name="scatter_w_in_start", chips_only=True)
        return pending["w_in"][1]

    sq_err, grad_x, g = _local_step(
        x[0], loss_target[0], w_main, w_f, b_forget, conv_b, g_pre_mix, g_post_mix, g_pre_ffn,
        g_post_ffn, proj_weights, ffn_weights, ffn_grads_ready, proj_grads_ready, mixer_grads_ready, after=ffn_tok)

    small_handles, small_tok = _exchange_start([g[n] for n in SMALL] + [sq_err], False, name="gather_small_start")
    tiles = dict(w_in=256, w_o_fox=512, w_o_dil=512, w_out=128, w_up=256, w_down=176, conv_w=3)
    adam = lambda n, p: _adamw(p, w[n][0], m[n][0], v[n][0], name=f"adamw_{n}", tm=tiles[n])
    res = {}
    for key, group in (("ffn", ("w_down", "w_up", "conv_w")), ("proj", ("w_o_fox", "w_o_dil", "w_out"))):
        landed = _exchange_wait(pending[key][0], small_tok, name=f"scatter_{key}_wait")
        res.update({n: adam(n, p) for n, p in zip(group, landed)})
    done = res["w_up"][3]
    res["w_in"] = adam("w_in", _exchange_wait(pending["w_in"][0], done, name="scatter_w_in_wait")[0])
    small_parts = _exchange_wait(small_handles, res["w_in"][3], name="gather_small_wait")
    small, loss = _adamw_small(small_parts[:-1], *[[t[n] for n in SMALL] for t in (w, m, v)], small_parts[-1])
    small = dict(zip(SMALL, small))
    out = [[(res[n][k][None] if n in sharded else small[n][k]) for n in names] for k in range(4)]
    return (loss, grad_x[None], *out[0], *out[1], *out[2], *out[3])
```

```python
import functools
import math

import jax
import jax.numpy as jnp
import numpy as np
from jax import lax
from jax.experimental import pallas as pl
from jax.experimental.pallas import tpu as pltpu

F32 = jnp.float32
BF16 = jnp.bfloat16

SEQ = 4096
D_MODEL = 1024
N_HEADS = 8
HEAD_DIM = 64
ATT_W = N_HEADS * HEAD_DIM
D_FF = 2816
Z_MAIN = 5120
F_PAD = 128
ROPE_DIM = 16
ROPE_THETA = 500000.0
RMS_EPS = 1e-6
NEG_INF = -1e30
SCALE = 1.0 / math.sqrt(HEAD_DIM)
DIL_PATTERNS = ((128, 1), (512, 4), (2048, 16))
DIL_BLK = 128
DIL_STEP_BLOCKS = 2
N_DEV = 8

ADAM_LR = 0.001
ADAM_B1 = 0.9
ADAM_B2 = 0.999
ADAM_EPS = 1e-08
ADAM_WD = 0.01
ADAM_STEP = 10

LANE = 128
SUBLANE = 8
VMEM_LIMIT = 56 * 1024 * 1024
MESH_ID = pl.DeviceIdType.MESH
ANY = pl.BlockSpec(memory_space=pl.ANY)


def _params(*sem):
    return pltpu.CompilerParams(dimension_semantics=sem, vmem_limit_bytes=VMEM_LIMIT)


def _sds(shape, dtype):
    return jax.ShapeDtypeStruct(shape, dtype)


def _also(after):
    return [] if after is None else [after]


def _matmul(a, b, *, ta=False, tb=False, out_dtype, tm, tn, tk, name, b_k_off=0, after=None, col_slots=1):
    n_after = len(_also(after))
    if ta:
        kk, m = a.shape
    else:
        m, kk = a.shape
    n = b.shape[0] if tb else b.shape[1]
    tm, tn, tk = min(tm, m), min(tn, n), min(tk, kk)
    assert (b.shape[1] if tb else b.shape[0]) >= b_k_off * tk + kk
    assert m % tm == 0 and n % tn == 0 and kk % tk == 0, (name, m, n, kk, tm, tn, tk)
    nk = kk // tk
    dims = (((0 if ta else 1,), (1 if tb else 0,)), ((), ()))
    slot_w = n // col_slots
    assert col_slots == 1 or (nk == 1 and tn == n and slot_w % LANE == 0), name

    def body(a_ref, b_ref, *rest):
        o_ref, scratch = rest[n_after], rest[n_after + 1:]
        p = lax.dot_general(a_ref[...].astype(BF16), b_ref[...].astype(BF16), dims,
                            preferred_element_type=F32)
        if col_slots > 1:
            for s in range(col_slots):
                o_ref[s] = p[:, s * slot_w:(s + 1) * slot_w].astype(o_ref.dtype)
        elif nk == 1:
            o_ref[...] = p.astype(o_ref.dtype)
        else:
            acc = scratch[0]
            k = pl.program_id(2)

            @pl.when(k == 0)
            def _():
                acc[...] = p

            @pl.when(k > 0)
            def _():
                acc[...] += p

            @pl.when(k == nk - 1)
            def _():
                o_ref[...] = acc[...].astype(o_ref.dtype)

    a_spec = (pl.BlockSpec((tk, tm), lambda i, j, k: (k, i)) if ta
              else pl.BlockSpec((tm, tk), lambda i, j, k: (i, k)))
    b_spec = (pl.BlockSpec((tn, tk), lambda i, j, k: (j, k + b_k_off)) if tb
              else pl.BlockSpec((tk, tn), lambda i, j, k: (k + b_k_off, j)))
    return pl.pallas_call(
        body, name=name, grid=(m // tm, n // tn, nk),
        in_specs=[a_spec, b_spec] + [ANY] * n_after,
        out_specs=(pl.BlockSpec((tm, tn), lambda i, j, k: (i, j)) if col_slots == 1
                   else pl.BlockSpec((col_slots, tm, slot_w), lambda i, j, k: (0, i, 0))),
        out_shape=_sds((m, n) if col_slots == 1 else (col_slots, m, slot_w), out_dtype),
        scratch_shapes=[pltpu.VMEM((tm, tn), F32)] if nk > 1 else [],
        compiler_params=_params("parallel", "parallel", "arbitrary"),
    )(a, b, *_also(after))


def _rms_fwd(x, g, *, name, tm=512, after=None):
    def body(x_ref, g_ref, *rest):
        h_ref = rest[-1]
        xv = x_ref[...]
        r = lax.rsqrt(jnp.mean(xv * xv, axis=-1, keepdims=True) + RMS_EPS)
        h_ref[...] = (xv * r * g_ref[...]).astype(h_ref.dtype)

    return pl.pallas_call(
        body, name=name, grid=(SEQ // tm,),
        in_specs=[pl.BlockSpec((tm, D_MODEL), lambda i: (i, 0)), pl.BlockSpec((1, D_MODEL), lambda i: (0, 0))]
        + [ANY] * len(_also(after)),
        out_specs=pl.BlockSpec((tm, D_MODEL), lambda i: (i, 0)),
        out_shape=_sds((SEQ, D_MODEL), BF16),
        compiler_params=_params("parallel"),
    )(x, g, *_also(after))


def _rms_bwd(dh_parts, xin, g, dres, *, out_dtype, name, tm=512):
    n_parts = len(dh_parts)
    has_res = dres is not None

    def body(*refs):
        parts = refs[:n_parts]
        x_ref, g_ref = refs[n_parts], refs[n_parts + 1]
        res_ref = refs[n_parts + 2] if has_res else None
        o_ref, gg_ref = refs[-2], refs[-1]
        dh = parts[0][...].astype(F32)
        for p in parts[1:]:
            dh = dh + p[...].astype(F32)
        xv = x_ref[...]
        r = lax.rsqrt(jnp.mean(xv * xv, axis=-1, keepdims=True) + RMS_EPS)
        xn = xv * r

        @pl.when(pl.program_id(0) == 0)
        def _():
            gg_ref[...] = jnp.zeros_like(gg_ref)

        gg_ref[...] += jnp.sum(dh * xn, axis=0, keepdims=True)
        dxn = dh * g_ref[...]
        dx = r * (dxn - xn * jnp.mean(dxn * xn, axis=-1, keepdims=True))
        if has_res:
            dx = dx + res_ref[...]
        o_ref[...] = dx.astype(o_ref.dtype)

    row = pl.BlockSpec((tm, D_MODEL), lambda i: (i, 0))
    vec = pl.BlockSpec((1, D_MODEL), lambda i: (0, 0))
    args = list(dh_parts) + [xin, g] + ([dres] if has_res else [])
    return pl.pallas_call(
        body, name=name, grid=(SEQ // tm,),
        in_specs=[row] * n_parts + [row, vec] + ([row] if has_res else []),
        out_specs=[row, vec],
        out_shape=[_sds((SEQ, D_MODEL), out_dtype), _sds((1, D_MODEL), F32)],
        compiler_params=_params("arbitrary"),
    )(*args)


def _rms_pair_bwd(dh_parts, x2, g_pre, dres, y1, g_post, *, tm=512, after=None):
    n_parts = len(dh_parts)

    def norm_bwd(dh, xin, g_ref, gg_ref):
        r = lax.rsqrt(jnp.mean(xin * xin, axis=-1, keepdims=True) + RMS_EPS)
        xn = xin * r
        gg_ref[...] += jnp.sum(dh * xn, axis=0, keepdims=True)
        dxn = dh * g_ref[...]
        return r * (dxn - xn * jnp.mean(dxn * xn, axis=-1, keepdims=True))

    def body(*refs):
        parts = refs[:n_parts]
        x2_ref, gpre_ref, res_ref, y1_ref, gpost_ref = refs[n_parts:n_parts + 5]
        dx2_ref, dy1_ref, ggpre_ref, ggpost_ref = refs[-4:]

        @pl.when(pl.program_id(0) == 0)
        def _():
            ggpre_ref[...] = jnp.zeros_like(ggpre_ref)
            ggpost_ref[...] = jnp.zeros_like(ggpost_ref)

        dh = parts[0][...].astype(F32)
        for p in parts[1:]:
            dh = dh + p[...].astype(F32)
        dx2 = res_ref[...] + norm_bwd(dh, x2_ref[...], gpre_ref, ggpre_ref)
        dx2_ref[...] = dx2
        dy1_ref[...] = norm_bwd(dx2, y1_ref[...], gpost_ref, ggpost_ref).astype(dy1_ref.dtype)

    row = pl.BlockSpec((tm, D_MODEL), lambda i: (i, 0))
    vec = pl.BlockSpec((1, D_MODEL), lambda i: (0, 0))
    return pl.pallas_call(
        body, name="rms_pair_bwd", grid=(SEQ // tm,),
        in_specs=[row] * n_parts + [row, vec, row, row, vec] + [ANY] * len(_also(after)),
        out_specs=[row, row, vec, vec],
        out_shape=[_sds((SEQ, D_MODEL), F32), _sds((SEQ, D_MODEL), BF16), _sds((1, D_MODEL), F32),
                   _sds((1, D_MODEL), F32)],
        compiler_params=_params("arbitrary"),
    )(*dh_parts, x2, g_pre, dres, y1, g_post, *_also(after))


SCAN_BLK = 512


def _split_dot(v, tri):
    hi = v.astype(BF16)
    r1 = v - hi.astype(F32)
    mid = r1.astype(BF16)
    lo = (r1 - mid.astype(F32)).astype(BF16)
    dot = functools.partial(jnp.dot, preferred_element_type=F32)
    return dot(hi, tri) + dot(mid, tri) + dot(lo, tri)


def _fox_prep(fa_t, b_col):
    nblk = SEQ // SCAN_BLK

    def body(fa_ref, b_ref, f_ref, sg_ref):
        row = lax.broadcasted_iota(jnp.int32, (SCAN_BLK, SCAN_BLK), 0)
        col = lax.broadcasted_iota(jnp.int32, (SCAN_BLK, SCAN_BLK), 1)
        upper = (row <= col).astype(BF16)
        carry = jnp.zeros((N_HEADS, 1), F32)
        for blk in range(nblk):
            sl = pl.ds(blk * SCAN_BLK, SCAN_BLK)
            xx = fa_ref[:, sl] + b_ref[...]
            e = jnp.exp(-jnp.abs(xx))
            logf = jnp.minimum(xx, 0.0) - jnp.log(1.0 + e)
            sg_ref[:, sl] = jnp.where(xx >= 0.0, e, 1.0) / (1.0 + e)
            c = _split_dot(logf, upper) + carry
            f_ref[:, sl] = c
            carry = c[:, SCAN_BLK - 1:SCAN_BLK]

    return pl.pallas_call(
        body, name="fox_prep",
        out_shape=[_sds((N_HEADS, SEQ), F32), _sds((N_HEADS, SEQ), F32)],
        compiler_params=pltpu.CompilerParams(vmem_limit_bytes=VMEM_LIMIT),
    )(fa_t, b_col)


def _fox_post_bwd(df_t, sg_t):
    nblk = SEQ // SCAN_BLK

    def body(df_ref, sg_ref, dfa_ref, gb_ref):
        row = lax.broadcasted_iota(jnp.int32, (SCAN_BLK, SCAN_BLK), 0)
        col = lax.broadcasted_iota(jnp.int32, (SCAN_BLK, SCAN_BLK), 1)
        lower = (row >= col).astype(BF16)
        carry = jnp.zeros((N_HEADS, 1), F32)
        gb = jnp.zeros((N_HEADS, 1), F32)
        for blk in reversed(range(nblk)):
            sl = pl.ds(blk * SCAN_BLK, SCAN_BLK)
            c = _split_dot(df_ref[:, sl], lower) + carry
            carry = c[:, 0:1]
            dfa = c * sg_ref[:, sl]
            dfa_ref[:, sl] = dfa
            gb = gb + jnp.sum(dfa, axis=1, keepdims=True)
        gb_ref[...] = gb

    return pl.pallas_call(
        body, name="fox_post_bwd",
        out_shape=[_sds((N_HEADS, SEQ), F32), _sds((N_HEADS, 1), F32)],
        compiler_params=pltpu.CompilerParams(vmem_limit_bytes=VMEM_LIMIT),
    )(df_t, sg_t)


FOX_T = 512
NT_DIMS = (((1,), (1,)), ((), ()))
TN_DIMS = (((0,), (0,)), ((), ()))


def _head(ref_or_val, h):
    return ref_or_val[:, h * HEAD_DIM:(h + 1) * HEAD_DIM]


def _split3(v):
    hi = v.astype(BF16).astype(F32)
    r1 = v - hi
    mid = r1.astype(BF16).astype(F32)
    return hi, mid, (r1 - mid).astype(BF16).astype(F32)


ONE_LANE = 3 * N_HEADS


def _pack_terms(v, with_one):
    hi, mid, lo = _split3(v)
    t = hi + pltpu.roll(mid, N_HEADS, 1) + pltpu.roll(lo, 2 * N_HEADS, 1)
    if with_one:
        t = t + (lax.broadcasted_iota(jnp.int32, v.shape, 1) == ONE_LANE).astype(F32)
    return t.astype(BF16)


def _aux_matrices():
    to_q = np.zeros((LANE, N_HEADS * 2 * HEAD_DIM), np.float32)
    to_k = np.zeros_like(to_q)
    for h in range(N_HEADS):
        base = h * 2 * HEAD_DIM + HEAD_DIM
        for s in range(3):
            to_q[s * N_HEADS + h, base + s] = 1.0
            to_q[ONE_LANE, base + 3 + s] = 1.0
            to_k[ONE_LANE, base + s] = 1.0
            to_k[s * N_HEADS + h, base + 3 + s] = -1.0
    return jnp.asarray(to_q, BF16), jnp.asarray(to_k, BF16)


def _head_sums():
    total = np.zeros((N_HEADS * HEAD_DIM, LANE), np.float32)
    first = np.zeros_like(total)
    for h in range(N_HEADS):
        total[h * HEAD_DIM:(h + 1) * HEAD_DIM, h] = 1.0
        first[h * HEAD_DIM, h] = 1.0
    return jnp.asarray(total, BF16), jnp.asarray(first, BF16)


SLOT = 2 * HEAD_DIM
N_SPLIT = 3
FOX_FWD_HEADS = 8
FOX_BWD_HEADS = 4


def _slot(ref, h):
    return ref[:, h * SLOT:(h + 1) * SLOT]


def _fox_pack_fwd(zm, f_cols, *, tm=512):
    def body(q_ref, k_ref, v_ref, f_ref, tq_ref, tk_ref, qs_ref, ks_ref, vs_ref):
        ones = jnp.ones((tm, HEAD_DIM), BF16)
        terms = _pack_terms(f_ref[...], True)
        q_aux = jnp.dot(terms, tq_ref[...], preferred_element_type=F32).astype(BF16)
        k_aux = jnp.dot(terms, tk_ref[...], preferred_element_type=F32).astype(BF16)
        for h in range(N_HEADS):
            aux = slice(h * SLOT + HEAD_DIM, (h + 1) * SLOT)
            qs_ref[:, h * SLOT:(h + 1) * SLOT] = jnp.concatenate(
                [(_head(q_ref, h).astype(F32) * SCALE).astype(BF16), q_aux[:, aux]], axis=1)
            ks_ref[:, h * SLOT:(h + 1) * SLOT] = jnp.concatenate([_head(k_ref, h), k_aux[:, aux]], axis=1)
            vs_ref[:, h * SLOT:(h + 1) * SLOT] = jnp.concatenate([_head(v_ref, h), ones], axis=1)

    col = lambda b: pl.BlockSpec((tm, ATT_W), lambda i: (i, b))
    wide = pl.BlockSpec((tm, N_HEADS * SLOT), lambda i: (i, 0))
    const = pl.BlockSpec((LANE, N_HEADS * SLOT), lambda i: (0, 0))
    return pl.pallas_call(
        body, name="fox_pack_fwd", grid=(SEQ // tm,),
        in_specs=[col(0), col(1), col(2), pl.BlockSpec((tm, LANE), lambda i: (i, 0)), const, const],
        out_specs=[wide] * 3, out_shape=[_sds((SEQ, N_HEADS * SLOT), BF16)] * 3,
        compiler_params=_params("parallel"),
    )(zm, zm, zm, f_cols, *_aux_matrices())


def _fox_pack_bwd(zm, f_cols, lse, o, do, *, tm=512, after=None):
    def body(q_ref, f_ref, lse_ref, o_ref, do_ref, tq_ref, total_ref, first_ref, *rest):
        qs_ref, ds_ref = rest[-2:]
        delta = _split_dot(o_ref[...].astype(F32) * do_ref[...].astype(F32), total_ref[...])
        lse_h = _split_dot(lse_ref[...], first_ref[...])
        q_aux = jnp.dot(_pack_terms(f_ref[...] - lse_h, True), tq_ref[...], preferred_element_type=F32).astype(BF16)
        d_aux = jnp.dot(_pack_terms(-delta, False), tq_ref[...], preferred_element_type=F32).astype(BF16)
        for h in range(N_HEADS):
            aux = slice(h * SLOT + HEAD_DIM, (h + 1) * SLOT)
            qs_ref[:, h * SLOT:(h + 1) * SLOT] = jnp.concatenate(
                [(_head(q_ref, h).astype(F32) * SCALE).astype(BF16), q_aux[:, aux]], axis=1)
            ds_ref[:, h * SLOT:(h + 1) * SLOT] = jnp.concatenate([_head(do_ref, h), d_aux[:, aux]], axis=1)

    row = pl.BlockSpec((tm, ATT_W), lambda i: (i, 0))
    wide = pl.BlockSpec((tm, N_HEADS * SLOT), lambda i: (i, 0))
    const = lambda r, c: pl.BlockSpec((r, c), lambda i: (0, 0))
    return pl.pallas_call(
        body, name="fox_pack_bwd", grid=(SEQ // tm,),
        in_specs=[row, pl.BlockSpec((tm, LANE), lambda i: (i, 0)), row, row, row,
                  const(LANE, N_HEADS * SLOT), const(ATT_W, LANE), const(ATT_W, LANE)] + [ANY] * len(_also(after)),
        out_specs=[wide] * 2, out_shape=[_sds((SEQ, N_HEADS * SLOT), BF16)] * 2,
        compiler_params=_params("parallel"),
    )(zm, f_cols, lse, o, do, _aux_matrices()[0], *_head_sums(), *_also(after))


def _causal_pairs(key_major):
    nb = SEQ // FOX_T
    if key_major:
        pairs = [(i, j) for j in range(nb) for i in range(j, nb)]
    else:
        pairs = [(i, j) for i in range(nb) for j in range(i + 1)]
    return (jnp.array([p[0] for p in pairs], jnp.int32), jnp.array([p[1] for p in pairs], jnp.int32), len(pairs))


FOX_HALF = FOX_T // 2
FOX_FULL = ((slice(0, FOX_T), slice(0, FOX_T), None),)
FOX_DIAG = ((slice(0, FOX_HALF), slice(0, FOX_HALF), 0), (slice(FOX_HALF, FOX_T), slice(0, FOX_T), FOX_HALF))


def _causal_piece_mask(q_rows, k_rows, offset):
    shape = (q_rows.stop - q_rows.start, k_rows.stop - k_rows.start)
    row = lax.broadcasted_iota(jnp.int32, shape, 0)
    col = lax.broadcasted_iota(jnp.int32, shape, 1)
    return col <= row + offset


def _fox_fwd(q_slots, k_slots, v_slots):
    i_tab, j_tab, n_pairs = _causal_pairs(False)

    def body(i_tab, j_tab, q_ref, k_ref, v_ref, o_ref, lse_ref, m_s, acc_s):
        t = pl.program_id(1)
        i, j = i_tab[t], j_tab[t]

        @pl.when(j == 0)
        def _():
            m_s[...] = jnp.full_like(m_s, NEG_INF)
            acc_s[...] = jnp.zeros_like(acc_s)

        def step(pieces):
            jobs = [(h, piece) for h in range(FOX_FWD_HEADS) for piece in pieces]
            lanes = lambda h: slice(h * SLOT, (h + 1) * SLOT)
            scores = [lax.dot_general(q_ref[qr, lanes(h)], k_ref[kr, lanes(h)], NT_DIMS, preferred_element_type=F32)
                      for h, (qr, kr, _) in jobs]
            probs, alphas = [], []
            for idx, (h, (qr, kr, offset)) in enumerate(jobs):
                s = scores[idx]
                if offset is not None:
                    s = jnp.where(_causal_piece_mask(qr, kr, offset), s, NEG_INF)
                m_prev = m_s[h, qr, :]
                m_new = jnp.maximum(m_prev, jnp.max(s, axis=-1, keepdims=True))
                probs.append(jnp.exp(s - jnp.tile(m_new, (1, s.shape[1] // LANE))).astype(BF16))
                alphas.append(jnp.exp(m_prev - m_new))
                m_s[h, qr, :] = m_new
            for idx, (h, (qr, kr, _)) in enumerate(jobs):
                acc_s[h, qr, :] = alphas[idx] * acc_s[h, qr, :] + jnp.dot(
                    probs[idx], v_ref[kr, lanes(h)], preferred_element_type=F32)

        @pl.when(j < i)
        def _():
            step(FOX_FULL)

        @pl.when(j == i)
        def _():
            step(FOX_DIAG)
            outs, lses = [], []
            for h in range(FOX_FWD_HEADS):
                acc = acc_s[h]
                l = acc[:, HEAD_DIM:]
                outs.append(acc[:, :HEAD_DIM] / l)
                lses.append(m_s[h][:, :HEAD_DIM] + jnp.log(l))
            o_ref[...] = jnp.concatenate(outs, axis=1).astype(o_ref.dtype)
            lse_ref[...] = jnp.concatenate(lses, axis=1)

    qspec = pl.BlockSpec((FOX_T, FOX_FWD_HEADS * SLOT), lambda p, t, it, jt: (it[t], p))
    kspec = pl.BlockSpec((FOX_T, FOX_FWD_HEADS * SLOT), lambda p, t, it, jt: (jt[t], p))
    ospec = pl.BlockSpec((FOX_T, FOX_FWD_HEADS * HEAD_DIM), lambda p, t, it, jt: (it[t], p))
    return pl.pallas_call(
        body, name="fox_fwd",
        grid_spec=pltpu.PrefetchScalarGridSpec(
            num_scalar_prefetch=2, grid=(N_HEADS // FOX_FWD_HEADS, n_pairs),
            in_specs=[qspec, kspec, kspec], out_specs=[ospec, ospec],
            scratch_shapes=[pltpu.VMEM((FOX_FWD_HEADS, FOX_T, LANE), F32),
                            pltpu.VMEM((FOX_FWD_HEADS, FOX_T, SLOT), F32)]),
        out_shape=[_sds((SEQ, ATT_W), BF16), _sds((SEQ, ATT_W), F32)],
        compiler_params=_params("parallel", "arbitrary"),
    )(i_tab, j_tab, q_slots, k_slots, v_slots)


def _fox_bwd(q_slots, k_slots, v_slots, do_slots):
    i_tab, j_tab, n_pairs = _causal_pairs(True)

    def body(i_tab, j_tab, q_ref, k_ref, v_ref, do_ref, dq_ref, dk_ref, dv_ref):
        t = pl.program_id(1)
        i, j = i_tab[t], j_tab[t]

        @pl.when(t == 0)
        def _():
            dq_ref[...] = jnp.zeros_like(dq_ref)

        @pl.when(i == j)
        def _():
            dk_ref[...] = jnp.zeros_like(dk_ref)
            dv_ref[...] = jnp.zeros_like(dv_ref)

        def step(pieces):
            jobs = [(h, piece) for h in range(FOX_BWD_HEADS) for piece in pieces]
            lanes = lambda h: slice(h * SLOT, (h + 1) * SLOT)
            scores = [lax.dot_general(q_ref[qr, lanes(h)], k_ref[kr, lanes(h)], NT_DIMS, preferred_element_type=F32)
                      for h, (qr, kr, _) in jobs]
            dps = [lax.dot_general(do_ref[qr, lanes(h)], v_ref[kr, lanes(h)], NT_DIMS, preferred_element_type=F32)
                   for h, (qr, kr, _) in jobs]
            ps, dss = [], []
            for idx, (h, (qr, kr, offset)) in enumerate(jobs):
                p = jnp.exp(scores[idx])
                if offset is not None:
                    p = jnp.where(_causal_piece_mask(qr, kr, offset), p, 0.0)
                ps.append(p.astype(BF16))
                dss.append((p * dps[idx]).astype(BF16))
            for idx, (h, (qr, kr, _)) in enumerate(jobs):
                rows = pl.ds(pl.multiple_of(i * FOX_T + qr.start, FOX_HALF), qr.stop - qr.start)
                dv_ref[kr, lanes(h)] += lax.dot_general(ps[idx], do_ref[qr, lanes(h)], TN_DIMS,
                                                        preferred_element_type=F32)
                dk_ref[kr, lanes(h)] += lax.dot_general(dss[idx], q_ref[qr, lanes(h)], TN_DIMS,
                                                        preferred_element_type=F32)
                dq_ref[rows, lanes(h)] += jnp.dot(dss[idx], k_ref[kr, lanes(h)], preferred_element_type=F32)

        @pl.when(i > j)
        def _():
            step(FOX_FULL)

        @pl.when(i == j)
        def _():
            step(FOX_DIAG)

    qspec = pl.BlockSpec((FOX_T, FOX_BWD_HEADS * SLOT), lambda p, t, it, jt: (it[t], p))
    kspec = pl.BlockSpec((FOX_T, FOX_BWD_HEADS * SLOT), lambda p, t, it, jt: (jt[t], p))
    return pl.pallas_call(
        body, name="fox_bwd",
        grid_spec=pltpu.PrefetchScalarGridSpec(
            num_scalar_prefetch=2, grid=(N_HEADS // FOX_BWD_HEADS, n_pairs),
            in_specs=[qspec, kspec, kspec, qspec],
            out_specs=[pl.BlockSpec((SEQ, FOX_BWD_HEADS * SLOT), lambda p, t, it, jt: (0, p)), kspec, kspec]),
        out_shape=[_sds((SEQ, N_HEADS * SLOT), F32)] * 3,
        compiler_params=_params("arbitrary", "arbitrary"),
    )(i_tab, j_tab, q_slots, k_slots, v_slots, do_slots)


def _fox_unpack(dq_slots, dk_slots, dv_slots, dz, *, tm=512):
    def body(dq_ref, dk_ref, dv_ref, dz_in, o_ref, df_ref):
        lane = lax.broadcasted_iota(jnp.int32, (tm, LANE), 1)
        df = jnp.zeros((tm, LANE), F32)
        for h in range(N_HEADS):
            lo = h * SLOT
            for part, (ref, mult) in enumerate(((dq_ref, SCALE), (dk_ref, 1.0), (dv_ref, 1.0))):
                o_ref[:, part * ATT_W + h * HEAD_DIM:part * ATT_W + (h + 1) * HEAD_DIM] = (
                    ref[:, lo:lo + HEAD_DIM] * mult).astype(o_ref.dtype)
            rows = dq_ref[:, lo + HEAD_DIM:lo + HEAD_DIM + 1]
            cols = dk_ref[:, lo + HEAD_DIM + N_SPLIT:lo + HEAD_DIM + N_SPLIT + 1]
            df = jnp.where(lane == h, rows - cols, df)
        df_ref[...] = df

    wide = pl.BlockSpec((tm, N_HEADS * SLOT), lambda i: (i, 0))
    return pl.pallas_call(
        body, name="fox_unpack", grid=(SEQ // tm,), in_specs=[wide] * 3 + [ANY],
        out_specs=[pl.BlockSpec((tm, 3 * ATT_W), lambda i: (i, 0)), pl.BlockSpec((tm, LANE), lambda i: (i, 0))],
        out_shape=[_sds((SEQ, Z_MAIN), BF16), _sds((SEQ, LANE), F32)],
        input_output_aliases={3: 0},
        compiler_params=_params("parallel"),
    )(dq_slots, dk_slots, dv_slots, dz)


def _dil_bwd_prep(o, do, lse, *, tm=512):
    dilations = [d for _, d in DIL_PATTERNS]
    o_chunks = ATT_W // LANE

    def body(o_ref, do_ref, lse_ref, *rest):
        outs, (do_scr, lse_scr, dl_scr) = rest[:-3], rest[-3:]
        dov = do_ref[...].astype(F32)
        prod = o_ref[...].astype(F32) * dov
        lane = lax.broadcasted_iota(jnp.int32, (tm, LANE), 1)
        delta = jnp.zeros((tm, LANE), F32)
        for h in range(N_HEADS):
            delta = jnp.where(lane == h, jnp.sum(_head(prod, h), axis=1, keepdims=True), delta)
        for ch in range(o_chunks):
            do_scr[ch] = dov[:, ch * LANE:(ch + 1) * LANE]
        lse_scr[0] = lse_ref[...]
        dl_scr[0] = delta
        for k, d in enumerate(dilations):
            for scr, out in zip((do_scr, lse_scr, dl_scr), outs[3 * k:3 * k + 3]):
                _slabs_from_rows(scr, out, d)

    row = pl.BlockSpec((tm, ATT_W), lambda i: (i, 0))
    view = lambda d, w: pl.BlockSpec((tm // d, d * w), lambda i: (i, 0))
    outs = pl.pallas_call(
        body, name="dil_bwd_prep", grid=(SEQ // tm,),
        in_specs=[row, row, pl.BlockSpec((tm, LANE), lambda i: (i, 0))],
        out_specs=[view(d, w) for d in dilations for w in (ATT_W, LANE, LANE)],
        out_shape=[_sds((SEQ // d, d * w), t) for d in dilations for w, t in ((ATT_W, BF16), (LANE, F32), (LANE, F32))],
        scratch_shapes=[pltpu.VMEM((o_chunks, tm, LANE), F32), pltpu.VMEM((1, tm, LANE), F32),
                        pltpu.VMEM((1, tm, LANE), F32)],
        compiler_params=_params("parallel"),
    )(o, do, lse)
    return [outs[3 * k:3 * k + 3] for k in range(len(dilations))]


def _rope_tables():
    half = ROPE_DIM // 2
    inv_freq = np.float32(ROPE_THETA) ** (-np.arange(half, dtype=np.float32) * np.float32(2.0) / np.float32(ROPE_DIM))
    ang = np.arange(SEQ, dtype=np.float32)[:, None] * inv_freq.astype(np.float32)[None, :]
    cos, sin = jnp.asarray(np.cos(ang).astype(np.float32)), jnp.asarray(np.sin(ang).astype(np.float32))
    ones = jnp.ones((SEQ, HEAD_DIM - ROPE_DIM), F32)
    zeros = jnp.zeros((SEQ, HEAD_DIM - ROPE_DIM), F32)
    zh = jnp.zeros((SEQ, half), F32)
    c_tab = jnp.concatenate([cos, cos, ones], axis=1)
    a_tab = jnp.concatenate([-sin, zh, zeros], axis=1)
    b_tab = jnp.concatenate([zh, sin, zeros], axis=1)
    two = lambda t: jnp.concatenate([t, t], axis=1)
    return two(c_tab), two(a_tab), two(b_tab)


def _rotate(x, c_tab, a_tab, b_tab):
    return x * c_tab + pltpu.roll(x, LANE - ROPE_DIM // 2, 1) * a_tab + pltpu.roll(x, ROPE_DIM // 2, 1) * b_tab


def _rope_fwd(zm, tabs, *, tm=512):
    width = 3 * ATT_W
    dilations = [d for _, d in DIL_PATTERNS]

    def body(q_ref, k_ref, v_ref, c_ref, a_ref, b_ref, *rest):
        outs, scr = rest[:-1], rest[-1]
        per_part = ATT_W // LANE
        for part, (x_ref, mult) in enumerate(((q_ref, SCALE), (k_ref, 1.0))):
            for cc in range(per_part):
                sl = slice(cc * LANE, (cc + 1) * LANE)
                scr[part * per_part + cc] = _rotate(x_ref[:, sl].astype(F32), c_ref[...], a_ref[...], b_ref[...]) * mult
        for cc in range(per_part):
            scr[2 * per_part + cc] = v_ref[:, cc * LANE:(cc + 1) * LANE].astype(F32)
        for o_ref, d in zip(outs, dilations):
            for r in range(d):
                for ch in range(width // LANE):
                    o_ref[:, r * width + ch * LANE:r * width + (ch + 1) * LANE] = (
                        scr.at[ch][pl.ds(r, tm // d, stride=d), :].astype(o_ref.dtype))

    tab = pl.BlockSpec((tm, LANE), lambda i: (i, 0))
    col = lambda b: pl.BlockSpec((tm, ATT_W), lambda i: (i, b))
    return pl.pallas_call(
        body, name="rope_fwd", grid=(SEQ // tm,),
        in_specs=[col(3), col(4), col(5), tab, tab, tab],
        out_specs=[pl.BlockSpec((tm // d, d * width), lambda i: (i, 0)) for d in dilations],
        out_shape=[_sds((SEQ // d, d * width), BF16) for d in dilations],
        scratch_shapes=[pltpu.VMEM((width // LANE, tm, LANE), F32)],
        compiler_params=_params("parallel"),
    )(zm, zm, zm, *tabs)


def _dil_grad_combine(dqs, dks, dvs, tabs, dz, *, tm=256):
    dilations = [d for _, d in DIL_PATTERNS]
    chunks = ATT_W // LANE

    def body(*refs):
        groups = (refs[0:3], refs[3:6], refs[6:9])
        c_ref, a_ref, b_ref, _, o_ref, scr = refs[9:]

        def total(part, cc):
            acc = None
            for g, (ref, d) in enumerate(zip(groups[part], dilations)):
                term = ref[:, cc * LANE:(cc + 1) * LANE].astype(F32) if d == 1 else scr[part, g, cc]
                acc = term if acc is None else acc + term
            return acc

        for part in range(3):
            for g, (ref, d) in enumerate(zip(groups[part], dilations)):
                if d > 1:
                    _rows_from_slabs(ref, scr.at[part, g], d)
        for cc in range(chunks):
            for part in range(2):
                o_ref[:, part * ATT_W + cc * LANE:part * ATT_W + (cc + 1) * LANE] = _rotate(
                    total(part, cc), c_ref[...], -a_ref[...], -b_ref[...]).astype(o_ref.dtype)
            o_ref[:, 2 * ATT_W + cc * LANE:2 * ATT_W + (cc + 1) * LANE] = total(2, cc).astype(o_ref.dtype)

    view = lambda d: pl.BlockSpec((tm // d, d * ATT_W), lambda i: (i, 0))
    tab = pl.BlockSpec((tm, LANE), lambda i: (i, 0))
    return pl.pallas_call(
        body, name="dil_grad_combine", grid=(SEQ // tm,),
        in_specs=[view(d) for d in dilations] * 3 + [tab] * 3 + [ANY],
        out_specs=pl.BlockSpec((tm, 3 * ATT_W), lambda i: (i, 1)),
        out_shape=_sds((SEQ, Z_MAIN), BF16),
        input_output_aliases={12: 0},
        scratch_shapes=[pltpu.VMEM((3, len(dilations), chunks, tm, LANE), F32)],
        compiler_params=_params("parallel"),
    )(*dqs, *dks, *dvs, *tabs, dz)


def _dil_valid(n):
    qi = lax.broadcasted_iota(jnp.int32, (DIL_BLK, 2 * DIL_BLK), 0)
    ki = lax.broadcasted_iota(jnp.int32, (DIL_BLK, 2 * DIL_BLK), 1)
    dist = qi + DIL_BLK - ki
    return (dist >= 0) & (dist <= DIL_BLK) & ((n > 0) | (ki >= DIL_BLK))


def _dil_fwd(qkv_v, d):
    length = SEQ // d
    nb = length // DIL_BLK
    nsub = min(DIL_STEP_BLOCKS, nb)

    def body(q_ref, kp_ref, kc_ref, vp_ref, vc_ref, o_ref, lse_ref):
        m_step = pl.program_id(1)
        lane = lax.broadcasted_iota(jnp.int32, (DIL_BLK, LANE), 1)
        jobs = [(sub, h) for sub in range(nsub) for h in range(N_HEADS)]
        rows = lambda sub: slice(sub * DIL_BLK, (sub + 1) * DIL_BLK)
        cols = lambda h: slice(h * HEAD_DIM, (h + 1) * HEAD_DIM)

        def keys(prev_ref, cur_ref, sub, h):
            before = prev_ref[:, cols(h)] if sub == 0 else cur_ref[rows(sub - 1), cols(h)]
            return jnp.concatenate([before, cur_ref[rows(sub), cols(h)]], axis=0)

        scores = [lax.dot_general(q_ref[rows(sub), cols(h)], keys(kp_ref, kc_ref, sub, h), NT_DIMS,
                                  preferred_element_type=F32) for sub, h in jobs]
        ok = [_dil_valid(m_step)] + [_dil_valid(1)] * (nsub - 1)
        probs, inv_l, lse_all = [], [], [jnp.zeros((DIL_BLK, LANE), F32)] * nsub
        for idx, (sub, h) in enumerate(jobs):
            s = jnp.where(ok[sub], scores[idx], NEG_INF)
            m = jnp.max(s, axis=-1, keepdims=True)
            p = jnp.exp(s - m)
            l = jnp.sum(p, axis=-1, keepdims=True)
            probs.append(p.astype(BF16))
            inv_l.append(1.0 / l)
            lse_all[sub] = jnp.where(lane == h, m + jnp.log(l), lse_all[sub])
        outs = [jnp.dot(probs[idx], keys(vp_ref, vc_ref, sub, h), preferred_element_type=F32) * inv_l[idx]
                for idx, (sub, h) in enumerate(jobs)]
        for sub in range(nsub):
            o_ref[rows(sub), :] = jnp.concatenate(outs[sub * N_HEADS:(sub + 1) * N_HEADS], axis=1).astype(o_ref.dtype)
            lse_ref[rows(sub), :] = lse_all[sub]

    pair = lambda f: pl.BlockSpec((nsub * DIL_BLK, ATT_W), f)
    one = lambda f: pl.BlockSpec((DIL_BLK, ATT_W), f)
    before = lambda m: jnp.maximum(nsub * m - 1, 0)
    o, lse = pl.pallas_call(
        body, name=f"dil_fwd_d{d}", grid=(d, nb // nsub),
        in_specs=[pair(lambda r, m: (m, 3 * r)),
                  one(lambda r, m: (before(m), 3 * r + 1)), pair(lambda r, m: (m, 3 * r + 1)),
                  one(lambda r, m: (before(m), 3 * r + 2)), pair(lambda r, m: (m, 3 * r + 2))],
        out_specs=[pair(lambda r, m: (m, r)), pl.BlockSpec((nsub * DIL_BLK, LANE), lambda r, m: (m, r))],
        out_shape=[_sds((length, d * ATT_W), BF16), _sds((length, d * LANE), F32)],
        compiler_params=_params("parallel", "arbitrary"),
    )(qkv_v, qkv_v, qkv_v, qkv_v, qkv_v)
    return o, lse


def _rows_from_slabs(view_ref, scr, d):
    chunks, rows = scr.shape[0], scr.shape[1]
    for r in range(d):
        for ch in range(chunks):
            lo = (r * chunks + ch) * LANE
            scr.at[ch][pl.ds(r, rows // d, stride=d), :] = view_ref[:, lo:lo + LANE].astype(F32)


def _slabs_from_rows(scr, view_ref, d):
    chunks, rows = scr.shape[0], scr.shape[1]
    for r in range(d):
        for ch in range(chunks):
            lo = (r * chunks + ch) * LANE
            view_ref[:, lo:lo + LANE] = scr.at[ch][pl.ds(r, rows // d, stride=d), :].astype(view_ref.dtype)


def _dil_merge(os_, lses, *, tm=512):
    dilations = [d for _, d in DIL_PATTERNS]
    o_chunks = ATT_W // LANE

    def body(o0, o1, o2, l0, l1, l2, y_ref, lse_ref, o_scr, l_scr):
        os_nat, ls = [], []
        for g, (o_ref, l_ref, d) in enumerate(zip((o0, o1, o2), (l0, l1, l2), dilations)):
            if d == 1:
                os_nat.append(o_ref[...].astype(F32))
                ls.append(l_ref[...])
            else:
                _rows_from_slabs(o_ref, o_scr.at[g], d)
                _rows_from_slabs(l_ref, l_scr.at[g], d)
                os_nat.append(jnp.concatenate([o_scr[g, ch] for ch in range(o_chunks)], axis=1))
                ls.append(l_scr[g, 0])
        m = jnp.maximum(jnp.maximum(ls[0], ls[1]), ls[2])
        es = [jnp.exp(l - m) for l in ls]
        tot = es[0] + es[1] + es[2]
        lse_ref[...] = m + jnp.log(tot)
        alphas = [e / tot for e in es]
        outs = []
        for h in range(N_HEADS):
            acc = None
            for g in range(3):
                term = alphas[g][:, h:h + 1] * _head(os_nat[g], h)
                acc = term if acc is None else acc + term
            outs.append(acc)
        y_ref[...] = jnp.concatenate(outs, axis=1).astype(y_ref.dtype)

    row = pl.BlockSpec((tm, ATT_W), lambda i: (i, 0))
    vec = pl.BlockSpec((tm, LANE), lambda i: (i, 0))
    view = lambda d, w: pl.BlockSpec((tm // d, d * w), lambda i: (i, 0))
    return pl.pallas_call(
        body, name="dil_merge", grid=(SEQ // tm,),
        in_specs=[view(d, ATT_W) for d in dilations] + [view(d, LANE) for d in dilations], out_specs=[row, vec],
        out_shape=[_sds((SEQ, ATT_W), BF16), _sds((SEQ, LANE), F32)],
        scratch_shapes=[pltpu.VMEM((3, o_chunks, tm, LANE), F32), pltpu.VMEM((3, 1, tm, LANE), F32)],
        compiler_params=_params("parallel"),
    )(*os_, *lses)


def _dil_bwd(qkv_v, do_v, lse_v, dl_v, d):
    length = SEQ // d
    nb = length // DIL_BLK
    nsub = min(DIL_STEP_BLOCKS, nb)
    n_steps = nb // nsub

    def body(q_ref, kp_ref, kc_ref, vp_ref, vc_ref, lse_ref, dl_ref, do_ref, dq_ref, dk_ref, dv_ref, dk_s, dv_s):
        m_step = pl.program_id(1)

        @pl.when(m_step == 0)
        def _():
            dk_s[...] = jnp.zeros_like(dk_s)
            dv_s[...] = jnp.zeros_like(dv_s)

        jobs = [(sub, h) for sub in range(nsub) for h in range(N_HEADS)]
        rows = lambda sub: slice(sub * DIL_BLK, (sub + 1) * DIL_BLK)
        cols = lambda h: slice(h * HEAD_DIM, (h + 1) * HEAD_DIM)

        def keys(prev_ref, cur_ref, sub, h):
            before = prev_ref[:, cols(h)] if sub == 0 else cur_ref[rows(sub - 1), cols(h)]
            return jnp.concatenate([before, cur_ref[rows(sub), cols(h)]], axis=0)

        kks = [keys(kp_ref, kc_ref, sub, h) for sub, h in jobs]
        scores = [lax.dot_general(q_ref[rows(sub), cols(h)], kks[idx], NT_DIMS, preferred_element_type=F32)
                  for idx, (sub, h) in enumerate(jobs)]
        dps = [lax.dot_general(do_ref[rows(sub), cols(h)], keys(vp_ref, vc_ref, sub, h), NT_DIMS,
                               preferred_element_type=F32) for sub, h in jobs]
        ok = [_dil_valid(m_step)] + [_dil_valid(1)] * (nsub - 1)
        ps, dss = [], []
        for idx, (sub, h) in enumerate(jobs):
            p = jnp.where(ok[sub], jnp.exp(scores[idx] - lse_ref[rows(sub), h:h + 1]), 0.0)
            ps.append(p.astype(BF16))
            dss.append((p * (dps[idx] - dl_ref[rows(sub), h:h + 1])).astype(BF16))
        dqs = [jnp.dot(dss[idx], kks[idx], preferred_element_type=F32) * SCALE for idx in range(len(jobs))]
        dkks = [lax.dot_general(dss[idx], q_ref[rows(sub), cols(h)], TN_DIMS, preferred_element_type=F32)
                for idx, (sub, h) in enumerate(jobs)]
        dvvs = [lax.dot_general(ps[idx], do_ref[rows(sub), cols(h)], TN_DIMS, preferred_element_type=F32)
                for idx, (sub, h) in enumerate(jobs)]
        for sub in range(nsub):
            dq_ref[rows(sub), :] = jnp.concatenate(dqs[sub * N_HEADS:(sub + 1) * N_HEADS], axis=1).astype(dq_ref.dtype)
        base = m_step * (nsub * DIL_BLK)
        blocks = [pl.ds(pl.multiple_of(jnp.maximum(base - DIL_BLK, 0), DIL_BLK), DIL_BLK)]
        blocks += [pl.ds(pl.multiple_of(base + s * DIL_BLK, DIL_BLK), DIL_BLK) for s in range(nsub)]
        for acc, parts in ((dk_s, dkks), (dv_s, dvvs)):
            top = lambda sub: jnp.concatenate([parts[sub * N_HEADS + h][:DIL_BLK] for h in range(N_HEADS)], axis=1)
            bottom = lambda sub: jnp.concatenate([parts[sub * N_HEADS + h][DIL_BLK:] for h in range(N_HEADS)], axis=1)
            acc[blocks[0], :] += top(0)
            for s in range(nsub):
                acc[blocks[s + 1], :] += bottom(s) + top(s + 1) if s + 1 < nsub else bottom(s)

        @pl.when(m_step == n_steps - 1)
        def _():
            dk_ref[...] = dk_s[...].astype(dk_ref.dtype)
            dv_ref[...] = dv_s[...].astype(dv_ref.dtype)

    pair = lambda f: pl.BlockSpec((nsub * DIL_BLK, ATT_W), f)
    one = lambda f: pl.BlockSpec((DIL_BLK, ATT_W), f)
    vec = lambda f: pl.BlockSpec((nsub * DIL_BLK, LANE), f)
    whole = pl.BlockSpec((length, ATT_W), lambda r, m: (0, r))
    before = lambda m: jnp.maximum(nsub * m - 1, 0)
    outs = pl.pallas_call(
        body, name=f"dil_bwd_d{d}", grid=(d, n_steps),
        in_specs=[pair(lambda r, m: (m, 3 * r)),
                  one(lambda r, m: (before(m), 3 * r + 1)), pair(lambda r, m: (m, 3 * r + 1)),
                  one(lambda r, m: (before(m), 3 * r + 2)), pair(lambda r, m: (m, 3 * r + 2)),
                  vec(lambda r, m: (m, r)), vec(lambda r, m: (m, r)), pair(lambda r, m: (m, r))],
        out_specs=[pair(lambda r, m: (m, r)), whole, whole],
        out_shape=[_sds((length, d * ATT_W), BF16)] * 3,
        scratch_shapes=[pltpu.VMEM((length, ATT_W), F32), pltpu.VMEM((length, ATT_W), F32)],
        compiler_params=_params("arbitrary", "arbitrary"),
    )(qkv_v, qkv_v, qkv_v, qkv_v, qkv_v, lse_v, dl_v, do_v)
    return outs


def _sigmoid(x):
    return 1.0 / (1.0 + jnp.exp(-x))


def _mix_fwd(ya, yb, w_oa, w_ob, zm, *, tm=512):
    def body(ya_ref, yb_ref, wa_ref, wb_ref, ga_ref, gb_ref, pa_ref, pb_ref, mix_ref):
        pa = jnp.dot(ya_ref[...], wa_ref[...], preferred_element_type=F32)
        pb = jnp.dot(yb_ref[...], wb_ref[...], preferred_element_type=F32)
        pa_ref[...] = pa.astype(pa_ref.dtype)
        pb_ref[...] = pb.astype(pb_ref.dtype)
        mix_ref[...] = (_sigmoid(ga_ref[...].astype(F32)) * pa + _sigmoid(gb_ref[...].astype(F32)) * pb
                        ).astype(mix_ref.dtype)

    row = pl.BlockSpec((tm, ATT_W), lambda i: (i, 0))
    wsp = pl.BlockSpec((ATT_W, D_MODEL), lambda i: (0, 0))
    wide = pl.BlockSpec((tm, D_MODEL), lambda i: (i, 0))
    return pl.pallas_call(
        body, name="mix_fwd", grid=(SEQ // tm,),
        in_specs=[row, row, wsp, wsp, pl.BlockSpec((tm, D_MODEL), lambda i: (i, 3)),
                  pl.BlockSpec((tm, D_MODEL), lambda i: (i, 4))],
        out_specs=[wide] * 3, out_shape=[_sds((SEQ, D_MODEL), BF16)] * 3,
        compiler_params=_params("parallel"),
    )(ya, yb, w_oa, w_ob, zm, zm)


def _gate_bwd(dmix, zm, p, gate_block, dz, *, name, tm=512):
    def body(dm_ref, g_ref, p_ref, *rest):
        dp_ref, dz_ref = rest[-2], rest[-1]
        dm = dm_ref[...].astype(F32)
        s = _sigmoid(g_ref[...].astype(F32))
        dp_ref[...] = (dm * s).astype(dp_ref.dtype)
        dz_ref[...] = (dm * p_ref[...].astype(F32) * s * (1.0 - s)).astype(dz_ref.dtype)

    wide = pl.BlockSpec((tm, D_MODEL), lambda i: (i, 0))
    gate = pl.BlockSpec((tm, D_MODEL), lambda i: (i, gate_block))
    extra = [] if dz is None else [dz]
    return pl.pallas_call(
        body, name=name, grid=(SEQ // tm,),
        in_specs=[wide, gate, wide] + [ANY] * len(extra),
        out_specs=[wide, gate],
        out_shape=[_sds((SEQ, D_MODEL), BF16), _sds((SEQ, Z_MAIN), BF16)],
        input_output_aliases={3: 1} if extra else {},
        compiler_params=_params("parallel"),
    )(dmix, zm, p, *extra)


def _out_fwd(mixed, w_out, x, g_post, g_pre, *, tm=512):
    def body(m_ref, w_ref, x_ref, gp_ref, gn_ref, y_ref, x2_ref, h_ref):
        y = jnp.dot(m_ref[...], w_ref[...], preferred_element_type=F32)
        y_ref[...] = y
        r = lax.rsqrt(jnp.mean(y * y, axis=-1, keepdims=True) + RMS_EPS)
        x2 = x_ref[...] + y * r * gp_ref[...]
        x2_ref[...] = x2
        r2 = lax.rsqrt(jnp.mean(x2 * x2, axis=-1, keepdims=True) + RMS_EPS)
        h_ref[...] = (x2 * r2 * gn_ref[...]).astype(h_ref.dtype)

    row = pl.BlockSpec((tm, D_MODEL), lambda i: (i, 0))
    vec = pl.BlockSpec((1, D_MODEL), lambda i: (0, 0))
    return pl.pallas_call(
        body, name="out_fwd", grid=(SEQ // tm,),
        in_specs=[row, pl.BlockSpec((D_MODEL, D_MODEL), lambda i: (0, 0)), row, vec, vec],
        out_specs=[row] * 3,
        out_shape=[_sds((SEQ, D_MODEL), F32), _sds((SEQ, D_MODEL), F32), _sds((SEQ, D_MODEL), BF16)],
        compiler_params=_params("parallel"),
    )(mixed, w_out, x, g_post, g_pre)


FFN_HALF = 256
FFN_TN = 2 * FFN_HALF
FFN_NJ = D_FF // FFN_HALF
FFN_GROUP = 2 * SUBLANE
UP_TM = 1024


def _ffn_interleave(t):
    lead = t.shape[:-1]
    return jnp.swapaxes(t.reshape(*lead, 2, FFN_NJ, FFN_HALF), -3, -2).reshape(*lead, 2 * D_FF)


def _ffn_deinterleave(t):
    lead = t.shape[:-1]
    return jnp.swapaxes(t.reshape(*lead, FFN_NJ, 2, FFN_HALF), -3, -2).reshape(*lead, 2 * D_FF)


W_IN_SHARD = (Z_MAIN + N_HEADS) // N_DEV
FORGET_LO = 3 * ATT_W


def _w_in_from_shards(shards, *, tm=256):
    def columns(g_ref, lo, width):
        p, off = divmod(lo, W_IN_SHARD)
        if off + width <= W_IN_SHARD:
            return g_ref[p, :, off:off + width]
        first = W_IN_SHARD - off
        return jnp.concatenate([g_ref[p, :, off:], g_ref[p + 1, :, :width - first]], axis=1)

    def body(g_ref, main_ref, f_ref):
        for t in range(Z_MAIN // LANE):
            lo = t * LANE
            main_ref[:, lo:lo + LANE] = columns(g_ref, lo if lo < FORGET_LO else lo + N_HEADS, LANE)
        f_ref[...] = jnp.concatenate([columns(g_ref, FORGET_LO, N_HEADS),
                                      jnp.zeros((tm, F_PAD - N_HEADS), f_ref.dtype)], axis=1)

    return pl.pallas_call(
        body, name="w_in_from_shards", grid=(D_MODEL // tm,),
        in_specs=[pl.BlockSpec((N_DEV, tm, W_IN_SHARD), lambda i: (0, i, 0))],
        out_specs=[pl.BlockSpec((tm, Z_MAIN), lambda i: (i, 0)), pl.BlockSpec((tm, F_PAD), lambda i: (i, 0))],
        out_shape=[_sds((D_MODEL, Z_MAIN), shards.dtype), _sds((D_MODEL, F_PAD), shards.dtype)],
        compiler_params=_params("parallel"),
    )(shards)


def _w_in_to_shards(g_main, g_f, *, tm=256):
    def natural(main_ref, f_ref, lo, width):
        pieces, hi = [], lo + width
        for ref, start, stop, shift in ((main_ref, 0, FORGET_LO, 0), (f_ref, FORGET_LO, FORGET_LO + N_HEADS, FORGET_LO),
                                        (main_ref, FORGET_LO + N_HEADS, Z_MAIN + N_HEADS, N_HEADS)):
            a, b = max(lo, start), min(hi, stop)
            if a < b:
                pieces.append(ref[:, a - shift:b - shift])
        return pieces[0] if len(pieces) == 1 else jnp.concatenate(pieces, axis=1)

    def body(main_ref, f_ref, o_ref):
        for p in range(N_DEV):
            for q in range(-(-W_IN_SHARD // LANE)):
                width = min(LANE, W_IN_SHARD - q * LANE)
                o_ref[p, :, q * LANE:q * LANE + width] = natural(main_ref, f_ref, p * W_IN_SHARD + q * LANE, width)

    return pl.pallas_call(
        body, name="w_in_to_shards", grid=(D_MODEL // tm,),
        in_specs=[pl.BlockSpec((tm, Z_MAIN), lambda i: (i, 0)), pl.BlockSpec((tm, F_PAD), lambda i: (i, 0))],
        out_specs=pl.BlockSpec((N_DEV, tm, W_IN_SHARD), lambda i: (0, i, 0)),
        out_shape=_sds((N_DEV, D_MODEL, W_IN_SHARD), g_main.dtype),
        compiler_params=_params("parallel"),
    )(g_main, g_f)


W_UP_SHARD = 2 * D_FF // N_DEV


def _w_up_lane_tile(k):
    block = k // 2
    return (2 * (block % FFN_NJ) + block // FFN_NJ) * FFN_HALF + (k % 2) * LANE


def _w_up_from_shards(shards, *, tm=256):
    def body(g_ref, o_ref):
        for k in range(2 * D_FF // LANE):
            p, off = divmod(k * LANE, W_UP_SHARD)
            if off + LANE <= W_UP_SHARD:
                tile = g_ref[p, :, off:off + LANE]
            else:
                tile = jnp.concatenate([g_ref[p, :, off:], g_ref[p + 1, :, :off + LANE - W_UP_SHARD]], axis=1)
            dst = _w_up_lane_tile(k)
            o_ref[:, dst:dst + LANE] = tile

    return pl.pallas_call(
        body, name="w_up_from_shards", grid=(D_MODEL // tm,),
        in_specs=[pl.BlockSpec((N_DEV, tm, W_UP_SHARD), lambda i: (0, i, 0))],
        out_specs=pl.BlockSpec((tm, 2 * D_FF), lambda i: (i, 0)),
        out_shape=_sds((D_MODEL, 2 * D_FF), shards.dtype),
        compiler_params=_params("parallel"),
    )(shards)


def _w_up_to_shards(t, *, tm=256):
    def body(x_ref, o_ref):
        for p in range(N_DEV):
            for q in range(-(-W_UP_SHARD // LANE)):
                width = min(LANE, W_UP_SHARD - q * LANE)
                k, off = divmod(p * W_UP_SHARD + q * LANE, LANE)
                src = _w_up_lane_tile(k)
                if off == 0:
                    tile = x_ref[:, src:src + width]
                else:
                    tile = x_ref[:, src + off:src + LANE]
                    if width > LANE - off:
                        nxt = _w_up_lane_tile(k + 1)
                        tile = jnp.concatenate([tile, x_ref[:, nxt:nxt + width - (LANE - off)]], axis=1)
                o_ref[p, :, q * LANE:q * LANE + width] = tile

    return pl.pallas_call(
        body, name="w_up_to_shards", grid=(D_MODEL // tm,),
        in_specs=[pl.BlockSpec((tm, 2 * D_FF), lambda i: (i, 0))],
        out_specs=pl.BlockSpec((N_DEV, tm, W_UP_SHARD), lambda i: (0, i, 0)),
        out_shape=_sds((N_DEV, D_MODEL, W_UP_SHARD), t.dtype),
        compiler_params=_params("parallel"),
    )(t)


def _gelu_parts(a):
    c = math.sqrt(2.0 / math.pi)
    a2 = a * a
    t = jnp.tanh((c * a) * (1.0 + 0.044715 * a2))
    half_a, one_t = 0.5 * a, 1.0 + t
    gelu = half_a * one_t
    dgelu = 0.5 * one_t + half_a * (1.0 - t * t) * (c + (3.0 * 0.044715 * c) * a2)
    return gelu, dgelu


def _row_masks(down):
    row = lax.broadcasted_iota(jnp.int32, (SUBLANE, FFN_TN), 0)
    return (row < 1, row < 2) if down else (row >= SUBLANE - 1, row >= SUBLANE - 2)


def _rolled(x, down):
    return (pltpu.roll(x, 1, 0), pltpu.roll(x, 2, 0)) if down else (
        pltpu.roll(x, SUBLANE - 1, 0), pltpu.roll(x, SUBLANE - 2, 0))


def _shifted(cur_rolled, neighbour_rolled, masks):
    return (jnp.where(masks[0], neighbour_rolled[0], cur_rolled[0]),
            jnp.where(masks[1], neighbour_rolled[1], cur_rolled[1]))


def _conv_consts(w_ref, b_ref):
    shape = (SUBLANE, FFN_TN)
    return [jnp.broadcast_to(w_ref[k:k + 1, :], shape) for k in range(3)] + [jnp.broadcast_to(b_ref[...], shape)]


def _up_conv_fwd(h2, w_up, conv_w, conv_b):
    nrow = SEQ // UP_TM
    n_tiles = FFN_NJ * nrow
    n_groups = UP_TM // FFN_GROUP

    def body(h_ref, wu_ref, w_ref, b_ref, u_ref, ab_ref, m_ref, ua_s, ub_s, c1_s, c2_s):
        k = pl.program_id(0)

        @pl.when(k == 0)
        def _():
            ub_s[...] = jnp.zeros_like(ub_s)

        @pl.when(jnp.maximum(k - 1, 0) % nrow == 0)
        def _():
            c1_s[...] = jnp.zeros_like(c1_s)
            c2_s[...] = jnp.zeros_like(c2_s)

        def step(write_s, read_s):
            h_rows = pl.ds(pl.multiple_of((this(k) % nrow) * UP_TM, UP_TM), UP_TM)
            u = jnp.dot(h_ref[h_rows, :], wu_ref[...], preferred_element_type=F32)
            write_s[...] = u
            u_ref[...] = u.astype(u_ref.dtype)
            w0, w1, w2, bias = _conv_consts(w_ref, b_ref)
            masks = _row_masks(True)
            above = (c1_s[...], c2_s[...])
            for g in range(n_groups):
                rows = slice(g * FFN_GROUP, (g + 1) * FFN_GROUP)
                x = read_s[rows, :]
                convs = []
                for c in range(2):
                    cur = x[c * SUBLANE:(c + 1) * SUBLANE]
                    cur_rolled = _rolled(cur, True)
                    s1, s2 = _shifted(cur_rolled, above, masks)
                    convs.append(w0 * s2 + w1 * s1 + w2 * cur + bias)
                    above = cur_rolled
                y = jnp.concatenate(convs, axis=0)
                ab_ref[rows, :] = y.astype(ab_ref.dtype)
                m_ref[rows, :] = (_gelu_parts(y[:, :FFN_HALF])[0] * y[:, FFN_HALF:]).astype(m_ref.dtype)
            c1_s[...], c2_s[...] = above

        @pl.when(k % 2 == 0)
        def _():
            step(ua_s, ub_s)

        @pl.when(k % 2 == 1)
        def _():
            step(ub_s, ua_s)

    this = lambda k: jnp.minimum(k, n_tiles - 1)
    last = lambda k: jnp.maximum(k - 1, 0)
    blk = lambda tile: pl.BlockSpec((UP_TM, FFN_TN), lambda k: (tile(k) % nrow, tile(k) // nrow))
    return pl.pallas_call(
        body, name="up_conv_fwd", grid=(n_tiles + 1,),
        in_specs=[pl.BlockSpec((SEQ, D_MODEL), lambda k: (0, 0)),
                  pl.BlockSpec((D_MODEL, FFN_TN), lambda k: (0, this(k) // nrow)),
                  pl.BlockSpec((3, FFN_TN), lambda k: (0, last(k) // nrow)),
                  pl.BlockSpec((1, FFN_TN), lambda k: (0, last(k) // nrow))],
        out_specs=[blk(this), blk(last), pl.BlockSpec((UP_TM, FFN_HALF), lambda k: (last(k) % nrow, last(k) // nrow))],
        out_shape=[_sds((SEQ, 2 * D_FF), BF16), _sds((SEQ, 2 * D_FF), BF16), _sds((SEQ, D_FF), BF16)],
        scratch_shapes=[pltpu.VMEM((UP_TM, FFN_TN), F32), pltpu.VMEM((UP_TM, FFN_TN), F32),
                        pltpu.VMEM((SUBLANE, FFN_TN), F32), pltpu.VMEM((SUBLANE, FFN_TN), F32)],
        compiler_params=_params("arbitrary"),
    )(h2, w_up, conv_w, conv_b)


def _ffn_mid_bwd(dy2, w_down, u, ab, conv_w):
    nrow = SEQ // UP_TM
    n_tiles = FFN_NJ * nrow
    n_groups = UP_TM // FFN_GROUP
    this = lambda k: jnp.minimum(k, n_tiles - 1)
    last = lambda k: jnp.maximum(k - 1, 0)
    row_of = lambda tile: nrow - 1 - tile % nrow

    def body(dy_ref, wd_ref, u_ref, ab_ref, w_ref, du_ref, gw_ref, gb_ref, c_s, dma_s, dmb_s):
        k = pl.program_id(0)

        @pl.when(k == 0)
        def _():
            dmb_s[...] = jnp.zeros_like(dmb_s)

        @pl.when(last(k) % nrow == 0)
        def _():
            c_s[...] = jnp.zeros_like(c_s)
            gw_ref[...] = jnp.zeros_like(gw_ref)
            gb_ref[...] = jnp.zeros_like(gb_ref)

        def step(write_s, read_s):
            dy_rows = pl.ds(pl.multiple_of(row_of(this(k)) * UP_TM, UP_TM), UP_TM)
            write_s[...] = lax.dot_general(dy_ref[dy_rows, :], wd_ref[...], NT_DIMS,
                                           preferred_element_type=F32)
            taps = [jnp.broadcast_to(w_ref[t:t + 1, :], (SUBLANE, FFN_TN)) for t in range(3)]
            masks = _row_masks(False)
            below = _rolled(c_s[...], False)
            acc = [jnp.zeros((SUBLANE, FFN_TN), F32)] * 4
            for g in reversed(range(n_groups)):
                rows = slice(g * FFN_GROUP, (g + 1) * FFN_GROUP)
                x, y, dmv = u_ref[rows, :].astype(F32), ab_ref[rows, :].astype(F32), read_s[rows, :]
                gelu, dgelu = _gelu_parts(y[:, :FFN_HALF])
                d = jnp.concatenate([dmv * y[:, FFN_HALF:] * dgelu, dmv * gelu], axis=1)
                pre = [None, None]
                for c in (1, 0):
                    sl = slice(c * SUBLANE, (c + 1) * SUBLANE)
                    cur, xs = d[sl], x[sl]
                    cur_rolled = _rolled(cur, False)
                    up1, up2 = _shifted(cur_rolled, below, masks)
                    acc = [acc[0] + up2 * xs, acc[1] + up1 * xs, acc[2] + cur * xs, acc[3] + cur]
                    pre[c] = taps[2] * cur + taps[1] * up1 + taps[0] * up2
                    below = cur_rolled
                du_ref[rows, :] = jnp.concatenate(pre, axis=0).astype(du_ref.dtype)
            c_s[...] = pltpu.roll(below[0], 1, 0)
            for t in range(3):
                gw_ref[t:t + 1, :] += jnp.sum(acc[t], axis=0, keepdims=True)
            gb_ref[...] += jnp.sum(acc[3], axis=0, keepdims=True)

        @pl.when(k % 2 == 0)
        def _():
            step(dma_s, dmb_s)

        @pl.when(k % 2 == 1)
        def _():
            step(dmb_s, dma_s)

    blk = pl.BlockSpec((UP_TM, FFN_TN), lambda k: (row_of(last(k)), last(k) // nrow))
    col = lambda rows: pl.BlockSpec((rows, FFN_TN), lambda k: (0, last(k) // nrow))
    return pl.pallas_call(
        body, name="ffn_mid_bwd", grid=(n_tiles + 1,),
        in_specs=[pl.BlockSpec((SEQ, D_MODEL), lambda k: (0, 0)),
                  pl.BlockSpec((FFN_HALF, D_MODEL), lambda k: (this(k) // nrow, 0)), blk, blk, col(3)],
        out_specs=[blk, col(3), col(1)],
        out_shape=[_sds((SEQ, 2 * D_FF), BF16), _sds((3, 2 * D_FF), F32), _sds((1, 2 * D_FF), F32)],
        scratch_shapes=[pltpu.VMEM((SUBLANE, FFN_TN), F32), pltpu.VMEM((UP_TM, FFN_HALF), F32),
                        pltpu.VMEM((UP_TM, FFN_HALF), F32)],
        compiler_params=_params("arbitrary"),
    )(dy2, w_down, u, ab, conv_w)


def _down_fwd(m, w_down, x2, g_post, target, *, tm=512):
    def body(m_ref, w_ref, x2_ref, g_ref, t_ref, dout_ref, dy_ref, gg_ref, loss_ref):
        @pl.when(pl.program_id(0) == 0)
        def _():
            gg_ref[...] = jnp.zeros_like(gg_ref)
            loss_ref[...] = jnp.zeros_like(loss_ref)

        y = jnp.dot(m_ref[...], w_ref[...], preferred_element_type=F32)
        r = lax.rsqrt(jnp.mean(y * y, axis=-1, keepdims=True) + RMS_EPS)
        yn = y * r
        diff = (x2_ref[...] + yn * g_ref[...]) - t_ref[...]
        loss_ref[...] += jnp.sum(diff * diff)
        dout = diff * (1.0 / D_MODEL)
        dout_ref[...] = dout
        gg_ref[...] += jnp.sum(dout * yn, axis=0, keepdims=True)
        dn = dout * g_ref[...]
        dy_ref[...] = (r * (dn - yn * jnp.mean(dn * yn, axis=-1, keepdims=True))).astype(dy_ref.dtype)

    row = pl.BlockSpec((tm, D_MODEL), lambda i: (i, 0))
    vec = pl.BlockSpec((1, D_MODEL), lambda i: (0, 0))
    return pl.pallas_call(
        body, name="down_fwd", grid=(SEQ // tm,),
        in_specs=[pl.BlockSpec((tm, D_FF), lambda i: (i, 0)), pl.BlockSpec((D_FF, D_MODEL), lambda i: (0, 0)),
                  row, vec, row],
        out_specs=[row, row, vec, pl.BlockSpec((1, LANE), lambda i: (0, 0))],
        out_shape=[_sds((SEQ, D_MODEL), F32), _sds((SEQ, D_MODEL), BF16), _sds((1, D_MODEL), F32),
                   _sds((1, LANE), F32)],
        compiler_params=_params("arbitrary"),
    )(m, w_down, x2, g_post, target)


def _local_step(x, target, w_main, w_f, b_forget, conv_b, g_pre_mix, g_post_mix, g_pre_ffn, g_post_ffn,
                proj_weights, ffn_weights, ffn_grads_ready, proj_grads_ready, mixer_grads_ready, after=None):
    mm = _matmul
    tabs = _rope_tables()

    h1 = _rms_fwd(x, g_pre_mix, name="rms_pre_mix", after=after)
    zm = mm(h1, w_main, out_dtype=BF16, tm=2048, tn=1024, tk=1024, name="in_proj")
    zf = mm(h1, w_f, out_dtype=F32, tm=2048, tn=F_PAD, tk=1024, name="in_proj_forget")
    f_row, sg_row = _fox_prep(zf[:, :N_HEADS].T, b_forget.reshape(N_HEADS, 1))
    f_cols = jnp.pad(f_row.T, ((0, 0), (0, LANE - N_HEADS)))
    q_slots, k_slots, v_slots = _fox_pack_fwd(zm, f_cols)
    ya, lse_a = _fox_fwd(q_slots, k_slots, v_slots)
    qkv_d = dict(zip([d for _, d in DIL_PATTERNS], _rope_fwd(zm, tabs)))
    dil = [_dil_fwd(qkv_d[d], d) for _, d in DIL_PATTERNS]
    yb, lse_b = _dil_merge([o for o, _ in dil], [l for _, l in dil])
    w_oa, w_ob, w_out = proj_weights(yb)
    pa, pb, mixed = _mix_fwd(ya, yb, w_oa, w_ob, zm)
    y1, x2, h2 = _out_fwd(mixed, w_out, x, g_post_mix, g_pre_ffn)
    w_up, conv_w, w_down = ffn_weights(h2)
    u, ab, m = _up_conv_fwd(h2, w_up, conv_w, _ffn_interleave(conv_b))
    dout, dy2, gg_post_ffn, sq_err = _down_fwd(m, w_down, x2, g_post_ffn, target)

    g_w_down = mm(m, dy2, ta=True, out_dtype=BF16, tm=D_FF // 2, tn=1024, tk=SEQ, name="grad_w_down")
    du, g_conv_w, g_conv_b = _ffn_mid_bwd(dy2, w_down, u, ab, conv_w)
    g_w_up = mm(h2, du, ta=True, out_dtype=BF16, tm=1024, tn=D_FF // 2, tk=SEQ, name="grad_w_up")
    tok = ffn_grads_ready(dict(w_down=g_w_down, w_up_blocks=g_w_up, conv_w=_ffn_deinterleave(g_conv_w)))
    dh2 = mm(du, w_up, tb=True, out_dtype=BF16, tm=512, tn=1024, tk=2 * D_FF, name="d_h2")

    dx2, dy1, gg_pre_ffn, gg_post_mix = _rms_pair_bwd([dh2], x2, g_pre_ffn, dout, y1, g_post_mix, after=tok)
    g_w_out = mm(mixed, dy1, ta=True, out_dtype=BF16, tm=1024, tn=1024, tk=SEQ, name="grad_w_out")
    dmix = mm(dy1, w_out, tb=True, out_dtype=BF16, tm=2048, tn=1024, tk=1024, name="d_mixed")
    dpa, dz = _gate_bwd(dmix, zm, pa, 3, None, name="gate_bwd_fox")
    dpb, dz = _gate_bwd(dmix, zm, pb, 4, dz, name="gate_bwd_dil")
    g_w_oa = mm(ya, dpa, ta=True, out_dtype=BF16, tm=512, tn=1024, tk=SEQ, name="grad_w_o_fox", col_slots=N_DEV)
    g_w_ob = mm(yb, dpb, ta=True, out_dtype=BF16, tm=512, tn=1024, tk=SEQ, name="grad_w_o_dil", col_slots=N_DEV)
    tok = proj_grads_ready(dict(w_o_fox=g_w_oa, w_o_dil=g_w_ob, w_out=g_w_out))
    dya = mm(dpa, w_oa, tb=True, out_dtype=BF16, tm=2048, tn=512, tk=1024, name="d_y_fox")
    dyb = mm(dpb, w_ob, tb=True, out_dtype=BF16, tm=2048, tn=512, tk=1024, name="d_y_dil")

    qb_slots, do_slots = _fox_pack_bwd(zm, f_cols, lse_a, ya, dya, after=tok)
    dz, df_cols = _fox_unpack(*_fox_bwd(qb_slots, k_slots, v_slots, do_slots), dz)
    dfa_t, g_b_forget = _fox_post_bwd(df_cols[:, :N_HEADS].T, sg_row)

    rows_d = _dil_bwd_prep(yb, dyb, lse_b)
    dil_g = [_dil_bwd(qkv_d[d], *rows_d[k], d) for k, (_, d) in enumerate(DIL_PATTERNS)]
    dz = _dil_grad_combine([g[0] for g in dil_g], [g[1] for g in dil_g], [g[2] for g in dil_g], tabs, dz)

    dzf = jnp.pad(dfa_t.T, ((0, 0), (0, F_PAD - N_HEADS)))
    g_w_main = mm(h1, dz, ta=True, out_dtype=BF16, tm=1024, tn=Z_MAIN // 4, tk=SEQ, name="grad_w_in")
    g_w_f = mm(h1, dzf, ta=True, out_dtype=BF16, tm=1024, tn=F_PAD, tk=1024, name="grad_w_in_forget")
    tok = mixer_grads_ready(dict(w_main=g_w_main, w_f=g_w_f))
    dh1 = [mm(dz, w_main, tb=True, out_dtype=BF16, tm=512, tn=1024, tk=Z_MAIN, name="d_h1", after=tok),
           mm(dzf, w_f, tb=True, out_dtype=BF16, tm=2048, tn=1024, tk=F_PAD, name="d_h1_forget", after=tok)]
    grad_x, gg_pre_mix = _rms_bwd(dh1, x, g_pre_mix, dx2, out_dtype=F32, name="rms_pre_mix_bwd")

    grads = dict(
        b_forget=g_b_forget.reshape(1, N_HEADS), conv_b=_ffn_deinterleave(g_conv_b),
        g_pre_mix=gg_pre_mix, g_post_mix=gg_post_mix, g_pre_ffn=gg_pre_ffn, g_post_ffn=gg_post_ffn)
    return sq_err, grad_x, grads


def _gather_two_level(shard, *, name):
    def body(x_ref, out_ref, send_sems, recv_sems, local_sem):
        x, y, c = lax.axis_index("x"), lax.axis_index("y"), lax.axis_index("c")
        me, sibling = (x, y, c), (x, y, 1 - c)
        chips = [(1 - x, y), (x, 1 - y), (1 - x, 1 - y)]

        def slot(px, py, pc):
            return out_ref.at[4 * px + 2 * py + pc]

        def copy(k, block, to, src=None):
            return pltpu.make_async_remote_copy(
                src_ref=slot(*block) if src is None else src, dst_ref=slot(*block),
                send_sem=send_sems.at[k], recv_sem=recv_sems.at[k], device_id=to, device_id_type=MESH_ID)

        mine = pltpu.make_async_copy(x_ref, slot(*me), local_sem)
        mine.start()
        first = [copy(0, me, sibling, src=x_ref)]
        first += [copy(1 + j, me, (*chip, c), src=x_ref) for j, chip in enumerate(chips)]
        for cp in first:
            cp.start()
        passed = [copy(4 + j, (*chip, c), sibling) for j, chip in enumerate(chips)]
        for j, chip in enumerate(chips):
            copy(1 + j, (*chip, c), me).wait_recv()
            passed[j].start()
        copy(0, sibling, me).wait_recv()
        for j, chip in enumerate(chips):
            copy(4 + j, (*chip, 1 - c), me).wait_recv()
        for cp in first + passed:
            cp.wait_send()
        mine.wait()

    return pl.pallas_call(
        body, name=name, in_specs=[ANY], out_specs=ANY, out_shape=_sds((N_DEV,) + shard.shape, shard.dtype),
        scratch_shapes=[pltpu.SemaphoreType.DMA((N_DEV - 1,)), pltpu.SemaphoreType.DMA((N_DEV - 1,)),
                        pltpu.SemaphoreType.DMA],
    )(shard)


N_CHIPS = N_DEV // 2


def _peers(chips_only=False):
    x, y, c = lax.axis_index("x"), lax.axis_index("y"), lax.axis_index("c")
    out = []
    if chips_only:
        for k in range(1, N_CHIPS):
            px = 1 - x if k & 2 else x
            py = 1 - y if k & 1 else y
            out.append(((px, py, c), 2 * px + py))
        return 2 * x + y, out
    for k in range(1, N_DEV):
        px = 1 - x if k & 4 else x
        py = 1 - y if k & 2 else y
        pc = 1 - c if k & 1 else c
        out.append(((px, py, pc), 4 * px + 2 * py + pc))
    return 4 * x + 2 * y + c, out


def _sibling_swap(slot_arrays, *, name):
    n = len(slot_arrays)

    def body(*refs):
        ins, outs, send_sems, recv_sems = refs[:n], refs[n:2 * n], refs[2 * n], refs[2 * n + 1]
        x, y, c = lax.axis_index("x"), lax.axis_index("y"), lax.axis_index("c")
        copies = [pltpu.make_async_remote_copy(
            src_ref=ins[a].at[2 * q + (1 - c)], dst_ref=outs[a].at[q], send_sem=send_sems.at[a, q],
            recv_sem=recv_sems.at[a, q], device_id=(x, y, 1 - c), device_id_type=MESH_ID)
            for a in range(n) for q in range(N_CHIPS)]
        for cp in copies:
            cp.start()
        for cp in copies:
            cp.wait_recv()
        for cp in copies:
            cp.wait_send()

    return pl.pallas_call(
        body, name=name, in_specs=[ANY] * n, out_specs=[ANY] * n,
        out_shape=[_sds((N_CHIPS,) + t.shape[1:], t.dtype) for t in slot_arrays],
        scratch_shapes=[pltpu.SemaphoreType.DMA((n, N_CHIPS)), pltpu.SemaphoreType.DMA((n, N_CHIPS))],
    )(*slot_arrays)


def _pair_sum(slots, from_sibling, *, name, tn):
    _, r, c = slots.shape
    core = lax.axis_index("c").astype(jnp.int32).reshape(1)

    def body(core_ref, a_ref, b_ref, o_ref):
        o_ref[...] = (a_ref[...].astype(F32) + b_ref[...].astype(F32)).astype(o_ref.dtype)

    blk = lambda f: pl.BlockSpec((1, r, tn), f)
    return pl.pallas_call(
        body, name=name,
        grid_spec=pltpu.PrefetchScalarGridSpec(
            num_scalar_prefetch=1, grid=(N_CHIPS, c // tn),
            in_specs=[blk(lambda q, j, core: (2 * q + core[0], 0, j)), blk(lambda q, j, core: (q, 0, j))],
            out_specs=blk(lambda q, j, core: (q, 0, j))),
        out_shape=_sds((N_CHIPS, r, c), slots.dtype),
        compiler_params=_params("parallel", "parallel"),
    )(core, slots, from_sibling)


HBM = pl.BlockSpec(memory_space=pltpu.HBM)
SEM = pl.BlockSpec(memory_space=pltpu.SEMAPHORE)
DATAFLOW = pltpu.SideEffectType.DATAFLOW_SIDE_EFFECTING


def _split_copy(srcs, lands, send_sems, recv_sems, scatter, a, k, me, peers, incoming=False):
    dev, slot = peers[k]
    if incoming:
        src = dst = lands[a].at[slot]
    else:
        src, dst = (srcs[a].at[slot] if scatter else srcs[a]), lands[a].at[me]
    sem = a * len(peers) + k
    return pltpu.make_async_remote_copy(
        src_ref=src, dst_ref=dst, send_sem=send_sems.at[sem], recv_sem=recv_sems.at[sem],
        device_id=dev, device_id_type=MESH_ID)


def _own_copy(srcs, lands, own_sems, scatter, a, me):
    return pltpu.make_async_copy(srcs[a].at[me] if scatter else srcs[a], lands[a].at[me], own_sems.at[a])


def _exchange_start(arrays, scatter, *, name, chips_only=False, after=None):
    n = len(arrays)
    n_slots = N_CHIPS if chips_only else N_DEV
    n_in = 2 * n + len(_also(after))

    def body(*refs):
        srcs, lands = refs[:n], refs[n:2 * n]
        send_sems, recv_sems, own_sems = refs[n_in:n_in + 3]
        token = refs[-1]
        me, peers = _peers(chips_only)
        for k in range(len(peers)):
            for a in range(n):
                _split_copy(srcs, lands, send_sems, recv_sems, scatter, a, k, me, peers).start()
        for a in range(n):
            _own_copy(srcs, lands, own_sems, scatter, a, me).start()
        token[...] = jnp.zeros_like(token)

    land_shapes = [((n_slots,) + a.shape[-2:], a.dtype) for a in arrays]
    sems = pltpu.SemaphoreType.DMA((n * (n_slots - 1),))
    outs = pl.pallas_call(
        body, name=name,
        out_shape=(sems, sems, pltpu.SemaphoreType.DMA((n,)), *[pltpu.HBM(a.shape, a.dtype) for a in arrays],
                   *[pltpu.HBM(s, d) for s, d in land_shapes], _sds((SUBLANE, LANE), F32)),
        in_specs=[HBM] * (2 * n) + [ANY] * len(_also(after)),
        out_specs=(SEM, SEM, SEM, *[HBM] * (2 * n), pl.BlockSpec(memory_space=pltpu.VMEM)),
        input_output_aliases={i: 3 + i for i in range(2 * n)},
        compiler_params=pltpu.CompilerParams(has_side_effects=DATAFLOW),
    )(*[pltpu.with_memory_space_constraint(a, pltpu.HBM) for a in arrays],
      *[pltpu.with_memory_space_constraint(lax.empty(s, d), pltpu.HBM) for s, d in land_shapes], *_also(after))
    return (outs[:3], outs[3:3 + n], outs[3 + n:3 + 2 * n], scatter, chips_only), outs[-1]


def _exchange_wait(handles, after, *, name):
    sems, srcs, lands, scatter, chips_only = handles
    n = len(srcs)

    def body(*refs):
        src_refs, land_refs = refs[:n], refs[n:2 * n]
        send_ref, recv_ref, own_ref = refs[2 * n:2 * n + 3]
        me, peers = _peers(chips_only)
        for k in range(len(peers)):
            for a in range(n):
                _split_copy(src_refs, land_refs, send_ref, recv_ref, scatter, a, k, me, peers).wait_send()
                _split_copy(src_refs, land_refs, send_ref, recv_ref, scatter, a, k, me, peers, True).wait_recv()
        for a in range(n):
            _own_copy(src_refs, land_refs, own_ref, scatter, a, me).wait()

    outs = pl.pallas_call(
        body, name=name,
        out_shape=tuple(pltpu.HBM(t.shape, t.dtype) for t in (*srcs, *lands)),
        in_specs=[HBM] * (2 * n) + [SEM, SEM, SEM, pl.BlockSpec(memory_space=pl.ANY)],
        out_specs=tuple([HBM] * (2 * n)),
        input_output_aliases={i: i for i in range(2 * n)},
        compiler_params=pltpu.CompilerParams(has_side_effects=DATAFLOW),
    )(*srcs, *lands, *sems, after)
    return outs[n:]


def _adamw(parts, w, m, v, *, name, tm):
    r, c = w.shape
    assert r % tm == 0

    def body(p_ref, w_ref, m_ref, v_ref, g_ref, d_ref, nm_ref, nv_ref):
        _adamw_update(p_ref, w_ref, m_ref, v_ref, g_ref, d_ref, nm_ref, nv_ref)

    blk = pl.BlockSpec((tm, c), lambda i: (i, 0))
    return pl.pallas_call(
        body, name=name, grid=(r // tm,),
        in_specs=[pl.BlockSpec((parts.shape[0], tm, c), lambda i: (0, i, 0)), blk, blk, blk],
        out_specs=[blk] * 4, out_shape=[_sds((r, c), F32)] * 4,
        compiler_params=_params("parallel"),
    )(parts, w, m, v)


def _adamw_update(p_ref, w_ref, m_ref, v_ref, g_ref, d_ref, nm_ref, nv_ref):
    g = p_ref[0].astype(F32)
    for s in range(1, p_ref.shape[0]):
        g = g + p_ref[s].astype(F32)
    g_ref[...] = g
    m_new = ADAM_B1 * m_ref[...] + (1.0 - ADAM_B1) * g
    v_new = ADAM_B2 * v_ref[...] + (1.0 - ADAM_B2) * (g * g)
    nm_ref[...] = m_new
    nv_ref[...] = v_new
    m_hat = m_new / (1.0 - ADAM_B1 ** ADAM_STEP)
    v_hat = v_new / (1.0 - ADAM_B2 ** ADAM_STEP)
    d_ref[...] = -ADAM_LR * (m_hat / (jnp.sqrt(v_hat) + ADAM_EPS) + ADAM_WD * w_ref[...])


SMALL = ("g_pre_mix", "b_forget", "g_post_mix", "g_pre_ffn", "conv_b", "g_post_ffn")


def _adamw_small(parts, ws, ms, vs, sq_err_parts):
    n = len(ws)

    def body(*refs):
        ins, sq_ref, outs, loss_ref = refs[:4 * n], refs[4 * n], refs[4 * n + 1:-1], refs[-1]
        for i in range(n):
            _adamw_update(ins[i], ins[n + i], ins[2 * n + i], ins[3 * n + i], *outs[4 * i:4 * i + 4])
        total = sq_ref[0]
        for s in range(1, N_DEV):
            total = total + sq_ref[s]
        loss_ref[...] = total * (0.5 / D_MODEL)

    res = pl.pallas_call(
        body, name="adamw_small",
        out_shape=[_sds(w.shape, F32) for w in ws for _ in range(4)] + [_sds((1, LANE), F32)],
        compiler_params=pltpu.CompilerParams(vmem_limit_bytes=VMEM_LIMIT),
    )(*parts, *ws, *ms, *vs, sq_err_parts)
    return [res[4 * i:4 * i + 4] for i in range(n)], res[-1][0, 0]


def kernel(x, g_pre_mix, w_in, b_forget, w_o_fox, w_o_dil, w_out, g_post_mix, g_pre_ffn, w_up, conv_w, conv_b, w_down, g_post_ffn, loss_target, m_g_pre_mix, m_w_in, m_b_forget, m_w_o_fox, m_w_o_dil, m_w_out, m_g_post_mix, m_g_pre_ffn, m_w_up, m_conv_w, m_conv_b, m_w_down, m_g_post_ffn, v_g_pre_mix, v_w_in, v_b_forget, v_w_o_fox, v_w_o_dil, v_w_out, v_g_post_mix, v_g_pre_ffn, v_w_up, v_conv_w, v_conv_b, v_w_down, v_g_post_ffn):
    names = ("g_pre_mix", "w_in", "b_forget", "w_o_fox", "w_o_dil", "w_out", "g_post_mix", "g_pre_ffn",
             "w_up", "conv_w", "conv_b", "w_down", "g_post_ffn")
    w = dict(g_pre_mix=g_pre_mix, w_in=w_in, b_forget=b_forget, w_o_fox=w_o_fox, w_o_dil=w_o_dil, w_out=w_out,
             g_post_mix=g_post_mix, g_pre_ffn=g_pre_ffn, w_up=w_up, conv_w=conv_w, conv_b=conv_b, w_down=w_down,
             g_post_ffn=g_post_ffn)
    m = dict(g_pre_mix=m_g_pre_mix, w_in=m_w_in, b_forget=m_b_forget, w_o_fox=m_w_o_fox, w_o_dil=m_w_o_dil,
             w_out=m_w_out, g_post_mix=m_g_post_mix, g_pre_ffn=m_g_pre_ffn, w_up=m_w_up, conv_w=m_conv_w,
             conv_b=m_conv_b, w_down=m_w_down, g_post_ffn=m_g_post_ffn)
    v = dict(g_pre_mix=v_g_pre_mix, w_in=v_w_in, b_forget=v_b_forget, w_o_fox=v_w_o_fox, w_o_dil=v_w_o_dil,
             w_out=v_w_out, g_post_mix=v_g_post_mix, g_pre_ffn=v_g_pre_ffn, w_up=v_w_up, conv_w=v_conv_w,
             conv_b=v_conv_b, w_down=v_w_down, g_post_ffn=v_g_post_ffn)
    sharded = ("w_in", "w_o_fox", "w_o_dil", "w_out", "w_up", "w_down", "conv_w")
    wire = lambda n: F32 if n == "conv_w" else BF16

    by_cols = lambda t: jnp.transpose(t, (1, 0, 2)).reshape(t.shape[1], N_DEV * t.shape[2])
    by_rows = lambda t: t.reshape(N_DEV * t.shape[1], t.shape[2])
    col_slots = lambda t: jnp.transpose(t.reshape(t.shape[0], N_DEV, t.shape[1] // N_DEV), (1, 0, 2))
    row_slots = lambda t: t.reshape(N_DEV, t.shape[0] // N_DEV, t.shape[1])
    to_slots = lambda n, t: (row_slots if n in ("w_out", "w_down") else col_slots)(t).astype(wire(n))
    shard = lambda n: w[n][0].astype(wire(n))

    w_main, w_f = _w_in_from_shards(_gather_two_level(shard("w_in"), name="gather_w_in"))
    proj_handles, proj_tok = _exchange_start(
        [shard("w_o_fox"), shard("w_o_dil"), shard("w_out")], False, name="gather_proj_start", after=w_f)
    ffn_handles, ffn_tok = _exchange_start(
        [shard("w_up"), shard("conv_w"), shard("w_down")], False, name="gather_ffn_start", after=proj_tok)

    def proj_weights(after):
        w_oa, w_ob, w_o = _exchange_wait(proj_handles, after, name="gather_proj_wait")
        return by_cols(w_oa), by_cols(w_ob), by_rows(w_o)

    def ffn_weights(after):
        w_u, conv, w_d = _exchange_wait(ffn_handles, after, name="gather_ffn_wait")
        return _w_up_from_shards(w_u), _ffn_interleave(by_cols(conv)), by_rows(w_d)

    pending = {}

    def ffn_grads_ready(g):
        slots = [to_slots("w_down", g["w_down"]), _w_up_to_shards(g["w_up_blocks"]), to_slots("conv_w", g["conv_w"])]
        pending["ffn"] = _exchange_start(slots, True, name="scatter_ffn_start")
        return pending["ffn"][1]

    def proj_grads_ready(g):
        pending["proj"] = _exchange_start([g["w_o_fox"], g["w_o_dil"], to_slots("w_out", g["w_out"])], True,
                                          name="scatter_proj_start")
        return pending["proj"][1]

    def mixer_grads_ready(g):
        slots = _w_in_to_shards(g["w_main"], g["w_f"])
        theirs = _sibling_swap([slots], name="scatter_w_in_swap")[0]
        chip_sums = _pair_sum(slots, theirs, name="scatter_w_in_pair_sum", tn=W_IN_SHARD)
        pending["w_in"] = _exchange_start([chip_sums], True, name="scatter_w_in_start", chips_only=True)
        return pending["w_in"][1]

    sq_err, grad_x, g = _local_step(
        x[0], loss_target[0], w_main, w_f, b_forget, conv_b, g_pre_mix, g_post_mix, g_pre_ffn,
        g_post_ffn, proj_weights, ffn_weights, ffn_grads_ready, proj_grads_ready, mixer_grads_ready, after=ffn_tok)

    small_handles, small_tok = _exchange_start([g[n] for n in SMALL] + [sq_err], False, name="gather_small_start")
    tiles = dict(w_in=256, w_o_fox=512, w_o_dil=512, w_out=128, w_up=256, w_down=176, conv_w=3)
    adam = lambda n, p: _adamw(p, w[n][0], m[n][0], v[n][0], name=f"adamw_{n}", tm=tiles[n])
    res = {}
    for key, group in (("ffn", ("w_down", "w_up", "conv_w")), ("proj", ("w_o_fox", "w_o_dil", "w_out"))):
        landed = _exchange_wait(pending[key][0], small_tok, name=f"scatter_{key}_wait")
        res.update({n: adam(n, p) for n, p in zip(group, landed)})
    done = res["w_up"][3]
    res["w_in"] = adam("w_in", _exchange_wait(pending["w_in"][0], done, name="scatter_w_in_wait")[0])
    small_parts = _exchange_wait(small_handles, res["w_in"][3], name="gather_small_wait")
    small, loss = _adamw_small(small_parts[:-1], *[[t[n] for n in SMALL] for t in (w, m, v)], small_parts[-1])
    small = dict(zip(SMALL, small))
    out = [[(res[n][k][None] if n in sharded else small[n][k]) for n in names] for k in range(4)]
    return (loss, grad_x[None], *out[0], *out[1], *out[2], *out[3])
```

```python
import functools
import math

import jax
import jax.numpy as jnp
import numpy as np
from jax import lax
from jax.experimental import pallas as pl
from jax.experimental.pallas import tpu as pltpu

F32 = jnp.float32
BF16 = jnp.bfloat16

SEQ = 4096
D_MODEL = 1024
N_HEADS = 8
HEAD_DIM = 64
ATT_W = N_HEADS * HEAD_DIM
D_FF = 2816
Z_MAIN = 5120
F_PAD = 128
ROPE_DIM = 16
ROPE_THETA = 500000.0
RMS_EPS = 1e-6
NEG_INF = -1e30
SCALE = 1.0 / math.sqrt(HEAD_DIM)
DIL_PATTERNS = ((128, 1), (512, 4), (2048, 16))
DIL_BLK = 128
DIL_STEP_BLOCKS = 2
N_DEV = 8

ADAM_LR = 0.001
ADAM_B1 = 0.9
ADAM_B2 = 0.999
ADAM_EPS = 1e-08
ADAM_WD = 0.01
ADAM_STEP = 10

LANE = 128
SUBLANE = 8
VMEM_LIMIT = 56 * 1024 * 1024
MESH_ID = pl.DeviceIdType.MESH
ANY = pl.BlockSpec(memory_space=pl.ANY)


def _params(*sem):
    return pltpu.CompilerParams(dimension_semantics=sem, vmem_limit_bytes=VMEM_LIMIT)


def _sds(shape, dtype):
    return jax.ShapeDtypeStruct(shape, dtype)


def _also(after):
    return [] if after is None else [after]


def _matmul(a, b, *, ta=False, tb=False, out_dtype, tm, tn, tk, name, b_k_off=0, after=None, col_slots=1):
    n_after = len(_also(after))
    if ta:
        kk, m = a.shape
    else:
        m, kk = a.shape
    n = b.shape[0] if tb else b.shape[1]
    tm, tn, tk = min(tm, m), min(tn, n), min(tk, kk)
    assert (b.shape[1] if tb else b.shape[0]) >= b_k_off * tk + kk
    assert m % tm == 0 and n % tn == 0 and kk % tk == 0, (name, m, n, kk, tm, tn, tk)
    nk = kk // tk
    dims = (((0 if ta else 1,), (1 if tb else 0,)), ((), ()))
    slot_w = n // col_slots
    assert col_slots == 1 or (nk == 1 and tn == n and slot_w % LANE == 0), name

    def body(a_ref, b_ref, *rest):
        o_ref, scratch = rest[n_after], rest[n_after + 1:]
        p = lax.dot_general(a_ref[...].astype(BF16), b_ref[...].astype(BF16), dims,
                            preferred_element_type=F32)
        if col_slots > 1:
            for s in range(col_slots):
                o_ref[s] = p[:, s * slot_w:(s + 1) * slot_w].astype(o_ref.dtype)
        elif nk == 1:
            o_ref[...] = p.astype(o_ref.dtype)
        else:
            acc = scratch[0]
            k = pl.program_id(2)

            @pl.when(k == 0)
            def _():
                acc[...] = p

            @pl.when(k > 0)
            def _():
                acc[...] += p

            @pl.when(k == nk - 1)
            def _():
                o_ref[...] = acc[...].astype(o_ref.dtype)

    a_spec = (pl.BlockSpec((tk, tm), lambda i, j, k: (k, i)) if ta
              else pl.BlockSpec((tm, tk), lambda i, j, k: (i, k)))
    b_spec = (pl.BlockSpec((tn, tk), lambda i, j, k: (j, k + b_k_off)) if tb
              else pl.BlockSpec((tk, tn), lambda i, j, k: (k + b_k_off, j)))
    return pl.pallas_call(
        body, name=name, grid=(m // tm, n // tn, nk),
        in_specs=[a_spec, b_spec] + [ANY] * n_after,
        out_specs=(pl.BlockSpec((tm, tn), lambda i, j, k: (i, j)) if col_slots == 1
                   else pl.BlockSpec((col_slots, tm, slot_w), lambda i, j, k: (0, i, 0))),
        out_shape=_sds((m, n) if col_slots == 1 else (col_slots, m, slot_w), out_dtype),
        scratch_shapes=[pltpu.VMEM((tm, tn), F32)] if nk > 1 else [],
        compiler_params=_params("parallel", "parallel", "arbitrary"),
    )(a, b, *_also(after))


def _rms_fwd(x, g, *, name, tm=512, after=None):
    def body(x_ref, g_ref, *rest):
        h_ref = rest[-1]
        xv = x_ref[...]
        r = lax.rsqrt(jnp.mean(xv * xv, axis=-1, keepdims=True) + RMS_EPS)
        h_ref[...] = (xv * r * g_ref[...]).astype(h_ref.dtype)

    return pl.pallas_call(
        body, name=name, grid=(SEQ // tm,),
        in_specs=[pl.BlockSpec((tm, D_MODEL), lambda i: (i, 0)), pl.BlockSpec((1, D_MODEL), lambda i: (0, 0))]
        + [ANY] * len(_also(after)),
        out_specs=pl.BlockSpec((tm, D_MODEL), lambda i: (i, 0)),
        out_shape=_sds((SEQ, D_MODEL), BF16),
        compiler_params=_params("parallel"),
    )(x, g, *_also(after))


def _rms_bwd(dh_parts, xin, g, dres, *, out_dtype, name, tm=512):
    n_parts = len(dh_parts)
    has_res = dres is not None

    def body(*refs):
        parts = refs[:n_parts]
        x_ref, g_ref = refs[n_parts], refs[n_parts + 1]
        res_ref = refs[n_parts + 2] if has_res else None
        o_ref, gg_ref = refs[-2], refs[-1]
        dh = parts[0][...].astype(F32)
        for p in parts[1:]:
            dh = dh + p[...].astype(F32)
        xv = x_ref[...]
        r = lax.rsqrt(jnp.mean(xv * xv, axis=-1, keepdims=True) + RMS_EPS)
        xn = xv * r

        @pl.when(pl.program_id(0) == 0)
        def _():
            gg_ref[...] = jnp.zeros_like(gg_ref)

        gg_ref[...] += jnp.sum(dh * xn, axis=0, keepdims=True)
        dxn = dh * g_ref[...]
        dx = r * (dxn - xn * jnp.mean(dxn * xn, axis=-1, keepdims=True))
        if has_res:
            dx = dx + res_ref[...]
        o_ref[...] = dx.astype(o_ref.dtype)

    row = pl.BlockSpec((tm, D_MODEL), lambda i: (i, 0))
    vec = pl.BlockSpec((1, D_MODEL), lambda i: (0, 0))
    args = list(dh_parts) + [xin, g] + ([dres] if has_res else [])
    return pl.pallas_call(
        body, name=name, grid=(SEQ // tm,),
        in_specs=[row] * n_parts + [row, vec] + ([row] if has_res else []),
        out_specs=[row, vec],
        out_shape=[_sds((SEQ, D_MODEL), out_dtype), _sds((1, D_MODEL), F32)],
        compiler_params=_params("arbitrary"),
    )(*args)


def _rms_pair_bwd(dh_parts, x2, g_pre, dres, y1, g_post, *, tm=512, after=None):
    n_parts = len(dh_parts)

    def norm_bwd(dh, xin, g_ref, gg_ref):
        r = lax.rsqrt(jnp.mean(xin * xin, axis=-1, keepdims=True) + RMS_EPS)
        xn = xin * r
        gg_ref[...] += jnp.sum(dh * xn, axis=0, keepdims=True)
        dxn = dh * g_ref[...]
        return r * (dxn - xn * jnp.mean(dxn * xn, axis=-1, keepdims=True))

    def body(*refs):
        parts = refs[:n_parts]
        x2_ref, gpre_ref, res_ref, y1_ref, gpost_ref = refs[n_parts:n_parts + 5]
        dx2_ref, dy1_ref, ggpre_ref, ggpost_ref = refs[-4:]

        @pl.when(pl.program_id(0) == 0)
        def _():
            ggpre_ref[...] = jnp.zeros_like(ggpre_ref)
            ggpost_ref[...] = jnp.zeros_like(ggpost_ref)

        dh = parts[0][...].astype(F32)
        for p in parts[1:]:
            dh = dh + p[...].astype(F32)
        dx2 = res_ref[...] + norm_bwd(dh, x2_ref[...], gpre_ref, ggpre_ref)
        dx2_ref[...] = dx2
        dy1_ref[...] = norm_bwd(dx2, y1_ref[...], gpost_ref, ggpost_ref).astype(dy1_ref.dtype)

    row = pl.BlockSpec((tm, D_MODEL), lambda i: (i, 0))
    vec = pl.BlockSpec((1, D_MODEL), lambda i: (0, 0))
    return pl.pallas_call(
        body, name="rms_pair_bwd", grid=(SEQ // tm,),
        in_specs=[row] * n_parts + [row, vec, row, row, vec] + [ANY] * len(_also(after)),
        out_specs=[row, row, vec, vec],
        out_shape=[_sds((SEQ, D_MODEL), F32), _sds((SEQ, D_MODEL), BF16), _sds((1, D_MODEL), F32),
                   _sds((1, D_MODEL), F32)],
        compiler_params=_params("arbitrary"),
    )(*dh_parts, x2, g_pre, dres, y1, g_post, *_also(after))


SCAN_BLK = 512


def _split_dot(v, tri):
    hi = v.astype(BF16)
    r1 = v - hi.astype(F32)
    mid = r1.astype(BF16)
    lo = (r1 - mid.astype(F32)).astype(BF16)
    dot = functools.partial(jnp.dot, preferred_element_type=F32)
    return dot(hi, tri) + dot(mid, tri) + dot(lo, tri)


def _fox_prep(fa_t, b_col):
    nblk = SEQ // SCAN_BLK

    def body(fa_ref, b_ref, f_ref, sg_ref):
        row = lax.broadcasted_iota(jnp.int32, (SCAN_BLK, SCAN_BLK), 0)
        col = lax.broadcasted_iota(jnp.int32, (SCAN_BLK, SCAN_BLK), 1)
        upper = (row <= col).astype(BF16)
        carry = jnp.zeros((N_HEADS, 1), F32)
        for blk in range(nblk):
            sl = pl.ds(blk * SCAN_BLK, SCAN_BLK)
            xx = fa_ref[:, sl] + b_ref[...]
            e = jnp.exp(-jnp.abs(xx))
            logf = jnp.minimum(xx, 0.0) - jnp.log(1.0 + e)
            sg_ref[:, sl] = jnp.where(xx >= 0.0, e, 1.0) / (1.0 + e)
            c = _split_dot(logf, upper) + carry
            f_ref[:, sl] = c
            carry = c[:, SCAN_BLK - 1:SCAN_BLK]

    return pl.pallas_call(
        body, name="fox_prep",
        out_shape=[_sds((N_HEADS, SEQ), F32), _sds((N_HEADS, SEQ), F32)],
        compiler_params=pltpu.CompilerParams(vmem_limit_bytes=VMEM_LIMIT),
    )(fa_t, b_col)


def _fox_post_bwd(df_t, sg_t):
    nblk = SEQ // SCAN_BLK

    def body(df_ref, sg_ref, dfa_ref, gb_ref):
        row = lax.broadcasted_iota(jnp.int32, (SCAN_BLK, SCAN_BLK), 0)
        col = lax.broadcasted_iota(jnp.int32, (SCAN_BLK, SCAN_BLK), 1)
        lower = (row >= col).astype(BF16)
        carry = jnp.zeros((N_HEADS, 1), F32)
        gb = jnp.zeros((N_HEADS, 1), F32)
        for blk in reversed(range(nblk)):
            sl = pl.ds(blk * SCAN_BLK, SCAN_BLK)
            c = _split_dot(df_ref[:, sl], lower) + carry
            carry = c[:, 0:1]
            dfa = c * sg_ref[:, sl]
            dfa_ref[:, sl] = dfa
            gb = gb + jnp.sum(dfa, axis=1, keepdims=True)
        gb_ref[...] = gb

    return pl.pallas_call(
        body, name="fox_post_bwd",
        out_shape=[_sds((N_HEADS, SEQ), F32), _sds((N_HEADS, 1), F32)],
        compiler_params=pltpu.CompilerParams(vmem_limit_bytes=VMEM_LIMIT),
    )(df_t, sg_t)


FOX_T = 512
NT_DIMS = (((1,), (1,)), ((), ()))
TN_DIMS = (((0,), (0,)), ((), ()))


def _head(ref_or_val, h):
    return ref_or_val[:, h * HEAD_DIM:(h + 1) * HEAD_DIM]


def _split3(v):
    hi = v.astype(BF16).astype(F32)
    r1 = v - hi
    mid = r1.astype(BF16).astype(F32)
    return hi, mid, (r1 - mid).astype(BF16).astype(F32)


ONE_LANE = 3 * N_HEADS


def _pack_terms(v, with_one):
    hi, mid, lo = _split3(v)
    t = hi + pltpu.roll(mid, N_HEADS, 1) + pltpu.roll(lo, 2 * N_HEADS, 1)
    if with_one:
        t = t + (lax.broadcasted_iota(jnp.int32, v.shape, 1) == ONE_LANE).astype(F32)
    return t.astype(BF16)


def _aux_matrices():
    to_q = np.zeros((LANE, N_HEADS * 2 * HEAD_DIM), np.float32)
    to_k = np.zeros_like(to_q)
    for h in range(N_HEADS):
        base = h * 2 * HEAD_DIM + HEAD_DIM
        for s in range(3):
            to_q[s * N_HEADS + h, base + s] = 1.0
            to_q[ONE_LANE, base + 3 + s] = 1.0
            to_k[ONE_LANE, base + s] = 1.0
            to_k[s * N_HEADS + h, base + 3 + s] = -1.0
    return jnp.asarray(to_q, BF16), jnp.asarray(to_k, BF16)


def _head_sums():
    total = np.zeros((N_HEADS * HEAD_DIM, LANE), np.float32)
    first = np.zeros_like(total)
    for h in range(N_HEADS):
        total[h * HEAD_DIM:(h + 1) * HEAD_DIM, h] = 1.0
        first[h * HEAD_DIM, h] = 1.0
    return jnp.asarray(total, BF16), jnp.asarray(first, BF16)


SLOT = 2 * HEAD_DIM
N_SPLIT = 3
FOX_FWD_HEADS = 8
FOX_BWD_HEADS = 8


def _slot(ref, h):
    return ref[:, h * SLOT:(h + 1) * SLOT]


def _fox_pack_fwd(zm, f_cols, *, tm=512):
    def body(q_ref, k_ref, v_ref, f_ref, tq_ref, tk_ref, qs_ref, ks_ref, vs_ref):
        ones = jnp.ones((tm, HEAD_DIM), BF16)
        terms = _pack_terms(f_ref[...], True)
        q_aux = jnp.dot(terms, tq_ref[...], preferred_element_type=F32).astype(BF16)
        k_aux = jnp.dot(terms, tk_ref[...], preferred_element_type=F32).astype(BF16)
        for h in range(N_HEADS):
            aux = slice(h * SLOT + HEAD_DIM, (h + 1) * SLOT)
            qs_ref[:, h * SLOT:(h + 1) * SLOT] = jnp.concatenate(
                [(_head(q_ref, h).astype(F32) * SCALE).astype(BF16), q_aux[:, aux]], axis=1)
            ks_ref[:, h * SLOT:(h + 1) * SLOT] = jnp.concatenate([_head(k_ref, h), k_aux[:, aux]], axis=1)
            vs_ref[:, h * SLOT:(h + 1) * SLOT] = jnp.concatenate([_head(v_ref, h), ones], axis=1)

    col = lambda b: pl.BlockSpec((tm, ATT_W), lambda i: (i, b))
    wide = pl.BlockSpec((tm, N_HEADS * SLOT), lambda i: (i, 0))
    const = pl.BlockSpec((LANE, N_HEADS * SLOT), lambda i: (0, 0))
    return pl.pallas_call(
        body, name="fox_pack_fwd", grid=(SEQ // tm,),
        in_specs=[col(0), col(1), col(2), pl.BlockSpec((tm, LANE), lambda i: (i, 0)), const, const],
        out_specs=[wide] * 3, out_shape=[_sds((SEQ, N_HEADS * SLOT), BF16)] * 3,
        compiler_params=_params("parallel"),
    )(zm, zm, zm, f_cols, *_aux_matrices())


def _fox_pack_bwd(zm, f_cols, lse, o, do, *, tm=512, after=None):
    def body(q_ref, f_ref, lse_ref, o_ref, do_ref, tq_ref, total_ref, first_ref, *rest):
        qs_ref, ds_ref = rest[-2:]
        delta = _split_dot(o_ref[...].astype(F32) * do_ref[...].astype(F32), total_ref[...])
        lse_h = _split_dot(lse_ref[...], first_ref[...])
        q_aux = jnp.dot(_pack_terms(f_ref[...] - lse_h, True), tq_ref[...], preferred_element_type=F32).astype(BF16)
        d_aux = jnp.dot(_pack_terms(-delta, False), tq_ref[...], preferred_element_type=F32).astype(BF16)
        for h in range(N_HEADS):
            aux = slice(h * SLOT + HEAD_DIM, (h + 1) * SLOT)
            qs_ref[:, h * SLOT:(h + 1) * SLOT] = jnp.concatenate(
                [(_head(q_ref, h).astype(F32) * SCALE).astype(BF16), q_aux[:, aux]], axis=1)
            ds_ref[:, h * SLOT:(h + 1) * SLOT] = jnp.concatenate([_head(do_ref, h), d_aux[:, aux]], axis=1)

    row = pl.BlockSpec((tm, ATT_W), lambda i: (i, 0))
    wide = pl.BlockSpec((tm, N_HEADS * SLOT), lambda i: (i, 0))
    const = lambda r, c: pl.BlockSpec((r, c), lambda i: (0, 0))
    return pl.pallas_call(
        body, name="fox_pack_bwd", grid=(SEQ // tm,),
        in_specs=[row, pl.BlockSpec((tm, LANE), lambda i: (i, 0)), row, row, row,
                  const(LANE, N_HEADS * SLOT), const(ATT_W, LANE), const(ATT_W, LANE)] + [ANY] * len(_also(after)),
        out_specs=[wide] * 2, out_shape=[_sds((SEQ, N_HEADS * SLOT), BF16)] * 2,
        compiler_params=_params("parallel"),
    )(zm, f_cols, lse, o, do, _aux_matrices()[0], *_head_sums(), *_also(after))


def _causal_pairs(key_major):
    nb = SEQ // FOX_T
    if key_major:
        pairs = [(i, j) for j in range(nb) for i in range(j, nb)]
    else:
        pairs = [(i, j) for i in range(nb) for j in range(i + 1)]
    return (jnp.array([p[0] for p in pairs], jnp.int32), jnp.array([p[1] for p in pairs], jnp.int32), len(pairs))


FOX_HALF = FOX_T // 2
FOX_FULL = ((slice(0, FOX_T), slice(0, FOX_T), None),)
FOX_DIAG = ((slice(0, FOX_HALF), slice(0, FOX_HALF), 0), (slice(FOX_HALF, FOX_T), slice(0, FOX_T), FOX_HALF))


def _causal_piece_mask(q_rows, k_rows, offset):
    shape = (q_rows.stop - q_rows.start, k_rows.stop - k_rows.start)
    row = lax.broadcasted_iota(jnp.int32, shape, 0)
    col = lax.broadcasted_iota(jnp.int32, shape, 1)
    return col <= row + offset


def _fox_fwd(q_slots, k_slots, v_slots):
    i_tab, j_tab, n_pairs = _causal_pairs(False)

    def body(i_tab, j_tab, q_ref, k_ref, v_ref, o_ref, lse_ref, m_s, acc_s):
        t = pl.program_id(1)
        i, j = i_tab[t], j_tab[t]

        @pl.when(j == 0)
        def _():
            m_s[...] = jnp.full_like(m_s, NEG_INF)
            acc_s[...] = jnp.zeros_like(acc_s)

        def step(pieces):
            jobs = [(h, piece) for h in range(FOX_FWD_HEADS) for piece in pieces]
            lanes = lambda h: slice(h * SLOT, (h + 1) * SLOT)
            scores = [lax.dot_general(q_ref[qr, lanes(h)], k_ref[kr, lanes(h)], NT_DIMS, preferred_element_type=F32)
                      for h, (qr, kr, _) in jobs]
            probs, alphas = [], []
            for idx, (h, (qr, kr, offset)) in enumerate(jobs):
                s = scores[idx]
                if offset is not None:
                    s = jnp.where(_causal_piece_mask(qr, kr, offset), s, NEG_INF)
                m_prev = m_s[h, qr, :]
                m_new = jnp.maximum(m_prev, jnp.max(s, axis=-1, keepdims=True))
                probs.append(jnp.exp(s - jnp.tile(m_new, (1, s.shape[1] // LANE))).astype(BF16))
                alphas.append(jnp.exp(m_prev - m_new))
                m_s[h, qr, :] = m_new
            for idx, (h, (qr, kr, _)) in enumerate(jobs):
                acc_s[h, qr, :] = alphas[idx] * acc_s[h, qr, :] + jnp.dot(
                    probs[idx], v_ref[kr, lanes(h)], preferred_element_type=F32)

        @pl.when(j < i)
        def _():
            step(FOX_FULL)

        @pl.when(j == i)
        def _():
            step(FOX_DIAG)
            outs, lses = [], []
            for h in range(FOX_FWD_HEADS):
                acc = acc_s[h]
                l = acc[:, HEAD_DIM:]
                outs.append(acc[:, :HEAD_DIM] / l)
                lses.append(m_s[h][:, :HEAD_DIM] + jnp.log(l))
            o_ref[...] = jnp.concatenate(outs, axis=1).astype(o_ref.dtype)
            lse_ref[...] = jnp.concatenate(lses, axis=1)

    qspec = pl.BlockSpec((FOX_T, FOX_FWD_HEADS * SLOT), lambda p, t, it, jt: (it[t], p))
    kspec = pl.BlockSpec((FOX_T, FOX_FWD_HEADS * SLOT), lambda p, t, it, jt: (jt[t], p))
    ospec = pl.BlockSpec((FOX_T, FOX_FWD_HEADS * HEAD_DIM), lambda p, t, it, jt: (it[t], p))
    return pl.pallas_call(
        body, name="fox_fwd",
        grid_spec=pltpu.PrefetchScalarGridSpec(
            num_scalar_prefetch=2, grid=(N_HEADS // FOX_FWD_HEADS, n_pairs),
            in_specs=[qspec, kspec, kspec], out_specs=[ospec, ospec],
            scratch_shapes=[pltpu.VMEM((FOX_FWD_HEADS, FOX_T, LANE), F32),
                            pltpu.VMEM((FOX_FWD_HEADS, FOX_T, SLOT), F32)]),
        out_shape=[_sds((SEQ, ATT_W), BF16), _sds((SEQ, ATT_W), F32)],
        compiler_params=_params("parallel", "arbitrary"),
    )(i_tab, j_tab, q_slots, k_slots, v_slots)


def _fox_bwd(q_slots, k_slots, v_slots, do_slots):
    i_tab, j_tab, n_pairs = _causal_pairs(True)

    def body(i_tab, j_tab, q_ref, k_ref, v_ref, do_ref, dq_ref, dk_ref, dv_ref):
        t = pl.program_id(1)
        i, j = i_tab[t], j_tab[t]

        @pl.when(t == 0)
        def _():
            dq_ref[...] = jnp.zeros_like(dq_ref)

        @pl.when(i == j)
        def _():
            dk_ref[...] = jnp.zeros_like(dk_ref)
            dv_ref[...] = jnp.zeros_like(dv_ref)

        def step(pieces):
            jobs = [(h, piece) for h in range(FOX_BWD_HEADS) for piece in pieces]
            lanes = lambda h: slice(h * SLOT, (h + 1) * SLOT)
            scores = [lax.dot_general(q_ref[qr, lanes(h)], k_ref[kr, lanes(h)], NT_DIMS, preferred_element_type=F32)
                      for h, (qr, kr, _) in jobs]
            dps = [lax.dot_general(do_ref[qr, lanes(h)], v_ref[kr, lanes(h)], NT_DIMS, preferred_element_type=F32)
                   for h, (qr, kr, _) in jobs]
            ps, dss = [], []
            for idx, (h, (qr, kr, offset)) in enumerate(jobs):
                p = jnp.exp(scores[idx])
                if offset is not None:
                    p = jnp.where(_causal_piece_mask(qr, kr, offset), p, 0.0)
                ps.append(p.astype(BF16))
                dss.append((p * dps[idx]).astype(BF16))
            for idx, (h, (qr, kr, _)) in enumerate(jobs):
                rows = pl.ds(pl.multiple_of(i * FOX_T + qr.start, FOX_HALF), qr.stop - qr.start)
                dv_ref[kr, lanes(h)] += lax.dot_general(ps[idx], do_ref[qr, lanes(h)], TN_DIMS,
                                                        preferred_element_type=F32)
                dk_ref[kr, lanes(h)] += lax.dot_general(dss[idx], q_ref[qr, lanes(h)], TN_DIMS,
                                                        preferred_element_type=F32)
                dq_ref[rows, lanes(h)] += jnp.dot(dss[idx], k_ref[kr, lanes(h)], preferred_element_type=F32)

        @pl.when(i > j)
        def _():
            step(FOX_FULL)

        @pl.when(i == j)
        def _():
            step(FOX_DIAG)

    qspec = pl.BlockSpec((FOX_T, FOX_BWD_HEADS * SLOT), lambda p, t, it, jt: (it[t], p))
    kspec = pl.BlockSpec((FOX_T, FOX_BWD_HEADS * SLOT), lambda p, t, it, jt: (jt[t], p))
    return pl.pallas_call(
        body, name="fox_bwd",
        grid_spec=pltpu.PrefetchScalarGridSpec(
            num_scalar_prefetch=2, grid=(N_HEADS // FOX_BWD_HEADS, n_pairs),
            in_specs=[qspec, kspec, kspec, qspec],
            out_specs=[pl.BlockSpec((SEQ, FOX_BWD_HEADS * SLOT), lambda p, t, it, jt: (0, p)), kspec, kspec]),
        out_shape=[_sds((SEQ, N_HEADS * SLOT), F32)] * 3,
        compiler_params=_params("arbitrary", "arbitrary"),
    )(i_tab, j_tab, q_slots, k_slots, v_slots, do_slots)


def _fox_unpack(dq_slots, dk_slots, dv_slots, dz, *, tm=512):
    def body(dq_ref, dk_ref, dv_ref, dz_in, o_ref, df_ref):
        lane = lax.broadcasted_iota(jnp.int32, (tm, LANE), 1)
        df = jnp.zeros((tm, LANE), F32)
        for h in range(N_HEADS):
            lo = h * SLOT
            for part, (ref, mult) in enumerate(((dq_ref, SCALE), (dk_ref, 1.0), (dv_ref, 1.0))):
                o_ref[:, part * ATT_W + h * HEAD_DIM:part * ATT_W + (h + 1) * HEAD_DIM] = (
                    ref[:, lo:lo + HEAD_DIM] * mult).astype(o_ref.dtype)
            rows = dq_ref[:, lo + HEAD_DIM:lo + HEAD_DIM + 1]
            cols = dk_ref[:, lo + HEAD_DIM + N_SPLIT:lo + HEAD_DIM + N_SPLIT + 1]
            df = jnp.where(lane == h, rows - cols, df)
        df_ref[...] = df

    wide = pl.BlockSpec((tm, N_HEADS * SLOT), lambda i: (i, 0))
    return pl.pallas_call(
        body, name="fox_unpack", grid=(SEQ // tm,), in_specs=[wide] * 3 + [ANY],
        out_specs=[pl.BlockSpec((tm, 3 * ATT_W), lambda i: (i, 0)), pl.BlockSpec((tm, LANE), lambda i: (i, 0))],
        out_shape=[_sds((SEQ, Z_MAIN), BF16), _sds((SEQ, LANE), F32)],
        input_output_aliases={3: 0},
        compiler_params=_params("parallel"),
    )(dq_slots, dk_slots, dv_slots, dz)


def _dil_bwd_prep(o, do, lse, *, tm=512):
    dilations = [d for _, d in DIL_PATTERNS]
    o_chunks = ATT_W // LANE

    def body(o_ref, do_ref, lse_ref, *rest):
        outs, (do_scr, lse_scr, dl_scr) = rest[:-3], rest[-3:]
        dov = do_ref[...].astype(F32)
        prod = o_ref[...].astype(F32) * dov
        lane = lax.broadcasted_iota(jnp.int32, (tm, LANE), 1)
        delta = jnp.zeros((tm, LANE), F32)
        for h in range(N_HEADS):
            delta = jnp.where(lane == h, jnp.sum(_head(prod, h), axis=1, keepdims=True), delta)
        for ch in range(o_chunks):
            do_scr[ch] = dov[:, ch * LANE:(ch + 1) * LANE]
        lse_scr[0] = lse_ref[...]
        dl_scr[0] = delta
        for k, d in enumerate(dilations):
            for scr, out in zip((do_scr, lse_scr, dl_scr), outs[3 * k:3 * k + 3]):
                _slabs_from_rows(scr, out, d)

    row = pl.BlockSpec((tm, ATT_W), lambda i: (i, 0))
    view = lambda d, w: pl.BlockSpec((tm // d, d * w), lambda i: (i, 0))
    outs = pl.pallas_call(
        body, name="dil_bwd_prep", grid=(SEQ // tm,),
        in_specs=[row, row, pl.BlockSpec((tm, LANE), lambda i: (i, 0))],
        out_specs=[view(d, w) for d in dilations for w in (ATT_W, LANE, LANE)],
        out_shape=[_sds((SEQ // d, d * w), t) for d in dilations for w, t in ((ATT_W, BF16), (LANE, F32), (LANE, F32))],
        scratch_shapes=[pltpu.VMEM((o_chunks, tm, LANE), F32), pltpu.VMEM((1, tm, LANE), F32),
                        pltpu.VMEM((1, tm, LANE), F32)],
        compiler_params=_params("parallel"),
    )(o, do, lse)
    return [outs[3 * k:3 * k + 3] for k in range(len(dilations))]


def _rope_tables():
    half = ROPE_DIM // 2
    inv_freq = np.float32(ROPE_THETA) ** (-np.arange(half, dtype=np.float32) * np.float32(2.0) / np.float32(ROPE_DIM))
    ang = np.arange(SEQ, dtype=np.float32)[:, None] * inv_freq.astype(np.float32)[None, :]
    cos, sin = jnp.asarray(np.cos(ang).astype(np.float32)), jnp.asarray(np.sin(ang).astype(np.float32))
    ones = jnp.ones((SEQ, HEAD_DIM - ROPE_DIM), F32)
    zeros = jnp.zeros((SEQ, HEAD_DIM - ROPE_DIM), F32)
    zh = jnp.zeros((SEQ, half), F32)
    c_tab = jnp.concatenate([cos, cos, ones], axis=1)
    a_tab = jnp.concatenate([-sin, zh, zeros], axis=1)
    b_tab = jnp.concatenate([zh, sin, zeros], axis=1)
    two = lambda t: jnp.concatenate([t, t], axis=1)
    return two(c_tab), two(a_tab), two(b_tab)


def _rotate(x, c_tab, a_tab, b_tab):
    return x * c_tab + pltpu.roll(x, LANE - ROPE_DIM // 2, 1) * a_tab + pltpu.roll(x, ROPE_DIM // 2, 1) * b_tab


def _rope_fwd(zm, tabs, *, tm=512):
    width = 3 * ATT_W
    dilations = [d for _, d in DIL_PATTERNS]

    def body(q_ref, k_ref, v_ref, c_ref, a_ref, b_ref, *rest):
        outs, scr = rest[:-1], rest[-1]
        per_part = ATT_W // LANE
        for part, (x_ref, mult) in enumerate(((q_ref, SCALE), (k_ref, 1.0))):
            for cc in range(per_part):
                sl = slice(cc * LANE, (cc + 1) * LANE)
                scr[part * per_part + cc] = _rotate(x_ref[:, sl].astype(F32), c_ref[...], a_ref[...], b_ref[...]) * mult
        for cc in range(per_part):
            scr[2 * per_part + cc] = v_ref[:, cc * LANE:(cc + 1) * LANE].astype(F32)
        for o_ref, d in zip(outs, dilations):
            for r in range(d):
                for ch in range(width // LANE):
                    o_ref[:, r * width + ch * LANE:r * width + (ch + 1) * LANE] = (
                        scr.at[ch][pl.ds(r, tm // d, stride=d), :].astype(o_ref.dtype))

    tab = pl.BlockSpec((tm, LANE), lambda i: (i, 0))
    col = lambda b: pl.BlockSpec((tm, ATT_W), lambda i: (i, b))
    return pl.pallas_call(
        body, name="rope_fwd", grid=(SEQ // tm,),
        in_specs=[col(3), col(4), col(5), tab, tab, tab],
        out_specs=[pl.BlockSpec((tm // d, d * width), lambda i: (i, 0)) for d in dilations],
        out_shape=[_sds((SEQ // d, d * width), BF16) for d in dilations],
        scratch_shapes=[pltpu.VMEM((width // LANE, tm, LANE), F32)],
        compiler_params=_params("parallel"),
    )(zm, zm, zm, *tabs)


def _dil_grad_combine(dqs, dks, dvs, tabs, dz, *, tm=256):
    dilations = [d for _, d in DIL_PATTERNS]
    chunks = ATT_W // LANE

    def body(*refs):
        groups = (refs[0:3], refs[3:6], refs[6:9])
        c_ref, a_ref, b_ref, _, o_ref, scr = refs[9:]

        def total(part, cc):
            acc = None
            for g, (ref, d) in enumerate(zip(groups[part], dilations)):
                term = ref[:, cc * LANE:(cc + 1) * LANE].astype(F32) if d == 1 else scr[part, g, cc]
                acc = term if acc is None else acc + term
            return acc

        for part in range(3):
            for g, (ref, d) in enumerate(zip(groups[part], dilations)):
                if d > 1:
                    _rows_from_slabs(ref, scr.at[part, g], d)
        for cc in range(chunks):
            for part in range(2):
                o_ref[:, part * ATT_W + cc * LANE:part * ATT_W + (cc + 1) * LANE] = _rotate(
                    total(part, cc), c_ref[...], -a_ref[...], -b_ref[...]).astype(o_ref.dtype)
            o_ref[:, 2 * ATT_W + cc * LANE:2 * ATT_W + (cc + 1) * LANE] = total(2, cc).astype(o_ref.dtype)

    view = lambda d: pl.BlockSpec((tm // d, d * ATT_W), lambda i: (i, 0))
    tab = pl.BlockSpec((tm, LANE), lambda i: (i, 0))
    return pl.pallas_call(
        body, name="dil_grad_combine", grid=(SEQ // tm,),
        in_specs=[view(d) for d in dilations] * 3 + [tab] * 3 + [ANY],
        out_specs=pl.BlockSpec((tm, 3 * ATT_W), lambda i: (i, 1)),
        out_shape=_sds((SEQ, Z_MAIN), BF16),
        input_output_aliases={12: 0},
        scratch_shapes=[pltpu.VMEM((3, len(dilations), chunks, tm, LANE), F32)],
        compiler_params=_params("parallel"),
    )(*dqs, *dks, *dvs, *tabs, dz)


def _dil_valid(n):
    qi = lax.broadcasted_iota(jnp.int32, (DIL_BLK, 2 * DIL_BLK), 0)
    ki = lax.broadcasted_iota(jnp.int32, (DIL_BLK, 2 * DIL_BLK), 1)
    dist = qi + DIL_BLK - ki
    return (dist >= 0) & (dist <= DIL_BLK) & ((n > 0) | (ki >= DIL_BLK))


def _dil_fwd(qkv_v, d):
    length = SEQ // d
    nb = length // DIL_BLK
    nsub = min(DIL_STEP_BLOCKS, nb)

    def body(q_ref, kp_ref, kc_ref, vp_ref, vc_ref, o_ref, lse_ref):
        m_step = pl.program_id(1)
        lane = lax.broadcasted_iota(jnp.int32, (DIL_BLK, LANE), 1)
        jobs = [(sub, h) for sub in range(nsub) for h in range(N_HEADS)]
        rows = lambda sub: slice(sub * DIL_BLK, (sub + 1) * DIL_BLK)
        cols = lambda h: slice(h * HEAD_DIM, (h + 1) * HEAD_DIM)

        def keys(prev_ref, cur_ref, sub, h):
            before = prev_ref[:, cols(h)] if sub == 0 else cur_ref[rows(sub - 1), cols(h)]
            return jnp.concatenate([before, cur_ref[rows(sub), cols(h)]], axis=0)

        scores = [lax.dot_general(q_ref[rows(sub), cols(h)], keys(kp_ref, kc_ref, sub, h), NT_DIMS,
                                  preferred_element_type=F32) for sub, h in jobs]
        ok = [_dil_valid(m_step)] + [_dil_valid(1)] * (nsub - 1)
        probs, inv_l, lse_all = [], [], [jnp.zeros((DIL_BLK, LANE), F32)] * nsub
        for idx, (sub, h) in enumerate(jobs):
            s = jnp.where(ok[sub], scores[idx], NEG_INF)
            m = jnp.max(s, axis=-1, keepdims=True)
            p = jnp.exp(s - m)
            l = jnp.sum(p, axis=-1, keepdims=True)
            probs.append(p.astype(BF16))
            inv_l.append(1.0 / l)
            lse_all[sub] = jnp.where(lane == h, m + jnp.log(l), lse_all[sub])
        outs = [jnp.dot(probs[idx], keys(vp_ref, vc_ref, sub, h), preferred_element_type=F32) * inv_l[idx]
                for idx, (sub, h) in enumerate(jobs)]
        for sub in range(nsub):
            o_ref[rows(sub), :] = jnp.concatenate(outs[sub * N_HEADS:(sub + 1) * N_HEADS], axis=1).astype(o_ref.dtype)
            lse_ref[rows(sub), :] = lse_all[sub]

    pair = lambda f: pl.BlockSpec((nsub * DIL_BLK, ATT_W), f)
    one = lambda f: pl.BlockSpec((DIL_BLK, ATT_W), f)
    before = lambda m: jnp.maximum(nsub * m - 1, 0)
    o, lse = pl.pallas_call(
        body, name=f"dil_fwd_d{d}", grid=(d, nb // nsub),
        in_specs=[pair(lambda r, m: (m, 3 * r)),
                  one(lambda r, m: (before(m), 3 * r + 1)), pair(lambda r, m: (m, 3 * r + 1)),
                  one(lambda r, m: (before(m), 3 * r + 2)), pair(lambda r, m: (m, 3 * r + 2))],
        out_specs=[pair(lambda r, m: (m, r)), pl.BlockSpec((nsub * DIL_BLK, LANE), lambda r, m: (m, r))],
        out_shape=[_sds((length, d * ATT_W), BF16), _sds((length, d * LANE), F32)],
        compiler_params=_params("parallel", "arbitrary"),
    )(qkv_v, qkv_v, qkv_v, qkv_v, qkv_v)
    return o, lse


def _rows_from_slabs(view_ref, scr, d):
    chunks, rows = scr.shape[0], scr.shape[1]
    for r in range(d):
        for ch in range(chunks):
            lo = (r * chunks + ch) * LANE
            scr.at[ch][pl.ds(r, rows // d, stride=d), :] = view_ref[:, lo:lo + LANE].astype(F32)


def _slabs_from_rows(scr, view_ref, d):
    chunks, rows = scr.shape[0], scr.shape[1]
    for r in range(d):
        for ch in range(chunks):
            lo = (r * chunks + ch) * LANE
            view_ref[:, lo:lo + LANE] = scr.at[ch][pl.ds(r, rows // d, stride=d), :].astype(view_ref.dtype)


def _dil_merge(os_, lses, *, tm=512):
    dilations = [d for _, d in DIL_PATTERNS]
    o_chunks = ATT_W // LANE

    def body(o0, o1, o2, l0, l1, l2, y_ref, lse_ref, o_scr, l_scr):
        os_nat, ls = [], []
        for g, (o_ref, l_ref, d) in enumerate(zip((o0, o1, o2), (l0, l1, l2), dilations)):
            if d == 1:
                os_nat.append(o_ref[...].astype(F32))
                ls.append(l_ref[...])
            else:
                _rows_from_slabs(o_ref, o_scr.at[g], d)
                _rows_from_slabs(l_ref, l_scr.at[g], d)
                os_nat.append(jnp.concatenate([o_scr[g, ch] for ch in range(o_chunks)], axis=1))
                ls.append(l_scr[g, 0])
        m = jnp.maximum(jnp.maximum(ls[0], ls[1]), ls[2])
        es = [jnp.exp(l - m) for l in ls]
        tot = es[0] + es[1] + es[2]
        lse_ref[...] = m + jnp.log(tot)
        alphas = [e / tot for e in es]
        outs = []
        for h in range(N_HEADS):
            acc = None
            for g in range(3):
                term = alphas[g][:, h:h + 1] * _head(os_nat[g], h)
                acc = term if acc is None else acc + term
            outs.append(acc)
        y_ref[...] = jnp.concatenate(outs, axis=1).astype(y_ref.dtype)

    row = pl.BlockSpec((tm, ATT_W), lambda i: (i, 0))
    vec = pl.BlockSpec((tm, LANE), lambda i: (i, 0))
    view = lambda d, w: pl.BlockSpec((tm // d, d * w), lambda i: (i, 0))
    return pl.pallas_call(
        body, name="dil_merge", grid=(SEQ // tm,),
        in_specs=[view(d, ATT_W) for d in dilations] + [view(d, LANE) for d in dilations], out_specs=[row, vec],
        out_shape=[_sds((SEQ, ATT_W), BF16), _sds((SEQ, LANE), F32)],
        scratch_shapes=[pltpu.VMEM((3, o_chunks, tm, LANE), F32), pltpu.VMEM((3, 1, tm, LANE), F32)],
        compiler_params=_params("parallel"),
    )(*os_, *lses)


def _dil_bwd(qkv_v, do_v, lse_v, dl_v, d):
    length = SEQ // d
    nb = length // DIL_BLK
    nsub = min(DIL_STEP_BLOCKS, nb)
    n_steps = nb // nsub

    def body(q_ref, kp_ref, kc_ref, vp_ref, vc_ref, lse_ref, dl_ref, do_ref, dq_ref, dk_ref, dv_ref, dk_s, dv_s):
        m_step = pl.program_id(1)

        @pl.when(m_step == 0)
        def _():
            dk_s[...] = jnp.zeros_like(dk_s)
            dv_s[...] = jnp.zeros_like(dv_s)

        jobs = [(sub, h) for sub in range(nsub) for h in range(N_HEADS)]
        rows = lambda sub: slice(sub * DIL_BLK, (sub + 1) * DIL_BLK)
        cols = lambda h: slice(h * HEAD_DIM, (h + 1) * HEAD_DIM)

        def keys(prev_ref, cur_ref, sub, h):
            before = prev_ref[:, cols(h)] if sub == 0 else cur_ref[rows(sub - 1), cols(h)]
            return jnp.concatenate([before, cur_ref[rows(sub), cols(h)]], axis=0)

        kks = [keys(kp_ref, kc_ref, sub, h) for sub, h in jobs]
        scores = [lax.dot_general(q_ref[rows(sub), cols(h)], kks[idx], NT_DIMS, preferred_element_type=F32)
                  for idx, (sub, h) in enumerate(jobs)]
        dps = [lax.dot_general(do_ref[rows(sub), cols(h)], keys(vp_ref, vc_ref, sub, h), NT_DIMS,
                               preferred_element_type=F32) for sub, h in jobs]
        ok = [_dil_valid(m_step)] + [_dil_valid(1)] * (nsub - 1)
        ps, dss = [], []
        for idx, (sub, h) in enumerate(jobs):
            p = jnp.where(ok[sub], jnp.exp(scores[idx] - lse_ref[rows(sub), h:h + 1]), 0.0)
            ps.append(p.astype(BF16))
            dss.append((p * (dps[idx] - dl_ref[rows(sub), h:h + 1])).astype(BF16))
        dqs = [jnp.dot(dss[idx], kks[idx], preferred_element_type=F32) * SCALE for idx in range(len(jobs))]
        dkks = [lax.dot_general(dss[idx], q_ref[rows(sub), cols(h)], TN_DIMS, preferred_element_type=F32)
                for idx, (sub, h) in enumerate(jobs)]
        dvvs = [lax.dot_general(ps[idx], do_ref[rows(sub), cols(h)], TN_DIMS, preferred_element_type=F32)
                for idx, (sub, h) in enumerate(jobs)]
        for sub in range(nsub):
            dq_ref[rows(sub), :] = jnp.concatenate(dqs[sub * N_HEADS:(sub + 1) * N_HEADS], axis=1).astype(dq_ref.dtype)
        base = m_step * (nsub * DIL_BLK)
        blocks = [pl.ds(pl.multiple_of(jnp.maximum(base - DIL_BLK, 0), DIL_BLK), DIL_BLK)]
        blocks += [pl.ds(pl.multiple_of(base + s * DIL_BLK, DIL_BLK), DIL_BLK) for s in range(nsub)]
        for acc, parts in ((dk_s, dkks), (dv_s, dvvs)):
            top = lambda sub: jnp.concatenate([parts[sub * N_HEADS + h][:DIL_BLK] for h in range(N_HEADS)], axis=1)
            bottom = lambda sub: jnp.concatenate([parts[sub * N_HEADS + h][DIL_BLK:] for h in range(N_HEADS)], axis=1)
            acc[blocks[0], :] += top(0)
            for s in range(nsub):
                acc[blocks[s + 1], :] += bottom(s) + top(s + 1) if s + 1 < nsub else bottom(s)

        @pl.when(m_step == n_steps - 1)
        def _():
            dk_ref[...] = dk_s[...].astype(dk_ref.dtype)
            dv_ref[...] = dv_s[...].astype(dv_ref.dtype)

    pair = lambda f: pl.BlockSpec((nsub * DIL_BLK, ATT_W), f)
    one = lambda f: pl.BlockSpec((DIL_BLK, ATT_W), f)
    vec = lambda f: pl.BlockSpec((nsub * DIL_BLK, LANE), f)
    whole = pl.BlockSpec((length, ATT_W), lambda r, m: (0, r))
    before = lambda m: jnp.maximum(nsub * m - 1, 0)
    outs = pl.pallas_call(
        body, name=f"dil_bwd_d{d}", grid=(d, n_steps),
        in_specs=[pair(lambda r, m: (m, 3 * r)),
                  one(lambda r, m: (before(m), 3 * r + 1)), pair(lambda r, m: (m, 3 * r + 1)),
                  one(lambda r, m: (before(m), 3 * r + 2)), pair(lambda r, m: (m, 3 * r + 2)),
                  vec(lambda r, m: (m, r)), vec(lambda r, m: (m, r)), pair(lambda r, m: (m, r))],
        out_specs=[pair(lambda r, m: (m, r)), whole, whole],
        out_shape=[_sds((length, d * ATT_W), BF16)] * 3,
        scratch_shapes=[pltpu.VMEM((length, ATT_W), F32), pltpu.VMEM((length, ATT_W), F32)],
        compiler_params=_params("arbitrary", "arbitrary"),
    )(qkv_v, qkv_v, qkv_v, qkv_v, qkv_v, lse_v, dl_v, do_v)
    return outs


def _sigmoid(x):
    return 1.0 / (1.0 + jnp.exp(-x))


def _mix_fwd(ya, yb, w_oa, w_ob, zm, *, tm=512):
    def body(ya_ref, yb_ref, wa_ref, wb_ref, ga_ref, gb_ref, pa_ref, pb_ref, mix_ref):
        pa = jnp.dot(ya_ref[...], wa_ref[...], preferred_element_type=F32)
        pb = jnp.dot(yb_ref[...], wb_ref[...], preferred_element_type=F32)
        pa_ref[...] = pa.astype(pa_ref.dtype)
        pb_ref[...] = pb.astype(pb_ref.dtype)
        mix_ref[...] = (_sigmoid(ga_ref[...].astype(F32)) * pa + _sigmoid(gb_ref[...].astype(F32)) * pb
                        ).astype(mix_ref.dtype)

    row = pl.BlockSpec((tm, ATT_W), lambda i: (i, 0))
    wsp = pl.BlockSpec((ATT_W, D_MODEL), lambda i: (0, 0))
    wide = pl.BlockSpec((tm, D_MODEL), lambda i: (i, 0))
    return pl.pallas_call(
        body, name="mix_fwd", grid=(SEQ // tm,),
        in_specs=[row, row, wsp, wsp, pl.BlockSpec((tm, D_MODEL), lambda i: (i, 3)),
                  pl.BlockSpec((tm, D_MODEL), lambda i: (i, 4))],
        out_specs=[wide] * 3, out_shape=[_sds((SEQ, D_MODEL), BF16)] * 3,
        compiler_params=_params("parallel"),
    )(ya, yb, w_oa, w_ob, zm, zm)


def _gate_bwd(dmix, zm, p, gate_block, dz, *, name, tm=512):
    def body(dm_ref, g_ref, p_ref, *rest):
        dp_ref, dz_ref = rest[-2], rest[-1]
        dm = dm_ref[...].astype(F32)
        s = _sigmoid(g_ref[...].astype(F32))
        dp_ref[...] = (dm * s).astype(dp_ref.dtype)
        dz_ref[...] = (dm * p_ref[...].astype(F32) * s * (1.0 - s)).astype(dz_ref.dtype)

    wide = pl.BlockSpec((tm, D_MODEL), lambda i: (i, 0))
    gate = pl.BlockSpec((tm, D_MODEL), lambda i: (i, gate_block))
    extra = [] if dz is None else [dz]
    return pl.pallas_call(
        body, name=name, grid=(SEQ // tm,),
        in_specs=[wide, gate, wide] + [ANY] * len(extra),
        out_specs=[wide, gate],
        out_shape=[_sds((SEQ, D_MODEL), BF16), _sds((SEQ, Z_MAIN), BF16)],
        input_output_aliases={3: 1} if extra else {},
        compiler_params=_params("parallel"),
    )(dmix, zm, p, *extra)


def _out_fwd(mixed, w_out, x, g_post, g_pre, *, tm=512):
    def body(m_ref, w_ref, x_ref, gp_ref, gn_ref, y_ref, x2_ref, h_ref):
        y = jnp.dot(m_ref[...], w_ref[...], preferred_element_type=F32)
        y_ref[...] = y
        r = lax.rsqrt(jnp.mean(y * y, axis=-1, keepdims=True) + RMS_EPS)
        x2 = x_ref[...] + y * r * gp_ref[...]
        x2_ref[...] = x2
        r2 = lax.rsqrt(jnp.mean(x2 * x2, axis=-1, keepdims=True) + RMS_EPS)
        h_ref[...] = (x2 * r2 * gn_ref[...]).astype(h_ref.dtype)

    row = pl.BlockSpec((tm, D_MODEL), lambda i: (i, 0))
    vec = pl.BlockSpec((1, D_MODEL), lambda i: (0, 0))
    return pl.pallas_call(
        body, name="out_fwd", grid=(SEQ // tm,),
        in_specs=[row, pl.BlockSpec((D_MODEL, D_MODEL), lambda i: (0, 0)), row, vec, vec],
        out_specs=[row] * 3,
        out_shape=[_sds((SEQ, D_MODEL), F32), _sds((SEQ, D_MODEL), F32), _sds((SEQ, D_MODEL), BF16)],
        compiler_params=_params("parallel"),
    )(mixed, w_out, x, g_post, g_pre)


FFN_HALF = 256
FFN_TN = 2 * FFN_HALF
FFN_NJ = D_FF // FFN_HALF
FFN_GROUP = 2 * SUBLANE
UP_TM = 1024


def _ffn_interleave(t):
    lead = t.shape[:-1]
    return jnp.swapaxes(t.reshape(*lead, 2, FFN_NJ, FFN_HALF), -3, -2).reshape(*lead, 2 * D_FF)


def _ffn_deinterleave(t):
    lead = t.shape[:-1]
    return jnp.swapaxes(t.reshape(*lead, FFN_NJ, 2, FFN_HALF), -3, -2).reshape(*lead, 2 * D_FF)


W_IN_SHARD = (Z_MAIN + N_HEADS) // N_DEV
FORGET_LO = 3 * ATT_W


def _w_in_from_shards(shards, *, tm=256):
    def columns(g_ref, lo, width):
        p, off = divmod(lo, W_IN_SHARD)
        if off + width <= W_IN_SHARD:
            return g_ref[p, :, off:off + width]
        first = W_IN_SHARD - off
        return jnp.concatenate([g_ref[p, :, off:], g_ref[p + 1, :, :width - first]], axis=1)

    def body(g_ref, main_ref, f_ref):
        for t in range(Z_MAIN // LANE):
            lo = t * LANE
            main_ref[:, lo:lo + LANE] = columns(g_ref, lo if lo < FORGET_LO else lo + N_HEADS, LANE)
        f_ref[...] = jnp.concatenate([columns(g_ref, FORGET_LO, N_HEADS),
                                      jnp.zeros((tm, F_PAD - N_HEADS), f_ref.dtype)], axis=1)

    return pl.pallas_call(
        body, name="w_in_from_shards", grid=(D_MODEL // tm,),
        in_specs=[pl.BlockSpec((N_DEV, tm, W_IN_SHARD), lambda i: (0, i, 0))],
        out_specs=[pl.BlockSpec((tm, Z_MAIN), lambda i: (i, 0)), pl.BlockSpec((tm, F_PAD), lambda i: (i, 0))],
        out_shape=[_sds((D_MODEL, Z_MAIN), shards.dtype), _sds((D_MODEL, F_PAD), shards.dtype)],
        compiler_params=_params("parallel"),
    )(shards)


def _w_in_to_shards(g_main, g_f, *, tm=256):
    def natural(main_ref, f_ref, lo, width):
        pieces, hi = [], lo + width
        for ref, start, stop, shift in ((main_ref, 0, FORGET_LO, 0), (f_ref, FORGET_LO, FORGET_LO + N_HEADS, FORGET_LO),
                                        (main_ref, FORGET_LO + N_HEADS, Z_MAIN + N_HEADS, N_HEADS)):
            a, b = max(lo, start), min(hi, stop)
            if a < b:
                pieces.append(ref[:, a - shift:b - shift])
        return pieces[0] if len(pieces) == 1 else jnp.concatenate(pieces, axis=1)

    def body(main_ref, f_ref, o_ref):
        for p in range(N_DEV):
            for q in range(-(-W_IN_SHARD // LANE)):
                width = min(LANE, W_IN_SHARD - q * LANE)
                o_ref[p, :, q * LANE:q * LANE + width] = natural(main_ref, f_ref, p * W_IN_SHARD + q * LANE, width)

    return pl.pallas_call(
        body, name="w_in_to_shards", grid=(D_MODEL // tm,),
        in_specs=[pl.BlockSpec((tm, Z_MAIN), lambda i: (i, 0)), pl.BlockSpec((tm, F_PAD), lambda i: (i, 0))],
        out_specs=pl.BlockSpec((N_DEV, tm, W_IN_SHARD), lambda i: (0, i, 0)),
        out_shape=_sds((N_DEV, D_MODEL, W_IN_SHARD), g_main.dtype),
        compiler_params=_params("parallel"),
    )(g_main, g_f)


W_UP_SHARD = 2 * D_FF // N_DEV


def _w_up_lane_tile(k):
    block = k // 2
    return (2 * (block % FFN_NJ) + block // FFN_NJ) * FFN_HALF + (k % 2) * LANE


def _w_up_from_shards(shards, *, tm=256):
    def body(g_ref, o_ref):
        for k in range(2 * D_FF // LANE):
            p, off = divmod(k * LANE, W_UP_SHARD)
            if off + LANE <= W_UP_SHARD:
                tile = g_ref[p, :, off:off + LANE]
            else:
                tile = jnp.concatenate([g_ref[p, :, off:], g_ref[p + 1, :, :off + LANE - W_UP_SHARD]], axis=1)
            dst = _w_up_lane_tile(k)
            o_ref[:, dst:dst + LANE] = tile

    return pl.pallas_call(
        body, name="w_up_from_shards", grid=(D_MODEL // tm,),
        in_specs=[pl.BlockSpec((N_DEV, tm, W_UP_SHARD), lambda i: (0, i, 0))],
        out_specs=pl.BlockSpec((tm, 2 * D_FF), lambda i: (i, 0)),
        out_shape=_sds((D_MODEL, 2 * D_FF), shards.dtype),
        compiler_params=_params("parallel"),
    )(shards)


def _w_up_to_shards(t, *, tm=256):
    def body(x_ref, o_ref):
        for p in range(N_DEV):
            for q in range(-(-W_UP_SHARD // LANE)):
                width = min(LANE, W_UP_SHARD - q * LANE)
                k, off = divmod(p * W_UP_SHARD + q * LANE, LANE)
                src = _w_up_lane_tile(k)
                if off == 0:
                    tile = x_ref[:, src:src + width]
                else:
                    tile = x_ref[:, src + off:src + LANE]
                    if width > LANE - off:
                        nxt = _w_up_lane_tile(k + 1)
                        tile = jnp.concatenate([tile, x_ref[:, nxt:nxt + width - (LANE - off)]], axis=1)
                o_ref[p, :, q * LANE:q * LANE + width] = tile

    return pl.pallas_call(
        body, name="w_up_to_shards", grid=(D_MODEL // tm,),
        in_specs=[pl.BlockSpec((tm, 2 * D_FF), lambda i: (i, 0))],
        out_specs=pl.BlockSpec((N_DEV, tm, W_UP_SHARD), lambda i: (0, i, 0)),
        out_shape=_sds((N_DEV, D_MODEL, W_UP_SHARD), t.dtype),
        compiler_params=_params("parallel"),
    )(t)


def _gelu_parts(a):
    c = math.sqrt(2.0 / math.pi)
    a2 = a * a
    t = jnp.tanh((c * a) * (1.0 + 0.044715 * a2))
    half_a, one_t = 0.5 * a, 1.0 + t
    gelu = half_a * one_t
    dgelu = 0.5 * one_t + half_a * (1.0 - t * t) * (c + (3.0 * 0.044715 * c) * a2)
    return gelu, dgelu


def _row_masks(down):
    row = lax.broadcasted_iota(jnp.int32, (SUBLANE, FFN_TN), 0)
    return (row < 1, row < 2) if down else (row >= SUBLANE - 1, row >= SUBLANE - 2)


def _rolled(x, down):
    return (pltpu.roll(x, 1, 0), pltpu.roll(x, 2, 0)) if down else (
        pltpu.roll(x, SUBLANE - 1, 0), pltpu.roll(x, SUBLANE - 2, 0))


def _shifted(cur_rolled, neighbour_rolled, masks):
    return (jnp.where(masks[0], neighbour_rolled[0], cur_rolled[0]),
            jnp.where(masks[1], neighbour_rolled[1], cur_rolled[1]))


def _conv_consts(w_ref, b_ref):
    shape = (SUBLANE, FFN_TN)
    return [jnp.broadcast_to(w_ref[k:k + 1, :], shape) for k in range(3)] + [jnp.broadcast_to(b_ref[...], shape)]


def _up_conv_fwd(h2, w_up, conv_w, conv_b):
    nrow = SEQ // UP_TM
    n_tiles = FFN_NJ * nrow
    n_groups = UP_TM // FFN_GROUP

    def body(h_ref, wu_ref, w_ref, b_ref, u_ref, ab_ref, m_ref, ua_s, ub_s, c1_s, c2_s):
        k = pl.program_id(0)

        @pl.when(k == 0)
        def _():
            ub_s[...] = jnp.zeros_like(ub_s)

        @pl.when(jnp.maximum(k - 1, 0) % nrow == 0)
        def _():
            c1_s[...] = jnp.zeros_like(c1_s)
            c2_s[...] = jnp.zeros_like(c2_s)

        def step(write_s, read_s):
            h_rows = pl.ds(pl.multiple_of((this(k) % nrow) * UP_TM, UP_TM), UP_TM)
            u = jnp.dot(h_ref[h_rows, :], wu_ref[...], preferred_element_type=F32)
            write_s[...] = u
            u_ref[...] = u.astype(u_ref.dtype)
            w0, w1, w2, bias = _conv_consts(w_ref, b_ref)
            masks = _row_masks(True)
            above = (c1_s[...], c2_s[...])
            for g in range(n_groups):
                rows = slice(g * FFN_GROUP, (g + 1) * FFN_GROUP)
                x = read_s[rows, :]
                convs = []
                for c in range(2):
                    cur = x[c * SUBLANE:(c + 1) * SUBLANE]
                    cur_rolled = _rolled(cur, True)
                    s1, s2 = _shifted(cur_rolled, above, masks)
                    convs.append(w0 * s2 + w1 * s1 + w2 * cur + bias)
                    above = cur_rolled
                y = jnp.concatenate(convs, axis=0)
                ab_ref[rows, :] = y.astype(ab_ref.dtype)
                m_ref[rows, :] = (_gelu_parts(y[:, :FFN_HALF])[0] * y[:, FFN_HALF:]).astype(m_ref.dtype)
            c1_s[...], c2_s[...] = above

        @pl.when(k % 2 == 0)
        def _():
            step(ua_s, ub_s)

        @pl.when(k % 2 == 1)
        def _():
            step(ub_s, ua_s)

    this = lambda k: jnp.minimum(k, n_tiles - 1)
    last = lambda k: jnp.maximum(k - 1, 0)
    blk = lambda tile: pl.BlockSpec((UP_TM, FFN_TN), lambda k: (tile(k) % nrow, tile(k) // nrow))
    return pl.pallas_call(
        body, name="up_conv_fwd", grid=(n_tiles + 1,),
        in_specs=[pl.BlockSpec((SEQ, D_MODEL), lambda k: (0, 0)),
                  pl.BlockSpec((D_MODEL, FFN_TN), lambda k: (0, this(k) // nrow)),
                  pl.BlockSpec((3, FFN_TN), lambda k: (0, last(k) // nrow)),
                  pl.BlockSpec((1, FFN_TN), lambda k: (0, last(k) // nrow))],
        out_specs=[blk(this), blk(last), pl.BlockSpec((UP_TM, FFN_HALF), lambda k: (last(k) % nrow, last(k) // nrow))],
        out_shape=[_sds((SEQ, 2 * D_FF), BF16), _sds((SEQ, 2 * D_FF), BF16), _sds((SEQ, D_FF), BF16)],
        scratch_shapes=[pltpu.VMEM((UP_TM, FFN_TN), F32), pltpu.VMEM((UP_TM, FFN_TN), F32),
                        pltpu.VMEM((SUBLANE, FFN_TN), F32), pltpu.VMEM((SUBLANE, FFN_TN), F32)],
        compiler_params=_params("arbitrary"),
    )(h2, w_up, conv_w, conv_b)


def _ffn_mid_bwd(dy2, w_down, u, ab, conv_w):
    nrow = SEQ // UP_TM
    n_tiles = FFN_NJ * nrow
    n_groups = UP_TM // FFN_GROUP
    this = lambda k: jnp.minimum(k, n_tiles - 1)
    last = lambda k: jnp.maximum(k - 1, 0)
    row_of = lambda tile: nrow - 1 - tile % nrow

    def body(dy_ref, wd_ref, u_ref, ab_ref, w_ref, du_ref, gw_ref, gb_ref, c_s, dma_s, dmb_s):
        k = pl.program_id(0)

        @pl.when(k == 0)
        def _():
            dmb_s[...] = jnp.zeros_like(dmb_s)

        @pl.when(last(k) % nrow == 0)
        def _():
            c_s[...] = jnp.zeros_like(c_s)
            gw_ref[...] = jnp.zeros_like(gw_ref)
            gb_ref[...] = jnp.zeros_like(gb_ref)

        def step(write_s, read_s):
            dy_rows = pl.ds(pl.multiple_of(row_of(this(k)) * UP_TM, UP_TM), UP_TM)
            write_s[...] = lax.dot_general(dy_ref[dy_rows, :], wd_ref[...], NT_DIMS,
                                           preferred_element_type=F32)
            taps = [jnp.broadcast_to(w_ref[t:t + 1, :], (SUBLANE, FFN_TN)) for t in range(3)]
            masks = _row_masks(False)
            below = _rolled(c_s[...], False)
            acc = [jnp.zeros((SUBLANE, FFN_TN), F32)] * 4
            for g in reversed(range(n_groups)):
                rows = slice(g * FFN_GROUP, (g + 1) * FFN_GROUP)
                x, y, dmv = u_ref[rows, :].astype(F32), ab_ref[rows, :].astype(F32), read_s[rows, :]
                gelu, dgelu = _gelu_parts(y[:, :FFN_HALF])
                d = jnp.concatenate([dmv * y[:, FFN_HALF:] * dgelu, dmv * gelu], axis=1)
                pre = [None, None]
                for c in (1, 0):
                    sl = slice(c * SUBLANE, (c + 1) * SUBLANE)
                    cur, xs = d[sl], x[sl]
                    cur_rolled = _rolled(cur, False)
                    up1, up2 = _shifted(cur_rolled, below, masks)
                    acc = [acc[0] + up2 * xs, acc[1] + up1 * xs, acc[2] + cur * xs, acc[3] + cur]
                    pre[c] = taps[2] * cur + taps[1] * up1 + taps[0] * up2
                    below = cur_rolled
                du_ref[rows, :] = jnp.concatenate(pre, axis=0).astype(du_ref.dtype)
            c_s[...] = pltpu.roll(below[0], 1, 0)
            for t in range(3):
                gw_ref[t:t + 1, :] += jnp.sum(acc[t], axis=0, keepdims=True)
            gb_ref[...] += jnp.sum(acc[3], axis=0, keepdims=True)

        @pl.when(k % 2 == 0)
        def _():
            step(dma_s, dmb_s)

        @pl.when(k % 2 == 1)
        def _():
            step(dmb_s, dma_s)

    blk = pl.BlockSpec((UP_TM, FFN_TN), lambda k: (row_of(last(k)), last(k) // nrow))
    col = lambda rows: pl.BlockSpec((rows, FFN_TN), lambda k: (0, last(k) // nrow))
    return pl.pallas_call(
        body, name="ffn_mid_bwd", grid=(n_tiles + 1,),
        in_specs=[pl.BlockSpec((SEQ, D_MODEL), lambda k: (0, 0)),
                  pl.BlockSpec((FFN_HALF, D_MODEL), lambda k: (this(k) // nrow, 0)), blk, blk, col(3)],
        out_specs=[blk, col(3), col(1)],
        out_shape=[_sds((SEQ, 2 * D_FF), BF16), _sds((3, 2 * D_FF), F32), _sds((1, 2 * D_FF), F32)],
        scratch_shapes=[pltpu.VMEM((SUBLANE, FFN_TN), F32), pltpu.VMEM((UP_TM, FFN_HALF), F32),
                        pltpu.VMEM((UP_TM, FFN_HALF), F32)],
        compiler_params=_params("arbitrary"),
    )(dy2, w_down, u, ab, conv_w)


def _down_fwd(m, w_down, x2, g_post, target, *, tm=512):
    def body(m_ref, w_ref, x2_ref, g_ref, t_ref, dout_ref, dy_ref, gg_ref, loss_ref):
        @pl.when(pl.program_id(0) == 0)
        def _():
            gg_ref[...] = jnp.zeros_like(gg_ref)
            loss_ref[...] = jnp.zeros_like(loss_ref)

        y = jnp.dot(m_ref[...], w_ref[...], preferred_element_type=F32)
        r = lax.rsqrt(jnp.mean(y * y, axis=-1, keepdims=True) + RMS_EPS)
        yn = y * r
        diff = (x2_ref[...] + yn * g_ref[...]) - t_ref[...]
        loss_ref[...] += jnp.sum(diff * diff)
        dout = diff * (1.0 / D_MODEL)
        dout_ref[...] = dout
        gg_ref[...] += jnp.sum(dout * yn, axis=0, keepdims=True)
        dn = dout * g_ref[...]
        dy_ref[...] = (r * (dn - yn * jnp.mean(dn * yn, axis=-1, keepdims=True))).astype(dy_ref.dtype)

    row = pl.BlockSpec((tm, D_MODEL), lambda i: (i, 0))
    vec = pl.BlockSpec((1, D_MODEL), lambda i: (0, 0))
    return pl.pallas_call(
        body, name="down_fwd", grid=(SEQ // tm,),
        in_specs=[pl.BlockSpec((tm, D_FF), lambda i: (i, 0)), pl.BlockSpec((D_FF, D_MODEL), lambda i: (0, 0)),
                  row, vec, row],
        out_specs=[row, row, vec, pl.BlockSpec((1, LANE), lambda i: (0, 0))],
        out_shape=[_sds((SEQ, D_MODEL), F32), _sds((SEQ, D_MODEL), BF16), _sds((1, D_MODEL), F32),
                   _sds((1, LANE), F32)],
        compiler_params=_params("arbitrary"),
    )(m, w_down, x2, g_post, target)


def _local_step(x, target, w_main, w_f, b_forget, conv_b, g_pre_mix, g_post_mix, g_pre_ffn, g_post_ffn,
                proj_weights, ffn_weights, ffn_grads_ready, proj_grads_ready, mixer_grads_ready, after=None):
    mm = _matmul
    tabs = _rope_tables()

    h1 = _rms_fwd(x, g_pre_mix, name="rms_pre_mix", after=after)
    zm = mm(h1, w_main, out_dtype=BF16, tm=2048, tn=1024, tk=1024, name="in_proj")
    zf = mm(h1, w_f, out_dtype=F32, tm=2048, tn=F_PAD, tk=1024, name="in_proj_forget")
    f_row, sg_row = _fox_prep(zf[:, :N_HEADS].T, b_forget.reshape(N_HEADS, 1))
    f_cols = jnp.pad(f_row.T, ((0, 0), (0, LANE - N_HEADS)))
    q_slots, k_slots, v_slots = _fox_pack_fwd(zm, f_cols)
    ya, lse_a = _fox_fwd(q_slots, k_slots, v_slots)
    qkv_d = dict(zip([d for _, d in DIL_PATTERNS], _rope_fwd(zm, tabs)))
    dil = [_dil_fwd(qkv_d[d], d) for _, d in DIL_PATTERNS]
    yb, lse_b = _dil_merge([o for o, _ in dil], [l for _, l in dil])
    w_oa, w_ob, w_out = proj_weights(yb)
    pa, pb, mixed = _mix_fwd(ya, yb, w_oa, w_ob, zm)
    y1, x2, h2 = _out_fwd(mixed, w_out, x, g_post_mix, g_pre_ffn)
    w_up, conv_w, w_down = ffn_weights(h2)
    u, ab, m = _up_conv_fwd(h2, w_up, conv_w, _ffn_interleave(conv_b))
    dout, dy2, gg_post_ffn, sq_err = _down_fwd(m, w_down, x2, g_post_ffn, target)

    g_w_down = mm(m, dy2, ta=True, out_dtype=BF16, tm=D_FF // 2, tn=1024, tk=SEQ, name="grad_w_down")
    du, g_conv_w, g_conv_b = _ffn_mid_bwd(dy2, w_down, u, ab, conv_w)
    g_w_up = mm(h2, du, ta=True, out_dtype=BF16, tm=1024, tn=D_FF // 2, tk=SEQ, name="grad_w_up")
    tok = ffn_grads_ready(dict(w_down=g_w_down, w_up_blocks=g_w_up, conv_w=_ffn_deinterleave(g_conv_w)))
    dh2 = mm(du, w_up, tb=True, out_dtype=BF16, tm=512, tn=1024, tk=2 * D_FF, name="d_h2")

    dx2, dy1, gg_pre_ffn, gg_post_mix = _rms_pair_bwd([dh2], x2, g_pre_ffn, dout, y1, g_post_mix, after=tok)
    g_w_out = mm(mixed, dy1, ta=True, out_dtype=BF16, tm=1024, tn=1024, tk=SEQ, name="grad_w_out")
    dmix = mm(dy1, w_out, tb=True, out_dtype=BF16, tm=2048, tn=1024, tk=1024, name="d_mixed")
    dpa, dz = _gate_bwd(dmix, zm, pa, 3, None, name="gate_bwd_fox")
    dpb, dz = _gate_bwd(dmix, zm, pb, 4, dz, name="gate_bwd_dil")
    g_w_oa = mm(ya, dpa, ta=True, out_dtype=BF16, tm=512, tn=1024, tk=SEQ, name="grad_w_o_fox", col_slots=N_DEV)
    g_w_ob = mm(yb, dpb, ta=True, out_dtype=BF16, tm=512, tn=1024, tk=SEQ, name="grad_w_o_dil", col_slots=N_DEV)
    tok = proj_grads_ready(dict(w_o_fox=g_w_oa, w_o_dil=g_w_ob, w_out=g_w_out))
    dya = mm(dpa, w_oa, tb=True, out_dtype=BF16, tm=2048, tn=512, tk=1024, name="d_y_fox")
    dyb = mm(dpb, w_ob, tb=True, out_dtype=BF16, tm=2048, tn=512, tk=1024, name="d_y_dil")

    qb_slots, do_slots = _fox_pack_bwd(zm, f_cols, lse_a, ya, dya, after=tok)
    dz, df_cols = _fox_unpack(*_fox_bwd(qb_slots, k_slots, v_slots, do_slots), dz)
    dfa_t, g_b_forget = _fox_post_bwd(df_cols[:, :N_HEADS].T, sg_row)

    rows_d = _dil_bwd_prep(yb, dyb, lse_b)
    dil_g = [_dil_bwd(qkv_d[d], *rows_d[k], d) for k, (_, d) in enumerate(DIL_PATTERNS)]
    dz = _dil_grad_combine([g[0] for g in dil_g], [g[1] for g in dil_g], [g[2] for g in dil_g], tabs, dz)

    dzf = jnp.pad(dfa_t.T, ((0, 0), (0, F_PAD - N_HEADS)))
    g_w_main = mm(h1, dz, ta=True, out_dtype=BF16, tm=1024, tn=Z_MAIN // 4, tk=SEQ, name="grad_w_in")
    g_w_f = mm(h1, dzf, ta=True, out_dtype=BF16, tm=1024, tn=F_PAD, tk=1024, name="grad_w_in_forget")
    tok = mixer_grads_ready(dict(w_main=g_w_main, w_f=g_w_f))
    dh1 = [mm(dz, w_main, tb=True, out_dtype=BF16, tm=512, tn=1024, tk=Z_MAIN, name="d_h1", after=tok),
           mm(dzf, w_f, tb=True, out_dtype=BF16, tm=2048, tn=1024, tk=F_PAD, name="d_h1_forget", after=tok)]
    grad_x, gg_pre_mix = _rms_bwd(dh1, x, g_pre_mix, dx2, out_dtype=F32, name="rms_pre_mix_bwd")

    grads = dict(
        b_forget=g_b_forget.reshape(1, N_HEADS), conv_b=_ffn_deinterleave(g_conv_b),
        g_pre_mix=gg_pre_mix, g_post_mix=gg_post_mix, g_pre_ffn=gg_pre_ffn, g_post_ffn=gg_post_ffn)
    return sq_err, grad_x, grads


def _gather_two_level(shard, *, name):
    def body(x_ref, out_ref, send_sems, recv_sems, local_sem):
        x, y, c = lax.axis_index("x"), lax.axis_index("y"), lax.axis_index("c")
        me, sibling = (x, y, c), (x, y, 1 - c)
        chips = [(1 - x, y), (x, 1 - y), (1 - x, 1 - y)]

        def slot(px, py, pc):
            return out_ref.at[4 * px + 2 * py + pc]

        def copy(k, block, to, src=None):
            return pltpu.make_async_remote_copy(
                src_ref=slot(*block) if src is None else src, dst_ref=slot(*block),
                send_sem=send_sems.at[k], recv_sem=recv_sems.at[k], device_id=to, device_id_type=MESH_ID)

        mine = pltpu.make_async_copy(x_ref, slot(*me), local_sem)
        mine.start()
        first = [copy(0, me, sibling, src=x_ref)]
        first += [copy(1 + j, me, (*chip, c), src=x_ref) for j, chip in enumerate(chips)]
        for cp in first:
            cp.start()
        passed = [copy(4 + j, (*chip, c), sibling) for j, chip in enumerate(chips)]
        for j, chip in enumerate(chips):
            copy(1 + j, (*chip, c), me).wait_recv()
            passed[j].start()
        copy(0, sibling, me).wait_recv()
        for j, chip in enumerate(chips):
            copy(4 + j, (*chip, 1 - c), me).wait_recv()
        for cp in first + passed:
            cp.wait_send()
        mine.wait()

    return pl.pallas_call(
        body, name=name, in_specs=[ANY], out_specs=ANY, out_shape=_sds((N_DEV,) + shard.shape, shard.dtype),
        scratch_shapes=[pltpu.SemaphoreType.DMA((N_DEV - 1,)), pltpu.SemaphoreType.DMA((N_DEV - 1,)),
                        pltpu.SemaphoreType.DMA],
    )(shard)


N_CHIPS = N_DEV // 2


def _peers(chips_only=False):
    x, y, c = lax.axis_index("x"), lax.axis_index("y"), lax.axis_index("c")
    out = []
    if chips_only:
        for k in range(1, N_CHIPS):
            px = 1 - x if k & 2 else x
            py = 1 - y if k & 1 else y
            out.append(((px, py, c), 2 * px + py))
        return 2 * x + y, out
    for k in range(1, N_DEV):
        px = 1 - x if k & 4 else x
        py = 1 - y if k & 2 else y
        pc = 1 - c if k & 1 else c
        out.append(((px, py, pc), 4 * px + 2 * py + pc))
    return 4 * x + 2 * y + c, out


def _sibling_swap(slot_arrays, *, name):
    n = len(slot_arrays)

    def body(*refs):
        ins, outs, send_sems, recv_sems = refs[:n], refs[n:2 * n], refs[2 * n], refs[2 * n + 1]
        x, y, c = lax.axis_index("x"), lax.axis_index("y"), lax.axis_index("c")
        copies = [pltpu.make_async_remote_copy(
            src_ref=ins[a].at[2 * q + (1 - c)], dst_ref=outs[a].at[q], send_sem=send_sems.at[a, q],
            recv_sem=recv_sems.at[a, q], device_id=(x, y, 1 - c), device_id_type=MESH_ID)
            for a in range(n) for q in range(N_CHIPS)]
        for cp in copies:
            cp.start()
        for cp in copies:
            cp.wait_recv()
        for cp in copies:
            cp.wait_send()

    return pl.pallas_call(
        body, name=name, in_specs=[ANY] * n, out_specs=[ANY] * n,
        out_shape=[_sds((N_CHIPS,) + t.shape[1:], t.dtype) for t in slot_arrays],
        scratch_shapes=[pltpu.SemaphoreType.DMA((n, N_CHIPS)), pltpu.SemaphoreType.DMA((n, N_CHIPS))],
    )(*slot_arrays)


def _pair_sum(slots, from_sibling, *, name, tn):
    _, r, c = slots.shape
    core = lax.axis_index("c").astype(jnp.int32).reshape(1)

    def body(core_ref, a_ref, b_ref, o_ref):
        o_ref[...] = (a_ref[...].astype(F32) + b_ref[...].astype(F32)).astype(o_ref.dtype)

    blk = lambda f: pl.BlockSpec((1, r, tn), f)
    return pl.pallas_call(
        body, name=name,
        grid_spec=pltpu.PrefetchScalarGridSpec(
            num_scalar_prefetch=1, grid=(N_CHIPS, c // tn),
            in_specs=[blk(lambda q, j, core: (2 * q + core[0], 0, j)), blk(lambda q, j, core: (q, 0, j))],
            out_specs=blk(lambda q, j, core: (q, 0, j))),
        out_shape=_sds((N_CHIPS, r, c), slots.dtype),
        compiler_params=_params("parallel", "parallel"),
    )(core, slots, from_sibling)


HBM = pl.BlockSpec(memory_space=pltpu.HBM)
SEM = pl.BlockSpec(memory_space=pltpu.SEMAPHORE)
DATAFLOW = pltpu.SideEffectType.DATAFLOW_SIDE_EFFECTING


def _split_copy(srcs, lands, send_sems, recv_sems, scatter, a, k, me, peers, incoming=False):
    dev, slot = peers[k]
    if incoming:
        src = dst = lands[a].at[slot]
    else:
        src, dst = (srcs[a].at[slot] if scatter else srcs[a]), lands[a].at[me]
    sem = a * len(peers) + k
    return pltpu.make_async_remote_copy(
        src_ref=src, dst_ref=dst, send_sem=send_sems.at[sem], recv_sem=recv_sems.at[sem],
        device_id=dev, device_id_type=MESH_ID)


def _own_copy(srcs, lands, own_sems, scatter, a, me):
    return pltpu.make_async_copy(srcs[a].at[me] if scatter else srcs[a], lands[a].at[me], own_sems.at[a])


def _exchange_start(arrays, scatter, *, name, chips_only=False, after=None):
    n = len(arrays)
    n_slots = N_CHIPS if chips_only else N_DEV
    n_in = 2 * n + len(_also(after))

    def body(*refs):
        srcs, lands = refs[:n], refs[n:2 * n]
        send_sems, recv_sems, own_sems = refs[n_in:n_in + 3]
        token = refs[-1]
        me, peers = _peers(chips_only)
        for k in range(len(peers)):
            for a in range(n):
                _split_copy(srcs, lands, send_sems, recv_sems, scatter, a, k, me, peers).start()
        for a in range(n):
            _own_copy(srcs, lands, own_sems, scatter, a, me).start()
        token[...] = jnp.zeros_like(token)

    land_shapes = [((n_slots,) + a.shape[-2:], a.dtype) for a in arrays]
    sems = pltpu.SemaphoreType.DMA((n * (n_slots - 1),))
    outs = pl.pallas_call(
        body, name=name,
        out_shape=(sems, sems, pltpu.SemaphoreType.DMA((n,)), *[pltpu.HBM(a.shape, a.dtype) for a in arrays],
                   *[pltpu.HBM(s, d) for s, d in land_shapes], _sds((SUBLANE, LANE), F32)),
        in_specs=[HBM] * (2 * n) + [ANY] * len(_also(after)),
        out_specs=(SEM, SEM, SEM, *[HBM] * (2 * n), pl.BlockSpec(memory_space=pltpu.VMEM)),
        input_output_aliases={i: 3 + i for i in range(2 * n)},
        compiler_params=pltpu.CompilerParams(has_side_effects=DATAFLOW),
    )(*[pltpu.with_memory_space_constraint(a, pltpu.HBM) for a in arrays],
      *[pltpu.with_memory_space_constraint(lax.empty(s, d), pltpu.HBM) for s, d in land_shapes], *_also(after))
    return (outs[:3], outs[3:3 + n], outs[3 + n:3 + 2 * n], scatter, chips_only), outs[-1]


def _exchange_wait(handles, after, *, name):
    sems, srcs, lands, scatter, chips_only = handles
    n = len(srcs)

    def body(*refs):
        src_refs, land_refs = refs[:n], refs[n:2 * n]
        send_ref, recv_ref, own_ref = refs[2 * n:2 * n + 3]
        me, peers = _peers(chips_only)
        for k in range(len(peers)):
            for a in range(n):
                _split_copy(src_refs, land_refs, send_ref, recv_ref, scatter, a, k, me, peers).wait_send()
                _split_copy(src_refs, land_refs, send_ref, recv_ref, scatter, a, k, me, peers, True).wait_recv()
        for a in range(n):
            _own_copy(src_refs, land_refs, own_ref, scatter, a, me).wait()

    outs = pl.pallas_call(
        body, name=name,
        out_shape=tuple(pltpu.HBM(t.shape, t.dtype) for t in (*srcs, *lands)),
        in_specs=[HBM] * (2 * n) + [SEM, SEM, SEM, pl.BlockSpec(memory_space=pl.ANY)],
        out_specs=tuple([HBM] * (2 * n)),
        input_output_aliases={i: i for i in range(2 * n)},
        compiler_params=pltpu.CompilerParams(has_side_effects=DATAFLOW),
    )(*srcs, *lands, *sems, after)
    return outs[n:]


def _adamw(parts, w, m, v, *, name, tm):
    r, c = w.shape
    assert r % tm == 0

    def body(p_ref, w_ref, m_ref, v_ref, g_ref, d_ref, nm_ref, nv_ref):
        _adamw_update(p_ref, w_ref, m_ref, v_ref, g_ref, d_ref, nm_ref, nv_ref)

    blk = pl.BlockSpec((tm, c), lambda i: (i, 0))
    return pl.pallas_call(
        body, name=name, grid=(r // tm,),
        in_specs=[pl.BlockSpec((parts.shape[0], tm, c), lambda i: (0, i, 0)), blk, blk, blk],
        out_specs=[blk] * 4, out_shape=[_sds((r, c), F32)] * 4,
        compiler_params=_params("parallel"),
    )(parts, w, m, v)


def _adamw_update(p_ref, w_ref, m_ref, v_ref, g_ref, d_ref, nm_ref, nv_ref):
    g = p_ref[0].astype(F32)
    for s in range(1, p_ref.shape[0]):
        g = g + p_ref[s].astype(F32)
    g_ref[...] = g
    m_new = ADAM_B1 * m_ref[...] + (1.0 - ADAM_B1) * g
    v_new = ADAM_B2 * v_ref[...] + (1.0 - ADAM_B2) * (g * g)
    nm_ref[...] = m_new
    nv_ref[...] = v_new
    m_hat = m_new / (1.0 - ADAM_B1 ** ADAM_STEP)
    v_hat = v_new / (1.0 - ADAM_B2 ** ADAM_STEP)
    d_ref[...] = -ADAM_LR * (m_hat / (jnp.sqrt(v_hat) + ADAM_EPS) + ADAM_WD * w_ref[...])


SMALL = ("g_pre_mix", "b_forget", "g_post_mix", "g_pre_ffn", "conv_b", "g_post_ffn")


def _adamw_small(parts, ws, ms, vs, sq_err_parts):
    n = len(ws)

    def body(*refs):
        ins, sq_ref, outs, loss_ref = refs[:4 * n], refs[4 * n], refs[4 * n + 1:-1], refs[-1]
        for i in range(n):
            _adamw_update(ins[i], ins[n + i], ins[2 * n + i], ins[3 * n + i], *outs[4 * i:4 * i + 4])
        total = sq_ref[0]
        for s in range(1, N_DEV):
            total = total + sq_ref[s]
        loss_ref[...] = total * (0.5 / D_MODEL)

    res = pl.pallas_call(
        body, name="adamw_small",
        out_shape=[_sds(w.shape, F32) for w in ws for _ in range(4)] + [_sds((1, LANE), F32)],
        compiler_params=pltpu.CompilerParams(vmem_limit_bytes=VMEM_LIMIT),
    )(*parts, *ws, *ms, *vs, sq_err_parts)
    return [res[4 * i:4 * i + 4] for i in range(n)], res[-1][0, 0]


def kernel(x, g_pre_mix, w_in, b_forget, w_o_fox, w_o_dil, w_out, g_post_mix, g_pre_ffn, w_up, conv_w, conv_b, w_down, g_post_ffn, loss_target, m_g_pre_mix, m_w_in, m_b_forget, m_w_o_fox, m_w_o_dil, m_w_out, m_g_post_mix, m_g_pre_ffn, m_w_up, m_conv_w, m_conv_b, m_w_down, m_g_post_ffn, v_g_pre_mix, v_w_in, v_b_forget, v_w_o_fox, v_w_o_dil, v_w_out, v_g_post_mix, v_g_pre_ffn, v_w_up, v_conv_w, v_conv_b, v_w_down, v_g_post_ffn):
    names = ("g_pre_mix", "w_in", "b_forget", "w_o_fox", "w_o_dil", "w_out", "g_post_mix", "g_pre_ffn",
             "w_up", "conv_w", "conv_b", "w_down", "g_post_ffn")
    w = dict(g_pre_mix=g_pre_mix, w_in=w_in, b_forget=b_forget, w_o_fox=w_o_fox, w_o_dil=w_o_dil, w_out=w_out,
             g_post_mix=g_post_mix, g_pre_ffn=g_pre_ffn, w_up=w_up, conv_w=conv_w, conv_b=conv_b, w_down=w_down,
             g_post_ffn=g_post_ffn)
    m = dict(g_pre_mix=m_g_pre_mix, w_in=m_w_in, b_forget=m_b_forget, w_o_fox=m_w_o_fox, w_o_dil=m_w_o_dil,
             w_out=m_w_out, g_post_mix=m_g_post_mix, g_pre_ffn=m_g_pre_ffn, w_up=m_w_up, conv_w=m_conv_w,
             conv_b=m_conv_b, w_down=m_w_down, g_post_ffn=m_g_post_ffn)
    v = dict(g_pre_mix=v_g_pre_mix, w_in=v_w_in, b_forget=v_b_forget, w_o_fox=v_w_o_fox, w_o_dil=v_w_o_dil,
             w_out=v_w_out, g_post_mix=v_g_post_mix, g_pre_ffn=v_g_pre_ffn, w_up=v_w_up, conv_w=v_conv_w,
             conv_b=v_conv_b, w_down=v_w_down, g_post_ffn=v_g_post_ffn)
    sharded = ("w_in", "w_o_fox", "w_o_dil", "w_out", "w_up", "w_down", "conv_w")
    wire = lambda n: F32 if n == "conv_w" else BF16

    by_cols = lambda t: jnp.transpose(t, (1, 0, 2)).reshape(t.shape[1], N_DEV * t.shape[2])
    by_rows = lambda t: t.reshape(N_DEV * t.shape[1], t.shape[2])
    col_slots = lambda t: jnp.transpose(t.reshape(t.shape[0], N_DEV, t.shape[1] // N_DEV), (1, 0, 2))
    row_slots = lambda t: t.reshape(N_DEV, t.shape[0] // N_DEV, t.shape[1])
    to_slots = lambda n, t: (row_slots if n in ("w_out", "w_down") else col_slots)(t).astype(wire(n))
    shard = lambda n: w[n][0].astype(wire(n))

    w_main, w_f = _w_in_from_shards(_gather_two_level(shard("w_in"), name="gather_w_in"))
    proj_handles, proj_tok = _exchange_start(
        [shard("w_o_fox"), shard("w_o_dil"), shard("w_out")], False, name="gather_proj_start", after=w_f)
    ffn_handles, ffn_tok = _exchange_start(
        [shard("w_up"), shard("conv_w"), shard("w_down")], False, name="gather_ffn_start", after=proj_tok)

    def proj_weights(after):
        w_oa, w_ob, w_o = _exchange_wait(proj_handles, after, name="gather_proj_wait")
        return by_cols(w_oa), by_cols(w_ob), by_rows(w_o)

    def ffn_weights(after):
        w_u, conv, w_d = _exchange_wait(ffn_handles, after, name="gather_ffn_wait")
        return _w_up_from_shards(w_u), _ffn_interleave(by_cols(conv)), by_rows(w_d)

    pending = {}

    def ffn_grads_ready(g):
        slots = [to_slots("w_down", g["w_down"]), _w_up_to_shards(g["w_up_blocks"]), to_slots("conv_w", g["conv_w"])]
        pending["ffn"] = _exchange_start(slots, True, name="scatter_ffn_start")
        return pending["ffn"][1]

    def proj_grads_ready(g):
        pending["proj"] = _exchange_start([g["w_o_fox"], g["w_o_dil"], to_slots("w_out", g["w_out"])], True,
                                          name="scatter_proj_start")
        return pending["proj"][1]

    def mixer_grads_ready(g):
        slots = _w_in_to_shards(g["w_main"], g["w_f"])
        theirs = _sibling_swap([slots], name="scatter_w_in_swap")[0]
        chip_sums = _pair_sum(slots, theirs, name="scatter_w_in_pair_sum", tn=W_IN_SHARD)
        pending["w_in"] = _exchange_start([chip_sums], True, name="scatter_w_in_start", chips_only=True)
        return pending["w_in"][1]

    sq_err, grad_x, g = _local_step(
        x[0], loss_target[0], w_main, w_f, b_forget, conv_b, g_pre_mix, g_post_mix, g_pre_ffn,
        g_post_ffn, proj_weights, ffn_weights, ffn_grads_ready, proj_grads_ready, mixer_grads_ready, after=ffn_tok)

    small_handles, small_tok = _exchange_start([g[n] for n in SMALL] + [sq_err], False, name="gather_small_start")
    tiles = dict(w_in=256, w_o_fox=512, w_o_dil=512, w_out=128, w_up=256, w_down=176, conv_w=3)
    adam = lambda n, p: _adamw(p, w[n][0], m[n][0], v[n][0], name=f"adamw_{n}", tm=tiles[n])
    res = {}
    for key, group in (("ffn", ("w_down", "w_up", "conv_w")), ("proj", ("w_o_fox", "w_o_dil", "w_out"))):
        landed = _exchange_wait(pending[key][0], small_tok, name=f"scatter_{key}_wait")
        res.update({n: adam(n, p) for n, p in zip(group, landed)})
    done = res["w_up"][3]
    res["w_in"] = adam("w_in", _exchange_wait(pending["w_in"][0], done, name="scatter_w_in_wait")[0])
    small_parts = _exchange_wait(small_handles, res["w_in"][3], name="gather_small_wait")
    small, loss = _adamw_small(small_parts[:-1], *[[t[n] for n in SMALL] for t in (w, m, v)], small_parts[-1])
    small = dict(zip(SMALL, small))
    out = [[(res[n][k][None] if n in sharded else small[n][k]) for n in names] for k in range(4)]
    return (loss, grad_x[None], *out[0], *out[1], *out[2], *out[3])
```

```python
import functools
import math

import jax
import jax.numpy as jnp
import numpy as np
from jax import lax
from jax.experimental import pallas as pl
from jax.experimental.pallas import tpu as pltpu

F32 = jnp.float32
BF16 = jnp.bfloat16

SEQ = 4096
D_MODEL = 1024
N_HEADS = 8
HEAD_DIM = 64
ATT_W = N_HEADS * HEAD_DIM
D_FF = 2816
Z_MAIN = 5120
F_PAD = 128
ROPE_DIM = 16
ROPE_THETA = 500000.0
RMS_EPS = 1e-6
NEG_INF = -1e30
SCALE = 1.0 / math.sqrt(HEAD_DIM)
DIL_PATTERNS = ((128, 1), (512, 4), (2048, 16))
DIL_BLK = 128
DIL_STEP_BLOCKS = 2
N_DEV = 8

ADAM_LR = 0.001
ADAM_B1 = 0.9
ADAM_B2 = 0.999
ADAM_EPS = 1e-08
ADAM_WD = 0.01
ADAM_STEP = 10

LANE = 128
SUBLANE = 8
VMEM_LIMIT = 56 * 1024 * 1024
MESH_ID = pl.DeviceIdType.MESH
ANY = pl.BlockSpec(memory_space=pl.ANY)


def _params(*sem):
    return pltpu.CompilerParams(dimension_semantics=sem, vmem_limit_bytes=VMEM_LIMIT)


def _sds(shape, dtype):
    return jax.ShapeDtypeStruct(shape, dtype)


def _also(after):
    return [] if after is None else [after]


def _matmul(a, b, *, ta=False, tb=False, out_dtype, tm, tn, tk, name, b_k_off=0, after=None, col_slots=1):
    n_after = len(_also(after))
    if ta:
        kk, m = a.shape
    else:
        m, kk = a.shape
    n = b.shape[0] if tb else b.shape[1]
    tm, tn, tk = min(tm, m), min(tn, n), min(tk, kk)
    assert (b.shape[1] if tb else b.shape[0]) >= b_k_off * tk + kk
    assert m % tm == 0 and n % tn == 0 and kk % tk == 0, (name, m, n, kk, tm, tn, tk)
    nk = kk // tk
    dims = (((0 if ta else 1,), (1 if tb else 0,)), ((), ()))
    slot_w = n // col_slots
    assert col_slots == 1 or (nk == 1 and tn == n and slot_w % LANE == 0), name

    def body(a_ref, b_ref, *rest):
        o_ref, scratch = rest[n_after], rest[n_after + 1:]
        p = lax.dot_general(a_ref[...].astype(BF16), b_ref[...].astype(BF16), dims,
                            preferred_element_type=F32)
        if col_slots > 1:
            for s in range(col_slots):
                o_ref[s] = p[:, s * slot_w:(s + 1) * slot_w].astype(o_ref.dtype)
        elif nk == 1:
            o_ref[...] = p.astype(o_ref.dtype)
        else:
            acc = scratch[0]
            k = pl.program_id(2)

            @pl.when(k == 0)
            def _():
                acc[...] = p

            @pl.when(k > 0)
            def _():
                acc[...] += p

            @pl.when(k == nk - 1)
            def _():
                o_ref[...] = acc[...].astype(o_ref.dtype)

    a_spec = (pl.BlockSpec((tk, tm), lambda i, j, k: (k, i)) if ta
              else pl.BlockSpec((tm, tk), lambda i, j, k: (i, k)))
    b_spec = (pl.BlockSpec((tn, tk), lambda i, j, k: (j, k + b_k_off)) if tb
              else pl.BlockSpec((tk, tn), lambda i, j, k: (k + b_k_off, j)))
    return pl.pallas_call(
        body, name=name, grid=(m // tm, n // tn, nk),
        in_specs=[a_spec, b_spec] + [ANY] * n_after,
        out_specs=(pl.BlockSpec((tm, tn), lambda i, j, k: (i, j)) if col_slots == 1
                   else pl.BlockSpec((col_slots, tm, slot_w), lambda i, j, k: (0, i, 0))),
        out_shape=_sds((m, n) if col_slots == 1 else (col_slots, m, slot_w), out_dtype),
        scratch_shapes=[pltpu.VMEM((tm, tn), F32)] if nk > 1 else [],
        compiler_params=_params("parallel", "parallel", "arbitrary"),
    )(a, b, *_also(after))


def _rms_fwd(x, g, *, name, tm=512, after=None):
    def body(x_ref, g_ref, *rest):
        h_ref = rest[-1]
        xv = x_ref[...]
        r = lax.rsqrt(jnp.mean(xv * xv, axis=-1, keepdims=True) + RMS_EPS)
        h_ref[...] = (xv * r * g_ref[...]).astype(h_ref.dtype)

    return pl.pallas_call(
        body, name=name, grid=(SEQ // tm,),
        in_specs=[pl.BlockSpec((tm, D_MODEL), lambda i: (i, 0)), pl.BlockSpec((1, D_MODEL), lambda i: (0, 0))]
        + [ANY] * len(_also(after)),
        out_specs=pl.BlockSpec((tm, D_MODEL), lambda i: (i, 0)),
        out_shape=_sds((SEQ, D_MODEL), BF16),
        compiler_params=_params("parallel"),
    )(x, g, *_also(after))


def _rms_bwd(dh_parts, xin, g, dres, *, out_dtype, name, tm=512):
    n_parts = len(dh_parts)
    has_res = dres is not None

    def body(*refs):
        parts = refs[:n_parts]
        x_ref, g_ref = refs[n_parts], refs[n_parts + 1]
        res_ref = refs[n_parts + 2] if has_res else None
        o_ref, gg_ref = refs[-2], refs[-1]
        dh = parts[0][...].astype(F32)
        for p in parts[1:]:
            dh = dh + p[...].astype(F32)
        xv = x_ref[...]
        r = lax.rsqrt(jnp.mean(xv * xv, axis=-1, keepdims=True) + RMS_EPS)
        xn = xv * r

        @pl.when(pl.program_id(0) == 0)
        def _():
            gg_ref[...] = jnp.zeros_like(gg_ref)

        gg_ref[...] += jnp.sum(dh * xn, axis=0, keepdims=True)
        dxn = dh * g_ref[...]
        dx = r * (dxn - xn * jnp.mean(dxn * xn, axis=-1, keepdims=True))
        if has_res:
            dx = dx + res_ref[...]
        o_ref[...] = dx.astype(o_ref.dtype)

    row = pl.BlockSpec((tm, D_MODEL), lambda i: (i, 0))
    vec = pl.BlockSpec((1, D_MODEL), lambda i: (0, 0))
    args = list(dh_parts) + [xin, g] + ([dres] if has_res else [])
    return pl.pallas_call(
        body, name=name, grid=(SEQ // tm,),
        in_specs=[row] * n_parts + [row, vec] + ([row] if has_res else []),
        out_specs=[row, vec],
        out_shape=[_sds((SEQ, D_MODEL), out_dtype), _sds((1, D_MODEL), F32)],
        compiler_params=_params("arbitrary"),
    )(*args)


def _rms_pair_bwd(dh_parts, x2, g_pre, dres, y1, g_post, *, tm=512, after=None):
    n_parts = len(dh_parts)

    def norm_bwd(dh, xin, g_ref, gg_ref):
        r = lax.rsqrt(jnp.mean(xin * xin, axis=-1, keepdims=True) + RMS_EPS)
        xn = xin * r
        gg_ref[...] += jnp.sum(dh * xn, axis=0, keepdims=True)
        dxn = dh * g_ref[...]
        return r * (dxn - xn * jnp.mean(dxn * xn, axis=-1, keepdims=True))

    def body(*refs):
        parts = refs[:n_parts]
        x2_ref, gpre_ref, res_ref, y1_ref, gpost_ref = refs[n_parts:n_parts + 5]
        dx2_ref, dy1_ref, ggpre_ref, ggpost_ref = refs[-4:]

        @pl.when(pl.program_id(0) == 0)
        def _():
            ggpre_ref[...] = jnp.zeros_like(ggpre_ref)
            ggpost_ref[...] = jnp.zeros_like(ggpost_ref)

        dh = parts[0][...].astype(F32)
        for p in parts[1:]:
            dh = dh + p[...].astype(F32)
        dx2 = res_ref[...] + norm_bwd(dh, x2_ref[...], gpre_ref, ggpre_ref)
        dx2_ref[...] = dx2
        dy1_ref[...] = norm_bwd(dx2, y1_ref[...], gpost_ref, ggpost_ref).astype(dy1_ref.dtype)

    row = pl.BlockSpec((tm, D_MODEL), lambda i: (i, 0))
    vec = pl.BlockSpec((1, D_MODEL), lambda i: (0, 0))
    return pl.pallas_call(
        body, name="rms_pair_bwd", grid=(SEQ // tm,),
        in_specs=[row] * n_parts + [row, vec, row, row, vec] + [ANY] * len(_also(after)),
        out_specs=[row, row, vec, vec],
        out_shape=[_sds((SEQ, D_MODEL), F32), _sds((SEQ, D_MODEL), BF16), _sds((1, D_MODEL), F32),
                   _sds((1, D_MODEL), F32)],
        compiler_params=_params("arbitrary"),
    )(*dh_parts, x2, g_pre, dres, y1, g_post, *_also(after))


SCAN_BLK = 512


def _split_dot(v, tri):
    hi = v.astype(BF16)
    r1 = v - hi.astype(F32)
    mid = r1.astype(BF16)
    lo = (r1 - mid.astype(F32)).astype(BF16)
    dot = functools.partial(jnp.dot, preferred_element_type=F32)
    return dot(hi, tri) + dot(mid, tri) + dot(lo, tri)


def _fox_prep(fa_t, b_col):
    nblk = SEQ // SCAN_BLK

    def body(fa_ref, b_ref, f_ref, sg_ref):
        row = lax.broadcasted_iota(jnp.int32, (SCAN_BLK, SCAN_BLK), 0)
        col = lax.broadcasted_iota(jnp.int32, (SCAN_BLK, SCAN_BLK), 1)
        upper = (row <= col).astype(BF16)
        carry = jnp.zeros((N_HEADS, 1), F32)
        for blk in range(nblk):
            sl = pl.ds(blk * SCAN_BLK, SCAN_BLK)
            xx = fa_ref[:, sl] + b_ref[...]
            e = jnp.exp(-jnp.abs(xx))
            logf = jnp.minimum(xx, 0.0) - jnp.log(1.0 + e)
            sg_ref[:, sl] = jnp.where(xx >= 0.0, e, 1.0) / (1.0 + e)
            c = _split_dot(logf, upper) + carry
            f_ref[:, sl] = c
            carry = c[:, SCAN_BLK - 1:SCAN_BLK]

    return pl.pallas_call(
        body, name="fox_prep",
        out_shape=[_sds((N_HEADS, SEQ), F32), _sds((N_HEADS, SEQ), F32)],
        compiler_params=pltpu.CompilerParams(vmem_limit_bytes=VMEM_LIMIT),
    )(fa_t, b_col)


def _fox_post_bwd(df_t, sg_t):
    nblk = SEQ // SCAN_BLK

    def body(df_ref, sg_ref, dfa_ref, gb_ref):
        row = lax.broadcasted_iota(jnp.int32, (SCAN_BLK, SCAN_BLK), 0)
        col = lax.broadcasted_iota(jnp.int32, (SCAN_BLK, SCAN_BLK), 1)
        lower = (row >= col).astype(BF16)
        carry = jnp.zeros((N_HEADS, 1), F32)
        gb = jnp.zeros((N_HEADS, 1), F32)
        for blk in reversed(range(nblk)):
            sl = pl.ds(blk * SCAN_BLK, SCAN_BLK)
            c = _split_dot(df_ref[:, sl], lower) + carry
            carry = c[:, 0:1]
            dfa = c * sg_ref[:, sl]
            dfa_ref[:, sl] = dfa
            gb = gb + jnp.sum(dfa, axis=1, keepdims=True)
        gb_ref[...] = gb

    return pl.pallas_call(
        body, name="fox_post_bwd",
        out_shape=[_sds((N_HEADS, SEQ), F32), _sds((N_HEADS, 1), F32)],
        compiler_params=pltpu.CompilerParams(vmem_limit_bytes=VMEM_LIMIT),
    )(df_t, sg_t)


FOX_T = 512
NT_DIMS = (((1,), (1,)), ((), ()))
TN_DIMS = (((0,), (0,)), ((), ()))


def _head(ref_or_val, h):
    return ref_or_val[:, h * HEAD_DIM:(h + 1) * HEAD_DIM]


def _split3(v):
    hi = v.astype(BF16).astype(F32)
    r1 = v - hi
    mid = r1.astype(BF16).astype(F32)
    return hi, mid, (r1 - mid).astype(BF16).astype(F32)


ONE_LANE = 3 * N_HEADS


def _pack_terms(v, with_one):
    hi, mid, lo = _split3(v)
    t = hi + pltpu.roll(mid, N_HEADS, 1) + pltpu.roll(lo, 2 * N_HEADS, 1)
    if with_one:
        t = t + (lax.broadcasted_iota(jnp.int32, v.shape, 1) == ONE_LANE).astype(F32)
    return t.astype(BF16)


def _aux_matrices():
    to_q = np.zeros((LANE, N_HEADS * 2 * HEAD_DIM), np.float32)
    to_k = np.zeros_like(to_q)
    for h in range(N_HEADS):
        base = h * 2 * HEAD_DIM + HEAD_DIM
        for s in range(3):
            to_q[s * N_HEADS + h, base + s] = 1.0
            to_q[ONE_LANE, base + 3 + s] = 1.0
            to_k[ONE_LANE, base + s] = 1.0
            to_k[s * N_HEADS + h, base + 3 + s] = -1.0
    return jnp.asarray(to_q, BF16), jnp.asarray(to_k, BF16)


def _head_sums():
    total = np.zeros((N_HEADS * HEAD_DIM, LANE), np.float32)
    first = np.zeros_like(total)
    for h in range(N_HEADS):
        total[h * HEAD_DIM:(h + 1) * HEAD_DIM, h] = 1.0
        first[h * HEAD_DIM, h] = 1.0
    return jnp.asarray(total, BF16), jnp.asarray(first, BF16)


SLOT = 2 * HEAD_DIM
N_SPLIT = 3
FOX_FWD_HEADS = 8
FOX_BWD_HEADS = 4


def _slot(ref, h):
    return ref[:, h * SLOT:(h + 1) * SLOT]


def _fox_pack_fwd(zm, f_cols, *, tm=512):
    def body(q_ref, k_ref, v_ref, f_ref, tq_ref, tk_ref, qs_ref, ks_ref, vs_ref):
        ones = jnp.ones((tm, HEAD_DIM), BF16)
        terms = _pack_terms(f_ref[...], True)
        q_aux = jnp.dot(terms, tq_ref[...], preferred_element_type=F32).astype(BF16)
        k_aux = jnp.dot(terms, tk_ref[...], preferred_element_type=F32).astype(BF16)
        for h in range(N_HEADS):
            aux = slice(h * SLOT + HEAD_DIM, (h + 1) * SLOT)
            qs_ref[:, h * SLOT:(h + 1) * SLOT] = jnp.concatenate(
                [(_head(q_ref, h).astype(F32) * SCALE).astype(BF16), q_aux[:, aux]], axis=1)
            ks_ref[:, h * SLOT:(h + 1) * SLOT] = jnp.concatenate([_head(k_ref, h), k_aux[:, aux]], axis=1)
            vs_ref[:, h * SLOT:(h + 1) * SLOT] = jnp.concatenate([_head(v_ref, h), ones], axis=1)

    col = lambda b: pl.BlockSpec((tm, ATT_W), lambda i: (i, b))
    wide = pl.BlockSpec((tm, N_HEADS * SLOT), lambda i: (i, 0))
    const = pl.BlockSpec((LANE, N_HEADS * SLOT), lambda i: (0, 0))
    return pl.pallas_call(
        body, name="fox_pack_fwd", grid=(SEQ // tm,),
        in_specs=[col(0), col(1), col(2), pl.BlockSpec((tm, LANE), lambda i: (i, 0)), const, const],
        out_specs=[wide] * 3, out_shape=[_sds((SEQ, N_HEADS * SLOT), BF16)] * 3,
        compiler_params=_params("parallel"),
    )(zm, zm, zm, f_cols, *_aux_matrices())


def _fox_pack_bwd(zm, f_cols, lse, o, do, *, tm=512, after=None):
    def body(q_ref, f_ref, lse_ref, o_ref, do_ref, tq_ref, total_ref, first_ref, *rest):
        qs_ref, ds_ref = rest[-2:]
        delta = _split_dot(o_ref[...].astype(F32) * do_ref[...].astype(F32), total_ref[...])
        lse_h = _split_dot(lse_ref[...], first_ref[...])
        q_aux = jnp.dot(_pack_terms(f_ref[...] - lse_h, True), tq_ref[...], preferred_element_type=F32).astype(BF16)
        d_aux = jnp.dot(_pack_terms(-delta, False), tq_ref[...], preferred_element_type=F32).astype(BF16)
        for h in range(N_HEADS):
            aux = slice(h * SLOT + HEAD_DIM, (h + 1) * SLOT)
            qs_ref[:, h * SLOT:(h + 1) * SLOT] = jnp.concatenate(
                [(_head(q_ref, h).astype(F32) * SCALE).astype(BF16), q_aux[:, aux]], axis=1)
            ds_ref[:, h * SLOT:(h + 1) * SLOT] = jnp.concatenate([_head(do_ref, h), d_aux[:, aux]], axis=1)

    row = pl.BlockSpec((tm, ATT_W), lambda i: (i, 0))
    wide = pl.BlockSpec((tm, N_HEADS * SLOT), lambda i: (i, 0))
    const = lambda r, c: pl.BlockSpec((r, c), lambda i: (0, 0))
    return pl.pallas_call(
        body, name="fox_pack_bwd", grid=(SEQ // tm,),
        in_specs=[row, pl.BlockSpec((tm, LANE), lambda i: (i, 0)), row, row, row,
                  const(LANE, N_HEADS * SLOT), const(ATT_W, LANE), const(ATT_W, LANE)] + [ANY] * len(_also(after)),
        out_specs=[wide] * 2, out_shape=[_sds((SEQ, N_HEADS * SLOT), BF16)] * 2,
        compiler_params=_params("parallel"),
    )(zm, f_cols, lse, o, do, _aux_matrices()[0], *_head_sums(), *_also(after))


def _causal_pairs(key_major):
    nb = SEQ // FOX_T
    if key_major:
        pairs = [(i, j) for j in range(nb) for i in range(j, nb)]
    else:
        pairs = [(i, j) for i in range(nb) for j in range(i + 1)]
    return (jnp.array([p[0] for p in pairs], jnp.int32), jnp.array([p[1] for p in pairs], jnp.int32), len(pairs))


FOX_HALF = FOX_T // 2
FOX_FULL = ((slice(0, FOX_T), slice(0, FOX_T), None),)
FOX_DIAG = ((slice(0, FOX_HALF), slice(0, FOX_HALF), 0), (slice(FOX_HALF, FOX_T), slice(0, FOX_T), FOX_HALF))


def _causal_piece_mask(q_rows, k_rows, offset):
    shape = (q_rows.stop - q_rows.start, k_rows.stop - k_rows.start)
    row = lax.broadcasted_iota(jnp.int32, shape, 0)
    col = lax.broadcasted_iota(jnp.int32, shape, 1)
    return col <= row + offset


def _fox_fwd(q_slots, k_slots, v_slots):
    i_tab, j_tab, n_pairs = _causal_pairs(False)

    def body(i_tab, j_tab, q_ref, k_ref, v_ref, o_ref, lse_ref, m_s, acc_s):
        t = pl.program_id(1)
        i, j = i_tab[t], j_tab[t]

        @pl.when(j == 0)
        def _():
            m_s[...] = jnp.full_like(m_s, NEG_INF)
            acc_s[...] = jnp.zeros_like(acc_s)

        def step(pieces):
            jobs = [(h, piece) for h in range(FOX_FWD_HEADS) for piece in pieces]
            lanes = lambda h: slice(h * SLOT, (h + 1) * SLOT)
            scores = [lax.dot_general(q_ref[qr, lanes(h)], k_ref[kr, lanes(h)], NT_DIMS, preferred_element_type=F32)
                      for h, (qr, kr, _) in jobs]
            probs, alphas = [], []
            for idx, (h, (qr, kr, offset)) in enumerate(jobs):
                s = scores[idx]
                if offset is not None:
                    s = jnp.where(_causal_piece_mask(qr, kr, offset), s, NEG_INF)
                m_prev = m_s[h, qr, :]
                m_new = jnp.maximum(m_prev, jnp.max(s, axis=-1, keepdims=True))
                probs.append(jnp.exp(s - jnp.tile(m_new, (1, s.shape[1] // LANE))).astype(BF16))
                alphas.append(jnp.exp(m_prev - m_new))
                m_s[h, qr, :] = m_new
            for idx, (h, (qr, kr, _)) in enumerate(jobs):
                acc_s[h, qr, :] = alphas[idx] * acc_s[h, qr, :] + jnp.dot(
                    probs[idx], v_ref[kr, lanes(h)], preferred_element_type=F32)

        @pl.when(j < i)
        def _():
            step(FOX_FULL)

        @pl.when(j == i)
        def _():
            step(FOX_DIAG)
            outs, lses = [], []
            for h in range(FOX_FWD_HEADS):
                acc = acc_s[h]
                l = acc[:, HEAD_DIM:]
                outs.append(acc[:, :HEAD_DIM] / l)
                lses.append(m_s[h][:, :HEAD_DIM] + jnp.log(l))
            o_ref[...] = jnp.concatenate(outs, axis=1).astype(o_ref.dtype)
            lse_ref[...] = jnp.concatenate(lses, axis=1)

    qspec = pl.BlockSpec((FOX_T, FOX_FWD_HEADS * SLOT), lambda p, t, it, jt: (it[t], p))
    kspec = pl.BlockSpec((FOX_T, FOX_FWD_HEADS * SLOT), lambda p, t, it, jt: (jt[t], p))
    ospec = pl.BlockSpec((FOX_T, FOX_FWD_HEADS * HEAD_DIM), lambda p, t, it, jt: (it[t], p))
    return pl.pallas_call(
        body, name="fox_fwd",
        grid_spec=pltpu.PrefetchScalarGridSpec(
            num_scalar_prefetch=2, grid=(N_HEADS // FOX_FWD_HEADS, n_pairs),
            in_specs=[qspec, kspec, kspec], out_specs=[ospec, ospec],
            scratch_shapes=[pltpu.VMEM((FOX_FWD_HEADS, FOX_T, LANE), F32),
                            pltpu.VMEM((FOX_FWD_HEADS, FOX_T, SLOT), F32)]),
        out_shape=[_sds((SEQ, ATT_W), BF16), _sds((SEQ, ATT_W), F32)],
        compiler_params=_params("parallel", "arbitrary"),
    )(i_tab, j_tab, q_slots, k_slots, v_slots)


def _fox_bwd(q_slots, k_slots, v_slots, do_slots):
    i_tab, j_tab, n_pairs = _causal_pairs(True)

    def body(i_tab, j_tab, q_ref, k_ref, v_ref, do_ref, dq_ref, dk_ref, dv_ref):
        t = pl.program_id(1)
        i, j = i_tab[t], j_tab[t]

        @pl.when(t == 0)
        def _():
            dq_ref[...] = jnp.zeros_like(dq_ref)

        @pl.when(i == j)
        def _():
            dk_ref[...] = jnp.zeros_like(dk_ref)
            dv_ref[...] = jnp.zeros_like(dv_ref)

        def step(pieces):
            jobs = [(h, piece) for h in range(FOX_BWD_HEADS) for piece in pieces]
            lanes = lambda h: slice(h * SLOT, (h + 1) * SLOT)
            scores = [lax.dot_general(q_ref[qr, lanes(h)], k_ref[kr, lanes(h)], NT_DIMS, preferred_element_type=F32)
                      for h, (qr, kr, _) in jobs]
            dps = [lax.dot_general(do_ref[qr, lanes(h)], v_ref[kr, lanes(h)], NT_DIMS, preferred_element_type=F32)
                   for h, (qr, kr, _) in jobs]
            ps, dss = [], []
            for idx, (h, (qr, kr, offset)) in enumerate(jobs):
                p = jnp.exp(scores[idx])
                if offset is not None:
                    p = jnp.where(_causal_piece_mask(qr, kr, offset), p, 0.0)
                ps.append(p.astype(BF16))
                dss.append((p * dps[idx]).astype(BF16))
            for idx, (h, (qr, kr, _)) in enumerate(jobs):
                rows = pl.ds(pl.multiple_of(i * FOX_T + qr.start, FOX_HALF), qr.stop - qr.start)
                dv_ref[kr, lanes(h)] += lax.dot_general(ps[idx], do_ref[qr, lanes(h)], TN_DIMS,
                                                        preferred_element_type=F32)
                dk_ref[kr, lanes(h)] += lax.dot_general(dss[idx], q_ref[qr, lanes(h)], TN_DIMS,
                                                        preferred_element_type=F32)
                dq_ref[rows, lanes(h)] += jnp.dot(dss[idx], k_ref[kr, lanes(h)], preferred_element_type=F32)

        @pl.when(i > j)
        def _():
            step(FOX_FULL)

        @pl.when(i == j)
        def _():
            step(FOX_DIAG)

    qspec = pl.BlockSpec((FOX_T, FOX_BWD_HEADS * SLOT), lambda p, t, it, jt: (it[t], p))
    kspec = pl.BlockSpec((FOX_T, FOX_BWD_HEADS * SLOT), lambda p, t, it, jt: (jt[t], p))
    return pl.pallas_call(
        body, name="fox_bwd",
        grid_spec=pltpu.PrefetchScalarGridSpec(
            num_scalar_prefetch=2, grid=(N_HEADS // FOX_BWD_HEADS, n_pairs),
            in_specs=[qspec, kspec, kspec, qspec],
            out_specs=[pl.BlockSpec((SEQ, FOX_BWD_HEADS * SLOT), lambda p, t, it, jt: (0, p)), kspec, kspec]),
        out_shape=[_sds((SEQ, N_HEADS * SLOT), F32)] * 3,
        compiler_params=_params("arbitrary", "arbitrary"),
    )(i_tab, j_tab, q_slots, k_slots, v_slots, do_slots)


def _fox_unpack(dq_slots, dk_slots, dv_slots, dz, *, tm=512):
    def body(dq_ref, dk_ref, dv_ref, dz_in, o_ref, df_ref):
        lane = lax.broadcasted_iota(jnp.int32, (tm, LANE), 1)
        df = jnp.zeros((tm, LANE), F32)
        for h in range(N_HEADS):
            lo = h * SLOT
            for part, (ref, mult) in enumerate(((dq_ref, SCALE), (dk_ref, 1.0), (dv_ref, 1.0))):
                o_ref[:, part * ATT_W + h * HEAD_DIM:part * ATT_W + (h + 1) * HEAD_DIM] = (
                    ref[:, lo:lo + HEAD_DIM] * mult).astype(o_ref.dtype)
            rows = dq_ref[:, lo + HEAD_DIM:lo + HEAD_DIM + 1]
            cols = dk_ref[:, lo + HEAD_DIM + N_SPLIT:lo + HEAD_DIM + N_SPLIT + 1]
            df = jnp.where(lane == h, rows - cols, df)
        df_ref[...] = df

    wide = pl.BlockSpec((tm, N_HEADS * SLOT), lambda i: (i, 0))
    return pl.pallas_call(
        body, name="fox_unpack", grid=(SEQ // tm,), in_specs=[wide] * 3 + [ANY],
        out_specs=[pl.BlockSpec((tm, 3 * ATT_W), lambda i: (i, 0)), pl.BlockSpec((tm, LANE), lambda i: (i, 0))],
        out_shape=[_sds((SEQ, Z_MAIN), BF16), _sds((SEQ, LANE), F32)],
        input_output_aliases={3: 0},
        compiler_params=_params("parallel"),
    )(dq_slots, dk_slots, dv_slots, dz)


def _dil_bwd_prep(o, do, lse, *, tm=512):
    dilations = [d for _, d in DIL_PATTERNS]
    o_chunks = ATT_W // LANE

    def body(o_ref, do_ref, lse_ref, *rest):
        outs, (do_scr, lse_scr, dl_scr) = rest[:-3], rest[-3:]
        dov = do_ref[...].astype(F32)
        prod = o_ref[...].astype(F32) * dov
        lane = lax.broadcasted_iota(jnp.int32, (tm, LANE), 1)
        delta = jnp.zeros((tm, LANE), F32)
        for h in range(N_HEADS):
            delta = jnp.where(lane == h, jnp.sum(_head(prod, h), axis=1, keepdims=True), delta)
        for ch in range(o_chunks):
            do_scr[ch] = dov[:, ch * LANE:(ch + 1) * LANE]
        lse_scr[0] = lse_ref[...]
        dl_scr[0] = delta
        for k, d in enumerate(dilations):
            for scr, out in zip((do_scr, lse_scr, dl_scr), outs[3 * k:3 * k + 3]):
                _slabs_from_rows(scr, out, d)

    row = pl.BlockSpec((tm, ATT_W), lambda i: (i, 0))
    view = lambda d, w: pl.BlockSpec((tm // d, d * w), lambda i: (i, 0))
    outs = pl.pallas_call(
        body, name="dil_bwd_prep", grid=(SEQ // tm,),
        in_specs=[row, row, pl.BlockSpec((tm, LANE), lambda i: (i, 0))],
        out_specs=[view(d, w) for d in dilations for w in (ATT_W, LANE, LANE)],
        out_shape=[_sds((SEQ // d, d * w), t) for d in dilations for w, t in ((ATT_W, BF16), (LANE, F32), (LANE, F32))],
        scratch_shapes=[pltpu.VMEM((o_chunks, tm, LANE), F32), pltpu.VMEM((1, tm, LANE), F32),
                        pltpu.VMEM((1, tm, LANE), F32)],
        compiler_params=_params("parallel"),
    )(o, do, lse)
    return [outs[3 * k:3 * k + 3] for k in range(len(dilations))]


def _rope_tables():
    half = ROPE_DIM // 2
    inv_freq = np.float32(ROPE_THETA) ** (-np.arange(half, dtype=np.float32) * np.float32(2.0) / np.float32(ROPE_DIM))
    ang = np.arange(SEQ, dtype=np.float32)[:, None] * inv_freq.astype(np.float32)[None, :]
    cos, sin = jnp.asarray(np.cos(ang).astype(np.float32)), jnp.asarray(np.sin(ang).astype(np.float32))
    ones = jnp.ones((SEQ, HEAD_DIM - ROPE_DIM), F32)
    zeros = jnp.zeros((SEQ, HEAD_DIM - ROPE_DIM), F32)
    zh = jnp.zeros((SEQ, half), F32)
    c_tab = jnp.concatenate([cos, cos, ones], axis=1)
    a_tab = jnp.concatenate([-sin, zh, zeros], axis=1)
    b_tab = jnp.concatenate([zh, sin, zeros], axis=1)
    two = lambda t: jnp.concatenate([t, t], axis=1)
    return two(c_tab), two(a_tab), two(b_tab)


def _rotate(x, c_tab, a_tab, b_tab):
    return x * c_tab + pltpu.roll(x, LANE - ROPE_DIM // 2, 1) * a_tab + pltpu.roll(x, ROPE_DIM // 2, 1) * b_tab


def _rope_fwd(zm, tabs, *, tm=512):
    width = 3 * ATT_W
    dilations = [d for _, d in DIL_PATTERNS]

    def body(q_ref, k_ref, v_ref, c_ref, a_ref, b_ref, *rest):
        outs, scr = rest[:-1], rest[-1]
        per_part = ATT_W // LANE
        for part, (x_ref, mult) in enumerate(((q_ref, SCALE), (k_ref, 1.0))):
            for cc in range(per_part):
                sl = slice(cc * LANE, (cc + 1) * LANE)
                scr[part * per_part + cc] = _rotate(x_ref[:, sl].astype(F32), c_ref[...], a_ref[...], b_ref[...]) * mult
        for cc in range(per_part):
            scr[2 * per_part + cc] = v_ref[:, cc * LANE:(cc + 1) * LANE].astype(F32)
        for o_ref, d in zip(outs, dilations):
            for r in range(d):
                for ch in range(width // LANE):
                    o_ref[:, r * width + ch * LANE:r * width + (ch + 1) * LANE] = (
                        scr.at[ch][pl.ds(r, tm // d, stride=d), :].astype(o_ref.dtype))

    tab = pl.BlockSpec((tm, LANE), lambda i: (i, 0))
    col = lambda b: pl.BlockSpec((tm, ATT_W), lambda i: (i, b))
    return pl.pallas_call(
        body, name="rope_fwd", grid=(SEQ // tm,),
        in_specs=[col(3), col(4), col(5), tab, tab, tab],
        out_specs=[pl.BlockSpec((tm // d, d * width), lambda i: (i, 0)) for d in dilations],
        out_shape=[_sds((SEQ // d, d * width), BF16) for d in dilations],
        scratch_shapes=[pltpu.VMEM((width // LANE, tm, LANE), F32)],
        compiler_params=_params("parallel"),
    )(zm, zm, zm, *tabs)


def _dil_grad_combine(dqs, dks, dvs, tabs, dz, *, tm=256):
    dilations = [d for _, d in DIL_PATTERNS]
    chunks = ATT_W // LANE

    def body(*refs):
        groups = (refs[0:3], refs[3:6], refs[6:9])
        c_ref, a_ref, b_ref, _, o_ref, scr = refs[9:]

        def total(part, cc):
            acc = None
            for g, (ref, d) in enumerate(zip(groups[part], dilations)):
                term = ref[:, cc * LANE:(cc + 1) * LANE].astype(F32) if d == 1 else scr[part, g, cc]
                acc = term if acc is None else acc + term
            return acc

        for part in range(3):
            for g, (ref, d) in enumerate(zip(groups[part], dilations)):
                if d > 1:
                    _rows_from_slabs(ref, scr.at[part, g], d)
        for cc in range(chunks):
            for part in range(2):
                o_ref[:, part * ATT_W + cc * LANE:part * ATT_W + (cc + 1) * LANE] = _rotate(
                    total(part, cc), c_ref[...], -a_ref[...], -b_ref[...]).astype(o_ref.dtype)
            o_ref[:, 2 * ATT_W + cc * LANE:2 * ATT_W + (cc + 1) * LANE] = total(2, cc).astype(o_ref.dtype)

    view = lambda d: pl.BlockSpec((tm // d, d * ATT_W), lambda i: (i, 0))
    tab = pl.BlockSpec((tm, LANE), lambda i: (i, 0))
    return pl.pallas_call(
        body, name="dil_grad_combine", grid=(SEQ // tm,),
        in_specs=[view(d) for d in dilations] * 3 + [tab] * 3 + [ANY],
        out_specs=pl.BlockSpec((tm, 3 * ATT_W), lambda i: (i, 1)),
        out_shape=_sds((SEQ, Z_MAIN), BF16),
        input_output_aliases={12: 0},
        scratch_shapes=[pltpu.VMEM((3, len(dilations), chunks, tm, LANE), F32)],
        compiler_params=_params("parallel"),
    )(*dqs, *dks, *dvs, *tabs, dz)


def _dil_valid(n):
    qi = lax.broadcasted_iota(jnp.int32, (DIL_BLK, 2 * DIL_BLK), 0)
    ki = lax.broadcasted_iota(jnp.int32, (DIL_BLK, 2 * DIL_BLK), 1)
    dist = qi + DIL_BLK - ki
    return (dist >= 0) & (dist <= DIL_BLK) & ((n > 0) | (ki >= DIL_BLK))


def _dil_fwd(qkv_v, d):
    length = SEQ // d
    nb = length // DIL_BLK
    nsub = min(DIL_STEP_BLOCKS, nb)

    def body(q_ref, kp_ref, kc_ref, vp_ref, vc_ref, o_ref, lse_ref):
        m_step = pl.program_id(1)
        lane = lax.broadcasted_iota(jnp.int32, (DIL_BLK, LANE), 1)
        jobs = [(sub, h) for sub in range(nsub) for h in range(N_HEADS)]
        rows = lambda sub: slice(sub * DIL_BLK, (sub + 1) * DIL_BLK)
        cols = lambda h: slice(h * HEAD_DIM, (h + 1) * HEAD_DIM)

        def keys(prev_ref, cur_ref, sub, h):
            before = prev_ref[:, cols(h)] if sub == 0 else cur_ref[rows(sub - 1), cols(h)]
            return jnp.concatenate([before, cur_ref[rows(sub), cols(h)]], axis=0)

        scores = [lax.dot_general(q_ref[rows(sub), cols(h)], keys(kp_ref, kc_ref, sub, h), NT_DIMS,
                                  preferred_element_type=F32) for sub, h in jobs]
        ok = [_dil_valid(m_step)] + [_dil_valid(1)] * (nsub - 1)
        probs, inv_l, lse_all = [], [], [jnp.zeros((DIL_BLK, LANE), F32)] * nsub
        for idx, (sub, h) in enumerate(jobs):
            s = jnp.where(ok[sub], scores[idx], NEG_INF)
            m = jnp.max(s, axis=-1, keepdims=True)
            p = jnp.exp(s - m)
            l = jnp.sum(p, axis=-1, keepdims=True)
            probs.append(p.astype(BF16))
            inv_l.append(1.0 / l)
            lse_all[sub] = jnp.where(lane == h, m + jnp.log(l), lse_all[sub])
        outs = [jnp.dot(probs[idx], keys(vp_ref, vc_ref, sub, h), preferred_element_type=F32) * inv_l[idx]
                for idx, (sub, h) in enumerate(jobs)]
        for sub in range(nsub):
            o_ref[rows(sub), :] = jnp.concatenate(outs[sub * N_HEADS:(sub + 1) * N_HEADS], axis=1).astype(o_ref.dtype)
            lse_ref[rows(sub), :] = lse_all[sub]

    pair = lambda f: pl.BlockSpec((nsub * DIL_BLK, ATT_W), f)
    one = lambda f: pl.BlockSpec((DIL_BLK, ATT_W), f)
    before = lambda m: jnp.maximum(nsub * m - 1, 0)
    o, lse = pl.pallas_call(
        body, name=f"dil_fwd_d{d}", grid=(d, nb // nsub),
        in_specs=[pair(lambda r, m: (m, 3 * r)),
                  one(lambda r, m: (before(m), 3 * r + 1)), pair(lambda r, m: (m, 3 * r + 1)),
                  one(lambda r, m: (before(m), 3 * r + 2)), pair(lambda r, m: (m, 3 * r + 2))],
        out_specs=[pair(lambda r, m: (m, r)), pl.BlockSpec((nsub * DIL_BLK, LANE), lambda r, m: (m, r))],
        out_shape=[_sds((length, d * ATT_W), BF16), _sds((length, d * LANE), F32)],
        compiler_params=_params("parallel", "arbitrary"),
    )(qkv_v, qkv_v, qkv_v, qkv_v, qkv_v)
    return o, lse


def _rows_from_slabs(view_ref, scr, d):
    chunks, rows = scr.shape[0], scr.shape[1]
    for r in range(d):
        for ch in range(chunks):
            lo = (r * chunks + ch) * LANE
            scr.at[ch][pl.ds(r, rows // d, stride=d), :] = view_ref[:, lo:lo + LANE].astype(F32)


def _slabs_from_rows(scr, view_ref, d):
    chunks, rows = scr.shape[0], scr.shape[1]
    for r in range(d):
        for ch in range(chunks):
            lo = (r * chunks + ch) * LANE
            view_ref[:, lo:lo + LANE] = scr.at[ch][pl.ds(r, rows // d, stride=d), :].astype(view_ref.dtype)


def _dil_merge(os_, lses, *, tm=512):
    dilations = [d for _, d in DIL_PATTERNS]
    o_chunks = ATT_W // LANE

    def body(o0, o1, o2, l0, l1, l2, y_ref, lse_ref, o_scr, l_scr):
        os_nat, ls = [], []
        for g, (o_ref, l_ref, d) in enumerate(zip((o0, o1, o2), (l0, l1, l2), dilations)):
            if d == 1:
                os_nat.append(o_ref[...].astype(F32))
                ls.append(l_ref[...])
            else:
                _rows_from_slabs(o_ref, o_scr.at[g], d)
                _rows_from_slabs(l_ref, l_scr.at[g], d)
                os_nat.append(jnp.concatenate([o_scr[g, ch] for ch in range(o_chunks)], axis=1))
                ls.append(l_scr[g, 0])
        m = jnp.maximum(jnp.maximum(ls[0], ls[1]), ls[2])
        es = [jnp.exp(l - m) for l in ls]
        tot = es[0] + es[1] + es[2]
        lse_ref[...] = m + jnp.log(tot)
        alphas = [e / tot for e in es]
        outs = []
        for h in range(N_HEADS):
            acc = None
            for g in range(3):
                term = alphas[g][:, h:h + 1] * _head(os_nat[g], h)
                acc = term if acc is None else acc + term
            outs.append(acc)
        y_ref[...] = jnp.concatenate(outs, axis=1).astype(y_ref.dtype)

    row = pl.BlockSpec((tm, ATT_W), lambda i: (i, 0))
    vec = pl.BlockSpec((tm, LANE), lambda i: (i, 0))
    view = lambda d, w: pl.BlockSpec((tm // d, d * w), lambda i: (i, 0))
    return pl.pallas_call(
        body, name="dil_merge", grid=(SEQ // tm,),
        in_specs=[view(d, ATT_W) for d in dilations] + [view(d, LANE) for d in dilations], out_specs=[row, vec],
        out_shape=[_sds((SEQ, ATT_W), BF16), _sds((SEQ, LANE), F32)],
        scratch_shapes=[pltpu.VMEM((3, o_chunks, tm, LANE), F32), pltpu.VMEM((3, 1, tm, LANE), F32)],
        compiler_params=_params("parallel"),
    )(*os_, *lses)


def _dil_bwd(qkv_v, do_v, lse_v, dl_v, d):
    length = SEQ // d
    nb = length // DIL_BLK
    nsub = min(DIL_STEP_BLOCKS, nb)
    n_steps = nb // nsub

    def body(q_ref, kp_ref, kc_ref, vp_ref, vc_ref, lse_ref, dl_ref, do_ref, dq_ref, dk_ref, dv_ref, dk_s, dv_s):
        m_step = pl.program_id(1)

        @pl.when(m_step == 0)
        def _():
            dk_s[...] = jnp.zeros_like(dk_s)
            dv_s[...] = jnp.zeros_like(dv_s)

        jobs = [(sub, h) for sub in range(nsub) for h in range(N_HEADS)]
        rows = lambda sub: slice(sub * DIL_BLK, (sub + 1) * DIL_BLK)
        cols = lambda h: slice(h * HEAD_DIM, (h + 1) * HEAD_DIM)

        def keys(prev_ref, cur_ref, sub, h):
            before = prev_ref[:, cols(h)] if sub == 0 else cur_ref[rows(sub - 1), cols(h)]
            return jnp.concatenate([before, cur_ref[rows(sub), cols(h)]], axis=0)

        kks = [keys(kp_ref, kc_ref, sub, h) for sub, h in jobs]
        scores = [lax.dot_general(q_ref[rows(sub), cols(h)], kks[idx], NT_DIMS, preferred_element_type=F32)
                  for idx, (sub, h) in enumerate(jobs)]
        dps = [lax.dot_general(do_ref[rows(sub), cols(h)], keys(vp_ref, vc_ref, sub, h), NT_DIMS,
                               preferred_element_type=F32) for sub, h in jobs]
        ok = [_dil_valid(m_step)] + [_dil_valid(1)] * (nsub - 1)
        ps, dss = [], []
        for idx, (sub, h) in enumerate(jobs):
            p = jnp.where(ok[sub], jnp.exp(scores[idx] - lse_ref[rows(sub), h:h + 1]), 0.0)
            ps.append(p.astype(BF16))
            dss.append((p * (dps[idx] - dl_ref[rows(sub), h:h + 1])).astype(BF16))
        dqs = [jnp.dot(dss[idx], kks[idx], preferred_element_type=F32) * SCALE for idx in range(len(jobs))]
        dkks = [lax.dot_general(dss[idx], q_ref[rows(sub), cols(h)], TN_DIMS, preferred_element_type=F32)
                for idx, (sub, h) in enumerate(jobs)]
        dvvs = [lax.dot_general(ps[idx], do_ref[rows(sub), cols(h)], TN_DIMS, preferred_element_type=F32)
                for idx, (sub, h) in enumerate(jobs)]
        for sub in range(nsub):
            dq_ref[rows(sub), :] = jnp.concatenate(dqs[sub * N_HEADS:(sub + 1) * N_HEADS], axis=1).astype(dq_ref.dtype)
        base = m_step * (nsub * DIL_BLK)
        blocks = [pl.ds(pl.multiple_of(jnp.maximum(base - DIL_BLK, 0), DIL_BLK), DIL_BLK)]
        blocks += [pl.ds(pl.multiple_of(base + s * DIL_BLK, DIL_BLK), DIL_BLK) for s in range(nsub)]
        for acc, parts in ((dk_s, dkks), (dv_s, dvvs)):
            top = lambda sub: jnp.concatenate([parts[sub * N_HEADS + h][:DIL_BLK] for h in range(N_HEADS)], axis=1)
            bottom = lambda sub: jnp.concatenate([parts[sub * N_HEADS + h][DIL_BLK:] for h in range(N_HEADS)], axis=1)
            acc[blocks[0], :] += top(0)
            for s in range(nsub):
                acc[blocks[s + 1], :] += bottom(s) + top(s + 1) if s + 1 < nsub else bottom(s)

        @pl.when(m_step == n_steps - 1)
        def _():
            dk_ref[...] = dk_s[...].astype(dk_ref.dtype)
            dv_ref[...] = dv_s[...].astype(dv_ref.dtype)

    pair = lambda f: pl.BlockSpec((nsub * DIL_BLK, ATT_W), f)
    one = lambda f: pl.BlockSpec((DIL_BLK, ATT_W), f)
    vec = lambda f: pl.BlockSpec((nsub * DIL_BLK, LANE), f)
    whole = pl.BlockSpec((length, ATT_W), lambda r, m: (0, r))
    before = lambda m: jnp.maximum(nsub * m - 1, 0)
    outs = pl.pallas_call(
        body, name=f"dil_bwd_d{d}", grid=(d, n_steps),
        in_specs=[pair(lambda r, m: (m, 3 * r)),
                  one(lambda r, m: (before(m), 3 * r + 1)), pair(lambda r, m: (m, 3 * r + 1)),
                  one(lambda r, m: (before(m), 3 * r + 2)), pair(lambda r, m: (m, 3 * r + 2)),
                  vec(lambda r, m: (m, r)), vec(lambda r, m: (m, r)), pair(lambda r, m: (m, r))],
        out_specs=[pair(lambda r, m: (m, r)), whole, whole],
        out_shape=[_sds((length, d * ATT_W), BF16)] * 3,
        scratch_shapes=[pltpu.VMEM((length, ATT_W), F32), pltpu.VMEM((length, ATT_W), F32)],
        compiler_params=_params("arbitrary", "arbitrary"),
    )(qkv_v, qkv_v, qkv_v, qkv_v, qkv_v, lse_v, dl_v, do_v)
    return outs


def _sigmoid(x):
    return 1.0 / (1.0 + jnp.exp(-x))


def _mix_fwd(ya, yb, w_oa, w_ob, zm, *, tm=512):
    def body(ya_ref, yb_ref, wa_ref, wb_ref, ga_ref, gb_ref, pa_ref, pb_ref, mix_ref):
        pa = jnp.dot(ya_ref[...], wa_ref[...], preferred_element_type=F32)
        pb = jnp.dot(yb_ref[...], wb_ref[...], preferred_element_type=F32)
        pa_ref[...] = pa.astype(pa_ref.dtype)
        pb_ref[...] = pb.astype(pb_ref.dtype)
        mix_ref[...] = (_sigmoid(ga_ref[...].astype(F32)) * pa + _sigmoid(gb_ref[...].astype(F32)) * pb
                        ).astype(mix_ref.dtype)

    row = pl.BlockSpec((tm, ATT_W), lambda i: (i, 0))
    wsp = pl.BlockSpec((ATT_W, D_MODEL), lambda i: (0, 0))
    wide = pl.BlockSpec((tm, D_MODEL), lambda i: (i, 0))
    return pl.pallas_call(
        body, name="mix_fwd", grid=(SEQ // tm,),
        in_specs=[row, row, wsp, wsp, pl.BlockSpec((tm, D_MODEL), lambda i: (i, 3)),
                  pl.BlockSpec((tm, D_MODEL), lambda i: (i, 4))],
        out_specs=[wide] * 3, out_shape=[_sds((SEQ, D_MODEL), BF16)] * 3,
        compiler_params=_params("parallel"),
    )(ya, yb, w_oa, w_ob, zm, zm)


def _gate_bwd(dmix, zm, p, gate_block, dz, *, name, tm=512):
    def body(dm_ref, g_ref, p_ref, *rest):
        dp_ref, dz_ref = rest[-2], rest[-1]
        dm = dm_ref[...].astype(F32)
        s = _sigmoid(g_ref[...].astype(F32))
        dp_ref[...] = (dm * s).astype(dp_ref.dtype)
        dz_ref[...] = (dm * p_ref[...].astype(F32) * s * (1.0 - s)).astype(dz_ref.dtype)

    wide = pl.BlockSpec((tm, D_MODEL), lambda i: (i, 0))
    gate = pl.BlockSpec((tm, D_MODEL), lambda i: (i, gate_block))
    extra = [] if dz is None else [dz]
    return pl.pallas_call(
        body, name=name, grid=(SEQ // tm,),
        in_specs=[wide, gate, wide] + [ANY] * len(extra),
        out_specs=[wide, gate],
        out_shape=[_sds((SEQ, D_MODEL), BF16), _sds((SEQ, Z_MAIN), BF16)],
        input_output_aliases={3: 1} if extra else {},
        compiler_params=_params("parallel"),
    )(dmix, zm, p, *extra)


def _out_fwd(mixed, w_out, x, g_post, g_pre, *, tm=512):
    def body(m_ref, w_ref, x_ref, gp_ref, gn_ref, y_ref, x2_ref, h_ref):
        y = jnp.dot(m_ref[...], w_ref[...], preferred_element_type=F32)
        y_ref[...] = y
        r = lax.rsqrt(jnp.mean(y * y, axis=-1, keepdims=True) + RMS_EPS)
        x2 = x_ref[...] + y * r * gp_ref[...]
        x2_ref[...] = x2
        r2 = lax.rsqrt(jnp.mean(x2 * x2, axis=-1, keepdims=True) + RMS_EPS)
        h_ref[...] = (x2 * r2 * gn_ref[...]).astype(h_ref.dtype)

    row = pl.BlockSpec((tm, D_MODEL), lambda i: (i, 0))
    vec = pl.BlockSpec((1, D_MODEL), lambda i: (0, 0))
    return pl.pallas_call(
        body, name="out_fwd", grid=(SEQ // tm,),
        in_specs=[row, pl.BlockSpec((D_MODEL, D_MODEL), lambda i: (0, 0)), row, vec, vec],
        out_specs=[row] * 3,
        out_shape=[_sds((SEQ, D_MODEL), F32), _sds((SEQ, D_MODEL), F32), _sds((SEQ, D_MODEL), BF16)],
        compiler_params=_params("parallel"),
    )(mixed, w_out, x, g_post, g_pre)


FFN_HALF = 256
FFN_TN = 2 * FFN_HALF
FFN_NJ = D_FF // FFN_HALF
FFN_GROUP = 2 * SUBLANE
UP_TM = 1024


def _ffn_interleave(t):
    lead = t.shape[:-1]
    return jnp.swapaxes(t.reshape(*lead, 2, FFN_NJ, FFN_HALF), -3, -2).reshape(*lead, 2 * D_FF)


def _ffn_deinterleave(t):
    lead = t.shape[:-1]
    return jnp.swapaxes(t.reshape(*lead, FFN_NJ, 2, FFN_HALF), -3, -2).reshape(*lead, 2 * D_FF)


W_IN_SHARD = (Z_MAIN + N_HEADS) // N_DEV
FORGET_LO = 3 * ATT_W


def _w_in_from_shards(shards, *, tm=256):
    def columns(g_ref, lo, width):
        p, off = divmod(lo, W_IN_SHARD)
        if off + width <= W_IN_SHARD:
            return g_ref[p, :, off:off + width]
        first = W_IN_SHARD - off
        return jnp.concatenate([g_ref[p, :, off:], g_ref[p + 1, :, :width - first]], axis=1)

    def body(g_ref, main_ref, f_ref):
        for t in range(Z_MAIN // LANE):
            lo = t * LANE
            main_ref[:, lo:lo + LANE] = columns(g_ref, lo if lo < FORGET_LO else lo + N_HEADS, LANE)
        f_ref[...] = jnp.concatenate([columns(g_ref, FORGET_LO, N_HEADS),
                                      jnp.zeros((tm, F_PAD - N_HEADS), f_ref.dtype)], axis=1)

    return pl.pallas_call(
        body, name="w_in_from_shards", grid=(D_MODEL // tm,),
        in_specs=[pl.BlockSpec((N_DEV, tm, W_IN_SHARD), lambda i: (0, i, 0))],
        out_specs=[pl.BlockSpec((tm, Z_MAIN), lambda i: (i, 0)), pl.BlockSpec((tm, F_PAD), lambda i: (i, 0))],
        out_shape=[_sds((D_MODEL, Z_MAIN), shards.dtype), _sds((D_MODEL, F_PAD), shards.dtype)],
        compiler_params=_params("parallel"),
    )(shards)


def _w_in_to_shards(g_main, g_f, *, tm=256):
    def natural(main_ref, f_ref, lo, width):
        pieces, hi = [], lo + width
        for ref, start, stop, shift in ((main_ref, 0, FORGET_LO, 0), (f_ref, FORGET_LO, FORGET_LO + N_HEADS, FORGET_LO),
                                        (main_ref, FORGET_LO + N_HEADS, Z_MAIN + N_HEADS, N_HEADS)):
            a, b = max(lo, start), min(hi, stop)
            if a < b:
                pieces.append(ref[:, a - shift:b - shift])
        return pieces[0] if len(pieces) == 1 else jnp.concatenate(pieces, axis=1)

    def body(main_ref, f_ref, o_ref):
        for p in range(N_DEV):
            for q in range(-(-W_IN_SHARD // LANE)):
                width = min(LANE, W_IN_SHARD - q * LANE)
                o_ref[p, :, q * LANE:q * LANE + width] = natural(main_ref, f_ref, p * W_IN_SHARD + q * LANE, width)

    return pl.pallas_call(
        body, name="w_in_to_shards", grid=(D_MODEL // tm,),
        in_specs=[pl.BlockSpec((tm, Z_MAIN), lambda i: (i, 0)), pl.BlockSpec((tm, F_PAD), lambda i: (i, 0))],
        out_specs=pl.BlockSpec((N_DEV, tm, W_IN_SHARD), lambda i: (0, i, 0)),
        out_shape=_sds((N_DEV, D_MODEL, W_IN_SHARD), g_main.dtype),
        compiler_params=_params("parallel"),
    )(g_main, g_f)


W_UP_SHARD = 2 * D_FF // N_DEV


def _w_up_lane_tile(k):
    block = k // 2
    return (2 * (block % FFN_NJ) + block // FFN_NJ) * FFN_HALF + (k % 2) * LANE


def _w_up_from_shards(shards, *, tm=256):
    def body(g_ref, o_ref):
        for k in range(2 * D_FF // LANE):
            p, off = divmod(k * LANE, W_UP_SHARD)
            if off + LANE <= W_UP_SHARD:
                tile = g_ref[p, :, off:off + LANE]
            else:
                tile = jnp.concatenate([g_ref[p, :, off:], g_ref[p + 1, :, :off + LANE - W_UP_SHARD]], axis=1)
            dst = _w_up_lane_tile(k)
            o_ref[:, dst:dst + LANE] = tile

    return pl.pallas_call(
        body, name="w_up_from_shards", grid=(D_MODEL // tm,),
        in_specs=[pl.BlockSpec((N_DEV, tm, W_UP_SHARD), lambda i: (0, i, 0))],
        out_specs=pl.BlockSpec((tm, 2 * D_FF), lambda i: (i, 0)),
        out_shape=_sds((D_MODEL, 2 * D_FF), shards.dtype),
        compiler_params=_params("parallel"),
    )(shards)


def _w_up_to_shards(t, *, tm=256):
    def body(x_ref, o_ref):
        for p in range(N_DEV):
            for q in range(-(-W_UP_SHARD // LANE)):
                width = min(LANE, W_UP_SHARD - q * LANE)
                k, off = divmod(p * W_UP_SHARD + q * LANE, LANE)
                src = _w_up_lane_tile(k)
                if off == 0:
                    tile = x_ref[:, src:src + width]
                else:
                    tile = x_ref[:, src + off:src + LANE]
                    if width > LANE - off:
                        nxt = _w_up_lane_tile(k + 1)
                        tile = jnp.concatenate([tile, x_ref[:, nxt:nxt + width - (LANE - off)]], axis=1)
                o_ref[p, :, q * LANE:q * LANE + width] = tile

    return pl.pallas_call(
        body, name="w_up_to_shards", grid=(D_MODEL // tm,),
        in_specs=[pl.BlockSpec((tm, 2 * D_FF), lambda i: (i, 0))],
        out_specs=pl.BlockSpec((N_DEV, tm, W_UP_SHARD), lambda i: (0, i, 0)),
        out_shape=_sds((N_DEV, D_MODEL, W_UP_SHARD), t.dtype),
        compiler_params=_params("parallel"),
    )(t)


def _gelu_parts(a):
    c = math.sqrt(2.0 / math.pi)
    a2 = a * a
    t = jnp.tanh((c * a) * (1.0 + 0.044715 * a2))
    half_a, one_t = 0.5 * a, 1.0 + t
    gelu = half_a * one_t
    dgelu = 0.5 * one_t + half_a * (1.0 - t * t) * (c + (3.0 * 0.044715 * c) * a2)
    return gelu, dgelu


def _row_masks(down):
    row = lax.broadcasted_iota(jnp.int32, (SUBLANE, FFN_TN), 0)
    return (row < 1, row < 2) if down else (row >= SUBLANE - 1, row >= SUBLANE - 2)


def _rolled(x, down):
    return (pltpu.roll(x, 1, 0), pltpu.roll(x, 2, 0)) if down else (
        pltpu.roll(x, SUBLANE - 1, 0), pltpu.roll(x, SUBLANE - 2, 0))


def _shifted(cur_rolled, neighbour_rolled, masks):
    return (jnp.where(masks[0], neighbour_rolled[0], cur_rolled[0]),
            jnp.where(masks[1], neighbour_rolled[1], cur_rolled[1]))


def _conv_consts(w_ref, b_ref):
    shape = (SUBLANE, FFN_TN)
    return [jnp.broadcast_to(w_ref[k:k + 1, :], shape) for k in range(3)] + [jnp.broadcast_to(b_ref[...], shape)]


def _up_conv_fwd(h2, w_up, conv_w, conv_b):
    nrow = SEQ // UP_TM
    n_tiles = FFN_NJ * nrow
    n_groups = UP_TM // FFN_GROUP

    def body(h_ref, wu_ref, w_ref, b_ref, u_ref, ab_ref, m_ref, ua_s, ub_s, c1_s, c2_s):
        k = pl.program_id(0)

        @pl.when(k == 0)
        def _():
            ub_s[...] = jnp.zeros_like(ub_s)

        @pl.when(jnp.maximum(k - 1, 0) % nrow == 0)
        def _():
            c1_s[...] = jnp.zeros_like(c1_s)
            c2_s[...] = jnp.zeros_like(c2_s)

        def step(write_s, read_s):
            h_rows = pl.ds(pl.multiple_of((this(k) % nrow) * UP_TM, UP_TM), UP_TM)
            u = jnp.dot(h_ref[h_rows, :], wu_ref[...], preferred_element_type=F32)
            write_s[...] = u
            u_ref[...] = u.astype(u_ref.dtype)
            w0, w1, w2, bias = _conv_consts(w_ref, b_ref)
            masks = _row_masks(True)
            above = (c1_s[...], c2_s[...])
            for g in range(n_groups):
                rows = slice(g * FFN_GROUP, (g + 1) * FFN_GROUP)
                x = read_s[rows, :]
                convs = []
                for c in range(2):
                    cur = x[c * SUBLANE:(c + 1) * SUBLANE]
                    cur_rolled = _rolled(cur, True)
                    s1, s2 = _shifted(cur_rolled, above, masks)
                    convs.append(w0 * s2 + w1 * s1 + w2 * cur + bias)
                    above = cur_rolled
                y = jnp.concatenate(convs, axis=0)
                ab_ref[rows, :] = y.astype(ab_ref.dtype)
                m_ref[rows, :] = (_gelu_parts(y[:, :FFN_HALF])[0] * y[:, FFN_HALF:]).astype(m_ref.dtype)
            c1_s[...], c2_s[...] = above

        @pl.when(k % 2 == 0)
        def _():
            step(ua_s, ub_s)

        @pl.when(k % 2 == 1)
        def _():
            step(ub_s, ua_s)

    this = lambda k: jnp.minimum(k, n_tiles - 1)
    last = lambda k: jnp.maximum(k - 1, 0)
    blk = lambda tile: pl.BlockSpec((UP_TM, FFN_TN), lambda k: (tile(k) % nrow, tile(k) // nrow))
    return pl.pallas_call(
        body, name="up_conv_fwd", grid=(n_tiles + 1,),
        in_specs=[pl.BlockSpec((SEQ, D_MODEL), lambda k: (0, 0)),
                  pl.BlockSpec((D_MODEL, FFN_TN), lambda k: (0, this(k) // nrow)),
                  pl.BlockSpec((3, FFN_TN), lambda k: (0, last(k) // nrow)),
                  pl.BlockSpec((1, FFN_TN), lambda k: (0, last(k) // nrow))],
        out_specs=[blk(this), blk(last), pl.BlockSpec((UP_TM, FFN_HALF), lambda k: (last(k) % nrow, last(k) // nrow))],
        out_shape=[_sds((SEQ, 2 * D_FF), BF16), _sds((SEQ, 2 * D_FF), BF16), _sds((SEQ, D_FF), BF16)],
        scratch_shapes=[pltpu.VMEM((UP_TM, FFN_TN), F32), pltpu.VMEM((UP_TM, FFN_TN), F32),
                        pltpu.VMEM((SUBLANE, FFN_TN), F32), pltpu.VMEM((SUBLANE, FFN_TN), F32)],
        compiler_params=_params("arbitrary"),
    )(h2, w_up, conv_w, conv_b)


def _ffn_mid_bwd(dy2, w_down, u, ab, conv_w):
    nrow = SEQ // UP_TM
    n_tiles = FFN_NJ * nrow
    n_groups = UP_TM // FFN_GROUP
    this = lambda k: jnp.minimum(k, n_tiles - 1)
    last = lambda k: jnp.maximum(k - 1, 0)
    row_of = lambda tile: nrow - 1 - tile % nrow

    def body(dy_ref, wd_ref, u_ref, ab_ref, w_ref, du_ref, gw_ref, gb_ref, c_s, dma_s, dmb_s):
        k = pl.program_id(0)

        @pl.when(k == 0)
        def _():
            dmb_s[...] = jnp.zeros_like(dmb_s)

        @pl.when(last(k) % nrow == 0)
        def _():
            c_s[...] = jnp.zeros_like(c_s)
            gw_ref[...] = jnp.zeros_like(gw_ref)
            gb_ref[...] = jnp.zeros_like(gb_ref)

        def step(write_s, read_s):
            dy_rows = pl.ds(pl.multiple_of(row_of(this(k)) * UP_TM, UP_TM), UP_TM)
            write_s[...] = lax.dot_general(dy_ref[dy_rows, :], wd_ref[...], NT_DIMS,
                                           preferred_element_type=F32)
            taps = [jnp.broadcast_to(w_ref[t:t + 1, :], (SUBLANE, FFN_TN)) for t in range(3)]
            masks = _row_masks(False)
            below = _rolled(c_s[...], False)
            acc = [jnp.zeros((SUBLANE, FFN_TN), F32)] * 4
            for g in reversed(range(n_groups)):
                rows = slice(g * FFN_GROUP, (g + 1) * FFN_GROUP)
                x, y, dmv = u_ref[rows, :].astype(F32), ab_ref[rows, :].astype(F32), read_s[rows, :]
                gelu, dgelu = _gelu_parts(y[:, :FFN_HALF])
                d = jnp.concatenate([dmv * y[:, FFN_HALF:] * dgelu, dmv * gelu], axis=1)
                pre = [None, None]
                for c in (1, 0):
                    sl = slice(c * SUBLANE, (c + 1) * SUBLANE)
                    cur, xs = d[sl], x[sl]
                    cur_rolled = _rolled(cur, False)
                    up1, up2 = _shifted(cur_rolled, below, masks)
                    acc = [acc[0] + up2 * xs, acc[1] + up1 * xs, acc[2] + cur * xs, acc[3] + cur]
                    pre[c] = taps[2] * cur + taps[1] * up1 + taps[0] * up2
                    below = cur_rolled
                du_ref[rows, :] = jnp.concatenate(pre, axis=0).astype(du_ref.dtype)
            c_s[...] = pltpu.roll(below[0], 1, 0)
            for t in range(3):
                gw_ref[t:t + 1, :] += jnp.sum(acc[t], axis=0, keepdims=True)
            gb_ref[...] += jnp.sum(acc[3], axis=0, keepdims=True)

        @pl.when(k % 2 == 0)
        def _():
            step(dma_s, dmb_s)

        @pl.when(k % 2 == 1)
        def _():
            step(dmb_s, dma_s)

    blk = pl.BlockSpec((UP_TM, FFN_TN), lambda k: (row_of(last(k)), last(k) // nrow))
    col = lambda rows: pl.BlockSpec((rows, FFN_TN), lambda k: (0, last(k) // nrow))
    return pl.pallas_call(
        body, name="ffn_mid_bwd", grid=(n_tiles + 1,),
        in_specs=[pl.BlockSpec((SEQ, D_MODEL), lambda k: (0, 0)),
                  pl.BlockSpec((FFN_HALF, D_MODEL), lambda k: (this(k) // nrow, 0)), blk, blk, col(3)],
        out_specs=[blk, col(3), col(1)],
        out_shape=[_sds((SEQ, 2 * D_FF), BF16), _sds((3, 2 * D_FF), F32), _sds((1, 2 * D_FF), F32)],
        scratch_shapes=[pltpu.VMEM((SUBLANE, FFN_TN), F32), pltpu.VMEM((UP_TM, FFN_HALF), F32),
                        pltpu.VMEM((UP_TM, FFN_HALF), F32)],
        compiler_params=_params("arbitrary"),
    )(dy2, w_down, u, ab, conv_w)


def _down_fwd(m, w_down, x2, g_post, target, *, tm=512):
    def body(m_ref, w_ref, x2_ref, g_ref, t_ref, dout_ref, dy_ref, gg_ref, loss_ref):
        @pl.when(pl.program_id(0) == 0)
        def _():
            gg_ref[...] = jnp.zeros_like(gg_ref)
            loss_ref[...] = jnp.zeros_like(loss_ref)

        y = jnp.dot(m_ref[...], w_ref[...], preferred_element_type=F32)
        r = lax.rsqrt(jnp.mean(y * y, axis=-1, keepdims=True) + RMS_EPS)
        yn = y * r
        diff = (x2_ref[...] + yn * g_ref[...]) - t_ref[...]
        loss_ref[...] += jnp.sum(diff * diff)
        dout = diff * (1.0 / D_MODEL)
        dout_ref[...] = dout
        gg_ref[...] += jnp.sum(dout * yn, axis=0, keepdims=True)
        dn = dout * g_ref[...]
        dy_ref[...] = (r * (dn - yn * jnp.mean(dn * yn, axis=-1, keepdims=True))).astype(dy_ref.dtype)

    row = pl.BlockSpec((tm, D_MODEL), lambda i: (i, 0))
    vec = pl.BlockSpec((1, D_MODEL), lambda i: (0, 0))
    return pl.pallas_call(
        body, name="down_fwd", grid=(SEQ // tm,),
        in_specs=[pl.BlockSpec((tm, D_FF), lambda i: (i, 0)), pl.BlockSpec((D_FF, D_MODEL), lambda i: (0, 0)),
                  row, vec, row],
        out_specs=[row, row, vec, pl.BlockSpec((1, LANE), lambda i: (0, 0))],
        out_shape=[_sds((SEQ, D_MODEL), F32), _sds((SEQ, D_MODEL), BF16), _sds((1, D_MODEL), F32),
                   _sds((1, LANE), F32)],
        compiler_params=_params("arbitrary"),
    )(m, w_down, x2, g_post, target)


def _local_step(x, target, w_main, w_f, b_forget, conv_b, g_pre_mix, g_post_mix, g_pre_ffn, g_post_ffn,
                proj_weights, ffn_weights, ffn_grads_ready, proj_grads_ready, mixer_grads_ready, after=None):
    mm = _matmul
    tabs = _rope_tables()

    h1 = _rms_fwd(x, g_pre_mix, name="rms_pre_mix", after=after)
    zm = mm(h1, w_main, out_dtype=BF16, tm=2048, tn=1024, tk=1024, name="in_proj")
    zf = mm(h1, w_f, out_dtype=F32, tm=2048, tn=F_PAD, tk=1024, name="in_proj_forget")
    f_row, sg_row = _fox_prep(zf[:, :N_HEADS].T, b_forget.reshape(N_HEADS, 1))
    f_cols = jnp.pad(f_row.T, ((0, 0), (0, LANE - N_HEADS)))
    q_slots, k_slots, v_slots = _fox_pack_fwd(zm, f_cols)
    ya, lse_a = _fox_fwd(q_slots, k_slots, v_slots)
    qkv_d = dict(zip([d for _, d in DIL_PATTERNS], _rope_fwd(zm, tabs)))
    dil = [_dil_fwd(qkv_d[d], d) for _, d in DIL_PATTERNS]
    yb, lse_b = _dil_merge([o for o, _ in dil], [l for _, l in dil])
    w_oa, w_ob, w_out = proj_weights(yb)
    pa, pb, mixed = _mix_fwd(ya, yb, w_oa, w_ob, zm)
    y1, x2, h2 = _out_fwd(mixed, w_out, x, g_post_mix, g_pre_ffn)
    w_up, conv_w, w_down = ffn_weights(h2)
    u, ab, m = _up_conv_fwd(h2, w_up, conv_w, _ffn_interleave(conv_b))
    dout, dy2, gg_post_ffn, sq_err = _down_fwd(m, w_down, x2, g_post_ffn, target)

    g_w_down = mm(m, dy2, ta=True, out_dtype=BF16, tm=D_FF // 2, tn=1024, tk=SEQ, name="grad_w_down")
    du, g_conv_w, g_conv_b = _ffn_mid_bwd(dy2, w_down, u, ab, conv_w)
    g_w_up = mm(h2, du, ta=True, out_dtype=BF16, tm=1024, tn=D_FF // 2, tk=SEQ, name="grad_w_up")
    tok = ffn_grads_ready(dict(w_down=g_w_down, w_up_blocks=g_w_up, conv_w=_ffn_deinterleave(g_conv_w)))
    dh2 = mm(du, w_up, tb=True, out_dtype=BF16, tm=512, tn=1024, tk=2 * D_FF, name="d_h2")

    dx2, dy1, gg_pre_ffn, gg_post_mix = _rms_pair_bwd([dh2], x2, g_pre_ffn, dout, y1, g_post_mix, after=tok)
    g_w_out = mm(mixed, dy1, ta=True, out_dtype=BF16, tm=1024, tn=1024, tk=SEQ, name="grad_w_out")
    dmix = mm(dy1, w_out, tb=True, out_dtype=BF16, tm=2048, tn=1024, tk=1024, name="d_mixed")
    dpa, dz = _gate_bwd(dmix, zm, pa, 3, None, name="gate_bwd_fox")
    dpb, dz = _gate_bwd(dmix, zm, pb, 4, dz, name="gate_bwd_dil")
    g_w_oa = mm(ya, dpa, ta=True, out_dtype=BF16, tm=512, tn=1024, tk=SEQ, name="grad_w_o_fox", col_slots=N_DEV)
    g_w_ob = mm(yb, dpb, ta=True, out_dtype=BF16, tm=512, tn=1024, tk=SEQ, name="grad_w_o_dil", col_slots=N_DEV)
    tok = proj_grads_ready(dict(w_o_fox=g_w_oa, w_o_dil=g_w_ob, w_out=g_w_out))
    dya = mm(dpa, w_oa, tb=True, out_dtype=BF16, tm=2048, tn=512, tk=1024, name="d_y_fox")
    dyb = mm(dpb, w_ob, tb=True, out_dtype=BF16, tm=2048, tn=512, tk=1024, name="d_y_dil")

    qb_slots, do_slots = _fox_pack_bwd(zm, f_cols, lse_a, ya, dya, after=tok)
    dz, df_cols = _fox_unpack(*_fox_bwd(qb_slots, k_slots, v_slots, do_slots), dz)
    dfa_t, g_b_forget = _fox_post_bwd(df_cols[:, :N_HEADS].T, sg_row)

    rows_d = _dil_bwd_prep(yb, dyb, lse_b)
    dil_g = [_dil_bwd(qkv_d[d], *rows_d[k], d) for k, (_, d) in enumerate(DIL_PATTERNS)]
    dz = _dil_grad_combine([g[0] for g in dil_g], [g[1] for g in dil_g], [g[2] for g in dil_g], tabs, dz)

    dzf = jnp.pad(dfa_t.T, ((0, 0), (0, F_PAD - N_HEADS)))
    g_w_main = mm(h1, dz, ta=True, out_dtype=BF16, tm=1024, tn=Z_MAIN // 4, tk=SEQ, name="grad_w_in")
    g_w_f = mm(h1, dzf, ta=True, out_dtype=BF16, tm=1024, tn=F_PAD, tk=1024, name="grad_w_in_forget")
    tok = mixer_grads_ready(dict(w_main=g_w_main, w_f=g_w_f))
    dh1 = [mm(dz, w_main, tb=True, out_dtype=BF16, tm=512, tn=1024, tk=Z_MAIN, name="d_h1", after=tok),
           mm(dzf, w_f, tb=True, out_dtype=BF16, tm=2048, tn=1024, tk=F_PAD, name="d_h1_forget", after=tok)]
    grad_x, gg_pre_mix = _rms_bwd(dh1, x, g_pre_mix, dx2, out_dtype=F32, name="rms_pre_mix_bwd")

    grads = dict(
        b_forget=g_b_forget.reshape(1, N_HEADS), conv_b=_ffn_deinterleave(g_conv_b),
        g_pre_mix=gg_pre_mix, g_post_mix=gg_post_mix, g_pre_ffn=gg_pre_ffn, g_post_ffn=gg_post_ffn)
    return sq_err, grad_x, grads


def _gather_two_level(shard, *, name):
    def body(x_ref, out_ref, send_sems, recv_sems, local_sem):
        x, y, c = lax.axis_index("x"), lax.axis_index("y"), lax.axis_index("c")
        me, sibling = (x, y, c), (x, y, 1 - c)
        chips = [(1 - x, y), (x, 1 - y), (1 - x, 1 - y)]

        def slot(px, py, pc):
            return out_ref.at[4 * px + 2 * py + pc]

        def copy(k, block, to, src=None):
            return pltpu.make_async_remote_copy(
                src_ref=slot(*block) if src is None else src, dst_ref=slot(*block),
                send_sem=send_sems.at[k], recv_sem=recv_sems.at[k], device_id=to, device_id_type=MESH_ID)

        mine = pltpu.make_async_copy(x_ref, slot(*me), local_sem)
        first = [copy(0, me, sibling, src=x_ref)]
        first += [copy(1 + j, me, (*chip, c), src=x_ref) for j, chip in enumerate(chips)]
        for cp in first:
            cp.start()
        mine.start(priority=1)
        passed = [copy(4 + j, (*chip, c), sibling) for j, chip in enumerate(chips)]
        for j, chip in enumerate(chips):
            copy(1 + j, (*chip, c), me).wait_recv()
            passed[j].start()
        copy(0, sibling, me).wait_recv()
        for j, chip in enumerate(chips):
            copy(4 + j, (*chip, 1 - c), me).wait_recv()
        for cp in first + passed:
            cp.wait_send()
        mine.wait()

    return pl.pallas_call(
        body, name=name, in_specs=[ANY], out_specs=ANY, out_shape=_sds((N_DEV,) + shard.shape, shard.dtype),
        scratch_shapes=[pltpu.SemaphoreType.DMA((N_DEV - 1,)), pltpu.SemaphoreType.DMA((N_DEV - 1,)),
                        pltpu.SemaphoreType.DMA],
    )(shard)


N_CHIPS = N_DEV // 2


def _peers(chips_only=False):
    x, y, c = lax.axis_index("x"), lax.axis_index("y"), lax.axis_index("c")
    out = []
    if chips_only:
        for k in range(1, N_CHIPS):
            px = 1 - x if k & 2 else x
            py = 1 - y if k & 1 else y
            out.append(((px, py, c), 2 * px + py))
        return 2 * x + y, out
    for k in range(1, N_DEV):
        px = 1 - x if k & 4 else x
        py = 1 - y if k & 2 else y
        pc = 1 - c if k & 1 else c
        out.append(((px, py, pc), 4 * px + 2 * py + pc))
    return 4 * x + 2 * y + c, out


def _sibling_swap(slot_arrays, *, name):
    n = len(slot_arrays)

    def body(*refs):
        ins, outs, send_sems, recv_sems = refs[:n], refs[n:2 * n], refs[2 * n], refs[2 * n + 1]
        x, y, c = lax.axis_index("x"), lax.axis_index("y"), lax.axis_index("c")
        copies = [pltpu.make_async_remote_copy(
            src_ref=ins[a].at[2 * q + (1 - c)], dst_ref=outs[a].at[q], send_sem=send_sems.at[a, q],
            recv_sem=recv_sems.at[a, q], device_id=(x, y, 1 - c), device_id_type=MESH_ID)
            for a in range(n) for q in range(N_CHIPS)]
        for cp in copies:
            cp.start()
        for cp in copies:
            cp.wait_recv()
        for cp in copies:
            cp.wait_send()

    return pl.pallas_call(
        body, name=name, in_specs=[ANY] * n, out_specs=[ANY] * n,
        out_shape=[_sds((N_CHIPS,) + t.shape[1:], t.dtype) for t in slot_arrays],
        scratch_shapes=[pltpu.SemaphoreType.DMA((n, N_CHIPS)), pltpu.SemaphoreType.DMA((n, N_CHIPS))],
    )(*slot_arrays)


def _pair_sum(slots, from_sibling, *, name, tn):
    _, r, c = slots.shape
    core = lax.axis_index("c").astype(jnp.int32).reshape(1)

    def body(core_ref, a_ref, b_ref, o_ref):
        o_ref[...] = (a_ref[...].astype(F32) + b_ref[...].astype(F32)).astype(o_ref.dtype)

    blk = lambda f: pl.BlockSpec((1, r, tn), f)
    return pl.pallas_call(
        body, name=name,
        grid_spec=pltpu.PrefetchScalarGridSpec(
            num_scalar_prefetch=1, grid=(N_CHIPS, c // tn),
            in_specs=[blk(lambda q, j, core: (2 * q + core[0], 0, j)), blk(lambda q, j, core: (q, 0, j))],
            out_specs=blk(lambda q, j, core: (q, 0, j))),
        out_shape=_sds((N_CHIPS, r, c), slots.dtype),
        compiler_params=_params("parallel", "parallel"),
    )(core, slots, from_sibling)


HBM = pl.BlockSpec(memory_space=pltpu.HBM)
SEM = pl.BlockSpec(memory_space=pltpu.SEMAPHORE)
DATAFLOW = pltpu.SideEffectType.DATAFLOW_SIDE_EFFECTING


def _split_copy(srcs, lands, send_sems, recv_sems, scatter, a, k, me, peers, incoming=False):
    dev, slot = peers[k]
    if incoming:
        src = dst = lands[a].at[slot]
    else:
        src, dst = (srcs[a].at[slot] if scatter else srcs[a]), lands[a].at[me]
    sem = a * len(peers) + k
    return pltpu.make_async_remote_copy(
        src_ref=src, dst_ref=dst, send_sem=send_sems.at[sem], recv_sem=recv_sems.at[sem],
        device_id=dev, device_id_type=MESH_ID)


def _own_copy(srcs, lands, own_sems, scatter, a, me):
    return pltpu.make_async_copy(srcs[a].at[me] if scatter else srcs[a], lands[a].at[me], own_sems.at[a])


def _exchange_start(arrays, scatter, *, name, chips_only=False, after=None):
    n = len(arrays)
    n_slots = N_CHIPS if chips_only else N_DEV
    n_in = 2 * n + len(_also(after))

    def body(*refs):
        srcs, lands = refs[:n], refs[n:2 * n]
        send_sems, recv_sems, own_sems = refs[n_in:n_in + 3]
        token = refs[-1]
        me, peers = _peers(chips_only)
        for k in range(len(peers)):
            for a in range(n):
                _split_copy(srcs, lands, send_sems, recv_sems, scatter, a, k, me, peers).start()
        for a in range(n):
            _own_copy(srcs, lands, own_sems, scatter, a, me).start(priority=1)
        token[...] = jnp.zeros_like(token)

    land_shapes = [((n_slots,) + a.shape[-2:], a.dtype) for a in arrays]
    sems = pltpu.SemaphoreType.DMA((n * (n_slots - 1),))
    outs = pl.pallas_call(
        body, name=name,
        out_shape=(sems, sems, pltpu.SemaphoreType.DMA((n,)), *[pltpu.HBM(a.shape, a.dtype) for a in arrays],
                   *[pltpu.HBM(s, d) for s, d in land_shapes], _sds((SUBLANE, LANE), F32)),
        in_specs=[HBM] * (2 * n) + [ANY] * len(_also(after)),
        out_specs=(SEM, SEM, SEM, *[HBM] * (2 * n), pl.BlockSpec(memory_space=pltpu.VMEM)),
        input_output_aliases={i: 3 + i for i in range(2 * n)},
        compiler_params=pltpu.CompilerParams(has_side_effects=DATAFLOW),
    )(*[pltpu.with_memory_space_constraint(a, pltpu.HBM) for a in arrays],
      *[pltpu.with_memory_space_constraint(lax.empty(s, d), pltpu.HBM) for s, d in land_shapes], *_also(after))
    return (outs[:3], outs[3:3 + n], outs[3 + n:3 + 2 * n], scatter, chips_only), outs[-1]


def _exchange_wait(handles, after, *, name):
    sems, srcs, lands, scatter, chips_only = handles
    n = len(srcs)

    def body(*refs):
        src_refs, land_refs = refs[:n], refs[n:2 * n]
        send_ref, recv_ref, own_ref = refs[2 * n:2 * n + 3]
        me, peers = _peers(chips_only)
        for k in range(len(peers)):
            for a in range(n):
                _split_copy(src_refs, land_refs, send_ref, recv_ref, scatter, a, k, me, peers).wait_send()
                _split_copy(src_refs, land_refs, send_ref, recv_ref, scatter, a, k, me, peers, True).wait_recv()
        for a in range(n):
            _own_copy(src_refs, land_refs, own_ref, scatter, a, me).wait()

    outs = pl.pallas_call(
        body, name=name,
        out_shape=tuple(pltpu.HBM(t.shape, t.dtype) for t in (*srcs, *lands)),
        in_specs=[HBM] * (2 * n) + [SEM, SEM, SEM, pl.BlockSpec(memory_space=pl.ANY)],
        out_specs=tuple([HBM] * (2 * n)),
        input_output_aliases={i: i for i in range(2 * n)},
        compiler_params=pltpu.CompilerParams(has_side_effects=DATAFLOW),
    )(*srcs, *lands, *sems, after)
    return outs[n:]


def _adamw(parts, w, m, v, *, name, tm):
    r, c = w.shape
    assert r % tm == 0

    def body(p_ref, w_ref, m_ref, v_ref, g_ref, d_ref, nm_ref, nv_ref):
        _adamw_update(p_ref, w_ref, m_ref, v_ref, g_ref, d_ref, nm_ref, nv_ref)

    blk = pl.BlockSpec((tm, c), lambda i: (i, 0))
    return pl.pallas_call(
        body, name=name, grid=(r // tm,),
        in_specs=[pl.BlockSpec((parts.shape[0], tm, c), lambda i: (0, i, 0)), blk, blk, blk],
        out_specs=[blk] * 4, out_shape=[_sds((r, c), F32)] * 4,
        compiler_params=_params("parallel"),
    )(parts, w, m, v)


def _adamw_update(p_ref, w_ref, m_ref, v_ref, g_ref, d_ref, nm_ref, nv_ref):
    g = p_ref[0].astype(F32)
    for s in range(1, p_ref.shape[0]):
        g = g + p_ref[s].astype(F32)
    g_ref[...] = g
    m_new = ADAM_B1 * m_ref[...] + (1.0 - ADAM_B1) * g
    v_new = ADAM_B2 * v_ref[...] + (1.0 - ADAM_B2) * (g * g)
    nm_ref[...] = m_new
    nv_ref[...] = v_new
    m_hat = m_new / (1.0 - ADAM_B1 ** ADAM_STEP)
    v_hat = v_new / (1.0 - ADAM_B2 ** ADAM_STEP)
    d_ref[...] = -ADAM_LR * (m_hat / (jnp.sqrt(v_hat) + ADAM_EPS) + ADAM_WD * w_ref[...])


SMALL = ("g_pre_mix", "b_forget", "g_post_mix", "g_pre_ffn", "conv_b", "g_post_ffn")


def _adamw_small(parts, ws, ms, vs, sq_err_parts):
    n = len(ws)

    def body(*refs):
        ins, sq_ref, outs, loss_ref = refs[:4 * n], refs[4 * n], refs[4 * n + 1:-1], refs[-1]
        for i in range(n):
            _adamw_update(ins[i], ins[n + i], ins[2 * n + i], ins[3 * n + i], *outs[4 * i:4 * i + 4])
        total = sq_ref[0]
        for s in range(1, N_DEV):
            total = total + sq_ref[s]
        loss_ref[...] = total * (0.5 / D_MODEL)

    res = pl.pallas_call(
        body, name="adamw_small",
        out_shape=[_sds(w.shape, F32) for w in ws for _ in range(4)] + [_sds((1, LANE), F32)],
        compiler_params=pltpu.CompilerParams(vmem_limit_bytes=VMEM_LIMIT),
    )(*parts, *ws, *ms, *vs, sq_err_parts)
    return [res[4 * i:4 * i + 4] for i in range(n)], res[-1][0, 0]


def kernel(x, g_pre_mix, w_in, b_forget, w_o_fox, w_o_dil, w_out, g_post_mix, g_pre_ffn, w_up, conv_w, conv_b, w_down, g_post_ffn, loss_target, m_g_pre_mix, m_w_in, m_b_forget, m_w_o_fox, m_w_o_dil, m_w_out, m_g_post_mix, m_g_pre_ffn, m_w_up, m_conv_w, m_conv_b, m_w_down, m_g_post_ffn, v_g_pre_mix, v_w_in, v_b_forget, v_w_o_fox, v_w_o_dil, v_w_out, v_g_post_mix, v_g_pre_ffn, v_w_up, v_conv_w, v_conv_b, v_w_down, v_g_post_ffn):
    names = ("g_pre_mix", "w_in", "b_forget", "w_o_fox", "w_o_dil", "w_out", "g_post_mix", "g_pre_ffn",
             "w_up", "conv_w", "conv_b", "w_down", "g_post_ffn")
    w = dict(g_pre_mix=g_pre_mix, w_in=w_in, b_forget=b_forget, w_o_fox=w_o_fox, w_o_dil=w_o_dil, w_out=w_out,
             g_post_mix=g_post_mix, g_pre_ffn=g_pre_ffn, w_up=w_up, conv_w=conv_w, conv_b=conv_b, w_down=w_down,
             g_post_ffn=g_post_ffn)
    m = dict(g_pre_mix=m_g_pre_mix, w_in=m_w_in, b_forget=m_b_forget, w_o_fox=m_w_o_fox, w_o_dil=m_w_o_dil,
             w_out=m_w_out, g_post_mix=m_g_post_mix, g_pre_ffn=m_g_pre_ffn, w_up=m_w_up, conv_w=m_conv_w,
             conv_b=m_conv_b, w_down=m_w_down, g_post_ffn=m_g_post_ffn)
    v = dict(g_pre_mix=v_g_pre_mix, w_in=v_w_in, b_forget=v_b_forget, w_o_fox=v_w_o_fox, w_o_dil=v_w_o_dil,
             w_out=v_w_out, g_post_mix=v_g_post_mix, g_pre_ffn=v_g_pre_ffn, w_up=v_w_up, conv_w=v_conv_w,
             conv_b=v_conv_b, w_down=v_w_down, g_post_ffn=v_g_post_ffn)
    sharded = ("w_in", "w_o_fox", "w_o_dil", "w_out", "w_up", "w_down", "conv_w")
    wire = lambda n: F32 if n == "conv_w" else BF16

    by_cols = lambda t: jnp.transpose(t, (1, 0, 2)).reshape(t.shape[1], N_DEV * t.shape[2])
    by_rows = lambda t: t.reshape(N_DEV * t.shape[1], t.shape[2])
    col_slots = lambda t: jnp.transpose(t.reshape(t.shape[0], N_DEV, t.shape[1] // N_DEV), (1, 0, 2))
    row_slots = lambda t: t.reshape(N_DEV, t.shape[0] // N_DEV, t.shape[1])
    to_slots = lambda n, t: (row_slots if n in ("w_out", "w_down") else col_slots)(t).astype(wire(n))
    shard = lambda n: w[n][0].astype(wire(n))

    w_main, w_f = _w_in_from_shards(_gather_two_level(shard("w_in"), name="gather_w_in"))
    proj_handles, proj_tok = _exchange_start(
        [shard("w_o_fox"), shard("w_o_dil"), shard("w_out")], False, name="gather_proj_start", after=w_f)
    ffn_handles, ffn_tok = _exchange_start(
        [shard("w_up"), shard("conv_w"), shard("w_down")], False, name="gather_ffn_start", after=proj_tok)

    def proj_weights(after):
        w_oa, w_ob, w_o = _exchange_wait(proj_handles, after, name="gather_proj_wait")
        return by_cols(w_oa), by_cols(w_ob), by_rows(w_o)

    def ffn_weights(after):
        w_u, conv, w_d = _exchange_wait(ffn_handles, after, name="gather_ffn_wait")
        return _w_up_from_shards(w_u), _ffn_interleave(by_cols(conv)), by_rows(w_d)

    pending = {}

    def ffn_grads_ready(g):
        slots = [to_slots("w_down", g["w_down"]), _w_up_to_shards(g["w_up_blocks"]), to_slots("conv_w", g["conv_w"])]
        pending["ffn"] = _exchange_start(slots, True, name="scatter_ffn_start")
        return pending["ffn"][1]

    def proj_grads_ready(g):
        pending["proj"] = _exchange_start([g["w_o_fox"], g["w_o_dil"], to_slots("w_out", g["w_out"])], True,
                                          name="scatter_proj_start")
        return pending["proj"][1]

    def mixer_grads_ready(g):
        slots = _w_in_to_shards(g["w_main"], g["w_f"])
        theirs = _sibling_swap([slots], name="scatter_w_in_swap")[0]
        chip_sums = _pair_sum(slots, theirs, name="scatter_w_in_pair_sum", tn=W_IN_SHARD)
        pending["w_in"] = _exchange_start([chip_sums], True, name="scatter_w_in_start", chips_only=True)
        return pending["w_in"][1]

    sq_err, grad_x, g = _local_step(
        x[0], loss_target[0], w_main, w_f, b_forget, conv_b, g_pre_mix, g_post_mix, g_pre_ffn,
        g_post_ffn, proj_weights, ffn_weights, ffn_grads_ready, proj_grads_ready, mixer_grads_ready, after=ffn_tok)

    small_handles, small_tok = _exchange_start([g[n] for n in SMALL] + [sq_err], False, name="gather_small_start")
    tiles = dict(w_in=256, w_o_fox=512, w_o_dil=512, w_out=128, w_up=256, w_down=176, conv_w=3)
    adam = lambda n, p: _adamw(p, w[n][0], m[n][0], v[n][0], name=f"adamw_{n}", tm=tiles[n])
    res = {}
    for key, group in (("ffn", ("w_down", "w_up", "conv_w")), ("proj", ("w_o_fox", "w_o_dil", "w_out"))):
        landed = _exchange_wait(pending[key][0], small_tok, name=f"scatter_{key}_wait")
        res.update({n: adam(n, p) for n, p in zip(group, landed)})
    done = res["w_up"][3]
    res["w_in"] = adam("w_in", _exchange_wait(pending["w_in"][0], done, name="scatter_w_in_wait")[0])
    small_parts = _exchange_wait(small_handles, res["w_in"][3], name="gather_small_wait")
    small, loss = _adamw_small(small_parts[:-1], *[[t[n] for n in SMALL] for t in (w, m, v)], small_parts[-1])
    small = dict(zip(SMALL, small))
    out = [[(res[n][k][None] if n in sharded else small[n][k]) for n in names] for k in range(4)]
    return (loss, grad_x[None], *out[0], *out[1], *out[2], *out[3])
```

```python
import functools
import math

import jax
import jax.numpy as jnp
import numpy as np
from jax import lax
from jax.experimental import pallas as pl
from jax.experimental.pallas import tpu as pltpu

F32 = jnp.float32
BF16 = jnp.bfloat16

SEQ = 4096
D_MODEL = 1024
N_HEADS = 8
HEAD_DIM = 64
ATT_W = N_HEADS * HEAD_DIM
D_FF = 2816
Z_MAIN = 5120
F_PAD = 128
ROPE_DIM = 16
ROPE_THETA = 500000.0
RMS_EPS = 1e-6
NEG_INF = -1e30
SCALE = 1.0 / math.sqrt(HEAD_DIM)
DIL_PATTERNS = ((128, 1), (512, 4), (2048, 16))
DIL_BLK = 128
DIL_STEP_BLOCKS = 2
N_DEV = 8

ADAM_LR = 0.001
ADAM_B1 = 0.9
ADAM_B2 = 0.999
ADAM_EPS = 1e-08
ADAM_WD = 0.01
ADAM_STEP = 10

LANE = 128
SUBLANE = 8
VMEM_LIMIT = 56 * 1024 * 1024
MESH_ID = pl.DeviceIdType.MESH
ANY = pl.BlockSpec(memory_space=pl.ANY)


def _params(*sem):
    return pltpu.CompilerParams(dimension_semantics=sem, vmem_limit_bytes=VMEM_LIMIT)


def _sds(shape, dtype):
    return jax.ShapeDtypeStruct(shape, dtype)


def _also(after):
    return [] if after is None else [after]


def _matmul(a, b, *, ta=False, tb=False, out_dtype, tm, tn, tk, name, b_k_off=0, after=None, col_slots=1):
    n_after = len(_also(after))
    if ta:
        kk, m = a.shape
    else:
        m, kk = a.shape
    n = b.shape[0] if tb else b.shape[1]
    tm, tn, tk = min(tm, m), min(tn, n), min(tk, kk)
    assert (b.shape[1] if tb else b.shape[0]) >= b_k_off * tk + kk
    assert m % tm == 0 and n % tn == 0 and kk % tk == 0, (name, m, n, kk, tm, tn, tk)
    nk = kk // tk
    dims = (((0 if ta else 1,), (1 if tb else 0,)), ((), ()))
    slot_w = n // col_slots
    assert col_slots == 1 or (nk == 1 and tn == n and slot_w % LANE == 0), name

    def body(a_ref, b_ref, *rest):
        o_ref, scratch = rest[n_after], rest[n_after + 1:]
        p = lax.dot_general(a_ref[...].astype(BF16), b_ref[...].astype(BF16), dims,
                            preferred_element_type=F32)
        if col_slots > 1:
            for s in range(col_slots):
                o_ref[s] = p[:, s * slot_w:(s + 1) * slot_w].astype(o_ref.dtype)
        elif nk == 1:
            o_ref[...] = p.astype(o_ref.dtype)
        else:
            acc = scratch[0]
            k = pl.program_id(2)

            @pl.when(k == 0)
            def _():
                acc[...] = p

            @pl.when(k > 0)
            def _():
                acc[...] += p

            @pl.when(k == nk - 1)
            def _():
                o_ref[...] = acc[...].astype(o_ref.dtype)

    a_spec = (pl.BlockSpec((tk, tm), lambda i, j, k: (k, i)) if ta
              else pl.BlockSpec((tm, tk), lambda i, j, k: (i, k)))
    b_spec = (pl.BlockSpec((tn, tk), lambda i, j, k: (j, k + b_k_off)) if tb
              else pl.BlockSpec((tk, tn), lambda i, j, k: (k + b_k_off, j)))
    return pl.pallas_call(
        body, name=name, grid=(m // tm, n // tn, nk),
        in_specs=[a_spec, b_spec] + [ANY] * n_after,
        out_specs=(pl.BlockSpec((tm, tn), lambda i, j, k: (i, j)) if col_slots == 1
                   else pl.BlockSpec((col_slots, tm, slot_w), lambda i, j, k: (0, i, 0))),
        out_shape=_sds((m, n) if col_slots == 1 else (col_slots, m, slot_w), out_dtype),
        scratch_shapes=[pltpu.VMEM((tm, tn), F32)] if nk > 1 else [],
        compiler_params=_params("parallel", "parallel", "arbitrary"),
    )(a, b, *_also(after))


def _rms_fwd(x, g, *, name, tm=512, after=None):
    def body(x_ref, g_ref, *rest):
        h_ref = rest[-1]
        xv = x_ref[...]
        r = lax.rsqrt(jnp.mean(xv * xv, axis=-1, keepdims=True) + RMS_EPS)
        h_ref[...] = (xv * r * g_ref[...]).astype(h_ref.dtype)

    return pl.pallas_call(
        body, name=name, grid=(SEQ // tm,),
        in_specs=[pl.BlockSpec((tm, D_MODEL), lambda i: (i, 0)), pl.BlockSpec((1, D_MODEL), lambda i: (0, 0))]
        + [ANY] * len(_also(after)),
        out_specs=pl.BlockSpec((tm, D_MODEL), lambda i: (i, 0)),
        out_shape=_sds((SEQ, D_MODEL), BF16),
        compiler_params=_params("parallel"),
    )(x, g, *_also(after))


def _rms_bwd(dh_parts, xin, g, dres, *, out_dtype, name, tm=512):
    n_parts = len(dh_parts)
    has_res = dres is not None

    def body(*refs):
        parts = refs[:n_parts]
        x_ref, g_ref = refs[n_parts], refs[n_parts + 1]
        res_ref = refs[n_parts + 2] if has_res else None
        o_ref, gg_ref = refs[-2], refs[-1]
        dh = parts[0][...].astype(F32)
        for p in parts[1:]:
            dh = dh + p[...].astype(F32)
        xv = x_ref[...]
        r = lax.rsqrt(jnp.mean(xv * xv, axis=-1, keepdims=True) + RMS_EPS)
        xn = xv * r

        @pl.when(pl.program_id(0) == 0)
        def _():
            gg_ref[...] = jnp.zeros_like(gg_ref)

        gg_ref[...] += jnp.sum(dh * xn, axis=0, keepdims=True)
        dxn = dh * g_ref[...]
        dx = r * (dxn - xn * jnp.mean(dxn * xn, axis=-1, keepdims=True))
        if has_res:
            dx = dx + res_ref[...]
        o_ref[...] = dx.astype(o_ref.dtype)

    row = pl.BlockSpec((tm, D_MODEL), lambda i: (i, 0))
    vec = pl.BlockSpec((1, D_MODEL), lambda i: (0, 0))
    args = list(dh_parts) + [xin, g] + ([dres] if has_res else [])
    return pl.pallas_call(
        body, name=name, grid=(SEQ // tm,),
        in_specs=[row] * n_parts + [row, vec] + ([row] if has_res else []),
        out_specs=[row, vec],
        out_shape=[_sds((SEQ, D_MODEL), out_dtype), _sds((1, D_MODEL), F32)],
        compiler_params=_params("arbitrary"),
    )(*args)


def _rms_pair_bwd(dh_parts, x2, g_pre, dres, y1, g_post, *, tm=512, after=None):
    n_parts = len(dh_parts)

    def norm_bwd(dh, xin, g_ref, gg_ref):
        r = lax.rsqrt(jnp.mean(xin * xin, axis=-1, keepdims=True) + RMS_EPS)
        xn = xin * r
        gg_ref[...] += jnp.sum(dh * xn, axis=0, keepdims=True)
        dxn = dh * g_ref[...]
        return r * (dxn - xn * jnp.mean(dxn * xn, axis=-1, keepdims=True))

    def body(*refs):
        parts = refs[:n_parts]
        x2_ref, gpre_ref, res_ref, y1_ref, gpost_ref = refs[n_parts:n_parts + 5]
        dx2_ref, dy1_ref, ggpre_ref, ggpost_ref = refs[-4:]

        @pl.when(pl.program_id(0) == 0)
        def _():
            ggpre_ref[...] = jnp.zeros_like(ggpre_ref)
            ggpost_ref[...] = jnp.zeros_like(ggpost_ref)

        dh = parts[0][...].astype(F32)
        for p in parts[1:]:
            dh = dh + p[...].astype(F32)
        dx2 = res_ref[...] + norm_bwd(dh, x2_ref[...], gpre_ref, ggpre_ref)
        dx2_ref[...] = dx2
        dy1_ref[...] = norm_bwd(dx2, y1_ref[...], gpost_ref, ggpost_ref).astype(dy1_ref.dtype)

    row = pl.BlockSpec((tm, D_MODEL), lambda i: (i, 0))
    vec = pl.BlockSpec((1, D_MODEL), lambda i: (0, 0))
    return pl.pallas_call(
        body, name="rms_pair_bwd", grid=(SEQ // tm,),
        in_specs=[row] * n_parts + [row, vec, row, row, vec] + [ANY] * len(_also(after)),
        out_specs=[row, row, vec, vec],
        out_shape=[_sds((SEQ, D_MODEL), F32), _sds((SEQ, D_MODEL), BF16), _sds((1, D_MODEL), F32),
                   _sds((1, D_MODEL), F32)],
        compiler_params=_params("arbitrary"),
    )(*dh_parts, x2, g_pre, dres, y1, g_post, *_also(after))


SCAN_BLK = 512


def _split_dot(v, tri):
    hi = v.astype(BF16)
    r1 = v - hi.astype(F32)
    mid = r1.astype(BF16)
    lo = (r1 - mid.astype(F32)).astype(BF16)
    dot = functools.partial(jnp.dot, preferred_element_type=F32)
    return dot(hi, tri) + dot(mid, tri) + dot(lo, tri)


def _fox_prep(fa_t, b_col):
    nblk = SEQ // SCAN_BLK

    def body(fa_ref, b_ref, f_ref, sg_ref):
        row = lax.broadcasted_iota(jnp.int32, (SCAN_BLK, SCAN_BLK), 0)
        col = lax.broadcasted_iota(jnp.int32, (SCAN_BLK, SCAN_BLK), 1)
        upper = (row <= col).astype(BF16)
        carry = jnp.zeros((N_HEADS, 1), F32)
        for blk in range(nblk):
            sl = pl.ds(blk * SCAN_BLK, SCAN_BLK)
            xx = fa_ref[:, sl] + b_ref[...]
            e = jnp.exp(-jnp.abs(xx))
            logf = jnp.minimum(xx, 0.0) - jnp.log(1.0 + e)
            sg_ref[:, sl] = jnp.where(xx >= 0.0, e, 1.0) / (1.0 + e)
            c = _split_dot(logf, upper) + carry
            f_ref[:, sl] = c
            carry = c[:, SCAN_BLK - 1:SCAN_BLK]

    return pl.pallas_call(
        body, name="fox_prep",
        out_shape=[_sds((N_HEADS, SEQ), F32), _sds((N_HEADS, SEQ), F32)],
        compiler_params=pltpu.CompilerParams(vmem_limit_bytes=VMEM_LIMIT),
    )(fa_t, b_col)


def _fox_post_bwd(df_t, sg_t):
    nblk = SEQ // SCAN_BLK

    def body(df_ref, sg_ref, dfa_ref, gb_ref):
        row = lax.broadcasted_iota(jnp.int32, (SCAN_BLK, SCAN_BLK), 0)
        col = lax.broadcasted_iota(jnp.int32, (SCAN_BLK, SCAN_BLK), 1)
        lower = (row >= col).astype(BF16)
        carry = jnp.zeros((N_HEADS, 1), F32)
        gb = jnp.zeros((N_HEADS, 1), F32)
        for blk in reversed(range(nblk)):
            sl = pl.ds(blk * SCAN_BLK, SCAN_BLK)
            c = _split_dot(df_ref[:, sl], lower) + carry
            carry = c[:, 0:1]
            dfa = c * sg_ref[:, sl]
            dfa_ref[:, sl] = dfa
            gb = gb + jnp.sum(dfa, axis=1, keepdims=True)
        gb_ref[...] = gb

    return pl.pallas_call(
        body, name="fox_post_bwd",
        out_shape=[_sds((N_HEADS, SEQ), F32), _sds((N_HEADS, 1), F32)],
        compiler_params=pltpu.CompilerParams(vmem_limit_bytes=VMEM_LIMIT),
    )(df_t, sg_t)


FOX_T = 512
NT_DIMS = (((1,), (1,)), ((), ()))
TN_DIMS = (((0,), (0,)), ((), ()))


def _head(ref_or_val, h):
    return ref_or_val[:, h * HEAD_DIM:(h + 1) * HEAD_DIM]


def _split3(v):
    hi = v.astype(BF16).astype(F32)
    r1 = v - hi
    mid = r1.astype(BF16).astype(F32)
    return hi, mid, (r1 - mid).astype(BF16).astype(F32)


ONE_LANE = 3 * N_HEADS


def _pack_terms(v, with_one):
    hi, mid, lo = _split3(v)
    t = hi + pltpu.roll(mid, N_HEADS, 1) + pltpu.roll(lo, 2 * N_HEADS, 1)
    if with_one:
        t = t + (lax.broadcasted_iota(jnp.int32, v.shape, 1) == ONE_LANE).astype(F32)
    return t.astype(BF16)


def _aux_matrices():
    to_q = np.zeros((LANE, N_HEADS * 2 * HEAD_DIM), np.float32)
    to_k = np.zeros_like(to_q)
    for h in range(N_HEADS):
        base = h * 2 * HEAD_DIM + HEAD_DIM
        for s in range(3):
            to_q[s * N_HEADS + h, base + s] = 1.0
            to_q[ONE_LANE, base + 3 + s] = 1.0
            to_k[ONE_LANE, base + s] = 1.0
            to_k[s * N_HEADS + h, base + 3 + s] = -1.0
    return jnp.asarray(to_q, BF16), jnp.asarray(to_k, BF16)


def _head_sums():
    total = np.zeros((N_HEADS * HEAD_DIM, LANE), np.float32)
    first = np.zeros_like(total)
    for h in range(N_HEADS):
        total[h * HEAD_DIM:(h + 1) * HEAD_DIM, h] = 1.0
        first[h * HEAD_DIM, h] = 1.0
    return jnp.asarray(total, BF16), jnp.asarray(first, BF16)


SLOT = 2 * HEAD_DIM
N_SPLIT = 3
FOX_FWD_HEADS = 8
FOX_BWD_HEADS = 4


def _slot(ref, h):
    return ref[:, h * SLOT:(h + 1) * SLOT]


def _fox_pack_fwd(zm, f_cols, *, tm=512):
    def body(q_ref, k_ref, v_ref, f_ref, tq_ref, tk_ref, qs_ref, ks_ref, vs_ref):
        ones = jnp.ones((tm, HEAD_DIM), BF16)
        terms = _pack_terms(f_ref[...], True)
        q_aux = jnp.dot(terms, tq_ref[...], preferred_element_type=F32).astype(BF16)
        k_aux = jnp.dot(terms, tk_ref[...], preferred_element_type=F32).astype(BF16)
        for h in range(N_HEADS):
            aux = slice(h * SLOT + HEAD_DIM, (h + 1) * SLOT)
            qs_ref[:, h * SLOT:(h + 1) * SLOT] = jnp.concatenate(
                [(_head(q_ref, h).astype(F32) * SCALE).astype(BF16), q_aux[:, aux]], axis=1)
            ks_ref[:, h * SLOT:(h + 1) * SLOT] = jnp.concatenate([_head(k_ref, h), k_aux[:, aux]], axis=1)
            vs_ref[:, h * SLOT:(h + 1) * SLOT] = jnp.concatenate([_head(v_ref, h), ones], axis=1)

    col = lambda b: pl.BlockSpec((tm, ATT_W), lambda i: (i, b))
    wide = pl.BlockSpec((tm, N_HEADS * SLOT), lambda i: (i, 0))
    const = pl.BlockSpec((LANE, N_HEADS * SLOT), lambda i: (0, 0))
    return pl.pallas_call(
        body, name="fox_pack_fwd", grid=(SEQ // tm,),
        in_specs=[col(0), col(1), col(2), pl.BlockSpec((tm, LANE), lambda i: (i, 0)), const, const],
        out_specs=[wide] * 3, out_shape=[_sds((SEQ, N_HEADS * SLOT), BF16)] * 3,
        compiler_params=_params("parallel"),
    )(zm, zm, zm, f_cols, *_aux_matrices())


def _fox_pack_bwd(zm, f_cols, lse, o, do, *, tm=512, after=None):
    def body(q_ref, f_ref, lse_ref, o_ref, do_ref, tq_ref, total_ref, first_ref, *rest):
        qs_ref, ds_ref = rest[-2:]
        delta = _split_dot(o_ref[...].astype(F32) * do_ref[...].astype(F32), total_ref[...])
        lse_h = _split_dot(lse_ref[...], first_ref[...])
        q_aux = jnp.dot(_pack_terms(f_ref[...] - lse_h, True), tq_ref[...], preferred_element_type=F32).astype(BF16)
        d_aux = jnp.dot(_pack_terms(-delta, False), tq_ref[...], preferred_element_type=F32).astype(BF16)
        for h in range(N_HEADS):
            aux = slice(h * SLOT + HEAD_DIM, (h + 1) * SLOT)
            qs_ref[:, h * SLOT:(h + 1) * SLOT] = jnp.concatenate(
                [(_head(q_ref, h).astype(F32) * SCALE).astype(BF16), q_aux[:, aux]], axis=1)
            ds_ref[:, h * SLOT:(h + 1) * SLOT] = jnp.concatenate([_head(do_ref, h), d_aux[:, aux]], axis=1)

    row = pl.BlockSpec((tm, ATT_W), lambda i: (i, 0))
    wide = pl.BlockSpec((tm, N_HEADS * SLOT), lambda i: (i, 0))
    const = lambda r, c: pl.BlockSpec((r, c), lambda i: (0, 0))
    return pl.pallas_call(
        body, name="fox_pack_bwd", grid=(SEQ // tm,),
        in_specs=[row, pl.BlockSpec((tm, LANE), lambda i: (i, 0)), row, row, row,
                  const(LANE, N_HEADS * SLOT), const(ATT_W, LANE), const(ATT_W, LANE)] + [ANY] * len(_also(after)),
        out_specs=[wide] * 2, out_shape=[_sds((SEQ, N_HEADS * SLOT), BF16)] * 2,
        compiler_params=_params("parallel"),
    )(zm, f_cols, lse, o, do, _aux_matrices()[0], *_head_sums(), *_also(after))


def _causal_pairs(key_major):
    nb = SEQ // FOX_T
    if key_major:
        pairs = [(i, j) for j in range(nb) for i in range(j, nb)]
    else:
        pairs = [(i, j) for i in range(nb) for j in range(i + 1)]
    return (jnp.array([p[0] for p in pairs], jnp.int32), jnp.array([p[1] for p in pairs], jnp.int32), len(pairs))


FOX_HALF = FOX_T // 2
FOX_FULL = ((slice(0, FOX_T), slice(0, FOX_T), None),)
FOX_DIAG = ((slice(0, FOX_HALF), slice(0, FOX_HALF), 0), (slice(FOX_HALF, FOX_T), slice(0, FOX_T), FOX_HALF))


def _causal_piece_mask(q_rows, k_rows, offset):
    shape = (q_rows.stop - q_rows.start, k_rows.stop - k_rows.start)
    row = lax.broadcasted_iota(jnp.int32, shape, 0)
    col = lax.broadcasted_iota(jnp.int32, shape, 1)
    return col <= row + offset


def _fox_fwd(q_slots, k_slots, v_slots):
    i_tab, j_tab, n_pairs = _causal_pairs(False)

    def body(i_tab, j_tab, q_ref, k_ref, v_ref, o_ref, lse_ref, m_s, acc_s):
        t = pl.program_id(1)
        i, j = i_tab[t], j_tab[t]

        @pl.when(j == 0)
        def _():
            m_s[...] = jnp.full_like(m_s, NEG_INF)
            acc_s[...] = jnp.zeros_like(acc_s)

        def step(pieces):
            jobs = [(h, piece) for h in range(FOX_FWD_HEADS) for piece in pieces]
            lanes = lambda h: slice(h * SLOT, (h + 1) * SLOT)
            scores = [lax.dot_general(q_ref[qr, lanes(h)], k_ref[kr, lanes(h)], NT_DIMS, preferred_element_type=F32)
                      for h, (qr, kr, _) in jobs]
            probs, alphas = [], []
            for idx, (h, (qr, kr, offset)) in enumerate(jobs):
                s = scores[idx]
                if offset is not None:
                    s = jnp.where(_causal_piece_mask(qr, kr, offset), s, NEG_INF)
                m_prev = m_s[h, qr, :]
                m_new = jnp.maximum(m_prev, jnp.max(s, axis=-1, keepdims=True))
                probs.append(jnp.exp(s - jnp.tile(m_new, (1, s.shape[1] // LANE))).astype(BF16))
                alphas.append(jnp.exp(m_prev - m_new))
                m_s[h, qr, :] = m_new
            for idx, (h, (qr, kr, _)) in enumerate(jobs):
                acc_s[h, qr, :] = alphas[idx] * acc_s[h, qr, :] + jnp.dot(
                    probs[idx], v_ref[kr, lanes(h)], preferred_element_type=F32)

        @pl.when(j < i)
        def _():
            step(FOX_FULL)

        @pl.when(j == i)
        def _():
            step(FOX_DIAG)
            outs, lses = [], []
            for h in range(FOX_FWD_HEADS):
                acc = acc_s[h]
                l = acc[:, HEAD_DIM:]
                outs.append(acc[:, :HEAD_DIM] / l)
                lses.append(m_s[h][:, :HEAD_DIM] + jnp.log(l))
            o_ref[...] = jnp.concatenate(outs, axis=1).astype(o_ref.dtype)
            lse_ref[...] = jnp.concatenate(lses, axis=1)

    qspec = pl.BlockSpec((FOX_T, FOX_FWD_HEADS * SLOT), lambda p, t, it, jt: (it[t], p))
    kspec = pl.BlockSpec((FOX_T, FOX_FWD_HEADS * SLOT), lambda p, t, it, jt: (jt[t], p))
    ospec = pl.BlockSpec((FOX_T, FOX_FWD_HEADS * HEAD_DIM), lambda p, t, it, jt: (it[t], p))
    return pl.pallas_call(
        body, name="fox_fwd",
        grid_spec=pltpu.PrefetchScalarGridSpec(
            num_scalar_prefetch=2, grid=(N_HEADS // FOX_FWD_HEADS, n_pairs),
            in_specs=[qspec, kspec, kspec], out_specs=[ospec, ospec],
            scratch_shapes=[pltpu.VMEM((FOX_FWD_HEADS, FOX_T, LANE), F32),
                            pltpu.VMEM((FOX_FWD_HEADS, FOX_T, SLOT), F32)]),
        out_shape=[_sds((SEQ, ATT_W), BF16), _sds((SEQ, ATT_W), F32)],
        compiler_params=_params("parallel", "arbitrary"),
    )(i_tab, j_tab, q_slots, k_slots, v_slots)


def _fox_bwd(q_slots, k_slots, v_slots, do_slots):
    i_tab, j_tab, n_pairs = _causal_pairs(True)

    def body(i_tab, j_tab, q_ref, k_ref, v_ref, do_ref, dq_ref, dk_ref, dv_ref):
        t = pl.program_id(1)
        i, j = i_tab[t], j_tab[t]

        @pl.when(t == 0)
        def _():
            dq_ref[...] = jnp.zeros_like(dq_ref)

        @pl.when(i == j)
        def _():
            dk_ref[...] = jnp.zeros_like(dk_ref)
            dv_ref[...] = jnp.zeros_like(dv_ref)

        def step(pieces):
            jobs = [(h, piece) for h in range(FOX_BWD_HEADS) for piece in pieces]
            lanes = lambda h: slice(h * SLOT, (h + 1) * SLOT)
            scores = [lax.dot_general(q_ref[qr, lanes(h)], k_ref[kr, lanes(h)], NT_DIMS, preferred_element_type=F32)
                      for h, (qr, kr, _) in jobs]
            dps = [lax.dot_general(do_ref[qr, lanes(h)], v_ref[kr, lanes(h)], NT_DIMS, preferred_element_type=F32)
                   for h, (qr, kr, _) in jobs]
            ps, dss = [], []
            for idx, (h, (qr, kr, offset)) in enumerate(jobs):
                p = jnp.exp(scores[idx])
                if offset is not None:
                    p = jnp.where(_causal_piece_mask(qr, kr, offset), p, 0.0)
                ps.append(p.astype(BF16))
                dss.append((p * dps[idx]).astype(BF16))
            for idx, (h, (qr, kr, _)) in enumerate(jobs):
                rows = pl.ds(pl.multiple_of(i * FOX_T + qr.start, FOX_HALF), qr.stop - qr.start)
                dv_ref[kr, lanes(h)] += lax.dot_general(ps[idx], do_ref[qr, lanes(h)], TN_DIMS,
                                                        preferred_element_type=F32)
                dk_ref[kr, lanes(h)] += lax.dot_general(dss[idx], q_ref[qr, lanes(h)], TN_DIMS,
                                                        preferred_element_type=F32)
                dq_ref[rows, lanes(h)] += jnp.dot(dss[idx], k_ref[kr, lanes(h)], preferred_element_type=F32)

        @pl.when(i > j)
        def _():
            step(FOX_FULL)

        @pl.when(i == j)
        def _():
            step(FOX_DIAG)

    qspec = pl.BlockSpec((FOX_T, FOX_BWD_HEADS * SLOT), lambda p, t, it, jt: (it[t], p))
    kspec = pl.BlockSpec((FOX_T, FOX_BWD_HEADS * SLOT), lambda p, t, it, jt: (jt[t], p))
    return pl.pallas_call(
        body, name="fox_bwd",
        grid_spec=pltpu.PrefetchScalarGridSpec(
            num_scalar_prefetch=2, grid=(N_HEADS // FOX_BWD_HEADS, n_pairs),
            in_specs=[qspec, kspec, kspec, qspec],
            out_specs=[pl.BlockSpec((SEQ, FOX_BWD_HEADS * SLOT), lambda p, t, it, jt: (0, p)), kspec, kspec]),
        out_shape=[_sds((SEQ, N_HEADS * SLOT), F32)] * 3,
        compiler_params=_params("arbitrary", "arbitrary"),
    )(i_tab, j_tab, q_slots, k_slots, v_slots, do_slots)


def _fox_unpack(dq_slots, dk_slots, dv_slots, dz, *, tm=512):
    def body(dq_ref, dk_ref, dv_ref, dz_in, o_ref, df_ref):
        lane = lax.broadcasted_iota(jnp.int32, (tm, LANE), 1)
        df = jnp.zeros((tm, LANE), F32)
        for h in range(N_HEADS):
            lo = h * SLOT
            for part, (ref, mult) in enumerate(((dq_ref, SCALE), (dk_ref, 1.0), (dv_ref, 1.0))):
                o_ref[:, part * ATT_W + h * HEAD_DIM:part * ATT_W + (h + 1) * HEAD_DIM] = (
                    ref[:, lo:lo + HEAD_DIM] * mult).astype(o_ref.dtype)
            rows = dq_ref[:, lo + HEAD_DIM:lo + HEAD_DIM + 1]
            cols = dk_ref[:, lo + HEAD_DIM + N_SPLIT:lo + HEAD_DIM + N_SPLIT + 1]
            df = jnp.where(lane == h, rows - cols, df)
        df_ref[...] = df

    wide = pl.BlockSpec((tm, N_HEADS * SLOT), lambda i: (i, 0))
    return pl.pallas_call(
        body, name="fox_unpack", grid=(SEQ // tm,), in_specs=[wide] * 3 + [ANY],
        out_specs=[pl.BlockSpec((tm, 3 * ATT_W), lambda i: (i, 0)), pl.BlockSpec((tm, LANE), lambda i: (i, 0))],
        out_shape=[_sds((SEQ, Z_MAIN), BF16), _sds((SEQ, LANE), F32)],
        input_output_aliases={3: 0},
        compiler_params=_params("parallel"),
    )(dq_slots, dk_slots, dv_slots, dz)


def _dil_bwd_prep(o, do, lse, *, tm=512):
    dilations = [d for _, d in DIL_PATTERNS]
    o_chunks = ATT_W // LANE

    def body(o_ref, do_ref, lse_ref, *rest):
        outs, (do_scr, lse_scr, dl_scr) = rest[:-3], rest[-3:]
        dov = do_ref[...].astype(F32)
        prod = o_ref[...].astype(F32) * dov
        lane = lax.broadcasted_iota(jnp.int32, (tm, LANE), 1)
        delta = jnp.zeros((tm, LANE), F32)
        for h in range(N_HEADS):
            delta = jnp.where(lane == h, jnp.sum(_head(prod, h), axis=1, keepdims=True), delta)
        for ch in range(o_chunks):
            do_scr[ch] = dov[:, ch * LANE:(ch + 1) * LANE]
        lse_scr[0] = lse_ref[...]
        dl_scr[0] = delta
        for k, d in enumerate(dilations):
            for scr, out in zip((do_scr, lse_scr, dl_scr), outs[3 * k:3 * k + 3]):
                _slabs_from_rows(scr, out, d)

    row = pl.BlockSpec((tm, ATT_W), lambda i: (i, 0))
    view = lambda d, w: pl.BlockSpec((tm // d, d * w), lambda i: (i, 0))
    outs = pl.pallas_call(
        body, name="dil_bwd_prep", grid=(SEQ // tm,),
        in_specs=[row, row, pl.BlockSpec((tm, LANE), lambda i: (i, 0))],
        out_specs=[view(d, w) for d in dilations for w in (ATT_W, LANE, LANE)],
        out_shape=[_sds((SEQ // d, d * w), t) for d in dilations for w, t in ((ATT_W, BF16), (LANE, F32), (LANE, F32))],
        scratch_shapes=[pltpu.VMEM((o_chunks, tm, LANE), F32), pltpu.VMEM((1, tm, LANE), F32),
                        pltpu.VMEM((1, tm, LANE), F32)],
        compiler_params=_params("parallel"),
    )(o, do, lse)
    return [outs[3 * k:3 * k + 3] for k in range(len(dilations))]


def _rope_tables():
    half = ROPE_DIM // 2
    inv_freq = np.float32(ROPE_THETA) ** (-np.arange(half, dtype=np.float32) * np.float32(2.0) / np.float32(ROPE_DIM))
    ang = np.arange(SEQ, dtype=np.float32)[:, None] * inv_freq.astype(np.float32)[None, :]
    cos, sin = jnp.asarray(np.cos(ang).astype(np.float32)), jnp.asarray(np.sin(ang).astype(np.float32))
    ones = jnp.ones((SEQ, HEAD_DIM - ROPE_DIM), F32)
    zeros = jnp.zeros((SEQ, HEAD_DIM - ROPE_DIM), F32)
    zh = jnp.zeros((SEQ, half), F32)
    c_tab = jnp.concatenate([cos, cos, ones], axis=1)
    a_tab = jnp.concatenate([-sin, zh, zeros], axis=1)
    b_tab = jnp.concatenate([zh, sin, zeros], axis=1)
    two = lambda t: jnp.concatenate([t, t], axis=1)
    return two(c_tab), two(a_tab), two(b_tab)


def _rotate(x, c_tab, a_tab, b_tab):
    return x * c_tab + pltpu.roll(x, LANE - ROPE_DIM // 2, 1) * a_tab + pltpu.roll(x, ROPE_DIM // 2, 1) * b_tab


def _rope_fwd(zm, tabs, *, tm=512):
    width = 3 * ATT_W
    dilations = [d for _, d in DIL_PATTERNS]

    def body(q_ref, k_ref, v_ref, c_ref, a_ref, b_ref, *rest):
        outs, scr = rest[:-1], rest[-1]
        per_part = ATT_W // LANE
        for part, (x_ref, mult) in enumerate(((q_ref, SCALE), (k_ref, 1.0))):
            for cc in range(per_part):
                sl = slice(cc * LANE, (cc + 1) * LANE)
                scr[part * per_part + cc] = _rotate(x_ref[:, sl].astype(F32), c_ref[...], a_ref[...], b_ref[...]) * mult
        for cc in range(per_part):
            scr[2 * per_part + cc] = v_ref[:, cc * LANE:(cc + 1) * LANE].astype(F32)
        for o_ref, d in zip(outs, dilations):
            for r in range(d):
                for ch in range(width // LANE):
                    o_ref[:, r * width + ch * LANE:r * width + (ch + 1) * LANE] = (
                        scr.at[ch][pl.ds(r, tm // d, stride=d), :].astype(o_ref.dtype))

    tab = pl.BlockSpec((tm, LANE), lambda i: (i, 0))
    col = lambda b: pl.BlockSpec((tm, ATT_W), lambda i: (i, b))
    return pl.pallas_call(
        body, name="rope_fwd", grid=(SEQ // tm,),
        in_specs=[col(3), col(4), col(5), tab, tab, tab],
        out_specs=[pl.BlockSpec((tm // d, d * width), lambda i: (i, 0)) for d in dilations],
        out_shape=[_sds((SEQ // d, d * width), BF16) for d in dilations],
        scratch_shapes=[pltpu.VMEM((width // LANE, tm, LANE), F32)],
        compiler_params=_params("parallel"),
    )(zm, zm, zm, *tabs)


def _dil_grad_combine(dqs, dks, dvs, tabs, dz, *, tm=256):
    dilations = [d for _, d in DIL_PATTERNS]
    chunks = ATT_W // LANE

    def body(*refs):
        groups = (refs[0:3], refs[3:6], refs[6:9])
        c_ref, a_ref, b_ref, _, o_ref, scr = refs[9:]

        def total(part, cc):
            acc = None
            for g, (ref, d) in enumerate(zip(groups[part], dilations)):
                term = ref[:, cc * LANE:(cc + 1) * LANE].astype(F32) if d == 1 else scr[part, g, cc]
                acc = term if acc is None else acc + term
            return acc

        for part in range(3):
            for g, (ref, d) in enumerate(zip(groups[part], dilations)):
                if d > 1:
                    _rows_from_slabs(ref, scr.at[part, g], d)
        for cc in range(chunks):
            for part in range(2):
                o_ref[:, part * ATT_W + cc * LANE:part * ATT_W + (cc + 1) * LANE] = _rotate(
                    total(part, cc), c_ref[...], -a_ref[...], -b_ref[...]).astype(o_ref.dtype)
            o_ref[:, 2 * ATT_W + cc * LANE:2 * ATT_W + (cc + 1) * LANE] = total(2, cc).astype(o_ref.dtype)

    view = lambda d: pl.BlockSpec((tm // d, d * ATT_W), lambda i: (i, 0))
    tab = pl.BlockSpec((tm, LANE), lambda i: (i, 0))
    return pl.pallas_call(
        body, name="dil_grad_combine", grid=(SEQ // tm,),
        in_specs=[view(d) for d in dilations] * 3 + [tab] * 3 + [ANY],
        out_specs=pl.BlockSpec((tm, 3 * ATT_W), lambda i: (i, 1)),
        out_shape=_sds((SEQ, Z_MAIN), BF16),
        input_output_aliases={12: 0},
        scratch_shapes=[pltpu.VMEM((3, len(dilations), chunks, tm, LANE), F32)],
        compiler_params=_params("parallel"),
    )(*dqs, *dks, *dvs, *tabs, dz)


def _dil_valid(n):
    qi = lax.broadcasted_iota(jnp.int32, (DIL_BLK, 2 * DIL_BLK), 0)
    ki = lax.broadcasted_iota(jnp.int32, (DIL_BLK, 2 * DIL_BLK), 1)
    dist = qi + DIL_BLK - ki
    return (dist >= 0) & (dist <= DIL_BLK) & ((n > 0) | (ki >= DIL_BLK))


def _dil_fwd(qkv_v, d):
    length = SEQ // d
    nb = length // DIL_BLK
    nsub = min(DIL_STEP_BLOCKS, nb)

    def body(q_ref, kp_ref, kc_ref, vp_ref, vc_ref, o_ref, lse_ref):
        m_step = pl.program_id(1)
        lane = lax.broadcasted_iota(jnp.int32, (DIL_BLK, LANE), 1)
        jobs = [(sub, h) for sub in range(nsub) for h in range(N_HEADS)]
        rows = lambda sub: slice(sub * DIL_BLK, (sub + 1) * DIL_BLK)
        cols = lambda h: slice(h * HEAD_DIM, (h + 1) * HEAD_DIM)

        def keys(prev_ref, cur_ref, sub, h):
            before = prev_ref[:, cols(h)] if sub == 0 else cur_ref[rows(sub - 1), cols(h)]
            return jnp.concatenate([before, cur_ref[rows(sub), cols(h)]], axis=0)

        scores = [lax.dot_general(q_ref[rows(sub), cols(h)], keys(kp_ref, kc_ref, sub, h), NT_DIMS,
                                  preferred_element_type=F32) for sub, h in jobs]
        ok = [_dil_valid(m_step)] + [_dil_valid(1)] * (nsub - 1)
        probs, inv_l, lse_all = [], [], [jnp.zeros((DIL_BLK, LANE), F32)] * nsub
        for idx, (sub, h) in enumerate(jobs):
            s = jnp.where(ok[sub], scores[idx], NEG_INF)
            m = jnp.max(s, axis=-1, keepdims=True)
            p = jnp.exp(s - m)
            l = jnp.sum(p, axis=-1, keepdims=True)
            probs.append(p.astype(BF16))
            inv_l.append(1.0 / l)
            lse_all[sub] = jnp.where(lane == h, m + jnp.log(l), lse_all[sub])
        outs = [jnp.dot(probs[idx], keys(vp_ref, vc_ref, sub, h), preferred_element_type=F32) * inv_l[idx]
                for idx, (sub, h) in enumerate(jobs)]
        for sub in range(nsub):
            o_ref[rows(sub), :] = jnp.concatenate(outs[sub * N_HEADS:(sub + 1) * N_HEADS], axis=1).astype(o_ref.dtype)
            lse_ref[rows(sub), :] = lse_all[sub]

    pair = lambda f: pl.BlockSpec((nsub * DIL_BLK, ATT_W), f)
    one = lambda f: pl.BlockSpec((DIL_BLK, ATT_W), f)
    before = lambda m: jnp.maximum(nsub * m - 1, 0)
    o, lse = pl.pallas_call(
        body, name=f"dil_fwd_d{d}", grid=(d, nb // nsub),
        in_specs=[pair(lambda r, m: (m, 3 * r)),
                  one(lambda r, m: (before(m), 3 * r + 1)), pair(lambda r, m: (m, 3 * r + 1)),
                  one(lambda r, m: (before(m), 3 * r + 2)), pair(lambda r, m: (m, 3 * r + 2))],
        out_specs=[pair(lambda r, m: (m, r)), pl.BlockSpec((nsub * DIL_BLK, LANE), lambda r, m: (m, r))],
        out_shape=[_sds((length, d * ATT_W), BF16), _sds((length, d * LANE), F32)],
        compiler_params=_params("parallel", "arbitrary"),
    )(qkv_v, qkv_v, qkv_v, qkv_v, qkv_v)
    return o, lse


def _rows_from_slabs(view_ref, scr, d):
    chunks, rows = scr.shape[0], scr.shape[1]
    for r in range(d):
        for ch in range(chunks):
            lo = (r * chunks + ch) * LANE
            scr.at[ch][pl.ds(r, rows // d, stride=d), :] = view_ref[:, lo:lo + LANE].astype(F32)


def _slabs_from_rows(scr, view_ref, d):
    chunks, rows = scr.shape[0], scr.shape[1]
    for r in range(d):
        for ch in range(chunks):
            lo = (r * chunks + ch) * LANE
            view_ref[:, lo:lo + LANE] = scr.at[ch][pl.ds(r, rows // d, stride=d), :].astype(view_ref.dtype)


def _dil_merge(os_, lses, *, tm=512):
    dilations = [d for _, d in DIL_PATTERNS]
    o_chunks = ATT_W // LANE

    def body(o0, o1, o2, l0, l1, l2, y_ref, lse_ref, o_scr, l_scr):
        os_nat, ls = [], []
        for g, (o_ref, l_ref, d) in enumerate(zip((o0, o1, o2), (l0, l1, l2), dilations)):
            if d == 1:
                os_nat.append(o_ref[...].astype(F32))
                ls.append(l_ref[...])
            else:
                _rows_from_slabs(o_ref, o_scr.at[g], d)
                _rows_from_slabs(l_ref, l_scr.at[g], d)
                os_nat.append(jnp.concatenate([o_scr[g, ch] for ch in range(o_chunks)], axis=1))
                ls.append(l_scr[g, 0])
        m = jnp.maximum(jnp.maximum(ls[0], ls[1]), ls[2])
        es = [jnp.exp(l - m) for l in ls]
        tot = es[0] + es[1] + es[2]
        lse_ref[...] = m + jnp.log(tot)
        alphas = [e / tot for e in es]
        outs = []
        for h in range(N_HEADS):
            acc = None
            for g in range(3):
                term = alphas[g][:, h:h + 1] * _head(os_nat[g], h)
                acc = term if acc is None else acc + term
            outs.append(acc)
        y_ref[...] = jnp.concatenate(outs, axis=1).astype(y_ref.dtype)

    row = pl.BlockSpec((tm, ATT_W), lambda i: (i, 0))
    vec = pl.BlockSpec((tm, LANE), lambda i: (i, 0))
    view = lambda d, w: pl.BlockSpec((tm // d, d * w), lambda i: (i, 0))
    return pl.pallas_call(
        body, name="dil_merge", grid=(SEQ // tm,),
        in_specs=[view(d, ATT_W) for d in dilations] + [view(d, LANE) for d in dilations], out_specs=[row, vec],
        out_shape=[_sds((SEQ, ATT_W), BF16), _sds((SEQ, LANE), F32)],
        scratch_shapes=[pltpu.VMEM((3, o_chunks, tm, LANE), F32), pltpu.VMEM((3, 1, tm, LANE), F32)],
        compiler_params=_params("parallel"),
    )(*os_, *lses)


def _dil_bwd(qkv_v, do_v, lse_v, dl_v, d):
    length = SEQ // d
    nb = length // DIL_BLK
    nsub = min(DIL_STEP_BLOCKS, nb)
    n_steps = nb // nsub

    def body(q_ref, kp_ref, kc_ref, vp_ref, vc_ref, lse_ref, dl_ref, do_ref, dq_ref, dk_ref, dv_ref, dk_s, dv_s):
        m_step = pl.program_id(1)

        @pl.when(m_step == 0)
        def _():
            dk_s[...] = jnp.zeros_like(dk_s)
            dv_s[...] = jnp.zeros_like(dv_s)

        jobs = [(sub, h) for sub in range(nsub) for h in range(N_HEADS)]
        rows = lambda sub: slice(sub * DIL_BLK, (sub + 1) * DIL_BLK)
        cols = lambda h: slice(h * HEAD_DIM, (h + 1) * HEAD_DIM)

        def keys(prev_ref, cur_ref, sub, h):
            before = prev_ref[:, cols(h)] if sub == 0 else cur_ref[rows(sub - 1), cols(h)]
            return jnp.concatenate([before, cur_ref[rows(sub), cols(h)]], axis=0)

        kks = [keys(kp_ref, kc_ref, sub, h) for sub, h in jobs]
        scores = [lax.dot_general(q_ref[rows(sub), cols(h)], kks[idx], NT_DIMS, preferred_element_type=F32)
                  for idx, (sub, h) in enumerate(jobs)]
        dps = [lax.dot_general(do_ref[rows(sub), cols(h)], keys(vp_ref, vc_ref, sub, h), NT_DIMS,
                               preferred_element_type=F32) for sub, h in jobs]
        ok = [_dil_valid(m_step)] + [_dil_valid(1)] * (nsub - 1)
        ps, dss = [], []
        for idx, (sub, h) in enumerate(jobs):
            p = jnp.where(ok[sub], jnp.exp(scores[idx] - lse_ref[rows(sub), h:h + 1]), 0.0)
            ps.append(p.astype(BF16))
            dss.append((p * (dps[idx] - dl_ref[rows(sub), h:h + 1])).astype(BF16))
        dqs = [jnp.dot(dss[idx], kks[idx], preferred_element_type=F32) * SCALE for idx in range(len(jobs))]
        dkks = [lax.dot_general(dss[idx], q_ref[rows(sub), cols(h)], TN_DIMS, preferred_element_type=F32)
                for idx, (sub, h) in enumerate(jobs)]
        dvvs = [lax.dot_general(ps[idx], do_ref[rows(sub), cols(h)], TN_DIMS, preferred_element_type=F32)
                for idx, (sub, h) in enumerate(jobs)]
        for sub in range(nsub):
            dq_ref[rows(sub), :] = jnp.concatenate(dqs[sub * N_HEADS:(sub + 1) * N_HEADS], axis=1).astype(dq_ref.dtype)
        base = m_step * (nsub * DIL_BLK)
        blocks = [pl.ds(pl.multiple_of(jnp.maximum(base - DIL_BLK, 0), DIL_BLK), DIL_BLK)]
        blocks += [pl.ds(pl.multiple_of(base + s * DIL_BLK, DIL_BLK), DIL_BLK) for s in range(nsub)]
        for acc, parts in ((dk_s, dkks), (dv_s, dvvs)):
            top = lambda sub: jnp.concatenate([parts[sub * N_HEADS + h][:DIL_BLK] for h in range(N_HEADS)], axis=1)
            bottom = lambda sub: jnp.concatenate([parts[sub * N_HEADS + h][DIL_BLK:] for h in range(N_HEADS)], axis=1)
            acc[blocks[0], :] += top(0)
            for s in range(nsub):
                acc[blocks[s + 1], :] += bottom(s) + top(s + 1) if s + 1 < nsub else bottom(s)

        @pl.when(m_step == n_steps - 1)
        def _():
            dk_ref[...] = dk_s[...].astype(dk_ref.dtype)
            dv_ref[...] = dv_s[...].astype(dv_ref.dtype)

    pair = lambda f: pl.BlockSpec((nsub * DIL_BLK, ATT_W), f)
    one = lambda f: pl.BlockSpec((DIL_BLK, ATT_W), f)
    vec = lambda f: pl.BlockSpec((nsub * DIL_BLK, LANE), f)
    whole = pl.BlockSpec((length, ATT_W), lambda r, m: (0, r))
    before = lambda m: jnp.maximum(nsub * m - 1, 0)
    outs = pl.pallas_call(
        body, name=f"dil_bwd_d{d}", grid=(d, n_steps),
        in_specs=[pair(lambda r, m: (m, 3 * r)),
                  one(lambda r, m: (before(m), 3 * r + 1)), pair(lambda r, m: (m, 3 * r + 1)),
                  one(lambda r, m: (before(m), 3 * r + 2)), pair(lambda r, m: (m, 3 * r + 2)),
                  vec(lambda r, m: (m, r)), vec(lambda r, m: (m, r)), pair(lambda r, m: (m, r))],
        out_specs=[pair(lambda r, m: (m, r)), whole, whole],
        out_shape=[_sds((length, d * ATT_W), BF16)] * 3,
        scratch_shapes=[pltpu.VMEM((length, ATT_W), F32), pltpu.VMEM((length, ATT_W), F32)],
        compiler_params=_params("arbitrary", "arbitrary"),
    )(qkv_v, qkv_v, qkv_v, qkv_v, qkv_v, lse_v, dl_v, do_v)
    return outs


def _sigmoid(x):
    return 1.0 / (1.0 + jnp.exp(-x))


def _mix_fwd(ya, yb, w_oa, w_ob, zm, *, tm=512):
    def body(ya_ref, yb_ref, wa_ref, wb_ref, ga_ref, gb_ref, pa_ref, pb_ref, mix_ref):
        pa = jnp.dot(ya_ref[...], wa_ref[...], preferred_element_type=F32)
        pb = jnp.dot(yb_ref[...], wb_ref[...], preferred_element_type=F32)
        pa_ref[...] = pa.astype(pa_ref.dtype)
        pb_ref[...] = pb.astype(pb_ref.dtype)
        mix_ref[...] = (_sigmoid(ga_ref[...].astype(F32)) * pa + _sigmoid(gb_ref[...].astype(F32)) * pb
                        ).astype(mix_ref.dtype)

    row = pl.BlockSpec((tm, ATT_W), lambda i: (i, 0))
    wsp = pl.BlockSpec((ATT_W, D_MODEL), lambda i: (0, 0))
    wide = pl.BlockSpec((tm, D_MODEL), lambda i: (i, 0))
    return pl.pallas_call(
        body, name="mix_fwd", grid=(SEQ // tm,),
        in_specs=[row, row, wsp, wsp, pl.BlockSpec((tm, D_MODEL), lambda i: (i, 3)),
                  pl.BlockSpec((tm, D_MODEL), lambda i: (i, 4))],
        out_specs=[wide] * 3, out_shape=[_sds((SEQ, D_MODEL), BF16)] * 3,
        compiler_params=_params("parallel"),
    )(ya, yb, w_oa, w_ob, zm, zm)


def _gate_bwd(dmix, zm, p, gate_block, dz, *, name, tm=512):
    def body(dm_ref, g_ref, p_ref, *rest):
        dp_ref, dz_ref = rest[-2], rest[-1]
        dm = dm_ref[...].astype(F32)
        s = _sigmoid(g_ref[...].astype(F32))
        dp_ref[...] = (dm * s).astype(dp_ref.dtype)
        dz_ref[...] = (dm * p_ref[...].astype(F32) * s * (1.0 - s)).astype(dz_ref.dtype)

    wide = pl.BlockSpec((tm, D_MODEL), lambda i: (i, 0))
    gate = pl.BlockSpec((tm, D_MODEL), lambda i: (i, gate_block))
    extra = [] if dz is None else [dz]
    return pl.pallas_call(
        body, name=name, grid=(SEQ // tm,),
        in_specs=[wide, gate, wide] + [ANY] * len(extra),
        out_specs=[wide, gate],
        out_shape=[_sds((SEQ, D_MODEL), BF16), _sds((SEQ, Z_MAIN), BF16)],
        input_output_aliases={3: 1} if extra else {},
        compiler_params=_params("parallel"),
    )(dmix, zm, p, *extra)


def _out_fwd(mixed, w_out, x, g_post, g_pre, *, tm=512):
    def body(m_ref, w_ref, x_ref, gp_ref, gn_ref, y_ref, x2_ref, h_ref):
        y = jnp.dot(m_ref[...], w_ref[...], preferred_element_type=F32)
        y_ref[...] = y
        r = lax.rsqrt(jnp.mean(y * y, axis=-1, keepdims=True) + RMS_EPS)
        x2 = x_ref[...] + y * r * gp_ref[...]
        x2_ref[...] = x2
        r2 = lax.rsqrt(jnp.mean(x2 * x2, axis=-1, keepdims=True) + RMS_EPS)
        h_ref[...] = (x2 * r2 * gn_ref[...]).astype(h_ref.dtype)

    row = pl.BlockSpec((tm, D_MODEL), lambda i: (i, 0))
    vec = pl.BlockSpec((1, D_MODEL), lambda i: (0, 0))
    return pl.pallas_call(
        body, name="out_fwd", grid=(SEQ // tm,),
        in_specs=[row, pl.BlockSpec((D_MODEL, D_MODEL), lambda i: (0, 0)), row, vec, vec],
        out_specs=[row] * 3,
        out_shape=[_sds((SEQ, D_MODEL), F32), _sds((SEQ, D_MODEL), F32), _sds((SEQ, D_MODEL), BF16)],
        compiler_params=_params("parallel"),
    )(mixed, w_out, x, g_post, g_pre)


FFN_HALF = 256
FFN_TN = 2 * FFN_HALF
FFN_NJ = D_FF // FFN_HALF
FFN_GROUP = 2 * SUBLANE
UP_TM = 1024


def _ffn_interleave(t):
    lead = t.shape[:-1]
    return jnp.swapaxes(t.reshape(*lead, 2, FFN_NJ, FFN_HALF), -3, -2).reshape(*lead, 2 * D_FF)


def _ffn_deinterleave(t):
    lead = t.shape[:-1]
    return jnp.swapaxes(t.reshape(*lead, FFN_NJ, 2, FFN_HALF), -3, -2).reshape(*lead, 2 * D_FF)


W_IN_SHARD = (Z_MAIN + N_HEADS) // N_DEV
FORGET_LO = 3 * ATT_W


def _w_in_from_shards(shards, *, tm=256):
    def columns(g_ref, lo, width):
        p, off = divmod(lo, W_IN_SHARD)
        if off + width <= W_IN_SHARD:
            return g_ref[p, :, off:off + width]
        first = W_IN_SHARD - off
        return jnp.concatenate([g_ref[p, :, off:], g_ref[p + 1, :, :width - first]], axis=1)

    def body(g_ref, main_ref, f_ref):
        for t in range(Z_MAIN // LANE):
            lo = t * LANE
            main_ref[:, lo:lo + LANE] = columns(g_ref, lo if lo < FORGET_LO else lo + N_HEADS, LANE)
        f_ref[...] = jnp.concatenate([columns(g_ref, FORGET_LO, N_HEADS),
                                      jnp.zeros((tm, F_PAD - N_HEADS), f_ref.dtype)], axis=1)

    return pl.pallas_call(
        body, name="w_in_from_shards", grid=(D_MODEL // tm,),
        in_specs=[pl.BlockSpec((N_DEV, tm, W_IN_SHARD), lambda i: (0, i, 0))],
        out_specs=[pl.BlockSpec((tm, Z_MAIN), lambda i: (i, 0)), pl.BlockSpec((tm, F_PAD), lambda i: (i, 0))],
        out_shape=[_sds((D_MODEL, Z_MAIN), shards.dtype), _sds((D_MODEL, F_PAD), shards.dtype)],
        compiler_params=_params("parallel"),
    )(shards)


def _w_in_to_shards(g_main, g_f, *, tm=256):
    def natural(main_ref, f_ref, lo, width):
        pieces, hi = [], lo + width
        for ref, start, stop, shift in ((main_ref, 0, FORGET_LO, 0), (f_ref, FORGET_LO, FORGET_LO + N_HEADS, FORGET_LO),
                                        (main_ref, FORGET_LO + N_HEADS, Z_MAIN + N_HEADS, N_HEADS)):
            a, b = max(lo, start), min(hi, stop)
            if a < b:
                pieces.append(ref[:, a - shift:b - shift])
        return pieces[0] if len(pieces) == 1 else jnp.concatenate(pieces, axis=1)

    def body(main_ref, f_ref, o_ref):
        for p in range(N_DEV):
            for q in range(-(-W_IN_SHARD // LANE)):
                width = min(LANE, W_IN_SHARD - q * LANE)
                o_ref[p, :, q * LANE:q * LANE + width] = natural(main_ref, f_ref, p * W_IN_SHARD + q * LANE, width)

    return pl.pallas_call(
        body, name="w_in_to_shards", grid=(D_MODEL // tm,),
        in_specs=[pl.BlockSpec((tm, Z_MAIN), lambda i: (i, 0)), pl.BlockSpec((tm, F_PAD), lambda i: (i, 0))],
        out_specs=pl.BlockSpec((N_DEV, tm, W_IN_SHARD), lambda i: (0, i, 0)),
        out_shape=_sds((N_DEV, D_MODEL, W_IN_SHARD), g_main.dtype),
        compiler_params=_params("parallel"),
    )(g_main, g_f)


W_UP_SHARD = 2 * D_FF // N_DEV


def _w_up_lane_tile(k):
    block = k // 2
    return (2 * (block % FFN_NJ) + block // FFN_NJ) * FFN_HALF + (k % 2) * LANE


def _w_up_from_shards(shards, *, tm=256):
    def body(g_ref, o_ref):
        for k in range(2 * D_FF // LANE):
            p, off = divmod(k * LANE, W_UP_SHARD)
            if off + LANE <= W_UP_SHARD:
                tile = g_ref[p, :, off:off + LANE]
            else:
                tile = jnp.concatenate([g_ref[p, :, off:], g_ref[p + 1, :, :off + LANE - W_UP_SHARD]], axis=1)
            dst = _w_up_lane_tile(k)
            o_ref[:, dst:dst + LANE] = tile

    return pl.pallas_call(
        body, name="w_up_from_shards", grid=(D_MODEL // tm,),
        in_specs=[pl.BlockSpec((N_DEV, tm, W_UP_SHARD), lambda i: (0, i, 0))],
        out_specs=pl.BlockSpec((tm, 2 * D_FF), lambda i: (i, 0)),
        out_shape=_sds((D_MODEL, 2 * D_FF), shards.dtype),
        compiler_params=_params("parallel"),
    )(shards)


def _w_up_to_shards(t, *, tm=256):
    def body(x_ref, o_ref):
        for p in range(N_DEV):
            for q in range(-(-W_UP_SHARD // LANE)):
                width = min(LANE, W_UP_SHARD - q * LANE)
                k, off = divmod(p * W_UP_SHARD + q * LANE, LANE)
                src = _w_up_lane_tile(k)
                if off == 0:
                    tile = x_ref[:, src:src + width]
                else:
                    tile = x_ref[:, src + off:src + LANE]
                    if width > LANE - off:
                        nxt = _w_up_lane_tile(k + 1)
                        tile = jnp.concatenate([tile, x_ref[:, nxt:nxt + width - (LANE - off)]], axis=1)
                o_ref[p, :, q * LANE:q * LANE + width] = tile

    return pl.pallas_call(
        body, name="w_up_to_shards", grid=(D_MODEL // tm,),
        in_specs=[pl.BlockSpec((tm, 2 * D_FF), lambda i: (i, 0))],
        out_specs=pl.BlockSpec((N_DEV, tm, W_UP_SHARD), lambda i: (0, i, 0)),
        out_shape=_sds((N_DEV, D_MODEL, W_UP_SHARD), t.dtype),
        compiler_params=_params("parallel"),
    )(t)


def _gelu_parts(a):
    c = math.sqrt(2.0 / math.pi)
    a2 = a * a
    t = jnp.tanh((c * a) * (1.0 + 0.044715 * a2))
    half_a, one_t = 0.5 * a, 1.0 + t
    gelu = half_a * one_t
    dgelu = 0.5 * one_t + half_a * (1.0 - t * t) * (c + (3.0 * 0.044715 * c) * a2)
    return gelu, dgelu


def _row_masks(down):
    row = lax.broadcasted_iota(jnp.int32, (SUBLANE, FFN_TN), 0)
    return (row < 1, row < 2) if down else (row >= SUBLANE - 1, row >= SUBLANE - 2)


def _rolled(x, down):
    return (pltpu.roll(x, 1, 0), pltpu.roll(x, 2, 0)) if down else (
        pltpu.roll(x, SUBLANE - 1, 0), pltpu.roll(x, SUBLANE - 2, 0))


def _shifted(cur_rolled, neighbour_rolled, masks):
    return (jnp.where(masks[0], neighbour_rolled[0], cur_rolled[0]),
            jnp.where(masks[1], neighbour_rolled[1], cur_rolled[1]))


def _conv_consts(w_ref, b_ref):
    shape = (SUBLANE, FFN_TN)
    return [jnp.broadcast_to(w_ref[k:k + 1, :], shape) for k in range(3)] + [jnp.broadcast_to(b_ref[...], shape)]


def _up_conv_fwd(h2, w_up, conv_w, conv_b):
    nrow = SEQ // UP_TM
    n_tiles = FFN_NJ * nrow
    n_groups = UP_TM // FFN_GROUP

    def body(h_ref, wu_ref, w_ref, b_ref, u_ref, ab_ref, m_ref, ua_s, ub_s, c1_s, c2_s):
        k = pl.program_id(0)

        @pl.when(k == 0)
        def _():
            ub_s[...] = jnp.zeros_like(ub_s)

        @pl.when(jnp.maximum(k - 1, 0) % nrow == 0)
        def _():
            c1_s[...] = jnp.zeros_like(c1_s)
            c2_s[...] = jnp.zeros_like(c2_s)

        def step(write_s, read_s):
            h_rows = pl.ds(pl.multiple_of((this(k) % nrow) * UP_TM, UP_TM), UP_TM)
            u = jnp.dot(h_ref[h_rows, :], wu_ref[...], preferred_element_type=F32)
            write_s[...] = u
            u_ref[...] = u.astype(u_ref.dtype)
            w0, w1, w2, bias = _conv_consts(w_ref, b_ref)
            masks = _row_masks(True)
            above = (c1_s[...], c2_s[...])
            for g in range(n_groups):
                rows = slice(g * FFN_GROUP, (g + 1) * FFN_GROUP)
                x = read_s[rows, :]
                convs = []
                for c in range(2):
                    cur = x[c * SUBLANE:(c + 1) * SUBLANE]
                    cur_rolled = _rolled(cur, True)
                    s1, s2 = _shifted(cur_rolled, above, masks)
                    convs.append(w0 * s2 + w1 * s1 + w2 * cur + bias)
                    above = cur_rolled
                y = jnp.concatenate(convs, axis=0)
                ab_ref[rows, :] = y.astype(ab_ref.dtype)
                m_ref[rows, :] = (_gelu_parts(y[:, :FFN_HALF])[0] * y[:, FFN_HALF:]).astype(m_ref.dtype)
            c1_s[...], c2_s[...] = above

        @pl.when(k % 2 == 0)
        def _():
            step(ua_s, ub_s)

        @pl.when(k % 2 == 1)
        def _():
            step(ub_s, ua_s)

    this = lambda k: jnp.minimum(k, n_tiles - 1)
    last = lambda k: jnp.maximum(k - 1, 0)
    blk = lambda tile: pl.BlockSpec((UP_TM, FFN_TN), lambda k: (tile(k) % nrow, tile(k) // nrow))
    return pl.pallas_call(
        body, name="up_conv_fwd", grid=(n_tiles + 1,),
        in_specs=[pl.BlockSpec((SEQ, D_MODEL), lambda k: (0, 0)),
                  pl.BlockSpec((D_MODEL, FFN_TN), lambda k: (0, this(k) // nrow)),
                  pl.BlockSpec((3, FFN_TN), lambda k: (0, last(k) // nrow)),
                  pl.BlockSpec((1, FFN_TN), lambda k: (0, last(k) // nrow))],
        out_specs=[blk(this), blk(last), pl.BlockSpec((UP_TM, FFN_HALF), lambda k: (last(k) % nrow, last(k) // nrow))],
        out_shape=[_sds((SEQ, 2 * D_FF), BF16), _sds((SEQ, 2 * D_FF), BF16), _sds((SEQ, D_FF), BF16)],
        scratch_shapes=[pltpu.VMEM((UP_TM, FFN_TN), F32), pltpu.VMEM((UP_TM, FFN_TN), F32),
                        pltpu.VMEM((SUBLANE, FFN_TN), F32), pltpu.VMEM((SUBLANE, FFN_TN), F32)],
        compiler_params=_params("arbitrary"),
    )(h2, w_up, conv_w, conv_b)


def _ffn_mid_bwd(dy2, w_down, u, ab, conv_w):
    nrow = SEQ // UP_TM
    n_tiles = FFN_NJ * nrow
    n_groups = UP_TM // FFN_GROUP
    this = lambda k: jnp.minimum(k, n_tiles - 1)
    last = lambda k: jnp.maximum(k - 1, 0)
    row_of = lambda tile: nrow - 1 - tile % nrow

    def body(dy_ref, wd_ref, u_ref, ab_ref, w_ref, du_ref, gw_ref, gb_ref, c_s, dma_s, dmb_s):
        k = pl.program_id(0)

        @pl.when(k == 0)
        def _():
            dmb_s[...] = jnp.zeros_like(dmb_s)

        @pl.when(last(k) % nrow == 0)
        def _():
            c_s[...] = jnp.zeros_like(c_s)
            gw_ref[...] = jnp.zeros_like(gw_ref)
            gb_ref[...] = jnp.zeros_like(gb_ref)

        def step(write_s, read_s):
            dy_rows = pl.ds(pl.multiple_of(row_of(this(k)) * UP_TM, UP_TM), UP_TM)
            write_s[...] = lax.dot_general(dy_ref[dy_rows, :], wd_ref[...], NT_DIMS,
                                           preferred_element_type=F32)
            taps = [jnp.broadcast_to(w_ref[t:t + 1, :], (SUBLANE, FFN_TN)) for t in range(3)]
            masks = _row_masks(False)
            below = _rolled(c_s[...], False)
            acc = [jnp.zeros((SUBLANE, FFN_TN), F32)] * 4
            for g in reversed(range(n_groups)):
                rows = slice(g * FFN_GROUP, (g + 1) * FFN_GROUP)
                x, y, dmv = u_ref[rows, :].astype(F32), ab_ref[rows, :].astype(F32), read_s[rows, :]
                gelu, dgelu = _gelu_parts(y[:, :FFN_HALF])
                d = jnp.concatenate([dmv * y[:, FFN_HALF:] * dgelu, dmv * gelu], axis=1)
                pre = [None, None]
                for c in (1, 0):
                    sl = slice(c * SUBLANE, (c + 1) * SUBLANE)
                    cur, xs = d[sl], x[sl]
                    cur_rolled = _rolled(cur, False)
                    up1, up2 = _shifted(cur_rolled, below, masks)
                    acc = [acc[0] + up2 * xs, acc[1] + up1 * xs, acc[2] + cur * xs, acc[3] + cur]
                    pre[c] = taps[2] * cur + taps[1] * up1 + taps[0] * up2
                    below = cur_rolled
                du_ref[rows, :] = jnp.concatenate(pre, axis=0).astype(du_ref.dtype)
            c_s[...] = pltpu.roll(below[0], 1, 0)
            for t in range(3):
                gw_ref[t:t + 1, :] += jnp.sum(acc[t], axis=0, keepdims=True)
            gb_ref[...] += jnp.sum(acc[3], axis=0, keepdims=True)

        @pl.when(k % 2 == 0)
        def _():
            step(dma_s, dmb_s)

        @pl.when(k % 2 == 1)
        def _():
            step(dmb_s, dma_s)

    blk = pl.BlockSpec((UP_TM, FFN_TN), lambda k: (row_of(last(k)), last(k) // nrow))
    col = lambda rows: pl.BlockSpec((rows, FFN_TN), lambda k: (0, last(k) // nrow))
    return pl.pallas_call(
        body, name="ffn_mid_bwd", grid=(n_tiles + 1,),
        in_specs=[pl.BlockSpec((SEQ, D_MODEL), lambda k: (0, 0)),
                  pl.BlockSpec((FFN_HALF, D_MODEL), lambda k: (this(k) // nrow, 0)), blk, blk, col(3)],
        out_specs=[blk, col(3), col(1)],
        out_shape=[_sds((SEQ, 2 * D_FF), BF16), _sds((3, 2 * D_FF), F32), _sds((1, 2 * D_FF), F32)],
        scratch_shapes=[pltpu.VMEM((SUBLANE, FFN_TN), F32), pltpu.VMEM((UP_TM, FFN_HALF), F32),
                        pltpu.VMEM((UP_TM, FFN_HALF), F32)],
        compiler_params=_params("arbitrary"),
    )(dy2, w_down, u, ab, conv_w)


def _down_fwd(m, w_down, x2, g_post, target, *, tm=512):
    def body(m_ref, w_ref, x2_ref, g_ref, t_ref, dout_ref, dy_ref, gg_ref, loss_ref):
        @pl.when(pl.program_id(0) == 0)
        def _():
            gg_ref[...] = jnp.zeros_like(gg_ref)
            loss_ref[...] = jnp.zeros_like(loss_ref)

        y = jnp.dot(m_ref[...], w_ref[...], preferred_element_type=F32)
        r = lax.rsqrt(jnp.mean(y * y, axis=-1, keepdims=True) + RMS_EPS)
        yn = y * r
        diff = (x2_ref[...] + yn * g_ref[...]) - t_ref[...]
        loss_ref[...] += jnp.sum(diff * diff)
        dout = diff * (1.0 / D_MODEL)
        dout_ref[...] = dout
        gg_ref[...] += jnp.sum(dout * yn, axis=0, keepdims=True)
        dn = dout * g_ref[...]
        dy_ref[...] = (r * (dn - yn * jnp.mean(dn * yn, axis=-1, keepdims=True))).astype(dy_ref.dtype)

    row = pl.BlockSpec((tm, D_MODEL), lambda i: (i, 0))
    vec = pl.BlockSpec((1, D_MODEL), lambda i: (0, 0))
    return pl.pallas_call(
        body, name="down_fwd", grid=(SEQ // tm,),
        in_specs=[pl.BlockSpec((tm, D_FF), lambda i: (i, 0)), pl.BlockSpec((D_FF, D_MODEL), lambda i: (0, 0)),
                  row, vec, row],
        out_specs=[row, row, vec, pl.BlockSpec((1, LANE), lambda i: (0, 0))],
        out_shape=[_sds((SEQ, D_MODEL), F32), _sds((SEQ, D_MODEL), BF16), _sds((1, D_MODEL), F32),
                   _sds((1, LANE), F32)],
        compiler_params=_params("arbitrary"),
    )(m, w_down, x2, g_post, target)


def _local_step(x, target, w_main, w_f, b_forget, conv_b, g_pre_mix, g_post_mix, g_pre_ffn, g_post_ffn,
                proj_weights, ffn_weights, ffn_grads_ready, proj_grads_ready, mixer_grads_ready, after=None):
    mm = _matmul
    tabs = _rope_tables()

    h1 = _rms_fwd(x, g_pre_mix, name="rms_pre_mix", after=after)
    zm = mm(h1, w_main, out_dtype=BF16, tm=2048, tn=1024, tk=1024, name="in_proj")
    zf = mm(h1, w_f, out_dtype=F32, tm=2048, tn=F_PAD, tk=1024, name="in_proj_forget")
    f_row, sg_row = _fox_prep(zf[:, :N_HEADS].T, b_forget.reshape(N_HEADS, 1))
    f_cols = jnp.pad(f_row.T, ((0, 0), (0, LANE - N_HEADS)))
    q_slots, k_slots, v_slots = _fox_pack_fwd(zm, f_cols)
    ya, lse_a = _fox_fwd(q_slots, k_slots, v_slots)
    qkv_d = dict(zip([d for _, d in DIL_PATTERNS], _rope_fwd(zm, tabs)))
    dil = [_dil_fwd(qkv_d[d], d) for _, d in DIL_PATTERNS]
    yb, lse_b = _dil_merge([o for o, _ in dil], [l for _, l in dil])
    w_oa, w_ob, w_out = proj_weights(yb)
    pa, pb, mixed = _mix_fwd(ya, yb, w_oa, w_ob, zm)
    y1, x2, h2 = _out_fwd(mixed, w_out, x, g_post_mix, g_pre_ffn)
    w_up, conv_w, w_down = ffn_weights(h2)
    u, ab, m = _up_conv_fwd(h2, w_up, conv_w, _ffn_interleave(conv_b))
    dout, dy2, gg_post_ffn, sq_err = _down_fwd(m, w_down, x2, g_post_ffn, target)

    g_w_down = mm(m, dy2, ta=True, out_dtype=BF16, tm=D_FF // 2, tn=1024, tk=SEQ, name="grad_w_down")
    du, g_conv_w, g_conv_b = _ffn_mid_bwd(dy2, w_down, u, ab, conv_w)
    g_w_up = mm(h2, du, ta=True, out_dtype=BF16, tm=1024, tn=D_FF // 2, tk=SEQ, name="grad_w_up")
    tok = ffn_grads_ready(dict(w_down=g_w_down, w_up_blocks=g_w_up, conv_w=_ffn_deinterleave(g_conv_w)))
    dh2 = mm(du, w_up, tb=True, out_dtype=BF16, tm=512, tn=1024, tk=2 * D_FF, name="d_h2")

    dx2, dy1, gg_pre_ffn, gg_post_mix = _rms_pair_bwd([dh2], x2, g_pre_ffn, dout, y1, g_post_mix, after=tok)
    g_w_out = mm(mixed, dy1, ta=True, out_dtype=BF16, tm=1024, tn=1024, tk=SEQ, name="grad_w_out")
    dmix = mm(dy1, w_out, tb=True, out_dtype=BF16, tm=2048, tn=1024, tk=1024, name="d_mixed")
    dpa, dz = _gate_bwd(dmix, zm, pa, 3, None, name="gate_bwd_fox")
    dpb, dz = _gate_bwd(dmix, zm, pb, 4, dz, name="gate_bwd_dil")
    g_w_oa = mm(ya, dpa, ta=True, out_dtype=BF16, tm=512, tn=1024, tk=SEQ, name="grad_w_o_fox", col_slots=N_DEV)
    g_w_ob = mm(yb, dpb, ta=True, out_dtype=BF16, tm=512, tn=1024, tk=SEQ, name="grad_w_o_dil", col_slots=N_DEV)
    tok = proj_grads_ready(dict(w_o_fox=g_w_oa, w_o_dil=g_w_ob, w_out=g_w_out))
    dya = mm(dpa, w_oa, tb=True, out_dtype=BF16, tm=2048, tn=512, tk=1024, name="d_y_fox")
    dyb = mm(dpb, w_ob, tb=True, out_dtype=BF16, tm=2048, tn=512, tk=1024, name="d_y_dil")

    qb_slots, do_slots = _fox_pack_bwd(zm, f_cols, lse_a, ya, dya, after=tok)
    dz, df_cols = _fox_unpack(*_fox_bwd(qb_slots, k_slots, v_slots, do_slots), dz)
    dfa_t, g_b_forget = _fox_post_bwd(df_cols[:, :N_HEADS].T, sg_row)

    rows_d = _dil_bwd_prep(yb, dyb, lse_b)
    dil_g = [_dil_bwd(qkv_d[d], *rows_d[k], d) for k, (_, d) in enumerate(DIL_PATTERNS)]
    dz = _dil_grad_combine([g[0] for g in dil_g], [g[1] for g in dil_g], [g[2] for g in dil_g], tabs, dz)

    dzf = jnp.pad(dfa_t.T, ((0, 0), (0, F_PAD - N_HEADS)))
    g_w_main = mm(h1, dz, ta=True, out_dtype=BF16, tm=1024, tn=Z_MAIN // 4, tk=SEQ, name="grad_w_in")
    g_w_f = mm(h1, dzf, ta=True, out_dtype=BF16, tm=1024, tn=F_PAD, tk=1024, name="grad_w_in_forget")
    tok = mixer_grads_ready(dict(w_main=g_w_main, w_f=g_w_f))
    dh1 = [mm(dz, w_main, tb=True, out_dtype=BF16, tm=512, tn=1024, tk=Z_MAIN, name="d_h1", after=tok),
           mm(dzf, w_f, tb=True, out_dtype=BF16, tm=2048, tn=1024, tk=F_PAD, name="d_h1_forget", after=tok)]
    grad_x, gg_pre_mix = _rms_bwd(dh1, x, g_pre_mix, dx2, out_dtype=F32, name="rms_pre_mix_bwd")

    grads = dict(
        b_forget=g_b_forget.reshape(1, N_HEADS), conv_b=_ffn_deinterleave(g_conv_b),
        g_pre_mix=gg_pre_mix, g_post_mix=gg_post_mix, g_pre_ffn=gg_pre_ffn, g_post_ffn=gg_post_ffn)
    return sq_err, grad_x, grads


def _gather_two_level(shard, *, name):
    def body(x_ref, out_ref, send_sems, recv_sems, local_sem):
        x, y, c = lax.axis_index("x"), lax.axis_index("y"), lax.axis_index("c")
        me, sibling = (x, y, c), (x, y, 1 - c)
        chips = [(1 - x, y), (x, 1 - y), (1 - x, 1 - y)]

        def slot(px, py, pc):
            return out_ref.at[4 * px + 2 * py + pc]

        def copy(k, block, to, src=None):
            return pltpu.make_async_remote_copy(
                src_ref=slot(*block) if src is None else src, dst_ref=slot(*block),
                send_sem=send_sems.at[k], recv_sem=recv_sems.at[k], device_id=to, device_id_type=MESH_ID)

        mine = pltpu.make_async_copy(x_ref, slot(*me), local_sem)
        first = [copy(0, me, sibling, src=x_ref)]
        first += [copy(1 + j, me, (*chips[j], c), src=x_ref) for j in range(2)]
        for cp in first:
            cp.start()
        mine.start(priority=1)
        passed = [copy(4 + j, (*chip, c), sibling) for j, chip in enumerate(chips)]

        def route(near, far):
            copy(1 + near, (*chips[near], c), me).wait_recv()
            onward = copy(3, (*chips[near], c), (*chips[far], c))
            onward.start()
            passed[near].start()
            copy(1 + far, (*chips[far], c), me).wait_recv()
            passed[far].start()
            copy(3, (*chips[2], c), me).wait_recv()
            passed[2].start()
            onward.wait_send()

        pl.when(c == 0)(lambda: route(0, 1))
        pl.when(c == 1)(lambda: route(1, 0))
        copy(0, sibling, me).wait_recv()
        for j, chip in enumerate(chips):
            copy(4 + j, (*chip, 1 - c), me).wait_recv()
        for cp in first + passed:
            cp.wait_send()
        mine.wait()

    return pl.pallas_call(
        body, name=name, in_specs=[ANY], out_specs=ANY, out_shape=_sds((N_DEV,) + shard.shape, shard.dtype),
        scratch_shapes=[pltpu.SemaphoreType.DMA((N_DEV - 1,)), pltpu.SemaphoreType.DMA((N_DEV - 1,)),
                        pltpu.SemaphoreType.DMA],
    )(shard)


N_CHIPS = N_DEV // 2


def _peers(chips_only=False):
    x, y, c = lax.axis_index("x"), lax.axis_index("y"), lax.axis_index("c")
    out = []
    if chips_only:
        for k in range(1, N_CHIPS):
            px = 1 - x if k & 2 else x
            py = 1 - y if k & 1 else y
            out.append(((px, py, c), 2 * px + py))
        return 2 * x + y, out
    for k in range(1, N_DEV):
        px = 1 - x if k & 4 else x
        py = 1 - y if k & 2 else y
        pc = 1 - c if k & 1 else c
        out.append(((px, py, pc), 4 * px + 2 * py + pc))
    return 4 * x + 2 * y + c, out


def _sibling_swap(slot_arrays, *, name):
    n = len(slot_arrays)

    def body(*refs):
        ins, outs, send_sems, recv_sems = refs[:n], refs[n:2 * n], refs[2 * n], refs[2 * n + 1]
        x, y, c = lax.axis_index("x"), lax.axis_index("y"), lax.axis_index("c")
        copies = [pltpu.make_async_remote_copy(
            src_ref=ins[a].at[2 * q + (1 - c)], dst_ref=outs[a].at[q], send_sem=send_sems.at[a, q],
            recv_sem=recv_sems.at[a, q], device_id=(x, y, 1 - c), device_id_type=MESH_ID)
            for a in range(n) for q in range(N_CHIPS)]
        for cp in copies:
            cp.start()
        for cp in copies:
            cp.wait_recv()
        for cp in copies:
            cp.wait_send()

    return pl.pallas_call(
        body, name=name, in_specs=[ANY] * n, out_specs=[ANY] * n,
        out_shape=[_sds((N_CHIPS,) + t.shape[1:], t.dtype) for t in slot_arrays],
        scratch_shapes=[pltpu.SemaphoreType.DMA((n, N_CHIPS)), pltpu.SemaphoreType.DMA((n, N_CHIPS))],
    )(*slot_arrays)


def _pair_sum(slots, from_sibling, *, name, tn):
    _, r, c = slots.shape
    core = lax.axis_index("c").astype(jnp.int32).reshape(1)

    def body(core_ref, a_ref, b_ref, o_ref):
        o_ref[...] = (a_ref[...].astype(F32) + b_ref[...].astype(F32)).astype(o_ref.dtype)

    blk = lambda f: pl.BlockSpec((1, r, tn), f)
    return pl.pallas_call(
        body, name=name,
        grid_spec=pltpu.PrefetchScalarGridSpec(
            num_scalar_prefetch=1, grid=(N_CHIPS, c // tn),
            in_specs=[blk(lambda q, j, core: (2 * q + core[0], 0, j)), blk(lambda q, j, core: (q, 0, j))],
            out_specs=blk(lambda q, j, core: (q, 0, j))),
        out_shape=_sds((N_CHIPS, r, c), slots.dtype),
        compiler_params=_params("parallel", "parallel"),
    )(core, slots, from_sibling)


HBM = pl.BlockSpec(memory_space=pltpu.HBM)
SEM = pl.BlockSpec(memory_space=pltpu.SEMAPHORE)
DATAFLOW = pltpu.SideEffectType.DATAFLOW_SIDE_EFFECTING


def _split_copy(srcs, lands, send_sems, recv_sems, scatter, a, k, me, peers, incoming=False):
    dev, slot = peers[k]
    if incoming:
        src = dst = lands[a].at[slot]
    else:
        src, dst = (srcs[a].at[slot] if scatter else srcs[a]), lands[a].at[me]
    sem = a * len(peers) + k
    return pltpu.make_async_remote_copy(
        src_ref=src, dst_ref=dst, send_sem=send_sems.at[sem], recv_sem=recv_sems.at[sem],
        device_id=dev, device_id_type=MESH_ID)


def _own_copy(srcs, lands, own_sems, scatter, a, me):
    return pltpu.make_async_copy(srcs[a].at[me] if scatter else srcs[a], lands[a].at[me], own_sems.at[a])


def _exchange_start(arrays, scatter, *, name, chips_only=False, after=None):
    n = len(arrays)
    n_slots = N_CHIPS if chips_only else N_DEV
    n_in = 2 * n + len(_also(after))

    def body(*refs):
        srcs, lands = refs[:n], refs[n:2 * n]
        send_sems, recv_sems, own_sems = refs[n_in:n_in + 3]
        token = refs[-1]
        me, peers = _peers(chips_only)
        for k in range(len(peers)):
            for a in range(n):
                _split_copy(srcs, lands, send_sems, recv_sems, scatter, a, k, me, peers).start()
        for a in range(n):
            _own_copy(srcs, lands, own_sems, scatter, a, me).start(priority=1)
        token[...] = jnp.zeros_like(token)

    land_shapes = [((n_slots,) + a.shape[-2:], a.dtype) for a in arrays]
    sems = pltpu.SemaphoreType.DMA((n * (n_slots - 1),))
    outs = pl.pallas_call(
        body, name=name,
        out_shape=(sems, sems, pltpu.SemaphoreType.DMA((n,)), *[pltpu.HBM(a.shape, a.dtype) for a in arrays],
                   *[pltpu.HBM(s, d) for s, d in land_shapes], _sds((SUBLANE, LANE), F32)),
        in_specs=[HBM] * (2 * n) + [ANY] * len(_also(after)),
        out_specs=(SEM, SEM, SEM, *[HBM] * (2 * n), pl.BlockSpec(memory_space=pltpu.VMEM)),
        input_output_aliases={i: 3 + i for i in range(2 * n)},
        compiler_params=pltpu.CompilerParams(has_side_effects=DATAFLOW),
    )(*[pltpu.with_memory_space_constraint(a, pltpu.HBM) for a in arrays],
      *[pltpu.with_memory_space_constraint(lax.empty(s, d), pltpu.HBM) for s, d in land_shapes], *_also(after))
    return (outs[:3], outs[3:3 + n], outs[3 + n:3 + 2 * n], scatter, chips_only), outs[-1]


def _exchange_wait(handles, after, *, name):
    sems, srcs, lands, scatter, chips_only = handles
    n = len(srcs)

    def body(*refs):
        src_refs, land_refs = refs[:n], refs[n:2 * n]
        send_ref, recv_ref, own_ref = refs[2 * n:2 * n + 3]
        me, peers = _peers(chips_only)
        for k in range(len(peers)):
            for a in range(n):
                _split_copy(src_refs, land_refs, send_ref, recv_ref, scatter, a, k, me, peers).wait_send()
                _split_copy(src_refs, land_refs, send_ref, recv_ref, scatter, a, k, me, peers, True).wait_recv()
        for a in range(n):
            _own_copy(src_refs, land_refs, own_ref, scatter, a, me).wait()

    outs = pl.pallas_call(
        body, name=name,
        out_shape=tuple(pltpu.HBM(t.shape, t.dtype) for t in (*srcs, *lands)),
        in_specs=[HBM] * (2 * n) + [SEM, SEM, SEM, pl.BlockSpec(memory_space=pl.ANY)],
        out_specs=tuple([HBM] * (2 * n)),
        input_output_aliases={i: i for i in range(2 * n)},
        compiler_params=pltpu.CompilerParams(has_side_effects=DATAFLOW),
    )(*srcs, *lands, *sems, after)
    return outs[n:]


def _adamw(parts, w, m, v, *, name, tm):
    r, c = w.shape
    assert r % tm == 0

    def body(p_ref, w_ref, m_ref, v_ref, g_ref, d_ref, nm_ref, nv_ref):
        _adamw_update(p_ref, w_ref, m_ref, v_ref, g_ref, d_ref, nm_ref, nv_ref)

    blk = pl.BlockSpec((tm, c), lambda i: (i, 0))
    return pl.pallas_call(
        body, name=name, grid=(r // tm,),
        in_specs=[pl.BlockSpec((parts.shape[0], tm, c), lambda i: (0, i, 0)), blk, blk, blk],
        out_specs=[blk] * 4, out_shape=[_sds((r, c), F32)] * 4,
        compiler_params=_params("parallel"),
    )(parts, w, m, v)


def _adamw_update(p_ref, w_ref, m_ref, v_ref, g_ref, d_ref, nm_ref, nv_ref):
    g = p_ref[0].astype(F32)
    for s in range(1, p_ref.shape[0]):
        g = g + p_ref[s].astype(F32)
    g_ref[...] = g
    m_new = ADAM_B1 * m_ref[...] + (1.0 - ADAM_B1) * g
    v_new = ADAM_B2 * v_ref[...] + (1.0 - ADAM_B2) * (g * g)
    nm_ref[...] = m_new
    nv_ref[...] = v_new
    m_hat = m_new / (1.0 - ADAM_B1 ** ADAM_STEP)
    v_hat = v_new / (1.0 - ADAM_B2 ** ADAM_STEP)
    d_ref[...] = -ADAM_LR * (m_hat / (jnp.sqrt(v_hat) + ADAM_EPS) + ADAM_WD * w_ref[...])


SMALL = ("g_pre_mix", "b_forget", "g_post_mix", "g_pre_ffn", "conv_b", "g_post_ffn")


def _adamw_small(parts, ws, ms, vs, sq_err_parts):
    n = len(ws)

    def body(*refs):
        ins, sq_ref, outs, loss_ref = refs[:4 * n], refs[4 * n], refs[4 * n + 1:-1], refs[-1]
        for i in range(n):
            _adamw_update(ins[i], ins[n + i], ins[2 * n + i], ins[3 * n + i], *outs[4 * i:4 * i + 4])
        total = sq_ref[0]
        for s in range(1, N_DEV):
            total = total + sq_ref[s]
        loss_ref[...] = total * (0.5 / D_MODEL)

    res = pl.pallas_call(
        body, name="adamw_small",
        out_shape=[_sds(w.shape, F32) for w in ws for _ in range(4)] + [_sds((1, LANE), F32)],
        compiler_params=pltpu.CompilerParams(vmem_limit_bytes=VMEM_LIMIT),
    )(*parts, *ws, *ms, *vs, sq_err_parts)
    return [res[4 * i:4 * i + 4] for i in range(n)], res[-1][0, 0]


def kernel(x, g_pre_mix, w_in, b_forget, w_o_fox, w_o_dil, w_out, g_post_mix, g_pre_ffn, w_up, conv_w, conv_b, w_down, g_post_ffn, loss_target, m_g_pre_mix, m_w_in, m_b_forget, m_w_o_fox, m_w_o_dil, m_w_out, m_g_post_mix, m_g_pre_ffn, m_w_up, m_conv_w, m_conv_b, m_w_down, m_g_post_ffn, v_g_pre_mix, v_w_in, v_b_forget, v_w_o_fox, v_w_o_dil, v_w_out, v_g_post_mix, v_g_pre_ffn, v_w_up, v_conv_w, v_conv_b, v_w_down, v_g_post_ffn):
    names = ("g_pre_mix", "w_in", "b_forget", "w_o_fox", "w_o_dil", "w_out", "g_post_mix", "g_pre_ffn",
             "w_up", "conv_w", "conv_b", "w_down", "g_post_ffn")
    w = dict(g_pre_mix=g_pre_mix, w_in=w_in, b_forget=b_forget, w_o_fox=w_o_fox, w_o_dil=w_o_dil, w_out=w_out,
             g_post_mix=g_post_mix, g_pre_ffn=g_pre_ffn, w_up=w_up, conv_w=conv_w, conv_b=conv_b, w_down=w_down,
             g_post_ffn=g_post_ffn)
    m = dict(g_pre_mix=m_g_pre_mix, w_in=m_w_in, b_forget=m_b_forget, w_o_fox=m_w_o_fox, w_o_dil=m_w_o_dil,
             w_out=m_w_out, g_post_mix=m_g_post_mix, g_pre_ffn=m_g_pre_ffn, w_up=m_w_up, conv_w=m_conv_w,
             conv_b=m_conv_b, w_down=m_w_down, g_post_ffn=m_g_post_ffn)
    v = dict(g_pre_mix=v_g_pre_mix, w_in=v_w_in, b_forget=v_b_forget, w_o_fox=v_w_o_fox, w_o_dil=v_w_o_dil,
             w_out=v_w_out, g_post_mix=v_g_post_mix, g_pre_ffn=v_g_pre_ffn, w_up=v_w_up, conv_w=v_conv_w,
             conv_b=v_conv_b, w_down=v_w_down, g_post_ffn=v_g_post_ffn)
    sharded = ("w_in", "w_o_fox", "w_o_dil", "w_out", "w_up", "w_down", "conv_w")
    wire = lambda n: F32 if n == "conv_w" else BF16

    by_cols = lambda t: jnp.transpose(t, (1, 0, 2)).reshape(t.shape[1], N_DEV * t.shape[2])
    by_rows = lambda t: t.reshape(N_DEV * t.shape[1], t.shape[2])
    col_slots = lambda t: jnp.transpose(t.reshape(t.shape[0], N_DEV, t.shape[1] // N_DEV), (1, 0, 2))
    row_slots = lambda t: t.reshape(N_DEV, t.shape[0] // N_DEV, t.shape[1])
    to_slots = lambda n, t: (row_slots if n in ("w_out", "w_down") else col_slots)(t).astype(wire(n))
    shard = lambda n: w[n][0].astype(wire(n))

    w_main, w_f = _w_in_from_shards(_gather_two_level(shard("w_in"), name="gather_w_in"))
    proj_handles, proj_tok = _exchange_start(
        [shard("w_o_fox"), shard("w_o_dil"), shard("w_out")], False, name="gather_proj_start", after=w_f)
    ffn_handles, ffn_tok = _exchange_start(
        [shard("w_up"), shard("conv_w"), shard("w_down")], False, name="gather_ffn_start", after=proj_tok)

    def proj_weights(after):
        w_oa, w_ob, w_o = _exchange_wait(proj_handles, after, name="gather_proj_wait")
        return by_cols(w_oa), by_cols(w_ob), by_rows(w_o)

    def ffn_weights(after):
        w_u, conv, w_d = _exchange_wait(ffn_handles, after, name="gather_ffn_wait")
        return _w_up_from_shards(w_u), _ffn_interleave(by_cols(conv)), by_rows(w_d)

    pending = {}

    def ffn_grads_ready(g):
        slots = [to_slots("w_down", g["w_down"]), _w_up_to_shards(g["w_up_blocks"]), to_slots("conv_w", g["conv_w"])]
        pending["ffn"] = _exchange_start(slots, True, name="scatter_ffn_start")
        return pending["ffn"][1]

    def proj_grads_ready(g):
        pending["proj"] = _exchange_start([g["w_o_fox"], g["w_o_dil"], to_slots("w_out", g["w_out"])], True,
                                          name="scatter_proj_start")
        return pending["proj"][1]

    def mixer_grads_ready(g):
        slots = _w_in_to_shards(g["w_main"], g["w_f"])
        theirs = _sibling_swap([slots], name="scatter_w_in_swap")[0]
        chip_sums = _pair_sum(slots, theirs, name="scatter_w_in_pair_sum", tn=W_IN_SHARD)
        pending["w_in"] = _exchange_start([chip_sums], True, name="scatter_w_in_start", chips_only=True)
        return pending["w_in"][1]

    sq_err, grad_x, g = _local_step(
        x[0], loss_target[0], w_main, w_f, b_forget, conv_b, g_pre_mix, g_post_mix, g_pre_ffn,
        g_post_ffn, proj_weights, ffn_weights, ffn_grads_ready, proj_grads_ready, mixer_grads_ready, after=ffn_tok)

    small_handles, small_tok = _exchange_start([g[n] for n in SMALL] + [sq_err], False, name="gather_small_start")
    tiles = dict(w_in=256, w_o_fox=512, w_o_dil=512, w_out=128, w_up=256, w_down=176, conv_w=3)
    adam = lambda n, p: _adamw(p, w[n][0], m[n][0], v[n][0], name=f"adamw_{n}", tm=tiles[n])
    res = {}
    for key, group in (("ffn", ("w_down", "w_up", "conv_w")), ("proj", ("w_o_fox", "w_o_dil", "w_out"))):
        landed = _exchange_wait(pending[key][0], small_tok, name=f"scatter_{key}_wait")
        res.update({n: adam(n, p) for n, p in zip(group, landed)})
    done = res["w_up"][3]
    res["w_in"] = adam("w_in", _exchange_wait(pending["w_in"][0], done, name="scatter_w_in_wait")[0])
    small_parts = _exchange_wait(small_handles, res["w_in"][3], name="gather_small_wait")
    small, loss = _adamw_small(small_parts[:-1], *[[t[n] for n in SMALL] for t in (w, m, v)], small_parts[-1])
    small = dict(zip(SMALL, small))
    out = [[(res[n][k][None] if n in sharded else small[n][k]) for n in names] for k in range(4)]
    return (loss, grad_x[None], *out[0], *out[1], *out[2], *out[3])
```

```python
import functools
import math

import jax
import jax.numpy as jnp
import numpy as np
from jax import lax
from jax.experimental import pallas as pl
from jax.experimental.pallas import tpu as pltpu

F32 = jnp.float32
BF16 = jnp.bfloat16

SEQ = 4096
D_MODEL = 1024
N_HEADS = 8
HEAD_DIM = 64
ATT_W = N_HEADS * HEAD_DIM
D_FF = 2816
Z_MAIN = 5120
F_PAD = 128
ROPE_DIM = 16
ROPE_THETA = 500000.0
RMS_EPS = 1e-6
NEG_INF = -1e30
SCALE = 1.0 / math.sqrt(HEAD_DIM)
DIL_PATTERNS = ((128, 1), (512, 4), (2048, 16))
DIL_BLK = 128
DIL_STEP_BLOCKS = 2
N_DEV = 8

ADAM_LR = 0.001
ADAM_B1 = 0.9
ADAM_B2 = 0.999
ADAM_EPS = 1e-08
ADAM_WD = 0.01
ADAM_STEP = 10

LANE = 128
SUBLANE = 8
VMEM_LIMIT = 56 * 1024 * 1024
MESH_ID = pl.DeviceIdType.MESH
ANY = pl.BlockSpec(memory_space=pl.ANY)


def _params(*sem):
    return pltpu.CompilerParams(dimension_semantics=sem, vmem_limit_bytes=VMEM_LIMIT)


def _sds(shape, dtype):
    return jax.ShapeDtypeStruct(shape, dtype)


def _also(after):
    return [] if after is None else [after]


def _matmul(a, b, *, ta=False, tb=False, out_dtype, tm, tn, tk, name, b_k_off=0, after=None, col_slots=1):
    n_after = len(_also(after))
    if ta:
        kk, m = a.shape
    else:
        m, kk = a.shape
    n = b.shape[0] if tb else b.shape[1]
    tm, tn, tk = min(tm, m), min(tn, n), min(tk, kk)
    assert (b.shape[1] if tb else b.shape[0]) >= b_k_off * tk + kk
    assert m % tm == 0 and n % tn == 0 and kk % tk == 0, (name, m, n, kk, tm, tn, tk)
    nk = kk // tk
    dims = (((0 if ta else 1,), (1 if tb else 0,)), ((), ()))
    slot_w = n // col_slots
    assert col_slots == 1 or (nk == 1 and tn == n and slot_w % LANE == 0), name

    def body(a_ref, b_ref, *rest):
        o_ref, scratch = rest[n_after], rest[n_after + 1:]
        p = lax.dot_general(a_ref[...].astype(BF16), b_ref[...].astype(BF16), dims,
                            preferred_element_type=F32)
        if col_slots > 1:
            for s in range(col_slots):
                o_ref[s] = p[:, s * slot_w:(s + 1) * slot_w].astype(o_ref.dtype)
        elif nk == 1:
            o_ref[...] = p.astype(o_ref.dtype)
        else:
            acc = scratch[0]
            k = pl.program_id(2)

            @pl.when(k == 0)
            def _():
                acc[...] = p

            @pl.when(k > 0)
            def _():
                acc[...] += p

            @pl.when(k == nk - 1)
            def _():
                o_ref[...] = acc[...].astype(o_ref.dtype)

    a_spec = (pl.BlockSpec((tk, tm), lambda i, j, k: (k, i)) if ta
              else pl.BlockSpec((tm, tk), lambda i, j, k: (i, k)))
    b_mode = dict(pipeline_mode=pl.Buffered(1)) if n == tn and nk == 1 else {}
    b_spec = (pl.BlockSpec((tn, tk), lambda i, j, k: (j, k + b_k_off), **b_mode) if tb
              else pl.BlockSpec((tk, tn), lambda i, j, k: (k + b_k_off, j), **b_mode))
    return pl.pallas_call(
        body, name=name, grid=(m // tm, n // tn, nk),
        in_specs=[a_spec, b_spec] + [ANY] * n_after,
        out_specs=(pl.BlockSpec((tm, tn), lambda i, j, k: (i, j)) if col_slots == 1
                   else pl.BlockSpec((col_slots, tm, slot_w), lambda i, j, k: (0, i, 0))),
        out_shape=_sds((m, n) if col_slots == 1 else (col_slots, m, slot_w), out_dtype),
        scratch_shapes=[pltpu.VMEM((tm, tn), F32)] if nk > 1 else [],
        compiler_params=_params("parallel", "parallel", "arbitrary"),
    )(a, b, *_also(after))


def _rms_fwd(x, g, *, name, tm=512, after=None):
    def body(x_ref, g_ref, *rest):
        h_ref = rest[-1]
        xv = x_ref[...]
        r = lax.rsqrt(jnp.mean(xv * xv, axis=-1, keepdims=True) + RMS_EPS)
        h_ref[...] = (xv * r * g_ref[...]).astype(h_ref.dtype)

    return pl.pallas_call(
        body, name=name, grid=(SEQ // tm,),
        in_specs=[pl.BlockSpec((tm, D_MODEL), lambda i: (i, 0)), pl.BlockSpec((1, D_MODEL), lambda i: (0, 0))]
        + [ANY] * len(_also(after)),
        out_specs=pl.BlockSpec((tm, D_MODEL), lambda i: (i, 0)),
        out_shape=_sds((SEQ, D_MODEL), BF16),
        compiler_params=_params("parallel"),
    )(x, g, *_also(after))


def _rms_bwd(dh_parts, xin, g, dres, *, out_dtype, name, tm=512):
    n_parts = len(dh_parts)
    has_res = dres is not None

    def body(*refs):
        parts = refs[:n_parts]
        x_ref, g_ref = refs[n_parts], refs[n_parts + 1]
        res_ref = refs[n_parts + 2] if has_res else None
        o_ref, gg_ref = refs[-2], refs[-1]
        dh = parts[0][...].astype(F32)
        for p in parts[1:]:
            dh = dh + p[...].astype(F32)
        xv = x_ref[...]
        r = lax.rsqrt(jnp.mean(xv * xv, axis=-1, keepdims=True) + RMS_EPS)
        xn = xv * r

        @pl.when(pl.program_id(0) == 0)
        def _():
            gg_ref[...] = jnp.zeros_like(gg_ref)

        gg_ref[...] += jnp.sum(dh * xn, axis=0, keepdims=True)
        dxn = dh * g_ref[...]
        dx = r * (dxn - xn * jnp.mean(dxn * xn, axis=-1, keepdims=True))
        if has_res:
            dx = dx + res_ref[...]
        o_ref[...] = dx.astype(o_ref.dtype)

    row = pl.BlockSpec((tm, D_MODEL), lambda i: (i, 0))
    vec = pl.BlockSpec((1, D_MODEL), lambda i: (0, 0))
    args = list(dh_parts) + [xin, g] + ([dres] if has_res else [])
    return pl.pallas_call(
        body, name=name, grid=(SEQ // tm,),
        in_specs=[row] * n_parts + [row, vec] + ([row] if has_res else []),
        out_specs=[row, vec],
        out_shape=[_sds((SEQ, D_MODEL), out_dtype), _sds((1, D_MODEL), F32)],
        compiler_params=_params("arbitrary"),
    )(*args)


def _rms_pair_bwd(dh_parts, x2, g_pre, dres, y1, g_post, *, tm=512, after=None):
    n_parts = len(dh_parts)

    def norm_bwd(dh, xin, g_ref, gg_ref):
        r = lax.rsqrt(jnp.mean(xin * xin, axis=-1, keepdims=True) + RMS_EPS)
        xn = xin * r
        gg_ref[...] += jnp.sum(dh * xn, axis=0, keepdims=True)
        dxn = dh * g_ref[...]
        return r * (dxn - xn * jnp.mean(dxn * xn, axis=-1, keepdims=True))

    def body(*refs):
        parts = refs[:n_parts]
        x2_ref, gpre_ref, res_ref, y1_ref, gpost_ref = refs[n_parts:n_parts + 5]
        dx2_ref, dy1_ref, ggpre_ref, ggpost_ref = refs[-4:]

        @pl.when(pl.program_id(0) == 0)
        def _():
            ggpre_ref[...] = jnp.zeros_like(ggpre_ref)
            ggpost_ref[...] = jnp.zeros_like(ggpost_ref)

        dh = parts[0][...].astype(F32)
        for p in parts[1:]:
            dh = dh + p[...].astype(F32)
        dx2 = res_ref[...] + norm_bwd(dh, x2_ref[...], gpre_ref, ggpre_ref)
        dx2_ref[...] = dx2
        dy1_ref[...] = norm_bwd(dx2, y1_ref[...], gpost_ref, ggpost_ref).astype(dy1_ref.dtype)

    row = pl.BlockSpec((tm, D_MODEL), lambda i: (i, 0))
    vec = pl.BlockSpec((1, D_MODEL), lambda i: (0, 0))
    return pl.pallas_call(
        body, name="rms_pair_bwd", grid=(SEQ // tm,),
        in_specs=[row] * n_parts + [row, vec, row, row, vec] + [ANY] * len(_also(after)),
        out_specs=[row, row, vec, vec],
        out_shape=[_sds((SEQ, D_MODEL), F32), _sds((SEQ, D_MODEL), BF16), _sds((1, D_MODEL), F32),
                   _sds((1, D_MODEL), F32)],
        compiler_params=_params("arbitrary"),
    )(*dh_parts, x2, g_pre, dres, y1, g_post, *_also(after))


SCAN_BLK = 512


def _split_dot(v, tri):
    hi = v.astype(BF16)
    r1 = v - hi.astype(F32)
    mid = r1.astype(BF16)
    lo = (r1 - mid.astype(F32)).astype(BF16)
    dot = functools.partial(jnp.dot, preferred_element_type=F32)
    return dot(hi, tri) + dot(mid, tri) + dot(lo, tri)


def _fox_prep(fa_t, b_col):
    nblk = SEQ // SCAN_BLK

    def body(fa_ref, b_ref, f_ref, sg_ref):
        row = lax.broadcasted_iota(jnp.int32, (SCAN_BLK, SCAN_BLK), 0)
        col = lax.broadcasted_iota(jnp.int32, (SCAN_BLK, SCAN_BLK), 1)
        upper = (row <= col).astype(BF16)
        carry = jnp.zeros((N_HEADS, 1), F32)
        for blk in range(nblk):
            sl = pl.ds(blk * SCAN_BLK, SCAN_BLK)
            xx = fa_ref[:, sl] + b_ref[...]
            e = jnp.exp(-jnp.abs(xx))
            logf = jnp.minimum(xx, 0.0) - jnp.log(1.0 + e)
            sg_ref[:, sl] = jnp.where(xx >= 0.0, e, 1.0) / (1.0 + e)
            c = _split_dot(logf, upper) + carry
            f_ref[:, sl] = c
            carry = c[:, SCAN_BLK - 1:SCAN_BLK]

    return pl.pallas_call(
        body, name="fox_prep",
        out_shape=[_sds((N_HEADS, SEQ), F32), _sds((N_HEADS, SEQ), F32)],
        compiler_params=pltpu.CompilerParams(vmem_limit_bytes=VMEM_LIMIT),
    )(fa_t, b_col)


def _fox_post_bwd(df_t, sg_t):
    nblk = SEQ // SCAN_BLK

    def body(df_ref, sg_ref, dfa_ref, gb_ref):
        row = lax.broadcasted_iota(jnp.int32, (SCAN_BLK, SCAN_BLK), 0)
        col = lax.broadcasted_iota(jnp.int32, (SCAN_BLK, SCAN_BLK), 1)
        lower = (row >= col).astype(BF16)
        carry = jnp.zeros((N_HEADS, 1), F32)
        gb = jnp.zeros((N_HEADS, 1), F32)
        for blk in reversed(range(nblk)):
            sl = pl.ds(blk * SCAN_BLK, SCAN_BLK)
            c = _split_dot(df_ref[:, sl], lower) + carry
            carry = c[:, 0:1]
            dfa = c * sg_ref[:, sl]
            dfa_ref[:, sl] = dfa
            gb = gb + jnp.sum(dfa, axis=1, keepdims=True)
        gb_ref[...] = gb

    return pl.pallas_call(
        body, name="fox_post_bwd",
        out_shape=[_sds((N_HEADS, SEQ), F32), _sds((N_HEADS, 1), F32)],
        compiler_params=pltpu.CompilerParams(vmem_limit_bytes=VMEM_LIMIT),
    )(df_t, sg_t)


FOX_T = 512
NT_DIMS = (((1,), (1,)), ((), ()))
TN_DIMS = (((0,), (0,)), ((), ()))


def _head(ref_or_val, h):
    return ref_or_val[:, h * HEAD_DIM:(h + 1) * HEAD_DIM]


def _split3(v):
    hi = v.astype(BF16).astype(F32)
    r1 = v - hi
    mid = r1.astype(BF16).astype(F32)
    return hi, mid, (r1 - mid).astype(BF16).astype(F32)


ONE_LANE = 3 * N_HEADS


def _pack_terms(v, with_one):
    hi, mid, lo = _split3(v)
    t = hi + pltpu.roll(mid, N_HEADS, 1) + pltpu.roll(lo, 2 * N_HEADS, 1)
    if with_one:
        t = t + (lax.broadcasted_iota(jnp.int32, v.shape, 1) == ONE_LANE).astype(F32)
    return t.astype(BF16)


def _aux_matrices():
    to_q = np.zeros((LANE, N_HEADS * 2 * HEAD_DIM), np.float32)
    to_k = np.zeros_like(to_q)
    for h in range(N_HEADS):
        base = h * 2 * HEAD_DIM + HEAD_DIM
        for s in range(3):
            to_q[s * N_HEADS + h, base + s] = 1.0
            to_q[ONE_LANE, base + 3 + s] = 1.0
            to_k[ONE_LANE, base + s] = 1.0
            to_k[s * N_HEADS + h, base + 3 + s] = -1.0
    return jnp.asarray(to_q, BF16), jnp.asarray(to_k, BF16)


def _head_sums():
    total = np.zeros((N_HEADS * HEAD_DIM, LANE), np.float32)
    first = np.zeros_like(total)
    for h in range(N_HEADS):
        total[h * HEAD_DIM:(h + 1) * HEAD_DIM, h] = 1.0
        first[h * HEAD_DIM, h] = 1.0
    return jnp.asarray(total, BF16), jnp.asarray(first, BF16)


SLOT = 2 * HEAD_DIM
N_SPLIT = 3
FOX_FWD_HEADS = 8
FOX_BWD_HEADS = 4


def _slot(ref, h):
    return ref[:, h * SLOT:(h + 1) * SLOT]


def _fox_pack_fwd(zm, f_cols, *, tm=512):
    def body(q_ref, k_ref, v_ref, f_ref, tq_ref, tk_ref, qs_ref, ks_ref, vs_ref):
        ones = jnp.ones((tm, HEAD_DIM), BF16)
        terms = _pack_terms(f_ref[...], True)
        q_aux = jnp.dot(terms, tq_ref[...], preferred_element_type=F32).astype(BF16)
        k_aux = jnp.dot(terms, tk_ref[...], preferred_element_type=F32).astype(BF16)
        for h in range(N_HEADS):
            aux = slice(h * SLOT + HEAD_DIM, (h + 1) * SLOT)
            qs_ref[:, h * SLOT:(h + 1) * SLOT] = jnp.concatenate(
                [(_head(q_ref, h).astype(F32) * SCALE).astype(BF16), q_aux[:, aux]], axis=1)
            ks_ref[:, h * SLOT:(h + 1) * SLOT] = jnp.concatenate([_head(k_ref, h), k_aux[:, aux]], axis=1)
            vs_ref[:, h * SLOT:(h + 1) * SLOT] = jnp.concatenate([_head(v_ref, h), ones], axis=1)

    col = lambda b: pl.BlockSpec((tm, ATT_W), lambda i: (i, b))
    wide = pl.BlockSpec((tm, N_HEADS * SLOT), lambda i: (i, 0))
    const = pl.BlockSpec((LANE, N_HEADS * SLOT), lambda i: (0, 0))
    return pl.pallas_call(
        body, name="fox_pack_fwd", grid=(SEQ // tm,),
        in_specs=[col(0), col(1), col(2), pl.BlockSpec((tm, LANE), lambda i: (i, 0)), const, const],
        out_specs=[wide] * 3, out_shape=[_sds((SEQ, N_HEADS * SLOT), BF16)] * 3,
        compiler_params=_params("parallel"),
    )(zm, zm, zm, f_cols, *_aux_matrices())


def _fox_pack_bwd(zm, f_cols, lse, o, do, *, tm=512, after=None):
    def body(q_ref, f_ref, lse_ref, o_ref, do_ref, tq_ref, total_ref, first_ref, *rest):
        qs_ref, ds_ref = rest[-2:]
        delta = _split_dot(o_ref[...].astype(F32) * do_ref[...].astype(F32), total_ref[...])
        lse_h = _split_dot(lse_ref[...], first_ref[...])
        q_aux = jnp.dot(_pack_terms(f_ref[...] - lse_h, True), tq_ref[...], preferred_element_type=F32).astype(BF16)
        d_aux = jnp.dot(_pack_terms(-delta, False), tq_ref[...], preferred_element_type=F32).astype(BF16)
        for h in range(N_HEADS):
            aux = slice(h * SLOT + HEAD_DIM, (h + 1) * SLOT)
            qs_ref[:, h * SLOT:(h + 1) * SLOT] = jnp.concatenate(
                [(_head(q_ref, h).astype(F32) * SCALE).astype(BF16), q_aux[:, aux]], axis=1)
            ds_ref[:, h * SLOT:(h + 1) * SLOT] = jnp.concatenate([_head(do_ref, h), d_aux[:, aux]], axis=1)

    row = pl.BlockSpec((tm, ATT_W), lambda i: (i, 0))
    wide = pl.BlockSpec((tm, N_HEADS * SLOT), lambda i: (i, 0))
    const = lambda r, c: pl.BlockSpec((r, c), lambda i: (0, 0))
    return pl.pallas_call(
        body, name="fox_pack_bwd", grid=(SEQ // tm,),
        in_specs=[row, pl.BlockSpec((tm, LANE), lambda i: (i, 0)), row, row, row,
                  const(LANE, N_HEADS * SLOT), const(ATT_W, LANE), const(ATT_W, LANE)] + [ANY] * len(_also(after)),
        out_specs=[wide] * 2, out_shape=[_sds((SEQ, N_HEADS * SLOT), BF16)] * 2,
        compiler_params=_params("parallel"),
    )(zm, f_cols, lse, o, do, _aux_matrices()[0], *_head_sums(), *_also(after))


def _causal_pairs(key_major):
    nb = SEQ // FOX_T
    if key_major:
        pairs = [(i, j) for j in range(nb) for i in range(j, nb)]
    else:
        pairs = [(i, j) for i in range(nb) for j in range(i + 1)]
    return (jnp.array([p[0] for p in pairs], jnp.int32), jnp.array([p[1] for p in pairs], jnp.int32), len(pairs))


FOX_HALF = FOX_T // 2
FOX_FULL = ((slice(0, FOX_T), slice(0, FOX_T), None),)
FOX_DIAG = ((slice(0, FOX_HALF), slice(0, FOX_HALF), 0), (slice(FOX_HALF, FOX_T), slice(0, FOX_T), FOX_HALF))


def _causal_piece_mask(q_rows, k_rows, offset):
    shape = (q_rows.stop - q_rows.start, k_rows.stop - k_rows.start)
    row = lax.broadcasted_iota(jnp.int32, shape, 0)
    col = lax.broadcasted_iota(jnp.int32, shape, 1)
    return col <= row + offset


def _fox_fwd(q_slots, k_slots, v_slots):
    i_tab, j_tab, n_pairs = _causal_pairs(False)

    def body(i_tab, j_tab, q_ref, k_ref, v_ref, o_ref, lse_ref, m_s, acc_s):
        t = pl.program_id(1)
        i, j = i_tab[t], j_tab[t]

        @pl.when(j == 0)
        def _():
            m_s[...] = jnp.full_like(m_s, NEG_INF)
            acc_s[...] = jnp.zeros_like(acc_s)

        def step(pieces):
            jobs = [(h, piece) for h in range(FOX_FWD_HEADS) for piece in pieces]
            lanes = lambda h: slice(h * SLOT, (h + 1) * SLOT)
            scores = [lax.dot_general(q_ref[qr, lanes(h)], k_ref[kr, lanes(h)], NT_DIMS, preferred_element_type=F32)
                      for h, (qr, kr, _) in jobs]
            probs, alphas = [], []
            for idx, (h, (qr, kr, offset)) in enumerate(jobs):
                s = scores[idx]
                if offset is not None:
                    s = jnp.where(_causal_piece_mask(qr, kr, offset), s, NEG_INF)
                m_prev = m_s[h, qr, :]
                m_new = jnp.maximum(m_prev, jnp.max(s, axis=-1, keepdims=True))
                probs.append(jnp.exp(s - jnp.tile(m_new, (1, s.shape[1] // LANE))).astype(BF16))
                alphas.append(jnp.exp(m_prev - m_new))
                m_s[h, qr, :] = m_new
            for idx, (h, (qr, kr, _)) in enumerate(jobs):
                acc_s[h, qr, :] = alphas[idx] * acc_s[h, qr, :] + jnp.dot(
                    probs[idx], v_ref[kr, lanes(h)], preferred_element_type=F32)

        @pl.when(j < i)
        def _():
            step(FOX_FULL)

        @pl.when(j == i)
        def _():
            step(FOX_DIAG)
            outs, lses = [], []
            for h in range(FOX_FWD_HEADS):
                acc = acc_s[h]
                l = acc[:, HEAD_DIM:]
                outs.append(acc[:, :HEAD_DIM] / l)
                lses.append(m_s[h][:, :HEAD_DIM] + jnp.log(l))
            o_ref[...] = jnp.concatenate(outs, axis=1).astype(o_ref.dtype)
            lse_ref[...] = jnp.concatenate(lses, axis=1)

    qspec = pl.BlockSpec((FOX_T, FOX_FWD_HEADS * SLOT), lambda p, t, it, jt: (it[t], p))
    kspec = pl.BlockSpec((FOX_T, FOX_FWD_HEADS * SLOT), lambda p, t, it, jt: (jt[t], p))
    ospec = pl.BlockSpec((FOX_T, FOX_FWD_HEADS * HEAD_DIM), lambda p, t, it, jt: (it[t], p))
    return pl.pallas_call(
        body, name="fox_fwd",
        grid_spec=pltpu.PrefetchScalarGridSpec(
            num_scalar_prefetch=2, grid=(N_HEADS // FOX_FWD_HEADS, n_pairs),
            in_specs=[qspec, kspec, kspec], out_specs=[ospec, ospec],
            scratch_shapes=[pltpu.VMEM((FOX_FWD_HEADS, FOX_T, LANE), F32),
                            pltpu.VMEM((FOX_FWD_HEADS, FOX_T, SLOT), F32)]),
        out_shape=[_sds((SEQ, ATT_W), BF16), _sds((SEQ, ATT_W), F32)],
        compiler_params=_params("parallel", "arbitrary"),
    )(i_tab, j_tab, q_slots, k_slots, v_slots)


def _fox_bwd(q_slots, k_slots, v_slots, do_slots):
    i_tab, j_tab, n_pairs = _causal_pairs(True)

    def body(i_tab, j_tab, q_ref, k_ref, v_ref, do_ref, dq_ref, dk_ref, dv_ref):
        t = pl.program_id(1)
        i, j = i_tab[t], j_tab[t]

        @pl.when(t == 0)
        def _():
            dq_ref[...] = jnp.zeros_like(dq_ref)

        @pl.when(i == j)
        def _():
            dk_ref[...] = jnp.zeros_like(dk_ref)
            dv_ref[...] = jnp.zeros_like(dv_ref)

        def step(pieces):
            jobs = [(h, piece) for h in range(FOX_BWD_HEADS) for piece in pieces]
            lanes = lambda h: slice(h * SLOT, (h + 1) * SLOT)
            scores = [lax.dot_general(q_ref[qr, lanes(h)], k_ref[kr, lanes(h)], NT_DIMS, preferred_element_type=F32)
                      for h, (qr, kr, _) in jobs]
            dps = [lax.dot_general(do_ref[qr, lanes(h)], v_ref[kr, lanes(h)], NT_DIMS, preferred_element_type=F32)
                   for h, (qr, kr, _) in jobs]
            ps, dss = [], []
            for idx, (h, (qr, kr, offset)) in enumerate(jobs):
                p = jnp.exp(scores[idx])
                if offset is not None:
                    p = jnp.where(_causal_piece_mask(qr, kr, offset), p, 0.0)
                ps.append(p.astype(BF16))
                dss.append((p * dps[idx]).astype(BF16))
            for idx, (h, (qr, kr, _)) in enumerate(jobs):
                rows = pl.ds(pl.multiple_of(i * FOX_T + qr.start, FOX_HALF), qr.stop - qr.start)
                dv_ref[kr, lanes(h)] += lax.dot_general(ps[idx], do_ref[qr, lanes(h)], TN_DIMS,
                                                        preferred_element_type=F32)
                dk_ref[kr, lanes(h)] += lax.dot_general(dss[idx], q_ref[qr, lanes(h)], TN_DIMS,
                                                        preferred_element_type=F32)
                dq_ref[rows, lanes(h)] += jnp.dot(dss[idx], k_ref[kr, lanes(h)], preferred_element_type=F32)

        @pl.when(i > j)
        def _():
            step(FOX_FULL)

        @pl.when(i == j)
        def _():
            step(FOX_DIAG)

    qspec = pl.BlockSpec((FOX_T, FOX_BWD_HEADS * SLOT), lambda p, t, it, jt: (it[t], p))
    kspec = pl.BlockSpec((FOX_T, FOX_BWD_HEADS * SLOT), lambda p, t, it, jt: (jt[t], p))
    return pl.pallas_call(
        body, name="fox_bwd",
        grid_spec=pltpu.PrefetchScalarGridSpec(
            num_scalar_prefetch=2, grid=(N_HEADS // FOX_BWD_HEADS, n_pairs),
            in_specs=[qspec, kspec, kspec, qspec],
            out_specs=[pl.BlockSpec((SEQ, FOX_BWD_HEADS * SLOT), lambda p, t, it, jt: (0, p)), kspec, kspec]),
        out_shape=[_sds((SEQ, N_HEADS * SLOT), F32)] * 3,
        compiler_params=_params("arbitrary", "arbitrary"),
    )(i_tab, j_tab, q_slots, k_slots, v_slots, do_slots)


def _fox_unpack(dq_slots, dk_slots, dv_slots, dz, *, tm=512):
    def body(dq_ref, dk_ref, dv_ref, dz_in, o_ref, df_ref):
        lane = lax.broadcasted_iota(jnp.int32, (tm, LANE), 1)
        df = jnp.zeros((tm, LANE), F32)
        for h in range(N_HEADS):
            lo = h * SLOT
            for part, (ref, mult) in enumerate(((dq_ref, SCALE), (dk_ref, 1.0), (dv_ref, 1.0))):
                o_ref[:, part * ATT_W + h * HEAD_DIM:part * ATT_W + (h + 1) * HEAD_DIM] = (
                    ref[:, lo:lo + HEAD_DIM] * mult).astype(o_ref.dtype)
            rows = dq_ref[:, lo + HEAD_DIM:lo + HEAD_DIM + 1]
            cols = dk_ref[:, lo + HEAD_DIM + N_SPLIT:lo + HEAD_DIM + N_SPLIT + 1]
            df = jnp.where(lane == h, rows - cols, df)
        df_ref[...] = df

    wide = pl.BlockSpec((tm, N_HEADS * SLOT), lambda i: (i, 0))
    return pl.pallas_call(
        body, name="fox_unpack", grid=(SEQ // tm,), in_specs=[wide] * 3 + [ANY],
        out_specs=[pl.BlockSpec((tm, 3 * ATT_W), lambda i: (i, 0)), pl.BlockSpec((tm, LANE), lambda i: (i, 0))],
        out_shape=[_sds((SEQ, Z_MAIN), BF16), _sds((SEQ, LANE), F32)],
        input_output_aliases={3: 0},
        compiler_params=_params("parallel"),
    )(dq_slots, dk_slots, dv_slots, dz)


def _dil_bwd_prep(o, do, lse, *, tm=512):
    dilations = [d for _, d in DIL_PATTERNS]
    o_chunks = ATT_W // LANE

    def body(o_ref, do_ref, lse_ref, *rest):
        outs, (do_scr, lse_scr, dl_scr) = rest[:-3], rest[-3:]
        dov = do_ref[...].astype(F32)
        prod = o_ref[...].astype(F32) * dov
        lane = lax.broadcasted_iota(jnp.int32, (tm, LANE), 1)
        delta = jnp.zeros((tm, LANE), F32)
        for h in range(N_HEADS):
            delta = jnp.where(lane == h, jnp.sum(_head(prod, h), axis=1, keepdims=True), delta)
        for ch in range(o_chunks):
            do_scr[ch] = dov[:, ch * LANE:(ch + 1) * LANE]
        lse_scr[0] = lse_ref[...]
        dl_scr[0] = delta
        for k, d in enumerate(dilations):
            for scr, out in zip((do_scr, lse_scr, dl_scr), outs[3 * k:3 * k + 3]):
                _slabs_from_rows(scr, out, d)

    row = pl.BlockSpec((tm, ATT_W), lambda i: (i, 0))
    view = lambda d, w: pl.BlockSpec((tm // d, d * w), lambda i: (i, 0))
    outs = pl.pallas_call(
        body, name="dil_bwd_prep", grid=(SEQ // tm,),
        in_specs=[row, row, pl.BlockSpec((tm, LANE), lambda i: (i, 0))],
        out_specs=[view(d, w) for d in dilations for w in (ATT_W, LANE, LANE)],
        out_shape=[_sds((SEQ // d, d * w), t) for d in dilations for w, t in ((ATT_W, BF16), (LANE, F32), (LANE, F32))],
        scratch_shapes=[pltpu.VMEM((o_chunks, tm, LANE), F32), pltpu.VMEM((1, tm, LANE), F32),
                        pltpu.VMEM((1, tm, LANE), F32)],
        compiler_params=_params("parallel"),
    )(o, do, lse)
    return [outs[3 * k:3 * k + 3] for k in range(len(dilations))]


def _rope_tables():
    half = ROPE_DIM // 2
    inv_freq = np.float32(ROPE_THETA) ** (-np.arange(half, dtype=np.float32) * np.float32(2.0) / np.float32(ROPE_DIM))
    ang = np.arange(SEQ, dtype=np.float32)[:, None] * inv_freq.astype(np.float32)[None, :]
    cos, sin = jnp.asarray(np.cos(ang).astype(np.float32)), jnp.asarray(np.sin(ang).astype(np.float32))
    ones = jnp.ones((SEQ, HEAD_DIM - ROPE_DIM), F32)
    zeros = jnp.zeros((SEQ, HEAD_DIM - ROPE_DIM), F32)
    zh = jnp.zeros((SEQ, half), F32)
    c_tab = jnp.concatenate([cos, cos, ones], axis=1)
    a_tab = jnp.concatenate([-sin, zh, zeros], axis=1)
    b_tab = jnp.concatenate([zh, sin, zeros], axis=1)
    two = lambda t: jnp.concatenate([t, t], axis=1)
    return two(c_tab), two(a_tab), two(b_tab)


def _rotate(x, c_tab, a_tab, b_tab):
    return x * c_tab + pltpu.roll(x, LANE - ROPE_DIM // 2, 1) * a_tab + pltpu.roll(x, ROPE_DIM // 2, 1) * b_tab


def _rope_fwd(zm, tabs, *, tm=512):
    width = 3 * ATT_W
    dilations = [d for _, d in DIL_PATTERNS]

    def body(q_ref, k_ref, v_ref, c_ref, a_ref, b_ref, *rest):
        outs, scr = rest[:-1], rest[-1]
        per_part = ATT_W // LANE
        for part, (x_ref, mult) in enumerate(((q_ref, SCALE), (k_ref, 1.0))):
            for cc in range(per_part):
                sl = slice(cc * LANE, (cc + 1) * LANE)
                scr[part * per_part + cc] = _rotate(x_ref[:, sl].astype(F32), c_ref[...], a_ref[...], b_ref[...]) * mult
        for cc in range(per_part):
            scr[2 * per_part + cc] = v_ref[:, cc * LANE:(cc + 1) * LANE].astype(F32)
        for o_ref, d in zip(outs, dilations):
            for r in range(d):
                for ch in range(width // LANE):
                    o_ref[:, r * width + ch * LANE:r * width + (ch + 1) * LANE] = (
                        scr.at[ch][pl.ds(r, tm // d, stride=d), :].astype(o_ref.dtype))

    tab = pl.BlockSpec((tm, LANE), lambda i: (i, 0))
    col = lambda b: pl.BlockSpec((tm, ATT_W), lambda i: (i, b))
    return pl.pallas_call(
        body, name="rope_fwd", grid=(SEQ // tm,),
        in_specs=[col(3), col(4), col(5), tab, tab, tab],
        out_specs=[pl.BlockSpec((tm // d, d * width), lambda i: (i, 0)) for d in dilations],
        out_shape=[_sds((SEQ // d, d * width), BF16) for d in dilations],
        scratch_shapes=[pltpu.VMEM((width // LANE, tm, LANE), F32)],
        compiler_params=_params("parallel"),
    )(zm, zm, zm, *tabs)


def _dil_grad_combine(dqs, dks, dvs, tabs, dz, *, tm=256):
    dilations = [d for _, d in DIL_PATTERNS]
    chunks = ATT_W // LANE

    def body(*refs):
        groups = (refs[0:3], refs[3:6], refs[6:9])
        c_ref, a_ref, b_ref, _, o_ref, scr = refs[9:]

        def total(part, cc):
            acc = None
            for g, (ref, d) in enumerate(zip(groups[part], dilations)):
                term = ref[:, cc * LANE:(cc + 1) * LANE].astype(F32) if d == 1 else scr[part, g, cc]
                acc = term if acc is None else acc + term
            return acc

        for part in range(3):
            for g, (ref, d) in enumerate(zip(groups[part], dilations)):
                if d > 1:
                    _rows_from_slabs(ref, scr.at[part, g], d)
        for cc in range(chunks):
            for part in range(2):
                o_ref[:, part * ATT_W + cc * LANE:part * ATT_W + (cc + 1) * LANE] = _rotate(
                    total(part, cc), c_ref[...], -a_ref[...], -b_ref[...]).astype(o_ref.dtype)
            o_ref[:, 2 * ATT_W + cc * LANE:2 * ATT_W + (cc + 1) * LANE] = total(2, cc).astype(o_ref.dtype)

    view = lambda d: pl.BlockSpec((tm // d, d * ATT_W), lambda i: (i, 0))
    tab = pl.BlockSpec((tm, LANE), lambda i: (i, 0))
    return pl.pallas_call(
        body, name="dil_grad_combine", grid=(SEQ // tm,),
        in_specs=[view(d) for d in dilations] * 3 + [tab] * 3 + [ANY],
        out_specs=pl.BlockSpec((tm, 3 * ATT_W), lambda i: (i, 1)),
        out_shape=_sds((SEQ, Z_MAIN), BF16),
        input_output_aliases={12: 0},
        scratch_shapes=[pltpu.VMEM((3, len(dilations), chunks, tm, LANE), F32)],
        compiler_params=_params("parallel"),
    )(*dqs, *dks, *dvs, *tabs, dz)


def _dil_valid(n):
    qi = lax.broadcasted_iota(jnp.int32, (DIL_BLK, 2 * DIL_BLK), 0)
    ki = lax.broadcasted_iota(jnp.int32, (DIL_BLK, 2 * DIL_BLK), 1)
    dist = qi + DIL_BLK - ki
    return (dist >= 0) & (dist <= DIL_BLK) & ((n > 0) | (ki >= DIL_BLK))


def _dil_fwd(qkv_v, d):
    length = SEQ // d
    nb = length // DIL_BLK
    nsub = min(DIL_STEP_BLOCKS, nb)

    def body(q_ref, kp_ref, kc_ref, vp_ref, vc_ref, o_ref, lse_ref):
        m_step = pl.program_id(1)
        lane = lax.broadcasted_iota(jnp.int32, (DIL_BLK, LANE), 1)
        jobs = [(sub, h) for sub in range(nsub) for h in range(N_HEADS)]
        rows = lambda sub: slice(sub * DIL_BLK, (sub + 1) * DIL_BLK)
        cols = lambda h: slice(h * HEAD_DIM, (h + 1) * HEAD_DIM)

        def keys(prev_ref, cur_ref, sub, h):
            before = prev_ref[:, cols(h)] if sub == 0 else cur_ref[rows(sub - 1), cols(h)]
            return jnp.concatenate([before, cur_ref[rows(sub), cols(h)]], axis=0)

        scores = [lax.dot_general(q_ref[rows(sub), cols(h)], keys(kp_ref, kc_ref, sub, h), NT_DIMS,
                                  preferred_element_type=F32) for sub, h in jobs]
        ok = [_dil_valid(m_step)] + [_dil_valid(1)] * (nsub - 1)
        probs, inv_l, lse_all = [], [], [jnp.zeros((DIL_BLK, LANE), F32)] * nsub
        for idx, (sub, h) in enumerate(jobs):
            s = jnp.where(ok[sub], scores[idx], NEG_INF)
            m = jnp.max(s, axis=-1, keepdims=True)
            p = jnp.exp(s - m)
            l = jnp.sum(p, axis=-1, keepdims=True)
            probs.append(p.astype(BF16))
            inv_l.append(1.0 / l)
            lse_all[sub] = jnp.where(lane == h, m + jnp.log(l), lse_all[sub])
        outs = [jnp.dot(probs[idx], keys(vp_ref, vc_ref, sub, h), preferred_element_type=F32) * inv_l[idx]
                for idx, (sub, h) in enumerate(jobs)]
        for sub in range(nsub):
            o_ref[rows(sub), :] = jnp.concatenate(outs[sub * N_HEADS:(sub + 1) * N_HEADS], axis=1).astype(o_ref.dtype)
            lse_ref[rows(sub), :] = lse_all[sub]

    pair = lambda f: pl.BlockSpec((nsub * DIL_BLK, ATT_W), f)
    one = lambda f: pl.BlockSpec((DIL_BLK, ATT_W), f)
    before = lambda m: jnp.maximum(nsub * m - 1, 0)
    o, lse = pl.pallas_call(
        body, name=f"dil_fwd_d{d}", grid=(d, nb // nsub),
        in_specs=[pair(lambda r, m: (m, 3 * r)),
                  one(lambda r, m: (before(m), 3 * r + 1)), pair(lambda r, m: (m, 3 * r + 1)),
                  one(lambda r, m: (before(m), 3 * r + 2)), pair(lambda r, m: (m, 3 * r + 2))],
        out_specs=[pair(lambda r, m: (m, r)), pl.BlockSpec((nsub * DIL_BLK, LANE), lambda r, m: (m, r))],
        out_shape=[_sds((length, d * ATT_W), BF16), _sds((length, d * LANE), F32)],
        compiler_params=_params("parallel", "arbitrary"),
    )(qkv_v, qkv_v, qkv_v, qkv_v, qkv_v)
    return o, lse


def _rows_from_slabs(view_ref, scr, d):
    chunks, rows = scr.shape[0], scr.shape[1]
    for r in range(d):
        for ch in range(chunks):
            lo = (r * chunks + ch) * LANE
            scr.at[ch][pl.ds(r, rows // d, stride=d), :] = view_ref[:, lo:lo + LANE].astype(F32)


def _slabs_from_rows(scr, view_ref, d):
    chunks, rows = scr.shape[0], scr.shape[1]
    for r in range(d):
        for ch in range(chunks):
            lo = (r * chunks + ch) * LANE
            view_ref[:, lo:lo + LANE] = scr.at[ch][pl.ds(r, rows // d, stride=d), :].astype(view_ref.dtype)


def _dil_merge(os_, lses, *, tm=512):
    dilations = [d for _, d in DIL_PATTERNS]
    o_chunks = ATT_W // LANE

    def body(o0, o1, o2, l0, l1, l2, y_ref, lse_ref, o_scr, l_scr):
        os_nat, ls = [], []
        for g, (o_ref, l_ref, d) in enumerate(zip((o0, o1, o2), (l0, l1, l2), dilations)):
            if d == 1:
                os_nat.append(o_ref[...].astype(F32))
                ls.append(l_ref[...])
            else:
                _rows_from_slabs(o_ref, o_scr.at[g], d)
                _rows_from_slabs(l_ref, l_scr.at[g], d)
                os_nat.append(jnp.concatenate([o_scr[g, ch] for ch in range(o_chunks)], axis=1))
                ls.append(l_scr[g, 0])
        m = jnp.maximum(jnp.maximum(ls[0], ls[1]), ls[2])
        es = [jnp.exp(l - m) for l in ls]
        tot = es[0] + es[1] + es[2]
        lse_ref[...] = m + jnp.log(tot)
        alphas = [e / tot for e in es]
        outs = []
        for h in range(N_HEADS):
            acc = None
            for g in range(3):
                term = alphas[g][:, h:h + 1] * _head(os_nat[g], h)
                acc = term if acc is None else acc + term
            outs.append(acc)
        y_ref[...] = jnp.concatenate(outs, axis=1).astype(y_ref.dtype)

    row = pl.BlockSpec((tm, ATT_W), lambda i: (i, 0))
    vec = pl.BlockSpec((tm, LANE), lambda i: (i, 0))
    view = lambda d, w: pl.BlockSpec((tm // d, d * w), lambda i: (i, 0))
    return pl.pallas_call(
        body, name="dil_merge", grid=(SEQ // tm,),
        in_specs=[view(d, ATT_W) for d in dilations] + [view(d, LANE) for d in dilations], out_specs=[row, vec],
        out_shape=[_sds((SEQ, ATT_W), BF16), _sds((SEQ, LANE), F32)],
        scratch_shapes=[pltpu.VMEM((3, o_chunks, tm, LANE), F32), pltpu.VMEM((3, 1, tm, LANE), F32)],
        compiler_params=_params("parallel"),
    )(*os_, *lses)


def _dil_bwd(qkv_v, do_v, lse_v, dl_v, d):
    length = SEQ // d
    nb = length // DIL_BLK
    nsub = min(DIL_STEP_BLOCKS, nb)
    n_steps = nb // nsub

    def body(q_ref, kp_ref, kc_ref, vp_ref, vc_ref, lse_ref, dl_ref, do_ref, dq_ref, dk_ref, dv_ref, dk_s, dv_s):
        m_step = pl.program_id(1)

        @pl.when(m_step == 0)
        def _():
            dk_s[...] = jnp.zeros_like(dk_s)
            dv_s[...] = jnp.zeros_like(dv_s)

        jobs = [(sub, h) for sub in range(nsub) for h in range(N_HEADS)]
        rows = lambda sub: slice(sub * DIL_BLK, (sub + 1) * DIL_BLK)
        cols = lambda h: slice(h * HEAD_DIM, (h + 1) * HEAD_DIM)

        def keys(prev_ref, cur_ref, sub, h):
            before = prev_ref[:, cols(h)] if sub == 0 else cur_ref[rows(sub - 1), cols(h)]
            return jnp.concatenate([before, cur_ref[rows(sub), cols(h)]], axis=0)

        kks = [keys(kp_ref, kc_ref, sub, h) for sub, h in jobs]
        scores = [lax.dot_general(q_ref[rows(sub), cols(h)], kks[idx], NT_DIMS, preferred_element_type=F32)
                  for idx, (sub, h) in enumerate(jobs)]
        dps = [lax.dot_general(do_ref[rows(sub), cols(h)], keys(vp_ref, vc_ref, sub, h), NT_DIMS,
                               preferred_element_type=F32) for sub, h in jobs]
        ok = [_dil_valid(m_step)] + [_dil_valid(1)] * (nsub - 1)
        ps, dss = [], []
        for idx, (sub, h) in enumerate(jobs):
            p = jnp.where(ok[sub], jnp.exp(scores[idx] - lse_ref[rows(sub), h:h + 1]), 0.0)
            ps.append(p.astype(BF16))
            dss.append((p * (dps[idx] - dl_ref[rows(sub), h:h + 1])).astype(BF16))
        dqs = [jnp.dot(dss[idx], kks[idx], preferred_element_type=F32) * SCALE for idx in range(len(jobs))]
        dkks = [lax.dot_general(dss[idx], q_ref[rows(sub), cols(h)], TN_DIMS, preferred_element_type=F32)
                for idx, (sub, h) in enumerate(jobs)]
        dvvs = [lax.dot_general(ps[idx], do_ref[rows(sub), cols(h)], TN_DIMS, preferred_element_type=F32)
                for idx, (sub, h) in enumerate(jobs)]
        for sub in range(nsub):
            dq_ref[rows(sub), :] = jnp.concatenate(dqs[sub * N_HEADS:(sub + 1) * N_HEADS], axis=1).astype(dq_ref.dtype)
        base = m_step * (nsub * DIL_BLK)
        blocks = [pl.ds(pl.multiple_of(jnp.maximum(base - DIL_BLK, 0), DIL_BLK), DIL_BLK)]
        blocks += [pl.ds(pl.multiple_of(base + s * DIL_BLK, DIL_BLK), DIL_BLK) for s in range(nsub)]
        for acc, parts in ((dk_s, dkks), (dv_s, dvvs)):
            top = lambda sub: jnp.concatenate([parts[sub * N_HEADS + h][:DIL_BLK] for h in range(N_HEADS)], axis=1)
            bottom = lambda sub: jnp.concatenate([parts[sub * N_HEADS + h][DIL_BLK:] for h in range(N_HEADS)], axis=1)
            acc[blocks[0], :] += top(0)
            for s in range(nsub):
                acc[blocks[s + 1], :] += bottom(s) + top(s + 1) if s + 1 < nsub else bottom(s)

        @pl.when(m_step == n_steps - 1)
        def _():
            dk_ref[...] = dk_s[...].astype(dk_ref.dtype)
            dv_ref[...] = dv_s[...].astype(dv_ref.dtype)

    pair = lambda f: pl.BlockSpec((nsub * DIL_BLK, ATT_W), f)
    one = lambda f: pl.BlockSpec((DIL_BLK, ATT_W), f)
    vec = lambda f: pl.BlockSpec((nsub * DIL_BLK, LANE), f)
    whole = pl.BlockSpec((length, ATT_W), lambda r, m: (0, r))
    before = lambda m: jnp.maximum(nsub * m - 1, 0)
    outs = pl.pallas_call(
        body, name=f"dil_bwd_d{d}", grid=(d, n_steps),
        in_specs=[pair(lambda r, m: (m, 3 * r)),
                  one(lambda r, m: (before(m), 3 * r + 1)), pair(lambda r, m: (m, 3 * r + 1)),
                  one(lambda r, m: (before(m), 3 * r + 2)), pair(lambda r, m: (m, 3 * r + 2)),
                  vec(lambda r, m: (m, r)), vec(lambda r, m: (m, r)), pair(lambda r, m: (m, r))],
        out_specs=[pair(lambda r, m: (m, r)), whole, whole],
        out_shape=[_sds((length, d * ATT_W), BF16)] * 3,
        scratch_shapes=[pltpu.VMEM((length, ATT_W), F32), pltpu.VMEM((length, ATT_W), F32)],
        compiler_params=_params("arbitrary", "arbitrary"),
    )(qkv_v, qkv_v, qkv_v, qkv_v, qkv_v, lse_v, dl_v, do_v)
    return outs


def _sigmoid(x):
    return 1.0 / (1.0 + jnp.exp(-x))


def _mix_fwd(ya, yb, w_oa, w_ob, zm, *, tm=512):
    def body(ya_ref, yb_ref, wa_ref, wb_ref, ga_ref, gb_ref, pa_ref, pb_ref, mix_ref):
        pa = jnp.dot(ya_ref[...], wa_ref[...], preferred_element_type=F32)
        pb = jnp.dot(yb_ref[...], wb_ref[...], preferred_element_type=F32)
        pa_ref[...] = pa.astype(pa_ref.dtype)
        pb_ref[...] = pb.astype(pb_ref.dtype)
        mix_ref[...] = (_sigmoid(ga_ref[...].astype(F32)) * pa + _sigmoid(gb_ref[...].astype(F32)) * pb
                        ).astype(mix_ref.dtype)

    row = pl.BlockSpec((tm, ATT_W), lambda i: (i, 0))
    wsp = pl.BlockSpec((ATT_W, D_MODEL), lambda i: (0, 0))
    wide = pl.BlockSpec((tm, D_MODEL), lambda i: (i, 0))
    return pl.pallas_call(
        body, name="mix_fwd", grid=(SEQ // tm,),
        in_specs=[row, row, wsp, wsp, pl.BlockSpec((tm, D_MODEL), lambda i: (i, 3)),
                  pl.BlockSpec((tm, D_MODEL), lambda i: (i, 4))],
        out_specs=[wide] * 3, out_shape=[_sds((SEQ, D_MODEL), BF16)] * 3,
        compiler_params=_params("parallel"),
    )(ya, yb, w_oa, w_ob, zm, zm)


def _gate_bwd(dmix, zm, p, gate_block, dz, *, name, tm=512):
    def body(dm_ref, g_ref, p_ref, *rest):
        dp_ref, dz_ref = rest[-2], rest[-1]
        dm = dm_ref[...].astype(F32)
        s = _sigmoid(g_ref[...].astype(F32))
        dp_ref[...] = (dm * s).astype(dp_ref.dtype)
        dz_ref[...] = (dm * p_ref[...].astype(F32) * s * (1.0 - s)).astype(dz_ref.dtype)

    wide = pl.BlockSpec((tm, D_MODEL), lambda i: (i, 0))
    gate = pl.BlockSpec((tm, D_MODEL), lambda i: (i, gate_block))
    extra = [] if dz is None else [dz]
    return pl.pallas_call(
        body, name=name, grid=(SEQ // tm,),
        in_specs=[wide, gate, wide] + [ANY] * len(extra),
        out_specs=[wide, gate],
        out_shape=[_sds((SEQ, D_MODEL), BF16), _sds((SEQ, Z_MAIN), BF16)],
        input_output_aliases={3: 1} if extra else {},
        compiler_params=_params("parallel"),
    )(dmix, zm, p, *extra)


def _out_fwd(mixed, w_out, x, g_post, g_pre, *, tm=512):
    def body(m_ref, w_ref, x_ref, gp_ref, gn_ref, y_ref, x2_ref, h_ref):
        y = jnp.dot(m_ref[...], w_ref[...], preferred_element_type=F32)
        y_ref[...] = y
        r = lax.rsqrt(jnp.mean(y * y, axis=-1, keepdims=True) + RMS_EPS)
        x2 = x_ref[...] + y * r * gp_ref[...]
        x2_ref[...] = x2
        r2 = lax.rsqrt(jnp.mean(x2 * x2, axis=-1, keepdims=True) + RMS_EPS)
        h_ref[...] = (x2 * r2 * gn_ref[...]).astype(h_ref.dtype)

    row = pl.BlockSpec((tm, D_MODEL), lambda i: (i, 0))
    vec = pl.BlockSpec((1, D_MODEL), lambda i: (0, 0))
    return pl.pallas_call(
        body, name="out_fwd", grid=(SEQ // tm,),
        in_specs=[row, pl.BlockSpec((D_MODEL, D_MODEL), lambda i: (0, 0)), row, vec, vec],
        out_specs=[row] * 3,
        out_shape=[_sds((SEQ, D_MODEL), F32), _sds((SEQ, D_MODEL), F32), _sds((SEQ, D_MODEL), BF16)],
        compiler_params=_params("parallel"),
    )(mixed, w_out, x, g_post, g_pre)


FFN_HALF = 256
FFN_TN = 2 * FFN_HALF
FFN_NJ = D_FF // FFN_HALF
FFN_GROUP = 2 * SUBLANE
UP_TM = 1024


def _ffn_interleave(t):
    lead = t.shape[:-1]
    return jnp.swapaxes(t.reshape(*lead, 2, FFN_NJ, FFN_HALF), -3, -2).reshape(*lead, 2 * D_FF)


def _ffn_deinterleave(t):
    lead = t.shape[:-1]
    return jnp.swapaxes(t.reshape(*lead, FFN_NJ, 2, FFN_HALF), -3, -2).reshape(*lead, 2 * D_FF)


W_IN_SHARD = (Z_MAIN + N_HEADS) // N_DEV
FORGET_LO = 3 * ATT_W


def _w_in_from_shards(shards, *, tm=256):
    def columns(g_ref, lo, width):
        p, off = divmod(lo, W_IN_SHARD)
        if off + width <= W_IN_SHARD:
            return g_ref[p, :, off:off + width]
        first = W_IN_SHARD - off
        return jnp.concatenate([g_ref[p, :, off:], g_ref[p + 1, :, :width - first]], axis=1)

    def body(g_ref, main_ref, f_ref):
        for t in range(Z_MAIN // LANE):
            lo = t * LANE
            main_ref[:, lo:lo + LANE] = columns(g_ref, lo if lo < FORGET_LO else lo + N_HEADS, LANE)
        f_ref[...] = jnp.concatenate([columns(g_ref, FORGET_LO, N_HEADS),
                                      jnp.zeros((tm, F_PAD - N_HEADS), f_ref.dtype)], axis=1)

    return pl.pallas_call(
        body, name="w_in_from_shards", grid=(D_MODEL // tm,),
        in_specs=[pl.BlockSpec((N_DEV, tm, W_IN_SHARD), lambda i: (0, i, 0))],
        out_specs=[pl.BlockSpec((tm, Z_MAIN), lambda i: (i, 0)), pl.BlockSpec((tm, F_PAD), lambda i: (i, 0))],
        out_shape=[_sds((D_MODEL, Z_MAIN), shards.dtype), _sds((D_MODEL, F_PAD), shards.dtype)],
        compiler_params=_params("parallel"),
    )(shards)


def _w_in_to_shards(g_main, g_f, *, tm=256):
    def natural(main_ref, f_ref, lo, width):
        pieces, hi = [], lo + width
        for ref, start, stop, shift in ((main_ref, 0, FORGET_LO, 0), (f_ref, FORGET_LO, FORGET_LO + N_HEADS, FORGET_LO),
                                        (main_ref, FORGET_LO + N_HEADS, Z_MAIN + N_HEADS, N_HEADS)):
            a, b = max(lo, start), min(hi, stop)
            if a < b:
                pieces.append(ref[:, a - shift:b - shift])
        return pieces[0] if len(pieces) == 1 else jnp.concatenate(pieces, axis=1)

    def body(main_ref, f_ref, o_ref):
        for p in range(N_DEV):
            for q in range(-(-W_IN_SHARD // LANE)):
                width = min(LANE, W_IN_SHARD - q * LANE)
                o_ref[p, :, q * LANE:q * LANE + width] = natural(main_ref, f_ref, p * W_IN_SHARD + q * LANE, width)

    return pl.pallas_call(
        body, name="w_in_to_shards", grid=(D_MODEL // tm,),
        in_specs=[pl.BlockSpec((tm, Z_MAIN), lambda i: (i, 0)), pl.BlockSpec((tm, F_PAD), lambda i: (i, 0))],
        out_specs=pl.BlockSpec((N_DEV, tm, W_IN_SHARD), lambda i: (0, i, 0)),
        out_shape=_sds((N_DEV, D_MODEL, W_IN_SHARD), g_main.dtype),
        compiler_params=_params("parallel"),
    )(g_main, g_f)


W_UP_SHARD = 2 * D_FF // N_DEV


def _w_up_lane_tile(k):
    block = k // 2
    return (2 * (block % FFN_NJ) + block // FFN_NJ) * FFN_HALF + (k % 2) * LANE


def _w_up_from_shards(shards, *, tm=256):
    def body(g_ref, o_ref):
        for k in range(2 * D_FF // LANE):
            p, off = divmod(k * LANE, W_UP_SHARD)
            if off + LANE <= W_UP_SHARD:
                tile = g_ref[p, :, off:off + LANE]
            else:
                tile = jnp.concatenate([g_ref[p, :, off:], g_ref[p + 1, :, :off + LANE - W_UP_SHARD]], axis=1)
            dst = _w_up_lane_tile(k)
            o_ref[:, dst:dst + LANE] = tile

    return pl.pallas_call(
        body, name="w_up_from_shards", grid=(D_MODEL // tm,),
        in_specs=[pl.BlockSpec((N_DEV, tm, W_UP_SHARD), lambda i: (0, i, 0))],
        out_specs=pl.BlockSpec((tm, 2 * D_FF), lambda i: (i, 0)),
        out_shape=_sds((D_MODEL, 2 * D_FF), shards.dtype),
        compiler_params=_params("parallel"),
    )(shards)


def _w_up_to_shards(t, *, tm=256):
    def body(x_ref, o_ref):
        for p in range(N_DEV):
            for q in range(-(-W_UP_SHARD // LANE)):
                width = min(LANE, W_UP_SHARD - q * LANE)
                k, off = divmod(p * W_UP_SHARD + q * LANE, LANE)
                src = _w_up_lane_tile(k)
                if off == 0:
                    tile = x_ref[:, src:src + width]
                else:
                    tile = x_ref[:, src + off:src + LANE]
                    if width > LANE - off:
                        nxt = _w_up_lane_tile(k + 1)
                        tile = jnp.concatenate([tile, x_ref[:, nxt:nxt + width - (LANE - off)]], axis=1)
                o_ref[p, :, q * LANE:q * LANE + width] = tile

    return pl.pallas_call(
        body, name="w_up_to_shards", grid=(D_MODEL // tm,),
        in_specs=[pl.BlockSpec((tm, 2 * D_FF), lambda i: (i, 0))],
        out_specs=pl.BlockSpec((N_DEV, tm, W_UP_SHARD), lambda i: (0, i, 0)),
        out_shape=_sds((N_DEV, D_MODEL, W_UP_SHARD), t.dtype),
        compiler_params=_params("parallel"),
    )(t)


def _gelu_parts(a):
    c = math.sqrt(2.0 / math.pi)
    a2 = a * a
    t = jnp.tanh((c * a) * (1.0 + 0.044715 * a2))
    half_a, one_t = 0.5 * a, 1.0 + t
    gelu = half_a * one_t
    dgelu = 0.5 * one_t + half_a * (1.0 - t * t) * (c + (3.0 * 0.044715 * c) * a2)
    return gelu, dgelu


def _row_masks(down):
    row = lax.broadcasted_iota(jnp.int32, (SUBLANE, FFN_TN), 0)
    return (row < 1, row < 2) if down else (row >= SUBLANE - 1, row >= SUBLANE - 2)


def _rolled(x, down):
    return (pltpu.roll(x, 1, 0), pltpu.roll(x, 2, 0)) if down else (
        pltpu.roll(x, SUBLANE - 1, 0), pltpu.roll(x, SUBLANE - 2, 0))


def _shifted(cur_rolled, neighbour_rolled, masks):
    return (jnp.where(masks[0], neighbour_rolled[0], cur_rolled[0]),
            jnp.where(masks[1], neighbour_rolled[1], cur_rolled[1]))


def _conv_consts(w_ref, b_ref):
    shape = (SUBLANE, FFN_TN)
    return [jnp.broadcast_to(w_ref[k:k + 1, :], shape) for k in range(3)] + [jnp.broadcast_to(b_ref[...], shape)]


def _up_conv_fwd(h2, w_up, conv_w, conv_b):
    nrow = SEQ // UP_TM
    n_tiles = FFN_NJ * nrow
    n_groups = UP_TM // FFN_GROUP

    def body(h_ref, wu_ref, w_ref, b_ref, u_ref, ab_ref, m_ref, ua_s, ub_s, c1_s, c2_s):
        k = pl.program_id(0)

        @pl.when(k == 0)
        def _():
            ub_s[...] = jnp.zeros_like(ub_s)

        @pl.when(jnp.maximum(k - 1, 0) % nrow == 0)
        def _():
            c1_s[...] = jnp.zeros_like(c1_s)
            c2_s[...] = jnp.zeros_like(c2_s)

        def step(write_s, read_s):
            h_rows = pl.ds(pl.multiple_of((this(k) % nrow) * UP_TM, UP_TM), UP_TM)
            u = jnp.dot(h_ref[h_rows, :], wu_ref[...], preferred_element_type=F32)
            write_s[...] = u
            u_ref[...] = u.astype(u_ref.dtype)
            w0, w1, w2, bias = _conv_consts(w_ref, b_ref)
            masks = _row_masks(True)
            above = (c1_s[...], c2_s[...])
            for g in range(n_groups):
                rows = slice(g * FFN_GROUP, (g + 1) * FFN_GROUP)
                x = read_s[rows, :]
                convs = []
                for c in range(2):
                    cur = x[c * SUBLANE:(c + 1) * SUBLANE]
                    cur_rolled = _rolled(cur, True)
                    s1, s2 = _shifted(cur_rolled, above, masks)
                    convs.append(w0 * s2 + w1 * s1 + w2 * cur + bias)
                    above = cur_rolled
                y = jnp.concatenate(convs, axis=0)
                ab_ref[rows, :] = y.astype(ab_ref.dtype)
                m_ref[rows, :] = (_gelu_parts(y[:, :FFN_HALF])[0] * y[:, FFN_HALF:]).astype(m_ref.dtype)
            c1_s[...], c2_s[...] = above

        @pl.when(k % 2 == 0)
        def _():
            step(ua_s, ub_s)

        @pl.when(k % 2 == 1)
        def _():
            step(ub_s, ua_s)

    this = lambda k: jnp.minimum(k, n_tiles - 1)
    last = lambda k: jnp.maximum(k - 1, 0)
    blk = lambda tile: pl.BlockSpec((UP_TM, FFN_TN), lambda k: (tile(k) % nrow, tile(k) // nrow))
    return pl.pallas_call(
        body, name="up_conv_fwd", grid=(n_tiles + 1,),
        in_specs=[pl.BlockSpec((SEQ, D_MODEL), lambda k: (0, 0)),
                  pl.BlockSpec((D_MODEL, FFN_TN), lambda k: (0, this(k) // nrow)),
                  pl.BlockSpec((3, FFN_TN), lambda k: (0, last(k) // nrow)),
                  pl.BlockSpec((1, FFN_TN), lambda k: (0, last(k) // nrow))],
        out_specs=[blk(this), blk(last), pl.BlockSpec((UP_TM, FFN_HALF), lambda k: (last(k) % nrow, last(k) // nrow))],
        out_shape=[_sds((SEQ, 2 * D_FF), BF16), _sds((SEQ, 2 * D_FF), BF16), _sds((SEQ, D_FF), BF16)],
        scratch_shapes=[pltpu.VMEM((UP_TM, FFN_TN), F32), pltpu.VMEM((UP_TM, FFN_TN), F32),
                        pltpu.VMEM((SUBLANE, FFN_TN), F32), pltpu.VMEM((SUBLANE, FFN_TN), F32)],
        compiler_params=_params("arbitrary"),
    )(h2, w_up, conv_w, conv_b)


def _ffn_mid_bwd(dy2, w_down, u, ab, conv_w):
    nrow = SEQ // UP_TM
    n_tiles = FFN_NJ * nrow
    n_groups = UP_TM // FFN_GROUP
    this = lambda k: jnp.minimum(k, n_tiles - 1)
    last = lambda k: jnp.maximum(k - 1, 0)
    row_of = lambda tile: nrow - 1 - tile % nrow

    def body(dy_ref, wd_ref, u_ref, ab_ref, w_ref, du_ref, gw_ref, gb_ref, c_s, dma_s, dmb_s):
        k = pl.program_id(0)

        @pl.when(k == 0)
        def _():
            dmb_s[...] = jnp.zeros_like(dmb_s)

        @pl.when(last(k) % nrow == 0)
        def _():
            c_s[...] = jnp.zeros_like(c_s)
            gw_ref[...] = jnp.zeros_like(gw_ref)
            gb_ref[...] = jnp.zeros_like(gb_ref)

        def step(write_s, read_s):
            dy_rows = pl.ds(pl.multiple_of(row_of(this(k)) * UP_TM, UP_TM), UP_TM)
            write_s[...] = lax.dot_general(dy_ref[dy_rows, :], wd_ref[...], NT_DIMS,
                                           preferred_element_type=F32)
            taps = [jnp.broadcast_to(w_ref[t:t + 1, :], (SUBLANE, FFN_TN)) for t in range(3)]
            masks = _row_masks(False)
            below = _rolled(c_s[...], False)
            acc = [jnp.zeros((SUBLANE, FFN_TN), F32)] * 4
            for g in reversed(range(n_groups)):
                rows = slice(g * FFN_GROUP, (g + 1) * FFN_GROUP)
                x, y, dmv = u_ref[rows, :].astype(F32), ab_ref[rows, :].astype(F32), read_s[rows, :]
                gelu, dgelu = _gelu_parts(y[:, :FFN_HALF])
                d = jnp.concatenate([dmv * y[:, FFN_HALF:] * dgelu, dmv * gelu], axis=1)
                pre = [None, None]
                for c in (1, 0):
                    sl = slice(c * SUBLANE, (c + 1) * SUBLANE)
                    cur, xs = d[sl], x[sl]
                    cur_rolled = _rolled(cur, False)
                    up1, up2 = _shifted(cur_rolled, below, masks)
                    acc = [acc[0] + up2 * xs, acc[1] + up1 * xs, acc[2] + cur * xs, acc[3] + cur]
                    pre[c] = taps[2] * cur + taps[1] * up1 + taps[0] * up2
                    below = cur_rolled
                du_ref[rows, :] = jnp.concatenate(pre, axis=0).astype(du_ref.dtype)
            c_s[...] = pltpu.roll(below[0], 1, 0)
            for t in range(3):
                gw_ref[t:t + 1, :] += jnp.sum(acc[t], axis=0, keepdims=True)
            gb_ref[...] += jnp.sum(acc[3], axis=0, keepdims=True)

        @pl.when(k % 2 == 0)
        def _():
            step(dma_s, dmb_s)

        @pl.when(k % 2 == 1)
        def _():
            step(dmb_s, dma_s)

    blk = pl.BlockSpec((UP_TM, FFN_TN), lambda k: (row_of(last(k)), last(k) // nrow))
    col = lambda rows: pl.BlockSpec((rows, FFN_TN), lambda k: (0, last(k) // nrow))
    return pl.pallas_call(
        body, name="ffn_mid_bwd", grid=(n_tiles + 1,),
        in_specs=[pl.BlockSpec((SEQ, D_MODEL), lambda k: (0, 0)),
                  pl.BlockSpec((FFN_HALF, D_MODEL), lambda k: (this(k) // nrow, 0)), blk, blk, col(3)],
        out_specs=[blk, col(3), col(1)],
        out_shape=[_sds((SEQ, 2 * D_FF), BF16), _sds((3, 2 * D_FF), F32), _sds((1, 2 * D_FF), F32)],
        scratch_shapes=[pltpu.VMEM((SUBLANE, FFN_TN), F32), pltpu.VMEM((UP_TM, FFN_HALF), F32),
                        pltpu.VMEM((UP_TM, FFN_HALF), F32)],
        compiler_params=_params("arbitrary"),
    )(dy2, w_down, u, ab, conv_w)


def _down_fwd(m, w_down, x2, g_post, target, *, tm=512):
    def body(m_ref, w_ref, x2_ref, g_ref, t_ref, dout_ref, dy_ref, gg_ref, loss_ref):
        @pl.when(pl.program_id(0) == 0)
        def _():
            gg_ref[...] = jnp.zeros_like(gg_ref)
            loss_ref[...] = jnp.zeros_like(loss_ref)

        y = jnp.dot(m_ref[...], w_ref[...], preferred_element_type=F32)
        r = lax.rsqrt(jnp.mean(y * y, axis=-1, keepdims=True) + RMS_EPS)
        yn = y * r
        diff = (x2_ref[...] + yn * g_ref[...]) - t_ref[...]
        loss_ref[...] += jnp.sum(diff * diff)
        dout = diff * (1.0 / D_MODEL)
        dout_ref[...] = dout
        gg_ref[...] += jnp.sum(dout * yn, axis=0, keepdims=True)
        dn = dout * g_ref[...]
        dy_ref[...] = (r * (dn - yn * jnp.mean(dn * yn, axis=-1, keepdims=True))).astype(dy_ref.dtype)

    row = pl.BlockSpec((tm, D_MODEL), lambda i: (i, 0))
    vec = pl.BlockSpec((1, D_MODEL), lambda i: (0, 0))
    return pl.pallas_call(
        body, name="down_fwd", grid=(SEQ // tm,),
        in_specs=[pl.BlockSpec((tm, D_FF), lambda i: (i, 0)), pl.BlockSpec((D_FF, D_MODEL), lambda i: (0, 0)),
                  row, vec, row],
        out_specs=[row, row, vec, pl.BlockSpec((1, LANE), lambda i: (0, 0))],
        out_shape=[_sds((SEQ, D_MODEL), F32), _sds((SEQ, D_MODEL), BF16), _sds((1, D_MODEL), F32),
                   _sds((1, LANE), F32)],
        compiler_params=_params("arbitrary"),
    )(m, w_down, x2, g_post, target)


def _local_step(x, target, w_main, w_f, b_forget, conv_b, g_pre_mix, g_post_mix, g_pre_ffn, g_post_ffn,
                proj_weights, ffn_weights, ffn_grads_ready, proj_grads_ready, mixer_grads_ready, after=None):
    mm = _matmul
    tabs = _rope_tables()

    h1 = _rms_fwd(x, g_pre_mix, name="rms_pre_mix", after=after)
    zm = mm(h1, w_main, out_dtype=BF16, tm=2048, tn=1024, tk=1024, name="in_proj")
    zf = mm(h1, w_f, out_dtype=F32, tm=2048, tn=F_PAD, tk=1024, name="in_proj_forget")
    f_row, sg_row = _fox_prep(zf[:, :N_HEADS].T, b_forget.reshape(N_HEADS, 1))
    f_cols = jnp.pad(f_row.T, ((0, 0), (0, LANE - N_HEADS)))
    q_slots, k_slots, v_slots = _fox_pack_fwd(zm, f_cols)
    ya, lse_a = _fox_fwd(q_slots, k_slots, v_slots)
    qkv_d = dict(zip([d for _, d in DIL_PATTERNS], _rope_fwd(zm, tabs)))
    dil = [_dil_fwd(qkv_d[d], d) for _, d in DIL_PATTERNS]
    yb, lse_b = _dil_merge([o for o, _ in dil], [l for _, l in dil])
    w_oa, w_ob, w_out = proj_weights(yb)
    pa, pb, mixed = _mix_fwd(ya, yb, w_oa, w_ob, zm)
    y1, x2, h2 = _out_fwd(mixed, w_out, x, g_post_mix, g_pre_ffn)
    w_up, conv_w, w_down = ffn_weights(h2)
    u, ab, m = _up_conv_fwd(h2, w_up, conv_w, _ffn_interleave(conv_b))
    dout, dy2, gg_post_ffn, sq_err = _down_fwd(m, w_down, x2, g_post_ffn, target)

    g_w_down = mm(m, dy2, ta=True, out_dtype=BF16, tm=D_FF // 2, tn=1024, tk=SEQ, name="grad_w_down")
    du, g_conv_w, g_conv_b = _ffn_mid_bwd(dy2, w_down, u, ab, conv_w)
    g_w_up = mm(h2, du, ta=True, out_dtype=BF16, tm=1024, tn=D_FF // 2, tk=SEQ, name="grad_w_up")
    tok = ffn_grads_ready(dict(w_down=g_w_down, w_up_blocks=g_w_up, conv_w=_ffn_deinterleave(g_conv_w)))
    dh2 = mm(du, w_up, tb=True, out_dtype=BF16, tm=1024, tn=1024, tk=2 * D_FF, name="d_h2")

    dx2, dy1, gg_pre_ffn, gg_post_mix = _rms_pair_bwd([dh2], x2, g_pre_ffn, dout, y1, g_post_mix, after=tok)
    g_w_out = mm(mixed, dy1, ta=True, out_dtype=BF16, tm=1024, tn=1024, tk=SEQ, name="grad_w_out")
    dmix = mm(dy1, w_out, tb=True, out_dtype=BF16, tm=2048, tn=1024, tk=1024, name="d_mixed")
    dpa, dz = _gate_bwd(dmix, zm, pa, 3, None, name="gate_bwd_fox")
    dpb, dz = _gate_bwd(dmix, zm, pb, 4, dz, name="gate_bwd_dil")
    g_w_oa = mm(ya, dpa, ta=True, out_dtype=BF16, tm=512, tn=1024, tk=SEQ, name="grad_w_o_fox", col_slots=N_DEV)
    g_w_ob = mm(yb, dpb, ta=True, out_dtype=BF16, tm=512, tn=1024, tk=SEQ, name="grad_w_o_dil", col_slots=N_DEV)
    tok = proj_grads_ready(dict(w_o_fox=g_w_oa, w_o_dil=g_w_ob, w_out=g_w_out))
    dya = mm(dpa, w_oa, tb=True, out_dtype=BF16, tm=2048, tn=512, tk=1024, name="d_y_fox")
    dyb = mm(dpb, w_ob, tb=True, out_dtype=BF16, tm=2048, tn=512, tk=1024, name="d_y_dil")

    qb_slots, do_slots = _fox_pack_bwd(zm, f_cols, lse_a, ya, dya, after=tok)
    dz, df_cols = _fox_unpack(*_fox_bwd(qb_slots, k_slots, v_slots, do_slots), dz)
    dfa_t, g_b_forget = _fox_post_bwd(df_cols[:, :N_HEADS].T, sg_row)

    rows_d = _dil_bwd_prep(yb, dyb, lse_b)
    dil_g = [_dil_bwd(qkv_d[d], *rows_d[k], d) for k, (_, d) in enumerate(DIL_PATTERNS)]
    dz = _dil_grad_combine([g[0] for g in dil_g], [g[1] for g in dil_g], [g[2] for g in dil_g], tabs, dz)

    dzf = jnp.pad(dfa_t.T, ((0, 0), (0, F_PAD - N_HEADS)))
    g_w_main = mm(h1, dz, ta=True, out_dtype=BF16, tm=1024, tn=Z_MAIN // 4, tk=SEQ, name="grad_w_in")
    g_w_f = mm(h1, dzf, ta=True, out_dtype=BF16, tm=1024, tn=F_PAD, tk=1024, name="grad_w_in_forget")
    tok = mixer_grads_ready(dict(w_main=g_w_main, w_f=g_w_f))
    dh1 = [mm(dz, w_main, tb=True, out_dtype=BF16, tm=1024, tn=1024, tk=Z_MAIN, name="d_h1", after=tok),
           mm(dzf, w_f, tb=True, out_dtype=BF16, tm=2048, tn=1024, tk=F_PAD, name="d_h1_forget", after=tok)]
    grad_x, gg_pre_mix = _rms_bwd(dh1, x, g_pre_mix, dx2, out_dtype=F32, name="rms_pre_mix_bwd")

    grads = dict(
        b_forget=g_b_forget.reshape(1, N_HEADS), conv_b=_ffn_deinterleave(g_conv_b),
        g_pre_mix=gg_pre_mix, g_post_mix=gg_post_mix, g_pre_ffn=gg_pre_ffn, g_post_ffn=gg_post_ffn)
    return sq_err, grad_x, grads


def _gather_two_level(shard, *, name):
    def body(x_ref, out_ref, send_sems, recv_sems, local_sem):
        x, y, c = lax.axis_index("x"), lax.axis_index("y"), lax.axis_index("c")
        me, sibling = (x, y, c), (x, y, 1 - c)
        chips = [(1 - x, y), (x, 1 - y), (1 - x, 1 - y)]

        def slot(px, py, pc):
            return out_ref.at[4 * px + 2 * py + pc]

        def copy(k, block, to, src=None):
            return pltpu.make_async_remote_copy(
                src_ref=slot(*block) if src is None else src, dst_ref=slot(*block),
                send_sem=send_sems.at[k], recv_sem=recv_sems.at[k], device_id=to, device_id_type=MESH_ID)

        mine = pltpu.make_async_copy(x_ref, slot(*me), local_sem)
        first = [copy(0, me, sibling, src=x_ref)]
        first += [copy(1 + j, me, (*chips[j], c), src=x_ref) for j in range(2)]
        for cp in first:
            cp.start()
        mine.start(priority=1)
        passed = [copy(4 + j, (*chip, c), sibling) for j, chip in enumerate(chips)]

        def route(near, far):
            copy(1 + near, (*chips[near], c), me).wait_recv()
            onward = copy(3, (*chips[near], c), (*chips[far], c))
            onward.start()
            passed[near].start()
            copy(1 + far, (*chips[far], c), me).wait_recv()
            passed[far].start()
            copy(3, (*chips[2], c), me).wait_recv()
            passed[2].start()
            onward.wait_send()

        pl.when(c == 0)(lambda: route(0, 1))
        pl.when(c == 1)(lambda: route(1, 0))
        copy(0, sibling, me).wait_recv()
        for j, chip in enumerate(chips):
            copy(4 + j, (*chip, 1 - c), me).wait_recv()
        for cp in first + passed:
            cp.wait_send()
        mine.wait()

    return pl.pallas_call(
        body, name=name, in_specs=[ANY], out_specs=ANY, out_shape=_sds((N_DEV,) + shard.shape, shard.dtype),
        scratch_shapes=[pltpu.SemaphoreType.DMA((N_DEV - 1,)), pltpu.SemaphoreType.DMA((N_DEV - 1,)),
                        pltpu.SemaphoreType.DMA],
    )(shard)


N_CHIPS = N_DEV // 2


def _peers(chips_only=False):
    x, y, c = lax.axis_index("x"), lax.axis_index("y"), lax.axis_index("c")
    out = []
    if chips_only:
        for k in range(1, N_CHIPS):
            px = 1 - x if k & 2 else x
            py = 1 - y if k & 1 else y
            out.append(((px, py, c), 2 * px + py))
        return 2 * x + y, out
    for k in range(1, N_DEV):
        px = 1 - x if k & 4 else x
        py = 1 - y if k & 2 else y
        pc = 1 - c if k & 1 else c
        out.append(((px, py, pc), 4 * px + 2 * py + pc))
    return 4 * x + 2 * y + c, out


def _sibling_swap(slot_arrays, *, name):
    n = len(slot_arrays)

    def body(*refs):
        ins, outs, send_sems, recv_sems = refs[:n], refs[n:2 * n], refs[2 * n], refs[2 * n + 1]
        x, y, c = lax.axis_index("x"), lax.axis_index("y"), lax.axis_index("c")
        copies = [pltpu.make_async_remote_copy(
            src_ref=ins[a].at[2 * q + (1 - c)], dst_ref=outs[a].at[q], send_sem=send_sems.at[a, q],
            recv_sem=recv_sems.at[a, q], device_id=(x, y, 1 - c), device_id_type=MESH_ID)
            for a in range(n) for q in range(N_CHIPS)]
        for cp in copies:
            cp.start()
        for cp in copies:
            cp.wait_recv()
        for cp in copies:
            cp.wait_send()

    return pl.pallas_call(
        body, name=name, in_specs=[ANY] * n, out_specs=[ANY] * n,
        out_shape=[_sds((N_CHIPS,) + t.shape[1:], t.dtype) for t in slot_arrays],
        scratch_shapes=[pltpu.SemaphoreType.DMA((n, N_CHIPS)), pltpu.SemaphoreType.DMA((n, N_CHIPS))],
    )(*slot_arrays)


def _pair_sum(slots, from_sibling, *, name, tn):
    _, r, c = slots.shape
    core = lax.axis_index("c").astype(jnp.int32).reshape(1)

    def body(core_ref, a_ref, b_ref, o_ref):
        o_ref[...] = (a_ref[...].astype(F32) + b_ref[...].astype(F32)).astype(o_ref.dtype)

    blk = lambda f: pl.BlockSpec((1, r, tn), f)
    return pl.pallas_call(
        body, name=name,
        grid_spec=pltpu.PrefetchScalarGridSpec(
            num_scalar_prefetch=1, grid=(N_CHIPS, c // tn),
            in_specs=[blk(lambda q, j, core: (2 * q + core[0], 0, j)), blk(lambda q, j, core: (q, 0, j))],
            out_specs=blk(lambda q, j, core: (q, 0, j))),
        out_shape=_sds((N_CHIPS, r, c), slots.dtype),
        compiler_params=_params("parallel", "parallel"),
    )(core, slots, from_sibling)


HBM = pl.BlockSpec(memory_space=pltpu.HBM)
SEM = pl.BlockSpec(memory_space=pltpu.SEMAPHORE)
DATAFLOW = pltpu.SideEffectType.DATAFLOW_SIDE_EFFECTING


def _split_copy(srcs, lands, send_sems, recv_sems, scatter, a, k, me, peers, incoming=False):
    dev, slot = peers[k]
    if incoming:
        src = dst = lands[a].at[slot]
    else:
        src, dst = (srcs[a].at[slot] if scatter else srcs[a]), lands[a].at[me]
    sem = a * len(peers) + k
    return pltpu.make_async_remote_copy(
        src_ref=src, dst_ref=dst, send_sem=send_sems.at[sem], recv_sem=recv_sems.at[sem],
        device_id=dev, device_id_type=MESH_ID)


def _own_copy(srcs, lands, own_sems, scatter, a, me):
    return pltpu.make_async_copy(srcs[a].at[me] if scatter else srcs[a], lands[a].at[me], own_sems.at[a])


def _exchange_start(arrays, scatter, *, name, chips_only=False, after=None):
    n = len(arrays)
    n_slots = N_CHIPS if chips_only else N_DEV
    n_in = 2 * n + len(_also(after))

    def body(*refs):
        srcs, lands = refs[:n], refs[n:2 * n]
        send_sems, recv_sems, own_sems = refs[n_in:n_in + 3]
        token = refs[-1]
        me, peers = _peers(chips_only)
        for k in range(len(peers)):
            for a in range(n):
                _split_copy(srcs, lands, send_sems, recv_sems, scatter, a, k, me, peers).start()
        for a in range(n):
            _own_copy(srcs, lands, own_sems, scatter, a, me).start(priority=1)
        token[...] = jnp.zeros_like(token)

    land_shapes = [((n_slots,) + a.shape[-2:], a.dtype) for a in arrays]
    sems = pltpu.SemaphoreType.DMA((n * (n_slots - 1),))
    outs = pl.pallas_call(
        body, name=name,
        out_shape=(sems, sems, pltpu.SemaphoreType.DMA((n,)), *[pltpu.HBM(a.shape, a.dtype) for a in arrays],
                   *[pltpu.HBM(s, d) for s, d in land_shapes], _sds((SUBLANE, LANE), F32)),
        in_specs=[HBM] * (2 * n) + [ANY] * len(_also(after)),
        out_specs=(SEM, SEM, SEM, *[HBM] * (2 * n), pl.BlockSpec(memory_space=pltpu.VMEM)),
        input_output_aliases={i: 3 + i for i in range(2 * n)},
        compiler_params=pltpu.CompilerParams(has_side_effects=DATAFLOW),
    )(*[pltpu.with_memory_space_constraint(a, pltpu.HBM) for a in arrays],
      *[pltpu.with_memory_space_constraint(lax.empty(s, d), pltpu.HBM) for s, d in land_shapes], *_also(after))
    return (outs[:3], outs[3:3 + n], outs[3 + n:3 + 2 * n], scatter, chips_only), outs[-1]


def _exchange_wait(handles, after, *, name):
    sems, srcs, lands, scatter, chips_only = handles
    n = len(srcs)

    def body(*refs):
        src_refs, land_refs = refs[:n], refs[n:2 * n]
        send_ref, recv_ref, own_ref = refs[2 * n:2 * n + 3]
        me, peers = _peers(chips_only)
        for k in range(len(peers)):
            for a in range(n):
                _split_copy(src_refs, land_refs, send_ref, recv_ref, scatter, a, k, me, peers).wait_send()
                _split_copy(src_refs, land_refs, send_ref, recv_ref, scatter, a, k, me, peers, True).wait_recv()
        for a in range(n):
            _own_copy(src_refs, land_refs, own_ref, scatter, a, me).wait()

    outs = pl.pallas_call(
        body, name=name,
        out_shape=tuple(pltpu.HBM(t.shape, t.dtype) for t in (*srcs, *lands)),
        in_specs=[HBM] * (2 * n) + [SEM, SEM, SEM, pl.BlockSpec(memory_space=pl.ANY)],
        out_specs=tuple([HBM] * (2 * n)),
        input_output_aliases={i: i for i in range(2 * n)},
        compiler_params=pltpu.CompilerParams(has_side_effects=DATAFLOW),
    )(*srcs, *lands, *sems, after)
    return outs[n:]


def _adamw(parts, w, m, v, *, name, tm):
    r, c = w.shape
    assert r % tm == 0

    def body(p_ref, w_ref, m_ref, v_ref, g_ref, d_ref, nm_ref, nv_ref):
        _adamw_update(p_ref, w_ref, m_ref, v_ref, g_ref, d_ref, nm_ref, nv_ref)

    blk = pl.BlockSpec((tm, c), lambda i: (i, 0))
    return pl.pallas_call(
        body, name=name, grid=(r // tm,),
        in_specs=[pl.BlockSpec((parts.shape[0], tm, c), lambda i: (0, i, 0)), blk, blk, blk],
        out_specs=[blk] * 4, out_shape=[_sds((r, c), F32)] * 4,
        compiler_params=_params("parallel"),
    )(parts, w, m, v)


def _adamw_update(p_ref, w_ref, m_ref, v_ref, g_ref, d_ref, nm_ref, nv_ref):
    g = p_ref[0].astype(F32)
    for s in range(1, p_ref.shape[0]):
        g = g + p_ref[s].astype(F32)
    g_ref[...] = g
    m_new = ADAM_B1 * m_ref[...] + (1.0 - ADAM_B1) * g
    v_new = ADAM_B2 * v_ref[...] + (1.0 - ADAM_B2) * (g * g)
    nm_ref[...] = m_new
    nv_ref[...] = v_new
    m_hat = m_new / (1.0 - ADAM_B1 ** ADAM_STEP)
    v_hat = v_new / (1.0 - ADAM_B2 ** ADAM_STEP)
    d_ref[...] = -ADAM_LR * (m_hat / (jnp.sqrt(v_hat) + ADAM_EPS) + ADAM_WD * w_ref[...])


SMALL = ("g_pre_mix", "b_forget", "g_post_mix", "g_pre_ffn", "conv_b", "g_post_ffn")


def _adamw_small(parts, ws, ms, vs, sq_err_parts):
    n = len(ws)

    def body(*refs):
        ins, sq_ref, outs, loss_ref = refs[:4 * n], refs[4 * n], refs[4 * n + 1:-1], refs[-1]
        for i in range(n):
            _adamw_update(ins[i], ins[n + i], ins[2 * n + i], ins[3 * n + i], *outs[4 * i:4 * i + 4])
        total = sq_ref[0]
        for s in range(1, N_DEV):
            total = total + sq_ref[s]
        loss_ref[...] = total * (0.5 / D_MODEL)

    res = pl.pallas_call(
        body, name="adamw_small",
        out_shape=[_sds(w.shape, F32) for w in ws for _ in range(4)] + [_sds((1, LANE), F32)],
        compiler_params=pltpu.CompilerParams(vmem_limit_bytes=VMEM_LIMIT),
    )(*parts, *ws, *ms, *vs, sq_err_parts)
    return [res[4 * i:4 * i + 4] for i in range(n)], res[-1][0, 0]


def kernel(x, g_pre_mix, w_in, b_forget, w_o_fox, w_o_dil, w_out, g_post_mix, g_pre_ffn, w_up, conv_w, conv_b, w_down, g_post_ffn, loss_target, m_g_pre_mix, m_w_in, m_b_forget, m_w_o_fox, m_w_o_dil, m_w_out, m_g_post_mix, m_g_pre_ffn, m_w_up, m_conv_w, m_conv_b, m_w_down, m_g_post_ffn, v_g_pre_mix, v_w_in, v_b_forget, v_w_o_fox, v_w_o_dil, v_w_out, v_g_post_mix, v_g_pre_ffn, v_w_up, v_conv_w, v_conv_b, v_w_down, v_g_post_ffn):
    names = ("g_pre_mix", "w_in", "b_forget", "w_o_fox", "w_o_dil", "w_out", "g_post_mix", "g_pre_ffn",
             "w_up", "conv_w", "conv_b", "w_down", "g_post_ffn")
    w = dict(g_pre_mix=g_pre_mix, w_in=w_in, b_forget=b_forget, w_o_fox=w_o_fox, w_o_dil=w_o_dil, w_out=w_out,
             g_post_mix=g_post_mix, g_pre_ffn=g_pre_ffn, w_up=w_up, conv_w=conv_w, conv_b=conv_b, w_down=w_down,
             g_post_ffn=g_post_ffn)
    m = dict(g_pre_mix=m_g_pre_mix, w_in=m_w_in, b_forget=m_b_forget, w_o_fox=m_w_o_fox, w_o_dil=m_w_o_dil,
             w_out=m_w_out, g_post_mix=m_g_post_mix, g_pre_ffn=m_g_pre_ffn, w_up=m_w_up, conv_w=m_conv_w,
             conv_b=m_conv_b, w_down=m_w_down, g_post_ffn=m_g_post_ffn)
    v = dict(g_pre_mix=v_g_pre_mix, w_in=v_w_in, b_forget=v_b_forget, w_o_fox=v_w_o_fox, w_o_dil=v_w_o_dil,
             w_out=v_w_out, g_post_mix=v_g_post_mix, g_pre_ffn=v_g_pre_ffn, w_up=v_w_up, conv_w=v_conv_w,
             conv_b=v_conv_b, w_down=v_w_down, g_post_ffn=v_g_post_ffn)
    sharded = ("w_in", "w_o_fox", "w_o_dil", "w_out", "w_up", "w_down", "conv_w")
    wire = lambda n: F32 if n == "conv_w" else BF16

    by_cols = lambda t: jnp.transpose(t, (1, 0, 2)).reshape(t.shape[1], N_DEV * t.shape[2])
    by_rows = lambda t: t.reshape(N_DEV * t.shape[1], t.shape[2])
    col_slots = lambda t: jnp.transpose(t.reshape(t.shape[0], N_DEV, t.shape[1] // N_DEV), (1, 0, 2))
    row_slots = lambda t: t.reshape(N_DEV, t.shape[0] // N_DEV, t.shape[1])
    to_slots = lambda n, t: (row_slots if n in ("w_out", "w_down") else col_slots)(t).astype(wire(n))
    shard = lambda n: w[n][0].astype(wire(n))

    w_main, w_f = _w_in_from_shards(_gather_two_level(shard("w_in"), name="gather_w_in"))
    proj_handles, proj_tok = _exchange_start(
        [shard("w_o_fox"), shard("w_o_dil"), shard("w_out")], False, name="gather_proj_start", after=w_f)
    ffn_handles, ffn_tok = _exchange_start(
        [shard("w_up"), shard("conv_w"), shard("w_down")], False, name="gather_ffn_start", after=proj_tok)

    def proj_weights(after):
        w_oa, w_ob, w_o = _exchange_wait(proj_handles, after, name="gather_proj_wait")
        return by_cols(w_oa), by_cols(w_ob), by_rows(w_o)

    def ffn_weights(after):
        w_u, conv, w_d = _exchange_wait(ffn_handles, after, name="gather_ffn_wait")
        return _w_up_from_shards(w_u), _ffn_interleave(by_cols(conv)), by_rows(w_d)

    pending = {}

    def ffn_grads_ready(g):
        slots = [to_slots("w_down", g["w_down"]), _w_up_to_shards(g["w_up_blocks"]), to_slots("conv_w", g["conv_w"])]
        pending["ffn"] = _exchange_start(slots, True, name="scatter_ffn_start")
        return pending["ffn"][1]

    def proj_grads_ready(g):
        pending["proj"] = _exchange_start([g["w_o_fox"], g["w_o_dil"], to_slots("w_out", g["w_out"])], True,
                                          name="scatter_proj_start")
        return pending["proj"][1]

    def mixer_grads_ready(g):
        slots = _w_in_to_shards(g["w_main"], g["w_f"])
        theirs = _sibling_swap([slots], name="scatter_w_in_swap")[0]
        chip_sums = _pair_sum(slots, theirs, name="scatter_w_in_pair_sum", tn=W_IN_SHARD)
        pending["w_in"] = _exchange_start([chip_sums], True, name="scatter_w_in_start", chips_only=True)
        return pending["w_in"][1]

    sq_err, grad_x, g = _local_step(
        x[0], loss_target[0], w_main, w_f, b_forget, conv_b, g_pre_mix, g_post_mix, g_pre_ffn,
        g_post_ffn, proj_weights, ffn_weights, ffn_grads_ready, proj_grads_ready, mixer_grads_ready, after=ffn_tok)

    small_handles, small_tok = _exchange_start([g[n] for n in SMALL] + [sq_err], False, name="gather_small_start")
    tiles = dict(w_in=256, w_o_fox=512, w_o_dil=512, w_out=128, w_up=256, w_down=176, conv_w=3)
    adam = lambda n, p: _adamw(p, w[n][0], m[n][0], v[n][0], name=f"adamw_{n}", tm=tiles[n])
    res = {}
    for key, group in (("ffn", ("w_down", "w_up", "conv_w")), ("proj", ("w_o_fox", "w_o_dil", "w_out"))):
        landed = _exchange_wait(pending[key][0], small_tok, name=f"scatter_{key}_wait")
        res.update({n: adam(n, p) for n, p in zip(group, landed)})
    done = res["w_up"][3]
    res["w_in"] = adam("w_in", _exchange_wait(pending["w_in"][0], done, name="scatter_w_in_wait")[0])
    small_parts = _exchange_wait(small_handles, res["w_in"][3], name="gather_small_wait")
    small, loss = _adamw_small(small_parts[:-1], *[[t[n] for n in SMALL] for t in (w, m, v)], small_parts[-1])
    small = dict(zip(SMALL, small))
    out = [[(res[n][k][None] if n in sharded else small[n][k]) for n in names] for k in range(4)]
    return (loss, grad_x[None], *out[0], *out[1], *out[2], *out[3])
```
